```python
import math
import jax, jax.numpy as jnp
from jax import lax
import numpy as np

D_MODEL = 1024
BATCH = 8
SEQ = 2048
DEPTH = 2

HEAD_DIM = 64
SB_HEADS = 8
SWA_Q_HEADS = 8
SWA_KV_HEADS = 2
SWA_GROUP = SWA_Q_HEADS // SWA_KV_HEADS
WINDOW = 128
BLOCK = 128
N_BUCKETS = 32
MAX_DISTANCE = 128
D_FF = 2816
EPS = 1e-6
NEG_INF = -1e30

SB_WIDTH = SB_HEADS * HEAD_DIM
SWA_WIDTH = SWA_Q_HEADS * HEAD_DIM
KV_WIDTH = SWA_KV_HEADS * HEAD_DIM
MIX_WIDTH = SB_WIDTH + SWA_WIDTH
IN_WIDTH = 3 * SB_WIDTH + SWA_WIDTH + 2 * KV_WIDTH
SPLITS = (SB_WIDTH, 2 * SB_WIDTH, 3 * SB_WIDTH,
          3 * SB_WIDTH + SWA_WIDTH, 3 * SB_WIDTH + SWA_WIDTH + KV_WIDTH)

kernel_name = "hymba_stickbreak_swa_sink_macaron"


def rms_norm(x, g):
    xf = x.astype(jnp.float32)
    y = xf * lax.rsqrt(jnp.mean(xf * xf, axis=-1, keepdims=True) + EPS)
    return (y * g.astype(jnp.float32)).astype(x.dtype)


def swiglu(x, w_gu, w_down):
    gate, up = jnp.split(x @ w_gu, 2, axis=-1)
    return (jax.nn.silu(gate) * up) @ w_down


def t5_causal_bucket(dist):
    max_exact = N_BUCKETS // 2
    d = jnp.maximum(dist, 1).astype(jnp.float32)
    large = max_exact + (jnp.log(d / max_exact) / math.log(MAX_DISTANCE / max_exact)
                         * (N_BUCKETS - max_exact)).astype(jnp.int32)
    large = jnp.minimum(large, N_BUCKETS - 1)
    return jnp.where(dist < max_exact, dist, large)


def stick_breaking_attention(q, k, v):
    B, S = q.shape[:2]
    nblk = S // BLOCK
    outs = []
    for i in range(nblk):
        L = (i + 1) * BLOCK
        qi = q[:, i * BLOCK:L]
        z = jnp.einsum('bqhd,bkhd->bhqk', qi, k[:, :L],
                       preferred_element_type=jnp.float32) * (HEAD_DIM ** -0.5)
        t_pos = i * BLOCK + jnp.arange(BLOCK)[:, None]
        s_pos = jnp.arange(L)[None, :]
        causal = s_pos < t_pos
        neg_log_keep = jnp.where(causal, jax.nn.softplus(z), 0.0)
        suffix = lax.cumsum(neg_log_keep, axis=3, reverse=True) - neg_log_keep
        log_w = jax.nn.log_sigmoid(z) - suffix
        w = jnp.where(causal, jnp.exp(log_w), 0.0).astype(v.dtype)
        outs.append(jnp.einsum('bhqk,bkhd->bqhd', w, v[:, :L]))
    return jnp.concatenate(outs, axis=1).reshape(B, S, SB_WIDTH)


def sliding_window_sink_attention(q, k, v, sinks, rel_bias):
    B, S = q.shape[:2]
    nblk = S // BLOCK
    qb = q.reshape(B, nblk, BLOCK, SWA_KV_HEADS, SWA_GROUP, HEAD_DIM)

    def band(t):
        tp = jnp.pad(t, ((0, 0), (BLOCK, 0), (0, 0), (0, 0)))
        prev = tp[:, :S].reshape(B, nblk, BLOCK, SWA_KV_HEADS, HEAD_DIM)
        cur = t.reshape(B, nblk, BLOCK, SWA_KV_HEADS, HEAD_DIM)
        return jnp.concatenate([prev, cur], axis=2)

    kb, vb = band(k), band(v)
    scores = jnp.einsum('bnqhgd,bnkhd->bnhgqk', qb, kb,
                        preferred_element_type=jnp.float32) * (HEAD_DIM ** -0.5)
    a = jnp.arange(BLOCK)[:, None]
    c = jnp.arange(2 * BLOCK)[None, :]
    dist = BLOCK + a - c
    bias = rel_bias.astype(jnp.float32)[t5_causal_bucket(jnp.maximum(dist, 0))]
    bias = bias.transpose(2, 0, 1).reshape(SWA_KV_HEADS, SWA_GROUP, BLOCK, 2 * BLOCK)
    in_band = (dist >= 0) & (dist < WINDOW)
    key_exists = (jnp.arange(nblk)[:, None] > 0) | (c >= BLOCK)
    mask = in_band[None] & key_exists[:, None, :]
    scores = jnp.where(mask[None, :, None, None], scores + bias, NEG_INF)
    sink = sinks.astype(jnp.float32).reshape(SWA_KV_HEADS, SWA_GROUP)[None, None, :, :, None, None]
    m = jnp.maximum(jnp.max(scores, axis=-1, keepdims=True), sink)
    p = jnp.exp(scores - m)
    p = (p / (jnp.sum(p, axis=-1, keepdims=True) + jnp.exp(sink - m))).astype(v.dtype)
    out = jnp.einsum('bnhgqk,bnkhd->bnqhgd', p, vb)
    return out.reshape(B, S, SWA_WIDTH)


def _fwd_setup_inputs(seed: int = 0) -> dict:
    key = jax.random.key(seed)
    ks = jax.random.split(key, 16)
    f32 = jnp.float32

    def w(k, shape, fan_in):
        return jax.random.normal(k, shape, f32) * (fan_in ** -0.5)

    def gain(k, shape):
        return 1.0 + 0.02 * jax.random.normal(k, shape, f32)

    return {
        "x": jax.random.normal(ks[0], (BATCH, SEQ, D_MODEL), f32),
        "norm_ffn1": gain(ks[1], (DEPTH, D_MODEL)),
        "w_ffn1_gu": w(ks[2], (DEPTH, D_MODEL, 2 * D_FF), D_MODEL),
        "w_ffn1_down": w(ks[3], (DEPTH, D_FF, D_MODEL), D_FF),
        "norm_mix": gain(ks[4], (DEPTH, D_MODEL)),
        "w_in": w(ks[5], (DEPTH, D_MODEL, IN_WIDTH), D_MODEL),
        "sinks": 0.5 * jax.random.normal(ks[6], (DEPTH, SWA_Q_HEADS), f32),
        "norm_out_sb": gain(ks[7], (DEPTH, SB_WIDTH)),
        "norm_out_swa": gain(ks[8], (DEPTH, SWA_WIDTH)),
        "w_out": w(ks[9], (DEPTH, MIX_WIDTH, D_MODEL), MIX_WIDTH),
        "norm_ffn2": gain(ks[10], (DEPTH, D_MODEL)),
        "w_ffn2_gu": w(ks[11], (DEPTH, D_MODEL, 2 * D_FF), D_MODEL),
        "w_ffn2_down": w(ks[12], (DEPTH, D_FF, D_MODEL), D_FF),
        "rel_bias": 0.5 * jax.random.normal(ks[13], (N_BUCKETS, SWA_Q_HEADS), f32),
        "norm_final": gain(ks[14], (D_MODEL,)),
    }


def _fwd_reference(x, norm_ffn1, w_ffn1_gu, w_ffn1_down, norm_mix, w_in, sinks,
              norm_out_sb, norm_out_swa, w_out, norm_ffn2, w_ffn2_gu, w_ffn2_down,
              rel_bias, norm_final):
    B, S, _ = x.shape
    h = x
    for l in range(DEPTH):
        h = h + 0.5 * swiglu(rms_norm(h, norm_ffn1[l]), w_ffn1_gu[l], w_ffn1_down[l])
        n = rms_norm(h, norm_mix[l])
        proj = n @ w_in[l]
        q_sb, k_sb, v_sb, q_sw, k_sw, v_sw = jnp.split(proj, SPLITS, axis=-1)
        o_sb = stick_breaking_attention(
            q_sb.reshape(B, S, SB_HEADS, HEAD_DIM),
            k_sb.reshape(B, S, SB_HEADS, HEAD_DIM),
            v_sb.reshape(B, S, SB_HEADS, HEAD_DIM))
        o_sw = sliding_window_sink_attention(
            q_sw.reshape(B, S, SWA_Q_HEADS, HEAD_DIM),
            k_sw.reshape(B, S, SWA_KV_HEADS, HEAD_DIM),
            v_sw.reshape(B, S, SWA_KV_HEADS, HEAD_DIM),
            sinks[l], rel_bias)
        mixed = jnp.concatenate([rms_norm(o_sb, norm_out_sb[l]),
                                 rms_norm(o_sw, norm_out_swa[l])], axis=-1)
        h = h + mixed @ w_out[l]
        h = h + 0.5 * swiglu(rms_norm(h, norm_ffn2[l]), w_ffn2_gu[l], w_ffn2_down[l])
    return rms_norm(h, norm_final)


import jax as _jax
import jax.numpy as _jnp

TWIN_FORMAT = 'train_step'
FWD_PARAMS = ['x', 'norm_ffn1', 'w_ffn1_gu', 'w_ffn1_down', 'norm_mix', 'w_in', 'sinks', 'norm_out_sb', 'norm_out_swa', 'w_out', 'norm_ffn2', 'w_ffn2_gu', 'w_ffn2_down', 'rel_bias', 'norm_final']
TWIN_WEIGHTS = ['norm_ffn1', 'w_ffn1_gu', 'w_ffn1_down', 'norm_mix', 'w_in', 'sinks', 'norm_out_sb', 'norm_out_swa', 'w_out', 'norm_ffn2', 'w_ffn2_gu', 'w_ffn2_down', 'rel_bias', 'norm_final']
TWIN_DIFF_INPUT = 'x'
TWIN_INPUTS = ['x', 'norm_ffn1', 'w_ffn1_gu', 'w_ffn1_down', 'norm_mix', 'w_in', 'sinks', 'norm_out_sb', 'norm_out_swa', 'w_out', 'norm_ffn2', 'w_ffn2_gu', 'w_ffn2_down', 'rel_bias', 'norm_final', 'loss_target', 'm_norm_ffn1', 'm_w_ffn1_gu', 'm_w_ffn1_down', 'm_norm_mix', 'm_w_in', 'm_sinks', 'm_norm_out_sb', 'm_norm_out_swa', 'm_w_out', 'm_norm_ffn2', 'm_w_ffn2_gu', 'm_w_ffn2_down', 'm_rel_bias', 'm_norm_final', 'v_norm_ffn1', 'v_w_ffn1_gu', 'v_w_ffn1_down', 'v_norm_mix', 'v_w_in', 'v_sinks', 'v_norm_out_sb', 'v_norm_out_swa', 'v_w_out', 'v_norm_ffn2', 'v_w_ffn2_gu', 'v_w_ffn2_down', 'v_rel_bias', 'v_norm_final']
TWIN_OUTPUTS = ['loss', 'grad_x', 'grad_norm_ffn1', 'grad_w_ffn1_gu', 'grad_w_ffn1_down', 'grad_norm_mix', 'grad_w_in', 'grad_sinks', 'grad_norm_out_sb', 'grad_norm_out_swa', 'grad_w_out', 'grad_norm_ffn2', 'grad_w_ffn2_gu', 'grad_w_ffn2_down', 'grad_rel_bias', 'grad_norm_final', 'delta_norm_ffn1', 'delta_w_ffn1_gu', 'delta_w_ffn1_down', 'delta_norm_mix', 'delta_w_in', 'delta_sinks', 'delta_norm_out_sb', 'delta_norm_out_swa', 'delta_w_out', 'delta_norm_ffn2', 'delta_w_ffn2_gu', 'delta_w_ffn2_down', 'delta_rel_bias', 'delta_norm_final', 'new_m_norm_ffn1', 'new_m_w_ffn1_gu', 'new_m_w_ffn1_down', 'new_m_norm_mix', 'new_m_w_in', 'new_m_sinks', 'new_m_norm_out_sb', 'new_m_norm_out_swa', 'new_m_w_out', 'new_m_norm_ffn2', 'new_m_w_ffn2_gu', 'new_m_w_ffn2_down', 'new_m_rel_bias', 'new_m_norm_final', 'new_v_norm_ffn1', 'new_v_w_ffn1_gu', 'new_v_w_ffn1_down', 'new_v_norm_mix', 'new_v_w_in', 'new_v_sinks', 'new_v_norm_out_sb', 'new_v_norm_out_swa', 'new_v_w_out', 'new_v_norm_ffn2', 'new_v_w_ffn2_gu', 'new_v_w_ffn2_down', 'new_v_rel_bias', 'new_v_norm_final']
TWIN_LEAF_KINDS = {'loss': 'loss', 'grad_x': 'grad_x', 'grad_norm_ffn1': 'grad_w', 'grad_w_ffn1_gu': 'grad_w', 'grad_w_ffn1_down': 'grad_w', 'grad_norm_mix': 'grad_w', 'grad_w_in': 'grad_w', 'grad_sinks': 'grad_w', 'grad_norm_out_sb': 'grad_w', 'grad_norm_out_swa': 'grad_w', 'grad_w_out': 'grad_w', 'grad_norm_ffn2': 'grad_w', 'grad_w_ffn2_gu': 'grad_w', 'grad_w_ffn2_down': 'grad_w', 'grad_rel_bias': 'grad_w', 'grad_norm_final': 'grad_w', 'delta_norm_ffn1': 'delta_w', 'delta_w_ffn1_gu': 'delta_w', 'delta_w_ffn1_down': 'delta_w', 'delta_norm_mix': 'delta_w', 'delta_w_in': 'delta_w', 'delta_sinks': 'delta_w', 'delta_norm_out_sb': 'delta_w', 'delta_norm_out_swa': 'delta_w', 'delta_w_out': 'delta_w', 'delta_norm_ffn2': 'delta_w', 'delta_w_ffn2_gu': 'delta_w', 'delta_w_ffn2_down': 'delta_w', 'delta_rel_bias': 'delta_w', 'delta_norm_final': 'delta_w', 'new_m_norm_ffn1': 'new_m', 'new_m_w_ffn1_gu': 'new_m', 'new_m_w_ffn1_down': 'new_m', 'new_m_norm_mix': 'new_m', 'new_m_w_in': 'new_m', 'new_m_sinks': 'new_m', 'new_m_norm_out_sb': 'new_m', 'new_m_norm_out_swa': 'new_m', 'new_m_w_out': 'new_m', 'new_m_norm_ffn2': 'new_m', 'new_m_w_ffn2_gu': 'new_m', 'new_m_w_ffn2_down': 'new_m', 'new_m_rel_bias': 'new_m', 'new_m_norm_final': 'new_m', 'new_v_norm_ffn1': 'new_v', 'new_v_w_ffn1_gu': 'new_v', 'new_v_w_ffn1_down': 'new_v', 'new_v_norm_mix': 'new_v', 'new_v_w_in': 'new_v', 'new_v_sinks': 'new_v', 'new_v_norm_out_sb': 'new_v', 'new_v_norm_out_swa': 'new_v', 'new_v_w_out': 'new_v', 'new_v_norm_ffn2': 'new_v', 'new_v_w_ffn2_gu': 'new_v', 'new_v_w_ffn2_down': 'new_v', 'new_v_rel_bias': 'new_v', 'new_v_norm_final': 'new_v'}


def _forward(args):
    return _fwd_reference(*[args[k] for k in FWD_PARAMS])


def _output_shape():
    out = _jax.eval_shape(lambda: _forward(_fwd_setup_inputs(0)))
    return out.shape, out.dtype

N_MICROBATCH = 1
ADAM_LR = 0.001
ADAM_B1 = 0.9
ADAM_B2 = 0.999
ADAM_EPS = 1e-08
ADAM_WD = 0.01
ADAM_STEP = 10
PER_EXAMPLE_BATCH_AXIS = {'x': 0, 'loss_target': 0}
SHARED_INPUTS = []
_WEIGHT_DTYPES = {'norm_ffn1': _jnp.float32, 'w_ffn1_gu': _jnp.float32, 'w_ffn1_down': _jnp.float32, 'norm_mix': _jnp.float32, 'w_in': _jnp.float32, 'sinks': _jnp.float32, 'norm_out_sb': _jnp.float32, 'norm_out_swa': _jnp.float32, 'w_out': _jnp.float32, 'norm_ffn2': _jnp.float32, 'w_ffn2_gu': _jnp.float32, 'w_ffn2_down': _jnp.float32, 'rel_bias': _jnp.float32, 'norm_final': _jnp.float32}
MOMENT_SCALE = {'norm_ffn1': 5.908279e-02, 'w_ffn1_gu': 2.398368e-02, 'w_ffn1_down': 3.913275e-02, 'norm_mix': 1.190949e-01, 'w_in': 7.852008e-02, 'sinks': 4.083893e-02, 'norm_out_sb': 8.809063e-02, 'norm_out_swa': 9.736912e-02, 'w_out': 8.891671e-02, 'norm_ffn2': 3.439121e-02, 'w_ffn2_gu': 1.462697e-02, 'w_ffn2_down': 2.395465e-02, 'rel_bias': 1.239752e-01, 'norm_final': 1.597962e+01}


def _to_microbatches(a, axis):
    t = _jnp.moveaxis(a, axis, 0)
    t = t.reshape((N_MICROBATCH, t.shape[0] // N_MICROBATCH) + t.shape[1:])
    return _jnp.moveaxis(t, 1, axis + 1)


def setup_inputs(seed: int = 0) -> dict:
    inp = _fwd_setup_inputs(seed)
    key = _jax.random.fold_in(_jax.random.key(seed), 7919)
    shape, _ = _output_shape()
    out = dict(inp)
    out["loss_target"] = _jax.random.normal(_jax.random.fold_in(key, 0), shape, _jnp.float32)
    for i, name in enumerate(TWIN_WEIGHTS):
        w = inp[name].astype(_jnp.float32)
        if MOMENT_SCALE is None:
            s = _jnp.sqrt(_jnp.mean(_jnp.square(w)) + 1e-30)
        else:
            s = MOMENT_SCALE[name]
        km, kv = _jax.random.split(_jax.random.fold_in(key, i + 1))
        out[name] = w
        out["m_" + name] = s * _jax.random.normal(km, w.shape, _jnp.float32)
        out["v_" + name] = (s * s) * _jax.random.uniform(kv, w.shape, _jnp.float32, 0.5, 1.5)
    if N_MICROBATCH > 1:
        for name, axis in PER_EXAMPLE_BATCH_AXIS.items():
            out[name] = _to_microbatches(out[name], axis)
    return {'x': out['x'], 'norm_ffn1': out['norm_ffn1'], 'w_ffn1_gu': out['w_ffn1_gu'], 'w_ffn1_down': out['w_ffn1_down'], 'norm_mix': out['norm_mix'], 'w_in': out['w_in'], 'sinks': out['sinks'], 'norm_out_sb': out['norm_out_sb'], 'norm_out_swa': out['norm_out_swa'], 'w_out': out['w_out'], 'norm_ffn2': out['norm_ffn2'], 'w_ffn2_gu': out['w_ffn2_gu'], 'w_ffn2_down': out['w_ffn2_down'], 'rel_bias': out['rel_bias'], 'norm_final': out['norm_final'], 'loss_target': out['loss_target'], 'm_norm_ffn1': out['m_norm_ffn1'], 'm_w_ffn1_gu': out['m_w_ffn1_gu'], 'm_w_ffn1_down': out['m_w_ffn1_down'], 'm_norm_mix': out['m_norm_mix'], 'm_w_in': out['m_w_in'], 'm_sinks': out['m_sinks'], 'm_norm_out_sb': out['m_norm_out_sb'], 'm_norm_out_swa': out['m_norm_out_swa'], 'm_w_out': out['m_w_out'], 'm_norm_ffn2': out['m_norm_ffn2'], 'm_w_ffn2_gu': out['m_w_ffn2_gu'], 'm_w_ffn2_down': out['m_w_ffn2_down'], 'm_rel_bias': out['m_rel_bias'], 'm_norm_final': out['m_norm_final'], 'v_norm_ffn1': out['v_norm_ffn1'], 'v_w_ffn1_gu': out['v_w_ffn1_gu'], 'v_w_ffn1_down': out['v_w_ffn1_down'], 'v_norm_mix': out['v_norm_mix'], 'v_w_in': out['v_w_in'], 'v_sinks': out['v_sinks'], 'v_norm_out_sb': out['v_norm_out_sb'], 'v_norm_out_swa': out['v_norm_out_swa'], 'v_w_out': out['v_w_out'], 'v_norm_ffn2': out['v_norm_ffn2'], 'v_w_ffn2_gu': out['v_w_ffn2_gu'], 'v_w_ffn2_down': out['v_w_ffn2_down'], 'v_rel_bias': out['v_rel_bias'], 'v_norm_final': out['v_norm_final']}


def _loss(weights, diff, rest, loss_target):
    with _jax.named_scope("forward"):
        args = {**rest, TWIN_DIFF_INPUT: diff, **{k: w.astype(_WEIGHT_DTYPES[k]) for k, w in weights.items()}}
        y = _forward(args)
    with _jax.named_scope("loss_head"):
        err = _jnp.square(y.astype(_jnp.float32) - loss_target)
        return 0.5 * _jnp.sum(_jnp.mean(err, axis=-1)) if err.ndim else 0.5 * err


def _adamw(w, g, m, v):
    m = ADAM_B1 * m + (1.0 - ADAM_B1) * g
    v = ADAM_B2 * v + (1.0 - ADAM_B2) * _jnp.square(g)
    m_hat = m / (1.0 - ADAM_B1 ** ADAM_STEP)
    v_hat = v / (1.0 - ADAM_B2 ** ADAM_STEP)
    delta = -ADAM_LR * (m_hat / (_jnp.sqrt(v_hat) + ADAM_EPS) + ADAM_WD * w)
    return delta, m, v


def reference(x, norm_ffn1, w_ffn1_gu, w_ffn1_down, norm_mix, w_in, sinks, norm_out_sb, norm_out_swa, w_out, norm_ffn2, w_ffn2_gu, w_ffn2_down, rel_bias, norm_final, loss_target, m_norm_ffn1, m_w_ffn1_gu, m_w_ffn1_down, m_norm_mix, m_w_in, m_sinks, m_norm_out_sb, m_norm_out_swa, m_w_out, m_norm_ffn2, m_w_ffn2_gu, m_w_ffn2_down, m_rel_bias, m_norm_final, v_norm_ffn1, v_w_ffn1_gu, v_w_ffn1_down, v_norm_mix, v_w_in, v_sinks, v_norm_out_sb, v_norm_out_swa, v_w_out, v_norm_ffn2, v_w_ffn2_gu, v_w_ffn2_down, v_rel_bias, v_norm_final):
    given = dict(x=x, norm_ffn1=norm_ffn1, w_ffn1_gu=w_ffn1_gu, w_ffn1_down=w_ffn1_down, norm_mix=norm_mix, w_in=w_in, sinks=sinks, norm_out_sb=norm_out_sb, norm_out_swa=norm_out_swa, w_out=w_out, norm_ffn2=norm_ffn2, w_ffn2_gu=w_ffn2_gu, w_ffn2_down=w_ffn2_down, rel_bias=rel_bias, norm_final=norm_final, loss_target=loss_target, m_norm_ffn1=m_norm_ffn1, m_w_ffn1_gu=m_w_ffn1_gu, m_w_ffn1_down=m_w_ffn1_down, m_norm_mix=m_norm_mix, m_w_in=m_w_in, m_sinks=m_sinks, m_norm_out_sb=m_norm_out_sb, m_norm_out_swa=m_norm_out_swa, m_w_out=m_w_out, m_norm_ffn2=m_norm_ffn2, m_w_ffn2_gu=m_w_ffn2_gu, m_w_ffn2_down=m_w_ffn2_down, m_rel_bias=m_rel_bias, m_norm_final=m_norm_final, v_norm_ffn1=v_norm_ffn1, v_w_ffn1_gu=v_w_ffn1_gu, v_w_ffn1_down=v_w_ffn1_down, v_norm_mix=v_norm_mix, v_w_in=v_w_in, v_sinks=v_sinks, v_norm_out_sb=v_norm_out_sb, v_norm_out_swa=v_norm_out_swa, v_w_out=v_w_out, v_norm_ffn2=v_norm_ffn2, v_w_ffn2_gu=v_w_ffn2_gu, v_w_ffn2_down=v_w_ffn2_down, v_rel_bias=v_rel_bias, v_norm_final=v_norm_final)
    weights = {n: given[n] for n in TWIN_WEIGHTS}
    shared = {n: given[n] for n in SHARED_INPUTS}
    per_example = {n: given[n] for n in ['x']}
    grad_fn = _jax.value_and_grad(_loss, argnums=(0, 1))

    def one_microbatch(ex, loss_target):
        ex = dict(ex)
        diff = ex.pop(TWIN_DIFF_INPUT)
        return grad_fn(weights, diff, {**shared, **ex}, loss_target)

    if N_MICROBATCH == 1:
        loss, (grad_w, grad_x) = one_microbatch(per_example, given["loss_target"])
    else:
        def body(carry, xs):
            loss_sum, grad_sum = carry
            l_k, (gw_k, gx_k) = one_microbatch(xs[0], xs[1])
            with _jax.named_scope("update"):
                return (loss_sum + l_k, _jax.tree.map(_jnp.add, grad_sum, gw_k)), gx_k

        init = (_jnp.zeros((), _jnp.float32), _jax.tree.map(_jnp.zeros_like, weights))
        (loss, grad_w), grad_x = _jax.lax.scan(body, init, (per_example, given["loss_target"]))
    with _jax.named_scope("update"):
        delta_w, new_m, new_v = {}, {}, {}
        for n in TWIN_WEIGHTS:
            delta_w[n], new_m[n], new_v[n] = _adamw(weights[n], grad_w[n], given["m_" + n], given["v_" + n])
    return (loss, grad_x, *[grad_w[n] for n in TWIN_WEIGHTS], *[delta_w[n] for n in TWIN_WEIGHTS],
            *[new_m[n] for n in TWIN_WEIGHTS], *[new_v[n] for n in TWIN_WEIGHTS])
```

```python
import math

import jax
import jax.numpy as jnp
from jax import lax
from jax.experimental import pallas as pl
from jax.experimental.pallas import tpu as pltpu

F32 = jnp.float32
BF16 = jnp.bfloat16
S = jax.ShapeDtypeStruct

N_DEV = 8
HEAD_DIM = 64
PAIR = 2 * HEAD_DIM
SB_W = 512
SWA_W = 512
KV_W = 128
IN_W = 3 * SB_W + SWA_W + 2 * KV_W
QB = 128
N_BUCKETS = 32
MAX_DISTANCE = 128
EPS = 1e-6
NEG_INF = -1e30
SCALE = HEAD_DIM ** -0.5

ADAM_LR = 0.001
ADAM_B1 = 0.9
ADAM_B2 = 0.999
ADAM_EPS = 1e-08
ADAM_WD = 0.01
ADAM_STEP = 10

VMEM_LIMIT = 56 * 1024 * 1024
MESH = pl.DeviceIdType.MESH


def _params(sem=None, vmem=VMEM_LIMIT):
    return pltpu.CompilerParams(dimension_semantics=sem, vmem_limit_bytes=vmem)


def _nn(a, b):
    return jnp.dot(a, b, preferred_element_type=F32)


def _nt(a, b):
    return lax.dot_general(a, b, (((1,), (1,)), ((), ())), preferred_element_type=F32)


def _tn(a, b):
    return lax.dot_general(a, b, (((0,), (0,)), ((), ())), preferred_element_type=F32)


def _tri(x, m):
    hi = x.astype(BF16)
    lo = (x - hi.astype(F32)).astype(BF16)
    return _nn(hi, m) + _nn(lo, m)


def _rms(x, g):
    r = lax.rsqrt(jnp.mean(x * x, axis=-1, keepdims=True) + EPS)
    return x * r * g


def _rms_bwd(dy, x, g):
    r = lax.rsqrt(jnp.mean(x * x, axis=-1, keepdims=True) + EPS)
    xhat = x * r
    u = dy * g
    dx = r * (u - xhat * jnp.mean(u * xhat, axis=-1, keepdims=True))
    return dx, jnp.sum(dy * xhat, axis=0, keepdims=True)


def _softplus(z):
    return jnp.maximum(z, 0.0) + jnp.log1p(jnp.exp(-jnp.abs(z)))


def _tile(n, want):
    t = min(n, want)
    while n % t:
        t //= 2
    return t


def _place():
    x, y, c = lax.axis_index("x"), lax.axis_index("y"), lax.axis_index("c")
    chips = [(1 - x, y), (x, 1 - y), (1 - x, 1 - y)]
    return x, y, c, chips


def all_gather_rows(v, name):
    R, C = v.shape

    def body(v_ref, out_ref, send_sems, recv_sems, local_sem):
        x, y, c, chips = _place()
        me, sibling = (x, y, c), (x, y, 1 - c)

        def slot(px, py, pc):
            return out_ref.at[4 * px + 2 * py + pc]

        def copy(k, block, to, src=None):
            return pltpu.make_async_remote_copy(
                src_ref=slot(*block) if src is None else src, dst_ref=slot(*block),
                send_sem=send_sems.at[k], recv_sem=recv_sems.at[k], device_id=to, device_id_type=MESH)

        mine = pltpu.make_async_copy(v_ref, slot(*me), local_sem)
        mine.start()
        first = [copy(0, me, sibling, src=v_ref)]
        first += [copy(1 + j, me, (*chip, c), src=v_ref) for j, chip in enumerate(chips)]
        for cp in first:
            cp.start()
        passed = [copy(4 + j, (*chip, c), sibling) for j, chip in enumerate(chips)]
        for j, chip in enumerate(chips):
            copy(1 + j, (*chip, c), me).wait_recv()
            passed[j].start()
        copy(0, sibling, me).wait_recv()
        for j, chip in enumerate(chips):
            copy(4 + j, (*chip, 1 - c), me).wait_recv()
        for cp in first + passed:
            cp.wait_send()
        mine.wait()

    return pl.pallas_call(
        body, name=name, out_shape=S((N_DEV, R, C), v.dtype),
        in_specs=[pl.BlockSpec(memory_space=pl.ANY)], out_specs=pl.BlockSpec(memory_space=pl.ANY),
        scratch_shapes=[pltpu.SemaphoreType.DMA((7,)), pltpu.SemaphoreType.DMA((7,)), pltpu.SemaphoreType.DMA],
    )(v)


def reduce_scatter_rows(g, off, r, name):
    C = g.shape[2]
    rc = _tile(r, 32)

    def rows_loop(fn):
        def step(i, carry):
            fn(pl.ds(pl.multiple_of(i * rc, rc), rc))
            return carry
        lax.fori_loop(0, r // rc, step, 0)

    def body(g_ref, out_ref, give, mine, send_a, recv_a, send_b, recv_b, load_sems, send_sems, recv_sems):
        x, y, c, chips = _place()
        owners = [(x, y)] + chips

        def load(j, pc, dst):
            px, py = owners[j]
            return pltpu.make_async_copy(g_ref.at[4 * px + 2 * py + pc, pl.ds(off, r), :], dst.at[j],
                                         load_sems.at[j + 4 * (0 if dst is give else 1)])

        loads = [load(j, 1 - c, give) for j in range(4)] + [load(j, c, mine) for j in range(4)]
        for cp in loads:
            cp.start()
        for cp in loads[:4]:
            cp.wait()

        def to_bf16(rs):
            for j in range(4):
                send_a[j, rs, :] = give[j, rs, :].astype(BF16)
        rows_loop(to_bf16)
        first = pltpu.make_async_remote_copy(src_ref=send_a, dst_ref=recv_a, send_sem=send_sems.at[0],
                                             recv_sem=recv_sems.at[0], device_id=(x, y, 1 - c), device_id_type=MESH)
        first.start()
        for cp in loads[4:]:
            cp.wait()
        first.wait_recv()

        def add_sibling(rs):
            for j in range(4):
                s = mine[j, rs, :] + recv_a[j, rs, :].astype(F32)
                mine[j, rs, :] = s
                if j:
                    send_b[j - 1, rs, :] = s.astype(BF16)
        rows_loop(add_sibling)
        second = [pltpu.make_async_remote_copy(src_ref=send_b.at[j], dst_ref=recv_b.at[j], send_sem=send_sems.at[1 + j],
                                               recv_sem=recv_sems.at[1 + j], device_id=(*chips[j], c), device_id_type=MESH)
                  for j in range(3)]
        for cp in second:
            cp.start()
        for cp in second:
            cp.wait_recv()

        def add_chips(rs):
            s = mine[0, rs, :]
            for j in range(3):
                s = s + recv_b[j, rs, :].astype(F32)
            out_ref[rs, :] = s
        rows_loop(add_chips)
        first.wait_send()
        for cp in second:
            cp.wait_send()

    return pl.pallas_call(
        body, name=name, out_shape=S((r, C), F32),
        in_specs=[pl.BlockSpec(memory_space=pl.ANY)], out_specs=pl.BlockSpec(memory_space=pltpu.VMEM),
        scratch_shapes=[pltpu.VMEM((4, r, C), F32), pltpu.VMEM((4, r, C), F32),
                        pltpu.VMEM((4, r, C), BF16), pltpu.VMEM((4, r, C), BF16),
                        pltpu.VMEM((3, r, C), BF16), pltpu.VMEM((3, r, C), BF16),
                        pltpu.SemaphoreType.DMA((8,)), pltpu.SemaphoreType.DMA((4,)), pltpu.SemaphoreType.DMA((4,))],
        compiler_params=_params(),
    )(g)


def ffn_up_fwd(h, g, wgu, name):
    T, D = h.shape
    F = wgu.shape[1]
    tm, tn = _tile(T, 512), _tile(F, 256)

    def body(h_ref, g_ref, wg_ref, wu_ref, n_ref, gate_ref, up_ref, a_ref):
        @pl.when(pl.program_id(1) == 0)
        def _():
            n_ref[...] = _rms(h_ref[...], g_ref[...]).astype(BF16)
        n = n_ref[...]
        gate = _nt(n, wg_ref[...])
        up = _nt(n, wu_ref[...])
        gate_ref[...] = gate
        up_ref[...] = up
        a_ref[...] = (gate * jax.nn.sigmoid(gate) * up).astype(BF16)

    tile = pl.BlockSpec((tm, tn), lambda i, j: (i, j))
    return pl.pallas_call(
        body, name=name, grid=(T // tm, F // tn),
        out_shape=(S((T, D), BF16), S((T, F), F32), S((T, F), F32), S((T, F), BF16)),
        in_specs=[pl.BlockSpec((tm, D), lambda i, j: (i, 0)), pl.BlockSpec((1, D), lambda i, j: (0, 0)),
                  pl.BlockSpec((None, tn, D), lambda i, j: (0, j, 0)), pl.BlockSpec((None, tn, D), lambda i, j: (1, j, 0))],
        out_specs=(pl.BlockSpec((tm, D), lambda i, j: (i, 0)), tile, tile, tile),
        compiler_params=_params(("parallel", "arbitrary")),
    )(h, g, wgu, wgu)


def ffn_down_fwd(a, wd, h, name):
    T, F = a.shape
    D = wd.shape[1]
    tm = _tile(T, 256)

    def body(a_ref, w_ref, h_ref, o_ref):
        o_ref[...] = h_ref[...] + 0.5 * _nn(a_ref[...], w_ref[...])

    return pl.pallas_call(
        body, name=name, grid=(T // tm,), out_shape=S((T, D), F32),
        in_specs=[pl.BlockSpec((tm, F), lambda i: (i, 0)), pl.BlockSpec((F, D), lambda i: (0, 0)),
                  pl.BlockSpec((tm, D), lambda i: (i, 0))],
        out_specs=pl.BlockSpec((tm, D), lambda i: (i, 0)),
        compiler_params=_params(("parallel",)),
    )(a, wd, h)


def mix_in_fwd(h, g, win, name):
    T, D = h.shape
    N = win.shape[0]
    tm = _tile(T, 256)

    def body(h_ref, g_ref, w_ref, n_ref, p_ref):
        n = _rms(h_ref[...], g_ref[...]).astype(BF16)
        n_ref[...] = n
        p_ref[...] = _nt(n, w_ref[...]).astype(BF16)

    return pl.pallas_call(
        body, name=name, grid=(T // tm,), out_shape=(S((T, D), BF16), S((T, N), BF16)),
        in_specs=[pl.BlockSpec((tm, D), lambda i: (i, 0)), pl.BlockSpec((1, D), lambda i: (0, 0)),
                  pl.BlockSpec((N, D), lambda i: (0, 0))],
        out_specs=(pl.BlockSpec((tm, D), lambda i: (i, 0)), pl.BlockSpec((tm, N), lambda i: (i, 0))),
        compiler_params=_params(("parallel",)),
    )(h, g, win)


def _tri_consts():
    r = lax.broadcasted_iota(jnp.int32, (QB, QB), 0)
    c = lax.broadcasted_iota(jnp.int32, (QB, QB), 1)
    ones = jnp.ones((QB, QB), BF16)
    after = jnp.concatenate([(r > c).astype(BF16), ones], axis=1)
    upto = jnp.concatenate([(r <= c).astype(BF16), ones], axis=1)
    before = jnp.concatenate([(r < c).astype(BF16), ones], axis=1)
    return after, upto, before


def _half_masks():
    lane = lax.broadcasted_iota(jnp.int32, (QB, PAIR), 1)
    row = lax.broadcasted_iota(jnp.int32, (QB, PAIR), 0)
    return lane < HEAD_DIM, lane, row


def sb_attn_fwd(p, after, name):
    T = p.shape[0]
    nq = T // QB

    def body(q_ref, k_ref, v_ref, m_ref, o_ref, tot_ref, acc_ref):
        i = pl.program_id(1)
        lo, lane, row = _half_masks()
        causal = lane < row
        q2 = q_ref[...].astype(F32)
        qs = [jnp.where(lo, q2, 0.0).astype(BF16), jnp.where(lo, 0.0, q2).astype(BF16)]
        m = m_ref[...]
        acc_ref[...] = jnp.zeros_like(acc_ref)

        def block(j, carry, diag):
            r0 = pl.multiple_of(j * QB, QB)
            k2 = k_ref[pl.ds(r0, QB), :]
            v2 = v_ref[pl.ds(r0, QB), :]
            out = []
            for hd in range(2):
                z = _nt(qs[hd], k2) * SCALE
                sp = _softplus(z)
                ls = z - sp
                if diag:
                    sp = jnp.where(causal, sp, 0.0)
                rr = _tri(sp, m)
                w = jnp.exp(ls - (carry[hd] + rr[:, :QB]))
                if diag:
                    w = jnp.where(causal, w, 0.0)
                acc_ref[hd] += _nn(w.astype(BF16), v2)
                out.append(carry[hd] + rr[:, QB:])
            return tuple(out)

        zero = jnp.zeros((QB, QB), F32)
        carry = block(i, (zero, zero), True)
        carry = lax.fori_loop(0, i, lambda t, cr: block(i - 1 - t, cr, False), carry)
        o_ref[...] = jnp.where(lo, acc_ref[0], acc_ref[1])
        tot_ref[:, :QB] = carry[0]
        tot_ref[:, QB:] = carry[1]

    nb = SB_W // PAIR
    return pl.pallas_call(
        body, name=name, grid=(nb, nq), out_shape=(S((T, SB_W), F32), S((T, 2 * SB_W), F32)),
        in_specs=[pl.BlockSpec((QB, PAIR), lambda hp, i: (i, hp)),
                  pl.BlockSpec((T, PAIR), lambda hp, i: (0, nb + hp)),
                  pl.BlockSpec((T, PAIR), lambda hp, i: (0, 2 * nb + hp)),
                  pl.BlockSpec((QB, 2 * QB), lambda hp, i: (0, 0))],
        out_specs=(pl.BlockSpec((QB, PAIR), lambda hp, i: (i, hp)), pl.BlockSpec((QB, 2 * QB), lambda hp, i: (i, hp))),
        scratch_shapes=[pltpu.VMEM((2, QB, PAIR), F32)],
        compiler_params=_params(("parallel", "arbitrary")),
    )(p, p, p, after)


def sb_attn_bwd(p, do, tot, upto, before, name):
    T = p.shape[0]
    nq = T // QB

    def body(q_ref, k_ref, v_ref, do_ref, tot_ref, mp_ref, mg_ref, dq_ref, dk_ref, dv_ref, dq_acc, dk_acc, dv_acc):
        i = pl.program_id(1)
        lo, lane, row = _half_masks()
        causal = lane < row
        q2 = q_ref[...].astype(F32)
        d2 = do_ref[...]
        qs = [jnp.where(lo, q2, 0.0).astype(BF16), jnp.where(lo, 0.0, q2).astype(BF16)]
        ds = [jnp.where(lo, d2, 0.0).astype(BF16), jnp.where(lo, 0.0, d2).astype(BF16)]
        tots = [tot_ref[:, :QB], tot_ref[:, QB:]]
        mp, mg = mp_ref[...], mg_ref[...]

        @pl.when(i == 0)
        def _():
            dk_acc[...] = jnp.zeros_like(dk_acc)
            dv_acc[...] = jnp.zeros_like(dv_acc)
        dq_acc[...] = jnp.zeros_like(dq_acc)

        def block(j, carry, diag):
            r0 = pl.multiple_of(j * QB, QB)
            k2 = k_ref[pl.ds(r0, QB), :]
            v2 = v_ref[pl.ds(r0, QB), :]
            out = []
            dk_c = None
            dv_c = None
            for hd in range(2):
                pc, gc = carry[2 * hd], carry[2 * hd + 1]
                z = _nt(qs[hd], k2) * SCALE
                sp = _softplus(z)
                ls = z - sp
                if diag:
                    sp = jnp.where(causal, sp, 0.0)
                rr = _tri(sp, mp)
                w = jnp.exp(ls - (tots[hd] - (pc + rr[:, :QB])))
                if diag:
                    w = jnp.where(causal, w, 0.0)
                gg = _nt(ds[hd], v2) * w
                rg = _tri(gg, mg)
                sig = jnp.exp(ls)
                dz = (gg * (1.0 - sig) - (gc + rg[:, :QB]) * sig) * SCALE
                if diag:
                    dz = jnp.where(causal, dz, 0.0)
                dzb = dz.astype(BF16)
                wb = w.astype(BF16)
                dq_acc[hd] += _nn(dzb, k2)
                dk_h = _tn(dzb, qs[hd])
                dv_h = _tn(wb, ds[hd])
                dk_c = dk_h if dk_c is None else dk_c + dk_h
                dv_c = dv_h if dv_c is None else dv_c + dv_h
                out += [pc + rr[:, QB:], gc + rg[:, QB:]]
            dk_acc[pl.ds(r0, QB), :] += dk_c
            dv_acc[pl.ds(r0, QB), :] += dv_c
            return tuple(out)

        zero = jnp.zeros((QB, QB), F32)
        carry = lax.fori_loop(0, i, lambda t, cr: block(t, cr, False), (zero, zero, zero, zero))
        block(i, carry, True)
        dq_ref[...] = jnp.where(lo, dq_acc[0], dq_acc[1]).astype(BF16)

        @pl.when(i == nq - 1)
        def _():
            dk_ref[...] = dk_acc[...].astype(BF16)
            dv_ref[...] = dv_acc[...].astype(BF16)

    nb = SB_W // PAIR
    qtile = pl.BlockSpec((QB, PAIR), lambda hp, i: (i, hp))
    col = pl.BlockSpec((T, PAIR), lambda hp, i: (0, hp))
    const = pl.BlockSpec((QB, 2 * QB), lambda hp, i: (0, 0))
    return pl.pallas_call(
        body, name=name, grid=(nb, nq), out_shape=(S((T, SB_W), BF16),) * 3,
        in_specs=[qtile, pl.BlockSpec((T, PAIR), lambda hp, i: (0, nb + hp)),
                  pl.BlockSpec((T, PAIR), lambda hp, i: (0, 2 * nb + hp)), qtile,
                  pl.BlockSpec((QB, 2 * QB), lambda hp, i: (i, hp)), const, const],
        out_specs=(qtile, col, col),
        scratch_shapes=[pltpu.VMEM((2, QB, PAIR), F32), pltpu.VMEM((T, PAIR), F32), pltpu.VMEM((T, PAIR), F32)],
        compiler_params=_params(("parallel", "arbitrary")),
    )(p, p, p, do, tot, upto, before)


def _t5_buckets():
    a = lax.broadcasted_iota(jnp.int32, (QB, QB), 0)
    c = lax.broadcasted_iota(jnp.int32, (QB, QB), 1)

    def bucket(dist):
        dist = jnp.maximum(dist, 0)
        max_exact = N_BUCKETS // 2
        d = jnp.maximum(dist, 1).astype(F32)
        large = max_exact + (jnp.log(d / max_exact) / math.log(MAX_DISTANCE / max_exact)
                             * (N_BUCKETS - max_exact)).astype(jnp.int32)
        large = jnp.minimum(large, N_BUCKETS - 1)
        return jnp.where(dist < max_exact, dist, large)

    return bucket(QB + a - c), bucket(a - c)


def _swa_common(i, kp_ref, kc_ref, vp_ref, vc_ref, bp_ref, bc_ref, rb_ref, bias_ref):
    lo, lane, row = _half_masks()

    @pl.when(i == 0)
    def _():
        for blk, b_ref in enumerate((bp_ref, bc_ref)):
            bk = b_ref[...]
            for h in range(8):
                acc = jnp.zeros((QB, QB), F32)
                for b in range(N_BUCKETS):
                    acc = jnp.where(bk == b, rb_ref[b, h], acc)
                bias_ref[h, blk] = acc

    band = [(lane > row) & (i > 0), lane <= row]

    def halves(ref):
        t = ref[...].astype(F32)
        sw = pltpu.roll(t, HEAD_DIM, 1)
        return [[jnp.where(lo, t, 0.0).astype(BF16), jnp.where(lo, 0.0, sw).astype(BF16)],
                [jnp.where(lo, sw, 0.0).astype(BF16), jnp.where(lo, 0.0, t).astype(BF16)]]

    ks = [halves(kp_ref), halves(kc_ref)]
    vs = [halves(vp_ref), halves(vc_ref)]
    return lo, band, ks, vs


def swa_fwd(p, sinks, rel_bias, bprev, bcur, name):
    T = p.shape[0]
    nq = T // QB
    kcol, vcol = (3 * SB_W + SWA_W) // KV_W, (3 * SB_W + SWA_W) // KV_W + 1

    def body(q_ref, kp_ref, kc_ref, vp_ref, vc_ref, bp_ref, bc_ref, sink_ref, rb_ref, o_ref, lse_ref, bias_ref):
        i = pl.program_id(0)
        lo, band, ks, vs = _swa_common(i, kp_ref, kc_ref, vp_ref, vc_ref, bp_ref, bc_ref, rb_ref, bias_ref)
        for g in range(4):
            kh = g // 2
            q2 = q_ref[:, g * PAIR:(g + 1) * PAIR]
            outs = []
            for pos in range(2):
                h = 2 * g + pos
                sc = [jnp.where(band[b], _nt(q2, ks[b][kh][pos]) * SCALE + bias_ref[h, b], NEG_INF) for b in range(2)]
                sink = sink_ref[0, h]
                m = jnp.maximum(jnp.maximum(jnp.max(sc[0], axis=1, keepdims=True),
                                            jnp.max(sc[1], axis=1, keepdims=True)), sink)
                e = [jnp.exp(sc[b] - m) for b in range(2)]
                den = jnp.sum(e[0], axis=1, keepdims=True) + jnp.sum(e[1], axis=1, keepdims=True) + jnp.exp(sink - m)
                outs.append(_nn((e[0] / den).astype(BF16), vs[0][kh][pos]) + _nn((e[1] / den).astype(BF16), vs[1][kh][pos]))
                lse_ref[:, h * QB:(h + 1) * QB] = jnp.broadcast_to(m + jnp.log(den), (QB, QB))
            o_ref[:, g * PAIR:(g + 1) * PAIR] = outs[0] + outs[1]

    kv = lambda col, prev: pl.BlockSpec((QB, KV_W), (lambda i: (jnp.maximum(i - 1, 0), col)) if prev else (lambda i: (i, col)))
    full = pl.BlockSpec((QB, QB), lambda i: (0, 0))
    smem = pl.BlockSpec(memory_space=pltpu.SMEM)
    return pl.pallas_call(
        body, name=name, grid=(nq,), out_shape=(S((T, SWA_W), F32), S((T, 8 * QB), F32)),
        in_specs=[pl.BlockSpec((QB, SWA_W), lambda i: (i, 3)), kv(kcol, True), kv(kcol, False), kv(vcol, True), kv(vcol, False),
                  full, full, smem, smem],
        out_specs=(pl.BlockSpec((QB, SWA_W), lambda i: (i, 0)), pl.BlockSpec((QB, 8 * QB), lambda i: (i, 0))),
        scratch_shapes=[pltpu.VMEM((8, 2, QB, QB), F32)],
        compiler_params=_params(("arbitrary",)),
    )(p, p, p, p, p, bprev, bcur, sinks, rel_bias)


def swa_bwd(p, do, lse, sinks, rel_bias, bprev, bcur, name):
    T = p.shape[0]
    nq = T // QB
    kcol, vcol = (3 * SB_W + SWA_W) // KV_W, (3 * SB_W + SWA_W) // KV_W + 1

    def body(q_ref, kp_ref, kc_ref, vp_ref, vc_ref, do_ref, lse_ref, bp_ref, bc_ref, sink_ref, rb_ref,
             dq_ref, dk_ref, dv_ref, dsink_ref, dsc_ref, bias_ref, dk_acc, dv_acc):
        i = pl.program_id(0)
        lo, band, ks, vs = _swa_common(i, kp_ref, kc_ref, vp_ref, vc_ref, bp_ref, bc_ref, rb_ref, bias_ref)

        @pl.when(i == 0)
        def _():
            dk_acc[...] = jnp.zeros_like(dk_acc)
            dv_acc[...] = jnp.zeros_like(dv_acc)
            dsc_ref[...] = jnp.zeros_like(dsc_ref)
            dsink_ref[...] = jnp.zeros_like(dsink_ref)

        lane1 = lax.broadcasted_iota(jnp.int32, (1, QB), 1)
        dsink = jnp.zeros((1, QB), F32)
        dk_parts = [[[None, None], [None, None]], [[None, None], [None, None]]]
        dv_parts = [[[None, None], [None, None]], [[None, None], [None, None]]]

        def add(parts, b, pos, kh, val):
            parts[b][pos][kh] = val if parts[b][pos][kh] is None else parts[b][pos][kh] + val

        for g in range(4):
            kh = g // 2
            q2 = q_ref[:, g * PAIR:(g + 1) * PAIR]
            q2f = q2.astype(F32)
            d2f = do_ref[:, g * PAIR:(g + 1) * PAIR]
            d2 = d2f.astype(BF16)
            dq = None
            for pos in range(2):
                h = 2 * g + pos
                keep = lo if pos == 0 else ~lo
                qh = jnp.where(keep, q2f, 0.0).astype(BF16)
                dh = jnp.where(keep, d2f, 0.0).astype(BF16)
                lse_h = lse_ref[:, h * QB:(h + 1) * QB]
                sink = sink_ref[0, h]
                pr = [jnp.exp(jnp.where(band[b], _nt(q2, ks[b][kh][pos]) * SCALE + bias_ref[h, b], NEG_INF) - lse_h)
                      for b in range(2)]
                dp = [_nt(d2, vs[b][kh][pos]) for b in range(2)]
                delta = jnp.sum(pr[0] * dp[0], axis=1, keepdims=True) + jnp.sum(pr[1] * dp[1], axis=1, keepdims=True)
                p_sink = jnp.exp(sink - lse_h[:, :1])
                dsink = dsink + jnp.where(lane1 == h, -jnp.sum(p_sink * delta), 0.0)
                for b in range(2):
                    dsc = pr[b] * (dp[b] - delta)
                    dsc_ref[h, b] += dsc
                    dzb = (dsc * SCALE).astype(BF16)
                    t = _nn(dzb, ks[b][kh][pos])
                    dq = t if dq is None else dq + t
                    add(dk_parts, b, pos, kh, _tn(dzb, qh))
                    add(dv_parts, b, pos, kh, _tn(pr[b].astype(BF16), dh))
            dq_ref[:, g * PAIR:(g + 1) * PAIR] = dq.astype(BF16)
        dsink_ref[...] += dsink

        def fold(parts, b):
            low = parts[b][0][0] + pltpu.roll(parts[b][1][0], HEAD_DIM, 1)
            high = parts[b][1][1] + pltpu.roll(parts[b][0][1], HEAD_DIM, 1)
            return jnp.where(lo, low, high)

        rp = pl.multiple_of(jnp.maximum(i - 1, 0) * QB, QB)
        rc = pl.multiple_of(i * QB, QB)
        dk_acc[pl.ds(rp, QB), :] += fold(dk_parts, 0)
        dv_acc[pl.ds(rp, QB), :] += fold(dv_parts, 0)
        dk_acc[pl.ds(rc, QB), :] += fold(dk_parts, 1)
        dv_acc[pl.ds(rc, QB), :] += fold(dv_parts, 1)

        @pl.when(i == nq - 1)
        def _():
            dk_ref[...] = dk_acc[...].astype(BF16)
            dv_ref[...] = dv_acc[...].astype(BF16)

    kv = lambda col, prev: pl.BlockSpec((QB, KV_W), (lambda i: (jnp.maximum(i - 1, 0), col)) if prev else (lambda i: (i, col)))
    full = pl.BlockSpec((QB, QB), lambda i: (0, 0))
    smem = pl.BlockSpec(memory_space=pltpu.SMEM)
    whole = lambda shape: pl.BlockSpec(shape, lambda i: (0,) * len(shape))
    return pl.pallas_call(
        body, name=name, grid=(nq,),
        out_shape=(S((T, SWA_W), BF16), S((T, KV_W), BF16), S((T, KV_W), BF16), S((1, QB), F32), S((8, 2, QB, QB), F32)),
        in_specs=[pl.BlockSpec((QB, SWA_W), lambda i: (i, 3)), kv(kcol, True), kv(kcol, False), kv(vcol, True), kv(vcol, False),
                  pl.BlockSpec((QB, SWA_W), lambda i: (i, 0)), pl.BlockSpec((QB, 8 * QB), lambda i: (i, 0)),
                  full, full, smem, smem],
        out_specs=(pl.BlockSpec((QB, SWA_W), lambda i: (i, 0)), whole((T, KV_W)), whole((T, KV_W)), whole((1, QB)),
                   whole((8, 2, QB, QB))),
        scratch_shapes=[pltpu.VMEM((8, 2, QB, QB), F32), pltpu.VMEM((T, KV_W), F32), pltpu.VMEM((T, KV_W), F32)],
        compiler_params=_params(("arbitrary",)),
    )(p, p, p, p, p, do, lse, bprev, bcur, sinks, rel_bias)


def mix_out_fwd(o_sb, o_sw, g_sb, g_sw, wout, h, name):
    T, D = h.shape
    M = SB_W + SWA_W
    tm = _tile(T, 256)

    def body(a_ref, b_ref, ga_ref, gb_ref, w_ref, h_ref, mx_ref, o_ref):
        mx_ref[:, :SB_W] = _rms(a_ref[...], ga_ref[...]).astype(BF16)
        mx_ref[:, SB_W:] = _rms(b_ref[...], gb_ref[...]).astype(BF16)
        o_ref[...] = h_ref[...] + _nn(mx_ref[...], w_ref[...])

    row = lambda n: pl.BlockSpec((tm, n), lambda i: (i, 0))
    vec = lambda n: pl.BlockSpec((1, n), lambda i: (0, 0))
    return pl.pallas_call(
        body, name=name, grid=(T // tm,), out_shape=(S((T, M), BF16), S((T, D), F32)),
        in_specs=[row(SB_W), row(SWA_W), vec(SB_W), vec(SWA_W), pl.BlockSpec((M, D), lambda i: (0, 0)), row(D)],
        out_specs=(row(M), row(D)),
        compiler_params=_params(("parallel",)),
    )(o_sb, o_sw, g_sb, g_sw, wout, h)


def loss_head(h, g, target, name):
    T, D = h.shape
    tm = _tile(T, 256)

    def body(h_ref, g_ref, t_ref, loss_ref, dh_ref, dhb_ref, dg_ref):
        @pl.when(pl.program_id(0) == 0)
        def _():
            loss_ref[...] = jnp.zeros_like(loss_ref)
            dg_ref[...] = jnp.zeros_like(dg_ref)
        x = h_ref[...]
        err = _rms(x, g_ref[...]) - t_ref[...]
        loss_ref[...] += jnp.full((1, QB), 0.5 * jnp.sum(jnp.mean(err * err, axis=-1)), F32)
        dx, dg = _rms_bwd(err / D, x, g_ref[...])
        dh_ref[...] = dx
        dhb_ref[...] = dx.astype(BF16)
        dg_ref[...] += dg

    row = pl.BlockSpec((tm, D), lambda i: (i, 0))
    vec = pl.BlockSpec((1, D), lambda i: (0, 0))
    return pl.pallas_call(
        body, name=name, grid=(T // tm,), out_shape=(S((1, QB), F32), S((T, D), F32), S((T, D), BF16), S((1, D), F32)),
        in_specs=[row, vec, row], out_specs=(pl.BlockSpec((1, QB), lambda i: (0, 0)), row, row, vec),
        compiler_params=_params(("arbitrary",)),
    )(h, g, target)


def ffn_down_bwd(dhb, wd, gate, up, name):
    T, D = dhb.shape
    F = wd.shape[0]
    tm, tn = _tile(T, 512), _tile(F, 256)

    def body(d_ref, w_ref, g_ref, u_ref, o_ref):
        da = 0.5 * _nt(d_ref[...], w_ref[...])
        gate = g_ref[...]
        s = jax.nn.sigmoid(gate)
        o_ref[0] = (da * u_ref[...] * (s * (1.0 + gate * (1.0 - s)))).astype(BF16)
        o_ref[1] = (da * gate * s).astype(BF16)

    tile = pl.BlockSpec((tm, tn), lambda i, j: (i, j))
    return pl.pallas_call(
        body, name=name, grid=(T // tm, F // tn), out_shape=S((2, T, F), BF16),
        in_specs=[pl.BlockSpec((tm, D), lambda i, j: (i, 0)), pl.BlockSpec((tn, D), lambda i, j: (j, 0)), tile, tile],
        out_specs=pl.BlockSpec((2, tm, tn), lambda i, j: (0, i, j)),
        compiler_params=_params(("parallel", "parallel")),
    )(dhb, wd, gate, up)


def tn_matmul(xs, y, alpha, name):
    B, T, N = xs.shape
    D = y.shape[1]
    tn = _tile(N, 256)

    def body(x_ref, y_ref, o_ref):
        o_ref[...] = alpha * _tn(x_ref[...], y_ref[...])

    return pl.pallas_call(
        body, name=name, grid=(B, N // tn), out_shape=S((B, N, D), F32),
        in_specs=[pl.BlockSpec((None, T, tn), lambda s, j: (s, 0, j)), pl.BlockSpec((T, D), lambda s, j: (0, 0))],
        out_specs=pl.BlockSpec((None, tn, D), lambda s, j: (s, j, 0)),
        compiler_params=_params(("parallel", "parallel")),
    )(xs, y)


def nn_rms_bwd(xs, ws, h_in, g, dh, tk, name):
    B, T, K = xs.shape
    D = ws.shape[2]
    tm = _tile(T, 512)
    nk = K // tk
    steps = B * nk

    def body(x_ref, w_ref, h_ref, g_ref, d_ref, o_ref, ob_ref, dg_ref, acc_ref):
        i, k = pl.program_id(0), pl.program_id(1)

        @pl.when((i == 0) & (k == 0))
        def _():
            dg_ref[...] = jnp.zeros_like(dg_ref)

        @pl.when(k == 0)
        def _():
            acc_ref[...] = jnp.zeros_like(acc_ref)
        acc_ref[...] += _nn(x_ref[...], w_ref[...])

        @pl.when(k == steps - 1)
        def _():
            dx, dg = _rms_bwd(acc_ref[...], h_ref[...], g_ref[...])
            out = d_ref[...] + dx
            o_ref[...] = out
            ob_ref[...] = out.astype(BF16)
            dg_ref[...] += dg

    row = pl.BlockSpec((tm, D), lambda i, k: (i, 0))
    vec = pl.BlockSpec((1, D), lambda i, k: (0, 0))
    return pl.pallas_call(
        body, name=name, grid=(T // tm, steps), out_shape=(S((T, D), F32), S((T, D), BF16), S((1, D), F32)),
        in_specs=[pl.BlockSpec((None, tm, tk), lambda i, k: (k // nk, i, k % nk)),
                  pl.BlockSpec((None, tk, D), lambda i, k: (k // nk, k % nk, 0)), row, vec, row],
        out_specs=(row, row, vec),
        scratch_shapes=[pltpu.VMEM((tm, D), F32)],
        compiler_params=_params(("arbitrary", "arbitrary")),
    )(xs, ws, h_in, g, dh)


def mix_out_bwd(dhb, wout, o_sb, o_sw, g_sb, g_sw, name):
    T, D = dhb.shape
    tm = _tile(T, 256)

    def body(d_ref, w_ref, a_ref, b_ref, ga_ref, gb_ref, da_ref, db_ref, dga_ref, dgb_ref):
        @pl.when(pl.program_id(0) == 0)
        def _():
            dga_ref[...] = jnp.zeros_like(dga_ref)
            dgb_ref[...] = jnp.zeros_like(dgb_ref)
        dm = _nt(d_ref[...], w_ref[...])
        dxa, dga = _rms_bwd(dm[:, :SB_W], a_ref[...], ga_ref[...])
        dxb, dgb = _rms_bwd(dm[:, SB_W:], b_ref[...], gb_ref[...])
        da_ref[...] = dxa
        db_ref[...] = dxb
        dga_ref[...] += dga
        dgb_ref[...] += dgb

    row = lambda n: pl.BlockSpec((tm, n), lambda i: (i, 0))
    vec = lambda n: pl.BlockSpec((1, n), lambda i: (0, 0))
    return pl.pallas_call(
        body, name=name, grid=(T // tm,),
        out_shape=(S((T, SB_W), F32), S((T, SWA_W), F32), S((1, SB_W), F32), S((1, SWA_W), F32)),
        in_specs=[row(D), pl.BlockSpec((SB_W + SWA_W, D), lambda i: (0, 0)), row(SB_W), row(SWA_W), vec(SB_W), vec(SWA_W)],
        out_specs=(row(SB_W), row(SWA_W), vec(SB_W), vec(SWA_W)),
        compiler_params=_params(("arbitrary",)),
    )(dhb, wout, o_sb, o_sw, g_sb, g_sw)


def rel_bias_grad(dscs, bprev, bcur, name):
    n = len(dscs)

    def body(*refs):
        bp_ref, bc_ref, o_ref = refs[n], refs[n + 1], refs[n + 2]
        bks = [bp_ref[...], bc_ref[...]]
        row = lax.broadcasted_iota(jnp.int32, (N_BUCKETS, QB), 0)
        lane = lax.broadcasted_iota(jnp.int32, (N_BUCKETS, QB), 1)
        out = jnp.zeros((N_BUCKETS, QB), F32)
        for h in range(8):
            tot = [sum(refs[l][h, b] for l in range(n)) for b in range(2)]
            for b in range(N_BUCKETS):
                val = jnp.sum(jnp.where(bks[0] == b, tot[0], 0.0)) + jnp.sum(jnp.where(bks[1] == b, tot[1], 0.0))
                out = jnp.where((row == b) & (lane == h), val, out)
        o_ref[...] = out

    return pl.pallas_call(body, name=name, out_shape=S((N_BUCKETS, QB), F32), compiler_params=_params())(*dscs, bprev, bcur)


def _adamw(w, g, m, v):
    m = ADAM_B1 * m + (1.0 - ADAM_B1) * g
    v = ADAM_B2 * v + (1.0 - ADAM_B2) * (g * g)
    m_hat = m / (1.0 - ADAM_B1 ** ADAM_STEP)
    v_hat = v / (1.0 - ADAM_B2 ** ADAM_STEP)
    delta = -ADAM_LR * (m_hat / (jnp.sqrt(v_hat) + ADAM_EPS) + ADAM_WD * w)
    return delta, m, v


def adamw_rows(w, g, m, v, name):
    L, R, C = w.shape
    tr = _tile(R, 256)

    def body(w_ref, g_ref, m_ref, v_ref, d_ref, mo_ref, vo_ref):
        d, mn, vn = _adamw(w_ref[...], g_ref[...], m_ref[...], v_ref[...])
        d_ref[...] = d
        mo_ref[...] = mn
        vo_ref[...] = vn

    tile = pl.BlockSpec((None, tr, C), lambda l, i: (l, i, 0))
    return pl.pallas_call(
        body, name=name, grid=(L, R // tr), out_shape=(S((L, R, C), F32),) * 3,
        in_specs=[tile] * 4, out_specs=(tile,) * 3,
        compiler_params=_params(("parallel", "parallel")),
    )(w, g, m, v)


def adamw_small(w, gs, m, v, name):
    R, C = w.shape

    def body(w_ref, g_ref, m_ref, v_ref, go_ref, d_ref, mo_ref, vo_ref):
        g = g_ref[0]
        for k in range(1, N_DEV):
            g = g + g_ref[k]
        d, mn, vn = _adamw(w_ref[...], g, m_ref[...], v_ref[...])
        go_ref[...] = g
        d_ref[...] = d
        mo_ref[...] = mn
        vo_ref[...] = vn

    return pl.pallas_call(body, name=name, out_shape=(S((R, C), F32),) * 4, compiler_params=_params())(w, gs, m, v)


def kernel(x, norm_ffn1, w_ffn1_gu, w_ffn1_down, norm_mix, w_in, sinks, norm_out_sb, norm_out_swa, w_out, norm_ffn2, w_ffn2_gu, w_ffn2_down, rel_bias, norm_final, loss_target, m_norm_ffn1, m_w_ffn1_gu, m_w_ffn1_down, m_norm_mix, m_w_in, m_sinks, m_norm_out_sb, m_norm_out_swa, m_w_out, m_norm_ffn2, m_w_ffn2_gu, m_w_ffn2_down, m_rel_bias, m_norm_final, v_norm_ffn1, v_w_ffn1_gu, v_w_ffn1_down, v_norm_mix, v_w_in, v_sinks, v_norm_out_sb, v_norm_out_swa, v_w_out, v_norm_ffn2, v_w_ffn2_gu, v_w_ffn2_down, v_rel_bias, v_norm_final):
    L = norm_ffn1.shape[0]
    T, D = x.shape[1], x.shape[2]
    F = w_ffn1_down.shape[1] * N_DEV
    h = x.reshape(T, D)
    target = loss_target.reshape(T, D)
    after, upto, before = _tri_consts()
    bprev, bcur = _t5_buckets()

    def gather(w, transpose, tag):
        out = []
        for l in range(L):
            wl = (w[l].T if transpose else w[l]).astype(BF16)
            full = all_gather_rows(wl, f"ag_{tag}{l}")
            out.append(full.reshape(N_DEV * wl.shape[0], wl.shape[1]))
        return out

    wgu1 = [w.reshape(2, F, D) for w in gather(w_ffn1_gu, True, "gu1_")]
    wd1 = gather(w_ffn1_down, False, "d1_")
    win = gather(w_in, True, "in_")
    wout = gather(w_out, False, "out_")
    wgu2 = [w.reshape(2, F, D) for w in gather(w_ffn2_gu, True, "gu2_")]
    wd2 = gather(w_ffn2_down, False, "d2_")

    vec = lambda a: a.reshape(1, -1)
    saved = []
    for l in range(L):
        s = {"h0": h}
        s["n1"], s["gate1"], s["up1"], s["a1"] = ffn_up_fwd(h, vec(norm_ffn1[l]), wgu1[l], f"ffn1_up{l}")
        h = ffn_down_fwd(s["a1"], wd1[l], h, f"ffn1_down{l}")
        s["h1"] = h
        s["n2"], s["p"] = mix_in_fwd(h, vec(norm_mix[l]), win[l], f"mix_in{l}")
        s["o_sb"], s["tot"] = sb_attn_fwd(s["p"], after, f"sb_fwd{l}")
        s["o_sw"], s["lse"] = swa_fwd(s["p"], vec(sinks[l]), rel_bias, bprev, bcur, f"swa_fwd{l}")
        s["mixed"], h = mix_out_fwd(s["o_sb"], s["o_sw"], vec(norm_out_sb[l]), vec(norm_out_swa[l]), wout[l], h, f"mix_out{l}")
        s["h2"] = h
        s["n3"], s["gate2"], s["up2"], s["a2"] = ffn_up_fwd(h, vec(norm_ffn2[l]), wgu2[l], f"ffn2_up{l}")
        h = ffn_down_fwd(s["a2"], wd2[l], h, f"ffn2_down{l}")
        saved.append(s)

    loss_part, dh, dhb, dg_final = loss_head(h, vec(norm_final), target, "loss_head")
    loss = lax.psum(loss_part[0, 0], ("x", "y", "c"))

    big = {k: [None] * L for k in ("gu1", "d1", "in", "out", "gu2", "d2")}
    small = {k: [None] * L for k in ("ffn1", "mix", "sinks", "osb", "osw", "ffn2", "dsc")}
    for l in reversed(range(L)):
        s = saved[l]

        def ffn_bwd(dh, dhb, wgu, wd, gate, up, a, n, h_in, g, tag):
            dgu = ffn_down_bwd(dhb, wd, gate, up, f"{tag}_down_bwd{l}")
            d_wd = tn_matmul(a[None], dhb, 0.5, f"{tag}_dwd{l}")[0]
            dh, dhb, dg = nn_rms_bwd(dgu, wgu, h_in, g, dh, F // 2, f"{tag}_up_bwd{l}")
            d_wgu = tn_matmul(dgu, n, 1.0, f"{tag}_dwgu{l}").reshape(2 * F, D)
            return dh, dhb, dg, d_wgu, d_wd

        dh, dhb, small["ffn2"][l], big["gu2"][l], big["d2"][l] = ffn_bwd(
            dh, dhb, wgu2[l], wd2[l], s["gate2"], s["up2"], s["a2"], s["n3"], s["h2"], vec(norm_ffn2[l]), "ffn2")

        do_sb, do_sw, small["osb"][l], small["osw"][l] = mix_out_bwd(
            dhb, wout[l], s["o_sb"], s["o_sw"], vec(norm_out_sb[l]), vec(norm_out_swa[l]), f"mix_out_bwd{l}")
        big["out"][l] = tn_matmul(s["mixed"][None], dhb, 1.0, f"dwout{l}")[0]
        dq_sb, dk_sb, dv_sb = sb_attn_bwd(s["p"], do_sb, s["tot"], upto, before, f"sb_bwd{l}")
        dq_sw, dk_sw, dv_sw, small["sinks"][l], small["dsc"][l] = swa_bwd(
            s["p"], do_sw, s["lse"], vec(sinks[l]), rel_bias, bprev, bcur, f"swa_bwd{l}")
        dp = jnp.concatenate([dq_sb, dk_sb, dv_sb, dq_sw, dk_sw, dv_sw], axis=1)
        dh, dhb, small["mix"][l] = nn_rms_bwd(dp[None], win[l][None], s["h1"], vec(norm_mix[l]), dh, IN_W // 2, f"mix_in_bwd{l}")
        big["in"][l] = tn_matmul(dp[None], s["n2"], 1.0, f"dwin{l}")[0]

        dh, dhb, small["ffn1"][l], big["gu1"][l], big["d1"][l] = ffn_bwd(
            dh, dhb, wgu1[l], wd1[l], s["gate1"], s["up1"], s["a1"], s["n1"], s["h0"], vec(norm_ffn1[l]), "ffn1")

    grad_x = dh.reshape(x.shape)

    def scatter(gs, transpose, tag):
        out = []
        for l in range(L):
            g3 = gs[l].reshape(N_DEV, -1, D)
            rows = g3.shape[1]
            r = rows if rows <= 352 else rows // 2
            parts = [reduce_scatter_rows(g3, off, r, f"rs_{tag}{l}_{off}") for off in range(0, rows, r)]
            mine = parts[0] if len(parts) == 1 else jnp.concatenate(parts, axis=0)
            out.append(mine.T if transpose else mine)
        return jnp.stack(out)

    g_gu1 = scatter(big["gu1"], True, "gu1_")
    g_d1 = scatter(big["d1"], False, "d1_")
    g_in = scatter(big["in"], True, "in_")
    g_out = scatter(big["out"], False, "out_")
    g_gu2 = scatter(big["gu2"], True, "gu2_")
    g_d2 = scatter(big["d2"], False, "d2_")

    upd = {}
    for nm, w, g, m, v in (("gu1", w_ffn1_gu, g_gu1, m_w_ffn1_gu, v_w_ffn1_gu), ("d1", w_ffn1_down, g_d1, m_w_ffn1_down, v_w_ffn1_down),
                           ("in", w_in, g_in, m_w_in, v_w_in), ("out", w_out, g_out, m_w_out, v_w_out),
                           ("gu2", w_ffn2_gu, g_gu2, m_w_ffn2_gu, v_w_ffn2_gu), ("d2", w_ffn2_down, g_d2, m_w_ffn2_down, v_w_ffn2_down)):
        upd[nm] = (g,) + tuple(adamw_rows(w, g, m, v, f"adamw_{nm}"))

    d_rel = rel_bias_grad(small["dsc"], bprev, bcur, "rel_bias_grad")[:, :8]

    PW = max(D, SB_W + SWA_W)

    def pack(ffn1, mix, ffn2, final, osb, osw, snk, rel):
        wide = lambda a: jnp.pad(a.reshape(-1), (0, PW - a.size))
        rows = [wide(ffn1[l]) for l in range(L)] + [wide(mix[l]) for l in range(L)] + [wide(ffn2[l]) for l in range(L)]
        rows.append(wide(final))
        rows += [wide(jnp.concatenate([osb[l].reshape(-1), osw[l].reshape(-1)])) for l in range(L)]
        rows.append(wide(jnp.concatenate([snk[l].reshape(-1)[:8] for l in range(L)] + [rel.reshape(-1)])))
        arr = jnp.stack(rows)
        return jnp.pad(arr, ((0, (-arr.shape[0]) % 8), (0, 0)))

    def unpack(arr):
        ffn1, mix, ffn2 = arr[0:L, :D], arr[L:2 * L, :D], arr[2 * L:3 * L, :D]
        final = arr[3 * L, :D]
        ob = arr[3 * L + 1:4 * L + 1]
        tail = arr[4 * L + 1]
        return (ffn1, mix, tail[:8 * L].reshape(L, 8), ob[:, :SB_W], ob[:, SB_W:SB_W + SWA_W], ffn2,
                tail[8 * L:8 * L + N_BUCKETS * 8].reshape(N_BUCKETS, 8), final)

    g_small = pack(small["ffn1"], small["mix"], small["ffn2"], dg_final, small["osb"], small["osw"], small["sinks"], d_rel)
    w_small = pack(norm_ffn1, norm_mix, norm_ffn2, norm_final, norm_out_sb, norm_out_swa, sinks, rel_bias)
    m_small = pack(m_norm_ffn1, m_norm_mix, m_norm_ffn2, m_norm_final, m_norm_out_sb, m_norm_out_swa, m_sinks, m_rel_bias)
    v_small = pack(v_norm_ffn1, v_norm_mix, v_norm_ffn2, v_norm_final, v_norm_out_sb, v_norm_out_swa, v_sinks, v_rel_bias)
    gs_small = all_gather_rows(g_small, "ag_small")
    small_out = [unpack(a) for a in adamw_small(w_small, gs_small, m_small, v_small, "adamw_small")]

    def group(k):
        sm = small_out[k]
        return (sm[0], upd["gu1"][k], upd["d1"][k], sm[1], upd["in"][k], sm[2], sm[3], sm[4], upd["out"][k], sm[5],
                upd["gu2"][k], upd["d2"][k], sm[6], sm[7])

    return (loss, grad_x, *group(0), *group(1), *group(2), *group(3))
```

```python
import math

import jax
import jax.numpy as jnp
from jax import lax
from jax.experimental import pallas as pl
from jax.experimental.pallas import tpu as pltpu

F32 = jnp.float32
BF16 = jnp.bfloat16
S = jax.ShapeDtypeStruct

N_DEV = 8
HEAD_DIM = 64
SB_HEADS = 8
PAIR = 2 * HEAD_DIM
SB_W = 512
SWA_W = 512
KV_W = 128
IN_W = 3 * SB_W + SWA_W + 2 * KV_W
QB = 128
N_BUCKETS = 32
MAX_DISTANCE = 128
EPS = 1e-6
NEG_INF = -1e30
SCALE = HEAD_DIM ** -0.5

ADAM_LR = 0.001
ADAM_B1 = 0.9
ADAM_B2 = 0.999
ADAM_EPS = 1e-08
ADAM_WD = 0.01
ADAM_STEP = 10

VMEM_LIMIT = 56 * 1024 * 1024
MESH = pl.DeviceIdType.MESH


def _params(sem=None, vmem=VMEM_LIMIT):
    return pltpu.CompilerParams(dimension_semantics=sem, vmem_limit_bytes=vmem)


def _nn(a, b):
    return jnp.dot(a, b, preferred_element_type=F32)


def _nt(a, b):
    return lax.dot_general(a, b, (((1,), (1,)), ((), ())), preferred_element_type=F32)


def _tn(a, b):
    return lax.dot_general(a, b, (((0,), (0,)), ((), ())), preferred_element_type=F32)


def _tri(x, m2):
    hi = x.astype(BF16)
    lo = (x - hi.astype(F32)).astype(BF16)
    return _nn(jnp.concatenate([hi, lo], axis=1), m2)


def _rms(x, g):
    r = lax.rsqrt(jnp.mean(x * x, axis=-1, keepdims=True) + EPS)
    return x * r * g


def _rms_bwd(dy, x, g):
    r = lax.rsqrt(jnp.mean(x * x, axis=-1, keepdims=True) + EPS)
    xhat = x * r
    u = dy * g
    dx = r * (u - xhat * jnp.mean(u * xhat, axis=-1, keepdims=True))
    return dx, jnp.sum(dy * xhat, axis=0, keepdims=True)


def _softplus_logsig(z):
    sp = jnp.maximum(z, 0.0) + jnp.log(1.0 + jnp.exp(-jnp.abs(z)))
    return sp, z - sp


def _tile(n, want):
    t = min(n, want)
    while n % t:
        t //= 2
    return t


def _place():
    x, y, c = lax.axis_index("x"), lax.axis_index("y"), lax.axis_index("c")
    chips = [(1 - x, y), (x, 1 - y), (1 - x, 1 - y)]
    return x, y, c, chips


def all_gather_rows(v, name):
    R, C = v.shape

    def body(v_ref, out_ref, send_sems, recv_sems, local_sem):
        x, y, c, chips = _place()
        me, sibling = (x, y, c), (x, y, 1 - c)

        def slot(px, py, pc):
            return out_ref.at[4 * px + 2 * py + pc]

        def copy(k, block, to, src=None):
            return pltpu.make_async_remote_copy(
                src_ref=slot(*block) if src is None else src, dst_ref=slot(*block),
                send_sem=send_sems.at[k], recv_sem=recv_sems.at[k], device_id=to, device_id_type=MESH)

        mine = pltpu.make_async_copy(v_ref, slot(*me), local_sem)
        mine.start()
        first = [copy(0, me, sibling, src=v_ref)]
        first += [copy(1 + j, me, (*chip, c), src=v_ref) for j, chip in enumerate(chips)]
        for cp in first:
            cp.start()
        passed = [copy(4 + j, (*chip, c), sibling) for j, chip in enumerate(chips)]
        for j, chip in enumerate(chips):
            copy(1 + j, (*chip, c), me).wait_recv()
            passed[j].start()
        copy(0, sibling, me).wait_recv()
        for j, chip in enumerate(chips):
            copy(4 + j, (*chip, 1 - c), me).wait_recv()
        for cp in first + passed:
            cp.wait_send()
        mine.wait()

    return pl.pallas_call(
        body, name=name, out_shape=S((N_DEV, R, C), v.dtype),
        in_specs=[pl.BlockSpec(memory_space=pl.ANY)], out_specs=pl.BlockSpec(memory_space=pl.ANY),
        scratch_shapes=[pltpu.SemaphoreType.DMA((7,)), pltpu.SemaphoreType.DMA((7,)), pltpu.SemaphoreType.DMA],
    )(v)


def reduce_scatter_rows(g, off, r, name):
    C = g.shape[2]
    rc = _tile(r, 32)

    def rows_loop(fn):
        def step(i, carry):
            fn(pl.ds(pl.multiple_of(i * rc, rc), rc))
            return carry
        lax.fori_loop(0, r // rc, step, 0)

    def body(g_ref, out_ref, give, mine, send_a, recv_a, send_b, recv_b, load_sems, send_sems, recv_sems):
        x, y, c, chips = _place()
        owners = [(x, y)] + chips

        def load(j, pc, dst):
            px, py = owners[j]
            return pltpu.make_async_copy(g_ref.at[4 * px + 2 * py + pc, pl.ds(off, r), :], dst.at[j],
                                         load_sems.at[j + 4 * (0 if dst is give else 1)])

        loads = [load(j, 1 - c, give) for j in range(4)] + [load(j, c, mine) for j in range(4)]
        for cp in loads:
            cp.start()
        for cp in loads[:4]:
            cp.wait()

        def to_bf16(rs):
            for j in range(4):
                send_a[j, rs, :] = give[j, rs, :].astype(BF16)
        rows_loop(to_bf16)
        first = pltpu.make_async_remote_copy(src_ref=send_a, dst_ref=recv_a, send_sem=send_sems.at[0],
                                             recv_sem=recv_sems.at[0], device_id=(x, y, 1 - c), device_id_type=MESH)
        first.start()
        for cp in loads[4:]:
            cp.wait()
        first.wait_recv()

        def add_sibling(rs):
            for j in range(4):
                s = mine[j, rs, :] + recv_a[j, rs, :].astype(F32)
                mine[j, rs, :] = s
                if j:
                    send_b[j - 1, rs, :] = s.astype(BF16)
        rows_loop(add_sibling)
        second = [pltpu.make_async_remote_copy(src_ref=send_b.at[j], dst_ref=recv_b.at[j], send_sem=send_sems.at[1 + j],
                                               recv_sem=recv_sems.at[1 + j], device_id=(*chips[j], c), device_id_type=MESH)
                  for j in range(3)]
        for cp in second:
            cp.start()
        for cp in second:
            cp.wait_recv()

        def add_chips(rs):
            s = mine[0, rs, :]
            for j in range(3):
                s = s + recv_b[j, rs, :].astype(F32)
            out_ref[rs, :] = s
        rows_loop(add_chips)
        first.wait_send()
        for cp in second:
            cp.wait_send()

    return pl.pallas_call(
        body, name=name, out_shape=S((r, C), F32),
        in_specs=[pl.BlockSpec(memory_space=pl.ANY)], out_specs=pl.BlockSpec(memory_space=pltpu.VMEM),
        scratch_shapes=[pltpu.VMEM((4, r, C), F32), pltpu.VMEM((4, r, C), F32),
                        pltpu.VMEM((4, r, C), BF16), pltpu.VMEM((4, r, C), BF16),
                        pltpu.VMEM((3, r, C), BF16), pltpu.VMEM((3, r, C), BF16),
                        pltpu.SemaphoreType.DMA((8,)), pltpu.SemaphoreType.DMA((4,)), pltpu.SemaphoreType.DMA((4,))],
        compiler_params=_params(),
    )(g)


def ffn_up_fwd(h, g, wgu, name):
    T, D = h.shape
    F = wgu.shape[1]
    tm, tn = _tile(T, 512), _tile(F, 256)

    def body(h_ref, g_ref, wg_ref, wu_ref, n_ref, gate_ref, up_ref, a_ref):
        @pl.when(pl.program_id(1) == 0)
        def _():
            n_ref[...] = _rms(h_ref[...], g_ref[...]).astype(BF16)
        n = n_ref[...]
        gate = _nt(n, wg_ref[...])
        up = _nt(n, wu_ref[...])
        gate_ref[...] = gate
        up_ref[...] = up
        a_ref[...] = (gate * jax.nn.sigmoid(gate) * up).astype(BF16)

    tile = pl.BlockSpec((tm, tn), lambda i, j: (i, j))
    return pl.pallas_call(
        body, name=name, grid=(T // tm, F // tn),
        out_shape=(S((T, D), BF16), S((T, F), F32), S((T, F), F32), S((T, F), BF16)),
        in_specs=[pl.BlockSpec((tm, D), lambda i, j: (i, 0)), pl.BlockSpec((1, D), lambda i, j: (0, 0)),
                  pl.BlockSpec((None, tn, D), lambda i, j: (0, j, 0)), pl.BlockSpec((None, tn, D), lambda i, j: (1, j, 0))],
        out_specs=(pl.BlockSpec((tm, D), lambda i, j: (i, 0)), tile, tile, tile),
        compiler_params=_params(("parallel", "arbitrary")),
    )(h, g, wgu, wgu)


def ffn_down_fwd(a, wd, h, name):
    T, F = a.shape
    D = wd.shape[1]
    tm = _tile(T, 256)

    def body(a_ref, w_ref, h_ref, o_ref):
        o_ref[...] = h_ref[...] + 0.5 * _nn(a_ref[...], w_ref[...])

    return pl.pallas_call(
        body, name=name, grid=(T // tm,), out_shape=S((T, D), F32),
        in_specs=[pl.BlockSpec((tm, F), lambda i: (i, 0)), pl.BlockSpec((F, D), lambda i: (0, 0)),
                  pl.BlockSpec((tm, D), lambda i: (i, 0))],
        out_specs=pl.BlockSpec((tm, D), lambda i: (i, 0)),
        compiler_params=_params(("parallel",)),
    )(a, wd, h)


def mix_in_fwd(h, g, win, name):
    T, D = h.shape
    N = win.shape[0]
    tm = _tile(T, 256)

    def body(h_ref, g_ref, w_ref, n_ref, p_ref):
        n = _rms(h_ref[...], g_ref[...]).astype(BF16)
        n_ref[...] = n
        p_ref[...] = _nt(n, w_ref[...]).astype(BF16)

    return pl.pallas_call(
        body, name=name, grid=(T // tm,), out_shape=(S((T, D), BF16), S((T, N), BF16)),
        in_specs=[pl.BlockSpec((tm, D), lambda i: (i, 0)), pl.BlockSpec((1, D), lambda i: (0, 0)),
                  pl.BlockSpec((N, D), lambda i: (0, 0))],
        out_specs=(pl.BlockSpec((tm, D), lambda i: (i, 0)), pl.BlockSpec((tm, N), lambda i: (i, 0))),
        compiler_params=_params(("parallel",)),
    )(h, g, win)


def _tri_consts():
    r = lax.broadcasted_iota(jnp.int32, (QB, QB), 0)
    c = lax.broadcasted_iota(jnp.int32, (QB, QB), 1)
    ones = jnp.ones((QB, QB), BF16)

    def stacked(tri):
        m = jnp.concatenate([tri.astype(BF16), ones], axis=1)
        return jnp.concatenate([m, m], axis=0)

    return stacked(r > c), stacked(r <= c), stacked(r < c)


def _half_masks():
    lane = lax.broadcasted_iota(jnp.int32, (QB, PAIR), 1)
    row = lax.broadcasted_iota(jnp.int32, (QB, PAIR), 0)
    return lane < HEAD_DIM, lane, row


def sb_attn_fwd(p, after, name):
    T = p.shape[0]
    nq = T // QB

    def body(q_ref, k_ref, v_ref, m_ref, o_ref, tot_ref, q_sc, acc_ref):
        i = pl.program_id(0)
        lo, lane, row = _half_masks()
        causal = lane < row
        for hp in range(SB_HEADS // 2):
            q2 = q_ref[:, hp * PAIR:(hp + 1) * PAIR].astype(F32) * SCALE
            q_sc[2 * hp] = jnp.where(lo, q2, 0.0).astype(BF16)
            q_sc[2 * hp + 1] = jnp.where(lo, 0.0, q2).astype(BF16)
        m2 = m_ref[...]

        def block(j, diag):
            r0 = pl.multiple_of(j * QB, QB)
            heads = range(SB_HEADS)
            k2 = [k_ref[pl.ds(r0, QB), hp * PAIR:(hp + 1) * PAIR] for hp in range(SB_HEADS // 2)]
            v2 = [v_ref[pl.ds(r0, QB), hp * PAIR:(hp + 1) * PAIR] for hp in range(SB_HEADS // 2)]
            z = [_nt(q_sc[h], k2[h // 2]) for h in heads]
            spls = [_softplus_logsig(z[h]) for h in heads]
            sp = [jnp.where(causal, spls[h][0], 0.0) if diag else spls[h][0] for h in heads]
            rr = [_tri(sp[h], m2) for h in heads]
            if diag:
                w = [jnp.where(causal, jnp.exp(spls[h][1] - rr[h][:, :QB]), 0.0) for h in heads]
                pv = [_nn(w[h].astype(BF16), v2[h // 2]) for h in heads]
                for h in heads:
                    acc_ref[h] = pv[h]
                    tot_ref[:, h * QB:(h + 1) * QB] = rr[h][:, QB:]
            else:
                c = [tot_ref[:, h * QB:(h + 1) * QB] for h in heads]
                w = [jnp.exp(spls[h][1] - (c[h] + rr[h][:, :QB])) for h in heads]
                pv = [_nn(w[h].astype(BF16), v2[h // 2]) for h in heads]
                for h in heads:
                    acc_ref[h] += pv[h]
                    tot_ref[:, h * QB:(h + 1) * QB] = c[h] + rr[h][:, QB:]

        block(i, True)

        def step(t, carry):
            block(i - 1 - t, False)
            return carry
        lax.fori_loop(0, i, step, 0)
        for hp in range(SB_HEADS // 2):
            o_ref[:, hp * PAIR:(hp + 1) * PAIR] = jnp.where(lo, acc_ref[2 * hp], acc_ref[2 * hp + 1])

    return pl.pallas_call(
        body, name=name, grid=(nq,), out_shape=(S((T, SB_W), F32), S((T, SB_HEADS * QB), F32)),
        in_specs=[pl.BlockSpec((QB, SB_W), lambda i: (i, 0)), pl.BlockSpec((T, SB_W), lambda i: (0, 1)),
                  pl.BlockSpec((T, SB_W), lambda i: (0, 2)), pl.BlockSpec((2 * QB, 2 * QB), lambda i: (0, 0))],
        out_specs=(pl.BlockSpec((QB, SB_W), lambda i: (i, 0)), pl.BlockSpec((QB, SB_HEADS * QB), lambda i: (i, 0))),
        scratch_shapes=[pltpu.VMEM((SB_HEADS, QB, PAIR), BF16), pltpu.VMEM((SB_HEADS, QB, PAIR), F32)],
        compiler_params=_params(("arbitrary",)),
    )(p, p, p, after)


def sb_attn_bwd(p, do, tot, upto, before, name):
    T = p.shape[0]
    nq = T // QB

    def body(q_ref, k_ref, v_ref, do_ref, tot_ref, mp_ref, mg_ref, dq_ref, dk_ref, dv_ref,
             q_sc, d_sc, pg_sc, dq_acc, dk_acc, dv_acc):
        i = pl.program_id(0)
        lo, lane, row = _half_masks()
        causal = lane < row
        for hp in range(SB_HEADS // 2):
            q2 = q_ref[:, hp * PAIR:(hp + 1) * PAIR].astype(F32) * SCALE
            d2 = do_ref[:, hp * PAIR:(hp + 1) * PAIR]
            q_sc[2 * hp] = jnp.where(lo, q2, 0.0).astype(BF16)
            q_sc[2 * hp + 1] = jnp.where(lo, 0.0, q2).astype(BF16)
            d_sc[2 * hp] = jnp.where(lo, d2, 0.0).astype(BF16)
            d_sc[2 * hp + 1] = jnp.where(lo, 0.0, d2).astype(BF16)
        mp, mg = mp_ref[...], mg_ref[...]

        @pl.when(i == 0)
        def _():
            dk_acc[...] = jnp.zeros_like(dk_acc)
            dv_acc[...] = jnp.zeros_like(dv_acc)
        pg_sc[...] = jnp.zeros_like(pg_sc)
        dq_acc[...] = jnp.zeros_like(dq_acc)

        def block(j, diag):
            r0 = pl.multiple_of(j * QB, QB)
            heads = range(SB_HEADS)
            pairs = range(SB_HEADS // 2)
            k2 = [k_ref[pl.ds(r0, QB), hp * PAIR:(hp + 1) * PAIR] for hp in pairs]
            v2 = [v_ref[pl.ds(r0, QB), hp * PAIR:(hp + 1) * PAIR] for hp in pairs]
            z = [_nt(q_sc[h], k2[h // 2]) for h in heads]
            dw = [_nt(d_sc[h], v2[h // 2]) for h in heads]
            spls = [_softplus_logsig(z[h]) for h in heads]
            sp = [jnp.where(causal, spls[h][0], 0.0) if diag else spls[h][0] for h in heads]
            rr = [_tri(sp[h], mp) for h in heads]
            pc = [pg_sc[2 * h] for h in heads]
            w = [jnp.exp(spls[h][1] - (tot_ref[:, h * QB:(h + 1) * QB] - (pc[h] + rr[h][:, :QB]))) for h in heads]
            if diag:
                w = [jnp.where(causal, w[h], 0.0) for h in heads]
            gg = [dw[h] * w[h] for h in heads]
            rg = [_tri(gg[h], mg) for h in heads]
            gc = [pg_sc[2 * h + 1] for h in heads]
            dz = [gg[h] - (gg[h] + gc[h] + rg[h][:, :QB]) * jnp.exp(spls[h][1]) for h in heads]
            if diag:
                dz = [jnp.where(causal, dz[h], 0.0) for h in heads]
            dzb = [dz[h].astype(BF16) for h in heads]
            wb = [w[h].astype(BF16) for h in heads]
            dq = [_nn(dzb[h], k2[h // 2]) for h in heads]
            dk = [_tn(dzb[2 * hp], q_sc[2 * hp]) + _tn(dzb[2 * hp + 1], q_sc[2 * hp + 1]) for hp in pairs]
            dv = [_tn(wb[2 * hp], d_sc[2 * hp]) + _tn(wb[2 * hp + 1], d_sc[2 * hp + 1]) for hp in pairs]
            for h in heads:
                dq_acc[h] += dq[h]
                if not diag:
                    pg_sc[2 * h] = pc[h] + rr[h][:, QB:]
                    pg_sc[2 * h + 1] = gc[h] + rg[h][:, QB:]
            for hp in pairs:
                dk_acc[pl.ds(r0, QB), hp * PAIR:(hp + 1) * PAIR] += dk[hp]
                dv_acc[pl.ds(r0, QB), hp * PAIR:(hp + 1) * PAIR] += dv[hp]

        def step(t, carry):
            block(t, False)
            return carry
        lax.fori_loop(0, i, step, 0)
        block(i, True)
        for hp in range(SB_HEADS // 2):
            dq = jnp.where(lo, dq_acc[2 * hp], dq_acc[2 * hp + 1]) * SCALE
            dq_ref[:, hp * PAIR:(hp + 1) * PAIR] = dq.astype(BF16)

        @pl.when(i == nq - 1)
        def _():
            dk_ref[...] = dk_acc[...].astype(BF16)
            dv_ref[...] = dv_acc[...].astype(BF16)

    qtile = pl.BlockSpec((QB, SB_W), lambda i: (i, 0))
    whole = pl.BlockSpec((T, SB_W), lambda i: (0, 0))
    const = pl.BlockSpec((2 * QB, 2 * QB), lambda i: (0, 0))
    return pl.pallas_call(
        body, name=name, grid=(nq,), out_shape=(S((T, SB_W), BF16),) * 3,
        in_specs=[qtile, pl.BlockSpec((T, SB_W), lambda i: (0, 1)), pl.BlockSpec((T, SB_W), lambda i: (0, 2)), qtile,
                  pl.BlockSpec((QB, SB_HEADS * QB), lambda i: (i, 0)), const, const],
        out_specs=(qtile, whole, whole),
        scratch_shapes=[pltpu.VMEM((SB_HEADS, QB, PAIR), BF16), pltpu.VMEM((SB_HEADS, QB, PAIR), BF16),
                        pltpu.VMEM((2 * SB_HEADS, QB, QB), F32), pltpu.VMEM((SB_HEADS, QB, PAIR), F32),
                        pltpu.VMEM((T, SB_W), F32), pltpu.VMEM((T, SB_W), F32)],
        compiler_params=_params(("arbitrary",)),
    )(p, p, p, do, tot, upto, before)


def _t5_buckets():
    a = lax.broadcasted_iota(jnp.int32, (QB, QB), 0)
    c = lax.broadcasted_iota(jnp.int32, (QB, QB), 1)

    def bucket(dist):
        dist = jnp.maximum(dist, 0)
        max_exact = N_BUCKETS // 2
        d = jnp.maximum(dist, 1).astype(F32)
        large = max_exact + (jnp.log(d / max_exact) / math.log(MAX_DISTANCE / max_exact)
                             * (N_BUCKETS - max_exact)).astype(jnp.int32)
        large = jnp.minimum(large, N_BUCKETS - 1)
        return jnp.where(dist < max_exact, dist, large)

    return bucket(QB + a - c), bucket(a - c)


def _swa_common(i, kp_ref, kc_ref, vp_ref, vc_ref, bp_ref, bc_ref, rb_ref, bias_ref):
    lo, lane, row = _half_masks()

    @pl.when(i == 0)
    def _():
        for blk, b_ref in enumerate((bp_ref, bc_ref)):
            bk = b_ref[...]
            for h in range(8):
                acc = jnp.zeros((QB, QB), F32)
                for b in range(N_BUCKETS):
                    acc = jnp.where(bk == b, rb_ref[b, h], acc)
                bias_ref[h, blk] = acc

    band = [(lane > row) & (i > 0), lane <= row]

    def halves(ref):
        t = ref[...].astype(F32)
        sw = pltpu.roll(t, HEAD_DIM, 1)
        return [[jnp.where(lo, t, 0.0).astype(BF16), jnp.where(lo, 0.0, sw).astype(BF16)],
                [jnp.where(lo, sw, 0.0).astype(BF16), jnp.where(lo, 0.0, t).astype(BF16)]]

    ks = [halves(kp_ref), halves(kc_ref)]
    vs = [halves(vp_ref), halves(vc_ref)]
    return lo, band, ks, vs


def swa_fwd(p, sinks, rel_bias, bprev, bcur, name):
    T = p.shape[0]
    nq = T // QB
    kcol, vcol = (3 * SB_W + SWA_W) // KV_W, (3 * SB_W + SWA_W) // KV_W + 1

    def body(q_ref, kp_ref, kc_ref, vp_ref, vc_ref, bp_ref, bc_ref, sink_ref, rb_ref, o_ref, lse_ref, bias_ref):
        i = pl.program_id(0)
        lo, band, ks, vs = _swa_common(i, kp_ref, kc_ref, vp_ref, vc_ref, bp_ref, bc_ref, rb_ref, bias_ref)
        for g in range(4):
            kh = g // 2
            q2 = q_ref[:, g * PAIR:(g + 1) * PAIR]
            outs = []
            for pos in range(2):
                h = 2 * g + pos
                sc = [jnp.where(band[b], _nt(q2, ks[b][kh][pos]) * SCALE + bias_ref[h, b], NEG_INF) for b in range(2)]
                sink = sink_ref[0, h]
                m = jnp.maximum(jnp.maximum(jnp.max(sc[0], axis=1, keepdims=True),
                                            jnp.max(sc[1], axis=1, keepdims=True)), sink)
                e = [jnp.exp(sc[b] - m) for b in range(2)]
                den = jnp.sum(e[0], axis=1, keepdims=True) + jnp.sum(e[1], axis=1, keepdims=True) + jnp.exp(sink - m)
                outs.append(_nn((e[0] / den).astype(BF16), vs[0][kh][pos]) + _nn((e[1] / den).astype(BF16), vs[1][kh][pos]))
                lse_ref[:, h * QB:(h + 1) * QB] = jnp.broadcast_to(m + jnp.log(den), (QB, QB))
            o_ref[:, g * PAIR:(g + 1) * PAIR] = outs[0] + outs[1]

    kv = lambda col, prev: pl.BlockSpec((QB, KV_W), (lambda i: (jnp.maximum(i - 1, 0), col)) if prev else (lambda i: (i, col)))
    full = pl.BlockSpec((QB, QB), lambda i: (0, 0))
    smem = pl.BlockSpec(memory_space=pltpu.SMEM)
    return pl.pallas_call(
        body, name=name, grid=(nq,), out_shape=(S((T, SWA_W), F32), S((T, 8 * QB), F32)),
        in_specs=[pl.BlockSpec((QB, SWA_W), lambda i: (i, 3)), kv(kcol, True), kv(kcol, False), kv(vcol, True), kv(vcol, False),
                  full, full, smem, smem],
        out_specs=(pl.BlockSpec((QB, SWA_W), lambda i: (i, 0)), pl.BlockSpec((QB, 8 * QB), lambda i: (i, 0))),
        scratch_shapes=[pltpu.VMEM((8, 2, QB, QB), F32)],
        compiler_params=_params(("arbitrary",)),
    )(p, p, p, p, p, bprev, bcur, sinks, rel_bias)


def swa_bwd(p, do, lse, sinks, rel_bias, bprev, bcur, name):
    T = p.shape[0]
    nq = T // QB
    kcol, vcol = (3 * SB_W + SWA_W) // KV_W, (3 * SB_W + SWA_W) // KV_W + 1

    def body(q_ref, kp_ref, kc_ref, vp_ref, vc_ref, do_ref, lse_ref, bp_ref, bc_ref, sink_ref, rb_ref,
             dq_ref, dk_ref, dv_ref, dsink_ref, dsc_ref, bias_ref, dk_acc, dv_acc):
        i = pl.program_id(0)
        lo, band, ks, vs = _swa_common(i, kp_ref, kc_ref, vp_ref, vc_ref, bp_ref, bc_ref, rb_ref, bias_ref)

        @pl.when(i == 0)
        def _():
            dk_acc[...] = jnp.zeros_like(dk_acc)
            dv_acc[...] = jnp.zeros_like(dv_acc)
            dsc_ref[...] = jnp.zeros_like(dsc_ref)
            dsink_ref[...] = jnp.zeros_like(dsink_ref)

        lane1 = lax.broadcasted_iota(jnp.int32, (1, QB), 1)
        dsink = jnp.zeros((1, QB), F32)
        dk_parts = [[[None, None], [None, None]], [[None, None], [None, None]]]
        dv_parts = [[[None, None], [None, None]], [[None, None], [None, None]]]

        def add(parts, b, pos, kh, val):
            parts[b][pos][kh] = val if parts[b][pos][kh] is None else parts[b][pos][kh] + val

        for g in range(4):
            kh = g // 2
            q2 = q_ref[:, g * PAIR:(g + 1) * PAIR]
            q2f = q2.astype(F32)
            d2f = do_ref[:, g * PAIR:(g + 1) * PAIR]
            d2 = d2f.astype(BF16)
            dq = None
            for pos in range(2):
                h = 2 * g + pos
                keep = lo if pos == 0 else ~lo
                qh = jnp.where(keep, q2f, 0.0).astype(BF16)
                dh = jnp.where(keep, d2f, 0.0).astype(BF16)
                lse_h = lse_ref[:, h * QB:(h + 1) * QB]
                sink = sink_ref[0, h]
                pr = [jnp.exp(jnp.where(band[b], _nt(q2, ks[b][kh][pos]) * SCALE + bias_ref[h, b], NEG_INF) - lse_h)
                      for b in range(2)]
                dp = [_nt(d2, vs[b][kh][pos]) for b in range(2)]
                delta = jnp.sum(pr[0] * dp[0], axis=1, keepdims=True) + jnp.sum(pr[1] * dp[1], axis=1, keepdims=True)
                p_sink = jnp.exp(sink - lse_h[:, :1])
                dsink = dsink + jnp.where(lane1 == h, -jnp.sum(p_sink * delta), 0.0)
                for b in range(2):
                    dsc = pr[b] * (dp[b] - delta)
                    dsc_ref[h, b] += dsc
                    dzb = (dsc * SCALE).astype(BF16)
                    t = _nn(dzb, ks[b][kh][pos])
                    dq = t if dq is None else dq + t
                    add(dk_parts, b, pos, kh, _tn(dzb, qh))
                    add(dv_parts, b, pos, kh, _tn(pr[b].astype(BF16), dh))
            dq_ref[:, g * PAIR:(g + 1) * PAIR] = dq.astype(BF16)
        dsink_ref[...] += dsink

        def fold(parts, b):
            low = parts[b][0][0] + pltpu.roll(parts[b][1][0], HEAD_DIM, 1)
            high = parts[b][1][1] + pltpu.roll(parts[b][0][1], HEAD_DIM, 1)
            return jnp.where(lo, low, high)

        rp = pl.multiple_of(jnp.maximum(i - 1, 0) * QB, QB)
        rc = pl.multiple_of(i * QB, QB)
        dk_acc[pl.ds(rp, QB), :] += fold(dk_parts, 0)
        dv_acc[pl.ds(rp, QB), :] += fold(dv_parts, 0)
        dk_acc[pl.ds(rc, QB), :] += fold(dk_parts, 1)
        dv_acc[pl.ds(rc, QB), :] += fold(dv_parts, 1)

        @pl.when(i == nq - 1)
        def _():
            dk_ref[...] = dk_acc[...].astype(BF16)
            dv_ref[...] = dv_acc[...].astype(BF16)

    kv = lambda col, prev: pl.BlockSpec((QB, KV_W), (lambda i: (jnp.maximum(i - 1, 0), col)) if prev else (lambda i: (i, col)))
    full = pl.BlockSpec((QB, QB), lambda i: (0, 0))
    smem = pl.BlockSpec(memory_space=pltpu.SMEM)
    whole = lambda shape: pl.BlockSpec(shape, lambda i: (0,) * len(shape))
    return pl.pallas_call(
        body, name=name, grid=(nq,),
        out_shape=(S((T, SWA_W), BF16), S((T, KV_W), BF16), S((T, KV_W), BF16), S((1, QB), F32), S((8, 2, QB, QB), F32)),
        in_specs=[pl.BlockSpec((QB, SWA_W), lambda i: (i, 3)), kv(kcol, True), kv(kcol, False), kv(vcol, True), kv(vcol, False),
                  pl.BlockSpec((QB, SWA_W), lambda i: (i, 0)), pl.BlockSpec((QB, 8 * QB), lambda i: (i, 0)),
                  full, full, smem, smem],
        out_specs=(pl.BlockSpec((QB, SWA_W), lambda i: (i, 0)), whole((T, KV_W)), whole((T, KV_W)), whole((1, QB)),
                   whole((8, 2, QB, QB))),
        scratch_shapes=[pltpu.VMEM((8, 2, QB, QB), F32), pltpu.VMEM((T, KV_W), F32), pltpu.VMEM((T, KV_W), F32)],
        compiler_params=_params(("arbitrary",)),
    )(p, p, p, p, p, do, lse, bprev, bcur, sinks, rel_bias)


def mix_out_fwd(o_sb, o_sw, g_sb, g_sw, wout, h, name):
    T, D = h.shape
    M = SB_W + SWA_W
    tm = _tile(T, 256)

    def body(a_ref, b_ref, ga_ref, gb_ref, w_ref, h_ref, mx_ref, o_ref):
        mx_ref[:, :SB_W] = _rms(a_ref[...], ga_ref[...]).astype(BF16)
        mx_ref[:, SB_W:] = _rms(b_ref[...], gb_ref[...]).astype(BF16)
        o_ref[...] = h_ref[...] + _nn(mx_ref[...], w_ref[...])

    row = lambda n: pl.BlockSpec((tm, n), lambda i: (i, 0))
    vec = lambda n: pl.BlockSpec((1, n), lambda i: (0, 0))
    return pl.pallas_call(
        body, name=name, grid=(T // tm,), out_shape=(S((T, M), BF16), S((T, D), F32)),
        in_specs=[row(SB_W), row(SWA_W), vec(SB_W), vec(SWA_W), pl.BlockSpec((M, D), lambda i: (0, 0)), row(D)],
        out_specs=(row(M), row(D)),
        compiler_params=_params(("parallel",)),
    )(o_sb, o_sw, g_sb, g_sw, wout, h)


def loss_head(h, g, target, name):
    T, D = h.shape
    tm = _tile(T, 256)

    def body(h_ref, g_ref, t_ref, loss_ref, dh_ref, dhb_ref, dg_ref):
        @pl.when(pl.program_id(0) == 0)
        def _():
            loss_ref[...] = jnp.zeros_like(loss_ref)
            dg_ref[...] = jnp.zeros_like(dg_ref)
        x = h_ref[...]
        err = _rms(x, g_ref[...]) - t_ref[...]
        loss_ref[...] += jnp.full((1, QB), 0.5 * jnp.sum(jnp.mean(err * err, axis=-1)), F32)
        dx, dg = _rms_bwd(err / D, x, g_ref[...])
        dh_ref[...] = dx
        dhb_ref[...] = dx.astype(BF16)
        dg_ref[...] += dg

    row = pl.BlockSpec((tm, D), lambda i: (i, 0))
    vec = pl.BlockSpec((1, D), lambda i: (0, 0))
    return pl.pallas_call(
        body, name=name, grid=(T // tm,), out_shape=(S((1, QB), F32), S((T, D), F32), S((T, D), BF16), S((1, D), F32)),
        in_specs=[row, vec, row], out_specs=(pl.BlockSpec((1, QB), lambda i: (0, 0)), row, row, vec),
        compiler_params=_params(("arbitrary",)),
    )(h, g, target)


def ffn_down_bwd(dhb, wd, gate, up, name):
    T, D = dhb.shape
    F = wd.shape[0]
    tm, tn = _tile(T, 512), _tile(F, 256)

    def body(d_ref, w_ref, g_ref, u_ref, o_ref):
        da = 0.5 * _nt(d_ref[...], w_ref[...])
        gate = g_ref[...]
        s = jax.nn.sigmoid(gate)
        o_ref[0] = (da * u_ref[...] * (s * (1.0 + gate * (1.0 - s)))).astype(BF16)
        o_ref[1] = (da * gate * s).astype(BF16)

    tile = pl.BlockSpec((tm, tn), lambda i, j: (i, j))
    return pl.pallas_call(
        body, name=name, grid=(T // tm, F // tn), out_shape=S((2, T, F), BF16),
        in_specs=[pl.BlockSpec((tm, D), lambda i, j: (i, 0)), pl.BlockSpec((tn, D), lambda i, j: (j, 0)), tile, tile],
        out_specs=pl.BlockSpec((2, tm, tn), lambda i, j: (0, i, j)),
        compiler_params=_params(("parallel", "parallel")),
    )(dhb, wd, gate, up)


def tn_matmul(xs, y, alpha, name):
    B, T, N = xs.shape
    D = y.shape[1]
    tn = _tile(N, 256)

    def body(x_ref, y_ref, o_ref):
        o_ref[...] = alpha * _tn(x_ref[...], y_ref[...])

    return pl.pallas_call(
        body, name=name, grid=(B, N // tn), out_shape=S((B, N, D), F32),
        in_specs=[pl.BlockSpec((None, T, tn), lambda s, j: (s, 0, j)), pl.BlockSpec((T, D), lambda s, j: (0, 0))],
        out_specs=pl.BlockSpec((None, tn, D), lambda s, j: (s, j, 0)),
        compiler_params=_params(("parallel", "parallel")),
    )(xs, y)


def nn_rms_bwd(xs, ws, h_in, g, dh, tk, name):
    B, T, K = xs.shape
    D = ws.shape[2]
    tm = _tile(T, 512)
    nk = K // tk
    steps = B * nk

    def body(x_ref, w_ref, h_ref, g_ref, d_ref, o_ref, ob_ref, dg_ref, acc_ref):
        i, k = pl.program_id(0), pl.program_id(1)

        @pl.when((i == 0) & (k == 0))
        def _():
            dg_ref[...] = jnp.zeros_like(dg_ref)

        @pl.when(k == 0)
        def _():
            acc_ref[...] = jnp.zeros_like(acc_ref)
        acc_ref[...] += _nn(x_ref[...], w_ref[...])

        @pl.when(k == steps - 1)
        def _():
            dx, dg = _rms_bwd(acc_ref[...], h_ref[...], g_ref[...])
            out = d_ref[...] + dx
            o_ref[...] = out
            ob_ref[...] = out.astype(BF16)
            dg_ref[...] += dg

    row = pl.BlockSpec((tm, D), lambda i, k: (i, 0))
    vec = pl.BlockSpec((1, D), lambda i, k: (0, 0))
    return pl.pallas_call(
        body, name=name, grid=(T // tm, steps), out_shape=(S((T, D), F32), S((T, D), BF16), S((1, D), F32)),
        in_specs=[pl.BlockSpec((None, tm, tk), lambda i, k: (k // nk, i, k % nk)),
                  pl.BlockSpec((None, tk, D), lambda i, k: (k // nk, k % nk, 0)), row, vec, row],
        out_specs=(row, row, vec),
        scratch_shapes=[pltpu.VMEM((tm, D), F32)],
        compiler_params=_params(("arbitrary", "arbitrary")),
    )(xs, ws, h_in, g, dh)


def mix_out_bwd(dhb, wout, o_sb, o_sw, g_sb, g_sw, name):
    T, D = dhb.shape
    tm = _tile(T, 256)

    def body(d_ref, w_ref, a_ref, b_ref, ga_ref, gb_ref, da_ref, db_ref, dga_ref, dgb_ref):
        @pl.when(pl.program_id(0) == 0)
        def _():
            dga_ref[...] = jnp.zeros_like(dga_ref)
            dgb_ref[...] = jnp.zeros_like(dgb_ref)
        dm = _nt(d_ref[...], w_ref[...])
        dxa, dga = _rms_bwd(dm[:, :SB_W], a_ref[...], ga_ref[...])
        dxb, dgb = _rms_bwd(dm[:, SB_W:], b_ref[...], gb_ref[...])
        da_ref[...] = dxa
        db_ref[...] = dxb
        dga_ref[...] += dga
        dgb_ref[...] += dgb

    row = lambda n: pl.BlockSpec((tm, n), lambda i: (i, 0))
    vec = lambda n: pl.BlockSpec((1, n), lambda i: (0, 0))
    return pl.pallas_call(
        body, name=name, grid=(T // tm,),
        out_shape=(S((T, SB_W), F32), S((T, SWA_W), F32), S((1, SB_W), F32), S((1, SWA_W), F32)),
        in_specs=[row(D), pl.BlockSpec((SB_W + SWA_W, D), lambda i: (0, 0)), row(SB_W), row(SWA_W), vec(SB_W), vec(SWA_W)],
        out_specs=(row(SB_W), row(SWA_W), vec(SB_W), vec(SWA_W)),
        compiler_params=_params(("arbitrary",)),
    )(dhb, wout, o_sb, o_sw, g_sb, g_sw)


def rel_bias_grad(dscs, bprev, bcur, name):
    n = len(dscs)

    def body(*refs):
        bp_ref, bc_ref, o_ref = refs[n], refs[n + 1], refs[n + 2]
        bks = [bp_ref[...], bc_ref[...]]
        row = lax.broadcasted_iota(jnp.int32, (N_BUCKETS, QB), 0)
        lane = lax.broadcasted_iota(jnp.int32, (N_BUCKETS, QB), 1)
        out = jnp.zeros((N_BUCKETS, QB), F32)
        for h in range(8):
            tot = [sum(refs[l][h, b] for l in range(n)) for b in range(2)]
            for b in range(N_BUCKETS):
                val = jnp.sum(jnp.where(bks[0] == b, tot[0], 0.0)) + jnp.sum(jnp.where(bks[1] == b, tot[1], 0.0))
                out = jnp.where((row == b) & (lane == h), val, out)
        o_ref[...] = out

    return pl.pallas_call(body, name=name, out_shape=S((N_BUCKETS, QB), F32), compiler_params=_params())(*dscs, bprev, bcur)


def _adamw(w, g, m, v):
    m = ADAM_B1 * m + (1.0 - ADAM_B1) * g
    v = ADAM_B2 * v + (1.0 - ADAM_B2) * (g * g)
    m_hat = m / (1.0 - ADAM_B1 ** ADAM_STEP)
    v_hat = v / (1.0 - ADAM_B2 ** ADAM_STEP)
    delta = -ADAM_LR * (m_hat / (jnp.sqrt(v_hat) + ADAM_EPS) + ADAM_WD * w)
    return delta, m, v


def adamw_rows(w, g, m, v, name):
    L, R, C = w.shape
    tr = _tile(R, 256)

    def body(w_ref, g_ref, m_ref, v_ref, d_ref, mo_ref, vo_ref):
        d, mn, vn = _adamw(w_ref[...], g_ref[...], m_ref[...], v_ref[...])
        d_ref[...] = d
        mo_ref[...] = mn
        vo_ref[...] = vn

    tile = pl.BlockSpec((None, tr, C), lambda l, i: (l, i, 0))
    return pl.pallas_call(
        body, name=name, grid=(L, R // tr), out_shape=(S((L, R, C), F32),) * 3,
        in_specs=[tile] * 4, out_specs=(tile,) * 3,
        compiler_params=_params(("parallel", "parallel")),
    )(w, g, m, v)


def adamw_small(w, gs, m, v, name):
    R, C = w.shape

    def body(w_ref, g_ref, m_ref, v_ref, go_ref, d_ref, mo_ref, vo_ref):
        g = g_ref[0]
        for k in range(1, N_DEV):
            g = g + g_ref[k]
        d, mn, vn = _adamw(w_ref[...], g, m_ref[...], v_ref[...])
        go_ref[...] = g
        d_ref[...] = d
        mo_ref[...] = mn
        vo_ref[...] = vn

    return pl.pallas_call(body, name=name, out_shape=(S((R, C), F32),) * 4, compiler_params=_params())(w, gs, m, v)


def kernel(x, norm_ffn1, w_ffn1_gu, w_ffn1_down, norm_mix, w_in, sinks, norm_out_sb, norm_out_swa, w_out, norm_ffn2, w_ffn2_gu, w_ffn2_down, rel_bias, norm_final, loss_target, m_norm_ffn1, m_w_ffn1_gu, m_w_ffn1_down, m_norm_mix, m_w_in, m_sinks, m_norm_out_sb, m_norm_out_swa, m_w_out, m_norm_ffn2, m_w_ffn2_gu, m_w_ffn2_down, m_rel_bias, m_norm_final, v_norm_ffn1, v_w_ffn1_gu, v_w_ffn1_down, v_norm_mix, v_w_in, v_sinks, v_norm_out_sb, v_norm_out_swa, v_w_out, v_norm_ffn2, v_w_ffn2_gu, v_w_ffn2_down, v_rel_bias, v_norm_final):
    L = norm_ffn1.shape[0]
    T, D = x.shape[1], x.shape[2]
    F = w_ffn1_down.shape[1] * N_DEV
    h = x.reshape(T, D)
    target = loss_target.reshape(T, D)
    after, upto, before = _tri_consts()
    bprev, bcur = _t5_buckets()

    def gather(w, transpose, tag):
        out = []
        for l in range(L):
            wl = (w[l].T if transpose else w[l]).astype(BF16)
            full = all_gather_rows(wl, f"ag_{tag}{l}")
            out.append(full.reshape(N_DEV * wl.shape[0], wl.shape[1]))
        return out

    wgu1 = [w.reshape(2, F, D) for w in gather(w_ffn1_gu, True, "gu1_")]
    wd1 = gather(w_ffn1_down, False, "d1_")
    win = gather(w_in, True, "in_")
    wout = gather(w_out, False, "out_")
    wgu2 = [w.reshape(2, F, D) for w in gather(w_ffn2_gu, True, "gu2_")]
    wd2 = gather(w_ffn2_down, False, "d2_")

    vec = lambda a: a.reshape(1, -1)
    saved = []
    for l in range(L):
        s = {"h0": h}
        s["n1"], s["gate1"], s["up1"], s["a1"] = ffn_up_fwd(h, vec(norm_ffn1[l]), wgu1[l], f"ffn1_up{l}")
        h = ffn_down_fwd(s["a1"], wd1[l], h, f"ffn1_down{l}")
        s["h1"] = h
        s["n2"], s["p"] = mix_in_fwd(h, vec(norm_mix[l]), win[l], f"mix_in{l}")
        s["o_sb"], s["tot"] = sb_attn_fwd(s["p"], after, f"sb_fwd{l}")
        s["o_sw"], s["lse"] = swa_fwd(s["p"], vec(sinks[l]), rel_bias, bprev, bcur, f"swa_fwd{l}")
        s["mixed"], h = mix_out_fwd(s["o_sb"], s["o_sw"], vec(norm_out_sb[l]), vec(norm_out_swa[l]), wout[l], h, f"mix_out{l}")
        s["h2"] = h
        s["n3"], s["gate2"], s["up2"], s["a2"] = ffn_up_fwd(h, vec(norm_ffn2[l]), wgu2[l], f"ffn2_up{l}")
        h = ffn_down_fwd(s["a2"], wd2[l], h, f"ffn2_down{l}")
        saved.append(s)

    loss_part, dh, dhb, dg_final = loss_head(h, vec(norm_final), target, "loss_head")
    loss = lax.psum(loss_part[0, 0], ("x", "y", "c"))

    big = {k: [None] * L for k in ("gu1", "d1", "in", "out", "gu2", "d2")}
    small = {k: [None] * L for k in ("ffn1", "mix", "sinks", "osb", "osw", "ffn2", "dsc")}
    for l in reversed(range(L)):
        s = saved[l]

        def ffn_bwd(dh, dhb, wgu, wd, gate, up, a, n, h_in, g, tag):
            dgu = ffn_down_bwd(dhb, wd, gate, up, f"{tag}_down_bwd{l}")
            d_wd = tn_matmul(a[None], dhb, 0.5, f"{tag}_dwd{l}")[0]
            dh, dhb, dg = nn_rms_bwd(dgu, wgu, h_in, g, dh, F // 2, f"{tag}_up_bwd{l}")
            d_wgu = tn_matmul(dgu, n, 1.0, f"{tag}_dwgu{l}").reshape(2 * F, D)
            return dh, dhb, dg, d_wgu, d_wd

        dh, dhb, small["ffn2"][l], big["gu2"][l], big["d2"][l] = ffn_bwd(
            dh, dhb, wgu2[l], wd2[l], s["gate2"], s["up2"], s["a2"], s["n3"], s["h2"], vec(norm_ffn2[l]), "ffn2")

        do_sb, do_sw, small["osb"][l], small["osw"][l] = mix_out_bwd(
            dhb, wout[l], s["o_sb"], s["o_sw"], vec(norm_out_sb[l]), vec(norm_out_swa[l]), f"mix_out_bwd{l}")
        big["out"][l] = tn_matmul(s["mixed"][None], dhb, 1.0, f"dwout{l}")[0]
        dq_sb, dk_sb, dv_sb = sb_attn_bwd(s["p"], do_sb, s["tot"], upto, before, f"sb_bwd{l}")
        dq_sw, dk_sw, dv_sw, small["sinks"][l], small["dsc"][l] = swa_bwd(
            s["p"], do_sw, s["lse"], vec(sinks[l]), rel_bias, bprev, bcur, f"swa_bwd{l}")
        dp = jnp.concatenate([dq_sb, dk_sb, dv_sb, dq_sw, dk_sw, dv_sw], axis=1)
        dh, dhb, small["mix"][l] = nn_rms_bwd(dp[None], win[l][None], s["h1"], vec(norm_mix[l]), dh, IN_W // 2, f"mix_in_bwd{l}")
        big["in"][l] = tn_matmul(dp[None], s["n2"], 1.0, f"dwin{l}")[0]

        dh, dhb, small["ffn1"][l], big["gu1"][l], big["d1"][l] = ffn_bwd(
            dh, dhb, wgu1[l], wd1[l], s["gate1"], s["up1"], s["a1"], s["n1"], s["h0"], vec(norm_ffn1[l]), "ffn1")

    grad_x = dh.reshape(x.shape)

    def scatter(gs, transpose, tag):
        out = []
        for l in range(L):
            g3 = gs[l].reshape(N_DEV, -1, D)
            rows = g3.shape[1]
            r = rows if rows <= 352 else rows // 2
            parts = [reduce_scatter_rows(g3, off, r, f"rs_{tag}{l}_{off}") for off in range(0, rows, r)]
            mine = parts[0] if len(parts) == 1 else jnp.concatenate(parts, axis=0)
            out.append(mine.T if transpose else mine)
        return jnp.stack(out)

    g_gu1 = scatter(big["gu1"], True, "gu1_")
    g_d1 = scatter(big["d1"], False, "d1_")
    g_in = scatter(big["in"], True, "in_")
    g_out = scatter(big["out"], False, "out_")
    g_gu2 = scatter(big["gu2"], True, "gu2_")
    g_d2 = scatter(big["d2"], False, "d2_")

    upd = {}
    for nm, w, g, m, v in (("gu1", w_ffn1_gu, g_gu1, m_w_ffn1_gu, v_w_ffn1_gu), ("d1", w_ffn1_down, g_d1, m_w_ffn1_down, v_w_ffn1_down),
                           ("in", w_in, g_in, m_w_in, v_w_in), ("out", w_out, g_out, m_w_out, v_w_out),
                           ("gu2", w_ffn2_gu, g_gu2, m_w_ffn2_gu, v_w_ffn2_gu), ("d2", w_ffn2_down, g_d2, m_w_ffn2_down, v_w_ffn2_down)):
        upd[nm] = (g,) + tuple(adamw_rows(w, g, m, v, f"adamw_{nm}"))

    d_rel = rel_bias_grad(small["dsc"], bprev, bcur, "rel_bias_grad")[:, :8]

    PW = max(D, SB_W + SWA_W)

    def pack(ffn1, mix, ffn2, final, osb, osw, snk, rel):
        wide = lambda a: jnp.pad(a.reshape(-1), (0, PW - a.size))
        rows = [wide(ffn1[l]) for l in range(L)] + [wide(mix[l]) for l in range(L)] + [wide(ffn2[l]) for l in range(L)]
        rows.append(wide(final))
        rows += [wide(jnp.concatenate([osb[l].reshape(-1), osw[l].reshape(-1)])) for l in range(L)]
        rows.append(wide(jnp.concatenate([snk[l].reshape(-1)[:8] for l in range(L)] + [rel.reshape(-1)])))
        arr = jnp.stack(rows)
        return jnp.pad(arr, ((0, (-arr.shape[0]) % 8), (0, 0)))

    def unpack(arr):
        ffn1, mix, ffn2 = arr[0:L, :D], arr[L:2 * L, :D], arr[2 * L:3 * L, :D]
        final = arr[3 * L, :D]
        ob = arr[3 * L + 1:4 * L + 1]
        tail = arr[4 * L + 1]
        return (ffn1, mix, tail[:8 * L].reshape(L, 8), ob[:, :SB_W], ob[:, SB_W:SB_W + SWA_W], ffn2,
                tail[8 * L:8 * L + N_BUCKETS * 8].reshape(N_BUCKETS, 8), final)

    g_small = pack(small["ffn1"], small["mix"], small["ffn2"], dg_final, small["osb"], small["osw"], small["sinks"], d_rel)
    w_small = pack(norm_ffn1, norm_mix, norm_ffn2, norm_final, norm_out_sb, norm_out_swa, sinks, rel_bias)
    m_small = pack(m_norm_ffn1, m_norm_mix, m_norm_ffn2, m_norm_final, m_norm_out_sb, m_norm_out_swa, m_sinks, m_rel_bias)
    v_small = pack(v_norm_ffn1, v_norm_mix, v_norm_ffn2, v_norm_final, v_norm_out_sb, v_norm_out_swa, v_sinks, v_rel_bias)
    gs_small = all_gather_rows(g_small, "ag_small")
    small_out = [unpack(a) for a in adamw_small(w_small, gs_small, m_small, v_small, "adamw_small")]

    def group(k):
        sm = small_out[k]
        return (sm[0], upd["gu1"][k], upd["d1"][k], sm[1], upd["in"][k], sm[2], sm[3], sm[4], upd["out"][k], sm[5],
                upd["gu2"][k], upd["d2"][k], sm[6], sm[7])

    return (loss, grad_x, *group(0), *group(1), *group(2), *group(3))
```

```python
import math

import jax
import jax.numpy as jnp
from jax import lax
from jax.experimental import pallas as pl
from jax.experimental.pallas import tpu as pltpu

F32 = jnp.float32
BF16 = jnp.bfloat16
S = jax.ShapeDtypeStruct

N_DEV = 8
HEAD_DIM = 64
SB_HEADS = 8
PAIR = 2 * HEAD_DIM
SB_W = 512
SWA_W = 512
KV_W = 128
IN_W = 3 * SB_W + SWA_W + 2 * KV_W
QB = 128
N_BUCKETS = 32
MAX_DISTANCE = 128
EPS = 1e-6
NEG_INF = -1e30
SCALE = HEAD_DIM ** -0.5

ADAM_LR = 0.001
ADAM_B1 = 0.9
ADAM_B2 = 0.999
ADAM_EPS = 1e-08
ADAM_WD = 0.01
ADAM_STEP = 10

VMEM_LIMIT = 56 * 1024 * 1024
MESH = pl.DeviceIdType.MESH


def _params(sem=None, vmem=VMEM_LIMIT):
    return pltpu.CompilerParams(dimension_semantics=sem, vmem_limit_bytes=vmem)


def _nn(a, b):
    return jnp.dot(a, b, preferred_element_type=F32)


def _nt(a, b):
    return lax.dot_general(a, b, (((1,), (1,)), ((), ())), preferred_element_type=F32)


def _tn(a, b):
    return lax.dot_general(a, b, (((0,), (0,)), ((), ())), preferred_element_type=F32)


def _tri(x, m2):
    hi = x.astype(BF16)
    lo = (x - hi.astype(F32)).astype(BF16)
    return _nn(jnp.concatenate([hi, lo], axis=1), m2)


def _rms(x, g):
    r = lax.rsqrt(jnp.mean(x * x, axis=-1, keepdims=True) + EPS)
    return x * r * g


def _rms_bwd(dy, x, g):
    r = lax.rsqrt(jnp.mean(x * x, axis=-1, keepdims=True) + EPS)
    xhat = x * r
    u = dy * g
    dx = r * (u - xhat * jnp.mean(u * xhat, axis=-1, keepdims=True))
    return dx, jnp.sum(dy * xhat, axis=0, keepdims=True)


def _softplus_logsig(z):
    sp = jnp.maximum(z, 0.0) + jnp.log(1.0 + jnp.exp(-jnp.abs(z)))
    return sp, z - sp


def _tile(n, want):
    t = min(n, want)
    while n % t:
        t //= 2
    return t


def _place():
    x, y, c = lax.axis_index("x"), lax.axis_index("y"), lax.axis_index("c")
    chips = [(1 - x, y), (x, 1 - y), (1 - x, 1 - y)]
    return x, y, c, chips


def all_gather_rows(v, name):
    R, C = v.shape

    def body(v_ref, out_ref, send_sems, recv_sems, local_sem):
        x, y, c, chips = _place()
        me, sibling = (x, y, c), (x, y, 1 - c)

        def slot(px, py, pc):
            return out_ref.at[4 * px + 2 * py + pc]

        def copy(k, block, to, src=None):
            return pltpu.make_async_remote_copy(
                src_ref=slot(*block) if src is None else src, dst_ref=slot(*block),
                send_sem=send_sems.at[k], recv_sem=recv_sems.at[k], device_id=to, device_id_type=MESH)

        mine = pltpu.make_async_copy(v_ref, slot(*me), local_sem)
        mine.start()
        first = [copy(0, me, sibling, src=v_ref)]
        first += [copy(1 + j, me, (*chip, c), src=v_ref) for j, chip in enumerate(chips)]
        for cp in first:
            cp.start()
        passed = [copy(4 + j, (*chip, c), sibling) for j, chip in enumerate(chips)]
        for j, chip in enumerate(chips):
            copy(1 + j, (*chip, c), me).wait_recv()
            passed[j].start()
        copy(0, sibling, me).wait_recv()
        for j, chip in enumerate(chips):
            copy(4 + j, (*chip, 1 - c), me).wait_recv()
        for cp in first + passed:
            cp.wait_send()
        mine.wait()

    return pl.pallas_call(
        body, name=name, out_shape=S((N_DEV, R, C), v.dtype),
        in_specs=[pl.BlockSpec(memory_space=pl.ANY)], out_specs=pl.BlockSpec(memory_space=pl.ANY),
        scratch_shapes=[pltpu.SemaphoreType.DMA((7,)), pltpu.SemaphoreType.DMA((7,)), pltpu.SemaphoreType.DMA],
    )(v)


class _Exchange:
    def __init__(self, ins, outs, n_remote, n_local, plan, aliases=None):
        self.ins, self.outs, self.n_remote, self.n_local, self.plan = list(ins), list(outs), n_remote, n_local, plan
        self.aliases = aliases or {}

    def scratch(self):
        return [pltpu.SemaphoreType.DMA((self.n_remote,)), pltpu.SemaphoreType.DMA((self.n_remote,)),
                pltpu.SemaphoreType.DMA((max(self.n_local, 1),))]

    def _copies(self, in_refs, out_refs, sems):
        send_sems, recv_sems, local_sems = sems
        remote, local = self.plan(in_refs, out_refs)
        rem = [pltpu.make_async_remote_copy(src_ref=s, dst_ref=d, send_sem=send_sems.at[k], recv_sem=recv_sems.at[k],
                                            device_id=dev, device_id_type=MESH) for k, (s, d, dev) in enumerate(remote)]
        return rem, [pltpu.make_async_copy(s, d, local_sems.at[k]) for k, (s, d) in enumerate(local)]

    def start(self, in_refs, out_refs, sems):
        rem, loc = self._copies(in_refs, out_refs, sems)
        for cp in rem + loc:
            cp.start()

    def finish(self, in_refs, out_refs, sems):
        rem, loc = self._copies(in_refs, out_refs, sems)
        for cp in rem:
            cp.wait_recv()
        for cp in rem:
            cp.wait_send()
        for cp in loc:
            cp.wait()


def gather_first(v):
    R, C = v.shape

    def plan(ins, outs):
        x, y, c, chips = _place()
        dst = outs[0].at[4 * x + 2 * y + c]
        return [(ins[0], dst, (x, y, 1 - c))] + [(ins[0], dst, (*chip, c)) for chip in chips], [(ins[0], dst)]

    return _Exchange([v], [S((N_DEV, R, C), v.dtype)], 4, 1, plan)


def gather_second(buf):
    def plan(ins, outs):
        x, y, c, chips = _place()
        return [(outs[0].at[4 * px + 2 * py + c], outs[0].at[4 * px + 2 * py + c], (x, y, 1 - c)) for px, py in chips], []

    return _Exchange([buf], [S(buf.shape, buf.dtype)], 3, 0, plan, aliases={0: 0})


def scatter_first(gb):
    _, R, C = gb.shape

    def plan(ins, outs):
        x, y, c, chips = _place()
        owners = [(x, y)] + chips
        return [(ins[0].at[4 * px + 2 * py + (1 - c)], outs[0].at[j], (x, y, 1 - c)) for j, (px, py) in enumerate(owners)], []

    return _Exchange([gb], [S((4, R, C), BF16)], 4, 0, plan)


def scatter_second(sb):
    def plan(ins, outs):
        x, y, c, chips = _place()
        return [(ins[0].at[j], outs[0].at[j], (*chips[j], c)) for j in range(3)], []

    return _Exchange([sb], [S(sb.shape, BF16)], 3, 0, plan)


def _call(body, *, name, grid, in_specs, out_specs, out_shape, args, scratch=(), sem=None, riders=()):
    single = not isinstance(out_shape, (tuple, list))
    out_shape = (out_shape,) if single else tuple(out_shape)
    out_specs = (out_specs,) if single else tuple(out_specs)
    n_in, n_out, n_sc = len(in_specs), len(out_shape), len(scratch)
    if not riders:
        res = pl.pallas_call(body, name=name, grid=grid, in_specs=list(in_specs), out_specs=out_specs, out_shape=out_shape,
                             scratch_shapes=list(scratch), compiler_params=_params(sem))(*args)
        return res[0] if single else res
    r_ins = [a for r in riders for a in r.ins]
    r_outs = [o for r in riders for o in r.outs]
    r_scr = [s for r in riders for s in r.scratch()]
    aliases, i0, o0 = {}, n_in, n_out
    for r in riders:
        for a, b in r.aliases.items():
            aliases[i0 + a] = o0 + b
        i0, o0 = i0 + len(r.ins), o0 + len(r.outs)
    steps = math.prod(grid)

    def full(*refs):
        ins, rin = refs[:n_in], refs[n_in:n_in + len(r_ins)]
        pos = n_in + len(r_ins)
        outs, rout = refs[pos:pos + n_out], refs[pos + n_out:pos + n_out + len(r_outs)]
        pos += n_out + len(r_outs)
        sc, rsc = refs[pos:pos + n_sc], refs[pos + n_sc:]
        step = 0
        for d, n in enumerate(grid):
            step = step * n + pl.program_id(d)

        def each(method):
            i, o = 0, 0
            for k, r in enumerate(riders):
                getattr(r, method)(rin[i:i + len(r.ins)], rout[o:o + len(r.outs)], rsc[3 * k:3 * k + 3])
                i, o = i + len(r.ins), o + len(r.outs)

        @pl.when(step == 0)
        def _():
            each("start")
        body(*ins, *outs, *sc)

        @pl.when(step == steps - 1)
        def _():
            each("finish")

    anywhere = pl.BlockSpec(memory_space=pl.ANY)
    res = pl.pallas_call(
        full, name=name, grid=grid, in_specs=list(in_specs) + [anywhere] * len(r_ins),
        out_specs=out_specs + (anywhere,) * len(r_outs), out_shape=out_shape + tuple(r_outs),
        scratch_shapes=list(scratch) + r_scr, input_output_aliases=aliases,
        compiler_params=_params(("arbitrary",) * len(grid)))(*args, *r_ins)
    host, rest, per = res[:n_out], list(res[n_out:]), []
    for r in riders:
        per.append(rest[:len(r.outs)])
        rest = rest[len(r.outs):]
    return (host[0] if single else tuple(host)), per


def idle_host(riders, name):
    def body(o_ref):
        o_ref[...] = jnp.zeros_like(o_ref)

    return _call(body, name=name, grid=(1,), in_specs=[], out_specs=pl.BlockSpec((8, QB), lambda i: (0, 0)),
                 out_shape=S((8, QB), F32), args=(), riders=riders)[1]


def _rows_tile(n, cap):
    return max(t for t in range(16, min(n, cap) + 1, 16) if n % t == 0)


def scatter_add(g, ra, name):
    _, R, C = g.shape
    tr = _rows_tile(R, 176)
    x, y, c, chips = _place()
    slots = jnp.stack([4 * px + 2 * py + c for px, py in [(x, y)] + chips]).astype(jnp.int32)

    def body(s_ref, g0, g1, g2, g3, ra_ref, own_ref, sb_ref):
        own_ref[...] = g0[...] + ra_ref[0].astype(F32)
        for j, gj in enumerate((g1, g2, g3)):
            sb_ref[j] = (gj[...] + ra_ref[j + 1].astype(F32)).astype(BF16)

    spec = pltpu.PrefetchScalarGridSpec(
        num_scalar_prefetch=1, grid=(R // tr,),
        in_specs=[pl.BlockSpec((None, tr, C), lambda i, s, j=j: (s[j], i, 0)) for j in range(4)]
        + [pl.BlockSpec((4, tr, C), lambda i, s: (0, i, 0))],
        out_specs=(pl.BlockSpec((tr, C), lambda i, s: (i, 0)), pl.BlockSpec((3, tr, C), lambda i, s: (0, i, 0))))
    return pl.pallas_call(body, name=name, grid_spec=spec, out_shape=(S((R, C), F32), S((3, R, C), BF16)),
                          compiler_params=_params(("parallel",)))(slots, g, g, g, g, ra)


def scatter_sum(own, rb, name):
    R, C = own.shape
    tr = _rows_tile(R, 176)

    def body(o_ref, r_ref, g_ref):
        g_ref[...] = o_ref[...] + r_ref[0].astype(F32) + r_ref[1].astype(F32) + r_ref[2].astype(F32)

    return _call(body, name=name, grid=(R // tr,), in_specs=[pl.BlockSpec((tr, C), lambda i: (i, 0)),
                                                              pl.BlockSpec((3, tr, C), lambda i: (0, i, 0))],
                 out_specs=pl.BlockSpec((tr, C), lambda i: (i, 0)), out_shape=S((R, C), F32), args=(own, rb), sem=("parallel",))


def ffn_up_fwd(h, g, wgu, name, riders=()):
    T, D = h.shape
    F = wgu.shape[1]
    tm, tn = _tile(T, 512), _tile(F, 256)

    def body(h_ref, g_ref, wg_ref, wu_ref, n_ref, gate_ref, up_ref, a_ref):
        @pl.when(pl.program_id(1) == 0)
        def _():
            n_ref[...] = _rms(h_ref[...], g_ref[...]).astype(BF16)
        n = n_ref[...]
        gate = _nt(n, wg_ref[...])
        up = _nt(n, wu_ref[...])
        gate_ref[...] = gate
        up_ref[...] = up
        a_ref[...] = (gate * jax.nn.sigmoid(gate) * up).astype(BF16)

    tile = pl.BlockSpec((tm, tn), lambda i, j: (i, j))
    return _call(
        body, name=name, grid=(T // tm, F // tn),
        out_shape=(S((T, D), BF16), S((T, F), F32), S((T, F), F32), S((T, F), BF16)),
        in_specs=[pl.BlockSpec((tm, D), lambda i, j: (i, 0)), pl.BlockSpec((1, D), lambda i, j: (0, 0)),
                  pl.BlockSpec((None, tn, D), lambda i, j: (0, j, 0)), pl.BlockSpec((None, tn, D), lambda i, j: (1, j, 0))],
        out_specs=(pl.BlockSpec((tm, D), lambda i, j: (i, 0)), tile, tile, tile),
        sem=("parallel", "arbitrary"), args=(h, g, wgu, wgu), riders=riders)


def ffn_down_fwd(a, wd, h, name, riders=()):
    T, F = a.shape
    D = wd.shape[1]
    tm = _tile(T, 256)

    def body(a_ref, w_ref, h_ref, o_ref):
        o_ref[...] = h_ref[...] + 0.5 * _nn(a_ref[...], w_ref[...])

    return _call(
        body, name=name, grid=(T // tm,), out_shape=S((T, D), F32),
        in_specs=[pl.BlockSpec((tm, F), lambda i: (i, 0)), pl.BlockSpec((F, D), lambda i: (0, 0)),
                  pl.BlockSpec((tm, D), lambda i: (i, 0))],
        out_specs=pl.BlockSpec((tm, D), lambda i: (i, 0)),
        sem=("parallel",), args=(a, wd, h), riders=riders)


def mix_in_fwd(h, g, win, name):
    T, D = h.shape
    N = win.shape[0]
    tm = _tile(T, 256)

    def body(h_ref, g_ref, w_ref, n_ref, p_ref):
        n = _rms(h_ref[...], g_ref[...]).astype(BF16)
        n_ref[...] = n
        p_ref[...] = _nt(n, w_ref[...]).astype(BF16)

    return pl.pallas_call(
        body, name=name, grid=(T // tm,), out_shape=(S((T, D), BF16), S((T, N), BF16)),
        in_specs=[pl.BlockSpec((tm, D), lambda i: (i, 0)), pl.BlockSpec((1, D), lambda i: (0, 0)),
                  pl.BlockSpec((N, D), lambda i: (0, 0))],
        out_specs=(pl.BlockSpec((tm, D), lambda i: (i, 0)), pl.BlockSpec((tm, N), lambda i: (i, 0))),
        compiler_params=_params(("parallel",)),
    )(h, g, win)


def _tri_consts():
    r = lax.broadcasted_iota(jnp.int32, (QB, QB), 0)
    c = lax.broadcasted_iota(jnp.int32, (QB, QB), 1)
    ones = jnp.ones((QB, QB), BF16)

    def stacked(tri):
        m = jnp.concatenate([tri.astype(BF16), ones], axis=1)
        return jnp.concatenate([m, m], axis=0)

    return stacked(r > c), stacked(r <= c), stacked(r < c)


def _half_masks():
    lane = lax.broadcasted_iota(jnp.int32, (QB, PAIR), 1)
    row = lax.broadcasted_iota(jnp.int32, (QB, PAIR), 0)
    return lane < HEAD_DIM, lane, row


def sb_attn_fwd(p, after, name, riders=()):
    T = p.shape[0]
    nq = T // QB

    def body(q_ref, k_ref, v_ref, m_ref, o_ref, tot_ref, q_sc, acc_ref):
        i = pl.program_id(0)
        lo, lane, row = _half_masks()
        causal = lane < row
        for hp in range(SB_HEADS // 2):
            q2 = q_ref[:, hp * PAIR:(hp + 1) * PAIR].astype(F32) * SCALE
            q_sc[2 * hp] = jnp.where(lo, q2, 0.0).astype(BF16)
            q_sc[2 * hp + 1] = jnp.where(lo, 0.0, q2).astype(BF16)
        m2 = m_ref[...]

        def block(j, diag):
            r0 = pl.multiple_of(j * QB, QB)
            heads = range(SB_HEADS)
            k2 = [k_ref[pl.ds(r0, QB), hp * PAIR:(hp + 1) * PAIR] for hp in range(SB_HEADS // 2)]
            v2 = [v_ref[pl.ds(r0, QB), hp * PAIR:(hp + 1) * PAIR] for hp in range(SB_HEADS // 2)]
            z = [_nt(q_sc[h], k2[h // 2]) for h in heads]
            spls = [_softplus_logsig(z[h]) for h in heads]
            sp = [jnp.where(causal, spls[h][0], 0.0) if diag else spls[h][0] for h in heads]
            rr = [_tri(sp[h], m2) for h in heads]
            if diag:
                w = [jnp.where(causal, jnp.exp(spls[h][1] - rr[h][:, :QB]), 0.0) for h in heads]
                pv = [_nn(w[h].astype(BF16), v2[h // 2]) for h in heads]
                for h in heads:
                    acc_ref[h] = pv[h]
                    tot_ref[:, h * QB:(h + 1) * QB] = rr[h][:, QB:]
            else:
                c = [tot_ref[:, h * QB:(h + 1) * QB] for h in heads]
                w = [jnp.exp(spls[h][1] - (c[h] + rr[h][:, :QB])) for h in heads]
                pv = [_nn(w[h].astype(BF16), v2[h // 2]) for h in heads]
                for h in heads:
                    acc_ref[h] += pv[h]
                    tot_ref[:, h * QB:(h + 1) * QB] = c[h] + rr[h][:, QB:]

        block(i, True)

        def step(t, carry):
            block(i - 1 - t, False)
            return carry
        lax.fori_loop(0, i, step, 0)
        for hp in range(SB_HEADS // 2):
            o_ref[:, hp * PAIR:(hp + 1) * PAIR] = jnp.where(lo, acc_ref[2 * hp], acc_ref[2 * hp + 1])

    return _call(
        body, name=name, grid=(nq,), out_shape=(S((T, SB_W), F32), S((T, SB_HEADS * QB), F32)),
        in_specs=[pl.BlockSpec((QB, SB_W), lambda i: (i, 0)), pl.BlockSpec((T, SB_W), lambda i: (0, 1)),
                  pl.BlockSpec((T, SB_W), lambda i: (0, 2)), pl.BlockSpec((2 * QB, 2 * QB), lambda i: (0, 0))],
        out_specs=(pl.BlockSpec((QB, SB_W), lambda i: (i, 0)), pl.BlockSpec((QB, SB_HEADS * QB), lambda i: (i, 0))),
        scratch=[pltpu.VMEM((SB_HEADS, QB, PAIR), BF16), pltpu.VMEM((SB_HEADS, QB, PAIR), F32)],
        sem=("arbitrary",), args=(p, p, p, after), riders=riders)


def sb_attn_bwd(p, do, tot, upto, before, name, riders=()):
    T = p.shape[0]
    nq = T // QB

    def body(q_ref, k_ref, v_ref, do_ref, tot_ref, mp_ref, mg_ref, dq_ref, dk_ref, dv_ref,
             q_sc, d_sc, pg_sc, dq_acc, dk_acc, dv_acc):
        i = pl.program_id(0)
        lo, lane, row = _half_masks()
        causal = lane < row
        for hp in range(SB_HEADS // 2):
            q2 = q_ref[:, hp * PAIR:(hp + 1) * PAIR].astype(F32) * SCALE
            d2 = do_ref[:, hp * PAIR:(hp + 1) * PAIR]
            q_sc[2 * hp] = jnp.where(lo, q2, 0.0).astype(BF16)
            q_sc[2 * hp + 1] = jnp.where(lo, 0.0, q2).astype(BF16)
            d_sc[2 * hp] = jnp.where(lo, d2, 0.0).astype(BF16)
            d_sc[2 * hp + 1] = jnp.where(lo, 0.0, d2).astype(BF16)
        mp, mg = mp_ref[...], mg_ref[...]

        @pl.when(i == 0)
        def _():
            dk_acc[...] = jnp.zeros_like(dk_acc)
            dv_acc[...] = jnp.zeros_like(dv_acc)
        pg_sc[...] = jnp.zeros_like(pg_sc)
        dq_acc[...] = jnp.zeros_like(dq_acc)

        def block(j, diag):
            r0 = pl.multiple_of(j * QB, QB)
            heads = range(SB_HEADS)
            pairs = range(SB_HEADS // 2)
            k2 = [k_ref[pl.ds(r0, QB), hp * PAIR:(hp + 1) * PAIR] for hp in pairs]
            v2 = [v_ref[pl.ds(r0, QB), hp * PAIR:(hp + 1) * PAIR] for hp in pairs]
            z = [_nt(q_sc[h], k2[h // 2]) for h in heads]
            dw = [_nt(d_sc[h], v2[h // 2]) for h in heads]
            spls = [_softplus_logsig(z[h]) for h in heads]
            sp = [jnp.where(causal, spls[h][0], 0.0) if diag else spls[h][0] for h in heads]
            rr = [_tri(sp[h], mp) for h in heads]
            pc = [pg_sc[2 * h] for h in heads]
            w = [jnp.exp(spls[h][1] - (tot_ref[:, h * QB:(h + 1) * QB] - (pc[h] + rr[h][:, :QB]))) for h in heads]
            if diag:
                w = [jnp.where(causal, w[h], 0.0) for h in heads]
            gg = [dw[h] * w[h] for h in heads]
            rg = [_tri(gg[h], mg) for h in heads]
            gc = [pg_sc[2 * h + 1] for h in heads]
            dz = [gg[h] - (gg[h] + gc[h] + rg[h][:, :QB]) * jnp.exp(spls[h][1]) for h in heads]
            if diag:
                dz = [jnp.where(causal, dz[h], 0.0) for h in heads]
            dzb = [dz[h].astype(BF16) for h in heads]
            wb = [w[h].astype(BF16) for h in heads]
            dq = [_nn(dzb[h], k2[h // 2]) for h in heads]
            dk = [_tn(dzb[2 * hp], q_sc[2 * hp]) + _tn(dzb[2 * hp + 1], q_sc[2 * hp + 1]) for hp in pairs]
            dv = [_tn(wb[2 * hp], d_sc[2 * hp]) + _tn(wb[2 * hp + 1], d_sc[2 * hp + 1]) for hp in pairs]
            for h in heads:
                dq_acc[h] += dq[h]
                if not diag:
                    pg_sc[2 * h] = pc[h] + rr[h][:, QB:]
                    pg_sc[2 * h + 1] = gc[h] + rg[h][:, QB:]
            for hp in pairs:
                dk_acc[pl.ds(r0, QB), hp * PAIR:(hp + 1) * PAIR] += dk[hp]
                dv_acc[pl.ds(r0, QB), hp * PAIR:(hp + 1) * PAIR] += dv[hp]

        def step(t, carry):
            block(t, False)
            return carry
        lax.fori_loop(0, i, step, 0)
        block(i, True)
        for hp in range(SB_HEADS // 2):
            dq = jnp.where(lo, dq_acc[2 * hp], dq_acc[2 * hp + 1]) * SCALE
            dq_ref[:, hp * PAIR:(hp + 1) * PAIR] = dq.astype(BF16)

        @pl.when(i == nq - 1)
        def _():
            dk_ref[...] = dk_acc[...].astype(BF16)
            dv_ref[...] = dv_acc[...].astype(BF16)

    qtile = pl.BlockSpec((QB, SB_W), lambda i: (i, 0))
    whole = pl.BlockSpec((T, SB_W), lambda i: (0, 0))
    const = pl.BlockSpec((2 * QB, 2 * QB), lambda i: (0, 0))
    return _call(
        body, name=name, grid=(nq,), out_shape=(S((T, SB_W), BF16),) * 3,
        in_specs=[qtile, pl.BlockSpec((T, SB_W), lambda i: (0, 1)), pl.BlockSpec((T, SB_W), lambda i: (0, 2)), qtile,
                  pl.BlockSpec((QB, SB_HEADS * QB), lambda i: (i, 0)), const, const],
        out_specs=(qtile, whole, whole),
        scratch=[pltpu.VMEM((SB_HEADS, QB, PAIR), BF16), pltpu.VMEM((SB_HEADS, QB, PAIR), BF16),
                 pltpu.VMEM((2 * SB_HEADS, QB, QB), F32), pltpu.VMEM((SB_HEADS, QB, PAIR), F32),
                 pltpu.VMEM((T, SB_W), F32), pltpu.VMEM((T, SB_W), F32)],
        sem=("arbitrary",), args=(p, p, p, do, tot, upto, before), riders=riders)


def _t5_buckets():
    a = lax.broadcasted_iota(jnp.int32, (QB, QB), 0)
    c = lax.broadcasted_iota(jnp.int32, (QB, QB), 1)

    def bucket(dist):
        dist = jnp.maximum(dist, 0)
        max_exact = N_BUCKETS // 2
        d = jnp.maximum(dist, 1).astype(F32)
        large = max_exact + (jnp.log(d / max_exact) / math.log(MAX_DISTANCE / max_exact)
                             * (N_BUCKETS - max_exact)).astype(jnp.int32)
        large = jnp.minimum(large, N_BUCKETS - 1)
        return jnp.where(dist < max_exact, dist, large)

    return bucket(QB + a - c), bucket(a - c)


def _swa_common(i, kp_ref, kc_ref, vp_ref, vc_ref, bp_ref, bc_ref, rb_ref, bias_ref):
    lo, lane, row = _half_masks()

    @pl.when(i == 0)
    def _():
        for blk, b_ref in enumerate((bp_ref, bc_ref)):
            bk = b_ref[...]
            for h in range(8):
                acc = jnp.zeros((QB, QB), F32)
                for b in range(N_BUCKETS):
                    acc = jnp.where(bk == b, rb_ref[b, h], acc)
                bias_ref[h, blk] = acc

    band = [(lane > row) & (i > 0), lane <= row]

    def halves(ref):
        t = ref[...].astype(F32)
        sw = pltpu.roll(t, HEAD_DIM, 1)
        return [[jnp.where(lo, t, 0.0).astype(BF16), jnp.where(lo, 0.0, sw).astype(BF16)],
                [jnp.where(lo, sw, 0.0).astype(BF16), jnp.where(lo, 0.0, t).astype(BF16)]]

    ks = [halves(kp_ref), halves(kc_ref)]
    vs = [halves(vp_ref), halves(vc_ref)]
    return lo, band, ks, vs


def swa_fwd(p, sinks, rel_bias, bprev, bcur, name, riders=()):
    T = p.shape[0]
    nq = T // QB
    kcol, vcol = (3 * SB_W + SWA_W) // KV_W, (3 * SB_W + SWA_W) // KV_W + 1

    def body(q_ref, kp_ref, kc_ref, vp_ref, vc_ref, bp_ref, bc_ref, sink_ref, rb_ref, o_ref, lse_ref, bias_ref):
        i = pl.program_id(0)
        lo, band, ks, vs = _swa_common(i, kp_ref, kc_ref, vp_ref, vc_ref, bp_ref, bc_ref, rb_ref, bias_ref)
        for g in range(4):
            kh = g // 2
            q2 = q_ref[:, g * PAIR:(g + 1) * PAIR]
            outs = []
            for pos in range(2):
                h = 2 * g + pos
                sc = [jnp.where(band[b], _nt(q2, ks[b][kh][pos]) * SCALE + bias_ref[h, b], NEG_INF) for b in range(2)]
                sink = sink_ref[0, h]
                m = jnp.maximum(jnp.maximum(jnp.max(sc[0], axis=1, keepdims=True),
                                            jnp.max(sc[1], axis=1, keepdims=True)), sink)
                e = [jnp.exp(sc[b] - m) for b in range(2)]
                den = jnp.sum(e[0], axis=1, keepdims=True) + jnp.sum(e[1], axis=1, keepdims=True) + jnp.exp(sink - m)
                outs.append(_nn((e[0] / den).astype(BF16), vs[0][kh][pos]) + _nn((e[1] / den).astype(BF16), vs[1][kh][pos]))
                lse_ref[:, h * QB:(h + 1) * QB] = jnp.broadcast_to(m + jnp.log(den), (QB, QB))
            o_ref[:, g * PAIR:(g + 1) * PAIR] = outs[0] + outs[1]

    kv = lambda col, prev: pl.BlockSpec((QB, KV_W), (lambda i: (jnp.maximum(i - 1, 0), col)) if prev else (lambda i: (i, col)))
    full = pl.BlockSpec((QB, QB), lambda i: (0, 0))
    smem = pl.BlockSpec(memory_space=pltpu.SMEM)
    return _call(
        body, name=name, grid=(nq,), out_shape=(S((T, SWA_W), F32), S((T, 8 * QB), F32)),
        in_specs=[pl.BlockSpec((QB, SWA_W), lambda i: (i, 3)), kv(kcol, True), kv(kcol, False), kv(vcol, True), kv(vcol, False),
                  full, full, smem, smem],
        out_specs=(pl.BlockSpec((QB, SWA_W), lambda i: (i, 0)), pl.BlockSpec((QB, 8 * QB), lambda i: (i, 0))),
        scratch=[pltpu.VMEM((8, 2, QB, QB), F32)],
        sem=("arbitrary",), args=(p, p, p, p, p, bprev, bcur, sinks, rel_bias), riders=riders)


def swa_bwd(p, do, lse, sinks, rel_bias, bprev, bcur, name, riders=()):
    T = p.shape[0]
    nq = T // QB
    kcol, vcol = (3 * SB_W + SWA_W) // KV_W, (3 * SB_W + SWA_W) // KV_W + 1

    def body(q_ref, kp_ref, kc_ref, vp_ref, vc_ref, do_ref, lse_ref, bp_ref, bc_ref, sink_ref, rb_ref,
             dq_ref, dk_ref, dv_ref, dsink_ref, dsc_ref, bias_ref, dk_acc, dv_acc):
        i = pl.program_id(0)
        lo, band, ks, vs = _swa_common(i, kp_ref, kc_ref, vp_ref, vc_ref, bp_ref, bc_ref, rb_ref, bias_ref)

        @pl.when(i == 0)
        def _():
            dk_acc[...] = jnp.zeros_like(dk_acc)
            dv_acc[...] = jnp.zeros_like(dv_acc)
            dsc_ref[...] = jnp.zeros_like(dsc_ref)
            dsink_ref[...] = jnp.zeros_like(dsink_ref)

        lane1 = lax.broadcasted_iota(jnp.int32, (1, QB), 1)
        dsink = jnp.zeros((1, QB), F32)
        dk_parts = [[[None, None], [None, None]], [[None, None], [None, None]]]
        dv_parts = [[[None, None], [None, None]], [[None, None], [None, None]]]

        def add(parts, b, pos, kh, val):
            parts[b][pos][kh] = val if parts[b][pos][kh] is None else parts[b][pos][kh] + val

        for g in range(4):
            kh = g // 2
            q2 = q_ref[:, g * PAIR:(g + 1) * PAIR]
            q2f = q2.astype(F32)
            d2f = do_ref[:, g * PAIR:(g + 1) * PAIR]
            d2 = d2f.astype(BF16)
            dq = None
            for pos in range(2):
                h = 2 * g + pos
                keep = lo if pos == 0 else ~lo
                qh = jnp.where(keep, q2f, 0.0).astype(BF16)
                dh = jnp.where(keep, d2f, 0.0).astype(BF16)
                lse_h = lse_ref[:, h * QB:(h + 1) * QB]
                sink = sink_ref[0, h]
                pr = [jnp.exp(jnp.where(band[b], _nt(q2, ks[b][kh][pos]) * SCALE + bias_ref[h, b], NEG_INF) - lse_h)
                      for b in range(2)]
                dp = [_nt(d2, vs[b][kh][pos]) for b in range(2)]
                delta = jnp.sum(pr[0] * dp[0], axis=1, keepdims=True) + jnp.sum(pr[1] * dp[1], axis=1, keepdims=True)
                p_sink = jnp.exp(sink - lse_h[:, :1])
                dsink = dsink + jnp.where(lane1 == h, -jnp.sum(p_sink * delta), 0.0)
                for b in range(2):
                    dsc = pr[b] * (dp[b] - delta)
                    dsc_ref[h, b] += dsc
                    dzb = (dsc * SCALE).astype(BF16)
                    t = _nn(dzb, ks[b][kh][pos])
                    dq = t if dq is None else dq + t
                    add(dk_parts, b, pos, kh, _tn(dzb, qh))
                    add(dv_parts, b, pos, kh, _tn(pr[b].astype(BF16), dh))
            dq_ref[:, g * PAIR:(g + 1) * PAIR] = dq.astype(BF16)
        dsink_ref[...] += dsink

        def fold(parts, b):
            low = parts[b][0][0] + pltpu.roll(parts[b][1][0], HEAD_DIM, 1)
            high = parts[b][1][1] + pltpu.roll(parts[b][0][1], HEAD_DIM, 1)
            return jnp.where(lo, low, high)

        rp = pl.multiple_of(jnp.maximum(i - 1, 0) * QB, QB)
        rc = pl.multiple_of(i * QB, QB)
        dk_acc[pl.ds(rp, QB), :] += fold(dk_parts, 0)
        dv_acc[pl.ds(rp, QB), :] += fold(dv_parts, 0)
        dk_acc[pl.ds(rc, QB), :] += fold(dk_parts, 1)
        dv_acc[pl.ds(rc, QB), :] += fold(dv_parts, 1)

        @pl.when(i == nq - 1)
        def _():
            dk_ref[...] = dk_acc[...].astype(BF16)
            dv_ref[...] = dv_acc[...].astype(BF16)

    kv = lambda col, prev: pl.BlockSpec((QB, KV_W), (lambda i: (jnp.maximum(i - 1, 0), col)) if prev else (lambda i: (i, col)))
    full = pl.BlockSpec((QB, QB), lambda i: (0, 0))
    smem = pl.BlockSpec(memory_space=pltpu.SMEM)
    whole = lambda shape: pl.BlockSpec(shape, lambda i: (0,) * len(shape))
    return _call(
        body, name=name, grid=(nq,),
        out_shape=(S((T, SWA_W), BF16), S((T, KV_W), BF16), S((T, KV_W), BF16), S((1, QB), F32), S((8, 2, QB, QB), F32)),
        in_specs=[pl.BlockSpec((QB, SWA_W), lambda i: (i, 3)), kv(kcol, True), kv(kcol, False), kv(vcol, True), kv(vcol, False),
                  pl.BlockSpec((QB, SWA_W), lambda i: (i, 0)), pl.BlockSpec((QB, 8 * QB), lambda i: (i, 0)),
                  full, full, smem, smem],
        out_specs=(pl.BlockSpec((QB, SWA_W), lambda i: (i, 0)), whole((T, KV_W)), whole((T, KV_W)), whole((1, QB)),
                   whole((8, 2, QB, QB))),
        scratch=[pltpu.VMEM((8, 2, QB, QB), F32), pltpu.VMEM((T, KV_W), F32), pltpu.VMEM((T, KV_W), F32)],
        sem=("arbitrary",), args=(p, p, p, p, p, do, lse, bprev, bcur, sinks, rel_bias), riders=riders)


def mix_out_fwd(o_sb, o_sw, g_sb, g_sw, wout, h, name, riders=()):
    T, D = h.shape
    M = SB_W + SWA_W
    tm = _tile(T, 256)

    def body(a_ref, b_ref, ga_ref, gb_ref, w_ref, h_ref, mx_ref, o_ref):
        mx_ref[:, :SB_W] = _rms(a_ref[...], ga_ref[...]).astype(BF16)
        mx_ref[:, SB_W:] = _rms(b_ref[...], gb_ref[...]).astype(BF16)
        o_ref[...] = h_ref[...] + _nn(mx_ref[...], w_ref[...])

    row = lambda n: pl.BlockSpec((tm, n), lambda i: (i, 0))
    vec = lambda n: pl.BlockSpec((1, n), lambda i: (0, 0))
    return _call(
        body, name=name, grid=(T // tm,), out_shape=(S((T, M), BF16), S((T, D), F32)),
        in_specs=[row(SB_W), row(SWA_W), vec(SB_W), vec(SWA_W), pl.BlockSpec((M, D), lambda i: (0, 0)), row(D)],
        out_specs=(row(M), row(D)),
        sem=("parallel",), args=(o_sb, o_sw, g_sb, g_sw, wout, h), riders=riders)


def loss_head(h, g, target, name):
    T, D = h.shape
    tm = _tile(T, 256)

    def body(h_ref, g_ref, t_ref, loss_ref, dh_ref, dhb_ref, dg_ref):
        @pl.when(pl.program_id(0) == 0)
        def _():
            loss_ref[...] = jnp.zeros_like(loss_ref)
            dg_ref[...] = jnp.zeros_like(dg_ref)
        x = h_ref[...]
        err = _rms(x, g_ref[...]) - t_ref[...]
        loss_ref[...] += jnp.full((1, QB), 0.5 * jnp.sum(jnp.mean(err * err, axis=-1)), F32)
        dx, dg = _rms_bwd(err / D, x, g_ref[...])
        dh_ref[...] = dx
        dhb_ref[...] = dx.astype(BF16)
        dg_ref[...] += dg

    row = pl.BlockSpec((tm, D), lambda i: (i, 0))
    vec = pl.BlockSpec((1, D), lambda i: (0, 0))
    return pl.pallas_call(
        body, name=name, grid=(T // tm,), out_shape=(S((1, QB), F32), S((T, D), F32), S((T, D), BF16), S((1, D), F32)),
        in_specs=[row, vec, row], out_specs=(pl.BlockSpec((1, QB), lambda i: (0, 0)), row, row, vec),
        compiler_params=_params(("arbitrary",)),
    )(h, g, target)


def ffn_down_bwd(dhb, wd, gate, up, name, riders=()):
    T, D = dhb.shape
    F = wd.shape[0]
    tm, tn = _tile(T, 512), _tile(F, 256)

    def body(d_ref, w_ref, g_ref, u_ref, o_ref):
        da = 0.5 * _nt(d_ref[...], w_ref[...])
        gate = g_ref[...]
        s = jax.nn.sigmoid(gate)
        o_ref[0] = (da * u_ref[...] * (s * (1.0 + gate * (1.0 - s)))).astype(BF16)
        o_ref[1] = (da * gate * s).astype(BF16)

    tile = pl.BlockSpec((tm, tn), lambda i, j: (i, j))
    return _call(
        body, name=name, grid=(T // tm, F // tn), out_shape=S((2, T, F), BF16),
        in_specs=[pl.BlockSpec((tm, D), lambda i, j: (i, 0)), pl.BlockSpec((tn, D), lambda i, j: (j, 0)), tile, tile],
        out_specs=pl.BlockSpec((2, tm, tn), lambda i, j: (0, i, j)),
        sem=("parallel", "parallel"), args=(dhb, wd, gate, up), riders=riders)


def tn_matmul(xs, y, alpha, name, riders=()):
    B, T, N = xs.shape
    D = y.shape[1]
    tn = _tile(N, 256)

    def body(x_ref, y_ref, o_ref, ob_ref):
        o = alpha * _tn(x_ref[...], y_ref[...])
        o_ref[...] = o
        ob_ref[...] = o.astype(BF16)

    tile = pl.BlockSpec((None, tn, D), lambda s, j: (s, j, 0))
    return _call(
        body, name=name, grid=(B, N // tn), out_shape=(S((B, N, D), F32), S((B, N, D), BF16)),
        in_specs=[pl.BlockSpec((None, T, tn), lambda s, j: (s, 0, j)), pl.BlockSpec((T, D), lambda s, j: (0, 0))],
        out_specs=(tile, tile), sem=("parallel", "parallel"), args=(xs, y), riders=riders)


def nn_rms_bwd(xs, ws, h_in, g, dh, tk, name, riders=()):
    B, T, K = xs.shape
    D = ws.shape[2]
    tm = _tile(T, 512)
    nk = K // tk
    steps = B * nk

    def body(x_ref, w_ref, h_ref, g_ref, d_ref, o_ref, ob_ref, dg_ref, acc_ref):
        i, k = pl.program_id(0), pl.program_id(1)

        @pl.when((i == 0) & (k == 0))
        def _():
            dg_ref[...] = jnp.zeros_like(dg_ref)

        @pl.when(k == 0)
        def _():
            acc_ref[...] = jnp.zeros_like(acc_ref)
        acc_ref[...] += _nn(x_ref[...], w_ref[...])

        @pl.when(k == steps - 1)
        def _():
            dx, dg = _rms_bwd(acc_ref[...], h_ref[...], g_ref[...])
            out = d_ref[...] + dx
            o_ref[...] = out
            ob_ref[...] = out.astype(BF16)
            dg_ref[...] += dg

    row = pl.BlockSpec((tm, D), lambda i, k: (i, 0))
    vec = pl.BlockSpec((1, D), lambda i, k: (0, 0))
    return _call(
        body, name=name, grid=(T // tm, steps), out_shape=(S((T, D), F32), S((T, D), BF16), S((1, D), F32)),
        in_specs=[pl.BlockSpec((None, tm, tk), lambda i, k: (k // nk, i, k % nk)),
                  pl.BlockSpec((None, tk, D), lambda i, k: (k // nk, k % nk, 0)), row, vec, row],
        out_specs=(row, row, vec),
        scratch=[pltpu.VMEM((tm, D), F32)],
        sem=("arbitrary", "arbitrary"), args=(xs, ws, h_in, g, dh), riders=riders)


def mix_out_bwd(dhb, wout, o_sb, o_sw, g_sb, g_sw, name):
    T, D = dhb.shape
    tm = _tile(T, 256)

    def body(d_ref, w_ref, a_ref, b_ref, ga_ref, gb_ref, da_ref, db_ref, dga_ref, dgb_ref):
        @pl.when(pl.program_id(0) == 0)
        def _():
            dga_ref[...] = jnp.zeros_like(dga_ref)
            dgb_ref[...] = jnp.zeros_like(dgb_ref)
        dm = _nt(d_ref[...], w_ref[...])
        dxa, dga = _rms_bwd(dm[:, :SB_W], a_ref[...], ga_ref[...])
        dxb, dgb = _rms_bwd(dm[:, SB_W:], b_ref[...], gb_ref[...])
        da_ref[...] = dxa
        db_ref[...] = dxb
        dga_ref[...] += dga
        dgb_ref[...] += dgb

    row = lambda n: pl.BlockSpec((tm, n), lambda i: (i, 0))
    vec = lambda n: pl.BlockSpec((1, n), lambda i: (0, 0))
    return pl.pallas_call(
        body, name=name, grid=(T // tm,),
        out_shape=(S((T, SB_W), F32), S((T, SWA_W), F32), S((1, SB_W), F32), S((1, SWA_W), F32)),
        in_specs=[row(D), pl.BlockSpec((SB_W + SWA_W, D), lambda i: (0, 0)), row(SB_W), row(SWA_W), vec(SB_W), vec(SWA_W)],
        out_specs=(row(SB_W), row(SWA_W), vec(SB_W), vec(SWA_W)),
        compiler_params=_params(("arbitrary",)),
    )(dhb, wout, o_sb, o_sw, g_sb, g_sw)


def rel_bias_grad(dscs, bprev, bcur, name):
    n = len(dscs)

    def body(*refs):
        bp_ref, bc_ref, o_ref = refs[n], refs[n + 1], refs[n + 2]
        bks = [bp_ref[...], bc_ref[...]]
        row = lax.broadcasted_iota(jnp.int32, (N_BUCKETS, QB), 0)
        lane = lax.broadcasted_iota(jnp.int32, (N_BUCKETS, QB), 1)
        out = jnp.zeros((N_BUCKETS, QB), F32)
        for h in range(8):
            tot = [sum(refs[l][h, b] for l in range(n)) for b in range(2)]
            for b in range(N_BUCKETS):
                val = jnp.sum(jnp.where(bks[0] == b, tot[0], 0.0)) + jnp.sum(jnp.where(bks[1] == b, tot[1], 0.0))
                out = jnp.where((row == b) & (lane == h), val, out)
        o_ref[...] = out

    return pl.pallas_call(body, name=name, out_shape=S((N_BUCKETS, QB), F32), compiler_params=_params())(*dscs, bprev, bcur)


def _adamw(w, g, m, v):
    m = ADAM_B1 * m + (1.0 - ADAM_B1) * g
    v = ADAM_B2 * v + (1.0 - ADAM_B2) * (g * g)
    m_hat = m / (1.0 - ADAM_B1 ** ADAM_STEP)
    v_hat = v / (1.0 - ADAM_B2 ** ADAM_STEP)
    delta = -ADAM_LR * (m_hat / (jnp.sqrt(v_hat) + ADAM_EPS) + ADAM_WD * w)
    return delta, m, v


def adamw_rows(w, g, m, v, name):
    L, R, C = w.shape
    tr = _tile(R, 256)

    def body(w_ref, g_ref, m_ref, v_ref, d_ref, mo_ref, vo_ref):
        d, mn, vn = _adamw(w_ref[...], g_ref[...], m_ref[...], v_ref[...])
        d_ref[...] = d
        mo_ref[...] = mn
        vo_ref[...] = vn

    tile = pl.BlockSpec((None, tr, C), lambda l, i: (l, i, 0))
    return pl.pallas_call(
        body, name=name, grid=(L, R // tr), out_shape=(S((L, R, C), F32),) * 3,
        in_specs=[tile] * 4, out_specs=(tile,) * 3,
        compiler_params=_params(("parallel", "parallel")),
    )(w, g, m, v)


def adamw_small(w, gs, m, v, name):
    R, C = w.shape

    def body(w_ref, g_ref, m_ref, v_ref, go_ref, d_ref, mo_ref, vo_ref):
        g = g_ref[0]
        for k in range(1, N_DEV):
            g = g + g_ref[k]
        d, mn, vn = _adamw(w_ref[...], g, m_ref[...], v_ref[...])
        go_ref[...] = g
        d_ref[...] = d
        mo_ref[...] = mn
        vo_ref[...] = vn

    return pl.pallas_call(body, name=name, out_shape=(S((R, C), F32),) * 4, compiler_params=_params())(w, gs, m, v)


def kernel(x, norm_ffn1, w_ffn1_gu, w_ffn1_down, norm_mix, w_in, sinks, norm_out_sb, norm_out_swa, w_out, norm_ffn2, w_ffn2_gu, w_ffn2_down, rel_bias, norm_final, loss_target, m_norm_ffn1, m_w_ffn1_gu, m_w_ffn1_down, m_norm_mix, m_w_in, m_sinks, m_norm_out_sb, m_norm_out_swa, m_w_out, m_norm_ffn2, m_w_ffn2_gu, m_w_ffn2_down, m_rel_bias, m_norm_final, v_norm_ffn1, v_w_ffn1_gu, v_w_ffn1_down, v_norm_mix, v_w_in, v_sinks, v_norm_out_sb, v_norm_out_swa, v_w_out, v_norm_ffn2, v_w_ffn2_gu, v_w_ffn2_down, v_rel_bias, v_norm_final):
    L = norm_ffn1.shape[0]
    T, D = x.shape[1], x.shape[2]
    F = w_ffn1_down.shape[1] * N_DEV
    h = x.reshape(T, D)
    target = loss_target.reshape(T, D)
    after, upto, before = _tri_consts()
    bprev, bcur = _t5_buckets()

    local = {}
    for l in range(L):
        local[f"gu1_{l}"] = w_ffn1_gu[l].T.astype(BF16)
        local[f"d1_{l}"] = w_ffn1_down[l].astype(BF16)
        local[f"in_{l}"] = w_in[l].T.astype(BF16)
        local[f"out_{l}"] = w_out[l].astype(BF16)
        local[f"gu2_{l}"] = w_ffn2_gu[l].T.astype(BF16)
        local[f"d2_{l}"] = w_ffn2_down[l].astype(BF16)
    half, full = {}, {}
    grads, chip_sum, recv_b = {}, {}, {}

    def run(fn, *args, ag1=(), ag2=(), rs1=(), rs2=()):
        ag1 = [n for n in ag1 if n in local]
        ag2 = [n for n in ag2 if n in half]
        riders = ([gather_first(local[n]) for n in ag1] + [gather_second(half[n]) for n in ag2]
                  + [scatter_first(grads[n][1]) for n in rs1] + [scatter_second(chip_sum[n][1]) for n in rs2])
        if not riders:
            return fn(*args)
        outs, per = fn(*args, riders=riders)
        per = [p[0] for p in per]
        for n in ag1:
            half[n] = per.pop(0)
        for n in ag2:
            buf = per.pop(0)
            full[n] = buf.reshape(N_DEV * buf.shape[1], D)
        for n in rs1:
            chip_sum[n] = scatter_add(grads[n][0], per.pop(0), f"rs_add_{n}")
        for n in rs2:
            recv_b[n] = per.pop(0)
        return outs

    def idle(name, riders=()):
        return None, idle_host(riders, name)

    gu = lambda n: full[n].reshape(2, F, D)
    slots = lambda pair: tuple(t.reshape(N_DEV, -1, D) for t in pair)
    vec = lambda a: a.reshape(1, -1)

    run(idle, "ag_head0", ag1=("gu1_0", "d1_0"))
    run(idle, "ag_head1", ag2=("gu1_0", "d1_0"))
    saved = []
    for l in range(L):
        nx = l + 1
        s = {"h0": h}
        s["n1"], s["gate1"], s["up1"], s["a1"] = run(ffn_up_fwd, h, vec(norm_ffn1[l]), gu(f"gu1_{l}"), f"ffn1_up{l}",
                                                     ag1=(f"in_{l}", f"out_{l}"))
        h = run(ffn_down_fwd, s["a1"], full[f"d1_{l}"], h, f"ffn1_down{l}", ag2=(f"in_{l}", f"out_{l}"))
        s["h1"] = h
        s["n2"], s["p"] = mix_in_fwd(h, vec(norm_mix[l]), full[f"in_{l}"], f"mix_in{l}")
        s["o_sb"], s["tot"] = run(sb_attn_fwd, s["p"], after, f"sb_fwd{l}", ag1=(f"gu2_{l}", f"d2_{l}"))
        s["o_sw"], s["lse"] = run(swa_fwd, s["p"], vec(sinks[l]), rel_bias, bprev, bcur, f"swa_fwd{l}",
                                  ag2=(f"gu2_{l}", f"d2_{l}"), ag1=(f"d1_{nx}",))
        s["mixed"], h = run(mix_out_fwd, s["o_sb"], s["o_sw"], vec(norm_out_sb[l]), vec(norm_out_swa[l]), full[f"out_{l}"], h,
                            f"mix_out{l}", ag2=(f"d1_{nx}",))
        s["h2"] = h
        s["n3"], s["gate2"], s["up2"], s["a2"] = run(ffn_up_fwd, h, vec(norm_ffn2[l]), gu(f"gu2_{l}"), f"ffn2_up{l}",
                                                     ag1=(f"gu1_{nx}",))
        h = run(ffn_down_fwd, s["a2"], full[f"d2_{l}"], h, f"ffn2_down{l}", ag2=(f"gu1_{nx}",))
        saved.append(s)

    loss_part, dh, dhb, dg_final = loss_head(h, vec(norm_final), target, "loss_head")
    loss = lax.psum(loss_part[0, 0], ("x", "y", "c"))

    small = {k: [None] * L for k in ("ffn1", "mix", "sinks", "osb", "osw", "ffn2", "dsc")}
    for l in reversed(range(L)):
        s = saved[l]

        def ffn_bwd(dh, dhb, tag, gate, up, a, n, h_in, g, r_down, r_dwgu, r_up):
            gu_n, d_n = f"gu{tag}_{l}", f"d{tag}_{l}"
            dgu = run(ffn_down_bwd, dhb, full[d_n], gate, up, f"ffn{tag}_down_bwd{l}", **r_down)
            grads[gu_n] = slots(run(tn_matmul, dgu, n, 1.0, f"ffn{tag}_dwgu{l}", **r_dwgu))
            grads[d_n] = slots(run(tn_matmul, a[None], dhb, 0.5, f"ffn{tag}_dwd{l}", rs1=(gu_n,)))
            return run(nn_rms_bwd, dgu, gu(gu_n), h_in, g, dh, F // 2, f"ffn{tag}_up_bwd{l}", rs1=(d_n,), **r_up)

        carried = (f"d1_{l + 1}",) if l + 1 < L else ()
        dh, dhb, small["ffn2"][l] = ffn_bwd(dh, dhb, 2, s["gate2"], s["up2"], s["a2"], s["n3"], s["h2"], vec(norm_ffn2[l]),
                                            dict(rs2=carried), {}, {})
        do_sb, do_sw, small["osb"][l], small["osw"][l] = mix_out_bwd(
            dhb, full[f"out_{l}"], s["o_sb"], s["o_sw"], vec(norm_out_sb[l]), vec(norm_out_swa[l]), f"mix_out_bwd{l}")
        grads[f"out_{l}"] = slots(tn_matmul(s["mixed"][None], dhb, 1.0, f"dwout{l}"))
        dq_sb, dk_sb, dv_sb = run(sb_attn_bwd, s["p"], do_sb, s["tot"], upto, before, f"sb_bwd{l}",
                                  rs2=(f"gu2_{l}", f"d2_{l}"), rs1=(f"out_{l}",))
        dq_sw, dk_sw, dv_sw, small["sinks"][l], small["dsc"][l] = run(
            swa_bwd, s["p"], do_sw, s["lse"], vec(sinks[l]), rel_bias, bprev, bcur, f"swa_bwd{l}", rs2=(f"out_{l}",))
        dp = jnp.concatenate([dq_sb, dk_sb, dv_sb, dq_sw, dk_sw, dv_sw], axis=1)
        dh, dhb, small["mix"][l] = nn_rms_bwd(dp[None], full[f"in_{l}"][None], s["h1"], vec(norm_mix[l]), dh, IN_W // 2,
                                              f"mix_in_bwd{l}")
        grads[f"in_{l}"] = slots(tn_matmul(dp[None], s["n2"], 1.0, f"dwin{l}"))
        dh, dhb, small["ffn1"][l] = ffn_bwd(dh, dhb, 1, s["gate1"], s["up1"], s["a1"], s["n1"], s["h0"], vec(norm_ffn1[l]),
                                            dict(rs1=(f"in_{l}",)), dict(rs2=(f"in_{l}",)), dict(rs2=(f"gu1_{l}",)))

    grad_x = dh.reshape(x.shape)
    run(idle, "rs_tail", rs2=("d1_0",))

    def scattered(tag, transpose):
        out = []
        for l in range(L):
            n = f"{tag}_{l}"
            mine = scatter_sum(chip_sum[n][0], recv_b[n], f"rs_sum_{n}")
            out.append(mine.T if transpose else mine)
        return jnp.stack(out)

    g_gu1, g_d1, g_in = scattered("gu1", True), scattered("d1", False), scattered("in", True)
    g_out, g_gu2, g_d2 = scattered("out", False), scattered("gu2", True), scattered("d2", False)

    upd = {}
    for nm, w, g, m, v in (("gu1", w_ffn1_gu, g_gu1, m_w_ffn1_gu, v_w_ffn1_gu), ("d1", w_ffn1_down, g_d1, m_w_ffn1_down, v_w_ffn1_down),
                           ("in", w_in, g_in, m_w_in, v_w_in), ("out", w_out, g_out, m_w_out, v_w_out),
                           ("gu2", w_ffn2_gu, g_gu2, m_w_ffn2_gu, v_w_ffn2_gu), ("d2", w_ffn2_down, g_d2, m_w_ffn2_down, v_w_ffn2_down)):
        upd[nm] = (g,) + tuple(adamw_rows(w, g, m, v, f"adamw_{nm}"))

    d_rel = rel_bias_grad(small["dsc"], bprev, bcur, "rel_bias_grad")[:, :8]

    PW = max(D, SB_W + SWA_W)

    def pack(ffn1, mix, ffn2, final, osb, osw, snk, rel):
        wide = lambda a: jnp.pad(a.reshape(-1), (0, PW - a.size))
        rows = [wide(ffn1[l]) for l in range(L)] + [wide(mix[l]) for l in range(L)] + [wide(ffn2[l]) for l in range(L)]
        rows.append(wide(final))
        rows += [wide(jnp.concatenate([osb[l].reshape(-1), osw[l].reshape(-1)])) for l in range(L)]
        rows.append(wide(jnp.concatenate([snk[l].reshape(-1)[:8] for l in range(L)] + [rel.reshape(-1)])))
        arr = jnp.stack(rows)
        return jnp.pad(arr, ((0, (-arr.shape[0]) % 8), (0, 0)))

    def unpack(arr):
        ffn1, mix, ffn2 = arr[0:L, :D], arr[L:2 * L, :D], arr[2 * L:3 * L, :D]
        final = arr[3 * L, :D]
        ob = arr[3 * L + 1:4 * L + 1]
        tail = arr[4 * L + 1]
        return (ffn1, mix, tail[:8 * L].reshape(L, 8), ob[:, :SB_W], ob[:, SB_W:SB_W + SWA_W], ffn2,
                tail[8 * L:8 * L + N_BUCKETS * 8].reshape(N_BUCKETS, 8), final)

    g_small = pack(small["ffn1"], small["mix"], small["ffn2"], dg_final, small["osb"], small["osw"], small["sinks"], d_rel)
    w_small = pack(norm_ffn1, norm_mix, norm_ffn2, norm_final, norm_out_sb, norm_out_swa, sinks, rel_bias)
    m_small = pack(m_norm_ffn1, m_norm_mix, m_norm_ffn2, m_norm_final, m_norm_out_sb, m_norm_out_swa, m_sinks, m_rel_bias)
    v_small = pack(v_norm_ffn1, v_norm_mix, v_norm_ffn2, v_norm_final, v_norm_out_sb, v_norm_out_swa, v_sinks, v_rel_bias)
    gs_small = all_gather_rows(g_small, "ag_small")
    small_out = [unpack(a) for a in adamw_small(w_small, gs_small, m_small, v_small, "adamw_small")]

    def group(k):
        sm = small_out[k]
        return (sm[0], upd["gu1"][k], upd["d1"][k], sm[1], upd["in"][k], sm[2], sm[3], sm[4], upd["out"][k], sm[5],
                upd["gu2"][k], upd["d2"][k], sm[6], sm[7])

    return (loss, grad_x, *group(0), *group(1), *group(2), *group(3))
```

```python
import math

import jax
import jax.numpy as jnp
from jax import lax
from jax.experimental import pallas as pl
from jax.experimental.pallas import tpu as pltpu

F32 = jnp.float32
BF16 = jnp.bfloat16
S = jax.ShapeDtypeStruct

N_DEV = 8
HEAD_DIM = 64
SB_HEADS = 8
PAIR = 2 * HEAD_DIM
SB_W = 512
SWA_W = 512
KV_W = 128
IN_W = 3 * SB_W + SWA_W + 2 * KV_W
QB = 128
N_BUCKETS = 32
MAX_DISTANCE = 128
EPS = 1e-6
NEG_INF = -1e30
SCALE = HEAD_DIM ** -0.5

ADAM_LR = 0.001
ADAM_B1 = 0.9
ADAM_B2 = 0.999
ADAM_EPS = 1e-08
ADAM_WD = 0.01
ADAM_STEP = 10

VMEM_LIMIT = 56 * 1024 * 1024
MESH = pl.DeviceIdType.MESH


def _params(sem=None, vmem=VMEM_LIMIT):
    return pltpu.CompilerParams(dimension_semantics=sem, vmem_limit_bytes=vmem)


def _nn(a, b):
    return jnp.dot(a, b, preferred_element_type=F32)


def _nt(a, b):
    return lax.dot_general(a, b, (((1,), (1,)), ((), ())), preferred_element_type=F32)


def _tn(a, b):
    return lax.dot_general(a, b, (((0,), (0,)), ((), ())), preferred_element_type=F32)


def _tri(x, m2):
    hi = x.astype(BF16)
    lo = (x - hi.astype(F32)).astype(BF16)
    return _nn(jnp.concatenate([hi, lo], axis=1), m2)


def _rms(x, g):
    r = lax.rsqrt(jnp.mean(x * x, axis=-1, keepdims=True) + EPS)
    return x * r * g


def _rms_bwd(dy, x, g):
    r = lax.rsqrt(jnp.mean(x * x, axis=-1, keepdims=True) + EPS)
    xhat = x * r
    u = dy * g
    dx = r * (u - xhat * jnp.mean(u * xhat, axis=-1, keepdims=True))
    return dx, jnp.sum(dy * xhat, axis=0, keepdims=True)


def _softplus_logsig(z):
    sp = jnp.maximum(z, 0.0) + jnp.log(1.0 + jnp.exp(-jnp.abs(z)))
    return sp, z - sp


def _tile(n, want):
    t = min(n, want)
    while n % t:
        t //= 2
    return t


def _place():
    x, y, c = lax.axis_index("x"), lax.axis_index("y"), lax.axis_index("c")
    chips = [(1 - x, y), (x, 1 - y), (1 - x, 1 - y)]
    return x, y, c, chips


def all_gather_rows(v, name):
    R, C = v.shape

    def body(v_ref, out_ref, send_sems, recv_sems, local_sem):
        x, y, c, chips = _place()
        me, sibling = (x, y, c), (x, y, 1 - c)

        def slot(px, py, pc):
            return out_ref.at[4 * px + 2 * py + pc]

        def copy(k, block, to, src=None):
            return pltpu.make_async_remote_copy(
                src_ref=slot(*block) if src is None else src, dst_ref=slot(*block),
                send_sem=send_sems.at[k], recv_sem=recv_sems.at[k], device_id=to, device_id_type=MESH)

        mine = pltpu.make_async_copy(v_ref, slot(*me), local_sem)
        mine.start()
        first = [copy(0, me, sibling, src=v_ref)]
        first += [copy(1 + j, me, (*chip, c), src=v_ref) for j, chip in enumerate(chips)]
        for cp in first:
            cp.start()
        passed = [copy(4 + j, (*chip, c), sibling) for j, chip in enumerate(chips)]
        for j, chip in enumerate(chips):
            copy(1 + j, (*chip, c), me).wait_recv()
            passed[j].start()
        copy(0, sibling, me).wait_recv()
        for j, chip in enumerate(chips):
            copy(4 + j, (*chip, 1 - c), me).wait_recv()
        for cp in first + passed:
            cp.wait_send()
        mine.wait()

    return pl.pallas_call(
        body, name=name, out_shape=S((N_DEV, R, C), v.dtype),
        in_specs=[pl.BlockSpec(memory_space=pl.ANY)], out_specs=pl.BlockSpec(memory_space=pl.ANY),
        scratch_shapes=[pltpu.SemaphoreType.DMA((7,)), pltpu.SemaphoreType.DMA((7,)), pltpu.SemaphoreType.DMA],
    )(v)


class _Exchange:
    def __init__(self, ins, outs, n_remote, n_local, plan, aliases=None):
        self.ins, self.outs, self.n_remote, self.n_local, self.plan = list(ins), list(outs), n_remote, n_local, plan
        self.aliases = aliases or {}

    def scratch(self):
        return [pltpu.SemaphoreType.DMA((self.n_remote,)), pltpu.SemaphoreType.DMA((self.n_remote,)),
                pltpu.SemaphoreType.DMA((max(self.n_local, 1),))]

    def _copies(self, in_refs, out_refs, sems):
        send_sems, recv_sems, local_sems = sems
        remote, local = self.plan(in_refs, out_refs)
        rem = [pltpu.make_async_remote_copy(src_ref=s, dst_ref=d, send_sem=send_sems.at[k], recv_sem=recv_sems.at[k],
                                            device_id=dev, device_id_type=MESH) for k, (s, d, dev) in enumerate(remote)]
        return rem, [pltpu.make_async_copy(s, d, local_sems.at[k]) for k, (s, d) in enumerate(local)]

    def start(self, in_refs, out_refs, sems):
        rem, loc = self._copies(in_refs, out_refs, sems)
        for cp in rem + loc:
            cp.start()

    def finish(self, in_refs, out_refs, sems):
        rem, loc = self._copies(in_refs, out_refs, sems)
        for cp in rem:
            cp.wait_recv()
        for cp in rem:
            cp.wait_send()
        for cp in loc:
            cp.wait()


def gather_first(v):
    R, C = v.shape

    def plan(ins, outs):
        x, y, c, chips = _place()
        dst = outs[0].at[4 * x + 2 * y + c]
        return [(ins[0], dst, (x, y, 1 - c))] + [(ins[0], dst, (*chip, c)) for chip in chips], [(ins[0], dst)]

    return _Exchange([v], [S((N_DEV, R, C), v.dtype)], 4, 1, plan)


def gather_second(buf):
    def plan(ins, outs):
        x, y, c, chips = _place()
        return [(outs[0].at[4 * px + 2 * py + c], outs[0].at[4 * px + 2 * py + c], (x, y, 1 - c)) for px, py in chips], []

    return _Exchange([buf], [S(buf.shape, buf.dtype)], 3, 0, plan, aliases={0: 0})


def scatter_first(gb):
    _, R, C = gb.shape

    def plan(ins, outs):
        x, y, c, chips = _place()
        owners = [(x, y)] + chips
        return [(ins[0].at[4 * px + 2 * py + (1 - c)], outs[0].at[j], (x, y, 1 - c)) for j, (px, py) in enumerate(owners)], []

    return _Exchange([gb], [S((4, R, C), BF16)], 4, 0, plan)


def scatter_second(sb):
    def plan(ins, outs):
        x, y, c, chips = _place()
        return [(ins[0].at[j], outs[0].at[j], (*chips[j], c)) for j in range(3)], []

    return _Exchange([sb], [S(sb.shape, BF16)], 3, 0, plan)


def _call(body, *, name, grid, in_specs, out_specs, out_shape, args, scratch=(), sem=None, riders=()):
    single = not isinstance(out_shape, (tuple, list))
    out_shape = (out_shape,) if single else tuple(out_shape)
    out_specs = (out_specs,) if single else tuple(out_specs)
    n_in, n_out, n_sc = len(in_specs), len(out_shape), len(scratch)
    if not riders:
        res = pl.pallas_call(body, name=name, grid=grid, in_specs=list(in_specs), out_specs=out_specs, out_shape=out_shape,
                             scratch_shapes=list(scratch), compiler_params=_params(sem))(*args)
        return res[0] if single else res
    r_ins = [a for r in riders for a in r.ins]
    r_outs = [o for r in riders for o in r.outs]
    r_scr = [s for r in riders for s in r.scratch()]
    aliases, i0, o0 = {}, n_in, n_out
    for r in riders:
        for a, b in r.aliases.items():
            aliases[i0 + a] = o0 + b
        i0, o0 = i0 + len(r.ins), o0 + len(r.outs)
    steps = math.prod(grid)

    def full(*refs):
        ins, rin = refs[:n_in], refs[n_in:n_in + len(r_ins)]
        pos = n_in + len(r_ins)
        outs, rout = refs[pos:pos + n_out], refs[pos + n_out:pos + n_out + len(r_outs)]
        pos += n_out + len(r_outs)
        sc, rsc = refs[pos:pos + n_sc], refs[pos + n_sc:]
        step = 0
        for d, n in enumerate(grid):
            step = step * n + pl.program_id(d)

        def each(method):
            i, o = 0, 0
            for k, r in enumerate(riders):
                getattr(r, method)(rin[i:i + len(r.ins)], rout[o:o + len(r.outs)], rsc[3 * k:3 * k + 3])
                i, o = i + len(r.ins), o + len(r.outs)

        @pl.when(step == 0)
        def _():
            each("start")
        body(*ins, *outs, *sc)

        @pl.when(step == steps - 1)
        def _():
            each("finish")

    anywhere = pl.BlockSpec(memory_space=pl.ANY)
    res = pl.pallas_call(
        full, name=name, grid=grid, in_specs=list(in_specs) + [anywhere] * len(r_ins),
        out_specs=out_specs + (anywhere,) * len(r_outs), out_shape=out_shape + tuple(r_outs),
        scratch_shapes=list(scratch) + r_scr, input_output_aliases=aliases,
        compiler_params=_params(("arbitrary",) * len(grid)))(*args, *r_ins)
    host, rest, per = res[:n_out], list(res[n_out:]), []
    for r in riders:
        per.append(rest[:len(r.outs)])
        rest = rest[len(r.outs):]
    return (host[0] if single else tuple(host)), per


def idle_host(riders, name):
    def body(o_ref):
        o_ref[...] = jnp.zeros_like(o_ref)

    return _call(body, name=name, grid=(1,), in_specs=[], out_specs=pl.BlockSpec((8, QB), lambda i: (0, 0)),
                 out_shape=S((8, QB), F32), args=(), riders=riders)[1]


def _rows_tile(n, cap):
    return max(t for t in range(16, min(n, cap) + 1, 16) if n % t == 0)


def scatter_add(g, ra, name):
    _, R, C = g.shape
    tr = _rows_tile(R, 176)
    x, y, c, chips = _place()
    slots = jnp.stack([4 * px + 2 * py + c for px, py in [(x, y)] + chips]).astype(jnp.int32)

    def body(s_ref, g0, g1, g2, g3, ra_ref, own_ref, sb_ref):
        own_ref[...] = g0[...] + ra_ref[0].astype(F32)
        for j, gj in enumerate((g1, g2, g3)):
            sb_ref[j] = (gj[...] + ra_ref[j + 1].astype(F32)).astype(BF16)

    spec = pltpu.PrefetchScalarGridSpec(
        num_scalar_prefetch=1, grid=(R // tr,),
        in_specs=[pl.BlockSpec((None, tr, C), lambda i, s, j=j: (s[j], i, 0)) for j in range(4)]
        + [pl.BlockSpec((4, tr, C), lambda i, s: (0, i, 0))],
        out_specs=(pl.BlockSpec((tr, C), lambda i, s: (i, 0)), pl.BlockSpec((3, tr, C), lambda i, s: (0, i, 0))))
    return pl.pallas_call(body, name=name, grid_spec=spec, out_shape=(S((R, C), F32), S((3, R, C), BF16)),
                          compiler_params=_params(("parallel",)))(slots, g, g, g, g, ra)


def scatter_sum(own, rb, name):
    R, C = own.shape
    tr = _rows_tile(R, 176)

    def body(o_ref, r_ref, g_ref):
        g_ref[...] = o_ref[...] + r_ref[0].astype(F32) + r_ref[1].astype(F32) + r_ref[2].astype(F32)

    return _call(body, name=name, grid=(R // tr,), in_specs=[pl.BlockSpec((tr, C), lambda i: (i, 0)),
                                                              pl.BlockSpec((3, tr, C), lambda i: (0, i, 0))],
                 out_specs=pl.BlockSpec((tr, C), lambda i: (i, 0)), out_shape=S((R, C), F32), args=(own, rb), sem=("parallel",))


def rms_cast(h, g, name):
    T, D = h.shape
    tm = _tile(T, 512)

    def body(h_ref, g_ref, n_ref):
        n_ref[...] = _rms(h_ref[...], g_ref[...]).astype(BF16)

    row = pl.BlockSpec((tm, D), lambda i: (i, 0))
    return _call(body, name=name, grid=(T // tm,), out_shape=S((T, D), BF16), in_specs=[row, pl.BlockSpec((1, D), lambda i: (0, 0))],
                 out_specs=row, sem=("parallel",), args=(h, g))


def ffn_up_fwd(n, wgu, name, riders=()):
    T, D = n.shape
    F = wgu.shape[1]
    tr, tn = _tile(T, 512), _tile(F, 256)

    def body(n_ref, wg_ref, wu_ref, dgate_ref, dup_ref, a_ref):
        wg, wu = wg_ref[...], wu_ref[...]
        for r in range(T // tr):
            rows = slice(r * tr, (r + 1) * tr)
            x = n_ref[rows, :]
            gate = _nt(x, wg)
            up = _nt(x, wu)
            s = jax.nn.sigmoid(gate)
            silu = gate * s
            dgate_ref[rows, :] = (up * (s * (1.0 + gate * (1.0 - s)))).astype(BF16)
            dup_ref[rows, :] = silu.astype(BF16)
            a_ref[rows, :] = (silu * up).astype(BF16)

    tile = pl.BlockSpec((T, tn), lambda j: (0, j))
    return _call(
        body, name=name, grid=(F // tn,), out_shape=(S((T, F), BF16),) * 3,
        in_specs=[pl.BlockSpec((T, D), lambda j: (0, 0)),
                  pl.BlockSpec((None, tn, D), lambda j: (0, j, 0)), pl.BlockSpec((None, tn, D), lambda j: (1, j, 0))],
        out_specs=(tile, tile, tile), sem=("parallel",), args=(n, wgu, wgu), riders=riders)


def ffn_down_fwd(a, wd, h, g_next, name, riders=()):
    T, F = a.shape
    D = wd.shape[1]
    tm = _tile(T, 256)

    def body(a_ref, w_ref, h_ref, *rest):
        out = h_ref[...] + 0.5 * _nn(a_ref[...], w_ref[...])
        if g_next is None:
            rest[0][...] = out
        else:
            g_ref, o_ref, n_ref = rest
            o_ref[...] = out
            n_ref[...] = _rms(out, g_ref[...]).astype(BF16)

    row = pl.BlockSpec((tm, D), lambda i: (i, 0))
    more = g_next is not None
    return _call(
        body, name=name, grid=(T // tm,), out_shape=(S((T, D), F32), S((T, D), BF16)) if more else S((T, D), F32),
        in_specs=[pl.BlockSpec((tm, F), lambda i: (i, 0)), pl.BlockSpec((F, D), lambda i: (0, 0)), row]
        + ([pl.BlockSpec((1, D), lambda i: (0, 0))] if more else []),
        out_specs=(row, row) if more else row,
        sem=("parallel",), args=(a, wd, h) + ((g_next,) if more else ()), riders=riders)


def mix_in_fwd(h, g, win, name):
    T, D = h.shape
    N = win.shape[0]
    tm = _tile(T, 256)

    def body(h_ref, g_ref, w_ref, n_ref, p_ref):
        n = _rms(h_ref[...], g_ref[...]).astype(BF16)
        n_ref[...] = n
        p_ref[...] = _nt(n, w_ref[...]).astype(BF16)

    return pl.pallas_call(
        body, name=name, grid=(T // tm,), out_shape=(S((T, D), BF16), S((T, N), BF16)),
        in_specs=[pl.BlockSpec((tm, D), lambda i: (i, 0)), pl.BlockSpec((1, D), lambda i: (0, 0)),
                  pl.BlockSpec((N, D), lambda i: (0, 0))],
        out_specs=(pl.BlockSpec((tm, D), lambda i: (i, 0)), pl.BlockSpec((tm, N), lambda i: (i, 0))),
        compiler_params=_params(("parallel",)),
    )(h, g, win)


def _tri_consts():
    r = lax.broadcasted_iota(jnp.int32, (QB, QB), 0)
    c = lax.broadcasted_iota(jnp.int32, (QB, QB), 1)
    ones = jnp.ones((QB, QB), BF16)

    def stacked(tri):
        m = jnp.concatenate([tri.astype(BF16), ones], axis=1)
        return jnp.concatenate([m, m], axis=0)

    return stacked(r > c), stacked(r <= c), stacked(r < c)


def _half_masks():
    lane = lax.broadcasted_iota(jnp.int32, (QB, PAIR), 1)
    row = lax.broadcasted_iota(jnp.int32, (QB, PAIR), 0)
    return lane < HEAD_DIM, lane, row


def sb_attn_fwd(p, after, name, riders=()):
    T = p.shape[0]
    nq = T // QB

    def body(q_ref, k_ref, v_ref, m_ref, o_ref, tot_ref, q_sc, acc_ref):
        i = pl.program_id(0)
        lo, lane, row = _half_masks()
        causal = lane < row
        for hp in range(SB_HEADS // 2):
            q2 = q_ref[:, hp * PAIR:(hp + 1) * PAIR].astype(F32) * SCALE
            q_sc[2 * hp] = jnp.where(lo, q2, 0.0).astype(BF16)
            q_sc[2 * hp + 1] = jnp.where(lo, 0.0, q2).astype(BF16)
        m2 = m_ref[...]

        def block(j, diag):
            r0 = pl.multiple_of(j * QB, QB)
            heads = range(SB_HEADS)
            k2 = [k_ref[pl.ds(r0, QB), hp * PAIR:(hp + 1) * PAIR] for hp in range(SB_HEADS // 2)]
            v2 = [v_ref[pl.ds(r0, QB), hp * PAIR:(hp + 1) * PAIR] for hp in range(SB_HEADS // 2)]
            z = [_nt(q_sc[h], k2[h // 2]) for h in heads]
            spls = [_softplus_logsig(z[h]) for h in heads]
            sp = [jnp.where(causal, spls[h][0], 0.0) if diag else spls[h][0] for h in heads]
            rr = [_tri(sp[h], m2) for h in heads]
            if diag:
                w = [jnp.where(causal, jnp.exp(spls[h][1] - rr[h][:, :QB]), 0.0) for h in heads]
                pv = [_nn(w[h].astype(BF16), v2[h // 2]) for h in heads]
                for h in heads:
                    acc_ref[h] = pv[h]
                    tot_ref[:, h * QB:(h + 1) * QB] = rr[h][:, QB:]
            else:
                c = [tot_ref[:, h * QB:(h + 1) * QB] for h in heads]
                w = [jnp.exp(spls[h][1] - (c[h] + rr[h][:, :QB])) for h in heads]
                pv = [_nn(w[h].astype(BF16), v2[h // 2]) for h in heads]
                for h in heads:
                    acc_ref[h] += pv[h]
                    tot_ref[:, h * QB:(h + 1) * QB] = c[h] + rr[h][:, QB:]

        block(i, True)

        def step(t, carry):
            block(i - 1 - t, False)
            return carry
        lax.fori_loop(0, i, step, 0)
        for hp in range(SB_HEADS // 2):
            o_ref[:, hp * PAIR:(hp + 1) * PAIR] = jnp.where(lo, acc_ref[2 * hp], acc_ref[2 * hp + 1])

    return _call(
        body, name=name, grid=(nq,), out_shape=(S((T, SB_W), F32), S((T, SB_HEADS * QB), F32)),
        in_specs=[pl.BlockSpec((QB, SB_W), lambda i: (i, 0)), pl.BlockSpec((T, SB_W), lambda i: (0, 1)),
                  pl.BlockSpec((T, SB_W), lambda i: (0, 2)), pl.BlockSpec((2 * QB, 2 * QB), lambda i: (0, 0))],
        out_specs=(pl.BlockSpec((QB, SB_W), lambda i: (i, 0)), pl.BlockSpec((QB, SB_HEADS * QB), lambda i: (i, 0))),
        scratch=[pltpu.VMEM((SB_HEADS, QB, PAIR), BF16), pltpu.VMEM((SB_HEADS, QB, PAIR), F32)],
        sem=("arbitrary",), args=(p, p, p, after), riders=riders)


def sb_attn_bwd(p, do, tot, upto, before, name, riders=()):
    T = p.shape[0]
    nq = T // QB

    def body(q_ref, k_ref, v_ref, do_ref, tot_ref, mp_ref, mg_ref, dq_ref, dk_ref, dv_ref,
             q_sc, d_sc, pg_sc, dq_acc, dk_acc, dv_acc):
        i = pl.program_id(0)
        lo, lane, row = _half_masks()
        causal = lane < row
        for hp in range(SB_HEADS // 2):
            q2 = q_ref[:, hp * PAIR:(hp + 1) * PAIR].astype(F32) * SCALE
            d2 = do_ref[:, hp * PAIR:(hp + 1) * PAIR]
            q_sc[2 * hp] = jnp.where(lo, q2, 0.0).astype(BF16)
            q_sc[2 * hp + 1] = jnp.where(lo, 0.0, q2).astype(BF16)
            d_sc[2 * hp] = jnp.where(lo, d2, 0.0).astype(BF16)
            d_sc[2 * hp + 1] = jnp.where(lo, 0.0, d2).astype(BF16)
        mp, mg = mp_ref[...], mg_ref[...]

        @pl.when(i == 0)
        def _():
            dk_acc[...] = jnp.zeros_like(dk_acc)
            dv_acc[...] = jnp.zeros_like(dv_acc)
        pg_sc[...] = jnp.zeros_like(pg_sc)
        dq_acc[...] = jnp.zeros_like(dq_acc)

        def block(j, diag):
            r0 = pl.multiple_of(j * QB, QB)
            heads = range(SB_HEADS)
            pairs = range(SB_HEADS // 2)
            k2 = [k_ref[pl.ds(r0, QB), hp * PAIR:(hp + 1) * PAIR] for hp in pairs]
            v2 = [v_ref[pl.ds(r0, QB), hp * PAIR:(hp + 1) * PAIR] for hp in pairs]
            z = [_nt(q_sc[h], k2[h // 2]) for h in heads]
            dw = [_nt(d_sc[h], v2[h // 2]) for h in heads]
            spls = [_softplus_logsig(z[h]) for h in heads]
            sp = [jnp.where(causal, spls[h][0], 0.0) if diag else spls[h][0] for h in heads]
            rr = [_tri(sp[h], mp) for h in heads]
            pc = [pg_sc[2 * h] for h in heads]
            w = [jnp.exp(spls[h][1] - (tot_ref[:, h * QB:(h + 1) * QB] - (pc[h] + rr[h][:, :QB]))) for h in heads]
            if diag:
                w = [jnp.where(causal, w[h], 0.0) for h in heads]
            gg = [dw[h] * w[h] for h in heads]
            rg = [_tri(gg[h], mg) for h in heads]
            gc = [pg_sc[2 * h + 1] for h in heads]
            dz = [gg[h] - (gg[h] + gc[h] + rg[h][:, :QB]) * jnp.exp(spls[h][1]) for h in heads]
            if diag:
                dz = [jnp.where(causal, dz[h], 0.0) for h in heads]
            dzb = [dz[h].astype(BF16) for h in heads]
            wb = [w[h].astype(BF16) for h in heads]
            dq = [_nn(dzb[h], k2[h // 2]) for h in heads]
            dk = [_tn(dzb[2 * hp], q_sc[2 * hp]) + _tn(dzb[2 * hp + 1], q_sc[2 * hp + 1]) for hp in pairs]
            dv = [_tn(wb[2 * hp], d_sc[2 * hp]) + _tn(wb[2 * hp + 1], d_sc[2 * hp + 1]) for hp in pairs]
            for h in heads:
                dq_acc[h] += dq[h]
                if not diag:
                    pg_sc[2 * h] = pc[h] + rr[h][:, QB:]
                    pg_sc[2 * h + 1] = gc[h] + rg[h][:, QB:]
            for hp in pairs:
                dk_acc[pl.ds(r0, QB), hp * PAIR:(hp + 1) * PAIR] += dk[hp]
                dv_acc[pl.ds(r0, QB), hp * PAIR:(hp + 1) * PAIR] += dv[hp]

        def step(t, carry):
            block(t, False)
            return carry
        lax.fori_loop(0, i, step, 0)
        block(i, True)
        for hp in range(SB_HEADS // 2):
            dq = jnp.where(lo, dq_acc[2 * hp], dq_acc[2 * hp + 1]) * SCALE
            dq_ref[:, hp * PAIR:(hp + 1) * PAIR] = dq.astype(BF16)

        @pl.when(i == nq - 1)
        def _():
            dk_ref[...] = dk_acc[...].astype(BF16)
            dv_ref[...] = dv_acc[...].astype(BF16)

    qtile = pl.BlockSpec((QB, SB_W), lambda i: (i, 0))
    whole = pl.BlockSpec((T, SB_W), lambda i: (0, 0))
    const = pl.BlockSpec((2 * QB, 2 * QB), lambda i: (0, 0))
    return _call(
        body, name=name, grid=(nq,), out_shape=(S((T, SB_W), BF16),) * 3,
        in_specs=[qtile, pl.BlockSpec((T, SB_W), lambda i: (0, 1)), pl.BlockSpec((T, SB_W), lambda i: (0, 2)), qtile,
                  pl.BlockSpec((QB, SB_HEADS * QB), lambda i: (i, 0)), const, const],
        out_specs=(qtile, whole, whole),
        scratch=[pltpu.VMEM((SB_HEADS, QB, PAIR), BF16), pltpu.VMEM((SB_HEADS, QB, PAIR), BF16),
                 pltpu.VMEM((2 * SB_HEADS, QB, QB), F32), pltpu.VMEM((SB_HEADS, QB, PAIR), F32),
                 pltpu.VMEM((T, SB_W), F32), pltpu.VMEM((T, SB_W), F32)],
        sem=("arbitrary",), args=(p, p, p, do, tot, upto, before), riders=riders)


def _t5_buckets():
    a = lax.broadcasted_iota(jnp.int32, (QB, QB), 0)
    c = lax.broadcasted_iota(jnp.int32, (QB, QB), 1)

    def bucket(dist):
        dist = jnp.maximum(dist, 0)
        max_exact = N_BUCKETS // 2
        d = jnp.maximum(dist, 1).astype(F32)
        large = max_exact + (jnp.log(d / max_exact) / math.log(MAX_DISTANCE / max_exact)
                             * (N_BUCKETS - max_exact)).astype(jnp.int32)
        large = jnp.minimum(large, N_BUCKETS - 1)
        return jnp.where(dist < max_exact, dist, large)

    return bucket(QB + a - c), bucket(a - c)


def _swa_common(i, kp_ref, kc_ref, vp_ref, vc_ref, bp_ref, bc_ref, rb_ref, bias_ref):
    lo, lane, row = _half_masks()

    @pl.when(i == 0)
    def _():
        for blk, b_ref in enumerate((bp_ref, bc_ref)):
            bk = b_ref[...]
            for h in range(8):
                acc = jnp.zeros((QB, QB), F32)
                for b in range(N_BUCKETS):
                    acc = jnp.where(bk == b, rb_ref[b, h], acc)
                bias_ref[h, blk] = acc

    band = [(lane > row) & (i > 0), lane <= row]

    def halves(ref):
        t = ref[...].astype(F32)
        sw = pltpu.roll(t, HEAD_DIM, 1)
        return [[jnp.where(lo, t, 0.0).astype(BF16), jnp.where(lo, 0.0, sw).astype(BF16)],
                [jnp.where(lo, sw, 0.0).astype(BF16), jnp.where(lo, 0.0, t).astype(BF16)]]

    ks = [halves(kp_ref), halves(kc_ref)]
    vs = [halves(vp_ref), halves(vc_ref)]
    return lo, band, ks, vs


def swa_fwd(p, sinks, rel_bias, bprev, bcur, name, riders=()):
    T = p.shape[0]
    nq = T // QB
    kcol, vcol = (3 * SB_W + SWA_W) // KV_W, (3 * SB_W + SWA_W) // KV_W + 1

    def body(q_ref, kp_ref, kc_ref, vp_ref, vc_ref, bp_ref, bc_ref, sink_ref, rb_ref, o_ref, lse_ref, bias_ref):
        i = pl.program_id(0)
        lo, band, ks, vs = _swa_common(i, kp_ref, kc_ref, vp_ref, vc_ref, bp_ref, bc_ref, rb_ref, bias_ref)
        for g in range(4):
            kh = g // 2
            q2 = q_ref[:, g * PAIR:(g + 1) * PAIR]
            outs = []
            for pos in range(2):
                h = 2 * g + pos
                sc = [jnp.where(band[b], _nt(q2, ks[b][kh][pos]) * SCALE + bias_ref[h, b], NEG_INF) for b in range(2)]
                sink = sink_ref[0, h]
                m = jnp.maximum(jnp.maximum(jnp.max(sc[0], axis=1, keepdims=True),
                                            jnp.max(sc[1], axis=1, keepdims=True)), sink)
                e = [jnp.exp(sc[b] - m) for b in range(2)]
                den = jnp.sum(e[0], axis=1, keepdims=True) + jnp.sum(e[1], axis=1, keepdims=True) + jnp.exp(sink - m)
                outs.append(_nn((e[0] / den).astype(BF16), vs[0][kh][pos]) + _nn((e[1] / den).astype(BF16), vs[1][kh][pos]))
                lse_ref[:, h * QB:(h + 1) * QB] = jnp.broadcast_to(m + jnp.log(den), (QB, QB))
            o_ref[:, g * PAIR:(g + 1) * PAIR] = outs[0] + outs[1]

    kv = lambda col, prev: pl.BlockSpec((QB, KV_W), (lambda i: (jnp.maximum(i - 1, 0), col)) if prev else (lambda i: (i, col)))
    full = pl.BlockSpec((QB, QB), lambda i: (0, 0))
    smem = pl.BlockSpec(memory_space=pltpu.SMEM)
    return _call(
        body, name=name, grid=(nq,), out_shape=(S((T, SWA_W), F32), S((T, 8 * QB), F32)),
        in_specs=[pl.BlockSpec((QB, SWA_W), lambda i: (i, 3)), kv(kcol, True), kv(kcol, False), kv(vcol, True), kv(vcol, False),
                  full, full, smem, smem],
        out_specs=(pl.BlockSpec((QB, SWA_W), lambda i: (i, 0)), pl.BlockSpec((QB, 8 * QB), lambda i: (i, 0))),
        scratch=[pltpu.VMEM((8, 2, QB, QB), F32)],
        sem=("arbitrary",), args=(p, p, p, p, p, bprev, bcur, sinks, rel_bias), riders=riders)


def swa_bwd(p, do, lse, sinks, rel_bias, bprev, bcur, name, riders=()):
    T = p.shape[0]
    nq = T // QB
    kcol, vcol = (3 * SB_W + SWA_W) // KV_W, (3 * SB_W + SWA_W) // KV_W + 1

    def body(q_ref, kp_ref, kc_ref, vp_ref, vc_ref, do_ref, lse_ref, bp_ref, bc_ref, sink_ref, rb_ref,
             dq_ref, dk_ref, dv_ref, dsink_ref, dsc_ref, bias_ref, dk_acc, dv_acc):
        i = pl.program_id(0)
        lo, band, ks, vs = _swa_common(i, kp_ref, kc_ref, vp_ref, vc_ref, bp_ref, bc_ref, rb_ref, bias_ref)

        @pl.when(i == 0)
        def _():
            dk_acc[...] = jnp.zeros_like(dk_acc)
            dv_acc[...] = jnp.zeros_like(dv_acc)
            dsc_ref[...] = jnp.zeros_like(dsc_ref)
            dsink_ref[...] = jnp.zeros_like(dsink_ref)

        lane1 = lax.broadcasted_iota(jnp.int32, (1, QB), 1)
        dsink = jnp.zeros((1, QB), F32)
        dk_parts = [[[None, None], [None, None]], [[None, None], [None, None]]]
        dv_parts = [[[None, None], [None, None]], [[None, None], [None, None]]]

        def add(parts, b, pos, kh, val):
            parts[b][pos][kh] = val if parts[b][pos][kh] is None else parts[b][pos][kh] + val

        for g in range(4):
            kh = g // 2
            q2 = q_ref[:, g * PAIR:(g + 1) * PAIR]
            q2f = q2.astype(F32)
            d2f = do_ref[:, g * PAIR:(g + 1) * PAIR]
            d2 = d2f.astype(BF16)
            dq = None
            for pos in range(2):
                h = 2 * g + pos
                keep = lo if pos == 0 else ~lo
                qh = jnp.where(keep, q2f, 0.0).astype(BF16)
                dh = jnp.where(keep, d2f, 0.0).astype(BF16)
                lse_h = lse_ref[:, h * QB:(h + 1) * QB]
                sink = sink_ref[0, h]
                pr = [jnp.exp(jnp.where(band[b], _nt(q2, ks[b][kh][pos]) * SCALE + bias_ref[h, b], NEG_INF) - lse_h)
                      for b in range(2)]
                dp = [_nt(d2, vs[b][kh][pos]) for b in range(2)]
                delta = jnp.sum(pr[0] * dp[0], axis=1, keepdims=True) + jnp.sum(pr[1] * dp[1], axis=1, keepdims=True)
                p_sink = jnp.exp(sink - lse_h[:, :1])
                dsink = dsink + jnp.where(lane1 == h, -jnp.sum(p_sink * delta), 0.0)
                for b in range(2):
                    dsc = pr[b] * (dp[b] - delta)
                    dsc_ref[h, b] += dsc
                    dzb = (dsc * SCALE).astype(BF16)
                    t = _nn(dzb, ks[b][kh][pos])
                    dq = t if dq is None else dq + t
                    add(dk_parts, b, pos, kh, _tn(dzb, qh))
                    add(dv_parts, b, pos, kh, _tn(pr[b].astype(BF16), dh))
            dq_ref[:, g * PAIR:(g + 1) * PAIR] = dq.astype(BF16)
        dsink_ref[...] += dsink

        def fold(parts, b):
            low = parts[b][0][0] + pltpu.roll(parts[b][1][0], HEAD_DIM, 1)
            high = parts[b][1][1] + pltpu.roll(parts[b][0][1], HEAD_DIM, 1)
            return jnp.where(lo, low, high)

        rp = pl.multiple_of(jnp.maximum(i - 1, 0) * QB, QB)
        rc = pl.multiple_of(i * QB, QB)
        dk_acc[pl.ds(rp, QB), :] += fold(dk_parts, 0)
        dv_acc[pl.ds(rp, QB), :] += fold(dv_parts, 0)
        dk_acc[pl.ds(rc, QB), :] += fold(dk_parts, 1)
        dv_acc[pl.ds(rc, QB), :] += fold(dv_parts, 1)

        @pl.when(i == nq - 1)
        def _():
            dk_ref[...] = dk_acc[...].astype(BF16)
            dv_ref[...] = dv_acc[...].astype(BF16)

    kv = lambda col, prev: pl.BlockSpec((QB, KV_W), (lambda i: (jnp.maximum(i - 1, 0), col)) if prev else (lambda i: (i, col)))
    full = pl.BlockSpec((QB, QB), lambda i: (0, 0))
    smem = pl.BlockSpec(memory_space=pltpu.SMEM)
    whole = lambda shape: pl.BlockSpec(shape, lambda i: (0,) * len(shape))
    return _call(
        body, name=name, grid=(nq,),
        out_shape=(S((T, SWA_W), BF16), S((T, KV_W), BF16), S((T, KV_W), BF16), S((1, QB), F32), S((8, 2, QB, QB), F32)),
        in_specs=[pl.BlockSpec((QB, SWA_W), lambda i: (i, 3)), kv(kcol, True), kv(kcol, False), kv(vcol, True), kv(vcol, False),
                  pl.BlockSpec((QB, SWA_W), lambda i: (i, 0)), pl.BlockSpec((QB, 8 * QB), lambda i: (i, 0)),
                  full, full, smem, smem],
        out_specs=(pl.BlockSpec((QB, SWA_W), lambda i: (i, 0)), whole((T, KV_W)), whole((T, KV_W)), whole((1, QB)),
                   whole((8, 2, QB, QB))),
        scratch=[pltpu.VMEM((8, 2, QB, QB), F32), pltpu.VMEM((T, KV_W), F32), pltpu.VMEM((T, KV_W), F32)],
        sem=("arbitrary",), args=(p, p, p, p, p, do, lse, bprev, bcur, sinks, rel_bias), riders=riders)


def mix_out_fwd(o_sb, o_sw, g_sb, g_sw, wout, h, g_next, name, riders=()):
    T, D = h.shape
    M = SB_W + SWA_W
    tm = _tile(T, 256)

    def body(a_ref, b_ref, ga_ref, gb_ref, w_ref, h_ref, gn_ref, mx_ref, o_ref, n_ref):
        mx_ref[:, :SB_W] = _rms(a_ref[...], ga_ref[...]).astype(BF16)
        mx_ref[:, SB_W:] = _rms(b_ref[...], gb_ref[...]).astype(BF16)
        out = h_ref[...] + _nn(mx_ref[...], w_ref[...])
        o_ref[...] = out
        n_ref[...] = _rms(out, gn_ref[...]).astype(BF16)

    row = lambda n: pl.BlockSpec((tm, n), lambda i: (i, 0))
    vec = lambda n: pl.BlockSpec((1, n), lambda i: (0, 0))
    return _call(
        body, name=name, grid=(T // tm,), out_shape=(S((T, M), BF16), S((T, D), F32), S((T, D), BF16)),
        in_specs=[row(SB_W), row(SWA_W), vec(SB_W), vec(SWA_W), pl.BlockSpec((M, D), lambda i: (0, 0)), row(D), vec(D)],
        out_specs=(row(M), row(D), row(D)),
        sem=("parallel",), args=(o_sb, o_sw, g_sb, g_sw, wout, h, g_next), riders=riders)


def loss_head(h, g, target, name):
    T, D = h.shape
    tm = _tile(T, 256)

    def body(h_ref, g_ref, t_ref, loss_ref, dh_ref, dhb_ref, dg_ref):
        @pl.when(pl.program_id(0) == 0)
        def _():
            loss_ref[...] = jnp.zeros_like(loss_ref)
            dg_ref[...] = jnp.zeros_like(dg_ref)
        x = h_ref[...]
        err = _rms(x, g_ref[...]) - t_ref[...]
        loss_ref[...] += jnp.full((1, QB), 0.5 * jnp.sum(jnp.mean(err * err, axis=-1)), F32)
        dx, dg = _rms_bwd(err / D, x, g_ref[...])
        dh_ref[...] = dx
        dhb_ref[...] = dx.astype(BF16)
        dg_ref[...] += dg

    row = pl.BlockSpec((tm, D), lambda i: (i, 0))
    vec = pl.BlockSpec((1, D), lambda i: (0, 0))
    return pl.pallas_call(
        body, name=name, grid=(T // tm,), out_shape=(S((1, QB), F32), S((T, D), F32), S((T, D), BF16), S((1, D), F32)),
        in_specs=[row, vec, row], out_specs=(pl.BlockSpec((1, QB), lambda i: (0, 0)), row, row, vec),
        compiler_params=_params(("arbitrary",)),
    )(h, g, target)


def ffn_down_bwd(dhb, wd, gate, up, name, riders=()):
    T, D = dhb.shape
    F = wd.shape[0]
    tr, tn = _tile(T, 512), _tile(F, 256)

    def body(d_ref, w_ref, g_ref, u_ref, o_ref):
        w = w_ref[...]
        for r in range(T // tr):
            rows = slice(r * tr, (r + 1) * tr)
            da = 0.5 * _nt(d_ref[rows, :], w)
            o_ref[0, rows, :] = (da * g_ref[rows, :].astype(F32)).astype(BF16)
            o_ref[1, rows, :] = (da * u_ref[rows, :].astype(F32)).astype(BF16)

    tile = pl.BlockSpec((T, tn), lambda j: (0, j))
    return _call(
        body, name=name, grid=(F // tn,), out_shape=S((2, T, F), BF16),
        in_specs=[pl.BlockSpec((T, D), lambda j: (0, 0)), pl.BlockSpec((tn, D), lambda j: (j, 0)), tile, tile],
        out_specs=pl.BlockSpec((2, T, tn), lambda j: (0, 0, j)),
        sem=("parallel",), args=(dhb, wd, gate, up), riders=riders)


def tn_matmul(xs, y, alpha, name, riders=()):
    B, T, N = xs.shape
    D = y.shape[1]
    tn = _tile(N, 256)

    def body(x_ref, y_ref, o_ref, ob_ref):
        o = alpha * _tn(x_ref[...], y_ref[...])
        o_ref[...] = o
        ob_ref[...] = o.astype(BF16)

    tile = pl.BlockSpec((None, tn, D), lambda s, j: (s, j, 0))
    return _call(
        body, name=name, grid=(B, N // tn), out_shape=(S((B, N, D), F32), S((B, N, D), BF16)),
        in_specs=[pl.BlockSpec((None, T, tn), lambda s, j: (s, 0, j)), pl.BlockSpec((T, D), lambda s, j: (0, 0))],
        out_specs=(tile, tile), sem=("parallel", "parallel"), args=(xs, y), riders=riders)


def nn_rms_bwd(xs, ws, h_in, g, dh, name, riders=()):
    B, T, K = xs.shape
    D = ws.shape[2]
    tm = _tile(T, 256)

    def body(x_ref, w_ref, h_ref, g_ref, d_ref, o_ref, ob_ref, dg_ref):
        @pl.when(pl.program_id(0) == 0)
        def _():
            dg_ref[...] = jnp.zeros_like(dg_ref)
        dn = _nn(x_ref[0], w_ref[0])
        for s in range(1, B):
            dn = dn + _nn(x_ref[s], w_ref[s])
        dx, dg = _rms_bwd(dn, h_ref[...], g_ref[...])
        out = d_ref[...] + dx
        o_ref[...] = out
        ob_ref[...] = out.astype(BF16)
        dg_ref[...] += dg

    row = pl.BlockSpec((tm, D), lambda i: (i, 0))
    vec = pl.BlockSpec((1, D), lambda i: (0, 0))
    return _call(
        body, name=name, grid=(T // tm,), out_shape=(S((T, D), F32), S((T, D), BF16), S((1, D), F32)),
        in_specs=[pl.BlockSpec((B, tm, K), lambda i: (0, i, 0)), pl.BlockSpec((B, K, D), lambda i: (0, 0, 0)), row, vec, row],
        out_specs=(row, row, vec),
        sem=("arbitrary",), args=(xs, ws, h_in, g, dh), riders=riders)


def mix_out_bwd(dhb, wout, o_sb, o_sw, g_sb, g_sw, name):
    T, D = dhb.shape
    tm = _tile(T, 256)

    def body(d_ref, w_ref, a_ref, b_ref, ga_ref, gb_ref, da_ref, db_ref, dga_ref, dgb_ref):
        @pl.when(pl.program_id(0) == 0)
        def _():
            dga_ref[...] = jnp.zeros_like(dga_ref)
            dgb_ref[...] = jnp.zeros_like(dgb_ref)
        dm = _nt(d_ref[...], w_ref[...])
        dxa, dga = _rms_bwd(dm[:, :SB_W], a_ref[...], ga_ref[...])
        dxb, dgb = _rms_bwd(dm[:, SB_W:], b_ref[...], gb_ref[...])
        da_ref[...] = dxa
        db_ref[...] = dxb
        dga_ref[...] += dga
        dgb_ref[...] += dgb

    row = lambda n: pl.BlockSpec((tm, n), lambda i: (i, 0))
    vec = lambda n: pl.BlockSpec((1, n), lambda i: (0, 0))
    return pl.pallas_call(
        body, name=name, grid=(T // tm,),
        out_shape=(S((T, SB_W), F32), S((T, SWA_W), F32), S((1, SB_W), F32), S((1, SWA_W), F32)),
        in_specs=[row(D), pl.BlockSpec((SB_W + SWA_W, D), lambda i: (0, 0)), row(SB_W), row(SWA_W), vec(SB_W), vec(SWA_W)],
        out_specs=(row(SB_W), row(SWA_W), vec(SB_W), vec(SWA_W)),
        compiler_params=_params(("arbitrary",)),
    )(dhb, wout, o_sb, o_sw, g_sb, g_sw)


def rel_bias_grad(dscs, bprev, bcur, name):
    n = len(dscs)

    def body(*refs):
        bp_ref, bc_ref, o_ref = refs[n], refs[n + 1], refs[n + 2]
        bks = [bp_ref[...], bc_ref[...]]
        row = lax.broadcasted_iota(jnp.int32, (N_BUCKETS, QB), 0)
        lane = lax.broadcasted_iota(jnp.int32, (N_BUCKETS, QB), 1)
        out = jnp.zeros((N_BUCKETS, QB), F32)
        for h in range(8):
            tot = [sum(refs[l][h, b] for l in range(n)) for b in range(2)]
            for b in range(N_BUCKETS):
                val = jnp.sum(jnp.where(bks[0] == b, tot[0], 0.0)) + jnp.sum(jnp.where(bks[1] == b, tot[1], 0.0))
                out = jnp.where((row == b) & (lane == h), val, out)
        o_ref[...] = out

    return pl.pallas_call(body, name=name, out_shape=S((N_BUCKETS, QB), F32), compiler_params=_params())(*dscs, bprev, bcur)


def _adamw(w, g, m, v):
    m = ADAM_B1 * m + (1.0 - ADAM_B1) * g
    v = ADAM_B2 * v + (1.0 - ADAM_B2) * (g * g)
    m_hat = m / (1.0 - ADAM_B1 ** ADAM_STEP)
    v_hat = v / (1.0 - ADAM_B2 ** ADAM_STEP)
    delta = -ADAM_LR * (m_hat / (jnp.sqrt(v_hat) + ADAM_EPS) + ADAM_WD * w)
    return delta, m, v


def adamw_rows(w, g, m, v, name):
    L, R, C = w.shape
    tr = _tile(R, 256)

    def body(w_ref, g_ref, m_ref, v_ref, d_ref, mo_ref, vo_ref):
        d, mn, vn = _adamw(w_ref[...], g_ref[...], m_ref[...], v_ref[...])
        d_ref[...] = d
        mo_ref[...] = mn
        vo_ref[...] = vn

    tile = pl.BlockSpec((None, tr, C), lambda l, i: (l, i, 0))
    return pl.pallas_call(
        body, name=name, grid=(L, R // tr), out_shape=(S((L, R, C), F32),) * 3,
        in_specs=[tile] * 4, out_specs=(tile,) * 3,
        compiler_params=_params(("parallel", "parallel")),
    )(w, g, m, v)


def adamw_small(w, gs, m, v, name):
    R, C = w.shape

    def body(w_ref, g_ref, m_ref, v_ref, go_ref, d_ref, mo_ref, vo_ref):
        g = g_ref[0]
        for k in range(1, N_DEV):
            g = g + g_ref[k]
        d, mn, vn = _adamw(w_ref[...], g, m_ref[...], v_ref[...])
        go_ref[...] = g
        d_ref[...] = d
        mo_ref[...] = mn
        vo_ref[...] = vn

    return pl.pallas_call(body, name=name, out_shape=(S((R, C), F32),) * 4, compiler_params=_params())(w, gs, m, v)


def kernel(x, norm_ffn1, w_ffn1_gu, w_ffn1_down, norm_mix, w_in, sinks, norm_out_sb, norm_out_swa, w_out, norm_ffn2, w_ffn2_gu, w_ffn2_down, rel_bias, norm_final, loss_target, m_norm_ffn1, m_w_ffn1_gu, m_w_ffn1_down, m_norm_mix, m_w_in, m_sinks, m_norm_out_sb, m_norm_out_swa, m_w_out, m_norm_ffn2, m_w_ffn2_gu, m_w_ffn2_down, m_rel_bias, m_norm_final, v_norm_ffn1, v_w_ffn1_gu, v_w_ffn1_down, v_norm_mix, v_w_in, v_sinks, v_norm_out_sb, v_norm_out_swa, v_w_out, v_norm_ffn2, v_w_ffn2_gu, v_w_ffn2_down, v_rel_bias, v_norm_final):
    L = norm_ffn1.shape[0]
    T, D = x.shape[1], x.shape[2]
    F = w_ffn1_down.shape[1] * N_DEV
    h = x.reshape(T, D)
    target = loss_target.reshape(T, D)
    after, upto, before = _tri_consts()
    bprev, bcur = _t5_buckets()

    local = {}
    for l in range(L):
        local[f"gu1_{l}"] = w_ffn1_gu[l].T.astype(BF16)
        local[f"d1_{l}"] = w_ffn1_down[l].astype(BF16)
        local[f"in_{l}"] = w_in[l].T.astype(BF16)
        local[f"out_{l}"] = w_out[l].astype(BF16)
        local[f"gu2_{l}"] = w_ffn2_gu[l].T.astype(BF16)
        local[f"d2_{l}"] = w_ffn2_down[l].astype(BF16)
    half, full = {}, {}
    grads, chip_sum, recv_b = {}, {}, {}

    def run(fn, *args, ag1=(), ag2=(), rs1=(), rs2=()):
        ag1 = [n for n in ag1 if n in local]
        ag2 = [n for n in ag2 if n in half]
        riders = ([gather_first(local[n]) for n in ag1] + [gather_second(half[n]) for n in ag2]
                  + [scatter_first(grads[n][1]) for n in rs1] + [scatter_second(chip_sum[n][1]) for n in rs2])
        if not riders:
            return fn(*args)
        outs, per = fn(*args, riders=riders)
        per = [p[0] for p in per]
        for n in ag1:
            half[n] = per.pop(0)
        for n in ag2:
            buf = per.pop(0)
            full[n] = buf.reshape(N_DEV * buf.shape[1], D)
        for n in rs1:
            chip_sum[n] = scatter_add(grads[n][0], per.pop(0), f"rs_add_{n}")
        for n in rs2:
            recv_b[n] = per.pop(0)
        return outs

    def idle(name, riders=()):
        return None, idle_host(riders, name)

    gu = lambda n: full[n].reshape(2, F, D)
    slots = lambda pair: tuple(t.reshape(N_DEV, -1, D) for t in pair)
    vec = lambda a: a.reshape(1, -1)

    run(idle, "ag_head0", ag1=("gu1_0", "d1_0"))
    run(idle, "ag_head1", ag2=("gu1_0", "d1_0"))
    saved = []
    n_next = rms_cast(h, vec(norm_ffn1[0]), "rms_first")
    for l in range(L):
        nx = l + 1
        s = {"h0": h, "n1": n_next}
        s["gate1"], s["up1"], s["a1"] = run(ffn_up_fwd, s["n1"], gu(f"gu1_{l}"), f"ffn1_up{l}", ag1=(f"in_{l}", f"out_{l}"))
        h = run(ffn_down_fwd, s["a1"], full[f"d1_{l}"], h, None, f"ffn1_down{l}", ag2=(f"in_{l}", f"out_{l}"))
        s["h1"] = h
        s["n2"], s["p"] = mix_in_fwd(h, vec(norm_mix[l]), full[f"in_{l}"], f"mix_in{l}")
        s["o_sb"], s["tot"] = run(sb_attn_fwd, s["p"], after, f"sb_fwd{l}", ag1=(f"gu2_{l}", f"d2_{l}"))
        s["o_sw"], s["lse"] = run(swa_fwd, s["p"], vec(sinks[l]), rel_bias, bprev, bcur, f"swa_fwd{l}",
                                  ag2=(f"gu2_{l}", f"d2_{l}"), ag1=(f"d1_{nx}",))
        s["mixed"], h, s["n3"] = run(mix_out_fwd, s["o_sb"], s["o_sw"], vec(norm_out_sb[l]), vec(norm_out_swa[l]),
                                     full[f"out_{l}"], h, vec(norm_ffn2[l]), f"mix_out{l}", ag2=(f"d1_{nx}",))
        s["h2"] = h
        s["gate2"], s["up2"], s["a2"] = run(ffn_up_fwd, s["n3"], gu(f"gu2_{l}"), f"ffn2_up{l}", ag1=(f"gu1_{nx}",))
        if nx < L:
            h, n_next = run(ffn_down_fwd, s["a2"], full[f"d2_{l}"], h, vec(norm_ffn1[nx]), f"ffn2_down{l}", ag2=(f"gu1_{nx}",))
        else:
            h = run(ffn_down_fwd, s["a2"], full[f"d2_{l}"], h, None, f"ffn2_down{l}")
        saved.append(s)

    loss_part, dh, dhb, dg_final = loss_head(h, vec(norm_final), target, "loss_head")
    loss = lax.psum(loss_part[0, 0], ("x", "y", "c"))

    small = {k: [None] * L for k in ("ffn1", "mix", "sinks", "osb", "osw", "ffn2", "dsc")}
    for l in reversed(range(L)):
        s = saved[l]

        def ffn_bwd(dh, dhb, tag, gate, up, a, n, h_in, g, r_down, r_dwgu, r_up):
            gu_n, d_n = f"gu{tag}_{l}", f"d{tag}_{l}"
            dgu = run(ffn_down_bwd, dhb, full[d_n], gate, up, f"ffn{tag}_down_bwd{l}", **r_down)
            grads[gu_n] = slots(run(tn_matmul, dgu, n, 1.0, f"ffn{tag}_dwgu{l}", **r_dwgu))
            grads[d_n] = slots(run(tn_matmul, a[None], dhb, 0.5, f"ffn{tag}_dwd{l}", rs1=(gu_n,)))
            return run(nn_rms_bwd, dgu, gu(gu_n), h_in, g, dh, f"ffn{tag}_up_bwd{l}", rs1=(d_n,), **r_up)

        carried = (f"d1_{l + 1}",) if l + 1 < L else ()
        dh, dhb, small["ffn2"][l] = ffn_bwd(dh, dhb, 2, s["gate2"], s["up2"], s["a2"], s["n3"], s["h2"], vec(norm_ffn2[l]),
                                            dict(rs2=carried), {}, {})
        do_sb, do_sw, small["osb"][l], small["osw"][l] = mix_out_bwd(
            dhb, full[f"out_{l}"], s["o_sb"], s["o_sw"], vec(norm_out_sb[l]), vec(norm_out_swa[l]), f"mix_out_bwd{l}")
        grads[f"out_{l}"] = slots(tn_matmul(s["mixed"][None], dhb, 1.0, f"dwout{l}"))
        dq_sb, dk_sb, dv_sb = run(sb_attn_bwd, s["p"], do_sb, s["tot"], upto, before, f"sb_bwd{l}",
                                  rs2=(f"gu2_{l}", f"d2_{l}"), rs1=(f"out_{l}",))
        dq_sw, dk_sw, dv_sw, small["sinks"][l], small["dsc"][l] = run(
            swa_bwd, s["p"], do_sw, s["lse"], vec(sinks[l]), rel_bias, bprev, bcur, f"swa_bwd{l}", rs2=(f"out_{l}",))
        dp = jnp.concatenate([dq_sb, dk_sb, dv_sb, dq_sw, dk_sw, dv_sw], axis=1)
        dh, dhb, small["mix"][l] = nn_rms_bwd(dp[None], full[f"in_{l}"][None], s["h1"], vec(norm_mix[l]), dh, f"mix_in_bwd{l}")
        grads[f"in_{l}"] = slots(tn_matmul(dp[None], s["n2"], 1.0, f"dwin{l}"))
        dh, dhb, small["ffn1"][l] = ffn_bwd(dh, dhb, 1, s["gate1"], s["up1"], s["a1"], s["n1"], s["h0"], vec(norm_ffn1[l]),
                                            dict(rs1=(f"in_{l}",)), dict(rs2=(f"in_{l}",)), dict(rs2=(f"gu1_{l}",)))

    grad_x = dh.reshape(x.shape)
    run(idle, "rs_tail", rs2=("d1_0",))

    def scattered(tag, transpose):
        out = []
        for l in range(L):
            n = f"{tag}_{l}"
            mine = scatter_sum(chip_sum[n][0], recv_b[n], f"rs_sum_{n}")
            out.append(mine.T if transpose else mine)
        return jnp.stack(out)

    g_gu1, g_d1, g_in = scattered("gu1", True), scattered("d1", False), scattered("in", True)
    g_out, g_gu2, g_d2 = scattered("out", False), scattered("gu2", True), scattered("d2", False)

    upd = {}
    for nm, w, g, m, v in (("gu1", w_ffn1_gu, g_gu1, m_w_ffn1_gu, v_w_ffn1_gu), ("d1", w_ffn1_down, g_d1, m_w_ffn1_down, v_w_ffn1_down),
                           ("in", w_in, g_in, m_w_in, v_w_in), ("out", w_out, g_out, m_w_out, v_w_out),
                           ("gu2", w_ffn2_gu, g_gu2, m_w_ffn2_gu, v_w_ffn2_gu), ("d2", w_ffn2_down, g_d2, m_w_ffn2_down, v_w_ffn2_down)):
        upd[nm] = (g,) + tuple(adamw_rows(w, g, m, v, f"adamw_{nm}"))

    d_rel = rel_bias_grad(small["dsc"], bprev, bcur, "rel_bias_grad")[:, :8]

    PW = max(D, SB_W + SWA_W)

    def pack(ffn1, mix, ffn2, final, osb, osw, snk, rel):
        wide = lambda a: jnp.pad(a.reshape(-1), (0, PW - a.size))
        rows = [wide(ffn1[l]) for l in range(L)] + [wide(mix[l]) for l in range(L)] + [wide(ffn2[l]) for l in range(L)]
        rows.append(wide(final))
        rows += [wide(jnp.concatenate([osb[l].reshape(-1), osw[l].reshape(-1)])) for l in range(L)]
        rows.append(wide(jnp.concatenate([snk[l].reshape(-1)[:8] for l in range(L)] + [rel.reshape(-1)])))
        arr = jnp.stack(rows)
        return jnp.pad(arr, ((0, (-arr.shape[0]) % 8), (0, 0)))

    def unpack(arr):
        ffn1, mix, ffn2 = arr[0:L, :D], arr[L:2 * L, :D], arr[2 * L:3 * L, :D]
        final = arr[3 * L, :D]
        ob = arr[3 * L + 1:4 * L + 1]
        tail = arr[4 * L + 1]
        return (ffn1, mix, tail[:8 * L].reshape(L, 8), ob[:, :SB_W], ob[:, SB_W:SB_W + SWA_W], ffn2,
                tail[8 * L:8 * L + N_BUCKETS * 8].reshape(N_BUCKETS, 8), final)

    g_small = pack(small["ffn1"], small["mix"], small["ffn2"], dg_final, small["osb"], small["osw"], small["sinks"], d_rel)
    w_small = pack(norm_ffn1, norm_mix, norm_ffn2, norm_final, norm_out_sb, norm_out_swa, sinks, rel_bias)
    m_small = pack(m_norm_ffn1, m_norm_mix, m_norm_ffn2, m_norm_final, m_norm_out_sb, m_norm_out_swa, m_sinks, m_rel_bias)
    v_small = pack(v_norm_ffn1, v_norm_mix, v_norm_ffn2, v_norm_final, v_norm_out_sb, v_norm_out_swa, v_sinks, v_rel_bias)
    gs_small = all_gather_rows(g_small, "ag_small")
    small_out = [unpack(a) for a in adamw_small(w_small, gs_small, m_small, v_small, "adamw_small")]

    def group(k):
        sm = small_out[k]
        return (sm[0], upd["gu1"][k], upd["d1"][k], sm[1], upd["in"][k], sm[2], sm[3], sm[4], upd["out"][k], sm[5],
                upd["gu2"][k], upd["d2"][k], sm[6], sm[7])

    return (loss, grad_x, *group(0), *group(1), *group(2), *group(3))
```

```python
import math

import jax
import jax.numpy as jnp
from jax import lax
from jax.experimental import pallas as pl
from jax.experimental.pallas import tpu as pltpu

F32 = jnp.float32
BF16 = jnp.bfloat16
S = jax.ShapeDtypeStruct

N_DEV = 8
HEAD_DIM = 64
SB_HEADS = 8
PAIR = 2 * HEAD_DIM
SB_W = 512
SWA_W = 512
KV_W = 128
IN_W = 3 * SB_W + SWA_W + 2 * KV_W
QB = 128
N_BUCKETS = 32
MAX_DISTANCE = 128
EPS = 1e-6
NEG_INF = -1e30
SCALE = HEAD_DIM ** -0.5

ADAM_LR = 0.001
ADAM_B1 = 0.9
ADAM_B2 = 0.999
ADAM_EPS = 1e-08
ADAM_WD = 0.01
ADAM_STEP = 10

VMEM_LIMIT = 56 * 1024 * 1024
MESH = pl.DeviceIdType.MESH


def _params(sem=None, vmem=VMEM_LIMIT):
    return pltpu.CompilerParams(dimension_semantics=sem, vmem_limit_bytes=vmem)


def _nn(a, b):
    return jnp.dot(a, b, preferred_element_type=F32)


def _nt(a, b):
    return lax.dot_general(a, b, (((1,), (1,)), ((), ())), preferred_element_type=F32)


def _tn(a, b):
    return lax.dot_general(a, b, (((0,), (0,)), ((), ())), preferred_element_type=F32)


def _tri(x, m2):
    hi = x.astype(BF16)
    lo = (x - hi.astype(F32)).astype(BF16)
    return _nn(jnp.concatenate([hi, lo], axis=1), m2)


def _rms(x, g):
    r = lax.rsqrt(jnp.mean(x * x, axis=-1, keepdims=True) + EPS)
    return x * r * g


def _rms_bwd(dy, x, g):
    r = lax.rsqrt(jnp.mean(x * x, axis=-1, keepdims=True) + EPS)
    xhat = x * r
    u = dy * g
    dx = r * (u - xhat * jnp.mean(u * xhat, axis=-1, keepdims=True))
    return dx, jnp.sum(dy * xhat, axis=0, keepdims=True)


def _softplus_logsig(z):
    sp = jnp.maximum(z, 0.0) + jnp.log(1.0 + jnp.exp(-jnp.abs(z)))
    return sp, z - sp


def _tile(n, want):
    t = min(n, want)
    while n % t:
        t //= 2
    return t


def _place():
    x, y, c = lax.axis_index("x"), lax.axis_index("y"), lax.axis_index("c")
    chips = [(1 - x, y), (x, 1 - y), (1 - x, 1 - y)]
    return x, y, c, chips


def all_gather_rows(v, name):
    R, C = v.shape

    def body(v_ref, out_ref, send_sems, recv_sems, local_sem):
        x, y, c, chips = _place()
        me, sibling = (x, y, c), (x, y, 1 - c)

        def slot(px, py, pc):
            return out_ref.at[4 * px + 2 * py + pc]

        def copy(k, block, to, src=None):
            return pltpu.make_async_remote_copy(
                src_ref=slot(*block) if src is None else src, dst_ref=slot(*block),
                send_sem=send_sems.at[k], recv_sem=recv_sems.at[k], device_id=to, device_id_type=MESH)

        mine = pltpu.make_async_copy(v_ref, slot(*me), local_sem)
        mine.start()
        first = [copy(0, me, sibling, src=v_ref)]
        first += [copy(1 + j, me, (*chip, c), src=v_ref) for j, chip in enumerate(chips)]
        for cp in first:
            cp.start()
        passed = [copy(4 + j, (*chip, c), sibling) for j, chip in enumerate(chips)]
        for j, chip in enumerate(chips):
            copy(1 + j, (*chip, c), me).wait_recv()
            passed[j].start()
        copy(0, sibling, me).wait_recv()
        for j, chip in enumerate(chips):
            copy(4 + j, (*chip, 1 - c), me).wait_recv()
        for cp in first + passed:
            cp.wait_send()
        mine.wait()

    return pl.pallas_call(
        body, name=name, out_shape=S((N_DEV, R, C), v.dtype),
        in_specs=[pl.BlockSpec(memory_space=pl.ANY)], out_specs=pl.BlockSpec(memory_space=pl.ANY),
        scratch_shapes=[pltpu.SemaphoreType.DMA((7,)), pltpu.SemaphoreType.DMA((7,)), pltpu.SemaphoreType.DMA],
    )(v)


class _Exchange:
    def __init__(self, ins, outs, n_first, n_second, n_local, plan, aliases=None):
        self.ins, self.outs, self.plan, self.aliases = list(ins), list(outs), plan, aliases or {}
        self.n_first, self.n_second, self.n_local = n_first, n_second, n_local

    def scratch(self):
        n = self.n_first + self.n_second
        return [pltpu.SemaphoreType.DMA((n,)), pltpu.SemaphoreType.DMA((n,)), pltpu.SemaphoreType.DMA((max(self.n_local, 1),))]

    def _copies(self, in_refs, out_refs, sems):
        send_sems, recv_sems, local_sems = sems
        first, second, local = self.plan(in_refs, out_refs)
        rem = [pltpu.make_async_remote_copy(src_ref=s, dst_ref=d, send_sem=send_sems.at[k], recv_sem=recv_sems.at[k],
                                            device_id=dev, device_id_type=MESH) for k, (s, d, dev) in enumerate(first + second)]
        loc = [pltpu.make_async_copy(s, d, local_sems.at[k]) for k, (s, d) in enumerate(local)]
        return rem[:len(first)], rem[len(first):], loc

    def start(self, in_refs, out_refs, sems):
        first, _, loc = self._copies(in_refs, out_refs, sems)
        for cp in first + loc:
            cp.start()

    def middle(self, in_refs, out_refs, sems):
        first, second, _ = self._copies(in_refs, out_refs, sems)
        if second:
            for cp in first:
                cp.wait_recv()
            for cp in second:
                cp.start()

    def finish(self, in_refs, out_refs, sems):
        first, second, loc = self._copies(in_refs, out_refs, sems)
        for cp in second if second else first:
            cp.wait_recv()
        for cp in first + second:
            cp.wait_send()
        for cp in loc:
            cp.wait()


def gather(v):
    R, C = v.shape

    def plan(ins, outs):
        x, y, c, chips = _place()
        slot = lambda px, py, pc: outs[0].at[4 * px + 2 * py + pc]
        mine = slot(x, y, c)
        first = [(ins[0], mine, (x, y, 1 - c))] + [(ins[0], mine, (*chip, c)) for chip in chips]
        second = [(slot(*chip, c), slot(*chip, c), (x, y, 1 - c)) for chip in chips]
        return first, second, [(ins[0], mine)]

    return _Exchange([v], [S((N_DEV, R, C), v.dtype)], 4, 3, 1, plan)


def scatter_first(gb):
    _, R, C = gb.shape

    def plan(ins, outs):
        x, y, c, chips = _place()
        owners = [(x, y)] + chips
        return [(ins[0].at[4 * px + 2 * py + (1 - c)], outs[0].at[j], (x, y, 1 - c)) for j, (px, py) in enumerate(owners)], [], []

    return _Exchange([gb], [S((4, R, C), BF16)], 4, 0, 0, plan)


def scatter_second(sb, rows=None, into=None):
    r0, nr = rows or (0, sb.shape[1])

    def plan(ins, outs):
        x, y, c, chips = _place()
        part = lambda ref, j: ref.at[j, pl.ds(r0, nr), :]
        return [(part(ins[0], j), part(outs[0], j), (*chips[j], c)) for j in range(3)], [], []

    if into is None:
        return _Exchange([sb], [S(sb.shape, BF16)], 3, 0, 0, plan)
    return _Exchange([sb, into], [S(sb.shape, BF16)], 3, 0, 0, plan, aliases={1: 0})


def _call(body, *, name, grid, in_specs, out_specs, out_shape, args, scratch=(), sem=None, riders=()):
    single = not isinstance(out_shape, (tuple, list))
    out_shape = (out_shape,) if single else tuple(out_shape)
    out_specs = (out_specs,) if single else tuple(out_specs)
    n_in, n_out, n_sc = len(in_specs), len(out_shape), len(scratch)
    if not riders:
        res = pl.pallas_call(body, name=name, grid=grid, in_specs=list(in_specs), out_specs=out_specs, out_shape=out_shape,
                             scratch_shapes=list(scratch), compiler_params=_params(sem))(*args)
        return res[0] if single else res
    r_ins = [a for r in riders for a in r.ins]
    r_outs = [o for r in riders for o in r.outs]
    r_scr = [s for r in riders for s in r.scratch()]
    aliases, i0, o0 = {}, n_in, n_out
    for r in riders:
        for a, b in r.aliases.items():
            aliases[i0 + a] = o0 + b
        i0, o0 = i0 + len(r.ins), o0 + len(r.outs)
    steps = math.prod(grid)

    def full(*refs):
        ins, rin = refs[:n_in], refs[n_in:n_in + len(r_ins)]
        pos = n_in + len(r_ins)
        outs, rout = refs[pos:pos + n_out], refs[pos + n_out:pos + n_out + len(r_outs)]
        pos += n_out + len(r_outs)
        sc, rsc = refs[pos:pos + n_sc], refs[pos + n_sc:]
        step = 0
        for d, n in enumerate(grid):
            step = step * n + pl.program_id(d)

        def each(method):
            i, o = 0, 0
            for k, r in enumerate(riders):
                getattr(r, method)(rin[i:i + len(r.ins)], rout[o:o + len(r.outs)], rsc[3 * k:3 * k + 3])
                i, o = i + len(r.ins), o + len(r.outs)

        @pl.when(step == 0)
        def _():
            each("start")
        body(*ins, *outs, *sc)

        @pl.when(step == (3 * steps) // 4)
        def _():
            each("middle")

        @pl.when(step == steps - 1)
        def _():
            each("finish")

    anywhere = pl.BlockSpec(memory_space=pl.ANY)
    res = pl.pallas_call(
        full, name=name, grid=grid, in_specs=list(in_specs) + [anywhere] * len(r_ins),
        out_specs=out_specs + (anywhere,) * len(r_outs), out_shape=out_shape + tuple(r_outs),
        scratch_shapes=list(scratch) + r_scr, input_output_aliases=aliases,
        compiler_params=_params(("arbitrary",) * len(grid)))(*args, *r_ins)
    host, rest, per = res[:n_out], list(res[n_out:]), []
    for r in riders:
        per.append(rest[:len(r.outs)])
        rest = rest[len(r.outs):]
    return (host[0] if single else tuple(host)), per


def idle_host(riders, name):
    def body(o_ref):
        o_ref[...] = jnp.zeros_like(o_ref)

    return _call(body, name=name, grid=(1,), in_specs=[], out_specs=pl.BlockSpec((8, QB), lambda i: (0, 0)),
                 out_shape=S((8, QB), F32), args=(), riders=riders)[1]


def _rows_tile(n, cap):
    return max(t for t in range(16, min(n, cap) + 1, 16) if n % t == 0)


def scatter_add(g, ra, name):
    _, R, C = g.shape
    tr = _rows_tile(R, 176)
    x, y, c, chips = _place()
    slots = jnp.stack([4 * px + 2 * py + c for px, py in [(x, y)] + chips]).astype(jnp.int32)

    def body(s_ref, g0, g1, g2, g3, ra_ref, own_ref, sb_ref):
        own_ref[...] = g0[...] + ra_ref[0].astype(F32)
        for j, gj in enumerate((g1, g2, g3)):
            sb_ref[j] = (gj[...] + ra_ref[j + 1].astype(F32)).astype(BF16)

    spec = pltpu.PrefetchScalarGridSpec(
        num_scalar_prefetch=1, grid=(R // tr,),
        in_specs=[pl.BlockSpec((None, tr, C), lambda i, s, j=j: (s[j], i, 0)) for j in range(4)]
        + [pl.BlockSpec((4, tr, C), lambda i, s: (0, i, 0))],
        out_specs=(pl.BlockSpec((tr, C), lambda i, s: (i, 0)), pl.BlockSpec((3, tr, C), lambda i, s: (0, i, 0))))
    return pl.pallas_call(body, name=name, grid_spec=spec, out_shape=(S((R, C), F32), S((3, R, C), BF16)),
                          compiler_params=_params(("parallel",)))(slots, g, g, g, g, ra)


def scatter_sum(own, rb, name):
    R, C = own.shape
    tr = _rows_tile(R, 176)

    def body(o_ref, r_ref, g_ref):
        g_ref[...] = o_ref[...] + r_ref[0].astype(F32) + r_ref[1].astype(F32) + r_ref[2].astype(F32)

    return _call(body, name=name, grid=(R // tr,), in_specs=[pl.BlockSpec((tr, C), lambda i: (i, 0)),
                                                              pl.BlockSpec((3, tr, C), lambda i: (0, i, 0))],
                 out_specs=pl.BlockSpec((tr, C), lambda i: (i, 0)), out_shape=S((R, C), F32), args=(own, rb), sem=("parallel",))


def rms_cast(h, g, name):
    T, D = h.shape
    tm = _tile(T, 512)

    def body(h_ref, g_ref, n_ref):
        n_ref[...] = _rms(h_ref[...], g_ref[...]).astype(BF16)

    row = pl.BlockSpec((tm, D), lambda i: (i, 0))
    return _call(body, name=name, grid=(T // tm,), out_shape=S((T, D), BF16), in_specs=[row, pl.BlockSpec((1, D), lambda i: (0, 0))],
                 out_specs=row, sem=("parallel",), args=(h, g))


def ffn_up_fwd(n, wgu, name, riders=()):
    T, D = n.shape
    F = wgu.shape[1]
    tr, tn = _tile(T, 512), _tile(F, 256)

    def body(n_ref, wg_ref, wu_ref, dgate_ref, dup_ref, a_ref):
        wg, wu = wg_ref[...], wu_ref[...]
        for r in range(T // tr):
            rows = slice(r * tr, (r + 1) * tr)
            x = n_ref[rows, :]
            gate = _nt(x, wg)
            up = _nt(x, wu)
            s = jax.nn.sigmoid(gate)
            silu = gate * s
            dgate_ref[rows, :] = (up * (s * (1.0 + gate * (1.0 - s)))).astype(BF16)
            dup_ref[rows, :] = silu.astype(BF16)
            a_ref[rows, :] = (silu * up).astype(BF16)

    tile = pl.BlockSpec((T, tn), lambda j: (0, j))
    return _call(
        body, name=name, grid=(F // tn,), out_shape=(S((T, F), BF16),) * 3,
        in_specs=[pl.BlockSpec((T, D), lambda j: (0, 0)),
                  pl.BlockSpec((None, tn, D), lambda j: (0, j, 0)), pl.BlockSpec((None, tn, D), lambda j: (1, j, 0))],
        out_specs=(tile, tile, tile), sem=("parallel",), args=(n, wgu, wgu), riders=riders)


def ffn_down_fwd(a, wd, h, g_next, name, riders=()):
    T, F = a.shape
    D = wd.shape[1]
    tm = _tile(T, 256)

    def body(a_ref, w_ref, h_ref, *rest):
        out = h_ref[...] + 0.5 * _nn(a_ref[...], w_ref[...])
        if g_next is None:
            rest[0][...] = out
        else:
            g_ref, o_ref, n_ref = rest
            o_ref[...] = out
            n_ref[...] = _rms(out, g_ref[...]).astype(BF16)

    row = pl.BlockSpec((tm, D), lambda i: (i, 0))
    more = g_next is not None
    return _call(
        body, name=name, grid=(T // tm,), out_shape=(S((T, D), F32), S((T, D), BF16)) if more else S((T, D), F32),
        in_specs=[pl.BlockSpec((tm, F), lambda i: (i, 0)), pl.BlockSpec((F, D), lambda i: (0, 0)), row]
        + ([pl.BlockSpec((1, D), lambda i: (0, 0))] if more else []),
        out_specs=(row, row) if more else row,
        sem=("parallel",), args=(a, wd, h) + ((g_next,) if more else ()), riders=riders)


def mix_in_fwd(h, g, win, name):
    T, D = h.shape
    N = win.shape[0]
    tm = _tile(T, 256)

    def body(h_ref, g_ref, w_ref, n_ref, p_ref):
        n = _rms(h_ref[...], g_ref[...]).astype(BF16)
        n_ref[...] = n
        p_ref[...] = _nt(n, w_ref[...]).astype(BF16)

    return pl.pallas_call(
        body, name=name, grid=(T // tm,), out_shape=(S((T, D), BF16), S((T, N), BF16)),
        in_specs=[pl.BlockSpec((tm, D), lambda i: (i, 0)), pl.BlockSpec((1, D), lambda i: (0, 0)),
                  pl.BlockSpec((N, D), lambda i: (0, 0))],
        out_specs=(pl.BlockSpec((tm, D), lambda i: (i, 0)), pl.BlockSpec((tm, N), lambda i: (i, 0))),
        compiler_params=_params(("parallel",)),
    )(h, g, win)


def _tri_consts():
    r = lax.broadcasted_iota(jnp.int32, (QB, QB), 0)
    c = lax.broadcasted_iota(jnp.int32, (QB, QB), 1)
    ones = jnp.ones((QB, QB), BF16)

    def stacked(tri):
        m = jnp.concatenate([tri.astype(BF16), ones], axis=1)
        return jnp.concatenate([m, m], axis=0)

    return stacked(r > c), stacked(r <= c), stacked(r < c)


def _half_masks():
    lane = lax.broadcasted_iota(jnp.int32, (QB, PAIR), 1)
    row = lax.broadcasted_iota(jnp.int32, (QB, PAIR), 0)
    return lane < HEAD_DIM, lane, row


def sb_attn_fwd(p, after, name, riders=()):
    T = p.shape[0]
    nq = T // QB

    def body(q_ref, k_ref, v_ref, m_ref, o_ref, tot_ref, q_sc, acc_ref):
        i = pl.program_id(0)
        lo, lane, row = _half_masks()
        causal = lane < row
        for hp in range(SB_HEADS // 2):
            q2 = q_ref[:, hp * PAIR:(hp + 1) * PAIR].astype(F32) * SCALE
            q_sc[2 * hp] = jnp.where(lo, q2, 0.0).astype(BF16)
            q_sc[2 * hp + 1] = jnp.where(lo, 0.0, q2).astype(BF16)
        m2 = m_ref[...]

        def block(j, diag):
            r0 = pl.multiple_of(j * QB, QB)
            heads = range(SB_HEADS)
            k2 = [k_ref[pl.ds(r0, QB), hp * PAIR:(hp + 1) * PAIR] for hp in range(SB_HEADS // 2)]
            v2 = [v_ref[pl.ds(r0, QB), hp * PAIR:(hp + 1) * PAIR] for hp in range(SB_HEADS // 2)]
            z = [_nt(q_sc[h], k2[h // 2]) for h in heads]
            spls = [_softplus_logsig(z[h]) for h in heads]
            sp = [jnp.where(causal, spls[h][0], 0.0) if diag else spls[h][0] for h in heads]
            rr = [_tri(sp[h], m2) for h in heads]
            if diag:
                w = [jnp.where(causal, jnp.exp(spls[h][1] - rr[h][:, :QB]), 0.0) for h in heads]
                pv = [_nn(w[h].astype(BF16), v2[h // 2]) for h in heads]
                for h in heads:
                    acc_ref[h] = pv[h]
                    tot_ref[:, h * QB:(h + 1) * QB] = rr[h][:, QB:]
            else:
                c = [tot_ref[:, h * QB:(h + 1) * QB] for h in heads]
                w = [jnp.exp(spls[h][1] - (c[h] + rr[h][:, :QB])) for h in heads]
                pv = [_nn(w[h].astype(BF16), v2[h // 2]) for h in heads]
                for h in heads:
                    acc_ref[h] += pv[h]
                    tot_ref[:, h * QB:(h + 1) * QB] = c[h] + rr[h][:, QB:]

        block(i, True)

        def step(t, carry):
            block(i - 1 - t, False)
            return carry
        lax.fori_loop(0, i, step, 0)
        for hp in range(SB_HEADS // 2):
            o_ref[:, hp * PAIR:(hp + 1) * PAIR] = jnp.where(lo, acc_ref[2 * hp], acc_ref[2 * hp + 1])

    return _call(
        body, name=name, grid=(nq,), out_shape=(S((T, SB_W), F32), S((T, SB_HEADS * QB), F32)),
        in_specs=[pl.BlockSpec((QB, SB_W), lambda i: (i, 0)), pl.BlockSpec((T, SB_W), lambda i: (0, 1)),
                  pl.BlockSpec((T, SB_W), lambda i: (0, 2)), pl.BlockSpec((2 * QB, 2 * QB), lambda i: (0, 0))],
        out_specs=(pl.BlockSpec((QB, SB_W), lambda i: (i, 0)), pl.BlockSpec((QB, SB_HEADS * QB), lambda i: (i, 0))),
        scratch=[pltpu.VMEM((SB_HEADS, QB, PAIR), BF16), pltpu.VMEM((SB_HEADS, QB, PAIR), F32)],
        sem=("arbitrary",), args=(p, p, p, after), riders=riders)


def sb_attn_bwd(p, do, tot, upto, before, name, riders=()):
    T = p.shape[0]
    nq = T // QB

    def body(q_ref, k_ref, v_ref, do_ref, tot_ref, mp_ref, mg_ref, dq_ref, dk_ref, dv_ref,
             q_sc, d_sc, pg_sc, dq_acc, dk_acc, dv_acc):
        i = pl.program_id(0)
        lo, lane, row = _half_masks()
        causal = lane < row
        for hp in range(SB_HEADS // 2):
            q2 = q_ref[:, hp * PAIR:(hp + 1) * PAIR].astype(F32) * SCALE
            d2 = do_ref[:, hp * PAIR:(hp + 1) * PAIR]
            q_sc[2 * hp] = jnp.where(lo, q2, 0.0).astype(BF16)
            q_sc[2 * hp + 1] = jnp.where(lo, 0.0, q2).astype(BF16)
            d_sc[2 * hp] = jnp.where(lo, d2, 0.0).astype(BF16)
            d_sc[2 * hp + 1] = jnp.where(lo, 0.0, d2).astype(BF16)
        mp, mg = mp_ref[...], mg_ref[...]

        @pl.when(i == 0)
        def _():
            dk_acc[...] = jnp.zeros_like(dk_acc)
            dv_acc[...] = jnp.zeros_like(dv_acc)
        pg_sc[...] = jnp.zeros_like(pg_sc)
        dq_acc[...] = jnp.zeros_like(dq_acc)

        def block(j, diag):
            r0 = pl.multiple_of(j * QB, QB)
            heads = range(SB_HEADS)
            pairs = range(SB_HEADS // 2)
            k2 = [k_ref[pl.ds(r0, QB), hp * PAIR:(hp + 1) * PAIR] for hp in pairs]
            v2 = [v_ref[pl.ds(r0, QB), hp * PAIR:(hp + 1) * PAIR] for hp in pairs]
            z = [_nt(q_sc[h], k2[h // 2]) for h in heads]
            dw = [_nt(d_sc[h], v2[h // 2]) for h in heads]
            spls = [_softplus_logsig(z[h]) for h in heads]
            sp = [jnp.where(causal, spls[h][0], 0.0) if diag else spls[h][0] for h in heads]
            rr = [_tri(sp[h], mp) for h in heads]
            pc = [pg_sc[2 * h] for h in heads]
            w = [jnp.exp(spls[h][1] - (tot_ref[:, h * QB:(h + 1) * QB] - (pc[h] + rr[h][:, :QB]))) for h in heads]
            if diag:
                w = [jnp.where(causal, w[h], 0.0) for h in heads]
            gg = [dw[h] * w[h] for h in heads]
            rg = [_tri(gg[h], mg) for h in heads]
            gc = [pg_sc[2 * h + 1] for h in heads]
            dz = [gg[h] - (gg[h] + gc[h] + rg[h][:, :QB]) * jnp.exp(spls[h][1]) for h in heads]
            if diag:
                dz = [jnp.where(causal, dz[h], 0.0) for h in heads]
            dzb = [dz[h].astype(BF16) for h in heads]
            wb = [w[h].astype(BF16) for h in heads]
            dq = [_nn(dzb[h], k2[h // 2]) for h in heads]
            dk = [_tn(dzb[2 * hp], q_sc[2 * hp]) + _tn(dzb[2 * hp + 1], q_sc[2 * hp + 1]) for hp in pairs]
            dv = [_tn(wb[2 * hp], d_sc[2 * hp]) + _tn(wb[2 * hp + 1], d_sc[2 * hp + 1]) for hp in pairs]
            for h in heads:
                dq_acc[h] += dq[h]
                if not diag:
                    pg_sc[2 * h] = pc[h] + rr[h][:, QB:]
                    pg_sc[2 * h + 1] = gc[h] + rg[h][:, QB:]
            for hp in pairs:
                dk_acc[pl.ds(r0, QB), hp * PAIR:(hp + 1) * PAIR] += dk[hp]
                dv_acc[pl.ds(r0, QB), hp * PAIR:(hp + 1) * PAIR] += dv[hp]

        def step(t, carry):
            block(t, False)
            return carry
        lax.fori_loop(0, i, step, 0)
        block(i, True)
        for hp in range(SB_HEADS // 2):
            dq = jnp.where(lo, dq_acc[2 * hp], dq_acc[2 * hp + 1]) * SCALE
            dq_ref[:, hp * PAIR:(hp + 1) * PAIR] = dq.astype(BF16)

        @pl.when(i == nq - 1)
        def _():
            dk_ref[...] = dk_acc[...].astype(BF16)
            dv_ref[...] = dv_acc[...].astype(BF16)

    qtile = pl.BlockSpec((QB, SB_W), lambda i: (i, 0))
    whole = pl.BlockSpec((T, SB_W), lambda i: (0, 0))
    const = pl.BlockSpec((2 * QB, 2 * QB), lambda i: (0, 0))
    return _call(
        body, name=name, grid=(nq,), out_shape=(S((T, SB_W), BF16),) * 3,
        in_specs=[qtile, pl.BlockSpec((T, SB_W), lambda i: (0, 1)), pl.BlockSpec((T, SB_W), lambda i: (0, 2)), qtile,
                  pl.BlockSpec((QB, SB_HEADS * QB), lambda i: (i, 0)), const, const],
        out_specs=(qtile, whole, whole),
        scratch=[pltpu.VMEM((SB_HEADS, QB, PAIR), BF16), pltpu.VMEM((SB_HEADS, QB, PAIR), BF16),
                 pltpu.VMEM((2 * SB_HEADS, QB, QB), F32), pltpu.VMEM((SB_HEADS, QB, PAIR), F32),
                 pltpu.VMEM((T, SB_W), F32), pltpu.VMEM((T, SB_W), F32)],
        sem=("arbitrary",), args=(p, p, p, do, tot, upto, before), riders=riders)


def _t5_buckets():
    a = lax.broadcasted_iota(jnp.int32, (QB, QB), 0)
    c = lax.broadcasted_iota(jnp.int32, (QB, QB), 1)

    def bucket(dist):
        dist = jnp.maximum(dist, 0)
        max_exact = N_BUCKETS // 2
        d = jnp.maximum(dist, 1).astype(F32)
        large = max_exact + (jnp.log(d / max_exact) / math.log(MAX_DISTANCE / max_exact)
                             * (N_BUCKETS - max_exact)).astype(jnp.int32)
        large = jnp.minimum(large, N_BUCKETS - 1)
        return jnp.where(dist < max_exact, dist, large)

    return bucket(QB + a - c), bucket(a - c)


def _swa_common(i, kp_ref, kc_ref, vp_ref, vc_ref, bp_ref, bc_ref, rb_ref, bias_ref):
    lo, lane, row = _half_masks()

    @pl.when(i == 0)
    def _():
        for blk, b_ref in enumerate((bp_ref, bc_ref)):
            bk = b_ref[...]
            for h in range(8):
                acc = jnp.zeros((QB, QB), F32)
                for b in range(N_BUCKETS):
                    acc = jnp.where(bk == b, rb_ref[b, h], acc)
                bias_ref[h, blk] = acc

    band = [(lane > row) & (i > 0), lane <= row]

    def halves(ref):
        t = ref[...].astype(F32)
        sw = pltpu.roll(t, HEAD_DIM, 1)
        return [[jnp.where(lo, t, 0.0).astype(BF16), jnp.where(lo, 0.0, sw).astype(BF16)],
                [jnp.where(lo, sw, 0.0).astype(BF16), jnp.where(lo, 0.0, t).astype(BF16)]]

    ks = [halves(kp_ref), halves(kc_ref)]
    vs = [halves(vp_ref), halves(vc_ref)]
    return lo, band, ks, vs


def swa_fwd(p, sinks, rel_bias, bprev, bcur, name, riders=()):
    T = p.shape[0]
    nq = T // QB
    kcol, vcol = (3 * SB_W + SWA_W) // KV_W, (3 * SB_W + SWA_W) // KV_W + 1

    def body(q_ref, kp_ref, kc_ref, vp_ref, vc_ref, bp_ref, bc_ref, sink_ref, rb_ref, o_ref, lse_ref, bias_ref):
        i = pl.program_id(0)
        lo, band, ks, vs = _swa_common(i, kp_ref, kc_ref, vp_ref, vc_ref, bp_ref, bc_ref, rb_ref, bias_ref)
        for g in range(4):
            kh = g // 2
            q2 = q_ref[:, g * PAIR:(g + 1) * PAIR]
            outs = []
            for pos in range(2):
                h = 2 * g + pos
                sc = [jnp.where(band[b], _nt(q2, ks[b][kh][pos]) * SCALE + bias_ref[h, b], NEG_INF) for b in range(2)]
                sink = sink_ref[0, h]
                m = jnp.maximum(jnp.maximum(jnp.max(sc[0], axis=1, keepdims=True),
                                            jnp.max(sc[1], axis=1, keepdims=True)), sink)
                e = [jnp.exp(sc[b] - m) for b in range(2)]
                den = jnp.sum(e[0], axis=1, keepdims=True) + jnp.sum(e[1], axis=1, keepdims=True) + jnp.exp(sink - m)
                outs.append(_nn((e[0] / den).astype(BF16), vs[0][kh][pos]) + _nn((e[1] / den).astype(BF16), vs[1][kh][pos]))
                lse_ref[:, h * QB:(h + 1) * QB] = jnp.broadcast_to(m + jnp.log(den), (QB, QB))
            o_ref[:, g * PAIR:(g + 1) * PAIR] = outs[0] + outs[1]

    kv = lambda col, prev: pl.BlockSpec((QB, KV_W), (lambda i: (jnp.maximum(i - 1, 0), col)) if prev else (lambda i: (i, col)))
    full = pl.BlockSpec((QB, QB), lambda i: (0, 0))
    smem = pl.BlockSpec(memory_space=pltpu.SMEM)
    return _call(
        body, name=name, grid=(nq,), out_shape=(S((T, SWA_W), F32), S((T, 8 * QB), F32)),
        in_specs=[pl.BlockSpec((QB, SWA_W), lambda i: (i, 3)), kv(kcol, True), kv(kcol, False), kv(vcol, True), kv(vcol, False),
                  full, full, smem, smem],
        out_specs=(pl.BlockSpec((QB, SWA_W), lambda i: (i, 0)), pl.BlockSpec((QB, 8 * QB), lambda i: (i, 0))),
        scratch=[pltpu.VMEM((8, 2, QB, QB), F32)],
        sem=("arbitrary",), args=(p, p, p, p, p, bprev, bcur, sinks, rel_bias), riders=riders)


def swa_bwd(p, do, lse, sinks, rel_bias, bprev, bcur, name, riders=()):
    T = p.shape[0]
    nq = T // QB
    kcol, vcol = (3 * SB_W + SWA_W) // KV_W, (3 * SB_W + SWA_W) // KV_W + 1

    def body(q_ref, kp_ref, kc_ref, vp_ref, vc_ref, do_ref, lse_ref, bp_ref, bc_ref, sink_ref, rb_ref,
             dq_ref, dk_ref, dv_ref, dsink_ref, dsc_ref, bias_ref, dk_acc, dv_acc):
        i = pl.program_id(0)
        lo, band, ks, vs = _swa_common(i, kp_ref, kc_ref, vp_ref, vc_ref, bp_ref, bc_ref, rb_ref, bias_ref)

        @pl.when(i == 0)
        def _():
            dk_acc[...] = jnp.zeros_like(dk_acc)
            dv_acc[...] = jnp.zeros_like(dv_acc)
            dsc_ref[...] = jnp.zeros_like(dsc_ref)
            dsink_ref[...] = jnp.zeros_like(dsink_ref)

        lane1 = lax.broadcasted_iota(jnp.int32, (1, QB), 1)
        dsink = jnp.zeros((1, QB), F32)
        dk_parts = [[[None, None], [None, None]], [[None, None], [None, None]]]
        dv_parts = [[[None, None], [None, None]], [[None, None], [None, None]]]

        def add(parts, b, pos, kh, val):
            parts[b][pos][kh] = val if parts[b][pos][kh] is None else parts[b][pos][kh] + val

        for g in range(4):
            kh = g // 2
            q2 = q_ref[:, g * PAIR:(g + 1) * PAIR]
            q2f = q2.astype(F32)
            d2f = do_ref[:, g * PAIR:(g + 1) * PAIR]
            d2 = d2f.astype(BF16)
            dq = None
            for pos in range(2):
                h = 2 * g + pos
                keep = lo if pos == 0 else ~lo
                qh = jnp.where(keep, q2f, 0.0).astype(BF16)
                dh = jnp.where(keep, d2f, 0.0).astype(BF16)
                lse_h = lse_ref[:, h * QB:(h + 1) * QB]
                sink = sink_ref[0, h]
                pr = [jnp.exp(jnp.where(band[b], _nt(q2, ks[b][kh][pos]) * SCALE + bias_ref[h, b], NEG_INF) - lse_h)
                      for b in range(2)]
                dp = [_nt(d2, vs[b][kh][pos]) for b in range(2)]
                delta = jnp.sum(pr[0] * dp[0], axis=1, keepdims=True) + jnp.sum(pr[1] * dp[1], axis=1, keepdims=True)
                p_sink = jnp.exp(sink - lse_h[:, :1])
                dsink = dsink + jnp.where(lane1 == h, -jnp.sum(p_sink * delta), 0.0)
                for b in range(2):
                    dsc = pr[b] * (dp[b] - delta)
                    dsc_ref[h, b] += dsc
                    dzb = (dsc * SCALE).astype(BF16)
                    t = _nn(dzb, ks[b][kh][pos])
                    dq = t if dq is None else dq + t
                    add(dk_parts, b, pos, kh, _tn(dzb, qh))
                    add(dv_parts, b, pos, kh, _tn(pr[b].astype(BF16), dh))
            dq_ref[:, g * PAIR:(g + 1) * PAIR] = dq.astype(BF16)
        dsink_ref[...] += dsink

        def fold(parts, b):
            low = parts[b][0][0] + pltpu.roll(parts[b][1][0], HEAD_DIM, 1)
            high = parts[b][1][1] + pltpu.roll(parts[b][0][1], HEAD_DIM, 1)
            return jnp.where(lo, low, high)

        rp = pl.multiple_of(jnp.maximum(i - 1, 0) * QB, QB)
        rc = pl.multiple_of(i * QB, QB)
        dk_acc[pl.ds(rp, QB), :] += fold(dk_parts, 0)
        dv_acc[pl.ds(rp, QB), :] += fold(dv_parts, 0)
        dk_acc[pl.ds(rc, QB), :] += fold(dk_parts, 1)
        dv_acc[pl.ds(rc, QB), :] += fold(dv_parts, 1)

        @pl.when(i == nq - 1)
        def _():
            dk_ref[...] = dk_acc[...].astype(BF16)
            dv_ref[...] = dv_acc[...].astype(BF16)

    kv = lambda col, prev: pl.BlockSpec((QB, KV_W), (lambda i: (jnp.maximum(i - 1, 0), col)) if prev else (lambda i: (i, col)))
    full = pl.BlockSpec((QB, QB), lambda i: (0, 0))
    smem = pl.BlockSpec(memory_space=pltpu.SMEM)
    whole = lambda shape: pl.BlockSpec(shape, lambda i: (0,) * len(shape))
    return _call(
        body, name=name, grid=(nq,),
        out_shape=(S((T, SWA_W), BF16), S((T, KV_W), BF16), S((T, KV_W), BF16), S((1, QB), F32), S((8, 2, QB, QB), F32)),
        in_specs=[pl.BlockSpec((QB, SWA_W), lambda i: (i, 3)), kv(kcol, True), kv(kcol, False), kv(vcol, True), kv(vcol, False),
                  pl.BlockSpec((QB, SWA_W), lambda i: (i, 0)), pl.BlockSpec((QB, 8 * QB), lambda i: (i, 0)),
                  full, full, smem, smem],
        out_specs=(pl.BlockSpec((QB, SWA_W), lambda i: (i, 0)), whole((T, KV_W)), whole((T, KV_W)), whole((1, QB)),
                   whole((8, 2, QB, QB))),
        scratch=[pltpu.VMEM((8, 2, QB, QB), F32), pltpu.VMEM((T, KV_W), F32), pltpu.VMEM((T, KV_W), F32)],
        sem=("arbitrary",), args=(p, p, p, p, p, do, lse, bprev, bcur, sinks, rel_bias), riders=riders)


def mix_out_fwd(o_sb, o_sw, g_sb, g_sw, wout, h, g_next, name, riders=()):
    T, D = h.shape
    M = SB_W + SWA_W
    tm = _tile(T, 256)

    def body(a_ref, b_ref, ga_ref, gb_ref, w_ref, h_ref, gn_ref, mx_ref, o_ref, n_ref):
        mx_ref[:, :SB_W] = _rms(a_ref[...], ga_ref[...]).astype(BF16)
        mx_ref[:, SB_W:] = _rms(b_ref[...], gb_ref[...]).astype(BF16)
        out = h_ref[...] + _nn(mx_ref[...], w_ref[...])
        o_ref[...] = out
        n_ref[...] = _rms(out, gn_ref[...]).astype(BF16)

    row = lambda n: pl.BlockSpec((tm, n), lambda i: (i, 0))
    vec = lambda n: pl.BlockSpec((1, n), lambda i: (0, 0))
    return _call(
        body, name=name, grid=(T // tm,), out_shape=(S((T, M), BF16), S((T, D), F32), S((T, D), BF16)),
        in_specs=[row(SB_W), row(SWA_W), vec(SB_W), vec(SWA_W), pl.BlockSpec((M, D), lambda i: (0, 0)), row(D), vec(D)],
        out_specs=(row(M), row(D), row(D)),
        sem=("parallel",), args=(o_sb, o_sw, g_sb, g_sw, wout, h, g_next), riders=riders)


def loss_head(h, g, target, name):
    T, D = h.shape
    tm = _tile(T, 256)

    def body(h_ref, g_ref, t_ref, loss_ref, dh_ref, dhb_ref, dg_ref):
        @pl.when(pl.program_id(0) == 0)
        def _():
            loss_ref[...] = jnp.zeros_like(loss_ref)
            dg_ref[...] = jnp.zeros_like(dg_ref)
        x = h_ref[...]
        err = _rms(x, g_ref[...]) - t_ref[...]
        loss_ref[...] += jnp.full((1, QB), 0.5 * jnp.sum(jnp.mean(err * err, axis=-1)), F32)
        dx, dg = _rms_bwd(err / D, x, g_ref[...])
        dh_ref[...] = dx
        dhb_ref[...] = dx.astype(BF16)
        dg_ref[...] += dg

    row = pl.BlockSpec((tm, D), lambda i: (i, 0))
    vec = pl.BlockSpec((1, D), lambda i: (0, 0))
    return pl.pallas_call(
        body, name=name, grid=(T // tm,), out_shape=(S((1, QB), F32), S((T, D), F32), S((T, D), BF16), S((1, D), F32)),
        in_specs=[row, vec, row], out_specs=(pl.BlockSpec((1, QB), lambda i: (0, 0)), row, row, vec),
        compiler_params=_params(("arbitrary",)),
    )(h, g, target)


def ffn_down_bwd(dhb, wd, gate, up, name, riders=()):
    T, D = dhb.shape
    F = wd.shape[0]
    tr, tn = _tile(T, 512), _tile(F, 256)

    def body(d_ref, w_ref, g_ref, u_ref, o_ref):
        w = w_ref[...]
        for r in range(T // tr):
            rows = slice(r * tr, (r + 1) * tr)
            da = 0.5 * _nt(d_ref[rows, :], w)
            o_ref[0, rows, :] = (da * g_ref[rows, :].astype(F32)).astype(BF16)
            o_ref[1, rows, :] = (da * u_ref[rows, :].astype(F32)).astype(BF16)

    tile = pl.BlockSpec((T, tn), lambda j: (0, j))
    return _call(
        body, name=name, grid=(F // tn,), out_shape=S((2, T, F), BF16),
        in_specs=[pl.BlockSpec((T, D), lambda j: (0, 0)), pl.BlockSpec((tn, D), lambda j: (j, 0)), tile, tile],
        out_specs=pl.BlockSpec((2, T, tn), lambda j: (0, 0, j)),
        sem=("parallel",), args=(dhb, wd, gate, up), riders=riders)


def tn_matmul(xs, y, alpha, name, riders=()):
    B, T, N = xs.shape
    D = y.shape[1]
    tn = _tile(N, 256)

    def body(x_ref, y_ref, o_ref, ob_ref):
        o = alpha * _tn(x_ref[...], y_ref[...])
        o_ref[...] = o
        ob_ref[...] = o.astype(BF16)

    tile = pl.BlockSpec((None, tn, D), lambda s, j: (s, j, 0))
    return _call(
        body, name=name, grid=(B, N // tn), out_shape=(S((B, N, D), F32), S((B, N, D), BF16)),
        in_specs=[pl.BlockSpec((None, T, tn), lambda s, j: (s, 0, j)), pl.BlockSpec((T, D), lambda s, j: (0, 0))],
        out_specs=(tile, tile), sem=("parallel", "parallel"), args=(xs, y), riders=riders)


def nn_rms_bwd(xs, ws, h_in, g, dh, name, riders=()):
    B, T, K = xs.shape
    D = ws.shape[2]
    tm = _tile(T, 256)

    def body(x_ref, w_ref, h_ref, g_ref, d_ref, o_ref, ob_ref, dg_ref):
        @pl.when(pl.program_id(0) == 0)
        def _():
            dg_ref[...] = jnp.zeros_like(dg_ref)
        dn = _nn(x_ref[0], w_ref[0])
        for s in range(1, B):
            dn = dn + _nn(x_ref[s], w_ref[s])
        dx, dg = _rms_bwd(dn, h_ref[...], g_ref[...])
        out = d_ref[...] + dx
        o_ref[...] = out
        ob_ref[...] = out.astype(BF16)
        dg_ref[...] += dg

    row = pl.BlockSpec((tm, D), lambda i: (i, 0))
    vec = pl.BlockSpec((1, D), lambda i: (0, 0))
    return _call(
        body, name=name, grid=(T // tm,), out_shape=(S((T, D), F32), S((T, D), BF16), S((1, D), F32)),
        in_specs=[pl.BlockSpec((B, tm, K), lambda i: (0, i, 0)), pl.BlockSpec((B, K, D), lambda i: (0, 0, 0)), row, vec, row],
        out_specs=(row, row, vec),
        sem=("arbitrary",), args=(xs, ws, h_in, g, dh), riders=riders)


def mix_out_bwd(dhb, wout, o_sb, o_sw, g_sb, g_sw, name):
    T, D = dhb.shape
    tm = _tile(T, 256)

    def body(d_ref, w_ref, a_ref, b_ref, ga_ref, gb_ref, da_ref, db_ref, dga_ref, dgb_ref):
        @pl.when(pl.program_id(0) == 0)
        def _():
            dga_ref[...] = jnp.zeros_like(dga_ref)
            dgb_ref[...] = jnp.zeros_like(dgb_ref)
        dm = _nt(d_ref[...], w_ref[...])
        dxa, dga = _rms_bwd(dm[:, :SB_W], a_ref[...], ga_ref[...])
        dxb, dgb = _rms_bwd(dm[:, SB_W:], b_ref[...], gb_ref[...])
        da_ref[...] = dxa
        db_ref[...] = dxb
        dga_ref[...] += dga
        dgb_ref[...] += dgb

    row = lambda n: pl.BlockSpec((tm, n), lambda i: (i, 0))
    vec = lambda n: pl.BlockSpec((1, n), lambda i: (0, 0))
    return pl.pallas_call(
        body, name=name, grid=(T // tm,),
        out_shape=(S((T, SB_W), F32), S((T, SWA_W), F32), S((1, SB_W), F32), S((1, SWA_W), F32)),
        in_specs=[row(D), pl.BlockSpec((SB_W + SWA_W, D), lambda i: (0, 0)), row(SB_W), row(SWA_W), vec(SB_W), vec(SWA_W)],
        out_specs=(row(SB_W), row(SWA_W), vec(SB_W), vec(SWA_W)),
        compiler_params=_params(("arbitrary",)),
    )(dhb, wout, o_sb, o_sw, g_sb, g_sw)


def rel_bias_grad(dscs, bprev, bcur, name):
    n = len(dscs)

    def body(*refs):
        bp_ref, bc_ref, o_ref = refs[n], refs[n + 1], refs[n + 2]
        bks = [bp_ref[...], bc_ref[...]]
        row = lax.broadcasted_iota(jnp.int32, (N_BUCKETS, QB), 0)
        lane = lax.broadcasted_iota(jnp.int32, (N_BUCKETS, QB), 1)
        out = jnp.zeros((N_BUCKETS, QB), F32)
        for h in range(8):
            tot = [sum(refs[l][h, b] for l in range(n)) for b in range(2)]
            for b in range(N_BUCKETS):
                val = jnp.sum(jnp.where(bks[0] == b, tot[0], 0.0)) + jnp.sum(jnp.where(bks[1] == b, tot[1], 0.0))
                out = jnp.where((row == b) & (lane == h), val, out)
        o_ref[...] = out

    return pl.pallas_call(body, name=name, out_shape=S((N_BUCKETS, QB), F32), compiler_params=_params())(*dscs, bprev, bcur)


def _adamw(w, g, m, v):
    m = ADAM_B1 * m + (1.0 - ADAM_B1) * g
    v = ADAM_B2 * v + (1.0 - ADAM_B2) * (g * g)
    m_hat = m / (1.0 - ADAM_B1 ** ADAM_STEP)
    v_hat = v / (1.0 - ADAM_B2 ** ADAM_STEP)
    delta = -ADAM_LR * (m_hat / (jnp.sqrt(v_hat) + ADAM_EPS) + ADAM_WD * w)
    return delta, m, v


def adamw_rows(w, g, m, v, name):
    L, R, C = w.shape
    tr = _tile(R, 256)

    def body(w_ref, g_ref, m_ref, v_ref, d_ref, mo_ref, vo_ref):
        d, mn, vn = _adamw(w_ref[...], g_ref[...], m_ref[...], v_ref[...])
        d_ref[...] = d
        mo_ref[...] = mn
        vo_ref[...] = vn

    tile = pl.BlockSpec((None, tr, C), lambda l, i: (l, i, 0))
    return pl.pallas_call(
        body, name=name, grid=(L, R // tr), out_shape=(S((L, R, C), F32),) * 3,
        in_specs=[tile] * 4, out_specs=(tile,) * 3,
        compiler_params=_params(("parallel", "parallel")),
    )(w, g, m, v)


def adamw_small(w, gs, m, v, name):
    R, C = w.shape

    def body(w_ref, g_ref, m_ref, v_ref, go_ref, d_ref, mo_ref, vo_ref):
        g = g_ref[0]
        for k in range(1, N_DEV):
            g = g + g_ref[k]
        d, mn, vn = _adamw(w_ref[...], g, m_ref[...], v_ref[...])
        go_ref[...] = g
        d_ref[...] = d
        mo_ref[...] = mn
        vo_ref[...] = vn

    return pl.pallas_call(body, name=name, out_shape=(S((R, C), F32),) * 4, compiler_params=_params())(w, gs, m, v)


def kernel(x, norm_ffn1, w_ffn1_gu, w_ffn1_down, norm_mix, w_in, sinks, norm_out_sb, norm_out_swa, w_out, norm_ffn2, w_ffn2_gu, w_ffn2_down, rel_bias, norm_final, loss_target, m_norm_ffn1, m_w_ffn1_gu, m_w_ffn1_down, m_norm_mix, m_w_in, m_sinks, m_norm_out_sb, m_norm_out_swa, m_w_out, m_norm_ffn2, m_w_ffn2_gu, m_w_ffn2_down, m_rel_bias, m_norm_final, v_norm_ffn1, v_w_ffn1_gu, v_w_ffn1_down, v_norm_mix, v_w_in, v_sinks, v_norm_out_sb, v_norm_out_swa, v_w_out, v_norm_ffn2, v_w_ffn2_gu, v_w_ffn2_down, v_rel_bias, v_norm_final):
    L = norm_ffn1.shape[0]
    T, D = x.shape[1], x.shape[2]
    F = w_ffn1_down.shape[1] * N_DEV
    h = x.reshape(T, D)
    target = loss_target.reshape(T, D)
    after, upto, before = _tri_consts()
    bprev, bcur = _t5_buckets()

    local = {}
    for l in range(L):
        local[f"gu1_{l}"] = w_ffn1_gu[l].T.astype(BF16)
        local[f"d1_{l}"] = w_ffn1_down[l].astype(BF16)
        local[f"in_{l}"] = w_in[l].T.astype(BF16)
        local[f"out_{l}"] = w_out[l].astype(BF16)
        local[f"gu2_{l}"] = w_ffn2_gu[l].T.astype(BF16)
        local[f"d2_{l}"] = w_ffn2_down[l].astype(BF16)
    full = {}
    grads, chip_sum, recv_b = {}, {}, {}

    def run(fn, *args, ag=(), rs1=(), rs2=()):
        ag = [n for n in ag if n in local]
        rs2 = [n if isinstance(n, tuple) else (n, None) for n in rs2]

        def second(n, k):
            sb = chip_sum[n][1]
            rows = None if k is None else (k * (sb.shape[1] // 2), sb.shape[1] // 2)
            return scatter_second(sb, rows, recv_b.get(n))

        riders = [gather(local[n]) for n in ag] + [scatter_first(grads[n][1]) for n in rs1] + [second(n, k) for n, k in rs2]
        if not riders:
            return fn(*args)
        outs, per = fn(*args, riders=riders)
        per = [p[0] for p in per]
        for n in ag:
            buf = per.pop(0)
            full[n] = buf.reshape(N_DEV * buf.shape[1], D)
        for n in rs1:
            chip_sum[n] = scatter_add(grads[n][0], per.pop(0), f"rs_add_{n}")
        for n, _ in rs2:
            recv_b[n] = per.pop(0)
        return outs

    def idle(name, riders=()):
        return None, idle_host(riders, name)

    gu = lambda n: full[n].reshape(2, F, D)
    slots = lambda pair: tuple(t.reshape(N_DEV, -1, D) for t in pair)
    vec = lambda a: a.reshape(1, -1)

    run(idle, "ag_head", ag=("gu1_0",))
    saved = []
    n_next = rms_cast(h, vec(norm_ffn1[0]), "rms_first")
    for l in range(L):
        nx = l + 1
        s = {"h0": h, "n1": n_next}
        s["gate1"], s["up1"], s["a1"] = run(ffn_up_fwd, s["n1"], gu(f"gu1_{l}"), f"ffn1_up{l}", ag=("d1_0",) if l == 0 else ())
        h = run(ffn_down_fwd, s["a1"], full[f"d1_{l}"], h, None, f"ffn1_down{l}", ag=("in_0",) if l == 0 else ())
        s["h1"] = h
        s["n2"], s["p"] = mix_in_fwd(h, vec(norm_mix[l]), full[f"in_{l}"], f"mix_in{l}")
        s["o_sb"], s["tot"] = run(sb_attn_fwd, s["p"], after, f"sb_fwd{l}", ag=(f"out_{l}", f"gu2_{l}", f"d2_{l}"))
        s["o_sw"], s["lse"] = run(swa_fwd, s["p"], vec(sinks[l]), rel_bias, bprev, bcur, f"swa_fwd{l}", ag=(f"gu1_{nx}",))
        s["mixed"], h, s["n3"] = run(mix_out_fwd, s["o_sb"], s["o_sw"], vec(norm_out_sb[l]), vec(norm_out_swa[l]),
                                     full[f"out_{l}"], h, vec(norm_ffn2[l]), f"mix_out{l}")
        s["h2"] = h
        s["gate2"], s["up2"], s["a2"] = run(ffn_up_fwd, s["n3"], gu(f"gu2_{l}"), f"ffn2_up{l}", ag=(f"d1_{nx}",))
        if nx < L:
            h, n_next = run(ffn_down_fwd, s["a2"], full[f"d2_{l}"], h, vec(norm_ffn1[nx]), f"ffn2_down{l}", ag=(f"in_{nx}",))
        else:
            h = run(ffn_down_fwd, s["a2"], full[f"d2_{l}"], h, None, f"ffn2_down{l}")
        saved.append(s)

    loss_part, dh, dhb, dg_final = loss_head(h, vec(norm_final), target, "loss_head")
    loss = lax.psum(loss_part[0, 0], ("x", "y", "c"))

    small = {k: [None] * L for k in ("ffn1", "mix", "sinks", "osb", "osw", "ffn2", "dsc")}
    for l in reversed(range(L)):
        s = saved[l]

        def ffn_bwd(dh, dhb, tag, gate, up, a, n, h_in, g, r_down, r_dwgu, r_up):
            gu_n, d_n = f"gu{tag}_{l}", f"d{tag}_{l}"
            dgu = run(ffn_down_bwd, dhb, full[d_n], gate, up, f"ffn{tag}_down_bwd{l}", **r_down)
            grads[gu_n] = slots(run(tn_matmul, dgu, n, 1.0, f"ffn{tag}_dwgu{l}", **r_dwgu))
            grads[d_n] = slots(run(tn_matmul, a[None], dhb, 0.5, f"ffn{tag}_dwd{l}", rs1=(gu_n,)))
            return run(nn_rms_bwd, dgu, gu(gu_n), h_in, g, dh, f"ffn{tag}_up_bwd{l}", rs1=(d_n,), **r_up)

        later = l + 1 < L
        dh, dhb, small["ffn2"][l] = ffn_bwd(dh, dhb, 2, s["gate2"], s["up2"], s["a2"], s["n3"], s["h2"], vec(norm_ffn2[l]),
                                            dict(rs2=((f"gu1_{l + 1}", 1),) if later else ()),
                                            dict(rs2=(f"d1_{l + 1}",) if later else ()), {})
        do_sb, do_sw, small["osb"][l], small["osw"][l] = mix_out_bwd(
            dhb, full[f"out_{l}"], s["o_sb"], s["o_sw"], vec(norm_out_sb[l]), vec(norm_out_swa[l]), f"mix_out_bwd{l}")
        grads[f"out_{l}"] = slots(tn_matmul(s["mixed"][None], dhb, 1.0, f"dwout{l}"))
        dq_sb, dk_sb, dv_sb = run(sb_attn_bwd, s["p"], do_sb, s["tot"], upto, before, f"sb_bwd{l}",
                                  rs2=(f"gu2_{l}", f"d2_{l}"), rs1=(f"out_{l}",))
        dq_sw, dk_sw, dv_sw, small["sinks"][l], small["dsc"][l] = run(
            swa_bwd, s["p"], do_sw, s["lse"], vec(sinks[l]), rel_bias, bprev, bcur, f"swa_bwd{l}", rs2=(f"out_{l}",))
        dp = jnp.concatenate([dq_sb, dk_sb, dv_sb, dq_sw, dk_sw, dv_sw], axis=1)
        dh, dhb, small["mix"][l] = nn_rms_bwd(dp[None], full[f"in_{l}"][None], s["h1"], vec(norm_mix[l]), dh, f"mix_in_bwd{l}")
        grads[f"in_{l}"] = slots(tn_matmul(dp[None], s["n2"], 1.0, f"dwin{l}"))
        dh, dhb, small["ffn1"][l] = ffn_bwd(dh, dhb, 1, s["gate1"], s["up1"], s["a1"], s["n1"], s["h0"], vec(norm_ffn1[l]),
                                            dict(rs1=(f"in_{l}",)), dict(rs2=(f"in_{l}",)), dict(rs2=((f"gu1_{l}", 0),)))

    grad_x = dh.reshape(x.shape)
    run(idle, "rs_tail", rs2=(("gu1_0", 1), "d1_0"))

    def scattered(tag, transpose):
        out = []
        for l in range(L):
            n = f"{tag}_{l}"
            mine = scatter_sum(chip_sum[n][0], recv_b[n], f"rs_sum_{n}")
            out.append(mine.T if transpose else mine)
        return jnp.stack(out)

    g_gu1, g_d1, g_in = scattered("gu1", True), scattered("d1", False), scattered("in", True)
    g_out, g_gu2, g_d2 = scattered("out", False), scattered("gu2", True), scattered("d2", False)

    upd = {}
    for nm, w, g, m, v in (("gu1", w_ffn1_gu, g_gu1, m_w_ffn1_gu, v_w_ffn1_gu), ("d1", w_ffn1_down, g_d1, m_w_ffn1_down, v_w_ffn1_down),
                           ("in", w_in, g_in, m_w_in, v_w_in), ("out", w_out, g_out, m_w_out, v_w_out),
                           ("gu2", w_ffn2_gu, g_gu2, m_w_ffn2_gu, v_w_ffn2_gu), ("d2", w_ffn2_down, g_d2, m_w_ffn2_down, v_w_ffn2_down)):
        upd[nm] = (g,) + tuple(adamw_rows(w, g, m, v, f"adamw_{nm}"))

    d_rel = rel_bias_grad(small["dsc"], bprev, bcur, "rel_bias_grad")[:, :8]

    PW = max(D, SB_W + SWA_W)

    def pack(ffn1, mix, ffn2, final, osb, osw, snk, rel):
        wide = lambda a: jnp.pad(a.reshape(-1), (0, PW - a.size))
        rows = [wide(ffn1[l]) for l in range(L)] + [wide(mix[l]) for l in range(L)] + [wide(ffn2[l]) for l in range(L)]
        rows.append(wide(final))
        rows += [wide(jnp.concatenate([osb[l].reshape(-1), osw[l].reshape(-1)])) for l in range(L)]
        rows.append(wide(jnp.concatenate([snk[l].reshape(-1)[:8] for l in range(L)] + [rel.reshape(-1)])))
        arr = jnp.stack(rows)
        return jnp.pad(arr, ((0, (-arr.shape[0]) % 8), (0, 0)))

    def unpack(arr):
        ffn1, mix, ffn2 = arr[0:L, :D], arr[L:2 * L, :D], arr[2 * L:3 * L, :D]
        final = arr[3 * L, :D]
        ob = arr[3 * L + 1:4 * L + 1]
        tail = arr[4 * L + 1]
        return (ffn1, mix, tail[:8 * L].reshape(L, 8), ob[:, :SB_W], ob[:, SB_W:SB_W + SWA_W], ffn2,
                tail[8 * L:8 * L + N_BUCKETS * 8].reshape(N_BUCKETS, 8), final)

    g_small = pack(small["ffn1"], small["mix"], small["ffn2"], dg_final, small["osb"], small["osw"], small["sinks"], d_rel)
    w_small = pack(norm_ffn1, norm_mix, norm_ffn2, norm_final, norm_out_sb, norm_out_swa, sinks, rel_bias)
    m_small = pack(m_norm_ffn1, m_norm_mix, m_norm_ffn2, m_norm_final, m_norm_out_sb, m_norm_out_swa, m_sinks, m_rel_bias)
    v_small = pack(v_norm_ffn1, v_norm_mix, v_norm_ffn2, v_norm_final, v_norm_out_sb, v_norm_out_swa, v_sinks, v_rel_bias)
    gs_small = all_gather_rows(g_small, "ag_small")
    small_out = [unpack(a) for a in adamw_small(w_small, gs_small, m_small, v_small, "adamw_small")]

    def group(k):
        sm = small_out[k]
        return (sm[0], upd["gu1"][k], upd["d1"][k], sm[1], upd["in"][k], sm[2], sm[3], sm[4], upd["out"][k], sm[5],
                upd["gu2"][k], upd["d2"][k], sm[6], sm[7])

    return (loss, grad_x, *group(0), *group(1), *group(2), *group(3))
```

```python
import math

import jax
import jax.numpy as jnp
from jax import lax
from jax.experimental import pallas as pl
from jax.experimental.pallas import tpu as pltpu

F32 = jnp.float32
BF16 = jnp.bfloat16
S = jax.ShapeDtypeStruct

N_DEV = 8
HEAD_DIM = 64
SB_HEADS = 8
PAIR = 2 * HEAD_DIM
SB_W = 512
SWA_W = 512
KV_W = 128
IN_W = 3 * SB_W + SWA_W + 2 * KV_W
QB = 128
N_BUCKETS = 32
MAX_DISTANCE = 128
EPS = 1e-6
NEG_INF = -1e30
SCALE = HEAD_DIM ** -0.5

ADAM_LR = 0.001
ADAM_B1 = 0.9
ADAM_B2 = 0.999
ADAM_EPS = 1e-08
ADAM_WD = 0.01
ADAM_STEP = 10

VMEM_LIMIT = 56 * 1024 * 1024
MESH = pl.DeviceIdType.MESH


def _params(sem=None, vmem=VMEM_LIMIT):
    return pltpu.CompilerParams(dimension_semantics=sem, vmem_limit_bytes=vmem)


def _nn(a, b):
    return jnp.dot(a, b, preferred_element_type=F32)


def _nt(a, b):
    return lax.dot_general(a, b, (((1,), (1,)), ((), ())), preferred_element_type=F32)


def _tn(a, b):
    return lax.dot_general(a, b, (((0,), (0,)), ((), ())), preferred_element_type=F32)


def _tri(x, m2):
    hi = x.astype(BF16)
    lo = (x - hi.astype(F32)).astype(BF16)
    return _nn(jnp.concatenate([hi, lo], axis=1), m2)


def _rms(x, g):
    r = lax.rsqrt(jnp.mean(x * x, axis=-1, keepdims=True) + EPS)
    return x * r * g


def _rms_bwd(dy, x, g):
    r = lax.rsqrt(jnp.mean(x * x, axis=-1, keepdims=True) + EPS)
    xhat = x * r
    u = dy * g
    dx = r * (u - xhat * jnp.mean(u * xhat, axis=-1, keepdims=True))
    return dx, jnp.sum(dy * xhat, axis=0, keepdims=True)


def _softplus_logsig(z):
    sp = jnp.maximum(z, 0.0) + jnp.log(1.0 + jnp.exp(-jnp.abs(z)))
    return sp, z - sp


def _tile(n, want):
    t = min(n, want)
    while n % t:
        t //= 2
    return t


def _place():
    x, y, c = lax.axis_index("x"), lax.axis_index("y"), lax.axis_index("c")
    chips = [(1 - x, y), (x, 1 - y), (1 - x, 1 - y)]
    return x, y, c, chips


def all_gather_rows(v, name):
    R, C = v.shape

    def body(v_ref, out_ref, send_sems, recv_sems, local_sem):
        x, y, c, chips = _place()
        me, sibling = (x, y, c), (x, y, 1 - c)

        def slot(px, py, pc):
            return out_ref.at[4 * px + 2 * py + pc]

        def copy(k, block, to, src=None):
            return pltpu.make_async_remote_copy(
                src_ref=slot(*block) if src is None else src, dst_ref=slot(*block),
                send_sem=send_sems.at[k], recv_sem=recv_sems.at[k], device_id=to, device_id_type=MESH)

        mine = pltpu.make_async_copy(v_ref, slot(*me), local_sem)
        mine.start()
        first = [copy(0, me, sibling, src=v_ref)]
        first += [copy(1 + j, me, (*chip, c), src=v_ref) for j, chip in enumerate(chips)]
        for cp in first:
            cp.start()
        passed = [copy(4 + j, (*chip, c), sibling) for j, chip in enumerate(chips)]
        for j, chip in enumerate(chips):
            copy(1 + j, (*chip, c), me).wait_recv()
            passed[j].start()
        copy(0, sibling, me).wait_recv()
        for j, chip in enumerate(chips):
            copy(4 + j, (*chip, 1 - c), me).wait_recv()
        for cp in first + passed:
            cp.wait_send()
        mine.wait()

    return pl.pallas_call(
        body, name=name, out_shape=S((N_DEV, R, C), v.dtype),
        in_specs=[pl.BlockSpec(memory_space=pl.ANY)], out_specs=pl.BlockSpec(memory_space=pl.ANY),
        scratch_shapes=[pltpu.SemaphoreType.DMA((7,)), pltpu.SemaphoreType.DMA((7,)), pltpu.SemaphoreType.DMA],
    )(v)


class _Exchange:
    def __init__(self, ins, outs, n_first, n_second, n_local, plan, aliases=None):
        self.ins, self.outs, self.plan, self.aliases = list(ins), list(outs), plan, aliases or {}
        self.n_first, self.n_second, self.n_local = n_first, n_second, n_local

    def scratch(self):
        n = self.n_first + self.n_second
        return [pltpu.SemaphoreType.DMA((n,)), pltpu.SemaphoreType.DMA((n,)), pltpu.SemaphoreType.DMA((max(self.n_local, 1),))]

    def _copies(self, in_refs, out_refs, sems):
        send_sems, recv_sems, local_sems = sems
        first, second, local = self.plan(in_refs, out_refs)
        rem = [pltpu.make_async_remote_copy(src_ref=s, dst_ref=d, send_sem=send_sems.at[k], recv_sem=recv_sems.at[k],
                                            device_id=dev, device_id_type=MESH) for k, (s, d, dev) in enumerate(first + second)]
        loc = [pltpu.make_async_copy(s, d, local_sems.at[k]) for k, (s, d) in enumerate(local)]
        return rem[:len(first)], rem[len(first):], loc

    def start(self, in_refs, out_refs, sems):
        first, _, loc = self._copies(in_refs, out_refs, sems)
        for cp in first + loc:
            cp.start()

    def middle(self, in_refs, out_refs, sems):
        first, second, _ = self._copies(in_refs, out_refs, sems)
        if second:
            for cp in first:
                cp.wait_recv()
            for cp in second:
                cp.start()

    def finish(self, in_refs, out_refs, sems):
        first, second, loc = self._copies(in_refs, out_refs, sems)
        for cp in second if second else first:
            cp.wait_recv()
        for cp in first + second:
            cp.wait_send()
        for cp in loc:
            cp.wait()


def gather(v, rows=None, into=None):
    R, C = v.shape
    r0, nr = rows or (0, R)

    def plan(ins, outs):
        x, y, c, chips = _place()
        slot = lambda px, py, pc: outs[0].at[4 * px + 2 * py + pc, pl.ds(r0, nr), :]
        src, mine = ins[0].at[pl.ds(r0, nr), :], slot(x, y, c)
        first = [(src, mine, (x, y, 1 - c))] + [(src, mine, (*chip, c)) for chip in chips]
        second = [(slot(*chip, c), slot(*chip, c), (x, y, 1 - c)) for chip in chips]
        return first, second, [(src, mine)]

    if into is None:
        return _Exchange([v], [S((N_DEV, R, C), v.dtype)], 4, 3, 1, plan)
    return _Exchange([v, into], [S((N_DEV, R, C), v.dtype)], 4, 3, 1, plan, aliases={1: 0})


def scatter_first(gb):
    _, R, C = gb.shape

    def plan(ins, outs):
        x, y, c, chips = _place()
        owners = [(x, y)] + chips
        return [(ins[0].at[4 * px + 2 * py + (1 - c)], outs[0].at[j], (x, y, 1 - c)) for j, (px, py) in enumerate(owners)], [], []

    return _Exchange([gb], [S((4, R, C), BF16)], 4, 0, 0, plan)


def scatter_second(sb, rows=None, into=None):
    r0, nr = rows or (0, sb.shape[1])

    def plan(ins, outs):
        x, y, c, chips = _place()
        part = lambda ref, j: ref.at[j, pl.ds(r0, nr), :]
        return [(part(ins[0], j), part(outs[0], j), (*chips[j], c)) for j in range(3)], [], []

    if into is None:
        return _Exchange([sb], [S(sb.shape, BF16)], 3, 0, 0, plan)
    return _Exchange([sb, into], [S(sb.shape, BF16)], 3, 0, 0, plan, aliases={1: 0})


def _call(body, *, name, grid, in_specs, out_specs, out_shape, args, scratch=(), sem=None, riders=()):
    single = not isinstance(out_shape, (tuple, list))
    out_shape = (out_shape,) if single else tuple(out_shape)
    out_specs = (out_specs,) if single else tuple(out_specs)
    n_in, n_out, n_sc = len(in_specs), len(out_shape), len(scratch)
    if not riders:
        res = pl.pallas_call(body, name=name, grid=grid, in_specs=list(in_specs), out_specs=out_specs, out_shape=out_shape,
                             scratch_shapes=list(scratch), compiler_params=_params(sem))(*args)
        return res[0] if single else res
    r_ins = [a for r in riders for a in r.ins]
    r_outs = [o for r in riders for o in r.outs]
    r_scr = [s for r in riders for s in r.scratch()]
    aliases, i0, o0 = {}, n_in, n_out
    for r in riders:
        for a, b in r.aliases.items():
            aliases[i0 + a] = o0 + b
        i0, o0 = i0 + len(r.ins), o0 + len(r.outs)
    steps = math.prod(grid)

    def full(*refs):
        ins, rin = refs[:n_in], refs[n_in:n_in + len(r_ins)]
        pos = n_in + len(r_ins)
        outs, rout = refs[pos:pos + n_out], refs[pos + n_out:pos + n_out + len(r_outs)]
        pos += n_out + len(r_outs)
        sc, rsc = refs[pos:pos + n_sc], refs[pos + n_sc:]
        step = 0
        for d, n in enumerate(grid):
            step = step * n + pl.program_id(d)

        def each(method):
            i, o = 0, 0
            for k, r in enumerate(riders):
                getattr(r, method)(rin[i:i + len(r.ins)], rout[o:o + len(r.outs)], rsc[3 * k:3 * k + 3])
                i, o = i + len(r.ins), o + len(r.outs)

        @pl.when(step == 0)
        def _():
            each("start")
        body(*ins, *outs, *sc)

        @pl.when(step == max(steps - 1 - max(steps // 8, 1), 0))
        def _():
            each("middle")

        @pl.when(step == steps - 1)
        def _():
            each("finish")

    anywhere = pl.BlockSpec(memory_space=pl.ANY)
    res = pl.pallas_call(
        full, name=name, grid=grid, in_specs=list(in_specs) + [anywhere] * len(r_ins),
        out_specs=out_specs + (anywhere,) * len(r_outs), out_shape=out_shape + tuple(r_outs),
        scratch_shapes=list(scratch) + r_scr, input_output_aliases=aliases,
        compiler_params=_params(("arbitrary",) * len(grid)))(*args, *r_ins)
    host, rest, per = res[:n_out], list(res[n_out:]), []
    for r in riders:
        per.append(rest[:len(r.outs)])
        rest = rest[len(r.outs):]
    return (host[0] if single else tuple(host)), per


def idle_host(riders, name):
    def body(o_ref):
        o_ref[...] = jnp.zeros_like(o_ref)

    return _call(body, name=name, grid=(1,), in_specs=[], out_specs=pl.BlockSpec((8, QB), lambda i: (0, 0)),
                 out_shape=S((8, QB), F32), args=(), riders=riders)[1]


def _rows_tile(n, cap):
    return max(t for t in range(16, min(n, cap) + 1, 16) if n % t == 0)


def scatter_add(g, ra, name):
    _, R, C = g.shape
    tr = _rows_tile(R, 176)
    x, y, c, chips = _place()
    slots = jnp.stack([4 * px + 2 * py + c for px, py in [(x, y)] + chips]).astype(jnp.int32)

    def body(s_ref, g0, g1, g2, g3, ra_ref, own_ref, sb_ref):
        own_ref[...] = g0[...] + ra_ref[0].astype(F32)
        for j, gj in enumerate((g1, g2, g3)):
            sb_ref[j] = (gj[...] + ra_ref[j + 1].astype(F32)).astype(BF16)

    spec = pltpu.PrefetchScalarGridSpec(
        num_scalar_prefetch=1, grid=(R // tr,),
        in_specs=[pl.BlockSpec((None, tr, C), lambda i, s, j=j: (s[j], i, 0)) for j in range(4)]
        + [pl.BlockSpec((4, tr, C), lambda i, s: (0, i, 0))],
        out_specs=(pl.BlockSpec((tr, C), lambda i, s: (i, 0)), pl.BlockSpec((3, tr, C), lambda i, s: (0, i, 0))))
    return pl.pallas_call(body, name=name, grid_spec=spec, out_shape=(S((R, C), F32), S((3, R, C), BF16)),
                          compiler_params=_params(("parallel",)))(slots, g, g, g, g, ra)


def scatter_sum(own, rb, name):
    R, C = own.shape
    tr = _rows_tile(R, 176)

    def body(o_ref, r_ref, g_ref):
        g_ref[...] = o_ref[...] + r_ref[0].astype(F32) + r_ref[1].astype(F32) + r_ref[2].astype(F32)

    return _call(body, name=name, grid=(R // tr,), in_specs=[pl.BlockSpec((tr, C), lambda i: (i, 0)),
                                                              pl.BlockSpec((3, tr, C), lambda i: (0, i, 0))],
                 out_specs=pl.BlockSpec((tr, C), lambda i: (i, 0)), out_shape=S((R, C), F32), args=(own, rb), sem=("parallel",))


def rms_cast(h, g, name):
    T, D = h.shape
    tm = _tile(T, 512)

    def body(h_ref, g_ref, n_ref):
        n_ref[...] = _rms(h_ref[...], g_ref[...]).astype(BF16)

    row = pl.BlockSpec((tm, D), lambda i: (i, 0))
    return _call(body, name=name, grid=(T // tm,), out_shape=S((T, D), BF16), in_specs=[row, pl.BlockSpec((1, D), lambda i: (0, 0))],
                 out_specs=row, sem=("parallel",), args=(h, g))


def ffn_up_fwd(n, wgu, name, riders=()):
    T, D = n.shape
    F = wgu.shape[1]
    tr, tn = _tile(T, 512), _tile(F, 256)

    def body(n_ref, wg_ref, wu_ref, dgate_ref, dup_ref, a_ref):
        wg, wu = wg_ref[...], wu_ref[...]
        for r in range(T // tr):
            rows = slice(r * tr, (r + 1) * tr)
            x = n_ref[rows, :]
            gate = _nt(x, wg)
            up = _nt(x, wu)
            s = jax.nn.sigmoid(gate)
            silu = gate * s
            dgate_ref[rows, :] = (up * (s * (1.0 + gate * (1.0 - s)))).astype(BF16)
            dup_ref[rows, :] = silu.astype(BF16)
            a_ref[rows, :] = (silu * up).astype(BF16)

    tile = pl.BlockSpec((T, tn), lambda j: (0, j))
    return _call(
        body, name=name, grid=(F // tn,), out_shape=(S((T, F), BF16),) * 3,
        in_specs=[pl.BlockSpec((T, D), lambda j: (0, 0)),
                  pl.BlockSpec((None, tn, D), lambda j: (0, j, 0)), pl.BlockSpec((None, tn, D), lambda j: (1, j, 0))],
        out_specs=(tile, tile, tile), sem=("parallel",), args=(n, wgu, wgu), riders=riders)


def ffn_down_fwd(a, wd, h, g_next, name, riders=()):
    T, F = a.shape
    D = wd.shape[1]
    tm = _tile(T, 256)

    def body(a_ref, w_ref, h_ref, *rest):
        out = h_ref[...] + 0.5 * _nn(a_ref[...], w_ref[...])
        if g_next is None:
            rest[0][...] = out
        else:
            g_ref, o_ref, n_ref = rest
            o_ref[...] = out
            n_ref[...] = _rms(out, g_ref[...]).astype(BF16)

    row = pl.BlockSpec((tm, D), lambda i: (i, 0))
    more = g_next is not None
    return _call(
        body, name=name, grid=(T // tm,), out_shape=(S((T, D), F32), S((T, D), BF16)) if more else S((T, D), F32),
        in_specs=[pl.BlockSpec((tm, F), lambda i: (i, 0)), pl.BlockSpec((F, D), lambda i: (0, 0)), row]
        + ([pl.BlockSpec((1, D), lambda i: (0, 0))] if more else []),
        out_specs=(row, row) if more else row,
        sem=("parallel",), args=(a, wd, h) + ((g_next,) if more else ()), riders=riders)


def mix_in_fwd(h, g, win, name):
    T, D = h.shape
    N = win.shape[0]
    tm = _tile(T, 256)

    def body(h_ref, g_ref, w_ref, n_ref, p_ref):
        n = _rms(h_ref[...], g_ref[...]).astype(BF16)
        n_ref[...] = n
        p_ref[...] = _nt(n, w_ref[...]).astype(BF16)

    return pl.pallas_call(
        body, name=name, grid=(T // tm,), out_shape=(S((T, D), BF16), S((T, N), BF16)),
        in_specs=[pl.BlockSpec((tm, D), lambda i: (i, 0)), pl.BlockSpec((1, D), lambda i: (0, 0)),
                  pl.BlockSpec((N, D), lambda i: (0, 0))],
        out_specs=(pl.BlockSpec((tm, D), lambda i: (i, 0)), pl.BlockSpec((tm, N), lambda i: (i, 0))),
        compiler_params=_params(("parallel",)),
    )(h, g, win)


def _tri_consts():
    r = lax.broadcasted_iota(jnp.int32, (QB, QB), 0)
    c = lax.broadcasted_iota(jnp.int32, (QB, QB), 1)
    ones = jnp.ones((QB, QB), BF16)

    def stacked(tri):
        m = jnp.concatenate([tri.astype(BF16), ones], axis=1)
        return jnp.concatenate([m, m], axis=0)

    return stacked(r > c), stacked(r <= c), stacked(r < c)


def _half_masks():
    lane = lax.broadcasted_iota(jnp.int32, (QB, PAIR), 1)
    row = lax.broadcasted_iota(jnp.int32, (QB, PAIR), 0)
    return lane < HEAD_DIM, lane, row


def sb_attn_fwd(p, after, name, riders=()):
    T = p.shape[0]
    nq = T // QB

    def body(q_ref, k_ref, v_ref, m_ref, o_ref, tot_ref, q_sc, acc_ref):
        i = pl.program_id(0)
        lo, lane, row = _half_masks()
        causal = lane < row
        for hp in range(SB_HEADS // 2):
            q2 = q_ref[:, hp * PAIR:(hp + 1) * PAIR].astype(F32) * SCALE
            q_sc[2 * hp] = jnp.where(lo, q2, 0.0).astype(BF16)
            q_sc[2 * hp + 1] = jnp.where(lo, 0.0, q2).astype(BF16)
        m2 = m_ref[...]

        def block(j, diag):
            r0 = pl.multiple_of(j * QB, QB)
            heads = range(SB_HEADS)
            k2 = [k_ref[pl.ds(r0, QB), hp * PAIR:(hp + 1) * PAIR] for hp in range(SB_HEADS // 2)]
            v2 = [v_ref[pl.ds(r0, QB), hp * PAIR:(hp + 1) * PAIR] for hp in range(SB_HEADS // 2)]
            z = [_nt(q_sc[h], k2[h // 2]) for h in heads]
            spls = [_softplus_logsig(z[h]) for h in heads]
            sp = [jnp.where(causal, spls[h][0], 0.0) if diag else spls[h][0] for h in heads]
            rr = [_tri(sp[h], m2) for h in heads]
            if diag:
                w = [jnp.where(causal, jnp.exp(spls[h][1] - rr[h][:, :QB]), 0.0) for h in heads]
                pv = [_nn(w[h].astype(BF16), v2[h // 2]) for h in heads]
                for h in heads:
                    acc_ref[h] = pv[h]
                    tot_ref[:, h * QB:(h + 1) * QB] = rr[h][:, QB:]
            else:
                c = [tot_ref[:, h * QB:(h + 1) * QB] for h in heads]
                w = [jnp.exp(spls[h][1] - (c[h] + rr[h][:, :QB])) for h in heads]
                pv = [_nn(w[h].astype(BF16), v2[h // 2]) for h in heads]
                for h in heads:
                    acc_ref[h] += pv[h]
                    tot_ref[:, h * QB:(h + 1) * QB] = c[h] + rr[h][:, QB:]

        block(i, True)

        def step(t, carry):
            block(i - 1 - t, False)
            return carry
        lax.fori_loop(0, i, step, 0)
        for hp in range(SB_HEADS // 2):
            o_ref[:, hp * PAIR:(hp + 1) * PAIR] = jnp.where(lo, acc_ref[2 * hp], acc_ref[2 * hp + 1])

    return _call(
        body, name=name, grid=(nq,), out_shape=(S((T, SB_W), F32), S((T, SB_HEADS * QB), F32)),
        in_specs=[pl.BlockSpec((QB, SB_W), lambda i: (i, 0)), pl.BlockSpec((T, SB_W), lambda i: (0, 1)),
                  pl.BlockSpec((T, SB_W), lambda i: (0, 2)), pl.BlockSpec((2 * QB, 2 * QB), lambda i: (0, 0))],
        out_specs=(pl.BlockSpec((QB, SB_W), lambda i: (i, 0)), pl.BlockSpec((QB, SB_HEADS * QB), lambda i: (i, 0))),
        scratch=[pltpu.VMEM((SB_HEADS, QB, PAIR), BF16), pltpu.VMEM((SB_HEADS, QB, PAIR), F32)],
        sem=("arbitrary",), args=(p, p, p, after), riders=riders)


def sb_attn_bwd(p, do, tot, upto, before, name, riders=()):
    T = p.shape[0]
    nq = T // QB

    def body(q_ref, k_ref, v_ref, do_ref, tot_ref, mp_ref, mg_ref, dq_ref, dk_ref, dv_ref,
             q_sc, d_sc, pg_sc, dq_acc, dk_acc, dv_acc):
        i = pl.program_id(0)
        lo, lane, row = _half_masks()
        causal = lane < row
        for hp in range(SB_HEADS // 2):
            q2 = q_ref[:, hp * PAIR:(hp + 1) * PAIR].astype(F32) * SCALE
            d2 = do_ref[:, hp * PAIR:(hp + 1) * PAIR]
            q_sc[2 * hp] = jnp.where(lo, q2, 0.0).astype(BF16)
            q_sc[2 * hp + 1] = jnp.where(lo, 0.0, q2).astype(BF16)
            d_sc[2 * hp] = jnp.where(lo, d2, 0.0).astype(BF16)
            d_sc[2 * hp + 1] = jnp.where(lo, 0.0, d2).astype(BF16)
        mp, mg = mp_ref[...], mg_ref[...]

        @pl.when(i == 0)
        def _():
            dk_acc[...] = jnp.zeros_like(dk_acc)
            dv_acc[...] = jnp.zeros_like(dv_acc)
        pg_sc[...] = jnp.zeros_like(pg_sc)
        dq_acc[...] = jnp.zeros_like(dq_acc)

        def block(j, diag):
            r0 = pl.multiple_of(j * QB, QB)
            heads = range(SB_HEADS)
            pairs = range(SB_HEADS // 2)
            k2 = [k_ref[pl.ds(r0, QB), hp * PAIR:(hp + 1) * PAIR] for hp in pairs]
            v2 = [v_ref[pl.ds(r0, QB), hp * PAIR:(hp + 1) * PAIR] for hp in pairs]
            z = [_nt(q_sc[h], k2[h // 2]) for h in heads]
            dw = [_nt(d_sc[h], v2[h // 2]) for h in heads]
            spls = [_softplus_logsig(z[h]) for h in heads]
            sp = [jnp.where(causal, spls[h][0], 0.0) if diag else spls[h][0] for h in heads]
            rr = [_tri(sp[h], mp) for h in heads]
            pc = [pg_sc[2 * h] for h in heads]
            w = [jnp.exp(spls[h][1] - (tot_ref[:, h * QB:(h + 1) * QB] - (pc[h] + rr[h][:, :QB]))) for h in heads]
            if diag:
                w = [jnp.where(causal, w[h], 0.0) for h in heads]
            gg = [dw[h] * w[h] for h in heads]
            rg = [_tri(gg[h], mg) for h in heads]
            gc = [pg_sc[2 * h + 1] for h in heads]
            dz = [gg[h] - (gg[h] + gc[h] + rg[h][:, :QB]) * jnp.exp(spls[h][1]) for h in heads]
            if diag:
                dz = [jnp.where(causal, dz[h], 0.0) for h in heads]
            dzb = [dz[h].astype(BF16) for h in heads]
            wb = [w[h].astype(BF16) for h in heads]
            dq = [_nn(dzb[h], k2[h // 2]) for h in heads]
            dk = [_tn(dzb[2 * hp], q_sc[2 * hp]) + _tn(dzb[2 * hp + 1], q_sc[2 * hp + 1]) for hp in pairs]
            dv = [_tn(wb[2 * hp], d_sc[2 * hp]) + _tn(wb[2 * hp + 1], d_sc[2 * hp + 1]) for hp in pairs]
            for h in heads:
                dq_acc[h] += dq[h]
                if not diag:
                    pg_sc[2 * h] = pc[h] + rr[h][:, QB:]
                    pg_sc[2 * h + 1] = gc[h] + rg[h][:, QB:]
            for hp in pairs:
                dk_acc[pl.ds(r0, QB), hp * PAIR:(hp + 1) * PAIR] += dk[hp]
                dv_acc[pl.ds(r0, QB), hp * PAIR:(hp + 1) * PAIR] += dv[hp]

        def step(t, carry):
            block(t, False)
            return carry
        lax.fori_loop(0, i, step, 0)
        block(i, True)
        for hp in range(SB_HEADS // 2):
            dq = jnp.where(lo, dq_acc[2 * hp], dq_acc[2 * hp + 1]) * SCALE
            dq_ref[:, hp * PAIR:(hp + 1) * PAIR] = dq.astype(BF16)

        @pl.when(i == nq - 1)
        def _():
            dk_ref[...] = dk_acc[...].astype(BF16)
            dv_ref[...] = dv_acc[...].astype(BF16)

    qtile = pl.BlockSpec((QB, SB_W), lambda i: (i, 0))
    whole = pl.BlockSpec((T, SB_W), lambda i: (0, 0))
    const = pl.BlockSpec((2 * QB, 2 * QB), lambda i: (0, 0))
    return _call(
        body, name=name, grid=(nq,), out_shape=(S((T, SB_W), BF16),) * 3,
        in_specs=[qtile, pl.BlockSpec((T, SB_W), lambda i: (0, 1)), pl.BlockSpec((T, SB_W), lambda i: (0, 2)), qtile,
                  pl.BlockSpec((QB, SB_HEADS * QB), lambda i: (i, 0)), const, const],
        out_specs=(qtile, whole, whole),
        scratch=[pltpu.VMEM((SB_HEADS, QB, PAIR), BF16), pltpu.VMEM((SB_HEADS, QB, PAIR), BF16),
                 pltpu.VMEM((2 * SB_HEADS, QB, QB), F32), pltpu.VMEM((SB_HEADS, QB, PAIR), F32),
                 pltpu.VMEM((T, SB_W), F32), pltpu.VMEM((T, SB_W), F32)],
        sem=("arbitrary",), args=(p, p, p, do, tot, upto, before), riders=riders)


def _t5_buckets():
    a = lax.broadcasted_iota(jnp.int32, (QB, QB), 0)
    c = lax.broadcasted_iota(jnp.int32, (QB, QB), 1)

    def bucket(dist):
        dist = jnp.maximum(dist, 0)
        max_exact = N_BUCKETS // 2
        d = jnp.maximum(dist, 1).astype(F32)
        large = max_exact + (jnp.log(d / max_exact) / math.log(MAX_DISTANCE / max_exact)
                             * (N_BUCKETS - max_exact)).astype(jnp.int32)
        large = jnp.minimum(large, N_BUCKETS - 1)
        return jnp.where(dist < max_exact, dist, large)

    return bucket(QB + a - c), bucket(a - c)


def _swa_common(i, kp_ref, kc_ref, vp_ref, vc_ref, bp_ref, bc_ref, rb_ref, bias_ref):
    lo, lane, row = _half_masks()

    @pl.when(i == 0)
    def _():
        for blk, b_ref in enumerate((bp_ref, bc_ref)):
            bk = b_ref[...]
            for h in range(8):
                acc = jnp.zeros((QB, QB), F32)
                for b in range(N_BUCKETS):
                    acc = jnp.where(bk == b, rb_ref[b, h], acc)
                bias_ref[h, blk] = acc

    band = [(lane > row) & (i > 0), lane <= row]

    def halves(ref):
        t = ref[...].astype(F32)
        sw = pltpu.roll(t, HEAD_DIM, 1)
        return [[jnp.where(lo, t, 0.0).astype(BF16), jnp.where(lo, 0.0, sw).astype(BF16)],
                [jnp.where(lo, sw, 0.0).astype(BF16), jnp.where(lo, 0.0, t).astype(BF16)]]

    ks = [halves(kp_ref), halves(kc_ref)]
    vs = [halves(vp_ref), halves(vc_ref)]
    return lo, band, ks, vs


def swa_fwd(p, sinks, rel_bias, bprev, bcur, name, riders=()):
    T = p.shape[0]
    nq = T // QB
    kcol, vcol = (3 * SB_W + SWA_W) // KV_W, (3 * SB_W + SWA_W) // KV_W + 1

    def body(q_ref, kp_ref, kc_ref, vp_ref, vc_ref, bp_ref, bc_ref, sink_ref, rb_ref, o_ref, lse_ref, bias_ref):
        i = pl.program_id(0)
        lo, band, ks, vs = _swa_common(i, kp_ref, kc_ref, vp_ref, vc_ref, bp_ref, bc_ref, rb_ref, bias_ref)
        for g in range(4):
            kh = g // 2
            q2 = q_ref[:, g * PAIR:(g + 1) * PAIR]
            outs = []
            for pos in range(2):
                h = 2 * g + pos
                sc = [jnp.where(band[b], _nt(q2, ks[b][kh][pos]) * SCALE + bias_ref[h, b], NEG_INF) for b in range(2)]
                sink = sink_ref[0, h]
                m = jnp.maximum(jnp.maximum(jnp.max(sc[0], axis=1, keepdims=True),
                                            jnp.max(sc[1], axis=1, keepdims=True)), sink)
                e = [jnp.exp(sc[b] - m) for b in range(2)]
                den = jnp.sum(e[0], axis=1, keepdims=True) + jnp.sum(e[1], axis=1, keepdims=True) + jnp.exp(sink - m)
                outs.append(_nn((e[0] / den).astype(BF16), vs[0][kh][pos]) + _nn((e[1] / den).astype(BF16), vs[1][kh][pos]))
                lse_ref[:, h * QB:(h + 1) * QB] = jnp.broadcast_to(m + jnp.log(den), (QB, QB))
            o_ref[:, g * PAIR:(g + 1) * PAIR] = outs[0] + outs[1]

    kv = lambda col, prev: pl.BlockSpec((QB, KV_W), (lambda i: (jnp.maximum(i - 1, 0), col)) if prev else (lambda i: (i, col)))
    full = pl.BlockSpec((QB, QB), lambda i: (0, 0))
    smem = pl.BlockSpec(memory_space=pltpu.SMEM)
    return _call(
        body, name=name, grid=(nq,), out_shape=(S((T, SWA_W), F32), S((T, 8 * QB), F32)),
        in_specs=[pl.BlockSpec((QB, SWA_W), lambda i: (i, 3)), kv(kcol, True), kv(kcol, False), kv(vcol, True), kv(vcol, False),
                  full, full, smem, smem],
        out_specs=(pl.BlockSpec((QB, SWA_W), lambda i: (i, 0)), pl.BlockSpec((QB, 8 * QB), lambda i: (i, 0))),
        scratch=[pltpu.VMEM((8, 2, QB, QB), F32)],
        sem=("arbitrary",), args=(p, p, p, p, p, bprev, bcur, sinks, rel_bias), riders=riders)


def swa_bwd(p, do, lse, sinks, rel_bias, bprev, bcur, name, riders=()):
    T = p.shape[0]
    nq = T // QB
    kcol, vcol = (3 * SB_W + SWA_W) // KV_W, (3 * SB_W + SWA_W) // KV_W + 1

    def body(q_ref, kp_ref, kc_ref, vp_ref, vc_ref, do_ref, lse_ref, bp_ref, bc_ref, sink_ref, rb_ref,
             dq_ref, dk_ref, dv_ref, dsink_ref, dsc_ref, bias_ref, dk_acc, dv_acc):
        i = pl.program_id(0)
        lo, band, ks, vs = _swa_common(i, kp_ref, kc_ref, vp_ref, vc_ref, bp_ref, bc_ref, rb_ref, bias_ref)

        @pl.when(i == 0)
        def _():
            dk_acc[...] = jnp.zeros_like(dk_acc)
            dv_acc[...] = jnp.zeros_like(dv_acc)
            dsc_ref[...] = jnp.zeros_like(dsc_ref)
            dsink_ref[...] = jnp.zeros_like(dsink_ref)

        lane1 = lax.broadcasted_iota(jnp.int32, (1, QB), 1)
        dsink = jnp.zeros((1, QB), F32)
        dk_parts = [[[None, None], [None, None]], [[None, None], [None, None]]]
        dv_parts = [[[None, None], [None, None]], [[None, None], [None, None]]]

        def add(parts, b, pos, kh, val):
            parts[b][pos][kh] = val if parts[b][pos][kh] is None else parts[b][pos][kh] + val

        for g in range(4):
            kh = g // 2
            q2 = q_ref[:, g * PAIR:(g + 1) * PAIR]
            q2f = q2.astype(F32)
            d2f = do_ref[:, g * PAIR:(g + 1) * PAIR]
            d2 = d2f.astype(BF16)
            dq = None
            for pos in range(2):
                h = 2 * g + pos
                keep = lo if pos == 0 else ~lo
                qh = jnp.where(keep, q2f, 0.0).astype(BF16)
                dh = jnp.where(keep, d2f, 0.0).astype(BF16)
                lse_h = lse_ref[:, h * QB:(h + 1) * QB]
                sink = sink_ref[0, h]
                pr = [jnp.exp(jnp.where(band[b], _nt(q2, ks[b][kh][pos]) * SCALE + bias_ref[h, b], NEG_INF) - lse_h)
                      for b in range(2)]
                dp = [_nt(d2, vs[b][kh][pos]) for b in range(2)]
                delta = jnp.sum(pr[0] * dp[0], axis=1, keepdims=True) + jnp.sum(pr[1] * dp[1], axis=1, keepdims=True)
                p_sink = jnp.exp(sink - lse_h[:, :1])
                dsink = dsink + jnp.where(lane1 == h, -jnp.sum(p_sink * delta), 0.0)
                for b in range(2):
                    dsc = pr[b] * (dp[b] - delta)
                    dsc_ref[h, b] += dsc
                    dzb = (dsc * SCALE).astype(BF16)
                    t = _nn(dzb, ks[b][kh][pos])
                    dq = t if dq is None else dq + t
                    add(dk_parts, b, pos, kh, _tn(dzb, qh))
                    add(dv_parts, b, pos, kh, _tn(pr[b].astype(BF16), dh))
            dq_ref[:, g * PAIR:(g + 1) * PAIR] = dq.astype(BF16)
        dsink_ref[...] += dsink

        def fold(parts, b):
            low = parts[b][0][0] + pltpu.roll(parts[b][1][0], HEAD_DIM, 1)
            high = parts[b][1][1] + pltpu.roll(parts[b][0][1], HEAD_DIM, 1)
            return jnp.where(lo, low, high)

        rp = pl.multiple_of(jnp.maximum(i - 1, 0) * QB, QB)
        rc = pl.multiple_of(i * QB, QB)
        dk_acc[pl.ds(rp, QB), :] += fold(dk_parts, 0)
        dv_acc[pl.ds(rp, QB), :] += fold(dv_parts, 0)
        dk_acc[pl.ds(rc, QB), :] += fold(dk_parts, 1)
        dv_acc[pl.ds(rc, QB), :] += fold(dv_parts, 1)

        @pl.when(i == nq - 1)
        def _():
            dk_ref[...] = dk_acc[...].astype(BF16)
            dv_ref[...] = dv_acc[...].astype(BF16)

    kv = lambda col, prev: pl.BlockSpec((QB, KV_W), (lambda i: (jnp.maximum(i - 1, 0), col)) if prev else (lambda i: (i, col)))
    full = pl.BlockSpec((QB, QB), lambda i: (0, 0))
    smem = pl.BlockSpec(memory_space=pltpu.SMEM)
    whole = lambda shape: pl.BlockSpec(shape, lambda i: (0,) * len(shape))
    return _call(
        body, name=name, grid=(nq,),
        out_shape=(S((T, SWA_W), BF16), S((T, KV_W), BF16), S((T, KV_W), BF16), S((1, QB), F32), S((8, 2, QB, QB), F32)),
        in_specs=[pl.BlockSpec((QB, SWA_W), lambda i: (i, 3)), kv(kcol, True), kv(kcol, False), kv(vcol, True), kv(vcol, False),
                  pl.BlockSpec((QB, SWA_W), lambda i: (i, 0)), pl.BlockSpec((QB, 8 * QB), lambda i: (i, 0)),
                  full, full, smem, smem],
        out_specs=(pl.BlockSpec((QB, SWA_W), lambda i: (i, 0)), whole((T, KV_W)), whole((T, KV_W)), whole((1, QB)),
                   whole((8, 2, QB, QB))),
        scratch=[pltpu.VMEM((8, 2, QB, QB), F32), pltpu.VMEM((T, KV_W), F32), pltpu.VMEM((T, KV_W), F32)],
        sem=("arbitrary",), args=(p, p, p, p, p, do, lse, bprev, bcur, sinks, rel_bias), riders=riders)


def mix_out_fwd(o_sb, o_sw, g_sb, g_sw, wout, h, g_next, name, riders=()):
    T, D = h.shape
    M = SB_W + SWA_W
    tm = _tile(T, 256)

    def body(a_ref, b_ref, ga_ref, gb_ref, w_ref, h_ref, gn_ref, mx_ref, o_ref, n_ref):
        mx_ref[:, :SB_W] = _rms(a_ref[...], ga_ref[...]).astype(BF16)
        mx_ref[:, SB_W:] = _rms(b_ref[...], gb_ref[...]).astype(BF16)
        out = h_ref[...] + _nn(mx_ref[...], w_ref[...])
        o_ref[...] = out
        n_ref[...] = _rms(out, gn_ref[...]).astype(BF16)

    row = lambda n: pl.BlockSpec((tm, n), lambda i: (i, 0))
    vec = lambda n: pl.BlockSpec((1, n), lambda i: (0, 0))
    return _call(
        body, name=name, grid=(T // tm,), out_shape=(S((T, M), BF16), S((T, D), F32), S((T, D), BF16)),
        in_specs=[row(SB_W), row(SWA_W), vec(SB_W), vec(SWA_W), pl.BlockSpec((M, D), lambda i: (0, 0)), row(D), vec(D)],
        out_specs=(row(M), row(D), row(D)),
        sem=("parallel",), args=(o_sb, o_sw, g_sb, g_sw, wout, h, g_next), riders=riders)


def loss_head(h, g, target, name):
    T, D = h.shape
    tm = _tile(T, 256)

    def body(h_ref, g_ref, t_ref, loss_ref, dh_ref, dhb_ref, dg_ref):
        @pl.when(pl.program_id(0) == 0)
        def _():
            loss_ref[...] = jnp.zeros_like(loss_ref)
            dg_ref[...] = jnp.zeros_like(dg_ref)
        x = h_ref[...]
        err = _rms(x, g_ref[...]) - t_ref[...]
        loss_ref[...] += jnp.full((1, QB), 0.5 * jnp.sum(jnp.mean(err * err, axis=-1)), F32)
        dx, dg = _rms_bwd(err / D, x, g_ref[...])
        dh_ref[...] = dx
        dhb_ref[...] = dx.astype(BF16)
        dg_ref[...] += dg

    row = pl.BlockSpec((tm, D), lambda i: (i, 0))
    vec = pl.BlockSpec((1, D), lambda i: (0, 0))
    return pl.pallas_call(
        body, name=name, grid=(T // tm,), out_shape=(S((1, QB), F32), S((T, D), F32), S((T, D), BF16), S((1, D), F32)),
        in_specs=[row, vec, row], out_specs=(pl.BlockSpec((1, QB), lambda i: (0, 0)), row, row, vec),
        compiler_params=_params(("arbitrary",)),
    )(h, g, target)


def ffn_down_bwd(dhb, wd, gate, up, name, riders=()):
    T, D = dhb.shape
    F = wd.shape[0]
    tr, tn = _tile(T, 512), _tile(F, 256)

    def body(d_ref, w_ref, g_ref, u_ref, o_ref):
        w = w_ref[...]
        for r in range(T // tr):
            rows = slice(r * tr, (r + 1) * tr)
            da = 0.5 * _nt(d_ref[rows, :], w)
            o_ref[0, rows, :] = (da * g_ref[rows, :].astype(F32)).astype(BF16)
            o_ref[1, rows, :] = (da * u_ref[rows, :].astype(F32)).astype(BF16)

    tile = pl.BlockSpec((T, tn), lambda j: (0, j))
    return _call(
        body, name=name, grid=(F // tn,), out_shape=S((2, T, F), BF16),
        in_specs=[pl.BlockSpec((T, D), lambda j: (0, 0)), pl.BlockSpec((tn, D), lambda j: (j, 0)), tile, tile],
        out_specs=pl.BlockSpec((2, T, tn), lambda j: (0, 0, j)),
        sem=("parallel",), args=(dhb, wd, gate, up), riders=riders)


def tn_matmul(xs, y, alpha, name, riders=()):
    B, T, N = xs.shape
    D = y.shape[1]
    tn = _tile(N, 256)

    def body(x_ref, y_ref, o_ref, ob_ref):
        o = alpha * _tn(x_ref[...], y_ref[...])
        o_ref[...] = o
        ob_ref[...] = o.astype(BF16)

    tile = pl.BlockSpec((None, tn, D), lambda s, j: (s, j, 0))
    return _call(
        body, name=name, grid=(B, N // tn), out_shape=(S((B, N, D), F32), S((B, N, D), BF16)),
        in_specs=[pl.BlockSpec((None, T, tn), lambda s, j: (s, 0, j)), pl.BlockSpec((T, D), lambda s, j: (0, 0))],
        out_specs=(tile, tile), sem=("parallel", "parallel"), args=(xs, y), riders=riders)


def nn_rms_bwd(xs, ws, h_in, g, dh, name, riders=()):
    B, T, K = xs.shape
    D = ws.shape[2]
    tm = _tile(T, 256)

    def body(x_ref, w_ref, h_ref, g_ref, d_ref, o_ref, ob_ref, dg_ref):
        @pl.when(pl.program_id(0) == 0)
        def _():
            dg_ref[...] = jnp.zeros_like(dg_ref)
        dn = _nn(x_ref[0], w_ref[0])
        for s in range(1, B):
            dn = dn + _nn(x_ref[s], w_ref[s])
        dx, dg = _rms_bwd(dn, h_ref[...], g_ref[...])
        out = d_ref[...] + dx
        o_ref[...] = out
        ob_ref[...] = out.astype(BF16)
        dg_ref[...] += dg

    row = pl.BlockSpec((tm, D), lambda i: (i, 0))
    vec = pl.BlockSpec((1, D), lambda i: (0, 0))
    return _call(
        body, name=name, grid=(T // tm,), out_shape=(S((T, D), F32), S((T, D), BF16), S((1, D), F32)),
        in_specs=[pl.BlockSpec((B, tm, K), lambda i: (0, i, 0)), pl.BlockSpec((B, K, D), lambda i: (0, 0, 0)), row, vec, row],
        out_specs=(row, row, vec),
        sem=("arbitrary",), args=(xs, ws, h_in, g, dh), riders=riders)


def mix_out_bwd(dhb, wout, o_sb, o_sw, g_sb, g_sw, name):
    T, D = dhb.shape
    tm = _tile(T, 256)

    def body(d_ref, w_ref, a_ref, b_ref, ga_ref, gb_ref, da_ref, db_ref, dga_ref, dgb_ref):
        @pl.when(pl.program_id(0) == 0)
        def _():
            dga_ref[...] = jnp.zeros_like(dga_ref)
            dgb_ref[...] = jnp.zeros_like(dgb_ref)
        dm = _nt(d_ref[...], w_ref[...])
        dxa, dga = _rms_bwd(dm[:, :SB_W], a_ref[...], ga_ref[...])
        dxb, dgb = _rms_bwd(dm[:, SB_W:], b_ref[...], gb_ref[...])
        da_ref[...] = dxa
        db_ref[...] = dxb
        dga_ref[...] += dga
        dgb_ref[...] += dgb

    row = lambda n: pl.BlockSpec((tm, n), lambda i: (i, 0))
    vec = lambda n: pl.BlockSpec((1, n), lambda i: (0, 0))
    return pl.pallas_call(
        body, name=name, grid=(T // tm,),
        out_shape=(S((T, SB_W), F32), S((T, SWA_W), F32), S((1, SB_W), F32), S((1, SWA_W), F32)),
        in_specs=[row(D), pl.BlockSpec((SB_W + SWA_W, D), lambda i: (0, 0)), row(SB_W), row(SWA_W), vec(SB_W), vec(SWA_W)],
        out_specs=(row(SB_W), row(SWA_W), vec(SB_W), vec(SWA_W)),
        compiler_params=_params(("arbitrary",)),
    )(dhb, wout, o_sb, o_sw, g_sb, g_sw)


def rel_bias_grad(dscs, bprev, bcur, name):
    n = len(dscs)

    def body(*refs):
        bp_ref, bc_ref, o_ref = refs[n], refs[n + 1], refs[n + 2]
        bks = [bp_ref[...], bc_ref[...]]
        row = lax.broadcasted_iota(jnp.int32, (N_BUCKETS, QB), 0)
        lane = lax.broadcasted_iota(jnp.int32, (N_BUCKETS, QB), 1)
        out = jnp.zeros((N_BUCKETS, QB), F32)
        for h in range(8):
            tot = [sum(refs[l][h, b] for l in range(n)) for b in range(2)]
            for b in range(N_BUCKETS):
                val = jnp.sum(jnp.where(bks[0] == b, tot[0], 0.0)) + jnp.sum(jnp.where(bks[1] == b, tot[1], 0.0))
                out = jnp.where((row == b) & (lane == h), val, out)
        o_ref[...] = out

    return pl.pallas_call(body, name=name, out_shape=S((N_BUCKETS, QB), F32), compiler_params=_params())(*dscs, bprev, bcur)


def _adamw(w, g, m, v):
    m = ADAM_B1 * m + (1.0 - ADAM_B1) * g
    v = ADAM_B2 * v + (1.0 - ADAM_B2) * (g * g)
    m_hat = m / (1.0 - ADAM_B1 ** ADAM_STEP)
    v_hat = v / (1.0 - ADAM_B2 ** ADAM_STEP)
    delta = -ADAM_LR * (m_hat / (jnp.sqrt(v_hat) + ADAM_EPS) + ADAM_WD * w)
    return delta, m, v


def adamw_rows(w, g, m, v, name):
    L, R, C = w.shape
    tr = _tile(R, 256)

    def body(w_ref, g_ref, m_ref, v_ref, d_ref, mo_ref, vo_ref):
        d, mn, vn = _adamw(w_ref[...], g_ref[...], m_ref[...], v_ref[...])
        d_ref[...] = d
        mo_ref[...] = mn
        vo_ref[...] = vn

    tile = pl.BlockSpec((None, tr, C), lambda l, i: (l, i, 0))
    return pl.pallas_call(
        body, name=name, grid=(L, R // tr), out_shape=(S((L, R, C), F32),) * 3,
        in_specs=[tile] * 4, out_specs=(tile,) * 3,
        compiler_params=_params(("parallel", "parallel")),
    )(w, g, m, v)


def adamw_small(w, gs, m, v, name):
    R, C = w.shape

    def body(w_ref, g_ref, m_ref, v_ref, go_ref, d_ref, mo_ref, vo_ref):
        g = g_ref[0]
        for k in range(1, N_DEV):
            g = g + g_ref[k]
        d, mn, vn = _adamw(w_ref[...], g, m_ref[...], v_ref[...])
        go_ref[...] = g
        d_ref[...] = d
        mo_ref[...] = mn
        vo_ref[...] = vn

    return pl.pallas_call(body, name=name, out_shape=(S((R, C), F32),) * 4, compiler_params=_params())(w, gs, m, v)


def kernel(x, norm_ffn1, w_ffn1_gu, w_ffn1_down, norm_mix, w_in, sinks, norm_out_sb, norm_out_swa, w_out, norm_ffn2, w_ffn2_gu, w_ffn2_down, rel_bias, norm_final, loss_target, m_norm_ffn1, m_w_ffn1_gu, m_w_ffn1_down, m_norm_mix, m_w_in, m_sinks, m_norm_out_sb, m_norm_out_swa, m_w_out, m_norm_ffn2, m_w_ffn2_gu, m_w_ffn2_down, m_rel_bias, m_norm_final, v_norm_ffn1, v_w_ffn1_gu, v_w_ffn1_down, v_norm_mix, v_w_in, v_sinks, v_norm_out_sb, v_norm_out_swa, v_w_out, v_norm_ffn2, v_w_ffn2_gu, v_w_ffn2_down, v_rel_bias, v_norm_final):
    L = norm_ffn1.shape[0]
    T, D = x.shape[1], x.shape[2]
    F = w_ffn1_down.shape[1] * N_DEV
    h = x.reshape(T, D)
    target = loss_target.reshape(T, D)
    after, upto, before = _tri_consts()
    bprev, bcur = _t5_buckets()

    local = {}
    for l in range(L):
        local[f"gu1_{l}"] = w_ffn1_gu[l].T.astype(BF16)
        local[f"d1_{l}"] = w_ffn1_down[l].astype(BF16)
        local[f"in_{l}"] = w_in[l].T.astype(BF16)
        local[f"out_{l}"] = w_out[l].astype(BF16)
        local[f"gu2_{l}"] = w_ffn2_gu[l].T.astype(BF16)
        local[f"d2_{l}"] = w_ffn2_down[l].astype(BF16)
    full, partial = {}, {}
    grads, chip_sum, recv_b = {}, {}, {}

    def run(fn, *args, ag=(), rs1=(), rs2=()):
        halves = lambda names: [n if isinstance(n, tuple) else (n, None) for n in names]
        ag, rs2 = [(n, k) for n, k in halves(ag) if n in local], halves(rs2)
        rows = lambda k, total: None if k is None else (k * (total // 2), total // 2)

        def second(n, k):
            sb = chip_sum[n][1]
            return scatter_second(sb, rows(k, sb.shape[1]), recv_b.get(n))

        riders = ([gather(local[n], rows(k, local[n].shape[0]), partial.get(n)) for n, k in ag]
                  + [scatter_first(grads[n][1]) for n in rs1] + [second(n, k) for n, k in rs2])
        if not riders:
            return fn(*args)
        outs, per = fn(*args, riders=riders)
        per = [p[0] for p in per]
        for n, k in ag:
            buf = per.pop(0)
            if k == 0:
                partial[n] = buf
            else:
                full[n] = buf.reshape(N_DEV * buf.shape[1], D)
        for n in rs1:
            chip_sum[n] = scatter_add(grads[n][0], per.pop(0), f"rs_add_{n}")
        for n, _ in rs2:
            recv_b[n] = per.pop(0)
        return outs

    def idle(name, riders=()):
        return None, idle_host(riders, name)

    gu = lambda n: full[n].reshape(2, F, D)
    slots = lambda pair: tuple(t.reshape(N_DEV, -1, D) for t in pair)
    vec = lambda a: a.reshape(1, -1)

    run(idle, "ag_head", ag=("gu1_0",))
    saved = []
    n_next = rms_cast(h, vec(norm_ffn1[0]), "rms_first")
    for l in range(L):
        nx = l + 1
        s = {"h0": h, "n1": n_next}
        s["gate1"], s["up1"], s["a1"] = run(ffn_up_fwd, s["n1"], gu(f"gu1_{l}"), f"ffn1_up{l}",
                                            ag=(f"d1_{l}",) + ((("in_0", 0),) if l == 0 else ()))
        h = run(ffn_down_fwd, s["a1"], full[f"d1_{l}"], h, None, f"ffn1_down{l}", ag=((f"in_{l}", 1),))
        s["h1"] = h
        s["n2"], s["p"] = mix_in_fwd(h, vec(norm_mix[l]), full[f"in_{l}"], f"mix_in{l}")
        s["o_sb"], s["tot"] = run(sb_attn_fwd, s["p"], after, f"sb_fwd{l}", ag=(f"out_{l}", f"gu2_{l}", f"d2_{l}"))
        s["o_sw"], s["lse"] = run(swa_fwd, s["p"], vec(sinks[l]), rel_bias, bprev, bcur, f"swa_fwd{l}", ag=((f"gu1_{nx}", 0),))
        s["mixed"], h, s["n3"] = run(mix_out_fwd, s["o_sb"], s["o_sw"], vec(norm_out_sb[l]), vec(norm_out_swa[l]),
                                     full[f"out_{l}"], h, vec(norm_ffn2[l]), f"mix_out{l}")
        s["h2"] = h
        s["gate2"], s["up2"], s["a2"] = run(ffn_up_fwd, s["n3"], gu(f"gu2_{l}"), f"ffn2_up{l}", ag=((f"gu1_{nx}", 1),))
        if nx < L:
            h, n_next = run(ffn_down_fwd, s["a2"], full[f"d2_{l}"], h, vec(norm_ffn1[nx]), f"ffn2_down{l}", ag=((f"in_{nx}", 0),))
        else:
            h = run(ffn_down_fwd, s["a2"], full[f"d2_{l}"], h, None, f"ffn2_down{l}")
        saved.append(s)

    loss_part, dh, dhb, dg_final = loss_head(h, vec(norm_final), target, "loss_head")
    loss = lax.psum(loss_part[0, 0], ("x", "y", "c"))

    small = {k: [None] * L for k in ("ffn1", "mix", "sinks", "osb", "osw", "ffn2", "dsc")}
    for l in reversed(range(L)):
        s = saved[l]

        def ffn_bwd(dh, dhb, tag, gate, up, a, n, h_in, g, r_down, r_dwgu, r_up):
            gu_n, d_n = f"gu{tag}_{l}", f"d{tag}_{l}"
            dgu = run(ffn_down_bwd, dhb, full[d_n], gate, up, f"ffn{tag}_down_bwd{l}", **r_down)
            grads[gu_n] = slots(run(tn_matmul, dgu, n, 1.0, f"ffn{tag}_dwgu{l}", **r_dwgu))
            grads[d_n] = slots(run(tn_matmul, a[None], dhb, 0.5, f"ffn{tag}_dwd{l}", rs1=(gu_n,)))
            return run(nn_rms_bwd, dgu, gu(gu_n), h_in, g, dh, f"ffn{tag}_up_bwd{l}", rs1=(d_n,), **r_up)

        later = l + 1 < L
        dh, dhb, small["ffn2"][l] = ffn_bwd(dh, dhb, 2, s["gate2"], s["up2"], s["a2"], s["n3"], s["h2"], vec(norm_ffn2[l]),
                                            dict(rs2=((f"gu1_{l + 1}", 1),) if later else ()),
                                            dict(rs2=(f"d1_{l + 1}",) if later else ()), {})
        do_sb, do_sw, small["osb"][l], small["osw"][l] = mix_out_bwd(
            dhb, full[f"out_{l}"], s["o_sb"], s["o_sw"], vec(norm_out_sb[l]), vec(norm_out_swa[l]), f"mix_out_bwd{l}")
        grads[f"out_{l}"] = slots(tn_matmul(s["mixed"][None], dhb, 1.0, f"dwout{l}"))
        dq_sb, dk_sb, dv_sb = run(sb_attn_bwd, s["p"], do_sb, s["tot"], upto, before, f"sb_bwd{l}",
                                  rs2=(f"gu2_{l}", f"d2_{l}"), rs1=(f"out_{l}",))
        dq_sw, dk_sw, dv_sw, small["sinks"][l], small["dsc"][l] = run(
            swa_bwd, s["p"], do_sw, s["lse"], vec(sinks[l]), rel_bias, bprev, bcur, f"swa_bwd{l}", rs2=(f"out_{l}",))
        dp = jnp.concatenate([dq_sb, dk_sb, dv_sb, dq_sw, dk_sw, dv_sw], axis=1)
        dh, dhb, small["mix"][l] = nn_rms_bwd(dp[None], full[f"in_{l}"][None], s["h1"], vec(norm_mix[l]), dh, f"mix_in_bwd{l}")
        grads[f"in_{l}"] = slots(tn_matmul(dp[None], s["n2"], 1.0, f"dwin{l}"))
        dh, dhb, small["ffn1"][l] = ffn_bwd(dh, dhb, 1, s["gate1"], s["up1"], s["a1"], s["n1"], s["h0"], vec(norm_ffn1[l]),
                                            dict(rs1=(f"in_{l}",)), dict(rs2=(f"in_{l}",)), dict(rs2=((f"gu1_{l}", 0),)))

    grad_x = dh.reshape(x.shape)
    run(idle, "rs_tail", rs2=(("gu1_0", 1), "d1_0"))

    def scattered(tag, transpose):
        out = []
        for l in range(L):
            n = f"{tag}_{l}"
            mine = scatter_sum(chip_sum[n][0], recv_b[n], f"rs_sum_{n}")
            out.append(mine.T if transpose else mine)
        return jnp.stack(out)

    g_gu1, g_d1, g_in = scattered("gu1", True), scattered("d1", False), scattered("in", True)
    g_out, g_gu2, g_d2 = scattered("out", False), scattered("gu2", True), scattered("d2", False)

    upd = {}
    for nm, w, g, m, v in (("gu1", w_ffn1_gu, g_gu1, m_w_ffn1_gu, v_w_ffn1_gu), ("d1", w_ffn1_down, g_d1, m_w_ffn1_down, v_w_ffn1_down),
                           ("in", w_in, g_in, m_w_in, v_w_in), ("out", w_out, g_out, m_w_out, v_w_out),
                           ("gu2", w_ffn2_gu, g_gu2, m_w_ffn2_gu, v_w_ffn2_gu), ("d2", w_ffn2_down, g_d2, m_w_ffn2_down, v_w_ffn2_down)):
        upd[nm] = (g,) + tuple(adamw_rows(w, g, m, v, f"adamw_{nm}"))

    d_rel = rel_bias_grad(small["dsc"], bprev, bcur, "rel_bias_grad")[:, :8]

    PW = max(D, SB_W + SWA_W)

    def pack(ffn1, mix, ffn2, final, osb, osw, snk, rel):
        wide = lambda a: jnp.pad(a.reshape(-1), (0, PW - a.size))
        rows = [wide(ffn1[l]) for l in range(L)] + [wide(mix[l]) for l in range(L)] + [wide(ffn2[l]) for l in range(L)]
        rows.append(wide(final))
        rows += [wide(jnp.concatenate([osb[l].reshape(-1), osw[l].reshape(-1)])) for l in range(L)]
        rows.append(wide(jnp.concatenate([snk[l].reshape(-1)[:8] for l in range(L)] + [rel.reshape(-1)])))
        arr = jnp.stack(rows)
        return jnp.pad(arr, ((0, (-arr.shape[0]) % 8), (0, 0)))

    def unpack(arr):
        ffn1, mix, ffn2 = arr[0:L, :D], arr[L:2 * L, :D], arr[2 * L:3 * L, :D]
        final = arr[3 * L, :D]
        ob = arr[3 * L + 1:4 * L + 1]
        tail = arr[4 * L + 1]
        return (ffn1, mix, tail[:8 * L].reshape(L, 8), ob[:, :SB_W], ob[:, SB_W:SB_W + SWA_W], ffn2,
                tail[8 * L:8 * L + N_BUCKETS * 8].reshape(N_BUCKETS, 8), final)

    g_small = pack(small["ffn1"], small["mix"], small["ffn2"], dg_final, small["osb"], small["osw"], small["sinks"], d_rel)
    w_small = pack(norm_ffn1, norm_mix, norm_ffn2, norm_final, norm_out_sb, norm_out_swa, sinks, rel_bias)
    m_small = pack(m_norm_ffn1, m_norm_mix, m_norm_ffn2, m_norm_final, m_norm_out_sb, m_norm_out_swa, m_sinks, m_rel_bias)
    v_small = pack(v_norm_ffn1, v_norm_mix, v_norm_ffn2, v_norm_final, v_norm_out_sb, v_norm_out_swa, v_sinks, v_rel_bias)
    gs_small = all_gather_rows(g_small, "ag_small")
    small_out = [unpack(a) for a in adamw_small(w_small, gs_small, m_small, v_small, "adamw_small")]

    def group(k):
        sm = small_out[k]
        return (sm[0], upd["gu1"][k], upd["d1"][k], sm[1], upd["in"][k], sm[2], sm[3], sm[4], upd["out"][k], sm[5],
                upd["gu2"][k], upd["d2"][k], sm[6], sm[7])

    return (loss, grad_x, *group(0), *group(1), *group(2), *group(3))
```

```python
import math

import jax
import jax.numpy as jnp
from jax import lax
from jax.experimental import pallas as pl
from jax.experimental.pallas import tpu as pltpu

F32 = jnp.float32
BF16 = jnp.bfloat16
S = jax.ShapeDtypeStruct

N_DEV = 8
HEAD_DIM = 64
SB_HEADS = 8
PAIR = 2 * HEAD_DIM
SB_W = 512
SWA_W = 512
KV_W = 128
IN_W = 3 * SB_W + SWA_W + 2 * KV_W
QB = 128
N_BUCKETS = 32
MAX_DISTANCE = 128
EPS = 1e-6
NEG_INF = -1e30
SCALE = HEAD_DIM ** -0.5

ADAM_LR = 0.001
ADAM_B1 = 0.9
ADAM_B2 = 0.999
ADAM_EPS = 1e-08
ADAM_WD = 0.01
ADAM_STEP = 10

VMEM_LIMIT = 56 * 1024 * 1024
MESH = pl.DeviceIdType.MESH


def _params(sem=None, vmem=VMEM_LIMIT):
    return pltpu.CompilerParams(dimension_semantics=sem, vmem_limit_bytes=vmem)


def _nn(a, b):
    return jnp.dot(a, b, preferred_element_type=F32)


def _nt(a, b):
    return lax.dot_general(a, b, (((1,), (1,)), ((), ())), preferred_element_type=F32)


def _tn(a, b):
    return lax.dot_general(a, b, (((0,), (0,)), ((), ())), preferred_element_type=F32)


def _tri(x, m2):
    hi = x.astype(BF16)
    lo = (x - hi.astype(F32)).astype(BF16)
    return _nn(jnp.concatenate([hi, lo], axis=1), m2)


def _rms(x, g):
    r = lax.rsqrt(jnp.mean(x * x, axis=-1, keepdims=True) + EPS)
    return x * r * g


def _rms_bwd(dy, x, g):
    r = lax.rsqrt(jnp.mean(x * x, axis=-1, keepdims=True) + EPS)
    xhat = x * r
    u = dy * g
    dx = r * (u - xhat * jnp.mean(u * xhat, axis=-1, keepdims=True))
    return dx, jnp.sum(dy * xhat, axis=0, keepdims=True)


def _softplus_logsig(z):
    sp = jnp.maximum(z, 0.0) + jnp.log(1.0 + jnp.exp(-jnp.abs(z)))
    return sp, z - sp


def _tile(n, want):
    t = min(n, want)
    while n % t:
        t //= 2
    return t


def _place():
    x, y, c = lax.axis_index("x"), lax.axis_index("y"), lax.axis_index("c")
    chips = [(1 - x, y), (x, 1 - y), (1 - x, 1 - y)]
    return x, y, c, chips


def all_gather_rows(v, name):
    R, C = v.shape

    def body(v_ref, out_ref, send_sems, recv_sems, local_sem):
        x, y, c, chips = _place()
        me, sibling = (x, y, c), (x, y, 1 - c)

        def slot(px, py, pc):
            return out_ref.at[4 * px + 2 * py + pc]

        def copy(k, block, to, src=None):
            return pltpu.make_async_remote_copy(
                src_ref=slot(*block) if src is None else src, dst_ref=slot(*block),
                send_sem=send_sems.at[k], recv_sem=recv_sems.at[k], device_id=to, device_id_type=MESH)

        mine = pltpu.make_async_copy(v_ref, slot(*me), local_sem)
        mine.start()
        first = [copy(0, me, sibling, src=v_ref)]
        first += [copy(1 + j, me, (*chip, c), src=v_ref) for j, chip in enumerate(chips)]
        for cp in first:
            cp.start()
        passed = [copy(4 + j, (*chip, c), sibling) for j, chip in enumerate(chips)]
        for j, chip in enumerate(chips):
            copy(1 + j, (*chip, c), me).wait_recv()
            passed[j].start()
        copy(0, sibling, me).wait_recv()
        for j, chip in enumerate(chips):
            copy(4 + j, (*chip, 1 - c), me).wait_recv()
        for cp in first + passed:
            cp.wait_send()
        mine.wait()

    return pl.pallas_call(
        body, name=name, out_shape=S((N_DEV, R, C), v.dtype),
        in_specs=[pl.BlockSpec(memory_space=pl.ANY)], out_specs=pl.BlockSpec(memory_space=pl.ANY),
        scratch_shapes=[pltpu.SemaphoreType.DMA((7,)), pltpu.SemaphoreType.DMA((7,)), pltpu.SemaphoreType.DMA],
    )(v)


class _Exchange:
    def __init__(self, ins, outs, n_first, n_second, n_local, plan, aliases=None):
        self.ins, self.outs, self.plan, self.aliases = list(ins), list(outs), plan, aliases or {}
        self.n_first, self.n_second, self.n_local = n_first, n_second, n_local

    def scratch(self):
        n = self.n_first + self.n_second
        return [pltpu.SemaphoreType.DMA((n,)), pltpu.SemaphoreType.DMA((n,)), pltpu.SemaphoreType.DMA((max(self.n_local, 1),))]

    def _copies(self, in_refs, out_refs, sems):
        send_sems, recv_sems, local_sems = sems
        first, second, local = self.plan(in_refs, out_refs)
        rem = [pltpu.make_async_remote_copy(src_ref=s, dst_ref=d, send_sem=send_sems.at[k], recv_sem=recv_sems.at[k],
                                            device_id=dev, device_id_type=MESH) for k, (s, d, dev) in enumerate(first + second)]
        loc = [pltpu.make_async_copy(s, d, local_sems.at[k]) for k, (s, d) in enumerate(local)]
        return rem[:len(first)], rem[len(first):], loc

    def start(self, in_refs, out_refs, sems):
        first, _, loc = self._copies(in_refs, out_refs, sems)
        for cp in first + loc:
            cp.start()

    def middle(self, in_refs, out_refs, sems):
        first, second, _ = self._copies(in_refs, out_refs, sems)
        if second:
            for cp in first:
                cp.wait_recv()
            for cp in second:
                cp.start()

    def finish(self, in_refs, out_refs, sems):
        first, second, loc = self._copies(in_refs, out_refs, sems)
        for cp in second if second else first:
            cp.wait_recv()
        for cp in first + second:
            cp.wait_send()
        for cp in loc:
            cp.wait()


def gather(v, rows=None, into=None):
    R, C = v.shape
    r0, nr = rows or (0, R)

    def plan(ins, outs):
        x, y, c, chips = _place()
        slot = lambda px, py, pc: outs[0].at[4 * px + 2 * py + pc, pl.ds(r0, nr), :]
        src, mine = ins[0].at[pl.ds(r0, nr), :], slot(x, y, c)
        first = [(src, mine, (x, y, 1 - c))] + [(src, mine, (*chip, c)) for chip in chips]
        second = [(slot(*chip, c), slot(*chip, c), (x, y, 1 - c)) for chip in chips]
        return first, second, [(src, mine)]

    if into is None:
        return _Exchange([v], [S((N_DEV, R, C), v.dtype)], 4, 3, 1, plan)
    return _Exchange([v, into], [S((N_DEV, R, C), v.dtype)], 4, 3, 1, plan, aliases={1: 0})


def scatter_first(gb):
    _, R, C = gb.shape

    def plan(ins, outs):
        x, y, c, chips = _place()
        owners = [(x, y)] + chips
        return [(ins[0].at[4 * px + 2 * py + (1 - c)], outs[0].at[j], (x, y, 1 - c)) for j, (px, py) in enumerate(owners)], [], []

    return _Exchange([gb], [S((4, R, C), BF16)], 4, 0, 0, plan)


def scatter_second(sb, rows=None, into=None):
    r0, nr = rows or (0, sb.shape[1])

    def plan(ins, outs):
        x, y, c, chips = _place()
        part = lambda ref, j: ref.at[j, pl.ds(r0, nr), :]
        return [(part(ins[0], j), part(outs[0], j), (*chips[j], c)) for j in range(3)], [], []

    if into is None:
        return _Exchange([sb], [S(sb.shape, BF16)], 3, 0, 0, plan)
    return _Exchange([sb, into], [S(sb.shape, BF16)], 3, 0, 0, plan, aliases={1: 0})


def _call(body, *, name, grid, in_specs, out_specs, out_shape, args, scratch=(), sem=None, riders=()):
    single = not isinstance(out_shape, (tuple, list))
    out_shape = (out_shape,) if single else tuple(out_shape)
    out_specs = (out_specs,) if single else tuple(out_specs)
    n_in, n_out, n_sc = len(in_specs), len(out_shape), len(scratch)
    if not riders:
        res = pl.pallas_call(body, name=name, grid=grid, in_specs=list(in_specs), out_specs=out_specs, out_shape=out_shape,
                             scratch_shapes=list(scratch), compiler_params=_params(sem))(*args)
        return res[0] if single else res
    r_ins = [a for r in riders for a in r.ins]
    r_outs = [o for r in riders for o in r.outs]
    r_scr = [s for r in riders for s in r.scratch()]
    aliases, i0, o0 = {}, n_in, n_out
    for r in riders:
        for a, b in r.aliases.items():
            aliases[i0 + a] = o0 + b
        i0, o0 = i0 + len(r.ins), o0 + len(r.outs)
    steps = math.prod(grid)

    def full(*refs):
        ins, rin = refs[:n_in], refs[n_in:n_in + len(r_ins)]
        pos = n_in + len(r_ins)
        outs, rout = refs[pos:pos + n_out], refs[pos + n_out:pos + n_out + len(r_outs)]
        pos += n_out + len(r_outs)
        sc, rsc = refs[pos:pos + n_sc], refs[pos + n_sc:]
        step = 0
        for d, n in enumerate(grid):
            step = step * n + pl.program_id(d)

        def each(method):
            i, o = 0, 0
            for k, r in enumerate(riders):
                getattr(r, method)(rin[i:i + len(r.ins)], rout[o:o + len(r.outs)], rsc[3 * k:3 * k + 3])
                i, o = i + len(r.ins), o + len(r.outs)

        @pl.when(step == 0)
        def _():
            each("start")
        body(*ins, *outs, *sc)

        @pl.when(step == max(steps - 1 - max(steps // 8, 1), 0))
        def _():
            each("middle")

        @pl.when(step == steps - 1)
        def _():
            each("finish")

    anywhere = pl.BlockSpec(memory_space=pl.ANY)
    res = pl.pallas_call(
        full, name=name, grid=grid, in_specs=list(in_specs) + [anywhere] * len(r_ins),
        out_specs=out_specs + (anywhere,) * len(r_outs), out_shape=out_shape + tuple(r_outs),
        scratch_shapes=list(scratch) + r_scr, input_output_aliases=aliases,
        compiler_params=_params(("arbitrary",) * len(grid)))(*args, *r_ins)
    host, rest, per = res[:n_out], list(res[n_out:]), []
    for r in riders:
        per.append(rest[:len(r.outs)])
        rest = rest[len(r.outs):]
    return (host[0] if single else tuple(host)), per


def idle_host(riders, name):
    def body(o_ref):
        o_ref[...] = jnp.zeros_like(o_ref)

    return _call(body, name=name, grid=(1,), in_specs=[], out_specs=pl.BlockSpec((8, QB), lambda i: (0, 0)),
                 out_shape=S((8, QB), F32), args=(), riders=riders)[1]


def _rows_tile(n, cap):
    return max(t for t in range(16, min(n, cap) + 1, 16) if n % t == 0)


def scatter_add(g, ra, name):
    _, R, C = g.shape
    tr = _rows_tile(R, 176)
    x, y, c, chips = _place()
    slots = jnp.stack([4 * px + 2 * py + c for px, py in [(x, y)] + chips]).astype(jnp.int32)

    def body(s_ref, g0, g1, g2, g3, ra_ref, own_ref, sb_ref):
        own_ref[...] = g0[...] + ra_ref[0].astype(F32)
        for j, gj in enumerate((g1, g2, g3)):
            sb_ref[j] = (gj[...] + ra_ref[j + 1].astype(F32)).astype(BF16)

    spec = pltpu.PrefetchScalarGridSpec(
        num_scalar_prefetch=1, grid=(R // tr,),
        in_specs=[pl.BlockSpec((None, tr, C), lambda i, s, j=j: (s[j], i, 0)) for j in range(4)]
        + [pl.BlockSpec((4, tr, C), lambda i, s: (0, i, 0))],
        out_specs=(pl.BlockSpec((tr, C), lambda i, s: (i, 0)), pl.BlockSpec((3, tr, C), lambda i, s: (0, i, 0))))
    return pl.pallas_call(body, name=name, grid_spec=spec, out_shape=(S((R, C), F32), S((3, R, C), BF16)),
                          compiler_params=_params(("parallel",)))(slots, g, g, g, g, ra)


def rms_cast(h, g, name):
    T, D = h.shape
    tm = _tile(T, 512)

    def body(h_ref, g_ref, n_ref):
        n_ref[...] = _rms(h_ref[...], g_ref[...]).astype(BF16)

    row = pl.BlockSpec((tm, D), lambda i: (i, 0))
    return _call(body, name=name, grid=(T // tm,), out_shape=S((T, D), BF16), in_specs=[row, pl.BlockSpec((1, D), lambda i: (0, 0))],
                 out_specs=row, sem=("parallel",), args=(h, g))


def ffn_up_fwd(n, wgu, name, riders=()):
    T, D = n.shape
    F = wgu.shape[1]
    tr, tn = _tile(T, 512), _tile(F, 256)

    def body(n_ref, wg_ref, wu_ref, dgate_ref, dup_ref, a_ref):
        wg, wu = wg_ref[...], wu_ref[...]
        for r in range(T // tr):
            rows = slice(r * tr, (r + 1) * tr)
            x = n_ref[rows, :]
            gate = _nt(x, wg)
            up = _nt(x, wu)
            s = jax.nn.sigmoid(gate)
            silu = gate * s
            dgate_ref[rows, :] = (up * (s * (1.0 + gate * (1.0 - s)))).astype(BF16)
            dup_ref[rows, :] = silu.astype(BF16)
            a_ref[rows, :] = (silu * up).astype(BF16)

    tile = pl.BlockSpec((T, tn), lambda j: (0, j))
    return _call(
        body, name=name, grid=(F // tn,), out_shape=(S((T, F), BF16),) * 3,
        in_specs=[pl.BlockSpec((T, D), lambda j: (0, 0)),
                  pl.BlockSpec((None, tn, D), lambda j: (0, j, 0)), pl.BlockSpec((None, tn, D), lambda j: (1, j, 0))],
        out_specs=(tile, tile, tile), sem=("parallel",), args=(n, wgu, wgu), riders=riders)


def ffn_down_fwd(a, wd, h, g_next, name, riders=()):
    T, F = a.shape
    D = wd.shape[1]
    tm = _tile(T, 256)

    def body(a_ref, w_ref, h_ref, *rest):
        out = h_ref[...] + 0.5 * _nn(a_ref[...], w_ref[...])
        if g_next is None:
            rest[0][...] = out
        else:
            g_ref, o_ref, n_ref = rest
            o_ref[...] = out
            n_ref[...] = _rms(out, g_ref[...]).astype(BF16)

    row = pl.BlockSpec((tm, D), lambda i: (i, 0))
    more = g_next is not None
    return _call(
        body, name=name, grid=(T // tm,), out_shape=(S((T, D), F32), S((T, D), BF16)) if more else S((T, D), F32),
        in_specs=[pl.BlockSpec((tm, F), lambda i: (i, 0)), pl.BlockSpec((F, D), lambda i: (0, 0)), row]
        + ([pl.BlockSpec((1, D), lambda i: (0, 0))] if more else []),
        out_specs=(row, row) if more else row,
        sem=("parallel",), args=(a, wd, h) + ((g_next,) if more else ()), riders=riders)


def mix_in_fwd(h, g, win, name):
    T, D = h.shape
    N = win.shape[0]
    tm = _tile(T, 256)

    def body(h_ref, g_ref, w_ref, n_ref, p_ref):
        n = _rms(h_ref[...], g_ref[...]).astype(BF16)
        n_ref[...] = n
        p_ref[...] = _nt(n, w_ref[...]).astype(BF16)

    return pl.pallas_call(
        body, name=name, grid=(T // tm,), out_shape=(S((T, D), BF16), S((T, N), BF16)),
        in_specs=[pl.BlockSpec((tm, D), lambda i: (i, 0)), pl.BlockSpec((1, D), lambda i: (0, 0)),
                  pl.BlockSpec((N, D), lambda i: (0, 0))],
        out_specs=(pl.BlockSpec((tm, D), lambda i: (i, 0)), pl.BlockSpec((tm, N), lambda i: (i, 0))),
        compiler_params=_params(("parallel",)),
    )(h, g, win)


def _tri_consts():
    r = lax.broadcasted_iota(jnp.int32, (QB, QB), 0)
    c = lax.broadcasted_iota(jnp.int32, (QB, QB), 1)
    ones = jnp.ones((QB, QB), BF16)

    def stacked(tri):
        m = jnp.concatenate([tri.astype(BF16), ones], axis=1)
        return jnp.concatenate([m, m], axis=0)

    return stacked(r > c), stacked(r <= c), stacked(r < c)


def _half_masks():
    lane = lax.broadcasted_iota(jnp.int32, (QB, PAIR), 1)
    row = lax.broadcasted_iota(jnp.int32, (QB, PAIR), 0)
    return lane < HEAD_DIM, lane, row


def sb_attn_fwd(p, after, name, riders=()):
    T = p.shape[0]
    nq = T // QB

    def body(q_ref, k_ref, v_ref, m_ref, o_ref, tot_ref, q_sc, acc_ref):
        i = pl.program_id(0)
        lo, lane, row = _half_masks()
        causal = lane < row
        for hp in range(SB_HEADS // 2):
            q2 = q_ref[:, hp * PAIR:(hp + 1) * PAIR].astype(F32) * SCALE
            q_sc[2 * hp] = jnp.where(lo, q2, 0.0).astype(BF16)
            q_sc[2 * hp + 1] = jnp.where(lo, 0.0, q2).astype(BF16)
        m2 = m_ref[...]

        def block(j, diag):
            r0 = pl.multiple_of(j * QB, QB)
            heads = range(SB_HEADS)
            k2 = [k_ref[pl.ds(r0, QB), hp * PAIR:(hp + 1) * PAIR] for hp in range(SB_HEADS // 2)]
            v2 = [v_ref[pl.ds(r0, QB), hp * PAIR:(hp + 1) * PAIR] for hp in range(SB_HEADS // 2)]
            z = [_nt(q_sc[h], k2[h // 2]) for h in heads]
            spls = [_softplus_logsig(z[h]) for h in heads]
            sp = [jnp.where(causal, spls[h][0], 0.0) if diag else spls[h][0] for h in heads]
            rr = [_tri(sp[h], m2) for h in heads]
            if diag:
                w = [jnp.where(causal, jnp.exp(spls[h][1] - rr[h][:, :QB]), 0.0) for h in heads]
                pv = [_nn(w[h].astype(BF16), v2[h // 2]) for h in heads]
                for h in heads:
                    acc_ref[h] = pv[h]
                    tot_ref[:, h * QB:(h + 1) * QB] = rr[h][:, QB:]
            else:
                c = [tot_ref[:, h * QB:(h + 1) * QB] for h in heads]
                w = [jnp.exp(spls[h][1] - (c[h] + rr[h][:, :QB])) for h in heads]
                pv = [_nn(w[h].astype(BF16), v2[h // 2]) for h in heads]
                for h in heads:
                    acc_ref[h] += pv[h]
                    tot_ref[:, h * QB:(h + 1) * QB] = c[h] + rr[h][:, QB:]

        block(i, True)

        def step(t, carry):
            block(i - 1 - t, False)
            return carry
        lax.fori_loop(0, i, step, 0)
        for hp in range(SB_HEADS // 2):
            o_ref[:, hp * PAIR:(hp + 1) * PAIR] = jnp.where(lo, acc_ref[2 * hp], acc_ref[2 * hp + 1])

    return _call(
        body, name=name, grid=(nq,), out_shape=(S((T, SB_W), F32), S((T, SB_HEADS * QB), F32)),
        in_specs=[pl.BlockSpec((QB, SB_W), lambda i: (i, 0)), pl.BlockSpec((T, SB_W), lambda i: (0, 1)),
                  pl.BlockSpec((T, SB_W), lambda i: (0, 2)), pl.BlockSpec((2 * QB, 2 * QB), lambda i: (0, 0))],
        out_specs=(pl.BlockSpec((QB, SB_W), lambda i: (i, 0)), pl.BlockSpec((QB, SB_HEADS * QB), lambda i: (i, 0))),
        scratch=[pltpu.VMEM((SB_HEADS, QB, PAIR), BF16), pltpu.VMEM((SB_HEADS, QB, PAIR), F32)],
        sem=("arbitrary",), args=(p, p, p, after), riders=riders)


def sb_attn_bwd(p, do, tot, upto, before, name, riders=()):
    T = p.shape[0]
    nq = T // QB

    def body(q_ref, k_ref, v_ref, do_ref, tot_ref, mp_ref, mg_ref, dq_ref, dk_ref, dv_ref,
             q_sc, d_sc, pg_sc, dq_acc, dk_acc, dv_acc):
        i = pl.program_id(0)
        lo, lane, row = _half_masks()
        causal = lane < row
        for hp in range(SB_HEADS // 2):
            q2 = q_ref[:, hp * PAIR:(hp + 1) * PAIR].astype(F32) * SCALE
            d2 = do_ref[:, hp * PAIR:(hp + 1) * PAIR]
            q_sc[2 * hp] = jnp.where(lo, q2, 0.0).astype(BF16)
            q_sc[2 * hp + 1] = jnp.where(lo, 0.0, q2).astype(BF16)
            d_sc[2 * hp] = jnp.where(lo, d2, 0.0).astype(BF16)
            d_sc[2 * hp + 1] = jnp.where(lo, 0.0, d2).astype(BF16)
        mp, mg = mp_ref[...], mg_ref[...]

        @pl.when(i == 0)
        def _():
            dk_acc[...] = jnp.zeros_like(dk_acc)
            dv_acc[...] = jnp.zeros_like(dv_acc)
        pg_sc[...] = jnp.zeros_like(pg_sc)
        dq_acc[...] = jnp.zeros_like(dq_acc)

        def block(j, diag):
            r0 = pl.multiple_of(j * QB, QB)
            heads = range(SB_HEADS)
            pairs = range(SB_HEADS // 2)
            k2 = [k_ref[pl.ds(r0, QB), hp * PAIR:(hp + 1) * PAIR] for hp in pairs]
            v2 = [v_ref[pl.ds(r0, QB), hp * PAIR:(hp + 1) * PAIR] for hp in pairs]
            z = [_nt(q_sc[h], k2[h // 2]) for h in heads]
            dw = [_nt(d_sc[h], v2[h // 2]) for h in heads]
            spls = [_softplus_logsig(z[h]) for h in heads]
            sp = [jnp.where(causal, spls[h][0], 0.0) if diag else spls[h][0] for h in heads]
            rr = [_tri(sp[h], mp) for h in heads]
            pc = [pg_sc[2 * h] for h in heads]
            w = [jnp.exp(spls[h][1] - (tot_ref[:, h * QB:(h + 1) * QB] - (pc[h] + rr[h][:, :QB]))) for h in heads]
            if diag:
                w = [jnp.where(causal, w[h], 0.0) for h in heads]
            gg = [dw[h] * w[h] for h in heads]
            rg = [_tri(gg[h], mg) for h in heads]
            gc = [pg_sc[2 * h + 1] for h in heads]
            dz = [gg[h] - (gg[h] + gc[h] + rg[h][:, :QB]) * jnp.exp(spls[h][1]) for h in heads]
            if diag:
                dz = [jnp.where(causal, dz[h], 0.0) for h in heads]
            dzb = [dz[h].astype(BF16) for h in heads]
            wb = [w[h].astype(BF16) for h in heads]
            dq = [_nn(dzb[h], k2[h // 2]) for h in heads]
            dk = [_tn(dzb[2 * hp], q_sc[2 * hp]) + _tn(dzb[2 * hp + 1], q_sc[2 * hp + 1]) for hp in pairs]
            dv = [_tn(wb[2 * hp], d_sc[2 * hp]) + _tn(wb[2 * hp + 1], d_sc[2 * hp + 1]) for hp in pairs]
            for h in heads:
                dq_acc[h] += dq[h]
                if not diag:
                    pg_sc[2 * h] = pc[h] + rr[h][:, QB:]
                    pg_sc[2 * h + 1] = gc[h] + rg[h][:, QB:]
            for hp in pairs:
                dk_acc[pl.ds(r0, QB), hp * PAIR:(hp + 1) * PAIR] += dk[hp]
                dv_acc[pl.ds(r0, QB), hp * PAIR:(hp + 1) * PAIR] += dv[hp]

        def step(t, carry):
            block(t, False)
            return carry
        lax.fori_loop(0, i, step, 0)
        block(i, True)
        for hp in range(SB_HEADS // 2):
            dq = jnp.where(lo, dq_acc[2 * hp], dq_acc[2 * hp + 1]) * SCALE
            dq_ref[:, hp * PAIR:(hp + 1) * PAIR] = dq.astype(BF16)

        @pl.when(i == nq - 1)
        def _():
            dk_ref[...] = dk_acc[...].astype(BF16)
            dv_ref[...] = dv_acc[...].astype(BF16)

    qtile = pl.BlockSpec((QB, SB_W), lambda i: (i, 0))
    whole = pl.BlockSpec((T, SB_W), lambda i: (0, 0))
    const = pl.BlockSpec((2 * QB, 2 * QB), lambda i: (0, 0))
    return _call(
        body, name=name, grid=(nq,), out_shape=(S((T, SB_W), BF16),) * 3,
        in_specs=[qtile, pl.BlockSpec((T, SB_W), lambda i: (0, 1)), pl.BlockSpec((T, SB_W), lambda i: (0, 2)), qtile,
                  pl.BlockSpec((QB, SB_HEADS * QB), lambda i: (i, 0)), const, const],
        out_specs=(qtile, whole, whole),
        scratch=[pltpu.VMEM((SB_HEADS, QB, PAIR), BF16), pltpu.VMEM((SB_HEADS, QB, PAIR), BF16),
                 pltpu.VMEM((2 * SB_HEADS, QB, QB), F32), pltpu.VMEM((SB_HEADS, QB, PAIR), F32),
                 pltpu.VMEM((T, SB_W), F32), pltpu.VMEM((T, SB_W), F32)],
        sem=("arbitrary",), args=(p, p, p, do, tot, upto, before), riders=riders)


def _t5_buckets():
    a = lax.broadcasted_iota(jnp.int32, (QB, QB), 0)
    c = lax.broadcasted_iota(jnp.int32, (QB, QB), 1)

    def bucket(dist):
        dist = jnp.maximum(dist, 0)
        max_exact = N_BUCKETS // 2
        d = jnp.maximum(dist, 1).astype(F32)
        large = max_exact + (jnp.log(d / max_exact) / math.log(MAX_DISTANCE / max_exact)
                             * (N_BUCKETS - max_exact)).astype(jnp.int32)
        large = jnp.minimum(large, N_BUCKETS - 1)
        return jnp.where(dist < max_exact, dist, large)

    return bucket(QB + a - c), bucket(a - c)


def _swa_common(i, kp_ref, kc_ref, vp_ref, vc_ref, bp_ref, bc_ref, rb_ref, bias_ref):
    lo, lane, row = _half_masks()

    @pl.when(i == 0)
    def _():
        for blk, b_ref in enumerate((bp_ref, bc_ref)):
            bk = b_ref[...]
            for h in range(8):
                acc = jnp.zeros((QB, QB), F32)
                for b in range(N_BUCKETS):
                    acc = jnp.where(bk == b, rb_ref[b, h], acc)
                bias_ref[h, blk] = acc

    band = [(lane > row) & (i > 0), lane <= row]

    def halves(ref):
        t = ref[...].astype(F32)
        sw = pltpu.roll(t, HEAD_DIM, 1)
        return [[jnp.where(lo, t, 0.0).astype(BF16), jnp.where(lo, 0.0, sw).astype(BF16)],
                [jnp.where(lo, sw, 0.0).astype(BF16), jnp.where(lo, 0.0, t).astype(BF16)]]

    ks = [halves(kp_ref), halves(kc_ref)]
    vs = [halves(vp_ref), halves(vc_ref)]
    return lo, band, ks, vs


def swa_fwd(p, sinks, rel_bias, bprev, bcur, name, riders=()):
    T = p.shape[0]
    nq = T // QB
    kcol, vcol = (3 * SB_W + SWA_W) // KV_W, (3 * SB_W + SWA_W) // KV_W + 1

    def body(q_ref, kp_ref, kc_ref, vp_ref, vc_ref, bp_ref, bc_ref, sink_ref, rb_ref, o_ref, lse_ref, bias_ref):
        i = pl.program_id(0)
        lo, band, ks, vs = _swa_common(i, kp_ref, kc_ref, vp_ref, vc_ref, bp_ref, bc_ref, rb_ref, bias_ref)
        for g in range(4):
            kh = g // 2
            q2 = q_ref[:, g * PAIR:(g + 1) * PAIR]
            outs = []
            for pos in range(2):
                h = 2 * g + pos
                sc = [jnp.where(band[b], _nt(q2, ks[b][kh][pos]) * SCALE + bias_ref[h, b], NEG_INF) for b in range(2)]
                sink = sink_ref[0, h]
                m = jnp.maximum(jnp.maximum(jnp.max(sc[0], axis=1, keepdims=True),
                                            jnp.max(sc[1], axis=1, keepdims=True)), sink)
                e = [jnp.exp(sc[b] - m) for b in range(2)]
                den = jnp.sum(e[0], axis=1, keepdims=True) + jnp.sum(e[1], axis=1, keepdims=True) + jnp.exp(sink - m)
                outs.append(_nn((e[0] / den).astype(BF16), vs[0][kh][pos]) + _nn((e[1] / den).astype(BF16), vs[1][kh][pos]))
                lse_ref[:, h * QB:(h + 1) * QB] = jnp.broadcast_to(m + jnp.log(den), (QB, QB))
            o_ref[:, g * PAIR:(g + 1) * PAIR] = outs[0] + outs[1]

    kv = lambda col, prev: pl.BlockSpec((QB, KV_W), (lambda i: (jnp.maximum(i - 1, 0), col)) if prev else (lambda i: (i, col)))
    full = pl.BlockSpec((QB, QB), lambda i: (0, 0))
    smem = pl.BlockSpec(memory_space=pltpu.SMEM)
    return _call(
        body, name=name, grid=(nq,), out_shape=(S((T, SWA_W), F32), S((T, 8 * QB), F32)),
        in_specs=[pl.BlockSpec((QB, SWA_W), lambda i: (i, 3)), kv(kcol, True), kv(kcol, False), kv(vcol, True), kv(vcol, False),
                  full, full, smem, smem],
        out_specs=(pl.BlockSpec((QB, SWA_W), lambda i: (i, 0)), pl.BlockSpec((QB, 8 * QB), lambda i: (i, 0))),
        scratch=[pltpu.VMEM((8, 2, QB, QB), F32)],
        sem=("arbitrary",), args=(p, p, p, p, p, bprev, bcur, sinks, rel_bias), riders=riders)


def swa_bwd(p, do, lse, sinks, rel_bias, bprev, bcur, name, riders=()):
    T = p.shape[0]
    nq = T // QB
    kcol, vcol = (3 * SB_W + SWA_W) // KV_W, (3 * SB_W + SWA_W) // KV_W + 1

    def body(q_ref, kp_ref, kc_ref, vp_ref, vc_ref, do_ref, lse_ref, bp_ref, bc_ref, sink_ref, rb_ref,
             dq_ref, dk_ref, dv_ref, dsink_ref, dsc_ref, bias_ref, dk_acc, dv_acc):
        i = pl.program_id(0)
        lo, band, ks, vs = _swa_common(i, kp_ref, kc_ref, vp_ref, vc_ref, bp_ref, bc_ref, rb_ref, bias_ref)

        @pl.when(i == 0)
        def _():
            dk_acc[...] = jnp.zeros_like(dk_acc)
            dv_acc[...] = jnp.zeros_like(dv_acc)
            dsc_ref[...] = jnp.zeros_like(dsc_ref)
            dsink_ref[...] = jnp.zeros_like(dsink_ref)

        lane1 = lax.broadcasted_iota(jnp.int32, (1, QB), 1)
        dsink = jnp.zeros((1, QB), F32)
        dk_parts = [[[None, None], [None, None]], [[None, None], [None, None]]]
        dv_parts = [[[None, None], [None, None]], [[None, None], [None, None]]]

        def add(parts, b, pos, kh, val):
            parts[b][pos][kh] = val if parts[b][pos][kh] is None else parts[b][pos][kh] + val

        for g in range(4):
            kh = g // 2
            q2 = q_ref[:, g * PAIR:(g + 1) * PAIR]
            q2f = q2.astype(F32)
            d2f = do_ref[:, g * PAIR:(g + 1) * PAIR]
            d2 = d2f.astype(BF16)
            dq = None
            for pos in range(2):
                h = 2 * g + pos
                keep = lo if pos == 0 else ~lo
                qh = jnp.where(keep, q2f, 0.0).astype(BF16)
                dh = jnp.where(keep, d2f, 0.0).astype(BF16)
                lse_h = lse_ref[:, h * QB:(h + 1) * QB]
                sink = sink_ref[0, h]
                pr = [jnp.exp(jnp.where(band[b], _nt(q2, ks[b][kh][pos]) * SCALE + bias_ref[h, b], NEG_INF) - lse_h)
                      for b in range(2)]
                dp = [_nt(d2, vs[b][kh][pos]) for b in range(2)]
                delta = jnp.sum(pr[0] * dp[0], axis=1, keepdims=True) + jnp.sum(pr[1] * dp[1], axis=1, keepdims=True)
                p_sink = jnp.exp(sink - lse_h[:, :1])
                dsink = dsink + jnp.where(lane1 == h, -jnp.sum(p_sink * delta), 0.0)
                for b in range(2):
                    dsc = pr[b] * (dp[b] - delta)
                    dsc_ref[h, b] += dsc
                    dzb = (dsc * SCALE).astype(BF16)
                    t = _nn(dzb, ks[b][kh][pos])
                    dq = t if dq is None else dq + t
                    add(dk_parts, b, pos, kh, _tn(dzb, qh))
                    add(dv_parts, b, pos, kh, _tn(pr[b].astype(BF16), dh))
            dq_ref[:, g * PAIR:(g + 1) * PAIR] = dq.astype(BF16)
        dsink_ref[...] += dsink

        def fold(parts, b):
            low = parts[b][0][0] + pltpu.roll(parts[b][1][0], HEAD_DIM, 1)
            high = parts[b][1][1] + pltpu.roll(parts[b][0][1], HEAD_DIM, 1)
            return jnp.where(lo, low, high)

        rp = pl.multiple_of(jnp.maximum(i - 1, 0) * QB, QB)
        rc = pl.multiple_of(i * QB, QB)
        dk_acc[pl.ds(rp, QB), :] += fold(dk_parts, 0)
        dv_acc[pl.ds(rp, QB), :] += fold(dv_parts, 0)
        dk_acc[pl.ds(rc, QB), :] += fold(dk_parts, 1)
        dv_acc[pl.ds(rc, QB), :] += fold(dv_parts, 1)

        @pl.when(i == nq - 1)
        def _():
            dk_ref[...] = dk_acc[...].astype(BF16)
            dv_ref[...] = dv_acc[...].astype(BF16)

    kv = lambda col, prev: pl.BlockSpec((QB, KV_W), (lambda i: (jnp.maximum(i - 1, 0), col)) if prev else (lambda i: (i, col)))
    full = pl.BlockSpec((QB, QB), lambda i: (0, 0))
    smem = pl.BlockSpec(memory_space=pltpu.SMEM)
    whole = lambda shape: pl.BlockSpec(shape, lambda i: (0,) * len(shape))
    return _call(
        body, name=name, grid=(nq,),
        out_shape=(S((T, SWA_W), BF16), S((T, KV_W), BF16), S((T, KV_W), BF16), S((1, QB), F32), S((8, 2, QB, QB), F32)),
        in_specs=[pl.BlockSpec((QB, SWA_W), lambda i: (i, 3)), kv(kcol, True), kv(kcol, False), kv(vcol, True), kv(vcol, False),
                  pl.BlockSpec((QB, SWA_W), lambda i: (i, 0)), pl.BlockSpec((QB, 8 * QB), lambda i: (i, 0)),
                  full, full, smem, smem],
        out_specs=(pl.BlockSpec((QB, SWA_W), lambda i: (i, 0)), whole((T, KV_W)), whole((T, KV_W)), whole((1, QB)),
                   whole((8, 2, QB, QB))),
        scratch=[pltpu.VMEM((8, 2, QB, QB), F32), pltpu.VMEM((T, KV_W), F32), pltpu.VMEM((T, KV_W), F32)],
        sem=("arbitrary",), args=(p, p, p, p, p, do, lse, bprev, bcur, sinks, rel_bias), riders=riders)


def mix_out_fwd(o_sb, o_sw, g_sb, g_sw, wout, h, g_next, name, riders=()):
    T, D = h.shape
    M = SB_W + SWA_W
    tm = _tile(T, 256)

    def body(a_ref, b_ref, ga_ref, gb_ref, w_ref, h_ref, gn_ref, mx_ref, o_ref, n_ref):
        mx_ref[:, :SB_W] = _rms(a_ref[...], ga_ref[...]).astype(BF16)
        mx_ref[:, SB_W:] = _rms(b_ref[...], gb_ref[...]).astype(BF16)
        out = h_ref[...] + _nn(mx_ref[...], w_ref[...])
        o_ref[...] = out
        n_ref[...] = _rms(out, gn_ref[...]).astype(BF16)

    row = lambda n: pl.BlockSpec((tm, n), lambda i: (i, 0))
    vec = lambda n: pl.BlockSpec((1, n), lambda i: (0, 0))
    return _call(
        body, name=name, grid=(T // tm,), out_shape=(S((T, M), BF16), S((T, D), F32), S((T, D), BF16)),
        in_specs=[row(SB_W), row(SWA_W), vec(SB_W), vec(SWA_W), pl.BlockSpec((M, D), lambda i: (0, 0)), row(D), vec(D)],
        out_specs=(row(M), row(D), row(D)),
        sem=("parallel",), args=(o_sb, o_sw, g_sb, g_sw, wout, h, g_next), riders=riders)


def loss_head(h, g, target, name):
    T, D = h.shape
    tm = _tile(T, 256)

    def body(h_ref, g_ref, t_ref, loss_ref, dh_ref, dhb_ref, dg_ref):
        @pl.when(pl.program_id(0) == 0)
        def _():
            loss_ref[...] = jnp.zeros_like(loss_ref)
            dg_ref[...] = jnp.zeros_like(dg_ref)
        x = h_ref[...]
        err = _rms(x, g_ref[...]) - t_ref[...]
        loss_ref[...] += jnp.full((1, QB), 0.5 * jnp.sum(jnp.mean(err * err, axis=-1)), F32)
        dx, dg = _rms_bwd(err / D, x, g_ref[...])
        dh_ref[...] = dx
        dhb_ref[...] = dx.astype(BF16)
        dg_ref[...] += dg

    row = pl.BlockSpec((tm, D), lambda i: (i, 0))
    vec = pl.BlockSpec((1, D), lambda i: (0, 0))
    return pl.pallas_call(
        body, name=name, grid=(T // tm,), out_shape=(S((1, QB), F32), S((T, D), F32), S((T, D), BF16), S((1, D), F32)),
        in_specs=[row, vec, row], out_specs=(pl.BlockSpec((1, QB), lambda i: (0, 0)), row, row, vec),
        compiler_params=_params(("arbitrary",)),
    )(h, g, target)


def ffn_down_bwd(dhb, wd, gate, up, name, riders=()):
    T, D = dhb.shape
    F = wd.shape[0]
    tr, tn = _tile(T, 512), _tile(F, 256)

    def body(d_ref, w_ref, g_ref, u_ref, o_ref):
        w = w_ref[...]
        for r in range(T // tr):
            rows = slice(r * tr, (r + 1) * tr)
            da = 0.5 * _nt(d_ref[rows, :], w)
            o_ref[0, rows, :] = (da * g_ref[rows, :].astype(F32)).astype(BF16)
            o_ref[1, rows, :] = (da * u_ref[rows, :].astype(F32)).astype(BF16)

    tile = pl.BlockSpec((T, tn), lambda j: (0, j))
    return _call(
        body, name=name, grid=(F // tn,), out_shape=S((2, T, F), BF16),
        in_specs=[pl.BlockSpec((T, D), lambda j: (0, 0)), pl.BlockSpec((tn, D), lambda j: (j, 0)), tile, tile],
        out_specs=pl.BlockSpec((2, T, tn), lambda j: (0, 0, j)),
        sem=("parallel",), args=(dhb, wd, gate, up), riders=riders)


def tn_matmul(xs, y, alpha, name, riders=()):
    B, T, N = xs.shape
    D = y.shape[1]
    tn = _tile(N, 256)

    def body(x_ref, y_ref, o_ref, ob_ref):
        o = alpha * _tn(x_ref[...], y_ref[...])
        o_ref[...] = o
        ob_ref[...] = o.astype(BF16)

    tile = pl.BlockSpec((None, tn, D), lambda s, j: (s, j, 0))
    return _call(
        body, name=name, grid=(B, N // tn), out_shape=(S((B, N, D), F32), S((B, N, D), BF16)),
        in_specs=[pl.BlockSpec((None, T, tn), lambda s, j: (s, 0, j)), pl.BlockSpec((T, D), lambda s, j: (0, 0))],
        out_specs=(tile, tile), sem=("parallel", "parallel"), args=(xs, y), riders=riders)


def nn_rms_bwd(xs, ws, h_in, g, dh, name, riders=()):
    B, T, K = xs.shape
    D = ws.shape[2]
    tm = _tile(T, 256)

    def body(x_ref, w_ref, h_ref, g_ref, d_ref, o_ref, ob_ref, dg_ref):
        @pl.when(pl.program_id(0) == 0)
        def _():
            dg_ref[...] = jnp.zeros_like(dg_ref)
        dn = _nn(x_ref[0], w_ref[0])
        for s in range(1, B):
            dn = dn + _nn(x_ref[s], w_ref[s])
        dx, dg = _rms_bwd(dn, h_ref[...], g_ref[...])
        out = d_ref[...] + dx
        o_ref[...] = out
        ob_ref[...] = out.astype(BF16)
        dg_ref[...] += dg

    row = pl.BlockSpec((tm, D), lambda i: (i, 0))
    vec = pl.BlockSpec((1, D), lambda i: (0, 0))
    return _call(
        body, name=name, grid=(T // tm,), out_shape=(S((T, D), F32), S((T, D), BF16), S((1, D), F32)),
        in_specs=[pl.BlockSpec((B, tm, K), lambda i: (0, i, 0)), pl.BlockSpec((B, K, D), lambda i: (0, 0, 0)), row, vec, row],
        out_specs=(row, row, vec),
        sem=("arbitrary",), args=(xs, ws, h_in, g, dh), riders=riders)


def mix_out_bwd(dhb, wout, o_sb, o_sw, g_sb, g_sw, name):
    T, D = dhb.shape
    tm = _tile(T, 256)

    def body(d_ref, w_ref, a_ref, b_ref, ga_ref, gb_ref, da_ref, db_ref, dga_ref, dgb_ref):
        @pl.when(pl.program_id(0) == 0)
        def _():
            dga_ref[...] = jnp.zeros_like(dga_ref)
            dgb_ref[...] = jnp.zeros_like(dgb_ref)
        dm = _nt(d_ref[...], w_ref[...])
        dxa, dga = _rms_bwd(dm[:, :SB_W], a_ref[...], ga_ref[...])
        dxb, dgb = _rms_bwd(dm[:, SB_W:], b_ref[...], gb_ref[...])
        da_ref[...] = dxa
        db_ref[...] = dxb
        dga_ref[...] += dga
        dgb_ref[...] += dgb

    row = lambda n: pl.BlockSpec((tm, n), lambda i: (i, 0))
    vec = lambda n: pl.BlockSpec((1, n), lambda i: (0, 0))
    return pl.pallas_call(
        body, name=name, grid=(T // tm,),
        out_shape=(S((T, SB_W), F32), S((T, SWA_W), F32), S((1, SB_W), F32), S((1, SWA_W), F32)),
        in_specs=[row(D), pl.BlockSpec((SB_W + SWA_W, D), lambda i: (0, 0)), row(SB_W), row(SWA_W), vec(SB_W), vec(SWA_W)],
        out_specs=(row(SB_W), row(SWA_W), vec(SB_W), vec(SWA_W)),
        compiler_params=_params(("arbitrary",)),
    )(dhb, wout, o_sb, o_sw, g_sb, g_sw)


def rel_bias_grad(dscs, bprev, bcur, name):
    n = len(dscs)

    def body(*refs):
        bp_ref, bc_ref, o_ref = refs[n], refs[n + 1], refs[n + 2]
        bks = [bp_ref[...], bc_ref[...]]
        row = lax.broadcasted_iota(jnp.int32, (N_BUCKETS, QB), 0)
        lane = lax.broadcasted_iota(jnp.int32, (N_BUCKETS, QB), 1)
        out = jnp.zeros((N_BUCKETS, QB), F32)
        for h in range(8):
            tot = [sum(refs[l][h, b] for l in range(n)) for b in range(2)]
            for b in range(N_BUCKETS):
                val = jnp.sum(jnp.where(bks[0] == b, tot[0], 0.0)) + jnp.sum(jnp.where(bks[1] == b, tot[1], 0.0))
                out = jnp.where((row == b) & (lane == h), val, out)
        o_ref[...] = out

    return pl.pallas_call(body, name=name, out_shape=S((N_BUCKETS, QB), F32), compiler_params=_params())(*dscs, bprev, bcur)


def _adamw(w, g, m, v):
    m = ADAM_B1 * m + (1.0 - ADAM_B1) * g
    v = ADAM_B2 * v + (1.0 - ADAM_B2) * (g * g)
    m_hat = m / (1.0 - ADAM_B1 ** ADAM_STEP)
    v_hat = v / (1.0 - ADAM_B2 ** ADAM_STEP)
    delta = -ADAM_LR * (m_hat / (jnp.sqrt(v_hat) + ADAM_EPS) + ADAM_WD * w)
    return delta, m, v


def adamw_scattered(w, m, v, owns, others, name):
    L, R, C = w.shape
    tr = _rows_tile(R, 176)

    def body(w_ref, m_ref, v_ref, *rest):
        own_refs, other_refs = rest[:L], rest[L:2 * L]
        g_ref, d_ref, mo_ref, vo_ref = rest[2 * L:]
        layer = pl.program_id(0)

        def grad(k):
            o = other_refs[k]
            return own_refs[k][...] + o[0].astype(F32) + o[1].astype(F32) + o[2].astype(F32)

        g = grad(0)
        for k in range(1, L):
            g = jnp.where(layer == k, grad(k), g)
        d, mn, vn = _adamw(w_ref[...], g, m_ref[...], v_ref[...])
        g_ref[...] = g
        d_ref[...] = d
        mo_ref[...] = mn
        vo_ref[...] = vn

    tile = pl.BlockSpec((None, tr, C), lambda l, i: (l, i, 0))
    return pl.pallas_call(
        body, name=name, grid=(L, R // tr), out_shape=(S((L, R, C), F32),) * 4,
        in_specs=[tile] * 3 + [pl.BlockSpec((tr, C), lambda l, i: (i, 0))] * L + [pl.BlockSpec((3, tr, C), lambda l, i: (0, i, 0))] * L,
        out_specs=(tile,) * 4, compiler_params=_params(("parallel", "parallel")),
    )(w, m, v, *owns, *others)


def adamw_small(w, gs, m, v, name):
    R, C = w.shape

    def body(w_ref, g_ref, m_ref, v_ref, go_ref, d_ref, mo_ref, vo_ref):
        g = g_ref[0]
        for k in range(1, N_DEV):
            g = g + g_ref[k]
        d, mn, vn = _adamw(w_ref[...], g, m_ref[...], v_ref[...])
        go_ref[...] = g
        d_ref[...] = d
        mo_ref[...] = mn
        vo_ref[...] = vn

    return pl.pallas_call(body, name=name, out_shape=(S((R, C), F32),) * 4, compiler_params=_params())(w, gs, m, v)


def kernel(x, norm_ffn1, w_ffn1_gu, w_ffn1_down, norm_mix, w_in, sinks, norm_out_sb, norm_out_swa, w_out, norm_ffn2, w_ffn2_gu, w_ffn2_down, rel_bias, norm_final, loss_target, m_norm_ffn1, m_w_ffn1_gu, m_w_ffn1_down, m_norm_mix, m_w_in, m_sinks, m_norm_out_sb, m_norm_out_swa, m_w_out, m_norm_ffn2, m_w_ffn2_gu, m_w_ffn2_down, m_rel_bias, m_norm_final, v_norm_ffn1, v_w_ffn1_gu, v_w_ffn1_down, v_norm_mix, v_w_in, v_sinks, v_norm_out_sb, v_norm_out_swa, v_w_out, v_norm_ffn2, v_w_ffn2_gu, v_w_ffn2_down, v_rel_bias, v_norm_final):
    L = norm_ffn1.shape[0]
    T, D = x.shape[1], x.shape[2]
    F = w_ffn1_down.shape[1] * N_DEV
    h = x.reshape(T, D)
    target = loss_target.reshape(T, D)
    after, upto, before = _tri_consts()
    bprev, bcur = _t5_buckets()

    local = {}
    for l in range(L):
        local[f"gu1_{l}"] = w_ffn1_gu[l].T.astype(BF16)
        local[f"d1_{l}"] = w_ffn1_down[l].astype(BF16)
        local[f"in_{l}"] = w_in[l].T.astype(BF16)
        local[f"out_{l}"] = w_out[l].astype(BF16)
        local[f"gu2_{l}"] = w_ffn2_gu[l].T.astype(BF16)
        local[f"d2_{l}"] = w_ffn2_down[l].astype(BF16)
    full, partial = {}, {}
    grads, chip_sum, recv_b = {}, {}, {}

    def run(fn, *args, ag=(), rs1=(), rs2=()):
        halves = lambda names: [n if isinstance(n, tuple) else (n, None) for n in names]
        ag, rs2 = [(n, k) for n, k in halves(ag) if n in local], halves(rs2)
        rows = lambda k, total: None if k is None else (k * (total // 2), total // 2)

        def second(n, k):
            sb = chip_sum[n][1]
            return scatter_second(sb, rows(k, sb.shape[1]), recv_b.get(n))

        riders = ([gather(local[n], rows(k, local[n].shape[0]), partial.get(n)) for n, k in ag]
                  + [scatter_first(grads[n][1]) for n in rs1] + [second(n, k) for n, k in rs2])
        if not riders:
            return fn(*args)
        outs, per = fn(*args, riders=riders)
        per = [p[0] for p in per]
        for n, k in ag:
            buf = per.pop(0)
            if k == 0:
                partial[n] = buf
            else:
                full[n] = buf.reshape(N_DEV * buf.shape[1], D)
        for n in rs1:
            chip_sum[n] = scatter_add(grads[n][0], per.pop(0), f"rs_add_{n}")
        for n, _ in rs2:
            recv_b[n] = per.pop(0)
        return outs

    def idle(name, riders=()):
        return None, idle_host(riders, name)

    gu = lambda n: full[n].reshape(2, F, D)
    slots = lambda pair: tuple(t.reshape(N_DEV, -1, D) for t in pair)
    vec = lambda a: a.reshape(1, -1)

    run(idle, "ag_head", ag=("gu1_0",))
    saved = []
    n_next = rms_cast(h, vec(norm_ffn1[0]), "rms_first")
    for l in range(L):
        nx = l + 1
        s = {"h0": h, "n1": n_next}
        s["gate1"], s["up1"], s["a1"] = run(ffn_up_fwd, s["n1"], gu(f"gu1_{l}"), f"ffn1_up{l}",
                                            ag=(f"d1_{l}",) + ((("in_0", 0),) if l == 0 else ()))
        h = run(ffn_down_fwd, s["a1"], full[f"d1_{l}"], h, None, f"ffn1_down{l}", ag=((f"in_{l}", 1),))
        s["h1"] = h
        s["n2"], s["p"] = mix_in_fwd(h, vec(norm_mix[l]), full[f"in_{l}"], f"mix_in{l}")
        s["o_sb"], s["tot"] = run(sb_attn_fwd, s["p"], after, f"sb_fwd{l}", ag=(f"out_{l}", f"gu2_{l}", f"d2_{l}"))
        s["o_sw"], s["lse"] = run(swa_fwd, s["p"], vec(sinks[l]), rel_bias, bprev, bcur, f"swa_fwd{l}", ag=((f"gu1_{nx}", 0),))
        s["mixed"], h, s["n3"] = run(mix_out_fwd, s["o_sb"], s["o_sw"], vec(norm_out_sb[l]), vec(norm_out_swa[l]),
                                     full[f"out_{l}"], h, vec(norm_ffn2[l]), f"mix_out{l}")
        s["h2"] = h
        s["gate2"], s["up2"], s["a2"] = run(ffn_up_fwd, s["n3"], gu(f"gu2_{l}"), f"ffn2_up{l}", ag=((f"gu1_{nx}", 1),))
        if nx < L:
            h, n_next = run(ffn_down_fwd, s["a2"], full[f"d2_{l}"], h, vec(norm_ffn1[nx]), f"ffn2_down{l}", ag=((f"in_{nx}", 0),))
        else:
            h = run(ffn_down_fwd, s["a2"], full[f"d2_{l}"], h, None, f"ffn2_down{l}")
        saved.append(s)

    loss_part, dh, dhb, dg_final = loss_head(h, vec(norm_final), target, "loss_head")
    loss = lax.psum(loss_part[0, 0], ("x", "y", "c"))

    small = {k: [None] * L for k in ("ffn1", "mix", "sinks", "osb", "osw", "ffn2", "dsc")}
    for l in reversed(range(L)):
        s = saved[l]

        def ffn_bwd(dh, dhb, tag, gate, up, a, n, h_in, g, r_down, r_dwgu, r_up):
            gu_n, d_n = f"gu{tag}_{l}", f"d{tag}_{l}"
            dgu = run(ffn_down_bwd, dhb, full[d_n], gate, up, f"ffn{tag}_down_bwd{l}", **r_down)
            grads[gu_n] = slots(run(tn_matmul, dgu, n, 1.0, f"ffn{tag}_dwgu{l}", **r_dwgu))
            grads[d_n] = slots(run(tn_matmul, a[None], dhb, 0.5, f"ffn{tag}_dwd{l}", rs1=(gu_n,)))
            return run(nn_rms_bwd, dgu, gu(gu_n), h_in, g, dh, f"ffn{tag}_up_bwd{l}", rs1=(d_n,), **r_up)

        later = l + 1 < L
        dh, dhb, small["ffn2"][l] = ffn_bwd(dh, dhb, 2, s["gate2"], s["up2"], s["a2"], s["n3"], s["h2"], vec(norm_ffn2[l]),
                                            dict(rs2=((f"gu1_{l + 1}", 1),) if later else ()),
                                            dict(rs2=(f"d1_{l + 1}",) if later else ()), {})
        do_sb, do_sw, small["osb"][l], small["osw"][l] = mix_out_bwd(
            dhb, full[f"out_{l}"], s["o_sb"], s["o_sw"], vec(norm_out_sb[l]), vec(norm_out_swa[l]), f"mix_out_bwd{l}")
        grads[f"out_{l}"] = slots(tn_matmul(s["mixed"][None], dhb, 1.0, f"dwout{l}"))
        dq_sb, dk_sb, dv_sb = run(sb_attn_bwd, s["p"], do_sb, s["tot"], upto, before, f"sb_bwd{l}",
                                  rs2=(f"gu2_{l}", f"d2_{l}"), rs1=(f"out_{l}",))
        dq_sw, dk_sw, dv_sw, small["sinks"][l], small["dsc"][l] = run(
            swa_bwd, s["p"], do_sw, s["lse"], vec(sinks[l]), rel_bias, bprev, bcur, f"swa_bwd{l}", rs2=(f"out_{l}",))
        dp = jnp.concatenate([dq_sb, dk_sb, dv_sb, dq_sw, dk_sw, dv_sw], axis=1)
        dh, dhb, small["mix"][l] = nn_rms_bwd(dp[None], full[f"in_{l}"][None], s["h1"], vec(norm_mix[l]), dh, f"mix_in_bwd{l}")
        grads[f"in_{l}"] = slots(tn_matmul(dp[None], s["n2"], 1.0, f"dwin{l}"))
        dh, dhb, small["ffn1"][l] = ffn_bwd(dh, dhb, 1, s["gate1"], s["up1"], s["a1"], s["n1"], s["h0"], vec(norm_ffn1[l]),
                                            dict(rs1=(f"in_{l}",)), dict(rs2=(f"in_{l}",)), dict(rs2=((f"gu1_{l}", 0),)))

    grad_x = dh.reshape(x.shape)
    run(idle, "rs_tail", rs2=(("gu1_0", 1), "d1_0"))

    upd = {}
    for nm, w, m, v, transposed in (("gu1", w_ffn1_gu, m_w_ffn1_gu, v_w_ffn1_gu, True), ("d1", w_ffn1_down, m_w_ffn1_down, v_w_ffn1_down, False),
                                    ("in", w_in, m_w_in, v_w_in, True), ("out", w_out, m_w_out, v_w_out, False),
                                    ("gu2", w_ffn2_gu, m_w_ffn2_gu, v_w_ffn2_gu, True), ("d2", w_ffn2_down, m_w_ffn2_down, v_w_ffn2_down, False)):
        turn = (lambda a: jnp.swapaxes(a, 1, 2)) if transposed else (lambda a: a)
        names = [f"{nm}_{l}" for l in range(L)]
        res = adamw_scattered(turn(w), turn(m), turn(v), [chip_sum[n][0] for n in names], [recv_b[n] for n in names], f"adamw_{nm}")
        upd[nm] = tuple(turn(r) for r in res)

    d_rel = rel_bias_grad(small["dsc"], bprev, bcur, "rel_bias_grad")[:, :8]

    PW = max(D, SB_W + SWA_W)

    def pack(ffn1, mix, ffn2, final, osb, osw, snk, rel):
        wide = lambda a: jnp.pad(a.reshape(-1), (0, PW - a.size))
        rows = [wide(ffn1[l]) for l in range(L)] + [wide(mix[l]) for l in range(L)] + [wide(ffn2[l]) for l in range(L)]
        rows.append(wide(final))
        rows += [wide(jnp.concatenate([osb[l].reshape(-1), osw[l].reshape(-1)])) for l in range(L)]
        rows.append(wide(jnp.concatenate([snk[l].reshape(-1)[:8] for l in range(L)] + [rel.reshape(-1)])))
        arr = jnp.stack(rows)
        return jnp.pad(arr, ((0, (-arr.shape[0]) % 8), (0, 0)))

    def unpack(arr):
        ffn1, mix, ffn2 = arr[0:L, :D], arr[L:2 * L, :D], arr[2 * L:3 * L, :D]
        final = arr[3 * L, :D]
        ob = arr[3 * L + 1:4 * L + 1]
        tail = arr[4 * L + 1]
        return (ffn1, mix, tail[:8 * L].reshape(L, 8), ob[:, :SB_W], ob[:, SB_W:SB_W + SWA_W], ffn2,
                tail[8 * L:8 * L + N_BUCKETS * 8].reshape(N_BUCKETS, 8), final)

    g_small = pack(small["ffn1"], small["mix"], small["ffn2"], dg_final, small["osb"], small["osw"], small["sinks"], d_rel)
    w_small = pack(norm_ffn1, norm_mix, norm_ffn2, norm_final, norm_out_sb, norm_out_swa, sinks, rel_bias)
    m_small = pack(m_norm_ffn1, m_norm_mix, m_norm_ffn2, m_norm_final, m_norm_out_sb, m_norm_out_swa, m_sinks, m_rel_bias)
    v_small = pack(v_norm_ffn1, v_norm_mix, v_norm_ffn2, v_norm_final, v_norm_out_sb, v_norm_out_swa, v_sinks, v_rel_bias)
    gs_small = all_gather_rows(g_small, "ag_small")
    small_out = [unpack(a) for a in adamw_small(w_small, gs_small, m_small, v_small, "adamw_small")]

    def group(k):
        sm = small_out[k]
        return (sm[0], upd["gu1"][k], upd["d1"][k], sm[1], upd["in"][k], sm[2], sm[3], sm[4], upd["out"][k], sm[5],
                upd["gu2"][k], upd["d2"][k], sm[6], sm[7])

    return (loss, grad_x, *group(0), *group(1), *group(2), *group(3))
```

```python
import math

import jax
import jax.numpy as jnp
from jax import lax
from jax.experimental import pallas as pl
from jax.experimental.pallas import tpu as pltpu

F32 = jnp.float32
BF16 = jnp.bfloat16
S = jax.ShapeDtypeStruct

N_DEV = 8
HEAD_DIM = 64
SB_HEADS = 8
PAIR = 2 * HEAD_DIM
SB_W = 512
SWA_W = 512
KV_W = 128
IN_W = 3 * SB_W + SWA_W + 2 * KV_W
QB = 128
N_BUCKETS = 32
MAX_DISTANCE = 128
EPS = 1e-6
NEG_INF = -1e30
SCALE = HEAD_DIM ** -0.5

ADAM_LR = 0.001
ADAM_B1 = 0.9
ADAM_B2 = 0.999
ADAM_EPS = 1e-08
ADAM_WD = 0.01
ADAM_STEP = 10

VMEM_LIMIT = 56 * 1024 * 1024
MESH = pl.DeviceIdType.MESH


def _params(sem=None, vmem=VMEM_LIMIT):
    return pltpu.CompilerParams(dimension_semantics=sem, vmem_limit_bytes=vmem)


def _nn(a, b):
    return jnp.dot(a, b, preferred_element_type=F32)


def _nt(a, b):
    return lax.dot_general(a, b, (((1,), (1,)), ((), ())), preferred_element_type=F32)


def _tn(a, b):
    return lax.dot_general(a, b, (((0,), (0,)), ((), ())), preferred_element_type=F32)


def _tri(x, m2):
    hi = x.astype(BF16)
    lo = (x - hi.astype(F32)).astype(BF16)
    return _nn(jnp.concatenate([hi, lo], axis=1), m2)


def _rms(x, g):
    r = lax.rsqrt(jnp.mean(x * x, axis=-1, keepdims=True) + EPS)
    return x * r * g


def _rms_bwd(dy, x, g):
    r = lax.rsqrt(jnp.mean(x * x, axis=-1, keepdims=True) + EPS)
    xhat = x * r
    u = dy * g
    dx = r * (u - xhat * jnp.mean(u * xhat, axis=-1, keepdims=True))
    return dx, jnp.sum(dy * xhat, axis=0, keepdims=True)


def _softplus_logsig(z):
    sp = jnp.maximum(z, 0.0) + jnp.log(1.0 + jnp.exp(-jnp.abs(z)))
    return sp, z - sp


def _tile(n, want):
    t = min(n, want)
    while n % t:
        t //= 2
    return t


def _place():
    x, y, c = lax.axis_index("x"), lax.axis_index("y"), lax.axis_index("c")
    chips = [(1 - x, y), (x, 1 - y), (1 - x, 1 - y)]
    return x, y, c, chips


def all_gather_rows(v, name):
    R, C = v.shape

    def body(v_ref, out_ref, send_sems, recv_sems, local_sem):
        x, y, c, chips = _place()
        me, sibling = (x, y, c), (x, y, 1 - c)

        def slot(px, py, pc):
            return out_ref.at[4 * px + 2 * py + pc]

        def copy(k, block, to, src=None):
            return pltpu.make_async_remote_copy(
                src_ref=slot(*block) if src is None else src, dst_ref=slot(*block),
                send_sem=send_sems.at[k], recv_sem=recv_sems.at[k], device_id=to, device_id_type=MESH)

        mine = pltpu.make_async_copy(v_ref, slot(*me), local_sem)
        mine.start()
        first = [copy(0, me, sibling, src=v_ref)]
        first += [copy(1 + j, me, (*chip, c), src=v_ref) for j, chip in enumerate(chips)]
        for cp in first:
            cp.start()
        passed = [copy(4 + j, (*chip, c), sibling) for j, chip in enumerate(chips)]
        for j, chip in enumerate(chips):
            copy(1 + j, (*chip, c), me).wait_recv()
            passed[j].start()
        copy(0, sibling, me).wait_recv()
        for j, chip in enumerate(chips):
            copy(4 + j, (*chip, 1 - c), me).wait_recv()
        for cp in first + passed:
            cp.wait_send()
        mine.wait()

    return pl.pallas_call(
        body, name=name, out_shape=S((N_DEV, R, C), v.dtype),
        in_specs=[pl.BlockSpec(memory_space=pl.ANY)], out_specs=pl.BlockSpec(memory_space=pl.ANY),
        scratch_shapes=[pltpu.SemaphoreType.DMA((7,)), pltpu.SemaphoreType.DMA((7,)), pltpu.SemaphoreType.DMA],
    )(v)


class _Exchange:
    def __init__(self, ins, outs, n_first, n_second, n_local, plan, aliases=None):
        self.ins, self.outs, self.plan, self.aliases = list(ins), list(outs), plan, aliases or {}
        self.n_first, self.n_second, self.n_local = n_first, n_second, n_local

    def scratch(self):
        n = self.n_first + self.n_second
        return [pltpu.SemaphoreType.DMA((n,)), pltpu.SemaphoreType.DMA((n,)), pltpu.SemaphoreType.DMA((max(self.n_local, 1),))]

    def _copies(self, in_refs, out_refs, sems):
        send_sems, recv_sems, local_sems = sems
        first, second, local = self.plan(in_refs, out_refs)
        rem = [pltpu.make_async_remote_copy(src_ref=s, dst_ref=d, send_sem=send_sems.at[k], recv_sem=recv_sems.at[k],
                                            device_id=dev, device_id_type=MESH) for k, (s, d, dev) in enumerate(first + second)]
        loc = [pltpu.make_async_copy(s, d, local_sems.at[k]) for k, (s, d) in enumerate(local)]
        return rem[:len(first)], rem[len(first):], loc

    def start(self, in_refs, out_refs, sems):
        first, _, loc = self._copies(in_refs, out_refs, sems)
        for cp in first + loc:
            cp.start()

    def middle(self, in_refs, out_refs, sems):
        first, second, _ = self._copies(in_refs, out_refs, sems)
        if second:
            for cp in first:
                cp.wait_recv()
            for cp in second:
                cp.start()

    def finish(self, in_refs, out_refs, sems):
        first, second, loc = self._copies(in_refs, out_refs, sems)
        for cp in second if second else first:
            cp.wait_recv()
        for cp in first + second:
            cp.wait_send()
        for cp in loc:
            cp.wait()


def gather(v, rows=None, into=None):
    R, C = v.shape
    r0, nr = rows or (0, R)

    def plan(ins, outs):
        x, y, c, chips = _place()
        slot = lambda px, py, pc: outs[0].at[4 * px + 2 * py + pc, pl.ds(r0, nr), :]
        src, mine = ins[0].at[pl.ds(r0, nr), :], slot(x, y, c)
        first = [(src, mine, (x, y, 1 - c))] + [(src, mine, (*chip, c)) for chip in chips]
        second = [(slot(*chip, c), slot(*chip, c), (x, y, 1 - c)) for chip in chips]
        return first, second, [(src, mine)]

    if into is None:
        return _Exchange([v], [S((N_DEV, R, C), v.dtype)], 4, 3, 1, plan)
    return _Exchange([v, into], [S((N_DEV, R, C), v.dtype)], 4, 3, 1, plan, aliases={1: 0})


def scatter_first(gb):
    _, R, C = gb.shape

    def plan(ins, outs):
        x, y, c, chips = _place()
        owners = [(x, y)] + chips
        return [(ins[0].at[4 * px + 2 * py + (1 - c)], outs[0].at[j], (x, y, 1 - c)) for j, (px, py) in enumerate(owners)], [], []

    return _Exchange([gb], [S((4, R, C), BF16)], 4, 0, 0, plan)


def scatter_second(sb, rows=None, into=None):
    r0, nr = rows or (0, sb.shape[1])

    def plan(ins, outs):
        x, y, c, chips = _place()
        part = lambda ref, j: ref.at[j, pl.ds(r0, nr), :]
        return [(part(ins[0], j), part(outs[0], j), (*chips[j], c)) for j in range(3)], [], []

    if into is None:
        return _Exchange([sb], [S(sb.shape, BF16)], 3, 0, 0, plan)
    return _Exchange([sb, into], [S(sb.shape, BF16)], 3, 0, 0, plan, aliases={1: 0})


def _call(body, *, name, grid, in_specs, out_specs, out_shape, args, scratch=(), sem=None, riders=()):
    single = not isinstance(out_shape, (tuple, list))
    out_shape = (out_shape,) if single else tuple(out_shape)
    out_specs = (out_specs,) if single else tuple(out_specs)
    n_in, n_out, n_sc = len(in_specs), len(out_shape), len(scratch)
    if not riders:
        res = pl.pallas_call(body, name=name, grid=grid, in_specs=list(in_specs), out_specs=out_specs, out_shape=out_shape,
                             scratch_shapes=list(scratch), compiler_params=_params(sem))(*args)
        return res[0] if single else res
    r_ins = [a for r in riders for a in r.ins]
    r_outs = [o for r in riders for o in r.outs]
    r_scr = [s for r in riders for s in r.scratch()]
    aliases, i0, o0 = {}, n_in, n_out
    for r in riders:
        for a, b in r.aliases.items():
            aliases[i0 + a] = o0 + b
        i0, o0 = i0 + len(r.ins), o0 + len(r.outs)
    steps = math.prod(grid)

    def full(*refs):
        ins, rin = refs[:n_in], refs[n_in:n_in + len(r_ins)]
        pos = n_in + len(r_ins)
        outs, rout = refs[pos:pos + n_out], refs[pos + n_out:pos + n_out + len(r_outs)]
        pos += n_out + len(r_outs)
        sc, rsc = refs[pos:pos + n_sc], refs[pos + n_sc:]
        step = 0
        for d, n in enumerate(grid):
            step = step * n + pl.program_id(d)

        def each(method):
            i, o = 0, 0
            for k, r in enumerate(riders):
                getattr(r, method)(rin[i:i + len(r.ins)], rout[o:o + len(r.outs)], rsc[3 * k:3 * k + 3])
                i, o = i + len(r.ins), o + len(r.outs)

        @pl.when(step == 0)
        def _():
            each("start")
        body(*ins, *outs, *sc)

        @pl.when(step == max(steps - 1 - max(steps // 8, 1), 0))
        def _():
            each("middle")

        @pl.when(step == steps - 1)
        def _():
            each("finish")

    anywhere = pl.BlockSpec(memory_space=pl.ANY)
    res = pl.pallas_call(
        full, name=name, grid=grid, in_specs=list(in_specs) + [anywhere] * len(r_ins),
        out_specs=out_specs + (anywhere,) * len(r_outs), out_shape=out_shape + tuple(r_outs),
        scratch_shapes=list(scratch) + r_scr, input_output_aliases=aliases,
        compiler_params=_params(("arbitrary",) * len(grid)))(*args, *r_ins)
    host, rest, per = res[:n_out], list(res[n_out:]), []
    for r in riders:
        per.append(rest[:len(r.outs)])
        rest = rest[len(r.outs):]
    return (host[0] if single else tuple(host)), per


def idle_host(riders, name):
    def body(o_ref):
        o_ref[...] = jnp.zeros_like(o_ref)

    return _call(body, name=name, grid=(1,), in_specs=[], out_specs=pl.BlockSpec((8, QB), lambda i: (0, 0)),
                 out_shape=S((8, QB), F32), args=(), riders=riders)[1]


def _rows_tile(n, cap):
    return max(t for t in range(16, min(n, cap) + 1, 16) if n % t == 0)


def scatter_add(g, ra, name):
    _, R, C = g.shape
    tr = _rows_tile(R, 176)
    x, y, c, chips = _place()
    slots = jnp.stack([4 * px + 2 * py + c for px, py in [(x, y)] + chips]).astype(jnp.int32)

    def body(s_ref, g0, g1, g2, g3, ra_ref, own_ref, sb_ref):
        own_ref[...] = g0[...] + ra_ref[0].astype(F32)
        for j, gj in enumerate((g1, g2, g3)):
            sb_ref[j] = (gj[...] + ra_ref[j + 1].astype(F32)).astype(BF16)

    spec = pltpu.PrefetchScalarGridSpec(
        num_scalar_prefetch=1, grid=(R // tr,),
        in_specs=[pl.BlockSpec((None, tr, C), lambda i, s, j=j: (s[j], i, 0)) for j in range(4)]
        + [pl.BlockSpec((4, tr, C), lambda i, s: (0, i, 0))],
        out_specs=(pl.BlockSpec((tr, C), lambda i, s: (i, 0)), pl.BlockSpec((3, tr, C), lambda i, s: (0, i, 0))))
    return pl.pallas_call(body, name=name, grid_spec=spec, out_shape=(S((R, C), F32), S((3, R, C), BF16)),
                          compiler_params=_params(("parallel",)))(slots, g, g, g, g, ra)


def rms_cast(h, g, name):
    T, D = h.shape
    tm = _tile(T, 512)

    def body(h_ref, g_ref, n_ref):
        n_ref[...] = _rms(h_ref[...], g_ref[...]).astype(BF16)

    row = pl.BlockSpec((tm, D), lambda i: (i, 0))
    return _call(body, name=name, grid=(T // tm,), out_shape=S((T, D), BF16), in_specs=[row, pl.BlockSpec((1, D), lambda i: (0, 0))],
                 out_specs=row, sem=("parallel",), args=(h, g))


def ffn_up_fwd(n, wgu, name, riders=()):
    T, D = n.shape
    F = wgu.shape[1]
    tr, tn = _tile(T, 512), _tile(F, 256)

    def body(n_ref, wg_ref, wu_ref, dgate_ref, dup_ref, a_ref):
        wg, wu = wg_ref[...], wu_ref[...]
        for r in range(T // tr):
            rows = slice(r * tr, (r + 1) * tr)
            x = n_ref[rows, :]
            gate = _nt(x, wg)
            up = _nt(x, wu)
            s = jax.nn.sigmoid(gate)
            silu = gate * s
            dgate_ref[rows, :] = (up * (s * (1.0 + gate * (1.0 - s)))).astype(BF16)
            dup_ref[rows, :] = silu.astype(BF16)
            a_ref[rows, :] = (silu * up).astype(BF16)

    tile = pl.BlockSpec((T, tn), lambda j: (0, j))
    return _call(
        body, name=name, grid=(F // tn,), out_shape=(S((T, F), BF16),) * 3,
        in_specs=[pl.BlockSpec((T, D), lambda j: (0, 0)),
                  pl.BlockSpec((None, tn, D), lambda j: (0, j, 0)), pl.BlockSpec((None, tn, D), lambda j: (1, j, 0))],
        out_specs=(tile, tile, tile), sem=("parallel",), args=(n, wgu, wgu), riders=riders)


def ffn_down_fwd(a, wd, h, g_next, name, riders=()):
    T, F = a.shape
    D = wd.shape[1]
    tm = _tile(T, 256)

    def body(a_ref, w_ref, h_ref, *rest):
        out = h_ref[...] + 0.5 * _nn(a_ref[...], w_ref[...])
        if g_next is None:
            rest[0][...] = out
        else:
            g_ref, o_ref, n_ref = rest
            o_ref[...] = out
            n_ref[...] = _rms(out, g_ref[...]).astype(BF16)

    row = pl.BlockSpec((tm, D), lambda i: (i, 0))
    more = g_next is not None
    return _call(
        body, name=name, grid=(T // tm,), out_shape=(S((T, D), F32), S((T, D), BF16)) if more else S((T, D), F32),
        in_specs=[pl.BlockSpec((tm, F), lambda i: (i, 0)), pl.BlockSpec((F, D), lambda i: (0, 0)), row]
        + ([pl.BlockSpec((1, D), lambda i: (0, 0))] if more else []),
        out_specs=(row, row) if more else row,
        sem=("parallel",), args=(a, wd, h) + ((g_next,) if more else ()), riders=riders)


def mix_in_fwd(h, g, win, name):
    T, D = h.shape
    N = win.shape[0]
    tm = _tile(T, 256)

    def body(h_ref, g_ref, w_ref, n_ref, p_ref):
        n = _rms(h_ref[...], g_ref[...]).astype(BF16)
        n_ref[...] = n
        p_ref[...] = _nt(n, w_ref[...]).astype(BF16)

    return pl.pallas_call(
        body, name=name, grid=(T // tm,), out_shape=(S((T, D), BF16), S((T, N), BF16)),
        in_specs=[pl.BlockSpec((tm, D), lambda i: (i, 0)), pl.BlockSpec((1, D), lambda i: (0, 0)),
                  pl.BlockSpec((N, D), lambda i: (0, 0))],
        out_specs=(pl.BlockSpec((tm, D), lambda i: (i, 0)), pl.BlockSpec((tm, N), lambda i: (i, 0))),
        compiler_params=_params(("parallel",)),
    )(h, g, win)


def _tri_consts():
    r = lax.broadcasted_iota(jnp.int32, (QB, QB), 0)
    c = lax.broadcasted_iota(jnp.int32, (QB, QB), 1)
    ones = jnp.ones((QB, QB), BF16)

    def stacked(tri):
        m = jnp.concatenate([tri.astype(BF16), ones], axis=1)
        return jnp.concatenate([m, m], axis=0)

    return stacked(r > c), stacked(r <= c), stacked(r < c)


def _half_masks():
    lane = lax.broadcasted_iota(jnp.int32, (QB, PAIR), 1)
    row = lax.broadcasted_iota(jnp.int32, (QB, PAIR), 0)
    return lane < HEAD_DIM, lane, row


def sb_attn_fwd(p, after, name, riders=()):
    T = p.shape[0]
    nq = T // QB

    def body(q_ref, k_ref, v_ref, m_ref, o_ref, tot_ref, q_sc, acc_ref, z_sc):
        i = pl.program_id(0)
        lo, lane, row = _half_masks()
        causal = lane < row
        heads, pairs = range(SB_HEADS), range(SB_HEADS // 2)
        for hp in pairs:
            q_sc[hp] = (q_ref[:, hp * PAIR:(hp + 1) * PAIR].astype(F32) * SCALE).astype(BF16)
        m2 = m_ref[...]

        def by_head(ref, j, hp):
            t = ref[pl.ds(pl.multiple_of(j * QB, QB), QB), hp * PAIR:(hp + 1) * PAIR]
            return jnp.concatenate([jnp.where(lo, t, 0), jnp.where(lo, 0, t)], axis=0)

        def scores(j):
            return [_nt(q_sc[hp], by_head(k_ref, j, hp)) for hp in pairs]

        def block(j, diag):
            z2 = [z_sc[hp] for hp in pairs]
            ahead = scores(jnp.maximum(j - 1, 0))
            for hp in pairs:
                z_sc[hp] = ahead[hp]
            vs = [by_head(v_ref, j, hp) for hp in pairs]
            spls = [_softplus_logsig(z2[h // 2][:, (h % 2) * QB:(h % 2 + 1) * QB]) for h in heads]
            sp = [jnp.where(causal, spls[h][0], 0.0) if diag else spls[h][0] for h in heads]
            rr = [_tri(sp[h], m2) for h in heads]
            if diag:
                w = [jnp.where(causal, jnp.exp(spls[h][1] - rr[h][:, :QB]), 0.0).astype(BF16) for h in heads]
            else:
                c = [tot_ref[:, h * QB:(h + 1) * QB] for h in heads]
                w = [jnp.exp(spls[h][1] - (c[h] + rr[h][:, :QB])).astype(BF16) for h in heads]
            pv = [_nn(jnp.concatenate([w[2 * hp], w[2 * hp + 1]], axis=1), vs[hp]) for hp in pairs]
            for hp in pairs:
                acc_ref[hp] = pv[hp] if diag else acc_ref[hp] + pv[hp]
            for h in heads:
                tot_ref[:, h * QB:(h + 1) * QB] = rr[h][:, QB:] if diag else c[h] + rr[h][:, QB:]

        first = scores(i)
        for hp in pairs:
            z_sc[hp] = first[hp]
        block(i, True)

        def step(t, carry):
            block(i - 1 - t, False)
            return carry
        lax.fori_loop(0, i, step, 0)
        for hp in pairs:
            o_ref[:, hp * PAIR:(hp + 1) * PAIR] = acc_ref[hp]

    npair = SB_HEADS // 2
    return _call(
        body, name=name, grid=(nq,), out_shape=(S((T, SB_W), F32), S((T, SB_HEADS * QB), F32)),
        in_specs=[pl.BlockSpec((QB, SB_W), lambda i: (i, 0)), pl.BlockSpec((T, SB_W), lambda i: (0, 1)),
                  pl.BlockSpec((T, SB_W), lambda i: (0, 2)), pl.BlockSpec((2 * QB, 2 * QB), lambda i: (0, 0))],
        out_specs=(pl.BlockSpec((QB, SB_W), lambda i: (i, 0)), pl.BlockSpec((QB, SB_HEADS * QB), lambda i: (i, 0))),
        scratch=[pltpu.VMEM((npair, QB, PAIR), BF16), pltpu.VMEM((npair, QB, PAIR), F32), pltpu.VMEM((npair, QB, 2 * QB), F32)],
        sem=("arbitrary",), args=(p, p, p, after), riders=riders)


def sb_attn_bwd(p, do, tot, upto, before, name, riders=()):
    T = p.shape[0]
    nq = T // QB

    def body(q_ref, k_ref, v_ref, do_ref, tot_ref, mp_ref, mg_ref, dq_ref, dk_ref, dv_ref,
             q_sc, d_sc, qd_sc, pg_sc, dq_acc, dk_acc, dv_acc, zd_sc):
        i = pl.program_id(0)
        lo, lane, row = _half_masks()
        causal = lane < row
        heads, pairs = range(SB_HEADS), range(SB_HEADS // 2)

        def by_head(t):
            return jnp.concatenate([jnp.where(lo, t, 0), jnp.where(lo, 0, t)], axis=0)

        for hp in pairs:
            q2 = (q_ref[:, hp * PAIR:(hp + 1) * PAIR].astype(F32) * SCALE).astype(BF16)
            d2 = do_ref[:, hp * PAIR:(hp + 1) * PAIR].astype(BF16)
            q_sc[hp] = q2
            d_sc[hp] = d2
            qd_sc[hp] = by_head(q2)
            qd_sc[SB_HEADS // 2 + hp] = by_head(d2)
        mp, mg = mp_ref[...], mg_ref[...]

        @pl.when(i == 0)
        def _():
            dk_acc[...] = jnp.zeros_like(dk_acc)
            dv_acc[...] = jnp.zeros_like(dv_acc)
        pg_sc[...] = jnp.zeros_like(pg_sc)
        dq_acc[...] = jnp.zeros_like(dq_acc)

        def rows(ref, j, hp):
            return ref[pl.ds(pl.multiple_of(j * QB, QB), QB), hp * PAIR:(hp + 1) * PAIR]

        def products(j):
            return ([_nt(q_sc[hp], by_head(rows(k_ref, j, hp))) for hp in pairs]
                    + [_nt(d_sc[hp], by_head(rows(v_ref, j, hp))) for hp in pairs])

        def block(j, diag):
            r0 = pl.multiple_of(j * QB, QB)
            half = lambda t, h: t[:, (h % 2) * QB:(h % 2 + 1) * QB]
            z = [half(zd_sc[h // 2], h) for h in heads]
            dw = [half(zd_sc[SB_HEADS // 2 + h // 2], h) for h in heads]
            if not diag:
                ahead = products(j + 1)
                for hp in range(SB_HEADS):
                    zd_sc[hp] = ahead[hp]
            ks = [by_head(rows(k_ref, j, hp)) for hp in pairs]
            spls = [_softplus_logsig(z[h]) for h in heads]
            sp = [jnp.where(causal, spls[h][0], 0.0) if diag else spls[h][0] for h in heads]
            rr = [_tri(sp[h], mp) for h in heads]
            pc = [pg_sc[2 * h] for h in heads]
            w = [jnp.exp(spls[h][1] - (tot_ref[:, h * QB:(h + 1) * QB] - (pc[h] + rr[h][:, :QB]))) for h in heads]
            if diag:
                w = [jnp.where(causal, w[h], 0.0) for h in heads]
            gg = [dw[h] * w[h] for h in heads]
            rg = [_tri(gg[h], mg) for h in heads]
            gc = [pg_sc[2 * h + 1] for h in heads]
            dz = [gg[h] - (gg[h] + gc[h] + rg[h][:, :QB]) * jnp.exp(spls[h][1]) for h in heads]
            if diag:
                dz = [jnp.where(causal, dz[h], 0.0) for h in heads]
            dzb = [dz[h].astype(BF16) for h in heads]
            wb = [w[h].astype(BF16) for h in heads]
            both = lambda t, hp, axis: jnp.concatenate([t[2 * hp], t[2 * hp + 1]], axis=axis)
            dq = [_nn(both(dzb, hp, 1), ks[hp]) for hp in pairs]
            dk = [_tn(both(dzb, hp, 0), qd_sc[hp]) for hp in pairs]
            dv = [_tn(both(wb, hp, 0), qd_sc[SB_HEADS // 2 + hp]) for hp in pairs]
            for h in heads:
                if not diag:
                    pg_sc[2 * h] = pc[h] + rr[h][:, QB:]
                    pg_sc[2 * h + 1] = gc[h] + rg[h][:, QB:]
            for hp in pairs:
                dq_acc[hp] += dq[hp]
                dk_acc[pl.ds(r0, QB), hp * PAIR:(hp + 1) * PAIR] += dk[hp]
                dv_acc[pl.ds(r0, QB), hp * PAIR:(hp + 1) * PAIR] += dv[hp]

        first = products(0)
        for hp in range(SB_HEADS):
            zd_sc[hp] = first[hp]

        def step(t, carry):
            block(t, False)
            return carry
        lax.fori_loop(0, i, step, 0)
        block(i, True)
        for hp in pairs:
            dq_ref[:, hp * PAIR:(hp + 1) * PAIR] = (dq_acc[hp] * SCALE).astype(BF16)

        @pl.when(i == nq - 1)
        def _():
            dk_ref[...] = dk_acc[...].astype(BF16)
            dv_ref[...] = dv_acc[...].astype(BF16)

    qtile = pl.BlockSpec((QB, SB_W), lambda i: (i, 0))
    whole = pl.BlockSpec((T, SB_W), lambda i: (0, 0))
    const = pl.BlockSpec((2 * QB, 2 * QB), lambda i: (0, 0))
    return _call(
        body, name=name, grid=(nq,), out_shape=(S((T, SB_W), BF16),) * 3,
        in_specs=[qtile, pl.BlockSpec((T, SB_W), lambda i: (0, 1)), pl.BlockSpec((T, SB_W), lambda i: (0, 2)), qtile,
                  pl.BlockSpec((QB, SB_HEADS * QB), lambda i: (i, 0)), const, const],
        out_specs=(qtile, whole, whole),
        scratch=[pltpu.VMEM((SB_HEADS // 2, QB, PAIR), BF16), pltpu.VMEM((SB_HEADS // 2, QB, PAIR), BF16),
                 pltpu.VMEM((SB_HEADS, 2 * QB, PAIR), BF16),
                 pltpu.VMEM((2 * SB_HEADS, QB, QB), F32), pltpu.VMEM((SB_HEADS // 2, QB, PAIR), F32),
                 pltpu.VMEM((T, SB_W), F32), pltpu.VMEM((T, SB_W), F32), pltpu.VMEM((SB_HEADS, QB, 2 * QB), F32)],
        sem=("arbitrary",), args=(p, p, p, do, tot, upto, before), riders=riders)


def _t5_buckets():
    a = lax.broadcasted_iota(jnp.int32, (QB, QB), 0)
    c = lax.broadcasted_iota(jnp.int32, (QB, QB), 1)

    def bucket(dist):
        dist = jnp.maximum(dist, 0)
        max_exact = N_BUCKETS // 2
        d = jnp.maximum(dist, 1).astype(F32)
        large = max_exact + (jnp.log(d / max_exact) / math.log(MAX_DISTANCE / max_exact)
                             * (N_BUCKETS - max_exact)).astype(jnp.int32)
        large = jnp.minimum(large, N_BUCKETS - 1)
        return jnp.where(dist < max_exact, dist, large)

    return bucket(QB + a - c), bucket(a - c)


def _swa_common(i, kp_ref, kc_ref, vp_ref, vc_ref, bp_ref, bc_ref, rb_ref, bias_ref):
    lo, lane, row = _half_masks()

    @pl.when(i == 0)
    def _():
        for blk, b_ref in enumerate((bp_ref, bc_ref)):
            bk = b_ref[...]
            for h in range(8):
                acc = jnp.zeros((QB, QB), F32)
                for b in range(N_BUCKETS):
                    acc = jnp.where(bk == b, rb_ref[b, h], acc)
                bias_ref[h, blk] = acc

    band = [(lane > row) & (i > 0), lane <= row]

    def halves(ref):
        t = ref[...].astype(F32)
        sw = pltpu.roll(t, HEAD_DIM, 1)
        return [[jnp.where(lo, t, 0.0).astype(BF16), jnp.where(lo, 0.0, sw).astype(BF16)],
                [jnp.where(lo, sw, 0.0).astype(BF16), jnp.where(lo, 0.0, t).astype(BF16)]]

    ks = [halves(kp_ref), halves(kc_ref)]
    vs = [halves(vp_ref), halves(vc_ref)]
    return lo, band, ks, vs


def swa_fwd(p, sinks, rel_bias, bprev, bcur, name, riders=()):
    T = p.shape[0]
    nq = T // QB
    kcol, vcol = (3 * SB_W + SWA_W) // KV_W, (3 * SB_W + SWA_W) // KV_W + 1

    def body(q_ref, kp_ref, kc_ref, vp_ref, vc_ref, bp_ref, bc_ref, sink_ref, rb_ref, o_ref, lse_ref, bias_ref):
        i = pl.program_id(0)
        lo, band, ks, vs = _swa_common(i, kp_ref, kc_ref, vp_ref, vc_ref, bp_ref, bc_ref, rb_ref, bias_ref)
        for g in range(4):
            kh = g // 2
            q2 = q_ref[:, g * PAIR:(g + 1) * PAIR]
            outs = []
            for pos in range(2):
                h = 2 * g + pos
                sc = [jnp.where(band[b], _nt(q2, ks[b][kh][pos]) * SCALE + bias_ref[h, b], NEG_INF) for b in range(2)]
                sink = sink_ref[0, h]
                m = jnp.maximum(jnp.maximum(jnp.max(sc[0], axis=1, keepdims=True),
                                            jnp.max(sc[1], axis=1, keepdims=True)), sink)
                e = [jnp.exp(sc[b] - m) for b in range(2)]
                den = jnp.sum(e[0], axis=1, keepdims=True) + jnp.sum(e[1], axis=1, keepdims=True) + jnp.exp(sink - m)
                outs.append(_nn((e[0] / den).astype(BF16), vs[0][kh][pos]) + _nn((e[1] / den).astype(BF16), vs[1][kh][pos]))
                lse_ref[:, h * QB:(h + 1) * QB] = jnp.broadcast_to(m + jnp.log(den), (QB, QB))
            o_ref[:, g * PAIR:(g + 1) * PAIR] = outs[0] + outs[1]

    kv = lambda col, prev: pl.BlockSpec((QB, KV_W), (lambda i: (jnp.maximum(i - 1, 0), col)) if prev else (lambda i: (i, col)))
    full = pl.BlockSpec((QB, QB), lambda i: (0, 0))
    smem = pl.BlockSpec(memory_space=pltpu.SMEM)
    return _call(
        body, name=name, grid=(nq,), out_shape=(S((T, SWA_W), F32), S((T, 8 * QB), F32)),
        in_specs=[pl.BlockSpec((QB, SWA_W), lambda i: (i, 3)), kv(kcol, True), kv(kcol, False), kv(vcol, True), kv(vcol, False),
                  full, full, smem, smem],
        out_specs=(pl.BlockSpec((QB, SWA_W), lambda i: (i, 0)), pl.BlockSpec((QB, 8 * QB), lambda i: (i, 0))),
        scratch=[pltpu.VMEM((8, 2, QB, QB), F32)],
        sem=("arbitrary",), args=(p, p, p, p, p, bprev, bcur, sinks, rel_bias), riders=riders)


def swa_bwd(p, do, lse, sinks, rel_bias, bprev, bcur, name, riders=()):
    T = p.shape[0]
    nq = T // QB
    kcol, vcol = (3 * SB_W + SWA_W) // KV_W, (3 * SB_W + SWA_W) // KV_W + 1

    def body(q_ref, kp_ref, kc_ref, vp_ref, vc_ref, do_ref, lse_ref, bp_ref, bc_ref, sink_ref, rb_ref,
             dq_ref, dk_ref, dv_ref, dsink_ref, dsc_ref, bias_ref, dk_acc, dv_acc):
        i = pl.program_id(0)
        lo, band, ks, vs = _swa_common(i, kp_ref, kc_ref, vp_ref, vc_ref, bp_ref, bc_ref, rb_ref, bias_ref)

        @pl.when(i == 0)
        def _():
            dk_acc[...] = jnp.zeros_like(dk_acc)
            dv_acc[...] = jnp.zeros_like(dv_acc)
            dsc_ref[...] = jnp.zeros_like(dsc_ref)
            dsink_ref[...] = jnp.zeros_like(dsink_ref)

        lane1 = lax.broadcasted_iota(jnp.int32, (1, QB), 1)
        dsink = jnp.zeros((1, QB), F32)
        dk_parts = [[[None, None], [None, None]], [[None, None], [None, None]]]
        dv_parts = [[[None, None], [None, None]], [[None, None], [None, None]]]

        def add(parts, b, pos, kh, val):
            parts[b][pos][kh] = val if parts[b][pos][kh] is None else parts[b][pos][kh] + val

        for g in range(4):
            kh = g // 2
            q2 = q_ref[:, g * PAIR:(g + 1) * PAIR]
            q2f = q2.astype(F32)
            d2f = do_ref[:, g * PAIR:(g + 1) * PAIR]
            d2 = d2f.astype(BF16)
            dq = None
            for pos in range(2):
                h = 2 * g + pos
                keep = lo if pos == 0 else ~lo
                qh = jnp.where(keep, q2f, 0.0).astype(BF16)
                dh = jnp.where(keep, d2f, 0.0).astype(BF16)
                lse_h = lse_ref[:, h * QB:(h + 1) * QB]
                sink = sink_ref[0, h]
                pr = [jnp.exp(jnp.where(band[b], _nt(q2, ks[b][kh][pos]) * SCALE + bias_ref[h, b], NEG_INF) - lse_h)
                      for b in range(2)]
                dp = [_nt(d2, vs[b][kh][pos]) for b in range(2)]
                delta = jnp.sum(pr[0] * dp[0], axis=1, keepdims=True) + jnp.sum(pr[1] * dp[1], axis=1, keepdims=True)
                p_sink = jnp.exp(sink - lse_h[:, :1])
                dsink = dsink + jnp.where(lane1 == h, -jnp.sum(p_sink * delta), 0.0)
                for b in range(2):
                    dsc = pr[b] * (dp[b] - delta)
                    dsc_ref[h, b] += dsc
                    dzb = (dsc * SCALE).astype(BF16)
                    t = _nn(dzb, ks[b][kh][pos])
                    dq = t if dq is None else dq + t
                    add(dk_parts, b, pos, kh, _tn(dzb, qh))
                    add(dv_parts, b, pos, kh, _tn(pr[b].astype(BF16), dh))
            dq_ref[:, g * PAIR:(g + 1) * PAIR] = dq.astype(BF16)
        dsink_ref[...] += dsink

        def fold(parts, b):
            low = parts[b][0][0] + pltpu.roll(parts[b][1][0], HEAD_DIM, 1)
            high = parts[b][1][1] + pltpu.roll(parts[b][0][1], HEAD_DIM, 1)
            return jnp.where(lo, low, high)

        rp = pl.multiple_of(jnp.maximum(i - 1, 0) * QB, QB)
        rc = pl.multiple_of(i * QB, QB)
        dk_acc[pl.ds(rp, QB), :] += fold(dk_parts, 0)
        dv_acc[pl.ds(rp, QB), :] += fold(dv_parts, 0)
        dk_acc[pl.ds(rc, QB), :] += fold(dk_parts, 1)
        dv_acc[pl.ds(rc, QB), :] += fold(dv_parts, 1)

        @pl.when(i == nq - 1)
        def _():
            dk_ref[...] = dk_acc[...].astype(BF16)
            dv_ref[...] = dv_acc[...].astype(BF16)

    kv = lambda col, prev: pl.BlockSpec((QB, KV_W), (lambda i: (jnp.maximum(i - 1, 0), col)) if prev else (lambda i: (i, col)))
    full = pl.BlockSpec((QB, QB), lambda i: (0, 0))
    smem = pl.BlockSpec(memory_space=pltpu.SMEM)
    whole = lambda shape: pl.BlockSpec(shape, lambda i: (0,) * len(shape))
    return _call(
        body, name=name, grid=(nq,),
        out_shape=(S((T, SWA_W), BF16), S((T, KV_W), BF16), S((T, KV_W), BF16), S((1, QB), F32), S((8, 2, QB, QB), F32)),
        in_specs=[pl.BlockSpec((QB, SWA_W), lambda i: (i, 3)), kv(kcol, True), kv(kcol, False), kv(vcol, True), kv(vcol, False),
                  pl.BlockSpec((QB, SWA_W), lambda i: (i, 0)), pl.BlockSpec((QB, 8 * QB), lambda i: (i, 0)),
                  full, full, smem, smem],
        out_specs=(pl.BlockSpec((QB, SWA_W), lambda i: (i, 0)), whole((T, KV_W)), whole((T, KV_W)), whole((1, QB)),
                   whole((8, 2, QB, QB))),
        scratch=[pltpu.VMEM((8, 2, QB, QB), F32), pltpu.VMEM((T, KV_W), F32), pltpu.VMEM((T, KV_W), F32)],
        sem=("arbitrary",), args=(p, p, p, p, p, do, lse, bprev, bcur, sinks, rel_bias), riders=riders)


def mix_out_fwd(o_sb, o_sw, g_sb, g_sw, wout, h, g_next, name, riders=()):
    T, D = h.shape
    M = SB_W + SWA_W
    tm = _tile(T, 256)

    def body(a_ref, b_ref, ga_ref, gb_ref, w_ref, h_ref, gn_ref, mx_ref, o_ref, n_ref):
        mx_ref[:, :SB_W] = _rms(a_ref[...], ga_ref[...]).astype(BF16)
        mx_ref[:, SB_W:] = _rms(b_ref[...], gb_ref[...]).astype(BF16)
        out = h_ref[...] + _nn(mx_ref[...], w_ref[...])
        o_ref[...] = out
        n_ref[...] = _rms(out, gn_ref[...]).astype(BF16)

    row = lambda n: pl.BlockSpec((tm, n), lambda i: (i, 0))
    vec = lambda n: pl.BlockSpec((1, n), lambda i: (0, 0))
    return _call(
        body, name=name, grid=(T // tm,), out_shape=(S((T, M), BF16), S((T, D), F32), S((T, D), BF16)),
        in_specs=[row(SB_W), row(SWA_W), vec(SB_W), vec(SWA_W), pl.BlockSpec((M, D), lambda i: (0, 0)), row(D), vec(D)],
        out_specs=(row(M), row(D), row(D)),
        sem=("parallel",), args=(o_sb, o_sw, g_sb, g_sw, wout, h, g_next), riders=riders)


def loss_head(h, g, target, name):
    T, D = h.shape
    tm = _tile(T, 256)

    def body(h_ref, g_ref, t_ref, loss_ref, dh_ref, dhb_ref, dg_ref):
        @pl.when(pl.program_id(0) == 0)
        def _():
            loss_ref[...] = jnp.zeros_like(loss_ref)
            dg_ref[...] = jnp.zeros_like(dg_ref)
        x = h_ref[...]
        err = _rms(x, g_ref[...]) - t_ref[...]
        loss_ref[...] += jnp.full((1, QB), 0.5 * jnp.sum(jnp.mean(err * err, axis=-1)), F32)
        dx, dg = _rms_bwd(err / D, x, g_ref[...])
        dh_ref[...] = dx
        dhb_ref[...] = dx.astype(BF16)
        dg_ref[...] += dg

    row = pl.BlockSpec((tm, D), lambda i: (i, 0))
    vec = pl.BlockSpec((1, D), lambda i: (0, 0))
    return pl.pallas_call(
        body, name=name, grid=(T // tm,), out_shape=(S((1, QB), F32), S((T, D), F32), S((T, D), BF16), S((1, D), F32)),
        in_specs=[row, vec, row], out_specs=(pl.BlockSpec((1, QB), lambda i: (0, 0)), row, row, vec),
        compiler_params=_params(("arbitrary",)),
    )(h, g, target)


def ffn_down_bwd(dhb, wd, gate, up, name, riders=()):
    T, D = dhb.shape
    F = wd.shape[0]
    tr, tn = _tile(T, 512), _tile(F, 256)

    def body(d_ref, w_ref, g_ref, u_ref, o_ref):
        w = w_ref[...]
        for r in range(T // tr):
            rows = slice(r * tr, (r + 1) * tr)
            da = 0.5 * _nt(d_ref[rows, :], w)
            o_ref[0, rows, :] = (da * g_ref[rows, :].astype(F32)).astype(BF16)
            o_ref[1, rows, :] = (da * u_ref[rows, :].astype(F32)).astype(BF16)

    tile = pl.BlockSpec((T, tn), lambda j: (0, j))
    return _call(
        body, name=name, grid=(F // tn,), out_shape=S((2, T, F), BF16),
        in_specs=[pl.BlockSpec((T, D), lambda j: (0, 0)), pl.BlockSpec((tn, D), lambda j: (j, 0)), tile, tile],
        out_specs=pl.BlockSpec((2, T, tn), lambda j: (0, 0, j)),
        sem=("parallel",), args=(dhb, wd, gate, up), riders=riders)


def tn_matmul(xs, y, alpha, name, riders=()):
    B, T, N = xs.shape
    D = y.shape[1]
    tn = _tile(N, 256)

    def body(x_ref, y_ref, o_ref, ob_ref):
        o = alpha * _tn(x_ref[...], y_ref[...])
        o_ref[...] = o
        ob_ref[...] = o.astype(BF16)

    tile = pl.BlockSpec((None, tn, D), lambda s, j: (s, j, 0))
    return _call(
        body, name=name, grid=(B, N // tn), out_shape=(S((B, N, D), F32), S((B, N, D), BF16)),
        in_specs=[pl.BlockSpec((None, T, tn), lambda s, j: (s, 0, j)), pl.BlockSpec((T, D), lambda s, j: (0, 0))],
        out_specs=(tile, tile), sem=("parallel", "parallel"), args=(xs, y), riders=riders)


def nn_rms_bwd(xs, ws, h_in, g, dh, name, riders=()):
    B, T, K = xs.shape
    D = ws.shape[2]
    tm = _tile(T, 256)

    def body(x_ref, w_ref, h_ref, g_ref, d_ref, o_ref, ob_ref, dg_ref):
        @pl.when(pl.program_id(0) == 0)
        def _():
            dg_ref[...] = jnp.zeros_like(dg_ref)
        dn = _nn(x_ref[0], w_ref[0])
        for s in range(1, B):
            dn = dn + _nn(x_ref[s], w_ref[s])
        dx, dg = _rms_bwd(dn, h_ref[...], g_ref[...])
        out = d_ref[...] + dx
        o_ref[...] = out
        ob_ref[...] = out.astype(BF16)
        dg_ref[...] += dg

    row = pl.BlockSpec((tm, D), lambda i: (i, 0))
    vec = pl.BlockSpec((1, D), lambda i: (0, 0))
    return _call(
        body, name=name, grid=(T // tm,), out_shape=(S((T, D), F32), S((T, D), BF16), S((1, D), F32)),
        in_specs=[pl.BlockSpec((B, tm, K), lambda i: (0, i, 0)), pl.BlockSpec((B, K, D), lambda i: (0, 0, 0)), row, vec, row],
        out_specs=(row, row, vec),
        sem=("arbitrary",), args=(xs, ws, h_in, g, dh), riders=riders)


def mix_out_bwd(dhb, wout, o_sb, o_sw, g_sb, g_sw, name):
    T, D = dhb.shape
    tm = _tile(T, 256)

    def body(d_ref, w_ref, a_ref, b_ref, ga_ref, gb_ref, da_ref, db_ref, dga_ref, dgb_ref):
        @pl.when(pl.program_id(0) == 0)
        def _():
            dga_ref[...] = jnp.zeros_like(dga_ref)
            dgb_ref[...] = jnp.zeros_like(dgb_ref)
        dm = _nt(d_ref[...], w_ref[...])
        dxa, dga = _rms_bwd(dm[:, :SB_W], a_ref[...], ga_ref[...])
        dxb, dgb = _rms_bwd(dm[:, SB_W:], b_ref[...], gb_ref[...])
        da_ref[...] = dxa
        db_ref[...] = dxb
        dga_ref[...] += dga
        dgb_ref[...] += dgb

    row = lambda n: pl.BlockSpec((tm, n), lambda i: (i, 0))
    vec = lambda n: pl.BlockSpec((1, n), lambda i: (0, 0))
    return pl.pallas_call(
        body, name=name, grid=(T // tm,),
        out_shape=(S((T, SB_W), F32), S((T, SWA_W), F32), S((1, SB_W), F32), S((1, SWA_W), F32)),
        in_specs=[row(D), pl.BlockSpec((SB_W + SWA_W, D), lambda i: (0, 0)), row(SB_W), row(SWA_W), vec(SB_W), vec(SWA_W)],
        out_specs=(row(SB_W), row(SWA_W), vec(SB_W), vec(SWA_W)),
        compiler_params=_params(("arbitrary",)),
    )(dhb, wout, o_sb, o_sw, g_sb, g_sw)


def rel_bias_grad(dscs, bprev, bcur, name):
    n = len(dscs)

    def body(*refs):
        bp_ref, bc_ref, o_ref = refs[n], refs[n + 1], refs[n + 2]
        bks = [bp_ref[...], bc_ref[...]]
        row = lax.broadcasted_iota(jnp.int32, (N_BUCKETS, QB), 0)
        lane = lax.broadcasted_iota(jnp.int32, (N_BUCKETS, QB), 1)
        out = jnp.zeros((N_BUCKETS, QB), F32)
        for h in range(8):
            tot = [sum(refs[l][h, b] for l in range(n)) for b in range(2)]
            for b in range(N_BUCKETS):
                val = jnp.sum(jnp.where(bks[0] == b, tot[0], 0.0)) + jnp.sum(jnp.where(bks[1] == b, tot[1], 0.0))
                out = jnp.where((row == b) & (lane == h), val, out)
        o_ref[...] = out

    return pl.pallas_call(body, name=name, out_shape=S((N_BUCKETS, QB), F32), compiler_params=_params())(*dscs, bprev, bcur)


def _adamw(w, g, m, v):
    m = ADAM_B1 * m + (1.0 - ADAM_B1) * g
    v = ADAM_B2 * v + (1.0 - ADAM_B2) * (g * g)
    m_hat = m / (1.0 - ADAM_B1 ** ADAM_STEP)
    v_hat = v / (1.0 - ADAM_B2 ** ADAM_STEP)
    delta = -ADAM_LR * (m_hat / (jnp.sqrt(v_hat) + ADAM_EPS) + ADAM_WD * w)
    return delta, m, v


def adamw_scattered(w, m, v, owns, others, name):
    L, R, C = w.shape
    tr = _rows_tile(R, 176)

    def body(w_ref, m_ref, v_ref, *rest):
        own_refs, other_refs = rest[:L], rest[L:2 * L]
        g_ref, d_ref, mo_ref, vo_ref = rest[2 * L:]
        layer = pl.program_id(0)

        def grad(k):
            o = other_refs[k]
            return own_refs[k][...] + o[0].astype(F32) + o[1].astype(F32) + o[2].astype(F32)

        g = grad(0)
        for k in range(1, L):
            g = jnp.where(layer == k, grad(k), g)
        d, mn, vn = _adamw(w_ref[...], g, m_ref[...], v_ref[...])
        g_ref[...] = g
        d_ref[...] = d
        mo_ref[...] = mn
        vo_ref[...] = vn

    tile = pl.BlockSpec((None, tr, C), lambda l, i: (l, i, 0))
    return pl.pallas_call(
        body, name=name, grid=(L, R // tr), out_shape=(S((L, R, C), F32),) * 4,
        in_specs=[tile] * 3 + [pl.BlockSpec((tr, C), lambda l, i: (i, 0))] * L + [pl.BlockSpec((3, tr, C), lambda l, i: (0, i, 0))] * L,
        out_specs=(tile,) * 4, compiler_params=_params(("parallel", "parallel")),
    )(w, m, v, *owns, *others)


def adamw_small(w, gs, m, v, name):
    R, C = w.shape

    def body(w_ref, g_ref, m_ref, v_ref, go_ref, d_ref, mo_ref, vo_ref):
        g = g_ref[0]
        for k in range(1, N_DEV):
            g = g + g_ref[k]
        d, mn, vn = _adamw(w_ref[...], g, m_ref[...], v_ref[...])
        go_ref[...] = g
        d_ref[...] = d
        mo_ref[...] = mn
        vo_ref[...] = vn

    return pl.pallas_call(body, name=name, out_shape=(S((R, C), F32),) * 4, compiler_params=_params())(w, gs, m, v)


def kernel(x, norm_ffn1, w_ffn1_gu, w_ffn1_down, norm_mix, w_in, sinks, norm_out_sb, norm_out_swa, w_out, norm_ffn2, w_ffn2_gu, w_ffn2_down, rel_bias, norm_final, loss_target, m_norm_ffn1, m_w_ffn1_gu, m_w_ffn1_down, m_norm_mix, m_w_in, m_sinks, m_norm_out_sb, m_norm_out_swa, m_w_out, m_norm_ffn2, m_w_ffn2_gu, m_w_ffn2_down, m_rel_bias, m_norm_final, v_norm_ffn1, v_w_ffn1_gu, v_w_ffn1_down, v_norm_mix, v_w_in, v_sinks, v_norm_out_sb, v_norm_out_swa, v_w_out, v_norm_ffn2, v_w_ffn2_gu, v_w_ffn2_down, v_rel_bias, v_norm_final):
    L = norm_ffn1.shape[0]
    T, D = x.shape[1], x.shape[2]
    F = w_ffn1_down.shape[1] * N_DEV
    h = x.reshape(T, D)
    target = loss_target.reshape(T, D)
    after, upto, before = _tri_consts()
    bprev, bcur = _t5_buckets()

    local = {}
    for l in range(L):
        local[f"gu1_{l}"] = w_ffn1_gu[l].T.astype(BF16)
        local[f"d1_{l}"] = w_ffn1_down[l].astype(BF16)
        local[f"in_{l}"] = w_in[l].T.astype(BF16)
        local[f"out_{l}"] = w_out[l].astype(BF16)
        local[f"gu2_{l}"] = w_ffn2_gu[l].T.astype(BF16)
        local[f"d2_{l}"] = w_ffn2_down[l].astype(BF16)
    full, partial = {}, {}
    grads, chip_sum, recv_b = {}, {}, {}

    def run(fn, *args, ag=(), rs1=(), rs2=()):
        halves = lambda names: [n if isinstance(n, tuple) else (n, None) for n in names]
        ag, rs2 = [(n, k) for n, k in halves(ag) if n in local], halves(rs2)
        rows = lambda k, total: None if k is None else (k * (total // 2), total // 2)

        def second(n, k):
            sb = chip_sum[n][1]
            return scatter_second(sb, rows(k, sb.shape[1]), recv_b.get(n))

        riders = ([gather(local[n], rows(k, local[n].shape[0]), partial.get(n)) for n, k in ag]
                  + [scatter_first(grads[n][1]) for n in rs1] + [second(n, k) for n, k in rs2])
        if not riders:
            return fn(*args)
        outs, per = fn(*args, riders=riders)
        per = [p[0] for p in per]
        for n, k in ag:
            buf = per.pop(0)
            if k == 0:
                partial[n] = buf
            else:
                full[n] = buf.reshape(N_DEV * buf.shape[1], D)
        for n in rs1:
            chip_sum[n] = scatter_add(grads[n][0], per.pop(0), f"rs_add_{n}")
        for n, _ in rs2:
            recv_b[n] = per.pop(0)
        return outs

    def idle(name, riders=()):
        return None, idle_host(riders, name)

    gu = lambda n: full[n].reshape(2, F, D)
    slots = lambda pair: tuple(t.reshape(N_DEV, -1, D) for t in pair)
    vec = lambda a: a.reshape(1, -1)

    run(idle, "ag_head", ag=("gu1_0",))
    saved = []
    n_next = rms_cast(h, vec(norm_ffn1[0]), "rms_first")
    for l in range(L):
        nx = l + 1
        s = {"h0": h, "n1": n_next}
        s["gate1"], s["up1"], s["a1"] = run(ffn_up_fwd, s["n1"], gu(f"gu1_{l}"), f"ffn1_up{l}",
                                            ag=(f"d1_{l}",) + ((("in_0", 0),) if l == 0 else ()))
        h = run(ffn_down_fwd, s["a1"], full[f"d1_{l}"], h, None, f"ffn1_down{l}", ag=((f"in_{l}", 1),))
        s["h1"] = h
        s["n2"], s["p"] = mix_in_fwd(h, vec(norm_mix[l]), full[f"in_{l}"], f"mix_in{l}")
        s["o_sb"], s["tot"] = run(sb_attn_fwd, s["p"], after, f"sb_fwd{l}", ag=(f"out_{l}", f"gu2_{l}", f"d2_{l}"))
        s["o_sw"], s["lse"] = run(swa_fwd, s["p"], vec(sinks[l]), rel_bias, bprev, bcur, f"swa_fwd{l}", ag=((f"gu1_{nx}", 0),))
        s["mixed"], h, s["n3"] = run(mix_out_fwd, s["o_sb"], s["o_sw"], vec(norm_out_sb[l]), vec(norm_out_swa[l]),
                                     full[f"out_{l}"], h, vec(norm_ffn2[l]), f"mix_out{l}")
        s["h2"] = h
        s["gate2"], s["up2"], s["a2"] = run(ffn_up_fwd, s["n3"], gu(f"gu2_{l}"), f"ffn2_up{l}", ag=((f"gu1_{nx}", 1),))
        if nx < L:
            h, n_next = run(ffn_down_fwd, s["a2"], full[f"d2_{l}"], h, vec(norm_ffn1[nx]), f"ffn2_down{l}", ag=((f"in_{nx}", 0),))
        else:
            h = run(ffn_down_fwd, s["a2"], full[f"d2_{l}"], h, None, f"ffn2_down{l}")
        saved.append(s)

    loss_part, dh, dhb, dg_final = loss_head(h, vec(norm_final), target, "loss_head")
    loss = lax.psum(loss_part[0, 0], ("x", "y", "c"))

    small = {k: [None] * L for k in ("ffn1", "mix", "sinks", "osb", "osw", "ffn2", "dsc")}
    for l in reversed(range(L)):
        s = saved[l]

        def ffn_bwd(dh, dhb, tag, gate, up, a, n, h_in, g, r_down, r_dwgu, r_up):
            gu_n, d_n = f"gu{tag}_{l}", f"d{tag}_{l}"
            dgu = run(ffn_down_bwd, dhb, full[d_n], gate, up, f"ffn{tag}_down_bwd{l}", **r_down)
            grads[gu_n] = slots(run(tn_matmul, dgu, n, 1.0, f"ffn{tag}_dwgu{l}", **r_dwgu))
            grads[d_n] = slots(run(tn_matmul, a[None], dhb, 0.5, f"ffn{tag}_dwd{l}", rs1=(gu_n,)))
            return run(nn_rms_bwd, dgu, gu(gu_n), h_in, g, dh, f"ffn{tag}_up_bwd{l}", rs1=(d_n,), **r_up)

        later = l + 1 < L
        dh, dhb, small["ffn2"][l] = ffn_bwd(dh, dhb, 2, s["gate2"], s["up2"], s["a2"], s["n3"], s["h2"], vec(norm_ffn2[l]),
                                            dict(rs2=((f"gu1_{l + 1}", 1),) if later else ()),
                                            dict(rs2=(f"d1_{l + 1}",) if later else ()), {})
        do_sb, do_sw, small["osb"][l], small["osw"][l] = mix_out_bwd(
            dhb, full[f"out_{l}"], s["o_sb"], s["o_sw"], vec(norm_out_sb[l]), vec(norm_out_swa[l]), f"mix_out_bwd{l}")
        grads[f"out_{l}"] = slots(tn_matmul(s["mixed"][None], dhb, 1.0, f"dwout{l}"))
        dq_sb, dk_sb, dv_sb = run(sb_attn_bwd, s["p"], do_sb, s["tot"], upto, before, f"sb_bwd{l}",
                                  rs2=(f"gu2_{l}", f"d2_{l}"), rs1=(f"out_{l}",))
        dq_sw, dk_sw, dv_sw, small["sinks"][l], small["dsc"][l] = run(
            swa_bwd, s["p"], do_sw, s["lse"], vec(sinks[l]), rel_bias, bprev, bcur, f"swa_bwd{l}", rs2=(f"out_{l}",))
        dp = jnp.concatenate([dq_sb, dk_sb, dv_sb, dq_sw, dk_sw, dv_sw], axis=1)
        dh, dhb, small["mix"][l] = nn_rms_bwd(dp[None], full[f"in_{l}"][None], s["h1"], vec(norm_mix[l]), dh, f"mix_in_bwd{l}")
        grads[f"in_{l}"] = slots(tn_matmul(dp[None], s["n2"], 1.0, f"dwin{l}"))
        dh, dhb, small["ffn1"][l] = ffn_bwd(dh, dhb, 1, s["gate1"], s["up1"], s["a1"], s["n1"], s["h0"], vec(norm_ffn1[l]),
                                            dict(rs1=(f"in_{l}",)), dict(rs2=(f"in_{l}",)), dict(rs2=((f"gu1_{l}", 0),)))

    grad_x = dh.reshape(x.shape)
    run(idle, "rs_tail", rs2=(("gu1_0", 1), "d1_0"))

    upd = {}
    for nm, w, m, v, transposed in (("gu1", w_ffn1_gu, m_w_ffn1_gu, v_w_ffn1_gu, True), ("d1", w_ffn1_down, m_w_ffn1_down, v_w_ffn1_down, False),
                                    ("in", w_in, m_w_in, v_w_in, True), ("out", w_out, m_w_out, v_w_out, False),
                                    ("gu2", w_ffn2_gu, m_w_ffn2_gu, v_w_ffn2_gu, True), ("d2", w_ffn2_down, m_w_ffn2_down, v_w_ffn2_down, False)):
        turn = (lambda a: jnp.swapaxes(a, 1, 2)) if transposed else (lambda a: a)
        names = [f"{nm}_{l}" for l in range(L)]
        res = adamw_scattered(turn(w), turn(m), turn(v), [chip_sum[n][0] for n in names], [recv_b[n] for n in names], f"adamw_{nm}")
        upd[nm] = tuple(turn(r) for r in res)

    d_rel = rel_bias_grad(small["dsc"], bprev, bcur, "rel_bias_grad")[:, :8]

    PW = max(D, SB_W + SWA_W)

    def pack(ffn1, mix, ffn2, final, osb, osw, snk, rel):
        wide = lambda a: jnp.pad(a.reshape(-1), (0, PW - a.size))
        rows = [wide(ffn1[l]) for l in range(L)] + [wide(mix[l]) for l in range(L)] + [wide(ffn2[l]) for l in range(L)]
        rows.append(wide(final))
        rows += [wide(jnp.concatenate([osb[l].reshape(-1), osw[l].reshape(-1)])) for l in range(L)]
        rows.append(wide(jnp.concatenate([snk[l].reshape(-1)[:8] for l in range(L)] + [rel.reshape(-1)])))
        arr = jnp.stack(rows)
        return jnp.pad(arr, ((0, (-arr.shape[0]) % 8), (0, 0)))

    def unpack(arr):
        ffn1, mix, ffn2 = arr[0:L, :D], arr[L:2 * L, :D], arr[2 * L:3 * L, :D]
        final = arr[3 * L, :D]
        ob = arr[3 * L + 1:4 * L + 1]
        tail = arr[4 * L + 1]
        return (ffn1, mix, tail[:8 * L].reshape(L, 8), ob[:, :SB_W], ob[:, SB_W:SB_W + SWA_W], ffn2,
                tail[8 * L:8 * L + N_BUCKETS * 8].reshape(N_BUCKETS, 8), final)

    g_small = pack(small["ffn1"], small["mix"], small["ffn2"], dg_final, small["osb"], small["osw"], small["sinks"], d_rel)
    w_small = pack(norm_ffn1, norm_mix, norm_ffn2, norm_final, norm_out_sb, norm_out_swa, sinks, rel_bias)
    m_small = pack(m_norm_ffn1, m_norm_mix, m_norm_ffn2, m_norm_final, m_norm_out_sb, m_norm_out_swa, m_sinks, m_rel_bias)
    v_small = pack(v_norm_ffn1, v_norm_mix, v_norm_ffn2, v_norm_final, v_norm_out_sb, v_norm_out_swa, v_sinks, v_rel_bias)
    gs_small = all_gather_rows(g_small, "ag_small")
    small_out = [unpack(a) for a in adamw_small(w_small, gs_small, m_small, v_small, "adamw_small")]

    def group(k):
        sm = small_out[k]
        return (sm[0], upd["gu1"][k], upd["d1"][k], sm[1], upd["in"][k], sm[2], sm[3], sm[4], upd["out"][k], sm[5],
                upd["gu2"][k], upd["d2"][k], sm[6], sm[7])

    return (loss, grad_x, *group(0), *group(1), *group(2), *group(3))
```

```python
import math

import jax
import jax.numpy as jnp
from jax import lax
from jax.experimental import pallas as pl
from jax.experimental.pallas import tpu as pltpu

F32 = jnp.float32
BF16 = jnp.bfloat16
S = jax.ShapeDtypeStruct

N_DEV = 8
HEAD_DIM = 64
SB_HEADS = 8
PAIR = 2 * HEAD_DIM
SB_W = 512
SWA_W = 512
KV_W = 128
IN_W = 3 * SB_W + SWA_W + 2 * KV_W
QB = 128
N_BUCKETS = 32
MAX_DISTANCE = 128
EPS = 1e-6
NEG_INF = -1e30
SCALE = HEAD_DIM ** -0.5

ADAM_LR = 0.001
ADAM_B1 = 0.9
ADAM_B2 = 0.999
ADAM_EPS = 1e-08
ADAM_WD = 0.01
ADAM_STEP = 10

VMEM_LIMIT = 56 * 1024 * 1024
MESH = pl.DeviceIdType.MESH


def _params(sem=None, vmem=VMEM_LIMIT):
    return pltpu.CompilerParams(dimension_semantics=sem, vmem_limit_bytes=vmem)


def _nn(a, b):
    return jnp.dot(a, b, preferred_element_type=F32)


def _nt(a, b):
    return lax.dot_general(a, b, (((1,), (1,)), ((), ())), preferred_element_type=F32)


def _tn(a, b):
    return lax.dot_general(a, b, (((0,), (0,)), ((), ())), preferred_element_type=F32)


def _tri(x, m):
    return _nn(x.astype(BF16), m)


def _rms(x, g):
    r = lax.rsqrt(jnp.mean(x * x, axis=-1, keepdims=True) + EPS)
    return x * r * g


def _rms_bwd(dy, x, g):
    r = lax.rsqrt(jnp.mean(x * x, axis=-1, keepdims=True) + EPS)
    xhat = x * r
    u = dy * g
    dx = r * (u - xhat * jnp.mean(u * xhat, axis=-1, keepdims=True))
    return dx, jnp.sum(dy * xhat, axis=0, keepdims=True)


def _softplus_logsig(z):
    sp = jnp.maximum(z, 0.0) + jnp.log(1.0 + jnp.exp(-jnp.abs(z)))
    return sp, z - sp


def _tile(n, want):
    t = min(n, want)
    while n % t:
        t //= 2
    return t


def _place():
    x, y, c = lax.axis_index("x"), lax.axis_index("y"), lax.axis_index("c")
    chips = [(1 - x, y), (x, 1 - y), (1 - x, 1 - y)]
    return x, y, c, chips


def all_gather_rows(v, name):
    R, C = v.shape

    def body(v_ref, out_ref, send_sems, recv_sems, local_sem):
        x, y, c, chips = _place()
        me, sibling = (x, y, c), (x, y, 1 - c)

        def slot(px, py, pc):
            return out_ref.at[4 * px + 2 * py + pc]

        def copy(k, block, to, src=None):
            return pltpu.make_async_remote_copy(
                src_ref=slot(*block) if src is None else src, dst_ref=slot(*block),
                send_sem=send_sems.at[k], recv_sem=recv_sems.at[k], device_id=to, device_id_type=MESH)

        mine = pltpu.make_async_copy(v_ref, slot(*me), local_sem)
        mine.start()
        first = [copy(0, me, sibling, src=v_ref)]
        first += [copy(1 + j, me, (*chip, c), src=v_ref) for j, chip in enumerate(chips)]
        for cp in first:
            cp.start()
        passed = [copy(4 + j, (*chip, c), sibling) for j, chip in enumerate(chips)]
        for j, chip in enumerate(chips):
            copy(1 + j, (*chip, c), me).wait_recv()
            passed[j].start()
        copy(0, sibling, me).wait_recv()
        for j, chip in enumerate(chips):
            copy(4 + j, (*chip, 1 - c), me).wait_recv()
        for cp in first + passed:
            cp.wait_send()
        mine.wait()

    return pl.pallas_call(
        body, name=name, out_shape=S((N_DEV, R, C), v.dtype),
        in_specs=[pl.BlockSpec(memory_space=pl.ANY)], out_specs=pl.BlockSpec(memory_space=pl.ANY),
        scratch_shapes=[pltpu.SemaphoreType.DMA((7,)), pltpu.SemaphoreType.DMA((7,)), pltpu.SemaphoreType.DMA],
    )(v)


class _Exchange:
    def __init__(self, ins, outs, n_first, n_second, n_local, plan, aliases=None):
        self.ins, self.outs, self.plan, self.aliases = list(ins), list(outs), plan, aliases or {}
        self.n_first, self.n_second, self.n_local = n_first, n_second, n_local

    def scratch(self):
        n = self.n_first + self.n_second
        return [pltpu.SemaphoreType.DMA((n,)), pltpu.SemaphoreType.DMA((n,)), pltpu.SemaphoreType.DMA((max(self.n_local, 1),))]

    def _copies(self, in_refs, out_refs, sems):
        send_sems, recv_sems, local_sems = sems
        first, second, local = self.plan(in_refs, out_refs)
        rem = [pltpu.make_async_remote_copy(src_ref=s, dst_ref=d, send_sem=send_sems.at[k], recv_sem=recv_sems.at[k],
                                            device_id=dev, device_id_type=MESH) for k, (s, d, dev) in enumerate(first + second)]
        loc = [pltpu.make_async_copy(s, d, local_sems.at[k]) for k, (s, d) in enumerate(local)]
        return rem[:len(first)], rem[len(first):], loc

    def start(self, in_refs, out_refs, sems):
        first, _, loc = self._copies(in_refs, out_refs, sems)
        for cp in first + loc:
            cp.start()

    def middle(self, in_refs, out_refs, sems):
        first, second, _ = self._copies(in_refs, out_refs, sems)
        if second:
            for cp in first:
                cp.wait_recv()
            for cp in second:
                cp.start()

    def finish(self, in_refs, out_refs, sems):
        first, second, loc = self._copies(in_refs, out_refs, sems)
        for cp in second if second else first:
            cp.wait_recv()
        for cp in first + second:
            cp.wait_send()
        for cp in loc:
            cp.wait()


def gather(v, rows=None, into=None):
    R, C = v.shape
    r0, nr = rows or (0, R)

    def plan(ins, outs):
        x, y, c, chips = _place()
        slot = lambda px, py, pc: outs[0].at[4 * px + 2 * py + pc, pl.ds(r0, nr), :]
        src, mine = ins[0].at[pl.ds(r0, nr), :], slot(x, y, c)
        first = [(src, mine, (x, y, 1 - c))] + [(src, mine, (*chip, c)) for chip in chips]
        second = [(slot(*chip, c), slot(*chip, c), (x, y, 1 - c)) for chip in chips]
        return first, second, [(src, mine)]

    if into is None:
        return _Exchange([v], [S((N_DEV, R, C), v.dtype)], 4, 3, 1, plan)
    return _Exchange([v, into], [S((N_DEV, R, C), v.dtype)], 4, 3, 1, plan, aliases={1: 0})


def scatter_first(gb):
    _, R, C = gb.shape

    def plan(ins, outs):
        x, y, c, chips = _place()
        owners = [(x, y)] + chips
        return [(ins[0].at[4 * px + 2 * py + (1 - c)], outs[0].at[j], (x, y, 1 - c)) for j, (px, py) in enumerate(owners)], [], []

    return _Exchange([gb], [S((4, R, C), BF16)], 4, 0, 0, plan)


def scatter_second(sb, rows=None, into=None):
    r0, nr = rows or (0, sb.shape[1])

    def plan(ins, outs):
        x, y, c, chips = _place()
        part = lambda ref, j: ref.at[j, pl.ds(r0, nr), :]
        return [(part(ins[0], j), part(outs[0], j), (*chips[j], c)) for j in range(3)], [], []

    if into is None:
        return _Exchange([sb], [S(sb.shape, BF16)], 3, 0, 0, plan)
    return _Exchange([sb, into], [S(sb.shape, BF16)], 3, 0, 0, plan, aliases={1: 0})


def _call(body, *, name, grid, in_specs, out_specs, out_shape, args, scratch=(), sem=None, riders=()):
    single = not isinstance(out_shape, (tuple, list))
    out_shape = (out_shape,) if single else tuple(out_shape)
    out_specs = (out_specs,) if single else tuple(out_specs)
    n_in, n_out, n_sc = len(in_specs), len(out_shape), len(scratch)
    if not riders:
        res = pl.pallas_call(body, name=name, grid=grid, in_specs=list(in_specs), out_specs=out_specs, out_shape=out_shape,
                             scratch_shapes=list(scratch), compiler_params=_params(sem))(*args)
        return res[0] if single else res
    r_ins = [a for r in riders for a in r.ins]
    r_outs = [o for r in riders for o in r.outs]
    r_scr = [s for r in riders for s in r.scratch()]
    aliases, i0, o0 = {}, n_in, n_out
    for r in riders:
        for a, b in r.aliases.items():
            aliases[i0 + a] = o0 + b
        i0, o0 = i0 + len(r.ins), o0 + len(r.outs)
    steps = math.prod(grid)

    def full(*refs):
        ins, rin = refs[:n_in], refs[n_in:n_in + len(r_ins)]
        pos = n_in + len(r_ins)
        outs, rout = refs[pos:pos + n_out], refs[pos + n_out:pos + n_out + len(r_outs)]
        pos += n_out + len(r_outs)
        sc, rsc = refs[pos:pos + n_sc], refs[pos + n_sc:]
        step = 0
        for d, n in enumerate(grid):
            step = step * n + pl.program_id(d)

        def each(method):
            i, o = 0, 0
            for k, r in enumerate(riders):
                getattr(r, method)(rin[i:i + len(r.ins)], rout[o:o + len(r.outs)], rsc[3 * k:3 * k + 3])
                i, o = i + len(r.ins), o + len(r.outs)

        @pl.when(step == 0)
        def _():
            each("start")
        body(*ins, *outs, *sc)

        @pl.when(step == max(steps - 1 - max(steps // 8, 1), 0))
        def _():
            each("middle")

        @pl.when(step == steps - 1)
        def _():
            each("finish")

    anywhere = pl.BlockSpec(memory_space=pl.ANY)
    res = pl.pallas_call(
        full, name=name, grid=grid, in_specs=list(in_specs) + [anywhere] * len(r_ins),
        out_specs=out_specs + (anywhere,) * len(r_outs), out_shape=out_shape + tuple(r_outs),
        scratch_shapes=list(scratch) + r_scr, input_output_aliases=aliases,
        compiler_params=_params(("arbitrary",) * len(grid)))(*args, *r_ins)
    host, rest, per = res[:n_out], list(res[n_out:]), []
    for r in riders:
        per.append(rest[:len(r.outs)])
        rest = rest[len(r.outs):]
    return (host[0] if single else tuple(host)), per


def idle_host(riders, name):
    def body(o_ref):
        o_ref[...] = jnp.zeros_like(o_ref)

    return _call(body, name=name, grid=(1,), in_specs=[], out_specs=pl.BlockSpec((8, QB), lambda i: (0, 0)),
                 out_shape=S((8, QB), F32), args=(), riders=riders)[1]


def _rows_tile(n, cap):
    return max(t for t in range(16, min(n, cap) + 1, 16) if n % t == 0)


def scatter_add(g, ra, name):
    _, R, C = g.shape
    tr = _rows_tile(R, 176)
    x, y, c, chips = _place()
    slots = jnp.stack([4 * px + 2 * py + c for px, py in [(x, y)] + chips]).astype(jnp.int32)

    def body(s_ref, g0, g1, g2, g3, ra_ref, own_ref, sb_ref):
        own_ref[...] = g0[...] + ra_ref[0].astype(F32)
        for j, gj in enumerate((g1, g2, g3)):
            sb_ref[j] = (gj[...] + ra_ref[j + 1].astype(F32)).astype(BF16)

    spec = pltpu.PrefetchScalarGridSpec(
        num_scalar_prefetch=1, grid=(R // tr,),
        in_specs=[pl.BlockSpec((None, tr, C), lambda i, s, j=j: (s[j], i, 0)) for j in range(4)]
        + [pl.BlockSpec((4, tr, C), lambda i, s: (0, i, 0))],
        out_specs=(pl.BlockSpec((tr, C), lambda i, s: (i, 0)), pl.BlockSpec((3, tr, C), lambda i, s: (0, i, 0))))
    return pl.pallas_call(body, name=name, grid_spec=spec, out_shape=(S((R, C), F32), S((3, R, C), BF16)),
                          compiler_params=_params(("parallel",)))(slots, g, g, g, g, ra)


def rms_cast(h, g, name):
    T, D = h.shape
    tm = _tile(T, 512)

    def body(h_ref, g_ref, n_ref):
        n_ref[...] = _rms(h_ref[...], g_ref[...]).astype(BF16)

    row = pl.BlockSpec((tm, D), lambda i: (i, 0))
    return _call(body, name=name, grid=(T // tm,), out_shape=S((T, D), BF16), in_specs=[row, pl.BlockSpec((1, D), lambda i: (0, 0))],
                 out_specs=row, sem=("parallel",), args=(h, g))


def ffn_up_fwd(n, wgu, name, riders=()):
    T, D = n.shape
    F = wgu.shape[1]
    tr, tn = _tile(T, 512), _tile(F, 256)

    def body(n_ref, wg_ref, wu_ref, dgate_ref, dup_ref, a_ref):
        wg, wu = wg_ref[...], wu_ref[...]
        for r in range(T // tr):
            rows = slice(r * tr, (r + 1) * tr)
            x = n_ref[rows, :]
            gate = _nt(x, wg)
            up = _nt(x, wu)
            s = jax.nn.sigmoid(gate)
            silu = gate * s
            dgate_ref[rows, :] = (up * (s * (1.0 + gate * (1.0 - s)))).astype(BF16)
            dup_ref[rows, :] = silu.astype(BF16)
            a_ref[rows, :] = (silu * up).astype(BF16)

    tile = pl.BlockSpec((T, tn), lambda j: (0, j))
    return _call(
        body, name=name, grid=(F // tn,), out_shape=(S((T, F), BF16),) * 3,
        in_specs=[pl.BlockSpec((T, D), lambda j: (0, 0)),
                  pl.BlockSpec((None, tn, D), lambda j: (0, j, 0)), pl.BlockSpec((None, tn, D), lambda j: (1, j, 0))],
        out_specs=(tile, tile, tile), sem=("parallel",), args=(n, wgu, wgu), riders=riders)


def ffn_down_fwd(a, wd, h, g_next, name, riders=()):
    T, F = a.shape
    D = wd.shape[1]
    tm = _tile(T, 256)

    def body(a_ref, w_ref, h_ref, *rest):
        out = h_ref[...] + 0.5 * _nn(a_ref[...], w_ref[...])
        if g_next is None:
            rest[0][...] = out
        else:
            g_ref, o_ref, n_ref = rest
            o_ref[...] = out
            n_ref[...] = _rms(out, g_ref[...]).astype(BF16)

    row = pl.BlockSpec((tm, D), lambda i: (i, 0))
    more = g_next is not None
    return _call(
        body, name=name, grid=(T // tm,), out_shape=(S((T, D), F32), S((T, D), BF16)) if more else S((T, D), F32),
        in_specs=[pl.BlockSpec((tm, F), lambda i: (i, 0)), pl.BlockSpec((F, D), lambda i: (0, 0)), row]
        + ([pl.BlockSpec((1, D), lambda i: (0, 0))] if more else []),
        out_specs=(row, row) if more else row,
        sem=("parallel",), args=(a, wd, h) + ((g_next,) if more else ()), riders=riders)


def mix_in_fwd(h, g, win, name):
    T, D = h.shape
    N = win.shape[0]
    tm = _tile(T, 256)

    def body(h_ref, g_ref, w_ref, n_ref, p_ref):
        n = _rms(h_ref[...], g_ref[...]).astype(BF16)
        n_ref[...] = n
        p_ref[...] = _nt(n, w_ref[...]).astype(BF16)

    return pl.pallas_call(
        body, name=name, grid=(T // tm,), out_shape=(S((T, D), BF16), S((T, N), BF16)),
        in_specs=[pl.BlockSpec((tm, D), lambda i: (i, 0)), pl.BlockSpec((1, D), lambda i: (0, 0)),
                  pl.BlockSpec((N, D), lambda i: (0, 0))],
        out_specs=(pl.BlockSpec((tm, D), lambda i: (i, 0)), pl.BlockSpec((tm, N), lambda i: (i, 0))),
        compiler_params=_params(("parallel",)),
    )(h, g, win)


def _tri_consts():
    r = lax.broadcasted_iota(jnp.int32, (QB, QB), 0)
    c = lax.broadcasted_iota(jnp.int32, (QB, QB), 1)
    ones = jnp.ones((QB, QB), BF16)
    with_sums = lambda tri: jnp.concatenate([tri.astype(BF16), ones], axis=1)
    return with_sums(r > c), with_sums(r <= c), with_sums(r < c)


def _half_masks():
    lane = lax.broadcasted_iota(jnp.int32, (QB, PAIR), 1)
    row = lax.broadcasted_iota(jnp.int32, (QB, PAIR), 0)
    return lane < HEAD_DIM, lane, row


def sb_attn_fwd(p, after, name, riders=()):
    T = p.shape[0]
    nq = T // QB

    def body(q_ref, k_ref, v_ref, m_ref, o_ref, tot_ref, q_sc, acc_ref, z_sc):
        i = pl.program_id(0)
        lo, lane, row = _half_masks()
        causal = lane < row
        heads, pairs = range(SB_HEADS), range(SB_HEADS // 2)
        for hp in pairs:
            q_sc[hp] = (q_ref[:, hp * PAIR:(hp + 1) * PAIR].astype(F32) * SCALE).astype(BF16)
        m2 = m_ref[...]

        def by_head(ref, j, hp):
            t = ref[pl.ds(pl.multiple_of(j * QB, QB), QB), hp * PAIR:(hp + 1) * PAIR]
            return jnp.concatenate([jnp.where(lo, t, 0), jnp.where(lo, 0, t)], axis=0)

        def scores(j):
            return [_nt(q_sc[hp], by_head(k_ref, j, hp)) for hp in pairs]

        def block(j, diag):
            z2 = [z_sc[hp] for hp in pairs]
            ahead = scores(jnp.maximum(j - 1, 0))
            for hp in pairs:
                z_sc[hp] = ahead[hp]
            vs = [by_head(v_ref, j, hp) for hp in pairs]
            spls = [_softplus_logsig(z2[h // 2][:, (h % 2) * QB:(h % 2 + 1) * QB]) for h in heads]
            sp = [jnp.where(causal, spls[h][0], 0.0) if diag else spls[h][0] for h in heads]
            rr = [_tri(sp[h], m2) for h in heads]
            if diag:
                w = [jnp.where(causal, jnp.exp(spls[h][1] - rr[h][:, :QB]), 0.0).astype(BF16) for h in heads]
            else:
                c = [tot_ref[:, h * QB:(h + 1) * QB] for h in heads]
                w = [jnp.exp(spls[h][1] - (c[h] + rr[h][:, :QB])).astype(BF16) for h in heads]
            pv = [_nn(jnp.concatenate([w[2 * hp], w[2 * hp + 1]], axis=1), vs[hp]) for hp in pairs]
            for hp in pairs:
                acc_ref[hp] = pv[hp] if diag else acc_ref[hp] + pv[hp]
            for h in heads:
                tot_ref[:, h * QB:(h + 1) * QB] = rr[h][:, QB:] if diag else c[h] + rr[h][:, QB:]

        first = scores(i)
        for hp in pairs:
            z_sc[hp] = first[hp]
        block(i, True)

        def step(t, carry):
            block(i - 1 - t, False)
            return carry
        lax.fori_loop(0, i, step, 0)
        for hp in pairs:
            o_ref[:, hp * PAIR:(hp + 1) * PAIR] = acc_ref[hp]

    npair = SB_HEADS // 2
    return _call(
        body, name=name, grid=(nq,), out_shape=(S((T, SB_W), F32), S((T, SB_HEADS * QB), F32)),
        in_specs=[pl.BlockSpec((QB, SB_W), lambda i: (i, 0)), pl.BlockSpec((T, SB_W), lambda i: (0, 1)),
                  pl.BlockSpec((T, SB_W), lambda i: (0, 2)), pl.BlockSpec((QB, 2 * QB), lambda i: (0, 0))],
        out_specs=(pl.BlockSpec((QB, SB_W), lambda i: (i, 0)), pl.BlockSpec((QB, SB_HEADS * QB), lambda i: (i, 0))),
        scratch=[pltpu.VMEM((npair, QB, PAIR), BF16), pltpu.VMEM((npair, QB, PAIR), F32), pltpu.VMEM((npair, QB, 2 * QB), F32)],
        sem=("arbitrary",), args=(p, p, p, after), riders=riders)


def sb_attn_bwd(p, do, tot, upto, before, name, riders=()):
    T = p.shape[0]
    nq = T // QB

    def body(q_ref, k_ref, v_ref, do_ref, tot_ref, mp_ref, mg_ref, dq_ref, dk_ref, dv_ref,
             q_sc, d_sc, qd_sc, pg_sc, dq_acc, dk_acc, dv_acc, zd_sc):
        i = pl.program_id(0)
        lo, lane, row = _half_masks()
        causal = lane < row
        heads, pairs = range(SB_HEADS), range(SB_HEADS // 2)

        def by_head(t):
            return jnp.concatenate([jnp.where(lo, t, 0), jnp.where(lo, 0, t)], axis=0)

        for hp in pairs:
            q2 = (q_ref[:, hp * PAIR:(hp + 1) * PAIR].astype(F32) * SCALE).astype(BF16)
            d2 = do_ref[:, hp * PAIR:(hp + 1) * PAIR].astype(BF16)
            q_sc[hp] = q2
            d_sc[hp] = d2
            qd_sc[hp] = by_head(q2)
            qd_sc[SB_HEADS // 2 + hp] = by_head(d2)
        mp, mg = mp_ref[...], mg_ref[...]

        @pl.when(i == 0)
        def _():
            dk_acc[...] = jnp.zeros_like(dk_acc)
            dv_acc[...] = jnp.zeros_like(dv_acc)
        pg_sc[...] = jnp.zeros_like(pg_sc)
        dq_acc[...] = jnp.zeros_like(dq_acc)

        def rows(ref, j, hp):
            return ref[pl.ds(pl.multiple_of(j * QB, QB), QB), hp * PAIR:(hp + 1) * PAIR]

        def products(j):
            return ([_nt(q_sc[hp], by_head(rows(k_ref, j, hp))) for hp in pairs]
                    + [_nt(d_sc[hp], by_head(rows(v_ref, j, hp))) for hp in pairs])

        def block(j, diag):
            r0 = pl.multiple_of(j * QB, QB)
            half = lambda t, h: t[:, (h % 2) * QB:(h % 2 + 1) * QB]
            z = [half(zd_sc[h // 2], h) for h in heads]
            dw = [half(zd_sc[SB_HEADS // 2 + h // 2], h) for h in heads]
            if not diag:
                ahead = products(j + 1)
                for hp in range(SB_HEADS):
                    zd_sc[hp] = ahead[hp]
            ks = [by_head(rows(k_ref, j, hp)) for hp in pairs]
            spls = [_softplus_logsig(z[h]) for h in heads]
            sp = [jnp.where(causal, spls[h][0], 0.0) if diag else spls[h][0] for h in heads]
            rr = [_tri(sp[h], mp) for h in heads]
            pc = [pg_sc[2 * h] for h in heads]
            w = [jnp.exp(spls[h][1] - (tot_ref[:, h * QB:(h + 1) * QB] - (pc[h] + rr[h][:, :QB]))) for h in heads]
            if diag:
                w = [jnp.where(causal, w[h], 0.0) for h in heads]
            gg = [dw[h] * w[h] for h in heads]
            rg = [_tri(gg[h], mg) for h in heads]
            gc = [pg_sc[2 * h + 1] for h in heads]
            dz = [gg[h] - (gg[h] + gc[h] + rg[h][:, :QB]) * jnp.exp(spls[h][1]) for h in heads]
            if diag:
                dz = [jnp.where(causal, dz[h], 0.0) for h in heads]
            dzb = [dz[h].astype(BF16) for h in heads]
            wb = [w[h].astype(BF16) for h in heads]
            both = lambda t, hp, axis: jnp.concatenate([t[2 * hp], t[2 * hp + 1]], axis=axis)
            dq = [_nn(both(dzb, hp, 1), ks[hp]) for hp in pairs]
            dk = [_tn(both(dzb, hp, 0), qd_sc[hp]) for hp in pairs]
            dv = [_tn(both(wb, hp, 0), qd_sc[SB_HEADS // 2 + hp]) for hp in pairs]
            for h in heads:
                if not diag:
                    pg_sc[2 * h] = pc[h] + rr[h][:, QB:]
                    pg_sc[2 * h + 1] = gc[h] + rg[h][:, QB:]
            for hp in pairs:
                dq_acc[hp] += dq[hp]
                dk_acc[pl.ds(r0, QB), hp * PAIR:(hp + 1) * PAIR] += dk[hp]
                dv_acc[pl.ds(r0, QB), hp * PAIR:(hp + 1) * PAIR] += dv[hp]

        first = products(0)
        for hp in range(SB_HEADS):
            zd_sc[hp] = first[hp]

        def step(t, carry):
            block(t, False)
            return carry
        lax.fori_loop(0, i, step, 0)
        block(i, True)
        for hp in pairs:
            dq_ref[:, hp * PAIR:(hp + 1) * PAIR] = (dq_acc[hp] * SCALE).astype(BF16)

        @pl.when(i == nq - 1)
        def _():
            dk_ref[...] = dk_acc[...].astype(BF16)
            dv_ref[...] = dv_acc[...].astype(BF16)

    qtile = pl.BlockSpec((QB, SB_W), lambda i: (i, 0))
    whole = pl.BlockSpec((T, SB_W), lambda i: (0, 0))
    const = pl.BlockSpec((QB, 2 * QB), lambda i: (0, 0))
    return _call(
        body, name=name, grid=(nq,), out_shape=(S((T, SB_W), BF16),) * 3,
        in_specs=[qtile, pl.BlockSpec((T, SB_W), lambda i: (0, 1)), pl.BlockSpec((T, SB_W), lambda i: (0, 2)), qtile,
                  pl.BlockSpec((QB, SB_HEADS * QB), lambda i: (i, 0)), const, const],
        out_specs=(qtile, whole, whole),
        scratch=[pltpu.VMEM((SB_HEADS // 2, QB, PAIR), BF16), pltpu.VMEM((SB_HEADS // 2, QB, PAIR), BF16),
                 pltpu.VMEM((SB_HEADS, 2 * QB, PAIR), BF16),
                 pltpu.VMEM((2 * SB_HEADS, QB, QB), F32), pltpu.VMEM((SB_HEADS // 2, QB, PAIR), F32),
                 pltpu.VMEM((T, SB_W), F32), pltpu.VMEM((T, SB_W), F32), pltpu.VMEM((SB_HEADS, QB, 2 * QB), F32)],
        sem=("arbitrary",), args=(p, p, p, do, tot, upto, before), riders=riders)


def _t5_buckets():
    a = lax.broadcasted_iota(jnp.int32, (QB, QB), 0)
    c = lax.broadcasted_iota(jnp.int32, (QB, QB), 1)

    def bucket(dist):
        dist = jnp.maximum(dist, 0)
        max_exact = N_BUCKETS // 2
        d = jnp.maximum(dist, 1).astype(F32)
        large = max_exact + (jnp.log(d / max_exact) / math.log(MAX_DISTANCE / max_exact)
                             * (N_BUCKETS - max_exact)).astype(jnp.int32)
        large = jnp.minimum(large, N_BUCKETS - 1)
        return jnp.where(dist < max_exact, dist, large)

    return bucket(QB + a - c), bucket(a - c)


def _swa_common(i, kp_ref, kc_ref, vp_ref, vc_ref, bp_ref, bc_ref, rb_ref, bias_ref):
    lo, lane, row = _half_masks()

    @pl.when(i == 0)
    def _():
        for blk, b_ref in enumerate((bp_ref, bc_ref)):
            bk = b_ref[...]
            for h in range(8):
                acc = jnp.zeros((QB, QB), F32)
                for b in range(N_BUCKETS):
                    acc = jnp.where(bk == b, rb_ref[b, h], acc)
                bias_ref[h, blk] = acc

    band = [(lane > row) & (i > 0), lane <= row]

    def halves(ref):
        t = ref[...].astype(F32)
        sw = pltpu.roll(t, HEAD_DIM, 1)
        return [[jnp.where(lo, t, 0.0).astype(BF16), jnp.where(lo, 0.0, sw).astype(BF16)],
                [jnp.where(lo, sw, 0.0).astype(BF16), jnp.where(lo, 0.0, t).astype(BF16)]]

    ks = [halves(kp_ref), halves(kc_ref)]
    vs = [halves(vp_ref), halves(vc_ref)]
    return lo, band, ks, vs


def swa_fwd(p, sinks, rel_bias, bprev, bcur, name, riders=()):
    T = p.shape[0]
    nq = T // QB
    kcol, vcol = (3 * SB_W + SWA_W) // KV_W, (3 * SB_W + SWA_W) // KV_W + 1

    def body(q_ref, kp_ref, kc_ref, vp_ref, vc_ref, bp_ref, bc_ref, sink_ref, rb_ref, o_ref, lse_ref, bias_ref):
        i = pl.program_id(0)
        lo, band, ks, vs = _swa_common(i, kp_ref, kc_ref, vp_ref, vc_ref, bp_ref, bc_ref, rb_ref, bias_ref)
        for g in range(4):
            kh = g // 2
            q2 = q_ref[:, g * PAIR:(g + 1) * PAIR]
            outs = []
            for pos in range(2):
                h = 2 * g + pos
                sc = [jnp.where(band[b], _nt(q2, ks[b][kh][pos]) * SCALE + bias_ref[h, b], NEG_INF) for b in range(2)]
                sink = sink_ref[0, h]
                m = jnp.maximum(jnp.maximum(jnp.max(sc[0], axis=1, keepdims=True),
                                            jnp.max(sc[1], axis=1, keepdims=True)), sink)
                e = [jnp.exp(sc[b] - m) for b in range(2)]
                den = jnp.sum(e[0], axis=1, keepdims=True) + jnp.sum(e[1], axis=1, keepdims=True) + jnp.exp(sink - m)
                outs.append(_nn((e[0] / den).astype(BF16), vs[0][kh][pos]) + _nn((e[1] / den).astype(BF16), vs[1][kh][pos]))
                lse_ref[:, h * QB:(h + 1) * QB] = jnp.broadcast_to(m + jnp.log(den), (QB, QB))
            o_ref[:, g * PAIR:(g + 1) * PAIR] = outs[0] + outs[1]

    kv = lambda col, prev: pl.BlockSpec((QB, KV_W), (lambda i: (jnp.maximum(i - 1, 0), col)) if prev else (lambda i: (i, col)))
    full = pl.BlockSpec((QB, QB), lambda i: (0, 0))
    smem = pl.BlockSpec(memory_space=pltpu.SMEM)
    return _call(
        body, name=name, grid=(nq,), out_shape=(S((T, SWA_W), F32), S((T, 8 * QB), F32)),
        in_specs=[pl.BlockSpec((QB, SWA_W), lambda i: (i, 3)), kv(kcol, True), kv(kcol, False), kv(vcol, True), kv(vcol, False),
                  full, full, smem, smem],
        out_specs=(pl.BlockSpec((QB, SWA_W), lambda i: (i, 0)), pl.BlockSpec((QB, 8 * QB), lambda i: (i, 0))),
        scratch=[pltpu.VMEM((8, 2, QB, QB), F32)],
        sem=("arbitrary",), args=(p, p, p, p, p, bprev, bcur, sinks, rel_bias), riders=riders)


def swa_bwd(p, do, lse, sinks, rel_bias, bprev, bcur, name, riders=()):
    T = p.shape[0]
    nq = T // QB
    kcol, vcol = (3 * SB_W + SWA_W) // KV_W, (3 * SB_W + SWA_W) // KV_W + 1

    def body(q_ref, kp_ref, kc_ref, vp_ref, vc_ref, do_ref, lse_ref, bp_ref, bc_ref, sink_ref, rb_ref,
             dq_ref, dk_ref, dv_ref, dsink_ref, dsc_ref, bias_ref, dk_acc, dv_acc):
        i = pl.program_id(0)
        lo, band, ks, vs = _swa_common(i, kp_ref, kc_ref, vp_ref, vc_ref, bp_ref, bc_ref, rb_ref, bias_ref)

        @pl.when(i == 0)
        def _():
            dk_acc[...] = jnp.zeros_like(dk_acc)
            dv_acc[...] = jnp.zeros_like(dv_acc)
            dsc_ref[...] = jnp.zeros_like(dsc_ref)
            dsink_ref[...] = jnp.zeros_like(dsink_ref)

        lane1 = lax.broadcasted_iota(jnp.int32, (1, QB), 1)
        dsink = jnp.zeros((1, QB), F32)
        dk_parts = [[[None, None], [None, None]], [[None, None], [None, None]]]
        dv_parts = [[[None, None], [None, None]], [[None, None], [None, None]]]

        def add(parts, b, pos, kh, val):
            parts[b][pos][kh] = val if parts[b][pos][kh] is None else parts[b][pos][kh] + val

        for g in range(4):
            kh = g // 2
            q2 = q_ref[:, g * PAIR:(g + 1) * PAIR]
            q2f = q2.astype(F32)
            d2f = do_ref[:, g * PAIR:(g + 1) * PAIR]
            d2 = d2f.astype(BF16)
            dq = None
            for pos in range(2):
                h = 2 * g + pos
                keep = lo if pos == 0 else ~lo
                qh = jnp.where(keep, q2f, 0.0).astype(BF16)
                dh = jnp.where(keep, d2f, 0.0).astype(BF16)
                lse_h = lse_ref[:, h * QB:(h + 1) * QB]
                sink = sink_ref[0, h]
                pr = [jnp.exp(jnp.where(band[b], _nt(q2, ks[b][kh][pos]) * SCALE + bias_ref[h, b], NEG_INF) - lse_h)
                      for b in range(2)]
                dp = [_nt(d2, vs[b][kh][pos]) for b in range(2)]
                delta = jnp.sum(pr[0] * dp[0], axis=1, keepdims=True) + jnp.sum(pr[1] * dp[1], axis=1, keepdims=True)
                p_sink = jnp.exp(sink - lse_h[:, :1])
                dsink = dsink + jnp.where(lane1 == h, -jnp.sum(p_sink * delta), 0.0)
                for b in range(2):
                    dsc = pr[b] * (dp[b] - delta)
                    dsc_ref[h, b] += dsc
                    dzb = (dsc * SCALE).astype(BF16)
                    t = _nn(dzb, ks[b][kh][pos])
                    dq = t if dq is None else dq + t
                    add(dk_parts, b, pos, kh, _tn(dzb, qh))
                    add(dv_parts, b, pos, kh, _tn(pr[b].astype(BF16), dh))
            dq_ref[:, g * PAIR:(g + 1) * PAIR] = dq.astype(BF16)
        dsink_ref[...] += dsink

        def fold(parts, b):
            low = parts[b][0][0] + pltpu.roll(parts[b][1][0], HEAD_DIM, 1)
            high = parts[b][1][1] + pltpu.roll(parts[b][0][1], HEAD_DIM, 1)
            return jnp.where(lo, low, high)

        rp = pl.multiple_of(jnp.maximum(i - 1, 0) * QB, QB)
        rc = pl.multiple_of(i * QB, QB)
        dk_acc[pl.ds(rp, QB), :] += fold(dk_parts, 0)
        dv_acc[pl.ds(rp, QB), :] += fold(dv_parts, 0)
        dk_acc[pl.ds(rc, QB), :] += fold(dk_parts, 1)
        dv_acc[pl.ds(rc, QB), :] += fold(dv_parts, 1)

        @pl.when(i == nq - 1)
        def _():
            dk_ref[...] = dk_acc[...].astype(BF16)
            dv_ref[...] = dv_acc[...].astype(BF16)

    kv = lambda col, prev: pl.BlockSpec((QB, KV_W), (lambda i: (jnp.maximum(i - 1, 0), col)) if prev else (lambda i: (i, col)))
    full = pl.BlockSpec((QB, QB), lambda i: (0, 0))
    smem = pl.BlockSpec(memory_space=pltpu.SMEM)
    whole = lambda shape: pl.BlockSpec(shape, lambda i: (0,) * len(shape))
    return _call(
        body, name=name, grid=(nq,),
        out_shape=(S((T, SWA_W), BF16), S((T, KV_W), BF16), S((T, KV_W), BF16), S((1, QB), F32), S((8, 2, QB, QB), F32)),
        in_specs=[pl.BlockSpec((QB, SWA_W), lambda i: (i, 3)), kv(kcol, True), kv(kcol, False), kv(vcol, True), kv(vcol, False),
                  pl.BlockSpec((QB, SWA_W), lambda i: (i, 0)), pl.BlockSpec((QB, 8 * QB), lambda i: (i, 0)),
                  full, full, smem, smem],
        out_specs=(pl.BlockSpec((QB, SWA_W), lambda i: (i, 0)), whole((T, KV_W)), whole((T, KV_W)), whole((1, QB)),
                   whole((8, 2, QB, QB))),
        scratch=[pltpu.VMEM((8, 2, QB, QB), F32), pltpu.VMEM((T, KV_W), F32), pltpu.VMEM((T, KV_W), F32)],
        sem=("arbitrary",), args=(p, p, p, p, p, do, lse, bprev, bcur, sinks, rel_bias), riders=riders)


def mix_out_fwd(o_sb, o_sw, g_sb, g_sw, wout, h, g_next, name, riders=()):
    T, D = h.shape
    M = SB_W + SWA_W
    tm = _tile(T, 256)

    def body(a_ref, b_ref, ga_ref, gb_ref, w_ref, h_ref, gn_ref, mx_ref, o_ref, n_ref):
        mx_ref[:, :SB_W] = _rms(a_ref[...], ga_ref[...]).astype(BF16)
        mx_ref[:, SB_W:] = _rms(b_ref[...], gb_ref[...]).astype(BF16)
        out = h_ref[...] + _nn(mx_ref[...], w_ref[...])
        o_ref[...] = out
        n_ref[...] = _rms(out, gn_ref[...]).astype(BF16)

    row = lambda n: pl.BlockSpec((tm, n), lambda i: (i, 0))
    vec = lambda n: pl.BlockSpec((1, n), lambda i: (0, 0))
    return _call(
        body, name=name, grid=(T // tm,), out_shape=(S((T, M), BF16), S((T, D), F32), S((T, D), BF16)),
        in_specs=[row(SB_W), row(SWA_W), vec(SB_W), vec(SWA_W), pl.BlockSpec((M, D), lambda i: (0, 0)), row(D), vec(D)],
        out_specs=(row(M), row(D), row(D)),
        sem=("parallel",), args=(o_sb, o_sw, g_sb, g_sw, wout, h, g_next), riders=riders)


def loss_head(h, g, target, name):
    T, D = h.shape
    tm = _tile(T, 256)

    def body(h_ref, g_ref, t_ref, loss_ref, dh_ref, dhb_ref, dg_ref):
        @pl.when(pl.program_id(0) == 0)
        def _():
            loss_ref[...] = jnp.zeros_like(loss_ref)
            dg_ref[...] = jnp.zeros_like(dg_ref)
        x = h_ref[...]
        err = _rms(x, g_ref[...]) - t_ref[...]
        loss_ref[...] += jnp.full((1, QB), 0.5 * jnp.sum(jnp.mean(err * err, axis=-1)), F32)
        dx, dg = _rms_bwd(err / D, x, g_ref[...])
        dh_ref[...] = dx
        dhb_ref[...] = dx.astype(BF16)
        dg_ref[...] += dg

    row = pl.BlockSpec((tm, D), lambda i: (i, 0))
    vec = pl.BlockSpec((1, D), lambda i: (0, 0))
    return pl.pallas_call(
        body, name=name, grid=(T // tm,), out_shape=(S((1, QB), F32), S((T, D), F32), S((T, D), BF16), S((1, D), F32)),
        in_specs=[row, vec, row], out_specs=(pl.BlockSpec((1, QB), lambda i: (0, 0)), row, row, vec),
        compiler_params=_params(("arbitrary",)),
    )(h, g, target)


def ffn_down_bwd(dhb, wd, gate, up, name, riders=()):
    T, D = dhb.shape
    F = wd.shape[0]
    tr, tn = _tile(T, 512), _tile(F, 256)

    def body(d_ref, w_ref, g_ref, u_ref, o_ref):
        w = w_ref[...]
        for r in range(T // tr):
            rows = slice(r * tr, (r + 1) * tr)
            da = 0.5 * _nt(d_ref[rows, :], w)
            o_ref[0, rows, :] = (da * g_ref[rows, :].astype(F32)).astype(BF16)
            o_ref[1, rows, :] = (da * u_ref[rows, :].astype(F32)).astype(BF16)

    tile = pl.BlockSpec((T, tn), lambda j: (0, j))
    return _call(
        body, name=name, grid=(F // tn,), out_shape=S((2, T, F), BF16),
        in_specs=[pl.BlockSpec((T, D), lambda j: (0, 0)), pl.BlockSpec((tn, D), lambda j: (j, 0)), tile, tile],
        out_specs=pl.BlockSpec((2, T, tn), lambda j: (0, 0, j)),
        sem=("parallel",), args=(dhb, wd, gate, up), riders=riders)


def tn_matmul(xs, y, alpha, name, riders=()):
    B, T, N = xs.shape
    D = y.shape[1]
    tn = _tile(N, 256)

    def body(x_ref, y_ref, o_ref, ob_ref):
        o = alpha * _tn(x_ref[...], y_ref[...])
        o_ref[...] = o
        ob_ref[...] = o.astype(BF16)

    tile = pl.BlockSpec((None, tn, D), lambda s, j: (s, j, 0))
    return _call(
        body, name=name, grid=(B, N // tn), out_shape=(S((B, N, D), F32), S((B, N, D), BF16)),
        in_specs=[pl.BlockSpec((None, T, tn), lambda s, j: (s, 0, j)), pl.BlockSpec((T, D), lambda s, j: (0, 0))],
        out_specs=(tile, tile), sem=("parallel", "parallel"), args=(xs, y), riders=riders)


def nn_rms_bwd(xs, ws, h_in, g, dh, name, riders=()):
    B, T, K = xs.shape
    D = ws.shape[2]
    tm = _tile(T, 256)

    def body(x_ref, w_ref, h_ref, g_ref, d_ref, o_ref, ob_ref, dg_ref):
        @pl.when(pl.program_id(0) == 0)
        def _():
            dg_ref[...] = jnp.zeros_like(dg_ref)
        dn = _nn(x_ref[0], w_ref[0])
        for s in range(1, B):
            dn = dn + _nn(x_ref[s], w_ref[s])
        dx, dg = _rms_bwd(dn, h_ref[...], g_ref[...])
        out = d_ref[...] + dx
        o_ref[...] = out
        ob_ref[...] = out.astype(BF16)
        dg_ref[...] += dg

    row = pl.BlockSpec((tm, D), lambda i: (i, 0))
    vec = pl.BlockSpec((1, D), lambda i: (0, 0))
    return _call(
        body, name=name, grid=(T // tm,), out_shape=(S((T, D), F32), S((T, D), BF16), S((1, D), F32)),
        in_specs=[pl.BlockSpec((B, tm, K), lambda i: (0, i, 0)), pl.BlockSpec((B, K, D), lambda i: (0, 0, 0)), row, vec, row],
        out_specs=(row, row, vec),
        sem=("arbitrary",), args=(xs, ws, h_in, g, dh), riders=riders)


def mix_out_bwd(dhb, wout, o_sb, o_sw, g_sb, g_sw, name):
    T, D = dhb.shape
    tm = _tile(T, 256)

    def body(d_ref, w_ref, a_ref, b_ref, ga_ref, gb_ref, da_ref, db_ref, dga_ref, dgb_ref):
        @pl.when(pl.program_id(0) == 0)
        def _():
            dga_ref[...] = jnp.zeros_like(dga_ref)
            dgb_ref[...] = jnp.zeros_like(dgb_ref)
        dm = _nt(d_ref[...], w_ref[...])
        dxa, dga = _rms_bwd(dm[:, :SB_W], a_ref[...], ga_ref[...])
        dxb, dgb = _rms_bwd(dm[:, SB_W:], b_ref[...], gb_ref[...])
        da_ref[...] = dxa
        db_ref[...] = dxb
        dga_ref[...] += dga
        dgb_ref[...] += dgb

    row = lambda n: pl.BlockSpec((tm, n), lambda i: (i, 0))
    vec = lambda n: pl.BlockSpec((1, n), lambda i: (0, 0))
    return pl.pallas_call(
        body, name=name, grid=(T // tm,),
        out_shape=(S((T, SB_W), F32), S((T, SWA_W), F32), S((1, SB_W), F32), S((1, SWA_W), F32)),
        in_specs=[row(D), pl.BlockSpec((SB_W + SWA_W, D), lambda i: (0, 0)), row(SB_W), row(SWA_W), vec(SB_W), vec(SWA_W)],
        out_specs=(row(SB_W), row(SWA_W), vec(SB_W), vec(SWA_W)),
        compiler_params=_params(("arbitrary",)),
    )(dhb, wout, o_sb, o_sw, g_sb, g_sw)


def rel_bias_grad(dscs, bprev, bcur, name):
    n = len(dscs)

    def body(*refs):
        bp_ref, bc_ref, o_ref = refs[n], refs[n + 1], refs[n + 2]
        bks = [bp_ref[...], bc_ref[...]]
        row = lax.broadcasted_iota(jnp.int32, (N_BUCKETS, QB), 0)
        lane = lax.broadcasted_iota(jnp.int32, (N_BUCKETS, QB), 1)
        out = jnp.zeros((N_BUCKETS, QB), F32)
        for h in range(8):
            tot = [sum(refs[l][h, b] for l in range(n)) for b in range(2)]
            for b in range(N_BUCKETS):
                val = jnp.sum(jnp.where(bks[0] == b, tot[0], 0.0)) + jnp.sum(jnp.where(bks[1] == b, tot[1], 0.0))
                out = jnp.where((row == b) & (lane == h), val, out)
        o_ref[...] = out

    return pl.pallas_call(body, name=name, out_shape=S((N_BUCKETS, QB), F32), compiler_params=_params())(*dscs, bprev, bcur)


def _adamw(w, g, m, v):
    m = ADAM_B1 * m + (1.0 - ADAM_B1) * g
    v = ADAM_B2 * v + (1.0 - ADAM_B2) * (g * g)
    m_hat = m / (1.0 - ADAM_B1 ** ADAM_STEP)
    v_hat = v / (1.0 - ADAM_B2 ** ADAM_STEP)
    delta = -ADAM_LR * (m_hat / (jnp.sqrt(v_hat) + ADAM_EPS) + ADAM_WD * w)
    return delta, m, v


def adamw_scattered(w, m, v, owns, others, name):
    L, R, C = w.shape
    tr = _rows_tile(R, 176)

    def body(w_ref, m_ref, v_ref, *rest):
        own_refs, other_refs = rest[:L], rest[L:2 * L]
        g_ref, d_ref, mo_ref, vo_ref = rest[2 * L:]
        layer = pl.program_id(0)

        def grad(k):
            o = other_refs[k]
            return own_refs[k][...] + o[0].astype(F32) + o[1].astype(F32) + o[2].astype(F32)

        g = grad(0)
        for k in range(1, L):
            g = jnp.where(layer == k, grad(k), g)
        d, mn, vn = _adamw(w_ref[...], g, m_ref[...], v_ref[...])
        g_ref[...] = g
        d_ref[...] = d
        mo_ref[...] = mn
        vo_ref[...] = vn

    tile = pl.BlockSpec((None, tr, C), lambda l, i: (l, i, 0))
    return pl.pallas_call(
        body, name=name, grid=(L, R // tr), out_shape=(S((L, R, C), F32),) * 4,
        in_specs=[tile] * 3 + [pl.BlockSpec((tr, C), lambda l, i: (i, 0))] * L + [pl.BlockSpec((3, tr, C), lambda l, i: (0, i, 0))] * L,
        out_specs=(tile,) * 4, compiler_params=_params(("parallel", "parallel")),
    )(w, m, v, *owns, *others)


def adamw_small(w, gs, m, v, name):
    R, C = w.shape

    def body(w_ref, g_ref, m_ref, v_ref, go_ref, d_ref, mo_ref, vo_ref):
        g = g_ref[0]
        for k in range(1, N_DEV):
            g = g + g_ref[k]
        d, mn, vn = _adamw(w_ref[...], g, m_ref[...], v_ref[...])
        go_ref[...] = g
        d_ref[...] = d
        mo_ref[...] = mn
        vo_ref[...] = vn

    return pl.pallas_call(body, name=name, out_shape=(S((R, C), F32),) * 4, compiler_params=_params())(w, gs, m, v)


def kernel(x, norm_ffn1, w_ffn1_gu, w_ffn1_down, norm_mix, w_in, sinks, norm_out_sb, norm_out_swa, w_out, norm_ffn2, w_ffn2_gu, w_ffn2_down, rel_bias, norm_final, loss_target, m_norm_ffn1, m_w_ffn1_gu, m_w_ffn1_down, m_norm_mix, m_w_in, m_sinks, m_norm_out_sb, m_norm_out_swa, m_w_out, m_norm_ffn2, m_w_ffn2_gu, m_w_ffn2_down, m_rel_bias, m_norm_final, v_norm_ffn1, v_w_ffn1_gu, v_w_ffn1_down, v_norm_mix, v_w_in, v_sinks, v_norm_out_sb, v_norm_out_swa, v_w_out, v_norm_ffn2, v_w_ffn2_gu, v_w_ffn2_down, v_rel_bias, v_norm_final):
    L = norm_ffn1.shape[0]
    T, D = x.shape[1], x.shape[2]
    F = w_ffn1_down.shape[1] * N_DEV
    h = x.reshape(T, D)
    target = loss_target.reshape(T, D)
    after, upto, before = _tri_consts()
    bprev, bcur = _t5_buckets()

    local = {}
    for l in range(L):
        local[f"gu1_{l}"] = w_ffn1_gu[l].T.astype(BF16)
        local[f"d1_{l}"] = w_ffn1_down[l].astype(BF16)
        local[f"in_{l}"] = w_in[l].T.astype(BF16)
        local[f"out_{l}"] = w_out[l].astype(BF16)
        local[f"gu2_{l}"] = w_ffn2_gu[l].T.astype(BF16)
        local[f"d2_{l}"] = w_ffn2_down[l].astype(BF16)
    full, partial = {}, {}
    grads, chip_sum, recv_b = {}, {}, {}

    def run(fn, *args, ag=(), rs1=(), rs2=()):
        halves = lambda names: [n if isinstance(n, tuple) else (n, None) for n in names]
        ag, rs2 = [(n, k) for n, k in halves(ag) if n in local], halves(rs2)
        rows = lambda k, total: None if k is None else (k * (total // 2), total // 2)

        def second(n, k):
            sb = chip_sum[n][1]
            return scatter_second(sb, rows(k, sb.shape[1]), recv_b.get(n))

        riders = ([gather(local[n], rows(k, local[n].shape[0]), partial.get(n)) for n, k in ag]
                  + [scatter_first(grads[n][1]) for n in rs1] + [second(n, k) for n, k in rs2])
        if not riders:
            return fn(*args)
        outs, per = fn(*args, riders=riders)
        per = [p[0] for p in per]
        for n, k in ag:
            buf = per.pop(0)
            if k == 0:
                partial[n] = buf
            else:
                full[n] = buf.reshape(N_DEV * buf.shape[1], D)
        for n in rs1:
            chip_sum[n] = scatter_add(grads[n][0], per.pop(0), f"rs_add_{n}")
        for n, _ in rs2:
            recv_b[n] = per.pop(0)
        return outs

    def idle(name, riders=()):
        return None, idle_host(riders, name)

    gu = lambda n: full[n].reshape(2, F, D)
    slots = lambda pair: tuple(t.reshape(N_DEV, -1, D) for t in pair)
    vec = lambda a: a.reshape(1, -1)

    run(idle, "ag_head", ag=("gu1_0",))
    saved = []
    n_next = rms_cast(h, vec(norm_ffn1[0]), "rms_first")
    for l in range(L):
        nx = l + 1
        s = {"h0": h, "n1": n_next}
        s["gate1"], s["up1"], s["a1"] = run(ffn_up_fwd, s["n1"], gu(f"gu1_{l}"), f"ffn1_up{l}",
                                            ag=(f"d1_{l}",) + ((("in_0", 0),) if l == 0 else ()))
        h = run(ffn_down_fwd, s["a1"], full[f"d1_{l}"], h, None, f"ffn1_down{l}", ag=((f"in_{l}", 1),))
        s["h1"] = h
        s["n2"], s["p"] = mix_in_fwd(h, vec(norm_mix[l]), full[f"in_{l}"], f"mix_in{l}")
        s["o_sb"], s["tot"] = run(sb_attn_fwd, s["p"], after, f"sb_fwd{l}", ag=(f"out_{l}", f"gu2_{l}", f"d2_{l}"))
        s["o_sw"], s["lse"] = run(swa_fwd, s["p"], vec(sinks[l]), rel_bias, bprev, bcur, f"swa_fwd{l}", ag=((f"gu1_{nx}", 0),))
        s["mixed"], h, s["n3"] = run(mix_out_fwd, s["o_sb"], s["o_sw"], vec(norm_out_sb[l]), vec(norm_out_swa[l]),
                                     full[f"out_{l}"], h, vec(norm_ffn2[l]), f"mix_out{l}")
        s["h2"] = h
        s["gate2"], s["up2"], s["a2"] = run(ffn_up_fwd, s["n3"], gu(f"gu2_{l}"), f"ffn2_up{l}", ag=((f"gu1_{nx}", 1),))
        if nx < L:
            h, n_next = run(ffn_down_fwd, s["a2"], full[f"d2_{l}"], h, vec(norm_ffn1[nx]), f"ffn2_down{l}", ag=((f"in_{nx}", 0),))
        else:
            h = run(ffn_down_fwd, s["a2"], full[f"d2_{l}"], h, None, f"ffn2_down{l}")
        saved.append(s)

    loss_part, dh, dhb, dg_final = loss_head(h, vec(norm_final), target, "loss_head")
    loss = lax.psum(loss_part[0, 0], ("x", "y", "c"))

    small = {k: [None] * L for k in ("ffn1", "mix", "sinks", "osb", "osw", "ffn2", "dsc")}
    for l in reversed(range(L)):
        s = saved[l]

        def ffn_bwd(dh, dhb, tag, gate, up, a, n, h_in, g, r_down, r_dwgu, r_up):
            gu_n, d_n = f"gu{tag}_{l}", f"d{tag}_{l}"
            dgu = run(ffn_down_bwd, dhb, full[d_n], gate, up, f"ffn{tag}_down_bwd{l}", **r_down)
            grads[gu_n] = slots(run(tn_matmul, dgu, n, 1.0, f"ffn{tag}_dwgu{l}", **r_dwgu))
            grads[d_n] = slots(run(tn_matmul, a[None], dhb, 0.5, f"ffn{tag}_dwd{l}", rs1=(gu_n,)))
            return run(nn_rms_bwd, dgu, gu(gu_n), h_in, g, dh, f"ffn{tag}_up_bwd{l}", rs1=(d_n,), **r_up)

        later = l + 1 < L
        dh, dhb, small["ffn2"][l] = ffn_bwd(dh, dhb, 2, s["gate2"], s["up2"], s["a2"], s["n3"], s["h2"], vec(norm_ffn2[l]),
                                            dict(rs2=((f"gu1_{l + 1}", 1),) if later else ()),
                                            dict(rs2=(f"d1_{l + 1}",) if later else ()), {})
        do_sb, do_sw, small["osb"][l], small["osw"][l] = mix_out_bwd(
            dhb, full[f"out_{l}"], s["o_sb"], s["o_sw"], vec(norm_out_sb[l]), vec(norm_out_swa[l]), f"mix_out_bwd{l}")
        grads[f"out_{l}"] = slots(tn_matmul(s["mixed"][None], dhb, 1.0, f"dwout{l}"))
        dq_sb, dk_sb, dv_sb = run(sb_attn_bwd, s["p"], do_sb, s["tot"], upto, before, f"sb_bwd{l}",
                                  rs2=(f"gu2_{l}", f"d2_{l}"), rs1=(f"out_{l}",))
        dq_sw, dk_sw, dv_sw, small["sinks"][l], small["dsc"][l] = run(
            swa_bwd, s["p"], do_sw, s["lse"], vec(sinks[l]), rel_bias, bprev, bcur, f"swa_bwd{l}", rs2=(f"out_{l}",))
        dp = jnp.concatenate([dq_sb, dk_sb, dv_sb, dq_sw, dk_sw, dv_sw], axis=1)
        dh, dhb, small["mix"][l] = nn_rms_bwd(dp[None], full[f"in_{l}"][None], s["h1"], vec(norm_mix[l]), dh, f"mix_in_bwd{l}")
        grads[f"in_{l}"] = slots(tn_matmul(dp[None], s["n2"], 1.0, f"dwin{l}"))
        dh, dhb, small["ffn1"][l] = ffn_bwd(dh, dhb, 1, s["gate1"], s["up1"], s["a1"], s["n1"], s["h0"], vec(norm_ffn1[l]),
                                            dict(rs1=(f"in_{l}",)), dict(rs2=(f"in_{l}",)), dict(rs2=((f"gu1_{l}", 0),)))

    grad_x = dh.reshape(x.shape)
    run(idle, "rs_tail", rs2=(("gu1_0", 1), "d1_0"))

    upd = {}
    for nm, w, m, v, transposed in (("gu1", w_ffn1_gu, m_w_ffn1_gu, v_w_ffn1_gu, True), ("d1", w_ffn1_down, m_w_ffn1_down, v_w_ffn1_down, False),
                                    ("in", w_in, m_w_in, v_w_in, True), ("out", w_out, m_w_out, v_w_out, False),
                                    ("gu2", w_ffn2_gu, m_w_ffn2_gu, v_w_ffn2_gu, True), ("d2", w_ffn2_down, m_w_ffn2_down, v_w_ffn2_down, False)):
        turn = (lambda a: jnp.swapaxes(a, 1, 2)) if transposed else (lambda a: a)
        names = [f"{nm}_{l}" for l in range(L)]
        res = adamw_scattered(turn(w), turn(m), turn(v), [chip_sum[n][0] for n in names], [recv_b[n] for n in names], f"adamw_{nm}")
        upd[nm] = tuple(turn(r) for r in res)

    d_rel = rel_bias_grad(small["dsc"], bprev, bcur, "rel_bias_grad")[:, :8]

    PW = max(D, SB_W + SWA_W)

    def pack(ffn1, mix, ffn2, final, osb, osw, snk, rel):
        wide = lambda a: jnp.pad(a.reshape(-1), (0, PW - a.size))
        rows = [wide(ffn1[l]) for l in range(L)] + [wide(mix[l]) for l in range(L)] + [wide(ffn2[l]) for l in range(L)]
        rows.append(wide(final))
        rows += [wide(jnp.concatenate([osb[l].reshape(-1), osw[l].reshape(-1)])) for l in range(L)]
        rows.append(wide(jnp.concatenate([snk[l].reshape(-1)[:8] for l in range(L)] + [rel.reshape(-1)])))
        arr = jnp.stack(rows)
        return jnp.pad(arr, ((0, (-arr.shape[0]) % 8), (0, 0)))

    def unpack(arr):
        ffn1, mix, ffn2 = arr[0:L, :D], arr[L:2 * L, :D], arr[2 * L:3 * L, :D]
        final = arr[3 * L, :D]
        ob = arr[3 * L + 1:4 * L + 1]
        tail = arr[4 * L + 1]
        return (ffn1, mix, tail[:8 * L].reshape(L, 8), ob[:, :SB_W], ob[:, SB_W:SB_W + SWA_W], ffn2,
                tail[8 * L:8 * L + N_BUCKETS * 8].reshape(N_BUCKETS, 8), final)

    g_small = pack(small["ffn1"], small["mix"], small["ffn2"], dg_final, small["osb"], small["osw"], small["sinks"], d_rel)
    w_small = pack(norm_ffn1, norm_mix, norm_ffn2, norm_final, norm_out_sb, norm_out_swa, sinks, rel_bias)
    m_small = pack(m_norm_ffn1, m_norm_mix, m_norm_ffn2, m_norm_final, m_norm_out_sb, m_norm_out_swa, m_sinks, m_rel_bias)
    v_small = pack(v_norm_ffn1, v_norm_mix, v_norm_ffn2, v_norm_final, v_norm_out_sb, v_norm_out_swa, v_sinks, v_rel_bias)
    gs_small = all_gather_rows(g_small, "ag_small")
    small_out = [unpack(a) for a in adamw_small(w_small, gs_small, m_small, v_small, "adamw_small")]

    def group(k):
        sm = small_out[k]
        return (sm[0], upd["gu1"][k], upd["d1"][k], sm[1], upd["in"][k], sm[2], sm[3], sm[4], upd["out"][k], sm[5],
                upd["gu2"][k], upd["d2"][k], sm[6], sm[7])

    return (loss, grad_x, *group(0), *group(1), *group(2), *group(3))
```

```python
import math

import jax
import jax.numpy as jnp
from jax import lax
from jax.experimental import pallas as pl
from jax.experimental.pallas import tpu as pltpu

F32 = jnp.float32
BF16 = jnp.bfloat16
S = jax.ShapeDtypeStruct

N_DEV = 8
HEAD_DIM = 64
SB_HEADS = 8
PAIR = 2 * HEAD_DIM
SB_W = 512
SWA_W = 512
KV_W = 128
IN_W = 3 * SB_W + SWA_W + 2 * KV_W
QB = 128
N_BUCKETS = 32
MAX_DISTANCE = 128
EPS = 1e-6
NEG_INF = -1e30
SCALE = HEAD_DIM ** -0.5

ADAM_LR = 0.001
ADAM_B1 = 0.9
ADAM_B2 = 0.999
ADAM_EPS = 1e-08
ADAM_WD = 0.01
ADAM_STEP = 10

VMEM_LIMIT = 56 * 1024 * 1024
MESH = pl.DeviceIdType.MESH


def _params(sem=None, vmem=VMEM_LIMIT):
    return pltpu.CompilerParams(dimension_semantics=sem, vmem_limit_bytes=vmem)


def _nn(a, b):
    return jnp.dot(a, b, preferred_element_type=F32)


def _nt(a, b):
    return lax.dot_general(a, b, (((1,), (1,)), ((), ())), preferred_element_type=F32)


def _tn(a, b):
    return lax.dot_general(a, b, (((0,), (0,)), ((), ())), preferred_element_type=F32)


def _tri(x, m):
    return _nn(x.astype(BF16), m)


def _rms(x, g):
    r = lax.rsqrt(jnp.mean(x * x, axis=-1, keepdims=True) + EPS)
    return x * r * g


def _rms_bwd(dy, x, g):
    r = lax.rsqrt(jnp.mean(x * x, axis=-1, keepdims=True) + EPS)
    xhat = x * r
    u = dy * g
    dx = r * (u - xhat * jnp.mean(u * xhat, axis=-1, keepdims=True))
    return dx, jnp.sum(dy * xhat, axis=0, keepdims=True)


def _softplus_logsig(z):
    sp = jnp.maximum(z, 0.0) + jnp.log(1.0 + jnp.exp(-jnp.abs(z)))
    return sp, z - sp


def _tile(n, want):
    t = min(n, want)
    while n % t:
        t //= 2
    return t


def _place():
    x, y, c = lax.axis_index("x"), lax.axis_index("y"), lax.axis_index("c")
    chips = [(1 - x, y), (x, 1 - y), (1 - x, 1 - y)]
    return x, y, c, chips


def all_gather_rows(v, name):
    R, C = v.shape

    def body(v_ref, out_ref, send_sems, recv_sems, local_sem):
        x, y, c, chips = _place()
        me, sibling = (x, y, c), (x, y, 1 - c)

        def slot(px, py, pc):
            return out_ref.at[4 * px + 2 * py + pc]

        def copy(k, block, to, src=None):
            return pltpu.make_async_remote_copy(
                src_ref=slot(*block) if src is None else src, dst_ref=slot(*block),
                send_sem=send_sems.at[k], recv_sem=recv_sems.at[k], device_id=to, device_id_type=MESH)

        mine = pltpu.make_async_copy(v_ref, slot(*me), local_sem)
        mine.start()
        first = [copy(0, me, sibling, src=v_ref)]
        first += [copy(1 + j, me, (*chip, c), src=v_ref) for j, chip in enumerate(chips)]
        for cp in first:
            cp.start()
        passed = [copy(4 + j, (*chip, c), sibling) for j, chip in enumerate(chips)]
        for j, chip in enumerate(chips):
            copy(1 + j, (*chip, c), me).wait_recv()
            passed[j].start()
        copy(0, sibling, me).wait_recv()
        for j, chip in enumerate(chips):
            copy(4 + j, (*chip, 1 - c), me).wait_recv()
        for cp in first + passed:
            cp.wait_send()
        mine.wait()

    return pl.pallas_call(
        body, name=name, out_shape=S((N_DEV, R, C), v.dtype),
        in_specs=[pl.BlockSpec(memory_space=pl.ANY)], out_specs=pl.BlockSpec(memory_space=pl.ANY),
        scratch_shapes=[pltpu.SemaphoreType.DMA((7,)), pltpu.SemaphoreType.DMA((7,)), pltpu.SemaphoreType.DMA],
    )(v)


class _Exchange:
    def __init__(self, ins, outs, sizes, n_local, plan, aliases=None):
        self.ins, self.outs, self.plan, self.aliases = list(ins), list(outs), plan, aliases or {}
        self.sizes, self.n_local = list(sizes), n_local

    def scratch(self):
        n = sum(self.sizes)
        return [pltpu.SemaphoreType.DMA((n,)), pltpu.SemaphoreType.DMA((n,)), pltpu.SemaphoreType.DMA((max(self.n_local, 1),))]

    def _copies(self, in_refs, out_refs, sems):
        send_sems, recv_sems, local_sems = sems
        phases, local = self.plan(in_refs, out_refs)
        out, k = [], 0
        for phase in phases:
            out.append([pltpu.make_async_remote_copy(src_ref=s, dst_ref=d, send_sem=send_sems.at[k + n], recv_sem=recv_sems.at[k + n],
                                                     device_id=dev, device_id_type=MESH) for n, (s, d, dev) in enumerate(phase)])
            k += len(phase)
        return out, [pltpu.make_async_copy(s, d, local_sems.at[n]) for n, (s, d) in enumerate(local)]

    def start(self, in_refs, out_refs, sems):
        phases, loc = self._copies(in_refs, out_refs, sems)
        for cp in phases[0] + loc:
            cp.start()

    def advance(self, hook, in_refs, out_refs, sems):
        p = hook - (3 - len(self.sizes))
        if p >= 1:
            phases, _ = self._copies(in_refs, out_refs, sems)
            for cp in phases[p - 1]:
                cp.wait_recv()
            for cp in phases[p]:
                cp.start()

    def finish(self, in_refs, out_refs, sems):
        phases, loc = self._copies(in_refs, out_refs, sems)
        for cp in phases[-1]:
            cp.wait_recv()
        for phase in phases:
            for cp in phase:
                cp.wait_send()
        for cp in loc:
            cp.wait()


def gather(v, rows=None, into=None):
    R, C = v.shape
    r0, nr = rows or (0, R)
    na = min(nr, ((nr // 2 + 15) // 16) * 16)

    def plan(ins, outs):
        x, y, c, _ = _place()
        xn, yn, dg, sibling = (1 - x, y), (x, 1 - y), (1 - x, 1 - y), (x, y, 1 - c)
        slot = lambda chip, start=r0, count=nr: outs[0].at[4 * chip[0] + 2 * chip[1] + c, pl.ds(start, count), :]
        src, mine = ins[0].at[pl.ds(r0, nr), :], slot((x, y))
        same = lambda ref, to: (ref, ref, to)
        first = [(src, mine, sibling), (src, mine, (*xn, c)), (src, mine, (*yn, c))]
        relay = [same(slot(xn, r0, na), (*yn, c)), same(slot(yn, r0 + na, nr - na), (*xn, c))]
        onward = [same(slot(xn), sibling), same(slot(yn), sibling), same(slot(dg), sibling)]
        return [first, relay, onward], [(src, mine)]

    if into is None:
        return _Exchange([v], [S((N_DEV, R, C), v.dtype)], (3, 2, 3), 1, plan)
    return _Exchange([v, into], [S((N_DEV, R, C), v.dtype)], (3, 2, 3), 1, plan, aliases={1: 0})


def scatter_first(gb):
    _, R, C = gb.shape

    def plan(ins, outs):
        x, y, c, chips = _place()
        owners = [(x, y)] + chips
        return [[(ins[0].at[4 * px + 2 * py + (1 - c)], outs[0].at[j], (x, y, 1 - c)) for j, (px, py) in enumerate(owners)]], []

    return _Exchange([gb], [S((4, R, C), BF16)], (4,), 0, plan)


def scatter_second(sb, rows=None, into=None):
    r0, nr = rows or (0, sb.shape[1])

    def plan(ins, outs):
        x, y, c, chips = _place()
        part = lambda ref, j: ref.at[j, pl.ds(r0, nr), :]
        return [[(part(ins[0], j), part(outs[0], j), (*chips[j], c)) for j in range(3)]], []

    if into is None:
        return _Exchange([sb], [S(sb.shape, BF16)], (3,), 0, plan)
    return _Exchange([sb, into], [S(sb.shape, BF16)], (3,), 0, plan, aliases={1: 0})


def _call(body, *, name, grid, in_specs, out_specs, out_shape, args, scratch=(), sem=None, riders=(), marks=None):
    single = not isinstance(out_shape, (tuple, list))
    out_shape = (out_shape,) if single else tuple(out_shape)
    out_specs = (out_specs,) if single else tuple(out_specs)
    n_in, n_out, n_sc = len(in_specs), len(out_shape), len(scratch)
    if not riders:
        res = pl.pallas_call(body, name=name, grid=grid, in_specs=list(in_specs), out_specs=out_specs, out_shape=out_shape,
                             scratch_shapes=list(scratch), compiler_params=_params(sem))(*args)
        return res[0] if single else res
    r_ins = [a for r in riders for a in r.ins]
    r_outs = [o for r in riders for o in r.outs]
    r_scr = [s for r in riders for s in r.scratch()]
    aliases, i0, o0 = {}, n_in, n_out
    for r in riders:
        for a, b in r.aliases.items():
            aliases[i0 + a] = o0 + b
        i0, o0 = i0 + len(r.ins), o0 + len(r.outs)
    steps = math.prod(grid)

    def full(*refs):
        ins, rin = refs[:n_in], refs[n_in:n_in + len(r_ins)]
        pos = n_in + len(r_ins)
        outs, rout = refs[pos:pos + n_out], refs[pos + n_out:pos + n_out + len(r_outs)]
        pos += n_out + len(r_outs)
        sc, rsc = refs[pos:pos + n_sc], refs[pos + n_sc:]
        step = 0
        for d, n in enumerate(grid):
            step = step * n + pl.program_id(d)

        def each(method, *lead):
            i, o = 0, 0
            for k, r in enumerate(riders):
                getattr(r, method)(*lead, rin[i:i + len(r.ins)], rout[o:o + len(r.outs)], rsc[3 * k:3 * k + 3])
                i, o = i + len(r.ins), o + len(r.outs)

        @pl.when(step == 0)
        def _():
            each("start")
        body(*ins, *outs, *sc)

        late = max(steps - 1 - max(steps // 8, 1), 0)
        first, second = marks or (min((3 * steps) // 5, late), late)

        @pl.when(step == first)
        def _():
            each("advance", 1)

        @pl.when(step == second)
        def _():
            each("advance", 2)

        @pl.when(step == steps - 1)
        def _():
            each("finish")

    anywhere = pl.BlockSpec(memory_space=pl.ANY)
    res = pl.pallas_call(
        full, name=name, grid=grid, in_specs=list(in_specs) + [anywhere] * len(r_ins),
        out_specs=out_specs + (anywhere,) * len(r_outs), out_shape=out_shape + tuple(r_outs),
        scratch_shapes=list(scratch) + r_scr, input_output_aliases=aliases,
        compiler_params=_params(("arbitrary",) * len(grid)))(*args, *r_ins)
    host, rest, per = res[:n_out], list(res[n_out:]), []
    for r in riders:
        per.append(rest[:len(r.outs)])
        rest = rest[len(r.outs):]
    return (host[0] if single else tuple(host)), per


def idle_host(riders, name):
    def body(o_ref):
        o_ref[...] = jnp.zeros_like(o_ref)

    return _call(body, name=name, grid=(1,), in_specs=[], out_specs=pl.BlockSpec((8, QB), lambda i: (0, 0)),
                 out_shape=S((8, QB), F32), args=(), riders=riders)[1]


def _rows_tile(n, cap):
    return max(t for t in range(16, min(n, cap) + 1, 16) if n % t == 0)


def scatter_add(g, ra, name):
    _, R, C = g.shape
    tr = _rows_tile(R, 176)
    x, y, c, chips = _place()
    slots = jnp.stack([4 * px + 2 * py + c for px, py in [(x, y)] + chips]).astype(jnp.int32)

    def body(s_ref, g0, g1, g2, g3, ra_ref, own_ref, sb_ref):
        own_ref[...] = g0[...] + ra_ref[0].astype(F32)
        for j, gj in enumerate((g1, g2, g3)):
            sb_ref[j] = (gj[...] + ra_ref[j + 1].astype(F32)).astype(BF16)

    spec = pltpu.PrefetchScalarGridSpec(
        num_scalar_prefetch=1, grid=(R // tr,),
        in_specs=[pl.BlockSpec((None, tr, C), lambda i, s, j=j: (s[j], i, 0)) for j in range(4)]
        + [pl.BlockSpec((4, tr, C), lambda i, s: (0, i, 0))],
        out_specs=(pl.BlockSpec((tr, C), lambda i, s: (i, 0)), pl.BlockSpec((3, tr, C), lambda i, s: (0, i, 0))))
    return pl.pallas_call(body, name=name, grid_spec=spec, out_shape=(S((R, C), F32), S((3, R, C), BF16)),
                          compiler_params=_params(("parallel",)))(slots, g, g, g, g, ra)


def rms_cast(h, g, name):
    T, D = h.shape
    tm = _tile(T, 512)

    def body(h_ref, g_ref, n_ref):
        n_ref[...] = _rms(h_ref[...], g_ref[...]).astype(BF16)

    row = pl.BlockSpec((tm, D), lambda i: (i, 0))
    return _call(body, name=name, grid=(T // tm,), out_shape=S((T, D), BF16), in_specs=[row, pl.BlockSpec((1, D), lambda i: (0, 0))],
                 out_specs=row, sem=("parallel",), args=(h, g))


def ffn_up_fwd(n, wgu, name, riders=()):
    T, D = n.shape
    F = wgu.shape[1]
    tr, tn = _tile(T, 512), _tile(F, 256)

    def body(n_ref, wg_ref, wu_ref, dgate_ref, dup_ref, a_ref):
        wg, wu = wg_ref[...], wu_ref[...]
        for r in range(T // tr):
            rows = slice(r * tr, (r + 1) * tr)
            x = n_ref[rows, :]
            gate = _nt(x, wg)
            up = _nt(x, wu)
            s = jax.nn.sigmoid(gate)
            silu = gate * s
            dgate_ref[rows, :] = (up * (s * (1.0 + gate * (1.0 - s)))).astype(BF16)
            dup_ref[rows, :] = silu.astype(BF16)
            a_ref[rows, :] = (silu * up).astype(BF16)

    tile = pl.BlockSpec((T, tn), lambda j: (0, j))
    return _call(
        body, name=name, grid=(F // tn,), out_shape=(S((T, F), BF16),) * 3,
        in_specs=[pl.BlockSpec((T, D), lambda j: (0, 0)),
                  pl.BlockSpec((None, tn, D), lambda j: (0, j, 0)), pl.BlockSpec((None, tn, D), lambda j: (1, j, 0))],
        out_specs=(tile, tile, tile), sem=("parallel",), args=(n, wgu, wgu), riders=riders)


def ffn_down_fwd(a, wd, h, g_next, name, riders=()):
    T, F = a.shape
    D = wd.shape[1]
    tm = _tile(T, 256)

    def body(a_ref, w_ref, h_ref, *rest):
        out = h_ref[...] + 0.5 * _nn(a_ref[...], w_ref[...])
        if g_next is None:
            rest[0][...] = out
        else:
            g_ref, o_ref, n_ref = rest
            o_ref[...] = out
            n_ref[...] = _rms(out, g_ref[...]).astype(BF16)

    row = pl.BlockSpec((tm, D), lambda i: (i, 0))
    more = g_next is not None
    return _call(
        body, name=name, grid=(T // tm,), out_shape=(S((T, D), F32), S((T, D), BF16)) if more else S((T, D), F32),
        in_specs=[pl.BlockSpec((tm, F), lambda i: (i, 0)), pl.BlockSpec((F, D), lambda i: (0, 0)), row]
        + ([pl.BlockSpec((1, D), lambda i: (0, 0))] if more else []),
        out_specs=(row, row) if more else row,
        sem=("parallel",), args=(a, wd, h) + ((g_next,) if more else ()), riders=riders)


def mix_in_fwd(h, g, win, name):
    T, D = h.shape
    N = win.shape[0]
    tm = _tile(T, 256)

    def body(h_ref, g_ref, w_ref, n_ref, p_ref):
        n = _rms(h_ref[...], g_ref[...]).astype(BF16)
        n_ref[...] = n
        p_ref[...] = _nt(n, w_ref[...]).astype(BF16)

    return pl.pallas_call(
        body, name=name, grid=(T // tm,), out_shape=(S((T, D), BF16), S((T, N), BF16)),
        in_specs=[pl.BlockSpec((tm, D), lambda i: (i, 0)), pl.BlockSpec((1, D), lambda i: (0, 0)),
                  pl.BlockSpec((N, D), lambda i: (0, 0))],
        out_specs=(pl.BlockSpec((tm, D), lambda i: (i, 0)), pl.BlockSpec((tm, N), lambda i: (i, 0))),
        compiler_params=_params(("parallel",)),
    )(h, g, win)


def _tri_consts():
    r = lax.broadcasted_iota(jnp.int32, (QB, QB), 0)
    c = lax.broadcasted_iota(jnp.int32, (QB, QB), 1)
    ones = jnp.ones((QB, QB), BF16)
    with_sums = lambda tri: jnp.concatenate([tri.astype(BF16), ones], axis=1)
    return with_sums(r > c), with_sums(r <= c), with_sums(r < c)


def _half_masks():
    lane = lax.broadcasted_iota(jnp.int32, (QB, PAIR), 1)
    row = lax.broadcasted_iota(jnp.int32, (QB, PAIR), 0)
    return lane < HEAD_DIM, lane, row


def sb_attn_fwd(p, after, name, riders=()):
    T = p.shape[0]
    nq = T // QB

    def body(q_ref, k_ref, v_ref, m_ref, o_ref, tot_ref, q_sc, acc_ref, z_sc):
        i = pl.program_id(0)
        lo, lane, row = _half_masks()
        causal = lane < row
        heads, pairs = range(SB_HEADS), range(SB_HEADS // 2)
        for hp in pairs:
            q_sc[hp] = (q_ref[:, hp * PAIR:(hp + 1) * PAIR].astype(F32) * SCALE).astype(BF16)
        m2 = m_ref[...]

        def by_head(ref, j, hp):
            t = ref[pl.ds(pl.multiple_of(j * QB, QB), QB), hp * PAIR:(hp + 1) * PAIR]
            return jnp.concatenate([jnp.where(lo, t, 0), jnp.where(lo, 0, t)], axis=0)

        def scores(j):
            return [_nt(q_sc[hp], by_head(k_ref, j, hp)) for hp in pairs]

        def block(j, diag):
            z2 = [z_sc[hp] for hp in pairs]
            ahead = scores(jnp.maximum(j - 1, 0))
            for hp in pairs:
                z_sc[hp] = ahead[hp]
            vs = [by_head(v_ref, j, hp) for hp in pairs]
            spls = [_softplus_logsig(z2[h // 2][:, (h % 2) * QB:(h % 2 + 1) * QB]) for h in heads]
            sp = [jnp.where(causal, spls[h][0], 0.0) if diag else spls[h][0] for h in heads]
            rr = [_tri(sp[h], m2) for h in heads]
            if diag:
                w = [jnp.where(causal, jnp.exp(spls[h][1] - rr[h][:, :QB]), 0.0).astype(BF16) for h in heads]
            else:
                c = [tot_ref[:, h * QB:(h + 1) * QB] for h in heads]
                w = [jnp.exp(spls[h][1] - (c[h] + rr[h][:, :QB])).astype(BF16) for h in heads]
            pv = [_nn(jnp.concatenate([w[2 * hp], w[2 * hp + 1]], axis=1), vs[hp]) for hp in pairs]
            for hp in pairs:
                acc_ref[hp] = pv[hp] if diag else acc_ref[hp] + pv[hp]
            for h in heads:
                tot_ref[:, h * QB:(h + 1) * QB] = rr[h][:, QB:] if diag else c[h] + rr[h][:, QB:]

        first = scores(i)
        for hp in pairs:
            z_sc[hp] = first[hp]
        block(i, True)

        def step(t, carry):
            block(i - 1 - t, False)
            return carry
        lax.fori_loop(0, i, step, 0)
        for hp in pairs:
            o_ref[:, hp * PAIR:(hp + 1) * PAIR] = acc_ref[hp]

    npair = SB_HEADS // 2
    return _call(
        body, name=name, grid=(nq,), out_shape=(S((T, SB_W), F32), S((T, SB_HEADS * QB), F32)),
        in_specs=[pl.BlockSpec((QB, SB_W), lambda i: (i, 0)), pl.BlockSpec((T, SB_W), lambda i: (0, 1)),
                  pl.BlockSpec((T, SB_W), lambda i: (0, 2)), pl.BlockSpec((QB, 2 * QB), lambda i: (0, 0))],
        out_specs=(pl.BlockSpec((QB, SB_W), lambda i: (i, 0)), pl.BlockSpec((QB, SB_HEADS * QB), lambda i: (i, 0))),
        scratch=[pltpu.VMEM((npair, QB, PAIR), BF16), pltpu.VMEM((npair, QB, PAIR), F32), pltpu.VMEM((npair, QB, 2 * QB), F32)],
        sem=("arbitrary",), args=(p, p, p, after), riders=riders,
        marks=((11 * nq) // 16, (13 * nq) // 16))


def sb_attn_bwd(p, do, tot, upto, before, name, riders=()):
    T = p.shape[0]
    nq = T // QB

    def body(q_ref, k_ref, v_ref, do_ref, tot_ref, mp_ref, mg_ref, dq_ref, dk_ref, dv_ref,
             q_sc, d_sc, qd_sc, pg_sc, dq_acc, dk_acc, dv_acc, zd_sc):
        i = pl.program_id(0)
        lo, lane, row = _half_masks()
        causal = lane < row
        heads, pairs = range(SB_HEADS), range(SB_HEADS // 2)

        def by_head(t):
            return jnp.concatenate([jnp.where(lo, t, 0), jnp.where(lo, 0, t)], axis=0)

        for hp in pairs:
            q2 = (q_ref[:, hp * PAIR:(hp + 1) * PAIR].astype(F32) * SCALE).astype(BF16)
            d2 = do_ref[:, hp * PAIR:(hp + 1) * PAIR].astype(BF16)
            q_sc[hp] = q2
            d_sc[hp] = d2
            qd_sc[hp] = by_head(q2)
            qd_sc[SB_HEADS // 2 + hp] = by_head(d2)
        mp, mg = mp_ref[...], mg_ref[...]

        @pl.when(i == 0)
        def _():
            dk_acc[...] = jnp.zeros_like(dk_acc)
            dv_acc[...] = jnp.zeros_like(dv_acc)
        pg_sc[...] = jnp.zeros_like(pg_sc)
        dq_acc[...] = jnp.zeros_like(dq_acc)

        def rows(ref, j, hp):
            return ref[pl.ds(pl.multiple_of(j * QB, QB), QB), hp * PAIR:(hp + 1) * PAIR]

        def products(j):
            return ([_nt(q_sc[hp], by_head(rows(k_ref, j, hp))) for hp in pairs]
                    + [_nt(d_sc[hp], by_head(rows(v_ref, j, hp))) for hp in pairs])

        def block(j, diag):
            r0 = pl.multiple_of(j * QB, QB)
            half = lambda t, h: t[:, (h % 2) * QB:(h % 2 + 1) * QB]
            z = [half(zd_sc[h // 2], h) for h in heads]
            dw = [half(zd_sc[SB_HEADS // 2 + h // 2], h) for h in heads]
            if not diag:
                ahead = products(j + 1)
                for hp in range(SB_HEADS):
                    zd_sc[hp] = ahead[hp]
            ks = [by_head(rows(k_ref, j, hp)) for hp in pairs]
            spls = [_softplus_logsig(z[h]) for h in heads]
            sp = [jnp.where(causal, spls[h][0], 0.0) if diag else spls[h][0] for h in heads]
            rr = [_tri(sp[h], mp) for h in heads]
            pc = [pg_sc[2 * h] for h in heads]
            w = [jnp.exp(spls[h][1] - (tot_ref[:, h * QB:(h + 1) * QB] - (pc[h] + rr[h][:, :QB]))) for h in heads]
            if diag:
                w = [jnp.where(causal, w[h], 0.0) for h in heads]
            gg = [dw[h] * w[h] for h in heads]
            rg = [_tri(gg[h], mg) for h in heads]
            gc = [pg_sc[2 * h + 1] for h in heads]
            dz = [gg[h] - (gg[h] + gc[h] + rg[h][:, :QB]) * jnp.exp(spls[h][1]) for h in heads]
            if diag:
                dz = [jnp.where(causal, dz[h], 0.0) for h in heads]
            dzb = [dz[h].astype(BF16) for h in heads]
            wb = [w[h].astype(BF16) for h in heads]
            both = lambda t, hp, axis: jnp.concatenate([t[2 * hp], t[2 * hp + 1]], axis=axis)
            dq = [_nn(both(dzb, hp, 1), ks[hp]) for hp in pairs]
            dk = [_tn(both(dzb, hp, 0), qd_sc[hp]) for hp in pairs]
            dv = [_tn(both(wb, hp, 0), qd_sc[SB_HEADS // 2 + hp]) for hp in pairs]
            for h in heads:
                if not diag:
                    pg_sc[2 * h] = pc[h] + rr[h][:, QB:]
                    pg_sc[2 * h + 1] = gc[h] + rg[h][:, QB:]
            for hp in pairs:
                dq_acc[hp] += dq[hp]
                dk_acc[pl.ds(r0, QB), hp * PAIR:(hp + 1) * PAIR] += dk[hp]
                dv_acc[pl.ds(r0, QB), hp * PAIR:(hp + 1) * PAIR] += dv[hp]

        first = products(0)
        for hp in range(SB_HEADS):
            zd_sc[hp] = first[hp]

        def step(t, carry):
            block(t, False)
            return carry
        lax.fori_loop(0, i, step, 0)
        block(i, True)
        for hp in pairs:
            dq_ref[:, hp * PAIR:(hp + 1) * PAIR] = (dq_acc[hp] * SCALE).astype(BF16)

        @pl.when(i == nq - 1)
        def _():
            dk_ref[...] = dk_acc[...].astype(BF16)
            dv_ref[...] = dv_acc[...].astype(BF16)

    qtile = pl.BlockSpec((QB, SB_W), lambda i: (i, 0))
    whole = pl.BlockSpec((T, SB_W), lambda i: (0, 0))
    const = pl.BlockSpec((QB, 2 * QB), lambda i: (0, 0))
    return _call(
        body, name=name, grid=(nq,), out_shape=(S((T, SB_W), BF16),) * 3,
        in_specs=[qtile, pl.BlockSpec((T, SB_W), lambda i: (0, 1)), pl.BlockSpec((T, SB_W), lambda i: (0, 2)), qtile,
                  pl.BlockSpec((QB, SB_HEADS * QB), lambda i: (i, 0)), const, const],
        out_specs=(qtile, whole, whole),
        scratch=[pltpu.VMEM((SB_HEADS // 2, QB, PAIR), BF16), pltpu.VMEM((SB_HEADS // 2, QB, PAIR), BF16),
                 pltpu.VMEM((SB_HEADS, 2 * QB, PAIR), BF16),
                 pltpu.VMEM((2 * SB_HEADS, QB, QB), F32), pltpu.VMEM((SB_HEADS // 2, QB, PAIR), F32),
                 pltpu.VMEM((T, SB_W), F32), pltpu.VMEM((T, SB_W), F32), pltpu.VMEM((SB_HEADS, QB, 2 * QB), F32)],
        sem=("arbitrary",), args=(p, p, p, do, tot, upto, before), riders=riders)


def _t5_buckets():
    a = lax.broadcasted_iota(jnp.int32, (QB, QB), 0)
    c = lax.broadcasted_iota(jnp.int32, (QB, QB), 1)

    def bucket(dist):
        dist = jnp.maximum(dist, 0)
        max_exact = N_BUCKETS // 2
        d = jnp.maximum(dist, 1).astype(F32)
        large = max_exact + (jnp.log(d / max_exact) / math.log(MAX_DISTANCE / max_exact)
                             * (N_BUCKETS - max_exact)).astype(jnp.int32)
        large = jnp.minimum(large, N_BUCKETS - 1)
        return jnp.where(dist < max_exact, dist, large)

    return bucket(QB + a - c), bucket(a - c)


def _swa_common(i, kp_ref, kc_ref, vp_ref, vc_ref, bp_ref, bc_ref, rb_ref, bias_ref):
    lo, lane, row = _half_masks()

    @pl.when(i == 0)
    def _():
        for blk, b_ref in enumerate((bp_ref, bc_ref)):
            bk = b_ref[...]
            for h in range(8):
                acc = jnp.zeros((QB, QB), F32)
                for b in range(N_BUCKETS):
                    acc = jnp.where(bk == b, rb_ref[b, h], acc)
                bias_ref[h, blk] = acc

    band = [(lane > row) & (i > 0), lane <= row]

    def halves(ref):
        t = ref[...].astype(F32)
        sw = pltpu.roll(t, HEAD_DIM, 1)
        return [[jnp.where(lo, t, 0.0).astype(BF16), jnp.where(lo, 0.0, sw).astype(BF16)],
                [jnp.where(lo, sw, 0.0).astype(BF16), jnp.where(lo, 0.0, t).astype(BF16)]]

    ks = [halves(kp_ref), halves(kc_ref)]
    vs = [halves(vp_ref), halves(vc_ref)]
    return lo, band, ks, vs


def swa_fwd(p, sinks, rel_bias, bprev, bcur, name, riders=()):
    T = p.shape[0]
    nq = T // QB
    kcol, vcol = (3 * SB_W + SWA_W) // KV_W, (3 * SB_W + SWA_W) // KV_W + 1

    def body(q_ref, kp_ref, kc_ref, vp_ref, vc_ref, bp_ref, bc_ref, sink_ref, rb_ref, o_ref, lse_ref, bias_ref):
        i = pl.program_id(0)
        lo, band, ks, vs = _swa_common(i, kp_ref, kc_ref, vp_ref, vc_ref, bp_ref, bc_ref, rb_ref, bias_ref)
        for g in range(4):
            kh = g // 2
            q2 = q_ref[:, g * PAIR:(g + 1) * PAIR]
            outs = []
            for pos in range(2):
                h = 2 * g + pos
                sc = [jnp.where(band[b], _nt(q2, ks[b][kh][pos]) * SCALE + bias_ref[h, b], NEG_INF) for b in range(2)]
                sink = sink_ref[0, h]
                m = jnp.maximum(jnp.maximum(jnp.max(sc[0], axis=1, keepdims=True),
                                            jnp.max(sc[1], axis=1, keepdims=True)), sink)
                e = [jnp.exp(sc[b] - m) for b in range(2)]
                den = jnp.sum(e[0], axis=1, keepdims=True) + jnp.sum(e[1], axis=1, keepdims=True) + jnp.exp(sink - m)
                outs.append(_nn((e[0] / den).astype(BF16), vs[0][kh][pos]) + _nn((e[1] / den).astype(BF16), vs[1][kh][pos]))
                lse_ref[:, h * QB:(h + 1) * QB] = jnp.broadcast_to(m + jnp.log(den), (QB, QB))
            o_ref[:, g * PAIR:(g + 1) * PAIR] = outs[0] + outs[1]

    kv = lambda col, prev: pl.BlockSpec((QB, KV_W), (lambda i: (jnp.maximum(i - 1, 0), col)) if prev else (lambda i: (i, col)))
    full = pl.BlockSpec((QB, QB), lambda i: (0, 0))
    smem = pl.BlockSpec(memory_space=pltpu.SMEM)
    return _call(
        body, name=name, grid=(nq,), out_shape=(S((T, SWA_W), F32), S((T, 8 * QB), F32)),
        in_specs=[pl.BlockSpec((QB, SWA_W), lambda i: (i, 3)), kv(kcol, True), kv(kcol, False), kv(vcol, True), kv(vcol, False),
                  full, full, smem, smem],
        out_specs=(pl.BlockSpec((QB, SWA_W), lambda i: (i, 0)), pl.BlockSpec((QB, 8 * QB), lambda i: (i, 0))),
        scratch=[pltpu.VMEM((8, 2, QB, QB), F32)],
        sem=("arbitrary",), args=(p, p, p, p, p, bprev, bcur, sinks, rel_bias), riders=riders)


def swa_bwd(p, do, lse, sinks, rel_bias, bprev, bcur, name, riders=()):
    T = p.shape[0]
    nq = T // QB
    kcol, vcol = (3 * SB_W + SWA_W) // KV_W, (3 * SB_W + SWA_W) // KV_W + 1

    def body(q_ref, kp_ref, kc_ref, vp_ref, vc_ref, do_ref, lse_ref, bp_ref, bc_ref, sink_ref, rb_ref,
             dq_ref, dk_ref, dv_ref, dsink_ref, dsc_ref, bias_ref, dk_acc, dv_acc):
        i = pl.program_id(0)
        lo, band, ks, vs = _swa_common(i, kp_ref, kc_ref, vp_ref, vc_ref, bp_ref, bc_ref, rb_ref, bias_ref)

        @pl.when(i == 0)
        def _():
            dk_acc[...] = jnp.zeros_like(dk_acc)
            dv_acc[...] = jnp.zeros_like(dv_acc)
            dsc_ref[...] = jnp.zeros_like(dsc_ref)
            dsink_ref[...] = jnp.zeros_like(dsink_ref)

        lane1 = lax.broadcasted_iota(jnp.int32, (1, QB), 1)
        dsink = jnp.zeros((1, QB), F32)
        dk_parts = [[[None, None], [None, None]], [[None, None], [None, None]]]
        dv_parts = [[[None, None], [None, None]], [[None, None], [None, None]]]

        def add(parts, b, pos, kh, val):
            parts[b][pos][kh] = val if parts[b][pos][kh] is None else parts[b][pos][kh] + val

        for g in range(4):
            kh = g // 2
            q2 = q_ref[:, g * PAIR:(g + 1) * PAIR]
            q2f = q2.astype(F32)
            d2f = do_ref[:, g * PAIR:(g + 1) * PAIR]
            d2 = d2f.astype(BF16)
            dq = None
            for pos in range(2):
                h = 2 * g + pos
                keep = lo if pos == 0 else ~lo
                qh = jnp.where(keep, q2f, 0.0).astype(BF16)
                dh = jnp.where(keep, d2f, 0.0).astype(BF16)
                lse_h = lse_ref[:, h * QB:(h + 1) * QB]
                sink = sink_ref[0, h]
                pr = [jnp.exp(jnp.where(band[b], _nt(q2, ks[b][kh][pos]) * SCALE + bias_ref[h, b], NEG_INF) - lse_h)
                      for b in range(2)]
                dp = [_nt(d2, vs[b][kh][pos]) for b in range(2)]
                delta = jnp.sum(pr[0] * dp[0], axis=1, keepdims=True) + jnp.sum(pr[1] * dp[1], axis=1, keepdims=True)
                p_sink = jnp.exp(sink - lse_h[:, :1])
                dsink = dsink + jnp.where(lane1 == h, -jnp.sum(p_sink * delta), 0.0)
                for b in range(2):
                    dsc = pr[b] * (dp[b] - delta)
                    dsc_ref[h, b] += dsc
                    dzb = (dsc * SCALE).astype(BF16)
                    t = _nn(dzb, ks[b][kh][pos])
                    dq = t if dq is None else dq + t
                    add(dk_parts, b, pos, kh, _tn(dzb, qh))
                    add(dv_parts, b, pos, kh, _tn(pr[b].astype(BF16), dh))
            dq_ref[:, g * PAIR:(g + 1) * PAIR] = dq.astype(BF16)
        dsink_ref[...] += dsink

        def fold(parts, b):
            low = parts[b][0][0] + pltpu.roll(parts[b][1][0], HEAD_DIM, 1)
            high = parts[b][1][1] + pltpu.roll(parts[b][0][1], HEAD_DIM, 1)
            return jnp.where(lo, low, high)

        rp = pl.multiple_of(jnp.maximum(i - 1, 0) * QB, QB)
        rc = pl.multiple_of(i * QB, QB)
        dk_acc[pl.ds(rp, QB), :] += fold(dk_parts, 0)
        dv_acc[pl.ds(rp, QB), :] += fold(dv_parts, 0)
        dk_acc[pl.ds(rc, QB), :] += fold(dk_parts, 1)
        dv_acc[pl.ds(rc, QB), :] += fold(dv_parts, 1)

        @pl.when(i == nq - 1)
        def _():
            dk_ref[...] = dk_acc[...].astype(BF16)
            dv_ref[...] = dv_acc[...].astype(BF16)

    kv = lambda col, prev: pl.BlockSpec((QB, KV_W), (lambda i: (jnp.maximum(i - 1, 0), col)) if prev else (lambda i: (i, col)))
    full = pl.BlockSpec((QB, QB), lambda i: (0, 0))
    smem = pl.BlockSpec(memory_space=pltpu.SMEM)
    whole = lambda shape: pl.BlockSpec(shape, lambda i: (0,) * len(shape))
    return _call(
        body, name=name, grid=(nq,),
        out_shape=(S((T, SWA_W), BF16), S((T, KV_W), BF16), S((T, KV_W), BF16), S((1, QB), F32), S((8, 2, QB, QB), F32)),
        in_specs=[pl.BlockSpec((QB, SWA_W), lambda i: (i, 3)), kv(kcol, True), kv(kcol, False), kv(vcol, True), kv(vcol, False),
                  pl.BlockSpec((QB, SWA_W), lambda i: (i, 0)), pl.BlockSpec((QB, 8 * QB), lambda i: (i, 0)),
                  full, full, smem, smem],
        out_specs=(pl.BlockSpec((QB, SWA_W), lambda i: (i, 0)), whole((T, KV_W)), whole((T, KV_W)), whole((1, QB)),
                   whole((8, 2, QB, QB))),
        scratch=[pltpu.VMEM((8, 2, QB, QB), F32), pltpu.VMEM((T, KV_W), F32), pltpu.VMEM((T, KV_W), F32)],
        sem=("arbitrary",), args=(p, p, p, p, p, do, lse, bprev, bcur, sinks, rel_bias), riders=riders)


def mix_out_fwd(o_sb, o_sw, g_sb, g_sw, wout, h, g_next, name, riders=()):
    T, D = h.shape
    M = SB_W + SWA_W
    tm = _tile(T, 256)

    def body(a_ref, b_ref, ga_ref, gb_ref, w_ref, h_ref, gn_ref, mx_ref, o_ref, n_ref):
        mx_ref[:, :SB_W] = _rms(a_ref[...], ga_ref[...]).astype(BF16)
        mx_ref[:, SB_W:] = _rms(b_ref[...], gb_ref[...]).astype(BF16)
        out = h_ref[...] + _nn(mx_ref[...], w_ref[...])
        o_ref[...] = out
        n_ref[...] = _rms(out, gn_ref[...]).astype(BF16)

    row = lambda n: pl.BlockSpec((tm, n), lambda i: (i, 0))
    vec = lambda n: pl.BlockSpec((1, n), lambda i: (0, 0))
    return _call(
        body, name=name, grid=(T // tm,), out_shape=(S((T, M), BF16), S((T, D), F32), S((T, D), BF16)),
        in_specs=[row(SB_W), row(SWA_W), vec(SB_W), vec(SWA_W), pl.BlockSpec((M, D), lambda i: (0, 0)), row(D), vec(D)],
        out_specs=(row(M), row(D), row(D)),
        sem=("parallel",), args=(o_sb, o_sw, g_sb, g_sw, wout, h, g_next), riders=riders)


def loss_head(h, g, target, name):
    T, D = h.shape
    tm = _tile(T, 256)

    def body(h_ref, g_ref, t_ref, loss_ref, dh_ref, dhb_ref, dg_ref):
        @pl.when(pl.program_id(0) == 0)
        def _():
            loss_ref[...] = jnp.zeros_like(loss_ref)
            dg_ref[...] = jnp.zeros_like(dg_ref)
        x = h_ref[...]
        err = _rms(x, g_ref[...]) - t_ref[...]
        loss_ref[...] += jnp.full((1, QB), 0.5 * jnp.sum(jnp.mean(err * err, axis=-1)), F32)
        dx, dg = _rms_bwd(err / D, x, g_ref[...])
        dh_ref[...] = dx
        dhb_ref[...] = dx.astype(BF16)
        dg_ref[...] += dg

    row = pl.BlockSpec((tm, D), lambda i: (i, 0))
    vec = pl.BlockSpec((1, D), lambda i: (0, 0))
    return pl.pallas_call(
        body, name=name, grid=(T // tm,), out_shape=(S((1, QB), F32), S((T, D), F32), S((T, D), BF16), S((1, D), F32)),
        in_specs=[row, vec, row], out_specs=(pl.BlockSpec((1, QB), lambda i: (0, 0)), row, row, vec),
        compiler_params=_params(("arbitrary",)),
    )(h, g, target)


def ffn_down_bwd(dhb, wd, gate, up, name, riders=()):
    T, D = dhb.shape
    F = wd.shape[0]
    tr, tn = _tile(T, 512), _tile(F, 256)

    def body(d_ref, w_ref, g_ref, u_ref, o_ref):
        w = w_ref[...]
        for r in range(T // tr):
            rows = slice(r * tr, (r + 1) * tr)
            da = 0.5 * _nt(d_ref[rows, :], w)
            o_ref[0, rows, :] = (da * g_ref[rows, :].astype(F32)).astype(BF16)
            o_ref[1, rows, :] = (da * u_ref[rows, :].astype(F32)).astype(BF16)

    tile = pl.BlockSpec((T, tn), lambda j: (0, j))
    return _call(
        body, name=name, grid=(F // tn,), out_shape=S((2, T, F), BF16),
        in_specs=[pl.BlockSpec((T, D), lambda j: (0, 0)), pl.BlockSpec((tn, D), lambda j: (j, 0)), tile, tile],
        out_specs=pl.BlockSpec((2, T, tn), lambda j: (0, 0, j)),
        sem=("parallel",), args=(dhb, wd, gate, up), riders=riders)


def tn_matmul(xs, y, alpha, name, riders=()):
    B, T, N = xs.shape
    D = y.shape[1]
    tn = _tile(N, 256)

    def body(x_ref, y_ref, o_ref, ob_ref):
        o = alpha * _tn(x_ref[...], y_ref[...])
        o_ref[...] = o
        ob_ref[...] = o.astype(BF16)

    tile = pl.BlockSpec((None, tn, D), lambda s, j: (s, j, 0))
    return _call(
        body, name=name, grid=(B, N // tn), out_shape=(S((B, N, D), F32), S((B, N, D), BF16)),
        in_specs=[pl.BlockSpec((None, T, tn), lambda s, j: (s, 0, j)), pl.BlockSpec((T, D), lambda s, j: (0, 0))],
        out_specs=(tile, tile), sem=("parallel", "parallel"), args=(xs, y), riders=riders)


def nn_rms_bwd(xs, ws, h_in, g, dh, name, riders=()):
    B, T, K = xs.shape
    D = ws.shape[2]
    tm = _tile(T, 256)

    def body(x_ref, w_ref, h_ref, g_ref, d_ref, o_ref, ob_ref, dg_ref):
        @pl.when(pl.program_id(0) == 0)
        def _():
            dg_ref[...] = jnp.zeros_like(dg_ref)
        dn = _nn(x_ref[0], w_ref[0])
        for s in range(1, B):
            dn = dn + _nn(x_ref[s], w_ref[s])
        dx, dg = _rms_bwd(dn, h_ref[...], g_ref[...])
        out = d_ref[...] + dx
        o_ref[...] = out
        ob_ref[...] = out.astype(BF16)
        dg_ref[...] += dg

    row = pl.BlockSpec((tm, D), lambda i: (i, 0))
    vec = pl.BlockSpec((1, D), lambda i: (0, 0))
    return _call(
        body, name=name, grid=(T // tm,), out_shape=(S((T, D), F32), S((T, D), BF16), S((1, D), F32)),
        in_specs=[pl.BlockSpec((B, tm, K), lambda i: (0, i, 0)), pl.BlockSpec((B, K, D), lambda i: (0, 0, 0)), row, vec, row],
        out_specs=(row, row, vec),
        sem=("arbitrary",), args=(xs, ws, h_in, g, dh), riders=riders)


def mix_out_bwd(dhb, wout, o_sb, o_sw, g_sb, g_sw, name):
    T, D = dhb.shape
    tm = _tile(T, 256)

    def body(d_ref, w_ref, a_ref, b_ref, ga_ref, gb_ref, da_ref, db_ref, dga_ref, dgb_ref):
        @pl.when(pl.program_id(0) == 0)
        def _():
            dga_ref[...] = jnp.zeros_like(dga_ref)
            dgb_ref[...] = jnp.zeros_like(dgb_ref)
        dm = _nt(d_ref[...], w_ref[...])
        dxa, dga = _rms_bwd(dm[:, :SB_W], a_ref[...], ga_ref[...])
        dxb, dgb = _rms_bwd(dm[:, SB_W:], b_ref[...], gb_ref[...])
        da_ref[...] = dxa
        db_ref[...] = dxb
        dga_ref[...] += dga
        dgb_ref[...] += dgb

    row = lambda n: pl.BlockSpec((tm, n), lambda i: (i, 0))
    vec = lambda n: pl.BlockSpec((1, n), lambda i: (0, 0))
    return pl.pallas_call(
        body, name=name, grid=(T // tm,),
        out_shape=(S((T, SB_W), F32), S((T, SWA_W), F32), S((1, SB_W), F32), S((1, SWA_W), F32)),
        in_specs=[row(D), pl.BlockSpec((SB_W + SWA_W, D), lambda i: (0, 0)), row(SB_W), row(SWA_W), vec(SB_W), vec(SWA_W)],
        out_specs=(row(SB_W), row(SWA_W), vec(SB_W), vec(SWA_W)),
        compiler_params=_params(("arbitrary",)),
    )(dhb, wout, o_sb, o_sw, g_sb, g_sw)


def rel_bias_grad(dscs, bprev, bcur, name):
    n = len(dscs)

    def body(*refs):
        bp_ref, bc_ref, o_ref = refs[n], refs[n + 1], refs[n + 2]
        bks = [bp_ref[...], bc_ref[...]]
        row = lax.broadcasted_iota(jnp.int32, (N_BUCKETS, QB), 0)
        lane = lax.broadcasted_iota(jnp.int32, (N_BUCKETS, QB), 1)
        out = jnp.zeros((N_BUCKETS, QB), F32)
        for h in range(8):
            tot = [sum(refs[l][h, b] for l in range(n)) for b in range(2)]
            for b in range(N_BUCKETS):
                val = jnp.sum(jnp.where(bks[0] == b, tot[0], 0.0)) + jnp.sum(jnp.where(bks[1] == b, tot[1], 0.0))
                out = jnp.where((row == b) & (lane == h), val, out)
        o_ref[...] = out

    return pl.pallas_call(body, name=name, out_shape=S((N_BUCKETS, QB), F32), compiler_params=_params())(*dscs, bprev, bcur)


def _adamw(w, g, m, v):
    m = ADAM_B1 * m + (1.0 - ADAM_B1) * g
    v = ADAM_B2 * v + (1.0 - ADAM_B2) * (g * g)
    m_hat = m / (1.0 - ADAM_B1 ** ADAM_STEP)
    v_hat = v / (1.0 - ADAM_B2 ** ADAM_STEP)
    delta = -ADAM_LR * (m_hat / (jnp.sqrt(v_hat) + ADAM_EPS) + ADAM_WD * w)
    return delta, m, v


def adamw_scattered(w, m, v, owns, others, name, riders=()):
    L, R, C = w.shape
    tr = _rows_tile(R, 176)

    def body(w_ref, m_ref, v_ref, *rest):
        own_refs, other_refs = rest[:L], rest[L:2 * L]
        g_ref, d_ref, mo_ref, vo_ref = rest[2 * L:]
        layer = pl.program_id(0)

        def grad(k):
            o = other_refs[k]
            return own_refs[k][...] + o[0].astype(F32) + o[1].astype(F32) + o[2].astype(F32)

        g = grad(0)
        for k in range(1, L):
            g = jnp.where(layer == k, grad(k), g)
        d, mn, vn = _adamw(w_ref[...], g, m_ref[...], v_ref[...])
        g_ref[...] = g
        d_ref[...] = d
        mo_ref[...] = mn
        vo_ref[...] = vn

    tile = pl.BlockSpec((None, tr, C), lambda l, i: (l, i, 0))
    return _call(
        body, name=name, grid=(L, R // tr), out_shape=(S((L, R, C), F32),) * 4,
        in_specs=[tile] * 3 + [pl.BlockSpec((tr, C), lambda l, i: (i, 0))] * L + [pl.BlockSpec((3, tr, C), lambda l, i: (0, i, 0))] * L,
        out_specs=(tile,) * 4, sem=("parallel", "parallel"), args=(w, m, v, *owns, *others), riders=riders)


def adamw_small(w, gs, m, v, name):
    R, C = w.shape

    def body(w_ref, g_ref, m_ref, v_ref, go_ref, d_ref, mo_ref, vo_ref):
        g = g_ref[0]
        for k in range(1, N_DEV):
            g = g + g_ref[k]
        d, mn, vn = _adamw(w_ref[...], g, m_ref[...], v_ref[...])
        go_ref[...] = g
        d_ref[...] = d
        mo_ref[...] = mn
        vo_ref[...] = vn

    return pl.pallas_call(body, name=name, out_shape=(S((R, C), F32),) * 4, compiler_params=_params())(w, gs, m, v)


def kernel(x, norm_ffn1, w_ffn1_gu, w_ffn1_down, norm_mix, w_in, sinks, norm_out_sb, norm_out_swa, w_out, norm_ffn2, w_ffn2_gu, w_ffn2_down, rel_bias, norm_final, loss_target, m_norm_ffn1, m_w_ffn1_gu, m_w_ffn1_down, m_norm_mix, m_w_in, m_sinks, m_norm_out_sb, m_norm_out_swa, m_w_out, m_norm_ffn2, m_w_ffn2_gu, m_w_ffn2_down, m_rel_bias, m_norm_final, v_norm_ffn1, v_w_ffn1_gu, v_w_ffn1_down, v_norm_mix, v_w_in, v_sinks, v_norm_out_sb, v_norm_out_swa, v_w_out, v_norm_ffn2, v_w_ffn2_gu, v_w_ffn2_down, v_rel_bias, v_norm_final):
    L = norm_ffn1.shape[0]
    T, D = x.shape[1], x.shape[2]
    F = w_ffn1_down.shape[1] * N_DEV
    h = x.reshape(T, D)
    target = loss_target.reshape(T, D)
    after, upto, before = _tri_consts()
    bprev, bcur = _t5_buckets()

    local = {}
    for l in range(L):
        local[f"gu1_{l}"] = w_ffn1_gu[l].T.astype(BF16)
        local[f"d1_{l}"] = w_ffn1_down[l].astype(BF16)
        local[f"in_{l}"] = w_in[l].T.astype(BF16)
        local[f"out_{l}"] = w_out[l].astype(BF16)
        local[f"gu2_{l}"] = w_ffn2_gu[l].T.astype(BF16)
        local[f"d2_{l}"] = w_ffn2_down[l].astype(BF16)
    full, partial = {}, {}
    grads, chip_sum, recv_b = {}, {}, {}

    def run(fn, *args, ag=(), rs1=(), rs2=()):
        halves = lambda names: [n if isinstance(n, tuple) else (n, None) for n in names]
        ag, rs2 = [(n, k) for n, k in halves(ag) if n in local], halves(rs2)
        rows = lambda k, total: None if k is None else (k * (total // 2), total // 2)

        def second(n, k):
            sb = chip_sum[n][1]
            return scatter_second(sb, rows(k, sb.shape[1]), recv_b.get(n))

        riders = ([gather(local[n], rows(k, local[n].shape[0]), partial.get(n)) for n, k in ag]
                  + [scatter_first(grads[n][1]) for n in rs1] + [second(n, k) for n, k in rs2])
        if not riders:
            return fn(*args)
        outs, per = fn(*args, riders=riders)
        per = [p[0] for p in per]
        for n, k in ag:
            buf = per.pop(0)
            if k == 0:
                partial[n] = buf
            else:
                full[n] = buf.reshape(N_DEV * buf.shape[1], D)
        for n in rs1:
            chip_sum[n] = scatter_add(grads[n][0], per.pop(0), f"rs_add_{n}")
        for n, _ in rs2:
            recv_b[n] = per.pop(0)
        return outs

    def idle(name, riders=()):
        return None, idle_host(riders, name)

    gu = lambda n: full[n].reshape(2, F, D)
    slots = lambda pair: tuple(t.reshape(N_DEV, -1, D) for t in pair)
    vec = lambda a: a.reshape(1, -1)

    run(idle, "ag_head", ag=("gu1_0",))
    saved = []
    n_next = rms_cast(h, vec(norm_ffn1[0]), "rms_first")
    for l in range(L):
        nx = l + 1
        s = {"h0": h, "n1": n_next}
        s["gate1"], s["up1"], s["a1"] = run(ffn_up_fwd, s["n1"], gu(f"gu1_{l}"), f"ffn1_up{l}",
                                            ag=(f"d1_{l}",) + ((("in_0", 0),) if l == 0 else ()))
        h = run(ffn_down_fwd, s["a1"], full[f"d1_{l}"], h, None, f"ffn1_down{l}", ag=((f"in_{l}", 1),))
        s["h1"] = h
        s["n2"], s["p"] = mix_in_fwd(h, vec(norm_mix[l]), full[f"in_{l}"], f"mix_in{l}")
        s["o_sb"], s["tot"] = run(sb_attn_fwd, s["p"], after, f"sb_fwd{l}", ag=(f"out_{l}", f"gu2_{l}", f"d2_{l}"))
        s["o_sw"], s["lse"] = run(swa_fwd, s["p"], vec(sinks[l]), rel_bias, bprev, bcur, f"swa_fwd{l}", ag=((f"gu1_{nx}", 0),))
        s["mixed"], h, s["n3"] = run(mix_out_fwd, s["o_sb"], s["o_sw"], vec(norm_out_sb[l]), vec(norm_out_swa[l]),
                                     full[f"out_{l}"], h, vec(norm_ffn2[l]), f"mix_out{l}")
        s["h2"] = h
        s["gate2"], s["up2"], s["a2"] = run(ffn_up_fwd, s["n3"], gu(f"gu2_{l}"), f"ffn2_up{l}", ag=((f"gu1_{nx}", 1),))
        if nx < L:
            h, n_next = run(ffn_down_fwd, s["a2"], full[f"d2_{l}"], h, vec(norm_ffn1[nx]), f"ffn2_down{l}", ag=((f"in_{nx}", 0),))
        else:
            h = run(ffn_down_fwd, s["a2"], full[f"d2_{l}"], h, None, f"ffn2_down{l}")
        saved.append(s)

    loss_part, dh, dhb, dg_final = loss_head(h, vec(norm_final), target, "loss_head")
    loss = lax.psum(loss_part[0, 0], ("x", "y", "c"))

    small = {k: [None] * L for k in ("ffn1", "mix", "sinks", "osb", "osw", "ffn2", "dsc")}
    for l in reversed(range(L)):
        s = saved[l]

        def ffn_bwd(dh, dhb, tag, gate, up, a, n, h_in, g, r_down, r_dwgu, r_up):
            gu_n, d_n = f"gu{tag}_{l}", f"d{tag}_{l}"
            dgu = run(ffn_down_bwd, dhb, full[d_n], gate, up, f"ffn{tag}_down_bwd{l}", **r_down)
            grads[gu_n] = slots(run(tn_matmul, dgu, n, 1.0, f"ffn{tag}_dwgu{l}", **r_dwgu))
            grads[d_n] = slots(run(tn_matmul, a[None], dhb, 0.5, f"ffn{tag}_dwd{l}", rs1=(gu_n,)))
            return run(nn_rms_bwd, dgu, gu(gu_n), h_in, g, dh, f"ffn{tag}_up_bwd{l}", rs1=(d_n,), **r_up)

        later = l + 1 < L
        dh, dhb, small["ffn2"][l] = ffn_bwd(dh, dhb, 2, s["gate2"], s["up2"], s["a2"], s["n3"], s["h2"], vec(norm_ffn2[l]),
                                            dict(rs2=((f"gu1_{l + 1}", 1),) if later else ()),
                                            dict(rs2=(f"d1_{l + 1}",) if later else ()), {})
        do_sb, do_sw, small["osb"][l], small["osw"][l] = mix_out_bwd(
            dhb, full[f"out_{l}"], s["o_sb"], s["o_sw"], vec(norm_out_sb[l]), vec(norm_out_swa[l]), f"mix_out_bwd{l}")
        grads[f"out_{l}"] = slots(tn_matmul(s["mixed"][None], dhb, 1.0, f"dwout{l}"))
        dq_sb, dk_sb, dv_sb = run(sb_attn_bwd, s["p"], do_sb, s["tot"], upto, before, f"sb_bwd{l}",
                                  rs2=(f"gu2_{l}", f"d2_{l}"), rs1=(f"out_{l}",))
        dq_sw, dk_sw, dv_sw, small["sinks"][l], small["dsc"][l] = run(
            swa_bwd, s["p"], do_sw, s["lse"], vec(sinks[l]), rel_bias, bprev, bcur, f"swa_bwd{l}", rs2=(f"out_{l}",))
        dp = jnp.concatenate([dq_sb, dk_sb, dv_sb, dq_sw, dk_sw, dv_sw], axis=1)
        dh, dhb, small["mix"][l] = nn_rms_bwd(dp[None], full[f"in_{l}"][None], s["h1"], vec(norm_mix[l]), dh, f"mix_in_bwd{l}")
        grads[f"in_{l}"] = slots(tn_matmul(dp[None], s["n2"], 1.0, f"dwin{l}"))
        dh, dhb, small["ffn1"][l] = ffn_bwd(dh, dhb, 1, s["gate1"], s["up1"], s["a1"], s["n1"], s["h0"], vec(norm_ffn1[l]),
                                            dict(rs1=(f"in_{l}",)), dict(rs2=(f"in_{l}",)), dict(rs2=((f"gu1_{l}", 0),)))

    grad_x = dh.reshape(x.shape)

    upd = {}
    for nm, w, m, v, transposed, last in (
            ("gu2", w_ffn2_gu, m_w_ffn2_gu, v_w_ffn2_gu, True, (("gu1_0", 1),)), ("d2", w_ffn2_down, m_w_ffn2_down, v_w_ffn2_down, False, ("d1_0",)),
            ("in", w_in, m_w_in, v_w_in, True, ()), ("out", w_out, m_w_out, v_w_out, False, ()),
            ("gu1", w_ffn1_gu, m_w_ffn1_gu, v_w_ffn1_gu, True, ()), ("d1", w_ffn1_down, m_w_ffn1_down, v_w_ffn1_down, False, ())):
        turn = (lambda a: jnp.swapaxes(a, 1, 2)) if transposed else (lambda a: a)
        names = [f"{nm}_{l}" for l in range(L)]
        res = run(adamw_scattered, turn(w), turn(m), turn(v), [chip_sum[n][0] for n in names], [recv_b[n] for n in names],
                  f"adamw_{nm}", rs2=last)
        upd[nm] = tuple(turn(r) for r in res)

    d_rel = rel_bias_grad(small["dsc"], bprev, bcur, "rel_bias_grad")[:, :8]

    PW = max(D, SB_W + SWA_W)

    def pack(ffn1, mix, ffn2, final, osb, osw, snk, rel):
        wide = lambda a: jnp.pad(a.reshape(-1), (0, PW - a.size))
        rows = [wide(ffn1[l]) for l in range(L)] + [wide(mix[l]) for l in range(L)] + [wide(ffn2[l]) for l in range(L)]
        rows.append(wide(final))
        rows += [wide(jnp.concatenate([osb[l].reshape(-1), osw[l].reshape(-1)])) for l in range(L)]
        rows.append(wide(jnp.concatenate([snk[l].reshape(-1)[:8] for l in range(L)] + [rel.reshape(-1)])))
        arr = jnp.stack(rows)
        return jnp.pad(arr, ((0, (-arr.shape[0]) % 8), (0, 0)))

    def unpack(arr):
        ffn1, mix, ffn2 = arr[0:L, :D], arr[L:2 * L, :D], arr[2 * L:3 * L, :D]
        final = arr[3 * L, :D]
        ob = arr[3 * L + 1:4 * L + 1]
        tail = arr[4 * L + 1]
        return (ffn1, mix, tail[:8 * L].reshape(L, 8), ob[:, :SB_W], ob[:, SB_W:SB_W + SWA_W], ffn2,
                tail[8 * L:8 * L + N_BUCKETS * 8].reshape(N_BUCKETS, 8), final)

    g_small = pack(small["ffn1"], small["mix"], small["ffn2"], dg_final, small["osb"], small["osw"], small["sinks"], d_rel)
    w_small = pack(norm_ffn1, norm_mix, norm_ffn2, norm_final, norm_out_sb, norm_out_swa, sinks, rel_bias)
    m_small = pack(m_norm_ffn1, m_norm_mix, m_norm_ffn2, m_norm_final, m_norm_out_sb, m_norm_out_swa, m_sinks, m_rel_bias)
    v_small = pack(v_norm_ffn1, v_norm_mix, v_norm_ffn2, v_norm_final, v_norm_out_sb, v_norm_out_swa, v_sinks, v_rel_bias)
    gs_small = all_gather_rows(g_small, "ag_small")
    small_out = [unpack(a) for a in adamw_small(w_small, gs_small, m_small, v_small, "adamw_small")]

    def group(k):
        sm = small_out[k]
        return (sm[0], upd["gu1"][k], upd["d1"][k], sm[1], upd["in"][k], sm[2], sm[3], sm[4], upd["out"][k], sm[5],
                upd["gu2"][k], upd["d2"][k], sm[6], sm[7])

    return (loss, grad_x, *group(0), *group(1), *group(2), *group(3))
```

```python
import math

import jax
import jax.numpy as jnp
from jax import lax
from jax.experimental import pallas as pl
from jax.experimental.pallas import tpu as pltpu

F32 = jnp.float32
BF16 = jnp.bfloat16
S = jax.ShapeDtypeStruct

N_DEV = 8
HEAD_DIM = 64
SB_HEADS = 8
PAIR = 2 * HEAD_DIM
SB_W = 512
SWA_W = 512
KV_W = 128
IN_W = 3 * SB_W + SWA_W + 2 * KV_W
QB = 128
N_BUCKETS = 32
MAX_DISTANCE = 128
EPS = 1e-6
NEG_INF = -1e30
SCALE = HEAD_DIM ** -0.5

ADAM_LR = 0.001
ADAM_B1 = 0.9
ADAM_B2 = 0.999
ADAM_EPS = 1e-08
ADAM_WD = 0.01
ADAM_STEP = 10

VMEM_LIMIT = 56 * 1024 * 1024
MESH = pl.DeviceIdType.MESH


def _params(sem=None, vmem=VMEM_LIMIT):
    return pltpu.CompilerParams(dimension_semantics=sem, vmem_limit_bytes=vmem)


def _nn(a, b):
    return jnp.dot(a, b, preferred_element_type=F32)


def _nt(a, b):
    return lax.dot_general(a, b, (((1,), (1,)), ((), ())), preferred_element_type=F32)


def _tn(a, b):
    return lax.dot_general(a, b, (((0,), (0,)), ((), ())), preferred_element_type=F32)


def _tri(x, m):
    return _nn(x.astype(BF16), m)


def _rms(x, g):
    r = lax.rsqrt(jnp.mean(x * x, axis=-1, keepdims=True) + EPS)
    return x * r * g


def _rms_bwd(dy, x, g):
    r = lax.rsqrt(jnp.mean(x * x, axis=-1, keepdims=True) + EPS)
    xhat = x * r
    u = dy * g
    dx = r * (u - xhat * jnp.mean(u * xhat, axis=-1, keepdims=True))
    return dx, jnp.sum(dy * xhat, axis=0, keepdims=True)


def _softplus_logsig(z):
    sp = jnp.maximum(z, 0.0) + jnp.log(1.0 + jnp.exp(-jnp.abs(z)))
    return sp, z - sp


def _tile(n, want):
    t = min(n, want)
    while n % t:
        t //= 2
    return t


def _place():
    x, y, c = lax.axis_index("x"), lax.axis_index("y"), lax.axis_index("c")
    chips = [(1 - x, y), (x, 1 - y), (1 - x, 1 - y)]
    return x, y, c, chips


def all_gather_rows(v, name):
    R, C = v.shape

    def body(v_ref, out_ref, send_sems, recv_sems, local_sem):
        x, y, c, chips = _place()
        me, sibling = (x, y, c), (x, y, 1 - c)

        def slot(px, py, pc):
            return out_ref.at[4 * px + 2 * py + pc]

        def copy(k, block, to, src=None):
            return pltpu.make_async_remote_copy(
                src_ref=slot(*block) if src is None else src, dst_ref=slot(*block),
                send_sem=send_sems.at[k], recv_sem=recv_sems.at[k], device_id=to, device_id_type=MESH)

        mine = pltpu.make_async_copy(v_ref, slot(*me), local_sem)
        mine.start()
        first = [copy(0, me, sibling, src=v_ref)]
        first += [copy(1 + j, me, (*chip, c), src=v_ref) for j, chip in enumerate(chips)]
        for cp in first:
            cp.start()
        passed = [copy(4 + j, (*chip, c), sibling) for j, chip in enumerate(chips)]
        for j, chip in enumerate(chips):
            copy(1 + j, (*chip, c), me).wait_recv()
            passed[j].start()
        copy(0, sibling, me).wait_recv()
        for j, chip in enumerate(chips):
            copy(4 + j, (*chip, 1 - c), me).wait_recv()
        for cp in first + passed:
            cp.wait_send()
        mine.wait()

    return pl.pallas_call(
        body, name=name, out_shape=S((N_DEV, R, C), v.dtype),
        in_specs=[pl.BlockSpec(memory_space=pl.ANY)], out_specs=pl.BlockSpec(memory_space=pl.ANY),
        scratch_shapes=[pltpu.SemaphoreType.DMA((7,)), pltpu.SemaphoreType.DMA((7,)), pltpu.SemaphoreType.DMA],
    )(v)


class _Exchange:
    def __init__(self, ins, outs, sizes, n_local, plan, aliases=None):
        self.ins, self.outs, self.plan, self.aliases = list(ins), list(outs), plan, aliases or {}
        self.sizes, self.n_local = list(sizes), n_local

    def scratch(self):
        n = sum(self.sizes)
        return [pltpu.SemaphoreType.DMA((n,)), pltpu.SemaphoreType.DMA((n,)), pltpu.SemaphoreType.DMA((max(self.n_local, 1),))]

    def _copies(self, in_refs, out_refs, sems):
        send_sems, recv_sems, local_sems = sems
        phases, local = self.plan(in_refs, out_refs)
        out, k = [], 0
        for phase in phases:
            out.append([pltpu.make_async_remote_copy(src_ref=s, dst_ref=d, send_sem=send_sems.at[k + n], recv_sem=recv_sems.at[k + n],
                                                     device_id=dev, device_id_type=MESH) for n, (s, d, dev) in enumerate(phase)])
            k += len(phase)
        return out, [pltpu.make_async_copy(s, d, local_sems.at[n]) for n, (s, d) in enumerate(local)]

    def start(self, in_refs, out_refs, sems):
        phases, loc = self._copies(in_refs, out_refs, sems)
        for cp in phases[0] + loc:
            cp.start()

    def advance(self, hook, in_refs, out_refs, sems):
        p = hook - (3 - len(self.sizes))
        if p >= 1:
            phases, _ = self._copies(in_refs, out_refs, sems)
            for cp in phases[p - 1]:
                cp.wait_recv()
            for cp in phases[p]:
                cp.start()

    def finish(self, in_refs, out_refs, sems):
        phases, loc = self._copies(in_refs, out_refs, sems)
        for cp in phases[-1]:
            cp.wait_recv()
        for phase in phases:
            for cp in phase:
                cp.wait_send()
        for cp in loc:
            cp.wait()


def gather(v, rows=None, into=None):
    R, C = v.shape
    r0, nr = rows or (0, R)
    na = min(nr, ((nr // 2 + 15) // 16) * 16)

    def plan(ins, outs):
        x, y, c, _ = _place()
        xn, yn, dg, sibling = (1 - x, y), (x, 1 - y), (1 - x, 1 - y), (x, y, 1 - c)
        slot = lambda chip, start=r0, count=nr: outs[0].at[4 * chip[0] + 2 * chip[1] + c, pl.ds(start, count), :]
        src, mine = ins[0].at[pl.ds(r0, nr), :], slot((x, y))
        same = lambda ref, to: (ref, ref, to)
        first = [(src, mine, sibling), (src, mine, (*xn, c)), (src, mine, (*yn, c))]
        relay = [same(slot(xn, r0, na), (*yn, c)), same(slot(yn, r0 + na, nr - na), (*xn, c))]
        onward = [same(slot(xn), sibling), same(slot(yn), sibling), same(slot(dg), sibling)]
        return [first, relay, onward], [(src, mine)]

    if into is None:
        return _Exchange([v], [S((N_DEV, R, C), v.dtype)], (3, 2, 3), 1, plan)
    return _Exchange([v, into], [S((N_DEV, R, C), v.dtype)], (3, 2, 3), 1, plan, aliases={1: 0})


def scatter_first(gb):
    _, R, C = gb.shape

    def plan(ins, outs):
        x, y, c, chips = _place()
        owners = [(x, y)] + chips
        return [[(ins[0].at[4 * px + 2 * py + (1 - c)], outs[0].at[j], (x, y, 1 - c)) for j, (px, py) in enumerate(owners)]], []

    return _Exchange([gb], [S((4, R, C), BF16)], (4,), 0, plan)


def scatter_second(sb, rows=None, into=None):
    r0, nr = rows or (0, sb.shape[1])

    def plan(ins, outs):
        x, y, c, chips = _place()
        part = lambda ref, j: ref.at[j, pl.ds(r0, nr), :]
        return [[(part(ins[0], j), part(outs[0], j), (*chips[j], c)) for j in range(3)]], []

    if into is None:
        return _Exchange([sb], [S(sb.shape, BF16)], (3,), 0, plan)
    return _Exchange([sb, into], [S(sb.shape, BF16)], (3,), 0, plan, aliases={1: 0})


def _call(body, *, name, grid, in_specs, out_specs, out_shape, args, scratch=(), sem=None, riders=(), marks=None):
    single = not isinstance(out_shape, (tuple, list))
    out_shape = (out_shape,) if single else tuple(out_shape)
    out_specs = (out_specs,) if single else tuple(out_specs)
    n_in, n_out, n_sc = len(in_specs), len(out_shape), len(scratch)
    if not riders:
        res = pl.pallas_call(body, name=name, grid=grid, in_specs=list(in_specs), out_specs=out_specs, out_shape=out_shape,
                             scratch_shapes=list(scratch), compiler_params=_params(sem))(*args)
        return res[0] if single else res
    r_ins = [a for r in riders for a in r.ins]
    r_outs = [o for r in riders for o in r.outs]
    r_scr = [s for r in riders for s in r.scratch()]
    aliases, i0, o0 = {}, n_in, n_out
    for r in riders:
        for a, b in r.aliases.items():
            aliases[i0 + a] = o0 + b
        i0, o0 = i0 + len(r.ins), o0 + len(r.outs)
    steps = math.prod(grid)

    def full(*refs):
        ins, rin = refs[:n_in], refs[n_in:n_in + len(r_ins)]
        pos = n_in + len(r_ins)
        outs, rout = refs[pos:pos + n_out], refs[pos + n_out:pos + n_out + len(r_outs)]
        pos += n_out + len(r_outs)
        sc, rsc = refs[pos:pos + n_sc], refs[pos + n_sc:]
        step = 0
        for d, n in enumerate(grid):
            step = step * n + pl.program_id(d)

        def each(method, *lead):
            i, o = 0, 0
            for k, r in enumerate(riders):
                getattr(r, method)(*lead, rin[i:i + len(r.ins)], rout[o:o + len(r.outs)], rsc[3 * k:3 * k + 3])
                i, o = i + len(r.ins), o + len(r.outs)

        @pl.when(step == 0)
        def _():
            each("start")
        body(*ins, *outs, *sc)

        late = max(steps - 1 - max(steps // 8, 1), 0)
        first, second = marks or (min((3 * steps) // 5, late), late)

        @pl.when(step == first)
        def _():
            each("advance", 1)

        @pl.when(step == second)
        def _():
            each("advance", 2)

        @pl.when(step == steps - 1)
        def _():
            each("finish")

    anywhere = pl.BlockSpec(memory_space=pl.ANY)
    res = pl.pallas_call(
        full, name=name, grid=grid, in_specs=list(in_specs) + [anywhere] * len(r_ins),
        out_specs=out_specs + (anywhere,) * len(r_outs), out_shape=out_shape + tuple(r_outs),
        scratch_shapes=list(scratch) + r_scr, input_output_aliases=aliases,
        compiler_params=_params(("arbitrary",) * len(grid)))(*args, *r_ins)
    host, rest, per = res[:n_out], list(res[n_out:]), []
    for r in riders:
        per.append(rest[:len(r.outs)])
        rest = rest[len(r.outs):]
    return (host[0] if single else tuple(host)), per


def idle_host(riders, name):
    def body(o_ref):
        o_ref[...] = jnp.zeros_like(o_ref)

    return _call(body, name=name, grid=(1,), in_specs=[], out_specs=pl.BlockSpec((8, QB), lambda i: (0, 0)),
                 out_shape=S((8, QB), F32), args=(), riders=riders)[1]


def _rows_tile(n, cap):
    return max(t for t in range(16, min(n, cap) + 1, 16) if n % t == 0)


def scatter_add(g, ra, name):
    _, R, C = g.shape
    tr = _rows_tile(R, 176)
    x, y, c, chips = _place()
    slots = jnp.stack([4 * px + 2 * py + c for px, py in [(x, y)] + chips]).astype(jnp.int32)

    def body(s_ref, g0, g1, g2, g3, ra_ref, own_ref, sb_ref):
        own_ref[...] = g0[...] + ra_ref[0].astype(F32)
        for j, gj in enumerate((g1, g2, g3)):
            sb_ref[j] = (gj[...] + ra_ref[j + 1].astype(F32)).astype(BF16)

    spec = pltpu.PrefetchScalarGridSpec(
        num_scalar_prefetch=1, grid=(R // tr,),
        in_specs=[pl.BlockSpec((None, tr, C), lambda i, s, j=j: (s[j], i, 0)) for j in range(4)]
        + [pl.BlockSpec((4, tr, C), lambda i, s: (0, i, 0))],
        out_specs=(pl.BlockSpec((tr, C), lambda i, s: (i, 0)), pl.BlockSpec((3, tr, C), lambda i, s: (0, i, 0))))
    return pl.pallas_call(body, name=name, grid_spec=spec, out_shape=(S((R, C), F32), S((3, R, C), BF16)),
                          compiler_params=_params(("parallel",)))(slots, g, g, g, g, ra)


def rms_cast(h, g, name):
    T, D = h.shape
    tm = _tile(T, 512)

    def body(h_ref, g_ref, n_ref):
        n_ref[...] = _rms(h_ref[...], g_ref[...]).astype(BF16)

    row = pl.BlockSpec((tm, D), lambda i: (i, 0))
    return _call(body, name=name, grid=(T // tm,), out_shape=S((T, D), BF16), in_specs=[row, pl.BlockSpec((1, D), lambda i: (0, 0))],
                 out_specs=row, sem=("parallel",), args=(h, g))


def ffn_up_fwd(n, wgu, name, riders=()):
    T, D = n.shape
    F = wgu.shape[1]
    tr, tn = _tile(T, 512), _tile(F, 256)

    def body(n_ref, wg_ref, wu_ref, dgate_ref, dup_ref, a_ref):
        wg, wu = wg_ref[...], wu_ref[...]
        for r in range(T // tr):
            rows = slice(r * tr, (r + 1) * tr)
            x = n_ref[rows, :]
            gate = _nt(x, wg)
            up = _nt(x, wu)
            s = jax.nn.sigmoid(gate)
            silu = gate * s
            dgate_ref[rows, :] = (up * (s * (1.0 + gate * (1.0 - s)))).astype(BF16)
            dup_ref[rows, :] = silu.astype(BF16)
            a_ref[rows, :] = (silu * up).astype(BF16)

    tile = pl.BlockSpec((T, tn), lambda j: (0, j))
    return _call(
        body, name=name, grid=(F // tn,), out_shape=(S((T, F), BF16),) * 3,
        in_specs=[pl.BlockSpec((T, D), lambda j: (0, 0)),
                  pl.BlockSpec((None, tn, D), lambda j: (0, j, 0)), pl.BlockSpec((None, tn, D), lambda j: (1, j, 0))],
        out_specs=(tile, tile, tile), sem=("parallel",), args=(n, wgu, wgu), riders=riders)


def ffn_down_fwd(a, wd, h, g_next, name, riders=()):
    T, F = a.shape
    D = wd.shape[1]
    tm = _tile(T, 256)

    def body(a_ref, w_ref, h_ref, *rest):
        out = h_ref[...] + 0.5 * _nn(a_ref[...], w_ref[...])
        if g_next is None:
            rest[0][...] = out
        else:
            g_ref, o_ref, n_ref = rest
            o_ref[...] = out
            n_ref[...] = _rms(out, g_ref[...]).astype(BF16)

    row = pl.BlockSpec((tm, D), lambda i: (i, 0))
    more = g_next is not None
    return _call(
        body, name=name, grid=(T // tm,), out_shape=(S((T, D), F32), S((T, D), BF16)) if more else S((T, D), F32),
        in_specs=[pl.BlockSpec((tm, F), lambda i: (i, 0)), pl.BlockSpec((F, D), lambda i: (0, 0)), row]
        + ([pl.BlockSpec((1, D), lambda i: (0, 0))] if more else []),
        out_specs=(row, row) if more else row,
        sem=("parallel",), args=(a, wd, h) + ((g_next,) if more else ()), riders=riders)


def mix_in_fwd(h, g, win, name):
    T, D = h.shape
    N = win.shape[0]
    tm = _tile(T, 256)

    def body(h_ref, g_ref, w_ref, n_ref, p_ref):
        n = _rms(h_ref[...], g_ref[...]).astype(BF16)
        n_ref[...] = n
        p_ref[...] = _nt(n, w_ref[...]).astype(BF16)

    return pl.pallas_call(
        body, name=name, grid=(T // tm,), out_shape=(S((T, D), BF16), S((T, N), BF16)),
        in_specs=[pl.BlockSpec((tm, D), lambda i: (i, 0)), pl.BlockSpec((1, D), lambda i: (0, 0)),
                  pl.BlockSpec((N, D), lambda i: (0, 0))],
        out_specs=(pl.BlockSpec((tm, D), lambda i: (i, 0)), pl.BlockSpec((tm, N), lambda i: (i, 0))),
        compiler_params=_params(("parallel",)),
    )(h, g, win)


def _tri_consts():
    r = lax.broadcasted_iota(jnp.int32, (QB, QB), 0)
    c = lax.broadcasted_iota(jnp.int32, (QB, QB), 1)
    ones = jnp.ones((QB, QB), BF16)
    with_sums = lambda tri: jnp.concatenate([tri.astype(BF16), ones], axis=1)
    return with_sums(r > c), with_sums(r <= c), with_sums(r < c)


def _half_masks():
    lane = lax.broadcasted_iota(jnp.int32, (QB, PAIR), 1)
    row = lax.broadcasted_iota(jnp.int32, (QB, PAIR), 0)
    return lane < HEAD_DIM, lane, row


def sb_attn_fwd(p, after, name, riders=()):
    T = p.shape[0]
    nq = T // QB

    def body(q_ref, k_ref, v_ref, m_ref, o_ref, tot_ref, q_sc, acc_ref, z_sc):
        i = pl.program_id(0)
        lo, lane, row = _half_masks()
        causal = lane < row
        heads, pairs = range(SB_HEADS), range(SB_HEADS // 2)
        for hp in pairs:
            q_sc[hp] = (q_ref[:, hp * PAIR:(hp + 1) * PAIR].astype(F32) * SCALE).astype(BF16)
        m2 = m_ref[...]

        def by_head(ref, j, hp):
            t = ref[pl.ds(pl.multiple_of(j * QB, QB), QB), hp * PAIR:(hp + 1) * PAIR]
            return jnp.concatenate([jnp.where(lo, t, 0), jnp.where(lo, 0, t)], axis=0)

        def scores(j):
            return [_nt(q_sc[hp], by_head(k_ref, j, hp)) for hp in pairs]

        def block(j, diag):
            z2 = [z_sc[hp] for hp in pairs]
            ahead = scores(jnp.maximum(j - 1, 0))
            for hp in pairs:
                z_sc[hp] = ahead[hp]
            vs = [by_head(v_ref, j, hp) for hp in pairs]
            spls = [_softplus_logsig(z2[h // 2][:, (h % 2) * QB:(h % 2 + 1) * QB]) for h in heads]
            sp = [jnp.where(causal, spls[h][0], 0.0) if diag else spls[h][0] for h in heads]
            rr = [_tri(sp[h], m2) for h in heads]
            if diag:
                w = [jnp.where(causal, jnp.exp(spls[h][1] - rr[h][:, :QB]), 0.0).astype(BF16) for h in heads]
            else:
                c = [tot_ref[:, h * QB:(h + 1) * QB] for h in heads]
                w = [jnp.exp(spls[h][1] - (c[h] + rr[h][:, :QB])).astype(BF16) for h in heads]
            pv = [_nn(jnp.concatenate([w[2 * hp], w[2 * hp + 1]], axis=1), vs[hp]) for hp in pairs]
            for hp in pairs:
                acc_ref[hp] = pv[hp] if diag else acc_ref[hp] + pv[hp]
            for h in heads:
                tot_ref[:, h * QB:(h + 1) * QB] = rr[h][:, QB:] if diag else c[h] + rr[h][:, QB:]

        first = scores(i)
        for hp in pairs:
            z_sc[hp] = first[hp]
        block(i, True)

        def step(t, carry):
            block(i - 1 - t, False)
            return carry
        lax.fori_loop(0, i, step, 0)
        for hp in pairs:
            o_ref[:, hp * PAIR:(hp + 1) * PAIR] = acc_ref[hp]

    npair = SB_HEADS // 2
    return _call(
        body, name=name, grid=(nq,), out_shape=(S((T, SB_W), F32), S((T, SB_HEADS * QB), F32)),
        in_specs=[pl.BlockSpec((QB, SB_W), lambda i: (i, 0)), pl.BlockSpec((T, SB_W), lambda i: (0, 1)),
                  pl.BlockSpec((T, SB_W), lambda i: (0, 2)), pl.BlockSpec((QB, 2 * QB), lambda i: (0, 0))],
        out_specs=(pl.BlockSpec((QB, SB_W), lambda i: (i, 0)), pl.BlockSpec((QB, SB_HEADS * QB), lambda i: (i, 0))),
        scratch=[pltpu.VMEM((npair, QB, PAIR), BF16), pltpu.VMEM((npair, QB, PAIR), F32), pltpu.VMEM((npair, QB, 2 * QB), F32)],
        sem=("arbitrary",), args=(p, p, p, after), riders=riders,
        marks=((11 * nq) // 16, (13 * nq) // 16))


def sb_attn_bwd(p, do, tot, upto, before, name, riders=()):
    T = p.shape[0]
    nq = T // QB

    def body(q_ref, k_ref, v_ref, do_ref, tot_ref, mp_ref, mg_ref, dq_ref, dk_ref, dv_ref,
             q_sc, d_sc, qd_sc, pg_sc, dq_acc, dk_acc, dv_acc, zd_sc):
        i = pl.program_id(0)
        lo, lane, row = _half_masks()
        causal = lane < row
        heads, pairs = range(SB_HEADS), range(SB_HEADS // 2)

        def by_head(t):
            return jnp.concatenate([jnp.where(lo, t, 0), jnp.where(lo, 0, t)], axis=0)

        for hp in pairs:
            q2 = (q_ref[:, hp * PAIR:(hp + 1) * PAIR].astype(F32) * SCALE).astype(BF16)
            d2 = do_ref[:, hp * PAIR:(hp + 1) * PAIR].astype(BF16)
            q_sc[hp] = q2
            d_sc[hp] = d2
            qd_sc[hp] = by_head(q2)
            qd_sc[SB_HEADS // 2 + hp] = by_head(d2)
        mp, mg = mp_ref[...], mg_ref[...]

        @pl.when(i == 0)
        def _():
            dk_acc[...] = jnp.zeros_like(dk_acc)
            dv_acc[...] = jnp.zeros_like(dv_acc)
        pg_sc[...] = jnp.zeros_like(pg_sc)
        dq_acc[...] = jnp.zeros_like(dq_acc)

        def rows(ref, j, hp):
            return ref[pl.ds(pl.multiple_of(j * QB, QB), QB), hp * PAIR:(hp + 1) * PAIR]

        def products(j):
            return ([_nt(q_sc[hp], by_head(rows(k_ref, j, hp))) for hp in pairs]
                    + [_nt(d_sc[hp], by_head(rows(v_ref, j, hp))) for hp in pairs])

        def block(j, diag):
            r0 = pl.multiple_of(j * QB, QB)
            half = lambda t, h: t[:, (h % 2) * QB:(h % 2 + 1) * QB]
            z = [half(zd_sc[h // 2], h) for h in heads]
            dw = [half(zd_sc[SB_HEADS // 2 + h // 2], h) for h in heads]
            if not diag:
                ahead = products(j + 1)
                for hp in range(SB_HEADS):
                    zd_sc[hp] = ahead[hp]
            ks = [by_head(rows(k_ref, j, hp)) for hp in pairs]
            spls = [_softplus_logsig(z[h]) for h in heads]
            sp = [jnp.where(causal, spls[h][0], 0.0) if diag else spls[h][0] for h in heads]
            rr = [_tri(sp[h], mp) for h in heads]
            pc = [pg_sc[2 * h] for h in heads]
            w = [jnp.exp(spls[h][1] - (tot_ref[:, h * QB:(h + 1) * QB] - (pc[h] + rr[h][:, :QB]))) for h in heads]
            if diag:
                w = [jnp.where(causal, w[h], 0.0) for h in heads]
            gg = [dw[h] * w[h] for h in heads]
            rg = [_tri(gg[h], mg) for h in heads]
            gc = [pg_sc[2 * h + 1] for h in heads]
            dz = [gg[h] - (gg[h] + gc[h] + rg[h][:, :QB]) * jnp.exp(spls[h][1]) for h in heads]
            if diag:
                dz = [jnp.where(causal, dz[h], 0.0) for h in heads]
            dzb = [dz[h].astype(BF16) for h in heads]
            wb = [w[h].astype(BF16) for h in heads]
            both = lambda t, hp, axis: jnp.concatenate([t[2 * hp], t[2 * hp + 1]], axis=axis)
            dq = [_nn(both(dzb, hp, 1), ks[hp]) for hp in pairs]
            dk = [_tn(both(dzb, hp, 0), qd_sc[hp]) for hp in pairs]
            dv = [_tn(both(wb, hp, 0), qd_sc[SB_HEADS // 2 + hp]) for hp in pairs]
            for h in heads:
                if not diag:
                    pg_sc[2 * h] = pc[h] + rr[h][:, QB:]
                    pg_sc[2 * h + 1] = gc[h] + rg[h][:, QB:]
            for hp in pairs:
                dq_acc[hp] += dq[hp]
                dk_acc[pl.ds(r0, QB), hp * PAIR:(hp + 1) * PAIR] += dk[hp]
                dv_acc[pl.ds(r0, QB), hp * PAIR:(hp + 1) * PAIR] += dv[hp]

        first = products(0)
        for hp in range(SB_HEADS):
            zd_sc[hp] = first[hp]

        def step(t, carry):
            block(t, False)
            return carry
        lax.fori_loop(0, i, step, 0)
        block(i, True)
        for hp in pairs:
            dq_ref[:, hp * PAIR:(hp + 1) * PAIR] = (dq_acc[hp] * SCALE).astype(BF16)

        @pl.when(i == nq - 1)
        def _():
            dk_ref[...] = dk_acc[...].astype(BF16)
            dv_ref[...] = dv_acc[...].astype(BF16)

    qtile = pl.BlockSpec((QB, SB_W), lambda i: (i, 0))
    whole = pl.BlockSpec((T, SB_W), lambda i: (0, 0))
    const = pl.BlockSpec((QB, 2 * QB), lambda i: (0, 0))
    return _call(
        body, name=name, grid=(nq,), out_shape=(S((T, SB_W), BF16),) * 3,
        in_specs=[qtile, pl.BlockSpec((T, SB_W), lambda i: (0, 1)), pl.BlockSpec((T, SB_W), lambda i: (0, 2)), qtile,
                  pl.BlockSpec((QB, SB_HEADS * QB), lambda i: (i, 0)), const, const],
        out_specs=(qtile, whole, whole),
        scratch=[pltpu.VMEM((SB_HEADS // 2, QB, PAIR), BF16), pltpu.VMEM((SB_HEADS // 2, QB, PAIR), BF16),
                 pltpu.VMEM((SB_HEADS, 2 * QB, PAIR), BF16),
                 pltpu.VMEM((2 * SB_HEADS, QB, QB), F32), pltpu.VMEM((SB_HEADS // 2, QB, PAIR), F32),
                 pltpu.VMEM((T, SB_W), F32), pltpu.VMEM((T, SB_W), F32), pltpu.VMEM((SB_HEADS, QB, 2 * QB), F32)],
        sem=("arbitrary",), args=(p, p, p, do, tot, upto, before), riders=riders)


def _t5_buckets():
    a = lax.broadcasted_iota(jnp.int32, (QB, QB), 0)
    c = lax.broadcasted_iota(jnp.int32, (QB, QB), 1)

    def bucket(dist):
        dist = jnp.maximum(dist, 0)
        max_exact = N_BUCKETS // 2
        d = jnp.maximum(dist, 1).astype(F32)
        large = max_exact + (jnp.log(d / max_exact) / math.log(MAX_DISTANCE / max_exact)
                             * (N_BUCKETS - max_exact)).astype(jnp.int32)
        large = jnp.minimum(large, N_BUCKETS - 1)
        return jnp.where(dist < max_exact, dist, large)

    return bucket(QB + a - c), bucket(a - c)


def _swa_common(i, kp_ref, kc_ref, vp_ref, vc_ref, bp_ref, bc_ref, rb_ref, bias_ref):
    lo, lane, row = _half_masks()

    @pl.when(i == 0)
    def _():
        for blk, b_ref in enumerate((bp_ref, bc_ref)):
            bk = b_ref[...]
            for h in range(8):
                acc = jnp.zeros((QB, QB), F32)
                for b in range(N_BUCKETS):
                    acc = jnp.where(bk == b, rb_ref[b, h], acc)
                bias_ref[h, blk] = acc

    band = [(lane > row) & (i > 0), lane <= row]

    def stacks(ref):
        t = ref[...].astype(F32)
        sw = pltpu.roll(t, HEAD_DIM, 1)
        return [jnp.concatenate([jnp.where(lo, t, 0.0), jnp.where(lo, 0.0, sw)], axis=0).astype(BF16),
                jnp.concatenate([jnp.where(lo, sw, 0.0), jnp.where(lo, 0.0, t)], axis=0).astype(BF16)]

    ks = [stacks(kp_ref), stacks(kc_ref)]
    vs = [stacks(vp_ref), stacks(vc_ref)]
    return lo, band, ks, vs


def _lane_half(t, h):
    return t[:, (h % 2) * QB:(h % 2 + 1) * QB]


def swa_fwd(p, sinks, rel_bias, bprev, bcur, name, riders=()):
    T = p.shape[0]
    nq = T // QB
    kcol, vcol = (3 * SB_W + SWA_W) // KV_W, (3 * SB_W + SWA_W) // KV_W + 1

    def body(q_ref, kp_ref, kc_ref, vp_ref, vc_ref, bp_ref, bc_ref, sink_ref, rb_ref, o_ref, lse_ref, bias_ref):
        i = pl.program_id(0)
        lo, band, ks, vs = _swa_common(i, kp_ref, kc_ref, vp_ref, vc_ref, bp_ref, bc_ref, rb_ref, bias_ref)
        heads, pairs, blocks = range(8), range(4), range(2)
        rowmax = lambda t: jnp.max(t, axis=1, keepdims=True)
        rowsum = lambda t: jnp.sum(t, axis=1, keepdims=True)
        q2 = [q_ref[:, g * PAIR:(g + 1) * PAIR] for g in pairs]
        s2 = [[_nt(q2[g], ks[b][g // 2]) for b in blocks] for g in pairs]
        sc = [[jnp.where(band[b], _lane_half(s2[h // 2][b], h) * SCALE + bias_ref[h, b], NEG_INF) for b in blocks] for h in heads]
        sink = [sink_ref[0, h] for h in heads]
        m = [jnp.maximum(jnp.maximum(rowmax(sc[h][0]), rowmax(sc[h][1])), sink[h]) for h in heads]
        e = [[jnp.exp(sc[h][b] - m[h]) for b in blocks] for h in heads]
        den = [rowsum(e[h][0]) + rowsum(e[h][1]) + jnp.exp(sink[h] - m[h]) for h in heads]
        pb = [[(e[h][b] / den[h]).astype(BF16) for b in blocks] for h in heads]
        for g in pairs:
            both = lambda b: jnp.concatenate([pb[2 * g][b], pb[2 * g + 1][b]], axis=1)
            o_ref[:, g * PAIR:(g + 1) * PAIR] = _nn(both(0), vs[0][g // 2]) + _nn(both(1), vs[1][g // 2])
        for h in heads:
            lse_ref[:, h * QB:(h + 1) * QB] = jnp.broadcast_to(m[h] + jnp.log(den[h]), (QB, QB))

    kv = lambda col, prev: pl.BlockSpec((QB, KV_W), (lambda i: (jnp.maximum(i - 1, 0), col)) if prev else (lambda i: (i, col)))
    full = pl.BlockSpec((QB, QB), lambda i: (0, 0))
    smem = pl.BlockSpec(memory_space=pltpu.SMEM)
    return _call(
        body, name=name, grid=(nq,), out_shape=(S((T, SWA_W), F32), S((T, 8 * QB), F32)),
        in_specs=[pl.BlockSpec((QB, SWA_W), lambda i: (i, 3)), kv(kcol, True), kv(kcol, False), kv(vcol, True), kv(vcol, False),
                  full, full, smem, smem],
        out_specs=(pl.BlockSpec((QB, SWA_W), lambda i: (i, 0)), pl.BlockSpec((QB, 8 * QB), lambda i: (i, 0))),
        scratch=[pltpu.VMEM((8, 2, QB, QB), F32)],
        sem=("arbitrary",), args=(p, p, p, p, p, bprev, bcur, sinks, rel_bias), riders=riders)


def swa_bwd(p, do, lse, sinks, rel_bias, bprev, bcur, name, riders=()):
    T = p.shape[0]
    nq = T // QB
    kcol, vcol = (3 * SB_W + SWA_W) // KV_W, (3 * SB_W + SWA_W) // KV_W + 1

    def body(q_ref, kp_ref, kc_ref, vp_ref, vc_ref, do_ref, lse_ref, bp_ref, bc_ref, sink_ref, rb_ref,
             dq_ref, dk_ref, dv_ref, dsink_ref, dsc_ref, bias_ref, dk_acc, dv_acc):
        i = pl.program_id(0)
        lo, band, ks, vs = _swa_common(i, kp_ref, kc_ref, vp_ref, vc_ref, bp_ref, bc_ref, rb_ref, bias_ref)

        @pl.when(i == 0)
        def _():
            dk_acc[...] = jnp.zeros_like(dk_acc)
            dv_acc[...] = jnp.zeros_like(dv_acc)
            dsc_ref[...] = jnp.zeros_like(dsc_ref)
            dsink_ref[...] = jnp.zeros_like(dsink_ref)

        heads, pairs, blocks = range(8), range(4), range(2)
        rowsum = lambda t: jnp.sum(t, axis=1, keepdims=True)
        by_head = lambda t: jnp.concatenate([jnp.where(lo, t, 0), jnp.where(lo, 0, t)], axis=0)
        q2 = [q_ref[:, g * PAIR:(g + 1) * PAIR] for g in pairs]
        d2 = [do_ref[:, g * PAIR:(g + 1) * PAIR].astype(BF16) for g in pairs]
        qs = [by_head(q2[g]) for g in pairs]
        dos = [by_head(d2[g]) for g in pairs]
        s2 = [[_nt(q2[g], ks[b][g // 2]) for b in blocks] for g in pairs]
        dp2 = [[_nt(d2[g], vs[b][g // 2]) for b in blocks] for g in pairs]
        lse_h = [lse_ref[:, h * QB:(h + 1) * QB] for h in heads]
        sink = [sink_ref[0, h] for h in heads]
        pr = [[jnp.exp(jnp.where(band[b], _lane_half(s2[h // 2][b], h) * SCALE + bias_ref[h, b], NEG_INF) - lse_h[h])
               for b in blocks] for h in heads]
        dp = [[_lane_half(dp2[h // 2][b], h) for b in blocks] for h in heads]
        delta = [rowsum(pr[h][0] * dp[h][0]) + rowsum(pr[h][1] * dp[h][1]) for h in heads]
        lane1 = lax.broadcasted_iota(jnp.int32, (1, QB), 1)
        dsink = jnp.zeros((1, QB), F32)
        for h in heads:
            dsink = dsink + jnp.where(lane1 == h, -jnp.sum(jnp.exp(sink[h] - lse_h[h][:, :1]) * delta[h]), 0.0)
        dsink_ref[...] += dsink
        dsc = [[pr[h][b] * (dp[h][b] - delta[h]) for b in blocks] for h in heads]
        for h in heads:
            for b in blocks:
                dsc_ref[h, b] += dsc[h][b]
        dzb = [[(dsc[h][b] * SCALE).astype(BF16) for b in blocks] for h in heads]
        prb = [[pr[h][b].astype(BF16) for b in blocks] for h in heads]
        pair_of = lambda t, g, b, axis: jnp.concatenate([t[2 * g][b], t[2 * g + 1][b]], axis=axis)
        for g in pairs:
            dq = _nn(pair_of(dzb, g, 0, 1), ks[0][g // 2]) + _nn(pair_of(dzb, g, 1, 1), ks[1][g // 2])
            dq_ref[:, g * PAIR:(g + 1) * PAIR] = dq.astype(BF16)

        def key_grad(t, other, b):
            per_kv = [_tn(pair_of(t, 2 * kh, b, 0), other[2 * kh]) + _tn(pair_of(t, 2 * kh + 1, b, 0), other[2 * kh + 1]) for kh in range(2)]
            both = [s + pltpu.roll(s, HEAD_DIM, 1) for s in per_kv]
            return jnp.where(lo, both[0], both[1])

        rp = pl.multiple_of(jnp.maximum(i - 1, 0) * QB, QB)
        rc = pl.multiple_of(i * QB, QB)
        dk_acc[pl.ds(rp, QB), :] += key_grad(dzb, qs, 0)
        dv_acc[pl.ds(rp, QB), :] += key_grad(prb, dos, 0)
        dk_acc[pl.ds(rc, QB), :] += key_grad(dzb, qs, 1)
        dv_acc[pl.ds(rc, QB), :] += key_grad(prb, dos, 1)

        @pl.when(i == nq - 1)
        def _():
            dk_ref[...] = dk_acc[...].astype(BF16)
            dv_ref[...] = dv_acc[...].astype(BF16)

    kv = lambda col, prev: pl.BlockSpec((QB, KV_W), (lambda i: (jnp.maximum(i - 1, 0), col)) if prev else (lambda i: (i, col)))
    full = pl.BlockSpec((QB, QB), lambda i: (0, 0))
    smem = pl.BlockSpec(memory_space=pltpu.SMEM)
    whole = lambda shape: pl.BlockSpec(shape, lambda i: (0,) * len(shape))
    return _call(
        body, name=name, grid=(nq,),
        out_shape=(S((T, SWA_W), BF16), S((T, KV_W), BF16), S((T, KV_W), BF16), S((1, QB), F32), S((8, 2, QB, QB), F32)),
        in_specs=[pl.BlockSpec((QB, SWA_W), lambda i: (i, 3)), kv(kcol, True), kv(kcol, False), kv(vcol, True), kv(vcol, False),
                  pl.BlockSpec((QB, SWA_W), lambda i: (i, 0)), pl.BlockSpec((QB, 8 * QB), lambda i: (i, 0)),
                  full, full, smem, smem],
        out_specs=(pl.BlockSpec((QB, SWA_W), lambda i: (i, 0)), whole((T, KV_W)), whole((T, KV_W)), whole((1, QB)),
                   whole((8, 2, QB, QB))),
        scratch=[pltpu.VMEM((8, 2, QB, QB), F32), pltpu.VMEM((T, KV_W), F32), pltpu.VMEM((T, KV_W), F32)],
        sem=("arbitrary",), args=(p, p, p, p, p, do, lse, bprev, bcur, sinks, rel_bias), riders=riders)


def mix_out_fwd(o_sb, o_sw, g_sb, g_sw, wout, h, g_next, name, riders=()):
    T, D = h.shape
    M = SB_W + SWA_W
    tm = _tile(T, 256)

    def body(a_ref, b_ref, ga_ref, gb_ref, w_ref, h_ref, gn_ref, mx_ref, o_ref, n_ref):
        mx_ref[:, :SB_W] = _rms(a_ref[...], ga_ref[...]).astype(BF16)
        mx_ref[:, SB_W:] = _rms(b_ref[...], gb_ref[...]).astype(BF16)
        out = h_ref[...] + _nn(mx_ref[...], w_ref[...])
        o_ref[...] = out
        n_ref[...] = _rms(out, gn_ref[...]).astype(BF16)

    row = lambda n: pl.BlockSpec((tm, n), lambda i: (i, 0))
    vec = lambda n: pl.BlockSpec((1, n), lambda i: (0, 0))
    return _call(
        body, name=name, grid=(T // tm,), out_shape=(S((T, M), BF16), S((T, D), F32), S((T, D), BF16)),
        in_specs=[row(SB_W), row(SWA_W), vec(SB_W), vec(SWA_W), pl.BlockSpec((M, D), lambda i: (0, 0)), row(D), vec(D)],
        out_specs=(row(M), row(D), row(D)),
        sem=("parallel",), args=(o_sb, o_sw, g_sb, g_sw, wout, h, g_next), riders=riders)


def loss_head(h, g, target, name):
    T, D = h.shape
    tm = _tile(T, 256)

    def body(h_ref, g_ref, t_ref, loss_ref, dh_ref, dhb_ref, dg_ref):
        @pl.when(pl.program_id(0) == 0)
        def _():
            loss_ref[...] = jnp.zeros_like(loss_ref)
            dg_ref[...] = jnp.zeros_like(dg_ref)
        x = h_ref[...]
        err = _rms(x, g_ref[...]) - t_ref[...]
        loss_ref[...] += jnp.full((1, QB), 0.5 * jnp.sum(jnp.mean(err * err, axis=-1)), F32)
        dx, dg = _rms_bwd(err / D, x, g_ref[...])
        dh_ref[...] = dx
        dhb_ref[...] = dx.astype(BF16)
        dg_ref[...] += dg

    row = pl.BlockSpec((tm, D), lambda i: (i, 0))
    vec = pl.BlockSpec((1, D), lambda i: (0, 0))
    return pl.pallas_call(
        body, name=name, grid=(T // tm,), out_shape=(S((1, QB), F32), S((T, D), F32), S((T, D), BF16), S((1, D), F32)),
        in_specs=[row, vec, row], out_specs=(pl.BlockSpec((1, QB), lambda i: (0, 0)), row, row, vec),
        compiler_params=_params(("arbitrary",)),
    )(h, g, target)


def ffn_down_bwd(dhb, wd, gate, up, name, riders=()):
    T, D = dhb.shape
    F = wd.shape[0]
    tr, tn = _tile(T, 512), _tile(F, 256)

    def body(d_ref, w_ref, g_ref, u_ref, o_ref):
        w = w_ref[...]
        for r in range(T // tr):
            rows = slice(r * tr, (r + 1) * tr)
            da = 0.5 * _nt(d_ref[rows, :], w)
            o_ref[0, rows, :] = (da * g_ref[rows, :].astype(F32)).astype(BF16)
            o_ref[1, rows, :] = (da * u_ref[rows, :].astype(F32)).astype(BF16)

    tile = pl.BlockSpec((T, tn), lambda j: (0, j))
    return _call(
        body, name=name, grid=(F // tn,), out_shape=S((2, T, F), BF16),
        in_specs=[pl.BlockSpec((T, D), lambda j: (0, 0)), pl.BlockSpec((tn, D), lambda j: (j, 0)), tile, tile],
        out_specs=pl.BlockSpec((2, T, tn), lambda j: (0, 0, j)),
        sem=("parallel",), args=(dhb, wd, gate, up), riders=riders)


def tn_matmul(xs, y, alpha, name, riders=()):
    B, T, N = xs.shape
    D = y.shape[1]
    tn = _tile(N, 256)

    def body(x_ref, y_ref, o_ref, ob_ref):
        o = alpha * _tn(x_ref[...], y_ref[...])
        o_ref[...] = o
        ob_ref[...] = o.astype(BF16)

    tile = pl.BlockSpec((None, tn, D), lambda s, j: (s, j, 0))
    return _call(
        body, name=name, grid=(B, N // tn), out_shape=(S((B, N, D), F32), S((B, N, D), BF16)),
        in_specs=[pl.BlockSpec((None, T, tn), lambda s, j: (s, 0, j)), pl.BlockSpec((T, D), lambda s, j: (0, 0))],
        out_specs=(tile, tile), sem=("parallel", "parallel"), args=(xs, y), riders=riders)


def nn_rms_bwd(xs, ws, h_in, g, dh, name, riders=()):
    B, T, K = xs.shape
    D = ws.shape[2]
    tm = _tile(T, 256)

    def body(x_ref, w_ref, h_ref, g_ref, d_ref, o_ref, ob_ref, dg_ref):
        @pl.when(pl.program_id(0) == 0)
        def _():
            dg_ref[...] = jnp.zeros_like(dg_ref)
        dn = _nn(x_ref[0], w_ref[0])
        for s in range(1, B):
            dn = dn + _nn(x_ref[s], w_ref[s])
        dx, dg = _rms_bwd(dn, h_ref[...], g_ref[...])
        out = d_ref[...] + dx
        o_ref[...] = out
        ob_ref[...] = out.astype(BF16)
        dg_ref[...] += dg

    row = pl.BlockSpec((tm, D), lambda i: (i, 0))
    vec = pl.BlockSpec((1, D), lambda i: (0, 0))
    return _call(
        body, name=name, grid=(T // tm,), out_shape=(S((T, D), F32), S((T, D), BF16), S((1, D), F32)),
        in_specs=[pl.BlockSpec((B, tm, K), lambda i: (0, i, 0)), pl.BlockSpec((B, K, D), lambda i: (0, 0, 0)), row, vec, row],
        out_specs=(row, row, vec),
        sem=("arbitrary",), args=(xs, ws, h_in, g, dh), riders=riders)


def mix_out_bwd(dhb, wout, o_sb, o_sw, g_sb, g_sw, name):
    T, D = dhb.shape
    tm = _tile(T, 256)

    def body(d_ref, w_ref, a_ref, b_ref, ga_ref, gb_ref, da_ref, db_ref, dga_ref, dgb_ref):
        @pl.when(pl.program_id(0) == 0)
        def _():
            dga_ref[...] = jnp.zeros_like(dga_ref)
            dgb_ref[...] = jnp.zeros_like(dgb_ref)
        dm = _nt(d_ref[...], w_ref[...])
        dxa, dga = _rms_bwd(dm[:, :SB_W], a_ref[...], ga_ref[...])
        dxb, dgb = _rms_bwd(dm[:, SB_W:], b_ref[...], gb_ref[...])
        da_ref[...] = dxa
        db_ref[...] = dxb
        dga_ref[...] += dga
        dgb_ref[...] += dgb

    row = lambda n: pl.BlockSpec((tm, n), lambda i: (i, 0))
    vec = lambda n: pl.BlockSpec((1, n), lambda i: (0, 0))
    return pl.pallas_call(
        body, name=name, grid=(T // tm,),
        out_shape=(S((T, SB_W), F32), S((T, SWA_W), F32), S((1, SB_W), F32), S((1, SWA_W), F32)),
        in_specs=[row(D), pl.BlockSpec((SB_W + SWA_W, D), lambda i: (0, 0)), row(SB_W), row(SWA_W), vec(SB_W), vec(SWA_W)],
        out_specs=(row(SB_W), row(SWA_W), vec(SB_W), vec(SWA_W)),
        compiler_params=_params(("arbitrary",)),
    )(dhb, wout, o_sb, o_sw, g_sb, g_sw)


def rel_bias_grad(dscs, bprev, bcur, name):
    n = len(dscs)

    def body(*refs):
        bp_ref, bc_ref, o_ref = refs[n], refs[n + 1], refs[n + 2]
        bks = [bp_ref[...], bc_ref[...]]
        row = lax.broadcasted_iota(jnp.int32, (N_BUCKETS, QB), 0)
        lane = lax.broadcasted_iota(jnp.int32, (N_BUCKETS, QB), 1)
        out = jnp.zeros((N_BUCKETS, QB), F32)
        for h in range(8):
            tot = [sum(refs[l][h, b] for l in range(n)) for b in range(2)]
            for b in range(N_BUCKETS):
                val = jnp.sum(jnp.where(bks[0] == b, tot[0], 0.0)) + jnp.sum(jnp.where(bks[1] == b, tot[1], 0.0))
                out = jnp.where((row == b) & (lane == h), val, out)
        o_ref[...] = out

    return pl.pallas_call(body, name=name, out_shape=S((N_BUCKETS, QB), F32), compiler_params=_params())(*dscs, bprev, bcur)


def _adamw(w, g, m, v):
    m = ADAM_B1 * m + (1.0 - ADAM_B1) * g
    v = ADAM_B2 * v + (1.0 - ADAM_B2) * (g * g)
    m_hat = m / (1.0 - ADAM_B1 ** ADAM_STEP)
    v_hat = v / (1.0 - ADAM_B2 ** ADAM_STEP)
    delta = -ADAM_LR * (m_hat / (jnp.sqrt(v_hat) + ADAM_EPS) + ADAM_WD * w)
    return delta, m, v


def adamw_scattered(w, m, v, owns, others, name, riders=()):
    L, R, C = w.shape
    tr = _rows_tile(R, 176)

    def body(w_ref, m_ref, v_ref, *rest):
        own_refs, other_refs = rest[:L], rest[L:2 * L]
        g_ref, d_ref, mo_ref, vo_ref = rest[2 * L:]
        layer = pl.program_id(0)

        def grad(k):
            o = other_refs[k]
            return own_refs[k][...] + o[0].astype(F32) + o[1].astype(F32) + o[2].astype(F32)

        g = grad(0)
        for k in range(1, L):
            g = jnp.where(layer == k, grad(k), g)
        d, mn, vn = _adamw(w_ref[...], g, m_ref[...], v_ref[...])
        g_ref[...] = g
        d_ref[...] = d
        mo_ref[...] = mn
        vo_ref[...] = vn

    tile = pl.BlockSpec((None, tr, C), lambda l, i: (l, i, 0))
    return _call(
        body, name=name, grid=(L, R // tr), out_shape=(S((L, R, C), F32),) * 4,
        in_specs=[tile] * 3 + [pl.BlockSpec((tr, C), lambda l, i: (i, 0))] * L + [pl.BlockSpec((3, tr, C), lambda l, i: (0, i, 0))] * L,
        out_specs=(tile,) * 4, sem=("parallel", "parallel"), args=(w, m, v, *owns, *others), riders=riders)


def adamw_small(w, gs, m, v, name):
    R, C = w.shape

    def body(w_ref, g_ref, m_ref, v_ref, go_ref, d_ref, mo_ref, vo_ref):
        g = g_ref[0]
        for k in range(1, N_DEV):
            g = g + g_ref[k]
        d, mn, vn = _adamw(w_ref[...], g, m_ref[...], v_ref[...])
        go_ref[...] = g
        d_ref[...] = d
        mo_ref[...] = mn
        vo_ref[...] = vn

    return pl.pallas_call(body, name=name, out_shape=(S((R, C), F32),) * 4, compiler_params=_params())(w, gs, m, v)


def kernel(x, norm_ffn1, w_ffn1_gu, w_ffn1_down, norm_mix, w_in, sinks, norm_out_sb, norm_out_swa, w_out, norm_ffn2, w_ffn2_gu, w_ffn2_down, rel_bias, norm_final, loss_target, m_norm_ffn1, m_w_ffn1_gu, m_w_ffn1_down, m_norm_mix, m_w_in, m_sinks, m_norm_out_sb, m_norm_out_swa, m_w_out, m_norm_ffn2, m_w_ffn2_gu, m_w_ffn2_down, m_rel_bias, m_norm_final, v_norm_ffn1, v_w_ffn1_gu, v_w_ffn1_down, v_norm_mix, v_w_in, v_sinks, v_norm_out_sb, v_norm_out_swa, v_w_out, v_norm_ffn2, v_w_ffn2_gu, v_w_ffn2_down, v_rel_bias, v_norm_final):
    L = norm_ffn1.shape[0]
    T, D = x.shape[1], x.shape[2]
    F = w_ffn1_down.shape[1] * N_DEV
    h = x.reshape(T, D)
    target = loss_target.reshape(T, D)
    after, upto, before = _tri_consts()
    bprev, bcur = _t5_buckets()

    local = {}
    for l in range(L):
        local[f"gu1_{l}"] = w_ffn1_gu[l].T.astype(BF16)
        local[f"d1_{l}"] = w_ffn1_down[l].astype(BF16)
        local[f"in_{l}"] = w_in[l].T.astype(BF16)
        local[f"out_{l}"] = w_out[l].astype(BF16)
        local[f"gu2_{l}"] = w_ffn2_gu[l].T.astype(BF16)
        local[f"d2_{l}"] = w_ffn2_down[l].astype(BF16)
    full, partial = {}, {}
    grads, chip_sum, recv_b = {}, {}, {}

    def run(fn, *args, ag=(), rs1=(), rs2=()):
        halves = lambda names: [n if isinstance(n, tuple) else (n, None) for n in names]
        ag, rs2 = [(n, k) for n, k in halves(ag) if n in local], halves(rs2)
        rows = lambda k, total: None if k is None else (k * (total // 2), total // 2)

        def second(n, k):
            sb = chip_sum[n][1]
            return scatter_second(sb, rows(k, sb.shape[1]), recv_b.get(n))

        riders = ([gather(local[n], rows(k, local[n].shape[0]), partial.get(n)) for n, k in ag]
                  + [scatter_first(grads[n][1]) for n in rs1] + [second(n, k) for n, k in rs2])
        if not riders:
            return fn(*args)
        outs, per = fn(*args, riders=riders)
        per = [p[0] for p in per]
        for n, k in ag:
            buf = per.pop(0)
            if k == 0:
                partial[n] = buf
            else:
                full[n] = buf.reshape(N_DEV * buf.shape[1], D)
        for n in rs1:
            chip_sum[n] = scatter_add(grads[n][0], per.pop(0), f"rs_add_{n}")
        for n, _ in rs2:
            recv_b[n] = per.pop(0)
        return outs

    def idle(name, riders=()):
        return None, idle_host(riders, name)

    gu = lambda n: full[n].reshape(2, F, D)
    slots = lambda pair: tuple(t.reshape(N_DEV, -1, D) for t in pair)
    vec = lambda a: a.reshape(1, -1)

    run(idle, "ag_head", ag=("gu1_0",))
    saved = []
    n_next = rms_cast(h, vec(norm_ffn1[0]), "rms_first")
    for l in range(L):
        nx = l + 1
        s = {"h0": h, "n1": n_next}
        s["gate1"], s["up1"], s["a1"] = run(ffn_up_fwd, s["n1"], gu(f"gu1_{l}"), f"ffn1_up{l}",
                                            ag=(f"d1_{l}",) + ((("in_0", 0),) if l == 0 else ()))
        h = run(ffn_down_fwd, s["a1"], full[f"d1_{l}"], h, None, f"ffn1_down{l}", ag=((f"in_{l}", 1),))
        s["h1"] = h
        s["n2"], s["p"] = mix_in_fwd(h, vec(norm_mix[l]), full[f"in_{l}"], f"mix_in{l}")
        s["o_sb"], s["tot"] = run(sb_attn_fwd, s["p"], after, f"sb_fwd{l}", ag=(f"out_{l}", f"gu2_{l}", f"d2_{l}"))
        s["o_sw"], s["lse"] = run(swa_fwd, s["p"], vec(sinks[l]), rel_bias, bprev, bcur, f"swa_fwd{l}", ag=((f"gu1_{nx}", 0),))
        s["mixed"], h, s["n3"] = run(mix_out_fwd, s["o_sb"], s["o_sw"], vec(norm_out_sb[l]), vec(norm_out_swa[l]),
                                     full[f"out_{l}"], h, vec(norm_ffn2[l]), f"mix_out{l}")
        s["h2"] = h
        s["gate2"], s["up2"], s["a2"] = run(ffn_up_fwd, s["n3"], gu(f"gu2_{l}"), f"ffn2_up{l}", ag=((f"gu1_{nx}", 1),))
        if nx < L:
            h, n_next = run(ffn_down_fwd, s["a2"], full[f"d2_{l}"], h, vec(norm_ffn1[nx]), f"ffn2_down{l}", ag=((f"in_{nx}", 0),))
        else:
            h = run(ffn_down_fwd, s["a2"], full[f"d2_{l}"], h, None, f"ffn2_down{l}")
        saved.append(s)

    loss_part, dh, dhb, dg_final = loss_head(h, vec(norm_final), target, "loss_head")
    loss = lax.psum(loss_part[0, 0], ("x", "y", "c"))

    small = {k: [None] * L for k in ("ffn1", "mix", "sinks", "osb", "osw", "ffn2", "dsc")}
    for l in reversed(range(L)):
        s = saved[l]

        def ffn_bwd(dh, dhb, tag, gate, up, a, n, h_in, g, r_down, r_dwgu, r_up):
            gu_n, d_n = f"gu{tag}_{l}", f"d{tag}_{l}"
            dgu = run(ffn_down_bwd, dhb, full[d_n], gate, up, f"ffn{tag}_down_bwd{l}", **r_down)
            grads[gu_n] = slots(run(tn_matmul, dgu, n, 1.0, f"ffn{tag}_dwgu{l}", **r_dwgu))
            grads[d_n] = slots(run(tn_matmul, a[None], dhb, 0.5, f"ffn{tag}_dwd{l}", rs1=(gu_n,)))
            return run(nn_rms_bwd, dgu, gu(gu_n), h_in, g, dh, f"ffn{tag}_up_bwd{l}", rs1=(d_n,), **r_up)

        later = l + 1 < L
        dh, dhb, small["ffn2"][l] = ffn_bwd(dh, dhb, 2, s["gate2"], s["up2"], s["a2"], s["n3"], s["h2"], vec(norm_ffn2[l]),
                                            dict(rs2=((f"gu1_{l + 1}", 1),) if later else ()),
                                            dict(rs2=(f"d1_{l + 1}",) if later else ()), {})
        do_sb, do_sw, small["osb"][l], small["osw"][l] = mix_out_bwd(
            dhb, full[f"out_{l}"], s["o_sb"], s["o_sw"], vec(norm_out_sb[l]), vec(norm_out_swa[l]), f"mix_out_bwd{l}")
        grads[f"out_{l}"] = slots(tn_matmul(s["mixed"][None], dhb, 1.0, f"dwout{l}"))
        dq_sb, dk_sb, dv_sb = run(sb_attn_bwd, s["p"], do_sb, s["tot"], upto, before, f"sb_bwd{l}",
                                  rs2=(f"gu2_{l}", f"d2_{l}"), rs1=(f"out_{l}",))
        dq_sw, dk_sw, dv_sw, small["sinks"][l], small["dsc"][l] = run(
            swa_bwd, s["p"], do_sw, s["lse"], vec(sinks[l]), rel_bias, bprev, bcur, f"swa_bwd{l}", rs2=(f"out_{l}",))
        dp = jnp.concatenate([dq_sb, dk_sb, dv_sb, dq_sw, dk_sw, dv_sw], axis=1)
        dh, dhb, small["mix"][l] = nn_rms_bwd(dp[None], full[f"in_{l}"][None], s["h1"], vec(norm_mix[l]), dh, f"mix_in_bwd{l}")
        grads[f"in_{l}"] = slots(tn_matmul(dp[None], s["n2"], 1.0, f"dwin{l}"))
        dh, dhb, small["ffn1"][l] = ffn_bwd(dh, dhb, 1, s["gate1"], s["up1"], s["a1"], s["n1"], s["h0"], vec(norm_ffn1[l]),
                                            dict(rs1=(f"in_{l}",)), dict(rs2=(f"in_{l}",)), dict(rs2=((f"gu1_{l}", 0),)))

    grad_x = dh.reshape(x.shape)

    upd = {}
    for nm, w, m, v, transposed, last in (
            ("gu2", w_ffn2_gu, m_w_ffn2_gu, v_w_ffn2_gu, True, (("gu1_0", 1),)), ("d2", w_ffn2_down, m_w_ffn2_down, v_w_ffn2_down, False, ("d1_0",)),
            ("in", w_in, m_w_in, v_w_in, True, ()), ("out", w_out, m_w_out, v_w_out, False, ()),
            ("gu1", w_ffn1_gu, m_w_ffn1_gu, v_w_ffn1_gu, True, ()), ("d1", w_ffn1_down, m_w_ffn1_down, v_w_ffn1_down, False, ())):
        turn = (lambda a: jnp.swapaxes(a, 1, 2)) if transposed else (lambda a: a)
        names = [f"{nm}_{l}" for l in range(L)]
        res = run(adamw_scattered, turn(w), turn(m), turn(v), [chip_sum[n][0] for n in names], [recv_b[n] for n in names],
                  f"adamw_{nm}", rs2=last)
        upd[nm] = tuple(turn(r) for r in res)

    d_rel = rel_bias_grad(small["dsc"], bprev, bcur, "rel_bias_grad")[:, :8]

    PW = max(D, SB_W + SWA_W)

    def pack(ffn1, mix, ffn2, final, osb, osw, snk, rel):
        wide = lambda a: jnp.pad(a.reshape(-1), (0, PW - a.size))
        rows = [wide(ffn1[l]) for l in range(L)] + [wide(mix[l]) for l in range(L)] + [wide(ffn2[l]) for l in range(L)]
        rows.append(wide(final))
        rows += [wide(jnp.concatenate([osb[l].reshape(-1), osw[l].reshape(-1)])) for l in range(L)]
        rows.append(wide(jnp.concatenate([snk[l].reshape(-1)[:8] for l in range(L)] + [rel.reshape(-1)])))
        arr = jnp.stack(rows)
        return jnp.pad(arr, ((0, (-arr.shape[0]) % 8), (0, 0)))

    def unpack(arr):
        ffn1, mix, ffn2 = arr[0:L, :D], arr[L:2 * L, :D], arr[2 * L:3 * L, :D]
        final = arr[3 * L, :D]
        ob = arr[3 * L + 1:4 * L + 1]
        tail = arr[4 * L + 1]
        return (ffn1, mix, tail[:8 * L].reshape(L, 8), ob[:, :SB_W], ob[:, SB_W:SB_W + SWA_W], ffn2,
                tail[8 * L:8 * L + N_BUCKETS * 8].reshape(N_BUCKETS, 8), final)

    g_small = pack(small["ffn1"], small["mix"], small["ffn2"], dg_final, small["osb"], small["osw"], small["sinks"], d_rel)
    w_small = pack(norm_ffn1, norm_mix, norm_ffn2, norm_final, norm_out_sb, norm_out_swa, sinks, rel_bias)
    m_small = pack(m_norm_ffn1, m_norm_mix, m_norm_ffn2, m_norm_final, m_norm_out_sb, m_norm_out_swa, m_sinks, m_rel_bias)
    v_small = pack(v_norm_ffn1, v_norm_mix, v_norm_ffn2, v_norm_final, v_norm_out_sb, v_norm_out_swa, v_sinks, v_rel_bias)
    gs_small = all_gather_rows(g_small, "ag_small")
    small_out = [unpack(a) for a in adamw_small(w_small, gs_small, m_small, v_small, "adamw_small")]

    def group(k):
        sm = small_out[k]
        return (sm[0], upd["gu1"][k], upd["d1"][k], sm[1], upd["in"][k], sm[2], sm[3], sm[4], upd["out"][k], sm[5],
                upd["gu2"][k], upd["d2"][k], sm[6], sm[7])

    return (loss, grad_x, *group(0), *group(1), *group(2), *group(3))
```

```python
import math

import jax
import jax.numpy as jnp
from jax import lax
from jax.experimental import pallas as pl
from jax.experimental.pallas import tpu as pltpu

F32 = jnp.float32
BF16 = jnp.bfloat16
S = jax.ShapeDtypeStruct

N_DEV = 8
HEAD_DIM = 64
SB_HEADS = 8
PAIR = 2 * HEAD_DIM
SB_W = 512
SWA_W = 512
KV_W = 128
IN_W = 3 * SB_W + SWA_W + 2 * KV_W
QB = 128
N_BUCKETS = 32
MAX_DISTANCE = 128
EPS = 1e-6
NEG_INF = -1e30
SCALE = HEAD_DIM ** -0.5

ADAM_LR = 0.001
ADAM_B1 = 0.9
ADAM_B2 = 0.999
ADAM_EPS = 1e-08
ADAM_WD = 0.01
ADAM_STEP = 10

VMEM_LIMIT = 56 * 1024 * 1024
MESH = pl.DeviceIdType.MESH


def _params(sem=None, vmem=VMEM_LIMIT):
    return pltpu.CompilerParams(dimension_semantics=sem, vmem_limit_bytes=vmem)


def _nn(a, b):
    return jnp.dot(a, b, preferred_element_type=F32)


def _nt(a, b):
    return lax.dot_general(a, b, (((1,), (1,)), ((), ())), preferred_element_type=F32)


def _tn(a, b):
    return lax.dot_general(a, b, (((0,), (0,)), ((), ())), preferred_element_type=F32)


def _tri(xs, m):
    return [_nn(x.astype(BF16), m) for x in xs]


def _rms(x, g):
    r = lax.rsqrt(jnp.mean(x * x, axis=-1, keepdims=True) + EPS)
    return x * r * g


def _rms_bwd(dy, x, g):
    r = lax.rsqrt(jnp.mean(x * x, axis=-1, keepdims=True) + EPS)
    xhat = x * r
    u = dy * g
    dx = r * (u - xhat * jnp.mean(u * xhat, axis=-1, keepdims=True))
    return dx, jnp.sum(dy * xhat, axis=0, keepdims=True)


def _softplus_logsig(z):
    sp = jnp.maximum(z, 0.0) + jnp.log(1.0 + jnp.exp(-jnp.abs(z)))
    return sp, z - sp


def _tile(n, want):
    t = min(n, want)
    while n % t:
        t //= 2
    return t


def _place():
    x, y, c = lax.axis_index("x"), lax.axis_index("y"), lax.axis_index("c")
    chips = [(1 - x, y), (x, 1 - y), (1 - x, 1 - y)]
    return x, y, c, chips


def all_gather_rows(v, name):
    R, C = v.shape

    def body(v_ref, out_ref, send_sems, recv_sems, local_sem):
        x, y, c, chips = _place()
        me, sibling = (x, y, c), (x, y, 1 - c)

        def slot(px, py, pc):
            return out_ref.at[4 * px + 2 * py + pc]

        def copy(k, block, to, src=None):
            return pltpu.make_async_remote_copy(
                src_ref=slot(*block) if src is None else src, dst_ref=slot(*block),
                send_sem=send_sems.at[k], recv_sem=recv_sems.at[k], device_id=to, device_id_type=MESH)

        mine = pltpu.make_async_copy(v_ref, slot(*me), local_sem)
        mine.start()
        first = [copy(0, me, sibling, src=v_ref)]
        first += [copy(1 + j, me, (*chip, c), src=v_ref) for j, chip in enumerate(chips)]
        for cp in first:
            cp.start()
        passed = [copy(4 + j, (*chip, c), sibling) for j, chip in enumerate(chips)]
        for j, chip in enumerate(chips):
            copy(1 + j, (*chip, c), me).wait_recv()
            passed[j].start()
        copy(0, sibling, me).wait_recv()
        for j, chip in enumerate(chips):
            copy(4 + j, (*chip, 1 - c), me).wait_recv()
        for cp in first + passed:
            cp.wait_send()
        mine.wait()

    return pl.pallas_call(
        body, name=name, out_shape=S((N_DEV, R, C), v.dtype),
        in_specs=[pl.BlockSpec(memory_space=pl.ANY)], out_specs=pl.BlockSpec(memory_space=pl.ANY),
        scratch_shapes=[pltpu.SemaphoreType.DMA((7,)), pltpu.SemaphoreType.DMA((7,)), pltpu.SemaphoreType.DMA],
    )(v)


class _Exchange:
    def __init__(self, ins, outs, sizes, n_local, plan, aliases=None):
        self.ins, self.outs, self.plan, self.aliases = list(ins), list(outs), plan, aliases or {}
        self.sizes, self.n_local = list(sizes), n_local

    def scratch(self):
        n = sum(self.sizes)
        return [pltpu.SemaphoreType.DMA((n,)), pltpu.SemaphoreType.DMA((n,)), pltpu.SemaphoreType.DMA((max(self.n_local, 1),))]

    def _copies(self, in_refs, out_refs, sems):
        send_sems, recv_sems, local_sems = sems
        phases, local = self.plan(in_refs, out_refs)
        out, k = [], 0
        for phase in phases:
            out.append([pltpu.make_async_remote_copy(src_ref=s, dst_ref=d, send_sem=send_sems.at[k + n], recv_sem=recv_sems.at[k + n],
                                                     device_id=dev, device_id_type=MESH) for n, (s, d, dev) in enumerate(phase)])
            k += len(phase)
        return out, [pltpu.make_async_copy(s, d, local_sems.at[n]) for n, (s, d) in enumerate(local)]

    def start(self, in_refs, out_refs, sems):
        phases, loc = self._copies(in_refs, out_refs, sems)
        for cp in phases[0] + loc:
            cp.start()

    def advance(self, hook, in_refs, out_refs, sems):
        p = hook - (3 - len(self.sizes))
        if p >= 1:
            phases, _ = self._copies(in_refs, out_refs, sems)
            for cp in phases[p - 1]:
                cp.wait_recv()
            for cp in phases[p]:
                cp.start()

    def finish(self, in_refs, out_refs, sems):
        phases, loc = self._copies(in_refs, out_refs, sems)
        for cp in phases[-1]:
            cp.wait_recv()
        for phase in phases:
            for cp in phase:
                cp.wait_send()
        for cp in loc:
            cp.wait()


def gather(v, rows=None, into=None):
    R, C = v.shape
    r0, nr = rows or (0, R)
    na = min(nr, ((nr // 2 + 15) // 16) * 16)

    def plan(ins, outs):
        x, y, c, _ = _place()
        xn, yn, dg, sibling = (1 - x, y), (x, 1 - y), (1 - x, 1 - y), (x, y, 1 - c)
        slot = lambda chip, start=r0, count=nr: outs[0].at[4 * chip[0] + 2 * chip[1] + c, pl.ds(start, count), :]
        src, mine = ins[0].at[pl.ds(r0, nr), :], slot((x, y))
        same = lambda ref, to: (ref, ref, to)
        first = [(src, mine, sibling), (src, mine, (*xn, c)), (src, mine, (*yn, c))]
        relay = [same(slot(xn, r0, na), (*yn, c)), same(slot(yn, r0 + na, nr - na), (*xn, c))]
        onward = [same(slot(xn), sibling), same(slot(yn), sibling), same(slot(dg), sibling)]
        return [first, relay, onward], [(src, mine)]

    if into is None:
        return _Exchange([v], [S((N_DEV, R, C), v.dtype)], (3, 2, 3), 1, plan)
    return _Exchange([v, into], [S((N_DEV, R, C), v.dtype)], (3, 2, 3), 1, plan, aliases={1: 0})


def scatter_first(gb):
    _, R, C = gb.shape

    def plan(ins, outs):
        x, y, c, chips = _place()
        owners = [(x, y)] + chips
        return [[(ins[0].at[4 * px + 2 * py + (1 - c)], outs[0].at[j], (x, y, 1 - c)) for j, (px, py) in enumerate(owners)]], []

    return _Exchange([gb], [S((4, R, C), BF16)], (4,), 0, plan)


def scatter_second(sb, rows=None, into=None):
    r0, nr = rows or (0, sb.shape[1])

    def plan(ins, outs):
        x, y, c, chips = _place()
        part = lambda ref, j: ref.at[j, pl.ds(r0, nr), :]
        return [[(part(ins[0], j), part(outs[0], j), (*chips[j], c)) for j in range(3)]], []

    if into is None:
        return _Exchange([sb], [S(sb.shape, BF16)], (3,), 0, plan)
    return _Exchange([sb, into], [S(sb.shape, BF16)], (3,), 0, plan, aliases={1: 0})


def _call(body, *, name, grid, in_specs, out_specs, out_shape, args, scratch=(), sem=None, riders=(), marks=None):
    single = not isinstance(out_shape, (tuple, list))
    out_shape = (out_shape,) if single else tuple(out_shape)
    out_specs = (out_specs,) if single else tuple(out_specs)
    n_in, n_out, n_sc = len(in_specs), len(out_shape), len(scratch)
    if not riders:
        res = pl.pallas_call(body, name=name, grid=grid, in_specs=list(in_specs), out_specs=out_specs, out_shape=out_shape,
                             scratch_shapes=list(scratch), compiler_params=_params(sem))(*args)
        return res[0] if single else res
    r_ins = [a for r in riders for a in r.ins]
    r_outs = [o for r in riders for o in r.outs]
    r_scr = [s for r in riders for s in r.scratch()]
    aliases, i0, o0 = {}, n_in, n_out
    for r in riders:
        for a, b in r.aliases.items():
            aliases[i0 + a] = o0 + b
        i0, o0 = i0 + len(r.ins), o0 + len(r.outs)
    steps = math.prod(grid)

    def full(*refs):
        ins, rin = refs[:n_in], refs[n_in:n_in + len(r_ins)]
        pos = n_in + len(r_ins)
        outs, rout = refs[pos:pos + n_out], refs[pos + n_out:pos + n_out + len(r_outs)]
        pos += n_out + len(r_outs)
        sc, rsc = refs[pos:pos + n_sc], refs[pos + n_sc:]
        step = 0
        for d, n in enumerate(grid):
            step = step * n + pl.program_id(d)

        def each(method, *lead):
            i, o = 0, 0
            for k, r in enumerate(riders):
                getattr(r, method)(*lead, rin[i:i + len(r.ins)], rout[o:o + len(r.outs)], rsc[3 * k:3 * k + 3])
                i, o = i + len(r.ins), o + len(r.outs)

        @pl.when(step == 0)
        def _():
            each("start")
        body(*ins, *outs, *sc)

        late = max(steps - 1 - max(steps // 8, 1), 0)
        first, second = marks or (min((3 * steps) // 5, late), late)

        @pl.when(step == first)
        def _():
            each("advance", 1)

        @pl.when(step == second)
        def _():
            each("advance", 2)

        @pl.when(step == steps - 1)
        def _():
            each("finish")

    anywhere = pl.BlockSpec(memory_space=pl.ANY)
    res = pl.pallas_call(
        full, name=name, grid=grid, in_specs=list(in_specs) + [anywhere] * len(r_ins),
        out_specs=out_specs + (anywhere,) * len(r_outs), out_shape=out_shape + tuple(r_outs),
        scratch_shapes=list(scratch) + r_scr, input_output_aliases=aliases,
        compiler_params=_params(("arbitrary",) * len(grid)))(*args, *r_ins)
    host, rest, per = res[:n_out], list(res[n_out:]), []
    for r in riders:
        per.append(rest[:len(r.outs)])
        rest = rest[len(r.outs):]
    return (host[0] if single else tuple(host)), per


def _rows_tile(n, cap):
    return max(t for t in range(16, min(n, cap) + 1, 16) if n % t == 0)


def scatter_add(g, ra, name):
    _, R, C = g.shape
    tr = _rows_tile(R, 176)
    x, y, c, chips = _place()
    slots = jnp.stack([4 * px + 2 * py + c for px, py in [(x, y)] + chips]).astype(jnp.int32)

    def body(s_ref, g0, g1, g2, g3, ra_ref, own_ref, sb_ref):
        own_ref[...] = g0[...] + ra_ref[0].astype(F32)
        for j, gj in enumerate((g1, g2, g3)):
            sb_ref[j] = (gj[...] + ra_ref[j + 1].astype(F32)).astype(BF16)

    spec = pltpu.PrefetchScalarGridSpec(
        num_scalar_prefetch=1, grid=(R // tr,),
        in_specs=[pl.BlockSpec((None, tr, C), lambda i, s, j=j: (s[j], i, 0)) for j in range(4)]
        + [pl.BlockSpec((4, tr, C), lambda i, s: (0, i, 0))],
        out_specs=(pl.BlockSpec((tr, C), lambda i, s: (i, 0)), pl.BlockSpec((3, tr, C), lambda i, s: (0, i, 0))))
    return pl.pallas_call(body, name=name, grid_spec=spec, out_shape=(S((R, C), F32), S((3, R, C), BF16)),
                          compiler_params=_params(("parallel",)))(slots, g, g, g, g, ra)


def rms_cast(h, g, name, riders=()):
    T, D = h.shape
    tm = _tile(T, 512)

    def body(h_ref, g_ref, n_ref):
        n_ref[...] = _rms(h_ref[...], g_ref[...]).astype(BF16)

    row = pl.BlockSpec((tm, D), lambda i: (i, 0))
    return _call(body, name=name, grid=(T // tm,), out_shape=S((T, D), BF16), in_specs=[row, pl.BlockSpec((1, D), lambda i: (0, 0))],
                 out_specs=row, sem=("parallel",), args=(h, g), riders=riders)


def ffn_up_fwd(n, wgu, name, riders=()):
    T, D = n.shape
    F = wgu.shape[1]
    tr, tn = _tile(T, 512), _tile(F, 256)

    def body(n_ref, wg_ref, wu_ref, dgate_ref, dup_ref, a_ref):
        wg, wu = wg_ref[...], wu_ref[...]
        for r in range(T // tr):
            rows = slice(r * tr, (r + 1) * tr)
            x = n_ref[rows, :]
            gate = _nt(x, wg)
            up = _nt(x, wu)
            s = jax.nn.sigmoid(gate)
            silu = gate * s
            dgate_ref[rows, :] = (up * (s * (1.0 + gate * (1.0 - s)))).astype(BF16)
            dup_ref[rows, :] = silu.astype(BF16)
            a_ref[rows, :] = (silu * up).astype(BF16)

    tile = pl.BlockSpec((T, tn), lambda j: (0, j))
    return _call(
        body, name=name, grid=(F // tn,), out_shape=(S((T, F), BF16),) * 3,
        in_specs=[pl.BlockSpec((T, D), lambda j: (0, 0)),
                  pl.BlockSpec((None, tn, D), lambda j: (0, j, 0)), pl.BlockSpec((None, tn, D), lambda j: (1, j, 0))],
        out_specs=(tile, tile, tile), sem=("parallel",), args=(n, wgu, wgu), riders=riders)


def ffn_down_fwd(a, wd, h, g_next, name, riders=()):
    T, F = a.shape
    D = wd.shape[1]
    tm = _tile(T, 256)

    def body(a_ref, w_ref, h_ref, *rest):
        out = h_ref[...] + 0.5 * _nn(a_ref[...], w_ref[...])
        if g_next is None:
            rest[0][...] = out
        else:
            g_ref, o_ref, n_ref = rest
            o_ref[...] = out
            n_ref[...] = _rms(out, g_ref[...]).astype(BF16)

    row = pl.BlockSpec((tm, D), lambda i: (i, 0))
    more = g_next is not None
    return _call(
        body, name=name, grid=(T // tm,), out_shape=(S((T, D), F32), S((T, D), BF16)) if more else S((T, D), F32),
        in_specs=[pl.BlockSpec((tm, F), lambda i: (i, 0)), pl.BlockSpec((F, D), lambda i: (0, 0)), row]
        + ([pl.BlockSpec((1, D), lambda i: (0, 0))] if more else []),
        out_specs=(row, row) if more else row,
        sem=("parallel",), args=(a, wd, h) + ((g_next,) if more else ()), riders=riders)


def mix_in_fwd(h, g, win, name):
    T, D = h.shape
    N = win.shape[0]
    tm = _tile(T, 256)

    def body(h_ref, g_ref, w_ref, n_ref, p_ref):
        n = _rms(h_ref[...], g_ref[...]).astype(BF16)
        n_ref[...] = n
        p_ref[...] = _nt(n, w_ref[...]).astype(BF16)

    return pl.pallas_call(
        body, name=name, grid=(T // tm,), out_shape=(S((T, D), BF16), S((T, N), BF16)),
        in_specs=[pl.BlockSpec((tm, D), lambda i: (i, 0)), pl.BlockSpec((1, D), lambda i: (0, 0)),
                  pl.BlockSpec((N, D), lambda i: (0, 0))],
        out_specs=(pl.BlockSpec((tm, D), lambda i: (i, 0)), pl.BlockSpec((tm, N), lambda i: (i, 0))),
        compiler_params=_params(("parallel",)),
    )(h, g, win)


def _tri_consts():
    r = lax.broadcasted_iota(jnp.int32, (QB, QB), 0)
    c = lax.broadcasted_iota(jnp.int32, (QB, QB), 1)
    ones = jnp.ones((QB, QB), BF16)
    with_sums = lambda tri: jnp.concatenate([tri.astype(BF16), ones], axis=1)
    return with_sums(r > c), with_sums(r <= c), with_sums(r < c)


def _half_masks():
    lane = lax.broadcasted_iota(jnp.int32, (QB, PAIR), 1)
    row = lax.broadcasted_iota(jnp.int32, (QB, PAIR), 0)
    return lane < HEAD_DIM, lane, row


def sb_attn_fwd(p, after, name, riders=()):
    T = p.shape[0]
    nq = T // QB

    def body(q_ref, k_ref, v_ref, m_ref, o_ref, tot_ref, q_sc, acc_ref, z_sc):
        i = pl.program_id(0)
        lo, lane, row = _half_masks()
        causal = lane < row
        heads, pairs = range(SB_HEADS), range(SB_HEADS // 2)
        for hp in pairs:
            q_sc[hp] = (q_ref[:, hp * PAIR:(hp + 1) * PAIR].astype(F32) * SCALE).astype(BF16)
        m2 = m_ref[...]

        def by_head(ref, j, hp):
            t = ref[pl.ds(pl.multiple_of(j * QB, QB), QB), hp * PAIR:(hp + 1) * PAIR]
            return jnp.concatenate([jnp.where(lo, t, 0), jnp.where(lo, 0, t)], axis=0)

        def scores(j):
            return [_nt(q_sc[hp], by_head(k_ref, j, hp)) for hp in pairs]

        def block(j, diag):
            z2 = [z_sc[hp] for hp in pairs]
            ahead = scores(jnp.maximum(j - 1, 0))
            for hp in pairs:
                z_sc[hp] = ahead[hp]
            vs = [by_head(v_ref, j, hp) for hp in pairs]
            spls = [_softplus_logsig(z2[h // 2][:, (h % 2) * QB:(h % 2 + 1) * QB]) for h in heads]
            sp = [jnp.where(causal, spls[h][0], 0.0) if diag else spls[h][0] for h in heads]
            rr = _tri(sp, m2)
            if diag:
                w = [jnp.where(causal, jnp.exp(spls[h][1] - rr[h][:, :QB]), 0.0).astype(BF16) for h in heads]
            else:
                c = [tot_ref[:, h * QB:(h + 1) * QB] for h in heads]
                w = [jnp.exp(spls[h][1] - (c[h] + rr[h][:, :QB])).astype(BF16) for h in heads]
            pv = [_nn(jnp.concatenate([w[2 * hp], w[2 * hp + 1]], axis=1), vs[hp]) for hp in pairs]
            for hp in pairs:
                acc_ref[hp] = pv[hp] if diag else acc_ref[hp] + pv[hp]
            for h in heads:
                tot_ref[:, h * QB:(h + 1) * QB] = rr[h][:, QB:] if diag else c[h] + rr[h][:, QB:]

        first = scores(i)
        for hp in pairs:
            z_sc[hp] = first[hp]
        block(i, True)

        def step(t, carry):
            block(i - 1 - t, False)
            return carry
        lax.fori_loop(0, i, step, 0)
        for hp in pairs:
            o_ref[:, hp * PAIR:(hp + 1) * PAIR] = acc_ref[hp]

    npair = SB_HEADS // 2
    return _call(
        body, name=name, grid=(nq,), out_shape=(S((T, SB_W), F32), S((T, SB_HEADS * QB), F32)),
        in_specs=[pl.BlockSpec((QB, SB_W), lambda i: (i, 0)), pl.BlockSpec((T, SB_W), lambda i: (0, 1)),
                  pl.BlockSpec((T, SB_W), lambda i: (0, 2)), pl.BlockSpec((QB, 2 * QB), lambda i: (0, 0))],
        out_specs=(pl.BlockSpec((QB, SB_W), lambda i: (i, 0)), pl.BlockSpec((QB, SB_HEADS * QB), lambda i: (i, 0))),
        scratch=[pltpu.VMEM((npair, QB, PAIR), BF16), pltpu.VMEM((npair, QB, PAIR), F32), pltpu.VMEM((npair, QB, 2 * QB), F32)],
        sem=("arbitrary",), args=(p, p, p, after), riders=riders,
        marks=((11 * nq) // 16, (13 * nq) // 16))


def sb_attn_bwd(p, do, tot, upto, before, name, riders=()):
    T = p.shape[0]
    nq = T // QB

    def body(q_ref, k_ref, v_ref, do_ref, tot_ref, mp_ref, mg_ref, dq_ref, dk_ref, dv_ref,
             q_sc, d_sc, qd_sc, pg_sc, dq_acc, dk_acc, dv_acc, zd_sc):
        i = pl.program_id(0)
        lo, lane, row = _half_masks()
        causal = lane < row
        heads, pairs = range(SB_HEADS), range(SB_HEADS // 2)

        def by_head(t):
            return jnp.concatenate([jnp.where(lo, t, 0), jnp.where(lo, 0, t)], axis=0)

        for hp in pairs:
            q2 = (q_ref[:, hp * PAIR:(hp + 1) * PAIR].astype(F32) * SCALE).astype(BF16)
            d2 = do_ref[:, hp * PAIR:(hp + 1) * PAIR].astype(BF16)
            q_sc[hp] = q2
            d_sc[hp] = d2
            qd_sc[hp] = by_head(q2)
            qd_sc[SB_HEADS // 2 + hp] = by_head(d2)
        mp, mg = mp_ref[...], mg_ref[...]

        @pl.when(i == 0)
        def _():
            dk_acc[...] = jnp.zeros_like(dk_acc)
            dv_acc[...] = jnp.zeros_like(dv_acc)
        pg_sc[...] = jnp.zeros_like(pg_sc)
        dq_acc[...] = jnp.zeros_like(dq_acc)

        def rows(ref, j, hp):
            return ref[pl.ds(pl.multiple_of(j * QB, QB), QB), hp * PAIR:(hp + 1) * PAIR]

        def products(j):
            return ([_nt(q_sc[hp], by_head(rows(k_ref, j, hp))) for hp in pairs]
                    + [_nt(d_sc[hp], by_head(rows(v_ref, j, hp))) for hp in pairs])

        def block(j, diag):
            r0 = pl.multiple_of(j * QB, QB)
            half = lambda t, h: t[:, (h % 2) * QB:(h % 2 + 1) * QB]
            z = [half(zd_sc[h // 2], h) for h in heads]
            dw = [half(zd_sc[SB_HEADS // 2 + h // 2], h) for h in heads]
            if not diag:
                ahead = products(j + 1)
                for hp in range(SB_HEADS):
                    zd_sc[hp] = ahead[hp]
            ks = [by_head(rows(k_ref, j, hp)) for hp in pairs]
            spls = [_softplus_logsig(z[h]) for h in heads]
            sp = [jnp.where(causal, spls[h][0], 0.0) if diag else spls[h][0] for h in heads]
            rr = _tri(sp, mp)
            pc = [pg_sc[2 * h] for h in heads]
            w = [jnp.exp(spls[h][1] - (tot_ref[:, h * QB:(h + 1) * QB] - (pc[h] + rr[h][:, :QB]))) for h in heads]
            if diag:
                w = [jnp.where(causal, w[h], 0.0) for h in heads]
            gg = [dw[h] * w[h] for h in heads]
            rg = _tri(gg, mg)
            gc = [pg_sc[2 * h + 1] for h in heads]
            dz = [gg[h] - (gg[h] + gc[h] + rg[h][:, :QB]) * jnp.exp(spls[h][1]) for h in heads]
            if diag:
                dz = [jnp.where(causal, dz[h], 0.0) for h in heads]
            dzb = [dz[h].astype(BF16) for h in heads]
            wb = [w[h].astype(BF16) for h in heads]
            both = lambda t, hp, axis: jnp.concatenate([t[2 * hp], t[2 * hp + 1]], axis=axis)
            dq = [_nn(both(dzb, hp, 1), ks[hp]) for hp in pairs]
            dk = [_tn(both(dzb, hp, 0), qd_sc[hp]) for hp in pairs]
            dv = [_tn(both(wb, hp, 0), qd_sc[SB_HEADS // 2 + hp]) for hp in pairs]
            for h in heads:
                if not diag:
                    pg_sc[2 * h] = pc[h] + rr[h][:, QB:]
                    pg_sc[2 * h + 1] = gc[h] + rg[h][:, QB:]
            for hp in pairs:
                dq_acc[hp] += dq[hp]
                dk_acc[pl.ds(r0, QB), hp * PAIR:(hp + 1) * PAIR] += dk[hp]
                dv_acc[pl.ds(r0, QB), hp * PAIR:(hp + 1) * PAIR] += dv[hp]

        first = products(0)
        for hp in range(SB_HEADS):
            zd_sc[hp] = first[hp]

        def step(t, carry):
            block(t, False)
            return carry
        lax.fori_loop(0, i, step, 0)
        block(i, True)
        for hp in pairs:
            dq_ref[:, hp * PAIR:(hp + 1) * PAIR] = (dq_acc[hp] * SCALE).astype(BF16)

        @pl.when(i == nq - 1)
        def _():
            dk_ref[...] = dk_acc[...].astype(BF16)
            dv_ref[...] = dv_acc[...].astype(BF16)

    qtile = pl.BlockSpec((QB, SB_W), lambda i: (i, 0))
    whole = pl.BlockSpec((T, SB_W), lambda i: (0, 0))
    const = pl.BlockSpec((QB, 2 * QB), lambda i: (0, 0))
    return _call(
        body, name=name, grid=(nq,), out_shape=(S((T, SB_W), BF16),) * 3,
        in_specs=[qtile, pl.BlockSpec((T, SB_W), lambda i: (0, 1)), pl.BlockSpec((T, SB_W), lambda i: (0, 2)), qtile,
                  pl.BlockSpec((QB, SB_HEADS * QB), lambda i: (i, 0)), const, const],
        out_specs=(qtile, whole, whole),
        scratch=[pltpu.VMEM((SB_HEADS // 2, QB, PAIR), BF16), pltpu.VMEM((SB_HEADS // 2, QB, PAIR), BF16),
                 pltpu.VMEM((SB_HEADS, 2 * QB, PAIR), BF16),
                 pltpu.VMEM((2 * SB_HEADS, QB, QB), F32), pltpu.VMEM((SB_HEADS // 2, QB, PAIR), F32),
                 pltpu.VMEM((T, SB_W), F32), pltpu.VMEM((T, SB_W), F32), pltpu.VMEM((SB_HEADS, QB, 2 * QB), F32)],
        sem=("arbitrary",), args=(p, p, p, do, tot, upto, before), riders=riders)


def _t5_buckets():
    a = lax.broadcasted_iota(jnp.int32, (QB, QB), 0)
    c = lax.broadcasted_iota(jnp.int32, (QB, QB), 1)

    def bucket(dist):
        dist = jnp.maximum(dist, 0)
        max_exact = N_BUCKETS // 2
        d = jnp.maximum(dist, 1).astype(F32)
        large = max_exact + (jnp.log(d / max_exact) / math.log(MAX_DISTANCE / max_exact)
                             * (N_BUCKETS - max_exact)).astype(jnp.int32)
        large = jnp.minimum(large, N_BUCKETS - 1)
        return jnp.where(dist < max_exact, dist, large)

    return bucket(QB + a - c), bucket(a - c)


def _swa_common(i, kp_ref, kc_ref, vp_ref, vc_ref, bp_ref, bc_ref, rb_ref, bias_ref):
    lo, lane, row = _half_masks()

    @pl.when(i == 0)
    def _():
        for blk, b_ref in enumerate((bp_ref, bc_ref)):
            bk = b_ref[...]
            for h in range(8):
                acc = jnp.zeros((QB, QB), F32)
                for b in range(N_BUCKETS):
                    acc = jnp.where(bk == b, rb_ref[b, h], acc)
                bias_ref[h, blk] = acc

    band = [(lane > row) & (i > 0), lane <= row]

    def stacks(ref):
        t = ref[...].astype(F32)
        sw = pltpu.roll(t, HEAD_DIM, 1)
        return [jnp.concatenate([jnp.where(lo, t, 0.0), jnp.where(lo, 0.0, sw)], axis=0).astype(BF16),
                jnp.concatenate([jnp.where(lo, sw, 0.0), jnp.where(lo, 0.0, t)], axis=0).astype(BF16)]

    ks = [stacks(kp_ref), stacks(kc_ref)]
    vs = [stacks(vp_ref), stacks(vc_ref)]
    return lo, band, ks, vs


def _lane_half(t, h):
    return t[:, (h % 2) * QB:(h % 2 + 1) * QB]


def swa_fwd(p, sinks, rel_bias, bprev, bcur, name, riders=()):
    T = p.shape[0]
    nq = T // QB
    kcol, vcol = (3 * SB_W + SWA_W) // KV_W, (3 * SB_W + SWA_W) // KV_W + 1

    def body(q_ref, kp_ref, kc_ref, vp_ref, vc_ref, bp_ref, bc_ref, sink_ref, rb_ref, o_ref, lse_ref, bias_ref):
        i = pl.program_id(0)
        lo, band, ks, vs = _swa_common(i, kp_ref, kc_ref, vp_ref, vc_ref, bp_ref, bc_ref, rb_ref, bias_ref)
        heads, pairs, blocks = range(8), range(4), range(2)
        rowmax = lambda t: jnp.max(t, axis=1, keepdims=True)
        rowsum = lambda t: jnp.sum(t, axis=1, keepdims=True)
        q2 = [q_ref[:, g * PAIR:(g + 1) * PAIR] for g in pairs]
        s2 = [[_nt(q2[g], ks[b][g // 2]) for b in blocks] for g in pairs]
        sc = [[jnp.where(band[b], _lane_half(s2[h // 2][b], h) * SCALE + bias_ref[h, b], NEG_INF) for b in blocks] for h in heads]
        sink = [sink_ref[0, h] for h in heads]
        m = [jnp.maximum(jnp.maximum(rowmax(sc[h][0]), rowmax(sc[h][1])), sink[h]) for h in heads]
        e = [[jnp.exp(sc[h][b] - m[h]) for b in blocks] for h in heads]
        den = [rowsum(e[h][0]) + rowsum(e[h][1]) + jnp.exp(sink[h] - m[h]) for h in heads]
        pb = [[(e[h][b] / den[h]).astype(BF16) for b in blocks] for h in heads]
        for g in pairs:
            both = lambda b: jnp.concatenate([pb[2 * g][b], pb[2 * g + 1][b]], axis=1)
            o_ref[:, g * PAIR:(g + 1) * PAIR] = _nn(both(0), vs[0][g // 2]) + _nn(both(1), vs[1][g // 2])
        for h in heads:
            lse_ref[:, h * QB:(h + 1) * QB] = jnp.broadcast_to(m[h] + jnp.log(den[h]), (QB, QB))

    kv = lambda col, prev: pl.BlockSpec((QB, KV_W), (lambda i: (jnp.maximum(i - 1, 0), col)) if prev else (lambda i: (i, col)))
    full = pl.BlockSpec((QB, QB), lambda i: (0, 0))
    smem = pl.BlockSpec(memory_space=pltpu.SMEM)
    return _call(
        body, name=name, grid=(nq,), out_shape=(S((T, SWA_W), F32), S((T, 8 * QB), F32)),
        in_specs=[pl.BlockSpec((QB, SWA_W), lambda i: (i, 3)), kv(kcol, True), kv(kcol, False), kv(vcol, True), kv(vcol, False),
                  full, full, smem, smem],
        out_specs=(pl.BlockSpec((QB, SWA_W), lambda i: (i, 0)), pl.BlockSpec((QB, 8 * QB), lambda i: (i, 0))),
        scratch=[pltpu.VMEM((8, 2, QB, QB), F32)],
        sem=("arbitrary",), args=(p, p, p, p, p, bprev, bcur, sinks, rel_bias), riders=riders)


def swa_bwd(p, do, lse, sinks, rel_bias, bprev, bcur, name, riders=()):
    T = p.shape[0]
    nq = T // QB
    kcol, vcol = (3 * SB_W + SWA_W) // KV_W, (3 * SB_W + SWA_W) // KV_W + 1

    def body(q_ref, kp_ref, kc_ref, vp_ref, vc_ref, do_ref, lse_ref, bp_ref, bc_ref, sink_ref, rb_ref,
             dq_ref, dk_ref, dv_ref, dsink_ref, dsc_ref, bias_ref, dk_acc, dv_acc):
        i = pl.program_id(0)
        lo, band, ks, vs = _swa_common(i, kp_ref, kc_ref, vp_ref, vc_ref, bp_ref, bc_ref, rb_ref, bias_ref)

        @pl.when(i == 0)
        def _():
            dk_acc[...] = jnp.zeros_like(dk_acc)
            dv_acc[...] = jnp.zeros_like(dv_acc)
            dsc_ref[...] = jnp.zeros_like(dsc_ref)
            dsink_ref[...] = jnp.zeros_like(dsink_ref)

        heads, pairs, blocks = range(8), range(4), range(2)
        rowsum = lambda t: jnp.sum(t, axis=1, keepdims=True)
        by_head = lambda t: jnp.concatenate([jnp.where(lo, t, 0), jnp.where(lo, 0, t)], axis=0)
        q2 = [q_ref[:, g * PAIR:(g + 1) * PAIR] for g in pairs]
        d2 = [do_ref[:, g * PAIR:(g + 1) * PAIR].astype(BF16) for g in pairs]
        qs = [by_head(q2[g]) for g in pairs]
        dos = [by_head(d2[g]) for g in pairs]
        s2 = [[_nt(q2[g], ks[b][g // 2]) for b in blocks] for g in pairs]
        dp2 = [[_nt(d2[g], vs[b][g // 2]) for b in blocks] for g in pairs]
        lse_h = [lse_ref[:, h * QB:(h + 1) * QB] for h in heads]
        sink = [sink_ref[0, h] for h in heads]
        pr = [[jnp.exp(jnp.where(band[b], _lane_half(s2[h // 2][b], h) * SCALE + bias_ref[h, b], NEG_INF) - lse_h[h])
               for b in blocks] for h in heads]
        dp = [[_lane_half(dp2[h // 2][b], h) for b in blocks] for h in heads]
        delta = [rowsum(pr[h][0] * dp[h][0]) + rowsum(pr[h][1] * dp[h][1]) for h in heads]
        lane1 = lax.broadcasted_iota(jnp.int32, (1, QB), 1)
        dsink = jnp.zeros((1, QB), F32)
        for h in heads:
            dsink = dsink + jnp.where(lane1 == h, -jnp.sum(jnp.exp(sink[h] - lse_h[h][:, :1]) * delta[h]), 0.0)
        dsink_ref[...] += dsink
        dsc = [[pr[h][b] * (dp[h][b] - delta[h]) for b in blocks] for h in heads]
        for h in heads:
            for b in blocks:
                dsc_ref[h, b] += dsc[h][b]
        dzb = [[(dsc[h][b] * SCALE).astype(BF16) for b in blocks] for h in heads]
        prb = [[pr[h][b].astype(BF16) for b in blocks] for h in heads]
        pair_of = lambda t, g, b, axis: jnp.concatenate([t[2 * g][b], t[2 * g + 1][b]], axis=axis)
        for g in pairs:
            dq = _nn(pair_of(dzb, g, 0, 1), ks[0][g // 2]) + _nn(pair_of(dzb, g, 1, 1), ks[1][g // 2])
            dq_ref[:, g * PAIR:(g + 1) * PAIR] = dq.astype(BF16)

        def key_grad(t, other, b):
            per_kv = [_tn(pair_of(t, 2 * kh, b, 0), other[2 * kh]) + _tn(pair_of(t, 2 * kh + 1, b, 0), other[2 * kh + 1]) for kh in range(2)]
            both = [s + pltpu.roll(s, HEAD_DIM, 1) for s in per_kv]
            return jnp.where(lo, both[0], both[1])

        rp = pl.multiple_of(jnp.maximum(i - 1, 0) * QB, QB)
        rc = pl.multiple_of(i * QB, QB)
        dk_acc[pl.ds(rp, QB), :] += key_grad(dzb, qs, 0)
        dv_acc[pl.ds(rp, QB), :] += key_grad(prb, dos, 0)
        dk_acc[pl.ds(rc, QB), :] += key_grad(dzb, qs, 1)
        dv_acc[pl.ds(rc, QB), :] += key_grad(prb, dos, 1)

        @pl.when(i == nq - 1)
        def _():
            dk_ref[...] = dk_acc[...].astype(BF16)
            dv_ref[...] = dv_acc[...].astype(BF16)

    kv = lambda col, prev: pl.BlockSpec((QB, KV_W), (lambda i: (jnp.maximum(i - 1, 0), col)) if prev else (lambda i: (i, col)))
    full = pl.BlockSpec((QB, QB), lambda i: (0, 0))
    smem = pl.BlockSpec(memory_space=pltpu.SMEM)
    whole = lambda shape: pl.BlockSpec(shape, lambda i: (0,) * len(shape))
    return _call(
        body, name=name, grid=(nq,),
        out_shape=(S((T, SWA_W), BF16), S((T, KV_W), BF16), S((T, KV_W), BF16), S((1, QB), F32), S((8, 2, QB, QB), F32)),
        in_specs=[pl.BlockSpec((QB, SWA_W), lambda i: (i, 3)), kv(kcol, True), kv(kcol, False), kv(vcol, True), kv(vcol, False),
                  pl.BlockSpec((QB, SWA_W), lambda i: (i, 0)), pl.BlockSpec((QB, 8 * QB), lambda i: (i, 0)),
                  full, full, smem, smem],
        out_specs=(pl.BlockSpec((QB, SWA_W), lambda i: (i, 0)), whole((T, KV_W)), whole((T, KV_W)), whole((1, QB)),
                   whole((8, 2, QB, QB))),
        scratch=[pltpu.VMEM((8, 2, QB, QB), F32), pltpu.VMEM((T, KV_W), F32), pltpu.VMEM((T, KV_W), F32)],
        sem=("arbitrary",), args=(p, p, p, p, p, do, lse, bprev, bcur, sinks, rel_bias), riders=riders)


def mix_out_fwd(o_sb, o_sw, g_sb, g_sw, wout, h, g_next, name, riders=()):
    T, D = h.shape
    M = SB_W + SWA_W
    tm = _tile(T, 256)

    def body(a_ref, b_ref, ga_ref, gb_ref, w_ref, h_ref, gn_ref, mx_ref, o_ref, n_ref):
        mx_ref[:, :SB_W] = _rms(a_ref[...], ga_ref[...]).astype(BF16)
        mx_ref[:, SB_W:] = _rms(b_ref[...], gb_ref[...]).astype(BF16)
        out = h_ref[...] + _nn(mx_ref[...], w_ref[...])
        o_ref[...] = out
        n_ref[...] = _rms(out, gn_ref[...]).astype(BF16)

    row = lambda n: pl.BlockSpec((tm, n), lambda i: (i, 0))
    vec = lambda n: pl.BlockSpec((1, n), lambda i: (0, 0))
    return _call(
        body, name=name, grid=(T // tm,), out_shape=(S((T, M), BF16), S((T, D), F32), S((T, D), BF16)),
        in_specs=[row(SB_W), row(SWA_W), vec(SB_W), vec(SWA_W), pl.BlockSpec((M, D), lambda i: (0, 0)), row(D), vec(D)],
        out_specs=(row(M), row(D), row(D)),
        sem=("parallel",), args=(o_sb, o_sw, g_sb, g_sw, wout, h, g_next), riders=riders)


def loss_head(h, g, target, name):
    T, D = h.shape
    tm = _tile(T, 256)

    def body(h_ref, g_ref, t_ref, loss_ref, dh_ref, dhb_ref, dg_ref):
        @pl.when(pl.program_id(0) == 0)
        def _():
            loss_ref[...] = jnp.zeros_like(loss_ref)
            dg_ref[...] = jnp.zeros_like(dg_ref)
        x = h_ref[...]
        err = _rms(x, g_ref[...]) - t_ref[...]
        loss_ref[...] += jnp.full((1, QB), 0.5 * jnp.sum(jnp.mean(err * err, axis=-1)), F32)
        dx, dg = _rms_bwd(err / D, x, g_ref[...])
        dh_ref[...] = dx
        dhb_ref[...] = dx.astype(BF16)
        dg_ref[...] += dg

    row = pl.BlockSpec((tm, D), lambda i: (i, 0))
    vec = pl.BlockSpec((1, D), lambda i: (0, 0))
    return pl.pallas_call(
        body, name=name, grid=(T // tm,), out_shape=(S((1, QB), F32), S((T, D), F32), S((T, D), BF16), S((1, D), F32)),
        in_specs=[row, vec, row], out_specs=(pl.BlockSpec((1, QB), lambda i: (0, 0)), row, row, vec),
        compiler_params=_params(("arbitrary",)),
    )(h, g, target)


def ffn_down_bwd(dhb, wd, gate, up, name, riders=()):
    T, D = dhb.shape
    F = wd.shape[0]
    tr, tn = _tile(T, 512), _tile(F, 256)

    def body(d_ref, w_ref, g_ref, u_ref, o_ref):
        w = w_ref[...]
        for r in range(T // tr):
            rows = slice(r * tr, (r + 1) * tr)
            da = 0.5 * _nt(d_ref[rows, :], w)
            o_ref[0, rows, :] = (da * g_ref[rows, :].astype(F32)).astype(BF16)
            o_ref[1, rows, :] = (da * u_ref[rows, :].astype(F32)).astype(BF16)

    tile = pl.BlockSpec((T, tn), lambda j: (0, j))
    return _call(
        body, name=name, grid=(F // tn,), out_shape=S((2, T, F), BF16),
        in_specs=[pl.BlockSpec((T, D), lambda j: (0, 0)), pl.BlockSpec((tn, D), lambda j: (j, 0)), tile, tile],
        out_specs=pl.BlockSpec((2, T, tn), lambda j: (0, 0, j)),
        sem=("parallel",), args=(dhb, wd, gate, up), riders=riders)


def tn_matmul(xs, y, alpha, name, riders=()):
    B, T, N = xs.shape
    D = y.shape[1]
    tn = _tile(N, 256)

    def body(x_ref, y_ref, o_ref, ob_ref):
        o = alpha * _tn(x_ref[...], y_ref[...])
        o_ref[...] = o
        ob_ref[...] = o.astype(BF16)

    tile = pl.BlockSpec((None, tn, D), lambda s, j: (s, j, 0))
    return _call(
        body, name=name, grid=(B, N // tn), out_shape=(S((B, N, D), F32), S((B, N, D), BF16)),
        in_specs=[pl.BlockSpec((None, T, tn), lambda s, j: (s, 0, j)), pl.BlockSpec((T, D), lambda s, j: (0, 0))],
        out_specs=(tile, tile), sem=("parallel", "parallel"), args=(xs, y), riders=riders)


def nn_rms_bwd(xs, ws, h_in, g, dh, name, riders=()):
    B, T, K = xs.shape
    D = ws.shape[2]
    tm = _tile(T, 256)

    def body(x_ref, w_ref, h_ref, g_ref, d_ref, o_ref, ob_ref, dg_ref):
        @pl.when(pl.program_id(0) == 0)
        def _():
            dg_ref[...] = jnp.zeros_like(dg_ref)
        dn = _nn(x_ref[0], w_ref[0])
        for s in range(1, B):
            dn = dn + _nn(x_ref[s], w_ref[s])
        dx, dg = _rms_bwd(dn, h_ref[...], g_ref[...])
        out = d_ref[...] + dx
        o_ref[...] = out
        ob_ref[...] = out.astype(BF16)
        dg_ref[...] += dg

    row = pl.BlockSpec((tm, D), lambda i: (i, 0))
    vec = pl.BlockSpec((1, D), lambda i: (0, 0))
    return _call(
        body, name=name, grid=(T // tm,), out_shape=(S((T, D), F32), S((T, D), BF16), S((1, D), F32)),
        in_specs=[pl.BlockSpec((B, tm, K), lambda i: (0, i, 0)), pl.BlockSpec((B, K, D), lambda i: (0, 0, 0)), row, vec, row],
        out_specs=(row, row, vec),
        sem=("arbitrary",), args=(xs, ws, h_in, g, dh), riders=riders)


def mix_out_bwd(dhb, wout, o_sb, o_sw, g_sb, g_sw, name):
    T, D = dhb.shape
    tm = _tile(T, 256)

    def body(d_ref, w_ref, a_ref, b_ref, ga_ref, gb_ref, da_ref, db_ref, dga_ref, dgb_ref):
        @pl.when(pl.program_id(0) == 0)
        def _():
            dga_ref[...] = jnp.zeros_like(dga_ref)
            dgb_ref[...] = jnp.zeros_like(dgb_ref)
        dm = _nt(d_ref[...], w_ref[...])
        dxa, dga = _rms_bwd(dm[:, :SB_W], a_ref[...], ga_ref[...])
        dxb, dgb = _rms_bwd(dm[:, SB_W:], b_ref[...], gb_ref[...])
        da_ref[...] = dxa
        db_ref[...] = dxb
        dga_ref[...] += dga
        dgb_ref[...] += dgb

    row = lambda n: pl.BlockSpec((tm, n), lambda i: (i, 0))
    vec = lambda n: pl.BlockSpec((1, n), lambda i: (0, 0))
    return pl.pallas_call(
        body, name=name, grid=(T // tm,),
        out_shape=(S((T, SB_W), F32), S((T, SWA_W), F32), S((1, SB_W), F32), S((1, SWA_W), F32)),
        in_specs=[row(D), pl.BlockSpec((SB_W + SWA_W, D), lambda i: (0, 0)), row(SB_W), row(SWA_W), vec(SB_W), vec(SWA_W)],
        out_specs=(row(SB_W), row(SWA_W), vec(SB_W), vec(SWA_W)),
        compiler_params=_params(("arbitrary",)),
    )(dhb, wout, o_sb, o_sw, g_sb, g_sw)


def rel_bias_grad(dscs, bprev, bcur, name):
    n = len(dscs)

    def body(*refs):
        bp_ref, bc_ref, o_ref = refs[n], refs[n + 1], refs[n + 2]
        bks = [bp_ref[...], bc_ref[...]]
        row = lax.broadcasted_iota(jnp.int32, (N_BUCKETS, QB), 0)
        lane = lax.broadcasted_iota(jnp.int32, (N_BUCKETS, QB), 1)
        out = jnp.zeros((N_BUCKETS, QB), F32)
        for h in range(8):
            tot = [sum(refs[l][h, b] for l in range(n)) for b in range(2)]
            for b in range(N_BUCKETS):
                val = jnp.sum(jnp.where(bks[0] == b, tot[0], 0.0)) + jnp.sum(jnp.where(bks[1] == b, tot[1], 0.0))
                out = jnp.where((row == b) & (lane == h), val, out)
        o_ref[...] = out

    return pl.pallas_call(body, name=name, out_shape=S((N_BUCKETS, QB), F32), compiler_params=_params())(*dscs, bprev, bcur)


def _adamw(w, g, m, v):
    m = ADAM_B1 * m + (1.0 - ADAM_B1) * g
    v = ADAM_B2 * v + (1.0 - ADAM_B2) * (g * g)
    m_hat = m / (1.0 - ADAM_B1 ** ADAM_STEP)
    v_hat = v / (1.0 - ADAM_B2 ** ADAM_STEP)
    delta = -ADAM_LR * (m_hat / (jnp.sqrt(v_hat) + ADAM_EPS) + ADAM_WD * w)
    return delta, m, v


def adamw_scattered(w, m, v, owns, others, name, riders=()):
    L, R, C = w.shape
    tr = _rows_tile(R, 176)

    def body(w_ref, m_ref, v_ref, *rest):
        own_refs, other_refs = rest[:L], rest[L:2 * L]
        g_ref, d_ref, mo_ref, vo_ref = rest[2 * L:]
        layer = pl.program_id(0)

        def grad(k):
            o = other_refs[k]
            return own_refs[k][...] + o[0].astype(F32) + o[1].astype(F32) + o[2].astype(F32)

        g = grad(0)
        for k in range(1, L):
            g = jnp.where(layer == k, grad(k), g)
        d, mn, vn = _adamw(w_ref[...], g, m_ref[...], v_ref[...])
        g_ref[...] = g
        d_ref[...] = d
        mo_ref[...] = mn
        vo_ref[...] = vn

    tile = pl.BlockSpec((None, tr, C), lambda l, i: (l, i, 0))
    return _call(
        body, name=name, grid=(L, R // tr), out_shape=(S((L, R, C), F32),) * 4,
        in_specs=[tile] * 3 + [pl.BlockSpec((tr, C), lambda l, i: (i, 0))] * L + [pl.BlockSpec((3, tr, C), lambda l, i: (0, i, 0))] * L,
        out_specs=(tile,) * 4, sem=("parallel", "parallel"), args=(w, m, v, *owns, *others), riders=riders)


def adamw_small(w, gs, m, v, name):
    R, C = w.shape

    def body(w_ref, g_ref, m_ref, v_ref, go_ref, d_ref, mo_ref, vo_ref):
        g = g_ref[0]
        for k in range(1, N_DEV):
            g = g + g_ref[k]
        d, mn, vn = _adamw(w_ref[...], g, m_ref[...], v_ref[...])
        go_ref[...] = g
        d_ref[...] = d
        mo_ref[...] = mn
        vo_ref[...] = vn

    return pl.pallas_call(body, name=name, out_shape=(S((R, C), F32),) * 4, compiler_params=_params())(w, gs, m, v)


def kernel(x, norm_ffn1, w_ffn1_gu, w_ffn1_down, norm_mix, w_in, sinks, norm_out_sb, norm_out_swa, w_out, norm_ffn2, w_ffn2_gu, w_ffn2_down, rel_bias, norm_final, loss_target, m_norm_ffn1, m_w_ffn1_gu, m_w_ffn1_down, m_norm_mix, m_w_in, m_sinks, m_norm_out_sb, m_norm_out_swa, m_w_out, m_norm_ffn2, m_w_ffn2_gu, m_w_ffn2_down, m_rel_bias, m_norm_final, v_norm_ffn1, v_w_ffn1_gu, v_w_ffn1_down, v_norm_mix, v_w_in, v_sinks, v_norm_out_sb, v_norm_out_swa, v_w_out, v_norm_ffn2, v_w_ffn2_gu, v_w_ffn2_down, v_rel_bias, v_norm_final):
    L = norm_ffn1.shape[0]
    T, D = x.shape[1], x.shape[2]
    F = w_ffn1_down.shape[1] * N_DEV
    h = x.reshape(T, D)
    target = loss_target.reshape(T, D)
    after, upto, before = _tri_consts()
    bprev, bcur = _t5_buckets()

    local = {}
    for l in range(L):
        local[f"gu1_{l}"] = w_ffn1_gu[l].T.astype(BF16)
        local[f"d1_{l}"] = w_ffn1_down[l].astype(BF16)
        local[f"in_{l}"] = w_in[l].T.astype(BF16)
        local[f"out_{l}"] = w_out[l].astype(BF16)
        local[f"gu2_{l}"] = w_ffn2_gu[l].T.astype(BF16)
        local[f"d2_{l}"] = w_ffn2_down[l].astype(BF16)
    full, partial = {}, {}
    grads, chip_sum, recv_b = {}, {}, {}

    def run(fn, *args, ag=(), rs1=(), rs2=()):
        halves = lambda names: [n if isinstance(n, tuple) else (n, None) for n in names]
        ag, rs2 = [(n, k) for n, k in halves(ag) if n in local], halves(rs2)
        rows = lambda k, total: None if k is None else (k * (total // 2), total // 2)

        def second(n, k):
            sb = chip_sum[n][1]
            return scatter_second(sb, rows(k, sb.shape[1]), recv_b.get(n))

        riders = ([gather(local[n], rows(k, local[n].shape[0]), partial.get(n)) for n, k in ag]
                  + [scatter_first(grads[n][1]) for n in rs1] + [second(n, k) for n, k in rs2])
        if not riders:
            return fn(*args)
        outs, per = fn(*args, riders=riders)
        per = [p[0] for p in per]
        for n, k in ag:
            buf = per.pop(0)
            if k == 0:
                partial[n] = buf
            else:
                full[n] = buf.reshape(N_DEV * buf.shape[1], D)
        for n in rs1:
            chip_sum[n] = scatter_add(grads[n][0], per.pop(0), f"rs_add_{n}")
        for n, _ in rs2:
            recv_b[n] = per.pop(0)
        return outs

    gu = lambda n: full[n].reshape(2, F, D)
    slots = lambda pair: tuple(t.reshape(N_DEV, -1, D) for t in pair)
    vec = lambda a: a.reshape(1, -1)

    saved = []
    n_next = run(rms_cast, h, vec(norm_ffn1[0]), "rms_first", ag=("gu1_0",))
    for l in range(L):
        nx = l + 1
        s = {"h0": h, "n1": n_next}
        s["gate1"], s["up1"], s["a1"] = run(ffn_up_fwd, s["n1"], gu(f"gu1_{l}"), f"ffn1_up{l}",
                                            ag=(f"d1_{l}",) + ((("in_0", 0),) if l == 0 else ()))
        h = run(ffn_down_fwd, s["a1"], full[f"d1_{l}"], h, None, f"ffn1_down{l}", ag=((f"in_{l}", 1),))
        s["h1"] = h
        s["n2"], s["p"] = mix_in_fwd(h, vec(norm_mix[l]), full[f"in_{l}"], f"mix_in{l}")
        s["o_sb"], s["tot"] = run(sb_attn_fwd, s["p"], after, f"sb_fwd{l}", ag=(f"out_{l}", f"gu2_{l}", f"d2_{l}"))
        s["o_sw"], s["lse"] = run(swa_fwd, s["p"], vec(sinks[l]), rel_bias, bprev, bcur, f"swa_fwd{l}", ag=((f"gu1_{nx}", 0),))
        s["mixed"], h, s["n3"] = run(mix_out_fwd, s["o_sb"], s["o_sw"], vec(norm_out_sb[l]), vec(norm_out_swa[l]),
                                     full[f"out_{l}"], h, vec(norm_ffn2[l]), f"mix_out{l}")
        s["h2"] = h
        s["gate2"], s["up2"], s["a2"] = run(ffn_up_fwd, s["n3"], gu(f"gu2_{l}"), f"ffn2_up{l}", ag=((f"gu1_{nx}", 1),))
        if nx < L:
            h, n_next = run(ffn_down_fwd, s["a2"], full[f"d2_{l}"], h, vec(norm_ffn1[nx]), f"ffn2_down{l}", ag=((f"in_{nx}", 0),))
        else:
            h = run(ffn_down_fwd, s["a2"], full[f"d2_{l}"], h, None, f"ffn2_down{l}")
        saved.append(s)

    loss_part, dh, dhb, dg_final = loss_head(h, vec(norm_final), target, "loss_head")

    small = {k: [None] * L for k in ("ffn1", "mix", "sinks", "osb", "osw", "ffn2", "dsc")}
    for l in reversed(range(L)):
        s = saved[l]

        def ffn_bwd(dh, dhb, tag, gate, up, a, n, h_in, g, r_down, r_dwgu, r_up):
            gu_n, d_n = f"gu{tag}_{l}", f"d{tag}_{l}"
            dgu = run(ffn_down_bwd, dhb, full[d_n], gate, up, f"ffn{tag}_down_bwd{l}", **r_down)
            grads[gu_n] = slots(run(tn_matmul, dgu, n, 1.0, f"ffn{tag}_dwgu{l}", **r_dwgu))
            grads[d_n] = slots(run(tn_matmul, a[None], dhb, 0.5, f"ffn{tag}_dwd{l}", rs1=(gu_n,)))
            return run(nn_rms_bwd, dgu, gu(gu_n), h_in, g, dh, f"ffn{tag}_up_bwd{l}", rs1=(d_n,), **r_up)

        later = l + 1 < L
        dh, dhb, small["ffn2"][l] = ffn_bwd(dh, dhb, 2, s["gate2"], s["up2"], s["a2"], s["n3"], s["h2"], vec(norm_ffn2[l]),
                                            {}, dict(rs2=(f"d1_{l + 1}",) if later else ()),
                                            dict(rs2=((f"gu1_{l + 1}", 1),) if later else ()))
        do_sb, do_sw, small["osb"][l], small["osw"][l] = mix_out_bwd(
            dhb, full[f"out_{l}"], s["o_sb"], s["o_sw"], vec(norm_out_sb[l]), vec(norm_out_swa[l]), f"mix_out_bwd{l}")
        grads[f"out_{l}"] = slots(tn_matmul(s["mixed"][None], dhb, 1.0, f"dwout{l}"))
        dq_sb, dk_sb, dv_sb = run(sb_attn_bwd, s["p"], do_sb, s["tot"], upto, before, f"sb_bwd{l}",
                                  rs2=(f"gu2_{l}", f"d2_{l}"), rs1=(f"out_{l}",))
        dq_sw, dk_sw, dv_sw, small["sinks"][l], small["dsc"][l] = run(
            swa_bwd, s["p"], do_sw, s["lse"], vec(sinks[l]), rel_bias, bprev, bcur, f"swa_bwd{l}", rs2=(f"out_{l}",))
        dp = jnp.concatenate([dq_sb, dk_sb, dv_sb, dq_sw, dk_sw, dv_sw], axis=1)
        dh, dhb, small["mix"][l] = nn_rms_bwd(dp[None], full[f"in_{l}"][None], s["h1"], vec(norm_mix[l]), dh, f"mix_in_bwd{l}")
        grads[f"in_{l}"] = slots(tn_matmul(dp[None], s["n2"], 1.0, f"dwin{l}"))
        dh, dhb, small["ffn1"][l] = ffn_bwd(dh, dhb, 1, s["gate1"], s["up1"], s["a1"], s["n1"], s["h0"], vec(norm_ffn1[l]),
                                            dict(rs1=(f"in_{l}",)), dict(rs2=(f"in_{l}",)), dict(rs2=((f"gu1_{l}", 0),)))

    grad_x = dh.reshape(x.shape)

    upd = {}
    for nm, w, m, v, transposed, last in (
            ("gu2", w_ffn2_gu, m_w_ffn2_gu, v_w_ffn2_gu, True, (("gu1_0", 1),)), ("d2", w_ffn2_down, m_w_ffn2_down, v_w_ffn2_down, False, ("d1_0",)),
            ("in", w_in, m_w_in, v_w_in, True, ()), ("out", w_out, m_w_out, v_w_out, False, ()),
            ("gu1", w_ffn1_gu, m_w_ffn1_gu, v_w_ffn1_gu, True, ()), ("d1", w_ffn1_down, m_w_ffn1_down, v_w_ffn1_down, False, ())):
        turn = (lambda a: jnp.swapaxes(a, 1, 2)) if transposed else (lambda a: a)
        names = [f"{nm}_{l}" for l in range(L)]
        res = run(adamw_scattered, turn(w), turn(m), turn(v), [chip_sum[n][0] for n in names], [recv_b[n] for n in names],
                  f"adamw_{nm}", rs2=last)
        upd[nm] = tuple(turn(r) for r in res)

    d_rel = rel_bias_grad(small["dsc"], bprev, bcur, "rel_bias_grad")[:, :8]

    PW = max(D, SB_W + SWA_W)

    n_rows = 4 * L + 2
    n_rows += (-n_rows) % 8

    def pack(ffn1, mix, ffn2, final, osb, osw, snk, rel, extra):
        pieces = []

        def row(*parts):
            flat = [a.reshape(-1) for a in parts]
            pieces.extend(flat)
            used = sum(a.size for a in flat)
            if used < PW:
                pieces.append(jnp.zeros((PW - used,), F32))

        for group in (ffn1, mix, ffn2):
            for l in range(L):
                row(group[l])
        row(final)
        for l in range(L):
            row(osb[l], osw[l])
        row(*[snk[l].reshape(-1)[:8] for l in range(L)], rel, extra)
        pieces.append(jnp.zeros(((n_rows - 4 * L - 2) * PW,), F32))
        return jnp.concatenate(pieces).reshape(n_rows, PW)

    def unpack(arr):
        ffn1, mix, ffn2 = arr[0:L, :D], arr[L:2 * L, :D], arr[2 * L:3 * L, :D]
        final = arr[3 * L, :D]
        ob = arr[3 * L + 1:4 * L + 1]
        tail = arr[4 * L + 1]
        return (ffn1, mix, tail[:8 * L].reshape(L, 8), ob[:, :SB_W], ob[:, SB_W:SB_W + SWA_W], ffn2,
                tail[8 * L:8 * L + N_BUCKETS * 8].reshape(N_BUCKETS, 8), final)

    zero = jnp.zeros((1,), F32)
    g_small = pack(small["ffn1"], small["mix"], small["ffn2"], dg_final, small["osb"], small["osw"], small["sinks"], d_rel,
                   loss_part[0, :1])
    w_small = pack(norm_ffn1, norm_mix, norm_ffn2, norm_final, norm_out_sb, norm_out_swa, sinks, rel_bias, zero)
    m_small = pack(m_norm_ffn1, m_norm_mix, m_norm_ffn2, m_norm_final, m_norm_out_sb, m_norm_out_swa, m_sinks, m_rel_bias, zero)
    v_small = pack(v_norm_ffn1, v_norm_mix, v_norm_ffn2, v_norm_final, v_norm_out_sb, v_norm_out_swa, v_sinks, v_rel_bias, zero)
    gs_small = all_gather_rows(g_small, "ag_small")
    summed = adamw_small(w_small, gs_small, m_small, v_small, "adamw_small")
    small_out = [unpack(a) for a in summed]
    loss = summed[0][4 * L + 1, 8 * L + N_BUCKETS * 8]

    def group(k):
        sm = small_out[k]
        return (sm[0], upd["gu1"][k], upd["d1"][k], sm[1], upd["in"][k], sm[2], sm[3], sm[4], upd["out"][k], sm[5],
                upd["gu2"][k], upd["d2"][k], sm[6], sm[7])

    return (loss, grad_x, *group(0), *group(1), *group(2), *group(3))
```

```python
import math

import jax
import jax.numpy as jnp
from jax import lax
from jax.experimental import pallas as pl
from jax.experimental.pallas import tpu as pltpu

F32 = jnp.float32
BF16 = jnp.bfloat16
S = jax.ShapeDtypeStruct

N_DEV = 8
HEAD_DIM = 64
SB_HEADS = 8
PAIR = 2 * HEAD_DIM
SB_W = 512
SWA_W = 512
KV_W = 128
IN_W = 3 * SB_W + SWA_W + 2 * KV_W
QB = 128
N_BUCKETS = 32
MAX_DISTANCE = 128
EPS = 1e-6
NEG_INF = -1e30
SCALE = HEAD_DIM ** -0.5

ADAM_LR = 0.001
ADAM_B1 = 0.9
ADAM_B2 = 0.999
ADAM_EPS = 1e-08
ADAM_WD = 0.01
ADAM_STEP = 10

VMEM_LIMIT = 56 * 1024 * 1024
MESH = pl.DeviceIdType.MESH


def _params(sem=None, vmem=VMEM_LIMIT):
    return pltpu.CompilerParams(dimension_semantics=sem, vmem_limit_bytes=vmem)


def _nn(a, b):
    return jnp.dot(a, b, preferred_element_type=F32)


def _nt(a, b):
    return lax.dot_general(a, b, (((1,), (1,)), ((), ())), preferred_element_type=F32)


def _tn(a, b):
    return lax.dot_general(a, b, (((0,), (0,)), ((), ())), preferred_element_type=F32)


def _tri(xs, m):
    return [_nn(x.astype(BF16), m) for x in xs]


def _rms(x, g):
    r = lax.rsqrt(jnp.mean(x * x, axis=-1, keepdims=True) + EPS)
    return x * r * g


def _rms_bwd(dy, x, g):
    r = lax.rsqrt(jnp.mean(x * x, axis=-1, keepdims=True) + EPS)
    xhat = x * r
    u = dy * g
    dx = r * (u - xhat * jnp.mean(u * xhat, axis=-1, keepdims=True))
    return dx, jnp.sum(dy * xhat, axis=0, keepdims=True)


def _softplus_logsig(z):
    sp = jnp.maximum(z, 0.0) + jnp.log(1.0 + jnp.exp(-jnp.abs(z)))
    return sp, z - sp


def _tile(n, want):
    t = min(n, want)
    while n % t:
        t //= 2
    return t


def _place():
    x, y, c = lax.axis_index("x"), lax.axis_index("y"), lax.axis_index("c")
    chips = [(1 - x, y), (x, 1 - y), (1 - x, 1 - y)]
    return x, y, c, chips


def all_gather_rows(v, name):
    R, C = v.shape

    def body(v_ref, out_ref, send_sems, recv_sems, local_sem):
        x, y, c, chips = _place()
        me, sibling = (x, y, c), (x, y, 1 - c)

        def slot(px, py, pc):
            return out_ref.at[4 * px + 2 * py + pc]

        def copy(k, block, to, src=None):
            return pltpu.make_async_remote_copy(
                src_ref=slot(*block) if src is None else src, dst_ref=slot(*block),
                send_sem=send_sems.at[k], recv_sem=recv_sems.at[k], device_id=to, device_id_type=MESH)

        mine = pltpu.make_async_copy(v_ref, slot(*me), local_sem)
        mine.start()
        first = [copy(0, me, sibling, src=v_ref)]
        first += [copy(1 + j, me, (*chip, c), src=v_ref) for j, chip in enumerate(chips)]
        for cp in first:
            cp.start()
        passed = [copy(4 + j, (*chip, c), sibling) for j, chip in enumerate(chips)]
        for j, chip in enumerate(chips):
            copy(1 + j, (*chip, c), me).wait_recv()
            passed[j].start()
        copy(0, sibling, me).wait_recv()
        for j, chip in enumerate(chips):
            copy(4 + j, (*chip, 1 - c), me).wait_recv()
        for cp in first + passed:
            cp.wait_send()
        mine.wait()

    return pl.pallas_call(
        body, name=name, out_shape=S((N_DEV, R, C), v.dtype),
        in_specs=[pl.BlockSpec(memory_space=pl.ANY)], out_specs=pl.BlockSpec(memory_space=pl.ANY),
        scratch_shapes=[pltpu.SemaphoreType.DMA((7,)), pltpu.SemaphoreType.DMA((7,)), pltpu.SemaphoreType.DMA],
    )(v)


class _Exchange:
    def __init__(self, ins, outs, sizes, n_local, plan, aliases=None):
        self.ins, self.outs, self.plan, self.aliases = list(ins), list(outs), plan, aliases or {}
        self.sizes, self.n_local = list(sizes), n_local

    def scratch(self):
        n = sum(self.sizes)
        return [pltpu.SemaphoreType.DMA((n,)), pltpu.SemaphoreType.DMA((n,)), pltpu.SemaphoreType.DMA((max(self.n_local, 1),))]

    def _copies(self, in_refs, out_refs, sems):
        send_sems, recv_sems, local_sems = sems
        phases, local = self.plan(in_refs, out_refs)
        out, k = [], 0
        for phase in phases:
            out.append([pltpu.make_async_remote_copy(src_ref=s, dst_ref=d, send_sem=send_sems.at[k + n], recv_sem=recv_sems.at[k + n],
                                                     device_id=dev, device_id_type=MESH) for n, (s, d, dev) in enumerate(phase)])
            k += len(phase)
        return out, [pltpu.make_async_copy(s, d, local_sems.at[n]) for n, (s, d) in enumerate(local)]

    def start(self, in_refs, out_refs, sems):
        phases, loc = self._copies(in_refs, out_refs, sems)
        for cp in phases[0] + loc:
            cp.start()

    def advance(self, hook, in_refs, out_refs, sems):
        p = hook - (3 - len(self.sizes))
        if p >= 1:
            phases, _ = self._copies(in_refs, out_refs, sems)
            for cp in phases[p - 1]:
                cp.wait_recv()
            for cp in phases[p]:
                cp.start()

    def finish(self, in_refs, out_refs, sems):
        phases, loc = self._copies(in_refs, out_refs, sems)
        for cp in phases[-1]:
            cp.wait_recv()
        for phase in phases:
            for cp in phase:
                cp.wait_send()
        for cp in loc:
            cp.wait()


def gather(v, rows=None, into=None):
    R, C = v.shape
    r0, nr = rows or (0, R)
    na = min(nr, ((nr // 2 + 15) // 16) * 16)

    def plan(ins, outs):
        x, y, c, _ = _place()
        xn, yn, dg, sibling = (1 - x, y), (x, 1 - y), (1 - x, 1 - y), (x, y, 1 - c)
        slot = lambda chip, start=r0, count=nr: outs[0].at[4 * chip[0] + 2 * chip[1] + c, pl.ds(start, count), :]
        src, mine = ins[0].at[pl.ds(r0, nr), :], slot((x, y))
        same = lambda ref, to: (ref, ref, to)
        first = [(src, mine, sibling), (src, mine, (*xn, c)), (src, mine, (*yn, c))]
        relay = [same(slot(xn, r0, na), (*yn, c)), same(slot(yn, r0 + na, nr - na), (*xn, c))]
        onward = [same(slot(xn), sibling), same(slot(yn), sibling), same(slot(dg), sibling)]
        return [first, relay, onward], [(src, mine)]

    if into is None:
        return _Exchange([v], [S((N_DEV, R, C), v.dtype)], (3, 2, 3), 1, plan)
    return _Exchange([v, into], [S((N_DEV, R, C), v.dtype)], (3, 2, 3), 1, plan, aliases={1: 0})


def scatter_first(gb):
    _, R, C = gb.shape

    def plan(ins, outs):
        x, y, c, chips = _place()
        owners = [(x, y)] + chips
        return [[(ins[0].at[4 * px + 2 * py + (1 - c)], outs[0].at[j], (x, y, 1 - c)) for j, (px, py) in enumerate(owners)]], []

    return _Exchange([gb], [S((4, R, C), BF16)], (4,), 0, plan)


def scatter_second(sb, rows=None, into=None):
    r0, nr = rows or (0, sb.shape[1])

    def plan(ins, outs):
        x, y, c, chips = _place()
        part = lambda ref, j: ref.at[j, pl.ds(r0, nr), :]
        return [[(part(ins[0], j), part(outs[0], j), (*chips[j], c)) for j in range(3)]], []

    if into is None:
        return _Exchange([sb], [S(sb.shape, BF16)], (3,), 0, plan)
    return _Exchange([sb, into], [S(sb.shape, BF16)], (3,), 0, plan, aliases={1: 0})


def _call(body, *, name, grid, in_specs, out_specs, out_shape, args, scratch=(), sem=None, riders=(), marks=None):
    single = not isinstance(out_shape, (tuple, list))
    out_shape = (out_shape,) if single else tuple(out_shape)
    out_specs = (out_specs,) if single else tuple(out_specs)
    n_in, n_out, n_sc = len(in_specs), len(out_shape), len(scratch)
    if not riders:
        res = pl.pallas_call(body, name=name, grid=grid, in_specs=list(in_specs), out_specs=out_specs, out_shape=out_shape,
                             scratch_shapes=list(scratch), compiler_params=_params(sem))(*args)
        return res[0] if single else res
    r_ins = [a for r in riders for a in r.ins]
    r_outs = [o for r in riders for o in r.outs]
    r_scr = [s for r in riders for s in r.scratch()]
    aliases, i0, o0 = {}, n_in, n_out
    for r in riders:
        for a, b in r.aliases.items():
            aliases[i0 + a] = o0 + b
        i0, o0 = i0 + len(r.ins), o0 + len(r.outs)
    steps = math.prod(grid)

    def full(*refs):
        ins, rin = refs[:n_in], refs[n_in:n_in + len(r_ins)]
        pos = n_in + len(r_ins)
        outs, rout = refs[pos:pos + n_out], refs[pos + n_out:pos + n_out + len(r_outs)]
        pos += n_out + len(r_outs)
        sc, rsc = refs[pos:pos + n_sc], refs[pos + n_sc:]
        step = 0
        for d, n in enumerate(grid):
            step = step * n + pl.program_id(d)

        def each(method, *lead):
            i, o = 0, 0
            for k, r in enumerate(riders):
                getattr(r, method)(*lead, rin[i:i + len(r.ins)], rout[o:o + len(r.outs)], rsc[3 * k:3 * k + 3])
                i, o = i + len(r.ins), o + len(r.outs)

        @pl.when(step == 0)
        def _():
            each("start")
        body(*ins, *outs, *sc)

        late = max(steps - 1 - max(steps // 8, 1), 0)
        first, second = marks or (min((3 * steps) // 5, late), late)

        @pl.when(step == first)
        def _():
            each("advance", 1)

        @pl.when(step == second)
        def _():
            each("advance", 2)

        @pl.when(step == steps - 1)
        def _():
            each("finish")

    anywhere = pl.BlockSpec(memory_space=pl.ANY)
    res = pl.pallas_call(
        full, name=name, grid=grid, in_specs=list(in_specs) + [anywhere] * len(r_ins),
        out_specs=out_specs + (anywhere,) * len(r_outs), out_shape=out_shape + tuple(r_outs),
        scratch_shapes=list(scratch) + r_scr, input_output_aliases=aliases,
        compiler_params=_params(("arbitrary",) * len(grid)))(*args, *r_ins)
    host, rest, per = res[:n_out], list(res[n_out:]), []
    for r in riders:
        per.append(rest[:len(r.outs)])
        rest = rest[len(r.outs):]
    return (host[0] if single else tuple(host)), per


def _rows_tile(n, cap):
    return max(t for t in range(16, min(n, cap) + 1, 16) if n % t == 0)


def scatter_add(g, ra, name):
    _, R, C = g.shape
    tr = _rows_tile(R, 176)
    x, y, c, chips = _place()
    slots = jnp.stack([4 * px + 2 * py + c for px, py in [(x, y)] + chips]).astype(jnp.int32)

    def body(s_ref, g0, g1, g2, g3, ra_ref, own_ref, sb_ref):
        own_ref[...] = g0[...] + ra_ref[0].astype(F32)
        for j, gj in enumerate((g1, g2, g3)):
            sb_ref[j] = (gj[...] + ra_ref[j + 1].astype(F32)).astype(BF16)

    spec = pltpu.PrefetchScalarGridSpec(
        num_scalar_prefetch=1, grid=(R // tr,),
        in_specs=[pl.BlockSpec((None, tr, C), lambda i, s, j=j: (s[j], i, 0)) for j in range(4)]
        + [pl.BlockSpec((4, tr, C), lambda i, s: (0, i, 0))],
        out_specs=(pl.BlockSpec((tr, C), lambda i, s: (i, 0)), pl.BlockSpec((3, tr, C), lambda i, s: (0, i, 0))))
    return pl.pallas_call(body, name=name, grid_spec=spec, out_shape=(S((R, C), F32), S((3, R, C), BF16)),
                          compiler_params=_params(("parallel",)))(slots, g, g, g, g, ra)


def rms_cast(h, g, name, riders=()):
    T, D = h.shape
    tm = _tile(T, 512)

    def body(h_ref, g_ref, n_ref):
        n_ref[...] = _rms(h_ref[...], g_ref[...]).astype(BF16)

    row = pl.BlockSpec((tm, D), lambda i: (i, 0))
    return _call(body, name=name, grid=(T // tm,), out_shape=S((T, D), BF16), in_specs=[row, pl.BlockSpec((1, D), lambda i: (0, 0))],
                 out_specs=row, sem=("parallel",), args=(h, g), riders=riders)


def ffn_up_fwd(n, wgu, name, riders=()):
    T, D = n.shape
    F = wgu.shape[1]
    tr, tn = _tile(T, 512), _tile(F, 256)

    def body(n_ref, wg_ref, wu_ref, dgate_ref, dup_ref, a_ref):
        wg, wu = wg_ref[...], wu_ref[...]
        for r in range(T // tr):
            rows = slice(r * tr, (r + 1) * tr)
            x = n_ref[rows, :]
            gate = _nt(x, wg)
            up = _nt(x, wu)
            s = jax.nn.sigmoid(gate)
            silu = gate * s
            dgate_ref[rows, :] = (up * (s * (1.0 + gate * (1.0 - s)))).astype(BF16)
            dup_ref[rows, :] = silu.astype(BF16)
            a_ref[rows, :] = (silu * up).astype(BF16)

    tile = pl.BlockSpec((T, tn), lambda j: (0, j))
    return _call(
        body, name=name, grid=(F // tn,), out_shape=(S((T, F), BF16),) * 3,
        in_specs=[pl.BlockSpec((T, D), lambda j: (0, 0)),
                  pl.BlockSpec((None, tn, D), lambda j: (0, j, 0)), pl.BlockSpec((None, tn, D), lambda j: (1, j, 0))],
        out_specs=(tile, tile, tile), sem=("parallel",), args=(n, wgu, wgu), riders=riders)


def ffn_down_fwd(a, wd, h, g_next, name, riders=()):
    T, F = a.shape
    D = wd.shape[1]
    tm = _tile(T, 256)

    def body(a_ref, w_ref, h_ref, *rest):
        out = h_ref[...] + 0.5 * _nn(a_ref[...], w_ref[...])
        if g_next is None:
            rest[0][...] = out
        else:
            g_ref, o_ref, n_ref = rest
            o_ref[...] = out
            n_ref[...] = _rms(out, g_ref[...]).astype(BF16)

    row = pl.BlockSpec((tm, D), lambda i: (i, 0))
    more = g_next is not None
    return _call(
        body, name=name, grid=(T // tm,), out_shape=(S((T, D), F32), S((T, D), BF16)) if more else S((T, D), F32),
        in_specs=[pl.BlockSpec((tm, F), lambda i: (i, 0)), pl.BlockSpec((F, D), lambda i: (0, 0)), row]
        + ([pl.BlockSpec((1, D), lambda i: (0, 0))] if more else []),
        out_specs=(row, row) if more else row,
        sem=("parallel",), args=(a, wd, h) + ((g_next,) if more else ()), riders=riders)


def mix_in_fwd(h, g, win, name):
    T, D = h.shape
    N = win.shape[0]
    tm = _tile(T, 256)

    def body(h_ref, g_ref, w_ref, n_ref, p_ref):
        n = _rms(h_ref[...], g_ref[...]).astype(BF16)
        n_ref[...] = n
        p_ref[...] = _nt(n, w_ref[...]).astype(BF16)

    return pl.pallas_call(
        body, name=name, grid=(T // tm,), out_shape=(S((T, D), BF16), S((T, N), BF16)),
        in_specs=[pl.BlockSpec((tm, D), lambda i: (i, 0)), pl.BlockSpec((1, D), lambda i: (0, 0)),
                  pl.BlockSpec((N, D), lambda i: (0, 0))],
        out_specs=(pl.BlockSpec((tm, D), lambda i: (i, 0)), pl.BlockSpec((tm, N), lambda i: (i, 0))),
        compiler_params=_params(("parallel",)),
    )(h, g, win)


def _tri_consts():
    r = lax.broadcasted_iota(jnp.int32, (QB, QB), 0)
    c = lax.broadcasted_iota(jnp.int32, (QB, QB), 1)
    ones = jnp.ones((QB, QB), BF16)
    with_sums = lambda tri: jnp.concatenate([tri.astype(BF16), ones], axis=1)
    return with_sums(r > c), with_sums(r <= c), with_sums(r < c)


def _half_masks():
    lane = lax.broadcasted_iota(jnp.int32, (QB, PAIR), 1)
    row = lax.broadcasted_iota(jnp.int32, (QB, PAIR), 0)
    return lane < HEAD_DIM, lane, row


def sb_attn_fwd(p, after, name, riders=()):
    T = p.shape[0]
    nq = T // QB

    def body(q_ref, k_ref, v_ref, m_ref, o_ref, tot_ref, q_sc, acc_ref, z_sc):
        i = pl.program_id(0)
        lo, lane, row = _half_masks()
        causal = lane < row
        heads, pairs = range(SB_HEADS), range(SB_HEADS // 2)
        for hp in pairs:
            q_sc[hp] = (q_ref[:, hp * PAIR:(hp + 1) * PAIR].astype(F32) * SCALE).astype(BF16)
        m2 = m_ref[...]

        def by_head(ref, j, hp):
            t = ref[pl.ds(pl.multiple_of(j * QB, QB), QB), hp * PAIR:(hp + 1) * PAIR]
            return jnp.concatenate([jnp.where(lo, t, 0), jnp.where(lo, 0, t)], axis=0)

        def scores(j):
            return [_nt(q_sc[hp], by_head(k_ref, j, hp)) for hp in pairs]

        def block(j, diag):
            z2 = [z_sc[hp] for hp in pairs]
            ahead = scores(jnp.maximum(j - 1, 0))
            for hp in pairs:
                z_sc[hp] = ahead[hp]
            vs = [by_head(v_ref, j, hp) for hp in pairs]
            spls = [_softplus_logsig(z2[h // 2][:, (h % 2) * QB:(h % 2 + 1) * QB]) for h in heads]
            sp = [jnp.where(causal, spls[h][0], 0.0) if diag else spls[h][0] for h in heads]
            rr = _tri(sp, m2)
            if diag:
                w = [jnp.where(causal, jnp.exp(spls[h][1] - rr[h][:, :QB]), 0.0).astype(BF16) for h in heads]
            else:
                c = [tot_ref[:, h * QB:(h + 1) * QB] for h in heads]
                w = [jnp.exp(spls[h][1] - (c[h] + rr[h][:, :QB])).astype(BF16) for h in heads]
            pv = [_nn(jnp.concatenate([w[2 * hp], w[2 * hp + 1]], axis=1), vs[hp]) for hp in pairs]
            for hp in pairs:
                acc_ref[hp] = pv[hp] if diag else acc_ref[hp] + pv[hp]
            for h in heads:
                tot_ref[:, h * QB:(h + 1) * QB] = rr[h][:, QB:] if diag else c[h] + rr[h][:, QB:]

        first = scores(i)
        for hp in pairs:
            z_sc[hp] = first[hp]
        block(i, True)

        def step(t, carry):
            block(i - 1 - t, False)
            return carry
        lax.fori_loop(0, i, step, 0)
        for hp in pairs:
            o_ref[:, hp * PAIR:(hp + 1) * PAIR] = acc_ref[hp]

    npair = SB_HEADS // 2
    return _call(
        body, name=name, grid=(nq,), out_shape=(S((T, SB_W), F32), S((T, SB_HEADS * QB), F32)),
        in_specs=[pl.BlockSpec((QB, SB_W), lambda i: (i, 0)), pl.BlockSpec((T, SB_W), lambda i: (0, 1)),
                  pl.BlockSpec((T, SB_W), lambda i: (0, 2)), pl.BlockSpec((QB, 2 * QB), lambda i: (0, 0))],
        out_specs=(pl.BlockSpec((QB, SB_W), lambda i: (i, 0)), pl.BlockSpec((QB, SB_HEADS * QB), lambda i: (i, 0))),
        scratch=[pltpu.VMEM((npair, QB, PAIR), BF16), pltpu.VMEM((npair, QB, PAIR), F32), pltpu.VMEM((npair, QB, 2 * QB), F32)],
        sem=("arbitrary",), args=(p, p, p, after), riders=riders,
        marks=((11 * nq) // 16, (13 * nq) // 16))


def sb_attn_bwd(p, do, tot, upto, before, name, riders=()):
    T = p.shape[0]
    nq = T // QB

    def body(q_ref, k_ref, v_ref, do_ref, tot_ref, mp_ref, mg_ref, dq_ref, dk_ref, dv_ref,
             q_sc, d_sc, qd_sc, pg_sc, dq_acc, dk_acc, dv_acc, zd_sc):
        i = pl.program_id(0)
        lo, lane, row = _half_masks()
        causal = lane < row
        heads, pairs = range(SB_HEADS), range(SB_HEADS // 2)

        def by_head(t):
            return jnp.concatenate([jnp.where(lo, t, 0), jnp.where(lo, 0, t)], axis=0)

        for hp in pairs:
            q2 = (q_ref[:, hp * PAIR:(hp + 1) * PAIR].astype(F32) * SCALE).astype(BF16)
            d2 = do_ref[:, hp * PAIR:(hp + 1) * PAIR].astype(BF16)
            q_sc[hp] = q2
            d_sc[hp] = d2
            qd_sc[hp] = by_head(q2)
            qd_sc[SB_HEADS // 2 + hp] = by_head(d2)
        mp, mg = mp_ref[...], mg_ref[...]

        @pl.when(i == 0)
        def _():
            dk_acc[...] = jnp.zeros_like(dk_acc)
            dv_acc[...] = jnp.zeros_like(dv_acc)
        pg_sc[...] = jnp.zeros_like(pg_sc)
        dq_acc[...] = jnp.zeros_like(dq_acc)

        def rows(ref, j, hp):
            return ref[pl.ds(pl.multiple_of(j * QB, QB), QB), hp * PAIR:(hp + 1) * PAIR]

        def products(j):
            return ([_nt(q_sc[hp], by_head(rows(k_ref, j, hp))) for hp in pairs]
                    + [_nt(d_sc[hp], by_head(rows(v_ref, j, hp))) for hp in pairs])

        def block(j, diag):
            r0 = pl.multiple_of(j * QB, QB)
            half = lambda t, h: t[:, (h % 2) * QB:(h % 2 + 1) * QB]
            z = [half(zd_sc[h // 2], h) for h in heads]
            dw = [half(zd_sc[SB_HEADS // 2 + h // 2], h) for h in heads]
            if not diag:
                ahead = products(j + 1)
                for hp in range(SB_HEADS):
                    zd_sc[hp] = ahead[hp]
            ks = [by_head(rows(k_ref, j, hp)) for hp in pairs]
            spls = [_softplus_logsig(z[h]) for h in heads]
            sp = [jnp.where(causal, spls[h][0], 0.0) if diag else spls[h][0] for h in heads]
            rr = _tri(sp, mp)
            pc = [pg_sc[2 * h] for h in heads]
            w = [jnp.exp(spls[h][1] - (tot_ref[:, h * QB:(h + 1) * QB] - (pc[h] + rr[h][:, :QB]))) for h in heads]
            if diag:
                w = [jnp.where(causal, w[h], 0.0) for h in heads]
            gg = [dw[h] * w[h] for h in heads]
            rg = _tri(gg, mg)
            gc = [pg_sc[2 * h + 1] for h in heads]
            dz = [gg[h] - (gg[h] + gc[h] + rg[h][:, :QB]) * jnp.exp(spls[h][1]) for h in heads]
            if diag:
                dz = [jnp.where(causal, dz[h], 0.0) for h in heads]
            dzb = [dz[h].astype(BF16) for h in heads]
            wb = [w[h].astype(BF16) for h in heads]
            both = lambda t, hp, axis: jnp.concatenate([t[2 * hp], t[2 * hp + 1]], axis=axis)
            dq = [_nn(both(dzb, hp, 1), ks[hp]) for hp in pairs]
            dk = [_tn(both(dzb, hp, 0), qd_sc[hp]) for hp in pairs]
            dv = [_tn(both(wb, hp, 0), qd_sc[SB_HEADS // 2 + hp]) for hp in pairs]
            for h in heads:
                if not diag:
                    pg_sc[2 * h] = pc[h] + rr[h][:, QB:]
                    pg_sc[2 * h + 1] = gc[h] + rg[h][:, QB:]
            for hp in pairs:
                dq_acc[hp] += dq[hp]
                dk_acc[pl.ds(r0, QB), hp * PAIR:(hp + 1) * PAIR] += dk[hp]
                dv_acc[pl.ds(r0, QB), hp * PAIR:(hp + 1) * PAIR] += dv[hp]

        first = products(0)
        for hp in range(SB_HEADS):
            zd_sc[hp] = first[hp]

        def step(t, carry):
            block(t, False)
            return carry
        lax.fori_loop(0, i, step, 0)
        block(i, True)
        for hp in pairs:
            dq_ref[:, hp * PAIR:(hp + 1) * PAIR] = (dq_acc[hp] * SCALE).astype(BF16)

        @pl.when(i == nq - 1)
        def _():
            dk_ref[...] = dk_acc[...].astype(BF16)
            dv_ref[...] = dv_acc[...].astype(BF16)

    qtile = pl.BlockSpec((QB, SB_W), lambda i: (i, 0))
    whole = pl.BlockSpec((T, SB_W), lambda i: (0, 0))
    const = pl.BlockSpec((QB, 2 * QB), lambda i: (0, 0))
    return _call(
        body, name=name, grid=(nq,), out_shape=(S((T, SB_W), BF16),) * 3,
        in_specs=[qtile, pl.BlockSpec((T, SB_W), lambda i: (0, 1)), pl.BlockSpec((T, SB_W), lambda i: (0, 2)), qtile,
                  pl.BlockSpec((QB, SB_HEADS * QB), lambda i: (i, 0)), const, const],
        out_specs=(qtile, whole, whole),
        scratch=[pltpu.VMEM((SB_HEADS // 2, QB, PAIR), BF16), pltpu.VMEM((SB_HEADS // 2, QB, PAIR), BF16),
                 pltpu.VMEM((SB_HEADS, 2 * QB, PAIR), BF16),
                 pltpu.VMEM((2 * SB_HEADS, QB, QB), F32), pltpu.VMEM((SB_HEADS // 2, QB, PAIR), F32),
                 pltpu.VMEM((T, SB_W), F32), pltpu.VMEM((T, SB_W), F32), pltpu.VMEM((SB_HEADS, QB, 2 * QB), F32)],
        sem=("arbitrary",), args=(p, p, p, do, tot, upto, before), riders=riders)


def _t5_buckets():
    a = lax.broadcasted_iota(jnp.int32, (QB, QB), 0)
    c = lax.broadcasted_iota(jnp.int32, (QB, QB), 1)

    def bucket(dist):
        dist = jnp.maximum(dist, 0)
        max_exact = N_BUCKETS // 2
        d = jnp.maximum(dist, 1).astype(F32)
        large = max_exact + (jnp.log(d / max_exact) / math.log(MAX_DISTANCE / max_exact)
                             * (N_BUCKETS - max_exact)).astype(jnp.int32)
        large = jnp.minimum(large, N_BUCKETS - 1)
        return jnp.where(dist < max_exact, dist, large)

    return bucket(QB + a - c), bucket(a - c)


def _swa_common(i, kp_ref, kc_ref, vp_ref, vc_ref, bp_ref, bc_ref, rb_ref, bias_ref):
    lo, lane, row = _half_masks()

    @pl.when(i == 0)
    def _():
        for blk, b_ref in enumerate((bp_ref, bc_ref)):
            bk = b_ref[...]
            for h in range(8):
                acc = jnp.zeros((QB, QB), F32)
                for b in range(N_BUCKETS):
                    acc = jnp.where(bk == b, rb_ref[b, h], acc)
                bias_ref[h, blk] = acc

    band = [(lane > row) & (i > 0), lane <= row]

    def stacks(ref):
        t = ref[...].astype(F32)
        sw = pltpu.roll(t, HEAD_DIM, 1)
        return [jnp.concatenate([jnp.where(lo, t, 0.0), jnp.where(lo, 0.0, sw)], axis=0).astype(BF16),
                jnp.concatenate([jnp.where(lo, sw, 0.0), jnp.where(lo, 0.0, t)], axis=0).astype(BF16)]

    ks = [stacks(kp_ref), stacks(kc_ref)]
    vs = [stacks(vp_ref), stacks(vc_ref)]
    return lo, band, ks, vs


def _lane_half(t, h):
    return t[:, (h % 2) * QB:(h % 2 + 1) * QB]


def swa_fwd(p, sinks, rel_bias, bprev, bcur, name, riders=()):
    T = p.shape[0]
    nq = T // QB
    kcol, vcol = (3 * SB_W + SWA_W) // KV_W, (3 * SB_W + SWA_W) // KV_W + 1

    def body(q_ref, kp_ref, kc_ref, vp_ref, vc_ref, bp_ref, bc_ref, sink_ref, rb_ref, o_ref, lse_ref, bias_ref):
        i = pl.program_id(0)
        lo, band, ks, vs = _swa_common(i, kp_ref, kc_ref, vp_ref, vc_ref, bp_ref, bc_ref, rb_ref, bias_ref)
        heads, pairs, blocks = range(8), range(4), range(2)
        rowmax = lambda t: jnp.max(t, axis=1, keepdims=True)
        rowsum = lambda t: jnp.sum(t, axis=1, keepdims=True)
        q2 = [q_ref[:, g * PAIR:(g + 1) * PAIR] for g in pairs]
        s2 = [[_nt(q2[g], ks[b][g // 2]) for b in blocks] for g in pairs]
        sc = [[jnp.where(band[b], _lane_half(s2[h // 2][b], h) * SCALE + bias_ref[h, b], NEG_INF) for b in blocks] for h in heads]
        sink = [sink_ref[0, h] for h in heads]
        m = [jnp.maximum(jnp.maximum(rowmax(sc[h][0]), rowmax(sc[h][1])), sink[h]) for h in heads]
        e = [[jnp.exp(sc[h][b] - m[h]) for b in blocks] for h in heads]
        den = [rowsum(e[h][0]) + rowsum(e[h][1]) + jnp.exp(sink[h] - m[h]) for h in heads]
        pb = [[(e[h][b] / den[h]).astype(BF16) for b in blocks] for h in heads]
        for g in pairs:
            both = lambda b: jnp.concatenate([pb[2 * g][b], pb[2 * g + 1][b]], axis=1)
            o_ref[:, g * PAIR:(g + 1) * PAIR] = _nn(both(0), vs[0][g // 2]) + _nn(both(1), vs[1][g // 2])
        for h in heads:
            lse_ref[:, h * QB:(h + 1) * QB] = jnp.broadcast_to(m[h] + jnp.log(den[h]), (QB, QB))

    kv = lambda col, prev: pl.BlockSpec((QB, KV_W), (lambda i: (jnp.maximum(i - 1, 0), col)) if prev else (lambda i: (i, col)))
    full = pl.BlockSpec((QB, QB), lambda i: (0, 0))
    smem = pl.BlockSpec(memory_space=pltpu.SMEM)
    return _call(
        body, name=name, grid=(nq,), out_shape=(S((T, SWA_W), F32), S((T, 8 * QB), F32)),
        in_specs=[pl.BlockSpec((QB, SWA_W), lambda i: (i, 3)), kv(kcol, True), kv(kcol, False), kv(vcol, True), kv(vcol, False),
                  full, full, smem, smem],
        out_specs=(pl.BlockSpec((QB, SWA_W), lambda i: (i, 0)), pl.BlockSpec((QB, 8 * QB), lambda i: (i, 0))),
        scratch=[pltpu.VMEM((8, 2, QB, QB), F32)],
        sem=("arbitrary",), args=(p, p, p, p, p, bprev, bcur, sinks, rel_bias), riders=riders)


def swa_bwd(p, do, lse, sinks, rel_bias, bprev, bcur, name, riders=()):
    T = p.shape[0]
    nq = T // QB
    kcol, vcol = (3 * SB_W + SWA_W) // KV_W, (3 * SB_W + SWA_W) // KV_W + 1

    def body(q_ref, kp_ref, kc_ref, vp_ref, vc_ref, do_ref, lse_ref, bp_ref, bc_ref, sink_ref, rb_ref,
             dq_ref, dk_ref, dv_ref, dsink_ref, dsc_ref, bias_ref, dk_acc, dv_acc):
        i = pl.program_id(0)
        lo, band, ks, vs = _swa_common(i, kp_ref, kc_ref, vp_ref, vc_ref, bp_ref, bc_ref, rb_ref, bias_ref)

        @pl.when(i == 0)
        def _():
            dk_acc[...] = jnp.zeros_like(dk_acc)
            dv_acc[...] = jnp.zeros_like(dv_acc)
            dsc_ref[...] = jnp.zeros_like(dsc_ref)
            dsink_ref[...] = jnp.zeros_like(dsink_ref)

        heads, pairs, blocks = range(8), range(4), range(2)
        rowsum = lambda t: jnp.sum(t, axis=1, keepdims=True)
        by_head = lambda t: jnp.concatenate([jnp.where(lo, t, 0), jnp.where(lo, 0, t)], axis=0)
        q2 = [q_ref[:, g * PAIR:(g + 1) * PAIR] for g in pairs]
        d2 = [do_ref[:, g * PAIR:(g + 1) * PAIR].astype(BF16) for g in pairs]
        qs = [by_head(q2[g]) for g in pairs]
        dos = [by_head(d2[g]) for g in pairs]
        s2 = [[_nt(q2[g], ks[b][g // 2]) for b in blocks] for g in pairs]
        dp2 = [[_nt(d2[g], vs[b][g // 2]) for b in blocks] for g in pairs]
        lse_h = [lse_ref[:, h * QB:(h + 1) * QB] for h in heads]
        sink = [sink_ref[0, h] for h in heads]
        pr = [[jnp.exp(jnp.where(band[b], _lane_half(s2[h // 2][b], h) * SCALE + bias_ref[h, b], NEG_INF) - lse_h[h])
               for b in blocks] for h in heads]
        dp = [[_lane_half(dp2[h // 2][b], h) for b in blocks] for h in heads]
        delta = [rowsum(pr[h][0] * dp[h][0]) + rowsum(pr[h][1] * dp[h][1]) for h in heads]
        lane1 = lax.broadcasted_iota(jnp.int32, (1, QB), 1)
        dsink = jnp.zeros((1, QB), F32)
        for h in heads:
            dsink = dsink + jnp.where(lane1 == h, -jnp.sum(jnp.exp(sink[h] - lse_h[h][:, :1]) * delta[h]), 0.0)
        dsink_ref[...] += dsink
        dsc = [[pr[h][b] * (dp[h][b] - delta[h]) for b in blocks] for h in heads]
        for h in heads:
            for b in blocks:
                dsc_ref[h, b] += dsc[h][b]
        dzb = [[(dsc[h][b] * SCALE).astype(BF16) for b in blocks] for h in heads]
        prb = [[pr[h][b].astype(BF16) for b in blocks] for h in heads]
        pair_of = lambda t, g, b, axis: jnp.concatenate([t[2 * g][b], t[2 * g + 1][b]], axis=axis)
        for g in pairs:
            dq = _nn(pair_of(dzb, g, 0, 1), ks[0][g // 2]) + _nn(pair_of(dzb, g, 1, 1), ks[1][g // 2])
            dq_ref[:, g * PAIR:(g + 1) * PAIR] = dq.astype(BF16)

        def key_grad(t, other, b):
            per_kv = [_tn(pair_of(t, 2 * kh, b, 0), other[2 * kh]) + _tn(pair_of(t, 2 * kh + 1, b, 0), other[2 * kh + 1]) for kh in range(2)]
            both = [s + pltpu.roll(s, HEAD_DIM, 1) for s in per_kv]
            return jnp.where(lo, both[0], both[1])

        rp = pl.multiple_of(jnp.maximum(i - 1, 0) * QB, QB)
        rc = pl.multiple_of(i * QB, QB)
        dk_acc[pl.ds(rp, QB), :] += key_grad(dzb, qs, 0)
        dv_acc[pl.ds(rp, QB), :] += key_grad(prb, dos, 0)
        dk_acc[pl.ds(rc, QB), :] += key_grad(dzb, qs, 1)
        dv_acc[pl.ds(rc, QB), :] += key_grad(prb, dos, 1)

        @pl.when(i == nq - 1)
        def _():
            dk_ref[...] = dk_acc[...].astype(BF16)
            dv_ref[...] = dv_acc[...].astype(BF16)

    kv = lambda col, prev: pl.BlockSpec((QB, KV_W), (lambda i: (jnp.maximum(i - 1, 0), col)) if prev else (lambda i: (i, col)))
    full = pl.BlockSpec((QB, QB), lambda i: (0, 0))
    smem = pl.BlockSpec(memory_space=pltpu.SMEM)
    whole = lambda shape: pl.BlockSpec(shape, lambda i: (0,) * len(shape))
    return _call(
        body, name=name, grid=(nq,),
        out_shape=(S((T, SWA_W), BF16), S((T, KV_W), BF16), S((T, KV_W), BF16), S((1, QB), F32), S((8, 2, QB, QB), F32)),
        in_specs=[pl.BlockSpec((QB, SWA_W), lambda i: (i, 3)), kv(kcol, True), kv(kcol, False), kv(vcol, True), kv(vcol, False),
                  pl.BlockSpec((QB, SWA_W), lambda i: (i, 0)), pl.BlockSpec((QB, 8 * QB), lambda i: (i, 0)),
                  full, full, smem, smem],
        out_specs=(pl.BlockSpec((QB, SWA_W), lambda i: (i, 0)), whole((T, KV_W)), whole((T, KV_W)), whole((1, QB)),
                   whole((8, 2, QB, QB))),
        scratch=[pltpu.VMEM((8, 2, QB, QB), F32), pltpu.VMEM((T, KV_W), F32), pltpu.VMEM((T, KV_W), F32)],
        sem=("arbitrary",), args=(p, p, p, p, p, do, lse, bprev, bcur, sinks, rel_bias), riders=riders)


def mix_out_fwd(o_sb, o_sw, g_sb, g_sw, wout, h, g_next, name, riders=()):
    T, D = h.shape
    M = SB_W + SWA_W
    tm = _tile(T, 256)

    def body(a_ref, b_ref, ga_ref, gb_ref, w_ref, h_ref, gn_ref, mx_ref, o_ref, n_ref):
        mx_ref[:, :SB_W] = _rms(a_ref[...], ga_ref[...]).astype(BF16)
        mx_ref[:, SB_W:] = _rms(b_ref[...], gb_ref[...]).astype(BF16)
        out = h_ref[...] + _nn(mx_ref[...], w_ref[...])
        o_ref[...] = out
        n_ref[...] = _rms(out, gn_ref[...]).astype(BF16)

    row = lambda n: pl.BlockSpec((tm, n), lambda i: (i, 0))
    vec = lambda n: pl.BlockSpec((1, n), lambda i: (0, 0))
    return _call(
        body, name=name, grid=(T // tm,), out_shape=(S((T, M), BF16), S((T, D), F32), S((T, D), BF16)),
        in_specs=[row(SB_W), row(SWA_W), vec(SB_W), vec(SWA_W), pl.BlockSpec((M, D), lambda i: (0, 0)), row(D), vec(D)],
        out_specs=(row(M), row(D), row(D)),
        sem=("parallel",), args=(o_sb, o_sw, g_sb, g_sw, wout, h, g_next), riders=riders)


def loss_head(h, g, target, name):
    T, D = h.shape
    tm = _tile(T, 256)

    def body(h_ref, g_ref, t_ref, loss_ref, dh_ref, dhb_ref, dg_ref):
        @pl.when(pl.program_id(0) == 0)
        def _():
            loss_ref[...] = jnp.zeros_like(loss_ref)
            dg_ref[...] = jnp.zeros_like(dg_ref)
        x = h_ref[...]
        err = _rms(x, g_ref[...]) - t_ref[...]
        loss_ref[...] += jnp.full((1, QB), 0.5 * jnp.sum(jnp.mean(err * err, axis=-1)), F32)
        dx, dg = _rms_bwd(err / D, x, g_ref[...])
        dh_ref[...] = dx
        dhb_ref[...] = dx.astype(BF16)
        dg_ref[...] += dg

    row = pl.BlockSpec((tm, D), lambda i: (i, 0))
    vec = pl.BlockSpec((1, D), lambda i: (0, 0))
    return pl.pallas_call(
        body, name=name, grid=(T // tm,), out_shape=(S((1, QB), F32), S((T, D), F32), S((T, D), BF16), S((1, D), F32)),
        in_specs=[row, vec, row], out_specs=(pl.BlockSpec((1, QB), lambda i: (0, 0)), row, row, vec),
        compiler_params=_params(("arbitrary",)),
    )(h, g, target)


def ffn_down_bwd(dhb, wd, gate, up, name, riders=()):
    T, D = dhb.shape
    F = wd.shape[0]
    tr, tn = _tile(T, 512), _tile(F, 256)

    def body(d_ref, w_ref, g_ref, u_ref, o_ref):
        w = w_ref[...]
        for r in range(T // tr):
            rows = slice(r * tr, (r + 1) * tr)
            da = 0.5 * _nt(d_ref[rows, :], w)
            o_ref[0, rows, :] = (da * g_ref[rows, :].astype(F32)).astype(BF16)
            o_ref[1, rows, :] = (da * u_ref[rows, :].astype(F32)).astype(BF16)

    tile = pl.BlockSpec((T, tn), lambda j: (0, j))
    return _call(
        body, name=name, grid=(F // tn,), out_shape=S((2, T, F), BF16),
        in_specs=[pl.BlockSpec((T, D), lambda j: (0, 0)), pl.BlockSpec((tn, D), lambda j: (j, 0)), tile, tile],
        out_specs=pl.BlockSpec((2, T, tn), lambda j: (0, 0, j)),
        sem=("parallel",), args=(dhb, wd, gate, up), riders=riders)


def tn_matmul(xs, y, alpha, name, riders=()):
    B, T, N = xs.shape
    D = y.shape[1]
    tn = _tile(N, 256)

    def body(x_ref, y_ref, o_ref, ob_ref):
        o = alpha * _tn(x_ref[...], y_ref[...])
        o_ref[...] = o
        ob_ref[...] = o.astype(BF16)

    tile = pl.BlockSpec((None, tn, D), lambda s, j: (s, j, 0))
    return _call(
        body, name=name, grid=(B, N // tn), out_shape=(S((B, N, D), F32), S((B, N, D), BF16)),
        in_specs=[pl.BlockSpec((None, T, tn), lambda s, j: (s, 0, j)), pl.BlockSpec((T, D), lambda s, j: (0, 0))],
        out_specs=(tile, tile), sem=("parallel", "parallel"), args=(xs, y), riders=riders)


def nn_rms_bwd(xs, ws, h_in, g, dh, name, riders=()):
    B, T, K = xs.shape
    D = ws.shape[2]
    tm = _tile(T, 256)

    def body(x_ref, w_ref, h_ref, g_ref, d_ref, o_ref, ob_ref, dg_ref):
        @pl.when(pl.program_id(0) == 0)
        def _():
            dg_ref[...] = jnp.zeros_like(dg_ref)
        dn = _nn(x_ref[0], w_ref[0])
        for s in range(1, B):
            dn = dn + _nn(x_ref[s], w_ref[s])
        dx, dg = _rms_bwd(dn, h_ref[...], g_ref[...])
        out = d_ref[...] + dx
        o_ref[...] = out
        ob_ref[...] = out.astype(BF16)
        dg_ref[...] += dg

    row = pl.BlockSpec((tm, D), lambda i: (i, 0))
    vec = pl.BlockSpec((1, D), lambda i: (0, 0))
    return _call(
        body, name=name, grid=(T // tm,), out_shape=(S((T, D), F32), S((T, D), BF16), S((1, D), F32)),
        in_specs=[pl.BlockSpec((B, tm, K), lambda i: (0, i, 0)), pl.BlockSpec((B, K, D), lambda i: (0, 0, 0)), row, vec, row],
        out_specs=(row, row, vec),
        sem=("arbitrary",), args=(xs, ws, h_in, g, dh), riders=riders)


def mix_out_bwd(dhb, wout, o_sb, o_sw, g_sb, g_sw, name):
    T, D = dhb.shape
    tm = _tile(T, 256)

    def body(d_ref, w_ref, a_ref, b_ref, ga_ref, gb_ref, da_ref, db_ref, dga_ref, dgb_ref):
        @pl.when(pl.program_id(0) == 0)
        def _():
            dga_ref[...] = jnp.zeros_like(dga_ref)
            dgb_ref[...] = jnp.zeros_like(dgb_ref)
        dm = _nt(d_ref[...], w_ref[...])
        dxa, dga = _rms_bwd(dm[:, :SB_W], a_ref[...], ga_ref[...])
        dxb, dgb = _rms_bwd(dm[:, SB_W:], b_ref[...], gb_ref[...])
        da_ref[...] = dxa
        db_ref[...] = dxb
        dga_ref[...] += dga
        dgb_ref[...] += dgb

    row = lambda n: pl.BlockSpec((tm, n), lambda i: (i, 0))
    vec = lambda n: pl.BlockSpec((1, n), lambda i: (0, 0))
    return pl.pallas_call(
        body, name=name, grid=(T // tm,),
        out_shape=(S((T, SB_W), F32), S((T, SWA_W), F32), S((1, SB_W), F32), S((1, SWA_W), F32)),
        in_specs=[row(D), pl.BlockSpec((SB_W + SWA_W, D), lambda i: (0, 0)), row(SB_W), row(SWA_W), vec(SB_W), vec(SWA_W)],
        out_specs=(row(SB_W), row(SWA_W), vec(SB_W), vec(SWA_W)),
        compiler_params=_params(("arbitrary",)),
    )(dhb, wout, o_sb, o_sw, g_sb, g_sw)


def rel_bias_grad(dscs, bprev, bcur, name):
    n = len(dscs)

    def body(*refs):
        bp_ref, bc_ref, o_ref = refs[n], refs[n + 1], refs[n + 2]
        bks = [bp_ref[...], bc_ref[...]]
        row = lax.broadcasted_iota(jnp.int32, (N_BUCKETS, QB), 0)
        lane = lax.broadcasted_iota(jnp.int32, (N_BUCKETS, QB), 1)
        out = jnp.zeros((N_BUCKETS, QB), F32)
        for h in range(8):
            tot = [sum(refs[l][h, b] for l in range(n)) for b in range(2)]
            for b in range(N_BUCKETS):
                val = jnp.sum(jnp.where(bks[0] == b, tot[0], 0.0)) + jnp.sum(jnp.where(bks[1] == b, tot[1], 0.0))
                out = jnp.where((row == b) & (lane == h), val, out)
        o_ref[...] = out

    return pl.pallas_call(body, name=name, out_shape=S((N_BUCKETS, QB), F32), compiler_params=_params())(*dscs, bprev, bcur)


def _adamw(w, g, m, v):
    m = ADAM_B1 * m + (1.0 - ADAM_B1) * g
    v = ADAM_B2 * v + (1.0 - ADAM_B2) * (g * g)
    m_hat = m / (1.0 - ADAM_B1 ** ADAM_STEP)
    v_hat = v / (1.0 - ADAM_B2 ** ADAM_STEP)
    delta = -ADAM_LR * (m_hat / (jnp.sqrt(v_hat) + ADAM_EPS) + ADAM_WD * w)
    return delta, m, v


def adamw_scattered(w, m, v, owns, others, name, riders=()):
    L, R, C = w.shape
    tr = _rows_tile(R, 176)

    def body(w_ref, m_ref, v_ref, *rest):
        own_refs, other_refs = rest[:L], rest[L:2 * L]
        g_ref, d_ref, mo_ref, vo_ref = rest[2 * L:]
        layer = pl.program_id(0)

        def grad(k):
            o = other_refs[k]
            return own_refs[k][...] + o[0].astype(F32) + o[1].astype(F32) + o[2].astype(F32)

        g = grad(0)
        for k in range(1, L):
            g = jnp.where(layer == k, grad(k), g)
        d, mn, vn = _adamw(w_ref[...], g, m_ref[...], v_ref[...])
        g_ref[...] = g
        d_ref[...] = d
        mo_ref[...] = mn
        vo_ref[...] = vn

    tile = pl.BlockSpec((None, tr, C), lambda l, i: (l, i, 0))
    return _call(
        body, name=name, grid=(L, R // tr), out_shape=(S((L, R, C), F32),) * 4,
        in_specs=[tile] * 3 + [pl.BlockSpec((tr, C), lambda l, i: (i, 0))] * L + [pl.BlockSpec((3, tr, C), lambda l, i: (0, i, 0))] * L,
        out_specs=(tile,) * 4, sem=("parallel", "parallel"), args=(w, m, v, *owns, *others), riders=riders)


def adamw_small(w, gs, m, v, name):
    R, C = w.shape

    def body(w_ref, g_ref, m_ref, v_ref, go_ref, d_ref, mo_ref, vo_ref):
        g = g_ref[0]
        for k in range(1, N_DEV):
            g = g + g_ref[k]
        d, mn, vn = _adamw(w_ref[...], g, m_ref[...], v_ref[...])
        go_ref[...] = g
        d_ref[...] = d
        mo_ref[...] = mn
        vo_ref[...] = vn

    return pl.pallas_call(body, name=name, out_shape=(S((R, C), F32),) * 4, compiler_params=_params())(w, gs, m, v)


def kernel(x, norm_ffn1, w_ffn1_gu, w_ffn1_down, norm_mix, w_in, sinks, norm_out_sb, norm_out_swa, w_out, norm_ffn2, w_ffn2_gu, w_ffn2_down, rel_bias, norm_final, loss_target, m_norm_ffn1, m_w_ffn1_gu, m_w_ffn1_down, m_norm_mix, m_w_in, m_sinks, m_norm_out_sb, m_norm_out_swa, m_w_out, m_norm_ffn2, m_w_ffn2_gu, m_w_ffn2_down, m_rel_bias, m_norm_final, v_norm_ffn1, v_w_ffn1_gu, v_w_ffn1_down, v_norm_mix, v_w_in, v_sinks, v_norm_out_sb, v_norm_out_swa, v_w_out, v_norm_ffn2, v_w_ffn2_gu, v_w_ffn2_down, v_rel_bias, v_norm_final):
    L = norm_ffn1.shape[0]
    T, D = x.shape[1], x.shape[2]
    F = w_ffn1_down.shape[1] * N_DEV
    h = x.reshape(T, D)
    target = loss_target.reshape(T, D)
    after, upto, before = _tri_consts()
    bprev, bcur = _t5_buckets()

    local = {}
    for l in range(L):
        local[f"gu1_{l}"] = w_ffn1_gu[l].T.astype(BF16)
        local[f"d1_{l}"] = w_ffn1_down[l].astype(BF16)
        local[f"in_{l}"] = w_in[l].T.astype(BF16)
        local[f"out_{l}"] = w_out[l].astype(BF16)
        local[f"gu2_{l}"] = w_ffn2_gu[l].T.astype(BF16)
        local[f"d2_{l}"] = w_ffn2_down[l].astype(BF16)
    full, partial = {}, {}
    grads, chip_sum, recv_b = {}, {}, {}

    def run(fn, *args, ag=(), rs1=(), rs2=()):
        halves = lambda names: [n if isinstance(n, tuple) else (n, None) for n in names]
        ag, rs2 = [(n, k) for n, k in halves(ag) if n in local], halves(rs2)
        rows = lambda k, total: None if k is None else (k * (total // 2), total // 2)

        def second(n, k):
            sb = chip_sum[n][1]
            return scatter_second(sb, rows(k, sb.shape[1]), recv_b.get(n))

        riders = ([gather(local[n], rows(k, local[n].shape[0]), partial.get(n)) for n, k in ag]
                  + [scatter_first(grads[n][1]) for n in rs1] + [second(n, k) for n, k in rs2])
        if not riders:
            return fn(*args)
        outs, per = fn(*args, riders=riders)
        per = [p[0] for p in per]
        for n, k in ag:
            buf = per.pop(0)
            if k == 0:
                partial[n] = buf
            else:
                full[n] = buf.reshape(N_DEV * buf.shape[1], D)
        for n in rs1:
            chip_sum[n] = scatter_add(grads[n][0], per.pop(0), f"rs_add_{n}")
        for n, _ in rs2:
            recv_b[n] = per.pop(0)
        return outs

    gu = lambda n: full[n].reshape(2, F, D)
    slots = lambda pair: tuple(t.reshape(N_DEV, -1, D) for t in pair)
    vec = lambda a: a.reshape(1, -1)

    PW = max(D, SB_W + SWA_W)
    n_rows = 4 * L + 2
    n_rows += (-n_rows) % 8

    def pack(ffn1, mix, ffn2, final, osb, osw, snk, rel, extra):
        pieces = []

        def row(*parts):
            flat = [a.reshape(-1) for a in parts]
            pieces.extend(flat)
            used = sum(a.size for a in flat)
            if used < PW:
                pieces.append(jnp.zeros((PW - used,), F32))

        for group in (ffn1, mix, ffn2):
            for l in range(L):
                row(group[l])
        row(final)
        for l in range(L):
            row(osb[l], osw[l])
        row(*[snk[l].reshape(-1)[:8] for l in range(L)], rel, extra)
        pieces.append(jnp.zeros(((n_rows - 4 * L - 2) * PW,), F32))
        return jnp.concatenate(pieces).reshape(n_rows, PW)

    def unpack(arr):
        ffn1, mix, ffn2 = arr[0:L, :D], arr[L:2 * L, :D], arr[2 * L:3 * L, :D]
        final = arr[3 * L, :D]
        ob = arr[3 * L + 1:4 * L + 1]
        tail = arr[4 * L + 1]
        return (ffn1, mix, tail[:8 * L].reshape(L, 8), ob[:, :SB_W], ob[:, SB_W:SB_W + SWA_W], ffn2,
                tail[8 * L:8 * L + N_BUCKETS * 8].reshape(N_BUCKETS, 8), final)

    zero = jnp.zeros((1,), F32)
    w_small = pack(norm_ffn1, norm_mix, norm_ffn2, norm_final, norm_out_sb, norm_out_swa, sinks, rel_bias, zero)
    norm_ffn1, norm_mix, sinks, norm_out_sb, norm_out_swa, norm_ffn2, _, norm_final = unpack(w_small)

    saved = []
    n_next = run(rms_cast, h, vec(norm_ffn1[0]), "rms_first", ag=("gu1_0",))
    for l in range(L):
        nx = l + 1
        s = {"h0": h, "n1": n_next}
        s["gate1"], s["up1"], s["a1"] = run(ffn_up_fwd, s["n1"], gu(f"gu1_{l}"), f"ffn1_up{l}",
                                            ag=(f"d1_{l}",) + ((("in_0", 0),) if l == 0 else ()))
        h = run(ffn_down_fwd, s["a1"], full[f"d1_{l}"], h, None, f"ffn1_down{l}", ag=((f"in_{l}", 1),))
        s["h1"] = h
        s["n2"], s["p"] = mix_in_fwd(h, vec(norm_mix[l]), full[f"in_{l}"], f"mix_in{l}")
        s["o_sb"], s["tot"] = run(sb_attn_fwd, s["p"], after, f"sb_fwd{l}", ag=(f"out_{l}", f"gu2_{l}", f"d2_{l}"))
        s["o_sw"], s["lse"] = run(swa_fwd, s["p"], vec(sinks[l]), rel_bias, bprev, bcur, f"swa_fwd{l}", ag=((f"gu1_{nx}", 0),))
        s["mixed"], h, s["n3"] = run(mix_out_fwd, s["o_sb"], s["o_sw"], vec(norm_out_sb[l]), vec(norm_out_swa[l]),
                                     full[f"out_{l}"], h, vec(norm_ffn2[l]), f"mix_out{l}")
        s["h2"] = h
        s["gate2"], s["up2"], s["a2"] = run(ffn_up_fwd, s["n3"], gu(f"gu2_{l}"), f"ffn2_up{l}", ag=((f"gu1_{nx}", 1),))
        if nx < L:
            h, n_next = run(ffn_down_fwd, s["a2"], full[f"d2_{l}"], h, vec(norm_ffn1[nx]), f"ffn2_down{l}", ag=((f"in_{nx}", 0),))
        else:
            h = run(ffn_down_fwd, s["a2"], full[f"d2_{l}"], h, None, f"ffn2_down{l}")
        saved.append(s)

    loss_part, dh, dhb, dg_final = loss_head(h, vec(norm_final), target, "loss_head")

    small = {k: [None] * L for k in ("ffn1", "mix", "sinks", "osb", "osw", "ffn2", "dsc")}
    for l in reversed(range(L)):
        s = saved[l]

        def ffn_bwd(dh, dhb, tag, gate, up, a, n, h_in, g, r_down, r_dwgu, r_up):
            gu_n, d_n = f"gu{tag}_{l}", f"d{tag}_{l}"
            dgu = run(ffn_down_bwd, dhb, full[d_n], gate, up, f"ffn{tag}_down_bwd{l}", **r_down)
            grads[gu_n] = slots(run(tn_matmul, dgu, n, 1.0, f"ffn{tag}_dwgu{l}", **r_dwgu))
            grads[d_n] = slots(run(tn_matmul, a[None], dhb, 0.5, f"ffn{tag}_dwd{l}", rs1=(gu_n,)))
            return run(nn_rms_bwd, dgu, gu(gu_n), h_in, g, dh, f"ffn{tag}_up_bwd{l}", rs1=(d_n,), **r_up)

        later = l + 1 < L
        dh, dhb, small["ffn2"][l] = ffn_bwd(dh, dhb, 2, s["gate2"], s["up2"], s["a2"], s["n3"], s["h2"], vec(norm_ffn2[l]),
                                            {}, dict(rs2=(f"d1_{l + 1}",) if later else ()),
                                            dict(rs2=((f"gu1_{l + 1}", 1),) if later else ()))
        do_sb, do_sw, small["osb"][l], small["osw"][l] = mix_out_bwd(
            dhb, full[f"out_{l}"], s["o_sb"], s["o_sw"], vec(norm_out_sb[l]), vec(norm_out_swa[l]), f"mix_out_bwd{l}")
        grads[f"out_{l}"] = slots(tn_matmul(s["mixed"][None], dhb, 1.0, f"dwout{l}"))
        dq_sb, dk_sb, dv_sb = run(sb_attn_bwd, s["p"], do_sb, s["tot"], upto, before, f"sb_bwd{l}",
                                  rs2=(f"gu2_{l}", f"d2_{l}"), rs1=(f"out_{l}",))
        dq_sw, dk_sw, dv_sw, small["sinks"][l], small["dsc"][l] = run(
            swa_bwd, s["p"], do_sw, s["lse"], vec(sinks[l]), rel_bias, bprev, bcur, f"swa_bwd{l}", rs2=(f"out_{l}",))
        dp = jnp.concatenate([dq_sb, dk_sb, dv_sb, dq_sw, dk_sw, dv_sw], axis=1)
        dh, dhb, small["mix"][l] = nn_rms_bwd(dp[None], full[f"in_{l}"][None], s["h1"], vec(norm_mix[l]), dh, f"mix_in_bwd{l}")
        grads[f"in_{l}"] = slots(tn_matmul(dp[None], s["n2"], 1.0, f"dwin{l}"))
        dh, dhb, small["ffn1"][l] = ffn_bwd(dh, dhb, 1, s["gate1"], s["up1"], s["a1"], s["n1"], s["h0"], vec(norm_ffn1[l]),
                                            dict(rs1=(f"in_{l}",)), dict(rs2=(f"in_{l}",)), dict(rs2=((f"gu1_{l}", 0),)))

    grad_x = dh.reshape(x.shape)

    upd = {}
    for nm, w, m, v, transposed, last in (
            ("gu2", w_ffn2_gu, m_w_ffn2_gu, v_w_ffn2_gu, True, (("gu1_0", 1), "d1_0")), ("d2", w_ffn2_down, m_w_ffn2_down, v_w_ffn2_down, False, ()),
            ("in", w_in, m_w_in, v_w_in, True, ()), ("out", w_out, m_w_out, v_w_out, False, ()),
            ("gu1", w_ffn1_gu, m_w_ffn1_gu, v_w_ffn1_gu, True, ()), ("d1", w_ffn1_down, m_w_ffn1_down, v_w_ffn1_down, False, ())):
        turn = (lambda a: jnp.swapaxes(a, 1, 2)) if transposed else (lambda a: a)
        names = [f"{nm}_{l}" for l in range(L)]
        res = run(adamw_scattered, turn(w), turn(m), turn(v), [chip_sum[n][0] for n in names], [recv_b[n] for n in names],
                  f"adamw_{nm}", rs2=last)
        upd[nm] = tuple(turn(r) for r in res)

    d_rel = rel_bias_grad(small["dsc"], bprev, bcur, "rel_bias_grad")[:, :8]
    g_small = pack(small["ffn1"], small["mix"], small["ffn2"], dg_final, small["osb"], small["osw"], small["sinks"], d_rel,
                   loss_part[0, :1])
    m_small = pack(m_norm_ffn1, m_norm_mix, m_norm_ffn2, m_norm_final, m_norm_out_sb, m_norm_out_swa, m_sinks, m_rel_bias, zero)
    v_small = pack(v_norm_ffn1, v_norm_mix, v_norm_ffn2, v_norm_final, v_norm_out_sb, v_norm_out_swa, v_sinks, v_rel_bias, zero)
    gs_small = all_gather_rows(g_small, "ag_small")
    summed = adamw_small(w_small, gs_small, m_small, v_small, "adamw_small")
    small_out = [unpack(a) for a in summed]
    loss = summed[0][4 * L + 1, 8 * L + N_BUCKETS * 8]

    def group(k):
        sm = small_out[k]
        return (sm[0], upd["gu1"][k], upd["d1"][k], sm[1], upd["in"][k], sm[2], sm[3], sm[4], upd["out"][k], sm[5],
                upd["gu2"][k], upd["d2"][k], sm[6], sm[7])

    return (loss, grad_x, *group(0), *group(1), *group(2), *group(3))
```

```python
import math

import jax
import jax.numpy as jnp
from jax import lax
from jax.experimental import pallas as pl
from jax.experimental.pallas import tpu as pltpu

F32 = jnp.float32
BF16 = jnp.bfloat16
S = jax.ShapeDtypeStruct

N_DEV = 8
HEAD_DIM = 64
SB_HEADS = 8
PAIR = 2 * HEAD_DIM
SB_W = 512
SWA_W = 512
KV_W = 128
IN_W = 3 * SB_W + SWA_W + 2 * KV_W
QB = 128
N_BUCKETS = 32
MAX_DISTANCE = 128
EPS = 1e-6
NEG_INF = -1e30
SCALE = HEAD_DIM ** -0.5

ADAM_LR = 0.001
ADAM_B1 = 0.9
ADAM_B2 = 0.999
ADAM_EPS = 1e-08
ADAM_WD = 0.01
ADAM_STEP = 10

VMEM_LIMIT = 56 * 1024 * 1024
MESH = pl.DeviceIdType.MESH


def _params(sem=None, vmem=VMEM_LIMIT):
    return pltpu.CompilerParams(dimension_semantics=sem, vmem_limit_bytes=vmem)


def _nn(a, b):
    return jnp.dot(a, b, preferred_element_type=F32)


def _nt(a, b):
    return lax.dot_general(a, b, (((1,), (1,)), ((), ())), preferred_element_type=F32)


def _tn(a, b):
    return lax.dot_general(a, b, (((0,), (0,)), ((), ())), preferred_element_type=F32)


def _tri(xs, m):
    return [_nn(x.astype(BF16), m) for x in xs]


def _rms(x, g):
    r = lax.rsqrt(jnp.mean(x * x, axis=-1, keepdims=True) + EPS)
    return x * r * g


def _rms_bwd(dy, x, g):
    r = lax.rsqrt(jnp.mean(x * x, axis=-1, keepdims=True) + EPS)
    xhat = x * r
    u = dy * g
    dx = r * (u - xhat * jnp.mean(u * xhat, axis=-1, keepdims=True))
    return dx, jnp.sum(dy * xhat, axis=0, keepdims=True)


def _softplus_logsig(z):
    sp = jnp.maximum(z, 0.0) + jnp.log(1.0 + jnp.exp(-jnp.abs(z)))
    return sp, z - sp


def _tile(n, want):
    t = min(n, want)
    while n % t:
        t //= 2
    return t


def _place():
    x, y, c = lax.axis_index("x"), lax.axis_index("y"), lax.axis_index("c")
    chips = [(1 - x, y), (x, 1 - y), (1 - x, 1 - y)]
    return x, y, c, chips


def all_gather_rows(v, name):
    R, C = v.shape

    def body(v_ref, out_ref, send_sems, recv_sems, local_sem):
        x, y, c, chips = _place()
        me, sibling = (x, y, c), (x, y, 1 - c)

        def slot(px, py, pc):
            return out_ref.at[4 * px + 2 * py + pc]

        def copy(k, block, to, src=None):
            return pltpu.make_async_remote_copy(
                src_ref=slot(*block) if src is None else src, dst_ref=slot(*block),
                send_sem=send_sems.at[k], recv_sem=recv_sems.at[k], device_id=to, device_id_type=MESH)

        mine = pltpu.make_async_copy(v_ref, slot(*me), local_sem)
        mine.start()
        first = [copy(0, me, sibling, src=v_ref)]
        first += [copy(1 + j, me, (*chip, c), src=v_ref) for j, chip in enumerate(chips)]
        for cp in first:
            cp.start()
        passed = [copy(4 + j, (*chip, c), sibling) for j, chip in enumerate(chips)]
        for j, chip in enumerate(chips):
            copy(1 + j, (*chip, c), me).wait_recv()
            passed[j].start()
        copy(0, sibling, me).wait_recv()
        for j, chip in enumerate(chips):
            copy(4 + j, (*chip, 1 - c), me).wait_recv()
        for cp in first + passed:
            cp.wait_send()
        mine.wait()

    return pl.pallas_call(
        body, name=name, out_shape=S((N_DEV, R, C), v.dtype),
        in_specs=[pl.BlockSpec(memory_space=pl.ANY)], out_specs=pl.BlockSpec(memory_space=pl.ANY),
        scratch_shapes=[pltpu.SemaphoreType.DMA((7,)), pltpu.SemaphoreType.DMA((7,)), pltpu.SemaphoreType.DMA],
    )(v)


class _Exchange:
    def __init__(self, ins, outs, sizes, n_local, plan, aliases=None):
        self.ins, self.outs, self.plan, self.aliases = list(ins), list(outs), plan, aliases or {}
        self.sizes, self.n_local = list(sizes), n_local

    def scratch(self):
        n = sum(self.sizes)
        return [pltpu.SemaphoreType.DMA((n,)), pltpu.SemaphoreType.DMA((n,)), pltpu.SemaphoreType.DMA((max(self.n_local, 1),))]

    def _copies(self, in_refs, out_refs, sems):
        send_sems, recv_sems, local_sems = sems
        phases, local = self.plan(in_refs, out_refs)
        out, k = [], 0
        for phase in phases:
            out.append([pltpu.make_async_remote_copy(src_ref=s, dst_ref=d, send_sem=send_sems.at[k + n], recv_sem=recv_sems.at[k + n],
                                                     device_id=dev, device_id_type=MESH) for n, (s, d, dev) in enumerate(phase)])
            k += len(phase)
        return out, [pltpu.make_async_copy(s, d, local_sems.at[n]) for n, (s, d) in enumerate(local)]

    def start(self, in_refs, out_refs, sems):
        phases, loc = self._copies(in_refs, out_refs, sems)
        for cp in phases[0] + loc:
            cp.start()

    def advance(self, hook, in_refs, out_refs, sems):
        p = hook - (3 - len(self.sizes))
        if p >= 1:
            phases, _ = self._copies(in_refs, out_refs, sems)
            for cp in phases[p - 1]:
                cp.wait_recv()
            for cp in phases[p]:
                cp.start()

    def finish(self, in_refs, out_refs, sems):
        phases, loc = self._copies(in_refs, out_refs, sems)
        for cp in phases[-1]:
            cp.wait_recv()
        for phase in phases:
            for cp in phase:
                cp.wait_send()
        for cp in loc:
            cp.wait()


def gather(v, rows=None, into=None):
    R, C = v.shape
    r0, nr = rows or (0, R)
    na = min(nr, ((nr // 2 + 15) // 16) * 16)

    def plan(ins, outs):
        x, y, c, _ = _place()
        xn, yn, dg, sibling = (1 - x, y), (x, 1 - y), (1 - x, 1 - y), (x, y, 1 - c)
        slot = lambda chip, start=r0, count=nr: outs[0].at[4 * chip[0] + 2 * chip[1] + c, pl.ds(start, count), :]
        src, mine = ins[0].at[pl.ds(r0, nr), :], slot((x, y))
        same = lambda ref, to: (ref, ref, to)
        first = [(src, mine, sibling), (src, mine, (*xn, c)), (src, mine, (*yn, c))]
        relay = [same(slot(xn, r0, na), (*yn, c)), same(slot(yn, r0 + na, nr - na), (*xn, c))]
        onward = [same(slot(xn), sibling), same(slot(yn), sibling), same(slot(dg), sibling)]
        return [first, relay, onward], [(src, mine)]

    if into is None:
        return _Exchange([v], [S((N_DEV, R, C), v.dtype)], (3, 2, 3), 1, plan)
    return _Exchange([v, into], [S((N_DEV, R, C), v.dtype)], (3, 2, 3), 1, plan, aliases={1: 0})


def scatter_first(gb):
    _, R, C = gb.shape

    def plan(ins, outs):
        x, y, c, chips = _place()
        owners = [(x, y)] + chips
        return [[(ins[0].at[4 * px + 2 * py + (1 - c)], outs[0].at[j], (x, y, 1 - c)) for j, (px, py) in enumerate(owners)]], []

    return _Exchange([gb], [S((4, R, C), BF16)], (4,), 0, plan)


def scatter_second(sb, rows=None, into=None):
    r0, nr = rows or (0, sb.shape[1])

    def plan(ins, outs):
        x, y, c, chips = _place()
        part = lambda ref, j: ref.at[j, pl.ds(r0, nr), :]
        return [[(part(ins[0], j), part(outs[0], j), (*chips[j], c)) for j in range(3)]], []

    if into is None:
        return _Exchange([sb], [S(sb.shape, BF16)], (3,), 0, plan)
    return _Exchange([sb, into], [S(sb.shape, BF16)], (3,), 0, plan, aliases={1: 0})


def _call(body, *, name, grid, in_specs, out_specs, out_shape, args, scratch=(), sem=None, riders=(), marks=None):
    single = not isinstance(out_shape, (tuple, list))
    out_shape = (out_shape,) if single else tuple(out_shape)
    out_specs = (out_specs,) if single else tuple(out_specs)
    n_in, n_out, n_sc = len(in_specs), len(out_shape), len(scratch)
    if not riders:
        res = pl.pallas_call(body, name=name, grid=grid, in_specs=list(in_specs), out_specs=out_specs, out_shape=out_shape,
                             scratch_shapes=list(scratch), compiler_params=_params(sem))(*args)
        return res[0] if single else res
    r_ins = [a for r in riders for a in r.ins]
    r_outs = [o for r in riders for o in r.outs]
    r_scr = [s for r in riders for s in r.scratch()]
    aliases, i0, o0 = {}, n_in, n_out
    for r in riders:
        for a, b in r.aliases.items():
            aliases[i0 + a] = o0 + b
        i0, o0 = i0 + len(r.ins), o0 + len(r.outs)
    steps = math.prod(grid)

    def full(*refs):
        ins, rin = refs[:n_in], refs[n_in:n_in + len(r_ins)]
        pos = n_in + len(r_ins)
        outs, rout = refs[pos:pos + n_out], refs[pos + n_out:pos + n_out + len(r_outs)]
        pos += n_out + len(r_outs)
        sc, rsc = refs[pos:pos + n_sc], refs[pos + n_sc:]
        step = 0
        for d, n in enumerate(grid):
            step = step * n + pl.program_id(d)

        def each(method, *lead):
            i, o = 0, 0
            for k, r in enumerate(riders):
                getattr(r, method)(*lead, rin[i:i + len(r.ins)], rout[o:o + len(r.outs)], rsc[3 * k:3 * k + 3])
                i, o = i + len(r.ins), o + len(r.outs)

        @pl.when(step == 0)
        def _():
            each("start")
        body(*ins, *outs, *sc)

        late = max(steps - 1 - max(steps // 8, 1), 0)
        first, second = marks or (min((3 * steps) // 5, late), late)

        @pl.when(step == first)
        def _():
            each("advance", 1)

        @pl.when(step == second)
        def _():
            each("advance", 2)

        @pl.when(step == steps - 1)
        def _():
            each("finish")

    anywhere = pl.BlockSpec(memory_space=pl.ANY)
    res = pl.pallas_call(
        full, name=name, grid=grid, in_specs=list(in_specs) + [anywhere] * len(r_ins),
        out_specs=out_specs + (anywhere,) * len(r_outs), out_shape=out_shape + tuple(r_outs),
        scratch_shapes=list(scratch) + r_scr, input_output_aliases=aliases,
        compiler_params=_params(("arbitrary",) * len(grid)))(*args, *r_ins)
    host, rest, per = res[:n_out], list(res[n_out:]), []
    for r in riders:
        per.append(rest[:len(r.outs)])
        rest = rest[len(r.outs):]
    return (host[0] if single else tuple(host)), per


def _rows_tile(n, cap):
    return max(t for t in range(16, min(n, cap) + 1, 16) if n % t == 0)


def scatter_add(g, ra, name):
    _, R, C = g.shape
    tr = _rows_tile(R, 176)
    x, y, c, chips = _place()
    slots = jnp.stack([4 * px + 2 * py + c for px, py in [(x, y)] + chips]).astype(jnp.int32)

    def body(s_ref, g0, g1, g2, g3, ra_ref, own_ref, sb_ref):
        own_ref[...] = g0[...] + ra_ref[0].astype(F32)
        for j, gj in enumerate((g1, g2, g3)):
            sb_ref[j] = (gj[...] + ra_ref[j + 1].astype(F32)).astype(BF16)

    spec = pltpu.PrefetchScalarGridSpec(
        num_scalar_prefetch=1, grid=(R // tr,),
        in_specs=[pl.BlockSpec((None, tr, C), lambda i, s, j=j: (s[j], i, 0)) for j in range(4)]
        + [pl.BlockSpec((4, tr, C), lambda i, s: (0, i, 0))],
        out_specs=(pl.BlockSpec((tr, C), lambda i, s: (i, 0)), pl.BlockSpec((3, tr, C), lambda i, s: (0, i, 0))))
    return pl.pallas_call(body, name=name, grid_spec=spec, out_shape=(S((R, C), F32), S((3, R, C), BF16)),
                          compiler_params=_params(("parallel",)))(slots, g, g, g, g, ra)


def rms_cast(h, g, name, riders=()):
    T, D = h.shape
    tm = _tile(T, 512)

    def body(h_ref, g_ref, n_ref):
        n_ref[...] = _rms(h_ref[...], g_ref[...]).astype(BF16)

    row = pl.BlockSpec((tm, D), lambda i: (i, 0))
    return _call(body, name=name, grid=(T // tm,), out_shape=S((T, D), BF16), in_specs=[row, pl.BlockSpec((1, D), lambda i: (0, 0))],
                 out_specs=row, sem=("parallel",), args=(h, g), riders=riders)


def ffn_up_fwd(n, wgu, name, riders=()):
    T, D = n.shape
    F = wgu.shape[1]
    tr, tn = _tile(T, 512), _tile(F, 256)

    def body(n_ref, wg_ref, wu_ref, dgate_ref, dup_ref, a_ref):
        wg, wu = wg_ref[...], wu_ref[...]
        for r in range(T // tr):
            rows = slice(r * tr, (r + 1) * tr)
            x = n_ref[rows, :]
            gate = _nt(x, wg)
            up = _nt(x, wu)
            s = jax.nn.sigmoid(gate)
            silu = gate * s
            dgate_ref[rows, :] = (up * (s * (1.0 + gate * (1.0 - s)))).astype(BF16)
            dup_ref[rows, :] = silu.astype(BF16)
            a_ref[rows, :] = (silu * up).astype(BF16)

    tile = pl.BlockSpec((T, tn), lambda j: (0, j))
    return _call(
        body, name=name, grid=(F // tn,), out_shape=(S((T, F), BF16),) * 3,
        in_specs=[pl.BlockSpec((T, D), lambda j: (0, 0)),
                  pl.BlockSpec((None, tn, D), lambda j: (0, j, 0)), pl.BlockSpec((None, tn, D), lambda j: (1, j, 0))],
        out_specs=(tile, tile, tile), sem=("parallel",), args=(n, wgu, wgu), riders=riders)


def ffn_down_fwd(a, wd, h, g_next, name, riders=()):
    T, F = a.shape
    D = wd.shape[1]
    tm = _tile(T, 256)

    def body(a_ref, w_ref, h_ref, *rest):
        out = h_ref[...] + 0.5 * _nn(a_ref[...], w_ref[...])
        if g_next is None:
            rest[0][...] = out
        else:
            g_ref, o_ref, n_ref = rest
            o_ref[...] = out
            n_ref[...] = _rms(out, g_ref[...]).astype(BF16)

    row = pl.BlockSpec((tm, D), lambda i: (i, 0))
    more = g_next is not None
    return _call(
        body, name=name, grid=(T // tm,), out_shape=(S((T, D), F32), S((T, D), BF16)) if more else S((T, D), F32),
        in_specs=[pl.BlockSpec((tm, F), lambda i: (i, 0)), pl.BlockSpec((F, D), lambda i: (0, 0)), row]
        + ([pl.BlockSpec((1, D), lambda i: (0, 0))] if more else []),
        out_specs=(row, row) if more else row,
        sem=("parallel",), args=(a, wd, h) + ((g_next,) if more else ()), riders=riders)


def mix_in_fwd(h, g, win, name):
    T, D = h.shape
    N = win.shape[0]
    tm = _tile(T, 256)

    def body(h_ref, g_ref, w_ref, n_ref, p_ref):
        n = _rms(h_ref[...], g_ref[...]).astype(BF16)
        n_ref[...] = n
        p_ref[...] = _nt(n, w_ref[...]).astype(BF16)

    return pl.pallas_call(
        body, name=name, grid=(T // tm,), out_shape=(S((T, D), BF16), S((T, N), BF16)),
        in_specs=[pl.BlockSpec((tm, D), lambda i: (i, 0)), pl.BlockSpec((1, D), lambda i: (0, 0)),
                  pl.BlockSpec((N, D), lambda i: (0, 0))],
        out_specs=(pl.BlockSpec((tm, D), lambda i: (i, 0)), pl.BlockSpec((tm, N), lambda i: (i, 0))),
        compiler_params=_params(("parallel",)),
    )(h, g, win)


def _tri_consts():
    r = lax.broadcasted_iota(jnp.int32, (QB, QB), 0)
    c = lax.broadcasted_iota(jnp.int32, (QB, QB), 1)
    ones = jnp.ones((QB, QB), BF16)
    with_sums = lambda tri: jnp.concatenate([tri.astype(BF16), ones], axis=1)
    return with_sums(r > c), with_sums(r <= c), with_sums(r < c)


def _half_masks():
    lane = lax.broadcasted_iota(jnp.int32, (QB, PAIR), 1)
    row = lax.broadcasted_iota(jnp.int32, (QB, PAIR), 0)
    return lane < HEAD_DIM, lane, row


def sb_attn_fwd(p, after, name, riders=()):
    T = p.shape[0]
    nq = T // QB

    def body(q_ref, k_ref, v_ref, m_ref, o_ref, tot_ref, q_sc, acc_ref, z_sc):
        i = pl.program_id(0)
        lo, lane, row = _half_masks()
        causal = lane < row
        heads, pairs = range(SB_HEADS), range(SB_HEADS // 2)
        for hp in pairs:
            q_sc[hp] = (q_ref[:, hp * PAIR:(hp + 1) * PAIR].astype(F32) * SCALE).astype(BF16)
        m2 = m_ref[...]

        def by_head(ref, j, hp):
            t = ref[pl.ds(pl.multiple_of(j * QB, QB), QB), hp * PAIR:(hp + 1) * PAIR]
            return jnp.concatenate([jnp.where(lo, t, 0), jnp.where(lo, 0, t)], axis=0)

        def scores(j):
            return [_nt(q_sc[hp], by_head(k_ref, j, hp)) for hp in pairs]

        def block(j, diag):
            z2 = [z_sc[hp] for hp in pairs]
            ahead = scores(jnp.maximum(j - 1, 0))
            for hp in pairs:
                z_sc[hp] = ahead[hp]
            vs = [by_head(v_ref, j, hp) for hp in pairs]
            spls = [_softplus_logsig(z2[h // 2][:, (h % 2) * QB:(h % 2 + 1) * QB]) for h in heads]
            sp = [jnp.where(causal, spls[h][0], 0.0) if diag else spls[h][0] for h in heads]
            rr = _tri(sp, m2)
            if diag:
                w = [jnp.where(causal, jnp.exp(spls[h][1] - rr[h][:, :QB]), 0.0).astype(BF16) for h in heads]
            else:
                c = [tot_ref[:, h * QB:(h + 1) * QB] for h in heads]
                w = [jnp.exp(spls[h][1] - (c[h] + rr[h][:, :QB])).astype(BF16) for h in heads]
            pv = [_nn(jnp.concatenate([w[2 * hp], w[2 * hp + 1]], axis=1), vs[hp]) for hp in pairs]
            for hp in pairs:
                acc_ref[hp] = pv[hp] if diag else acc_ref[hp] + pv[hp]
            for h in heads:
                tot_ref[:, h * QB:(h + 1) * QB] = rr[h][:, QB:] if diag else c[h] + rr[h][:, QB:]

        first = scores(i)
        for hp in pairs:
            z_sc[hp] = first[hp]
        block(i, True)

        def step(t, carry):
            block(i - 1 - t, False)
            return carry
        lax.fori_loop(0, i, step, 0)
        for hp in pairs:
            o_ref[:, hp * PAIR:(hp + 1) * PAIR] = acc_ref[hp]

    npair = SB_HEADS // 2
    return _call(
        body, name=name, grid=(nq,), out_shape=(S((T, SB_W), F32), S((T, SB_HEADS * QB), F32)),
        in_specs=[pl.BlockSpec((QB, SB_W), lambda i: (i, 0)), pl.BlockSpec((T, SB_W), lambda i: (0, 1)),
                  pl.BlockSpec((T, SB_W), lambda i: (0, 2)), pl.BlockSpec((QB, 2 * QB), lambda i: (0, 0))],
        out_specs=(pl.BlockSpec((QB, SB_W), lambda i: (i, 0)), pl.BlockSpec((QB, SB_HEADS * QB), lambda i: (i, 0))),
        scratch=[pltpu.VMEM((npair, QB, PAIR), BF16), pltpu.VMEM((npair, QB, PAIR), F32), pltpu.VMEM((npair, QB, 2 * QB), F32)],
        sem=("arbitrary",), args=(p, p, p, after), riders=riders,
        marks=((11 * nq) // 16, (13 * nq) // 16))


def sb_attn_bwd(p, do, tot, upto, before, name, riders=()):
    T = p.shape[0]
    nq = T // QB

    def body(q_ref, k_ref, v_ref, do_ref, tot_ref, mp_ref, mg_ref, dq_ref, dk_ref, dv_ref,
             q_sc, d_sc, qd_sc, pg_sc, dq_acc, dk_acc, dv_acc, zd_sc):
        i = pl.program_id(0)
        lo, lane, row = _half_masks()
        causal = lane < row
        heads, pairs = range(SB_HEADS), range(SB_HEADS // 2)

        def by_head(t):
            return jnp.concatenate([jnp.where(lo, t, 0), jnp.where(lo, 0, t)], axis=0)

        for hp in pairs:
            q2 = (q_ref[:, hp * PAIR:(hp + 1) * PAIR].astype(F32) * SCALE).astype(BF16)
            d2 = do_ref[:, hp * PAIR:(hp + 1) * PAIR].astype(BF16)
            q_sc[hp] = q2
            d_sc[hp] = d2
            qd_sc[hp] = by_head(q2)
            qd_sc[SB_HEADS // 2 + hp] = by_head(d2)
        mp, mg = mp_ref[...], mg_ref[...]

        @pl.when(i == 0)
        def _():
            dk_acc[...] = jnp.zeros_like(dk_acc)
            dv_acc[...] = jnp.zeros_like(dv_acc)
        pg_sc[...] = jnp.zeros_like(pg_sc)
        dq_acc[...] = jnp.zeros_like(dq_acc)

        def rows(ref, j, hp):
            return ref[pl.ds(pl.multiple_of(j * QB, QB), QB), hp * PAIR:(hp + 1) * PAIR]

        def products(j):
            return ([_nt(q_sc[hp], by_head(rows(k_ref, j, hp))) for hp in pairs]
                    + [_nt(d_sc[hp], by_head(rows(v_ref, j, hp))) for hp in pairs])

        def block(j, diag):
            r0 = pl.multiple_of(j * QB, QB)
            half = lambda t, h: t[:, (h % 2) * QB:(h % 2 + 1) * QB]
            z = [half(zd_sc[h // 2], h) for h in heads]
            dw = [half(zd_sc[SB_HEADS // 2 + h // 2], h) for h in heads]
            if not diag:
                ahead = products(j + 1)
                for hp in range(SB_HEADS):
                    zd_sc[hp] = ahead[hp]
            ks = [by_head(rows(k_ref, j, hp)) for hp in pairs]
            spls = [_softplus_logsig(z[h]) for h in heads]
            sp = [jnp.where(causal, spls[h][0], 0.0) if diag else spls[h][0] for h in heads]
            rr = _tri(sp, mp)
            pc = [pg_sc[2 * h] for h in heads]
            w = [jnp.exp(spls[h][1] - (tot_ref[:, h * QB:(h + 1) * QB] - (pc[h] + rr[h][:, :QB]))) for h in heads]
            if diag:
                w = [jnp.where(causal, w[h], 0.0) for h in heads]
            gg = [dw[h] * w[h] for h in heads]
            rg = _tri(gg, mg)
            gc = [pg_sc[2 * h + 1] for h in heads]
            dz = [gg[h] - (gg[h] + gc[h] + rg[h][:, :QB]) * jnp.exp(spls[h][1]) for h in heads]
            if diag:
                dz = [jnp.where(causal, dz[h], 0.0) for h in heads]
            dzb = [dz[h].astype(BF16) for h in heads]
            wb = [w[h].astype(BF16) for h in heads]
            both = lambda t, hp, axis: jnp.concatenate([t[2 * hp], t[2 * hp + 1]], axis=axis)
            dq = [_nn(both(dzb, hp, 1), ks[hp]) for hp in pairs]
            dk = [_tn(both(dzb, hp, 0), qd_sc[hp]) for hp in pairs]
            dv = [_tn(both(wb, hp, 0), qd_sc[SB_HEADS // 2 + hp]) for hp in pairs]
            for h in heads:
                if not diag:
                    pg_sc[2 * h] = pc[h] + rr[h][:, QB:]
                    pg_sc[2 * h + 1] = gc[h] + rg[h][:, QB:]
            for hp in pairs:
                dq_acc[hp] += dq[hp]
                dk_acc[pl.ds(r0, QB), hp * PAIR:(hp + 1) * PAIR] += dk[hp]
                dv_acc[pl.ds(r0, QB), hp * PAIR:(hp + 1) * PAIR] += dv[hp]

        first = products(0)
        for hp in range(SB_HEADS):
            zd_sc[hp] = first[hp]

        def step(t, carry):
            block(t, False)
            return carry
        lax.fori_loop(0, i, step, 0)
        block(i, True)
        for hp in pairs:
            dq_ref[:, hp * PAIR:(hp + 1) * PAIR] = (dq_acc[hp] * SCALE).astype(BF16)

        @pl.when(i == nq - 1)
        def _():
            dk_ref[...] = dk_acc[...].astype(BF16)
            dv_ref[...] = dv_acc[...].astype(BF16)

    qtile = pl.BlockSpec((QB, SB_W), lambda i: (i, 0))
    whole = pl.BlockSpec((T, SB_W), lambda i: (0, 0))
    const = pl.BlockSpec((QB, 2 * QB), lambda i: (0, 0))
    return _call(
        body, name=name, grid=(nq,), out_shape=(S((T, SB_W), BF16),) * 3,
        in_specs=[qtile, pl.BlockSpec((T, SB_W), lambda i: (0, 1)), pl.BlockSpec((T, SB_W), lambda i: (0, 2)), qtile,
                  pl.BlockSpec((QB, SB_HEADS * QB), lambda i: (i, 0)), const, const],
        out_specs=(qtile, whole, whole),
        scratch=[pltpu.VMEM((SB_HEADS // 2, QB, PAIR), BF16), pltpu.VMEM((SB_HEADS // 2, QB, PAIR), BF16),
                 pltpu.VMEM((SB_HEADS, 2 * QB, PAIR), BF16),
                 pltpu.VMEM((2 * SB_HEADS, QB, QB), F32), pltpu.VMEM((SB_HEADS // 2, QB, PAIR), F32),
                 pltpu.VMEM((T, SB_W), F32), pltpu.VMEM((T, SB_W), F32), pltpu.VMEM((SB_HEADS, QB, 2 * QB), F32)],
        sem=("arbitrary",), args=(p, p, p, do, tot, upto, before), riders=riders)


def _t5_buckets():
    a = lax.broadcasted_iota(jnp.int32, (QB, QB), 0)
    c = lax.broadcasted_iota(jnp.int32, (QB, QB), 1)

    def bucket(dist):
        dist = jnp.maximum(dist, 0)
        max_exact = N_BUCKETS // 2
        d = jnp.maximum(dist, 1).astype(F32)
        large = max_exact + (jnp.log(d / max_exact) / math.log(MAX_DISTANCE / max_exact)
                             * (N_BUCKETS - max_exact)).astype(jnp.int32)
        large = jnp.minimum(large, N_BUCKETS - 1)
        return jnp.where(dist < max_exact, dist, large)

    return bucket(QB + a - c), bucket(a - c)


def _swa_common(i, kp_ref, kc_ref, vp_ref, vc_ref, bp_ref, bc_ref, rb_ref, bias_ref):
    lo, lane, row = _half_masks()

    @pl.when(i == 0)
    def _():
        for blk, b_ref in enumerate((bp_ref, bc_ref)):
            bk = b_ref[...]
            for h in range(8):
                acc = jnp.zeros((QB, QB), F32)
                for b in range(N_BUCKETS):
                    acc = jnp.where(bk == b, rb_ref[b, h], acc)
                bias_ref[h, blk] = acc

    band = [(lane > row) & (i > 0), lane <= row]

    def stacks(ref):
        t = ref[...].astype(F32)
        sw = pltpu.roll(t, HEAD_DIM, 1)
        return [jnp.concatenate([jnp.where(lo, t, 0.0), jnp.where(lo, 0.0, sw)], axis=0).astype(BF16),
                jnp.concatenate([jnp.where(lo, sw, 0.0), jnp.where(lo, 0.0, t)], axis=0).astype(BF16)]

    ks = [stacks(kp_ref), stacks(kc_ref)]
    vs = [stacks(vp_ref), stacks(vc_ref)]
    return lo, band, ks, vs


def _lane_half(t, h):
    return t[:, (h % 2) * QB:(h % 2 + 1) * QB]


def swa_fwd(p, sinks, rel_bias, bprev, bcur, name, riders=()):
    T = p.shape[0]
    nq = T // QB
    kcol, vcol = (3 * SB_W + SWA_W) // KV_W, (3 * SB_W + SWA_W) // KV_W + 1

    def body(q_ref, kp_ref, kc_ref, vp_ref, vc_ref, bp_ref, bc_ref, sink_ref, rb_ref, o_ref, lse_ref, bias_ref):
        i = pl.program_id(0)
        lo, band, ks, vs = _swa_common(i, kp_ref, kc_ref, vp_ref, vc_ref, bp_ref, bc_ref, rb_ref, bias_ref)
        heads, pairs, blocks = range(8), range(4), range(2)
        rowmax = lambda t: jnp.max(t, axis=1, keepdims=True)
        rowsum = lambda t: jnp.sum(t, axis=1, keepdims=True)
        q2 = [q_ref[:, g * PAIR:(g + 1) * PAIR] for g in pairs]
        s2 = [[_nt(q2[g], ks[b][g // 2]) for b in blocks] for g in pairs]
        sc = [[jnp.where(band[b], _lane_half(s2[h // 2][b], h) * SCALE + bias_ref[h, b], NEG_INF) for b in blocks] for h in heads]
        sink = [sink_ref[0, h] for h in heads]
        m = [jnp.maximum(jnp.maximum(rowmax(sc[h][0]), rowmax(sc[h][1])), sink[h]) for h in heads]
        e = [[jnp.exp(sc[h][b] - m[h]) for b in blocks] for h in heads]
        den = [rowsum(e[h][0]) + rowsum(e[h][1]) + jnp.exp(sink[h] - m[h]) for h in heads]
        pb = [[(e[h][b] / den[h]).astype(BF16) for b in blocks] for h in heads]
        for g in pairs:
            both = lambda b: jnp.concatenate([pb[2 * g][b], pb[2 * g + 1][b]], axis=1)
            o_ref[:, g * PAIR:(g + 1) * PAIR] = _nn(both(0), vs[0][g // 2]) + _nn(both(1), vs[1][g // 2])
        for h in heads:
            lse_ref[:, h * QB:(h + 1) * QB] = jnp.broadcast_to(m[h] + jnp.log(den[h]), (QB, QB))

    kv = lambda col, prev: pl.BlockSpec((QB, KV_W), (lambda i: (jnp.maximum(i - 1, 0), col)) if prev else (lambda i: (i, col)))
    full = pl.BlockSpec((QB, QB), lambda i: (0, 0))
    smem = pl.BlockSpec(memory_space=pltpu.SMEM)
    return _call(
        body, name=name, grid=(nq,), out_shape=(S((T, SWA_W), F32), S((T, 8 * QB), F32)),
        in_specs=[pl.BlockSpec((QB, SWA_W), lambda i: (i, 3)), kv(kcol, True), kv(kcol, False), kv(vcol, True), kv(vcol, False),
                  full, full, smem, smem],
        out_specs=(pl.BlockSpec((QB, SWA_W), lambda i: (i, 0)), pl.BlockSpec((QB, 8 * QB), lambda i: (i, 0))),
        scratch=[pltpu.VMEM((8, 2, QB, QB), F32)],
        sem=("arbitrary",), args=(p, p, p, p, p, bprev, bcur, sinks, rel_bias), riders=riders)


def swa_bwd(p, do, lse, sinks, rel_bias, bprev, bcur, name, riders=()):
    T = p.shape[0]
    nq = T // QB
    kcol, vcol = (3 * SB_W + SWA_W) // KV_W, (3 * SB_W + SWA_W) // KV_W + 1

    def body(q_ref, kp_ref, kc_ref, vp_ref, vc_ref, do_ref, lse_ref, bp_ref, bc_ref, sink_ref, rb_ref,
             dq_ref, dk_ref, dv_ref, dsink_ref, dsc_ref, bias_ref, dk_acc, dv_acc):
        i = pl.program_id(0)
        lo, band, ks, vs = _swa_common(i, kp_ref, kc_ref, vp_ref, vc_ref, bp_ref, bc_ref, rb_ref, bias_ref)

        @pl.when(i == 0)
        def _():
            dk_acc[...] = jnp.zeros_like(dk_acc)
            dv_acc[...] = jnp.zeros_like(dv_acc)
            dsc_ref[...] = jnp.zeros_like(dsc_ref)
            dsink_ref[...] = jnp.zeros_like(dsink_ref)

        heads, pairs, blocks = range(8), range(4), range(2)
        rowsum = lambda t: jnp.sum(t, axis=1, keepdims=True)
        by_head = lambda t: jnp.concatenate([jnp.where(lo, t, 0), jnp.where(lo, 0, t)], axis=0)
        q2 = [q_ref[:, g * PAIR:(g + 1) * PAIR] for g in pairs]
        d2 = [do_ref[:, g * PAIR:(g + 1) * PAIR].astype(BF16) for g in pairs]
        qs = [by_head(q2[g]) for g in pairs]
        dos = [by_head(d2[g]) for g in pairs]
        s2 = [[_nt(q2[g], ks[b][g // 2]) for b in blocks] for g in pairs]
        dp2 = [[_nt(d2[g], vs[b][g // 2]) for b in blocks] for g in pairs]
        lse_h = [lse_ref[:, h * QB:(h + 1) * QB] for h in heads]
        sink = [sink_ref[0, h] for h in heads]
        pr = [[jnp.exp(jnp.where(band[b], _lane_half(s2[h // 2][b], h) * SCALE + bias_ref[h, b], NEG_INF) - lse_h[h])
               for b in blocks] for h in heads]
        dp = [[_lane_half(dp2[h // 2][b], h) for b in blocks] for h in heads]
        delta = [rowsum(pr[h][0] * dp[h][0]) + rowsum(pr[h][1] * dp[h][1]) for h in heads]
        lane1 = lax.broadcasted_iota(jnp.int32, (1, QB), 1)
        dsink = jnp.zeros((1, QB), F32)
        for h in heads:
            dsink = dsink + jnp.where(lane1 == h, -jnp.sum(jnp.exp(sink[h] - lse_h[h][:, :1]) * delta[h]), 0.0)
        dsink_ref[...] += dsink
        dsc = [[pr[h][b] * (dp[h][b] - delta[h]) for b in blocks] for h in heads]
        for h in heads:
            for b in blocks:
                dsc_ref[h, b] += dsc[h][b]
        dzb = [[(dsc[h][b] * SCALE).astype(BF16) for b in blocks] for h in heads]
        prb = [[pr[h][b].astype(BF16) for b in blocks] for h in heads]
        pair_of = lambda t, g, b, axis: jnp.concatenate([t[2 * g][b], t[2 * g + 1][b]], axis=axis)
        for g in pairs:
            dq = _nn(pair_of(dzb, g, 0, 1), ks[0][g // 2]) + _nn(pair_of(dzb, g, 1, 1), ks[1][g // 2])
            dq_ref[:, g * PAIR:(g + 1) * PAIR] = dq.astype(BF16)

        def key_grad(t, other, b):
            per_kv = [_tn(pair_of(t, 2 * kh, b, 0), other[2 * kh]) + _tn(pair_of(t, 2 * kh + 1, b, 0), other[2 * kh + 1]) for kh in range(2)]
            both = [s + pltpu.roll(s, HEAD_DIM, 1) for s in per_kv]
            return jnp.where(lo, both[0], both[1])

        rp = pl.multiple_of(jnp.maximum(i - 1, 0) * QB, QB)
        rc = pl.multiple_of(i * QB, QB)
        dk_acc[pl.ds(rp, QB), :] += key_grad(dzb, qs, 0)
        dv_acc[pl.ds(rp, QB), :] += key_grad(prb, dos, 0)
        dk_acc[pl.ds(rc, QB), :] += key_grad(dzb, qs, 1)
        dv_acc[pl.ds(rc, QB), :] += key_grad(prb, dos, 1)

        @pl.when(i == nq - 1)
        def _():
            dk_ref[...] = dk_acc[...].astype(BF16)
            dv_ref[...] = dv_acc[...].astype(BF16)

    kv = lambda col, prev: pl.BlockSpec((QB, KV_W), (lambda i: (jnp.maximum(i - 1, 0), col)) if prev else (lambda i: (i, col)))
    full = pl.BlockSpec((QB, QB), lambda i: (0, 0))
    smem = pl.BlockSpec(memory_space=pltpu.SMEM)
    whole = lambda shape: pl.BlockSpec(shape, lambda i: (0,) * len(shape))
    return _call(
        body, name=name, grid=(nq,),
        out_shape=(S((T, SWA_W), BF16), S((T, KV_W), BF16), S((T, KV_W), BF16), S((1, QB), F32), S((8, 2, QB, QB), F32)),
        in_specs=[pl.BlockSpec((QB, SWA_W), lambda i: (i, 3)), kv(kcol, True), kv(kcol, False), kv(vcol, True), kv(vcol, False),
                  pl.BlockSpec((QB, SWA_W), lambda i: (i, 0)), pl.BlockSpec((QB, 8 * QB), lambda i: (i, 0)),
                  full, full, smem, smem],
        out_specs=(pl.BlockSpec((QB, SWA_W), lambda i: (i, 0)), whole((T, KV_W)), whole((T, KV_W)), whole((1, QB)),
                   whole((8, 2, QB, QB))),
        scratch=[pltpu.VMEM((8, 2, QB, QB), F32), pltpu.VMEM((T, KV_W), F32), pltpu.VMEM((T, KV_W), F32)],
        sem=("arbitrary",), args=(p, p, p, p, p, do, lse, bprev, bcur, sinks, rel_bias), riders=riders)


def mix_out_fwd(o_sb, o_sw, g_sb, g_sw, wout, h, g_next, name, riders=()):
    T, D = h.shape
    M = SB_W + SWA_W
    tm = _tile(T, 256)

    def body(a_ref, b_ref, ga_ref, gb_ref, w_ref, h_ref, gn_ref, mx_ref, o_ref, n_ref):
        mx_ref[:, :SB_W] = _rms(a_ref[...], ga_ref[...]).astype(BF16)
        mx_ref[:, SB_W:] = _rms(b_ref[...], gb_ref[...]).astype(BF16)
        out = h_ref[...] + _nn(mx_ref[...], w_ref[...])
        o_ref[...] = out
        n_ref[...] = _rms(out, gn_ref[...]).astype(BF16)

    row = lambda n: pl.BlockSpec((tm, n), lambda i: (i, 0))
    vec = lambda n: pl.BlockSpec((1, n), lambda i: (0, 0))
    return _call(
        body, name=name, grid=(T // tm,), out_shape=(S((T, M), BF16), S((T, D), F32), S((T, D), BF16)),
        in_specs=[row(SB_W), row(SWA_W), vec(SB_W), vec(SWA_W), pl.BlockSpec((M, D), lambda i: (0, 0)), row(D), vec(D)],
        out_specs=(row(M), row(D), row(D)),
        sem=("parallel",), args=(o_sb, o_sw, g_sb, g_sw, wout, h, g_next), riders=riders)


def loss_head(h, g, target, name):
    T, D = h.shape
    tm = _tile(T, 256)

    def body(h_ref, g_ref, t_ref, loss_ref, dh_ref, dhb_ref, dg_ref):
        @pl.when(pl.program_id(0) == 0)
        def _():
            loss_ref[...] = jnp.zeros_like(loss_ref)
            dg_ref[...] = jnp.zeros_like(dg_ref)
        x = h_ref[...]
        err = _rms(x, g_ref[...]) - t_ref[...]
        loss_ref[...] += jnp.full((1, QB), 0.5 * jnp.sum(jnp.mean(err * err, axis=-1)), F32)
        dx, dg = _rms_bwd(err / D, x, g_ref[...])
        dh_ref[...] = dx
        dhb_ref[...] = dx.astype(BF16)
        dg_ref[...] += dg

    row = pl.BlockSpec((tm, D), lambda i: (i, 0))
    vec = pl.BlockSpec((1, D), lambda i: (0, 0))
    return pl.pallas_call(
        body, name=name, grid=(T // tm,), out_shape=(S((1, QB), F32), S((T, D), F32), S((T, D), BF16), S((1, D), F32)),
        in_specs=[row, vec, row], out_specs=(pl.BlockSpec((1, QB), lambda i: (0, 0)), row, row, vec),
        compiler_params=_params(("arbitrary",)),
    )(h, g, target)


def ffn_down_bwd(dhb, wd, gate, up, name, riders=()):
    T, D = dhb.shape
    F = wd.shape[0]
    tr, tn = _tile(T, 512), _tile(F, 256)

    def body(d_ref, w_ref, g_ref, u_ref, o_ref):
        w = w_ref[...]
        for r in range(T // tr):
            rows = slice(r * tr, (r + 1) * tr)
            da = 0.5 * _nt(d_ref[rows, :], w)
            o_ref[0, rows, :] = (da * g_ref[rows, :].astype(F32)).astype(BF16)
            o_ref[1, rows, :] = (da * u_ref[rows, :].astype(F32)).astype(BF16)

    tile = pl.BlockSpec((T, tn), lambda j: (0, j))
    return _call(
        body, name=name, grid=(F // tn,), out_shape=S((2, T, F), BF16),
        in_specs=[pl.BlockSpec((T, D), lambda j: (0, 0)), pl.BlockSpec((tn, D), lambda j: (j, 0)), tile, tile],
        out_specs=pl.BlockSpec((2, T, tn), lambda j: (0, 0, j)),
        sem=("parallel",), args=(dhb, wd, gate, up), riders=riders)


def tn_matmul(xs, y, alpha, name, riders=()):
    B, T, N = xs.shape
    D = y.shape[1]
    tn = _tile(N, 256)

    def body(x_ref, y_ref, o_ref, ob_ref):
        o = alpha * _tn(x_ref[...], y_ref[...])
        o_ref[...] = o
        ob_ref[...] = o.astype(BF16)

    tile = pl.BlockSpec((None, tn, D), lambda s, j: (s, j, 0))
    return _call(
        body, name=name, grid=(B, N // tn), out_shape=(S((B, N, D), F32), S((B, N, D), BF16)),
        in_specs=[pl.BlockSpec((None, T, tn), lambda s, j: (s, 0, j)), pl.BlockSpec((T, D), lambda s, j: (0, 0))],
        out_specs=(tile, tile), sem=("parallel", "parallel"), args=(xs, y), riders=riders)


def nn_rms_bwd(xs, ws, h_in, g, dh, name, riders=()):
    B, T, K = xs.shape
    D = ws.shape[2]
    tm = _tile(T, 256)

    def body(x_ref, w_ref, h_ref, g_ref, d_ref, o_ref, ob_ref, dg_ref):
        @pl.when(pl.program_id(0) == 0)
        def _():
            dg_ref[...] = jnp.zeros_like(dg_ref)
        dn = _nn(x_ref[0], w_ref[0])
        for s in range(1, B):
            dn = dn + _nn(x_ref[s], w_ref[s])
        dx, dg = _rms_bwd(dn, h_ref[...], g_ref[...])
        out = d_ref[...] + dx
        o_ref[...] = out
        ob_ref[...] = out.astype(BF16)
        dg_ref[...] += dg

    row = pl.BlockSpec((tm, D), lambda i: (i, 0))
    vec = pl.BlockSpec((1, D), lambda i: (0, 0))
    return _call(
        body, name=name, grid=(T // tm,), out_shape=(S((T, D), F32), S((T, D), BF16), S((1, D), F32)),
        in_specs=[pl.BlockSpec((B, tm, K), lambda i: (0, i, 0)), pl.BlockSpec((B, K, D), lambda i: (0, 0, 0)), row, vec, row],
        out_specs=(row, row, vec),
        sem=("arbitrary",), args=(xs, ws, h_in, g, dh), riders=riders)


def mix_out_bwd(dhb, wout, o_sb, o_sw, g_sb, g_sw, name):
    T, D = dhb.shape
    tm = _tile(T, 256)

    def body(d_ref, w_ref, a_ref, b_ref, ga_ref, gb_ref, da_ref, db_ref, dga_ref, dgb_ref):
        @pl.when(pl.program_id(0) == 0)
        def _():
            dga_ref[...] = jnp.zeros_like(dga_ref)
            dgb_ref[...] = jnp.zeros_like(dgb_ref)
        dm = _nt(d_ref[...], w_ref[...])
        dxa, dga = _rms_bwd(dm[:, :SB_W], a_ref[...], ga_ref[...])
        dxb, dgb = _rms_bwd(dm[:, SB_W:], b_ref[...], gb_ref[...])
        da_ref[...] = dxa
        db_ref[...] = dxb
        dga_ref[...] += dga
        dgb_ref[...] += dgb

    row = lambda n: pl.BlockSpec((tm, n), lambda i: (i, 0))
    vec = lambda n: pl.BlockSpec((1, n), lambda i: (0, 0))
    return pl.pallas_call(
        body, name=name, grid=(T // tm,),
        out_shape=(S((T, SB_W), F32), S((T, SWA_W), F32), S((1, SB_W), F32), S((1, SWA_W), F32)),
        in_specs=[row(D), pl.BlockSpec((SB_W + SWA_W, D), lambda i: (0, 0)), row(SB_W), row(SWA_W), vec(SB_W), vec(SWA_W)],
        out_specs=(row(SB_W), row(SWA_W), vec(SB_W), vec(SWA_W)),
        compiler_params=_params(("arbitrary",)),
    )(dhb, wout, o_sb, o_sw, g_sb, g_sw)


def rel_bias_grad(dscs, bprev, bcur, name):
    n = len(dscs)

    def body(*refs):
        bp_ref, bc_ref, o_ref = refs[n], refs[n + 1], refs[n + 2]
        bks = [bp_ref[...], bc_ref[...]]
        row = lax.broadcasted_iota(jnp.int32, (N_BUCKETS, QB), 0)
        lane = lax.broadcasted_iota(jnp.int32, (N_BUCKETS, QB), 1)
        out = jnp.zeros((N_BUCKETS, QB), F32)
        for h in range(8):
            tot = [sum(refs[l][h, b] for l in range(n)) for b in range(2)]
            for b in range(N_BUCKETS):
                val = jnp.sum(jnp.where(bks[0] == b, tot[0], 0.0)) + jnp.sum(jnp.where(bks[1] == b, tot[1], 0.0))
                out = jnp.where((row == b) & (lane == h), val, out)
        o_ref[...] = out

    return pl.pallas_call(body, name=name, out_shape=S((N_BUCKETS, QB), F32), compiler_params=_params())(*dscs, bprev, bcur)


def _adamw(w, g, m, v):
    m = ADAM_B1 * m + (1.0 - ADAM_B1) * g
    v = ADAM_B2 * v + (1.0 - ADAM_B2) * (g * g)
    m_hat = m / (1.0 - ADAM_B1 ** ADAM_STEP)
    v_hat = v / (1.0 - ADAM_B2 ** ADAM_STEP)
    delta = -ADAM_LR * (m_hat / (jnp.sqrt(v_hat) + ADAM_EPS) + ADAM_WD * w)
    return delta, m, v


def adamw_scattered(w, m, v, owns, others, name, riders=()):
    L, R, C = w.shape
    tr = _rows_tile(R, 176)

    def body(w_ref, m_ref, v_ref, *rest):
        own_refs, other_refs = rest[:L], rest[L:2 * L]
        g_ref, d_ref, mo_ref, vo_ref = rest[2 * L:]
        layer = pl.program_id(0)

        def grad(k):
            o = other_refs[k]
            return own_refs[k][...] + o[0].astype(F32) + o[1].astype(F32) + o[2].astype(F32)

        g = grad(0)
        for k in range(1, L):
            g = jnp.where(layer == k, grad(k), g)
        d, mn, vn = _adamw(w_ref[...], g, m_ref[...], v_ref[...])
        g_ref[...] = g
        d_ref[...] = d
        mo_ref[...] = mn
        vo_ref[...] = vn

    tile = pl.BlockSpec((None, tr, C), lambda l, i: (l, i, 0))
    return _call(
        body, name=name, grid=(L, R // tr), out_shape=(S((L, R, C), F32),) * 4,
        in_specs=[tile] * 3 + [pl.BlockSpec((tr, C), lambda l, i: (i, 0))] * L + [pl.BlockSpec((3, tr, C), lambda l, i: (0, i, 0))] * L,
        out_specs=(tile,) * 4, sem=("parallel", "parallel"), args=(w, m, v, *owns, *others), riders=riders)


def adamw_small(w, gs, m, v, name):
    R, C = w.shape

    def body(w_ref, g_ref, m_ref, v_ref, go_ref, d_ref, mo_ref, vo_ref):
        g = g_ref[0]
        for k in range(1, N_DEV):
            g = g + g_ref[k]
        d, mn, vn = _adamw(w_ref[...], g, m_ref[...], v_ref[...])
        go_ref[...] = g
        d_ref[...] = d
        mo_ref[...] = mn
        vo_ref[...] = vn

    return pl.pallas_call(body, name=name, out_shape=(S((R, C), F32),) * 4, compiler_params=_params())(w, gs, m, v)


def kernel(x, norm_ffn1, w_ffn1_gu, w_ffn1_down, norm_mix, w_in, sinks, norm_out_sb, norm_out_swa, w_out, norm_ffn2, w_ffn2_gu, w_ffn2_down, rel_bias, norm_final, loss_target, m_norm_ffn1, m_w_ffn1_gu, m_w_ffn1_down, m_norm_mix, m_w_in, m_sinks, m_norm_out_sb, m_norm_out_swa, m_w_out, m_norm_ffn2, m_w_ffn2_gu, m_w_ffn2_down, m_rel_bias, m_norm_final, v_norm_ffn1, v_w_ffn1_gu, v_w_ffn1_down, v_norm_mix, v_w_in, v_sinks, v_norm_out_sb, v_norm_out_swa, v_w_out, v_norm_ffn2, v_w_ffn2_gu, v_w_ffn2_down, v_rel_bias, v_norm_final):
    L = norm_ffn1.shape[0]
    T, D = x.shape[1], x.shape[2]
    F = w_ffn1_down.shape[1] * N_DEV
    h = x.reshape(T, D)
    target = loss_target.reshape(T, D)
    after, upto, before = _tri_consts()
    bprev, bcur = _t5_buckets()

    local = {}
    for l in range(L):
        local[f"gu1_{l}"] = w_ffn1_gu[l].T.astype(BF16)
        local[f"d1_{l}"] = w_ffn1_down[l].astype(BF16)
        local[f"in_{l}"] = w_in[l].T.astype(BF16)
        local[f"out_{l}"] = w_out[l].astype(BF16)
        local[f"gu2_{l}"] = w_ffn2_gu[l].T.astype(BF16)
        local[f"d2_{l}"] = w_ffn2_down[l].astype(BF16)
    full, partial = {}, {}
    grads, chip_sum, recv_b = {}, {}, {}

    def run(fn, *args, ag=(), rs1=(), rs2=()):
        halves = lambda names: [n if isinstance(n, tuple) else (n, None) for n in names]
        ag, rs2 = [(n, k) for n, k in halves(ag) if n in local], halves(rs2)
        rows = lambda k, total: None if k is None else (k * (total // 2), total // 2)

        def second(n, k):
            sb = chip_sum[n][1]
            return scatter_second(sb, rows(k, sb.shape[1]), recv_b.get(n))

        riders = ([gather(local[n], rows(k, local[n].shape[0]), partial.get(n)) for n, k in ag]
                  + [scatter_first(grads[n][1]) for n in rs1] + [second(n, k) for n, k in rs2])
        if not riders:
            return fn(*args)
        outs, per = fn(*args, riders=riders)
        per = [p[0] for p in per]
        for n, k in ag:
            buf = per.pop(0)
            if k == 0:
                partial[n] = buf
            else:
                full[n] = buf.reshape(N_DEV * buf.shape[1], D)
        for n in rs1:
            chip_sum[n] = scatter_add(grads[n][0], per.pop(0), f"rs_add_{n}")
        for n, _ in rs2:
            recv_b[n] = per.pop(0)
        return outs

    gu = lambda n: full[n].reshape(2, F, D)
    slots = lambda pair: tuple(t.reshape(N_DEV, -1, D) for t in pair)
    vec = lambda a: a.reshape(1, -1)

    PW = max(D, SB_W + SWA_W)
    n_rows = 4 * L + 2
    n_rows += (-n_rows) % 8

    def pack(ffn1, mix, ffn2, final, osb, osw, snk, rel, extra):
        pieces = []

        def row(*parts):
            flat = [a.reshape(-1) for a in parts]
            pieces.extend(flat)
            used = sum(a.size for a in flat)
            if used < PW:
                pieces.append(jnp.zeros((PW - used,), F32))

        for group in (ffn1, mix, ffn2):
            for l in range(L):
                row(group[l])
        row(final)
        for l in range(L):
            row(osb[l], osw[l])
        row(*[snk[l].reshape(-1)[:8] for l in range(L)], rel, extra)
        pieces.append(jnp.zeros(((n_rows - 4 * L - 2) * PW,), F32))
        return jnp.concatenate(pieces).reshape(n_rows, PW)

    def unpack(arr):
        ffn1, mix, ffn2 = arr[0:L, :D], arr[L:2 * L, :D], arr[2 * L:3 * L, :D]
        final = arr[3 * L, :D]
        ob = arr[3 * L + 1:4 * L + 1]
        tail = arr[4 * L + 1]
        return (ffn1, mix, tail[:8 * L].reshape(L, 8), ob[:, :SB_W], ob[:, SB_W:SB_W + SWA_W], ffn2,
                tail[8 * L:8 * L + N_BUCKETS * 8].reshape(N_BUCKETS, 8), final)

    zero = jnp.zeros((1,), F32)
    w_small = pack(norm_ffn1, norm_mix, norm_ffn2, norm_final, norm_out_sb, norm_out_swa, sinks, rel_bias, zero)
    norm_ffn1, norm_mix, sinks, norm_out_sb, norm_out_swa, norm_ffn2, _, norm_final = unpack(w_small)

    saved = []
    n_next = run(rms_cast, h, vec(norm_ffn1[0]), "rms_first", ag=("gu1_0",))
    for l in range(L):
        nx = l + 1
        s = {"h0": h, "n1": n_next}
        s["gate1"], s["up1"], s["a1"] = run(ffn_up_fwd, s["n1"], gu(f"gu1_{l}"), f"ffn1_up{l}",
                                            ag=(f"d1_{l}", ("in_0", 0) if l == 0 else (f"in_{l}", 1)))
        h = run(ffn_down_fwd, s["a1"], full[f"d1_{l}"], h, None, f"ffn1_down{l}", ag=(("in_0", 1),) if l == 0 else ())
        s["h1"] = h
        s["n2"], s["p"] = mix_in_fwd(h, vec(norm_mix[l]), full[f"in_{l}"], f"mix_in{l}")
        s["o_sb"], s["tot"] = run(sb_attn_fwd, s["p"], after, f"sb_fwd{l}", ag=(f"out_{l}", f"gu2_{l}", f"d2_{l}"))
        s["o_sw"], s["lse"] = run(swa_fwd, s["p"], vec(sinks[l]), rel_bias, bprev, bcur, f"swa_fwd{l}", ag=((f"gu1_{nx}", 0),))
        s["mixed"], h, s["n3"] = run(mix_out_fwd, s["o_sb"], s["o_sw"], vec(norm_out_sb[l]), vec(norm_out_swa[l]),
                                     full[f"out_{l}"], h, vec(norm_ffn2[l]), f"mix_out{l}")
        s["h2"] = h
        s["gate2"], s["up2"], s["a2"] = run(ffn_up_fwd, s["n3"], gu(f"gu2_{l}"), f"ffn2_up{l}",
                                            ag=((f"gu1_{nx}", 1), (f"in_{nx}", 0)))
        if nx < L:
            h, n_next = run(ffn_down_fwd, s["a2"], full[f"d2_{l}"], h, vec(norm_ffn1[nx]), f"ffn2_down{l}")
        else:
            h = run(ffn_down_fwd, s["a2"], full[f"d2_{l}"], h, None, f"ffn2_down{l}")
        saved.append(s)

    loss_part, dh, dhb, dg_final = loss_head(h, vec(norm_final), target, "loss_head")

    small = {k: [None] * L for k in ("ffn1", "mix", "sinks", "osb", "osw", "ffn2", "dsc")}
    for l in reversed(range(L)):
        s = saved[l]

        def ffn_bwd(dh, dhb, tag, gate, up, a, n, h_in, g, r_down, r_dwgu, r_dwd, r_up):
            gu_n, d_n = f"gu{tag}_{l}", f"d{tag}_{l}"
            dgu = run(ffn_down_bwd, dhb, full[d_n], gate, up, f"ffn{tag}_down_bwd{l}", **r_down)
            grads[gu_n] = slots(run(tn_matmul, dgu, n, 1.0, f"ffn{tag}_dwgu{l}", **r_dwgu))
            grads[d_n] = slots(run(tn_matmul, a[None], dhb, 0.5, f"ffn{tag}_dwd{l}", **r_dwd))
            return run(nn_rms_bwd, dgu, gu(gu_n), h_in, g, dh, f"ffn{tag}_up_bwd{l}", **r_up)

        later = l + 1 < L
        dh, dhb, small["ffn2"][l] = ffn_bwd(dh, dhb, 2, s["gate2"], s["up2"], s["a2"], s["n3"], s["h2"], vec(norm_ffn2[l]),
                                            {}, dict(rs2=(f"d1_{l + 1}",) if later else ()), {},
                                            dict(rs1=(f"gu2_{l}", f"d2_{l}"), rs2=((f"gu1_{l + 1}", 1),) if later else ()))
        do_sb, do_sw, small["osb"][l], small["osw"][l] = mix_out_bwd(
            dhb, full[f"out_{l}"], s["o_sb"], s["o_sw"], vec(norm_out_sb[l]), vec(norm_out_swa[l]), f"mix_out_bwd{l}")
        grads[f"out_{l}"] = slots(tn_matmul(s["mixed"][None], dhb, 1.0, f"dwout{l}"))
        dq_sb, dk_sb, dv_sb = run(sb_attn_bwd, s["p"], do_sb, s["tot"], upto, before, f"sb_bwd{l}",
                                  rs2=(f"gu2_{l}", f"d2_{l}"), rs1=(f"out_{l}",))
        dq_sw, dk_sw, dv_sw, small["sinks"][l], small["dsc"][l] = swa_bwd(
            s["p"], do_sw, s["lse"], vec(sinks[l]), rel_bias, bprev, bcur, f"swa_bwd{l}")
        dp = jnp.concatenate([dq_sb, dk_sb, dv_sb, dq_sw, dk_sw, dv_sw], axis=1)
        dh, dhb, small["mix"][l] = nn_rms_bwd(dp[None], full[f"in_{l}"][None], s["h1"], vec(norm_mix[l]), dh, f"mix_in_bwd{l}")
        grads[f"in_{l}"] = slots(tn_matmul(dp[None], s["n2"], 1.0, f"dwin{l}"))
        dh, dhb, small["ffn1"][l] = ffn_bwd(dh, dhb, 1, s["gate1"], s["up1"], s["a1"], s["n1"], s["h0"], vec(norm_ffn1[l]),
                                            dict(rs1=(f"in_{l}",), rs2=(f"out_{l}",)), dict(rs2=(f"in_{l}",)),
                                            dict(rs1=(f"gu1_{l}",)), dict(rs1=(f"d1_{l}",), rs2=((f"gu1_{l}", 0),)))

    grad_x = dh.reshape(x.shape)

    upd = {}
    for nm, w, m, v, transposed, last in (
            ("gu2", w_ffn2_gu, m_w_ffn2_gu, v_w_ffn2_gu, True, (("gu1_0", 1), "d1_0")), ("d2", w_ffn2_down, m_w_ffn2_down, v_w_ffn2_down, False, ()),
            ("in", w_in, m_w_in, v_w_in, True, ()), ("out", w_out, m_w_out, v_w_out, False, ()),
            ("gu1", w_ffn1_gu, m_w_ffn1_gu, v_w_ffn1_gu, True, ()), ("d1", w_ffn1_down, m_w_ffn1_down, v_w_ffn1_down, False, ())):
        turn = (lambda a: jnp.swapaxes(a, 1, 2)) if transposed else (lambda a: a)
        names = [f"{nm}_{l}" for l in range(L)]
        res = run(adamw_scattered, turn(w), turn(m), turn(v), [chip_sum[n][0] for n in names], [recv_b[n] for n in names],
                  f"adamw_{nm}", rs2=last)
        upd[nm] = tuple(turn(r) for r in res)

    d_rel = rel_bias_grad(small["dsc"], bprev, bcur, "rel_bias_grad")[:, :8]
    g_small = pack(small["ffn1"], small["mix"], small["ffn2"], dg_final, small["osb"], small["osw"], small["sinks"], d_rel,
                   loss_part[0, :1])
    m_small = pack(m_norm_ffn1, m_norm_mix, m_norm_ffn2, m_norm_final, m_norm_out_sb, m_norm_out_swa, m_sinks, m_rel_bias, zero)
    v_small = pack(v_norm_ffn1, v_norm_mix, v_norm_ffn2, v_norm_final, v_norm_out_sb, v_norm_out_swa, v_sinks, v_rel_bias, zero)
    gs_small = all_gather_rows(g_small, "ag_small")
    summed = adamw_small(w_small, gs_small, m_small, v_small, "adamw_small")
    small_out = [unpack(a) for a in summed]
    loss = summed[0][4 * L + 1, 8 * L + N_BUCKETS * 8]

    def group(k):
        sm = small_out[k]
        return (sm[0], upd["gu1"][k], upd["d1"][k], sm[1], upd["in"][k], sm[2], sm[3], sm[4], upd["out"][k], sm[5],
                upd["gu2"][k], upd["d2"][k], sm[6], sm[7])

    return (loss, grad_x, *group(0), *group(1), *group(2), *group(3))
```

```python
import math

import jax
import jax.numpy as jnp
from jax import lax
from jax.experimental import pallas as pl
from jax.experimental.pallas import tpu as pltpu

F32 = jnp.float32
BF16 = jnp.bfloat16
S = jax.ShapeDtypeStruct

N_DEV = 8
HEAD_DIM = 64
SB_HEADS = 8
PAIR = 2 * HEAD_DIM
SB_W = 512
SWA_W = 512
KV_W = 128
IN_W = 3 * SB_W + SWA_W + 2 * KV_W
QB = 128
N_BUCKETS = 32
MAX_DISTANCE = 128
EPS = 1e-6
NEG_INF = -1e30
SCALE = HEAD_DIM ** -0.5

ADAM_LR = 0.001
ADAM_B1 = 0.9
ADAM_B2 = 0.999
ADAM_EPS = 1e-08
ADAM_WD = 0.01
ADAM_STEP = 10

VMEM_LIMIT = 56 * 1024 * 1024
MESH = pl.DeviceIdType.MESH


def _params(sem=None, vmem=VMEM_LIMIT):
    return pltpu.CompilerParams(dimension_semantics=sem, vmem_limit_bytes=vmem)


def _nn(a, b):
    return jnp.dot(a, b, preferred_element_type=F32)


def _nt(a, b):
    return lax.dot_general(a, b, (((1,), (1,)), ((), ())), preferred_element_type=F32)


def _tn(a, b):
    return lax.dot_general(a, b, (((0,), (0,)), ((), ())), preferred_element_type=F32)


def _tri(xs, m):
    return [_nn(x.astype(BF16), m) for x in xs]


def _rms(x, g):
    r = lax.rsqrt(jnp.mean(x * x, axis=-1, keepdims=True) + EPS)
    return x * r * g


def _rms_bwd(dy, x, g):
    r = lax.rsqrt(jnp.mean(x * x, axis=-1, keepdims=True) + EPS)
    xhat = x * r
    u = dy * g
    dx = r * (u - xhat * jnp.mean(u * xhat, axis=-1, keepdims=True))
    return dx, jnp.sum(dy * xhat, axis=0, keepdims=True)


def _softplus_logsig(z):
    sp = jnp.maximum(z, 0.0) + jnp.log(1.0 + jnp.exp(-jnp.abs(z)))
    return sp, z - sp


def _tile(n, want):
    t = min(n, want)
    while n % t:
        t //= 2
    return t


def _place():
    x, y, c = lax.axis_index("x"), lax.axis_index("y"), lax.axis_index("c")
    chips = [(1 - x, y), (x, 1 - y), (1 - x, 1 - y)]
    return x, y, c, chips


def all_gather_rows(v, name):
    R, C = v.shape

    def body(v_ref, out_ref, send_sems, recv_sems, local_sem):
        x, y, c, chips = _place()
        me, sibling = (x, y, c), (x, y, 1 - c)

        def slot(px, py, pc):
            return out_ref.at[4 * px + 2 * py + pc]

        def copy(k, block, to, src=None):
            return pltpu.make_async_remote_copy(
                src_ref=slot(*block) if src is None else src, dst_ref=slot(*block),
                send_sem=send_sems.at[k], recv_sem=recv_sems.at[k], device_id=to, device_id_type=MESH)

        mine = pltpu.make_async_copy(v_ref, slot(*me), local_sem)
        mine.start()
        first = [copy(0, me, sibling, src=v_ref)]
        first += [copy(1 + j, me, (*chip, c), src=v_ref) for j, chip in enumerate(chips)]
        for cp in first:
            cp.start()
        passed = [copy(4 + j, (*chip, c), sibling) for j, chip in enumerate(chips)]
        for j, chip in enumerate(chips):
            copy(1 + j, (*chip, c), me).wait_recv()
            passed[j].start()
        copy(0, sibling, me).wait_recv()
        for j, chip in enumerate(chips):
            copy(4 + j, (*chip, 1 - c), me).wait_recv()
        for cp in first + passed:
            cp.wait_send()
        mine.wait()

    return pl.pallas_call(
        body, name=name, out_shape=S((N_DEV, R, C), v.dtype),
        in_specs=[pl.BlockSpec(memory_space=pl.ANY)], out_specs=pl.BlockSpec(memory_space=pl.ANY),
        scratch_shapes=[pltpu.SemaphoreType.DMA((7,)), pltpu.SemaphoreType.DMA((7,)), pltpu.SemaphoreType.DMA],
    )(v)


class _Exchange:
    def __init__(self, ins, outs, sizes, n_local, plan, aliases=None):
        self.ins, self.outs, self.plan, self.aliases = list(ins), list(outs), plan, aliases or {}
        self.sizes, self.n_local = list(sizes), n_local

    def scratch(self):
        n = sum(self.sizes)
        return [pltpu.SemaphoreType.DMA((n,)), pltpu.SemaphoreType.DMA((n,)), pltpu.SemaphoreType.DMA((max(self.n_local, 1),))]

    def _copies(self, in_refs, out_refs, sems):
        send_sems, recv_sems, local_sems = sems
        phases, local = self.plan(in_refs, out_refs)
        out, k = [], 0
        for phase in phases:
            out.append([pltpu.make_async_remote_copy(src_ref=s, dst_ref=d, send_sem=send_sems.at[k + n], recv_sem=recv_sems.at[k + n],
                                                     device_id=dev, device_id_type=MESH) for n, (s, d, dev) in enumerate(phase)])
            k += len(phase)
        return out, [pltpu.make_async_copy(s, d, local_sems.at[n]) for n, (s, d) in enumerate(local)]

    def start(self, in_refs, out_refs, sems):
        phases, loc = self._copies(in_refs, out_refs, sems)
        for cp in phases[0] + loc:
            cp.start()

    def advance(self, hook, in_refs, out_refs, sems):
        p = hook - (3 - len(self.sizes))
        if p >= 1:
            phases, _ = self._copies(in_refs, out_refs, sems)
            for cp in phases[p - 1]:
                cp.wait_recv()
            for cp in phases[p]:
                cp.start()

    def finish(self, in_refs, out_refs, sems):
        phases, loc = self._copies(in_refs, out_refs, sems)
        for cp in phases[-1]:
            cp.wait_recv()
        for phase in phases:
            for cp in phase:
                cp.wait_send()
        for cp in loc:
            cp.wait()


def gather(v, rows=None, into=None):
    R, C = v.shape
    r0, nr = rows or (0, R)
    na = min(nr, ((nr // 2 + 15) // 16) * 16)

    def plan(ins, outs):
        x, y, c, _ = _place()
        xn, yn, dg, sibling = (1 - x, y), (x, 1 - y), (1 - x, 1 - y), (x, y, 1 - c)
        slot = lambda chip, start=r0, count=nr: outs[0].at[4 * chip[0] + 2 * chip[1] + c, pl.ds(start, count), :]
        src, mine = ins[0].at[pl.ds(r0, nr), :], slot((x, y))
        same = lambda ref, to: (ref, ref, to)
        first = [(src, mine, sibling), (src, mine, (*xn, c)), (src, mine, (*yn, c))]
        relay = [same(slot(xn, r0, na), (*yn, c)), same(slot(yn, r0 + na, nr - na), (*xn, c))]
        onward = [same(slot(xn), sibling), same(slot(yn), sibling), same(slot(dg), sibling)]
        return [first, relay, onward], [(src, mine)]

    if into is None:
        return _Exchange([v], [S((N_DEV, R, C), v.dtype)], (3, 2, 3), 1, plan)
    return _Exchange([v, into], [S((N_DEV, R, C), v.dtype)], (3, 2, 3), 1, plan, aliases={1: 0})


def scatter_first(gb):
    _, R, C = gb.shape

    def plan(ins, outs):
        x, y, c, chips = _place()
        owners = [(x, y)] + chips
        return [[(ins[0].at[4 * px + 2 * py + (1 - c)], outs[0].at[j], (x, y, 1 - c)) for j, (px, py) in enumerate(owners)]], []

    return _Exchange([gb], [S((4, R, C), BF16)], (4,), 0, plan)


def scatter_second(sb, rows=None, into=None):
    r0, nr = rows or (0, sb.shape[1])

    def plan(ins, outs):
        x, y, c, chips = _place()
        part = lambda ref, j: ref.at[j, pl.ds(r0, nr), :]
        return [[(part(ins[0], j), part(outs[0], j), (*chips[j], c)) for j in range(3)]], []

    if into is None:
        return _Exchange([sb], [S(sb.shape, BF16)], (3,), 0, plan)
    return _Exchange([sb, into], [S(sb.shape, BF16)], (3,), 0, plan, aliases={1: 0})


def _call(body, *, name, grid, in_specs, out_specs, out_shape, args, scratch=(), sem=None, riders=(), marks=None):
    single = not isinstance(out_shape, (tuple, list))
    out_shape = (out_shape,) if single else tuple(out_shape)
    out_specs = (out_specs,) if single else tuple(out_specs)
    n_in, n_out, n_sc = len(in_specs), len(out_shape), len(scratch)
    if not riders:
        res = pl.pallas_call(body, name=name, grid=grid, in_specs=list(in_specs), out_specs=out_specs, out_shape=out_shape,
                             scratch_shapes=list(scratch), compiler_params=_params(sem))(*args)
        return res[0] if single else res
    r_ins = [a for r in riders for a in r.ins]
    r_outs = [o for r in riders for o in r.outs]
    r_scr = [s for r in riders for s in r.scratch()]
    aliases, i0, o0 = {}, n_in, n_out
    for r in riders:
        for a, b in r.aliases.items():
            aliases[i0 + a] = o0 + b
        i0, o0 = i0 + len(r.ins), o0 + len(r.outs)
    steps = math.prod(grid)

    def full(*refs):
        ins, rin = refs[:n_in], refs[n_in:n_in + len(r_ins)]
        pos = n_in + len(r_ins)
        outs, rout = refs[pos:pos + n_out], refs[pos + n_out:pos + n_out + len(r_outs)]
        pos += n_out + len(r_outs)
        sc, rsc = refs[pos:pos + n_sc], refs[pos + n_sc:]
        step = 0
        for d, n in enumerate(grid):
            step = step * n + pl.program_id(d)

        def each(method, *lead):
            i, o = 0, 0
            for k, r in enumerate(riders):
                getattr(r, method)(*lead, rin[i:i + len(r.ins)], rout[o:o + len(r.outs)], rsc[3 * k:3 * k + 3])
                i, o = i + len(r.ins), o + len(r.outs)

        @pl.when(step == 0)
        def _():
            each("start")
        body(*ins, *outs, *sc)

        late = max(steps - 1 - max(steps // 8, 1), 0)
        first, second = marks or (min((3 * steps) // 5, late), late)

        @pl.when(step == first)
        def _():
            each("advance", 1)

        @pl.when(step == second)
        def _():
            each("advance", 2)

        @pl.when(step == steps - 1)
        def _():
            each("finish")

    anywhere = pl.BlockSpec(memory_space=pl.ANY)
    res = pl.pallas_call(
        full, name=name, grid=grid, in_specs=list(in_specs) + [anywhere] * len(r_ins),
        out_specs=out_specs + (anywhere,) * len(r_outs), out_shape=out_shape + tuple(r_outs),
        scratch_shapes=list(scratch) + r_scr, input_output_aliases=aliases,
        compiler_params=_params(("arbitrary",) * len(grid)))(*args, *r_ins)
    host, rest, per = res[:n_out], list(res[n_out:]), []
    for r in riders:
        per.append(rest[:len(r.outs)])
        rest = rest[len(r.outs):]
    return (host[0] if single else tuple(host)), per


def _rows_tile(n, cap):
    return max(t for t in range(16, min(n, cap) + 1, 16) if n % t == 0)


def scatter_add(g, ra, name):
    _, R, C = g.shape
    tr = _rows_tile(R, 176)
    x, y, c, chips = _place()
    slots = jnp.stack([4 * px + 2 * py + c for px, py in [(x, y)] + chips]).astype(jnp.int32)

    def body(s_ref, g0, g1, g2, g3, ra_ref, own_ref, sb_ref):
        own_ref[...] = g0[...] + ra_ref[0].astype(F32)
        for j, gj in enumerate((g1, g2, g3)):
            sb_ref[j] = (gj[...] + ra_ref[j + 1].astype(F32)).astype(BF16)

    spec = pltpu.PrefetchScalarGridSpec(
        num_scalar_prefetch=1, grid=(R // tr,),
        in_specs=[pl.BlockSpec((None, tr, C), lambda i, s, j=j: (s[j], i, 0)) for j in range(4)]
        + [pl.BlockSpec((4, tr, C), lambda i, s: (0, i, 0))],
        out_specs=(pl.BlockSpec((tr, C), lambda i, s: (i, 0)), pl.BlockSpec((3, tr, C), lambda i, s: (0, i, 0))))
    return pl.pallas_call(body, name=name, grid_spec=spec, out_shape=(S((R, C), F32), S((3, R, C), BF16)),
                          compiler_params=_params(("parallel",)))(slots, g, g, g, g, ra)


def rms_cast(h, g, name, riders=()):
    T, D = h.shape
    tm = _tile(T, 512)

    def body(h_ref, g_ref, n_ref):
        n_ref[...] = _rms(h_ref[...], g_ref[...]).astype(BF16)

    row = pl.BlockSpec((tm, D), lambda i: (i, 0))
    return _call(body, name=name, grid=(T // tm,), out_shape=S((T, D), BF16), in_specs=[row, pl.BlockSpec((1, D), lambda i: (0, 0))],
                 out_specs=row, sem=("parallel",), args=(h, g), riders=riders)


def ffn_up_fwd(n, wgu, name, riders=()):
    T, D = n.shape
    F = wgu.shape[1]
    tr, tn = _tile(T, 512), _tile(F, 256)

    def body(n_ref, wg_ref, wu_ref, dgate_ref, dup_ref, a_ref):
        wg, wu = wg_ref[...], wu_ref[...]
        for r in range(T // tr):
            rows = slice(r * tr, (r + 1) * tr)
            x = n_ref[rows, :]
            gate = _nt(x, wg)
            up = _nt(x, wu)
            s = jax.nn.sigmoid(gate)
            silu = gate * s
            dgate_ref[rows, :] = (up * (s * (1.0 + gate * (1.0 - s)))).astype(BF16)
            dup_ref[rows, :] = silu.astype(BF16)
            a_ref[rows, :] = (silu * up).astype(BF16)

    tile = pl.BlockSpec((T, tn), lambda j: (0, j))
    return _call(
        body, name=name, grid=(F // tn,), out_shape=(S((T, F), BF16),) * 3,
        in_specs=[pl.BlockSpec((T, D), lambda j: (0, 0)),
                  pl.BlockSpec((None, tn, D), lambda j: (0, j, 0)), pl.BlockSpec((None, tn, D), lambda j: (1, j, 0))],
        out_specs=(tile, tile, tile), sem=("parallel",), args=(n, wgu, wgu), riders=riders)


def ffn_down_fwd(a, wd, h, g_next, name, riders=()):
    T, F = a.shape
    D = wd.shape[1]
    tm = _tile(T, 256)

    def body(a_ref, w_ref, h_ref, *rest):
        out = h_ref[...] + 0.5 * _nn(a_ref[...], w_ref[...])
        if g_next is None:
            rest[0][...] = out
        else:
            g_ref, o_ref, n_ref = rest
            o_ref[...] = out
            n_ref[...] = _rms(out, g_ref[...]).astype(BF16)

    row = pl.BlockSpec((tm, D), lambda i: (i, 0))
    more = g_next is not None
    return _call(
        body, name=name, grid=(T // tm,), out_shape=(S((T, D), F32), S((T, D), BF16)) if more else S((T, D), F32),
        in_specs=[pl.BlockSpec((tm, F), lambda i: (i, 0)), pl.BlockSpec((F, D), lambda i: (0, 0)), row]
        + ([pl.BlockSpec((1, D), lambda i: (0, 0))] if more else []),
        out_specs=(row, row) if more else row,
        sem=("parallel",), args=(a, wd, h) + ((g_next,) if more else ()), riders=riders)


def mix_in_fwd(h, g, win, name):
    T, D = h.shape
    N = win.shape[0]
    tm = _tile(T, 256)

    def body(h_ref, g_ref, w_ref, n_ref, p_ref):
        n = _rms(h_ref[...], g_ref[...]).astype(BF16)
        n_ref[...] = n
        p_ref[...] = _nt(n, w_ref[...]).astype(BF16)

    return pl.pallas_call(
        body, name=name, grid=(T // tm,), out_shape=(S((T, D), BF16), S((T, N), BF16)),
        in_specs=[pl.BlockSpec((tm, D), lambda i: (i, 0)), pl.BlockSpec((1, D), lambda i: (0, 0)),
                  pl.BlockSpec((N, D), lambda i: (0, 0))],
        out_specs=(pl.BlockSpec((tm, D), lambda i: (i, 0)), pl.BlockSpec((tm, N), lambda i: (i, 0))),
        compiler_params=_params(("parallel",)),
    )(h, g, win)


def _tri_consts():
    r = lax.broadcasted_iota(jnp.int32, (QB, QB), 0)
    c = lax.broadcasted_iota(jnp.int32, (QB, QB), 1)
    ones = jnp.ones((QB, QB), BF16)
    with_sums = lambda tri: jnp.concatenate([tri.astype(BF16), ones], axis=1)
    return with_sums(r > c), with_sums(r <= c), with_sums(r < c)


def _half_masks():
    lane = lax.broadcasted_iota(jnp.int32, (QB, PAIR), 1)
    row = lax.broadcasted_iota(jnp.int32, (QB, PAIR), 0)
    return lane < HEAD_DIM, lane, row


def sb_attn_fwd(p, after, name, riders=()):
    T = p.shape[0]
    nq = T // QB

    def body(q_ref, k_ref, v_ref, m_ref, o_ref, tot_ref, q_sc, acc_ref, z_sc):
        i = pl.program_id(0)
        lo, lane, row = _half_masks()
        causal = lane < row
        heads, pairs = range(SB_HEADS), range(SB_HEADS // 2)
        for hp in pairs:
            q_sc[hp] = (q_ref[:, hp * PAIR:(hp + 1) * PAIR].astype(F32) * SCALE).astype(BF16)
        m2 = m_ref[...]

        def by_head(ref, j, hp):
            t = ref[pl.ds(pl.multiple_of(j * QB, QB), QB), hp * PAIR:(hp + 1) * PAIR]
            return jnp.concatenate([jnp.where(lo, t, 0), jnp.where(lo, 0, t)], axis=0)

        def scores(j):
            return [_nt(q_sc[hp], by_head(k_ref, j, hp)) for hp in pairs]

        def block(j, diag):
            z2 = [z_sc[hp] for hp in pairs]
            ahead = scores(jnp.maximum(j - 1, 0))
            for hp in pairs:
                z_sc[hp] = ahead[hp]
            vs = [by_head(v_ref, j, hp) for hp in pairs]
            spls = [_softplus_logsig(z2[h // 2][:, (h % 2) * QB:(h % 2 + 1) * QB]) for h in heads]
            sp = [jnp.where(causal, spls[h][0], 0.0) if diag else spls[h][0] for h in heads]
            rr = _tri(sp, m2)
            if diag:
                w = [jnp.where(causal, jnp.exp(spls[h][1] - rr[h][:, :QB]), 0.0).astype(BF16) for h in heads]
            else:
                c = [tot_ref[:, h * QB:(h + 1) * QB] for h in heads]
                w = [jnp.exp(spls[h][1] - (c[h] + rr[h][:, :QB])).astype(BF16) for h in heads]
            pv = [_nn(jnp.concatenate([w[2 * hp], w[2 * hp + 1]], axis=1), vs[hp]) for hp in pairs]
            for hp in pairs:
                acc_ref[hp] = pv[hp] if diag else acc_ref[hp] + pv[hp]
            for h in heads:
                tot_ref[:, h * QB:(h + 1) * QB] = rr[h][:, QB:] if diag else c[h] + rr[h][:, QB:]

        first = scores(i)
        for hp in pairs:
            z_sc[hp] = first[hp]
        block(i, True)

        def step(t, carry):
            block(i - 1 - t, False)
            return carry
        lax.fori_loop(0, i, step, 0)
        for hp in pairs:
            o_ref[:, hp * PAIR:(hp + 1) * PAIR] = acc_ref[hp]

    npair = SB_HEADS // 2
    return _call(
        body, name=name, grid=(nq,), out_shape=(S((T, SB_W), F32), S((T, SB_HEADS * QB), F32)),
        in_specs=[pl.BlockSpec((QB, SB_W), lambda i: (i, 0)), pl.BlockSpec((T, SB_W), lambda i: (0, 1)),
                  pl.BlockSpec((T, SB_W), lambda i: (0, 2)), pl.BlockSpec((QB, 2 * QB), lambda i: (0, 0))],
        out_specs=(pl.BlockSpec((QB, SB_W), lambda i: (i, 0)), pl.BlockSpec((QB, SB_HEADS * QB), lambda i: (i, 0))),
        scratch=[pltpu.VMEM((npair, QB, PAIR), BF16), pltpu.VMEM((npair, QB, PAIR), F32), pltpu.VMEM((npair, QB, 2 * QB), F32)],
        sem=("arbitrary",), args=(p, p, p, after), riders=riders,
        marks=((11 * nq) // 16, (14 * nq) // 16))


def sb_attn_bwd(p, do, tot, upto, before, name, riders=()):
    T = p.shape[0]
    nq = T // QB

    def body(q_ref, k_ref, v_ref, do_ref, tot_ref, mp_ref, mg_ref, dq_ref, dk_ref, dv_ref,
             q_sc, d_sc, qd_sc, pg_sc, dq_acc, dk_acc, dv_acc, zd_sc):
        i = pl.program_id(0)
        lo, lane, row = _half_masks()
        causal = lane < row
        heads, pairs = range(SB_HEADS), range(SB_HEADS // 2)

        def by_head(t):
            return jnp.concatenate([jnp.where(lo, t, 0), jnp.where(lo, 0, t)], axis=0)

        for hp in pairs:
            q2 = (q_ref[:, hp * PAIR:(hp + 1) * PAIR].astype(F32) * SCALE).astype(BF16)
            d2 = do_ref[:, hp * PAIR:(hp + 1) * PAIR].astype(BF16)
            q_sc[hp] = q2
            d_sc[hp] = d2
            qd_sc[hp] = by_head(q2)
            qd_sc[SB_HEADS // 2 + hp] = by_head(d2)
        mp, mg = mp_ref[...], mg_ref[...]

        @pl.when(i == 0)
        def _():
            dk_acc[...] = jnp.zeros_like(dk_acc)
            dv_acc[...] = jnp.zeros_like(dv_acc)
        pg_sc[...] = jnp.zeros_like(pg_sc)
        dq_acc[...] = jnp.zeros_like(dq_acc)

        def rows(ref, j, hp):
            return ref[pl.ds(pl.multiple_of(j * QB, QB), QB), hp * PAIR:(hp + 1) * PAIR]

        def products(j):
            return ([_nt(q_sc[hp], by_head(rows(k_ref, j, hp))) for hp in pairs]
                    + [_nt(d_sc[hp], by_head(rows(v_ref, j, hp))) for hp in pairs])

        def block(j, diag):
            r0 = pl.multiple_of(j * QB, QB)
            half = lambda t, h: t[:, (h % 2) * QB:(h % 2 + 1) * QB]
            z = [half(zd_sc[h // 2], h) for h in heads]
            dw = [half(zd_sc[SB_HEADS // 2 + h // 2], h) for h in heads]
            if not diag:
                ahead = products(j + 1)
                for hp in range(SB_HEADS):
                    zd_sc[hp] = ahead[hp]
            ks = [by_head(rows(k_ref, j, hp)) for hp in pairs]
            spls = [_softplus_logsig(z[h]) for h in heads]
            sp = [jnp.where(causal, spls[h][0], 0.0) if diag else spls[h][0] for h in heads]
            rr = _tri(sp, mp)
            pc = [pg_sc[2 * h] for h in heads]
            w = [jnp.exp(spls[h][1] - (tot_ref[:, h * QB:(h + 1) * QB] - (pc[h] + rr[h][:, :QB]))) for h in heads]
            if diag:
                w = [jnp.where(causal, w[h], 0.0) for h in heads]
            gg = [dw[h] * w[h] for h in heads]
            rg = _tri(gg, mg)
            gc = [pg_sc[2 * h + 1] for h in heads]
            dz = [gg[h] - (gg[h] + gc[h] + rg[h][:, :QB]) * jnp.exp(spls[h][1]) for h in heads]
            if diag:
                dz = [jnp.where(causal, dz[h], 0.0) for h in heads]
            dzb = [dz[h].astype(BF16) for h in heads]
            wb = [w[h].astype(BF16) for h in heads]
            both = lambda t, hp, axis: jnp.concatenate([t[2 * hp], t[2 * hp + 1]], axis=axis)
            dq = [_nn(both(dzb, hp, 1), ks[hp]) for hp in pairs]
            dk = [_tn(both(dzb, hp, 0), qd_sc[hp]) for hp in pairs]
            dv = [_tn(both(wb, hp, 0), qd_sc[SB_HEADS // 2 + hp]) for hp in pairs]
            for h in heads:
                if not diag:
                    pg_sc[2 * h] = pc[h] + rr[h][:, QB:]
                    pg_sc[2 * h + 1] = gc[h] + rg[h][:, QB:]
            for hp in pairs:
                dq_acc[hp] += dq[hp]
                dk_acc[pl.ds(r0, QB), hp * PAIR:(hp + 1) * PAIR] += dk[hp]
                dv_acc[pl.ds(r0, QB), hp * PAIR:(hp + 1) * PAIR] += dv[hp]

        first = products(0)
        for hp in range(SB_HEADS):
            zd_sc[hp] = first[hp]

        def step(t, carry):
            block(t, False)
            return carry
        lax.fori_loop(0, i, step, 0)
        block(i, True)
        for hp in pairs:
            dq_ref[:, hp * PAIR:(hp + 1) * PAIR] = (dq_acc[hp] * SCALE).astype(BF16)

        @pl.when(i == nq - 1)
        def _():
            dk_ref[...] = dk_acc[...].astype(BF16)
            dv_ref[...] = dv_acc[...].astype(BF16)

    qtile = pl.BlockSpec((QB, SB_W), lambda i: (i, 0))
    whole = pl.BlockSpec((T, SB_W), lambda i: (0, 0))
    const = pl.BlockSpec((QB, 2 * QB), lambda i: (0, 0))
    return _call(
        body, name=name, grid=(nq,), out_shape=(S((T, SB_W), BF16),) * 3,
        in_specs=[qtile, pl.BlockSpec((T, SB_W), lambda i: (0, 1)), pl.BlockSpec((T, SB_W), lambda i: (0, 2)), qtile,
                  pl.BlockSpec((QB, SB_HEADS * QB), lambda i: (i, 0)), const, const],
        out_specs=(qtile, whole, whole),
        scratch=[pltpu.VMEM((SB_HEADS // 2, QB, PAIR), BF16), pltpu.VMEM((SB_HEADS // 2, QB, PAIR), BF16),
                 pltpu.VMEM((SB_HEADS, 2 * QB, PAIR), BF16),
                 pltpu.VMEM((2 * SB_HEADS, QB, QB), F32), pltpu.VMEM((SB_HEADS // 2, QB, PAIR), F32),
                 pltpu.VMEM((T, SB_W), F32), pltpu.VMEM((T, SB_W), F32), pltpu.VMEM((SB_HEADS, QB, 2 * QB), F32)],
        sem=("arbitrary",), args=(p, p, p, do, tot, upto, before), riders=riders)


def _t5_buckets():
    a = lax.broadcasted_iota(jnp.int32, (QB, QB), 0)
    c = lax.broadcasted_iota(jnp.int32, (QB, QB), 1)

    def bucket(dist):
        dist = jnp.maximum(dist, 0)
        max_exact = N_BUCKETS // 2
        d = jnp.maximum(dist, 1).astype(F32)
        large = max_exact + (jnp.log(d / max_exact) / math.log(MAX_DISTANCE / max_exact)
                             * (N_BUCKETS - max_exact)).astype(jnp.int32)
        large = jnp.minimum(large, N_BUCKETS - 1)
        return jnp.where(dist < max_exact, dist, large)

    return bucket(QB + a - c), bucket(a - c)


def _swa_common(i, kp_ref, kc_ref, vp_ref, vc_ref, bp_ref, bc_ref, rb_ref, bias_ref):
    lo, lane, row = _half_masks()

    @pl.when(i == 0)
    def _():
        for blk, b_ref in enumerate((bp_ref, bc_ref)):
            bk = b_ref[...]
            for h in range(8):
                acc = jnp.zeros((QB, QB), F32)
                for b in range(N_BUCKETS):
                    acc = jnp.where(bk == b, rb_ref[b, h], acc)
                bias_ref[h, blk] = acc

    band = [(lane > row) & (i > 0), lane <= row]

    def stacks(ref):
        t = ref[...].astype(F32)
        sw = pltpu.roll(t, HEAD_DIM, 1)
        return [jnp.concatenate([jnp.where(lo, t, 0.0), jnp.where(lo, 0.0, sw)], axis=0).astype(BF16),
                jnp.concatenate([jnp.where(lo, sw, 0.0), jnp.where(lo, 0.0, t)], axis=0).astype(BF16)]

    ks = [stacks(kp_ref), stacks(kc_ref)]
    vs = [stacks(vp_ref), stacks(vc_ref)]
    return lo, band, ks, vs


def _lane_half(t, h):
    return t[:, (h % 2) * QB:(h % 2 + 1) * QB]


def swa_fwd(p, sinks, rel_bias, bprev, bcur, name, riders=()):
    T = p.shape[0]
    nq = T // QB
    kcol, vcol = (3 * SB_W + SWA_W) // KV_W, (3 * SB_W + SWA_W) // KV_W + 1

    def body(q_ref, kp_ref, kc_ref, vp_ref, vc_ref, bp_ref, bc_ref, sink_ref, rb_ref, o_ref, lse_ref, bias_ref):
        i = pl.program_id(0)
        lo, band, ks, vs = _swa_common(i, kp_ref, kc_ref, vp_ref, vc_ref, bp_ref, bc_ref, rb_ref, bias_ref)
        heads, pairs, blocks = range(8), range(4), range(2)
        rowmax = lambda t: jnp.max(t, axis=1, keepdims=True)
        rowsum = lambda t: jnp.sum(t, axis=1, keepdims=True)
        q2 = [q_ref[:, g * PAIR:(g + 1) * PAIR] for g in pairs]
        s2 = [[_nt(q2[g], ks[b][g // 2]) for b in blocks] for g in pairs]
        sc = [[jnp.where(band[b], _lane_half(s2[h // 2][b], h) * SCALE + bias_ref[h, b], NEG_INF) for b in blocks] for h in heads]
        sink = [sink_ref[0, h] for h in heads]
        m = [jnp.maximum(jnp.maximum(rowmax(sc[h][0]), rowmax(sc[h][1])), sink[h]) for h in heads]
        e = [[jnp.exp(sc[h][b] - m[h]) for b in blocks] for h in heads]
        den = [rowsum(e[h][0]) + rowsum(e[h][1]) + jnp.exp(sink[h] - m[h]) for h in heads]
        pb = [[(e[h][b] / den[h]).astype(BF16) for b in blocks] for h in heads]
        for g in pairs:
            both = lambda b: jnp.concatenate([pb[2 * g][b], pb[2 * g + 1][b]], axis=1)
            o_ref[:, g * PAIR:(g + 1) * PAIR] = _nn(both(0), vs[0][g // 2]) + _nn(both(1), vs[1][g // 2])
        for h in heads:
            lse_ref[:, h * QB:(h + 1) * QB] = jnp.broadcast_to(m[h] + jnp.log(den[h]), (QB, QB))

    kv = lambda col, prev: pl.BlockSpec((QB, KV_W), (lambda i: (jnp.maximum(i - 1, 0), col)) if prev else (lambda i: (i, col)))
    full = pl.BlockSpec((QB, QB), lambda i: (0, 0))
    smem = pl.BlockSpec(memory_space=pltpu.SMEM)
    return _call(
        body, name=name, grid=(nq,), out_shape=(S((T, SWA_W), F32), S((T, 8 * QB), F32)),
        in_specs=[pl.BlockSpec((QB, SWA_W), lambda i: (i, 3)), kv(kcol, True), kv(kcol, False), kv(vcol, True), kv(vcol, False),
                  full, full, smem, smem],
        out_specs=(pl.BlockSpec((QB, SWA_W), lambda i: (i, 0)), pl.BlockSpec((QB, 8 * QB), lambda i: (i, 0))),
        scratch=[pltpu.VMEM((8, 2, QB, QB), F32)],
        sem=("arbitrary",), args=(p, p, p, p, p, bprev, bcur, sinks, rel_bias), riders=riders)


def swa_bwd(p, do, lse, sinks, rel_bias, bprev, bcur, name, riders=()):
    T = p.shape[0]
    nq = T // QB
    kcol, vcol = (3 * SB_W + SWA_W) // KV_W, (3 * SB_W + SWA_W) // KV_W + 1

    def body(q_ref, kp_ref, kc_ref, vp_ref, vc_ref, do_ref, lse_ref, bp_ref, bc_ref, sink_ref, rb_ref,
             dq_ref, dk_ref, dv_ref, dsink_ref, dsc_ref, bias_ref, dk_acc, dv_acc):
        i = pl.program_id(0)
        lo, band, ks, vs = _swa_common(i, kp_ref, kc_ref, vp_ref, vc_ref, bp_ref, bc_ref, rb_ref, bias_ref)

        @pl.when(i == 0)
        def _():
            dk_acc[...] = jnp.zeros_like(dk_acc)
            dv_acc[...] = jnp.zeros_like(dv_acc)
            dsc_ref[...] = jnp.zeros_like(dsc_ref)
            dsink_ref[...] = jnp.zeros_like(dsink_ref)

        heads, pairs, blocks = range(8), range(4), range(2)
        rowsum = lambda t: jnp.sum(t, axis=1, keepdims=True)
        by_head = lambda t: jnp.concatenate([jnp.where(lo, t, 0), jnp.where(lo, 0, t)], axis=0)
        q2 = [q_ref[:, g * PAIR:(g + 1) * PAIR] for g in pairs]
        d2 = [do_ref[:, g * PAIR:(g + 1) * PAIR].astype(BF16) for g in pairs]
        qs = [by_head(q2[g]) for g in pairs]
        dos = [by_head(d2[g]) for g in pairs]
        s2 = [[_nt(q2[g], ks[b][g // 2]) for b in blocks] for g in pairs]
        dp2 = [[_nt(d2[g], vs[b][g // 2]) for b in blocks] for g in pairs]
        lse_h = [lse_ref[:, h * QB:(h + 1) * QB] for h in heads]
        sink = [sink_ref[0, h] for h in heads]
        pr = [[jnp.exp(jnp.where(band[b], _lane_half(s2[h // 2][b], h) * SCALE + bias_ref[h, b], NEG_INF) - lse_h[h])
               for b in blocks] for h in heads]
        dp = [[_lane_half(dp2[h // 2][b], h) for b in blocks] for h in heads]
        delta = [rowsum(pr[h][0] * dp[h][0]) + rowsum(pr[h][1] * dp[h][1]) for h in heads]
        lane1 = lax.broadcasted_iota(jnp.int32, (1, QB), 1)
        dsink = jnp.zeros((1, QB), F32)
        for h in heads:
            dsink = dsink + jnp.where(lane1 == h, -jnp.sum(jnp.exp(sink[h] - lse_h[h][:, :1]) * delta[h]), 0.0)
        dsink_ref[...] += dsink
        dsc = [[pr[h][b] * (dp[h][b] - delta[h]) for b in blocks] for h in heads]
        for h in heads:
            for b in blocks:
                dsc_ref[h, b] += dsc[h][b]
        dzb = [[(dsc[h][b] * SCALE).astype(BF16) for b in blocks] for h in heads]
        prb = [[pr[h][b].astype(BF16) for b in blocks] for h in heads]
        pair_of = lambda t, g, b, axis: jnp.concatenate([t[2 * g][b], t[2 * g + 1][b]], axis=axis)
        for g in pairs:
            dq = _nn(pair_of(dzb, g, 0, 1), ks[0][g // 2]) + _nn(pair_of(dzb, g, 1, 1), ks[1][g // 2])
            dq_ref[:, g * PAIR:(g + 1) * PAIR] = dq.astype(BF16)

        def key_grad(t, other, b):
            per_kv = [_tn(pair_of(t, 2 * kh, b, 0), other[2 * kh]) + _tn(pair_of(t, 2 * kh + 1, b, 0), other[2 * kh + 1]) for kh in range(2)]
            both = [s + pltpu.roll(s, HEAD_DIM, 1) for s in per_kv]
            return jnp.where(lo, both[0], both[1])

        rp = pl.multiple_of(jnp.maximum(i - 1, 0) * QB, QB)
        rc = pl.multiple_of(i * QB, QB)
        dk_acc[pl.ds(rp, QB), :] += key_grad(dzb, qs, 0)
        dv_acc[pl.ds(rp, QB), :] += key_grad(prb, dos, 0)
        dk_acc[pl.ds(rc, QB), :] += key_grad(dzb, qs, 1)
        dv_acc[pl.ds(rc, QB), :] += key_grad(prb, dos, 1)

        @pl.when(i == nq - 1)
        def _():
            dk_ref[...] = dk_acc[...].astype(BF16)
            dv_ref[...] = dv_acc[...].astype(BF16)

    kv = lambda col, prev: pl.BlockSpec((QB, KV_W), (lambda i: (jnp.maximum(i - 1, 0), col)) if prev else (lambda i: (i, col)))
    full = pl.BlockSpec((QB, QB), lambda i: (0, 0))
    smem = pl.BlockSpec(memory_space=pltpu.SMEM)
    whole = lambda shape: pl.BlockSpec(shape, lambda i: (0,) * len(shape))
    return _call(
        body, name=name, grid=(nq,),
        out_shape=(S((T, SWA_W), BF16), S((T, KV_W), BF16), S((T, KV_W), BF16), S((1, QB), F32), S((8, 2, QB, QB), F32)),
        in_specs=[pl.BlockSpec((QB, SWA_W), lambda i: (i, 3)), kv(kcol, True), kv(kcol, False), kv(vcol, True), kv(vcol, False),
                  pl.BlockSpec((QB, SWA_W), lambda i: (i, 0)), pl.BlockSpec((QB, 8 * QB), lambda i: (i, 0)),
                  full, full, smem, smem],
        out_specs=(pl.BlockSpec((QB, SWA_W), lambda i: (i, 0)), whole((T, KV_W)), whole((T, KV_W)), whole((1, QB)),
                   whole((8, 2, QB, QB))),
        scratch=[pltpu.VMEM((8, 2, QB, QB), F32), pltpu.VMEM((T, KV_W), F32), pltpu.VMEM((T, KV_W), F32)],
        sem=("arbitrary",), args=(p, p, p, p, p, do, lse, bprev, bcur, sinks, rel_bias), riders=riders)


def mix_out_fwd(o_sb, o_sw, g_sb, g_sw, wout, h, g_next, name, riders=()):
    T, D = h.shape
    M = SB_W + SWA_W
    tm = _tile(T, 256)

    def body(a_ref, b_ref, ga_ref, gb_ref, w_ref, h_ref, gn_ref, mx_ref, o_ref, n_ref):
        mx_ref[:, :SB_W] = _rms(a_ref[...], ga_ref[...]).astype(BF16)
        mx_ref[:, SB_W:] = _rms(b_ref[...], gb_ref[...]).astype(BF16)
        out = h_ref[...] + _nn(mx_ref[...], w_ref[...])
        o_ref[...] = out
        n_ref[...] = _rms(out, gn_ref[...]).astype(BF16)

    row = lambda n: pl.BlockSpec((tm, n), lambda i: (i, 0))
    vec = lambda n: pl.BlockSpec((1, n), lambda i: (0, 0))
    return _call(
        body, name=name, grid=(T // tm,), out_shape=(S((T, M), BF16), S((T, D), F32), S((T, D), BF16)),
        in_specs=[row(SB_W), row(SWA_W), vec(SB_W), vec(SWA_W), pl.BlockSpec((M, D), lambda i: (0, 0)), row(D), vec(D)],
        out_specs=(row(M), row(D), row(D)),
        sem=("parallel",), args=(o_sb, o_sw, g_sb, g_sw, wout, h, g_next), riders=riders)


def loss_head(h, g, target, name):
    T, D = h.shape
    tm = _tile(T, 256)

    def body(h_ref, g_ref, t_ref, loss_ref, dh_ref, dhb_ref, dg_ref):
        @pl.when(pl.program_id(0) == 0)
        def _():
            loss_ref[...] = jnp.zeros_like(loss_ref)
            dg_ref[...] = jnp.zeros_like(dg_ref)
        x = h_ref[...]
        err = _rms(x, g_ref[...]) - t_ref[...]
        loss_ref[...] += jnp.full((1, QB), 0.5 * jnp.sum(jnp.mean(err * err, axis=-1)), F32)
        dx, dg = _rms_bwd(err / D, x, g_ref[...])
        dh_ref[...] = dx
        dhb_ref[...] = dx.astype(BF16)
        dg_ref[...] += dg

    row = pl.BlockSpec((tm, D), lambda i: (i, 0))
    vec = pl.BlockSpec((1, D), lambda i: (0, 0))
    return pl.pallas_call(
        body, name=name, grid=(T // tm,), out_shape=(S((1, QB), F32), S((T, D), F32), S((T, D), BF16), S((1, D), F32)),
        in_specs=[row, vec, row], out_specs=(pl.BlockSpec((1, QB), lambda i: (0, 0)), row, row, vec),
        compiler_params=_params(("arbitrary",)),
    )(h, g, target)


def ffn_down_bwd(dhb, wd, gate, up, name, riders=()):
    T, D = dhb.shape
    F = wd.shape[0]
    tr, tn = _tile(T, 512), _tile(F, 256)

    def body(d_ref, w_ref, g_ref, u_ref, o_ref):
        w = w_ref[...]
        for r in range(T // tr):
            rows = slice(r * tr, (r + 1) * tr)
            da = 0.5 * _nt(d_ref[rows, :], w)
            o_ref[0, rows, :] = (da * g_ref[rows, :].astype(F32)).astype(BF16)
            o_ref[1, rows, :] = (da * u_ref[rows, :].astype(F32)).astype(BF16)

    tile = pl.BlockSpec((T, tn), lambda j: (0, j))
    return _call(
        body, name=name, grid=(F // tn,), out_shape=S((2, T, F), BF16),
        in_specs=[pl.BlockSpec((T, D), lambda j: (0, 0)), pl.BlockSpec((tn, D), lambda j: (j, 0)), tile, tile],
        out_specs=pl.BlockSpec((2, T, tn), lambda j: (0, 0, j)),
        sem=("parallel",), args=(dhb, wd, gate, up), riders=riders)


def tn_matmul(xs, y, alpha, name, riders=()):
    B, T, N = xs.shape
    D = y.shape[1]
    tn = _tile(N, 256)

    def body(x_ref, y_ref, o_ref, ob_ref):
        o = alpha * _tn(x_ref[...], y_ref[...])
        o_ref[...] = o
        ob_ref[...] = o.astype(BF16)

    tile = pl.BlockSpec((None, tn, D), lambda s, j: (s, j, 0))
    return _call(
        body, name=name, grid=(B, N // tn), out_shape=(S((B, N, D), F32), S((B, N, D), BF16)),
        in_specs=[pl.BlockSpec((None, T, tn), lambda s, j: (s, 0, j)), pl.BlockSpec((T, D), lambda s, j: (0, 0))],
        out_specs=(tile, tile), sem=("parallel", "parallel"), args=(xs, y), riders=riders)


def nn_rms_bwd(xs, ws, h_in, g, dh, name, riders=()):
    B, T, K = xs.shape
    D = ws.shape[2]
    tm = _tile(T, 256)

    def body(x_ref, w_ref, h_ref, g_ref, d_ref, o_ref, ob_ref, dg_ref):
        @pl.when(pl.program_id(0) == 0)
        def _():
            dg_ref[...] = jnp.zeros_like(dg_ref)
        dn = _nn(x_ref[0], w_ref[0])
        for s in range(1, B):
            dn = dn + _nn(x_ref[s], w_ref[s])
        dx, dg = _rms_bwd(dn, h_ref[...], g_ref[...])
        out = d_ref[...] + dx
        o_ref[...] = out
        ob_ref[...] = out.astype(BF16)
        dg_ref[...] += dg

    row = pl.BlockSpec((tm, D), lambda i: (i, 0))
    vec = pl.BlockSpec((1, D), lambda i: (0, 0))
    return _call(
        body, name=name, grid=(T // tm,), out_shape=(S((T, D), F32), S((T, D), BF16), S((1, D), F32)),
        in_specs=[pl.BlockSpec((B, tm, K), lambda i: (0, i, 0)), pl.BlockSpec((B, K, D), lambda i: (0, 0, 0)), row, vec, row],
        out_specs=(row, row, vec),
        sem=("arbitrary",), args=(xs, ws, h_in, g, dh), riders=riders)


def mix_out_bwd(dhb, wout, o_sb, o_sw, g_sb, g_sw, name):
    T, D = dhb.shape
    tm = _tile(T, 256)

    def body(d_ref, w_ref, a_ref, b_ref, ga_ref, gb_ref, da_ref, db_ref, dga_ref, dgb_ref):
        @pl.when(pl.program_id(0) == 0)
        def _():
            dga_ref[...] = jnp.zeros_like(dga_ref)
            dgb_ref[...] = jnp.zeros_like(dgb_ref)
        dm = _nt(d_ref[...], w_ref[...])
        dxa, dga = _rms_bwd(dm[:, :SB_W], a_ref[...], ga_ref[...])
        dxb, dgb = _rms_bwd(dm[:, SB_W:], b_ref[...], gb_ref[...])
        da_ref[...] = dxa
        db_ref[...] = dxb
        dga_ref[...] += dga
        dgb_ref[...] += dgb

    row = lambda n: pl.BlockSpec((tm, n), lambda i: (i, 0))
    vec = lambda n: pl.BlockSpec((1, n), lambda i: (0, 0))
    return pl.pallas_call(
        body, name=name, grid=(T // tm,),
        out_shape=(S((T, SB_W), F32), S((T, SWA_W), F32), S((1, SB_W), F32), S((1, SWA_W), F32)),
        in_specs=[row(D), pl.BlockSpec((SB_W + SWA_W, D), lambda i: (0, 0)), row(SB_W), row(SWA_W), vec(SB_W), vec(SWA_W)],
        out_specs=(row(SB_W), row(SWA_W), vec(SB_W), vec(SWA_W)),
        compiler_params=_params(("arbitrary",)),
    )(dhb, wout, o_sb, o_sw, g_sb, g_sw)


def rel_bias_grad(dscs, bprev, bcur, name):
    n = len(dscs)

    def body(*refs):
        bp_ref, bc_ref, o_ref = refs[n], refs[n + 1], refs[n + 2]
        bks = [bp_ref[...], bc_ref[...]]
        row = lax.broadcasted_iota(jnp.int32, (N_BUCKETS, QB), 0)
        lane = lax.broadcasted_iota(jnp.int32, (N_BUCKETS, QB), 1)
        out = jnp.zeros((N_BUCKETS, QB), F32)
        for h in range(8):
            tot = [sum(refs[l][h, b] for l in range(n)) for b in range(2)]
            for b in range(N_BUCKETS):
                val = jnp.sum(jnp.where(bks[0] == b, tot[0], 0.0)) + jnp.sum(jnp.where(bks[1] == b, tot[1], 0.0))
                out = jnp.where((row == b) & (lane == h), val, out)
        o_ref[...] = out

    return pl.pallas_call(body, name=name, out_shape=S((N_BUCKETS, QB), F32), compiler_params=_params())(*dscs, bprev, bcur)


def _adamw(w, g, m, v):
    m = ADAM_B1 * m + (1.0 - ADAM_B1) * g
    v = ADAM_B2 * v + (1.0 - ADAM_B2) * (g * g)
    m_hat = m / (1.0 - ADAM_B1 ** ADAM_STEP)
    v_hat = v / (1.0 - ADAM_B2 ** ADAM_STEP)
    delta = -ADAM_LR * (m_hat / (jnp.sqrt(v_hat) + ADAM_EPS) + ADAM_WD * w)
    return delta, m, v


def adamw_scattered(w, m, v, owns, others, name, riders=()):
    L, R, C = w.shape
    tr = _rows_tile(R, 176)

    def body(w_ref, m_ref, v_ref, *rest):
        own_refs, other_refs = rest[:L], rest[L:2 * L]
        g_ref, d_ref, mo_ref, vo_ref = rest[2 * L:]
        layer = pl.program_id(0)

        def grad(k):
            o = other_refs[k]
            return own_refs[k][...] + o[0].astype(F32) + o[1].astype(F32) + o[2].astype(F32)

        g = grad(0)
        for k in range(1, L):
            g = jnp.where(layer == k, grad(k), g)
        d, mn, vn = _adamw(w_ref[...], g, m_ref[...], v_ref[...])
        g_ref[...] = g
        d_ref[...] = d
        mo_ref[...] = mn
        vo_ref[...] = vn

    tile = pl.BlockSpec((None, tr, C), lambda l, i: (l, i, 0))
    return _call(
        body, name=name, grid=(L, R // tr), out_shape=(S((L, R, C), F32),) * 4,
        in_specs=[tile] * 3 + [pl.BlockSpec((tr, C), lambda l, i: (i, 0))] * L + [pl.BlockSpec((3, tr, C), lambda l, i: (0, i, 0))] * L,
        out_specs=(tile,) * 4, sem=("parallel", "parallel"), args=(w, m, v, *owns, *others), riders=riders)


def adamw_small(w, gs, m, v, name):
    R, C = w.shape

    def body(w_ref, g_ref, m_ref, v_ref, go_ref, d_ref, mo_ref, vo_ref):
        g = g_ref[0]
        for k in range(1, N_DEV):
            g = g + g_ref[k]
        d, mn, vn = _adamw(w_ref[...], g, m_ref[...], v_ref[...])
        go_ref[...] = g
        d_ref[...] = d
        mo_ref[...] = mn
        vo_ref[...] = vn

    return pl.pallas_call(body, name=name, out_shape=(S((R, C), F32),) * 4, compiler_params=_params())(w, gs, m, v)


def kernel(x, norm_ffn1, w_ffn1_gu, w_ffn1_down, norm_mix, w_in, sinks, norm_out_sb, norm_out_swa, w_out, norm_ffn2, w_ffn2_gu, w_ffn2_down, rel_bias, norm_final, loss_target, m_norm_ffn1, m_w_ffn1_gu, m_w_ffn1_down, m_norm_mix, m_w_in, m_sinks, m_norm_out_sb, m_norm_out_swa, m_w_out, m_norm_ffn2, m_w_ffn2_gu, m_w_ffn2_down, m_rel_bias, m_norm_final, v_norm_ffn1, v_w_ffn1_gu, v_w_ffn1_down, v_norm_mix, v_w_in, v_sinks, v_norm_out_sb, v_norm_out_swa, v_w_out, v_norm_ffn2, v_w_ffn2_gu, v_w_ffn2_down, v_rel_bias, v_norm_final):
    L = norm_ffn1.shape[0]
    T, D = x.shape[1], x.shape[2]
    F = w_ffn1_down.shape[1] * N_DEV
    h = x.reshape(T, D)
    target = loss_target.reshape(T, D)
    after, upto, before = _tri_consts()
    bprev, bcur = _t5_buckets()

    local = {}
    for l in range(L):
        local[f"gu1_{l}"] = w_ffn1_gu[l].T.astype(BF16)
        local[f"d1_{l}"] = w_ffn1_down[l].astype(BF16)
        local[f"in_{l}"] = w_in[l].T.astype(BF16)
        local[f"out_{l}"] = w_out[l].astype(BF16)
        local[f"gu2_{l}"] = w_ffn2_gu[l].T.astype(BF16)
        local[f"d2_{l}"] = w_ffn2_down[l].astype(BF16)
    full, partial = {}, {}
    grads, chip_sum, recv_b = {}, {}, {}

    def run(fn, *args, ag=(), rs1=(), rs2=()):
        halves = lambda names: [n if isinstance(n, tuple) else (n, None) for n in names]
        ag, rs2 = [(n, k) for n, k in halves(ag) if n in local], halves(rs2)
        rows = lambda k, total: None if k is None else (k * (total // 2), total // 2)

        def second(n, k):
            sb = chip_sum[n][1]
            return scatter_second(sb, rows(k, sb.shape[1]), recv_b.get(n))

        riders = ([gather(local[n], rows(k, local[n].shape[0]), partial.get(n)) for n, k in ag]
                  + [scatter_first(grads[n][1]) for n in rs1] + [second(n, k) for n, k in rs2])
        if not riders:
            return fn(*args)
        outs, per = fn(*args, riders=riders)
        per = [p[0] for p in per]
        for n, k in ag:
            buf = per.pop(0)
            if k == 0:
                partial[n] = buf
            else:
                full[n] = buf.reshape(N_DEV * buf.shape[1], D)
        for n in rs1:
            chip_sum[n] = scatter_add(grads[n][0], per.pop(0), f"rs_add_{n}")
        for n, _ in rs2:
            recv_b[n] = per.pop(0)
        return outs

    gu = lambda n: full[n].reshape(2, F, D)
    slots = lambda pair: tuple(t.reshape(N_DEV, -1, D) for t in pair)
    vec = lambda a: a.reshape(1, -1)

    PW = max(D, SB_W + SWA_W)
    n_rows = 4 * L + 2
    n_rows += (-n_rows) % 8

    def pack(ffn1, mix, ffn2, final, osb, osw, snk, rel, extra):
        pieces = []

        def row(*parts):
            flat = [a.reshape(-1) for a in parts]
            pieces.extend(flat)
            used = sum(a.size for a in flat)
            if used < PW:
                pieces.append(jnp.zeros((PW - used,), F32))

        for group in (ffn1, mix, ffn2):
            for l in range(L):
                row(group[l])
        row(final)
        for l in range(L):
            row(osb[l], osw[l])
        row(*[snk[l].reshape(-1)[:8] for l in range(L)], rel, extra)
        pieces.append(jnp.zeros(((n_rows - 4 * L - 2) * PW,), F32))
        return jnp.concatenate(pieces).reshape(n_rows, PW)

    def unpack(arr):
        ffn1, mix, ffn2 = arr[0:L, :D], arr[L:2 * L, :D], arr[2 * L:3 * L, :D]
        final = arr[3 * L, :D]
        ob = arr[3 * L + 1:4 * L + 1]
        tail = arr[4 * L + 1]
        return (ffn1, mix, tail[:8 * L].reshape(L, 8), ob[:, :SB_W], ob[:, SB_W:SB_W + SWA_W], ffn2,
                tail[8 * L:8 * L + N_BUCKETS * 8].reshape(N_BUCKETS, 8), final)

    zero = jnp.zeros((1,), F32)
    w_small = pack(norm_ffn1, norm_mix, norm_ffn2, norm_final, norm_out_sb, norm_out_swa, sinks, rel_bias, zero)
    norm_ffn1, norm_mix, sinks, norm_out_sb, norm_out_swa, norm_ffn2, _, norm_final = unpack(w_small)

    saved = []
    n_next = run(rms_cast, h, vec(norm_ffn1[0]), "rms_first", ag=("gu1_0",))
    for l in range(L):
        nx = l + 1
        s = {"h0": h, "n1": n_next}
        s["gate1"], s["up1"], s["a1"] = run(ffn_up_fwd, s["n1"], gu(f"gu1_{l}"), f"ffn1_up{l}",
                                            ag=(f"d1_{l}", ("in_0", 0) if l == 0 else (f"in_{l}", 1)))
        h = run(ffn_down_fwd, s["a1"], full[f"d1_{l}"], h, None, f"ffn1_down{l}", ag=(("in_0", 1),) if l == 0 else ())
        s["h1"] = h
        s["n2"], s["p"] = mix_in_fwd(h, vec(norm_mix[l]), full[f"in_{l}"], f"mix_in{l}")
        s["o_sb"], s["tot"] = run(sb_attn_fwd, s["p"], after, f"sb_fwd{l}", ag=(f"out_{l}", f"gu2_{l}", f"d2_{l}"))
        s["o_sw"], s["lse"] = run(swa_fwd, s["p"], vec(sinks[l]), rel_bias, bprev, bcur, f"swa_fwd{l}", ag=((f"gu1_{nx}", 0),))
        s["mixed"], h, s["n3"] = run(mix_out_fwd, s["o_sb"], s["o_sw"], vec(norm_out_sb[l]), vec(norm_out_swa[l]),
                                     full[f"out_{l}"], h, vec(norm_ffn2[l]), f"mix_out{l}")
        s["h2"] = h
        s["gate2"], s["up2"], s["a2"] = run(ffn_up_fwd, s["n3"], gu(f"gu2_{l}"), f"ffn2_up{l}",
                                            ag=((f"gu1_{nx}", 1), (f"in_{nx}", 0)))
        if nx < L:
            h, n_next = run(ffn_down_fwd, s["a2"], full[f"d2_{l}"], h, vec(norm_ffn1[nx]), f"ffn2_down{l}")
        else:
            h = run(ffn_down_fwd, s["a2"], full[f"d2_{l}"], h, None, f"ffn2_down{l}")
        saved.append(s)

    loss_part, dh, dhb, dg_final = loss_head(h, vec(norm_final), target, "loss_head")

    small = {k: [None] * L for k in ("ffn1", "mix", "sinks", "osb", "osw", "ffn2", "dsc")}
    for l in reversed(range(L)):
        s = saved[l]

        def ffn_bwd(dh, dhb, tag, gate, up, a, n, h_in, g, r_down, r_dwgu, r_dwd, r_up):
            gu_n, d_n = f"gu{tag}_{l}", f"d{tag}_{l}"
            dgu = run(ffn_down_bwd, dhb, full[d_n], gate, up, f"ffn{tag}_down_bwd{l}", **r_down)
            grads[gu_n] = slots(run(tn_matmul, dgu, n, 1.0, f"ffn{tag}_dwgu{l}", **r_dwgu))
            grads[d_n] = slots(run(tn_matmul, a[None], dhb, 0.5, f"ffn{tag}_dwd{l}", **r_dwd))
            return run(nn_rms_bwd, dgu, gu(gu_n), h_in, g, dh, f"ffn{tag}_up_bwd{l}", **r_up)

        later = l + 1 < L
        dh, dhb, small["ffn2"][l] = ffn_bwd(dh, dhb, 2, s["gate2"], s["up2"], s["a2"], s["n3"], s["h2"], vec(norm_ffn2[l]),
                                            {}, dict(rs2=(f"d1_{l + 1}",) if later else ()), {},
                                            dict(rs1=(f"gu2_{l}", f"d2_{l}"), rs2=((f"gu1_{l + 1}", 1),) if later else ()))
        do_sb, do_sw, small["osb"][l], small["osw"][l] = mix_out_bwd(
            dhb, full[f"out_{l}"], s["o_sb"], s["o_sw"], vec(norm_out_sb[l]), vec(norm_out_swa[l]), f"mix_out_bwd{l}")
        grads[f"out_{l}"] = slots(tn_matmul(s["mixed"][None], dhb, 1.0, f"dwout{l}"))
        dq_sb, dk_sb, dv_sb = run(sb_attn_bwd, s["p"], do_sb, s["tot"], upto, before, f"sb_bwd{l}",
                                  rs2=(f"gu2_{l}", f"d2_{l}"), rs1=(f"out_{l}",))
        dq_sw, dk_sw, dv_sw, small["sinks"][l], small["dsc"][l] = swa_bwd(
            s["p"], do_sw, s["lse"], vec(sinks[l]), rel_bias, bprev, bcur, f"swa_bwd{l}")
        dp = jnp.concatenate([dq_sb, dk_sb, dv_sb, dq_sw, dk_sw, dv_sw], axis=1)
        dh, dhb, small["mix"][l] = nn_rms_bwd(dp[None], full[f"in_{l}"][None], s["h1"], vec(norm_mix[l]), dh, f"mix_in_bwd{l}")
        grads[f"in_{l}"] = slots(tn_matmul(dp[None], s["n2"], 1.0, f"dwin{l}"))
        dh, dhb, small["ffn1"][l] = ffn_bwd(dh, dhb, 1, s["gate1"], s["up1"], s["a1"], s["n1"], s["h0"], vec(norm_ffn1[l]),
                                            dict(rs1=(f"in_{l}",), rs2=(f"out_{l}",)), dict(rs2=(f"in_{l}",)),
                                            dict(rs1=(f"gu1_{l}",)), dict(rs1=(f"d1_{l}",), rs2=((f"gu1_{l}", 0),)))

    grad_x = dh.reshape(x.shape)

    upd = {}
    for nm, w, m, v, transposed, last in (
            ("gu2", w_ffn2_gu, m_w_ffn2_gu, v_w_ffn2_gu, True, (("gu1_0", 1), "d1_0")), ("d2", w_ffn2_down, m_w_ffn2_down, v_w_ffn2_down, False, ()),
            ("in", w_in, m_w_in, v_w_in, True, ()), ("out", w_out, m_w_out, v_w_out, False, ()),
            ("gu1", w_ffn1_gu, m_w_ffn1_gu, v_w_ffn1_gu, True, ()), ("d1", w_ffn1_down, m_w_ffn1_down, v_w_ffn1_down, False, ())):
        turn = (lambda a: jnp.swapaxes(a, 1, 2)) if transposed else (lambda a: a)
        names = [f"{nm}_{l}" for l in range(L)]
        res = run(adamw_scattered, turn(w), turn(m), turn(v), [chip_sum[n][0] for n in names], [recv_b[n] for n in names],
                  f"adamw_{nm}", rs2=last)
        upd[nm] = tuple(turn(r) for r in res)

    d_rel = rel_bias_grad(small["dsc"], bprev, bcur, "rel_bias_grad")[:, :8]
    g_small = pack(small["ffn1"], small["mix"], small["ffn2"], dg_final, small["osb"], small["osw"], small["sinks"], d_rel,
                   loss_part[0, :1])
    m_small = pack(m_norm_ffn1, m_norm_mix, m_norm_ffn2, m_norm_final, m_norm_out_sb, m_norm_out_swa, m_sinks, m_rel_bias, zero)
    v_small = pack(v_norm_ffn1, v_norm_mix, v_norm_ffn2, v_norm_final, v_norm_out_sb, v_norm_out_swa, v_sinks, v_rel_bias, zero)
    gs_small = all_gather_rows(g_small, "ag_small")
    summed = adamw_small(w_small, gs_small, m_small, v_small, "adamw_small")
    small_out = [unpack(a) for a in summed]
    loss = summed[0][4 * L + 1, 8 * L + N_BUCKETS * 8]

    def group(k):
        sm = small_out[k]
        return (sm[0], upd["gu1"][k], upd["d1"][k], sm[1], upd["in"][k], sm[2], sm[3], sm[4], upd["out"][k], sm[5],
                upd["gu2"][k], upd["d2"][k], sm[6], sm[7])

    return (loss, grad_x, *group(0), *group(1), *group(2), *group(3))
```

```python
import math

import jax
import jax.numpy as jnp
from jax import lax
from jax.experimental import pallas as pl
from jax.experimental.pallas import tpu as pltpu

F32 = jnp.float32
BF16 = jnp.bfloat16
S = jax.ShapeDtypeStruct

N_DEV = 8
HEAD_DIM = 64
SB_HEADS = 8
PAIR = 2 * HEAD_DIM
SB_W = 512
SWA_W = 512
KV_W = 128
IN_W = 3 * SB_W + SWA_W + 2 * KV_W
QB = 128
N_BUCKETS = 32
MAX_DISTANCE = 128
EPS = 1e-6
NEG_INF = -1e30
SCALE = HEAD_DIM ** -0.5

ADAM_LR = 0.001
ADAM_B1 = 0.9
ADAM_B2 = 0.999
ADAM_EPS = 1e-08
ADAM_WD = 0.01
ADAM_STEP = 10

VMEM_LIMIT = 56 * 1024 * 1024
MESH = pl.DeviceIdType.MESH


def _params(sem=None, vmem=VMEM_LIMIT):
    return pltpu.CompilerParams(dimension_semantics=sem, vmem_limit_bytes=vmem)


def _nn(a, b):
    return jnp.dot(a, b, preferred_element_type=F32)


def _nt(a, b):
    return lax.dot_general(a, b, (((1,), (1,)), ((), ())), preferred_element_type=F32)


def _tn(a, b):
    return lax.dot_general(a, b, (((0,), (0,)), ((), ())), preferred_element_type=F32)


def _tri(xs, m):
    return [_nn(x.astype(BF16), m) for x in xs]


def _rms(x, g):
    r = lax.rsqrt(jnp.mean(x * x, axis=-1, keepdims=True) + EPS)
    return x * r * g


def _rms_bwd(dy, x, g):
    r = lax.rsqrt(jnp.mean(x * x, axis=-1, keepdims=True) + EPS)
    xhat = x * r
    u = dy * g
    dx = r * (u - xhat * jnp.mean(u * xhat, axis=-1, keepdims=True))
    return dx, jnp.sum(dy * xhat, axis=0, keepdims=True)


def _softplus_logsig(z):
    sp = jnp.maximum(z, 0.0) + jnp.log(1.0 + jnp.exp(-jnp.abs(z)))
    return sp, z - sp


def _tile(n, want):
    t = min(n, want)
    while n % t:
        t //= 2
    return t


def _place():
    x, y, c = lax.axis_index("x"), lax.axis_index("y"), lax.axis_index("c")
    chips = [(1 - x, y), (x, 1 - y), (1 - x, 1 - y)]
    return x, y, c, chips


def all_gather_rows(v, name):
    R, C = v.shape

    def body(v_ref, out_ref, send_sems, recv_sems, local_sem):
        x, y, c, chips = _place()
        me, sibling = (x, y, c), (x, y, 1 - c)

        def slot(px, py, pc):
            return out_ref.at[4 * px + 2 * py + pc]

        def copy(k, block, to, src=None):
            return pltpu.make_async_remote_copy(
                src_ref=slot(*block) if src is None else src, dst_ref=slot(*block),
                send_sem=send_sems.at[k], recv_sem=recv_sems.at[k], device_id=to, device_id_type=MESH)

        mine = pltpu.make_async_copy(v_ref, slot(*me), local_sem)
        mine.start()
        first = [copy(0, me, sibling, src=v_ref)]
        first += [copy(1 + j, me, (*chip, c), src=v_ref) for j, chip in enumerate(chips)]
        for cp in first:
            cp.start()
        passed = [copy(4 + j, (*chip, c), sibling) for j, chip in enumerate(chips)]
        for j, chip in enumerate(chips):
            copy(1 + j, (*chip, c), me).wait_recv()
            passed[j].start()
        copy(0, sibling, me).wait_recv()
        for j, chip in enumerate(chips):
            copy(4 + j, (*chip, 1 - c), me).wait_recv()
        for cp in first + passed:
            cp.wait_send()
        mine.wait()

    return pl.pallas_call(
        body, name=name, out_shape=S((N_DEV, R, C), v.dtype),
        in_specs=[pl.BlockSpec(memory_space=pl.ANY)], out_specs=pl.BlockSpec(memory_space=pl.ANY),
        scratch_shapes=[pltpu.SemaphoreType.DMA((7,)), pltpu.SemaphoreType.DMA((7,)), pltpu.SemaphoreType.DMA],
    )(v)


class _Exchange:
    def __init__(self, ins, outs, sizes, n_local, plan, aliases=None):
        self.ins, self.outs, self.plan, self.aliases = list(ins), list(outs), plan, aliases or {}
        self.sizes, self.n_local = list(sizes), n_local

    def scratch(self):
        n = sum(self.sizes)
        return [pltpu.SemaphoreType.DMA((n,)), pltpu.SemaphoreType.DMA((n,)), pltpu.SemaphoreType.DMA((max(self.n_local, 1),))]

    def _copies(self, in_refs, out_refs, sems):
        send_sems, recv_sems, local_sems = sems
        phases, local = self.plan(in_refs, out_refs)
        out, k = [], 0
        for phase in phases:
            out.append([pltpu.make_async_remote_copy(src_ref=s, dst_ref=d, send_sem=send_sems.at[k + n], recv_sem=recv_sems.at[k + n],
                                                     device_id=dev, device_id_type=MESH) for n, (s, d, dev) in enumerate(phase)])
            k += len(phase)
        return out, [pltpu.make_async_copy(s, d, local_sems.at[n]) for n, (s, d) in enumerate(local)]

    def start(self, in_refs, out_refs, sems):
        phases, loc = self._copies(in_refs, out_refs, sems)
        for cp in phases[0] + loc:
            cp.start()

    def advance(self, hook, in_refs, out_refs, sems):
        p = hook - (3 - len(self.sizes))
        if p >= 1:
            phases, _ = self._copies(in_refs, out_refs, sems)
            for cp in phases[p - 1]:
                cp.wait_recv()
            for cp in phases[p]:
                cp.start()

    def finish(self, in_refs, out_refs, sems):
        phases, loc = self._copies(in_refs, out_refs, sems)
        for cp in phases[-1]:
            cp.wait_recv()
        for phase in phases:
            for cp in phase:
                cp.wait_send()
        for cp in loc:
            cp.wait()


def gather(v, rows=None, into=None):
    R, C = v.shape
    r0, nr = rows or (0, R)
    na = min(nr, ((nr // 2 + 15) // 16) * 16)

    def plan(ins, outs):
        x, y, c, _ = _place()
        xn, yn, dg, sibling = (1 - x, y), (x, 1 - y), (1 - x, 1 - y), (x, y, 1 - c)
        slot = lambda chip, start=r0, count=nr: outs[0].at[4 * chip[0] + 2 * chip[1] + c, pl.ds(start, count), :]
        src, mine = ins[0].at[pl.ds(r0, nr), :], slot((x, y))
        same = lambda ref, to: (ref, ref, to)
        first = [(src, mine, sibling), (src, mine, (*xn, c)), (src, mine, (*yn, c))]
        relay = [same(slot(xn, r0, na), (*yn, c)), same(slot(yn, r0 + na, nr - na), (*xn, c))]
        onward = [same(slot(xn), sibling), same(slot(yn), sibling), same(slot(dg), sibling)]
        return [first, relay, onward], [(src, mine)]

    if into is None:
        return _Exchange([v], [S((N_DEV, R, C), v.dtype)], (3, 2, 3), 1, plan)
    return _Exchange([v, into], [S((N_DEV, R, C), v.dtype)], (3, 2, 3), 1, plan, aliases={1: 0})


def scatter_first(gb):
    _, R, C = gb.shape

    def plan(ins, outs):
        x, y, c, chips = _place()
        owners = [(x, y)] + chips
        return [[(ins[0].at[4 * px + 2 * py + (1 - c)], outs[0].at[j], (x, y, 1 - c)) for j, (px, py) in enumerate(owners)]], []

    return _Exchange([gb], [S((4, R, C), BF16)], (4,), 0, plan)


def scatter_second(sb, rows=None, into=None):
    r0, nr = rows or (0, sb.shape[1])

    def plan(ins, outs):
        x, y, c, chips = _place()
        part = lambda ref, j: ref.at[j, pl.ds(r0, nr), :]
        return [[(part(ins[0], j), part(outs[0], j), (*chips[j], c)) for j in range(3)]], []

    if into is None:
        return _Exchange([sb], [S(sb.shape, BF16)], (3,), 0, plan)
    return _Exchange([sb, into], [S(sb.shape, BF16)], (3,), 0, plan, aliases={1: 0})


def _call(body, *, name, grid, in_specs, out_specs, out_shape, args, scratch=(), sem=None, riders=(), marks=None):
    single = not isinstance(out_shape, (tuple, list))
    out_shape = (out_shape,) if single else tuple(out_shape)
    out_specs = (out_specs,) if single else tuple(out_specs)
    n_in, n_out, n_sc = len(in_specs), len(out_shape), len(scratch)
    if not riders:
        res = pl.pallas_call(body, name=name, grid=grid, in_specs=list(in_specs), out_specs=out_specs, out_shape=out_shape,
                             scratch_shapes=list(scratch), compiler_params=_params(sem))(*args)
        return res[0] if single else res
    r_ins = [a for r in riders for a in r.ins]
    r_outs = [o for r in riders for o in r.outs]
    r_scr = [s for r in riders for s in r.scratch()]
    aliases, i0, o0 = {}, n_in, n_out
    for r in riders:
        for a, b in r.aliases.items():
            aliases[i0 + a] = o0 + b
        i0, o0 = i0 + len(r.ins), o0 + len(r.outs)
    steps = math.prod(grid)

    def full(*refs):
        ins, rin = refs[:n_in], refs[n_in:n_in + len(r_ins)]
        pos = n_in + len(r_ins)
        outs, rout = refs[pos:pos + n_out], refs[pos + n_out:pos + n_out + len(r_outs)]
        pos += n_out + len(r_outs)
        sc, rsc = refs[pos:pos + n_sc], refs[pos + n_sc:]
        step = 0
        for d, n in enumerate(grid):
            step = step * n + pl.program_id(d)

        def each(method, *lead):
            i, o = 0, 0
            for k, r in enumerate(riders):
                getattr(r, method)(*lead, rin[i:i + len(r.ins)], rout[o:o + len(r.outs)], rsc[3 * k:3 * k + 3])
                i, o = i + len(r.ins), o + len(r.outs)

        @pl.when(step == 0)
        def _():
            each("start")
        body(*ins, *outs, *sc)

        late = max(steps - 1 - max(steps // 8, 1), 0)
        first, second = marks or (min((3 * steps) // 5, late), late)

        @pl.when(step == first)
        def _():
            each("advance", 1)

        @pl.when(step == second)
        def _():
            each("advance", 2)

        @pl.when(step == steps - 1)
        def _():
            each("finish")

    anywhere = pl.BlockSpec(memory_space=pl.ANY)
    res = pl.pallas_call(
        full, name=name, grid=grid, in_specs=list(in_specs) + [anywhere] * len(r_ins),
        out_specs=out_specs + (anywhere,) * len(r_outs), out_shape=out_shape + tuple(r_outs),
        scratch_shapes=list(scratch) + r_scr, input_output_aliases=aliases,
        compiler_params=_params(("arbitrary",) * len(grid)))(*args, *r_ins)
    host, rest, per = res[:n_out], list(res[n_out:]), []
    for r in riders:
        per.append(rest[:len(r.outs)])
        rest = rest[len(r.outs):]
    return (host[0] if single else tuple(host)), per


def _rows_tile(n, cap):
    return max(t for t in range(16, min(n, cap) + 1, 16) if n % t == 0)


def scatter_add(g, ra, name):
    _, R, C = g.shape
    tr = _rows_tile(R, 176)
    x, y, c, chips = _place()
    slots = jnp.stack([4 * px + 2 * py + c for px, py in [(x, y)] + chips]).astype(jnp.int32)

    def body(s_ref, g0, g1, g2, g3, ra_ref, own_ref, sb_ref):
        own_ref[...] = g0[...] + ra_ref[0].astype(F32)
        for j, gj in enumerate((g1, g2, g3)):
            sb_ref[j] = (gj[...] + ra_ref[j + 1].astype(F32)).astype(BF16)

    spec = pltpu.PrefetchScalarGridSpec(
        num_scalar_prefetch=1, grid=(R // tr,),
        in_specs=[pl.BlockSpec((None, tr, C), lambda i, s, j=j: (s[j], i, 0)) for j in range(4)]
        + [pl.BlockSpec((4, tr, C), lambda i, s: (0, i, 0))],
        out_specs=(pl.BlockSpec((tr, C), lambda i, s: (i, 0)), pl.BlockSpec((3, tr, C), lambda i, s: (0, i, 0))))
    return pl.pallas_call(body, name=name, grid_spec=spec, out_shape=(S((R, C), F32), S((3, R, C), BF16)),
                          compiler_params=_params(("parallel",)))(slots, g, g, g, g, ra)


def rms_cast(h, g, name, riders=()):
    T, D = h.shape
    tm = _tile(T, 512)

    def body(h_ref, g_ref, n_ref):
        n_ref[...] = _rms(h_ref[...], g_ref[...]).astype(BF16)

    row = pl.BlockSpec((tm, D), lambda i: (i, 0))
    return _call(body, name=name, grid=(T // tm,), out_shape=S((T, D), BF16), in_specs=[row, pl.BlockSpec((1, D), lambda i: (0, 0))],
                 out_specs=row, sem=("parallel",), args=(h, g), riders=riders)


def ffn_up_fwd(n, wgu, name, riders=()):
    T, D = n.shape
    F = wgu.shape[1]
    tr, tn = _tile(T, 512), _tile(F, 256)

    def body(n_ref, wg_ref, wu_ref, dgate_ref, dup_ref, a_ref):
        wg, wu = wg_ref[...], wu_ref[...]
        for r in range(T // tr):
            rows = slice(r * tr, (r + 1) * tr)
            x = n_ref[rows, :]
            gate = _nt(x, wg)
            up = _nt(x, wu)
            s = jax.nn.sigmoid(gate)
            silu = gate * s
            dgate_ref[rows, :] = (up * (s * (1.0 + gate * (1.0 - s)))).astype(BF16)
            dup_ref[rows, :] = silu.astype(BF16)
            a_ref[rows, :] = (silu * up).astype(BF16)

    tile = pl.BlockSpec((T, tn), lambda j: (0, j))
    return _call(
        body, name=name, grid=(F // tn,), out_shape=(S((T, F), BF16),) * 3,
        in_specs=[pl.BlockSpec((T, D), lambda j: (0, 0)),
                  pl.BlockSpec((None, tn, D), lambda j: (0, j, 0)), pl.BlockSpec((None, tn, D), lambda j: (1, j, 0))],
        out_specs=(tile, tile, tile), sem=("parallel",), args=(n, wgu, wgu), riders=riders)


def ffn_down_fwd(a, wd, h, g_next, name, riders=()):
    T, F = a.shape
    D = wd.shape[1]
    tm = _tile(T, 256)

    def body(a_ref, w_ref, h_ref, *rest):
        out = h_ref[...] + 0.5 * _nn(a_ref[...], w_ref[...])
        if g_next is None:
            rest[0][...] = out
        else:
            g_ref, o_ref, n_ref = rest
            o_ref[...] = out
            n_ref[...] = _rms(out, g_ref[...]).astype(BF16)

    row = pl.BlockSpec((tm, D), lambda i: (i, 0))
    more = g_next is not None
    return _call(
        body, name=name, grid=(T // tm,), out_shape=(S((T, D), F32), S((T, D), BF16)) if more else S((T, D), F32),
        in_specs=[pl.BlockSpec((tm, F), lambda i: (i, 0)), pl.BlockSpec((F, D), lambda i: (0, 0)), row]
        + ([pl.BlockSpec((1, D), lambda i: (0, 0))] if more else []),
        out_specs=(row, row) if more else row,
        sem=("parallel",), args=(a, wd, h) + ((g_next,) if more else ()), riders=riders)


def mix_in_fwd(h, g, win, name):
    T, D = h.shape
    N = win.shape[0]
    tm = _tile(T, 256)

    def body(h_ref, g_ref, w_ref, n_ref, p_ref):
        n = _rms(h_ref[...], g_ref[...]).astype(BF16)
        n_ref[...] = n
        p_ref[...] = _nt(n, w_ref[...]).astype(BF16)

    return pl.pallas_call(
        body, name=name, grid=(T // tm,), out_shape=(S((T, D), BF16), S((T, N), BF16)),
        in_specs=[pl.BlockSpec((tm, D), lambda i: (i, 0)), pl.BlockSpec((1, D), lambda i: (0, 0)),
                  pl.BlockSpec((N, D), lambda i: (0, 0))],
        out_specs=(pl.BlockSpec((tm, D), lambda i: (i, 0)), pl.BlockSpec((tm, N), lambda i: (i, 0))),
        compiler_params=_params(("parallel",)),
    )(h, g, win)


def _tri_consts():
    r = lax.broadcasted_iota(jnp.int32, (QB, QB), 0)
    c = lax.broadcasted_iota(jnp.int32, (QB, QB), 1)
    ones = jnp.ones((QB, QB), BF16)
    with_sums = lambda tri: jnp.concatenate([tri.astype(BF16), ones], axis=1)
    return with_sums(r > c), with_sums(r <= c), with_sums(r < c)


def _half_masks():
    lane = lax.broadcasted_iota(jnp.int32, (QB, PAIR), 1)
    row = lax.broadcasted_iota(jnp.int32, (QB, PAIR), 0)
    return lane < HEAD_DIM, lane, row


def sb_attn_fwd(p, after, name, riders=()):
    T = p.shape[0]
    nq = T // QB

    def body(q_ref, k_ref, v_ref, m_ref, o_ref, tot_ref, q_sc, acc_ref, z_sc):
        i = pl.program_id(0)
        lo, lane, row = _half_masks()
        causal = lane < row
        heads, pairs = range(SB_HEADS), range(SB_HEADS // 2)
        for hp in pairs:
            q_sc[hp] = (q_ref[:, hp * PAIR:(hp + 1) * PAIR].astype(F32) * SCALE).astype(BF16)
        m2 = m_ref[...]

        def by_head(ref, j, hp):
            t = ref[pl.ds(pl.multiple_of(j * QB, QB), QB), hp * PAIR:(hp + 1) * PAIR]
            return jnp.concatenate([jnp.where(lo, t, 0), jnp.where(lo, 0, t)], axis=0)

        def scores(j):
            return [_nt(q_sc[hp], by_head(k_ref, j, hp)) for hp in pairs]

        def block(j, diag):
            z2 = [z_sc[hp] for hp in pairs]
            ahead = scores(jnp.maximum(j - 1, 0))
            for hp in pairs:
                z_sc[hp] = ahead[hp]
            vs = [by_head(v_ref, j, hp) for hp in pairs]
            spls = [_softplus_logsig(z2[h // 2][:, (h % 2) * QB:(h % 2 + 1) * QB]) for h in heads]
            sp = [jnp.where(causal, spls[h][0], 0.0) if diag else spls[h][0] for h in heads]
            rr = _tri(sp, m2)
            if diag:
                w = [jnp.where(causal, jnp.exp(spls[h][1] - rr[h][:, :QB]), 0.0).astype(BF16) for h in heads]
            else:
                c = [tot_ref[:, h * QB:(h + 1) * QB] for h in heads]
                w = [jnp.exp(spls[h][1] - (c[h] + rr[h][:, :QB])).astype(BF16) for h in heads]
            pv = [_nn(jnp.concatenate([w[2 * hp], w[2 * hp + 1]], axis=1), vs[hp]) for hp in pairs]
            for hp in pairs:
                acc_ref[hp] = pv[hp] if diag else acc_ref[hp] + pv[hp]
            for h in heads:
                tot_ref[:, h * QB:(h + 1) * QB] = rr[h][:, QB:] if diag else c[h] + rr[h][:, QB:]

        first = scores(i)
        for hp in pairs:
            z_sc[hp] = first[hp]
        block(i, True)

        def step(t, carry):
            block(i - 1 - t, False)
            return carry
        lax.fori_loop(0, i, step, 0)
        for hp in pairs:
            o_ref[:, hp * PAIR:(hp + 1) * PAIR] = acc_ref[hp]

    npair = SB_HEADS // 2
    return _call(
        body, name=name, grid=(nq,), out_shape=(S((T, SB_W), F32), S((T, SB_HEADS * QB), F32)),
        in_specs=[pl.BlockSpec((QB, SB_W), lambda i: (i, 0)), pl.BlockSpec((T, SB_W), lambda i: (0, 1)),
                  pl.BlockSpec((T, SB_W), lambda i: (0, 2)), pl.BlockSpec((QB, 2 * QB), lambda i: (0, 0))],
        out_specs=(pl.BlockSpec((QB, SB_W), lambda i: (i, 0)), pl.BlockSpec((QB, SB_HEADS * QB), lambda i: (i, 0))),
        scratch=[pltpu.VMEM((npair, QB, PAIR), BF16), pltpu.VMEM((npair, QB, PAIR), F32), pltpu.VMEM((npair, QB, 2 * QB), F32)],
        sem=("arbitrary",), args=(p, p, p, after), riders=riders,
        marks=((11 * nq) // 16, (14 * nq) // 16))


def sb_attn_bwd(p, do, tot, upto, before, name, riders=()):
    T = p.shape[0]
    nq = T // QB

    def body(q_ref, k_ref, v_ref, do_ref, tot_ref, mp_ref, mg_ref, dq_ref, dk_ref, dv_ref,
             q_sc, d_sc, qd_sc, pg_sc, dq_acc, dk_acc, dv_acc, zd_sc):
        i = pl.program_id(0)
        lo, lane, row = _half_masks()
        causal = lane < row
        heads, pairs = range(SB_HEADS), range(SB_HEADS // 2)

        def by_head(t):
            return jnp.concatenate([jnp.where(lo, t, 0), jnp.where(lo, 0, t)], axis=0)

        for hp in pairs:
            q2 = (q_ref[:, hp * PAIR:(hp + 1) * PAIR].astype(F32) * SCALE).astype(BF16)
            d2 = do_ref[:, hp * PAIR:(hp + 1) * PAIR].astype(BF16)
            q_sc[hp] = q2
            d_sc[hp] = d2
            qd_sc[hp] = by_head(q2)
            qd_sc[SB_HEADS // 2 + hp] = by_head(d2)
        mp, mg = mp_ref[...], mg_ref[...]

        @pl.when(i == 0)
        def _():
            dk_acc[...] = jnp.zeros_like(dk_acc)
            dv_acc[...] = jnp.zeros_like(dv_acc)
        pg_sc[...] = jnp.zeros_like(pg_sc)
        dq_acc[...] = jnp.zeros_like(dq_acc)

        def rows(ref, j, hp):
            return ref[pl.ds(pl.multiple_of(j * QB, QB), QB), hp * PAIR:(hp + 1) * PAIR]

        def products(j):
            return ([_nt(q_sc[hp], by_head(rows(k_ref, j, hp))) for hp in pairs]
                    + [_nt(d_sc[hp], by_head(rows(v_ref, j, hp))) for hp in pairs])

        def block(j, diag):
            r0 = pl.multiple_of(j * QB, QB)
            half = lambda t, h: t[:, (h % 2) * QB:(h % 2 + 1) * QB]
            z = [half(zd_sc[h // 2], h) for h in heads]
            dw = [half(zd_sc[SB_HEADS // 2 + h // 2], h) for h in heads]
            if not diag:
                ahead = products(j + 1)
                for hp in range(SB_HEADS):
                    zd_sc[hp] = ahead[hp]
            ks = [by_head(rows(k_ref, j, hp)) for hp in pairs]
            spls = [_softplus_logsig(z[h]) for h in heads]
            sp = [jnp.where(causal, spls[h][0], 0.0) if diag else spls[h][0] for h in heads]
            rr = _tri(sp, mp)
            pc = [pg_sc[2 * h] for h in heads]
            w = [jnp.exp(spls[h][1] - (tot_ref[:, h * QB:(h + 1) * QB] - (pc[h] + rr[h][:, :QB]))) for h in heads]
            if diag:
                w = [jnp.where(causal, w[h], 0.0) for h in heads]
            gg = [dw[h] * w[h] for h in heads]
            rg = _tri(gg, mg)
            gc = [pg_sc[2 * h + 1] for h in heads]
            dz = [gg[h] - (gg[h] + gc[h] + rg[h][:, :QB]) * jnp.exp(spls[h][1]) for h in heads]
            if diag:
                dz = [jnp.where(causal, dz[h], 0.0) for h in heads]
            dzb = [dz[h].astype(BF16) for h in heads]
            wb = [w[h].astype(BF16) for h in heads]
            both = lambda t, hp, axis: jnp.concatenate([t[2 * hp], t[2 * hp + 1]], axis=axis)
            dq = [_nn(both(dzb, hp, 1), ks[hp]) for hp in pairs]
            dk = [_tn(both(dzb, hp, 0), qd_sc[hp]) for hp in pairs]
            dv = [_tn(both(wb, hp, 0), qd_sc[SB_HEADS // 2 + hp]) for hp in pairs]
            for h in heads:
                if not diag:
                    pg_sc[2 * h] = pc[h] + rr[h][:, QB:]
                    pg_sc[2 * h + 1] = gc[h] + rg[h][:, QB:]
            for hp in pairs:
                dq_acc[hp] += dq[hp]
                dk_acc[pl.ds(r0, QB), hp * PAIR:(hp + 1) * PAIR] += dk[hp]
                dv_acc[pl.ds(r0, QB), hp * PAIR:(hp + 1) * PAIR] += dv[hp]

        first = products(0)
        for hp in range(SB_HEADS):
            zd_sc[hp] = first[hp]

        def step(t, carry):
            block(t, False)
            return carry
        lax.fori_loop(0, i, step, 0)
        block(i, True)
        for hp in pairs:
            dq_ref[:, hp * PAIR:(hp + 1) * PAIR] = (dq_acc[hp] * SCALE).astype(BF16)

        @pl.when(i == nq - 1)
        def _():
            dk_ref[...] = dk_acc[...].astype(BF16)
            dv_ref[...] = dv_acc[...].astype(BF16)

    qtile = pl.BlockSpec((QB, SB_W), lambda i: (i, 0))
    whole = pl.BlockSpec((T, SB_W), lambda i: (0, 0))
    const = pl.BlockSpec((QB, 2 * QB), lambda i: (0, 0))
    return _call(
        body, name=name, grid=(nq,), out_shape=(S((T, SB_W), BF16),) * 3,
        in_specs=[qtile, pl.BlockSpec((T, SB_W), lambda i: (0, 1)), pl.BlockSpec((T, SB_W), lambda i: (0, 2)), qtile,
                  pl.BlockSpec((QB, SB_HEADS * QB), lambda i: (i, 0)), const, const],
        out_specs=(qtile, whole, whole),
        scratch=[pltpu.VMEM((SB_HEADS // 2, QB, PAIR), BF16), pltpu.VMEM((SB_HEADS // 2, QB, PAIR), BF16),
                 pltpu.VMEM((SB_HEADS, 2 * QB, PAIR), BF16),
                 pltpu.VMEM((2 * SB_HEADS, QB, QB), F32), pltpu.VMEM((SB_HEADS // 2, QB, PAIR), F32),
                 pltpu.VMEM((T, SB_W), F32), pltpu.VMEM((T, SB_W), F32), pltpu.VMEM((SB_HEADS, QB, 2 * QB), F32)],
        sem=("arbitrary",), args=(p, p, p, do, tot, upto, before), riders=riders)


def _t5_buckets():
    a = lax.broadcasted_iota(jnp.int32, (QB, QB), 0)
    c = lax.broadcasted_iota(jnp.int32, (QB, QB), 1)

    def bucket(dist):
        dist = jnp.maximum(dist, 0)
        max_exact = N_BUCKETS // 2
        d = jnp.maximum(dist, 1).astype(F32)
        large = max_exact + (jnp.log(d / max_exact) / math.log(MAX_DISTANCE / max_exact)
                             * (N_BUCKETS - max_exact)).astype(jnp.int32)
        large = jnp.minimum(large, N_BUCKETS - 1)
        return jnp.where(dist < max_exact, dist, large)

    return bucket(QB + a - c), bucket(a - c)


def _swa_common(i, kp_ref, kc_ref, vp_ref, vc_ref, bp_ref, bc_ref, rb_ref, bias_ref):
    lo, lane, row = _half_masks()

    @pl.when(i == 0)
    def _():
        for blk, b_ref in enumerate((bp_ref, bc_ref)):
            bk = b_ref[...]
            for h in range(8):
                acc = jnp.zeros((QB, QB), F32)
                for b in range(N_BUCKETS):
                    acc = jnp.where(bk == b, rb_ref[b, h], acc)
                bias_ref[h, blk] = acc

    band = [(lane > row) & (i > 0), lane <= row]

    def stacks(ref):
        t = ref[...].astype(F32)
        sw = pltpu.roll(t, HEAD_DIM, 1)
        return [jnp.concatenate([jnp.where(lo, t, 0.0), jnp.where(lo, 0.0, sw)], axis=0).astype(BF16),
                jnp.concatenate([jnp.where(lo, sw, 0.0), jnp.where(lo, 0.0, t)], axis=0).astype(BF16)]

    ks = [stacks(kp_ref), stacks(kc_ref)]
    vs = [stacks(vp_ref), stacks(vc_ref)]
    return lo, band, ks, vs


def _lane_half(t, h):
    return t[:, (h % 2) * QB:(h % 2 + 1) * QB]


def swa_fwd(p, sinks, rel_bias, bprev, bcur, name, riders=()):
    T = p.shape[0]
    nq = T // QB
    kcol, vcol = (3 * SB_W + SWA_W) // KV_W, (3 * SB_W + SWA_W) // KV_W + 1

    def body(q_ref, kp_ref, kc_ref, vp_ref, vc_ref, bp_ref, bc_ref, sink_ref, rb_ref, o_ref, lse_ref, bias_ref):
        i = pl.program_id(0)
        lo, band, ks, vs = _swa_common(i, kp_ref, kc_ref, vp_ref, vc_ref, bp_ref, bc_ref, rb_ref, bias_ref)
        heads, pairs, blocks = range(8), range(4), range(2)
        rowmax = lambda t: jnp.max(t, axis=1, keepdims=True)
        rowsum = lambda t: jnp.sum(t, axis=1, keepdims=True)
        q2 = [q_ref[:, g * PAIR:(g + 1) * PAIR] for g in pairs]
        s2 = [[_nt(q2[g], ks[b][g // 2]) for b in blocks] for g in pairs]
        sc = [[jnp.where(band[b], _lane_half(s2[h // 2][b], h) * SCALE + bias_ref[h, b], NEG_INF) for b in blocks] for h in heads]
        sink = [sink_ref[0, h] for h in heads]
        m = [jnp.maximum(jnp.maximum(rowmax(sc[h][0]), rowmax(sc[h][1])), sink[h]) for h in heads]
        e = [[jnp.exp(sc[h][b] - m[h]) for b in blocks] for h in heads]
        den = [rowsum(e[h][0]) + rowsum(e[h][1]) + jnp.exp(sink[h] - m[h]) for h in heads]
        pb = [[(e[h][b] / den[h]).astype(BF16) for b in blocks] for h in heads]
        for g in pairs:
            both = lambda b: jnp.concatenate([pb[2 * g][b], pb[2 * g + 1][b]], axis=1)
            o_ref[:, g * PAIR:(g + 1) * PAIR] = _nn(both(0), vs[0][g // 2]) + _nn(both(1), vs[1][g // 2])
        for h in heads:
            lse_ref[:, h * QB:(h + 1) * QB] = jnp.broadcast_to(m[h] + jnp.log(den[h]), (QB, QB))

    kv = lambda col, prev: pl.BlockSpec((QB, KV_W), (lambda i: (jnp.maximum(i - 1, 0), col)) if prev else (lambda i: (i, col)))
    full = pl.BlockSpec((QB, QB), lambda i: (0, 0))
    smem = pl.BlockSpec(memory_space=pltpu.SMEM)
    return _call(
        body, name=name, grid=(nq,), out_shape=(S((T, SWA_W), F32), S((T, 8 * QB), F32)),
        in_specs=[pl.BlockSpec((QB, SWA_W), lambda i: (i, 3)), kv(kcol, True), kv(kcol, False), kv(vcol, True), kv(vcol, False),
                  full, full, smem, smem],
        out_specs=(pl.BlockSpec((QB, SWA_W), lambda i: (i, 0)), pl.BlockSpec((QB, 8 * QB), lambda i: (i, 0))),
        scratch=[pltpu.VMEM((8, 2, QB, QB), F32)],
        sem=("arbitrary",), args=(p, p, p, p, p, bprev, bcur, sinks, rel_bias), riders=riders)


def swa_bwd(p, do, lse, sinks, rel_bias, bprev, bcur, name, riders=()):
    T = p.shape[0]
    nq = T // QB
    kcol, vcol = (3 * SB_W + SWA_W) // KV_W, (3 * SB_W + SWA_W) // KV_W + 1

    def body(q_ref, kp_ref, kc_ref, vp_ref, vc_ref, do_ref, lse_ref, bp_ref, bc_ref, sink_ref, rb_ref,
             dq_ref, dk_ref, dv_ref, dsink_ref, dsc_ref, bias_ref, dk_acc, dv_acc):
        i = pl.program_id(0)
        lo, band, ks, vs = _swa_common(i, kp_ref, kc_ref, vp_ref, vc_ref, bp_ref, bc_ref, rb_ref, bias_ref)

        @pl.when(i == 0)
        def _():
            dk_acc[...] = jnp.zeros_like(dk_acc)
            dv_acc[...] = jnp.zeros_like(dv_acc)
            dsc_ref[...] = jnp.zeros_like(dsc_ref)
            dsink_ref[...] = jnp.zeros_like(dsink_ref)

        heads, pairs, blocks = range(8), range(4), range(2)
        rowsum = lambda t: jnp.sum(t, axis=1, keepdims=True)
        by_head = lambda t: jnp.concatenate([jnp.where(lo, t, 0), jnp.where(lo, 0, t)], axis=0)
        q2 = [q_ref[:, g * PAIR:(g + 1) * PAIR] for g in pairs]
        d2 = [do_ref[:, g * PAIR:(g + 1) * PAIR].astype(BF16) for g in pairs]
        qs = [by_head(q2[g]) for g in pairs]
        dos = [by_head(d2[g]) for g in pairs]
        s2 = [[_nt(q2[g], ks[b][g // 2]) for b in blocks] for g in pairs]
        dp2 = [[_nt(d2[g], vs[b][g // 2]) for b in blocks] for g in pairs]
        lse_h = [lse_ref[:, h * QB:(h + 1) * QB] for h in heads]
        sink = [sink_ref[0, h] for h in heads]
        pr = [[jnp.exp(jnp.where(band[b], _lane_half(s2[h // 2][b], h) * SCALE + bias_ref[h, b], NEG_INF) - lse_h[h])
               for b in blocks] for h in heads]
        dp = [[_lane_half(dp2[h // 2][b], h) for b in blocks] for h in heads]
        delta = [rowsum(pr[h][0] * dp[h][0]) + rowsum(pr[h][1] * dp[h][1]) for h in heads]
        lane1 = lax.broadcasted_iota(jnp.int32, (1, QB), 1)
        dsink = jnp.zeros((1, QB), F32)
        for h in heads:
            dsink = dsink + jnp.where(lane1 == h, -jnp.sum(jnp.exp(sink[h] - lse_h[h][:, :1]) * delta[h]), 0.0)
        dsink_ref[...] += dsink
        dsc = [[pr[h][b] * (dp[h][b] - delta[h]) for b in blocks] for h in heads]
        for h in heads:
            for b in blocks:
                dsc_ref[h, b] += dsc[h][b]
        dzb = [[(dsc[h][b] * SCALE).astype(BF16) for b in blocks] for h in heads]
        prb = [[pr[h][b].astype(BF16) for b in blocks] for h in heads]
        pair_of = lambda t, g, b, axis: jnp.concatenate([t[2 * g][b], t[2 * g + 1][b]], axis=axis)
        for g in pairs:
            dq = _nn(pair_of(dzb, g, 0, 1), ks[0][g // 2]) + _nn(pair_of(dzb, g, 1, 1), ks[1][g // 2])
            dq_ref[:, g * PAIR:(g + 1) * PAIR] = dq.astype(BF16)

        def key_grad(t, other, b):
            per_kv = [_tn(pair_of(t, 2 * kh, b, 0), other[2 * kh]) + _tn(pair_of(t, 2 * kh + 1, b, 0), other[2 * kh + 1]) for kh in range(2)]
            both = [s + pltpu.roll(s, HEAD_DIM, 1) for s in per_kv]
            return jnp.where(lo, both[0], both[1])

        rp = pl.multiple_of(jnp.maximum(i - 1, 0) * QB, QB)
        rc = pl.multiple_of(i * QB, QB)
        dk_acc[pl.ds(rp, QB), :] += key_grad(dzb, qs, 0)
        dv_acc[pl.ds(rp, QB), :] += key_grad(prb, dos, 0)
        dk_acc[pl.ds(rc, QB), :] += key_grad(dzb, qs, 1)
        dv_acc[pl.ds(rc, QB), :] += key_grad(prb, dos, 1)

        @pl.when(i == nq - 1)
        def _():
            dk_ref[...] = dk_acc[...].astype(BF16)
            dv_ref[...] = dv_acc[...].astype(BF16)

    kv = lambda col, prev: pl.BlockSpec((QB, KV_W), (lambda i: (jnp.maximum(i - 1, 0), col)) if prev else (lambda i: (i, col)))
    full = pl.BlockSpec((QB, QB), lambda i: (0, 0))
    smem = pl.BlockSpec(memory_space=pltpu.SMEM)
    whole = lambda shape: pl.BlockSpec(shape, lambda i: (0,) * len(shape))
    return _call(
        body, name=name, grid=(nq,),
        out_shape=(S((T, SWA_W), BF16), S((T, KV_W), BF16), S((T, KV_W), BF16), S((1, QB), F32), S((8, 2, QB, QB), F32)),
        in_specs=[pl.BlockSpec((QB, SWA_W), lambda i: (i, 3)), kv(kcol, True), kv(kcol, False), kv(vcol, True), kv(vcol, False),
                  pl.BlockSpec((QB, SWA_W), lambda i: (i, 0)), pl.BlockSpec((QB, 8 * QB), lambda i: (i, 0)),
                  full, full, smem, smem],
        out_specs=(pl.BlockSpec((QB, SWA_W), lambda i: (i, 0)), whole((T, KV_W)), whole((T, KV_W)), whole((1, QB)),
                   whole((8, 2, QB, QB))),
        scratch=[pltpu.VMEM((8, 2, QB, QB), F32), pltpu.VMEM((T, KV_W), F32), pltpu.VMEM((T, KV_W), F32)],
        sem=("arbitrary",), args=(p, p, p, p, p, do, lse, bprev, bcur, sinks, rel_bias), riders=riders)


def mix_out_fwd(o_sb, o_sw, g_sb, g_sw, wout, h, g_next, name, riders=()):
    T, D = h.shape
    M = SB_W + SWA_W
    tm = _tile(T, 256)

    def body(a_ref, b_ref, ga_ref, gb_ref, w_ref, h_ref, gn_ref, mx_ref, o_ref, n_ref):
        mx_ref[:, :SB_W] = _rms(a_ref[...], ga_ref[...]).astype(BF16)
        mx_ref[:, SB_W:] = _rms(b_ref[...], gb_ref[...]).astype(BF16)
        out = h_ref[...] + _nn(mx_ref[...], w_ref[...])
        o_ref[...] = out
        n_ref[...] = _rms(out, gn_ref[...]).astype(BF16)

    row = lambda n: pl.BlockSpec((tm, n), lambda i: (i, 0))
    vec = lambda n: pl.BlockSpec((1, n), lambda i: (0, 0))
    return _call(
        body, name=name, grid=(T // tm,), out_shape=(S((T, M), BF16), S((T, D), F32), S((T, D), BF16)),
        in_specs=[row(SB_W), row(SWA_W), vec(SB_W), vec(SWA_W), pl.BlockSpec((M, D), lambda i: (0, 0)), row(D), vec(D)],
        out_specs=(row(M), row(D), row(D)),
        sem=("parallel",), args=(o_sb, o_sw, g_sb, g_sw, wout, h, g_next), riders=riders)


def loss_head(h, g, target, name):
    T, D = h.shape
    tm = _tile(T, 256)

    def body(h_ref, g_ref, t_ref, loss_ref, dh_ref, dhb_ref, dg_ref):
        @pl.when(pl.program_id(0) == 0)
        def _():
            loss_ref[...] = jnp.zeros_like(loss_ref)
            dg_ref[...] = jnp.zeros_like(dg_ref)
        x = h_ref[...]
        err = _rms(x, g_ref[...]) - t_ref[...]
        loss_ref[...] += jnp.full((1, QB), 0.5 * jnp.sum(jnp.mean(err * err, axis=-1)), F32)
        dx, dg = _rms_bwd(err / D, x, g_ref[...])
        dh_ref[...] = dx
        dhb_ref[...] = dx.astype(BF16)
        dg_ref[...] += dg

    row = pl.BlockSpec((tm, D), lambda i: (i, 0))
    vec = pl.BlockSpec((1, D), lambda i: (0, 0))
    return pl.pallas_call(
        body, name=name, grid=(T // tm,), out_shape=(S((1, QB), F32), S((T, D), F32), S((T, D), BF16), S((1, D), F32)),
        in_specs=[row, vec, row], out_specs=(pl.BlockSpec((1, QB), lambda i: (0, 0)), row, row, vec),
        compiler_params=_params(("arbitrary",)),
    )(h, g, target)


def ffn_down_bwd(dhb, wd, gate, up, name, riders=()):
    T, D = dhb.shape
    F = wd.shape[0]
    tr, tn = _tile(T, 512), _tile(F, 256)

    def body(d_ref, w_ref, g_ref, u_ref, o_ref):
        w = w_ref[...]
        for r in range(T // tr):
            rows = slice(r * tr, (r + 1) * tr)
            da = 0.5 * _nt(d_ref[rows, :], w)
            o_ref[0, rows, :] = (da * g_ref[rows, :].astype(F32)).astype(BF16)
            o_ref[1, rows, :] = (da * u_ref[rows, :].astype(F32)).astype(BF16)

    tile = pl.BlockSpec((T, tn), lambda j: (0, j))
    return _call(
        body, name=name, grid=(F // tn,), out_shape=S((2, T, F), BF16),
        in_specs=[pl.BlockSpec((T, D), lambda j: (0, 0)), pl.BlockSpec((tn, D), lambda j: (j, 0)), tile, tile],
        out_specs=pl.BlockSpec((2, T, tn), lambda j: (0, 0, j)),
        sem=("parallel",), args=(dhb, wd, gate, up), riders=riders)


def tn_matmul(xs, y, alpha, name, riders=()):
    B, T, N = xs.shape
    D = y.shape[1]
    tn = _tile(N, 256)

    def body(x_ref, y_ref, o_ref, ob_ref):
        o = alpha * _tn(x_ref[...], y_ref[...])
        o_ref[...] = o
        ob_ref[...] = o.astype(BF16)

    tile = pl.BlockSpec((None, tn, D), lambda s, j: (s, j, 0))
    return _call(
        body, name=name, grid=(B, N // tn), out_shape=(S((B, N, D), F32), S((B, N, D), BF16)),
        in_specs=[pl.BlockSpec((None, T, tn), lambda s, j: (s, 0, j)), pl.BlockSpec((T, D), lambda s, j: (0, 0))],
        out_specs=(tile, tile), sem=("parallel", "parallel"), args=(xs, y), riders=riders)


def nn_rms_bwd(xs, ws, h_in, g, dh, name, riders=()):
    B, T, K = xs.shape
    D = ws.shape[2]
    tm = _tile(T, 256)

    def body(x_ref, w_ref, h_ref, g_ref, d_ref, o_ref, ob_ref, dg_ref):
        @pl.when(pl.program_id(0) == 0)
        def _():
            dg_ref[...] = jnp.zeros_like(dg_ref)
        dn = _nn(x_ref[0], w_ref[0])
        for s in range(1, B):
            dn = dn + _nn(x_ref[s], w_ref[s])
        dx, dg = _rms_bwd(dn, h_ref[...], g_ref[...])
        out = d_ref[...] + dx
        o_ref[...] = out
        ob_ref[...] = out.astype(BF16)
        dg_ref[...] += dg

    row = pl.BlockSpec((tm, D), lambda i: (i, 0))
    vec = pl.BlockSpec((1, D), lambda i: (0, 0))
    return _call(
        body, name=name, grid=(T // tm,), out_shape=(S((T, D), F32), S((T, D), BF16), S((1, D), F32)),
        in_specs=[pl.BlockSpec((B, tm, K), lambda i: (0, i, 0)), pl.BlockSpec((B, K, D), lambda i: (0, 0, 0)), row, vec, row],
        out_specs=(row, row, vec),
        sem=("arbitrary",), args=(xs, ws, h_in, g, dh), riders=riders)


def mix_out_bwd(dhb, wout, o_sb, o_sw, g_sb, g_sw, name):
    T, D = dhb.shape
    tm = _tile(T, 256)

    def body(d_ref, w_ref, a_ref, b_ref, ga_ref, gb_ref, da_ref, db_ref, dga_ref, dgb_ref):
        @pl.when(pl.program_id(0) == 0)
        def _():
            dga_ref[...] = jnp.zeros_like(dga_ref)
            dgb_ref[...] = jnp.zeros_like(dgb_ref)
        dm = _nt(d_ref[...], w_ref[...])
        dxa, dga = _rms_bwd(dm[:, :SB_W], a_ref[...], ga_ref[...])
        dxb, dgb = _rms_bwd(dm[:, SB_W:], b_ref[...], gb_ref[...])
        da_ref[...] = dxa
        db_ref[...] = dxb
        dga_ref[...] += dga
        dgb_ref[...] += dgb

    row = lambda n: pl.BlockSpec((tm, n), lambda i: (i, 0))
    vec = lambda n: pl.BlockSpec((1, n), lambda i: (0, 0))
    return pl.pallas_call(
        body, name=name, grid=(T // tm,),
        out_shape=(S((T, SB_W), F32), S((T, SWA_W), F32), S((1, SB_W), F32), S((1, SWA_W), F32)),
        in_specs=[row(D), pl.BlockSpec((SB_W + SWA_W, D), lambda i: (0, 0)), row(SB_W), row(SWA_W), vec(SB_W), vec(SWA_W)],
        out_specs=(row(SB_W), row(SWA_W), vec(SB_W), vec(SWA_W)),
        compiler_params=_params(("arbitrary",)),
    )(dhb, wout, o_sb, o_sw, g_sb, g_sw)


def rel_bias_grad(dscs, bprev, bcur, name):
    n = len(dscs)

    def body(*refs):
        bp_ref, bc_ref, o_ref = refs[n], refs[n + 1], refs[n + 2]
        bks = [bp_ref[...], bc_ref[...]]
        row = lax.broadcasted_iota(jnp.int32, (N_BUCKETS, QB), 0)
        lane = lax.broadcasted_iota(jnp.int32, (N_BUCKETS, QB), 1)
        out = jnp.zeros((N_BUCKETS, QB), F32)
        for h in range(8):
            tot = [sum(refs[l][h, b] for l in range(n)) for b in range(2)]
            for b in range(N_BUCKETS):
                val = jnp.sum(jnp.where(bks[0] == b, tot[0], 0.0)) + jnp.sum(jnp.where(bks[1] == b, tot[1], 0.0))
                out = jnp.where((row == b) & (lane == h), val, out)
        o_ref[...] = out

    return pl.pallas_call(body, name=name, out_shape=S((N_BUCKETS, QB), F32), compiler_params=_params())(*dscs, bprev, bcur)


def _adamw(w, g, m, v):
    m = ADAM_B1 * m + (1.0 - ADAM_B1) * g
    v = ADAM_B2 * v + (1.0 - ADAM_B2) * (g * g)
    m_hat = m / (1.0 - ADAM_B1 ** ADAM_STEP)
    v_hat = v / (1.0 - ADAM_B2 ** ADAM_STEP)
    delta = -ADAM_LR * (m_hat / (jnp.sqrt(v_hat) + ADAM_EPS) + ADAM_WD * w)
    return delta, m, v


def adamw_scattered(w, m, v, owns, others, name, riders=()):
    L, R, C = w.shape
    tr = _rows_tile(R, 176)

    def body(w_ref, m_ref, v_ref, *rest):
        own_refs, other_refs = rest[:L], rest[L:2 * L]
        g_ref, d_ref, mo_ref, vo_ref = rest[2 * L:]
        layer = pl.program_id(0)

        def grad(k):
            o = other_refs[k]
            return own_refs[k][...] + o[0].astype(F32) + o[1].astype(F32) + o[2].astype(F32)

        g = grad(0)
        for k in range(1, L):
            g = jnp.where(layer == k, grad(k), g)
        d, mn, vn = _adamw(w_ref[...], g, m_ref[...], v_ref[...])
        g_ref[...] = g
        d_ref[...] = d
        mo_ref[...] = mn
        vo_ref[...] = vn

    tile = pl.BlockSpec((None, tr, C), lambda l, i: (l, i, 0))
    return _call(
        body, name=name, grid=(L, R // tr), out_shape=(S((L, R, C), F32),) * 4,
        in_specs=[tile] * 3 + [pl.BlockSpec((tr, C), lambda l, i: (i, 0))] * L + [pl.BlockSpec((3, tr, C), lambda l, i: (0, i, 0))] * L,
        out_specs=(tile,) * 4, sem=("parallel", "parallel"), args=(w, m, v, *owns, *others), riders=riders)


def adamw_small(w, gs, m, v, name):
    R, C = w.shape

    def body(w_ref, g_ref, m_ref, v_ref, go_ref, d_ref, mo_ref, vo_ref):
        g = g_ref[0]
        for k in range(1, N_DEV):
            g = g + g_ref[k]
        d, mn, vn = _adamw(w_ref[...], g, m_ref[...], v_ref[...])
        go_ref[...] = g
        d_ref[...] = d
        mo_ref[...] = mn
        vo_ref[...] = vn

    return pl.pallas_call(body, name=name, out_shape=(S((R, C), F32),) * 4, compiler_params=_params())(w, gs, m, v)


def kernel(x, norm_ffn1, w_ffn1_gu, w_ffn1_down, norm_mix, w_in, sinks, norm_out_sb, norm_out_swa, w_out, norm_ffn2, w_ffn2_gu, w_ffn2_down, rel_bias, norm_final, loss_target, m_norm_ffn1, m_w_ffn1_gu, m_w_ffn1_down, m_norm_mix, m_w_in, m_sinks, m_norm_out_sb, m_norm_out_swa, m_w_out, m_norm_ffn2, m_w_ffn2_gu, m_w_ffn2_down, m_rel_bias, m_norm_final, v_norm_ffn1, v_w_ffn1_gu, v_w_ffn1_down, v_norm_mix, v_w_in, v_sinks, v_norm_out_sb, v_norm_out_swa, v_w_out, v_norm_ffn2, v_w_ffn2_gu, v_w_ffn2_down, v_rel_bias, v_norm_final):
    L = norm_ffn1.shape[0]
    T, D = x.shape[1], x.shape[2]
    F = w_ffn1_down.shape[1] * N_DEV
    h = x.reshape(T, D)
    target = loss_target.reshape(T, D)
    after, upto, before = _tri_consts()
    bprev, bcur = _t5_buckets()

    local = {}
    for l in range(L):
        local[f"gu1_{l}"] = w_ffn1_gu[l].T.astype(BF16)
        local[f"d1_{l}"] = w_ffn1_down[l].astype(BF16)
        local[f"in_{l}"] = w_in[l].T.astype(BF16)
        local[f"out_{l}"] = w_out[l].astype(BF16)
        local[f"gu2_{l}"] = w_ffn2_gu[l].T.astype(BF16)
        local[f"d2_{l}"] = w_ffn2_down[l].astype(BF16)
    full, partial = {}, {}
    grads, chip_sum, recv_b = {}, {}, {}

    def run(fn, *args, ag=(), rs1=(), rs2=()):
        halves = lambda names: [n if isinstance(n, tuple) else (n, None) for n in names]
        ag, rs2 = [(n, k) for n, k in halves(ag) if n in local], halves(rs2)
        rows = lambda k, total: None if k is None else (k * (total // 2), total // 2)

        def second(n, k):
            sb = chip_sum[n][1]
            return scatter_second(sb, rows(k, sb.shape[1]), recv_b.get(n))

        riders = ([gather(local[n], rows(k, local[n].shape[0]), partial.get(n)) for n, k in ag]
                  + [scatter_first(grads[n][1]) for n in rs1] + [second(n, k) for n, k in rs2])
        if not riders:
            return fn(*args)
        outs, per = fn(*args, riders=riders)
        per = [p[0] for p in per]
        for n, k in ag:
            buf = per.pop(0)
            if k == 0:
                partial[n] = buf
            else:
                full[n] = buf.reshape(N_DEV * buf.shape[1], D)
        for n in rs1:
            chip_sum[n] = scatter_add(grads[n][0], per.pop(0), f"rs_add_{n}")
        for n, _ in rs2:
            recv_b[n] = per.pop(0)
        return outs

    gu = lambda n: full[n].reshape(2, F, D)
    slots = lambda pair: tuple(t.reshape(N_DEV, -1, D) for t in pair)
    vec = lambda a: a.reshape(1, -1)

    PW = max(D, SB_W + SWA_W)
    n_rows = 4 * L + 2
    n_rows += (-n_rows) % 8

    def pack(ffn1, mix, ffn2, final, osb, osw, snk, rel, extra):
        pieces = []

        def row(*parts):
            flat = [a.reshape(-1) for a in parts]
            pieces.extend(flat)
            used = sum(a.size for a in flat)
            if used < PW:
                pieces.append(jnp.zeros((PW - used,), F32))

        for group in (ffn1, mix, ffn2):
            for l in range(L):
                row(group[l])
        row(final)
        for l in range(L):
            row(osb[l], osw[l])
        row(*[snk[l].reshape(-1)[:8] for l in range(L)], rel, extra)
        pieces.append(jnp.zeros(((n_rows - 4 * L - 2) * PW,), F32))
        return jnp.concatenate(pieces).reshape(n_rows, PW)

    def unpack(arr):
        ffn1, mix, ffn2 = arr[0:L, :D], arr[L:2 * L, :D], arr[2 * L:3 * L, :D]
        final = arr[3 * L, :D]
        ob = arr[3 * L + 1:4 * L + 1]
        tail = arr[4 * L + 1]
        return (ffn1, mix, tail[:8 * L].reshape(L, 8), ob[:, :SB_W], ob[:, SB_W:SB_W + SWA_W], ffn2,
                tail[8 * L:8 * L + N_BUCKETS * 8].reshape(N_BUCKETS, 8), final)

    zero = jnp.zeros((1,), F32)
    w_small = pack(norm_ffn1, norm_mix, norm_ffn2, norm_final, norm_out_sb, norm_out_swa, sinks, rel_bias, zero)
    norm_ffn1, norm_mix, sinks, norm_out_sb, norm_out_swa, norm_ffn2, _, norm_final = unpack(w_small)

    saved = []
    n_next = run(rms_cast, h, vec(norm_ffn1[0]), "rms_first", ag=("gu1_0",))
    for l in range(L):
        nx = l + 1
        s = {"h0": h, "n1": n_next}
        s["gate1"], s["up1"], s["a1"] = run(ffn_up_fwd, s["n1"], gu(f"gu1_{l}"), f"ffn1_up{l}",
                                            ag=(f"d1_{l}", ("in_0", 0) if l == 0 else (f"in_{l}", 1)))
        h = run(ffn_down_fwd, s["a1"], full[f"d1_{l}"], h, None, f"ffn1_down{l}", ag=(("in_0", 1),) if l == 0 else ())
        s["h1"] = h
        s["n2"], s["p"] = mix_in_fwd(h, vec(norm_mix[l]), full[f"in_{l}"], f"mix_in{l}")
        s["o_sb"], s["tot"] = run(sb_attn_fwd, s["p"], after, f"sb_fwd{l}", ag=(f"out_{l}", f"gu2_{l}", f"d2_{l}"))
        s["o_sw"], s["lse"] = run(swa_fwd, s["p"], vec(sinks[l]), rel_bias, bprev, bcur, f"swa_fwd{l}", ag=((f"gu1_{nx}", 0),))
        s["mixed"], h, s["n3"] = run(mix_out_fwd, s["o_sb"], s["o_sw"], vec(norm_out_sb[l]), vec(norm_out_swa[l]),
                                     full[f"out_{l}"], h, vec(norm_ffn2[l]), f"mix_out{l}")
        s["h2"] = h
        s["gate2"], s["up2"], s["a2"] = run(ffn_up_fwd, s["n3"], gu(f"gu2_{l}"), f"ffn2_up{l}",
                                            ag=((f"gu1_{nx}", 1), (f"in_{nx}", 0)))
        if nx < L:
            h, n_next = run(ffn_down_fwd, s["a2"], full[f"d2_{l}"], h, vec(norm_ffn1[nx]), f"ffn2_down{l}")
        else:
            h = run(ffn_down_fwd, s["a2"], full[f"d2_{l}"], h, None, f"ffn2_down{l}")
        saved.append(s)

    loss_part, dh, dhb, dg_final = loss_head(h, vec(norm_final), target, "loss_head")

    small = {k: [None] * L for k in ("ffn1", "mix", "sinks", "osb", "osw", "ffn2", "dsc")}
    for l in reversed(range(L)):
        s = saved[l]

        def ffn_bwd(dh, dhb, tag, gate, up, a, n, h_in, g, r_down, r_dwgu, r_dwd, r_up):
            gu_n, d_n = f"gu{tag}_{l}", f"d{tag}_{l}"
            dgu = run(ffn_down_bwd, dhb, full[d_n], gate, up, f"ffn{tag}_down_bwd{l}", **r_down)
            grads[gu_n] = slots(run(tn_matmul, dgu, n, 1.0, f"ffn{tag}_dwgu{l}", **r_dwgu))
            grads[d_n] = slots(run(tn_matmul, a[None], dhb, 0.5, f"ffn{tag}_dwd{l}", **r_dwd))
            return run(nn_rms_bwd, dgu, gu(gu_n), h_in, g, dh, f"ffn{tag}_up_bwd{l}", **r_up)

        later = l + 1 < L
        dh, dhb, small["ffn2"][l] = ffn_bwd(dh, dhb, 2, s["gate2"], s["up2"], s["a2"], s["n3"], s["h2"], vec(norm_ffn2[l]),
                                            {}, dict(rs2=(f"d1_{l + 1}",) if later else ()), {},
                                            dict(rs1=(f"gu2_{l}", f"d2_{l}"), rs2=((f"gu1_{l + 1}", 1),) if later else ()))
        do_sb, do_sw, small["osb"][l], small["osw"][l] = mix_out_bwd(
            dhb, full[f"out_{l}"], s["o_sb"], s["o_sw"], vec(norm_out_sb[l]), vec(norm_out_swa[l]), f"mix_out_bwd{l}")
        grads[f"out_{l}"] = slots(tn_matmul(s["mixed"][None], dhb, 1.0, f"dwout{l}"))
        dq_sb, dk_sb, dv_sb = run(sb_attn_bwd, s["p"], do_sb, s["tot"], upto, before, f"sb_bwd{l}",
                                  rs2=(f"gu2_{l}", f"d2_{l}"), rs1=(f"out_{l}",))
        dq_sw, dk_sw, dv_sw, small["sinks"][l], small["dsc"][l] = swa_bwd(
            s["p"], do_sw, s["lse"], vec(sinks[l]), rel_bias, bprev, bcur, f"swa_bwd{l}")
        dp = jnp.concatenate([dq_sb, dk_sb, dv_sb, dq_sw, dk_sw, dv_sw], axis=1)
        dh, dhb, small["mix"][l] = nn_rms_bwd(dp[None], full[f"in_{l}"][None], s["h1"], vec(norm_mix[l]), dh, f"mix_in_bwd{l}")
        grads[f"in_{l}"] = slots(tn_matmul(dp[None], s["n2"], 1.0, f"dwin{l}"))
        dh, dhb, small["ffn1"][l] = ffn_bwd(dh, dhb, 1, s["gate1"], s["up1"], s["a1"], s["n1"], s["h0"], vec(norm_ffn1[l]),
                                            dict(rs1=(f"in_{l}",), rs2=(f"out_{l}",)), dict(rs2=(f"in_{l}",)),
                                            dict(rs1=(f"gu1_{l}",)),
                                            dict(rs1=(f"d1_{l}",), rs2=((f"gu1_{l}", 0),) if l else (f"gu1_{l}",)))

    grad_x = dh.reshape(x.shape)

    upd = {}
    for nm, w, m, v, transposed, last in (
            ("gu2", w_ffn2_gu, m_w_ffn2_gu, v_w_ffn2_gu, True, ("d1_0",)), ("d2", w_ffn2_down, m_w_ffn2_down, v_w_ffn2_down, False, ()),
            ("in", w_in, m_w_in, v_w_in, True, ()), ("out", w_out, m_w_out, v_w_out, False, ()),
            ("gu1", w_ffn1_gu, m_w_ffn1_gu, v_w_ffn1_gu, True, ()), ("d1", w_ffn1_down, m_w_ffn1_down, v_w_ffn1_down, False, ())):
        turn = (lambda a: jnp.swapaxes(a, 1, 2)) if transposed else (lambda a: a)
        names = [f"{nm}_{l}" for l in range(L)]
        res = run(adamw_scattered, turn(w), turn(m), turn(v), [chip_sum[n][0] for n in names], [recv_b[n] for n in names],
                  f"adamw_{nm}", rs2=last)
        upd[nm] = tuple(turn(r) for r in res)

    d_rel = rel_bias_grad(small["dsc"], bprev, bcur, "rel_bias_grad")[:, :8]
    g_small = pack(small["ffn1"], small["mix"], small["ffn2"], dg_final, small["osb"], small["osw"], small["sinks"], d_rel,
                   loss_part[0, :1])
    m_small = pack(m_norm_ffn1, m_norm_mix, m_norm_ffn2, m_norm_final, m_norm_out_sb, m_norm_out_swa, m_sinks, m_rel_bias, zero)
    v_small = pack(v_norm_ffn1, v_norm_mix, v_norm_ffn2, v_norm_final, v_norm_out_sb, v_norm_out_swa, v_sinks, v_rel_bias, zero)
    gs_small = all_gather_rows(g_small, "ag_small")
    summed = adamw_small(w_small, gs_small, m_small, v_small, "adamw_small")
    small_out = [unpack(a) for a in summed]
    loss = summed[0][4 * L + 1, 8 * L + N_BUCKETS * 8]

    def group(k):
        sm = small_out[k]
        return (sm[0], upd["gu1"][k], upd["d1"][k], sm[1], upd["in"][k], sm[2], sm[3], sm[4], upd["out"][k], sm[5],
                upd["gu2"][k], upd["d2"][k], sm[6], sm[7])

    return (loss, grad_x, *group(0), *group(1), *group(2), *group(3))
```

```python
import math

import jax
import jax.numpy as jnp
from jax import lax
from jax.experimental import pallas as pl
from jax.experimental.pallas import tpu as pltpu

F32 = jnp.float32
BF16 = jnp.bfloat16
S = jax.ShapeDtypeStruct

N_DEV = 8
HEAD_DIM = 64
SB_HEADS = 8
PAIR = 2 * HEAD_DIM
SB_W = 512
SWA_W = 512
KV_W = 128
IN_W = 3 * SB_W + SWA_W + 2 * KV_W
QB = 128
N_BUCKETS = 32
MAX_DISTANCE = 128
EPS = 1e-6
NEG_INF = -1e30
SCALE = HEAD_DIM ** -0.5

ADAM_LR = 0.001
ADAM_B1 = 0.9
ADAM_B2 = 0.999
ADAM_EPS = 1e-08
ADAM_WD = 0.01
ADAM_STEP = 10

VMEM_LIMIT = 56 * 1024 * 1024
MESH = pl.DeviceIdType.MESH


def _params(sem=None, vmem=VMEM_LIMIT):
    return pltpu.CompilerParams(dimension_semantics=sem, vmem_limit_bytes=vmem)


def _nn(a, b):
    return jnp.dot(a, b, preferred_element_type=F32)


def _nt(a, b):
    return lax.dot_general(a, b, (((1,), (1,)), ((), ())), preferred_element_type=F32)


def _tn(a, b):
    return lax.dot_general(a, b, (((0,), (0,)), ((), ())), preferred_element_type=F32)


def _tri(xs, m):
    return [_nn(x.astype(BF16), m) for x in xs]


def _rms(x, g):
    r = lax.rsqrt(jnp.mean(x * x, axis=-1, keepdims=True) + EPS)
    return x * r * g


def _rms_bwd(dy, x, g):
    r = lax.rsqrt(jnp.mean(x * x, axis=-1, keepdims=True) + EPS)
    xhat = x * r
    u = dy * g
    dx = r * (u - xhat * jnp.mean(u * xhat, axis=-1, keepdims=True))
    return dx, jnp.sum(dy * xhat, axis=0, keepdims=True)


def _softplus_logsig(z):
    sp = jnp.maximum(z, 0.0) + jnp.log(1.0 + jnp.exp(-jnp.abs(z)))
    return sp, z - sp


def _tile(n, want):
    t = min(n, want)
    while n % t:
        t //= 2
    return t


def _place():
    x, y, c = lax.axis_index("x"), lax.axis_index("y"), lax.axis_index("c")
    chips = [(1 - x, y), (x, 1 - y), (1 - x, 1 - y)]
    return x, y, c, chips


def all_gather_rows(v, name):
    R, C = v.shape

    def body(v_ref, out_ref, send_sems, recv_sems, local_sem):
        x, y, c, chips = _place()
        me, sibling = (x, y, c), (x, y, 1 - c)

        def slot(px, py, pc):
            return out_ref.at[4 * px + 2 * py + pc]

        def copy(k, block, to, src=None):
            return pltpu.make_async_remote_copy(
                src_ref=slot(*block) if src is None else src, dst_ref=slot(*block),
                send_sem=send_sems.at[k], recv_sem=recv_sems.at[k], device_id=to, device_id_type=MESH)

        mine = pltpu.make_async_copy(v_ref, slot(*me), local_sem)
        mine.start()
        first = [copy(0, me, sibling, src=v_ref)]
        first += [copy(1 + j, me, (*chip, c), src=v_ref) for j, chip in enumerate(chips)]
        for cp in first:
            cp.start()
        passed = [copy(4 + j, (*chip, c), sibling) for j, chip in enumerate(chips)]
        for j, chip in enumerate(chips):
            copy(1 + j, (*chip, c), me).wait_recv()
            passed[j].start()
        copy(0, sibling, me).wait_recv()
        for j, chip in enumerate(chips):
            copy(4 + j, (*chip, 1 - c), me).wait_recv()
        for cp in first + passed:
            cp.wait_send()
        mine.wait()

    return pl.pallas_call(
        body, name=name, out_shape=S((N_DEV, R, C), v.dtype),
        in_specs=[pl.BlockSpec(memory_space=pl.ANY)], out_specs=pl.BlockSpec(memory_space=pl.ANY),
        scratch_shapes=[pltpu.SemaphoreType.DMA((7,)), pltpu.SemaphoreType.DMA((7,)), pltpu.SemaphoreType.DMA],
    )(v)


class _Exchange:
    def __init__(self, ins, outs, sizes, n_local, plan, aliases=None):
        self.ins, self.outs, self.plan, self.aliases = list(ins), list(outs), plan, aliases or {}
        self.sizes, self.n_local = list(sizes), n_local

    def scratch(self):
        n = sum(self.sizes)
        return [pltpu.SemaphoreType.DMA((n,)), pltpu.SemaphoreType.DMA((n,)), pltpu.SemaphoreType.DMA((max(self.n_local, 1),))]

    def _copies(self, in_refs, out_refs, sems):
        send_sems, recv_sems, local_sems = sems
        phases, local = self.plan(in_refs, out_refs)
        out, k = [], 0
        for phase in phases:
            out.append([pltpu.make_async_remote_copy(src_ref=s, dst_ref=d, send_sem=send_sems.at[k + n], recv_sem=recv_sems.at[k + n],
                                                     device_id=dev, device_id_type=MESH) for n, (s, d, dev) in enumerate(phase)])
            k += len(phase)
        return out, [pltpu.make_async_copy(s, d, local_sems.at[n]) for n, (s, d) in enumerate(local)]

    def start(self, in_refs, out_refs, sems):
        phases, loc = self._copies(in_refs, out_refs, sems)
        for cp in phases[0] + loc:
            cp.start()

    def advance(self, hook, in_refs, out_refs, sems):
        p = hook - (3 - len(self.sizes))
        if p >= 1:
            phases, _ = self._copies(in_refs, out_refs, sems)
            for cp in phases[p - 1]:
                cp.wait_recv()
            for cp in phases[p]:
                cp.start()

    def finish(self, in_refs, out_refs, sems):
        phases, loc = self._copies(in_refs, out_refs, sems)
        for cp in phases[-1]:
            cp.wait_recv()
        for phase in phases:
            for cp in phase:
                cp.wait_send()
        for cp in loc:
            cp.wait()


def gather(v, rows=None, into=None):
    R, C = v.shape
    r0, nr = rows or (0, R)
    na = min(nr, ((nr // 2 + 15) // 16) * 16)

    def plan(ins, outs):
        x, y, c, _ = _place()
        xn, yn, dg, sibling = (1 - x, y), (x, 1 - y), (1 - x, 1 - y), (x, y, 1 - c)
        slot = lambda chip, start=r0, count=nr: outs[0].at[4 * chip[0] + 2 * chip[1] + c, pl.ds(start, count), :]
        src, mine = ins[0].at[pl.ds(r0, nr), :], slot((x, y))
        same = lambda ref, to: (ref, ref, to)
        first = [(src, mine, sibling), (src, mine, (*xn, c)), (src, mine, (*yn, c))]
        relay = [same(slot(xn, r0, na), (*yn, c)), same(slot(yn, r0 + na, nr - na), (*xn, c))]
        onward = [same(slot(xn), sibling), same(slot(yn), sibling), same(slot(dg), sibling)]
        return [first, relay, onward], [(src, mine)]

    if into is None:
        return _Exchange([v], [S((N_DEV, R, C), v.dtype)], (3, 2, 3), 1, plan)
    return _Exchange([v, into], [S((N_DEV, R, C), v.dtype)], (3, 2, 3), 1, plan, aliases={1: 0})


def scatter_first(gb):
    _, R, C = gb.shape

    def plan(ins, outs):
        x, y, c, chips = _place()
        owners = [(x, y)] + chips
        return [[(ins[0].at[4 * px + 2 * py + (1 - c)], outs[0].at[j], (x, y, 1 - c)) for j, (px, py) in enumerate(owners)]], []

    return _Exchange([gb], [S((4, R, C), BF16)], (4,), 0, plan)


def scatter_second(sb, rows=None, into=None):
    r0, nr = rows or (0, sb.shape[1])

    def plan(ins, outs):
        x, y, c, chips = _place()
        part = lambda ref, j: ref.at[j, pl.ds(r0, nr), :]
        return [[(part(ins[0], j), part(outs[0], j), (*chips[j], c)) for j in range(3)]], []

    if into is None:
        return _Exchange([sb], [S(sb.shape, BF16)], (3,), 0, plan)
    return _Exchange([sb, into], [S(sb.shape, BF16)], (3,), 0, plan, aliases={1: 0})


def _call(body, *, name, grid, in_specs, out_specs, out_shape, args, scratch=(), sem=None, riders=(), marks=None):
    single = not isinstance(out_shape, (tuple, list))
    out_shape = (out_shape,) if single else tuple(out_shape)
    out_specs = (out_specs,) if single else tuple(out_specs)
    n_in, n_out, n_sc = len(in_specs), len(out_shape), len(scratch)
    if not riders:
        res = pl.pallas_call(body, name=name, grid=grid, in_specs=list(in_specs), out_specs=out_specs, out_shape=out_shape,
                             scratch_shapes=list(scratch), compiler_params=_params(sem))(*args)
        return res[0] if single else res
    r_ins = [a for r in riders for a in r.ins]
    r_outs = [o for r in riders for o in r.outs]
    r_scr = [s for r in riders for s in r.scratch()]
    aliases, i0, o0 = {}, n_in, n_out
    for r in riders:
        for a, b in r.aliases.items():
            aliases[i0 + a] = o0 + b
        i0, o0 = i0 + len(r.ins), o0 + len(r.outs)
    steps = math.prod(grid)

    def full(*refs):
        ins, rin = refs[:n_in], refs[n_in:n_in + len(r_ins)]
        pos = n_in + len(r_ins)
        outs, rout = refs[pos:pos + n_out], refs[pos + n_out:pos + n_out + len(r_outs)]
        pos += n_out + len(r_outs)
        sc, rsc = refs[pos:pos + n_sc], refs[pos + n_sc:]
        step = 0
        for d, n in enumerate(grid):
            step = step * n + pl.program_id(d)

        def each(method, *lead):
            i, o = 0, 0
            for k, r in enumerate(riders):
                getattr(r, method)(*lead, rin[i:i + len(r.ins)], rout[o:o + len(r.outs)], rsc[3 * k:3 * k + 3])
                i, o = i + len(r.ins), o + len(r.outs)

        @pl.when(step == 0)
        def _():
            each("start")
        body(*ins, *outs, *sc)

        late = max(steps - 1 - max(steps // 8, 1), 0)
        first, second = marks or (min((3 * steps) // 5, late), late)

        @pl.when(step == first)
        def _():
            each("advance", 1)

        @pl.when(step == second)
        def _():
            each("advance", 2)

        @pl.when(step == steps - 1)
        def _():
            each("finish")

    anywhere = pl.BlockSpec(memory_space=pl.ANY)
    res = pl.pallas_call(
        full, name=name, grid=grid, in_specs=list(in_specs) + [anywhere] * len(r_ins),
        out_specs=out_specs + (anywhere,) * len(r_outs), out_shape=out_shape + tuple(r_outs),
        scratch_shapes=list(scratch) + r_scr, input_output_aliases=aliases,
        compiler_params=_params(("arbitrary",) * len(grid)))(*args, *r_ins)
    host, rest, per = res[:n_out], list(res[n_out:]), []
    for r in riders:
        per.append(rest[:len(r.outs)])
        rest = rest[len(r.outs):]
    return (host[0] if single else tuple(host)), per


def _rows_tile(n, cap):
    return max(t for t in range(16, min(n, cap) + 1, 16) if n % t == 0)


def scatter_add(g, ra, name):
    _, R, C = g.shape
    tr = _rows_tile(R, 176)
    x, y, c, chips = _place()
    slots = jnp.stack([4 * px + 2 * py + c for px, py in [(x, y)] + chips]).astype(jnp.int32)

    def body(s_ref, g0, g1, g2, g3, ra_ref, own_ref, sb_ref):
        own_ref[...] = g0[...] + ra_ref[0].astype(F32)
        for j, gj in enumerate((g1, g2, g3)):
            sb_ref[j] = (gj[...] + ra_ref[j + 1].astype(F32)).astype(BF16)

    spec = pltpu.PrefetchScalarGridSpec(
        num_scalar_prefetch=1, grid=(R // tr,),
        in_specs=[pl.BlockSpec((None, tr, C), lambda i, s, j=j: (s[j], i, 0)) for j in range(4)]
        + [pl.BlockSpec((4, tr, C), lambda i, s: (0, i, 0))],
        out_specs=(pl.BlockSpec((tr, C), lambda i, s: (i, 0)), pl.BlockSpec((3, tr, C), lambda i, s: (0, i, 0))))
    return pl.pallas_call(body, name=name, grid_spec=spec, out_shape=(S((R, C), F32), S((3, R, C), BF16)),
                          compiler_params=_params(("parallel",)))(slots, g, g, g, g, ra)


def rms_cast(h, g, name, riders=()):
    T, D = h.shape
    tm = _tile(T, 512)

    def body(h_ref, g_ref, n_ref):
        n_ref[...] = _rms(h_ref[...], g_ref[...]).astype(BF16)

    row = pl.BlockSpec((tm, D), lambda i: (i, 0))
    return _call(body, name=name, grid=(T // tm,), out_shape=S((T, D), BF16), in_specs=[row, pl.BlockSpec((1, D), lambda i: (0, 0))],
                 out_specs=row, sem=("parallel",), args=(h, g), riders=riders)


def ffn_up_fwd(n, wgu, name, riders=()):
    T, D = n.shape
    F = wgu.shape[1]
    tr, tn = _tile(T, 512), _tile(F, 256)

    def body(n_ref, wg_ref, wu_ref, dgate_ref, dup_ref, a_ref):
        wg, wu = wg_ref[...], wu_ref[...]
        for r in range(T // tr):
            rows = slice(r * tr, (r + 1) * tr)
            x = n_ref[rows, :]
            gate = _nt(x, wg)
            up = _nt(x, wu)
            s = jax.nn.sigmoid(gate)
            silu = gate * s
            dgate_ref[rows, :] = (up * (s * (1.0 + gate * (1.0 - s)))).astype(BF16)
            dup_ref[rows, :] = silu.astype(BF16)
            a_ref[rows, :] = (silu * up).astype(BF16)

    tile = pl.BlockSpec((T, tn), lambda j: (0, j))
    return _call(
        body, name=name, grid=(F // tn,), out_shape=(S((T, F), BF16),) * 3,
        in_specs=[pl.BlockSpec((T, D), lambda j: (0, 0)),
                  pl.BlockSpec((None, tn, D), lambda j: (0, j, 0)), pl.BlockSpec((None, tn, D), lambda j: (1, j, 0))],
        out_specs=(tile, tile, tile), sem=("parallel",), args=(n, wgu, wgu), riders=riders)


def ffn_down_fwd(a, wd, h, g_next, name, riders=()):
    T, F = a.shape
    D = wd.shape[1]
    tm = _tile(T, 256)

    def body(a_ref, w_ref, h_ref, *rest):
        out = h_ref[...] + 0.5 * _nn(a_ref[...], w_ref[...])
        if g_next is None:
            rest[0][...] = out
        else:
            g_ref, o_ref, n_ref = rest
            o_ref[...] = out
            n_ref[...] = _rms(out, g_ref[...]).astype(BF16)

    row = pl.BlockSpec((tm, D), lambda i: (i, 0))
    more = g_next is not None
    return _call(
        body, name=name, grid=(T // tm,), out_shape=(S((T, D), F32), S((T, D), BF16)) if more else S((T, D), F32),
        in_specs=[pl.BlockSpec((tm, F), lambda i: (i, 0)), pl.BlockSpec((F, D), lambda i: (0, 0)), row]
        + ([pl.BlockSpec((1, D), lambda i: (0, 0))] if more else []),
        out_specs=(row, row) if more else row,
        sem=("parallel",), args=(a, wd, h) + ((g_next,) if more else ()), riders=riders)


def mix_in_fwd(h, g, win, name):
    T, D = h.shape
    N = win.shape[0]
    tm = _tile(T, 256)

    def body(h_ref, g_ref, w_ref, n_ref, p_ref):
        n = _rms(h_ref[...], g_ref[...]).astype(BF16)
        n_ref[...] = n
        p_ref[...] = _nt(n, w_ref[...]).astype(BF16)

    return pl.pallas_call(
        body, name=name, grid=(T // tm,), out_shape=(S((T, D), BF16), S((T, N), BF16)),
        in_specs=[pl.BlockSpec((tm, D), lambda i: (i, 0)), pl.BlockSpec((1, D), lambda i: (0, 0)),
                  pl.BlockSpec((N, D), lambda i: (0, 0))],
        out_specs=(pl.BlockSpec((tm, D), lambda i: (i, 0)), pl.BlockSpec((tm, N), lambda i: (i, 0))),
        compiler_params=_params(("parallel",)),
    )(h, g, win)


def _tri_consts():
    r = lax.broadcasted_iota(jnp.int32, (QB, QB), 0)
    c = lax.broadcasted_iota(jnp.int32, (QB, QB), 1)
    ones = jnp.ones((QB, QB), BF16)
    with_sums = lambda tri: jnp.concatenate([tri.astype(BF16), ones], axis=1)
    return with_sums(r > c), with_sums(r <= c), with_sums(r < c)


def _half_masks():
    lane = lax.broadcasted_iota(jnp.int32, (QB, PAIR), 1)
    row = lax.broadcasted_iota(jnp.int32, (QB, PAIR), 0)
    return lane < HEAD_DIM, lane, row


def sb_attn_fwd(p, after, name, riders=()):
    T = p.shape[0]
    nq = T // QB

    def body(q_ref, k_ref, v_ref, m_ref, o_ref, tot_ref, q_sc, acc_ref, z_sc):
        i = pl.program_id(0)
        lo, lane, row = _half_masks()
        causal = lane < row
        heads, pairs = range(SB_HEADS), range(SB_HEADS // 2)
        for hp in pairs:
            q_sc[hp] = (q_ref[:, hp * PAIR:(hp + 1) * PAIR].astype(F32) * SCALE).astype(BF16)
        m2 = m_ref[...]

        def by_head(ref, j, hp):
            t = ref[pl.ds(pl.multiple_of(j * QB, QB), QB), hp * PAIR:(hp + 1) * PAIR]
            return jnp.concatenate([jnp.where(lo, t, 0), jnp.where(lo, 0, t)], axis=0)

        def scores(j):
            return [_nt(q_sc[hp], by_head(k_ref, j, hp)) for hp in pairs]

        def block(j, diag):
            z2 = [z_sc[hp] for hp in pairs]
            ahead = scores(jnp.maximum(j - 1, 0))
            for hp in pairs:
                z_sc[hp] = ahead[hp]
            vs = [by_head(v_ref, j, hp) for hp in pairs]
            spls = [_softplus_logsig(z2[h // 2][:, (h % 2) * QB:(h % 2 + 1) * QB]) for h in heads]
            sp = [jnp.where(causal, spls[h][0], 0.0) if diag else spls[h][0] for h in heads]
            rr = _tri(sp, m2)
            if diag:
                w = [jnp.where(causal, jnp.exp(spls[h][1] - rr[h][:, :QB]), 0.0).astype(BF16) for h in heads]
            else:
                c = [tot_ref[:, h * QB:(h + 1) * QB] for h in heads]
                w = [jnp.exp(spls[h][1] - (c[h] + rr[h][:, :QB])).astype(BF16) for h in heads]
            pv = [_nn(jnp.concatenate([w[2 * hp], w[2 * hp + 1]], axis=1), vs[hp]) for hp in pairs]
            for hp in pairs:
                acc_ref[hp] = pv[hp] if diag else acc_ref[hp] + pv[hp]
            for h in heads:
                tot_ref[:, h * QB:(h + 1) * QB] = rr[h][:, QB:] if diag else c[h] + rr[h][:, QB:]

        first = scores(i)
        for hp in pairs:
            z_sc[hp] = first[hp]
        block(i, True)

        def step(t, carry):
            block(i - 1 - t, False)
            return carry
        lax.fori_loop(0, i, step, 0)
        for hp in pairs:
            o_ref[:, hp * PAIR:(hp + 1) * PAIR] = acc_ref[hp]

    npair = SB_HEADS // 2
    return _call(
        body, name=name, grid=(nq,), out_shape=(S((T, SB_W), F32), S((T, SB_HEADS * QB), F32)),
        in_specs=[pl.BlockSpec((QB, SB_W), lambda i: (i, 0)), pl.BlockSpec((T, SB_W), lambda i: (0, 1)),
                  pl.BlockSpec((T, SB_W), lambda i: (0, 2)), pl.BlockSpec((QB, 2 * QB), lambda i: (0, 0))],
        out_specs=(pl.BlockSpec((QB, SB_W), lambda i: (i, 0)), pl.BlockSpec((QB, SB_HEADS * QB), lambda i: (i, 0))),
        scratch=[pltpu.VMEM((npair, QB, PAIR), BF16), pltpu.VMEM((npair, QB, PAIR), F32), pltpu.VMEM((npair, QB, 2 * QB), F32)],
        sem=("arbitrary",), args=(p, p, p, after), riders=riders,
        marks=((11 * nq) // 16, (14 * nq) // 16))


def sb_attn_bwd(p, do, tot, upto, before, name, riders=()):
    T = p.shape[0]
    nq = T // QB

    def body(q_ref, k_ref, v_ref, do_ref, tot_ref, mp_ref, mg_ref, dq_ref, dk_ref, dv_ref,
             q_sc, d_sc, qd_sc, pg_sc, dq_acc, dk_acc, dv_acc, zd_sc):
        i = pl.program_id(0)
        lo, lane, row = _half_masks()
        causal = lane < row
        heads, pairs = range(SB_HEADS), range(SB_HEADS // 2)

        def by_head(t):
            return jnp.concatenate([jnp.where(lo, t, 0), jnp.where(lo, 0, t)], axis=0)

        for hp in pairs:
            q2 = (q_ref[:, hp * PAIR:(hp + 1) * PAIR].astype(F32) * SCALE).astype(BF16)
            d2 = do_ref[:, hp * PAIR:(hp + 1) * PAIR].astype(BF16)
            q_sc[hp] = q2
            d_sc[hp] = d2
            qd_sc[hp] = by_head(q2)
            qd_sc[SB_HEADS // 2 + hp] = by_head(d2)
        mp, mg = mp_ref[...], mg_ref[...]

        @pl.when(i == 0)
        def _():
            dk_acc[...] = jnp.zeros_like(dk_acc)
            dv_acc[...] = jnp.zeros_like(dv_acc)
        pg_sc[...] = jnp.zeros_like(pg_sc)
        dq_acc[...] = jnp.zeros_like(dq_acc)

        def rows(ref, j, hp):
            return ref[pl.ds(pl.multiple_of(j * QB, QB), QB), hp * PAIR:(hp + 1) * PAIR]

        def products(j):
            return ([_nt(q_sc[hp], by_head(rows(k_ref, j, hp))) for hp in pairs]
                    + [_nt(d_sc[hp], by_head(rows(v_ref, j, hp))) for hp in pairs])

        def block(j, diag):
            r0 = pl.multiple_of(j * QB, QB)
            half = lambda t, h: t[:, (h % 2) * QB:(h % 2 + 1) * QB]
            z = [half(zd_sc[h // 2], h) for h in heads]
            dw = [half(zd_sc[SB_HEADS // 2 + h // 2], h) for h in heads]
            if not diag:
                ahead = products(j + 1)
                for hp in range(SB_HEADS):
                    zd_sc[hp] = ahead[hp]
            ks = [by_head(rows(k_ref, j, hp)) for hp in pairs]
            spls = [_softplus_logsig(z[h]) for h in heads]
            sp = [jnp.where(causal, spls[h][0], 0.0) if diag else spls[h][0] for h in heads]
            rr = _tri(sp, mp)
            pc = [pg_sc[2 * h] for h in heads]
            w = [jnp.exp(spls[h][1] - (tot_ref[:, h * QB:(h + 1) * QB] - (pc[h] + rr[h][:, :QB]))) for h in heads]
            if diag:
                w = [jnp.where(causal, w[h], 0.0) for h in heads]
            gg = [dw[h] * w[h] for h in heads]
            rg = _tri(gg, mg)
            gc = [pg_sc[2 * h + 1] for h in heads]
            dz = [gg[h] - (gg[h] + gc[h] + rg[h][:, :QB]) * jnp.exp(spls[h][1]) for h in heads]
            if diag:
                dz = [jnp.where(causal, dz[h], 0.0) for h in heads]
            dzb = [dz[h].astype(BF16) for h in heads]
            wb = [w[h].astype(BF16) for h in heads]
            both = lambda t, hp, axis: jnp.concatenate([t[2 * hp], t[2 * hp + 1]], axis=axis)
            dq = [_nn(both(dzb, hp, 1), ks[hp]) for hp in pairs]
            dk = [_tn(both(dzb, hp, 0), qd_sc[hp]) for hp in pairs]
            dv = [_tn(both(wb, hp, 0), qd_sc[SB_HEADS // 2 + hp]) for hp in pairs]
            for h in heads:
                if not diag:
                    pg_sc[2 * h] = pc[h] + rr[h][:, QB:]
                    pg_sc[2 * h + 1] = gc[h] + rg[h][:, QB:]
            for hp in pairs:
                dq_acc[hp] += dq[hp]
                dk_acc[pl.ds(r0, QB), hp * PAIR:(hp + 1) * PAIR] += dk[hp]
                dv_acc[pl.ds(r0, QB), hp * PAIR:(hp + 1) * PAIR] += dv[hp]

        first = products(0)
        for hp in range(SB_HEADS):
            zd_sc[hp] = first[hp]

        def step(t, carry):
            block(t, False)
            return carry
        lax.fori_loop(0, i, step, 0)
        block(i, True)
        for hp in pairs:
            dq_ref[:, hp * PAIR:(hp + 1) * PAIR] = (dq_acc[hp] * SCALE).astype(BF16)

        @pl.when(i == nq - 1)
        def _():
            dk_ref[...] = dk_acc[...].astype(BF16)
            dv_ref[...] = dv_acc[...].astype(BF16)

    qtile = pl.BlockSpec((QB, SB_W), lambda i: (i, 0))
    whole = pl.BlockSpec((T, SB_W), lambda i: (0, 0))
    const = pl.BlockSpec((QB, 2 * QB), lambda i: (0, 0))
    return _call(
        body, name=name, grid=(nq,), out_shape=(S((T, SB_W), BF16),) * 3,
        in_specs=[qtile, pl.BlockSpec((T, SB_W), lambda i: (0, 1)), pl.BlockSpec((T, SB_W), lambda i: (0, 2)), qtile,
                  pl.BlockSpec((QB, SB_HEADS * QB), lambda i: (i, 0)), const, const],
        out_specs=(qtile, whole, whole),
        scratch=[pltpu.VMEM((SB_HEADS // 2, QB, PAIR), BF16), pltpu.VMEM((SB_HEADS // 2, QB, PAIR), BF16),
                 pltpu.VMEM((SB_HEADS, 2 * QB, PAIR), BF16),
                 pltpu.VMEM((2 * SB_HEADS, QB, QB), F32), pltpu.VMEM((SB_HEADS // 2, QB, PAIR), F32),
                 pltpu.VMEM((T, SB_W), F32), pltpu.VMEM((T, SB_W), F32), pltpu.VMEM((SB_HEADS, QB, 2 * QB), F32)],
        sem=("arbitrary",), args=(p, p, p, do, tot, upto, before), riders=riders)


def _t5_buckets():
    a = lax.broadcasted_iota(jnp.int32, (QB, QB), 0)
    c = lax.broadcasted_iota(jnp.int32, (QB, QB), 1)

    def bucket(dist):
        dist = jnp.maximum(dist, 0)
        max_exact = N_BUCKETS // 2
        d = jnp.maximum(dist, 1).astype(F32)
        large = max_exact + (jnp.log(d / max_exact) / math.log(MAX_DISTANCE / max_exact)
                             * (N_BUCKETS - max_exact)).astype(jnp.int32)
        large = jnp.minimum(large, N_BUCKETS - 1)
        return jnp.where(dist < max_exact, dist, large)

    return bucket(QB + a - c), bucket(a - c)


def _swa_common(i, kp_ref, kc_ref, vp_ref, vc_ref, bp_ref, bc_ref, rb_ref, bias_ref):
    lo, lane, row = _half_masks()

    @pl.when(i == 0)
    def _():
        for blk, b_ref in enumerate((bp_ref, bc_ref)):
            bk = b_ref[...]
            for h in range(8):
                acc = jnp.zeros((QB, QB), F32)
                for b in range(N_BUCKETS):
                    acc = jnp.where(bk == b, rb_ref[b, h], acc)
                bias_ref[h, blk] = acc

    band = [(lane > row) & (i > 0), lane <= row]

    def stacks(ref):
        t = ref[...].astype(F32)
        sw = pltpu.roll(t, HEAD_DIM, 1)
        return [jnp.concatenate([jnp.where(lo, t, 0.0), jnp.where(lo, 0.0, sw)], axis=0).astype(BF16),
                jnp.concatenate([jnp.where(lo, sw, 0.0), jnp.where(lo, 0.0, t)], axis=0).astype(BF16)]

    ks = [stacks(kp_ref), stacks(kc_ref)]
    vs = [stacks(vp_ref), stacks(vc_ref)]
    return lo, band, ks, vs


def _lane_half(t, h):
    return t[:, (h % 2) * QB:(h % 2 + 1) * QB]


def swa_fwd(p, sinks, rel_bias, bprev, bcur, name, riders=()):
    T = p.shape[0]
    nq = T // QB
    kcol, vcol = (3 * SB_W + SWA_W) // KV_W, (3 * SB_W + SWA_W) // KV_W + 1

    def body(q_ref, kp_ref, kc_ref, vp_ref, vc_ref, bp_ref, bc_ref, sink_ref, rb_ref, o_ref, lse_ref, bias_ref):
        i = pl.program_id(0)
        lo, band, ks, vs = _swa_common(i, kp_ref, kc_ref, vp_ref, vc_ref, bp_ref, bc_ref, rb_ref, bias_ref)
        heads, pairs, blocks = range(8), range(4), range(2)
        rowmax = lambda t: jnp.max(t, axis=1, keepdims=True)
        rowsum = lambda t: jnp.sum(t, axis=1, keepdims=True)
        q2 = [q_ref[:, g * PAIR:(g + 1) * PAIR] for g in pairs]
        s2 = [[_nt(q2[g], ks[b][g // 2]) for b in blocks] for g in pairs]
        sc = [[jnp.where(band[b], _lane_half(s2[h // 2][b], h) * SCALE + bias_ref[h, b], NEG_INF) for b in blocks] for h in heads]
        sink = [sink_ref[0, h] for h in heads]
        m = [jnp.maximum(jnp.maximum(rowmax(sc[h][0]), rowmax(sc[h][1])), sink[h]) for h in heads]
        e = [[jnp.exp(sc[h][b] - m[h]) for b in blocks] for h in heads]
        den = [rowsum(e[h][0]) + rowsum(e[h][1]) + jnp.exp(sink[h] - m[h]) for h in heads]
        pb = [[(e[h][b] / den[h]).astype(BF16) for b in blocks] for h in heads]
        for g in pairs:
            both = lambda b: jnp.concatenate([pb[2 * g][b], pb[2 * g + 1][b]], axis=1)
            o_ref[:, g * PAIR:(g + 1) * PAIR] = _nn(both(0), vs[0][g // 2]) + _nn(both(1), vs[1][g // 2])
        for h in heads:
            lse_ref[:, h * QB:(h + 1) * QB] = jnp.broadcast_to(m[h] + jnp.log(den[h]), (QB, QB))

    kv = lambda col, prev: pl.BlockSpec((QB, KV_W), (lambda i: (jnp.maximum(i - 1, 0), col)) if prev else (lambda i: (i, col)))
    full = pl.BlockSpec((QB, QB), lambda i: (0, 0))
    smem = pl.BlockSpec(memory_space=pltpu.SMEM)
    return _call(
        body, name=name, grid=(nq,), out_shape=(S((T, SWA_W), F32), S((T, 8 * QB), F32)),
        in_specs=[pl.BlockSpec((QB, SWA_W), lambda i: (i, 3)), kv(kcol, True), kv(kcol, False), kv(vcol, True), kv(vcol, False),
                  full, full, smem, smem],
        out_specs=(pl.BlockSpec((QB, SWA_W), lambda i: (i, 0)), pl.BlockSpec((QB, 8 * QB), lambda i: (i, 0))),
        scratch=[pltpu.VMEM((8, 2, QB, QB), F32)],
        sem=("arbitrary",), args=(p, p, p, p, p, bprev, bcur, sinks, rel_bias), riders=riders)


def swa_bwd(p, do, lse, sinks, rel_bias, bprev, bcur, name, riders=()):
    T = p.shape[0]
    nq = T // QB
    kcol, vcol = (3 * SB_W + SWA_W) // KV_W, (3 * SB_W + SWA_W) // KV_W + 1

    def body(q_ref, kp_ref, kc_ref, vp_ref, vc_ref, do_ref, lse_ref, bp_ref, bc_ref, sink_ref, rb_ref,
             dq_ref, dk_ref, dv_ref, dsink_ref, dsc_ref, bias_ref, dk_acc, dv_acc):
        i = pl.program_id(0)
        lo, band, ks, vs = _swa_common(i, kp_ref, kc_ref, vp_ref, vc_ref, bp_ref, bc_ref, rb_ref, bias_ref)

        @pl.when(i == 0)
        def _():
            dk_acc[...] = jnp.zeros_like(dk_acc)
            dv_acc[...] = jnp.zeros_like(dv_acc)
            dsc_ref[...] = jnp.zeros_like(dsc_ref)
            dsink_ref[...] = jnp.zeros_like(dsink_ref)

        heads, pairs, blocks = range(8), range(4), range(2)
        rowsum = lambda t: jnp.sum(t, axis=1, keepdims=True)
        by_head = lambda t: jnp.concatenate([jnp.where(lo, t, 0), jnp.where(lo, 0, t)], axis=0)
        q2 = [q_ref[:, g * PAIR:(g + 1) * PAIR] for g in pairs]
        d2 = [do_ref[:, g * PAIR:(g + 1) * PAIR].astype(BF16) for g in pairs]
        qs = [by_head(q2[g]) for g in pairs]
        dos = [by_head(d2[g]) for g in pairs]
        s2 = [[_nt(q2[g], ks[b][g // 2]) for b in blocks] for g in pairs]
        dp2 = [[_nt(d2[g], vs[b][g // 2]) for b in blocks] for g in pairs]
        lse_h = [lse_ref[:, h * QB:(h + 1) * QB] for h in heads]
        sink = [sink_ref[0, h] for h in heads]
        pr = [[jnp.exp(jnp.where(band[b], _lane_half(s2[h // 2][b], h) * SCALE + bias_ref[h, b], NEG_INF) - lse_h[h])
               for b in blocks] for h in heads]
        dp = [[_lane_half(dp2[h // 2][b], h) for b in blocks] for h in heads]
        delta = [rowsum(pr[h][0] * dp[h][0]) + rowsum(pr[h][1] * dp[h][1]) for h in heads]
        lane1 = lax.broadcasted_iota(jnp.int32, (1, QB), 1)
        dsink = jnp.zeros((1, QB), F32)
        for h in heads:
            dsink = dsink + jnp.where(lane1 == h, -jnp.sum(jnp.exp(sink[h] - lse_h[h][:, :1]) * delta[h]), 0.0)
        dsink_ref[...] += dsink
        dsc = [[pr[h][b] * (dp[h][b] - delta[h]) for b in blocks] for h in heads]
        for h in heads:
            for b in blocks:
                dsc_ref[h, b] += dsc[h][b]
        dzb = [[(dsc[h][b] * SCALE).astype(BF16) for b in blocks] for h in heads]
        prb = [[pr[h][b].astype(BF16) for b in blocks] for h in heads]
        pair_of = lambda t, g, b, axis: jnp.concatenate([t[2 * g][b], t[2 * g + 1][b]], axis=axis)
        for g in pairs:
            dq = _nn(pair_of(dzb, g, 0, 1), ks[0][g // 2]) + _nn(pair_of(dzb, g, 1, 1), ks[1][g // 2])
            dq_ref[:, g * PAIR:(g + 1) * PAIR] = dq.astype(BF16)

        def key_grad(t, other, b):
            per_kv = [_tn(pair_of(t, 2 * kh, b, 0), other[2 * kh]) + _tn(pair_of(t, 2 * kh + 1, b, 0), other[2 * kh + 1]) for kh in range(2)]
            both = [s + pltpu.roll(s, HEAD_DIM, 1) for s in per_kv]
            return jnp.where(lo, both[0], both[1])

        rp = pl.multiple_of(jnp.maximum(i - 1, 0) * QB, QB)
        rc = pl.multiple_of(i * QB, QB)
        dk_acc[pl.ds(rp, QB), :] += key_grad(dzb, qs, 0)
        dv_acc[pl.ds(rp, QB), :] += key_grad(prb, dos, 0)
        dk_acc[pl.ds(rc, QB), :] += key_grad(dzb, qs, 1)
        dv_acc[pl.ds(rc, QB), :] += key_grad(prb, dos, 1)

        @pl.when(i == nq - 1)
        def _():
            dk_ref[...] = dk_acc[...].astype(BF16)
            dv_ref[...] = dv_acc[...].astype(BF16)

    kv = lambda col, prev: pl.BlockSpec((QB, KV_W), (lambda i: (jnp.maximum(i - 1, 0), col)) if prev else (lambda i: (i, col)))
    full = pl.BlockSpec((QB, QB), lambda i: (0, 0))
    smem = pl.BlockSpec(memory_space=pltpu.SMEM)
    whole = lambda shape: pl.BlockSpec(shape, lambda i: (0,) * len(shape))
    return _call(
        body, name=name, grid=(nq,),
        out_shape=(S((T, SWA_W), BF16), S((T, KV_W), BF16), S((T, KV_W), BF16), S((1, QB), F32), S((8, 2, QB, QB), F32)),
        in_specs=[pl.BlockSpec((QB, SWA_W), lambda i: (i, 3)), kv(kcol, True), kv(kcol, False), kv(vcol, True), kv(vcol, False),
                  pl.BlockSpec((QB, SWA_W), lambda i: (i, 0)), pl.BlockSpec((QB, 8 * QB), lambda i: (i, 0)),
                  full, full, smem, smem],
        out_specs=(pl.BlockSpec((QB, SWA_W), lambda i: (i, 0)), whole((T, KV_W)), whole((T, KV_W)), whole((1, QB)),
                   whole((8, 2, QB, QB))),
        scratch=[pltpu.VMEM((8, 2, QB, QB), F32), pltpu.VMEM((T, KV_W), F32), pltpu.VMEM((T, KV_W), F32)],
        sem=("arbitrary",), args=(p, p, p, p, p, do, lse, bprev, bcur, sinks, rel_bias), riders=riders)


def mix_out_fwd(o_sb, o_sw, g_sb, g_sw, wout, h, g_next, name, riders=()):
    T, D = h.shape
    M = SB_W + SWA_W
    tm = _tile(T, 256)

    def body(a_ref, b_ref, ga_ref, gb_ref, w_ref, h_ref, gn_ref, mx_ref, o_ref, n_ref):
        mx_ref[:, :SB_W] = _rms(a_ref[...], ga_ref[...]).astype(BF16)
        mx_ref[:, SB_W:] = _rms(b_ref[...], gb_ref[...]).astype(BF16)
        out = h_ref[...] + _nn(mx_ref[...], w_ref[...])
        o_ref[...] = out
        n_ref[...] = _rms(out, gn_ref[...]).astype(BF16)

    row = lambda n: pl.BlockSpec((tm, n), lambda i: (i, 0))
    vec = lambda n: pl.BlockSpec((1, n), lambda i: (0, 0))
    return _call(
        body, name=name, grid=(T // tm,), out_shape=(S((T, M), BF16), S((T, D), F32), S((T, D), BF16)),
        in_specs=[row(SB_W), row(SWA_W), vec(SB_W), vec(SWA_W), pl.BlockSpec((M, D), lambda i: (0, 0)), row(D), vec(D)],
        out_specs=(row(M), row(D), row(D)),
        sem=("parallel",), args=(o_sb, o_sw, g_sb, g_sw, wout, h, g_next), riders=riders)


def loss_head(h, g, target, name):
    T, D = h.shape
    tm = _tile(T, 256)

    def body(h_ref, g_ref, t_ref, loss_ref, dh_ref, dhb_ref, dg_ref):
        @pl.when(pl.program_id(0) == 0)
        def _():
            loss_ref[...] = jnp.zeros_like(loss_ref)
            dg_ref[...] = jnp.zeros_like(dg_ref)
        x = h_ref[...]
        err = _rms(x, g_ref[...]) - t_ref[...]
        loss_ref[...] += jnp.full((1, QB), 0.5 * jnp.sum(jnp.mean(err * err, axis=-1)), F32)
        dx, dg = _rms_bwd(err / D, x, g_ref[...])
        dh_ref[...] = dx
        dhb_ref[...] = dx.astype(BF16)
        dg_ref[...] += dg

    row = pl.BlockSpec((tm, D), lambda i: (i, 0))
    vec = pl.BlockSpec((1, D), lambda i: (0, 0))
    return pl.pallas_call(
        body, name=name, grid=(T // tm,), out_shape=(S((1, QB), F32), S((T, D), F32), S((T, D), BF16), S((1, D), F32)),
        in_specs=[row, vec, row], out_specs=(pl.BlockSpec((1, QB), lambda i: (0, 0)), row, row, vec),
        compiler_params=_params(("arbitrary",)),
    )(h, g, target)


def ffn_down_bwd(dhb, wd, gate, up, name, riders=()):
    T, D = dhb.shape
    F = wd.shape[0]
    tr, tn = _tile(T, 512), _tile(F, 256)

    def body(d_ref, w_ref, g_ref, u_ref, o_ref):
        w = w_ref[...]
        for r in range(T // tr):
            rows = slice(r * tr, (r + 1) * tr)
            da = 0.5 * _nt(d_ref[rows, :], w)
            o_ref[0, rows, :] = (da * g_ref[rows, :].astype(F32)).astype(BF16)
            o_ref[1, rows, :] = (da * u_ref[rows, :].astype(F32)).astype(BF16)

    tile = pl.BlockSpec((T, tn), lambda j: (0, j))
    return _call(
        body, name=name, grid=(F // tn,), out_shape=S((2, T, F), BF16),
        in_specs=[pl.BlockSpec((T, D), lambda j: (0, 0)), pl.BlockSpec((tn, D), lambda j: (j, 0)), tile, tile],
        out_specs=pl.BlockSpec((2, T, tn), lambda j: (0, 0, j)),
        sem=("parallel",), args=(dhb, wd, gate, up), riders=riders)


def tn_matmul(xs, y, alpha, name, riders=()):
    B, T, N = xs.shape
    D = y.shape[1]
    tn = _tile(N, 256)

    def body(x_ref, y_ref, o_ref, ob_ref):
        o = alpha * _tn(x_ref[...], y_ref[...])
        o_ref[...] = o
        ob_ref[...] = o.astype(BF16)

    tile = pl.BlockSpec((None, tn, D), lambda s, j: (s, j, 0))
    return _call(
        body, name=name, grid=(B, N // tn), out_shape=(S((B, N, D), F32), S((B, N, D), BF16)),
        in_specs=[pl.BlockSpec((None, T, tn), lambda s, j: (s, 0, j)), pl.BlockSpec((T, D), lambda s, j: (0, 0))],
        out_specs=(tile, tile), sem=("parallel", "parallel"), args=(xs, y), riders=riders)


def nn_rms_bwd(xs, ws, h_in, g, dh, name, riders=()):
    B, T, K = xs.shape
    D = ws.shape[2]
    tm = _tile(T, 256)

    def body(x_ref, w_ref, h_ref, g_ref, d_ref, o_ref, ob_ref, dg_ref):
        @pl.when(pl.program_id(0) == 0)
        def _():
            dg_ref[...] = jnp.zeros_like(dg_ref)
        dn = _nn(x_ref[0], w_ref[0])
        for s in range(1, B):
            dn = dn + _nn(x_ref[s], w_ref[s])
        dx, dg = _rms_bwd(dn, h_ref[...], g_ref[...])
        out = d_ref[...] + dx
        o_ref[...] = out
        ob_ref[...] = out.astype(BF16)
        dg_ref[...] += dg

    row = pl.BlockSpec((tm, D), lambda i: (i, 0))
    vec = pl.BlockSpec((1, D), lambda i: (0, 0))
    return _call(
        body, name=name, grid=(T // tm,), out_shape=(S((T, D), F32), S((T, D), BF16), S((1, D), F32)),
        in_specs=[pl.BlockSpec((B, tm, K), lambda i: (0, i, 0)), pl.BlockSpec((B, K, D), lambda i: (0, 0, 0)), row, vec, row],
        out_specs=(row, row, vec),
        sem=("arbitrary",), args=(xs, ws, h_in, g, dh), riders=riders)


def mix_out_bwd(dhb, wout, o_sb, o_sw, g_sb, g_sw, name):
    T, D = dhb.shape
    tm = _tile(T, 256)

    def body(d_ref, w_ref, a_ref, b_ref, ga_ref, gb_ref, da_ref, db_ref, dga_ref, dgb_ref):
        @pl.when(pl.program_id(0) == 0)
        def _():
            dga_ref[...] = jnp.zeros_like(dga_ref)
            dgb_ref[...] = jnp.zeros_like(dgb_ref)
        dm = _nt(d_ref[...], w_ref[...])
        dxa, dga = _rms_bwd(dm[:, :SB_W], a_ref[...], ga_ref[...])
        dxb, dgb = _rms_bwd(dm[:, SB_W:], b_ref[...], gb_ref[...])
        da_ref[...] = dxa
        db_ref[...] = dxb
        dga_ref[...] += dga
        dgb_ref[...] += dgb

    row = lambda n: pl.BlockSpec((tm, n), lambda i: (i, 0))
    vec = lambda n: pl.BlockSpec((1, n), lambda i: (0, 0))
    return pl.pallas_call(
        body, name=name, grid=(T // tm,),
        out_shape=(S((T, SB_W), F32), S((T, SWA_W), F32), S((1, SB_W), F32), S((1, SWA_W), F32)),
        in_specs=[row(D), pl.BlockSpec((SB_W + SWA_W, D), lambda i: (0, 0)), row(SB_W), row(SWA_W), vec(SB_W), vec(SWA_W)],
        out_specs=(row(SB_W), row(SWA_W), vec(SB_W), vec(SWA_W)),
        compiler_params=_params(("arbitrary",)),
    )(dhb, wout, o_sb, o_sw, g_sb, g_sw)


def rel_bias_grad(dscs, bprev, bcur, name):
    n = len(dscs)

    def body(*refs):
        bp_ref, bc_ref, o_ref = refs[n], refs[n + 1], refs[n + 2]
        bks = [bp_ref[...], bc_ref[...]]
        row = lax.broadcasted_iota(jnp.int32, (N_BUCKETS, QB), 0)
        lane = lax.broadcasted_iota(jnp.int32, (N_BUCKETS, QB), 1)
        out = jnp.zeros((N_BUCKETS, QB), F32)
        for h in range(8):
            tot = [sum(refs[l][h, b] for l in range(n)) for b in range(2)]
            for b in range(N_BUCKETS):
                val = jnp.sum(jnp.where(bks[0] == b, tot[0], 0.0)) + jnp.sum(jnp.where(bks[1] == b, tot[1], 0.0))
                out = jnp.where((row == b) & (lane == h), val, out)
        o_ref[...] = out

    return pl.pallas_call(body, name=name, out_shape=S((N_BUCKETS, QB), F32), compiler_params=_params())(*dscs, bprev, bcur)


def _adamw(w, g, m, v):
    m = ADAM_B1 * m + (1.0 - ADAM_B1) * g
    v = ADAM_B2 * v + (1.0 - ADAM_B2) * (g * g)
    m_hat = m / (1.0 - ADAM_B1 ** ADAM_STEP)
    v_hat = v / (1.0 - ADAM_B2 ** ADAM_STEP)
    delta = -ADAM_LR * (m_hat / (jnp.sqrt(v_hat) + ADAM_EPS) + ADAM_WD * w)
    return delta, m, v


def adamw_scattered(w, m, v, owns, others, name, riders=()):
    L, R, C = w.shape
    tr = _rows_tile(R, 176)

    def body(w_ref, m_ref, v_ref, *rest):
        own_refs, other_refs = rest[:L], rest[L:2 * L]
        g_ref, d_ref, mo_ref, vo_ref = rest[2 * L:]
        layer = pl.program_id(0)

        def grad(k):
            o = other_refs[k]
            return own_refs[k][...] + o[0].astype(F32) + o[1].astype(F32) + o[2].astype(F32)

        g = grad(0)
        for k in range(1, L):
            g = jnp.where(layer == k, grad(k), g)
        d, mn, vn = _adamw(w_ref[...], g, m_ref[...], v_ref[...])
        g_ref[...] = g
        d_ref[...] = d
        mo_ref[...] = mn
        vo_ref[...] = vn

    tile = pl.BlockSpec((None, tr, C), lambda l, i: (l, i, 0))
    return _call(
        body, name=name, grid=(L, R // tr), out_shape=(S((L, R, C), F32),) * 4,
        in_specs=[tile] * 3 + [pl.BlockSpec((tr, C), lambda l, i: (i, 0))] * L + [pl.BlockSpec((3, tr, C), lambda l, i: (0, i, 0))] * L,
        out_specs=(tile,) * 4, sem=("parallel", "parallel"), args=(w, m, v, *owns, *others), riders=riders)


def adamw_small(w, gs, m, v, name):
    R, C = w.shape

    def body(w_ref, g_ref, m_ref, v_ref, go_ref, d_ref, mo_ref, vo_ref):
        g = g_ref[0]
        for k in range(1, N_DEV):
            g = g + g_ref[k]
        d, mn, vn = _adamw(w_ref[...], g, m_ref[...], v_ref[...])
        go_ref[...] = g
        d_ref[...] = d
        mo_ref[...] = mn
        vo_ref[...] = vn

    return pl.pallas_call(body, name=name, out_shape=(S((R, C), F32),) * 4, compiler_params=_params())(w, gs, m, v)


def kernel(x, norm_ffn1, w_ffn1_gu, w_ffn1_down, norm_mix, w_in, sinks, norm_out_sb, norm_out_swa, w_out, norm_ffn2, w_ffn2_gu, w_ffn2_down, rel_bias, norm_final, loss_target, m_norm_ffn1, m_w_ffn1_gu, m_w_ffn1_down, m_norm_mix, m_w_in, m_sinks, m_norm_out_sb, m_norm_out_swa, m_w_out, m_norm_ffn2, m_w_ffn2_gu, m_w_ffn2_down, m_rel_bias, m_norm_final, v_norm_ffn1, v_w_ffn1_gu, v_w_ffn1_down, v_norm_mix, v_w_in, v_sinks, v_norm_out_sb, v_norm_out_swa, v_w_out, v_norm_ffn2, v_w_ffn2_gu, v_w_ffn2_down, v_rel_bias, v_norm_final):
    L = norm_ffn1.shape[0]
    T, D = x.shape[1], x.shape[2]
    F = w_ffn1_down.shape[1] * N_DEV
    h = x.reshape(T, D)
    target = loss_target.reshape(T, D)
    after, upto, before = _tri_consts()
    bprev, bcur = _t5_buckets()

    local = {}
    for l in range(L):
        local[f"gu1_{l}"] = w_ffn1_gu[l].T.astype(BF16)
        local[f"d1_{l}"] = w_ffn1_down[l].astype(BF16)
        local[f"in_{l}"] = w_in[l].T.astype(BF16)
        local[f"out_{l}"] = w_out[l].astype(BF16)
        local[f"gu2_{l}"] = w_ffn2_gu[l].T.astype(BF16)
        local[f"d2_{l}"] = w_ffn2_down[l].astype(BF16)
    full, partial = {}, {}
    grads, chip_sum, recv_b = {}, {}, {}

    def run(fn, *args, ag=(), rs1=(), rs2=()):
        halves = lambda names: [n if isinstance(n, tuple) else (n, None) for n in names]
        ag, rs2 = [(n, k) for n, k in halves(ag) if n in local], halves(rs2)
        rows = lambda k, total: None if k is None else (k * (total // 2), total // 2)

        def second(n, k):
            sb = chip_sum[n][1]
            return scatter_second(sb, rows(k, sb.shape[1]), recv_b.get(n))

        riders = ([gather(local[n], rows(k, local[n].shape[0]), partial.get(n)) for n, k in ag]
                  + [scatter_first(grads[n][1]) for n in rs1] + [second(n, k) for n, k in rs2])
        if not riders:
            return fn(*args)
        outs, per = fn(*args, riders=riders)
        per = [p[0] for p in per]
        for n, k in ag:
            buf = per.pop(0)
            if k == 0:
                partial[n] = buf
            else:
                full[n] = buf.reshape(N_DEV * buf.shape[1], D)
        for n in rs1:
            chip_sum[n] = scatter_add(grads[n][0], per.pop(0), f"rs_add_{n}")
        for n, _ in rs2:
            recv_b[n] = per.pop(0)
        return outs

    gu = lambda n: full[n].reshape(2, F, D)
    slots = lambda pair: tuple(t.reshape(N_DEV, -1, D) for t in pair)
    vec = lambda a: a.reshape(1, -1)

    PW = 8 * QB
    sizes = [L * D, L * D, L * D, D, L * SB_W, L * SWA_W, L * 8, N_BUCKETS * 8, 1]
    n_rows = -(-sum(sizes) // PW)
    n_rows += (-n_rows) % 8

    def pack(ffn1, mix, ffn2, final, osb, osw, snk, rel, extra):
        flat = lambda a: jnp.concatenate([t.reshape(-1) for t in a]) if isinstance(a, (list, tuple)) else a.reshape(-1)
        if isinstance(snk, (list, tuple)):
            snk = [t.reshape(-1)[:8] for t in snk]
        pieces = [flat(a) for a in (ffn1, mix, ffn2, final, osb, osw, snk, rel, extra)]
        pieces.append(jnp.zeros((n_rows * PW - sum(sizes),), F32))
        return jnp.concatenate(pieces).reshape(n_rows, PW)

    def unpack(arr):
        flat, parts, at = arr.reshape(-1), [], 0
        for n in sizes:
            parts.append(flat[at:at + n])
            at += n
        ffn1, mix, ffn2, final, osb, osw, snk, rel, extra = parts
        return (ffn1.reshape(L, D), mix.reshape(L, D), snk.reshape(L, 8), osb.reshape(L, SB_W), osw.reshape(L, SWA_W),
                ffn2.reshape(L, D), rel.reshape(N_BUCKETS, 8), final, extra[0])

    zero = jnp.zeros((1,), F32)
    w_small = pack(norm_ffn1, norm_mix, norm_ffn2, norm_final, norm_out_sb, norm_out_swa, sinks, rel_bias, zero)

    saved = []
    n_next = run(rms_cast, h, vec(norm_ffn1[0]), "rms_first", ag=("gu1_0",))
    for l in range(L):
        nx = l + 1
        s = {"h0": h, "n1": n_next}
        s["gate1"], s["up1"], s["a1"] = run(ffn_up_fwd, s["n1"], gu(f"gu1_{l}"), f"ffn1_up{l}",
                                            ag=(f"d1_{l}", ("in_0", 0) if l == 0 else (f"in_{l}", 1)))
        h = run(ffn_down_fwd, s["a1"], full[f"d1_{l}"], h, None, f"ffn1_down{l}", ag=(("in_0", 1),) if l == 0 else ())
        s["h1"] = h
        s["n2"], s["p"] = mix_in_fwd(h, vec(norm_mix[l]), full[f"in_{l}"], f"mix_in{l}")
        s["o_sb"], s["tot"] = run(sb_attn_fwd, s["p"], after, f"sb_fwd{l}", ag=(f"out_{l}", f"gu2_{l}", f"d2_{l}"))
        s["o_sw"], s["lse"] = run(swa_fwd, s["p"], vec(sinks[l]), rel_bias, bprev, bcur, f"swa_fwd{l}", ag=((f"gu1_{nx}", 0),))
        s["mixed"], h, s["n3"] = run(mix_out_fwd, s["o_sb"], s["o_sw"], vec(norm_out_sb[l]), vec(norm_out_swa[l]),
                                     full[f"out_{l}"], h, vec(norm_ffn2[l]), f"mix_out{l}")
        s["h2"] = h
        s["gate2"], s["up2"], s["a2"] = run(ffn_up_fwd, s["n3"], gu(f"gu2_{l}"), f"ffn2_up{l}",
                                            ag=((f"gu1_{nx}", 1), (f"in_{nx}", 0)))
        if nx < L:
            h, n_next = run(ffn_down_fwd, s["a2"], full[f"d2_{l}"], h, vec(norm_ffn1[nx]), f"ffn2_down{l}")
        else:
            h = run(ffn_down_fwd, s["a2"], full[f"d2_{l}"], h, None, f"ffn2_down{l}")
        saved.append(s)

    loss_part, dh, dhb, dg_final = loss_head(h, vec(norm_final), target, "loss_head")

    small = {k: [None] * L for k in ("ffn1", "mix", "sinks", "osb", "osw", "ffn2", "dsc")}
    for l in reversed(range(L)):
        s = saved[l]

        def ffn_bwd(dh, dhb, tag, gate, up, a, n, h_in, g, r_down, r_dwgu, r_dwd, r_up):
            gu_n, d_n = f"gu{tag}_{l}", f"d{tag}_{l}"
            dgu = run(ffn_down_bwd, dhb, full[d_n], gate, up, f"ffn{tag}_down_bwd{l}", **r_down)
            grads[gu_n] = slots(run(tn_matmul, dgu, n, 1.0, f"ffn{tag}_dwgu{l}", **r_dwgu))
            grads[d_n] = slots(run(tn_matmul, a[None], dhb, 0.5, f"ffn{tag}_dwd{l}", **r_dwd))
            return run(nn_rms_bwd, dgu, gu(gu_n), h_in, g, dh, f"ffn{tag}_up_bwd{l}", **r_up)

        later = l + 1 < L
        dh, dhb, small["ffn2"][l] = ffn_bwd(dh, dhb, 2, s["gate2"], s["up2"], s["a2"], s["n3"], s["h2"], vec(norm_ffn2[l]),
                                            {}, dict(rs2=(f"d1_{l + 1}",) if later else ()), {},
                                            dict(rs1=(f"gu2_{l}", f"d2_{l}"), rs2=((f"gu1_{l + 1}", 1),) if later else ()))
        do_sb, do_sw, small["osb"][l], small["osw"][l] = mix_out_bwd(
            dhb, full[f"out_{l}"], s["o_sb"], s["o_sw"], vec(norm_out_sb[l]), vec(norm_out_swa[l]), f"mix_out_bwd{l}")
        grads[f"out_{l}"] = slots(tn_matmul(s["mixed"][None], dhb, 1.0, f"dwout{l}"))
        dq_sb, dk_sb, dv_sb = run(sb_attn_bwd, s["p"], do_sb, s["tot"], upto, before, f"sb_bwd{l}",
                                  rs2=(f"gu2_{l}", f"d2_{l}"), rs1=(f"out_{l}",))
        dq_sw, dk_sw, dv_sw, small["sinks"][l], small["dsc"][l] = swa_bwd(
            s["p"], do_sw, s["lse"], vec(sinks[l]), rel_bias, bprev, bcur, f"swa_bwd{l}")
        dp = jnp.concatenate([dq_sb, dk_sb, dv_sb, dq_sw, dk_sw, dv_sw], axis=1)
        dh, dhb, small["mix"][l] = nn_rms_bwd(dp[None], full[f"in_{l}"][None], s["h1"], vec(norm_mix[l]), dh, f"mix_in_bwd{l}")
        grads[f"in_{l}"] = slots(tn_matmul(dp[None], s["n2"], 1.0, f"dwin{l}"))
        dh, dhb, small["ffn1"][l] = ffn_bwd(dh, dhb, 1, s["gate1"], s["up1"], s["a1"], s["n1"], s["h0"], vec(norm_ffn1[l]),
                                            dict(rs1=(f"in_{l}",), rs2=(f"out_{l}",)), dict(rs2=(f"in_{l}",)),
                                            dict(rs1=(f"gu1_{l}",)), dict(rs1=(f"d1_{l}",), rs2=((f"gu1_{l}", 0),)))

    grad_x = dh.reshape(x.shape)

    upd = {}
    for nm, w, m, v, transposed, last in (
            ("gu2", w_ffn2_gu, m_w_ffn2_gu, v_w_ffn2_gu, True, (("gu1_0", 1), "d1_0")), ("d2", w_ffn2_down, m_w_ffn2_down, v_w_ffn2_down, False, ()),
            ("in", w_in, m_w_in, v_w_in, True, ()), ("out", w_out, m_w_out, v_w_out, False, ()),
            ("gu1", w_ffn1_gu, m_w_ffn1_gu, v_w_ffn1_gu, True, ()), ("d1", w_ffn1_down, m_w_ffn1_down, v_w_ffn1_down, False, ())):
        turn = (lambda a: jnp.swapaxes(a, 1, 2)) if transposed else (lambda a: a)
        names = [f"{nm}_{l}" for l in range(L)]
        res = run(adamw_scattered, turn(w), turn(m), turn(v), [chip_sum[n][0] for n in names], [recv_b[n] for n in names],
                  f"adamw_{nm}", rs2=last)
        upd[nm] = tuple(turn(r) for r in res)

    d_rel = rel_bias_grad(small["dsc"], bprev, bcur, "rel_bias_grad")[:, :8]
    g_small = pack(small["ffn1"], small["mix"], small["ffn2"], dg_final, small["osb"], small["osw"], small["sinks"], d_rel,
                   loss_part[0, :1])
    m_small = pack(m_norm_ffn1, m_norm_mix, m_norm_ffn2, m_norm_final, m_norm_out_sb, m_norm_out_swa, m_sinks, m_rel_bias, zero)
    v_small = pack(v_norm_ffn1, v_norm_mix, v_norm_ffn2, v_norm_final, v_norm_out_sb, v_norm_out_swa, v_sinks, v_rel_bias, zero)
    gs_small = all_gather_rows(g_small, "ag_small")
    summed = adamw_small(w_small, gs_small, m_small, v_small, "adamw_small")
    small_out = [unpack(a) for a in summed]
    loss = small_out[0][8]

    def group(k):
        sm = small_out[k]
        return (sm[0], upd["gu1"][k], upd["d1"][k], sm[1], upd["in"][k], sm[2], sm[3], sm[4], upd["out"][k], sm[5],
                upd["gu2"][k], upd["d2"][k], sm[6], sm[7])

    return (loss, grad_x, *group(0), *group(1), *group(2), *group(3))
```

```python
import math

import jax
import jax.numpy as jnp
from jax import lax
from jax.experimental import pallas as pl
from jax.experimental.pallas import tpu as pltpu

F32 = jnp.float32
BF16 = jnp.bfloat16
S = jax.ShapeDtypeStruct

N_DEV = 8
HEAD_DIM = 64
SB_HEADS = 8
PAIR = 2 * HEAD_DIM
SB_W = 512
SWA_W = 512
KV_W = 128
IN_W = 3 * SB_W + SWA_W + 2 * KV_W
QB = 128
N_BUCKETS = 32
MAX_DISTANCE = 128
EPS = 1e-6
NEG_INF = -1e30
SCALE = HEAD_DIM ** -0.5

ADAM_LR = 0.001
ADAM_B1 = 0.9
ADAM_B2 = 0.999
ADAM_EPS = 1e-08
ADAM_WD = 0.01
ADAM_STEP = 10

VMEM_LIMIT = 56 * 1024 * 1024
MESH = pl.DeviceIdType.MESH


def _params(sem=None, vmem=VMEM_LIMIT):
    return pltpu.CompilerParams(dimension_semantics=sem, vmem_limit_bytes=vmem)


def _nn(a, b):
    return jnp.dot(a, b, preferred_element_type=F32)


def _nt(a, b):
    return lax.dot_general(a, b, (((1,), (1,)), ((), ())), preferred_element_type=F32)


def _tn(a, b):
    return lax.dot_general(a, b, (((0,), (0,)), ((), ())), preferred_element_type=F32)


def _tri(xs, m):
    return [_nn(x.astype(BF16), m) for x in xs]


def _rms(x, g):
    r = lax.rsqrt(jnp.mean(x * x, axis=-1, keepdims=True) + EPS)
    return x * r * g


def _rms_bwd(dy, x, g):
    r = lax.rsqrt(jnp.mean(x * x, axis=-1, keepdims=True) + EPS)
    xhat = x * r
    u = dy * g
    dx = r * (u - xhat * jnp.mean(u * xhat, axis=-1, keepdims=True))
    return dx, jnp.sum(dy * xhat, axis=0, keepdims=True)


def _softplus_logsig(z):
    sp = jnp.maximum(z, 0.0) + jnp.log(1.0 + jnp.exp(-jnp.abs(z)))
    return sp, z - sp


def _tile(n, want):
    t = min(n, want)
    while n % t:
        t //= 2
    return t


def _place():
    x, y, c = lax.axis_index("x"), lax.axis_index("y"), lax.axis_index("c")
    chips = [(1 - x, y), (x, 1 - y), (1 - x, 1 - y)]
    return x, y, c, chips


def all_gather_rows(v, name):
    R, C = v.shape

    def body(v_ref, out_ref, send_sems, recv_sems, local_sem):
        x, y, c, chips = _place()
        me, sibling = (x, y, c), (x, y, 1 - c)

        def slot(px, py, pc):
            return out_ref.at[4 * px + 2 * py + pc]

        def copy(k, block, to, src=None):
            return pltpu.make_async_remote_copy(
                src_ref=slot(*block) if src is None else src, dst_ref=slot(*block),
                send_sem=send_sems.at[k], recv_sem=recv_sems.at[k], device_id=to, device_id_type=MESH)

        mine = pltpu.make_async_copy(v_ref, slot(*me), local_sem)
        mine.start()
        first = [copy(0, me, sibling, src=v_ref)]
        first += [copy(1 + j, me, (*chip, c), src=v_ref) for j, chip in enumerate(chips)]
        for cp in first:
            cp.start()
        passed = [copy(4 + j, (*chip, c), sibling) for j, chip in enumerate(chips)]
        for j, chip in enumerate(chips):
            copy(1 + j, (*chip, c), me).wait_recv()
            passed[j].start()
        copy(0, sibling, me).wait_recv()
        for j, chip in enumerate(chips):
            copy(4 + j, (*chip, 1 - c), me).wait_recv()
        for cp in first + passed:
            cp.wait_send()
        mine.wait()

    return pl.pallas_call(
        body, name=name, out_shape=S((N_DEV, R, C), v.dtype),
        in_specs=[pl.BlockSpec(memory_space=pl.ANY)], out_specs=pl.BlockSpec(memory_space=pl.ANY),
        scratch_shapes=[pltpu.SemaphoreType.DMA((7,)), pltpu.SemaphoreType.DMA((7,)), pltpu.SemaphoreType.DMA],
    )(v)


class _Exchange:
    def __init__(self, ins, outs, sizes, n_local, plan, aliases=None):
        self.ins, self.outs, self.plan, self.aliases = list(ins), list(outs), plan, aliases or {}
        self.sizes, self.n_local = list(sizes), n_local

    def scratch(self):
        n = sum(self.sizes)
        return [pltpu.SemaphoreType.DMA((n,)), pltpu.SemaphoreType.DMA((n,)), pltpu.SemaphoreType.DMA((max(self.n_local, 1),))]

    def _copies(self, in_refs, out_refs, sems):
        send_sems, recv_sems, local_sems = sems
        phases, local = self.plan(in_refs, out_refs)
        out, k = [], 0
        for phase in phases:
            out.append([pltpu.make_async_remote_copy(src_ref=s, dst_ref=d, send_sem=send_sems.at[k + n], recv_sem=recv_sems.at[k + n],
                                                     device_id=dev, device_id_type=MESH) for n, (s, d, dev) in enumerate(phase)])
            k += len(phase)
        return out, [pltpu.make_async_copy(s, d, local_sems.at[n]) for n, (s, d) in enumerate(local)]

    def start(self, in_refs, out_refs, sems):
        phases, loc = self._copies(in_refs, out_refs, sems)
        for cp in phases[0] + loc:
            cp.start()

    def advance(self, hook, in_refs, out_refs, sems):
        p = hook - (3 - len(self.sizes))
        if p >= 1:
            phases, _ = self._copies(in_refs, out_refs, sems)
            for cp in phases[p - 1]:
                cp.wait_recv()
            for cp in phases[p]:
                cp.start()

    def finish(self, in_refs, out_refs, sems):
        phases, loc = self._copies(in_refs, out_refs, sems)
        for cp in phases[-1]:
            cp.wait_recv()
        for phase in phases:
            for cp in phase:
                cp.wait_send()
        for cp in loc:
            cp.wait()


def gather(v, rows=None, into=None):
    R, C = v.shape
    r0, nr = rows or (0, R)
    na = min(nr, ((nr // 2 + 15) // 16) * 16)

    def plan(ins, outs):
        x, y, c, _ = _place()
        xn, yn, dg, sibling = (1 - x, y), (x, 1 - y), (1 - x, 1 - y), (x, y, 1 - c)
        slot = lambda chip, start=r0, count=nr: outs[0].at[4 * chip[0] + 2 * chip[1] + c, pl.ds(start, count), :]
        src, mine = ins[0].at[pl.ds(r0, nr), :], slot((x, y))
        same = lambda ref, to: (ref, ref, to)
        first = [(src, mine, sibling), (src, mine, (*xn, c)), (src, mine, (*yn, c))]
        relay = [same(slot(xn, r0, na), (*yn, c)), same(slot(yn, r0 + na, nr - na), (*xn, c))]
        onward = [same(slot(xn), sibling), same(slot(yn), sibling), same(slot(dg), sibling)]
        return [first, relay, onward], [(src, mine)]

    if into is None:
        return _Exchange([v], [S((N_DEV, R, C), v.dtype)], (3, 2, 3), 1, plan)
    return _Exchange([v, into], [S((N_DEV, R, C), v.dtype)], (3, 2, 3), 1, plan, aliases={1: 0})


def scatter_first(gb):
    _, R, C = gb.shape

    def plan(ins, outs):
        x, y, c, chips = _place()
        owners = [(x, y)] + chips
        return [[(ins[0].at[4 * px + 2 * py + (1 - c)], outs[0].at[j], (x, y, 1 - c)) for j, (px, py) in enumerate(owners)]], []

    return _Exchange([gb], [S((4, R, C), BF16)], (4,), 0, plan)


def scatter_second(sb, rows=None, into=None):
    r0, nr = rows or (0, sb.shape[1])

    def plan(ins, outs):
        x, y, c, chips = _place()
        part = lambda ref, j: ref.at[j, pl.ds(r0, nr), :]
        return [[(part(ins[0], j), part(outs[0], j), (*chips[j], c)) for j in range(3)]], []

    if into is None:
        return _Exchange([sb], [S(sb.shape, BF16)], (3,), 0, plan)
    return _Exchange([sb, into], [S(sb.shape, BF16)], (3,), 0, plan, aliases={1: 0})


def _call(body, *, name, grid, in_specs, out_specs, out_shape, args, scratch=(), sem=None, riders=(), marks=None):
    single = not isinstance(out_shape, (tuple, list))
    out_shape = (out_shape,) if single else tuple(out_shape)
    out_specs = (out_specs,) if single else tuple(out_specs)
    n_in, n_out, n_sc = len(in_specs), len(out_shape), len(scratch)
    if not riders:
        res = pl.pallas_call(body, name=name, grid=grid, in_specs=list(in_specs), out_specs=out_specs, out_shape=out_shape,
                             scratch_shapes=list(scratch), compiler_params=_params(sem))(*args)
        return res[0] if single else res
    r_ins = [a for r in riders for a in r.ins]
    r_outs = [o for r in riders for o in r.outs]
    r_scr = [s for r in riders for s in r.scratch()]
    aliases, i0, o0 = {}, n_in, n_out
    for r in riders:
        for a, b in r.aliases.items():
            aliases[i0 + a] = o0 + b
        i0, o0 = i0 + len(r.ins), o0 + len(r.outs)
    steps = math.prod(grid)

    def full(*refs):
        ins, rin = refs[:n_in], refs[n_in:n_in + len(r_ins)]
        pos = n_in + len(r_ins)
        outs, rout = refs[pos:pos + n_out], refs[pos + n_out:pos + n_out + len(r_outs)]
        pos += n_out + len(r_outs)
        sc, rsc = refs[pos:pos + n_sc], refs[pos + n_sc:]
        step = 0
        for d, n in enumerate(grid):
            step = step * n + pl.program_id(d)

        def each(method, *lead):
            i, o = 0, 0
            for k, r in enumerate(riders):
                getattr(r, method)(*lead, rin[i:i + len(r.ins)], rout[o:o + len(r.outs)], rsc[3 * k:3 * k + 3])
                i, o = i + len(r.ins), o + len(r.outs)

        @pl.when(step == 0)
        def _():
            each("start")
        body(*ins, *outs, *sc)

        late = max(steps - 1 - max(steps // 8, 1), 0)
        first, second = marks or (min((3 * steps) // 5, late), late)

        @pl.when(step == first)
        def _():
            each("advance", 1)

        @pl.when(step == second)
        def _():
            each("advance", 2)

        @pl.when(step == steps - 1)
        def _():
            each("finish")

    anywhere = pl.BlockSpec(memory_space=pl.ANY)
    res = pl.pallas_call(
        full, name=name, grid=grid, in_specs=list(in_specs) + [anywhere] * len(r_ins),
        out_specs=out_specs + (anywhere,) * len(r_outs), out_shape=out_shape + tuple(r_outs),
        scratch_shapes=list(scratch) + r_scr, input_output_aliases=aliases,
        compiler_params=_params(("arbitrary",) * len(grid)))(*args, *r_ins)
    host, rest, per = res[:n_out], list(res[n_out:]), []
    for r in riders:
        per.append(rest[:len(r.outs)])
        rest = rest[len(r.outs):]
    return (host[0] if single else tuple(host)), per


def _rows_tile(n, cap):
    return max(t for t in range(16, min(n, cap) + 1, 16) if n % t == 0)


def scatter_add(g, ra, name):
    _, R, C = g.shape
    tr = _rows_tile(R, 176)
    x, y, c, chips = _place()
    slots = jnp.stack([4 * px + 2 * py + c for px, py in [(x, y)] + chips]).astype(jnp.int32)

    def body(s_ref, g0, g1, g2, g3, ra_ref, own_ref, sb_ref):
        own_ref[...] = g0[...] + ra_ref[0].astype(F32)
        for j, gj in enumerate((g1, g2, g3)):
            sb_ref[j] = (gj[...] + ra_ref[j + 1].astype(F32)).astype(BF16)

    spec = pltpu.PrefetchScalarGridSpec(
        num_scalar_prefetch=1, grid=(R // tr,),
        in_specs=[pl.BlockSpec((None, tr, C), lambda i, s, j=j: (s[j], i, 0)) for j in range(4)]
        + [pl.BlockSpec((4, tr, C), lambda i, s: (0, i, 0))],
        out_specs=(pl.BlockSpec((tr, C), lambda i, s: (i, 0)), pl.BlockSpec((3, tr, C), lambda i, s: (0, i, 0))))
    return pl.pallas_call(body, name=name, grid_spec=spec, out_shape=(S((R, C), F32), S((3, R, C), BF16)),
                          compiler_params=_params(("parallel",)))(slots, g, g, g, g, ra)


def rms_cast(h, g, name, riders=()):
    T, D = h.shape
    tm = _tile(T, 512)

    def body(h_ref, g_ref, n_ref):
        n_ref[...] = _rms(h_ref[...], g_ref[...]).astype(BF16)

    row = pl.BlockSpec((tm, D), lambda i: (i, 0))
    return _call(body, name=name, grid=(T // tm,), out_shape=S((T, D), BF16), in_specs=[row, pl.BlockSpec((1, D), lambda i: (0, 0))],
                 out_specs=row, sem=("parallel",), args=(h, g), riders=riders)


def ffn_up_fwd(n, wgu, name, riders=()):
    T, D = n.shape
    F = wgu.shape[1]
    tr, tn = _tile(T, 512), _tile(F, 256)

    def body(n_ref, wg_ref, wu_ref, dgate_ref, dup_ref, a_ref):
        wg, wu = wg_ref[...], wu_ref[...]
        for r in range(T // tr):
            rows = slice(r * tr, (r + 1) * tr)
            x = n_ref[rows, :]
            gate = _nt(x, wg)
            up = _nt(x, wu)
            s = jax.nn.sigmoid(gate)
            silu = gate * s
            dgate_ref[rows, :] = (up * (s * (1.0 + gate * (1.0 - s)))).astype(BF16)
            dup_ref[rows, :] = silu.astype(BF16)
            a_ref[rows, :] = (silu * up).astype(BF16)

    tile = pl.BlockSpec((T, tn), lambda j: (0, j))
    return _call(
        body, name=name, grid=(F // tn,), out_shape=(S((T, F), BF16),) * 3,
        in_specs=[pl.BlockSpec((T, D), lambda j: (0, 0)),
                  pl.BlockSpec((None, tn, D), lambda j: (0, j, 0)), pl.BlockSpec((None, tn, D), lambda j: (1, j, 0))],
        out_specs=(tile, tile, tile), sem=("parallel",), args=(n, wgu, wgu), riders=riders)


def ffn_down_fwd(a, wd, h, g_next, name, riders=()):
    T, F = a.shape
    D = wd.shape[1]
    tm = _tile(T, 256)

    def body(a_ref, w_ref, h_ref, *rest):
        out = h_ref[...] + 0.5 * _nn(a_ref[...], w_ref[...])
        if g_next is None:
            rest[0][...] = out
        else:
            g_ref, o_ref, n_ref = rest
            o_ref[...] = out
            n_ref[...] = _rms(out, g_ref[...]).astype(BF16)

    row = pl.BlockSpec((tm, D), lambda i: (i, 0))
    more = g_next is not None
    return _call(
        body, name=name, grid=(T // tm,), out_shape=(S((T, D), F32), S((T, D), BF16)) if more else S((T, D), F32),
        in_specs=[pl.BlockSpec((tm, F), lambda i: (i, 0)), pl.BlockSpec((F, D), lambda i: (0, 0)), row]
        + ([pl.BlockSpec((1, D), lambda i: (0, 0))] if more else []),
        out_specs=(row, row) if more else row,
        sem=("parallel",), args=(a, wd, h) + ((g_next,) if more else ()), riders=riders)


def mix_in_fwd(h, g, win, name):
    T, D = h.shape
    N = win.shape[0]
    tm = _tile(T, 256)

    def body(h_ref, g_ref, w_ref, n_ref, p_ref):
        n = _rms(h_ref[...], g_ref[...]).astype(BF16)
        n_ref[...] = n
        p_ref[...] = _nt(n, w_ref[...]).astype(BF16)

    return pl.pallas_call(
        body, name=name, grid=(T // tm,), out_shape=(S((T, D), BF16), S((T, N), BF16)),
        in_specs=[pl.BlockSpec((tm, D), lambda i: (i, 0)), pl.BlockSpec((1, D), lambda i: (0, 0)),
                  pl.BlockSpec((N, D), lambda i: (0, 0))],
        out_specs=(pl.BlockSpec((tm, D), lambda i: (i, 0)), pl.BlockSpec((tm, N), lambda i: (i, 0))),
        compiler_params=_params(("parallel",)),
    )(h, g, win)


def _tri_consts():
    r = lax.broadcasted_iota(jnp.int32, (QB, QB), 0)
    c = lax.broadcasted_iota(jnp.int32, (QB, QB), 1)
    ones = jnp.ones((QB, QB), BF16)
    with_sums = lambda tri: jnp.concatenate([tri.astype(BF16), ones], axis=1)
    return with_sums(r > c), with_sums(r <= c), with_sums(r < c)


def _half_masks():
    lane = lax.broadcasted_iota(jnp.int32, (QB, PAIR), 1)
    row = lax.broadcasted_iota(jnp.int32, (QB, PAIR), 0)
    return lane < HEAD_DIM, lane, row


def sb_attn_fwd(p, after, name, riders=()):
    T = p.shape[0]
    nq = T // QB

    def body(q_ref, k_ref, v_ref, m_ref, o_ref, tot_ref, q_sc, acc_ref, z_sc):
        i = pl.program_id(0)
        lo, lane, row = _half_masks()
        causal = lane < row
        heads, pairs = range(SB_HEADS), range(SB_HEADS // 2)
        for hp in pairs:
            q_sc[hp] = (q_ref[:, hp * PAIR:(hp + 1) * PAIR].astype(F32) * SCALE).astype(BF16)
        m2 = m_ref[...]

        def by_head(ref, j, hp):
            t = ref[pl.ds(pl.multiple_of(j * QB, QB), QB), hp * PAIR:(hp + 1) * PAIR]
            return jnp.concatenate([jnp.where(lo, t, 0), jnp.where(lo, 0, t)], axis=0)

        def scores(j):
            return [_nt(q_sc[hp], by_head(k_ref, j, hp)) for hp in pairs]

        def block(j, diag):
            z2 = [z_sc[hp] for hp in pairs]
            ahead = scores(jnp.maximum(j - 1, 0))
            for hp in pairs:
                z_sc[hp] = ahead[hp]
            vs = [by_head(v_ref, j, hp) for hp in pairs]
            spls = [_softplus_logsig(z2[h // 2][:, (h % 2) * QB:(h % 2 + 1) * QB]) for h in heads]
            sp = [jnp.where(causal, spls[h][0], 0.0) if diag else spls[h][0] for h in heads]
            rr = _tri(sp, m2)
            if diag:
                w = [jnp.where(causal, jnp.exp(spls[h][1] - rr[h][:, :QB]), 0.0).astype(BF16) for h in heads]
            else:
                c = [tot_ref[:, h * QB:(h + 1) * QB] for h in heads]
                w = [jnp.exp(spls[h][1] - (c[h] + rr[h][:, :QB])).astype(BF16) for h in heads]
            pv = [_nn(jnp.concatenate([w[2 * hp], w[2 * hp + 1]], axis=1), vs[hp]) for hp in pairs]
            for hp in pairs:
                acc_ref[hp] = pv[hp] if diag else acc_ref[hp] + pv[hp]
            for h in heads:
                tot_ref[:, h * QB:(h + 1) * QB] = rr[h][:, QB:] if diag else c[h] + rr[h][:, QB:]

        first = scores(i)
        for hp in pairs:
            z_sc[hp] = first[hp]
        block(i, True)

        def step(t, carry):
            block(i - 1 - t, False)
            return carry
        lax.fori_loop(0, i, step, 0)
        for hp in pairs:
            o_ref[:, hp * PAIR:(hp + 1) * PAIR] = acc_ref[hp]

    npair = SB_HEADS // 2
    return _call(
        body, name=name, grid=(nq,), out_shape=(S((T, SB_W), F32), S((T, SB_HEADS * QB), F32)),
        in_specs=[pl.BlockSpec((QB, SB_W), lambda i: (i, 0)), pl.BlockSpec((T, SB_W), lambda i: (0, 1)),
                  pl.BlockSpec((T, SB_W), lambda i: (0, 2)), pl.BlockSpec((QB, 2 * QB), lambda i: (0, 0))],
        out_specs=(pl.BlockSpec((QB, SB_W), lambda i: (i, 0)), pl.BlockSpec((QB, SB_HEADS * QB), lambda i: (i, 0))),
        scratch=[pltpu.VMEM((npair, QB, PAIR), BF16), pltpu.VMEM((npair, QB, PAIR), F32), pltpu.VMEM((npair, QB, 2 * QB), F32)],
        sem=("arbitrary",), args=(p, p, p, after), riders=riders,
        marks=((11 * nq) // 16, (14 * nq) // 16))


def sb_attn_bwd(p, do, tot, upto, before, name, riders=()):
    T = p.shape[0]
    nq = T // QB

    def body(q_ref, k_ref, v_ref, do_ref, tot_ref, mp_ref, mg_ref, dq_ref, dk_ref, dv_ref,
             q_sc, d_sc, qd_sc, pg_sc, dq_acc, dk_acc, dv_acc, zd_sc):
        i = pl.program_id(0)
        lo, lane, row = _half_masks()
        causal = lane < row
        heads, pairs = range(SB_HEADS), range(SB_HEADS // 2)

        def by_head(t):
            return jnp.concatenate([jnp.where(lo, t, 0), jnp.where(lo, 0, t)], axis=0)

        for hp in pairs:
            q2 = (q_ref[:, hp * PAIR:(hp + 1) * PAIR].astype(F32) * SCALE).astype(BF16)
            d2 = do_ref[:, hp * PAIR:(hp + 1) * PAIR].astype(BF16)
            q_sc[hp] = q2
            d_sc[hp] = d2
            qd_sc[hp] = by_head(q2)
            qd_sc[SB_HEADS // 2 + hp] = by_head(d2)
        mp, mg = mp_ref[...], mg_ref[...]

        @pl.when(i == 0)
        def _():
            dk_acc[...] = jnp.zeros_like(dk_acc)
            dv_acc[...] = jnp.zeros_like(dv_acc)
        pg_sc[...] = jnp.zeros_like(pg_sc)
        dq_acc[...] = jnp.zeros_like(dq_acc)

        def rows(ref, j, hp):
            return ref[pl.ds(pl.multiple_of(j * QB, QB), QB), hp * PAIR:(hp + 1) * PAIR]

        def products(j):
            return ([_nt(q_sc[hp], by_head(rows(k_ref, j, hp))) for hp in pairs]
                    + [_nt(d_sc[hp], by_head(rows(v_ref, j, hp))) for hp in pairs])

        def block(j, diag):
            r0 = pl.multiple_of(j * QB, QB)
            half = lambda t, h: t[:, (h % 2) * QB:(h % 2 + 1) * QB]
            z = [half(zd_sc[h // 2], h) for h in heads]
            dw = [half(zd_sc[SB_HEADS // 2 + h // 2], h) for h in heads]
            if not diag:
                ahead = products(j + 1)
                for hp in range(SB_HEADS):
                    zd_sc[hp] = ahead[hp]
            ks = [by_head(rows(k_ref, j, hp)) for hp in pairs]
            spls = [_softplus_logsig(z[h]) for h in heads]
            sp = [jnp.where(causal, spls[h][0], 0.0) if diag else spls[h][0] for h in heads]
            rr = _tri(sp, mp)
            pc = [pg_sc[2 * h] for h in heads]
            w = [jnp.exp(spls[h][1] - (tot_ref[:, h * QB:(h + 1) * QB] - (pc[h] + rr[h][:, :QB]))) for h in heads]
            if diag:
                w = [jnp.where(causal, w[h], 0.0) for h in heads]
            gg = [dw[h] * w[h] for h in heads]
            rg = _tri(gg, mg)
            gc = [pg_sc[2 * h + 1] for h in heads]
            dz = [gg[h] - (gg[h] + gc[h] + rg[h][:, :QB]) * jnp.exp(spls[h][1]) for h in heads]
            if diag:
                dz = [jnp.where(causal, dz[h], 0.0) for h in heads]
            dzb = [dz[h].astype(BF16) for h in heads]
            wb = [w[h].astype(BF16) for h in heads]
            both = lambda t, hp, axis: jnp.concatenate([t[2 * hp], t[2 * hp + 1]], axis=axis)
            dq = [_nn(both(dzb, hp, 1), ks[hp]) for hp in pairs]
            dk = [_tn(both(dzb, hp, 0), qd_sc[hp]) for hp in pairs]
            dv = [_tn(both(wb, hp, 0), qd_sc[SB_HEADS // 2 + hp]) for hp in pairs]
            for h in heads:
                if not diag:
                    pg_sc[2 * h] = pc[h] + rr[h][:, QB:]
                    pg_sc[2 * h + 1] = gc[h] + rg[h][:, QB:]
            for hp in pairs:
                dq_acc[hp] += dq[hp]
                dk_acc[pl.ds(r0, QB), hp * PAIR:(hp + 1) * PAIR] += dk[hp]
                dv_acc[pl.ds(r0, QB), hp * PAIR:(hp + 1) * PAIR] += dv[hp]

        first = products(0)
        for hp in range(SB_HEADS):
            zd_sc[hp] = first[hp]

        def step(t, carry):
            block(t, False)
            return carry
        lax.fori_loop(0, i, step, 0)
        block(i, True)
        for hp in pairs:
            dq_ref[:, hp * PAIR:(hp + 1) * PAIR] = (dq_acc[hp] * SCALE).astype(BF16)

        @pl.when(i == nq - 1)
        def _():
            dk_ref[...] = dk_acc[...].astype(BF16)
            dv_ref[...] = dv_acc[...].astype(BF16)

    qtile = pl.BlockSpec((QB, SB_W), lambda i: (i, 0))
    whole = pl.BlockSpec((T, SB_W), lambda i: (0, 0))
    const = pl.BlockSpec((QB, 2 * QB), lambda i: (0, 0))
    return _call(
        body, name=name, grid=(nq,), out_shape=(S((T, SB_W), BF16),) * 3,
        in_specs=[qtile, pl.BlockSpec((T, SB_W), lambda i: (0, 1)), pl.BlockSpec((T, SB_W), lambda i: (0, 2)), qtile,
                  pl.BlockSpec((QB, SB_HEADS * QB), lambda i: (i, 0)), const, const],
        out_specs=(qtile, whole, whole),
        scratch=[pltpu.VMEM((SB_HEADS // 2, QB, PAIR), BF16), pltpu.VMEM((SB_HEADS // 2, QB, PAIR), BF16),
                 pltpu.VMEM((SB_HEADS, 2 * QB, PAIR), BF16),
                 pltpu.VMEM((2 * SB_HEADS, QB, QB), F32), pltpu.VMEM((SB_HEADS // 2, QB, PAIR), F32),
                 pltpu.VMEM((T, SB_W), F32), pltpu.VMEM((T, SB_W), F32), pltpu.VMEM((SB_HEADS, QB, 2 * QB), F32)],
        sem=("arbitrary",), args=(p, p, p, do, tot, upto, before), riders=riders)


def _t5_buckets():
    a = lax.broadcasted_iota(jnp.int32, (QB, QB), 0)
    c = lax.broadcasted_iota(jnp.int32, (QB, QB), 1)

    def bucket(dist):
        dist = jnp.maximum(dist, 0)
        max_exact = N_BUCKETS // 2
        d = jnp.maximum(dist, 1).astype(F32)
        large = max_exact + (jnp.log(d / max_exact) / math.log(MAX_DISTANCE / max_exact)
                             * (N_BUCKETS - max_exact)).astype(jnp.int32)
        large = jnp.minimum(large, N_BUCKETS - 1)
        return jnp.where(dist < max_exact, dist, large)

    return bucket(QB + a - c), bucket(a - c)


def _swa_common(i, kp_ref, kc_ref, vp_ref, vc_ref, bp_ref, bc_ref, rb_ref, bias_ref):
    lo, lane, row = _half_masks()

    @pl.when(i == 0)
    def _():
        for blk, b_ref in enumerate((bp_ref, bc_ref)):
            bk = b_ref[...]
            for h in range(8):
                acc = jnp.zeros((QB, QB), F32)
                for b in range(N_BUCKETS):
                    acc = jnp.where(bk == b, rb_ref[b, h], acc)
                bias_ref[h, blk] = acc

    band = [(lane > row) & (i > 0), lane <= row]

    def stacks(ref):
        t = ref[...].astype(F32)
        sw = pltpu.roll(t, HEAD_DIM, 1)
        return [jnp.concatenate([jnp.where(lo, t, 0.0), jnp.where(lo, 0.0, sw)], axis=0).astype(BF16),
                jnp.concatenate([jnp.where(lo, sw, 0.0), jnp.where(lo, 0.0, t)], axis=0).astype(BF16)]

    ks = [stacks(kp_ref), stacks(kc_ref)]
    vs = [stacks(vp_ref), stacks(vc_ref)]
    return lo, band, ks, vs


def _lane_half(t, h):
    return t[:, (h % 2) * QB:(h % 2 + 1) * QB]


def swa_fwd(p, sinks, rel_bias, bprev, bcur, name, riders=()):
    T = p.shape[0]
    nq = T // QB
    kcol, vcol = (3 * SB_W + SWA_W) // KV_W, (3 * SB_W + SWA_W) // KV_W + 1

    def body(q_ref, kp_ref, kc_ref, vp_ref, vc_ref, bp_ref, bc_ref, sink_ref, rb_ref, o_ref, lse_ref, bias_ref):
        i = pl.program_id(0)
        lo, band, ks, vs = _swa_common(i, kp_ref, kc_ref, vp_ref, vc_ref, bp_ref, bc_ref, rb_ref, bias_ref)
        heads, pairs, blocks = range(8), range(4), range(2)
        rowmax = lambda t: jnp.max(t, axis=1, keepdims=True)
        rowsum = lambda t: jnp.sum(t, axis=1, keepdims=True)
        q2 = [q_ref[:, g * PAIR:(g + 1) * PAIR] for g in pairs]
        s2 = [[_nt(q2[g], ks[b][g // 2]) for b in blocks] for g in pairs]
        sc = [[jnp.where(band[b], _lane_half(s2[h // 2][b], h) * SCALE + bias_ref[h, b], NEG_INF) for b in blocks] for h in heads]
        sink = [sink_ref[0, h] for h in heads]
        m = [jnp.maximum(jnp.maximum(rowmax(sc[h][0]), rowmax(sc[h][1])), sink[h]) for h in heads]
        e = [[jnp.exp(sc[h][b] - m[h]) for b in blocks] for h in heads]
        den = [rowsum(e[h][0]) + rowsum(e[h][1]) + jnp.exp(sink[h] - m[h]) for h in heads]
        pb = [[(e[h][b] / den[h]).astype(BF16) for b in blocks] for h in heads]
        for g in pairs:
            both = lambda b: jnp.concatenate([pb[2 * g][b], pb[2 * g + 1][b]], axis=1)
            o_ref[:, g * PAIR:(g + 1) * PAIR] = _nn(both(0), vs[0][g // 2]) + _nn(both(1), vs[1][g // 2])
        for h in heads:
            lse_ref[:, h * QB:(h + 1) * QB] = jnp.broadcast_to(m[h] + jnp.log(den[h]), (QB, QB))

    kv = lambda col, prev: pl.BlockSpec((QB, KV_W), (lambda i: (jnp.maximum(i - 1, 0), col)) if prev else (lambda i: (i, col)))
    full = pl.BlockSpec((QB, QB), lambda i: (0, 0))
    smem = pl.BlockSpec(memory_space=pltpu.SMEM)
    return _call(
        body, name=name, grid=(nq,), out_shape=(S((T, SWA_W), F32), S((T, 8 * QB), F32)),
        in_specs=[pl.BlockSpec((QB, SWA_W), lambda i: (i, 3)), kv(kcol, True), kv(kcol, False), kv(vcol, True), kv(vcol, False),
                  full, full, smem, smem],
        out_specs=(pl.BlockSpec((QB, SWA_W), lambda i: (i, 0)), pl.BlockSpec((QB, 8 * QB), lambda i: (i, 0))),
        scratch=[pltpu.VMEM((8, 2, QB, QB), F32)],
        sem=("arbitrary",), args=(p, p, p, p, p, bprev, bcur, sinks, rel_bias), riders=riders)


def swa_bwd(p, do, lse, sinks, rel_bias, bprev, bcur, name, riders=()):
    T = p.shape[0]
    nq = T // QB
    kcol, vcol = (3 * SB_W + SWA_W) // KV_W, (3 * SB_W + SWA_W) // KV_W + 1

    def body(q_ref, kp_ref, kc_ref, vp_ref, vc_ref, do_ref, lse_ref, bp_ref, bc_ref, sink_ref, rb_ref,
             dq_ref, dk_ref, dv_ref, dsink_ref, dsc_ref, bias_ref, dk_acc, dv_acc):
        i = pl.program_id(0)
        lo, band, ks, vs = _swa_common(i, kp_ref, kc_ref, vp_ref, vc_ref, bp_ref, bc_ref, rb_ref, bias_ref)

        @pl.when(i == 0)
        def _():
            dk_acc[...] = jnp.zeros_like(dk_acc)
            dv_acc[...] = jnp.zeros_like(dv_acc)
            dsc_ref[...] = jnp.zeros_like(dsc_ref)
            dsink_ref[...] = jnp.zeros_like(dsink_ref)

        heads, pairs, blocks = range(8), range(4), range(2)
        rowsum = lambda t: jnp.sum(t, axis=1, keepdims=True)
        by_head = lambda t: jnp.concatenate([jnp.where(lo, t, 0), jnp.where(lo, 0, t)], axis=0)
        q2 = [q_ref[:, g * PAIR:(g + 1) * PAIR] for g in pairs]
        d2 = [do_ref[:, g * PAIR:(g + 1) * PAIR].astype(BF16) for g in pairs]
        qs = [by_head(q2[g]) for g in pairs]
        dos = [by_head(d2[g]) for g in pairs]
        s2 = [[_nt(q2[g], ks[b][g // 2]) for b in blocks] for g in pairs]
        dp2 = [[_nt(d2[g], vs[b][g // 2]) for b in blocks] for g in pairs]
        lse_h = [lse_ref[:, h * QB:(h + 1) * QB] for h in heads]
        sink = [sink_ref[0, h] for h in heads]
        pr = [[jnp.exp(jnp.where(band[b], _lane_half(s2[h // 2][b], h) * SCALE + bias_ref[h, b], NEG_INF) - lse_h[h])
               for b in blocks] for h in heads]
        dp = [[_lane_half(dp2[h // 2][b], h) for b in blocks] for h in heads]
        delta = [rowsum(pr[h][0] * dp[h][0]) + rowsum(pr[h][1] * dp[h][1]) for h in heads]
        lane1 = lax.broadcasted_iota(jnp.int32, (1, QB), 1)
        dsink = jnp.zeros((1, QB), F32)
        for h in heads:
            dsink = dsink + jnp.where(lane1 == h, -jnp.sum(jnp.exp(sink[h] - lse_h[h][:, :1]) * delta[h]), 0.0)
        dsink_ref[...] += dsink
        dsc = [[pr[h][b] * (dp[h][b] - delta[h]) for b in blocks] for h in heads]
        for h in heads:
            for b in blocks:
                dsc_ref[h, b] += dsc[h][b]
        dzb = [[(dsc[h][b] * SCALE).astype(BF16) for b in blocks] for h in heads]
        prb = [[pr[h][b].astype(BF16) for b in blocks] for h in heads]
        pair_of = lambda t, g, b, axis: jnp.concatenate([t[2 * g][b], t[2 * g + 1][b]], axis=axis)
        for g in pairs:
            dq = _nn(pair_of(dzb, g, 0, 1), ks[0][g // 2]) + _nn(pair_of(dzb, g, 1, 1), ks[1][g // 2])
            dq_ref[:, g * PAIR:(g + 1) * PAIR] = dq.astype(BF16)

        def key_grad(t, other, b):
            per_kv = [_tn(pair_of(t, 2 * kh, b, 0), other[2 * kh]) + _tn(pair_of(t, 2 * kh + 1, b, 0), other[2 * kh + 1]) for kh in range(2)]
            both = [s + pltpu.roll(s, HEAD_DIM, 1) for s in per_kv]
            return jnp.where(lo, both[0], both[1])

        rp = pl.multiple_of(jnp.maximum(i - 1, 0) * QB, QB)
        rc = pl.multiple_of(i * QB, QB)
        dk_acc[pl.ds(rp, QB), :] += key_grad(dzb, qs, 0)
        dv_acc[pl.ds(rp, QB), :] += key_grad(prb, dos, 0)
        dk_acc[pl.ds(rc, QB), :] += key_grad(dzb, qs, 1)
        dv_acc[pl.ds(rc, QB), :] += key_grad(prb, dos, 1)

        @pl.when(i == nq - 1)
        def _():
            dk_ref[...] = dk_acc[...].astype(BF16)
            dv_ref[...] = dv_acc[...].astype(BF16)

    kv = lambda col, prev: pl.BlockSpec((QB, KV_W), (lambda i: (jnp.maximum(i - 1, 0), col)) if prev else (lambda i: (i, col)))
    full = pl.BlockSpec((QB, QB), lambda i: (0, 0))
    smem = pl.BlockSpec(memory_space=pltpu.SMEM)
    whole = lambda shape: pl.BlockSpec(shape, lambda i: (0,) * len(shape))
    return _call(
        body, name=name, grid=(nq,),
        out_shape=(S((T, SWA_W), BF16), S((T, KV_W), BF16), S((T, KV_W), BF16), S((1, QB), F32), S((8, 2, QB, QB), F32)),
        in_specs=[pl.BlockSpec((QB, SWA_W), lambda i: (i, 3)), kv(kcol, True), kv(kcol, False), kv(vcol, True), kv(vcol, False),
                  pl.BlockSpec((QB, SWA_W), lambda i: (i, 0)), pl.BlockSpec((QB, 8 * QB), lambda i: (i, 0)),
                  full, full, smem, smem],
        out_specs=(pl.BlockSpec((QB, SWA_W), lambda i: (i, 0)), whole((T, KV_W)), whole((T, KV_W)), whole((1, QB)),
                   whole((8, 2, QB, QB))),
        scratch=[pltpu.VMEM((8, 2, QB, QB), F32), pltpu.VMEM((T, KV_W), F32), pltpu.VMEM((T, KV_W), F32)],
        sem=("arbitrary",), args=(p, p, p, p, p, do, lse, bprev, bcur, sinks, rel_bias), riders=riders)


def mix_out_fwd(o_sb, o_sw, g_sb, g_sw, wout, h, g_next, name, riders=()):
    T, D = h.shape
    M = SB_W + SWA_W
    tm = _tile(T, 256)

    def body(a_ref, b_ref, ga_ref, gb_ref, w_ref, h_ref, gn_ref, mx_ref, o_ref, n_ref):
        mx_ref[:, :SB_W] = _rms(a_ref[...], ga_ref[...]).astype(BF16)
        mx_ref[:, SB_W:] = _rms(b_ref[...], gb_ref[...]).astype(BF16)
        out = h_ref[...] + _nn(mx_ref[...], w_ref[...])
        o_ref[...] = out
        n_ref[...] = _rms(out, gn_ref[...]).astype(BF16)

    row = lambda n: pl.BlockSpec((tm, n), lambda i: (i, 0))
    vec = lambda n: pl.BlockSpec((1, n), lambda i: (0, 0))
    return _call(
        body, name=name, grid=(T // tm,), out_shape=(S((T, M), BF16), S((T, D), F32), S((T, D), BF16)),
        in_specs=[row(SB_W), row(SWA_W), vec(SB_W), vec(SWA_W), pl.BlockSpec((M, D), lambda i: (0, 0)), row(D), vec(D)],
        out_specs=(row(M), row(D), row(D)),
        sem=("parallel",), args=(o_sb, o_sw, g_sb, g_sw, wout, h, g_next), riders=riders)


def loss_head(h, g, target, name):
    T, D = h.shape
    tm = _tile(T, 256)

    def body(h_ref, g_ref, t_ref, loss_ref, dh_ref, dhb_ref, dg_ref):
        @pl.when(pl.program_id(0) == 0)
        def _():
            loss_ref[...] = jnp.zeros_like(loss_ref)
            dg_ref[...] = jnp.zeros_like(dg_ref)
        x = h_ref[...]
        err = _rms(x, g_ref[...]) - t_ref[...]
        loss_ref[...] += jnp.full((1, QB), 0.5 * jnp.sum(jnp.mean(err * err, axis=-1)), F32)
        dx, dg = _rms_bwd(err / D, x, g_ref[...])
        dh_ref[...] = dx
        dhb_ref[...] = dx.astype(BF16)
        dg_ref[...] += dg

    row = pl.BlockSpec((tm, D), lambda i: (i, 0))
    vec = pl.BlockSpec((1, D), lambda i: (0, 0))
    return pl.pallas_call(
        body, name=name, grid=(T // tm,), out_shape=(S((1, QB), F32), S((T, D), F32), S((T, D), BF16), S((1, D), F32)),
        in_specs=[row, vec, row], out_specs=(pl.BlockSpec((1, QB), lambda i: (0, 0)), row, row, vec),
        compiler_params=_params(("arbitrary",)),
    )(h, g, target)


def ffn_down_bwd(dhb, wd, gate, up, a, n, name, riders=()):
    T, D = dhb.shape
    F = wd.shape[0]
    tr, tn = _tile(T, 512), _tile(F, 256)

    def body(d_ref, n_ref, w_ref, g_ref, u_ref, a_ref, o_ref, dwd_ref, dwdb_ref, dwgu_ref, dwgub_ref):
        w = w_ref[...]
        for r in range(T // tr):
            rows = slice(r * tr, (r + 1) * tr)
            da = 0.5 * _nt(d_ref[rows, :], w)
            o_ref[0, rows, :] = (da * g_ref[rows, :].astype(F32)).astype(BF16)
            o_ref[1, rows, :] = (da * u_ref[rows, :].astype(F32)).astype(BF16)
        dwd = 0.5 * _tn(a_ref[...], d_ref[...])
        dwd_ref[...] = dwd
        dwdb_ref[...] = dwd.astype(BF16)
        for s in range(2):
            dwgu = _tn(o_ref[s], n_ref[...])
            dwgu_ref[s] = dwgu
            dwgub_ref[s] = dwgu.astype(BF16)

    tile = pl.BlockSpec((T, tn), lambda j: (0, j))
    whole = pl.BlockSpec((T, D), lambda j: (0, 0))
    rows1, rows2 = pl.BlockSpec((tn, D), lambda j: (j, 0)), pl.BlockSpec((2, tn, D), lambda j: (0, j, 0))
    return _call(
        body, name=name, grid=(F // tn,),
        out_shape=(S((2, T, F), BF16), S((F, D), F32), S((F, D), BF16), S((2, F, D), F32), S((2, F, D), BF16)),
        in_specs=[whole, whole, rows1, tile, tile, tile],
        out_specs=(pl.BlockSpec((2, T, tn), lambda j: (0, 0, j)), rows1, rows1, rows2, rows2),
        sem=("parallel",), args=(dhb, n, wd, gate, up, a), riders=riders)


def tn_matmul(xs, y, alpha, name, riders=()):
    B, T, N = xs.shape
    D = y.shape[1]
    tn = _tile(N, 256)

    def body(x_ref, y_ref, o_ref, ob_ref):
        o = alpha * _tn(x_ref[...], y_ref[...])
        o_ref[...] = o
        ob_ref[...] = o.astype(BF16)

    tile = pl.BlockSpec((None, tn, D), lambda s, j: (s, j, 0))
    return _call(
        body, name=name, grid=(B, N // tn), out_shape=(S((B, N, D), F32), S((B, N, D), BF16)),
        in_specs=[pl.BlockSpec((None, T, tn), lambda s, j: (s, 0, j)), pl.BlockSpec((T, D), lambda s, j: (0, 0))],
        out_specs=(tile, tile), sem=("parallel", "parallel"), args=(xs, y), riders=riders)


def nn_rms_bwd(xs, ws, h_in, g, dh, name, riders=()):
    B, T, K = xs.shape
    D = ws.shape[2]
    tm = _tile(T, 256)

    def body(x_ref, w_ref, h_ref, g_ref, d_ref, o_ref, ob_ref, dg_ref):
        @pl.when(pl.program_id(0) == 0)
        def _():
            dg_ref[...] = jnp.zeros_like(dg_ref)
        dn = _nn(x_ref[0], w_ref[0])
        for s in range(1, B):
            dn = dn + _nn(x_ref[s], w_ref[s])
        dx, dg = _rms_bwd(dn, h_ref[...], g_ref[...])
        out = d_ref[...] + dx
        o_ref[...] = out
        ob_ref[...] = out.astype(BF16)
        dg_ref[...] += dg

    row = pl.BlockSpec((tm, D), lambda i: (i, 0))
    vec = pl.BlockSpec((1, D), lambda i: (0, 0))
    return _call(
        body, name=name, grid=(T // tm,), out_shape=(S((T, D), F32), S((T, D), BF16), S((1, D), F32)),
        in_specs=[pl.BlockSpec((B, tm, K), lambda i: (0, i, 0)), pl.BlockSpec((B, K, D), lambda i: (0, 0, 0)), row, vec, row],
        out_specs=(row, row, vec),
        sem=("arbitrary",), args=(xs, ws, h_in, g, dh), riders=riders)


def mix_out_bwd(dhb, wout, o_sb, o_sw, g_sb, g_sw, name):
    T, D = dhb.shape
    tm = _tile(T, 256)

    def body(d_ref, w_ref, a_ref, b_ref, ga_ref, gb_ref, da_ref, db_ref, dga_ref, dgb_ref):
        @pl.when(pl.program_id(0) == 0)
        def _():
            dga_ref[...] = jnp.zeros_like(dga_ref)
            dgb_ref[...] = jnp.zeros_like(dgb_ref)
        dm = _nt(d_ref[...], w_ref[...])
        dxa, dga = _rms_bwd(dm[:, :SB_W], a_ref[...], ga_ref[...])
        dxb, dgb = _rms_bwd(dm[:, SB_W:], b_ref[...], gb_ref[...])
        da_ref[...] = dxa
        db_ref[...] = dxb
        dga_ref[...] += dga
        dgb_ref[...] += dgb

    row = lambda n: pl.BlockSpec((tm, n), lambda i: (i, 0))
    vec = lambda n: pl.BlockSpec((1, n), lambda i: (0, 0))
    return pl.pallas_call(
        body, name=name, grid=(T // tm,),
        out_shape=(S((T, SB_W), F32), S((T, SWA_W), F32), S((1, SB_W), F32), S((1, SWA_W), F32)),
        in_specs=[row(D), pl.BlockSpec((SB_W + SWA_W, D), lambda i: (0, 0)), row(SB_W), row(SWA_W), vec(SB_W), vec(SWA_W)],
        out_specs=(row(SB_W), row(SWA_W), vec(SB_W), vec(SWA_W)),
        compiler_params=_params(("arbitrary",)),
    )(dhb, wout, o_sb, o_sw, g_sb, g_sw)


def rel_bias_grad(dscs, bprev, bcur, name):
    n = len(dscs)

    def body(*refs):
        bp_ref, bc_ref, o_ref = refs[n], refs[n + 1], refs[n + 2]
        bks = [bp_ref[...], bc_ref[...]]
        row = lax.broadcasted_iota(jnp.int32, (N_BUCKETS, QB), 0)
        lane = lax.broadcasted_iota(jnp.int32, (N_BUCKETS, QB), 1)
        out = jnp.zeros((N_BUCKETS, QB), F32)
        for h in range(8):
            tot = [sum(refs[l][h, b] for l in range(n)) for b in range(2)]
            for b in range(N_BUCKETS):
                val = jnp.sum(jnp.where(bks[0] == b, tot[0], 0.0)) + jnp.sum(jnp.where(bks[1] == b, tot[1], 0.0))
                out = jnp.where((row == b) & (lane == h), val, out)
        o_ref[...] = out

    return pl.pallas_call(body, name=name, out_shape=S((N_BUCKETS, QB), F32), compiler_params=_params())(*dscs, bprev, bcur)


def _adamw(w, g, m, v):
    m = ADAM_B1 * m + (1.0 - ADAM_B1) * g
    v = ADAM_B2 * v + (1.0 - ADAM_B2) * (g * g)
    m_hat = m / (1.0 - ADAM_B1 ** ADAM_STEP)
    v_hat = v / (1.0 - ADAM_B2 ** ADAM_STEP)
    delta = -ADAM_LR * (m_hat / (jnp.sqrt(v_hat) + ADAM_EPS) + ADAM_WD * w)
    return delta, m, v


def adamw_scattered(w, m, v, owns, others, name, riders=()):
    L, R, C = w.shape
    tr = _rows_tile(R, 176)

    def body(w_ref, m_ref, v_ref, *rest):
        own_refs, other_refs = rest[:L], rest[L:2 * L]
        g_ref, d_ref, mo_ref, vo_ref = rest[2 * L:]
        layer = pl.program_id(0)

        def grad(k):
            o = other_refs[k]
            return own_refs[k][...] + o[0].astype(F32) + o[1].astype(F32) + o[2].astype(F32)

        g = grad(0)
        for k in range(1, L):
            g = jnp.where(layer == k, grad(k), g)
        d, mn, vn = _adamw(w_ref[...], g, m_ref[...], v_ref[...])
        g_ref[...] = g
        d_ref[...] = d
        mo_ref[...] = mn
        vo_ref[...] = vn

    tile = pl.BlockSpec((None, tr, C), lambda l, i: (l, i, 0))
    return _call(
        body, name=name, grid=(L, R // tr), out_shape=(S((L, R, C), F32),) * 4,
        in_specs=[tile] * 3 + [pl.BlockSpec((tr, C), lambda l, i: (i, 0))] * L + [pl.BlockSpec((3, tr, C), lambda l, i: (0, i, 0))] * L,
        out_specs=(tile,) * 4, sem=("parallel", "parallel"), args=(w, m, v, *owns, *others), riders=riders)


def adamw_small(w, gs, m, v, name):
    R, C = w.shape

    def body(w_ref, g_ref, m_ref, v_ref, go_ref, d_ref, mo_ref, vo_ref):
        g = g_ref[0]
        for k in range(1, N_DEV):
            g = g + g_ref[k]
        d, mn, vn = _adamw(w_ref[...], g, m_ref[...], v_ref[...])
        go_ref[...] = g
        d_ref[...] = d
        mo_ref[...] = mn
        vo_ref[...] = vn

    return pl.pallas_call(body, name=name, out_shape=(S((R, C), F32),) * 4, compiler_params=_params())(w, gs, m, v)


def kernel(x, norm_ffn1, w_ffn1_gu, w_ffn1_down, norm_mix, w_in, sinks, norm_out_sb, norm_out_swa, w_out, norm_ffn2, w_ffn2_gu, w_ffn2_down, rel_bias, norm_final, loss_target, m_norm_ffn1, m_w_ffn1_gu, m_w_ffn1_down, m_norm_mix, m_w_in, m_sinks, m_norm_out_sb, m_norm_out_swa, m_w_out, m_norm_ffn2, m_w_ffn2_gu, m_w_ffn2_down, m_rel_bias, m_norm_final, v_norm_ffn1, v_w_ffn1_gu, v_w_ffn1_down, v_norm_mix, v_w_in, v_sinks, v_norm_out_sb, v_norm_out_swa, v_w_out, v_norm_ffn2, v_w_ffn2_gu, v_w_ffn2_down, v_rel_bias, v_norm_final):
    L = norm_ffn1.shape[0]
    T, D = x.shape[1], x.shape[2]
    F = w_ffn1_down.shape[1] * N_DEV
    h = x.reshape(T, D)
    target = loss_target.reshape(T, D)
    after, upto, before = _tri_consts()
    bprev, bcur = _t5_buckets()

    local = {}
    for l in range(L):
        local[f"gu1_{l}"] = w_ffn1_gu[l].T.astype(BF16)
        local[f"d1_{l}"] = w_ffn1_down[l].astype(BF16)
        local[f"in_{l}"] = w_in[l].T.astype(BF16)
        local[f"out_{l}"] = w_out[l].astype(BF16)
        local[f"gu2_{l}"] = w_ffn2_gu[l].T.astype(BF16)
        local[f"d2_{l}"] = w_ffn2_down[l].astype(BF16)
    full, partial = {}, {}
    grads, chip_sum, recv_b = {}, {}, {}

    def run(fn, *args, ag=(), rs1=(), rs2=()):
        halves = lambda names: [n if isinstance(n, tuple) else (n, None) for n in names]
        ag, rs2 = [(n, k) for n, k in halves(ag) if n in local], halves(rs2)
        rows = lambda k, total: None if k is None else (k * (total // 2), total // 2)

        def second(n, k):
            sb = chip_sum[n][1]
            return scatter_second(sb, rows(k, sb.shape[1]), recv_b.get(n))

        riders = ([gather(local[n], rows(k, local[n].shape[0]), partial.get(n)) for n, k in ag]
                  + [scatter_first(grads[n][1]) for n in rs1] + [second(n, k) for n, k in rs2])
        if not riders:
            return fn(*args)
        outs, per = fn(*args, riders=riders)
        per = [p[0] for p in per]
        for n, k in ag:
            buf = per.pop(0)
            if k == 0:
                partial[n] = buf
            else:
                full[n] = buf.reshape(N_DEV * buf.shape[1], D)
        for n in rs1:
            chip_sum[n] = scatter_add(grads[n][0], per.pop(0), f"rs_add_{n}")
        for n, _ in rs2:
            recv_b[n] = per.pop(0)
        return outs

    gu = lambda n: full[n].reshape(2, F, D)
    slots = lambda pair: tuple(t.reshape(N_DEV, -1, D) for t in pair)
    vec = lambda a: a.reshape(1, -1)

    PW = max(D, SB_W + SWA_W)
    n_rows = 4 * L + 2
    n_rows += (-n_rows) % 8

    def pack(ffn1, mix, ffn2, final, osb, osw, snk, rel, extra):
        pieces = []

        def row(*parts):
            flat = [a.reshape(-1) for a in parts]
            pieces.extend(flat)
            used = sum(a.size for a in flat)
            if used < PW:
                pieces.append(jnp.zeros((PW - used,), F32))

        for group in (ffn1, mix, ffn2):
            for l in range(L):
                row(group[l])
        row(final)
        for l in range(L):
            row(osb[l], osw[l])
        row(*[snk[l].reshape(-1)[:8] for l in range(L)], rel, extra)
        pieces.append(jnp.zeros(((n_rows - 4 * L - 2) * PW,), F32))
        return jnp.concatenate(pieces).reshape(n_rows, PW)

    def unpack(arr):
        ffn1, mix, ffn2 = arr[0:L, :D], arr[L:2 * L, :D], arr[2 * L:3 * L, :D]
        final = arr[3 * L, :D]
        ob = arr[3 * L + 1:4 * L + 1]
        tail = arr[4 * L + 1]
        return (ffn1, mix, tail[:8 * L].reshape(L, 8), ob[:, :SB_W], ob[:, SB_W:SB_W + SWA_W], ffn2,
                tail[8 * L:8 * L + N_BUCKETS * 8].reshape(N_BUCKETS, 8), final)

    zero = jnp.zeros((1,), F32)
    w_small = pack(norm_ffn1, norm_mix, norm_ffn2, norm_final, norm_out_sb, norm_out_swa, sinks, rel_bias, zero)
    norm_ffn1, norm_mix, sinks, norm_out_sb, norm_out_swa, norm_ffn2, _, norm_final = unpack(w_small)

    saved = []
    n_next = run(rms_cast, h, vec(norm_ffn1[0]), "rms_first", ag=("gu1_0",))
    for l in range(L):
        nx = l + 1
        s = {"h0": h, "n1": n_next}
        s["gate1"], s["up1"], s["a1"] = run(ffn_up_fwd, s["n1"], gu(f"gu1_{l}"), f"ffn1_up{l}",
                                            ag=(f"d1_{l}", ("in_0", 0) if l == 0 else (f"in_{l}", 1)))
        h = run(ffn_down_fwd, s["a1"], full[f"d1_{l}"], h, None, f"ffn1_down{l}", ag=(("in_0", 1),) if l == 0 else ())
        s["h1"] = h
        s["n2"], s["p"] = mix_in_fwd(h, vec(norm_mix[l]), full[f"in_{l}"], f"mix_in{l}")
        s["o_sb"], s["tot"] = run(sb_attn_fwd, s["p"], after, f"sb_fwd{l}", ag=(f"out_{l}", f"gu2_{l}", f"d2_{l}"))
        s["o_sw"], s["lse"] = run(swa_fwd, s["p"], vec(sinks[l]), rel_bias, bprev, bcur, f"swa_fwd{l}", ag=((f"gu1_{nx}", 0),))
        s["mixed"], h, s["n3"] = run(mix_out_fwd, s["o_sb"], s["o_sw"], vec(norm_out_sb[l]), vec(norm_out_swa[l]),
                                     full[f"out_{l}"], h, vec(norm_ffn2[l]), f"mix_out{l}")
        s["h2"] = h
        s["gate2"], s["up2"], s["a2"] = run(ffn_up_fwd, s["n3"], gu(f"gu2_{l}"), f"ffn2_up{l}",
                                            ag=((f"gu1_{nx}", 1), (f"in_{nx}", 0)))
        if nx < L:
            h, n_next = run(ffn_down_fwd, s["a2"], full[f"d2_{l}"], h, vec(norm_ffn1[nx]), f"ffn2_down{l}")
        else:
            h = run(ffn_down_fwd, s["a2"], full[f"d2_{l}"], h, None, f"ffn2_down{l}")
        saved.append(s)

    loss_part, dh, dhb, dg_final = loss_head(h, vec(norm_final), target, "loss_head")

    small = {k: [None] * L for k in ("ffn1", "mix", "sinks", "osb", "osw", "ffn2", "dsc")}
    for l in reversed(range(L)):
        s = saved[l]

        def ffn_bwd(dh, dhb, tag, gate, up, a, n, h_in, g, r_down, r_up):
            gu_n, d_n = f"gu{tag}_{l}", f"d{tag}_{l}"
            dgu, dwd, dwdb, dwgu, dwgub = run(ffn_down_bwd, dhb, full[d_n], gate, up, a, n, f"ffn{tag}_down_bwd{l}", **r_down)
            grads[gu_n], grads[d_n] = slots((dwgu, dwgub)), slots((dwd, dwdb))
            return run(nn_rms_bwd, dgu, gu(gu_n), h_in, g, dh, f"ffn{tag}_up_bwd{l}", **r_up)

        later = l + 1 < L
        dh, dhb, small["ffn2"][l] = ffn_bwd(dh, dhb, 2, s["gate2"], s["up2"], s["a2"], s["n3"], s["h2"], vec(norm_ffn2[l]),
                                            dict(rs2=(f"gu1_{l + 1}", f"d1_{l + 1}") if later else ()),
                                            dict(rs1=(f"gu2_{l}", f"d2_{l}")))
        do_sb, do_sw, small["osb"][l], small["osw"][l] = mix_out_bwd(
            dhb, full[f"out_{l}"], s["o_sb"], s["o_sw"], vec(norm_out_sb[l]), vec(norm_out_swa[l]), f"mix_out_bwd{l}")
        grads[f"out_{l}"] = slots(tn_matmul(s["mixed"][None], dhb, 1.0, f"dwout{l}"))
        dq_sb, dk_sb, dv_sb = run(sb_attn_bwd, s["p"], do_sb, s["tot"], upto, before, f"sb_bwd{l}",
                                  rs2=(f"gu2_{l}", f"d2_{l}"), rs1=(f"out_{l}",))
        dq_sw, dk_sw, dv_sw, small["sinks"][l], small["dsc"][l] = swa_bwd(
            s["p"], do_sw, s["lse"], vec(sinks[l]), rel_bias, bprev, bcur, f"swa_bwd{l}")
        dp = jnp.concatenate([dq_sb, dk_sb, dv_sb, dq_sw, dk_sw, dv_sw], axis=1)
        dh, dhb, small["mix"][l] = nn_rms_bwd(dp[None], full[f"in_{l}"][None], s["h1"], vec(norm_mix[l]), dh, f"mix_in_bwd{l}")
        grads[f"in_{l}"] = slots(tn_matmul(dp[None], s["n2"], 1.0, f"dwin{l}"))
        dh, dhb, small["ffn1"][l] = ffn_bwd(dh, dhb, 1, s["gate1"], s["up1"], s["a1"], s["n1"], s["h0"], vec(norm_ffn1[l]),
                                            dict(rs1=(f"in_{l}",), rs2=(f"out_{l}",)),
                                            dict(rs1=(f"gu1_{l}", f"d1_{l}"), rs2=(f"in_{l}",)))

    grad_x = dh.reshape(x.shape)

    upd = {}
    for nm, w, m, v, transposed, last in (
            ("gu2", w_ffn2_gu, m_w_ffn2_gu, v_w_ffn2_gu, True, ("gu1_0", "d1_0")), ("d2", w_ffn2_down, m_w_ffn2_down, v_w_ffn2_down, False, ()),
            ("in", w_in, m_w_in, v_w_in, True, ()), ("out", w_out, m_w_out, v_w_out, False, ()),
            ("gu1", w_ffn1_gu, m_w_ffn1_gu, v_w_ffn1_gu, True, ()), ("d1", w_ffn1_down, m_w_ffn1_down, v_w_ffn1_down, False, ())):
        turn = (lambda a: jnp.swapaxes(a, 1, 2)) if transposed else (lambda a: a)
        names = [f"{nm}_{l}" for l in range(L)]
        res = run(adamw_scattered, turn(w), turn(m), turn(v), [chip_sum[n][0] for n in names], [recv_b[n] for n in names],
                  f"adamw_{nm}", rs2=last)
        upd[nm] = tuple(turn(r) for r in res)

    d_rel = rel_bias_grad(small["dsc"], bprev, bcur, "rel_bias_grad")[:, :8]
    g_small = pack(small["ffn1"], small["mix"], small["ffn2"], dg_final, small["osb"], small["osw"], small["sinks"], d_rel,
                   loss_part[0, :1])
    m_small = pack(m_norm_ffn1, m_norm_mix, m_norm_ffn2, m_norm_final, m_norm_out_sb, m_norm_out_swa, m_sinks, m_rel_bias, zero)
    v_small = pack(v_norm_ffn1, v_norm_mix, v_norm_ffn2, v_norm_final, v_norm_out_sb, v_norm_out_swa, v_sinks, v_rel_bias, zero)
    gs_small = all_gather_rows(g_small, "ag_small")
    summed = adamw_small(w_small, gs_small, m_small, v_small, "adamw_small")
    small_out = [unpack(a) for a in summed]
    loss = summed[0][4 * L + 1, 8 * L + N_BUCKETS * 8]

    def group(k):
        sm = small_out[k]
        return (sm[0], upd["gu1"][k], upd["d1"][k], sm[1], upd["in"][k], sm[2], sm[3], sm[4], upd["out"][k], sm[5],
                upd["gu2"][k], upd["d2"][k], sm[6], sm[7])

    return (loss, grad_x, *group(0), *group(1), *group(2), *group(3))
```

```python
import math

import jax
import jax.numpy as jnp
from jax import lax
from jax.experimental import pallas as pl
from jax.experimental.pallas import tpu as pltpu

F32 = jnp.float32
BF16 = jnp.bfloat16
S = jax.ShapeDtypeStruct

N_DEV = 8
HEAD_DIM = 64
SB_HEADS = 8
PAIR = 2 * HEAD_DIM
SB_W = 512
SWA_W = 512
KV_W = 128
IN_W = 3 * SB_W + SWA_W + 2 * KV_W
QB = 128
N_BUCKETS = 32
MAX_DISTANCE = 128
EPS = 1e-6
NEG_INF = -1e30
SCALE = HEAD_DIM ** -0.5

ADAM_LR = 0.001
ADAM_B1 = 0.9
ADAM_B2 = 0.999
ADAM_EPS = 1e-08
ADAM_WD = 0.01
ADAM_STEP = 10

VMEM_LIMIT = 56 * 1024 * 1024
MESH = pl.DeviceIdType.MESH


def _params(sem=None, vmem=VMEM_LIMIT):
    return pltpu.CompilerParams(dimension_semantics=sem, vmem_limit_bytes=vmem)


def _nn(a, b):
    return jnp.dot(a, b, preferred_element_type=F32)


def _nt(a, b):
    return lax.dot_general(a, b, (((1,), (1,)), ((), ())), preferred_element_type=F32)


def _tn(a, b):
    return lax.dot_general(a, b, (((0,), (0,)), ((), ())), preferred_element_type=F32)


def _tri(xs, m):
    return [_nn(x.astype(BF16), m) for x in xs]


def _rms(x, g):
    r = lax.rsqrt(jnp.mean(x * x, axis=-1, keepdims=True) + EPS)
    return x * r * g


def _rms_bwd(dy, x, g):
    r = lax.rsqrt(jnp.mean(x * x, axis=-1, keepdims=True) + EPS)
    xhat = x * r
    u = dy * g
    dx = r * (u - xhat * jnp.mean(u * xhat, axis=-1, keepdims=True))
    return dx, jnp.sum(dy * xhat, axis=0, keepdims=True)


def _softplus_logsig(z):
    sp = jnp.maximum(z, 0.0) + jnp.log(1.0 + jnp.exp(-jnp.abs(z)))
    return sp, z - sp


def _tile(n, want):
    t = min(n, want)
    while n % t:
        t //= 2
    return t


def _place():
    x, y, c = lax.axis_index("x"), lax.axis_index("y"), lax.axis_index("c")
    chips = [(1 - x, y), (x, 1 - y), (1 - x, 1 - y)]
    return x, y, c, chips


def all_gather_rows(v, name):
    R, C = v.shape

    def body(v_ref, out_ref, send_sems, recv_sems, local_sem):
        x, y, c, chips = _place()
        me, sibling = (x, y, c), (x, y, 1 - c)

        def slot(px, py, pc):
            return out_ref.at[4 * px + 2 * py + pc]

        def copy(k, block, to, src=None):
            return pltpu.make_async_remote_copy(
                src_ref=slot(*block) if src is None else src, dst_ref=slot(*block),
                send_sem=send_sems.at[k], recv_sem=recv_sems.at[k], device_id=to, device_id_type=MESH)

        mine = pltpu.make_async_copy(v_ref, slot(*me), local_sem)
        mine.start()
        first = [copy(0, me, sibling, src=v_ref)]
        first += [copy(1 + j, me, (*chip, c), src=v_ref) for j, chip in enumerate(chips)]
        for cp in first:
            cp.start()
        passed = [copy(4 + j, (*chip, c), sibling) for j, chip in enumerate(chips)]
        for j, chip in enumerate(chips):
            copy(1 + j, (*chip, c), me).wait_recv()
            passed[j].start()
        copy(0, sibling, me).wait_recv()
        for j, chip in enumerate(chips):
            copy(4 + j, (*chip, 1 - c), me).wait_recv()
        for cp in first + passed:
            cp.wait_send()
        mine.wait()

    return pl.pallas_call(
        body, name=name, out_shape=S((N_DEV, R, C), v.dtype),
        in_specs=[pl.BlockSpec(memory_space=pl.ANY)], out_specs=pl.BlockSpec(memory_space=pl.ANY),
        scratch_shapes=[pltpu.SemaphoreType.DMA((7,)), pltpu.SemaphoreType.DMA((7,)), pltpu.SemaphoreType.DMA],
    )(v)


class _Exchange:
    def __init__(self, ins, outs, sizes, n_local, plan, aliases=None):
        self.ins, self.outs, self.plan, self.aliases = list(ins), list(outs), plan, aliases or {}
        self.sizes, self.n_local = list(sizes), n_local

    def scratch(self):
        n = sum(self.sizes)
        return [pltpu.SemaphoreType.DMA((n,)), pltpu.SemaphoreType.DMA((n,)), pltpu.SemaphoreType.DMA((max(self.n_local, 1),))]

    def _copies(self, in_refs, out_refs, sems):
        send_sems, recv_sems, local_sems = sems
        phases, local = self.plan(in_refs, out_refs)
        out, k = [], 0
        for phase in phases:
            out.append([pltpu.make_async_remote_copy(src_ref=s, dst_ref=d, send_sem=send_sems.at[k + n], recv_sem=recv_sems.at[k + n],
                                                     device_id=dev, device_id_type=MESH) for n, (s, d, dev) in enumerate(phase)])
            k += len(phase)
        return out, [pltpu.make_async_copy(s, d, local_sems.at[n]) for n, (s, d) in enumerate(local)]

    def start(self, in_refs, out_refs, sems):
        phases, loc = self._copies(in_refs, out_refs, sems)
        for cp in phases[0] + loc:
            cp.start()

    def advance(self, hook, in_refs, out_refs, sems):
        p = hook - (3 - len(self.sizes))
        if p >= 1:
            phases, _ = self._copies(in_refs, out_refs, sems)
            for cp in phases[p - 1]:
                cp.wait_recv()
            for cp in phases[p]:
                cp.start()

    def finish(self, in_refs, out_refs, sems):
        phases, loc = self._copies(in_refs, out_refs, sems)
        for cp in phases[-1]:
            cp.wait_recv()
        for phase in phases:
            for cp in phase:
                cp.wait_send()
        for cp in loc:
            cp.wait()


def gather(v, rows=None, into=None):
    R, C = v.shape
    r0, nr = rows or (0, R)
    na = min(nr, ((nr // 2 + 15) // 16) * 16)

    def plan(ins, outs):
        x, y, c, _ = _place()
        xn, yn, dg, sibling = (1 - x, y), (x, 1 - y), (1 - x, 1 - y), (x, y, 1 - c)
        slot = lambda chip, start=r0, count=nr: outs[0].at[4 * chip[0] + 2 * chip[1] + c, pl.ds(start, count), :]
        src, mine = ins[0].at[pl.ds(r0, nr), :], slot((x, y))
        same = lambda ref, to: (ref, ref, to)
        first = [(src, mine, sibling), (src, mine, (*xn, c)), (src, mine, (*yn, c))]
        relay = [same(slot(xn, r0, na), (*yn, c)), same(slot(yn, r0 + na, nr - na), (*xn, c))]
        onward = [same(slot(xn), sibling), same(slot(yn), sibling), same(slot(dg), sibling)]
        return [first, relay, onward], [(src, mine)]

    if into is None:
        return _Exchange([v], [S((N_DEV, R, C), v.dtype)], (3, 2, 3), 1, plan)
    return _Exchange([v, into], [S((N_DEV, R, C), v.dtype)], (3, 2, 3), 1, plan, aliases={1: 0})


def scatter_first(gb):
    _, R, C = gb.shape

    def plan(ins, outs):
        x, y, c, chips = _place()
        owners = [(x, y)] + chips
        return [[(ins[0].at[4 * px + 2 * py + (1 - c)], outs[0].at[j], (x, y, 1 - c)) for j, (px, py) in enumerate(owners)]], []

    return _Exchange([gb], [S((4, R, C), BF16)], (4,), 0, plan)


def scatter_second(sb, rows=None, into=None):
    r0, nr = rows or (0, sb.shape[1])

    def plan(ins, outs):
        x, y, c, chips = _place()
        part = lambda ref, j: ref.at[j, pl.ds(r0, nr), :]
        return [[(part(ins[0], j), part(outs[0], j), (*chips[j], c)) for j in range(3)]], []

    if into is None:
        return _Exchange([sb], [S(sb.shape, BF16)], (3,), 0, plan)
    return _Exchange([sb, into], [S(sb.shape, BF16)], (3,), 0, plan, aliases={1: 0})


def _call(body, *, name, grid, in_specs, out_specs, out_shape, args, scratch=(), sem=None, riders=(), marks=None):
    single = not isinstance(out_shape, (tuple, list))
    out_shape = (out_shape,) if single else tuple(out_shape)
    out_specs = (out_specs,) if single else tuple(out_specs)
    n_in, n_out, n_sc = len(in_specs), len(out_shape), len(scratch)
    if not riders:
        res = pl.pallas_call(body, name=name, grid=grid, in_specs=list(in_specs), out_specs=out_specs, out_shape=out_shape,
                             scratch_shapes=list(scratch), compiler_params=_params(sem))(*args)
        return res[0] if single else res
    r_ins = [a for r in riders for a in r.ins]
    r_outs = [o for r in riders for o in r.outs]
    r_scr = [s for r in riders for s in r.scratch()]
    aliases, i0, o0 = {}, n_in, n_out
    for r in riders:
        for a, b in r.aliases.items():
            aliases[i0 + a] = o0 + b
        i0, o0 = i0 + len(r.ins), o0 + len(r.outs)
    steps = math.prod(grid)

    def full(*refs):
        ins, rin = refs[:n_in], refs[n_in:n_in + len(r_ins)]
        pos = n_in + len(r_ins)
        outs, rout = refs[pos:pos + n_out], refs[pos + n_out:pos + n_out + len(r_outs)]
        pos += n_out + len(r_outs)
        sc, rsc = refs[pos:pos + n_sc], refs[pos + n_sc:]
        step = 0
        for d, n in enumerate(grid):
            step = step * n + pl.program_id(d)

        def each(method, *lead):
            i, o = 0, 0
            for k, r in enumerate(riders):
                getattr(r, method)(*lead, rin[i:i + len(r.ins)], rout[o:o + len(r.outs)], rsc[3 * k:3 * k + 3])
                i, o = i + len(r.ins), o + len(r.outs)

        @pl.when(step == 0)
        def _():
            each("start")
        body(*ins, *outs, *sc)

        late = max(steps - 1 - max(steps // 8, 1), 0)
        first, second = marks or (min((3 * steps) // 5, late), late)

        @pl.when(step == first)
        def _():
            each("advance", 1)

        @pl.when(step == second)
        def _():
            each("advance", 2)

        @pl.when(step == steps - 1)
        def _():
            each("finish")

    anywhere = pl.BlockSpec(memory_space=pl.ANY)
    res = pl.pallas_call(
        full, name=name, grid=grid, in_specs=list(in_specs) + [anywhere] * len(r_ins),
        out_specs=out_specs + (anywhere,) * len(r_outs), out_shape=out_shape + tuple(r_outs),
        scratch_shapes=list(scratch) + r_scr, input_output_aliases=aliases,
        compiler_params=_params(("arbitrary",) * len(grid)))(*args, *r_ins)
    host, rest, per = res[:n_out], list(res[n_out:]), []
    for r in riders:
        per.append(rest[:len(r.outs)])
        rest = rest[len(r.outs):]
    return (host[0] if single else tuple(host)), per


def _rows_tile(n, cap):
    return max(t for t in range(16, min(n, cap) + 1, 16) if n % t == 0)


def scatter_add(g, ra, name):
    _, R, C = g.shape
    tr = _rows_tile(R, 176)
    x, y, c, chips = _place()
    slots = jnp.stack([4 * px + 2 * py + c for px, py in [(x, y)] + chips]).astype(jnp.int32)

    def body(s_ref, g0, g1, g2, g3, ra_ref, own_ref, sb_ref):
        own_ref[...] = g0[...] + ra_ref[0].astype(F32)
        for j, gj in enumerate((g1, g2, g3)):
            sb_ref[j] = (gj[...] + ra_ref[j + 1].astype(F32)).astype(BF16)

    spec = pltpu.PrefetchScalarGridSpec(
        num_scalar_prefetch=1, grid=(R // tr,),
        in_specs=[pl.BlockSpec((None, tr, C), lambda i, s, j=j: (s[j], i, 0)) for j in range(4)]
        + [pl.BlockSpec((4, tr, C), lambda i, s: (0, i, 0))],
        out_specs=(pl.BlockSpec((tr, C), lambda i, s: (i, 0)), pl.BlockSpec((3, tr, C), lambda i, s: (0, i, 0))))
    return pl.pallas_call(body, name=name, grid_spec=spec, out_shape=(S((R, C), F32), S((3, R, C), BF16)),
                          compiler_params=_params(("parallel",)))(slots, g, g, g, g, ra)


def rms_cast(h, g, name, riders=()):
    T, D = h.shape
    tm = _tile(T, 512)

    def body(h_ref, g_ref, n_ref):
        n_ref[...] = _rms(h_ref[...], g_ref[...]).astype(BF16)

    row = pl.BlockSpec((tm, D), lambda i: (i, 0))
    return _call(body, name=name, grid=(T // tm,), out_shape=S((T, D), BF16), in_specs=[row, pl.BlockSpec((1, D), lambda i: (0, 0))],
                 out_specs=row, sem=("parallel",), args=(h, g), riders=riders)


def ffn_up_fwd(n, wgu, name, riders=()):
    T, D = n.shape
    F = wgu.shape[1]
    tr, tn = _tile(T, 512), _tile(F, 256)

    def body(n_ref, wg_ref, wu_ref, dgate_ref, dup_ref, a_ref):
        wg, wu = wg_ref[...], wu_ref[...]
        for r in range(T // tr):
            rows = slice(r * tr, (r + 1) * tr)
            x = n_ref[rows, :]
            gate = _nt(x, wg)
            up = _nt(x, wu)
            s = jax.nn.sigmoid(gate)
            silu = gate * s
            dgate_ref[rows, :] = (up * (s * (1.0 + gate * (1.0 - s)))).astype(BF16)
            dup_ref[rows, :] = silu.astype(BF16)
            a_ref[rows, :] = (silu * up).astype(BF16)

    tile = pl.BlockSpec((T, tn), lambda j: (0, j))
    return _call(
        body, name=name, grid=(F // tn,), out_shape=(S((T, F), BF16),) * 3,
        in_specs=[pl.BlockSpec((T, D), lambda j: (0, 0)),
                  pl.BlockSpec((None, tn, D), lambda j: (0, j, 0)), pl.BlockSpec((None, tn, D), lambda j: (1, j, 0))],
        out_specs=(tile, tile, tile), sem=("parallel",), args=(n, wgu, wgu), riders=riders)


def ffn_down_fwd(a, wd, h, g_next, name, riders=()):
    T, F = a.shape
    D = wd.shape[1]
    tm = _tile(T, 256)

    def body(a_ref, w_ref, h_ref, *rest):
        out = h_ref[...] + 0.5 * _nn(a_ref[...], w_ref[...])
        if g_next is None:
            rest[0][...] = out
        else:
            g_ref, o_ref, n_ref = rest
            o_ref[...] = out
            n_ref[...] = _rms(out, g_ref[...]).astype(BF16)

    row = pl.BlockSpec((tm, D), lambda i: (i, 0))
    more = g_next is not None
    return _call(
        body, name=name, grid=(T // tm,), out_shape=(S((T, D), F32), S((T, D), BF16)) if more else S((T, D), F32),
        in_specs=[pl.BlockSpec((tm, F), lambda i: (i, 0)), pl.BlockSpec((F, D), lambda i: (0, 0)), row]
        + ([pl.BlockSpec((1, D), lambda i: (0, 0))] if more else []),
        out_specs=(row, row) if more else row,
        sem=("parallel",), args=(a, wd, h) + ((g_next,) if more else ()), riders=riders)


def mix_in_fwd(h, g, win, name):
    T, D = h.shape
    N = win.shape[0]
    tm = _tile(T, 256)

    def body(h_ref, g_ref, w_ref, n_ref, p_ref):
        n = _rms(h_ref[...], g_ref[...]).astype(BF16)
        n_ref[...] = n
        p_ref[...] = _nt(n, w_ref[...]).astype(BF16)

    return pl.pallas_call(
        body, name=name, grid=(T // tm,), out_shape=(S((T, D), BF16), S((T, N), BF16)),
        in_specs=[pl.BlockSpec((tm, D), lambda i: (i, 0)), pl.BlockSpec((1, D), lambda i: (0, 0)),
                  pl.BlockSpec((N, D), lambda i: (0, 0))],
        out_specs=(pl.BlockSpec((tm, D), lambda i: (i, 0)), pl.BlockSpec((tm, N), lambda i: (i, 0))),
        compiler_params=_params(("parallel",)),
    )(h, g, win)


def _tri_consts():
    r = lax.broadcasted_iota(jnp.int32, (QB, QB), 0)
    c = lax.broadcasted_iota(jnp.int32, (QB, QB), 1)
    ones = jnp.ones((QB, QB), BF16)
    with_sums = lambda tri: jnp.concatenate([tri.astype(BF16), ones], axis=1)
    return with_sums(r > c), with_sums(r <= c), with_sums(r < c)


def _half_masks():
    lane = lax.broadcasted_iota(jnp.int32, (QB, PAIR), 1)
    row = lax.broadcasted_iota(jnp.int32, (QB, PAIR), 0)
    return lane < HEAD_DIM, lane, row


def sb_attn_fwd(p, after, name, riders=()):
    T = p.shape[0]
    nq = T // QB

    def body(q_ref, k_ref, v_ref, m_ref, o_ref, tot_ref, q_sc, acc_ref, z_sc):
        i = pl.program_id(0)
        lo, lane, row = _half_masks()
        causal = lane < row
        heads, pairs = range(SB_HEADS), range(SB_HEADS // 2)
        for hp in pairs:
            q_sc[hp] = (q_ref[:, hp * PAIR:(hp + 1) * PAIR].astype(F32) * SCALE).astype(BF16)
        m2 = m_ref[...]

        def by_head(ref, j, hp):
            t = ref[pl.ds(pl.multiple_of(j * QB, QB), QB), hp * PAIR:(hp + 1) * PAIR]
            return jnp.concatenate([jnp.where(lo, t, 0), jnp.where(lo, 0, t)], axis=0)

        def scores(j):
            return [_nt(q_sc[hp], by_head(k_ref, j, hp)) for hp in pairs]

        def block(j, diag):
            z2 = [z_sc[hp] for hp in pairs]
            ahead = scores(jnp.maximum(j - 1, 0))
            for hp in pairs:
                z_sc[hp] = ahead[hp]
            vs = [by_head(v_ref, j, hp) for hp in pairs]
            spls = [_softplus_logsig(z2[h // 2][:, (h % 2) * QB:(h % 2 + 1) * QB]) for h in heads]
            sp = [jnp.where(causal, spls[h][0], 0.0) if diag else spls[h][0] for h in heads]
            rr = _tri(sp, m2)
            if diag:
                w = [jnp.where(causal, jnp.exp(spls[h][1] - rr[h][:, :QB]), 0.0).astype(BF16) for h in heads]
            else:
                c = [tot_ref[:, h * QB:(h + 1) * QB] for h in heads]
                w = [jnp.exp(spls[h][1] - (c[h] + rr[h][:, :QB])).astype(BF16) for h in heads]
            pv = [_nn(jnp.concatenate([w[2 * hp], w[2 * hp + 1]], axis=1), vs[hp]) for hp in pairs]
            for hp in pairs:
                acc_ref[hp] = pv[hp] if diag else acc_ref[hp] + pv[hp]
            for h in heads:
                tot_ref[:, h * QB:(h + 1) * QB] = rr[h][:, QB:] if diag else c[h] + rr[h][:, QB:]

        first = scores(i)
        for hp in pairs:
            z_sc[hp] = first[hp]
        block(i, True)

        def step(t, carry):
            block(i - 1 - t, False)
            return carry
        lax.fori_loop(0, i, step, 0)
        for hp in pairs:
            o_ref[:, hp * PAIR:(hp + 1) * PAIR] = acc_ref[hp]

    npair = SB_HEADS // 2
    return _call(
        body, name=name, grid=(nq,), out_shape=(S((T, SB_W), F32), S((T, SB_HEADS * QB), F32)),
        in_specs=[pl.BlockSpec((QB, SB_W), lambda i: (i, 0)), pl.BlockSpec((T, SB_W), lambda i: (0, 1)),
                  pl.BlockSpec((T, SB_W), lambda i: (0, 2)), pl.BlockSpec((QB, 2 * QB), lambda i: (0, 0))],
        out_specs=(pl.BlockSpec((QB, SB_W), lambda i: (i, 0)), pl.BlockSpec((QB, SB_HEADS * QB), lambda i: (i, 0))),
        scratch=[pltpu.VMEM((npair, QB, PAIR), BF16), pltpu.VMEM((npair, QB, PAIR), F32), pltpu.VMEM((npair, QB, 2 * QB), F32)],
        sem=("arbitrary",), args=(p, p, p, after), riders=riders,
        marks=((11 * nq) // 16, (14 * nq) // 16))


def sb_attn_bwd(p, do, tot, upto, before, name, riders=()):
    T = p.shape[0]
    nq = T // QB

    def body(q_ref, k_ref, v_ref, do_ref, tot_ref, mp_ref, mg_ref, dq_ref, dk_ref, dv_ref,
             q_sc, d_sc, qd_sc, pg_sc, dq_acc, dk_acc, dv_acc, zd_sc):
        i = pl.program_id(0)
        lo, lane, row = _half_masks()
        causal = lane < row
        heads, pairs = range(SB_HEADS), range(SB_HEADS // 2)

        def by_head(t):
            return jnp.concatenate([jnp.where(lo, t, 0), jnp.where(lo, 0, t)], axis=0)

        for hp in pairs:
            q2 = (q_ref[:, hp * PAIR:(hp + 1) * PAIR].astype(F32) * SCALE).astype(BF16)
            d2 = do_ref[:, hp * PAIR:(hp + 1) * PAIR].astype(BF16)
            q_sc[hp] = q2
            d_sc[hp] = d2
            qd_sc[hp] = by_head(q2)
            qd_sc[SB_HEADS // 2 + hp] = by_head(d2)
        mp, mg = mp_ref[...], mg_ref[...]

        @pl.when(i == 0)
        def _():
            dk_acc[...] = jnp.zeros_like(dk_acc)
            dv_acc[...] = jnp.zeros_like(dv_acc)
        pg_sc[...] = jnp.zeros_like(pg_sc)
        dq_acc[...] = jnp.zeros_like(dq_acc)

        def rows(ref, j, hp):
            return ref[pl.ds(pl.multiple_of(j * QB, QB), QB), hp * PAIR:(hp + 1) * PAIR]

        def products(j):
            return ([_nt(q_sc[hp], by_head(rows(k_ref, j, hp))) for hp in pairs]
                    + [_nt(d_sc[hp], by_head(rows(v_ref, j, hp))) for hp in pairs])

        def block(j, diag):
            r0 = pl.multiple_of(j * QB, QB)
            half = lambda t, h: t[:, (h % 2) * QB:(h % 2 + 1) * QB]
            z = [half(zd_sc[h // 2], h) for h in heads]
            dw = [half(zd_sc[SB_HEADS // 2 + h // 2], h) for h in heads]
            if not diag:
                ahead = products(j + 1)
                for hp in range(SB_HEADS):
                    zd_sc[hp] = ahead[hp]
            ks = [by_head(rows(k_ref, j, hp)) for hp in pairs]
            spls = [_softplus_logsig(z[h]) for h in heads]
            sp = [jnp.where(causal, spls[h][0], 0.0) if diag else spls[h][0] for h in heads]
            rr = _tri(sp, mp)
            pc = [pg_sc[2 * h] for h in heads]
            w = [jnp.exp(spls[h][1] - (tot_ref[:, h * QB:(h + 1) * QB] - (pc[h] + rr[h][:, :QB]))) for h in heads]
            if diag:
                w = [jnp.where(causal, w[h], 0.0) for h in heads]
            gg = [dw[h] * w[h] for h in heads]
            rg = _tri(gg, mg)
            gc = [pg_sc[2 * h + 1] for h in heads]
            dz = [gg[h] - (gg[h] + gc[h] + rg[h][:, :QB]) * jnp.exp(spls[h][1]) for h in heads]
            if diag:
                dz = [jnp.where(causal, dz[h], 0.0) for h in heads]
            dzb = [dz[h].astype(BF16) for h in heads]
            wb = [w[h].astype(BF16) for h in heads]
            both = lambda t, hp, axis: jnp.concatenate([t[2 * hp], t[2 * hp + 1]], axis=axis)
            dq = [_nn(both(dzb, hp, 1), ks[hp]) for hp in pairs]
            dk = [_tn(both(dzb, hp, 0), qd_sc[hp]) for hp in pairs]
            dv = [_tn(both(wb, hp, 0), qd_sc[SB_HEADS // 2 + hp]) for hp in pairs]
            for h in heads:
                if not diag:
                    pg_sc[2 * h] = pc[h] + rr[h][:, QB:]
                    pg_sc[2 * h + 1] = gc[h] + rg[h][:, QB:]
            for hp in pairs:
                dq_acc[hp] += dq[hp]
                dk_acc[pl.ds(r0, QB), hp * PAIR:(hp + 1) * PAIR] += dk[hp]
                dv_acc[pl.ds(r0, QB), hp * PAIR:(hp + 1) * PAIR] += dv[hp]

        first = products(0)
        for hp in range(SB_HEADS):
            zd_sc[hp] = first[hp]

        def step(t, carry):
            block(t, False)
            return carry
        lax.fori_loop(0, i, step, 0)
        block(i, True)
        for hp in pairs:
            dq_ref[:, hp * PAIR:(hp + 1) * PAIR] = (dq_acc[hp] * SCALE).astype(BF16)

        @pl.when(i == nq - 1)
        def _():
            dk_ref[...] = dk_acc[...].astype(BF16)
            dv_ref[...] = dv_acc[...].astype(BF16)

    qtile = pl.BlockSpec((QB, SB_W), lambda i: (i, 0))
    whole = pl.BlockSpec((T, SB_W), lambda i: (0, 0))
    const = pl.BlockSpec((QB, 2 * QB), lambda i: (0, 0))
    return _call(
        body, name=name, grid=(nq,), out_shape=(S((T, SB_W), BF16),) * 3,
        in_specs=[qtile, pl.BlockSpec((T, SB_W), lambda i: (0, 1)), pl.BlockSpec((T, SB_W), lambda i: (0, 2)), qtile,
                  pl.BlockSpec((QB, SB_HEADS * QB), lambda i: (i, 0)), const, const],
        out_specs=(qtile, whole, whole),
        scratch=[pltpu.VMEM((SB_HEADS // 2, QB, PAIR), BF16), pltpu.VMEM((SB_HEADS // 2, QB, PAIR), BF16),
                 pltpu.VMEM((SB_HEADS, 2 * QB, PAIR), BF16),
                 pltpu.VMEM((2 * SB_HEADS, QB, QB), F32), pltpu.VMEM((SB_HEADS // 2, QB, PAIR), F32),
                 pltpu.VMEM((T, SB_W), F32), pltpu.VMEM((T, SB_W), F32), pltpu.VMEM((SB_HEADS, QB, 2 * QB), F32)],
        sem=("arbitrary",), args=(p, p, p, do, tot, upto, before), riders=riders)


def _t5_buckets():
    a = lax.broadcasted_iota(jnp.int32, (QB, QB), 0)
    c = lax.broadcasted_iota(jnp.int32, (QB, QB), 1)

    def bucket(dist):
        dist = jnp.maximum(dist, 0)
        max_exact = N_BUCKETS // 2
        d = jnp.maximum(dist, 1).astype(F32)
        large = max_exact + (jnp.log(d / max_exact) / math.log(MAX_DISTANCE / max_exact)
                             * (N_BUCKETS - max_exact)).astype(jnp.int32)
        large = jnp.minimum(large, N_BUCKETS - 1)
        return jnp.where(dist < max_exact, dist, large)

    return bucket(QB + a - c), bucket(a - c)


def _swa_common(i, kp_ref, kc_ref, vp_ref, vc_ref, bp_ref, bc_ref, rb_ref, bias_ref):
    lo, lane, row = _half_masks()

    @pl.when(i == 0)
    def _():
        for blk, b_ref in enumerate((bp_ref, bc_ref)):
            bk = b_ref[...]
            for h in range(8):
                acc = jnp.zeros((QB, QB), F32)
                for b in range(N_BUCKETS):
                    acc = jnp.where(bk == b, rb_ref[b, h], acc)
                bias_ref[h, blk] = acc

    band = [(lane > row) & (i > 0), lane <= row]

    def stacks(ref):
        t = ref[...].astype(F32)
        sw = pltpu.roll(t, HEAD_DIM, 1)
        return [jnp.concatenate([jnp.where(lo, t, 0.0), jnp.where(lo, 0.0, sw)], axis=0).astype(BF16),
                jnp.concatenate([jnp.where(lo, sw, 0.0), jnp.where(lo, 0.0, t)], axis=0).astype(BF16)]

    ks = [stacks(kp_ref), stacks(kc_ref)]
    vs = [stacks(vp_ref), stacks(vc_ref)]
    return lo, band, ks, vs


def _lane_half(t, h):
    return t[:, (h % 2) * QB:(h % 2 + 1) * QB]


def swa_fwd(p, sinks, rel_bias, bprev, bcur, name, riders=()):
    T = p.shape[0]
    nq = T // QB
    kcol, vcol = (3 * SB_W + SWA_W) // KV_W, (3 * SB_W + SWA_W) // KV_W + 1

    def body(q_ref, kp_ref, kc_ref, vp_ref, vc_ref, bp_ref, bc_ref, sink_ref, rb_ref, o_ref, lse_ref, bias_ref):
        i = pl.program_id(0)
        lo, band, ks, vs = _swa_common(i, kp_ref, kc_ref, vp_ref, vc_ref, bp_ref, bc_ref, rb_ref, bias_ref)
        heads, pairs, blocks = range(8), range(4), range(2)
        rowmax = lambda t: jnp.max(t, axis=1, keepdims=True)
        rowsum = lambda t: jnp.sum(t, axis=1, keepdims=True)
        q2 = [q_ref[:, g * PAIR:(g + 1) * PAIR] for g in pairs]
        s2 = [[_nt(q2[g], ks[b][g // 2]) for b in blocks] for g in pairs]
        sc = [[jnp.where(band[b], _lane_half(s2[h // 2][b], h) * SCALE + bias_ref[h, b], NEG_INF) for b in blocks] for h in heads]
        sink = [sink_ref[0, h] for h in heads]
        m = [jnp.maximum(jnp.maximum(rowmax(sc[h][0]), rowmax(sc[h][1])), sink[h]) for h in heads]
        e = [[jnp.exp(sc[h][b] - m[h]) for b in blocks] for h in heads]
        den = [rowsum(e[h][0]) + rowsum(e[h][1]) + jnp.exp(sink[h] - m[h]) for h in heads]
        pb = [[(e[h][b] / den[h]).astype(BF16) for b in blocks] for h in heads]
        for g in pairs:
            both = lambda b: jnp.concatenate([pb[2 * g][b], pb[2 * g + 1][b]], axis=1)
            o_ref[:, g * PAIR:(g + 1) * PAIR] = _nn(both(0), vs[0][g // 2]) + _nn(both(1), vs[1][g // 2])
        for h in heads:
            lse_ref[:, h * QB:(h + 1) * QB] = jnp.broadcast_to(m[h] + jnp.log(den[h]), (QB, QB))

    kv = lambda col, prev: pl.BlockSpec((QB, KV_W), (lambda i: (jnp.maximum(i - 1, 0), col)) if prev else (lambda i: (i, col)))
    full = pl.BlockSpec((QB, QB), lambda i: (0, 0))
    smem = pl.BlockSpec(memory_space=pltpu.SMEM)
    return _call(
        body, name=name, grid=(nq,), out_shape=(S((T, SWA_W), F32), S((T, 8 * QB), F32)),
        in_specs=[pl.BlockSpec((QB, SWA_W), lambda i: (i, 3)), kv(kcol, True), kv(kcol, False), kv(vcol, True), kv(vcol, False),
                  full, full, smem, smem],
        out_specs=(pl.BlockSpec((QB, SWA_W), lambda i: (i, 0)), pl.BlockSpec((QB, 8 * QB), lambda i: (i, 0))),
        scratch=[pltpu.VMEM((8, 2, QB, QB), F32)],
        sem=("arbitrary",), args=(p, p, p, p, p, bprev, bcur, sinks, rel_bias), riders=riders)


def swa_bwd(p, do, lse, sinks, rel_bias, bprev, bcur, name, riders=()):
    T = p.shape[0]
    nq = T // QB
    kcol, vcol = (3 * SB_W + SWA_W) // KV_W, (3 * SB_W + SWA_W) // KV_W + 1

    def body(q_ref, kp_ref, kc_ref, vp_ref, vc_ref, do_ref, lse_ref, bp_ref, bc_ref, sink_ref, rb_ref,
             dq_ref, dk_ref, dv_ref, dsink_ref, dsc_ref, bias_ref, dk_acc, dv_acc):
        i = pl.program_id(0)
        lo, band, ks, vs = _swa_common(i, kp_ref, kc_ref, vp_ref, vc_ref, bp_ref, bc_ref, rb_ref, bias_ref)

        @pl.when(i == 0)
        def _():
            dk_acc[...] = jnp.zeros_like(dk_acc)
            dv_acc[...] = jnp.zeros_like(dv_acc)
            dsc_ref[...] = jnp.zeros_like(dsc_ref)
            dsink_ref[...] = jnp.zeros_like(dsink_ref)

        heads, pairs, blocks = range(8), range(4), range(2)
        rowsum = lambda t: jnp.sum(t, axis=1, keepdims=True)
        by_head = lambda t: jnp.concatenate([jnp.where(lo, t, 0), jnp.where(lo, 0, t)], axis=0)
        q2 = [q_ref[:, g * PAIR:(g + 1) * PAIR] for g in pairs]
        d2 = [do_ref[:, g * PAIR:(g + 1) * PAIR].astype(BF16) for g in pairs]
        qs = [by_head(q2[g]) for g in pairs]
        dos = [by_head(d2[g]) for g in pairs]
        s2 = [[_nt(q2[g], ks[b][g // 2]) for b in blocks] for g in pairs]
        dp2 = [[_nt(d2[g], vs[b][g // 2]) for b in blocks] for g in pairs]
        lse_h = [lse_ref[:, h * QB:(h + 1) * QB] for h in heads]
        sink = [sink_ref[0, h] for h in heads]
        pr = [[jnp.exp(jnp.where(band[b], _lane_half(s2[h // 2][b], h) * SCALE + bias_ref[h, b], NEG_INF) - lse_h[h])
               for b in blocks] for h in heads]
        dp = [[_lane_half(dp2[h // 2][b], h) for b in blocks] for h in heads]
        delta = [rowsum(pr[h][0] * dp[h][0]) + rowsum(pr[h][1] * dp[h][1]) for h in heads]
        lane1 = lax.broadcasted_iota(jnp.int32, (1, QB), 1)
        dsink = jnp.zeros((1, QB), F32)
        for h in heads:
            dsink = dsink + jnp.where(lane1 == h, -jnp.sum(jnp.exp(sink[h] - lse_h[h][:, :1]) * delta[h]), 0.0)
        dsink_ref[...] += dsink
        dsc = [[pr[h][b] * (dp[h][b] - delta[h]) for b in blocks] for h in heads]
        for h in heads:
            for b in blocks:
                dsc_ref[h, b] += dsc[h][b]
        dzb = [[(dsc[h][b] * SCALE).astype(BF16) for b in blocks] for h in heads]
        prb = [[pr[h][b].astype(BF16) for b in blocks] for h in heads]
        pair_of = lambda t, g, b, axis: jnp.concatenate([t[2 * g][b], t[2 * g + 1][b]], axis=axis)
        for g in pairs:
            dq = _nn(pair_of(dzb, g, 0, 1), ks[0][g // 2]) + _nn(pair_of(dzb, g, 1, 1), ks[1][g // 2])
            dq_ref[:, g * PAIR:(g + 1) * PAIR] = dq.astype(BF16)

        def key_grad(t, other, b):
            per_kv = [_tn(pair_of(t, 2 * kh, b, 0), other[2 * kh]) + _tn(pair_of(t, 2 * kh + 1, b, 0), other[2 * kh + 1]) for kh in range(2)]
            both = [s + pltpu.roll(s, HEAD_DIM, 1) for s in per_kv]
            return jnp.where(lo, both[0], both[1])

        rp = pl.multiple_of(jnp.maximum(i - 1, 0) * QB, QB)
        rc = pl.multiple_of(i * QB, QB)
        dk_acc[pl.ds(rp, QB), :] += key_grad(dzb, qs, 0)
        dv_acc[pl.ds(rp, QB), :] += key_grad(prb, dos, 0)
        dk_acc[pl.ds(rc, QB), :] += key_grad(dzb, qs, 1)
        dv_acc[pl.ds(rc, QB), :] += key_grad(prb, dos, 1)

        @pl.when(i == nq - 1)
        def _():
            dk_ref[...] = dk_acc[...].astype(BF16)
            dv_ref[...] = dv_acc[...].astype(BF16)

    kv = lambda col, prev: pl.BlockSpec((QB, KV_W), (lambda i: (jnp.maximum(i - 1, 0), col)) if prev else (lambda i: (i, col)))
    full = pl.BlockSpec((QB, QB), lambda i: (0, 0))
    smem = pl.BlockSpec(memory_space=pltpu.SMEM)
    whole = lambda shape: pl.BlockSpec(shape, lambda i: (0,) * len(shape))
    return _call(
        body, name=name, grid=(nq,),
        out_shape=(S((T, SWA_W), BF16), S((T, KV_W), BF16), S((T, KV_W), BF16), S((1, QB), F32), S((8, 2, QB, QB), F32)),
        in_specs=[pl.BlockSpec((QB, SWA_W), lambda i: (i, 3)), kv(kcol, True), kv(kcol, False), kv(vcol, True), kv(vcol, False),
                  pl.BlockSpec((QB, SWA_W), lambda i: (i, 0)), pl.BlockSpec((QB, 8 * QB), lambda i: (i, 0)),
                  full, full, smem, smem],
        out_specs=(pl.BlockSpec((QB, SWA_W), lambda i: (i, 0)), whole((T, KV_W)), whole((T, KV_W)), whole((1, QB)),
                   whole((8, 2, QB, QB))),
        scratch=[pltpu.VMEM((8, 2, QB, QB), F32), pltpu.VMEM((T, KV_W), F32), pltpu.VMEM((T, KV_W), F32)],
        sem=("arbitrary",), args=(p, p, p, p, p, do, lse, bprev, bcur, sinks, rel_bias), riders=riders)


def mix_out_fwd(o_sb, o_sw, g_sb, g_sw, wout, h, g_next, name, riders=()):
    T, D = h.shape
    M = SB_W + SWA_W
    tm = _tile(T, 256)

    def body(a_ref, b_ref, ga_ref, gb_ref, w_ref, h_ref, gn_ref, mx_ref, o_ref, n_ref):
        mx_ref[:, :SB_W] = _rms(a_ref[...], ga_ref[...]).astype(BF16)
        mx_ref[:, SB_W:] = _rms(b_ref[...], gb_ref[...]).astype(BF16)
        out = h_ref[...] + _nn(mx_ref[...], w_ref[...])
        o_ref[...] = out
        n_ref[...] = _rms(out, gn_ref[...]).astype(BF16)

    row = lambda n: pl.BlockSpec((tm, n), lambda i: (i, 0))
    vec = lambda n: pl.BlockSpec((1, n), lambda i: (0, 0))
    return _call(
        body, name=name, grid=(T // tm,), out_shape=(S((T, M), BF16), S((T, D), F32), S((T, D), BF16)),
        in_specs=[row(SB_W), row(SWA_W), vec(SB_W), vec(SWA_W), pl.BlockSpec((M, D), lambda i: (0, 0)), row(D), vec(D)],
        out_specs=(row(M), row(D), row(D)),
        sem=("parallel",), args=(o_sb, o_sw, g_sb, g_sw, wout, h, g_next), riders=riders)


def loss_head(h, g, target, name):
    T, D = h.shape
    tm = _tile(T, 256)

    def body(h_ref, g_ref, t_ref, loss_ref, dh_ref, dhb_ref, dg_ref):
        @pl.when(pl.program_id(0) == 0)
        def _():
            loss_ref[...] = jnp.zeros_like(loss_ref)
            dg_ref[...] = jnp.zeros_like(dg_ref)
        x = h_ref[...]
        err = _rms(x, g_ref[...]) - t_ref[...]
        loss_ref[...] += jnp.full((1, QB), 0.5 * jnp.sum(jnp.mean(err * err, axis=-1)), F32)
        dx, dg = _rms_bwd(err / D, x, g_ref[...])
        dh_ref[...] = dx
        dhb_ref[...] = dx.astype(BF16)
        dg_ref[...] += dg

    row = pl.BlockSpec((tm, D), lambda i: (i, 0))
    vec = pl.BlockSpec((1, D), lambda i: (0, 0))
    return pl.pallas_call(
        body, name=name, grid=(T // tm,), out_shape=(S((1, QB), F32), S((T, D), F32), S((T, D), BF16), S((1, D), F32)),
        in_specs=[row, vec, row], out_specs=(pl.BlockSpec((1, QB), lambda i: (0, 0)), row, row, vec),
        compiler_params=_params(("arbitrary",)),
    )(h, g, target)


def ffn_down_bwd(dhb, wd, gate, up, a, n, name, riders=()):
    T, D = dhb.shape
    F = wd.shape[0]
    tr, tn = _tile(T, 512), _tile(F, 256)

    def body(d_ref, n_ref, w_ref, g_ref, u_ref, a_ref, o_ref, dwd_ref, dwdb_ref, dwgu_ref, dwgub_ref):
        w = w_ref[...]
        for r in range(T // tr):
            rows = slice(r * tr, (r + 1) * tr)
            da = 0.5 * _nt(d_ref[rows, :], w)
            o_ref[0, rows, :] = (da * g_ref[rows, :].astype(F32)).astype(BF16)
            o_ref[1, rows, :] = (da * u_ref[rows, :].astype(F32)).astype(BF16)
        dwd = 0.5 * _tn(a_ref[...], d_ref[...])
        dwd_ref[...] = dwd
        dwdb_ref[...] = dwd.astype(BF16)
        for s in range(2):
            dwgu = _tn(o_ref[s], n_ref[...])
            dwgu_ref[s] = dwgu
            dwgub_ref[s] = dwgu.astype(BF16)

    tile = pl.BlockSpec((T, tn), lambda j: (0, j))
    whole = pl.BlockSpec((T, D), lambda j: (0, 0))
    rows1, rows2 = pl.BlockSpec((tn, D), lambda j: (j, 0)), pl.BlockSpec((2, tn, D), lambda j: (0, j, 0))
    return _call(
        body, name=name, grid=(F // tn,),
        out_shape=(S((2, T, F), BF16), S((F, D), F32), S((F, D), BF16), S((2, F, D), F32), S((2, F, D), BF16)),
        in_specs=[whole, whole, rows1, tile, tile, tile],
        out_specs=(pl.BlockSpec((2, T, tn), lambda j: (0, 0, j)), rows1, rows1, rows2, rows2),
        sem=("parallel",), args=(dhb, n, wd, gate, up, a), riders=riders)


def tn_matmul(xs, y, alpha, name, riders=()):
    B, T, N = xs.shape
    D = y.shape[1]
    tn = _tile(N, 256)

    def body(x_ref, y_ref, o_ref, ob_ref):
        o = alpha * _tn(x_ref[...], y_ref[...])
        o_ref[...] = o
        ob_ref[...] = o.astype(BF16)

    tile = pl.BlockSpec((None, tn, D), lambda s, j: (s, j, 0))
    return _call(
        body, name=name, grid=(B, N // tn), out_shape=(S((B, N, D), F32), S((B, N, D), BF16)),
        in_specs=[pl.BlockSpec((None, T, tn), lambda s, j: (s, 0, j)), pl.BlockSpec((T, D), lambda s, j: (0, 0))],
        out_specs=(tile, tile), sem=("parallel", "parallel"), args=(xs, y), riders=riders)


def nn_rms_bwd(xs, ws, h_in, g, dh, name, riders=()):
    B, T, K = xs.shape
    D = ws.shape[2]
    tm = _tile(T, 256)

    def body(x_ref, w_ref, h_ref, g_ref, d_ref, o_ref, ob_ref, dg_ref):
        @pl.when(pl.program_id(0) == 0)
        def _():
            dg_ref[...] = jnp.zeros_like(dg_ref)
        dn = _nn(x_ref[0], w_ref[0])
        for s in range(1, B):
            dn = dn + _nn(x_ref[s], w_ref[s])
        dx, dg = _rms_bwd(dn, h_ref[...], g_ref[...])
        out = d_ref[...] + dx
        o_ref[...] = out
        ob_ref[...] = out.astype(BF16)
        dg_ref[...] += dg

    row = pl.BlockSpec((tm, D), lambda i: (i, 0))
    vec = pl.BlockSpec((1, D), lambda i: (0, 0))
    return _call(
        body, name=name, grid=(T // tm,), out_shape=(S((T, D), F32), S((T, D), BF16), S((1, D), F32)),
        in_specs=[pl.BlockSpec((B, tm, K), lambda i: (0, i, 0)), pl.BlockSpec((B, K, D), lambda i: (0, 0, 0)), row, vec, row],
        out_specs=(row, row, vec),
        sem=("arbitrary",), args=(xs, ws, h_in, g, dh), riders=riders)


def mix_out_bwd(dhb, wout, o_sb, o_sw, g_sb, g_sw, name):
    T, D = dhb.shape
    tm = _tile(T, 256)

    def body(d_ref, w_ref, a_ref, b_ref, ga_ref, gb_ref, da_ref, db_ref, dga_ref, dgb_ref):
        @pl.when(pl.program_id(0) == 0)
        def _():
            dga_ref[...] = jnp.zeros_like(dga_ref)
            dgb_ref[...] = jnp.zeros_like(dgb_ref)
        dm = _nt(d_ref[...], w_ref[...])
        dxa, dga = _rms_bwd(dm[:, :SB_W], a_ref[...], ga_ref[...])
        dxb, dgb = _rms_bwd(dm[:, SB_W:], b_ref[...], gb_ref[...])
        da_ref[...] = dxa
        db_ref[...] = dxb
        dga_ref[...] += dga
        dgb_ref[...] += dgb

    row = lambda n: pl.BlockSpec((tm, n), lambda i: (i, 0))
    vec = lambda n: pl.BlockSpec((1, n), lambda i: (0, 0))
    return pl.pallas_call(
        body, name=name, grid=(T // tm,),
        out_shape=(S((T, SB_W), F32), S((T, SWA_W), F32), S((1, SB_W), F32), S((1, SWA_W), F32)),
        in_specs=[row(D), pl.BlockSpec((SB_W + SWA_W, D), lambda i: (0, 0)), row(SB_W), row(SWA_W), vec(SB_W), vec(SWA_W)],
        out_specs=(row(SB_W), row(SWA_W), vec(SB_W), vec(SWA_W)),
        compiler_params=_params(("arbitrary",)),
    )(dhb, wout, o_sb, o_sw, g_sb, g_sw)


def rel_bias_grad(dscs, bprev, bcur, name):
    n = len(dscs)

    def body(*refs):
        bp_ref, bc_ref, o_ref = refs[n], refs[n + 1], refs[n + 2]
        bks = [bp_ref[...], bc_ref[...]]
        row = lax.broadcasted_iota(jnp.int32, (N_BUCKETS, QB), 0)
        lane = lax.broadcasted_iota(jnp.int32, (N_BUCKETS, QB), 1)
        out = jnp.zeros((N_BUCKETS, QB), F32)
        for h in range(8):
            tot = [sum(refs[l][h, b] for l in range(n)) for b in range(2)]
            for b in range(N_BUCKETS):
                val = jnp.sum(jnp.where(bks[0] == b, tot[0], 0.0)) + jnp.sum(jnp.where(bks[1] == b, tot[1], 0.0))
                out = jnp.where((row == b) & (lane == h), val, out)
        o_ref[...] = out

    return pl.pallas_call(body, name=name, out_shape=S((N_BUCKETS, QB), F32), compiler_params=_params())(*dscs, bprev, bcur)


def _adamw(w, g, m, v):
    m = ADAM_B1 * m + (1.0 - ADAM_B1) * g
    v = ADAM_B2 * v + (1.0 - ADAM_B2) * (g * g)
    m_hat = m / (1.0 - ADAM_B1 ** ADAM_STEP)
    v_hat = v / (1.0 - ADAM_B2 ** ADAM_STEP)
    delta = -ADAM_LR * (m_hat / (jnp.sqrt(v_hat) + ADAM_EPS) + ADAM_WD * w)
    return delta, m, v


def adamw_scattered(w, m, v, owns, others, name, riders=()):
    L, R, C = w.shape
    tr = _rows_tile(R, 176)

    def body(w_ref, m_ref, v_ref, *rest):
        own_refs, other_refs = rest[:L], rest[L:2 * L]
        g_ref, d_ref, mo_ref, vo_ref = rest[2 * L:]
        layer = pl.program_id(0)

        def grad(k):
            o = other_refs[k]
            return own_refs[k][...] + o[0].astype(F32) + o[1].astype(F32) + o[2].astype(F32)

        g = grad(0)
        for k in range(1, L):
            g = jnp.where(layer == k, grad(k), g)
        d, mn, vn = _adamw(w_ref[...], g, m_ref[...], v_ref[...])
        g_ref[...] = g
        d_ref[...] = d
        mo_ref[...] = mn
        vo_ref[...] = vn

    tile = pl.BlockSpec((None, tr, C), lambda l, i: (l, i, 0))
    return _call(
        body, name=name, grid=(L, R // tr), out_shape=(S((L, R, C), F32),) * 4,
        in_specs=[tile] * 3 + [pl.BlockSpec((tr, C), lambda l, i: (i, 0))] * L + [pl.BlockSpec((3, tr, C), lambda l, i: (0, i, 0))] * L,
        out_specs=(tile,) * 4, sem=("parallel", "parallel"), args=(w, m, v, *owns, *others), riders=riders)


def adamw_small(w, gs, m, v, name):
    R, C = w.shape

    def body(w_ref, g_ref, m_ref, v_ref, go_ref, d_ref, mo_ref, vo_ref):
        g = g_ref[0]
        for k in range(1, N_DEV):
            g = g + g_ref[k]
        d, mn, vn = _adamw(w_ref[...], g, m_ref[...], v_ref[...])
        go_ref[...] = g
        d_ref[...] = d
        mo_ref[...] = mn
        vo_ref[...] = vn

    return pl.pallas_call(body, name=name, out_shape=(S((R, C), F32),) * 4, compiler_params=_params())(w, gs, m, v)


def kernel(x, norm_ffn1, w_ffn1_gu, w_ffn1_down, norm_mix, w_in, sinks, norm_out_sb, norm_out_swa, w_out, norm_ffn2, w_ffn2_gu, w_ffn2_down, rel_bias, norm_final, loss_target, m_norm_ffn1, m_w_ffn1_gu, m_w_ffn1_down, m_norm_mix, m_w_in, m_sinks, m_norm_out_sb, m_norm_out_swa, m_w_out, m_norm_ffn2, m_w_ffn2_gu, m_w_ffn2_down, m_rel_bias, m_norm_final, v_norm_ffn1, v_w_ffn1_gu, v_w_ffn1_down, v_norm_mix, v_w_in, v_sinks, v_norm_out_sb, v_norm_out_swa, v_w_out, v_norm_ffn2, v_w_ffn2_gu, v_w_ffn2_down, v_rel_bias, v_norm_final):
    L = norm_ffn1.shape[0]
    T, D = x.shape[1], x.shape[2]
    F = w_ffn1_down.shape[1] * N_DEV
    h = x.reshape(T, D)
    target = loss_target.reshape(T, D)
    after, upto, before = _tri_consts()
    bprev, bcur = _t5_buckets()

    local = {}
    for l in range(L):
        local[f"gu1_{l}"] = w_ffn1_gu[l].T.astype(BF16)
        local[f"d1_{l}"] = w_ffn1_down[l].astype(BF16)
        local[f"in_{l}"] = w_in[l].T.astype(BF16)
        local[f"out_{l}"] = w_out[l].astype(BF16)
        local[f"gu2_{l}"] = w_ffn2_gu[l].T.astype(BF16)
        local[f"d2_{l}"] = w_ffn2_down[l].astype(BF16)
    full, partial = {}, {}
    grads, chip_sum, recv_b = {}, {}, {}

    def run(fn, *args, ag=(), rs1=(), rs2=()):
        halves = lambda names: [n if isinstance(n, tuple) else (n, None) for n in names]
        ag, rs2 = [(n, k) for n, k in halves(ag) if n in local], halves(rs2)
        rows = lambda k, total: None if k is None else (k * (total // 2), total // 2)

        def second(n, k):
            sb = chip_sum[n][1]
            return scatter_second(sb, rows(k, sb.shape[1]), recv_b.get(n))

        riders = ([gather(local[n], rows(k, local[n].shape[0]), partial.get(n)) for n, k in ag]
                  + [scatter_first(grads[n][1]) for n in rs1] + [second(n, k) for n, k in rs2])
        if not riders:
            return fn(*args)
        outs, per = fn(*args, riders=riders)
        per = [p[0] for p in per]
        for n, k in ag:
            buf = per.pop(0)
            if k == 0:
                partial[n] = buf
            else:
                full[n] = buf.reshape(N_DEV * buf.shape[1], D)
        for n in rs1:
            chip_sum[n] = scatter_add(grads[n][0], per.pop(0), f"rs_add_{n}")
        for n, _ in rs2:
            recv_b[n] = per.pop(0)
        return outs

    gu = lambda n: full[n].reshape(2, F, D)
    slots = lambda pair: tuple(t.reshape(N_DEV, -1, D) for t in pair)
    vec = lambda a: a.reshape(1, -1)

    PW = max(D, SB_W + SWA_W)
    n_rows = 4 * L + 2
    n_rows += (-n_rows) % 8

    def pack(ffn1, mix, ffn2, final, osb, osw, snk, rel, extra):
        pieces = []

        def row(*parts):
            flat = [a.reshape(-1) for a in parts]
            pieces.extend(flat)
            used = sum(a.size for a in flat)
            if used < PW:
                pieces.append(jnp.zeros((PW - used,), F32))

        for group in (ffn1, mix, ffn2):
            for l in range(L):
                row(group[l])
        row(final)
        for l in range(L):
            row(osb[l], osw[l])
        row(*[snk[l].reshape(-1)[:8] for l in range(L)], rel, extra)
        pieces.append(jnp.zeros(((n_rows - 4 * L - 2) * PW,), F32))
        return jnp.concatenate(pieces).reshape(n_rows, PW)

    def unpack(arr):
        ffn1, mix, ffn2 = arr[0:L, :D], arr[L:2 * L, :D], arr[2 * L:3 * L, :D]
        final = arr[3 * L, :D]
        ob = arr[3 * L + 1:4 * L + 1]
        tail = arr[4 * L + 1]
        return (ffn1, mix, tail[:8 * L].reshape(L, 8), ob[:, :SB_W], ob[:, SB_W:SB_W + SWA_W], ffn2,
                tail[8 * L:8 * L + N_BUCKETS * 8].reshape(N_BUCKETS, 8), final)

    zero = jnp.zeros((1,), F32)
    w_small = pack(norm_ffn1, norm_mix, norm_ffn2, norm_final, norm_out_sb, norm_out_swa, sinks, rel_bias, zero)
    norm_ffn1, norm_mix, sinks, norm_out_sb, norm_out_swa, norm_ffn2, _, norm_final = unpack(w_small)

    saved = []
    n_next = run(rms_cast, h, vec(norm_ffn1[0]), "rms_first", ag=("gu1_0",))
    for l in range(L):
        nx = l + 1
        s = {"h0": h, "n1": n_next}
        s["gate1"], s["up1"], s["a1"] = run(ffn_up_fwd, s["n1"], gu(f"gu1_{l}"), f"ffn1_up{l}",
                                            ag=(f"d1_{l}", ("in_0", 0) if l == 0 else (f"in_{l}", 1)))
        h = run(ffn_down_fwd, s["a1"], full[f"d1_{l}"], h, None, f"ffn1_down{l}", ag=(("in_0", 1),) if l == 0 else ())
        s["h1"] = h
        s["n2"], s["p"] = mix_in_fwd(h, vec(norm_mix[l]), full[f"in_{l}"], f"mix_in{l}")
        s["o_sb"], s["tot"] = run(sb_attn_fwd, s["p"], after, f"sb_fwd{l}", ag=(f"out_{l}", f"gu2_{l}", f"d2_{l}"))
        s["o_sw"], s["lse"] = run(swa_fwd, s["p"], vec(sinks[l]), rel_bias, bprev, bcur, f"swa_fwd{l}", ag=((f"gu1_{nx}", 0),))
        s["mixed"], h, s["n3"] = run(mix_out_fwd, s["o_sb"], s["o_sw"], vec(norm_out_sb[l]), vec(norm_out_swa[l]),
                                     full[f"out_{l}"], h, vec(norm_ffn2[l]), f"mix_out{l}")
        s["h2"] = h
        s["gate2"], s["up2"], s["a2"] = run(ffn_up_fwd, s["n3"], gu(f"gu2_{l}"), f"ffn2_up{l}",
                                            ag=((f"gu1_{nx}", 1), (f"in_{nx}", 0)))
        if nx < L:
            h, n_next = run(ffn_down_fwd, s["a2"], full[f"d2_{l}"], h, vec(norm_ffn1[nx]), f"ffn2_down{l}")
        else:
            h = run(ffn_down_fwd, s["a2"], full[f"d2_{l}"], h, None, f"ffn2_down{l}")
        saved.append(s)

    loss_part, dh, dhb, dg_final = loss_head(h, vec(norm_final), target, "loss_head")

    small = {k: [None] * L for k in ("ffn1", "mix", "sinks", "osb", "osw", "ffn2", "dsc")}
    for l in reversed(range(L)):
        s = saved[l]

        def ffn_bwd(dh, dhb, tag, gate, up, a, n, h_in, g, r_down, r_up):
            gu_n, d_n = f"gu{tag}_{l}", f"d{tag}_{l}"
            dgu, dwd, dwdb, dwgu, dwgub = run(ffn_down_bwd, dhb, full[d_n], gate, up, a, n, f"ffn{tag}_down_bwd{l}", **r_down)
            grads[gu_n], grads[d_n] = slots((dwgu, dwgub)), slots((dwd, dwdb))
            return run(nn_rms_bwd, dgu, gu(gu_n), h_in, g, dh, f"ffn{tag}_up_bwd{l}", **r_up)

        later = l + 1 < L
        dh, dhb, small["ffn2"][l] = ffn_bwd(dh, dhb, 2, s["gate2"], s["up2"], s["a2"], s["n3"], s["h2"], vec(norm_ffn2[l]),
                                            dict(rs2=((f"gu1_{l + 1}", 0), f"d1_{l + 1}") if later else ()),
                                            dict(rs1=(f"gu2_{l}", f"d2_{l}"), rs2=((f"gu1_{l + 1}", 1),) if later else ()))
        do_sb, do_sw, small["osb"][l], small["osw"][l] = mix_out_bwd(
            dhb, full[f"out_{l}"], s["o_sb"], s["o_sw"], vec(norm_out_sb[l]), vec(norm_out_swa[l]), f"mix_out_bwd{l}")
        grads[f"out_{l}"] = slots(tn_matmul(s["mixed"][None], dhb, 1.0, f"dwout{l}"))
        dq_sb, dk_sb, dv_sb = run(sb_attn_bwd, s["p"], do_sb, s["tot"], upto, before, f"sb_bwd{l}",
                                  rs2=(f"gu2_{l}", f"d2_{l}"), rs1=(f"out_{l}",))
        dq_sw, dk_sw, dv_sw, small["sinks"][l], small["dsc"][l] = swa_bwd(
            s["p"], do_sw, s["lse"], vec(sinks[l]), rel_bias, bprev, bcur, f"swa_bwd{l}")
        dp = jnp.concatenate([dq_sb, dk_sb, dv_sb, dq_sw, dk_sw, dv_sw], axis=1)
        dh, dhb, small["mix"][l] = nn_rms_bwd(dp[None], full[f"in_{l}"][None], s["h1"], vec(norm_mix[l]), dh, f"mix_in_bwd{l}")
        grads[f"in_{l}"] = slots(tn_matmul(dp[None], s["n2"], 1.0, f"dwin{l}"))
        dh, dhb, small["ffn1"][l] = ffn_bwd(dh, dhb, 1, s["gate1"], s["up1"], s["a1"], s["n1"], s["h0"], vec(norm_ffn1[l]),
                                            dict(rs1=(f"in_{l}",), rs2=(f"out_{l}",)),
                                            dict(rs1=(f"gu1_{l}", f"d1_{l}"), rs2=(f"in_{l}",)))

    grad_x = dh.reshape(x.shape)

    upd = {}
    for nm, w, m, v, transposed, last in (
            ("gu2", w_ffn2_gu, m_w_ffn2_gu, v_w_ffn2_gu, True, (("gu1_0", 0),)), ("d2", w_ffn2_down, m_w_ffn2_down, v_w_ffn2_down, False, (("gu1_0", 1),)),
            ("in", w_in, m_w_in, v_w_in, True, ("d1_0",)), ("out", w_out, m_w_out, v_w_out, False, ()),
            ("gu1", w_ffn1_gu, m_w_ffn1_gu, v_w_ffn1_gu, True, ()), ("d1", w_ffn1_down, m_w_ffn1_down, v_w_ffn1_down, False, ())):
        turn = (lambda a: jnp.swapaxes(a, 1, 2)) if transposed else (lambda a: a)
        names = [f"{nm}_{l}" for l in range(L)]
        res = run(adamw_scattered, turn(w), turn(m), turn(v), [chip_sum[n][0] for n in names], [recv_b[n] for n in names],
                  f"adamw_{nm}", rs2=last)
        upd[nm] = tuple(turn(r) for r in res)

    d_rel = rel_bias_grad(small["dsc"], bprev, bcur, "rel_bias_grad")[:, :8]
    g_small = pack(small["ffn1"], small["mix"], small["ffn2"], dg_final, small["osb"], small["osw"], small["sinks"], d_rel,
                   loss_part[0, :1])
    m_small = pack(m_norm_ffn1, m_norm_mix, m_norm_ffn2, m_norm_final, m_norm_out_sb, m_norm_out_swa, m_sinks, m_rel_bias, zero)
    v_small = pack(v_norm_ffn1, v_norm_mix, v_norm_ffn2, v_norm_final, v_norm_out_sb, v_norm_out_swa, v_sinks, v_rel_bias, zero)
    gs_small = all_gather_rows(g_small, "ag_small")
    summed = adamw_small(w_small, gs_small, m_small, v_small, "adamw_small")
    small_out = [unpack(a) for a in summed]
    loss = summed[0][4 * L + 1, 8 * L + N_BUCKETS * 8]

    def group(k):
        sm = small_out[k]
        return (sm[0], upd["gu1"][k], upd["d1"][k], sm[1], upd["in"][k], sm[2], sm[3], sm[4], upd["out"][k], sm[5],
                upd["gu2"][k], upd["d2"][k], sm[6], sm[7])

    return (loss, grad_x, *group(0), *group(1), *group(2), *group(3))
```

```python
import math

import jax
import jax.numpy as jnp
from jax import lax
from jax.experimental import pallas as pl
from jax.experimental.pallas import tpu as pltpu

F32 = jnp.float32
BF16 = jnp.bfloat16
S = jax.ShapeDtypeStruct

N_DEV = 8
HEAD_DIM = 64
SB_HEADS = 8
PAIR = 2 * HEAD_DIM
SB_W = 512
SWA_W = 512
KV_W = 128
IN_W = 3 * SB_W + SWA_W + 2 * KV_W
QB = 128
N_BUCKETS = 32
MAX_DISTANCE = 128
EPS = 1e-6
NEG_INF = -1e30
SCALE = HEAD_DIM ** -0.5

ADAM_LR = 0.001
ADAM_B1 = 0.9
ADAM_B2 = 0.999
ADAM_EPS = 1e-08
ADAM_WD = 0.01
ADAM_STEP = 10

VMEM_LIMIT = 56 * 1024 * 1024
MESH = pl.DeviceIdType.MESH


def _params(sem=None, vmem=VMEM_LIMIT):
    return pltpu.CompilerParams(dimension_semantics=sem, vmem_limit_bytes=vmem)


def _nn(a, b):
    return jnp.dot(a, b, preferred_element_type=F32)


def _nt(a, b):
    return lax.dot_general(a, b, (((1,), (1,)), ((), ())), preferred_element_type=F32)


def _tn(a, b):
    return lax.dot_general(a, b, (((0,), (0,)), ((), ())), preferred_element_type=F32)


def _tri(xs, m):
    return [_nn(x.astype(BF16), m) for x in xs]


def _rms(x, g):
    r = lax.rsqrt(jnp.mean(x * x, axis=-1, keepdims=True) + EPS)
    return x * r * g


def _rms_bwd(dy, x, g):
    r = lax.rsqrt(jnp.mean(x * x, axis=-1, keepdims=True) + EPS)
    xhat = x * r
    u = dy * g
    dx = r * (u - xhat * jnp.mean(u * xhat, axis=-1, keepdims=True))
    return dx, jnp.sum(dy * xhat, axis=0, keepdims=True)


def _softplus_logsig(z):
    sp = jnp.maximum(z, 0.0) + jnp.log(1.0 + jnp.exp(-jnp.abs(z)))
    return sp, z - sp


def _tile(n, want):
    t = min(n, want)
    while n % t:
        t //= 2
    return t


def _place():
    x, y, c = lax.axis_index("x"), lax.axis_index("y"), lax.axis_index("c")
    chips = [(1 - x, y), (x, 1 - y), (1 - x, 1 - y)]
    return x, y, c, chips


def all_gather_rows(v, name):
    R, C = v.shape

    def body(v_ref, out_ref, send_sems, recv_sems, local_sem):
        x, y, c, chips = _place()
        me, sibling = (x, y, c), (x, y, 1 - c)

        def slot(px, py, pc):
            return out_ref.at[4 * px + 2 * py + pc]

        def copy(k, block, to, src=None):
            return pltpu.make_async_remote_copy(
                src_ref=slot(*block) if src is None else src, dst_ref=slot(*block),
                send_sem=send_sems.at[k], recv_sem=recv_sems.at[k], device_id=to, device_id_type=MESH)

        mine = pltpu.make_async_copy(v_ref, slot(*me), local_sem)
        mine.start()
        first = [copy(0, me, sibling, src=v_ref)]
        first += [copy(1 + j, me, (*chip, c), src=v_ref) for j, chip in enumerate(chips)]
        for cp in first:
            cp.start()
        passed = [copy(4 + j, (*chip, c), sibling) for j, chip in enumerate(chips)]
        for j, chip in enumerate(chips):
            copy(1 + j, (*chip, c), me).wait_recv()
            passed[j].start()
        copy(0, sibling, me).wait_recv()
        for j, chip in enumerate(chips):
            copy(4 + j, (*chip, 1 - c), me).wait_recv()
        for cp in first + passed:
            cp.wait_send()
        mine.wait()

    return pl.pallas_call(
        body, name=name, out_shape=S((N_DEV, R, C), v.dtype),
        in_specs=[pl.BlockSpec(memory_space=pl.ANY)], out_specs=pl.BlockSpec(memory_space=pl.ANY),
        scratch_shapes=[pltpu.SemaphoreType.DMA((7,)), pltpu.SemaphoreType.DMA((7,)), pltpu.SemaphoreType.DMA],
    )(v)


class _Exchange:
    def __init__(self, ins, outs, sizes, n_local, plan, aliases=None):
        self.ins, self.outs, self.plan, self.aliases = list(ins), list(outs), plan, aliases or {}
        self.sizes, self.n_local = list(sizes), n_local

    def scratch(self):
        n = sum(self.sizes)
        return [pltpu.SemaphoreType.DMA((n,)), pltpu.SemaphoreType.DMA((n,)), pltpu.SemaphoreType.DMA((max(self.n_local, 1),))]

    def _copies(self, in_refs, out_refs, sems):
        send_sems, recv_sems, local_sems = sems
        phases, local = self.plan(in_refs, out_refs)
        out, k = [], 0
        for phase in phases:
            out.append([pltpu.make_async_remote_copy(src_ref=s, dst_ref=d, send_sem=send_sems.at[k + n], recv_sem=recv_sems.at[k + n],
                                                     device_id=dev, device_id_type=MESH) for n, (s, d, dev) in enumerate(phase)])
            k += len(phase)
        return out, [pltpu.make_async_copy(s, d, local_sems.at[n]) for n, (s, d) in enumerate(local)]

    def start(self, in_refs, out_refs, sems):
        phases, loc = self._copies(in_refs, out_refs, sems)
        for cp in phases[0] + loc:
            cp.start()

    def advance(self, hook, in_refs, out_refs, sems):
        p = hook - (3 - len(self.sizes))
        if p >= 1:
            phases, _ = self._copies(in_refs, out_refs, sems)
            for cp in phases[p - 1]:
                cp.wait_recv()
            for cp in phases[p]:
                cp.start()

    def finish(self, in_refs, out_refs, sems):
        phases, loc = self._copies(in_refs, out_refs, sems)
        for cp in phases[-1]:
            cp.wait_recv()
        for phase in phases:
            for cp in phase:
                cp.wait_send()
        for cp in loc:
            cp.wait()


def gather(v, rows=None, into=None):
    R, C = v.shape
    r0, nr = rows or (0, R)
    na = min(nr, ((nr // 2 + 15) // 16) * 16)

    def plan(ins, outs):
        x, y, c, _ = _place()
        xn, yn, dg, sibling = (1 - x, y), (x, 1 - y), (1 - x, 1 - y), (x, y, 1 - c)
        slot = lambda chip, start=r0, count=nr: outs[0].at[4 * chip[0] + 2 * chip[1] + c, pl.ds(start, count), :]
        src, mine = ins[0].at[pl.ds(r0, nr), :], slot((x, y))
        same = lambda ref, to: (ref, ref, to)
        first = [(src, mine, sibling), (src, mine, (*xn, c)), (src, mine, (*yn, c))]
        relay = [same(slot(xn, r0, na), (*yn, c)), same(slot(yn, r0 + na, nr - na), (*xn, c))]
        onward = [same(slot(xn), sibling), same(slot(yn), sibling), same(slot(dg), sibling)]
        return [first, relay, onward], [(src, mine)]

    if into is None:
        return _Exchange([v], [S((N_DEV, R, C), v.dtype)], (3, 2, 3), 1, plan)
    return _Exchange([v, into], [S((N_DEV, R, C), v.dtype)], (3, 2, 3), 1, plan, aliases={1: 0})


def scatter_first(gb):
    _, R, C = gb.shape

    def plan(ins, outs):
        x, y, c, chips = _place()
        owners = [(x, y)] + chips
        return [[(ins[0].at[4 * px + 2 * py + (1 - c)], outs[0].at[j], (x, y, 1 - c)) for j, (px, py) in enumerate(owners)]], []

    return _Exchange([gb], [S((4, R, C), BF16)], (4,), 0, plan)


def scatter_second(sb, rows=None, into=None):
    r0, nr = rows or (0, sb.shape[1])

    def plan(ins, outs):
        x, y, c, chips = _place()
        part = lambda ref, j: ref.at[j, pl.ds(r0, nr), :]
        return [[(part(ins[0], j), part(outs[0], j), (*chips[j], c)) for j in range(3)]], []

    if into is None:
        return _Exchange([sb], [S(sb.shape, BF16)], (3,), 0, plan)
    return _Exchange([sb, into], [S(sb.shape, BF16)], (3,), 0, plan, aliases={1: 0})


def _call(body, *, name, grid, in_specs, out_specs, out_shape, args, scratch=(), sem=None, riders=(), marks=None):
    single = not isinstance(out_shape, (tuple, list))
    out_shape = (out_shape,) if single else tuple(out_shape)
    out_specs = (out_specs,) if single else tuple(out_specs)
    n_in, n_out, n_sc = len(in_specs), len(out_shape), len(scratch)
    if not riders:
        res = pl.pallas_call(body, name=name, grid=grid, in_specs=list(in_specs), out_specs=out_specs, out_shape=out_shape,
                             scratch_shapes=list(scratch), compiler_params=_params(sem))(*args)
        return res[0] if single else res
    r_ins = [a for r in riders for a in r.ins]
    r_outs = [o for r in riders for o in r.outs]
    r_scr = [s for r in riders for s in r.scratch()]
    aliases, i0, o0 = {}, n_in, n_out
    for r in riders:
        for a, b in r.aliases.items():
            aliases[i0 + a] = o0 + b
        i0, o0 = i0 + len(r.ins), o0 + len(r.outs)
    steps = math.prod(grid)

    def full(*refs):
        ins, rin = refs[:n_in], refs[n_in:n_in + len(r_ins)]
        pos = n_in + len(r_ins)
        outs, rout = refs[pos:pos + n_out], refs[pos + n_out:pos + n_out + len(r_outs)]
        pos += n_out + len(r_outs)
        sc, rsc = refs[pos:pos + n_sc], refs[pos + n_sc:]
        step = 0
        for d, n in enumerate(grid):
            step = step * n + pl.program_id(d)

        def each(method, *lead):
            i, o = 0, 0
            for k, r in enumerate(riders):
                getattr(r, method)(*lead, rin[i:i + len(r.ins)], rout[o:o + len(r.outs)], rsc[3 * k:3 * k + 3])
                i, o = i + len(r.ins), o + len(r.outs)

        @pl.when(step == 0)
        def _():
            each("start")
        body(*ins, *outs, *sc)

        late = max(steps - 1 - max(steps // 8, 1), 0)
        first, second = marks or (min((3 * steps) // 5, late), late)

        @pl.when(step == first)
        def _():
            each("advance", 1)

        @pl.when(step == second)
        def _():
            each("advance", 2)

        @pl.when(step == steps - 1)
        def _():
            each("finish")

    anywhere = pl.BlockSpec(memory_space=pl.ANY)
    res = pl.pallas_call(
        full, name=name, grid=grid, in_specs=list(in_specs) + [anywhere] * len(r_ins),
        out_specs=out_specs + (anywhere,) * len(r_outs), out_shape=out_shape + tuple(r_outs),
        scratch_shapes=list(scratch) + r_scr, input_output_aliases=aliases,
        compiler_params=_params(("arbitrary",) * len(grid)))(*args, *r_ins)
    host, rest, per = res[:n_out], list(res[n_out:]), []
    for r in riders:
        per.append(rest[:len(r.outs)])
        rest = rest[len(r.outs):]
    return (host[0] if single else tuple(host)), per


def _rows_tile(n, cap):
    return max(t for t in range(16, min(n, cap) + 1, 16) if n % t == 0)


def scatter_add(gs, ras, name):
    C = gs[0].shape[2]
    trs = [_rows_tile(g.shape[1], 176) for g in gs]
    nts = [g.shape[1] // tr for g, tr in zip(gs, trs)]
    steps = max(nts)
    x, y, c, chips = _place()
    slots = jnp.stack([4 * px + 2 * py + c for px, py in [(x, y)] + chips]).astype(jnp.int32)

    def body(s_ref, *refs):
        ins, outs = refs[:5 * len(gs)], refs[5 * len(gs):]
        for k in range(len(gs)):
            g0, g1, g2, g3, ra_ref = ins[5 * k:5 * k + 5]
            own_ref, sb_ref = outs[2 * k:2 * k + 2]

            def work(g0=g0, g1=g1, g2=g2, g3=g3, ra_ref=ra_ref, own_ref=own_ref, sb_ref=sb_ref):
                own_ref[...] = g0[...] + ra_ref[0].astype(F32)
                for j, gj in enumerate((g1, g2, g3)):
                    sb_ref[j] = (gj[...] + ra_ref[j + 1].astype(F32)).astype(BF16)

            if nts[k] == steps:
                work()
            else:
                pl.when(pl.program_id(0) < nts[k])(work)

    in_specs, out_specs, out_shape, args = [], [], [], [slots]
    for k, (g, ra, tr) in enumerate(zip(gs, ras, trs)):
        tile = lambda i, k=k: jnp.minimum(i, nts[k] - 1)
        in_specs += [pl.BlockSpec((None, tr, C), lambda i, s, j=j, tile=tile: (s[j], tile(i), 0)) for j in range(4)]
        in_specs.append(pl.BlockSpec((4, tr, C), lambda i, s, tile=tile: (0, tile(i), 0)))
        out_specs += [pl.BlockSpec((tr, C), lambda i, s, tile=tile: (tile(i), 0)),
                      pl.BlockSpec((3, tr, C), lambda i, s, tile=tile: (0, tile(i), 0))]
        out_shape += [S((g.shape[1], C), F32), S((3, g.shape[1], C), BF16)]
        args += [g, g, g, g, ra]
    spec = pltpu.PrefetchScalarGridSpec(num_scalar_prefetch=1, grid=(steps,), in_specs=in_specs, out_specs=tuple(out_specs))
    res = pl.pallas_call(body, name=name, grid_spec=spec, out_shape=tuple(out_shape), compiler_params=_params(("arbitrary",)))(*args)
    return [(res[2 * k], res[2 * k + 1]) for k in range(len(gs))]


def rms_cast(h, g, name, riders=()):
    T, D = h.shape
    tm = _tile(T, 512)

    def body(h_ref, g_ref, n_ref):
        n_ref[...] = _rms(h_ref[...], g_ref[...]).astype(BF16)

    row = pl.BlockSpec((tm, D), lambda i: (i, 0))
    return _call(body, name=name, grid=(T // tm,), out_shape=S((T, D), BF16), in_specs=[row, pl.BlockSpec((1, D), lambda i: (0, 0))],
                 out_specs=row, sem=("parallel",), args=(h, g), riders=riders)


def ffn_up_fwd(n, wgu, name, riders=()):
    T, D = n.shape
    F = wgu.shape[1]
    tr, tn = _tile(T, 512), _tile(F, 256)

    def body(n_ref, wg_ref, wu_ref, dgate_ref, dup_ref, a_ref):
        wg, wu = wg_ref[...], wu_ref[...]
        for r in range(T // tr):
            rows = slice(r * tr, (r + 1) * tr)
            x = n_ref[rows, :]
            gate = _nt(x, wg)
            up = _nt(x, wu)
            s = jax.nn.sigmoid(gate)
            silu = gate * s
            dgate_ref[rows, :] = (up * (s * (1.0 + gate * (1.0 - s)))).astype(BF16)
            dup_ref[rows, :] = silu.astype(BF16)
            a_ref[rows, :] = (silu * up).astype(BF16)

    tile = pl.BlockSpec((T, tn), lambda j: (0, j))
    return _call(
        body, name=name, grid=(F // tn,), out_shape=(S((T, F), BF16),) * 3,
        in_specs=[pl.BlockSpec((T, D), lambda j: (0, 0)),
                  pl.BlockSpec((None, tn, D), lambda j: (0, j, 0)), pl.BlockSpec((None, tn, D), lambda j: (1, j, 0))],
        out_specs=(tile, tile, tile), sem=("parallel",), args=(n, wgu, wgu), riders=riders)


def ffn_down_fwd(a, wd, h, g_next, name, riders=()):
    T, F = a.shape
    D = wd.shape[1]
    tm = _tile(T, 256)

    def body(a_ref, w_ref, h_ref, *rest):
        out = h_ref[...] + 0.5 * _nn(a_ref[...], w_ref[...])
        if g_next is None:
            rest[0][...] = out
        else:
            g_ref, o_ref, n_ref = rest
            o_ref[...] = out
            n_ref[...] = _rms(out, g_ref[...]).astype(BF16)

    row = pl.BlockSpec((tm, D), lambda i: (i, 0))
    more = g_next is not None
    return _call(
        body, name=name, grid=(T // tm,), out_shape=(S((T, D), F32), S((T, D), BF16)) if more else S((T, D), F32),
        in_specs=[pl.BlockSpec((tm, F), lambda i: (i, 0)), pl.BlockSpec((F, D), lambda i: (0, 0)), row]
        + ([pl.BlockSpec((1, D), lambda i: (0, 0))] if more else []),
        out_specs=(row, row) if more else row,
        sem=("parallel",), args=(a, wd, h) + ((g_next,) if more else ()), riders=riders)


def mix_in_fwd(h, g, win, name):
    T, D = h.shape
    N = win.shape[0]
    tm = _tile(T, 256)

    def body(h_ref, g_ref, w_ref, n_ref, p_ref):
        n = _rms(h_ref[...], g_ref[...]).astype(BF16)
        n_ref[...] = n
        p_ref[...] = _nt(n, w_ref[...]).astype(BF16)

    return pl.pallas_call(
        body, name=name, grid=(T // tm,), out_shape=(S((T, D), BF16), S((T, N), BF16)),
        in_specs=[pl.BlockSpec((tm, D), lambda i: (i, 0)), pl.BlockSpec((1, D), lambda i: (0, 0)),
                  pl.BlockSpec((N, D), lambda i: (0, 0))],
        out_specs=(pl.BlockSpec((tm, D), lambda i: (i, 0)), pl.BlockSpec((tm, N), lambda i: (i, 0))),
        compiler_params=_params(("parallel",)),
    )(h, g, win)


def _tri_consts():
    r = lax.broadcasted_iota(jnp.int32, (QB, QB), 0)
    c = lax.broadcasted_iota(jnp.int32, (QB, QB), 1)
    ones = jnp.ones((QB, QB), BF16)
    with_sums = lambda tri: jnp.concatenate([tri.astype(BF16), ones], axis=1)
    return with_sums(r > c), with_sums(r <= c), with_sums(r < c)


def _half_masks():
    lane = lax.broadcasted_iota(jnp.int32, (QB, PAIR), 1)
    row = lax.broadcasted_iota(jnp.int32, (QB, PAIR), 0)
    return lane < HEAD_DIM, lane, row


def sb_attn_fwd(p, after, name, riders=()):
    T = p.shape[0]
    nq = T // QB

    def body(q_ref, k_ref, v_ref, m_ref, o_ref, tot_ref, q_sc, acc_ref, z_sc):
        i = pl.program_id(0)
        lo, lane, row = _half_masks()
        causal = lane < row
        heads, pairs = range(SB_HEADS), range(SB_HEADS // 2)
        for hp in pairs:
            q_sc[hp] = (q_ref[:, hp * PAIR:(hp + 1) * PAIR].astype(F32) * SCALE).astype(BF16)
        m2 = m_ref[...]

        def by_head(ref, j, hp):
            t = ref[pl.ds(pl.multiple_of(j * QB, QB), QB), hp * PAIR:(hp + 1) * PAIR]
            return jnp.concatenate([jnp.where(lo, t, 0), jnp.where(lo, 0, t)], axis=0)

        def scores(j):
            return [_nt(q_sc[hp], by_head(k_ref, j, hp)) for hp in pairs]

        def block(j, diag):
            z2 = [z_sc[hp] for hp in pairs]
            ahead = scores(jnp.maximum(j - 1, 0))
            for hp in pairs:
                z_sc[hp] = ahead[hp]
            vs = [by_head(v_ref, j, hp) for hp in pairs]
            spls = [_softplus_logsig(z2[h // 2][:, (h % 2) * QB:(h % 2 + 1) * QB]) for h in heads]
            sp = [jnp.where(causal, spls[h][0], 0.0) if diag else spls[h][0] for h in heads]
            rr = _tri(sp, m2)
            if diag:
                w = [jnp.where(causal, jnp.exp(spls[h][1] - rr[h][:, :QB]), 0.0).astype(BF16) for h in heads]
            else:
                c = [tot_ref[:, h * QB:(h + 1) * QB] for h in heads]
                w = [jnp.exp(spls[h][1] - (c[h] + rr[h][:, :QB])).astype(BF16) for h in heads]
            pv = [_nn(jnp.concatenate([w[2 * hp], w[2 * hp + 1]], axis=1), vs[hp]) for hp in pairs]
            for hp in pairs:
                acc_ref[hp] = pv[hp] if diag else acc_ref[hp] + pv[hp]
            for h in heads:
                tot_ref[:, h * QB:(h + 1) * QB] = rr[h][:, QB:] if diag else c[h] + rr[h][:, QB:]

        first = scores(i)
        for hp in pairs:
            z_sc[hp] = first[hp]
        block(i, True)

        def step(t, carry):
            block(i - 1 - t, False)
            return carry
        lax.fori_loop(0, i, step, 0)
        for hp in pairs:
            o_ref[:, hp * PAIR:(hp + 1) * PAIR] = acc_ref[hp]

    npair = SB_HEADS // 2
    return _call(
        body, name=name, grid=(nq,), out_shape=(S((T, SB_W), F32), S((T, SB_HEADS * QB), F32)),
        in_specs=[pl.BlockSpec((QB, SB_W), lambda i: (i, 0)), pl.BlockSpec((T, SB_W), lambda i: (0, 1)),
                  pl.BlockSpec((T, SB_W), lambda i: (0, 2)), pl.BlockSpec((QB, 2 * QB), lambda i: (0, 0))],
        out_specs=(pl.BlockSpec((QB, SB_W), lambda i: (i, 0)), pl.BlockSpec((QB, SB_HEADS * QB), lambda i: (i, 0))),
        scratch=[pltpu.VMEM((npair, QB, PAIR), BF16), pltpu.VMEM((npair, QB, PAIR), F32), pltpu.VMEM((npair, QB, 2 * QB), F32)],
        sem=("arbitrary",), args=(p, p, p, after), riders=riders,
        marks=((11 * nq) // 16, (14 * nq) // 16))


def sb_attn_bwd(p, do, tot, upto, before, name, riders=()):
    T = p.shape[0]
    nq = T // QB

    def body(q_ref, k_ref, v_ref, do_ref, tot_ref, mp_ref, mg_ref, dq_ref, dk_ref, dv_ref,
             q_sc, d_sc, qd_sc, pg_sc, dq_acc, dk_acc, dv_acc, zd_sc):
        i = pl.program_id(0)
        lo, lane, row = _half_masks()
        causal = lane < row
        heads, pairs = range(SB_HEADS), range(SB_HEADS // 2)

        def by_head(t):
            return jnp.concatenate([jnp.where(lo, t, 0), jnp.where(lo, 0, t)], axis=0)

        for hp in pairs:
            q2 = (q_ref[:, hp * PAIR:(hp + 1) * PAIR].astype(F32) * SCALE).astype(BF16)
            d2 = do_ref[:, hp * PAIR:(hp + 1) * PAIR].astype(BF16)
            q_sc[hp] = q2
            d_sc[hp] = d2
            qd_sc[hp] = by_head(q2)
            qd_sc[SB_HEADS // 2 + hp] = by_head(d2)
        mp, mg = mp_ref[...], mg_ref[...]

        @pl.when(i == 0)
        def _():
            dk_acc[...] = jnp.zeros_like(dk_acc)
            dv_acc[...] = jnp.zeros_like(dv_acc)
        pg_sc[...] = jnp.zeros_like(pg_sc)
        dq_acc[...] = jnp.zeros_like(dq_acc)

        def rows(ref, j, hp):
            return ref[pl.ds(pl.multiple_of(j * QB, QB), QB), hp * PAIR:(hp + 1) * PAIR]

        def products(j):
            return ([_nt(q_sc[hp], by_head(rows(k_ref, j, hp))) for hp in pairs]
                    + [_nt(d_sc[hp], by_head(rows(v_ref, j, hp))) for hp in pairs])

        def block(j, diag):
            r0 = pl.multiple_of(j * QB, QB)
            half = lambda t, h: t[:, (h % 2) * QB:(h % 2 + 1) * QB]
            z = [half(zd_sc[h // 2], h) for h in heads]
            dw = [half(zd_sc[SB_HEADS // 2 + h // 2], h) for h in heads]
            if not diag:
                ahead = products(j + 1)
                for hp in range(SB_HEADS):
                    zd_sc[hp] = ahead[hp]
            ks = [by_head(rows(k_ref, j, hp)) for hp in pairs]
            spls = [_softplus_logsig(z[h]) for h in heads]
            sp = [jnp.where(causal, spls[h][0], 0.0) if diag else spls[h][0] for h in heads]
            rr = _tri(sp, mp)
            pc = [pg_sc[2 * h] for h in heads]
            w = [jnp.exp(spls[h][1] - (tot_ref[:, h * QB:(h + 1) * QB] - (pc[h] + rr[h][:, :QB]))) for h in heads]
            if diag:
                w = [jnp.where(causal, w[h], 0.0) for h in heads]
            gg = [dw[h] * w[h] for h in heads]
            rg = _tri(gg, mg)
            gc = [pg_sc[2 * h + 1] for h in heads]
            dz = [gg[h] - (gg[h] + gc[h] + rg[h][:, :QB]) * jnp.exp(spls[h][1]) for h in heads]
            if diag:
                dz = [jnp.where(causal, dz[h], 0.0) for h in heads]
            dzb = [dz[h].astype(BF16) for h in heads]
            wb = [w[h].astype(BF16) for h in heads]
            both = lambda t, hp, axis: jnp.concatenate([t[2 * hp], t[2 * hp + 1]], axis=axis)
            dq = [_nn(both(dzb, hp, 1), ks[hp]) for hp in pairs]
            dk = [_tn(both(dzb, hp, 0), qd_sc[hp]) for hp in pairs]
            dv = [_tn(both(wb, hp, 0), qd_sc[SB_HEADS // 2 + hp]) for hp in pairs]
            for h in heads:
                if not diag:
                    pg_sc[2 * h] = pc[h] + rr[h][:, QB:]
                    pg_sc[2 * h + 1] = gc[h] + rg[h][:, QB:]
            for hp in pairs:
                dq_acc[hp] += dq[hp]
                dk_acc[pl.ds(r0, QB), hp * PAIR:(hp + 1) * PAIR] += dk[hp]
                dv_acc[pl.ds(r0, QB), hp * PAIR:(hp + 1) * PAIR] += dv[hp]

        first = products(0)
        for hp in range(SB_HEADS):
            zd_sc[hp] = first[hp]

        def step(t, carry):
            block(t, False)
            return carry
        lax.fori_loop(0, i, step, 0)
        block(i, True)
        for hp in pairs:
            dq_ref[:, hp * PAIR:(hp + 1) * PAIR] = (dq_acc[hp] * SCALE).astype(BF16)

        @pl.when(i == nq - 1)
        def _():
            dk_ref[...] = dk_acc[...].astype(BF16)
            dv_ref[...] = dv_acc[...].astype(BF16)

    qtile = pl.BlockSpec((QB, SB_W), lambda i: (i, 0))
    whole = pl.BlockSpec((T, SB_W), lambda i: (0, 0))
    const = pl.BlockSpec((QB, 2 * QB), lambda i: (0, 0))
    return _call(
        body, name=name, grid=(nq,), out_shape=(S((T, SB_W), BF16),) * 3,
        in_specs=[qtile, pl.BlockSpec((T, SB_W), lambda i: (0, 1)), pl.BlockSpec((T, SB_W), lambda i: (0, 2)), qtile,
                  pl.BlockSpec((QB, SB_HEADS * QB), lambda i: (i, 0)), const, const],
        out_specs=(qtile, whole, whole),
        scratch=[pltpu.VMEM((SB_HEADS // 2, QB, PAIR), BF16), pltpu.VMEM((SB_HEADS // 2, QB, PAIR), BF16),
                 pltpu.VMEM((SB_HEADS, 2 * QB, PAIR), BF16),
                 pltpu.VMEM((2 * SB_HEADS, QB, QB), F32), pltpu.VMEM((SB_HEADS // 2, QB, PAIR), F32),
                 pltpu.VMEM((T, SB_W), F32), pltpu.VMEM((T, SB_W), F32), pltpu.VMEM((SB_HEADS, QB, 2 * QB), F32)],
        sem=("arbitrary",), args=(p, p, p, do, tot, upto, before), riders=riders)


def _t5_buckets():
    a = lax.broadcasted_iota(jnp.int32, (QB, QB), 0)
    c = lax.broadcasted_iota(jnp.int32, (QB, QB), 1)

    def bucket(dist):
        dist = jnp.maximum(dist, 0)
        max_exact = N_BUCKETS // 2
        d = jnp.maximum(dist, 1).astype(F32)
        large = max_exact + (jnp.log(d / max_exact) / math.log(MAX_DISTANCE / max_exact)
                             * (N_BUCKETS - max_exact)).astype(jnp.int32)
        large = jnp.minimum(large, N_BUCKETS - 1)
        return jnp.where(dist < max_exact, dist, large)

    return bucket(QB + a - c), bucket(a - c)


def _swa_common(i, kp_ref, kc_ref, vp_ref, vc_ref, bp_ref, bc_ref, rb_ref, bias_ref):
    lo, lane, row = _half_masks()

    @pl.when(i == 0)
    def _():
        for blk, b_ref in enumerate((bp_ref, bc_ref)):
            bk = b_ref[...]
            for h in range(8):
                acc = jnp.zeros((QB, QB), F32)
                for b in range(N_BUCKETS):
                    acc = jnp.where(bk == b, rb_ref[b, h], acc)
                bias_ref[h, blk] = acc

    band = [(lane > row) & (i > 0), lane <= row]

    def stacks(ref):
        t = ref[...].astype(F32)
        sw = pltpu.roll(t, HEAD_DIM, 1)
        return [jnp.concatenate([jnp.where(lo, t, 0.0), jnp.where(lo, 0.0, sw)], axis=0).astype(BF16),
                jnp.concatenate([jnp.where(lo, sw, 0.0), jnp.where(lo, 0.0, t)], axis=0).astype(BF16)]

    ks = [stacks(kp_ref), stacks(kc_ref)]
    vs = [stacks(vp_ref), stacks(vc_ref)]
    return lo, band, ks, vs


def _lane_half(t, h):
    return t[:, (h % 2) * QB:(h % 2 + 1) * QB]


def swa_fwd(p, sinks, rel_bias, bprev, bcur, name, riders=()):
    T = p.shape[0]
    nq = T // QB
    kcol, vcol = (3 * SB_W + SWA_W) // KV_W, (3 * SB_W + SWA_W) // KV_W + 1

    def body(q_ref, kp_ref, kc_ref, vp_ref, vc_ref, bp_ref, bc_ref, sink_ref, rb_ref, o_ref, lse_ref, bias_ref):
        i = pl.program_id(0)
        lo, band, ks, vs = _swa_common(i, kp_ref, kc_ref, vp_ref, vc_ref, bp_ref, bc_ref, rb_ref, bias_ref)
        heads, pairs, blocks = range(8), range(4), range(2)
        rowmax = lambda t: jnp.max(t, axis=1, keepdims=True)
        rowsum = lambda t: jnp.sum(t, axis=1, keepdims=True)
        q2 = [q_ref[:, g * PAIR:(g + 1) * PAIR] for g in pairs]
        s2 = [[_nt(q2[g], ks[b][g // 2]) for b in blocks] for g in pairs]
        sc = [[jnp.where(band[b], _lane_half(s2[h // 2][b], h) * SCALE + bias_ref[h, b], NEG_INF) for b in blocks] for h in heads]
        sink = [sink_ref[0, h] for h in heads]
        m = [jnp.maximum(jnp.maximum(rowmax(sc[h][0]), rowmax(sc[h][1])), sink[h]) for h in heads]
        e = [[jnp.exp(sc[h][b] - m[h]) for b in blocks] for h in heads]
        den = [rowsum(e[h][0]) + rowsum(e[h][1]) + jnp.exp(sink[h] - m[h]) for h in heads]
        pb = [[(e[h][b] / den[h]).astype(BF16) for b in blocks] for h in heads]
        for g in pairs:
            both = lambda b: jnp.concatenate([pb[2 * g][b], pb[2 * g + 1][b]], axis=1)
            o_ref[:, g * PAIR:(g + 1) * PAIR] = _nn(both(0), vs[0][g // 2]) + _nn(both(1), vs[1][g // 2])
        for h in heads:
            lse_ref[:, h * QB:(h + 1) * QB] = jnp.broadcast_to(m[h] + jnp.log(den[h]), (QB, QB))

    kv = lambda col, prev: pl.BlockSpec((QB, KV_W), (lambda i: (jnp.maximum(i - 1, 0), col)) if prev else (lambda i: (i, col)))
    full = pl.BlockSpec((QB, QB), lambda i: (0, 0))
    smem = pl.BlockSpec(memory_space=pltpu.SMEM)
    return _call(
        body, name=name, grid=(nq,), out_shape=(S((T, SWA_W), F32), S((T, 8 * QB), F32)),
        in_specs=[pl.BlockSpec((QB, SWA_W), lambda i: (i, 3)), kv(kcol, True), kv(kcol, False), kv(vcol, True), kv(vcol, False),
                  full, full, smem, smem],
        out_specs=(pl.BlockSpec((QB, SWA_W), lambda i: (i, 0)), pl.BlockSpec((QB, 8 * QB), lambda i: (i, 0))),
        scratch=[pltpu.VMEM((8, 2, QB, QB), F32)],
        sem=("arbitrary",), args=(p, p, p, p, p, bprev, bcur, sinks, rel_bias), riders=riders)


def swa_bwd(p, do, lse, sinks, rel_bias, bprev, bcur, name, riders=()):
    T = p.shape[0]
    nq = T // QB
    kcol, vcol = (3 * SB_W + SWA_W) // KV_W, (3 * SB_W + SWA_W) // KV_W + 1

    def body(q_ref, kp_ref, kc_ref, vp_ref, vc_ref, do_ref, lse_ref, bp_ref, bc_ref, sink_ref, rb_ref,
             dq_ref, dk_ref, dv_ref, dsink_ref, dsc_ref, bias_ref, dk_acc, dv_acc):
        i = pl.program_id(0)
        lo, band, ks, vs = _swa_common(i, kp_ref, kc_ref, vp_ref, vc_ref, bp_ref, bc_ref, rb_ref, bias_ref)

        @pl.when(i == 0)
        def _():
            dk_acc[...] = jnp.zeros_like(dk_acc)
            dv_acc[...] = jnp.zeros_like(dv_acc)
            dsc_ref[...] = jnp.zeros_like(dsc_ref)
            dsink_ref[...] = jnp.zeros_like(dsink_ref)

        heads, pairs, blocks = range(8), range(4), range(2)
        rowsum = lambda t: jnp.sum(t, axis=1, keepdims=True)
        by_head = lambda t: jnp.concatenate([jnp.where(lo, t, 0), jnp.where(lo, 0, t)], axis=0)
        q2 = [q_ref[:, g * PAIR:(g + 1) * PAIR] for g in pairs]
        d2 = [do_ref[:, g * PAIR:(g + 1) * PAIR].astype(BF16) for g in pairs]
        qs = [by_head(q2[g]) for g in pairs]
        dos = [by_head(d2[g]) for g in pairs]
        s2 = [[_nt(q2[g], ks[b][g // 2]) for b in blocks] for g in pairs]
        dp2 = [[_nt(d2[g], vs[b][g // 2]) for b in blocks] for g in pairs]
        lse_h = [lse_ref[:, h * QB:(h + 1) * QB] for h in heads]
        sink = [sink_ref[0, h] for h in heads]
        pr = [[jnp.exp(jnp.where(band[b], _lane_half(s2[h // 2][b], h) * SCALE + bias_ref[h, b], NEG_INF) - lse_h[h])
               for b in blocks] for h in heads]
        dp = [[_lane_half(dp2[h // 2][b], h) for b in blocks] for h in heads]
        delta = [rowsum(pr[h][0] * dp[h][0]) + rowsum(pr[h][1] * dp[h][1]) for h in heads]
        lane1 = lax.broadcasted_iota(jnp.int32, (1, QB), 1)
        dsink = jnp.zeros((1, QB), F32)
        for h in heads:
            dsink = dsink + jnp.where(lane1 == h, -jnp.sum(jnp.exp(sink[h] - lse_h[h][:, :1]) * delta[h]), 0.0)
        dsink_ref[...] += dsink
        dsc = [[pr[h][b] * (dp[h][b] - delta[h]) for b in blocks] for h in heads]
        for h in heads:
            for b in blocks:
                dsc_ref[h, b] += dsc[h][b]
        dzb = [[(dsc[h][b] * SCALE).astype(BF16) for b in blocks] for h in heads]
        prb = [[pr[h][b].astype(BF16) for b in blocks] for h in heads]
        pair_of = lambda t, g, b, axis: jnp.concatenate([t[2 * g][b], t[2 * g + 1][b]], axis=axis)
        for g in pairs:
            dq = _nn(pair_of(dzb, g, 0, 1), ks[0][g // 2]) + _nn(pair_of(dzb, g, 1, 1), ks[1][g // 2])
            dq_ref[:, g * PAIR:(g + 1) * PAIR] = dq.astype(BF16)

        def key_grad(t, other, b):
            per_kv = [_tn(pair_of(t, 2 * kh, b, 0), other[2 * kh]) + _tn(pair_of(t, 2 * kh + 1, b, 0), other[2 * kh + 1]) for kh in range(2)]
            both = [s + pltpu.roll(s, HEAD_DIM, 1) for s in per_kv]
            return jnp.where(lo, both[0], both[1])

        rp = pl.multiple_of(jnp.maximum(i - 1, 0) * QB, QB)
        rc = pl.multiple_of(i * QB, QB)
        dk_acc[pl.ds(rp, QB), :] += key_grad(dzb, qs, 0)
        dv_acc[pl.ds(rp, QB), :] += key_grad(prb, dos, 0)
        dk_acc[pl.ds(rc, QB), :] += key_grad(dzb, qs, 1)
        dv_acc[pl.ds(rc, QB), :] += key_grad(prb, dos, 1)

        @pl.when(i == nq - 1)
        def _():
            dk_ref[...] = dk_acc[...].astype(BF16)
            dv_ref[...] = dv_acc[...].astype(BF16)

    kv = lambda col, prev: pl.BlockSpec((QB, KV_W), (lambda i: (jnp.maximum(i - 1, 0), col)) if prev else (lambda i: (i, col)))
    full = pl.BlockSpec((QB, QB), lambda i: (0, 0))
    smem = pl.BlockSpec(memory_space=pltpu.SMEM)
    whole = lambda shape: pl.BlockSpec(shape, lambda i: (0,) * len(shape))
    return _call(
        body, name=name, grid=(nq,),
        out_shape=(S((T, SWA_W), BF16), S((T, KV_W), BF16), S((T, KV_W), BF16), S((1, QB), F32), S((8, 2, QB, QB), F32)),
        in_specs=[pl.BlockSpec((QB, SWA_W), lambda i: (i, 3)), kv(kcol, True), kv(kcol, False), kv(vcol, True), kv(vcol, False),
                  pl.BlockSpec((QB, SWA_W), lambda i: (i, 0)), pl.BlockSpec((QB, 8 * QB), lambda i: (i, 0)),
                  full, full, smem, smem],
        out_specs=(pl.BlockSpec((QB, SWA_W), lambda i: (i, 0)), whole((T, KV_W)), whole((T, KV_W)), whole((1, QB)),
                   whole((8, 2, QB, QB))),
        scratch=[pltpu.VMEM((8, 2, QB, QB), F32), pltpu.VMEM((T, KV_W), F32), pltpu.VMEM((T, KV_W), F32)],
        sem=("arbitrary",), args=(p, p, p, p, p, do, lse, bprev, bcur, sinks, rel_bias), riders=riders)


def mix_out_fwd(o_sb, o_sw, g_sb, g_sw, wout, h, g_next, name, riders=()):
    T, D = h.shape
    M = SB_W + SWA_W
    tm = _tile(T, 256)

    def body(a_ref, b_ref, ga_ref, gb_ref, w_ref, h_ref, gn_ref, mx_ref, o_ref, n_ref):
        mx_ref[:, :SB_W] = _rms(a_ref[...], ga_ref[...]).astype(BF16)
        mx_ref[:, SB_W:] = _rms(b_ref[...], gb_ref[...]).astype(BF16)
        out = h_ref[...] + _nn(mx_ref[...], w_ref[...])
        o_ref[...] = out
        n_ref[...] = _rms(out, gn_ref[...]).astype(BF16)

    row = lambda n: pl.BlockSpec((tm, n), lambda i: (i, 0))
    vec = lambda n: pl.BlockSpec((1, n), lambda i: (0, 0))
    return _call(
        body, name=name, grid=(T // tm,), out_shape=(S((T, M), BF16), S((T, D), F32), S((T, D), BF16)),
        in_specs=[row(SB_W), row(SWA_W), vec(SB_W), vec(SWA_W), pl.BlockSpec((M, D), lambda i: (0, 0)), row(D), vec(D)],
        out_specs=(row(M), row(D), row(D)),
        sem=("parallel",), args=(o_sb, o_sw, g_sb, g_sw, wout, h, g_next), riders=riders)


def loss_head(h, g, target, name):
    T, D = h.shape
    tm = _tile(T, 256)

    def body(h_ref, g_ref, t_ref, loss_ref, dh_ref, dhb_ref, dg_ref):
        @pl.when(pl.program_id(0) == 0)
        def _():
            loss_ref[...] = jnp.zeros_like(loss_ref)
            dg_ref[...] = jnp.zeros_like(dg_ref)
        x = h_ref[...]
        err = _rms(x, g_ref[...]) - t_ref[...]
        loss_ref[...] += jnp.full((1, QB), 0.5 * jnp.sum(jnp.mean(err * err, axis=-1)), F32)
        dx, dg = _rms_bwd(err / D, x, g_ref[...])
        dh_ref[...] = dx
        dhb_ref[...] = dx.astype(BF16)
        dg_ref[...] += dg

    row = pl.BlockSpec((tm, D), lambda i: (i, 0))
    vec = pl.BlockSpec((1, D), lambda i: (0, 0))
    return pl.pallas_call(
        body, name=name, grid=(T // tm,), out_shape=(S((1, QB), F32), S((T, D), F32), S((T, D), BF16), S((1, D), F32)),
        in_specs=[row, vec, row], out_specs=(pl.BlockSpec((1, QB), lambda i: (0, 0)), row, row, vec),
        compiler_params=_params(("arbitrary",)),
    )(h, g, target)


def ffn_down_bwd(dhb, wd, gate, up, a, n, name, riders=()):
    T, D = dhb.shape
    F = wd.shape[0]
    tr, tn = _tile(T, 512), _tile(F, 256)

    def body(d_ref, n_ref, w_ref, g_ref, u_ref, a_ref, o_ref, dwd_ref, dwdb_ref, dwgu_ref, dwgub_ref):
        w = w_ref[...]
        for r in range(T // tr):
            rows = slice(r * tr, (r + 1) * tr)
            da = 0.5 * _nt(d_ref[rows, :], w)
            o_ref[0, rows, :] = (da * g_ref[rows, :].astype(F32)).astype(BF16)
            o_ref[1, rows, :] = (da * u_ref[rows, :].astype(F32)).astype(BF16)
        dwd = 0.5 * _tn(a_ref[...], d_ref[...])
        dwd_ref[...] = dwd
        dwdb_ref[...] = dwd.astype(BF16)
        for s in range(2):
            dwgu = _tn(o_ref[s], n_ref[...])
            dwgu_ref[s] = dwgu
            dwgub_ref[s] = dwgu.astype(BF16)

    tile = pl.BlockSpec((T, tn), lambda j: (0, j))
    whole = pl.BlockSpec((T, D), lambda j: (0, 0))
    rows1, rows2 = pl.BlockSpec((tn, D), lambda j: (j, 0)), pl.BlockSpec((2, tn, D), lambda j: (0, j, 0))
    return _call(
        body, name=name, grid=(F // tn,),
        out_shape=(S((2, T, F), BF16), S((F, D), F32), S((F, D), BF16), S((2, F, D), F32), S((2, F, D), BF16)),
        in_specs=[whole, whole, rows1, tile, tile, tile],
        out_specs=(pl.BlockSpec((2, T, tn), lambda j: (0, 0, j)), rows1, rows1, rows2, rows2),
        sem=("parallel",), args=(dhb, n, wd, gate, up, a), riders=riders)


def tn_matmul(xs, y, alpha, name, riders=()):
    B, T, N = xs.shape
    D = y.shape[1]
    tn = _tile(N, 256)

    def body(x_ref, y_ref, o_ref, ob_ref):
        o = alpha * _tn(x_ref[...], y_ref[...])
        o_ref[...] = o
        ob_ref[...] = o.astype(BF16)

    tile = pl.BlockSpec((None, tn, D), lambda s, j: (s, j, 0))
    return _call(
        body, name=name, grid=(B, N // tn), out_shape=(S((B, N, D), F32), S((B, N, D), BF16)),
        in_specs=[pl.BlockSpec((None, T, tn), lambda s, j: (s, 0, j)), pl.BlockSpec((T, D), lambda s, j: (0, 0))],
        out_specs=(tile, tile), sem=("parallel", "parallel"), args=(xs, y), riders=riders)


def nn_rms_bwd(xs, ws, h_in, g, dh, name, riders=()):
    B, T, K = xs.shape
    D = ws.shape[2]
    tm = _tile(T, 256)

    def body(x_ref, w_ref, h_ref, g_ref, d_ref, o_ref, ob_ref, dg_ref):
        @pl.when(pl.program_id(0) == 0)
        def _():
            dg_ref[...] = jnp.zeros_like(dg_ref)
        dn = _nn(x_ref[0], w_ref[0])
        for s in range(1, B):
            dn = dn + _nn(x_ref[s], w_ref[s])
        dx, dg = _rms_bwd(dn, h_ref[...], g_ref[...])
        out = d_ref[...] + dx
        o_ref[...] = out
        ob_ref[...] = out.astype(BF16)
        dg_ref[...] += dg

    row = pl.BlockSpec((tm, D), lambda i: (i, 0))
    vec = pl.BlockSpec((1, D), lambda i: (0, 0))
    return _call(
        body, name=name, grid=(T // tm,), out_shape=(S((T, D), F32), S((T, D), BF16), S((1, D), F32)),
        in_specs=[pl.BlockSpec((B, tm, K), lambda i: (0, i, 0)), pl.BlockSpec((B, K, D), lambda i: (0, 0, 0)), row, vec, row],
        out_specs=(row, row, vec),
        sem=("arbitrary",), args=(xs, ws, h_in, g, dh), riders=riders)


def mix_out_bwd(dhb, wout, o_sb, o_sw, g_sb, g_sw, name):
    T, D = dhb.shape
    tm = _tile(T, 256)

    def body(d_ref, w_ref, a_ref, b_ref, ga_ref, gb_ref, da_ref, db_ref, dga_ref, dgb_ref):
        @pl.when(pl.program_id(0) == 0)
        def _():
            dga_ref[...] = jnp.zeros_like(dga_ref)
            dgb_ref[...] = jnp.zeros_like(dgb_ref)
        dm = _nt(d_ref[...], w_ref[...])
        dxa, dga = _rms_bwd(dm[:, :SB_W], a_ref[...], ga_ref[...])
        dxb, dgb = _rms_bwd(dm[:, SB_W:], b_ref[...], gb_ref[...])
        da_ref[...] = dxa
        db_ref[...] = dxb
        dga_ref[...] += dga
        dgb_ref[...] += dgb

    row = lambda n: pl.BlockSpec((tm, n), lambda i: (i, 0))
    vec = lambda n: pl.BlockSpec((1, n), lambda i: (0, 0))
    return pl.pallas_call(
        body, name=name, grid=(T // tm,),
        out_shape=(S((T, SB_W), F32), S((T, SWA_W), F32), S((1, SB_W), F32), S((1, SWA_W), F32)),
        in_specs=[row(D), pl.BlockSpec((SB_W + SWA_W, D), lambda i: (0, 0)), row(SB_W), row(SWA_W), vec(SB_W), vec(SWA_W)],
        out_specs=(row(SB_W), row(SWA_W), vec(SB_W), vec(SWA_W)),
        compiler_params=_params(("arbitrary",)),
    )(dhb, wout, o_sb, o_sw, g_sb, g_sw)


def rel_bias_grad(dscs, bprev, bcur, name):
    n = len(dscs)

    def body(*refs):
        bp_ref, bc_ref, o_ref = refs[n], refs[n + 1], refs[n + 2]
        bks = [bp_ref[...], bc_ref[...]]
        row = lax.broadcasted_iota(jnp.int32, (N_BUCKETS, QB), 0)
        lane = lax.broadcasted_iota(jnp.int32, (N_BUCKETS, QB), 1)
        out = jnp.zeros((N_BUCKETS, QB), F32)
        for h in range(8):
            tot = [sum(refs[l][h, b] for l in range(n)) for b in range(2)]
            for b in range(N_BUCKETS):
                val = jnp.sum(jnp.where(bks[0] == b, tot[0], 0.0)) + jnp.sum(jnp.where(bks[1] == b, tot[1], 0.0))
                out = jnp.where((row == b) & (lane == h), val, out)
        o_ref[...] = out

    return pl.pallas_call(body, name=name, out_shape=S((N_BUCKETS, QB), F32), compiler_params=_params())(*dscs, bprev, bcur)


def _adamw(w, g, m, v):
    m = ADAM_B1 * m + (1.0 - ADAM_B1) * g
    v = ADAM_B2 * v + (1.0 - ADAM_B2) * (g * g)
    m_hat = m / (1.0 - ADAM_B1 ** ADAM_STEP)
    v_hat = v / (1.0 - ADAM_B2 ** ADAM_STEP)
    delta = -ADAM_LR * (m_hat / (jnp.sqrt(v_hat) + ADAM_EPS) + ADAM_WD * w)
    return delta, m, v


def adamw_scattered(w, m, v, owns, others, name, riders=()):
    L, R, C = w.shape
    tr = _rows_tile(R, 176)

    def body(w_ref, m_ref, v_ref, *rest):
        own_refs, other_refs = rest[:L], rest[L:2 * L]
        g_ref, d_ref, mo_ref, vo_ref = rest[2 * L:]
        layer = pl.program_id(0)

        def grad(k):
            o = other_refs[k]
            return own_refs[k][...] + o[0].astype(F32) + o[1].astype(F32) + o[2].astype(F32)

        g = grad(0)
        for k in range(1, L):
            g = jnp.where(layer == k, grad(k), g)
        d, mn, vn = _adamw(w_ref[...], g, m_ref[...], v_ref[...])
        g_ref[...] = g
        d_ref[...] = d
        mo_ref[...] = mn
        vo_ref[...] = vn

    tile = pl.BlockSpec((None, tr, C), lambda l, i: (l, i, 0))
    return _call(
        body, name=name, grid=(L, R // tr), out_shape=(S((L, R, C), F32),) * 4,
        in_specs=[tile] * 3 + [pl.BlockSpec((tr, C), lambda l, i: (i, 0))] * L + [pl.BlockSpec((3, tr, C), lambda l, i: (0, i, 0))] * L,
        out_specs=(tile,) * 4, sem=("parallel", "parallel"), args=(w, m, v, *owns, *others), riders=riders)


def adamw_small(w, gs, m, v, name):
    R, C = w.shape

    def body(w_ref, g_ref, m_ref, v_ref, go_ref, d_ref, mo_ref, vo_ref):
        g = g_ref[0]
        for k in range(1, N_DEV):
            g = g + g_ref[k]
        d, mn, vn = _adamw(w_ref[...], g, m_ref[...], v_ref[...])
        go_ref[...] = g
        d_ref[...] = d
        mo_ref[...] = mn
        vo_ref[...] = vn

    return pl.pallas_call(body, name=name, out_shape=(S((R, C), F32),) * 4, compiler_params=_params())(w, gs, m, v)


def kernel(x, norm_ffn1, w_ffn1_gu, w_ffn1_down, norm_mix, w_in, sinks, norm_out_sb, norm_out_swa, w_out, norm_ffn2, w_ffn2_gu, w_ffn2_down, rel_bias, norm_final, loss_target, m_norm_ffn1, m_w_ffn1_gu, m_w_ffn1_down, m_norm_mix, m_w_in, m_sinks, m_norm_out_sb, m_norm_out_swa, m_w_out, m_norm_ffn2, m_w_ffn2_gu, m_w_ffn2_down, m_rel_bias, m_norm_final, v_norm_ffn1, v_w_ffn1_gu, v_w_ffn1_down, v_norm_mix, v_w_in, v_sinks, v_norm_out_sb, v_norm_out_swa, v_w_out, v_norm_ffn2, v_w_ffn2_gu, v_w_ffn2_down, v_rel_bias, v_norm_final):
    L = norm_ffn1.shape[0]
    T, D = x.shape[1], x.shape[2]
    F = w_ffn1_down.shape[1] * N_DEV
    h = x.reshape(T, D)
    target = loss_target.reshape(T, D)
    after, upto, before = _tri_consts()
    bprev, bcur = _t5_buckets()

    local = {}
    for l in range(L):
        local[f"gu1_{l}"] = w_ffn1_gu[l].T.astype(BF16)
        local[f"d1_{l}"] = w_ffn1_down[l].astype(BF16)
        local[f"in_{l}"] = w_in[l].T.astype(BF16)
        local[f"out_{l}"] = w_out[l].astype(BF16)
        local[f"gu2_{l}"] = w_ffn2_gu[l].T.astype(BF16)
        local[f"d2_{l}"] = w_ffn2_down[l].astype(BF16)
    full, partial = {}, {}
    grads, chip_sum, recv_b = {}, {}, {}

    def run(fn, *args, ag=(), rs1=(), rs2=()):
        halves = lambda names: [n if isinstance(n, tuple) else (n, None) for n in names]
        ag, rs2 = [(n, k) for n, k in halves(ag) if n in local], halves(rs2)
        rows = lambda k, total: None if k is None else (k * (total // 2), total // 2)

        def second(n, k):
            sb = chip_sum[n][1]
            return scatter_second(sb, rows(k, sb.shape[1]), recv_b.get(n))

        riders = ([gather(local[n], rows(k, local[n].shape[0]), partial.get(n)) for n, k in ag]
                  + [scatter_first(grads[n][1]) for n in rs1] + [second(n, k) for n, k in rs2])
        if not riders:
            return fn(*args)
        outs, per = fn(*args, riders=riders)
        per = [p[0] for p in per]
        for n, k in ag:
            buf = per.pop(0)
            if k == 0:
                partial[n] = buf
            else:
                full[n] = buf.reshape(N_DEV * buf.shape[1], D)
        if rs1:
            sums = scatter_add([grads[n][0] for n in rs1], [per.pop(0) for n in rs1], "rs_add_" + "_".join(rs1))
            chip_sum.update(zip(rs1, sums))
        for n, _ in rs2:
            recv_b[n] = per.pop(0)
        return outs

    gu = lambda n: full[n].reshape(2, F, D)
    slots = lambda pair: tuple(t.reshape(N_DEV, -1, D) for t in pair)
    vec = lambda a: a.reshape(1, -1)

    PW = max(D, SB_W + SWA_W)
    n_rows = 4 * L + 2
    n_rows += (-n_rows) % 8

    def pack(ffn1, mix, ffn2, final, osb, osw, snk, rel, extra):
        pieces = []

        def row(*parts):
            flat = [a.reshape(-1) for a in parts]
            pieces.extend(flat)
            used = sum(a.size for a in flat)
            if used < PW:
                pieces.append(jnp.zeros((PW - used,), F32))

        for group in (ffn1, mix, ffn2):
            for l in range(L):
                row(group[l])
        row(final)
        for l in range(L):
            row(osb[l], osw[l])
        row(*[snk[l].reshape(-1)[:8] for l in range(L)], rel, extra)
        pieces.append(jnp.zeros(((n_rows - 4 * L - 2) * PW,), F32))
        return jnp.concatenate(pieces).reshape(n_rows, PW)

    def unpack(arr):
        ffn1, mix, ffn2 = arr[0:L, :D], arr[L:2 * L, :D], arr[2 * L:3 * L, :D]
        final = arr[3 * L, :D]
        ob = arr[3 * L + 1:4 * L + 1]
        tail = arr[4 * L + 1]
        return (ffn1, mix, tail[:8 * L].reshape(L, 8), ob[:, :SB_W], ob[:, SB_W:SB_W + SWA_W], ffn2,
                tail[8 * L:8 * L + N_BUCKETS * 8].reshape(N_BUCKETS, 8), final)

    zero = jnp.zeros((1,), F32)
    w_small = pack(norm_ffn1, norm_mix, norm_ffn2, norm_final, norm_out_sb, norm_out_swa, sinks, rel_bias, zero)
    norm_ffn1, norm_mix, sinks, norm_out_sb, norm_out_swa, norm_ffn2, _, norm_final = unpack(w_small)

    saved = []
    n_next = run(rms_cast, h, vec(norm_ffn1[0]), "rms_first", ag=("gu1_0",))
    for l in range(L):
        nx = l + 1
        s = {"h0": h, "n1": n_next}
        s["gate1"], s["up1"], s["a1"] = run(ffn_up_fwd, s["n1"], gu(f"gu1_{l}"), f"ffn1_up{l}",
                                            ag=(f"d1_{l}", ("in_0", 0) if l == 0 else (f"in_{l}", 1)))
        h = run(ffn_down_fwd, s["a1"], full[f"d1_{l}"], h, None, f"ffn1_down{l}", ag=(("in_0", 1),) if l == 0 else ())
        s["h1"] = h
        s["n2"], s["p"] = mix_in_fwd(h, vec(norm_mix[l]), full[f"in_{l}"], f"mix_in{l}")
        s["o_sb"], s["tot"] = run(sb_attn_fwd, s["p"], after, f"sb_fwd{l}", ag=(f"out_{l}", f"gu2_{l}", f"d2_{l}"))
        s["o_sw"], s["lse"] = run(swa_fwd, s["p"], vec(sinks[l]), rel_bias, bprev, bcur, f"swa_fwd{l}", ag=((f"gu1_{nx}", 0),))
        s["mixed"], h, s["n3"] = run(mix_out_fwd, s["o_sb"], s["o_sw"], vec(norm_out_sb[l]), vec(norm_out_swa[l]),
                                     full[f"out_{l}"], h, vec(norm_ffn2[l]), f"mix_out{l}")
        s["h2"] = h
        s["gate2"], s["up2"], s["a2"] = run(ffn_up_fwd, s["n3"], gu(f"gu2_{l}"), f"ffn2_up{l}",
                                            ag=((f"gu1_{nx}", 1), (f"in_{nx}", 0)))
        if nx < L:
            h, n_next = run(ffn_down_fwd, s["a2"], full[f"d2_{l}"], h, vec(norm_ffn1[nx]), f"ffn2_down{l}")
        else:
            h = run(ffn_down_fwd, s["a2"], full[f"d2_{l}"], h, None, f"ffn2_down{l}")
        saved.append(s)

    loss_part, dh, dhb, dg_final = loss_head(h, vec(norm_final), target, "loss_head")

    small = {k: [None] * L for k in ("ffn1", "mix", "sinks", "osb", "osw", "ffn2", "dsc")}
    for l in reversed(range(L)):
        s = saved[l]

        def ffn_bwd(dh, dhb, tag, gate, up, a, n, h_in, g, r_down, r_up):
            gu_n, d_n = f"gu{tag}_{l}", f"d{tag}_{l}"
            dgu, dwd, dwdb, dwgu, dwgub = run(ffn_down_bwd, dhb, full[d_n], gate, up, a, n, f"ffn{tag}_down_bwd{l}", **r_down)
            grads[gu_n], grads[d_n] = slots((dwgu, dwgub)), slots((dwd, dwdb))
            return run(nn_rms_bwd, dgu, gu(gu_n), h_in, g, dh, f"ffn{tag}_up_bwd{l}", **r_up)

        later = l + 1 < L
        dh, dhb, small["ffn2"][l] = ffn_bwd(dh, dhb, 2, s["gate2"], s["up2"], s["a2"], s["n3"], s["h2"], vec(norm_ffn2[l]),
                                            dict(rs2=((f"gu1_{l + 1}", 0), f"d1_{l + 1}") if later else ()),
                                            dict(rs1=(f"gu2_{l}", f"d2_{l}"), rs2=((f"gu1_{l + 1}", 1),) if later else ()))
        do_sb, do_sw, small["osb"][l], small["osw"][l] = mix_out_bwd(
            dhb, full[f"out_{l}"], s["o_sb"], s["o_sw"], vec(norm_out_sb[l]), vec(norm_out_swa[l]), f"mix_out_bwd{l}")
        grads[f"out_{l}"] = slots(tn_matmul(s["mixed"][None], dhb, 1.0, f"dwout{l}"))
        dq_sb, dk_sb, dv_sb = run(sb_attn_bwd, s["p"], do_sb, s["tot"], upto, before, f"sb_bwd{l}",
                                  rs2=(f"gu2_{l}", f"d2_{l}"), rs1=(f"out_{l}",))
        dq_sw, dk_sw, dv_sw, small["sinks"][l], small["dsc"][l] = swa_bwd(
            s["p"], do_sw, s["lse"], vec(sinks[l]), rel_bias, bprev, bcur, f"swa_bwd{l}")
        dp = jnp.concatenate([dq_sb, dk_sb, dv_sb, dq_sw, dk_sw, dv_sw], axis=1)
        dh, dhb, small["mix"][l] = nn_rms_bwd(dp[None], full[f"in_{l}"][None], s["h1"], vec(norm_mix[l]), dh, f"mix_in_bwd{l}")
        grads[f"in_{l}"] = slots(tn_matmul(dp[None], s["n2"], 1.0, f"dwin{l}"))
        dh, dhb, small["ffn1"][l] = ffn_bwd(dh, dhb, 1, s["gate1"], s["up1"], s["a1"], s["n1"], s["h0"], vec(norm_ffn1[l]),
                                            dict(rs1=(f"in_{l}",), rs2=(f"out_{l}",)),
                                            dict(rs1=(f"gu1_{l}", f"d1_{l}"), rs2=(f"in_{l}",)))

    grad_x = dh.reshape(x.shape)

    upd = {}
    for nm, w, m, v, transposed, last in (
            ("gu2", w_ffn2_gu, m_w_ffn2_gu, v_w_ffn2_gu, True, (("gu1_0", 0),)), ("d2", w_ffn2_down, m_w_ffn2_down, v_w_ffn2_down, False, (("gu1_0", 1),)),
            ("in", w_in, m_w_in, v_w_in, True, ("d1_0",)), ("out", w_out, m_w_out, v_w_out, False, ()),
            ("gu1", w_ffn1_gu, m_w_ffn1_gu, v_w_ffn1_gu, True, ()), ("d1", w_ffn1_down, m_w_ffn1_down, v_w_ffn1_down, False, ())):
        turn = (lambda a: jnp.swapaxes(a, 1, 2)) if transposed else (lambda a: a)
        names = [f"{nm}_{l}" for l in range(L)]
        res = run(adamw_scattered, turn(w), turn(m), turn(v), [chip_sum[n][0] for n in names], [recv_b[n] for n in names],
                  f"adamw_{nm}", rs2=last)
        upd[nm] = tuple(turn(r) for r in res)

    d_rel = rel_bias_grad(small["dsc"], bprev, bcur, "rel_bias_grad")[:, :8]
    g_small = pack(small["ffn1"], small["mix"], small["ffn2"], dg_final, small["osb"], small["osw"], small["sinks"], d_rel,
                   loss_part[0, :1])
    m_small = pack(m_norm_ffn1, m_norm_mix, m_norm_ffn2, m_norm_final, m_norm_out_sb, m_norm_out_swa, m_sinks, m_rel_bias, zero)
    v_small = pack(v_norm_ffn1, v_norm_mix, v_norm_ffn2, v_norm_final, v_norm_out_sb, v_norm_out_swa, v_sinks, v_rel_bias, zero)
    gs_small = all_gather_rows(g_small, "ag_small")
    summed = adamw_small(w_small, gs_small, m_small, v_small, "adamw_small")
    small_out = [unpack(a) for a in summed]
    loss = summed[0][4 * L + 1, 8 * L + N_BUCKETS * 8]

    def group(k):
        sm = small_out[k]
        return (sm[0], upd["gu1"][k], upd["d1"][k], sm[1], upd["in"][k], sm[2], sm[3], sm[4], upd["out"][k], sm[5],
                upd["gu2"][k], upd["d2"][k], sm[6], sm[7])

    return (loss, grad_x, *group(0), *group(1), *group(2), *group(3))
```

```python
import math

import jax
import jax.numpy as jnp
from jax import lax
from jax.experimental import pallas as pl
from jax.experimental.pallas import tpu as pltpu

F32 = jnp.float32
BF16 = jnp.bfloat16
S = jax.ShapeDtypeStruct

N_DEV = 8
HEAD_DIM = 64
SB_HEADS = 8
PAIR = 2 * HEAD_DIM
SB_W = 512
SWA_W = 512
KV_W = 128
IN_W = 3 * SB_W + SWA_W + 2 * KV_W
QB = 128
N_BUCKETS = 32
MAX_DISTANCE = 128
EPS = 1e-6
NEG_INF = -1e30
SCALE = HEAD_DIM ** -0.5

ADAM_LR = 0.001
ADAM_B1 = 0.9
ADAM_B2 = 0.999
ADAM_EPS = 1e-08
ADAM_WD = 0.01
ADAM_STEP = 10

VMEM_LIMIT = 56 * 1024 * 1024
MESH = pl.DeviceIdType.MESH


def _params(sem=None, vmem=VMEM_LIMIT):
    return pltpu.CompilerParams(dimension_semantics=sem, vmem_limit_bytes=vmem)


def _nn(a, b):
    return jnp.dot(a, b, preferred_element_type=F32)


def _nt(a, b):
    return lax.dot_general(a, b, (((1,), (1,)), ((), ())), preferred_element_type=F32)


def _tn(a, b):
    return lax.dot_general(a, b, (((0,), (0,)), ((), ())), preferred_element_type=F32)


def _tri(xs, m):
    return [_nn(x.astype(BF16), m) for x in xs]


def _rms(x, g):
    r = lax.rsqrt(jnp.mean(x * x, axis=-1, keepdims=True) + EPS)
    return x * r * g


def _rms_bwd(dy, x, g):
    r = lax.rsqrt(jnp.mean(x * x, axis=-1, keepdims=True) + EPS)
    xhat = x * r
    u = dy * g
    dx = r * (u - xhat * jnp.mean(u * xhat, axis=-1, keepdims=True))
    return dx, jnp.sum(dy * xhat, axis=0, keepdims=True)


def _softplus_logsig(z):
    sp = jnp.maximum(z, 0.0) + jnp.log(1.0 + jnp.exp(-jnp.abs(z)))
    return sp, z - sp


def _tile(n, want):
    t = min(n, want)
    while n % t:
        t //= 2
    return t


def _place():
    x, y, c = lax.axis_index("x"), lax.axis_index("y"), lax.axis_index("c")
    chips = [(1 - x, y), (x, 1 - y), (1 - x, 1 - y)]
    return x, y, c, chips


def all_gather_rows(v, name):
    R, C = v.shape

    def body(v_ref, out_ref, send_sems, recv_sems, local_sem):
        x, y, c, chips = _place()
        me, sibling = (x, y, c), (x, y, 1 - c)

        def slot(px, py, pc):
            return out_ref.at[4 * px + 2 * py + pc]

        def copy(k, block, to, src=None):
            return pltpu.make_async_remote_copy(
                src_ref=slot(*block) if src is None else src, dst_ref=slot(*block),
                send_sem=send_sems.at[k], recv_sem=recv_sems.at[k], device_id=to, device_id_type=MESH)

        mine = pltpu.make_async_copy(v_ref, slot(*me), local_sem)
        mine.start()
        first = [copy(0, me, sibling, src=v_ref)]
        first += [copy(1 + j, me, (*chip, c), src=v_ref) for j, chip in enumerate(chips)]
        for cp in first:
            cp.start()
        passed = [copy(4 + j, (*chip, c), sibling) for j, chip in enumerate(chips)]
        for j, chip in enumerate(chips):
            copy(1 + j, (*chip, c), me).wait_recv()
            passed[j].start()
        copy(0, sibling, me).wait_recv()
        for j, chip in enumerate(chips):
            copy(4 + j, (*chip, 1 - c), me).wait_recv()
        for cp in first + passed:
            cp.wait_send()
        mine.wait()

    return pl.pallas_call(
        body, name=name, out_shape=S((N_DEV, R, C), v.dtype),
        in_specs=[pl.BlockSpec(memory_space=pl.ANY)], out_specs=pl.BlockSpec(memory_space=pl.ANY),
        scratch_shapes=[pltpu.SemaphoreType.DMA((7,)), pltpu.SemaphoreType.DMA((7,)), pltpu.SemaphoreType.DMA],
    )(v)


class _Exchange:
    def __init__(self, ins, outs, sizes, n_local, plan, aliases=None):
        self.ins, self.outs, self.plan, self.aliases = list(ins), list(outs), plan, aliases or {}
        self.sizes, self.n_local = list(sizes), n_local

    def scratch(self):
        n = sum(self.sizes)
        return [pltpu.SemaphoreType.DMA((n,)), pltpu.SemaphoreType.DMA((n,)), pltpu.SemaphoreType.DMA((max(self.n_local, 1),))]

    def _copies(self, in_refs, out_refs, sems):
        send_sems, recv_sems, local_sems = sems
        phases, local = self.plan(in_refs, out_refs)
        out, k = [], 0
        for phase in phases:
            out.append([pltpu.make_async_remote_copy(src_ref=s, dst_ref=d, send_sem=send_sems.at[k + n], recv_sem=recv_sems.at[k + n],
                                                     device_id=dev, device_id_type=MESH) for n, (s, d, dev) in enumerate(phase)])
            k += len(phase)
        return out, [pltpu.make_async_copy(s, d, local_sems.at[n]) for n, (s, d) in enumerate(local)]

    def start(self, in_refs, out_refs, sems):
        phases, loc = self._copies(in_refs, out_refs, sems)
        for cp in phases[0] + loc:
            cp.start()

    def advance(self, hook, in_refs, out_refs, sems):
        p = hook - (3 - len(self.sizes))
        if p >= 1:
            phases, _ = self._copies(in_refs, out_refs, sems)
            for cp in phases[p - 1]:
                cp.wait_recv()
            for cp in phases[p]:
                cp.start()

    def finish(self, in_refs, out_refs, sems):
        phases, loc = self._copies(in_refs, out_refs, sems)
        for cp in phases[-1]:
            cp.wait_recv()
        for phase in phases:
            for cp in phase:
                cp.wait_send()
        for cp in loc:
            cp.wait()


def gather(v, rows=None, into=None):
    R, C = v.shape
    r0, nr = rows or (0, R)
    na = min(nr, ((nr // 2 + 15) // 16) * 16)

    def plan(ins, outs):
        x, y, c, _ = _place()
        xn, yn, dg, sibling = (1 - x, y), (x, 1 - y), (1 - x, 1 - y), (x, y, 1 - c)
        slot = lambda chip, start=r0, count=nr: outs[0].at[4 * chip[0] + 2 * chip[1] + c, pl.ds(start, count), :]
        src, mine = ins[0].at[pl.ds(r0, nr), :], slot((x, y))
        same = lambda ref, to: (ref, ref, to)
        first = [(src, mine, sibling), (src, mine, (*xn, c)), (src, mine, (*yn, c))]
        relay = [same(slot(xn, r0, na), (*yn, c)), same(slot(yn, r0 + na, nr - na), (*xn, c))]
        onward = [same(slot(xn), sibling), same(slot(yn), sibling), same(slot(dg), sibling)]
        return [first, relay, onward], [(src, mine)]

    if into is None:
        return _Exchange([v], [S((N_DEV, R, C), v.dtype)], (3, 2, 3), 1, plan)
    return _Exchange([v, into], [S((N_DEV, R, C), v.dtype)], (3, 2, 3), 1, plan, aliases={1: 0})


def scatter_first(gb):
    _, R, C = gb.shape

    def plan(ins, outs):
        x, y, c, chips = _place()
        owners = [(x, y)] + chips
        return [[(ins[0].at[4 * px + 2 * py + (1 - c)], outs[0].at[j], (x, y, 1 - c)) for j, (px, py) in enumerate(owners)]], []

    return _Exchange([gb], [S((4, R, C), BF16)], (4,), 0, plan)


def scatter_second(sb, rows=None, into=None):
    r0, nr = rows or (0, sb.shape[1])

    def plan(ins, outs):
        x, y, c, chips = _place()
        part = lambda ref, j: ref.at[j, pl.ds(r0, nr), :]
        return [[(part(ins[0], j), part(outs[0], j), (*chips[j], c)) for j in range(3)]], []

    if into is None:
        return _Exchange([sb], [S(sb.shape, BF16)], (3,), 0, plan)
    return _Exchange([sb, into], [S(sb.shape, BF16)], (3,), 0, plan, aliases={1: 0})


PARTS = "parts"


def _call(body, *, name, grid, in_specs, out_specs, out_shape, args, scratch=(), sem=None, riders=(), marks=None):
    single = not isinstance(out_shape, (tuple, list))
    out_shape = (out_shape,) if single else tuple(out_shape)
    out_specs = (out_specs,) if single else tuple(out_specs)
    n_in, n_out, n_sc = len(in_specs), len(out_shape), len(scratch)
    if riders is PARTS:
        return dict(body=body, grid=grid, in_specs=list(in_specs), out_specs=out_specs, out_shape=out_shape, args=tuple(args),
                    scratch=list(scratch), marks=marks)
    if not riders:
        res = pl.pallas_call(body, name=name, grid=grid, in_specs=list(in_specs), out_specs=out_specs, out_shape=out_shape,
                             scratch_shapes=list(scratch), compiler_params=_params(sem))(*args)
        return res[0] if single else res
    r_ins = [a for r in riders for a in r.ins]
    r_outs = [o for r in riders for o in r.outs]
    r_scr = [s for r in riders for s in r.scratch()]
    aliases, i0, o0 = {}, n_in, n_out
    for r in riders:
        for a, b in r.aliases.items():
            aliases[i0 + a] = o0 + b
        i0, o0 = i0 + len(r.ins), o0 + len(r.outs)
    steps = math.prod(grid)

    def full(*refs):
        ins, rin = refs[:n_in], refs[n_in:n_in + len(r_ins)]
        pos = n_in + len(r_ins)
        outs, rout = refs[pos:pos + n_out], refs[pos + n_out:pos + n_out + len(r_outs)]
        pos += n_out + len(r_outs)
        sc, rsc = refs[pos:pos + n_sc], refs[pos + n_sc:]
        step = 0
        for d, n in enumerate(grid):
            step = step * n + pl.program_id(d)

        def each(method, *lead):
            i, o = 0, 0
            for k, r in enumerate(riders):
                getattr(r, method)(*lead, rin[i:i + len(r.ins)], rout[o:o + len(r.outs)], rsc[3 * k:3 * k + 3])
                i, o = i + len(r.ins), o + len(r.outs)

        @pl.when(step == 0)
        def _():
            each("start")
        body(*ins, *outs, *sc)

        late = max(steps - 1 - max(steps // 8, 1), 0)
        first, second = marks or (min((3 * steps) // 5, late), late)

        @pl.when(step == first)
        def _():
            each("advance", 1)

        @pl.when(step == second)
        def _():
            each("advance", 2)

        @pl.when(step == steps - 1)
        def _():
            each("finish")

    anywhere = pl.BlockSpec(memory_space=pl.ANY)
    res = pl.pallas_call(
        full, name=name, grid=grid, in_specs=list(in_specs) + [anywhere] * len(r_ins),
        out_specs=out_specs + (anywhere,) * len(r_outs), out_shape=out_shape + tuple(r_outs),
        scratch_shapes=list(scratch) + r_scr, input_output_aliases=aliases,
        compiler_params=_params(("arbitrary",) * len(grid)))(*args, *r_ins)
    host, rest, per = res[:n_out], list(res[n_out:]), []
    for r in riders:
        per.append(rest[:len(r.outs)])
        rest = rest[len(r.outs):]
    return (host[0] if single else tuple(host)), per


def side_by_side(first, second, name, riders=()):
    a_in, a_out, a_sc = len(first["in_specs"]), len(first["out_shape"]), len(first["scratch"])
    n_in, n_out = a_in + len(second["in_specs"]), a_out + len(second["out_shape"])

    def body(*refs):
        ins, outs, sc = refs[:n_in], refs[n_in:n_in + n_out], refs[n_in + n_out:]
        first["body"](*ins[:a_in], *outs[:a_out], *sc[:a_sc])
        second["body"](*ins[a_in:], *outs[a_out:], *sc[a_sc:])

    return _call(body, name=name, grid=first["grid"], in_specs=first["in_specs"] + second["in_specs"],
                 out_specs=first["out_specs"] + second["out_specs"], out_shape=first["out_shape"] + second["out_shape"],
                 args=first["args"] + second["args"], scratch=first["scratch"] + second["scratch"],
                 sem=("arbitrary",) * len(first["grid"]), riders=riders, marks=first["marks"])


def _rows_tile(n, cap):
    return max(t for t in range(16, min(n, cap) + 1, 16) if n % t == 0)


def scatter_add(gs, ras, name):
    C = gs[0].shape[2]
    trs = [_rows_tile(g.shape[1], 176) for g in gs]
    nts = [g.shape[1] // tr for g, tr in zip(gs, trs)]
    steps = max(nts)
    x, y, c, chips = _place()
    slots = jnp.stack([4 * px + 2 * py + c for px, py in [(x, y)] + chips]).astype(jnp.int32)

    def body(s_ref, *refs):
        ins, outs = refs[:5 * len(gs)], refs[5 * len(gs):]
        for k in range(len(gs)):
            g0, g1, g2, g3, ra_ref = ins[5 * k:5 * k + 5]
            own_ref, sb_ref = outs[2 * k:2 * k + 2]

            def work(g0=g0, g1=g1, g2=g2, g3=g3, ra_ref=ra_ref, own_ref=own_ref, sb_ref=sb_ref):
                own_ref[...] = g0[...] + ra_ref[0].astype(F32)
                for j, gj in enumerate((g1, g2, g3)):
                    sb_ref[j] = (gj[...] + ra_ref[j + 1].astype(F32)).astype(BF16)

            if nts[k] == steps:
                work()
            else:
                pl.when(pl.program_id(0) < nts[k])(work)

    in_specs, out_specs, out_shape, args = [], [], [], [slots]
    for k, (g, ra, tr) in enumerate(zip(gs, ras, trs)):
        tile = lambda i, k=k: jnp.minimum(i, nts[k] - 1)
        in_specs += [pl.BlockSpec((None, tr, C), lambda i, s, j=j, tile=tile: (s[j], tile(i), 0)) for j in range(4)]
        in_specs.append(pl.BlockSpec((4, tr, C), lambda i, s, tile=tile: (0, tile(i), 0)))
        out_specs += [pl.BlockSpec((tr, C), lambda i, s, tile=tile: (tile(i), 0)),
                      pl.BlockSpec((3, tr, C), lambda i, s, tile=tile: (0, tile(i), 0))]
        out_shape += [S((g.shape[1], C), F32), S((3, g.shape[1], C), BF16)]
        args += [g, g, g, g, ra]
    spec = pltpu.PrefetchScalarGridSpec(num_scalar_prefetch=1, grid=(steps,), in_specs=in_specs, out_specs=tuple(out_specs))
    res = pl.pallas_call(body, name=name, grid_spec=spec, out_shape=tuple(out_shape), compiler_params=_params(("arbitrary",)))(*args)
    return [(res[2 * k], res[2 * k + 1]) for k in range(len(gs))]


def rms_cast(h, g, name, riders=()):
    T, D = h.shape
    tm = _tile(T, 512)

    def body(h_ref, g_ref, n_ref):
        n_ref[...] = _rms(h_ref[...], g_ref[...]).astype(BF16)

    row = pl.BlockSpec((tm, D), lambda i: (i, 0))
    return _call(body, name=name, grid=(T // tm,), out_shape=S((T, D), BF16), in_specs=[row, pl.BlockSpec((1, D), lambda i: (0, 0))],
                 out_specs=row, sem=("parallel",), args=(h, g), riders=riders)


def ffn_up_fwd(n, wgu, name, riders=()):
    T, D = n.shape
    F = wgu.shape[1]
    tr, tn = _tile(T, 512), _tile(F, 256)

    def body(n_ref, wg_ref, wu_ref, dgate_ref, dup_ref, a_ref):
        wg, wu = wg_ref[...], wu_ref[...]
        for r in range(T // tr):
            rows = slice(r * tr, (r + 1) * tr)
            x = n_ref[rows, :]
            gate = _nt(x, wg)
            up = _nt(x, wu)
            s = jax.nn.sigmoid(gate)
            silu = gate * s
            dgate_ref[rows, :] = (up * (s * (1.0 + gate * (1.0 - s)))).astype(BF16)
            dup_ref[rows, :] = silu.astype(BF16)
            a_ref[rows, :] = (silu * up).astype(BF16)

    tile = pl.BlockSpec((T, tn), lambda j: (0, j))
    return _call(
        body, name=name, grid=(F // tn,), out_shape=(S((T, F), BF16),) * 3,
        in_specs=[pl.BlockSpec((T, D), lambda j: (0, 0)),
                  pl.BlockSpec((None, tn, D), lambda j: (0, j, 0)), pl.BlockSpec((None, tn, D), lambda j: (1, j, 0))],
        out_specs=(tile, tile, tile), sem=("parallel",), args=(n, wgu, wgu), riders=riders)


def ffn_down_fwd(a, wd, h, g_next, name, riders=()):
    T, F = a.shape
    D = wd.shape[1]
    tm = _tile(T, 256)

    def body(a_ref, w_ref, h_ref, *rest):
        out = h_ref[...] + 0.5 * _nn(a_ref[...], w_ref[...])
        if g_next is None:
            rest[0][...] = out
        else:
            g_ref, o_ref, n_ref = rest
            o_ref[...] = out
            n_ref[...] = _rms(out, g_ref[...]).astype(BF16)

    row = pl.BlockSpec((tm, D), lambda i: (i, 0))
    more = g_next is not None
    return _call(
        body, name=name, grid=(T // tm,), out_shape=(S((T, D), F32), S((T, D), BF16)) if more else S((T, D), F32),
        in_specs=[pl.BlockSpec((tm, F), lambda i: (i, 0)), pl.BlockSpec((F, D), lambda i: (0, 0)), row]
        + ([pl.BlockSpec((1, D), lambda i: (0, 0))] if more else []),
        out_specs=(row, row) if more else row,
        sem=("parallel",), args=(a, wd, h) + ((g_next,) if more else ()), riders=riders)


def mix_in_fwd(h, g, win, name):
    T, D = h.shape
    N = win.shape[0]
    tm = _tile(T, 256)

    def body(h_ref, g_ref, w_ref, n_ref, p_ref):
        n = _rms(h_ref[...], g_ref[...]).astype(BF16)
        n_ref[...] = n
        p_ref[...] = _nt(n, w_ref[...]).astype(BF16)

    return pl.pallas_call(
        body, name=name, grid=(T // tm,), out_shape=(S((T, D), BF16), S((T, N), BF16)),
        in_specs=[pl.BlockSpec((tm, D), lambda i: (i, 0)), pl.BlockSpec((1, D), lambda i: (0, 0)),
                  pl.BlockSpec((N, D), lambda i: (0, 0))],
        out_specs=(pl.BlockSpec((tm, D), lambda i: (i, 0)), pl.BlockSpec((tm, N), lambda i: (i, 0))),
        compiler_params=_params(("parallel",)),
    )(h, g, win)


def _tri_consts():
    r = lax.broadcasted_iota(jnp.int32, (QB, QB), 0)
    c = lax.broadcasted_iota(jnp.int32, (QB, QB), 1)
    ones = jnp.ones((QB, QB), BF16)
    with_sums = lambda tri: jnp.concatenate([tri.astype(BF16), ones], axis=1)
    return with_sums(r > c), with_sums(r <= c), with_sums(r < c)


def _half_masks():
    lane = lax.broadcasted_iota(jnp.int32, (QB, PAIR), 1)
    row = lax.broadcasted_iota(jnp.int32, (QB, PAIR), 0)
    return lane < HEAD_DIM, lane, row


def sb_attn_fwd(p, after, name, riders=()):
    T = p.shape[0]
    nq = T // QB

    def body(q_ref, k_ref, v_ref, m_ref, o_ref, tot_ref, q_sc, acc_ref, z_sc):
        i = pl.program_id(0)
        lo, lane, row = _half_masks()
        causal = lane < row
        heads, pairs = range(SB_HEADS), range(SB_HEADS // 2)
        for hp in pairs:
            q_sc[hp] = (q_ref[:, hp * PAIR:(hp + 1) * PAIR].astype(F32) * SCALE).astype(BF16)
        m2 = m_ref[...]

        def by_head(ref, j, hp):
            t = ref[pl.ds(pl.multiple_of(j * QB, QB), QB), hp * PAIR:(hp + 1) * PAIR]
            return jnp.concatenate([jnp.where(lo, t, 0), jnp.where(lo, 0, t)], axis=0)

        def scores(j):
            return [_nt(q_sc[hp], by_head(k_ref, j, hp)) for hp in pairs]

        def block(j, diag):
            z2 = [z_sc[hp] for hp in pairs]
            ahead = scores(jnp.maximum(j - 1, 0))
            for hp in pairs:
                z_sc[hp] = ahead[hp]
            vs = [by_head(v_ref, j, hp) for hp in pairs]
            spls = [_softplus_logsig(z2[h // 2][:, (h % 2) * QB:(h % 2 + 1) * QB]) for h in heads]
            sp = [jnp.where(causal, spls[h][0], 0.0) if diag else spls[h][0] for h in heads]
            rr = _tri(sp, m2)
            if diag:
                w = [jnp.where(causal, jnp.exp(spls[h][1] - rr[h][:, :QB]), 0.0).astype(BF16) for h in heads]
            else:
                c = [tot_ref[:, h * QB:(h + 1) * QB] for h in heads]
                w = [jnp.exp(spls[h][1] - (c[h] + rr[h][:, :QB])).astype(BF16) for h in heads]
            pv = [_nn(jnp.concatenate([w[2 * hp], w[2 * hp + 1]], axis=1), vs[hp]) for hp in pairs]
            for hp in pairs:
                acc_ref[hp] = pv[hp] if diag else acc_ref[hp] + pv[hp]
            for h in heads:
                tot_ref[:, h * QB:(h + 1) * QB] = rr[h][:, QB:] if diag else c[h] + rr[h][:, QB:]

        first = scores(i)
        for hp in pairs:
            z_sc[hp] = first[hp]
        block(i, True)

        def step(t, carry):
            block(i - 1 - t, False)
            return carry
        lax.fori_loop(0, i, step, 0)
        for hp in pairs:
            o_ref[:, hp * PAIR:(hp + 1) * PAIR] = acc_ref[hp]

    npair = SB_HEADS // 2
    return _call(
        body, name=name, grid=(nq,), out_shape=(S((T, SB_W), F32), S((T, SB_HEADS * QB), F32)),
        in_specs=[pl.BlockSpec((QB, SB_W), lambda i: (i, 0)), pl.BlockSpec((T, SB_W), lambda i: (0, 1)),
                  pl.BlockSpec((T, SB_W), lambda i: (0, 2)), pl.BlockSpec((QB, 2 * QB), lambda i: (0, 0))],
        out_specs=(pl.BlockSpec((QB, SB_W), lambda i: (i, 0)), pl.BlockSpec((QB, SB_HEADS * QB), lambda i: (i, 0))),
        scratch=[pltpu.VMEM((npair, QB, PAIR), BF16), pltpu.VMEM((npair, QB, PAIR), F32), pltpu.VMEM((npair, QB, 2 * QB), F32)],
        sem=("arbitrary",), args=(p, p, p, after), riders=riders,
        marks=((11 * nq) // 16, (14 * nq) // 16))


def sb_attn_bwd(p, do, tot, upto, before, name, riders=()):
    T = p.shape[0]
    nq = T // QB

    def body(q_ref, k_ref, v_ref, do_ref, tot_ref, mp_ref, mg_ref, dq_ref, dk_ref, dv_ref,
             q_sc, d_sc, qd_sc, pg_sc, dq_acc, dk_acc, dv_acc, zd_sc):
        i = pl.program_id(0)
        lo, lane, row = _half_masks()
        causal = lane < row
        heads, pairs = range(SB_HEADS), range(SB_HEADS // 2)

        def by_head(t):
            return jnp.concatenate([jnp.where(lo, t, 0), jnp.where(lo, 0, t)], axis=0)

        for hp in pairs:
            q2 = (q_ref[:, hp * PAIR:(hp + 1) * PAIR].astype(F32) * SCALE).astype(BF16)
            d2 = do_ref[:, hp * PAIR:(hp + 1) * PAIR].astype(BF16)
            q_sc[hp] = q2
            d_sc[hp] = d2
            qd_sc[hp] = by_head(q2)
            qd_sc[SB_HEADS // 2 + hp] = by_head(d2)
        mp, mg = mp_ref[...], mg_ref[...]

        @pl.when(i == 0)
        def _():
            dk_acc[...] = jnp.zeros_like(dk_acc)
            dv_acc[...] = jnp.zeros_like(dv_acc)
        pg_sc[...] = jnp.zeros_like(pg_sc)
        dq_acc[...] = jnp.zeros_like(dq_acc)

        def rows(ref, j, hp):
            return ref[pl.ds(pl.multiple_of(j * QB, QB), QB), hp * PAIR:(hp + 1) * PAIR]

        def products(j):
            return ([_nt(q_sc[hp], by_head(rows(k_ref, j, hp))) for hp in pairs]
                    + [_nt(d_sc[hp], by_head(rows(v_ref, j, hp))) for hp in pairs])

        def block(j, diag):
            r0 = pl.multiple_of(j * QB, QB)
            half = lambda t, h: t[:, (h % 2) * QB:(h % 2 + 1) * QB]
            z = [half(zd_sc[h // 2], h) for h in heads]
            dw = [half(zd_sc[SB_HEADS // 2 + h // 2], h) for h in heads]
            if not diag:
                ahead = products(j + 1)
                for hp in range(SB_HEADS):
                    zd_sc[hp] = ahead[hp]
            ks = [by_head(rows(k_ref, j, hp)) for hp in pairs]
            spls = [_softplus_logsig(z[h]) for h in heads]
            sp = [jnp.where(causal, spls[h][0], 0.0) if diag else spls[h][0] for h in heads]
            rr = _tri(sp, mp)
            pc = [pg_sc[2 * h] for h in heads]
            w = [jnp.exp(spls[h][1] - (tot_ref[:, h * QB:(h + 1) * QB] - (pc[h] + rr[h][:, :QB]))) for h in heads]
            if diag:
                w = [jnp.where(causal, w[h], 0.0) for h in heads]
            gg = [dw[h] * w[h] for h in heads]
            rg = _tri(gg, mg)
            gc = [pg_sc[2 * h + 1] for h in heads]
            dz = [gg[h] - (gg[h] + gc[h] + rg[h][:, :QB]) * jnp.exp(spls[h][1]) for h in heads]
            if diag:
                dz = [jnp.where(causal, dz[h], 0.0) for h in heads]
            dzb = [dz[h].astype(BF16) for h in heads]
            wb = [w[h].astype(BF16) for h in heads]
            both = lambda t, hp, axis: jnp.concatenate([t[2 * hp], t[2 * hp + 1]], axis=axis)
            dq = [_nn(both(dzb, hp, 1), ks[hp]) for hp in pairs]
            dk = [_tn(both(dzb, hp, 0), qd_sc[hp]) for hp in pairs]
            dv = [_tn(both(wb, hp, 0), qd_sc[SB_HEADS // 2 + hp]) for hp in pairs]
            for h in heads:
                if not diag:
                    pg_sc[2 * h] = pc[h] + rr[h][:, QB:]
                    pg_sc[2 * h + 1] = gc[h] + rg[h][:, QB:]
            for hp in pairs:
                dq_acc[hp] += dq[hp]
                dk_acc[pl.ds(r0, QB), hp * PAIR:(hp + 1) * PAIR] += dk[hp]
                dv_acc[pl.ds(r0, QB), hp * PAIR:(hp + 1) * PAIR] += dv[hp]

        first = products(0)
        for hp in range(SB_HEADS):
            zd_sc[hp] = first[hp]

        def step(t, carry):
            block(t, False)
            return carry
        lax.fori_loop(0, i, step, 0)
        block(i, True)
        for hp in pairs:
            dq_ref[:, hp * PAIR:(hp + 1) * PAIR] = (dq_acc[hp] * SCALE).astype(BF16)

        @pl.when(i == nq - 1)
        def _():
            dk_ref[...] = dk_acc[...].astype(BF16)
            dv_ref[...] = dv_acc[...].astype(BF16)

    qtile = pl.BlockSpec((QB, SB_W), lambda i: (i, 0))
    whole = pl.BlockSpec((T, SB_W), lambda i: (0, 0))
    const = pl.BlockSpec((QB, 2 * QB), lambda i: (0, 0))
    return _call(
        body, name=name, grid=(nq,), out_shape=(S((T, SB_W), BF16),) * 3,
        in_specs=[qtile, pl.BlockSpec((T, SB_W), lambda i: (0, 1)), pl.BlockSpec((T, SB_W), lambda i: (0, 2)), qtile,
                  pl.BlockSpec((QB, SB_HEADS * QB), lambda i: (i, 0)), const, const],
        out_specs=(qtile, whole, whole),
        scratch=[pltpu.VMEM((SB_HEADS // 2, QB, PAIR), BF16), pltpu.VMEM((SB_HEADS // 2, QB, PAIR), BF16),
                 pltpu.VMEM((SB_HEADS, 2 * QB, PAIR), BF16),
                 pltpu.VMEM((2 * SB_HEADS, QB, QB), F32), pltpu.VMEM((SB_HEADS // 2, QB, PAIR), F32),
                 pltpu.VMEM((T, SB_W), F32), pltpu.VMEM((T, SB_W), F32), pltpu.VMEM((SB_HEADS, QB, 2 * QB), F32)],
        sem=("arbitrary",), args=(p, p, p, do, tot, upto, before), riders=riders)


def _t5_buckets():
    a = lax.broadcasted_iota(jnp.int32, (QB, QB), 0)
    c = lax.broadcasted_iota(jnp.int32, (QB, QB), 1)

    def bucket(dist):
        dist = jnp.maximum(dist, 0)
        max_exact = N_BUCKETS // 2
        d = jnp.maximum(dist, 1).astype(F32)
        large = max_exact + (jnp.log(d / max_exact) / math.log(MAX_DISTANCE / max_exact)
                             * (N_BUCKETS - max_exact)).astype(jnp.int32)
        large = jnp.minimum(large, N_BUCKETS - 1)
        return jnp.where(dist < max_exact, dist, large)

    return bucket(QB + a - c), bucket(a - c)


def _swa_common(i, kp_ref, kc_ref, vp_ref, vc_ref, bp_ref, bc_ref, rb_ref, bias_ref):
    lo, lane, row = _half_masks()

    @pl.when(i == 0)
    def _():
        for blk, b_ref in enumerate((bp_ref, bc_ref)):
            bk = b_ref[...]
            for h in range(8):
                acc = jnp.zeros((QB, QB), F32)
                for b in range(N_BUCKETS):
                    acc = jnp.where(bk == b, rb_ref[b, h], acc)
                bias_ref[h, blk] = acc

    band = [(lane > row) & (i > 0), lane <= row]

    def stacks(ref):
        t = ref[...].astype(F32)
        sw = pltpu.roll(t, HEAD_DIM, 1)
        return [jnp.concatenate([jnp.where(lo, t, 0.0), jnp.where(lo, 0.0, sw)], axis=0).astype(BF16),
                jnp.concatenate([jnp.where(lo, sw, 0.0), jnp.where(lo, 0.0, t)], axis=0).astype(BF16)]

    ks = [stacks(kp_ref), stacks(kc_ref)]
    vs = [stacks(vp_ref), stacks(vc_ref)]
    return lo, band, ks, vs


def _lane_half(t, h):
    return t[:, (h % 2) * QB:(h % 2 + 1) * QB]


def swa_fwd(p, sinks, rel_bias, bprev, bcur, name, riders=()):
    T = p.shape[0]
    nq = T // QB
    kcol, vcol = (3 * SB_W + SWA_W) // KV_W, (3 * SB_W + SWA_W) // KV_W + 1

    def body(q_ref, kp_ref, kc_ref, vp_ref, vc_ref, bp_ref, bc_ref, sink_ref, rb_ref, o_ref, lse_ref, bias_ref):
        i = pl.program_id(0)
        lo, band, ks, vs = _swa_common(i, kp_ref, kc_ref, vp_ref, vc_ref, bp_ref, bc_ref, rb_ref, bias_ref)
        heads, pairs, blocks = range(8), range(4), range(2)
        rowmax = lambda t: jnp.max(t, axis=1, keepdims=True)
        rowsum = lambda t: jnp.sum(t, axis=1, keepdims=True)
        q2 = [q_ref[:, g * PAIR:(g + 1) * PAIR] for g in pairs]
        s2 = [[_nt(q2[g], ks[b][g // 2]) for b in blocks] for g in pairs]
        sc = [[jnp.where(band[b], _lane_half(s2[h // 2][b], h) * SCALE + bias_ref[h, b], NEG_INF) for b in blocks] for h in heads]
        sink = [sink_ref[0, h] for h in heads]
        m = [jnp.maximum(jnp.maximum(rowmax(sc[h][0]), rowmax(sc[h][1])), sink[h]) for h in heads]
        e = [[jnp.exp(sc[h][b] - m[h]) for b in blocks] for h in heads]
        den = [rowsum(e[h][0]) + rowsum(e[h][1]) + jnp.exp(sink[h] - m[h]) for h in heads]
        pb = [[(e[h][b] / den[h]).astype(BF16) for b in blocks] for h in heads]
        for g in pairs:
            both = lambda b: jnp.concatenate([pb[2 * g][b], pb[2 * g + 1][b]], axis=1)
            o_ref[:, g * PAIR:(g + 1) * PAIR] = _nn(both(0), vs[0][g // 2]) + _nn(both(1), vs[1][g // 2])
        for h in heads:
            lse_ref[:, h * QB:(h + 1) * QB] = jnp.broadcast_to(m[h] + jnp.log(den[h]), (QB, QB))

    kv = lambda col, prev: pl.BlockSpec((QB, KV_W), (lambda i: (jnp.maximum(i - 1, 0), col)) if prev else (lambda i: (i, col)))
    full = pl.BlockSpec((QB, QB), lambda i: (0, 0))
    smem = pl.BlockSpec(memory_space=pltpu.SMEM)
    return _call(
        body, name=name, grid=(nq,), out_shape=(S((T, SWA_W), F32), S((T, 8 * QB), F32)),
        in_specs=[pl.BlockSpec((QB, SWA_W), lambda i: (i, 3)), kv(kcol, True), kv(kcol, False), kv(vcol, True), kv(vcol, False),
                  full, full, smem, smem],
        out_specs=(pl.BlockSpec((QB, SWA_W), lambda i: (i, 0)), pl.BlockSpec((QB, 8 * QB), lambda i: (i, 0))),
        scratch=[pltpu.VMEM((8, 2, QB, QB), F32)],
        sem=("arbitrary",), args=(p, p, p, p, p, bprev, bcur, sinks, rel_bias), riders=riders)


def swa_bwd(p, do, lse, sinks, rel_bias, bprev, bcur, name, riders=()):
    T = p.shape[0]
    nq = T // QB
    kcol, vcol = (3 * SB_W + SWA_W) // KV_W, (3 * SB_W + SWA_W) // KV_W + 1

    def body(q_ref, kp_ref, kc_ref, vp_ref, vc_ref, do_ref, lse_ref, bp_ref, bc_ref, sink_ref, rb_ref,
             dq_ref, dk_ref, dv_ref, dsink_ref, dsc_ref, bias_ref, dk_acc, dv_acc):
        i = pl.program_id(0)
        lo, band, ks, vs = _swa_common(i, kp_ref, kc_ref, vp_ref, vc_ref, bp_ref, bc_ref, rb_ref, bias_ref)

        @pl.when(i == 0)
        def _():
            dk_acc[...] = jnp.zeros_like(dk_acc)
            dv_acc[...] = jnp.zeros_like(dv_acc)
            dsc_ref[...] = jnp.zeros_like(dsc_ref)
            dsink_ref[...] = jnp.zeros_like(dsink_ref)

        heads, pairs, blocks = range(8), range(4), range(2)
        rowsum = lambda t: jnp.sum(t, axis=1, keepdims=True)
        by_head = lambda t: jnp.concatenate([jnp.where(lo, t, 0), jnp.where(lo, 0, t)], axis=0)
        q2 = [q_ref[:, g * PAIR:(g + 1) * PAIR] for g in pairs]
        d2 = [do_ref[:, g * PAIR:(g + 1) * PAIR].astype(BF16) for g in pairs]
        qs = [by_head(q2[g]) for g in pairs]
        dos = [by_head(d2[g]) for g in pairs]
        s2 = [[_nt(q2[g], ks[b][g // 2]) for b in blocks] for g in pairs]
        dp2 = [[_nt(d2[g], vs[b][g // 2]) for b in blocks] for g in pairs]
        lse_h = [lse_ref[:, h * QB:(h + 1) * QB] for h in heads]
        sink = [sink_ref[0, h] for h in heads]
        pr = [[jnp.exp(jnp.where(band[b], _lane_half(s2[h // 2][b], h) * SCALE + bias_ref[h, b], NEG_INF) - lse_h[h])
               for b in blocks] for h in heads]
        dp = [[_lane_half(dp2[h // 2][b], h) for b in blocks] for h in heads]
        delta = [rowsum(pr[h][0] * dp[h][0]) + rowsum(pr[h][1] * dp[h][1]) for h in heads]
        lane1 = lax.broadcasted_iota(jnp.int32, (1, QB), 1)
        dsink = jnp.zeros((1, QB), F32)
        for h in heads:
            dsink = dsink + jnp.where(lane1 == h, -jnp.sum(jnp.exp(sink[h] - lse_h[h][:, :1]) * delta[h]), 0.0)
        dsink_ref[...] += dsink
        dsc = [[pr[h][b] * (dp[h][b] - delta[h]) for b in blocks] for h in heads]
        for h in heads:
            for b in blocks:
                dsc_ref[h, b] += dsc[h][b]
        dzb = [[(dsc[h][b] * SCALE).astype(BF16) for b in blocks] for h in heads]
        prb = [[pr[h][b].astype(BF16) for b in blocks] for h in heads]
        pair_of = lambda t, g, b, axis: jnp.concatenate([t[2 * g][b], t[2 * g + 1][b]], axis=axis)
        for g in pairs:
            dq = _nn(pair_of(dzb, g, 0, 1), ks[0][g // 2]) + _nn(pair_of(dzb, g, 1, 1), ks[1][g // 2])
            dq_ref[:, g * PAIR:(g + 1) * PAIR] = dq.astype(BF16)

        def key_grad(t, other, b):
            per_kv = [_tn(pair_of(t, 2 * kh, b, 0), other[2 * kh]) + _tn(pair_of(t, 2 * kh + 1, b, 0), other[2 * kh + 1]) for kh in range(2)]
            both = [s + pltpu.roll(s, HEAD_DIM, 1) for s in per_kv]
            return jnp.where(lo, both[0], both[1])

        rp = pl.multiple_of(jnp.maximum(i - 1, 0) * QB, QB)
        rc = pl.multiple_of(i * QB, QB)
        dk_acc[pl.ds(rp, QB), :] += key_grad(dzb, qs, 0)
        dv_acc[pl.ds(rp, QB), :] += key_grad(prb, dos, 0)
        dk_acc[pl.ds(rc, QB), :] += key_grad(dzb, qs, 1)
        dv_acc[pl.ds(rc, QB), :] += key_grad(prb, dos, 1)

        @pl.when(i == nq - 1)
        def _():
            dk_ref[...] = dk_acc[...].astype(BF16)
            dv_ref[...] = dv_acc[...].astype(BF16)

    kv = lambda col, prev: pl.BlockSpec((QB, KV_W), (lambda i: (jnp.maximum(i - 1, 0), col)) if prev else (lambda i: (i, col)))
    full = pl.BlockSpec((QB, QB), lambda i: (0, 0))
    smem = pl.BlockSpec(memory_space=pltpu.SMEM)
    whole = lambda shape: pl.BlockSpec(shape, lambda i: (0,) * len(shape))
    return _call(
        body, name=name, grid=(nq,),
        out_shape=(S((T, SWA_W), BF16), S((T, KV_W), BF16), S((T, KV_W), BF16), S((1, QB), F32), S((8, 2, QB, QB), F32)),
        in_specs=[pl.BlockSpec((QB, SWA_W), lambda i: (i, 3)), kv(kcol, True), kv(kcol, False), kv(vcol, True), kv(vcol, False),
                  pl.BlockSpec((QB, SWA_W), lambda i: (i, 0)), pl.BlockSpec((QB, 8 * QB), lambda i: (i, 0)),
                  full, full, smem, smem],
        out_specs=(pl.BlockSpec((QB, SWA_W), lambda i: (i, 0)), whole((T, KV_W)), whole((T, KV_W)), whole((1, QB)),
                   whole((8, 2, QB, QB))),
        scratch=[pltpu.VMEM((8, 2, QB, QB), F32), pltpu.VMEM((T, KV_W), F32), pltpu.VMEM((T, KV_W), F32)],
        sem=("arbitrary",), args=(p, p, p, p, p, do, lse, bprev, bcur, sinks, rel_bias), riders=riders)


def mix_out_fwd(o_sb, o_sw, g_sb, g_sw, wout, h, g_next, name, riders=()):
    T, D = h.shape
    M = SB_W + SWA_W
    tm = _tile(T, 256)

    def body(a_ref, b_ref, ga_ref, gb_ref, w_ref, h_ref, gn_ref, mx_ref, o_ref, n_ref):
        mx_ref[:, :SB_W] = _rms(a_ref[...], ga_ref[...]).astype(BF16)
        mx_ref[:, SB_W:] = _rms(b_ref[...], gb_ref[...]).astype(BF16)
        out = h_ref[...] + _nn(mx_ref[...], w_ref[...])
        o_ref[...] = out
        n_ref[...] = _rms(out, gn_ref[...]).astype(BF16)

    row = lambda n: pl.BlockSpec((tm, n), lambda i: (i, 0))
    vec = lambda n: pl.BlockSpec((1, n), lambda i: (0, 0))
    return _call(
        body, name=name, grid=(T // tm,), out_shape=(S((T, M), BF16), S((T, D), F32), S((T, D), BF16)),
        in_specs=[row(SB_W), row(SWA_W), vec(SB_W), vec(SWA_W), pl.BlockSpec((M, D), lambda i: (0, 0)), row(D), vec(D)],
        out_specs=(row(M), row(D), row(D)),
        sem=("parallel",), args=(o_sb, o_sw, g_sb, g_sw, wout, h, g_next), riders=riders)


def loss_head(h, g, target, name):
    T, D = h.shape
    tm = _tile(T, 256)

    def body(h_ref, g_ref, t_ref, loss_ref, dh_ref, dhb_ref, dg_ref):
        @pl.when(pl.program_id(0) == 0)
        def _():
            loss_ref[...] = jnp.zeros_like(loss_ref)
            dg_ref[...] = jnp.zeros_like(dg_ref)
        x = h_ref[...]
        err = _rms(x, g_ref[...]) - t_ref[...]
        loss_ref[...] += jnp.full((1, QB), 0.5 * jnp.sum(jnp.mean(err * err, axis=-1)), F32)
        dx, dg = _rms_bwd(err / D, x, g_ref[...])
        dh_ref[...] = dx
        dhb_ref[...] = dx.astype(BF16)
        dg_ref[...] += dg

    row = pl.BlockSpec((tm, D), lambda i: (i, 0))
    vec = pl.BlockSpec((1, D), lambda i: (0, 0))
    return pl.pallas_call(
        body, name=name, grid=(T // tm,), out_shape=(S((1, QB), F32), S((T, D), F32), S((T, D), BF16), S((1, D), F32)),
        in_specs=[row, vec, row], out_specs=(pl.BlockSpec((1, QB), lambda i: (0, 0)), row, row, vec),
        compiler_params=_params(("arbitrary",)),
    )(h, g, target)


def ffn_down_bwd(dhb, wd, gate, up, a, n, name, riders=()):
    T, D = dhb.shape
    F = wd.shape[0]
    tr, tn = _tile(T, 512), _tile(F, 256)

    def body(d_ref, n_ref, w_ref, g_ref, u_ref, a_ref, o_ref, dwd_ref, dwdb_ref, dwgu_ref, dwgub_ref):
        w = w_ref[...]
        for r in range(T // tr):
            rows = slice(r * tr, (r + 1) * tr)
            da = 0.5 * _nt(d_ref[rows, :], w)
            o_ref[0, rows, :] = (da * g_ref[rows, :].astype(F32)).astype(BF16)
            o_ref[1, rows, :] = (da * u_ref[rows, :].astype(F32)).astype(BF16)
        dwd = 0.5 * _tn(a_ref[...], d_ref[...])
        dwd_ref[...] = dwd
        dwdb_ref[...] = dwd.astype(BF16)
        for s in range(2):
            dwgu = _tn(o_ref[s], n_ref[...])
            dwgu_ref[s] = dwgu
            dwgub_ref[s] = dwgu.astype(BF16)

    tile = pl.BlockSpec((T, tn), lambda j: (0, j))
    whole = pl.BlockSpec((T, D), lambda j: (0, 0))
    rows1, rows2 = pl.BlockSpec((tn, D), lambda j: (j, 0)), pl.BlockSpec((2, tn, D), lambda j: (0, j, 0))
    return _call(
        body, name=name, grid=(F // tn,),
        out_shape=(S((2, T, F), BF16), S((F, D), F32), S((F, D), BF16), S((2, F, D), F32), S((2, F, D), BF16)),
        in_specs=[whole, whole, rows1, tile, tile, tile],
        out_specs=(pl.BlockSpec((2, T, tn), lambda j: (0, 0, j)), rows1, rows1, rows2, rows2),
        sem=("parallel",), args=(dhb, n, wd, gate, up, a), riders=riders)


def tn_matmul(xs, y, alpha, name, riders=()):
    B, T, N = xs.shape
    D = y.shape[1]
    tn = _tile(N, 256)

    def body(x_ref, y_ref, o_ref, ob_ref):
        o = alpha * _tn(x_ref[...], y_ref[...])
        o_ref[...] = o
        ob_ref[...] = o.astype(BF16)

    tile = pl.BlockSpec((None, tn, D), lambda s, j: (s, j, 0))
    return _call(
        body, name=name, grid=(B, N // tn), out_shape=(S((B, N, D), F32), S((B, N, D), BF16)),
        in_specs=[pl.BlockSpec((None, T, tn), lambda s, j: (s, 0, j)), pl.BlockSpec((T, D), lambda s, j: (0, 0))],
        out_specs=(tile, tile), sem=("parallel", "parallel"), args=(xs, y), riders=riders)


def nn_rms_bwd(xs, ws, h_in, g, dh, name, riders=()):
    B, T, K = xs.shape
    D = ws.shape[2]
    tm = _tile(T, 256)

    def body(x_ref, w_ref, h_ref, g_ref, d_ref, o_ref, ob_ref, dg_ref):
        @pl.when(pl.program_id(0) == 0)
        def _():
            dg_ref[...] = jnp.zeros_like(dg_ref)
        dn = _nn(x_ref[0], w_ref[0])
        for s in range(1, B):
            dn = dn + _nn(x_ref[s], w_ref[s])
        dx, dg = _rms_bwd(dn, h_ref[...], g_ref[...])
        out = d_ref[...] + dx
        o_ref[...] = out
        ob_ref[...] = out.astype(BF16)
        dg_ref[...] += dg

    row = pl.BlockSpec((tm, D), lambda i: (i, 0))
    vec = pl.BlockSpec((1, D), lambda i: (0, 0))
    return _call(
        body, name=name, grid=(T // tm,), out_shape=(S((T, D), F32), S((T, D), BF16), S((1, D), F32)),
        in_specs=[pl.BlockSpec((B, tm, K), lambda i: (0, i, 0)), pl.BlockSpec((B, K, D), lambda i: (0, 0, 0)), row, vec, row],
        out_specs=(row, row, vec),
        sem=("arbitrary",), args=(xs, ws, h_in, g, dh), riders=riders)


def mix_out_bwd(dhb, wout, o_sb, o_sw, g_sb, g_sw, name):
    T, D = dhb.shape
    tm = _tile(T, 256)

    def body(d_ref, w_ref, a_ref, b_ref, ga_ref, gb_ref, da_ref, db_ref, dga_ref, dgb_ref):
        @pl.when(pl.program_id(0) == 0)
        def _():
            dga_ref[...] = jnp.zeros_like(dga_ref)
            dgb_ref[...] = jnp.zeros_like(dgb_ref)
        dm = _nt(d_ref[...], w_ref[...])
        dxa, dga = _rms_bwd(dm[:, :SB_W], a_ref[...], ga_ref[...])
        dxb, dgb = _rms_bwd(dm[:, SB_W:], b_ref[...], gb_ref[...])
        da_ref[...] = dxa
        db_ref[...] = dxb
        dga_ref[...] += dga
        dgb_ref[...] += dgb

    row = lambda n: pl.BlockSpec((tm, n), lambda i: (i, 0))
    vec = lambda n: pl.BlockSpec((1, n), lambda i: (0, 0))
    return pl.pallas_call(
        body, name=name, grid=(T // tm,),
        out_shape=(S((T, SB_W), F32), S((T, SWA_W), F32), S((1, SB_W), F32), S((1, SWA_W), F32)),
        in_specs=[row(D), pl.BlockSpec((SB_W + SWA_W, D), lambda i: (0, 0)), row(SB_W), row(SWA_W), vec(SB_W), vec(SWA_W)],
        out_specs=(row(SB_W), row(SWA_W), vec(SB_W), vec(SWA_W)),
        compiler_params=_params(("arbitrary",)),
    )(dhb, wout, o_sb, o_sw, g_sb, g_sw)


def rel_bias_grad(dscs, bprev, bcur, name):
    n = len(dscs)

    def body(*refs):
        bp_ref, bc_ref, o_ref = refs[n], refs[n + 1], refs[n + 2]
        bks = [bp_ref[...], bc_ref[...]]
        row = lax.broadcasted_iota(jnp.int32, (N_BUCKETS, QB), 0)
        lane = lax.broadcasted_iota(jnp.int32, (N_BUCKETS, QB), 1)
        out = jnp.zeros((N_BUCKETS, QB), F32)
        for h in range(8):
            tot = [sum(refs[l][h, b] for l in range(n)) for b in range(2)]
            for b in range(N_BUCKETS):
                val = jnp.sum(jnp.where(bks[0] == b, tot[0], 0.0)) + jnp.sum(jnp.where(bks[1] == b, tot[1], 0.0))
                out = jnp.where((row == b) & (lane == h), val, out)
        o_ref[...] = out

    return pl.pallas_call(body, name=name, out_shape=S((N_BUCKETS, QB), F32), compiler_params=_params())(*dscs, bprev, bcur)


def _adamw(w, g, m, v):
    m = ADAM_B1 * m + (1.0 - ADAM_B1) * g
    v = ADAM_B2 * v + (1.0 - ADAM_B2) * (g * g)
    m_hat = m / (1.0 - ADAM_B1 ** ADAM_STEP)
    v_hat = v / (1.0 - ADAM_B2 ** ADAM_STEP)
    delta = -ADAM_LR * (m_hat / (jnp.sqrt(v_hat) + ADAM_EPS) + ADAM_WD * w)
    return delta, m, v


def adamw_scattered(w, m, v, owns, others, name, riders=()):
    L, R, C = w.shape
    tr = _rows_tile(R, 176)

    def body(w_ref, m_ref, v_ref, *rest):
        own_refs, other_refs = rest[:L], rest[L:2 * L]
        g_ref, d_ref, mo_ref, vo_ref = rest[2 * L:]
        layer = pl.program_id(0)

        def grad(k):
            o = other_refs[k]
            return own_refs[k][...] + o[0].astype(F32) + o[1].astype(F32) + o[2].astype(F32)

        g = grad(0)
        for k in range(1, L):
            g = jnp.where(layer == k, grad(k), g)
        d, mn, vn = _adamw(w_ref[...], g, m_ref[...], v_ref[...])
        g_ref[...] = g
        d_ref[...] = d
        mo_ref[...] = mn
        vo_ref[...] = vn

    tile = pl.BlockSpec((None, tr, C), lambda l, i: (l, i, 0))
    return _call(
        body, name=name, grid=(L, R // tr), out_shape=(S((L, R, C), F32),) * 4,
        in_specs=[tile] * 3 + [pl.BlockSpec((tr, C), lambda l, i: (i, 0))] * L + [pl.BlockSpec((3, tr, C), lambda l, i: (0, i, 0))] * L,
        out_specs=(tile,) * 4, sem=("parallel", "parallel"), args=(w, m, v, *owns, *others), riders=riders)


def adamw_small(w, gs, m, v, name):
    R, C = w.shape

    def body(w_ref, g_ref, m_ref, v_ref, go_ref, d_ref, mo_ref, vo_ref):
        g = g_ref[0]
        for k in range(1, N_DEV):
            g = g + g_ref[k]
        d, mn, vn = _adamw(w_ref[...], g, m_ref[...], v_ref[...])
        go_ref[...] = g
        d_ref[...] = d
        mo_ref[...] = mn
        vo_ref[...] = vn

    return pl.pallas_call(body, name=name, out_shape=(S((R, C), F32),) * 4, compiler_params=_params())(w, gs, m, v)


def kernel(x, norm_ffn1, w_ffn1_gu, w_ffn1_down, norm_mix, w_in, sinks, norm_out_sb, norm_out_swa, w_out, norm_ffn2, w_ffn2_gu, w_ffn2_down, rel_bias, norm_final, loss_target, m_norm_ffn1, m_w_ffn1_gu, m_w_ffn1_down, m_norm_mix, m_w_in, m_sinks, m_norm_out_sb, m_norm_out_swa, m_w_out, m_norm_ffn2, m_w_ffn2_gu, m_w_ffn2_down, m_rel_bias, m_norm_final, v_norm_ffn1, v_w_ffn1_gu, v_w_ffn1_down, v_norm_mix, v_w_in, v_sinks, v_norm_out_sb, v_norm_out_swa, v_w_out, v_norm_ffn2, v_w_ffn2_gu, v_w_ffn2_down, v_rel_bias, v_norm_final):
    L = norm_ffn1.shape[0]
    T, D = x.shape[1], x.shape[2]
    F = w_ffn1_down.shape[1] * N_DEV
    h = x.reshape(T, D)
    target = loss_target.reshape(T, D)
    after, upto, before = _tri_consts()
    bprev, bcur = _t5_buckets()

    local = {}
    for l in range(L):
        local[f"gu1_{l}"] = w_ffn1_gu[l].T.astype(BF16)
        local[f"d1_{l}"] = w_ffn1_down[l].astype(BF16)
        local[f"in_{l}"] = w_in[l].T.astype(BF16)
        local[f"out_{l}"] = w_out[l].astype(BF16)
        local[f"gu2_{l}"] = w_ffn2_gu[l].T.astype(BF16)
        local[f"d2_{l}"] = w_ffn2_down[l].astype(BF16)
    full, partial = {}, {}
    grads, chip_sum, recv_b = {}, {}, {}

    def run(fn, *args, ag=(), rs1=(), rs2=()):
        halves = lambda names: [n if isinstance(n, tuple) else (n, None) for n in names]
        ag, rs2 = [(n, k) for n, k in halves(ag) if n in local], halves(rs2)
        rows = lambda k, total: None if k is None else (k * (total // 2), total // 2)

        def second(n, k):
            sb = chip_sum[n][1]
            return scatter_second(sb, rows(k, sb.shape[1]), recv_b.get(n))

        riders = ([gather(local[n], rows(k, local[n].shape[0]), partial.get(n)) for n, k in ag]
                  + [scatter_first(grads[n][1]) for n in rs1] + [second(n, k) for n, k in rs2])
        if not riders:
            return fn(*args)
        outs, per = fn(*args, riders=riders)
        per = [p[0] for p in per]
        for n, k in ag:
            buf = per.pop(0)
            if k == 0:
                partial[n] = buf
            else:
                full[n] = buf.reshape(N_DEV * buf.shape[1], D)
        if rs1:
            sums = scatter_add([grads[n][0] for n in rs1], [per.pop(0) for n in rs1], "rs_add_" + "_".join(rs1))
            chip_sum.update(zip(rs1, sums))
        for n, _ in rs2:
            recv_b[n] = per.pop(0)
        return outs

    def attn_fwd(p, sink, name, riders=()):
        return side_by_side(sb_attn_fwd(p, after, name, riders=PARTS), swa_fwd(p, sink, rel_bias, bprev, bcur, name, riders=PARTS),
                            name, riders)

    def attn_bwd(p, do_sb, tot, do_sw, lse, sink, name, riders=()):
        return side_by_side(sb_attn_bwd(p, do_sb, tot, upto, before, name, riders=PARTS),
                            swa_bwd(p, do_sw, lse, sink, rel_bias, bprev, bcur, name, riders=PARTS), name, riders)

    gu = lambda n: full[n].reshape(2, F, D)
    slots = lambda pair: tuple(t.reshape(N_DEV, -1, D) for t in pair)
    vec = lambda a: a.reshape(1, -1)

    PW = max(D, SB_W + SWA_W)
    n_rows = 4 * L + 2
    n_rows += (-n_rows) % 8

    def pack(ffn1, mix, ffn2, final, osb, osw, snk, rel, extra):
        pieces = []

        def row(*parts):
            flat = [a.reshape(-1) for a in parts]
            pieces.extend(flat)
            used = sum(a.size for a in flat)
            if used < PW:
                pieces.append(jnp.zeros((PW - used,), F32))

        for group in (ffn1, mix, ffn2):
            for l in range(L):
                row(group[l])
        row(final)
        for l in range(L):
            row(osb[l], osw[l])
        row(*[snk[l].reshape(-1)[:8] for l in range(L)], rel, extra)
        pieces.append(jnp.zeros(((n_rows - 4 * L - 2) * PW,), F32))
        return jnp.concatenate(pieces).reshape(n_rows, PW)

    def unpack(arr):
        ffn1, mix, ffn2 = arr[0:L, :D], arr[L:2 * L, :D], arr[2 * L:3 * L, :D]
        final = arr[3 * L, :D]
        ob = arr[3 * L + 1:4 * L + 1]
        tail = arr[4 * L + 1]
        return (ffn1, mix, tail[:8 * L].reshape(L, 8), ob[:, :SB_W], ob[:, SB_W:SB_W + SWA_W], ffn2,
                tail[8 * L:8 * L + N_BUCKETS * 8].reshape(N_BUCKETS, 8), final)

    zero = jnp.zeros((1,), F32)
    w_small = pack(norm_ffn1, norm_mix, norm_ffn2, norm_final, norm_out_sb, norm_out_swa, sinks, rel_bias, zero)
    norm_ffn1, norm_mix, sinks, norm_out_sb, norm_out_swa, norm_ffn2, _, norm_final = unpack(w_small)

    saved = []
    n_next = run(rms_cast, h, vec(norm_ffn1[0]), "rms_first", ag=("gu1_0",))
    for l in range(L):
        nx = l + 1
        s = {"h0": h, "n1": n_next}
        s["gate1"], s["up1"], s["a1"] = run(ffn_up_fwd, s["n1"], gu(f"gu1_{l}"), f"ffn1_up{l}",
                                            ag=(f"d1_{l}", ("in_0", 0) if l == 0 else (f"in_{l}", 1)))
        h = run(ffn_down_fwd, s["a1"], full[f"d1_{l}"], h, None, f"ffn1_down{l}", ag=(("in_0", 1),) if l == 0 else ())
        s["h1"] = h
        s["n2"], s["p"] = mix_in_fwd(h, vec(norm_mix[l]), full[f"in_{l}"], f"mix_in{l}")
        s["o_sb"], s["tot"], s["o_sw"], s["lse"] = run(attn_fwd, s["p"], vec(sinks[l]), f"attn_fwd{l}",
                                                       ag=(f"out_{l}", f"gu2_{l}", f"d2_{l}", (f"gu1_{nx}", 0)))
        s["mixed"], h, s["n3"] = run(mix_out_fwd, s["o_sb"], s["o_sw"], vec(norm_out_sb[l]), vec(norm_out_swa[l]),
                                     full[f"out_{l}"], h, vec(norm_ffn2[l]), f"mix_out{l}")
        s["h2"] = h
        s["gate2"], s["up2"], s["a2"] = run(ffn_up_fwd, s["n3"], gu(f"gu2_{l}"), f"ffn2_up{l}",
                                            ag=((f"gu1_{nx}", 1), (f"in_{nx}", 0)))
        if nx < L:
            h, n_next = run(ffn_down_fwd, s["a2"], full[f"d2_{l}"], h, vec(norm_ffn1[nx]), f"ffn2_down{l}")
        else:
            h = run(ffn_down_fwd, s["a2"], full[f"d2_{l}"], h, None, f"ffn2_down{l}")
        saved.append(s)

    loss_part, dh, dhb, dg_final = loss_head(h, vec(norm_final), target, "loss_head")

    small = {k: [None] * L for k in ("ffn1", "mix", "sinks", "osb", "osw", "ffn2", "dsc")}
    for l in reversed(range(L)):
        s = saved[l]

        def ffn_bwd(dh, dhb, tag, gate, up, a, n, h_in, g, r_down, r_up):
            gu_n, d_n = f"gu{tag}_{l}", f"d{tag}_{l}"
            dgu, dwd, dwdb, dwgu, dwgub = run(ffn_down_bwd, dhb, full[d_n], gate, up, a, n, f"ffn{tag}_down_bwd{l}", **r_down)
            grads[gu_n], grads[d_n] = slots((dwgu, dwgub)), slots((dwd, dwdb))
            return run(nn_rms_bwd, dgu, gu(gu_n), h_in, g, dh, f"ffn{tag}_up_bwd{l}", **r_up)

        later = l + 1 < L
        dh, dhb, small["ffn2"][l] = ffn_bwd(dh, dhb, 2, s["gate2"], s["up2"], s["a2"], s["n3"], s["h2"], vec(norm_ffn2[l]),
                                            dict(rs2=((f"gu1_{l + 1}", 0), f"d1_{l + 1}") if later else ()),
                                            dict(rs1=(f"gu2_{l}", f"d2_{l}"), rs2=((f"gu1_{l + 1}", 1),) if later else ()))
        do_sb, do_sw, small["osb"][l], small["osw"][l] = mix_out_bwd(
            dhb, full[f"out_{l}"], s["o_sb"], s["o_sw"], vec(norm_out_sb[l]), vec(norm_out_swa[l]), f"mix_out_bwd{l}")
        grads[f"out_{l}"] = slots(tn_matmul(s["mixed"][None], dhb, 1.0, f"dwout{l}"))
        dq_sb, dk_sb, dv_sb, dq_sw, dk_sw, dv_sw, small["sinks"][l], small["dsc"][l] = run(
            attn_bwd, s["p"], do_sb, s["tot"], do_sw, s["lse"], vec(sinks[l]), f"attn_bwd{l}",
            rs2=(f"gu2_{l}", f"d2_{l}"), rs1=(f"out_{l}",))
        dp = jnp.concatenate([dq_sb, dk_sb, dv_sb, dq_sw, dk_sw, dv_sw], axis=1)
        dh, dhb, small["mix"][l] = nn_rms_bwd(dp[None], full[f"in_{l}"][None], s["h1"], vec(norm_mix[l]), dh, f"mix_in_bwd{l}")
        grads[f"in_{l}"] = slots(tn_matmul(dp[None], s["n2"], 1.0, f"dwin{l}"))
        dh, dhb, small["ffn1"][l] = ffn_bwd(dh, dhb, 1, s["gate1"], s["up1"], s["a1"], s["n1"], s["h0"], vec(norm_ffn1[l]),
                                            dict(rs1=(f"in_{l}",), rs2=(f"out_{l}",)),
                                            dict(rs1=(f"gu1_{l}", f"d1_{l}"), rs2=(f"in_{l}",)))

    grad_x = dh.reshape(x.shape)

    upd = {}
    for nm, w, m, v, transposed, last in (
            ("gu2", w_ffn2_gu, m_w_ffn2_gu, v_w_ffn2_gu, True, (("gu1_0", 0),)), ("d2", w_ffn2_down, m_w_ffn2_down, v_w_ffn2_down, False, (("gu1_0", 1),)),
            ("in", w_in, m_w_in, v_w_in, True, ("d1_0",)), ("out", w_out, m_w_out, v_w_out, False, ()),
            ("gu1", w_ffn1_gu, m_w_ffn1_gu, v_w_ffn1_gu, True, ()), ("d1", w_ffn1_down, m_w_ffn1_down, v_w_ffn1_down, False, ())):
        turn = (lambda a: jnp.swapaxes(a, 1, 2)) if transposed else (lambda a: a)
        names = [f"{nm}_{l}" for l in range(L)]
        res = run(adamw_scattered, turn(w), turn(m), turn(v), [chip_sum[n][0] for n in names], [recv_b[n] for n in names],
                  f"adamw_{nm}", rs2=last)
        upd[nm] = tuple(turn(r) for r in res)

    d_rel = rel_bias_grad(small["dsc"], bprev, bcur, "rel_bias_grad")[:, :8]
    g_small = pack(small["ffn1"], small["mix"], small["ffn2"], dg_final, small["osb"], small["osw"], small["sinks"], d_rel,
                   loss_part[0, :1])
    m_small = pack(m_norm_ffn1, m_norm_mix, m_norm_ffn2, m_norm_final, m_norm_out_sb, m_norm_out_swa, m_sinks, m_rel_bias, zero)
    v_small = pack(v_norm_ffn1, v_norm_mix, v_norm_ffn2, v_norm_final, v_norm_out_sb, v_norm_out_swa, v_sinks, v_rel_bias, zero)
    gs_small = all_gather_rows(g_small, "ag_small")
    summed = adamw_small(w_small, gs_small, m_small, v_small, "adamw_small")
    small_out = [unpack(a) for a in summed]
    loss = summed[0][4 * L + 1, 8 * L + N_BUCKETS * 8]

    def group(k):
        sm = small_out[k]
        return (sm[0], upd["gu1"][k], upd["d1"][k], sm[1], upd["in"][k], sm[2], sm[3], sm[4], upd["out"][k], sm[5],
                upd["gu2"][k], upd["d2"][k], sm[6], sm[7])

    return (loss, grad_x, *group(0), *group(1), *group(2), *group(3))
```

```python
import math

import jax
import jax.numpy as jnp
from jax import lax
from jax.experimental import pallas as pl
from jax.experimental.pallas import tpu as pltpu

F32 = jnp.float32
BF16 = jnp.bfloat16
S = jax.ShapeDtypeStruct

N_DEV = 8
HEAD_DIM = 64
SB_HEADS = 8
PAIR = 2 * HEAD_DIM
SB_W = 512
SWA_W = 512
KV_W = 128
IN_W = 3 * SB_W + SWA_W + 2 * KV_W
QB = 128
N_BUCKETS = 32
MAX_DISTANCE = 128
EPS = 1e-6
NEG_INF = -1e30
SCALE = HEAD_DIM ** -0.5

ADAM_LR = 0.001
ADAM_B1 = 0.9
ADAM_B2 = 0.999
ADAM_EPS = 1e-08
ADAM_WD = 0.01
ADAM_STEP = 10

VMEM_LIMIT = 56 * 1024 * 1024
MESH = pl.DeviceIdType.MESH


def _params(sem=None, vmem=VMEM_LIMIT):
    return pltpu.CompilerParams(dimension_semantics=sem, vmem_limit_bytes=vmem)


def _nn(a, b):
    return jnp.dot(a, b, preferred_element_type=F32)


def _nt(a, b):
    return lax.dot_general(a, b, (((1,), (1,)), ((), ())), preferred_element_type=F32)


def _tn(a, b):
    return lax.dot_general(a, b, (((0,), (0,)), ((), ())), preferred_element_type=F32)


def _tri(xs, m):
    return [_nn(x.astype(BF16), m) for x in xs]


def _rms(x, g):
    r = lax.rsqrt(jnp.mean(x * x, axis=-1, keepdims=True) + EPS)
    return x * r * g


def _rms_bwd(dy, x, g):
    r = lax.rsqrt(jnp.mean(x * x, axis=-1, keepdims=True) + EPS)
    xhat = x * r
    u = dy * g
    dx = r * (u - xhat * jnp.mean(u * xhat, axis=-1, keepdims=True))
    return dx, jnp.sum(dy * xhat, axis=0, keepdims=True)


def _softplus_logsig(z):
    sp = jnp.maximum(z, 0.0) + jnp.log(1.0 + jnp.exp(-jnp.abs(z)))
    return sp, z - sp


def _tile(n, want):
    t = min(n, want)
    while n % t:
        t //= 2
    return t


def _place():
    x, y, c = lax.axis_index("x"), lax.axis_index("y"), lax.axis_index("c")
    chips = [(1 - x, y), (x, 1 - y), (1 - x, 1 - y)]
    return x, y, c, chips


def all_gather_rows(v, name):
    R, C = v.shape

    def body(v_ref, out_ref, send_sems, recv_sems, local_sem):
        x, y, c, chips = _place()
        me, sibling = (x, y, c), (x, y, 1 - c)

        def slot(px, py, pc):
            return out_ref.at[4 * px + 2 * py + pc]

        def copy(k, block, to, src=None):
            return pltpu.make_async_remote_copy(
                src_ref=slot(*block) if src is None else src, dst_ref=slot(*block),
                send_sem=send_sems.at[k], recv_sem=recv_sems.at[k], device_id=to, device_id_type=MESH)

        mine = pltpu.make_async_copy(v_ref, slot(*me), local_sem)
        mine.start()
        first = [copy(0, me, sibling, src=v_ref)]
        first += [copy(1 + j, me, (*chip, c), src=v_ref) for j, chip in enumerate(chips)]
        for cp in first:
            cp.start()
        passed = [copy(4 + j, (*chip, c), sibling) for j, chip in enumerate(chips)]
        for j, chip in enumerate(chips):
            copy(1 + j, (*chip, c), me).wait_recv()
            passed[j].start()
        copy(0, sibling, me).wait_recv()
        for j, chip in enumerate(chips):
            copy(4 + j, (*chip, 1 - c), me).wait_recv()
        for cp in first + passed:
            cp.wait_send()
        mine.wait()

    return pl.pallas_call(
        body, name=name, out_shape=S((N_DEV, R, C), v.dtype),
        in_specs=[pl.BlockSpec(memory_space=pl.ANY)], out_specs=pl.BlockSpec(memory_space=pl.ANY),
        scratch_shapes=[pltpu.SemaphoreType.DMA((7,)), pltpu.SemaphoreType.DMA((7,)), pltpu.SemaphoreType.DMA],
    )(v)


class _Exchange:
    def __init__(self, ins, outs, sizes, n_local, plan, aliases=None):
        self.ins, self.outs, self.plan, self.aliases = list(ins), list(outs), plan, aliases or {}
        self.sizes, self.n_local = list(sizes), n_local

    def scratch(self):
        n = sum(self.sizes)
        return [pltpu.SemaphoreType.DMA((n,)), pltpu.SemaphoreType.DMA((n,)), pltpu.SemaphoreType.DMA((max(self.n_local, 1),))]

    def _copies(self, in_refs, out_refs, sems):
        send_sems, recv_sems, local_sems = sems
        phases, local = self.plan(in_refs, out_refs)
        out, k = [], 0
        for phase in phases:
            out.append([pltpu.make_async_remote_copy(src_ref=s, dst_ref=d, send_sem=send_sems.at[k + n], recv_sem=recv_sems.at[k + n],
                                                     device_id=dev, device_id_type=MESH) for n, (s, d, dev) in enumerate(phase)])
            k += len(phase)
        return out, [pltpu.make_async_copy(s, d, local_sems.at[n]) for n, (s, d) in enumerate(local)]

    def start(self, in_refs, out_refs, sems):
        phases, loc = self._copies(in_refs, out_refs, sems)
        for cp in phases[0] + loc:
            cp.start()

    def advance(self, hook, in_refs, out_refs, sems):
        p = hook - (3 - len(self.sizes))
        if p >= 1:
            phases, _ = self._copies(in_refs, out_refs, sems)
            for cp in phases[p - 1]:
                cp.wait_recv()
            for cp in phases[p]:
                cp.start()

    def finish(self, in_refs, out_refs, sems):
        phases, loc = self._copies(in_refs, out_refs, sems)
        for cp in phases[-1]:
            cp.wait_recv()
        for phase in phases:
            for cp in phase:
                cp.wait_send()
        for cp in loc:
            cp.wait()


def gather(v, rows=None, into=None):
    R, C = v.shape
    r0, nr = rows or (0, R)
    na = min(nr, ((nr // 2 + 15) // 16) * 16)

    def plan(ins, outs):
        x, y, c, _ = _place()
        xn, yn, dg, sibling = (1 - x, y), (x, 1 - y), (1 - x, 1 - y), (x, y, 1 - c)
        slot = lambda chip, start=r0, count=nr: outs[0].at[4 * chip[0] + 2 * chip[1] + c, pl.ds(start, count), :]
        src, mine = ins[0].at[pl.ds(r0, nr), :], slot((x, y))
        same = lambda ref, to: (ref, ref, to)
        first = [(src, mine, sibling), (src, mine, (*xn, c)), (src, mine, (*yn, c))]
        relay = [same(slot(xn, r0, na), (*yn, c)), same(slot(yn, r0 + na, nr - na), (*xn, c))]
        onward = [same(slot(xn), sibling), same(slot(yn), sibling), same(slot(dg), sibling)]
        return [first, relay, onward], [(src, mine)]

    if into is None:
        return _Exchange([v], [S((N_DEV, R, C), v.dtype)], (3, 2, 3), 1, plan)
    return _Exchange([v, into], [S((N_DEV, R, C), v.dtype)], (3, 2, 3), 1, plan, aliases={1: 0})


def scatter_first(gb):
    _, R, C = gb.shape

    def plan(ins, outs):
        x, y, c, chips = _place()
        owners = [(x, y)] + chips
        return [[(ins[0].at[4 * px + 2 * py + (1 - c)], outs[0].at[j], (x, y, 1 - c)) for j, (px, py) in enumerate(owners)]], []

    return _Exchange([gb], [S((4, R, C), BF16)], (4,), 0, plan)


def scatter_second(sb, rows=None, into=None):
    r0, nr = rows or (0, sb.shape[1])

    def plan(ins, outs):
        x, y, c, chips = _place()
        part = lambda ref, j: ref.at[j, pl.ds(r0, nr), :]
        return [[(part(ins[0], j), part(outs[0], j), (*chips[j], c)) for j in range(3)]], []

    if into is None:
        return _Exchange([sb], [S(sb.shape, BF16)], (3,), 0, plan)
    return _Exchange([sb, into], [S(sb.shape, BF16)], (3,), 0, plan, aliases={1: 0})


PARTS = "parts"


def _call(body, *, name, grid, in_specs, out_specs, out_shape, args, scratch=(), sem=None, riders=(), marks=None):
    single = not isinstance(out_shape, (tuple, list))
    out_shape = (out_shape,) if single else tuple(out_shape)
    out_specs = (out_specs,) if single else tuple(out_specs)
    n_in, n_out, n_sc = len(in_specs), len(out_shape), len(scratch)
    if riders is PARTS:
        return dict(body=body, grid=grid, in_specs=list(in_specs), out_specs=out_specs, out_shape=out_shape, args=tuple(args),
                    scratch=list(scratch), marks=marks)
    if not riders:
        res = pl.pallas_call(body, name=name, grid=grid, in_specs=list(in_specs), out_specs=out_specs, out_shape=out_shape,
                             scratch_shapes=list(scratch), compiler_params=_params(sem))(*args)
        return res[0] if single else res
    r_ins = [a for r in riders for a in r.ins]
    r_outs = [o for r in riders for o in r.outs]
    r_scr = [s for r in riders for s in r.scratch()]
    aliases, i0, o0 = {}, n_in, n_out
    for r in riders:
        for a, b in r.aliases.items():
            aliases[i0 + a] = o0 + b
        i0, o0 = i0 + len(r.ins), o0 + len(r.outs)
    steps = math.prod(grid)

    def full(*refs):
        ins, rin = refs[:n_in], refs[n_in:n_in + len(r_ins)]
        pos = n_in + len(r_ins)
        outs, rout = refs[pos:pos + n_out], refs[pos + n_out:pos + n_out + len(r_outs)]
        pos += n_out + len(r_outs)
        sc, rsc = refs[pos:pos + n_sc], refs[pos + n_sc:]
        step = 0
        for d, n in enumerate(grid):
            step = step * n + pl.program_id(d)

        def each(method, *lead):
            i, o = 0, 0
            for k, r in enumerate(riders):
                getattr(r, method)(*lead, rin[i:i + len(r.ins)], rout[o:o + len(r.outs)], rsc[3 * k:3 * k + 3])
                i, o = i + len(r.ins), o + len(r.outs)

        @pl.when(step == 0)
        def _():
            each("start")
        body(*ins, *outs, *sc)

        late = max(steps - 1 - max(steps // 8, 1), 0)
        first, second = marks or (min((3 * steps) // 5, late), late)

        @pl.when(step == first)
        def _():
            each("advance", 1)

        @pl.when(step == second)
        def _():
            each("advance", 2)

        @pl.when(step == steps - 1)
        def _():
            each("finish")

    anywhere = pl.BlockSpec(memory_space=pl.ANY)
    res = pl.pallas_call(
        full, name=name, grid=grid, in_specs=list(in_specs) + [anywhere] * len(r_ins),
        out_specs=out_specs + (anywhere,) * len(r_outs), out_shape=out_shape + tuple(r_outs),
        scratch_shapes=list(scratch) + r_scr, input_output_aliases=aliases,
        compiler_params=_params(("arbitrary",) * len(grid)))(*args, *r_ins)
    host, rest, per = res[:n_out], list(res[n_out:]), []
    for r in riders:
        per.append(rest[:len(r.outs)])
        rest = rest[len(r.outs):]
    return (host[0] if single else tuple(host)), per


def side_by_side(first, second, name, riders=()):
    a_in, a_out, a_sc = len(first["in_specs"]), len(first["out_shape"]), len(first["scratch"])
    n_in, n_out = a_in + len(second["in_specs"]), a_out + len(second["out_shape"])

    def body(*refs):
        ins, outs, sc = refs[:n_in], refs[n_in:n_in + n_out], refs[n_in + n_out:]
        first["body"](*ins[:a_in], *outs[:a_out], *sc[:a_sc])
        second["body"](*ins[a_in:], *outs[a_out:], *sc[a_sc:])

    return _call(body, name=name, grid=first["grid"], in_specs=first["in_specs"] + second["in_specs"],
                 out_specs=first["out_specs"] + second["out_specs"], out_shape=first["out_shape"] + second["out_shape"],
                 args=first["args"] + second["args"], scratch=first["scratch"] + second["scratch"],
                 sem=("arbitrary",) * len(first["grid"]), riders=riders, marks=first["marks"])


def _rows_tile(n, cap):
    return max(t for t in range(16, min(n, cap) + 1, 16) if n % t == 0)


def scatter_add(gs, ras, name):
    C = gs[0].shape[2]
    trs = [_rows_tile(g.shape[1], 176) for g in gs]
    nts = [g.shape[1] // tr for g, tr in zip(gs, trs)]
    steps = max(nts)
    x, y, c, chips = _place()
    slots = jnp.stack([4 * px + 2 * py + c for px, py in [(x, y)] + chips]).astype(jnp.int32)

    def body(s_ref, *refs):
        ins, outs = refs[:5 * len(gs)], refs[5 * len(gs):]
        for k in range(len(gs)):
            g0, g1, g2, g3, ra_ref = ins[5 * k:5 * k + 5]
            own_ref, sb_ref = outs[2 * k:2 * k + 2]

            def work(g0=g0, g1=g1, g2=g2, g3=g3, ra_ref=ra_ref, own_ref=own_ref, sb_ref=sb_ref):
                own_ref[...] = g0[...] + ra_ref[0].astype(F32)
                for j, gj in enumerate((g1, g2, g3)):
                    sb_ref[j] = (gj[...] + ra_ref[j + 1].astype(F32)).astype(BF16)

            if nts[k] == steps:
                work()
            else:
                pl.when(pl.program_id(0) < nts[k])(work)

    in_specs, out_specs, out_shape, args = [], [], [], [slots]
    for k, (g, ra, tr) in enumerate(zip(gs, ras, trs)):
        tile = lambda i, k=k: jnp.minimum(i, nts[k] - 1)
        in_specs += [pl.BlockSpec((None, tr, C), lambda i, s, j=j, tile=tile: (s[j], tile(i), 0)) for j in range(4)]
        in_specs.append(pl.BlockSpec((4, tr, C), lambda i, s, tile=tile: (0, tile(i), 0)))
        out_specs += [pl.BlockSpec((tr, C), lambda i, s, tile=tile: (tile(i), 0)),
                      pl.BlockSpec((3, tr, C), lambda i, s, tile=tile: (0, tile(i), 0))]
        out_shape += [S((g.shape[1], C), F32), S((3, g.shape[1], C), BF16)]
        args += [g, g, g, g, ra]
    spec = pltpu.PrefetchScalarGridSpec(num_scalar_prefetch=1, grid=(steps,), in_specs=in_specs, out_specs=tuple(out_specs))
    res = pl.pallas_call(body, name=name, grid_spec=spec, out_shape=tuple(out_shape), compiler_params=_params(("arbitrary",)))(*args)
    return [(res[2 * k], res[2 * k + 1]) for k in range(len(gs))]


def rms_cast(h, g, name, riders=()):
    T, D = h.shape
    tm = _tile(T, 512)

    def body(h_ref, g_ref, n_ref):
        n_ref[...] = _rms(h_ref[...], g_ref[...]).astype(BF16)

    row = pl.BlockSpec((tm, D), lambda i: (i, 0))
    return _call(body, name=name, grid=(T // tm,), out_shape=S((T, D), BF16), in_specs=[row, pl.BlockSpec((1, D), lambda i: (0, 0))],
                 out_specs=row, sem=("parallel",), args=(h, g), riders=riders)


def ffn_up_fwd(n, wgu, name, riders=()):
    T, D = n.shape
    F = wgu.shape[1]
    tr, tn = _tile(T, 512), _tile(F, 256)

    def body(n_ref, wg_ref, wu_ref, dgate_ref, dup_ref, a_ref):
        wg, wu = wg_ref[...], wu_ref[...]
        for r in range(T // tr):
            rows = slice(r * tr, (r + 1) * tr)
            x = n_ref[rows, :]
            gate = _nt(x, wg)
            up = _nt(x, wu)
            s = jax.nn.sigmoid(gate)
            silu = gate * s
            dgate_ref[rows, :] = (up * (s * (1.0 + gate * (1.0 - s)))).astype(BF16)
            dup_ref[rows, :] = silu.astype(BF16)
            a_ref[rows, :] = (silu * up).astype(BF16)

    tile = pl.BlockSpec((T, tn), lambda j: (0, j))
    return _call(
        body, name=name, grid=(F // tn,), out_shape=(S((T, F), BF16),) * 3,
        in_specs=[pl.BlockSpec((T, D), lambda j: (0, 0)),
                  pl.BlockSpec((None, tn, D), lambda j: (0, j, 0)), pl.BlockSpec((None, tn, D), lambda j: (1, j, 0))],
        out_specs=(tile, tile, tile), sem=("parallel",), args=(n, wgu, wgu), riders=riders)


def ffn_down_fwd(a, wd, h, g_next, name, riders=()):
    T, F = a.shape
    D = wd.shape[1]
    tm = _tile(T, 256)

    def body(a_ref, w_ref, h_ref, *rest):
        out = h_ref[...] + 0.5 * _nn(a_ref[...], w_ref[...])
        if g_next is None:
            rest[0][...] = out
        else:
            g_ref, o_ref, n_ref = rest
            o_ref[...] = out
            n_ref[...] = _rms(out, g_ref[...]).astype(BF16)

    row = pl.BlockSpec((tm, D), lambda i: (i, 0))
    more = g_next is not None
    return _call(
        body, name=name, grid=(T // tm,), out_shape=(S((T, D), F32), S((T, D), BF16)) if more else S((T, D), F32),
        in_specs=[pl.BlockSpec((tm, F), lambda i: (i, 0)), pl.BlockSpec((F, D), lambda i: (0, 0)), row]
        + ([pl.BlockSpec((1, D), lambda i: (0, 0))] if more else []),
        out_specs=(row, row) if more else row,
        sem=("parallel",), args=(a, wd, h) + ((g_next,) if more else ()), riders=riders)


def mix_in_fwd(h, g, win, name):
    T, D = h.shape
    N = win.shape[0]
    tm = _tile(T, 256)

    def body(h_ref, g_ref, w_ref, n_ref, p_ref):
        n = _rms(h_ref[...], g_ref[...]).astype(BF16)
        n_ref[...] = n
        p_ref[...] = _nt(n, w_ref[...]).astype(BF16)

    return pl.pallas_call(
        body, name=name, grid=(T // tm,), out_shape=(S((T, D), BF16), S((T, N), BF16)),
        in_specs=[pl.BlockSpec((tm, D), lambda i: (i, 0)), pl.BlockSpec((1, D), lambda i: (0, 0)),
                  pl.BlockSpec((N, D), lambda i: (0, 0))],
        out_specs=(pl.BlockSpec((tm, D), lambda i: (i, 0)), pl.BlockSpec((tm, N), lambda i: (i, 0))),
        compiler_params=_params(("parallel",)),
    )(h, g, win)


def _tri_consts():
    r = lax.broadcasted_iota(jnp.int32, (QB, QB), 0)
    c = lax.broadcasted_iota(jnp.int32, (QB, QB), 1)
    ones = jnp.ones((QB, QB), BF16)
    with_sums = lambda tri: jnp.concatenate([tri.astype(BF16), ones], axis=1)
    return with_sums(r > c), with_sums(r <= c), with_sums(r < c)


def _half_masks():
    lane = lax.broadcasted_iota(jnp.int32, (QB, PAIR), 1)
    row = lax.broadcasted_iota(jnp.int32, (QB, PAIR), 0)
    return lane < HEAD_DIM, lane, row


def sb_attn_fwd(p, after, name, riders=()):
    T = p.shape[0]
    nq = T // QB

    def body(q_ref, k_ref, v_ref, m_ref, o_ref, tot_ref, q_sc, acc_ref, z_sc):
        i = pl.program_id(0)
        lo, lane, row = _half_masks()
        causal = lane < row
        heads, pairs = range(SB_HEADS), range(SB_HEADS // 2)
        for hp in pairs:
            q_sc[hp] = (q_ref[:, hp * PAIR:(hp + 1) * PAIR].astype(F32) * SCALE).astype(BF16)
        m2 = m_ref[...]

        def by_head(ref, j, hp):
            t = ref[pl.ds(pl.multiple_of(j * QB, QB), QB), hp * PAIR:(hp + 1) * PAIR]
            return jnp.concatenate([jnp.where(lo, t, 0), jnp.where(lo, 0, t)], axis=0)

        def scores(j):
            return [_nt(q_sc[hp], by_head(k_ref, j, hp)) for hp in pairs]

        def block(j, diag):
            z2 = [z_sc[hp] for hp in pairs]
            ahead = scores(jnp.maximum(j - 1, 0))
            for hp in pairs:
                z_sc[hp] = ahead[hp]
            vs = [by_head(v_ref, j, hp) for hp in pairs]
            spls = [_softplus_logsig(z2[h // 2][:, (h % 2) * QB:(h % 2 + 1) * QB]) for h in heads]
            sp = [jnp.where(causal, spls[h][0], 0.0) if diag else spls[h][0] for h in heads]
            rr = _tri(sp, m2)
            if diag:
                w = [jnp.where(causal, jnp.exp(spls[h][1] - rr[h][:, :QB]), 0.0).astype(BF16) for h in heads]
            else:
                c = [tot_ref[:, h * QB:(h + 1) * QB] for h in heads]
                w = [jnp.exp(spls[h][1] - (c[h] + rr[h][:, :QB])).astype(BF16) for h in heads]
            pv = [_nn(jnp.concatenate([w[2 * hp], w[2 * hp + 1]], axis=1), vs[hp]) for hp in pairs]
            for hp in pairs:
                acc_ref[hp] = pv[hp] if diag else acc_ref[hp] + pv[hp]
            for h in heads:
                tot_ref[:, h * QB:(h + 1) * QB] = rr[h][:, QB:] if diag else c[h] + rr[h][:, QB:]

        first = scores(i)
        for hp in pairs:
            z_sc[hp] = first[hp]
        block(i, True)

        def step(t, carry):
            block(i - 1 - t, False)
            return carry
        lax.fori_loop(0, i, step, 0)
        for hp in pairs:
            o_ref[:, hp * PAIR:(hp + 1) * PAIR] = acc_ref[hp]

    npair = SB_HEADS // 2
    return _call(
        body, name=name, grid=(nq,), out_shape=(S((T, SB_W), F32), S((T, SB_HEADS * QB), F32)),
        in_specs=[pl.BlockSpec((QB, SB_W), lambda i: (i, 0)), pl.BlockSpec((T, SB_W), lambda i: (0, 1)),
                  pl.BlockSpec((T, SB_W), lambda i: (0, 2)), pl.BlockSpec((QB, 2 * QB), lambda i: (0, 0))],
        out_specs=(pl.BlockSpec((QB, SB_W), lambda i: (i, 0)), pl.BlockSpec((QB, SB_HEADS * QB), lambda i: (i, 0))),
        scratch=[pltpu.VMEM((npair, QB, PAIR), BF16), pltpu.VMEM((npair, QB, PAIR), F32), pltpu.VMEM((npair, QB, 2 * QB), F32)],
        sem=("arbitrary",), args=(p, p, p, after), riders=riders,
        marks=((11 * nq) // 16, (14 * nq) // 16))


def sb_attn_bwd(p, do, tot, upto, before, name, riders=()):
    T = p.shape[0]
    nq = T // QB

    def body(q_ref, k_ref, v_ref, do_ref, tot_ref, mp_ref, mg_ref, dq_ref, dk_ref, dv_ref,
             q_sc, d_sc, qd_sc, pg_sc, dq_acc, dk_acc, dv_acc, zd_sc):
        i = pl.program_id(0)
        lo, lane, row = _half_masks()
        causal = lane < row
        heads, pairs = range(SB_HEADS), range(SB_HEADS // 2)

        def by_head(t):
            return jnp.concatenate([jnp.where(lo, t, 0), jnp.where(lo, 0, t)], axis=0)

        for hp in pairs:
            q2 = (q_ref[:, hp * PAIR:(hp + 1) * PAIR].astype(F32) * SCALE).astype(BF16)
            d2 = do_ref[:, hp * PAIR:(hp + 1) * PAIR].astype(BF16)
            q_sc[hp] = q2
            d_sc[hp] = d2
            qd_sc[hp] = by_head(q2)
            qd_sc[SB_HEADS // 2 + hp] = by_head(d2)
        mp, mg = mp_ref[...], mg_ref[...]

        @pl.when(i == 0)
        def _():
            dk_acc[...] = jnp.zeros_like(dk_acc)
            dv_acc[...] = jnp.zeros_like(dv_acc)
        pg_sc[...] = jnp.zeros_like(pg_sc)
        dq_acc[...] = jnp.zeros_like(dq_acc)

        def rows(ref, j, hp):
            return ref[pl.ds(pl.multiple_of(j * QB, QB), QB), hp * PAIR:(hp + 1) * PAIR]

        def products(j):
            return ([_nt(q_sc[hp], by_head(rows(k_ref, j, hp))) for hp in pairs]
                    + [_nt(d_sc[hp], by_head(rows(v_ref, j, hp))) for hp in pairs])

        def block(j, diag):
            r0 = pl.multiple_of(j * QB, QB)
            half = lambda t, h: t[:, (h % 2) * QB:(h % 2 + 1) * QB]
            z = [half(zd_sc[h // 2], h) for h in heads]
            dw = [half(zd_sc[SB_HEADS // 2 + h // 2], h) for h in heads]
            if not diag:
                ahead = products(j + 1)
                for hp in range(SB_HEADS):
                    zd_sc[hp] = ahead[hp]
            ks = [by_head(rows(k_ref, j, hp)) for hp in pairs]
            spls = [_softplus_logsig(z[h]) for h in heads]
            sp = [jnp.where(causal, spls[h][0], 0.0) if diag else spls[h][0] for h in heads]
            rr = _tri(sp, mp)
            pc = [pg_sc[2 * h] for h in heads]
            w = [jnp.exp(spls[h][1] - (tot_ref[:, h * QB:(h + 1) * QB] - (pc[h] + rr[h][:, :QB]))) for h in heads]
            if diag:
                w = [jnp.where(causal, w[h], 0.0) for h in heads]
            gg = [dw[h] * w[h] for h in heads]
            rg = _tri(gg, mg)
            gc = [pg_sc[2 * h + 1] for h in heads]
            dz = [gg[h] - (gg[h] + gc[h] + rg[h][:, :QB]) * jnp.exp(spls[h][1]) for h in heads]
            if diag:
                dz = [jnp.where(causal, dz[h], 0.0) for h in heads]
            dzb = [dz[h].astype(BF16) for h in heads]
            wb = [w[h].astype(BF16) for h in heads]
            both = lambda t, hp, axis: jnp.concatenate([t[2 * hp], t[2 * hp + 1]], axis=axis)
            dq = [_nn(both(dzb, hp, 1), ks[hp]) for hp in pairs]
            dk = [_tn(both(dzb, hp, 0), qd_sc[hp]) for hp in pairs]
            dv = [_tn(both(wb, hp, 0), qd_sc[SB_HEADS // 2 + hp]) for hp in pairs]
            for h in heads:
                if not diag:
                    pg_sc[2 * h] = pc[h] + rr[h][:, QB:]
                    pg_sc[2 * h + 1] = gc[h] + rg[h][:, QB:]
            for hp in pairs:
                dq_acc[hp] += dq[hp]
                dk_acc[pl.ds(r0, QB), hp * PAIR:(hp + 1) * PAIR] += dk[hp]
                dv_acc[pl.ds(r0, QB), hp * PAIR:(hp + 1) * PAIR] += dv[hp]

        first = products(0)
        for hp in range(SB_HEADS):
            zd_sc[hp] = first[hp]

        def step(t, carry):
            block(t, False)
            return carry
        lax.fori_loop(0, i, step, 0)
        block(i, True)
        for hp in pairs:
            dq_ref[:, hp * PAIR:(hp + 1) * PAIR] = (dq_acc[hp] * SCALE).astype(BF16)

        @pl.when(i == nq - 1)
        def _():
            dk_ref[...] = dk_acc[...].astype(BF16)
            dv_ref[...] = dv_acc[...].astype(BF16)

    qtile = pl.BlockSpec((QB, SB_W), lambda i: (i, 0))
    whole = pl.BlockSpec((T, SB_W), lambda i: (0, 0))
    const = pl.BlockSpec((QB, 2 * QB), lambda i: (0, 0))
    return _call(
        body, name=name, grid=(nq,), out_shape=(S((T, SB_W), BF16),) * 3,
        in_specs=[qtile, pl.BlockSpec((T, SB_W), lambda i: (0, 1)), pl.BlockSpec((T, SB_W), lambda i: (0, 2)), qtile,
                  pl.BlockSpec((QB, SB_HEADS * QB), lambda i: (i, 0)), const, const],
        out_specs=(qtile, whole, whole),
        scratch=[pltpu.VMEM((SB_HEADS // 2, QB, PAIR), BF16), pltpu.VMEM((SB_HEADS // 2, QB, PAIR), BF16),
                 pltpu.VMEM((SB_HEADS, 2 * QB, PAIR), BF16),
                 pltpu.VMEM((2 * SB_HEADS, QB, QB), F32), pltpu.VMEM((SB_HEADS // 2, QB, PAIR), F32),
                 pltpu.VMEM((T, SB_W), F32), pltpu.VMEM((T, SB_W), F32), pltpu.VMEM((SB_HEADS, QB, 2 * QB), F32)],
        sem=("arbitrary",), args=(p, p, p, do, tot, upto, before), riders=riders)


def _t5_buckets():
    a = lax.broadcasted_iota(jnp.int32, (QB, QB), 0)
    c = lax.broadcasted_iota(jnp.int32, (QB, QB), 1)

    def bucket(dist):
        dist = jnp.maximum(dist, 0)
        max_exact = N_BUCKETS // 2
        d = jnp.maximum(dist, 1).astype(F32)
        large = max_exact + (jnp.log(d / max_exact) / math.log(MAX_DISTANCE / max_exact)
                             * (N_BUCKETS - max_exact)).astype(jnp.int32)
        large = jnp.minimum(large, N_BUCKETS - 1)
        return jnp.where(dist < max_exact, dist, large)

    return bucket(QB + a - c), bucket(a - c)


def _swa_common(i, kp_ref, kc_ref, vp_ref, vc_ref, bp_ref, bc_ref, rb_ref, bias_ref):
    lo, lane, row = _half_masks()

    @pl.when(i == 0)
    def _():
        for blk, b_ref in enumerate((bp_ref, bc_ref)):
            bk = b_ref[...]
            for h in range(8):
                acc = jnp.zeros((QB, QB), F32)
                for b in range(N_BUCKETS):
                    acc = jnp.where(bk == b, rb_ref[b, h], acc)
                bias_ref[h, blk] = acc

    band = [(lane > row) & (i > 0), lane <= row]

    def stacks(ref):
        t = ref[...].astype(F32)
        sw = pltpu.roll(t, HEAD_DIM, 1)
        return [jnp.concatenate([jnp.where(lo, t, 0.0), jnp.where(lo, 0.0, sw)], axis=0).astype(BF16),
                jnp.concatenate([jnp.where(lo, sw, 0.0), jnp.where(lo, 0.0, t)], axis=0).astype(BF16)]

    ks = [stacks(kp_ref), stacks(kc_ref)]
    vs = [stacks(vp_ref), stacks(vc_ref)]
    return lo, band, ks, vs


def _lane_half(t, h):
    return t[:, (h % 2) * QB:(h % 2 + 1) * QB]


def swa_fwd(p, sinks, rel_bias, bprev, bcur, name, riders=()):
    T = p.shape[0]
    nq = T // QB
    kcol, vcol = (3 * SB_W + SWA_W) // KV_W, (3 * SB_W + SWA_W) // KV_W + 1

    def body(q_ref, kp_ref, kc_ref, vp_ref, vc_ref, bp_ref, bc_ref, sink_ref, rb_ref, o_ref, lse_ref, bias_ref):
        i = pl.program_id(0)
        lo, band, ks, vs = _swa_common(i, kp_ref, kc_ref, vp_ref, vc_ref, bp_ref, bc_ref, rb_ref, bias_ref)
        heads, pairs, blocks = range(8), range(4), range(2)
        rowmax = lambda t: jnp.max(t, axis=1, keepdims=True)
        rowsum = lambda t: jnp.sum(t, axis=1, keepdims=True)
        q2 = [q_ref[:, g * PAIR:(g + 1) * PAIR] for g in pairs]
        s2 = [[_nt(q2[g], ks[b][g // 2]) for b in blocks] for g in pairs]
        sc = [[jnp.where(band[b], _lane_half(s2[h // 2][b], h) * SCALE + bias_ref[h, b], NEG_INF) for b in blocks] for h in heads]
        sink = [sink_ref[0, h] for h in heads]
        m = [jnp.maximum(jnp.maximum(rowmax(sc[h][0]), rowmax(sc[h][1])), sink[h]) for h in heads]
        e = [[jnp.exp(sc[h][b] - m[h]) for b in blocks] for h in heads]
        den = [rowsum(e[h][0]) + rowsum(e[h][1]) + jnp.exp(sink[h] - m[h]) for h in heads]
        pb = [[(e[h][b] / den[h]).astype(BF16) for b in blocks] for h in heads]
        for g in pairs:
            both = lambda b: jnp.concatenate([pb[2 * g][b], pb[2 * g + 1][b]], axis=1)
            o_ref[:, g * PAIR:(g + 1) * PAIR] = _nn(both(0), vs[0][g // 2]) + _nn(both(1), vs[1][g // 2])
        for h in heads:
            lse_ref[:, h * QB:(h + 1) * QB] = jnp.broadcast_to(m[h] + jnp.log(den[h]), (QB, QB))

    kv = lambda col, prev: pl.BlockSpec((QB, KV_W), (lambda i: (jnp.maximum(i - 1, 0), col)) if prev else (lambda i: (i, col)))
    full = pl.BlockSpec((QB, QB), lambda i: (0, 0))
    smem = pl.BlockSpec(memory_space=pltpu.SMEM)
    return _call(
        body, name=name, grid=(nq,), out_shape=(S((T, SWA_W), F32), S((T, 8 * QB), F32)),
        in_specs=[pl.BlockSpec((QB, SWA_W), lambda i: (i, 3)), kv(kcol, True), kv(kcol, False), kv(vcol, True), kv(vcol, False),
                  full, full, smem, smem],
        out_specs=(pl.BlockSpec((QB, SWA_W), lambda i: (i, 0)), pl.BlockSpec((QB, 8 * QB), lambda i: (i, 0))),
        scratch=[pltpu.VMEM((8, 2, QB, QB), F32)],
        sem=("arbitrary",), args=(p, p, p, p, p, bprev, bcur, sinks, rel_bias), riders=riders)


def swa_bwd(p, do, lse, sinks, rel_bias, bprev, bcur, name, riders=()):
    T = p.shape[0]
    nq = T // QB
    kcol, vcol = (3 * SB_W + SWA_W) // KV_W, (3 * SB_W + SWA_W) // KV_W + 1

    def body(q_ref, kp_ref, kc_ref, vp_ref, vc_ref, do_ref, lse_ref, bp_ref, bc_ref, sink_ref, rb_ref,
             dq_ref, dk_ref, dv_ref, dsink_ref, dsc_ref, bias_ref, dk_acc, dv_acc):
        i = pl.program_id(0)
        lo, band, ks, vs = _swa_common(i, kp_ref, kc_ref, vp_ref, vc_ref, bp_ref, bc_ref, rb_ref, bias_ref)

        @pl.when(i == 0)
        def _():
            dk_acc[...] = jnp.zeros_like(dk_acc)
            dv_acc[...] = jnp.zeros_like(dv_acc)
            dsc_ref[...] = jnp.zeros_like(dsc_ref)
            dsink_ref[...] = jnp.zeros_like(dsink_ref)

        heads, pairs, blocks = range(8), range(4), range(2)
        rowsum = lambda t: jnp.sum(t, axis=1, keepdims=True)
        by_head = lambda t: jnp.concatenate([jnp.where(lo, t, 0), jnp.where(lo, 0, t)], axis=0)
        q2 = [q_ref[:, g * PAIR:(g + 1) * PAIR] for g in pairs]
        d2 = [do_ref[:, g * PAIR:(g + 1) * PAIR].astype(BF16) for g in pairs]
        qs = [by_head(q2[g]) for g in pairs]
        dos = [by_head(d2[g]) for g in pairs]
        s2 = [[_nt(q2[g], ks[b][g // 2]) for b in blocks] for g in pairs]
        dp2 = [[_nt(d2[g], vs[b][g // 2]) for b in blocks] for g in pairs]
        lse_h = [lse_ref[:, h * QB:(h + 1) * QB] for h in heads]
        sink = [sink_ref[0, h] for h in heads]
        pr = [[jnp.exp(jnp.where(band[b], _lane_half(s2[h // 2][b], h) * SCALE + bias_ref[h, b], NEG_INF) - lse_h[h])
               for b in blocks] for h in heads]
        dp = [[_lane_half(dp2[h // 2][b], h) for b in blocks] for h in heads]
        delta = [rowsum(pr[h][0] * dp[h][0]) + rowsum(pr[h][1] * dp[h][1]) for h in heads]
        lane1 = lax.broadcasted_iota(jnp.int32, (1, QB), 1)
        dsink = jnp.zeros((1, QB), F32)
        for h in heads:
            dsink = dsink + jnp.where(lane1 == h, -jnp.sum(jnp.exp(sink[h] - lse_h[h][:, :1]) * delta[h]), 0.0)
        dsink_ref[...] += dsink
        dsc = [[pr[h][b] * (dp[h][b] - delta[h]) for b in blocks] for h in heads]
        for h in heads:
            for b in blocks:
                dsc_ref[h, b] += dsc[h][b]
        dzb = [[(dsc[h][b] * SCALE).astype(BF16) for b in blocks] for h in heads]
        prb = [[pr[h][b].astype(BF16) for b in blocks] for h in heads]
        pair_of = lambda t, g, b, axis: jnp.concatenate([t[2 * g][b], t[2 * g + 1][b]], axis=axis)
        for g in pairs:
            dq = _nn(pair_of(dzb, g, 0, 1), ks[0][g // 2]) + _nn(pair_of(dzb, g, 1, 1), ks[1][g // 2])
            dq_ref[:, g * PAIR:(g + 1) * PAIR] = dq.astype(BF16)

        def key_grad(t, other, b):
            per_kv = [_tn(pair_of(t, 2 * kh, b, 0), other[2 * kh]) + _tn(pair_of(t, 2 * kh + 1, b, 0), other[2 * kh + 1]) for kh in range(2)]
            both = [s + pltpu.roll(s, HEAD_DIM, 1) for s in per_kv]
            return jnp.where(lo, both[0], both[1])

        rp = pl.multiple_of(jnp.maximum(i - 1, 0) * QB, QB)
        rc = pl.multiple_of(i * QB, QB)
        dk_acc[pl.ds(rp, QB), :] += key_grad(dzb, qs, 0)
        dv_acc[pl.ds(rp, QB), :] += key_grad(prb, dos, 0)
        dk_acc[pl.ds(rc, QB), :] += key_grad(dzb, qs, 1)
        dv_acc[pl.ds(rc, QB), :] += key_grad(prb, dos, 1)

        @pl.when(i == nq - 1)
        def _():
            dk_ref[...] = dk_acc[...].astype(BF16)
            dv_ref[...] = dv_acc[...].astype(BF16)

    kv = lambda col, prev: pl.BlockSpec((QB, KV_W), (lambda i: (jnp.maximum(i - 1, 0), col)) if prev else (lambda i: (i, col)))
    full = pl.BlockSpec((QB, QB), lambda i: (0, 0))
    smem = pl.BlockSpec(memory_space=pltpu.SMEM)
    whole = lambda shape: pl.BlockSpec(shape, lambda i: (0,) * len(shape))
    return _call(
        body, name=name, grid=(nq,),
        out_shape=(S((T, SWA_W), BF16), S((T, KV_W), BF16), S((T, KV_W), BF16), S((1, QB), F32), S((8, 2, QB, QB), F32)),
        in_specs=[pl.BlockSpec((QB, SWA_W), lambda i: (i, 3)), kv(kcol, True), kv(kcol, False), kv(vcol, True), kv(vcol, False),
                  pl.BlockSpec((QB, SWA_W), lambda i: (i, 0)), pl.BlockSpec((QB, 8 * QB), lambda i: (i, 0)),
                  full, full, smem, smem],
        out_specs=(pl.BlockSpec((QB, SWA_W), lambda i: (i, 0)), whole((T, KV_W)), whole((T, KV_W)), whole((1, QB)),
                   whole((8, 2, QB, QB))),
        scratch=[pltpu.VMEM((8, 2, QB, QB), F32), pltpu.VMEM((T, KV_W), F32), pltpu.VMEM((T, KV_W), F32)],
        sem=("arbitrary",), args=(p, p, p, p, p, do, lse, bprev, bcur, sinks, rel_bias), riders=riders)


def mix_out_fwd(o_sb, o_sw, g_sb, g_sw, wout, h, g_next, name, riders=()):
    T, D = h.shape
    M = SB_W + SWA_W
    tm = _tile(T, 256)

    def body(a_ref, b_ref, ga_ref, gb_ref, w_ref, h_ref, gn_ref, mx_ref, o_ref, n_ref):
        mx_ref[:, :SB_W] = _rms(a_ref[...], ga_ref[...]).astype(BF16)
        mx_ref[:, SB_W:] = _rms(b_ref[...], gb_ref[...]).astype(BF16)
        out = h_ref[...] + _nn(mx_ref[...], w_ref[...])
        o_ref[...] = out
        n_ref[...] = _rms(out, gn_ref[...]).astype(BF16)

    row = lambda n: pl.BlockSpec((tm, n), lambda i: (i, 0))
    vec = lambda n: pl.BlockSpec((1, n), lambda i: (0, 0))
    return _call(
        body, name=name, grid=(T // tm,), out_shape=(S((T, M), BF16), S((T, D), F32), S((T, D), BF16)),
        in_specs=[row(SB_W), row(SWA_W), vec(SB_W), vec(SWA_W), pl.BlockSpec((M, D), lambda i: (0, 0)), row(D), vec(D)],
        out_specs=(row(M), row(D), row(D)),
        sem=("parallel",), args=(o_sb, o_sw, g_sb, g_sw, wout, h, g_next), riders=riders)


def loss_head(h, g, target, name):
    T, D = h.shape
    tm = _tile(T, 256)

    def body(h_ref, g_ref, t_ref, loss_ref, dh_ref, dhb_ref, dg_ref):
        @pl.when(pl.program_id(0) == 0)
        def _():
            loss_ref[...] = jnp.zeros_like(loss_ref)
            dg_ref[...] = jnp.zeros_like(dg_ref)
        x = h_ref[...]
        err = _rms(x, g_ref[...]) - t_ref[...]
        loss_ref[...] += jnp.full((1, QB), 0.5 * jnp.sum(jnp.mean(err * err, axis=-1)), F32)
        dx, dg = _rms_bwd(err / D, x, g_ref[...])
        dh_ref[...] = dx
        dhb_ref[...] = dx.astype(BF16)
        dg_ref[...] += dg

    row = pl.BlockSpec((tm, D), lambda i: (i, 0))
    vec = pl.BlockSpec((1, D), lambda i: (0, 0))
    return pl.pallas_call(
        body, name=name, grid=(T // tm,), out_shape=(S((1, QB), F32), S((T, D), F32), S((T, D), BF16), S((1, D), F32)),
        in_specs=[row, vec, row], out_specs=(pl.BlockSpec((1, QB), lambda i: (0, 0)), row, row, vec),
        compiler_params=_params(("arbitrary",)),
    )(h, g, target)


def ffn_down_bwd(dhb, wd, gate, up, a, n, name, riders=()):
    T, D = dhb.shape
    F = wd.shape[0]
    tr, tn = _tile(T, 512), _tile(F, 256)

    def body(d_ref, n_ref, w_ref, g_ref, u_ref, a_ref, o_ref, dwd_ref, dwdb_ref, dwgu_ref, dwgub_ref):
        w = w_ref[...]
        for r in range(T // tr):
            rows = slice(r * tr, (r + 1) * tr)
            da = 0.5 * _nt(d_ref[rows, :], w)
            o_ref[0, rows, :] = (da * g_ref[rows, :].astype(F32)).astype(BF16)
            o_ref[1, rows, :] = (da * u_ref[rows, :].astype(F32)).astype(BF16)
        dwd = 0.5 * _tn(a_ref[...], d_ref[...])
        dwd_ref[...] = dwd
        dwdb_ref[...] = dwd.astype(BF16)
        for s in range(2):
            dwgu = _tn(o_ref[s], n_ref[...])
            dwgu_ref[s] = dwgu
            dwgub_ref[s] = dwgu.astype(BF16)

    tile = pl.BlockSpec((T, tn), lambda j: (0, j))
    whole = pl.BlockSpec((T, D), lambda j: (0, 0))
    rows1, rows2 = pl.BlockSpec((tn, D), lambda j: (j, 0)), pl.BlockSpec((2, tn, D), lambda j: (0, j, 0))
    return _call(
        body, name=name, grid=(F // tn,),
        out_shape=(S((2, T, F), BF16), S((F, D), F32), S((F, D), BF16), S((2, F, D), F32), S((2, F, D), BF16)),
        in_specs=[whole, whole, rows1, tile, tile, tile],
        out_specs=(pl.BlockSpec((2, T, tn), lambda j: (0, 0, j)), rows1, rows1, rows2, rows2),
        sem=("parallel",), args=(dhb, n, wd, gate, up, a), riders=riders)


def tn_matmul(xs, y, alpha, name, riders=()):
    B, T, N = xs.shape
    D = y.shape[1]
    tn = _tile(N, 256)

    def body(x_ref, y_ref, o_ref, ob_ref):
        o = alpha * _tn(x_ref[...], y_ref[...])
        o_ref[...] = o
        ob_ref[...] = o.astype(BF16)

    tile = pl.BlockSpec((None, tn, D), lambda s, j: (s, j, 0))
    return _call(
        body, name=name, grid=(B, N // tn), out_shape=(S((B, N, D), F32), S((B, N, D), BF16)),
        in_specs=[pl.BlockSpec((None, T, tn), lambda s, j: (s, 0, j)), pl.BlockSpec((T, D), lambda s, j: (0, 0))],
        out_specs=(tile, tile), sem=("parallel", "parallel"), args=(xs, y), riders=riders)


def nn_rms_bwd(xs, ws, h_in, g, dh, name, riders=()):
    B, T, K = xs.shape
    D = ws.shape[2]
    tm = _tile(T, 256)

    def body(x_ref, w_ref, h_ref, g_ref, d_ref, o_ref, ob_ref, dg_ref):
        @pl.when(pl.program_id(0) == 0)
        def _():
            dg_ref[...] = jnp.zeros_like(dg_ref)
        dn = _nn(x_ref[0], w_ref[0])
        for s in range(1, B):
            dn = dn + _nn(x_ref[s], w_ref[s])
        dx, dg = _rms_bwd(dn, h_ref[...], g_ref[...])
        out = d_ref[...] + dx
        o_ref[...] = out
        ob_ref[...] = out.astype(BF16)
        dg_ref[...] += dg

    row = pl.BlockSpec((tm, D), lambda i: (i, 0))
    vec = pl.BlockSpec((1, D), lambda i: (0, 0))
    return _call(
        body, name=name, grid=(T // tm,), out_shape=(S((T, D), F32), S((T, D), BF16), S((1, D), F32)),
        in_specs=[pl.BlockSpec((B, tm, K), lambda i: (0, i, 0)), pl.BlockSpec((B, K, D), lambda i: (0, 0, 0)), row, vec, row],
        out_specs=(row, row, vec),
        sem=("arbitrary",), args=(xs, ws, h_in, g, dh), riders=riders)


def mix_out_bwd(dhb, wout, mixed, o_sb, o_sw, g_sb, g_sw, name):
    T, D = dhb.shape
    M = SB_W + SWA_W
    tm = _tile(T, 256)
    steps = T // tm

    def body(d_ref, w_ref, mx_ref, a_ref, b_ref, ga_ref, gb_ref, da_ref, db_ref, dga_ref, dgb_ref, dw_ref, dwb_ref):
        i = pl.program_id(0)

        @pl.when(i == 0)
        def _():
            dga_ref[...] = jnp.zeros_like(dga_ref)
            dgb_ref[...] = jnp.zeros_like(dgb_ref)
            dw_ref[...] = jnp.zeros_like(dw_ref)
        dm = _nt(d_ref[...], w_ref[...])
        dxa, dga = _rms_bwd(dm[:, :SB_W], a_ref[...], ga_ref[...])
        dxb, dgb = _rms_bwd(dm[:, SB_W:], b_ref[...], gb_ref[...])
        da_ref[...] = dxa
        db_ref[...] = dxb
        dga_ref[...] += dga
        dgb_ref[...] += dgb
        dw_ref[...] += _tn(mx_ref[...], d_ref[...])

        @pl.when(i == steps - 1)
        def _():
            dwb_ref[...] = dw_ref[...].astype(BF16)

    row = lambda n: pl.BlockSpec((tm, n), lambda i: (i, 0))
    vec = lambda n: pl.BlockSpec((1, n), lambda i: (0, 0))
    whole = pl.BlockSpec((M, D), lambda i: (0, 0))
    return pl.pallas_call(
        body, name=name, grid=(steps,),
        out_shape=(S((T, SB_W), F32), S((T, SWA_W), F32), S((1, SB_W), F32), S((1, SWA_W), F32), S((M, D), F32), S((M, D), BF16)),
        in_specs=[row(D), whole, row(M), row(SB_W), row(SWA_W), vec(SB_W), vec(SWA_W)],
        out_specs=(row(SB_W), row(SWA_W), vec(SB_W), vec(SWA_W), whole, whole),
        compiler_params=_params(("arbitrary",)),
    )(dhb, wout, mixed, o_sb, o_sw, g_sb, g_sw)


def rel_bias_grad(dscs, bprev, bcur, name):
    n = len(dscs)

    def body(*refs):
        bp_ref, bc_ref, o_ref = refs[n], refs[n + 1], refs[n + 2]
        bks = [bp_ref[...], bc_ref[...]]
        row = lax.broadcasted_iota(jnp.int32, (N_BUCKETS, QB), 0)
        lane = lax.broadcasted_iota(jnp.int32, (N_BUCKETS, QB), 1)
        out = jnp.zeros((N_BUCKETS, QB), F32)
        for h in range(8):
            tot = [sum(refs[l][h, b] for l in range(n)) for b in range(2)]
            for b in range(N_BUCKETS):
                val = jnp.sum(jnp.where(bks[0] == b, tot[0], 0.0)) + jnp.sum(jnp.where(bks[1] == b, tot[1], 0.0))
                out = jnp.where((row == b) & (lane == h), val, out)
        o_ref[...] = out

    return pl.pallas_call(body, name=name, out_shape=S((N_BUCKETS, QB), F32), compiler_params=_params())(*dscs, bprev, bcur)


def _adamw(w, g, m, v):
    m = ADAM_B1 * m + (1.0 - ADAM_B1) * g
    v = ADAM_B2 * v + (1.0 - ADAM_B2) * (g * g)
    m_hat = m / (1.0 - ADAM_B1 ** ADAM_STEP)
    v_hat = v / (1.0 - ADAM_B2 ** ADAM_STEP)
    delta = -ADAM_LR * (m_hat / (jnp.sqrt(v_hat) + ADAM_EPS) + ADAM_WD * w)
    return delta, m, v


def adamw_scattered(w, m, v, owns, others, name, riders=()):
    L, R, C = w.shape
    tr = _rows_tile(R, 176)

    def body(w_ref, m_ref, v_ref, *rest):
        own_refs, other_refs = rest[:L], rest[L:2 * L]
        g_ref, d_ref, mo_ref, vo_ref = rest[2 * L:]
        layer = pl.program_id(0)

        def grad(k):
            o = other_refs[k]
            return own_refs[k][...] + o[0].astype(F32) + o[1].astype(F32) + o[2].astype(F32)

        g = grad(0)
        for k in range(1, L):
            g = jnp.where(layer == k, grad(k), g)
        d, mn, vn = _adamw(w_ref[...], g, m_ref[...], v_ref[...])
        g_ref[...] = g
        d_ref[...] = d
        mo_ref[...] = mn
        vo_ref[...] = vn

    tile = pl.BlockSpec((None, tr, C), lambda l, i: (l, i, 0))
    return _call(
        body, name=name, grid=(L, R // tr), out_shape=(S((L, R, C), F32),) * 4,
        in_specs=[tile] * 3 + [pl.BlockSpec((tr, C), lambda l, i: (i, 0))] * L + [pl.BlockSpec((3, tr, C), lambda l, i: (0, i, 0))] * L,
        out_specs=(tile,) * 4, sem=("parallel", "parallel"), args=(w, m, v, *owns, *others), riders=riders)


def adamw_small(w, gs, m, v, name):
    R, C = w.shape

    def body(w_ref, g_ref, m_ref, v_ref, go_ref, d_ref, mo_ref, vo_ref):
        g = g_ref[0]
        for k in range(1, N_DEV):
            g = g + g_ref[k]
        d, mn, vn = _adamw(w_ref[...], g, m_ref[...], v_ref[...])
        go_ref[...] = g
        d_ref[...] = d
        mo_ref[...] = mn
        vo_ref[...] = vn

    return pl.pallas_call(body, name=name, out_shape=(S((R, C), F32),) * 4, compiler_params=_params())(w, gs, m, v)


def kernel(x, norm_ffn1, w_ffn1_gu, w_ffn1_down, norm_mix, w_in, sinks, norm_out_sb, norm_out_swa, w_out, norm_ffn2, w_ffn2_gu, w_ffn2_down, rel_bias, norm_final, loss_target, m_norm_ffn1, m_w_ffn1_gu, m_w_ffn1_down, m_norm_mix, m_w_in, m_sinks, m_norm_out_sb, m_norm_out_swa, m_w_out, m_norm_ffn2, m_w_ffn2_gu, m_w_ffn2_down, m_rel_bias, m_norm_final, v_norm_ffn1, v_w_ffn1_gu, v_w_ffn1_down, v_norm_mix, v_w_in, v_sinks, v_norm_out_sb, v_norm_out_swa, v_w_out, v_norm_ffn2, v_w_ffn2_gu, v_w_ffn2_down, v_rel_bias, v_norm_final):
    L = norm_ffn1.shape[0]
    T, D = x.shape[1], x.shape[2]
    F = w_ffn1_down.shape[1] * N_DEV
    h = x.reshape(T, D)
    target = loss_target.reshape(T, D)
    after, upto, before = _tri_consts()
    bprev, bcur = _t5_buckets()

    local = {}
    for l in range(L):
        local[f"gu1_{l}"] = w_ffn1_gu[l].T.astype(BF16)
        local[f"d1_{l}"] = w_ffn1_down[l].astype(BF16)
        local[f"in_{l}"] = w_in[l].T.astype(BF16)
        local[f"out_{l}"] = w_out[l].astype(BF16)
        local[f"gu2_{l}"] = w_ffn2_gu[l].T.astype(BF16)
        local[f"d2_{l}"] = w_ffn2_down[l].astype(BF16)
    full, partial = {}, {}
    grads, chip_sum, recv_b = {}, {}, {}

    def run(fn, *args, ag=(), rs1=(), rs2=()):
        halves = lambda names: [n if isinstance(n, tuple) else (n, None) for n in names]
        ag, rs2 = [(n, k) for n, k in halves(ag) if n in local], halves(rs2)
        rows = lambda k, total: None if k is None else (k * (total // 2), total // 2)

        def second(n, k):
            sb = chip_sum[n][1]
            return scatter_second(sb, rows(k, sb.shape[1]), recv_b.get(n))

        riders = ([gather(local[n], rows(k, local[n].shape[0]), partial.get(n)) for n, k in ag]
                  + [scatter_first(grads[n][1]) for n in rs1] + [second(n, k) for n, k in rs2])
        if not riders:
            return fn(*args)
        outs, per = fn(*args, riders=riders)
        per = [p[0] for p in per]
        for n, k in ag:
            buf = per.pop(0)
            if k == 0:
                partial[n] = buf
            else:
                full[n] = buf.reshape(N_DEV * buf.shape[1], D)
        if rs1:
            sums = scatter_add([grads[n][0] for n in rs1], [per.pop(0) for n in rs1], "rs_add_" + "_".join(rs1))
            chip_sum.update(zip(rs1, sums))
        for n, _ in rs2:
            recv_b[n] = per.pop(0)
        return outs

    def attn_fwd(p, sink, name, riders=()):
        return side_by_side(sb_attn_fwd(p, after, name, riders=PARTS), swa_fwd(p, sink, rel_bias, bprev, bcur, name, riders=PARTS),
                            name, riders)

    def attn_bwd(p, do_sb, tot, do_sw, lse, sink, name, riders=()):
        return side_by_side(sb_attn_bwd(p, do_sb, tot, upto, before, name, riders=PARTS),
                            swa_bwd(p, do_sw, lse, sink, rel_bias, bprev, bcur, name, riders=PARTS), name, riders)

    gu = lambda n: full[n].reshape(2, F, D)
    slots = lambda pair: tuple(t.reshape(N_DEV, -1, D) for t in pair)
    vec = lambda a: a.reshape(1, -1)

    PW = max(D, SB_W + SWA_W)
    n_rows = 4 * L + 2
    n_rows += (-n_rows) % 8

    def pack(ffn1, mix, ffn2, final, osb, osw, snk, rel, extra):
        pieces = []

        def row(*parts):
            flat = [a.reshape(-1) for a in parts]
            pieces.extend(flat)
            used = sum(a.size for a in flat)
            if used < PW:
                pieces.append(jnp.zeros((PW - used,), F32))

        for group in (ffn1, mix, ffn2):
            for l in range(L):
                row(group[l])
        row(final)
        for l in range(L):
            row(osb[l], osw[l])
        row(*[snk[l].reshape(-1)[:8] for l in range(L)], rel, extra)
        pieces.append(jnp.zeros(((n_rows - 4 * L - 2) * PW,), F32))
        return jnp.concatenate(pieces).reshape(n_rows, PW)

    def unpack(arr):
        ffn1, mix, ffn2 = arr[0:L, :D], arr[L:2 * L, :D], arr[2 * L:3 * L, :D]
        final = arr[3 * L, :D]
        ob = arr[3 * L + 1:4 * L + 1]
        tail = arr[4 * L + 1]
        return (ffn1, mix, tail[:8 * L].reshape(L, 8), ob[:, :SB_W], ob[:, SB_W:SB_W + SWA_W], ffn2,
                tail[8 * L:8 * L + N_BUCKETS * 8].reshape(N_BUCKETS, 8), final)

    zero = jnp.zeros((1,), F32)
    w_small = pack(norm_ffn1, norm_mix, norm_ffn2, norm_final, norm_out_sb, norm_out_swa, sinks, rel_bias, zero)
    norm_ffn1, norm_mix, sinks, norm_out_sb, norm_out_swa, norm_ffn2, _, norm_final = unpack(w_small)

    saved = []
    n_next = run(rms_cast, h, vec(norm_ffn1[0]), "rms_first", ag=("gu1_0",))
    for l in range(L):
        nx = l + 1
        s = {"h0": h, "n1": n_next}
        s["gate1"], s["up1"], s["a1"] = run(ffn_up_fwd, s["n1"], gu(f"gu1_{l}"), f"ffn1_up{l}",
                                            ag=(f"d1_{l}", ("in_0", 0) if l == 0 else (f"in_{l}", 1)))
        h = run(ffn_down_fwd, s["a1"], full[f"d1_{l}"], h, None, f"ffn1_down{l}", ag=(("in_0", 1),) if l == 0 else ())
        s["h1"] = h
        s["n2"], s["p"] = mix_in_fwd(h, vec(norm_mix[l]), full[f"in_{l}"], f"mix_in{l}")
        s["o_sb"], s["tot"], s["o_sw"], s["lse"] = run(attn_fwd, s["p"], vec(sinks[l]), f"attn_fwd{l}",
                                                       ag=(f"out_{l}", f"gu2_{l}", f"d2_{l}", (f"gu1_{nx}", 0)))
        s["mixed"], h, s["n3"] = run(mix_out_fwd, s["o_sb"], s["o_sw"], vec(norm_out_sb[l]), vec(norm_out_swa[l]),
                                     full[f"out_{l}"], h, vec(norm_ffn2[l]), f"mix_out{l}")
        s["h2"] = h
        s["gate2"], s["up2"], s["a2"] = run(ffn_up_fwd, s["n3"], gu(f"gu2_{l}"), f"ffn2_up{l}",
                                            ag=((f"gu1_{nx}", 1), (f"in_{nx}", 0)))
        if nx < L:
            h, n_next = run(ffn_down_fwd, s["a2"], full[f"d2_{l}"], h, vec(norm_ffn1[nx]), f"ffn2_down{l}")
        else:
            h = run(ffn_down_fwd, s["a2"], full[f"d2_{l}"], h, None, f"ffn2_down{l}")
        saved.append(s)

    loss_part, dh, dhb, dg_final = loss_head(h, vec(norm_final), target, "loss_head")

    small = {k: [None] * L for k in ("ffn1", "mix", "sinks", "osb", "osw", "ffn2", "dsc")}
    for l in reversed(range(L)):
        s = saved[l]

        def ffn_bwd(dh, dhb, tag, gate, up, a, n, h_in, g, r_down, r_up):
            gu_n, d_n = f"gu{tag}_{l}", f"d{tag}_{l}"
            dgu, dwd, dwdb, dwgu, dwgub = run(ffn_down_bwd, dhb, full[d_n], gate, up, a, n, f"ffn{tag}_down_bwd{l}", **r_down)
            grads[gu_n], grads[d_n] = slots((dwgu, dwgub)), slots((dwd, dwdb))
            return run(nn_rms_bwd, dgu, gu(gu_n), h_in, g, dh, f"ffn{tag}_up_bwd{l}", **r_up)

        later = l + 1 < L
        dh, dhb, small["ffn2"][l] = ffn_bwd(dh, dhb, 2, s["gate2"], s["up2"], s["a2"], s["n3"], s["h2"], vec(norm_ffn2[l]),
                                            dict(rs2=((f"gu1_{l + 1}", 0), f"d1_{l + 1}") if later else ()),
                                            dict(rs1=(f"gu2_{l}", f"d2_{l}"), rs2=((f"gu1_{l + 1}", 1),) if later else ()))
        do_sb, do_sw, small["osb"][l], small["osw"][l], dw_out, dw_out_b = mix_out_bwd(
            dhb, full[f"out_{l}"], s["mixed"], s["o_sb"], s["o_sw"], vec(norm_out_sb[l]), vec(norm_out_swa[l]), f"mix_out_bwd{l}")
        grads[f"out_{l}"] = slots((dw_out, dw_out_b))
        dq_sb, dk_sb, dv_sb, dq_sw, dk_sw, dv_sw, small["sinks"][l], small["dsc"][l] = run(
            attn_bwd, s["p"], do_sb, s["tot"], do_sw, s["lse"], vec(sinks[l]), f"attn_bwd{l}",
            rs2=(f"gu2_{l}", f"d2_{l}"), rs1=(f"out_{l}",))
        dp = jnp.concatenate([dq_sb, dk_sb, dv_sb, dq_sw, dk_sw, dv_sw], axis=1)
        dh, dhb, small["mix"][l] = nn_rms_bwd(dp[None], full[f"in_{l}"][None], s["h1"], vec(norm_mix[l]), dh, f"mix_in_bwd{l}")
        grads[f"in_{l}"] = slots(tn_matmul(dp[None], s["n2"], 1.0, f"dwin{l}"))
        dh, dhb, small["ffn1"][l] = ffn_bwd(dh, dhb, 1, s["gate1"], s["up1"], s["a1"], s["n1"], s["h0"], vec(norm_ffn1[l]),
                                            dict(rs1=(f"in_{l}",), rs2=(f"out_{l}",)),
                                            dict(rs1=(f"gu1_{l}", f"d1_{l}"), rs2=(f"in_{l}",)))

    grad_x = dh.reshape(x.shape)

    upd = {}
    for nm, w, m, v, transposed, last in (
            ("gu2", w_ffn2_gu, m_w_ffn2_gu, v_w_ffn2_gu, True, (("gu1_0", 0),)), ("d2", w_ffn2_down, m_w_ffn2_down, v_w_ffn2_down, False, (("gu1_0", 1),)),
            ("in", w_in, m_w_in, v_w_in, True, ("d1_0",)), ("out", w_out, m_w_out, v_w_out, False, ()),
            ("gu1", w_ffn1_gu, m_w_ffn1_gu, v_w_ffn1_gu, True, ()), ("d1", w_ffn1_down, m_w_ffn1_down, v_w_ffn1_down, False, ())):
        turn = (lambda a: jnp.swapaxes(a, 1, 2)) if transposed else (lambda a: a)
        names = [f"{nm}_{l}" for l in range(L)]
        res = run(adamw_scattered, turn(w), turn(m), turn(v), [chip_sum[n][0] for n in names], [recv_b[n] for n in names],
                  f"adamw_{nm}", rs2=last)
        upd[nm] = tuple(turn(r) for r in res)

    d_rel = rel_bias_grad(small["dsc"], bprev, bcur, "rel_bias_grad")[:, :8]
    g_small = pack(small["ffn1"], small["mix"], small["ffn2"], dg_final, small["osb"], small["osw"], small["sinks"], d_rel,
                   loss_part[0, :1])
    m_small = pack(m_norm_ffn1, m_norm_mix, m_norm_ffn2, m_norm_final, m_norm_out_sb, m_norm_out_swa, m_sinks, m_rel_bias, zero)
    v_small = pack(v_norm_ffn1, v_norm_mix, v_norm_ffn2, v_norm_final, v_norm_out_sb, v_norm_out_swa, v_sinks, v_rel_bias, zero)
    gs_small = all_gather_rows(g_small, "ag_small")
    summed = adamw_small(w_small, gs_small, m_small, v_small, "adamw_small")
    small_out = [unpack(a) for a in summed]
    loss = summed[0][4 * L + 1, 8 * L + N_BUCKETS * 8]

    def group(k):
        sm = small_out[k]
        return (sm[0], upd["gu1"][k], upd["d1"][k], sm[1], upd["in"][k], sm[2], sm[3], sm[4], upd["out"][k], sm[5],
                upd["gu2"][k], upd["d2"][k], sm[6], sm[7])

    return (loss, grad_x, *group(0), *group(1), *group(2), *group(3))
```

```python
import math

import jax
import jax.numpy as jnp
from jax import lax
from jax.experimental import pallas as pl
from jax.experimental.pallas import tpu as pltpu

F32 = jnp.float32
BF16 = jnp.bfloat16
S = jax.ShapeDtypeStruct

N_DEV = 8
HEAD_DIM = 64
SB_HEADS = 8
PAIR = 2 * HEAD_DIM
SB_W = 512
SWA_W = 512
KV_W = 128
IN_W = 3 * SB_W + SWA_W + 2 * KV_W
QB = 128
N_BUCKETS = 32
MAX_DISTANCE = 128
EPS = 1e-6
NEG_INF = -1e30
SCALE = HEAD_DIM ** -0.5

ADAM_LR = 0.001
ADAM_B1 = 0.9
ADAM_B2 = 0.999
ADAM_EPS = 1e-08
ADAM_WD = 0.01
ADAM_STEP = 10

VMEM_LIMIT = 56 * 1024 * 1024
MESH = pl.DeviceIdType.MESH


def _params(sem=None, vmem=VMEM_LIMIT):
    return pltpu.CompilerParams(dimension_semantics=sem, vmem_limit_bytes=vmem)


def _nn(a, b):
    return jnp.dot(a, b, preferred_element_type=F32)


def _nt(a, b):
    return lax.dot_general(a, b, (((1,), (1,)), ((), ())), preferred_element_type=F32)


def _tn(a, b):
    return lax.dot_general(a, b, (((0,), (0,)), ((), ())), preferred_element_type=F32)


def _tri(xs, m):
    return [_nn(x.astype(BF16), m) for x in xs]


def _rms(x, g):
    r = lax.rsqrt(jnp.mean(x * x, axis=-1, keepdims=True) + EPS)
    return x * r * g


def _rms_bwd(dy, x, g):
    r = lax.rsqrt(jnp.mean(x * x, axis=-1, keepdims=True) + EPS)
    xhat = x * r
    u = dy * g
    dx = r * (u - xhat * jnp.mean(u * xhat, axis=-1, keepdims=True))
    return dx, jnp.sum(dy * xhat, axis=0, keepdims=True)


def _softplus_logsig(z):
    sp = jnp.maximum(z, 0.0) + jnp.log(1.0 + jnp.exp(-jnp.abs(z)))
    return sp, z - sp


def _gain(g):
    if isinstance(g, tuple):
        rows, n = g
        return rows, pl.BlockSpec((None, 1, rows.shape[2]), lambda *_: (n, 0, 0))
    return g, pl.BlockSpec((1, g.shape[1]), lambda *_: (0, 0))


def _tile(n, want):
    t = min(n, want)
    while n % t:
        t //= 2
    return t


def _place():
    x, y, c = lax.axis_index("x"), lax.axis_index("y"), lax.axis_index("c")
    chips = [(1 - x, y), (x, 1 - y), (1 - x, 1 - y)]
    return x, y, c, chips


def all_gather_rows(v, name):
    R, C = v.shape

    def body(v_ref, out_ref, send_sems, recv_sems, local_sem):
        x, y, c, chips = _place()
        me, sibling = (x, y, c), (x, y, 1 - c)

        def slot(px, py, pc):
            return out_ref.at[4 * px + 2 * py + pc]

        def copy(k, block, to, src=None):
            return pltpu.make_async_remote_copy(
                src_ref=slot(*block) if src is None else src, dst_ref=slot(*block),
                send_sem=send_sems.at[k], recv_sem=recv_sems.at[k], device_id=to, device_id_type=MESH)

        mine = pltpu.make_async_copy(v_ref, slot(*me), local_sem)
        mine.start()
        first = [copy(0, me, sibling, src=v_ref)]
        first += [copy(1 + j, me, (*chip, c), src=v_ref) for j, chip in enumerate(chips)]
        for cp in first:
            cp.start()
        passed = [copy(4 + j, (*chip, c), sibling) for j, chip in enumerate(chips)]
        for j, chip in enumerate(chips):
            copy(1 + j, (*chip, c), me).wait_recv()
            passed[j].start()
        copy(0, sibling, me).wait_recv()
        for j, chip in enumerate(chips):
            copy(4 + j, (*chip, 1 - c), me).wait_recv()
        for cp in first + passed:
            cp.wait_send()
        mine.wait()

    return pl.pallas_call(
        body, name=name, out_shape=S((N_DEV, R, C), v.dtype),
        in_specs=[pl.BlockSpec(memory_space=pl.ANY)], out_specs=pl.BlockSpec(memory_space=pl.ANY),
        scratch_shapes=[pltpu.SemaphoreType.DMA((7,)), pltpu.SemaphoreType.DMA((7,)), pltpu.SemaphoreType.DMA],
    )(v)


class _Exchange:
    def __init__(self, ins, outs, sizes, n_local, plan, aliases=None):
        self.ins, self.outs, self.plan, self.aliases = list(ins), list(outs), plan, aliases or {}
        self.sizes, self.n_local = list(sizes), n_local

    def scratch(self):
        n = sum(self.sizes)
        return [pltpu.SemaphoreType.DMA((n,)), pltpu.SemaphoreType.DMA((n,)), pltpu.SemaphoreType.DMA((max(self.n_local, 1),))]

    def _copies(self, in_refs, out_refs, sems):
        send_sems, recv_sems, local_sems = sems
        phases, local = self.plan(in_refs, out_refs)
        out, k = [], 0
        for phase in phases:
            out.append([pltpu.make_async_remote_copy(src_ref=s, dst_ref=d, send_sem=send_sems.at[k + n], recv_sem=recv_sems.at[k + n],
                                                     device_id=dev, device_id_type=MESH) for n, (s, d, dev) in enumerate(phase)])
            k += len(phase)
        return out, [pltpu.make_async_copy(s, d, local_sems.at[n]) for n, (s, d) in enumerate(local)]

    def start(self, in_refs, out_refs, sems):
        phases, loc = self._copies(in_refs, out_refs, sems)
        for cp in phases[0] + loc:
            cp.start()

    def advance(self, hook, in_refs, out_refs, sems):
        p = hook - (3 - len(self.sizes))
        if p >= 1:
            phases, _ = self._copies(in_refs, out_refs, sems)
            for cp in phases[p - 1]:
                cp.wait_recv()
            for cp in phases[p]:
                cp.start()

    def finish(self, in_refs, out_refs, sems):
        phases, loc = self._copies(in_refs, out_refs, sems)
        for cp in phases[-1]:
            cp.wait_recv()
        for phase in phases:
            for cp in phase:
                cp.wait_send()
        for cp in loc:
            cp.wait()


def gather(v, rows=None, into=None):
    R, C = v.shape
    r0, nr = rows or (0, R)
    na = min(nr, ((nr // 2 + 15) // 16) * 16)

    def plan(ins, outs):
        x, y, c, _ = _place()
        xn, yn, dg, sibling = (1 - x, y), (x, 1 - y), (1 - x, 1 - y), (x, y, 1 - c)
        slot = lambda chip, start=r0, count=nr: outs[0].at[4 * chip[0] + 2 * chip[1] + c, pl.ds(start, count), :]
        src, mine = ins[0].at[pl.ds(r0, nr), :], slot((x, y))
        same = lambda ref, to: (ref, ref, to)
        first = [(src, mine, sibling), (src, mine, (*xn, c)), (src, mine, (*yn, c))]
        relay = [same(slot(xn, r0, na), (*yn, c)), same(slot(yn, r0 + na, nr - na), (*xn, c))]
        onward = [same(slot(xn), sibling), same(slot(yn), sibling), same(slot(dg), sibling)]
        return [first, relay, onward], [(src, mine)]

    if into is None:
        return _Exchange([v], [S((N_DEV, R, C), v.dtype)], (3, 2, 3), 1, plan)
    return _Exchange([v, into], [S((N_DEV, R, C), v.dtype)], (3, 2, 3), 1, plan, aliases={1: 0})


def scatter_first(gb):
    _, R, C = gb.shape

    def plan(ins, outs):
        x, y, c, chips = _place()
        owners = [(x, y)] + chips
        return [[(ins[0].at[4 * px + 2 * py + (1 - c)], outs[0].at[j], (x, y, 1 - c)) for j, (px, py) in enumerate(owners)]], []

    return _Exchange([gb], [S((4, R, C), BF16)], (4,), 0, plan)


def scatter_second(sb, rows=None, into=None):
    r0, nr = rows or (0, sb.shape[1])

    def plan(ins, outs):
        x, y, c, chips = _place()
        part = lambda ref, j: ref.at[j, pl.ds(r0, nr), :]
        return [[(part(ins[0], j), part(outs[0], j), (*chips[j], c)) for j in range(3)]], []

    if into is None:
        return _Exchange([sb], [S(sb.shape, BF16)], (3,), 0, plan)
    return _Exchange([sb, into], [S(sb.shape, BF16)], (3,), 0, plan, aliases={1: 0})


PARTS = "parts"


def _call(body, *, name, grid, in_specs, out_specs, out_shape, args, scratch=(), sem=None, riders=(), marks=None):
    single = not isinstance(out_shape, (tuple, list))
    out_shape = (out_shape,) if single else tuple(out_shape)
    out_specs = (out_specs,) if single else tuple(out_specs)
    n_in, n_out, n_sc = len(in_specs), len(out_shape), len(scratch)
    if riders is PARTS:
        return dict(body=body, grid=grid, in_specs=list(in_specs), out_specs=out_specs, out_shape=out_shape, args=tuple(args),
                    scratch=list(scratch), marks=marks)
    if not riders:
        res = pl.pallas_call(body, name=name, grid=grid, in_specs=list(in_specs), out_specs=out_specs, out_shape=out_shape,
                             scratch_shapes=list(scratch), compiler_params=_params(sem))(*args)
        return res[0] if single else res
    r_ins = [a for r in riders for a in r.ins]
    r_outs = [o for r in riders for o in r.outs]
    r_scr = [s for r in riders for s in r.scratch()]
    aliases, i0, o0 = {}, n_in, n_out
    for r in riders:
        for a, b in r.aliases.items():
            aliases[i0 + a] = o0 + b
        i0, o0 = i0 + len(r.ins), o0 + len(r.outs)
    steps = math.prod(grid)

    def full(*refs):
        ins, rin = refs[:n_in], refs[n_in:n_in + len(r_ins)]
        pos = n_in + len(r_ins)
        outs, rout = refs[pos:pos + n_out], refs[pos + n_out:pos + n_out + len(r_outs)]
        pos += n_out + len(r_outs)
        sc, rsc = refs[pos:pos + n_sc], refs[pos + n_sc:]
        step = 0
        for d, n in enumerate(grid):
            step = step * n + pl.program_id(d)

        def each(method, *lead):
            i, o = 0, 0
            for k, r in enumerate(riders):
                getattr(r, method)(*lead, rin[i:i + len(r.ins)], rout[o:o + len(r.outs)], rsc[3 * k:3 * k + 3])
                i, o = i + len(r.ins), o + len(r.outs)

        @pl.when(step == 0)
        def _():
            each("start")
        body(*ins, *outs, *sc)

        late = max(steps - 1 - max(steps // 8, 1), 0)
        first, second = marks or (min((3 * steps) // 5, late), late)

        @pl.when(step == first)
        def _():
            each("advance", 1)

        @pl.when(step == second)
        def _():
            each("advance", 2)

        @pl.when(step == steps - 1)
        def _():
            each("finish")

    anywhere = pl.BlockSpec(memory_space=pl.ANY)
    res = pl.pallas_call(
        full, name=name, grid=grid, in_specs=list(in_specs) + [anywhere] * len(r_ins),
        out_specs=out_specs + (anywhere,) * len(r_outs), out_shape=out_shape + tuple(r_outs),
        scratch_shapes=list(scratch) + r_scr, input_output_aliases=aliases,
        compiler_params=_params(("arbitrary",) * len(grid)))(*args, *r_ins)
    host, rest, per = res[:n_out], list(res[n_out:]), []
    for r in riders:
        per.append(rest[:len(r.outs)])
        rest = rest[len(r.outs):]
    return (host[0] if single else tuple(host)), per


def side_by_side(first, second, name, riders=()):
    a_in, a_out, a_sc = len(first["in_specs"]), len(first["out_shape"]), len(first["scratch"])
    n_in, n_out = a_in + len(second["in_specs"]), a_out + len(second["out_shape"])

    def body(*refs):
        ins, outs, sc = refs[:n_in], refs[n_in:n_in + n_out], refs[n_in + n_out:]
        first["body"](*ins[:a_in], *outs[:a_out], *sc[:a_sc])
        second["body"](*ins[a_in:], *outs[a_out:], *sc[a_sc:])

    return _call(body, name=name, grid=first["grid"], in_specs=first["in_specs"] + second["in_specs"],
                 out_specs=first["out_specs"] + second["out_specs"], out_shape=first["out_shape"] + second["out_shape"],
                 args=first["args"] + second["args"], scratch=first["scratch"] + second["scratch"],
                 sem=("arbitrary",) * len(first["grid"]), riders=riders, marks=first["marks"])


def _rows_tile(n, cap):
    return max(t for t in range(16, min(n, cap) + 1, 16) if n % t == 0)


def scatter_add(gs, ras, name):
    C = gs[0].shape[2]
    trs = [_rows_tile(g.shape[1], 176) for g in gs]
    nts = [g.shape[1] // tr for g, tr in zip(gs, trs)]
    steps = max(nts)
    x, y, c, chips = _place()
    slots = jnp.stack([4 * px + 2 * py + c for px, py in [(x, y)] + chips]).astype(jnp.int32)

    def body(s_ref, *refs):
        ins, outs = refs[:5 * len(gs)], refs[5 * len(gs):]
        for k in range(len(gs)):
            g0, g1, g2, g3, ra_ref = ins[5 * k:5 * k + 5]
            own_ref, sb_ref = outs[2 * k:2 * k + 2]

            def work(g0=g0, g1=g1, g2=g2, g3=g3, ra_ref=ra_ref, own_ref=own_ref, sb_ref=sb_ref):
                own_ref[...] = g0[...] + ra_ref[0].astype(F32)
                for j, gj in enumerate((g1, g2, g3)):
                    sb_ref[j] = (gj[...] + ra_ref[j + 1].astype(F32)).astype(BF16)

            if nts[k] == steps:
                work()
            else:
                pl.when(pl.program_id(0) < nts[k])(work)

    in_specs, out_specs, out_shape, args = [], [], [], [slots]
    for k, (g, ra, tr) in enumerate(zip(gs, ras, trs)):
        tile = lambda i, k=k: jnp.minimum(i, nts[k] - 1)
        in_specs += [pl.BlockSpec((None, tr, C), lambda i, s, j=j, tile=tile: (s[j], tile(i), 0)) for j in range(4)]
        in_specs.append(pl.BlockSpec((4, tr, C), lambda i, s, tile=tile: (0, tile(i), 0)))
        out_specs += [pl.BlockSpec((tr, C), lambda i, s, tile=tile: (tile(i), 0)),
                      pl.BlockSpec((3, tr, C), lambda i, s, tile=tile: (0, tile(i), 0))]
        out_shape += [S((g.shape[1], C), F32), S((3, g.shape[1], C), BF16)]
        args += [g, g, g, g, ra]
    spec = pltpu.PrefetchScalarGridSpec(num_scalar_prefetch=1, grid=(steps,), in_specs=in_specs, out_specs=tuple(out_specs))
    res = pl.pallas_call(body, name=name, grid_spec=spec, out_shape=tuple(out_shape), compiler_params=_params(("arbitrary",)))(*args)
    return [(res[2 * k], res[2 * k + 1]) for k in range(len(gs))]


def rms_cast(h, g, name, riders=()):
    T, D = h.shape
    tm = _tile(T, 512)

    def body(h_ref, g_ref, n_ref):
        n_ref[...] = _rms(h_ref[...], g_ref[...]).astype(BF16)

    row = pl.BlockSpec((tm, D), lambda i: (i, 0))
    g, g_spec = _gain(g)
    return _call(body, name=name, grid=(T // tm,), out_shape=S((T, D), BF16), in_specs=[row, g_spec],
                 out_specs=row, sem=("parallel",), args=(h, g), riders=riders)


def ffn_up_fwd(n, wgu, name, riders=()):
    T, D = n.shape
    F = wgu.shape[1]
    tr, tn = _tile(T, 512), _tile(F, 256)

    def body(n_ref, wg_ref, wu_ref, dgate_ref, dup_ref, a_ref):
        wg, wu = wg_ref[...], wu_ref[...]
        for r in range(T // tr):
            rows = slice(r * tr, (r + 1) * tr)
            x = n_ref[rows, :]
            gate = _nt(x, wg)
            up = _nt(x, wu)
            s = jax.nn.sigmoid(gate)
            silu = gate * s
            dgate_ref[rows, :] = (up * (s * (1.0 + gate * (1.0 - s)))).astype(BF16)
            dup_ref[rows, :] = silu.astype(BF16)
            a_ref[rows, :] = (silu * up).astype(BF16)

    tile = pl.BlockSpec((T, tn), lambda j: (0, j))
    return _call(
        body, name=name, grid=(F // tn,), out_shape=(S((T, F), BF16),) * 3,
        in_specs=[pl.BlockSpec((T, D), lambda j: (0, 0)),
                  pl.BlockSpec((None, tn, D), lambda j: (0, j, 0)), pl.BlockSpec((None, tn, D), lambda j: (1, j, 0))],
        out_specs=(tile, tile, tile), sem=("parallel",), args=(n, wgu, wgu), riders=riders)


def ffn_down_fwd(a, wd, h, g_next, name, riders=()):
    T, F = a.shape
    D = wd.shape[1]
    tm = _tile(T, 256)

    def body(a_ref, w_ref, h_ref, *rest):
        out = h_ref[...] + 0.5 * _nn(a_ref[...], w_ref[...])
        if g_next is None:
            rest[0][...] = out
        else:
            g_ref, o_ref, n_ref = rest
            o_ref[...] = out
            n_ref[...] = _rms(out, g_ref[...]).astype(BF16)

    row = pl.BlockSpec((tm, D), lambda i: (i, 0))
    more = g_next is not None
    g_arg, g_spec = _gain(g_next) if more else (None, None)
    return _call(
        body, name=name, grid=(T // tm,), out_shape=(S((T, D), F32), S((T, D), BF16)) if more else S((T, D), F32),
        in_specs=[pl.BlockSpec((tm, F), lambda i: (i, 0)), pl.BlockSpec((F, D), lambda i: (0, 0)), row] + ([g_spec] if more else []),
        out_specs=(row, row) if more else row,
        sem=("parallel",), args=(a, wd, h) + ((g_arg,) if more else ()), riders=riders)


def mix_in_fwd(h, g, win, name):
    T, D = h.shape
    N = win.shape[0]
    tm = _tile(T, 256)

    def body(h_ref, g_ref, w_ref, n_ref, p_ref):
        n = _rms(h_ref[...], g_ref[...]).astype(BF16)
        n_ref[...] = n
        p_ref[...] = _nt(n, w_ref[...]).astype(BF16)

    g, g_spec = _gain(g)
    return pl.pallas_call(
        body, name=name, grid=(T // tm,), out_shape=(S((T, D), BF16), S((T, N), BF16)),
        in_specs=[pl.BlockSpec((tm, D), lambda i: (i, 0)), g_spec, pl.BlockSpec((N, D), lambda i: (0, 0))],
        out_specs=(pl.BlockSpec((tm, D), lambda i: (i, 0)), pl.BlockSpec((tm, N), lambda i: (i, 0))),
        compiler_params=_params(("parallel",)),
    )(h, g, win)


def _tri_consts():
    r = lax.broadcasted_iota(jnp.int32, (QB, QB), 0)
    c = lax.broadcasted_iota(jnp.int32, (QB, QB), 1)
    ones = jnp.ones((QB, QB), BF16)
    with_sums = lambda tri: jnp.concatenate([tri.astype(BF16), ones], axis=1)
    return with_sums(r > c), with_sums(r <= c), with_sums(r < c)


def _half_masks():
    lane = lax.broadcasted_iota(jnp.int32, (QB, PAIR), 1)
    row = lax.broadcasted_iota(jnp.int32, (QB, PAIR), 0)
    return lane < HEAD_DIM, lane, row


def sb_attn_fwd(p, after, name, riders=()):
    T = p.shape[0]
    nq = T // QB

    def body(q_ref, k_ref, v_ref, m_ref, o_ref, tot_ref, q_sc, acc_ref, z_sc):
        i = pl.program_id(0)
        lo, lane, row = _half_masks()
        causal = lane < row
        heads, pairs = range(SB_HEADS), range(SB_HEADS // 2)
        for hp in pairs:
            q_sc[hp] = (q_ref[:, hp * PAIR:(hp + 1) * PAIR].astype(F32) * SCALE).astype(BF16)
        m2 = m_ref[...]

        def by_head(ref, j, hp):
            t = ref[pl.ds(pl.multiple_of(j * QB, QB), QB), hp * PAIR:(hp + 1) * PAIR]
            return jnp.concatenate([jnp.where(lo, t, 0), jnp.where(lo, 0, t)], axis=0)

        def scores(j):
            return [_nt(q_sc[hp], by_head(k_ref, j, hp)) for hp in pairs]

        def block(j, diag):
            z2 = [z_sc[hp] for hp in pairs]
            ahead = scores(jnp.maximum(j - 1, 0))
            for hp in pairs:
                z_sc[hp] = ahead[hp]
            vs = [by_head(v_ref, j, hp) for hp in pairs]
            spls = [_softplus_logsig(z2[h // 2][:, (h % 2) * QB:(h % 2 + 1) * QB]) for h in heads]
            sp = [jnp.where(causal, spls[h][0], 0.0) if diag else spls[h][0] for h in heads]
            rr = _tri(sp, m2)
            if diag:
                w = [jnp.where(causal, jnp.exp(spls[h][1] - rr[h][:, :QB]), 0.0).astype(BF16) for h in heads]
            else:
                c = [tot_ref[:, h * QB:(h + 1) * QB] for h in heads]
                w = [jnp.exp(spls[h][1] - (c[h] + rr[h][:, :QB])).astype(BF16) for h in heads]
            pv = [_nn(jnp.concatenate([w[2 * hp], w[2 * hp + 1]], axis=1), vs[hp]) for hp in pairs]
            for hp in pairs:
                acc_ref[hp] = pv[hp] if diag else acc_ref[hp] + pv[hp]
            for h in heads:
                tot_ref[:, h * QB:(h + 1) * QB] = rr[h][:, QB:] if diag else c[h] + rr[h][:, QB:]

        first = scores(i)
        for hp in pairs:
            z_sc[hp] = first[hp]
        block(i, True)

        def step(t, carry):
            block(i - 1 - t, False)
            return carry
        lax.fori_loop(0, i, step, 0)
        for hp in pairs:
            o_ref[:, hp * PAIR:(hp + 1) * PAIR] = acc_ref[hp]

    npair = SB_HEADS // 2
    return _call(
        body, name=name, grid=(nq,), out_shape=(S((T, SB_W), F32), S((T, SB_HEADS * QB), F32)),
        in_specs=[pl.BlockSpec((QB, SB_W), lambda i: (i, 0)), pl.BlockSpec((T, SB_W), lambda i: (0, 1)),
                  pl.BlockSpec((T, SB_W), lambda i: (0, 2)), pl.BlockSpec((QB, 2 * QB), lambda i: (0, 0))],
        out_specs=(pl.BlockSpec((QB, SB_W), lambda i: (i, 0)), pl.BlockSpec((QB, SB_HEADS * QB), lambda i: (i, 0))),
        scratch=[pltpu.VMEM((npair, QB, PAIR), BF16), pltpu.VMEM((npair, QB, PAIR), F32), pltpu.VMEM((npair, QB, 2 * QB), F32)],
        sem=("arbitrary",), args=(p, p, p, after), riders=riders,
        marks=((11 * nq) // 16, (14 * nq) // 16))


def sb_attn_bwd(p, do, tot, upto, before, name, riders=()):
    T = p.shape[0]
    nq = T // QB

    def body(q_ref, k_ref, v_ref, do_ref, tot_ref, mp_ref, mg_ref, dq_ref, dk_ref, dv_ref,
             q_sc, d_sc, qd_sc, pg_sc, dq_acc, dk_acc, dv_acc, zd_sc):
        i = pl.program_id(0)
        lo, lane, row = _half_masks()
        causal = lane < row
        heads, pairs = range(SB_HEADS), range(SB_HEADS // 2)

        def by_head(t):
            return jnp.concatenate([jnp.where(lo, t, 0), jnp.where(lo, 0, t)], axis=0)

        for hp in pairs:
            q2 = (q_ref[:, hp * PAIR:(hp + 1) * PAIR].astype(F32) * SCALE).astype(BF16)
            d2 = do_ref[:, hp * PAIR:(hp + 1) * PAIR].astype(BF16)
            q_sc[hp] = q2
            d_sc[hp] = d2
            qd_sc[hp] = by_head(q2)
            qd_sc[SB_HEADS // 2 + hp] = by_head(d2)
        mp, mg = mp_ref[...], mg_ref[...]

        @pl.when(i == 0)
        def _():
            dk_acc[...] = jnp.zeros_like(dk_acc)
            dv_acc[...] = jnp.zeros_like(dv_acc)
        pg_sc[...] = jnp.zeros_like(pg_sc)
        dq_acc[...] = jnp.zeros_like(dq_acc)

        def rows(ref, j, hp):
            return ref[pl.ds(pl.multiple_of(j * QB, QB), QB), hp * PAIR:(hp + 1) * PAIR]

        def products(j):
            return ([_nt(q_sc[hp], by_head(rows(k_ref, j, hp))) for hp in pairs]
                    + [_nt(d_sc[hp], by_head(rows(v_ref, j, hp))) for hp in pairs])

        def block(j, diag):
            r0 = pl.multiple_of(j * QB, QB)
            half = lambda t, h: t[:, (h % 2) * QB:(h % 2 + 1) * QB]
            z = [half(zd_sc[h // 2], h) for h in heads]
            dw = [half(zd_sc[SB_HEADS // 2 + h // 2], h) for h in heads]
            if not diag:
                ahead = products(j + 1)
                for hp in range(SB_HEADS):
                    zd_sc[hp] = ahead[hp]
            ks = [by_head(rows(k_ref, j, hp)) for hp in pairs]
            spls = [_softplus_logsig(z[h]) for h in heads]
            sp = [jnp.where(causal, spls[h][0], 0.0) if diag else spls[h][0] for h in heads]
            rr = _tri(sp, mp)
            pc = [pg_sc[2 * h] for h in heads]
            w = [jnp.exp(spls[h][1] - (tot_ref[:, h * QB:(h + 1) * QB] - (pc[h] + rr[h][:, :QB]))) for h in heads]
            if diag:
                w = [jnp.where(causal, w[h], 0.0) for h in heads]
            gg = [dw[h] * w[h] for h in heads]
            rg = _tri(gg, mg)
            gc = [pg_sc[2 * h + 1] for h in heads]
            dz = [gg[h] - (gg[h] + gc[h] + rg[h][:, :QB]) * jnp.exp(spls[h][1]) for h in heads]
            if diag:
                dz = [jnp.where(causal, dz[h], 0.0) for h in heads]
            dzb = [dz[h].astype(BF16) for h in heads]
            wb = [w[h].astype(BF16) for h in heads]
            both = lambda t, hp, axis: jnp.concatenate([t[2 * hp], t[2 * hp + 1]], axis=axis)
            dq = [_nn(both(dzb, hp, 1), ks[hp]) for hp in pairs]
            dk = [_tn(both(dzb, hp, 0), qd_sc[hp]) for hp in pairs]
            dv = [_tn(both(wb, hp, 0), qd_sc[SB_HEADS // 2 + hp]) for hp in pairs]
            for h in heads:
                if not diag:
                    pg_sc[2 * h] = pc[h] + rr[h][:, QB:]
                    pg_sc[2 * h + 1] = gc[h] + rg[h][:, QB:]
            for hp in pairs:
                dq_acc[hp] += dq[hp]
                dk_acc[pl.ds(r0, QB), hp * PAIR:(hp + 1) * PAIR] += dk[hp]
                dv_acc[pl.ds(r0, QB), hp * PAIR:(hp + 1) * PAIR] += dv[hp]

        first = products(0)
        for hp in range(SB_HEADS):
            zd_sc[hp] = first[hp]

        def step(t, carry):
            block(t, False)
            return carry
        lax.fori_loop(0, i, step, 0)
        block(i, True)
        for hp in pairs:
            dq_ref[:, hp * PAIR:(hp + 1) * PAIR] = (dq_acc[hp] * SCALE).astype(BF16)

        @pl.when(i == nq - 1)
        def _():
            dk_ref[...] = dk_acc[...].astype(BF16)
            dv_ref[...] = dv_acc[...].astype(BF16)

    qtile = pl.BlockSpec((QB, SB_W), lambda i: (i, 0))
    whole = pl.BlockSpec((T, SB_W), lambda i: (0, 0))
    const = pl.BlockSpec((QB, 2 * QB), lambda i: (0, 0))
    return _call(
        body, name=name, grid=(nq,), out_shape=(S((T, SB_W), BF16),) * 3,
        in_specs=[qtile, pl.BlockSpec((T, SB_W), lambda i: (0, 1)), pl.BlockSpec((T, SB_W), lambda i: (0, 2)), qtile,
                  pl.BlockSpec((QB, SB_HEADS * QB), lambda i: (i, 0)), const, const],
        out_specs=(qtile, whole, whole),
        scratch=[pltpu.VMEM((SB_HEADS // 2, QB, PAIR), BF16), pltpu.VMEM((SB_HEADS // 2, QB, PAIR), BF16),
                 pltpu.VMEM((SB_HEADS, 2 * QB, PAIR), BF16),
                 pltpu.VMEM((2 * SB_HEADS, QB, QB), F32), pltpu.VMEM((SB_HEADS // 2, QB, PAIR), F32),
                 pltpu.VMEM((T, SB_W), F32), pltpu.VMEM((T, SB_W), F32), pltpu.VMEM((SB_HEADS, QB, 2 * QB), F32)],
        sem=("arbitrary",), args=(p, p, p, do, tot, upto, before), riders=riders)


def _t5_buckets():
    a = lax.broadcasted_iota(jnp.int32, (QB, QB), 0)
    c = lax.broadcasted_iota(jnp.int32, (QB, QB), 1)

    def bucket(dist):
        dist = jnp.maximum(dist, 0)
        max_exact = N_BUCKETS // 2
        d = jnp.maximum(dist, 1).astype(F32)
        large = max_exact + (jnp.log(d / max_exact) / math.log(MAX_DISTANCE / max_exact)
                             * (N_BUCKETS - max_exact)).astype(jnp.int32)
        large = jnp.minimum(large, N_BUCKETS - 1)
        return jnp.where(dist < max_exact, dist, large)

    return bucket(QB + a - c), bucket(a - c)


def _swa_common(i, kp_ref, kc_ref, vp_ref, vc_ref, bp_ref, bc_ref, rb_ref, bias_ref):
    lo, lane, row = _half_masks()

    @pl.when(i == 0)
    def _():
        for blk, b_ref in enumerate((bp_ref, bc_ref)):
            bk = b_ref[...]
            for h in range(8):
                acc = jnp.zeros((QB, QB), F32)
                for b in range(N_BUCKETS):
                    acc = jnp.where(bk == b, rb_ref[b, h], acc)
                bias_ref[h, blk] = acc

    band = [(lane > row) & (i > 0), lane <= row]

    def stacks(ref):
        t = ref[...].astype(F32)
        sw = pltpu.roll(t, HEAD_DIM, 1)
        return [jnp.concatenate([jnp.where(lo, t, 0.0), jnp.where(lo, 0.0, sw)], axis=0).astype(BF16),
                jnp.concatenate([jnp.where(lo, sw, 0.0), jnp.where(lo, 0.0, t)], axis=0).astype(BF16)]

    ks = [stacks(kp_ref), stacks(kc_ref)]
    vs = [stacks(vp_ref), stacks(vc_ref)]
    return lo, band, ks, vs


def _lane_half(t, h):
    return t[:, (h % 2) * QB:(h % 2 + 1) * QB]


def swa_fwd(p, sinks, rel_bias, bprev, bcur, name, riders=()):
    T = p.shape[0]
    nq = T // QB
    kcol, vcol = (3 * SB_W + SWA_W) // KV_W, (3 * SB_W + SWA_W) // KV_W + 1
    sinks, srow = sinks if isinstance(sinks, tuple) else (sinks, 0)

    def body(q_ref, kp_ref, kc_ref, vp_ref, vc_ref, bp_ref, bc_ref, sink_ref, rb_ref, o_ref, lse_ref, bias_ref):
        i = pl.program_id(0)
        lo, band, ks, vs = _swa_common(i, kp_ref, kc_ref, vp_ref, vc_ref, bp_ref, bc_ref, rb_ref, bias_ref)
        heads, pairs, blocks = range(8), range(4), range(2)
        rowmax = lambda t: jnp.max(t, axis=1, keepdims=True)
        rowsum = lambda t: jnp.sum(t, axis=1, keepdims=True)
        q2 = [q_ref[:, g * PAIR:(g + 1) * PAIR] for g in pairs]
        s2 = [[_nt(q2[g], ks[b][g // 2]) for b in blocks] for g in pairs]
        sc = [[jnp.where(band[b], _lane_half(s2[h // 2][b], h) * SCALE + bias_ref[h, b], NEG_INF) for b in blocks] for h in heads]
        sink = [sink_ref[srow, h] for h in heads]
        m = [jnp.maximum(jnp.maximum(rowmax(sc[h][0]), rowmax(sc[h][1])), sink[h]) for h in heads]
        e = [[jnp.exp(sc[h][b] - m[h]) for b in blocks] for h in heads]
        den = [rowsum(e[h][0]) + rowsum(e[h][1]) + jnp.exp(sink[h] - m[h]) for h in heads]
        pb = [[(e[h][b] / den[h]).astype(BF16) for b in blocks] for h in heads]
        for g in pairs:
            both = lambda b: jnp.concatenate([pb[2 * g][b], pb[2 * g + 1][b]], axis=1)
            o_ref[:, g * PAIR:(g + 1) * PAIR] = _nn(both(0), vs[0][g // 2]) + _nn(both(1), vs[1][g // 2])
        for h in heads:
            lse_ref[:, h * QB:(h + 1) * QB] = jnp.broadcast_to(m[h] + jnp.log(den[h]), (QB, QB))

    kv = lambda col, prev: pl.BlockSpec((QB, KV_W), (lambda i: (jnp.maximum(i - 1, 0), col)) if prev else (lambda i: (i, col)))
    full = pl.BlockSpec((QB, QB), lambda i: (0, 0))
    smem = pl.BlockSpec(memory_space=pltpu.SMEM)
    return _call(
        body, name=name, grid=(nq,), out_shape=(S((T, SWA_W), F32), S((T, 8 * QB), F32)),
        in_specs=[pl.BlockSpec((QB, SWA_W), lambda i: (i, 3)), kv(kcol, True), kv(kcol, False), kv(vcol, True), kv(vcol, False),
                  full, full, smem, smem],
        out_specs=(pl.BlockSpec((QB, SWA_W), lambda i: (i, 0)), pl.BlockSpec((QB, 8 * QB), lambda i: (i, 0))),
        scratch=[pltpu.VMEM((8, 2, QB, QB), F32)],
        sem=("arbitrary",), args=(p, p, p, p, p, bprev, bcur, sinks, rel_bias), riders=riders)


def swa_bwd(p, do, lse, sinks, rel_bias, bprev, bcur, name, riders=()):
    T = p.shape[0]
    nq = T // QB
    kcol, vcol = (3 * SB_W + SWA_W) // KV_W, (3 * SB_W + SWA_W) // KV_W + 1
    sinks, srow = sinks if isinstance(sinks, tuple) else (sinks, 0)

    def body(q_ref, kp_ref, kc_ref, vp_ref, vc_ref, do_ref, lse_ref, bp_ref, bc_ref, sink_ref, rb_ref,
             dq_ref, dk_ref, dv_ref, dsink_ref, dsc_ref, bias_ref, dk_acc, dv_acc):
        i = pl.program_id(0)
        lo, band, ks, vs = _swa_common(i, kp_ref, kc_ref, vp_ref, vc_ref, bp_ref, bc_ref, rb_ref, bias_ref)

        @pl.when(i == 0)
        def _():
            dk_acc[...] = jnp.zeros_like(dk_acc)
            dv_acc[...] = jnp.zeros_like(dv_acc)
            dsc_ref[...] = jnp.zeros_like(dsc_ref)
            dsink_ref[...] = jnp.zeros_like(dsink_ref)

        heads, pairs, blocks = range(8), range(4), range(2)
        rowsum = lambda t: jnp.sum(t, axis=1, keepdims=True)
        by_head = lambda t: jnp.concatenate([jnp.where(lo, t, 0), jnp.where(lo, 0, t)], axis=0)
        q2 = [q_ref[:, g * PAIR:(g + 1) * PAIR] for g in pairs]
        d2 = [do_ref[:, g * PAIR:(g + 1) * PAIR].astype(BF16) for g in pairs]
        qs = [by_head(q2[g]) for g in pairs]
        dos = [by_head(d2[g]) for g in pairs]
        s2 = [[_nt(q2[g], ks[b][g // 2]) for b in blocks] for g in pairs]
        dp2 = [[_nt(d2[g], vs[b][g // 2]) for b in blocks] for g in pairs]
        lse_h = [lse_ref[:, h * QB:(h + 1) * QB] for h in heads]
        sink = [sink_ref[srow, h] for h in heads]
        pr = [[jnp.exp(jnp.where(band[b], _lane_half(s2[h // 2][b], h) * SCALE + bias_ref[h, b], NEG_INF) - lse_h[h])
               for b in blocks] for h in heads]
        dp = [[_lane_half(dp2[h // 2][b], h) for b in blocks] for h in heads]
        delta = [rowsum(pr[h][0] * dp[h][0]) + rowsum(pr[h][1] * dp[h][1]) for h in heads]
        lane1 = lax.broadcasted_iota(jnp.int32, (1, QB), 1)
        dsink = jnp.zeros((1, QB), F32)
        for h in heads:
            dsink = dsink + jnp.where(lane1 == h, -jnp.sum(jnp.exp(sink[h] - lse_h[h][:, :1]) * delta[h]), 0.0)
        dsink_ref[...] += dsink
        dsc = [[pr[h][b] * (dp[h][b] - delta[h]) for b in blocks] for h in heads]
        for h in heads:
            for b in blocks:
                dsc_ref[h, b] += dsc[h][b]
        dzb = [[(dsc[h][b] * SCALE).astype(BF16) for b in blocks] for h in heads]
        prb = [[pr[h][b].astype(BF16) for b in blocks] for h in heads]
        pair_of = lambda t, g, b, axis: jnp.concatenate([t[2 * g][b], t[2 * g + 1][b]], axis=axis)
        for g in pairs:
            dq = _nn(pair_of(dzb, g, 0, 1), ks[0][g // 2]) + _nn(pair_of(dzb, g, 1, 1), ks[1][g // 2])
            dq_ref[:, g * PAIR:(g + 1) * PAIR] = dq.astype(BF16)

        def key_grad(t, other, b):
            per_kv = [_tn(pair_of(t, 2 * kh, b, 0), other[2 * kh]) + _tn(pair_of(t, 2 * kh + 1, b, 0), other[2 * kh + 1]) for kh in range(2)]
            both = [s + pltpu.roll(s, HEAD_DIM, 1) for s in per_kv]
            return jnp.where(lo, both[0], both[1])

        rp = pl.multiple_of(jnp.maximum(i - 1, 0) * QB, QB)
        rc = pl.multiple_of(i * QB, QB)
        dk_acc[pl.ds(rp, QB), :] += key_grad(dzb, qs, 0)
        dv_acc[pl.ds(rp, QB), :] += key_grad(prb, dos, 0)
        dk_acc[pl.ds(rc, QB), :] += key_grad(dzb, qs, 1)
        dv_acc[pl.ds(rc, QB), :] += key_grad(prb, dos, 1)

        @pl.when(i == nq - 1)
        def _():
            dk_ref[...] = dk_acc[...].astype(BF16)
            dv_ref[...] = dv_acc[...].astype(BF16)

    kv = lambda col, prev: pl.BlockSpec((QB, KV_W), (lambda i: (jnp.maximum(i - 1, 0), col)) if prev else (lambda i: (i, col)))
    full = pl.BlockSpec((QB, QB), lambda i: (0, 0))
    smem = pl.BlockSpec(memory_space=pltpu.SMEM)
    whole = lambda shape: pl.BlockSpec(shape, lambda i: (0,) * len(shape))
    return _call(
        body, name=name, grid=(nq,),
        out_shape=(S((T, SWA_W), BF16), S((T, KV_W), BF16), S((T, KV_W), BF16), S((1, QB), F32), S((8, 2, QB, QB), F32)),
        in_specs=[pl.BlockSpec((QB, SWA_W), lambda i: (i, 3)), kv(kcol, True), kv(kcol, False), kv(vcol, True), kv(vcol, False),
                  pl.BlockSpec((QB, SWA_W), lambda i: (i, 0)), pl.BlockSpec((QB, 8 * QB), lambda i: (i, 0)),
                  full, full, smem, smem],
        out_specs=(pl.BlockSpec((QB, SWA_W), lambda i: (i, 0)), whole((T, KV_W)), whole((T, KV_W)), whole((1, QB)),
                   whole((8, 2, QB, QB))),
        scratch=[pltpu.VMEM((8, 2, QB, QB), F32), pltpu.VMEM((T, KV_W), F32), pltpu.VMEM((T, KV_W), F32)],
        sem=("arbitrary",), args=(p, p, p, p, p, do, lse, bprev, bcur, sinks, rel_bias), riders=riders)


def mix_out_fwd(o_sb, o_sw, g_sb, g_sw, wout, h, g_next, name, riders=()):
    T, D = h.shape
    M = SB_W + SWA_W
    tm = _tile(T, 256)

    def body(a_ref, b_ref, ga_ref, gb_ref, w_ref, h_ref, gn_ref, mx_ref, o_ref, n_ref):
        mx_ref[:, :SB_W] = _rms(a_ref[...], ga_ref[...]).astype(BF16)
        mx_ref[:, SB_W:] = _rms(b_ref[...], gb_ref[...]).astype(BF16)
        out = h_ref[...] + _nn(mx_ref[...], w_ref[...])
        o_ref[...] = out
        n_ref[...] = _rms(out, gn_ref[...]).astype(BF16)

    row = lambda n: pl.BlockSpec((tm, n), lambda i: (i, 0))
    (g_sb, sb_spec), (g_sw, sw_spec), (g_next, next_spec) = _gain(g_sb), _gain(g_sw), _gain(g_next)
    return _call(
        body, name=name, grid=(T // tm,), out_shape=(S((T, M), BF16), S((T, D), F32), S((T, D), BF16)),
        in_specs=[row(SB_W), row(SWA_W), sb_spec, sw_spec, pl.BlockSpec((M, D), lambda i: (0, 0)), row(D), next_spec],
        out_specs=(row(M), row(D), row(D)),
        sem=("parallel",), args=(o_sb, o_sw, g_sb, g_sw, wout, h, g_next), riders=riders)


def loss_head(h, g, target, name):
    T, D = h.shape
    tm = _tile(T, 256)

    def body(h_ref, g_ref, t_ref, loss_ref, dh_ref, dhb_ref, dg_ref):
        @pl.when(pl.program_id(0) == 0)
        def _():
            loss_ref[...] = jnp.zeros_like(loss_ref)
            dg_ref[...] = jnp.zeros_like(dg_ref)
        x = h_ref[...]
        err = _rms(x, g_ref[...]) - t_ref[...]
        loss_ref[...] += jnp.full((1, QB), 0.5 * jnp.sum(jnp.mean(err * err, axis=-1)), F32)
        dx, dg = _rms_bwd(err / D, x, g_ref[...])
        dh_ref[...] = dx
        dhb_ref[...] = dx.astype(BF16)
        dg_ref[...] += dg

    row = pl.BlockSpec((tm, D), lambda i: (i, 0))
    vec = pl.BlockSpec((1, D), lambda i: (0, 0))
    return pl.pallas_call(
        body, name=name, grid=(T // tm,), out_shape=(S((1, QB), F32), S((T, D), F32), S((T, D), BF16), S((1, D), F32)),
        in_specs=[row, vec, row], out_specs=(pl.BlockSpec((1, QB), lambda i: (0, 0)), row, row, vec),
        compiler_params=_params(("arbitrary",)),
    )(h, g, target)


def ffn_down_bwd(dhb, wd, gate, up, a, n, name, riders=()):
    T, D = dhb.shape
    F = wd.shape[0]
    tr, tn = _tile(T, 512), _tile(F, 256)

    def body(d_ref, n_ref, w_ref, g_ref, u_ref, a_ref, o_ref, dwd_ref, dwdb_ref, dwgu_ref, dwgub_ref):
        w = w_ref[...]
        for r in range(T // tr):
            rows = slice(r * tr, (r + 1) * tr)
            da = 0.5 * _nt(d_ref[rows, :], w)
            o_ref[0, rows, :] = (da * g_ref[rows, :].astype(F32)).astype(BF16)
            o_ref[1, rows, :] = (da * u_ref[rows, :].astype(F32)).astype(BF16)
        dwd = 0.5 * _tn(a_ref[...], d_ref[...])
        dwd_ref[...] = dwd
        dwdb_ref[...] = dwd.astype(BF16)
        for s in range(2):
            dwgu = _tn(o_ref[s], n_ref[...])
            dwgu_ref[s] = dwgu
            dwgub_ref[s] = dwgu.astype(BF16)

    tile = pl.BlockSpec((T, tn), lambda j: (0, j))
    whole = pl.BlockSpec((T, D), lambda j: (0, 0))
    rows1, rows2 = pl.BlockSpec((tn, D), lambda j: (j, 0)), pl.BlockSpec((2, tn, D), lambda j: (0, j, 0))
    return _call(
        body, name=name, grid=(F // tn,),
        out_shape=(S((2, T, F), BF16), S((F, D), F32), S((F, D), BF16), S((2, F, D), F32), S((2, F, D), BF16)),
        in_specs=[whole, whole, rows1, tile, tile, tile],
        out_specs=(pl.BlockSpec((2, T, tn), lambda j: (0, 0, j)), rows1, rows1, rows2, rows2),
        sem=("parallel",), args=(dhb, n, wd, gate, up, a), riders=riders)


def tn_matmul(xs, y, alpha, name, riders=()):
    B, T, N = xs.shape
    D = y.shape[1]
    tn = _tile(N, 256)

    def body(x_ref, y_ref, o_ref, ob_ref):
        o = alpha * _tn(x_ref[...], y_ref[...])
        o_ref[...] = o
        ob_ref[...] = o.astype(BF16)

    tile = pl.BlockSpec((None, tn, D), lambda s, j: (s, j, 0))
    return _call(
        body, name=name, grid=(B, N // tn), out_shape=(S((B, N, D), F32), S((B, N, D), BF16)),
        in_specs=[pl.BlockSpec((None, T, tn), lambda s, j: (s, 0, j)), pl.BlockSpec((T, D), lambda s, j: (0, 0))],
        out_specs=(tile, tile), sem=("parallel", "parallel"), args=(xs, y), riders=riders)


def nn_rms_bwd(xs, ws, h_in, g, dh, name, riders=()):
    B, T, K = xs.shape
    D = ws.shape[2]
    tm = _tile(T, 256)

    def body(x_ref, w_ref, h_ref, g_ref, d_ref, o_ref, ob_ref, dg_ref):
        @pl.when(pl.program_id(0) == 0)
        def _():
            dg_ref[...] = jnp.zeros_like(dg_ref)
        dn = _nn(x_ref[0], w_ref[0])
        for s in range(1, B):
            dn = dn + _nn(x_ref[s], w_ref[s])
        dx, dg = _rms_bwd(dn, h_ref[...], g_ref[...])
        out = d_ref[...] + dx
        o_ref[...] = out
        ob_ref[...] = out.astype(BF16)
        dg_ref[...] += dg

    row = pl.BlockSpec((tm, D), lambda i: (i, 0))
    vec = pl.BlockSpec((1, D), lambda i: (0, 0))
    g, g_spec = _gain(g)
    return _call(
        body, name=name, grid=(T // tm,), out_shape=(S((T, D), F32), S((T, D), BF16), S((1, D), F32)),
        in_specs=[pl.BlockSpec((B, tm, K), lambda i: (0, i, 0)), pl.BlockSpec((B, K, D), lambda i: (0, 0, 0)), row, g_spec, row],
        out_specs=(row, row, vec),
        sem=("arbitrary",), args=(xs, ws, h_in, g, dh), riders=riders)


def mix_out_bwd(dhb, wout, mixed, o_sb, o_sw, g_sb, g_sw, name):
    T, D = dhb.shape
    M = SB_W + SWA_W
    tm = _tile(T, 256)
    steps = T // tm

    def body(d_ref, w_ref, mx_ref, a_ref, b_ref, ga_ref, gb_ref, da_ref, db_ref, dga_ref, dgb_ref, dw_ref, dwb_ref):
        i = pl.program_id(0)

        @pl.when(i == 0)
        def _():
            dga_ref[...] = jnp.zeros_like(dga_ref)
            dgb_ref[...] = jnp.zeros_like(dgb_ref)
            dw_ref[...] = jnp.zeros_like(dw_ref)
        dm = _nt(d_ref[...], w_ref[...])
        dxa, dga = _rms_bwd(dm[:, :SB_W], a_ref[...], ga_ref[...])
        dxb, dgb = _rms_bwd(dm[:, SB_W:], b_ref[...], gb_ref[...])
        da_ref[...] = dxa
        db_ref[...] = dxb
        dga_ref[...] += dga
        dgb_ref[...] += dgb
        dw_ref[...] += _tn(mx_ref[...], d_ref[...])

        @pl.when(i == steps - 1)
        def _():
            dwb_ref[...] = dw_ref[...].astype(BF16)

    row = lambda n: pl.BlockSpec((tm, n), lambda i: (i, 0))
    vec = lambda n: pl.BlockSpec((1, n), lambda i: (0, 0))
    whole = pl.BlockSpec((M, D), lambda i: (0, 0))
    (g_sb, sb_spec), (g_sw, sw_spec) = _gain(g_sb), _gain(g_sw)
    return pl.pallas_call(
        body, name=name, grid=(steps,),
        out_shape=(S((T, SB_W), F32), S((T, SWA_W), F32), S((1, SB_W), F32), S((1, SWA_W), F32), S((M, D), F32), S((M, D), BF16)),
        in_specs=[row(D), whole, row(M), row(SB_W), row(SWA_W), sb_spec, sw_spec],
        out_specs=(row(SB_W), row(SWA_W), vec(SB_W), vec(SWA_W), whole, whole),
        compiler_params=_params(("arbitrary",)),
    )(dhb, wout, mixed, o_sb, o_sw, g_sb, g_sw)


def rel_bias_grad(dscs, bprev, bcur, name):
    n = len(dscs)

    def body(*refs):
        bp_ref, bc_ref, o_ref = refs[n], refs[n + 1], refs[n + 2]
        bks = [bp_ref[...], bc_ref[...]]
        row = lax.broadcasted_iota(jnp.int32, (N_BUCKETS, QB), 0)
        lane = lax.broadcasted_iota(jnp.int32, (N_BUCKETS, QB), 1)
        out = jnp.zeros((N_BUCKETS, QB), F32)
        for h in range(8):
            tot = [sum(refs[l][h, b] for l in range(n)) for b in range(2)]
            for b in range(N_BUCKETS):
                val = jnp.sum(jnp.where(bks[0] == b, tot[0], 0.0)) + jnp.sum(jnp.where(bks[1] == b, tot[1], 0.0))
                out = jnp.where((row == b) & (lane == h), val, out)
        o_ref[...] = out

    return pl.pallas_call(body, name=name, out_shape=S((N_BUCKETS, QB), F32), compiler_params=_params())(*dscs, bprev, bcur)


def _adamw(w, g, m, v):
    m = ADAM_B1 * m + (1.0 - ADAM_B1) * g
    v = ADAM_B2 * v + (1.0 - ADAM_B2) * (g * g)
    m_hat = m / (1.0 - ADAM_B1 ** ADAM_STEP)
    v_hat = v / (1.0 - ADAM_B2 ** ADAM_STEP)
    delta = -ADAM_LR * (m_hat / (jnp.sqrt(v_hat) + ADAM_EPS) + ADAM_WD * w)
    return delta, m, v


def adamw_scattered(w, m, v, owns, others, name, riders=()):
    L, R, C = w.shape
    tr = _rows_tile(R, 176)

    def body(w_ref, m_ref, v_ref, *rest):
        own_refs, other_refs = rest[:L], rest[L:2 * L]
        g_ref, d_ref, mo_ref, vo_ref = rest[2 * L:]
        layer = pl.program_id(0)

        def grad(k):
            o = other_refs[k]
            return own_refs[k][...] + o[0].astype(F32) + o[1].astype(F32) + o[2].astype(F32)

        g = grad(0)
        for k in range(1, L):
            g = jnp.where(layer == k, grad(k), g)
        d, mn, vn = _adamw(w_ref[...], g, m_ref[...], v_ref[...])
        g_ref[...] = g
        d_ref[...] = d
        mo_ref[...] = mn
        vo_ref[...] = vn

    tile = pl.BlockSpec((None, tr, C), lambda l, i: (l, i, 0))
    return _call(
        body, name=name, grid=(L, R // tr), out_shape=(S((L, R, C), F32),) * 4,
        in_specs=[tile] * 3 + [pl.BlockSpec((tr, C), lambda l, i: (i, 0))] * L + [pl.BlockSpec((3, tr, C), lambda l, i: (0, i, 0))] * L,
        out_specs=(tile,) * 4, sem=("parallel", "parallel"), args=(w, m, v, *owns, *others), riders=riders)


def adamw_small(w, gs, m, v, name):
    R, C = w.shape

    def body(w_ref, g_ref, m_ref, v_ref, go_ref, d_ref, mo_ref, vo_ref):
        g = g_ref[0]
        for k in range(1, N_DEV):
            g = g + g_ref[k]
        d, mn, vn = _adamw(w_ref[...], g, m_ref[...], v_ref[...])
        go_ref[...] = g
        d_ref[...] = d
        mo_ref[...] = mn
        vo_ref[...] = vn

    return pl.pallas_call(body, name=name, out_shape=(S((R, C), F32),) * 4, compiler_params=_params())(w, gs, m, v)


def kernel(x, norm_ffn1, w_ffn1_gu, w_ffn1_down, norm_mix, w_in, sinks, norm_out_sb, norm_out_swa, w_out, norm_ffn2, w_ffn2_gu, w_ffn2_down, rel_bias, norm_final, loss_target, m_norm_ffn1, m_w_ffn1_gu, m_w_ffn1_down, m_norm_mix, m_w_in, m_sinks, m_norm_out_sb, m_norm_out_swa, m_w_out, m_norm_ffn2, m_w_ffn2_gu, m_w_ffn2_down, m_rel_bias, m_norm_final, v_norm_ffn1, v_w_ffn1_gu, v_w_ffn1_down, v_norm_mix, v_w_in, v_sinks, v_norm_out_sb, v_norm_out_swa, v_w_out, v_norm_ffn2, v_w_ffn2_gu, v_w_ffn2_down, v_rel_bias, v_norm_final):
    L = norm_ffn1.shape[0]
    T, D = x.shape[1], x.shape[2]
    F = w_ffn1_down.shape[1] * N_DEV
    h = x.reshape(T, D)
    target = loss_target.reshape(T, D)
    after, upto, before = _tri_consts()
    bprev, bcur = _t5_buckets()

    local = {}
    for l in range(L):
        local[f"gu1_{l}"] = w_ffn1_gu[l].T.astype(BF16)
        local[f"d1_{l}"] = w_ffn1_down[l].astype(BF16)
        local[f"in_{l}"] = w_in[l].T.astype(BF16)
        local[f"out_{l}"] = w_out[l].astype(BF16)
        local[f"gu2_{l}"] = w_ffn2_gu[l].T.astype(BF16)
        local[f"d2_{l}"] = w_ffn2_down[l].astype(BF16)
    full, partial = {}, {}
    grads, chip_sum, recv_b = {}, {}, {}

    def run(fn, *args, ag=(), rs1=(), rs2=()):
        halves = lambda names: [n if isinstance(n, tuple) else (n, None) for n in names]
        ag, rs2 = [(n, k) for n, k in halves(ag) if n in local], halves(rs2)
        rows = lambda k, total: None if k is None else (k * (total // 2), total // 2)

        def second(n, k):
            sb = chip_sum[n][1]
            return scatter_second(sb, rows(k, sb.shape[1]), recv_b.get(n))

        riders = ([gather(local[n], rows(k, local[n].shape[0]), partial.get(n)) for n, k in ag]
                  + [scatter_first(grads[n][1]) for n in rs1] + [second(n, k) for n, k in rs2])
        if not riders:
            return fn(*args)
        outs, per = fn(*args, riders=riders)
        per = [p[0] for p in per]
        for n, k in ag:
            buf = per.pop(0)
            if k == 0:
                partial[n] = buf
            else:
                full[n] = buf.reshape(N_DEV * buf.shape[1], D)
        if rs1:
            sums = scatter_add([grads[n][0] for n in rs1], [per.pop(0) for n in rs1], "rs_add_" + "_".join(rs1))
            chip_sum.update(zip(rs1, sums))
        for n, _ in rs2:
            recv_b[n] = per.pop(0)
        return outs

    def attn_fwd(p, sink, name, riders=()):
        return side_by_side(sb_attn_fwd(p, after, name, riders=PARTS), swa_fwd(p, sink, rel_bias, bprev, bcur, name, riders=PARTS),
                            name, riders)

    def attn_bwd(p, do_sb, tot, do_sw, lse, sink, name, riders=()):
        return side_by_side(sb_attn_bwd(p, do_sb, tot, upto, before, name, riders=PARTS),
                            swa_bwd(p, do_sw, lse, sink, rel_bias, bprev, bcur, name, riders=PARTS), name, riders)

    gu = lambda n: full[n].reshape(2, F, D)
    slots = lambda pair: tuple(t.reshape(N_DEV, -1, D) for t in pair)
    vec = lambda a: a.reshape(1, -1)

    PW = max(D, SB_W + SWA_W)
    n_rows = 4 * L + 2
    n_rows += (-n_rows) % 8

    def pack(ffn1, mix, ffn2, final, osb, osw, snk, rel, extra):
        pieces = []

        def row(*parts):
            flat = [a.reshape(-1) for a in parts]
            pieces.extend(flat)
            used = sum(a.size for a in flat)
            if used < PW:
                pieces.append(jnp.zeros((PW - used,), F32))

        for group in (ffn1, mix, ffn2):
            for l in range(L):
                row(group[l])
        row(final)
        for l in range(L):
            row(osb[l], osw[l])
        row(*[snk[l].reshape(-1)[:8] for l in range(L)], rel, extra)
        pieces.append(jnp.zeros(((n_rows - 4 * L - 2) * PW,), F32))
        return jnp.concatenate(pieces).reshape(n_rows, PW)

    def unpack(arr):
        ffn1, mix, ffn2 = arr[0:L, :D], arr[L:2 * L, :D], arr[2 * L:3 * L, :D]
        final = arr[3 * L, :D]
        ob = arr[3 * L + 1:4 * L + 1]
        tail = arr[4 * L + 1]
        return (ffn1, mix, tail[:8 * L].reshape(L, 8), ob[:, :SB_W], ob[:, SB_W:SB_W + SWA_W], ffn2,
                tail[8 * L:8 * L + N_BUCKETS * 8].reshape(N_BUCKETS, 8), final)

    zero = jnp.zeros((1,), F32)
    w_small = pack(norm_ffn1, norm_mix, norm_ffn2, norm_final, norm_out_sb, norm_out_swa, sinks, rel_bias, zero)
    g_ffn1, g_mix, g_ffn2, g_osb, g_osw = [a.reshape(L, 1, -1) for a in (norm_ffn1, norm_mix, norm_ffn2, norm_out_sb, norm_out_swa)]

    saved = []
    n_next = run(rms_cast, h, (g_ffn1, 0), "rms_first", ag=("gu1_0",))
    for l in range(L):
        nx = l + 1
        s = {"h0": h, "n1": n_next}
        s["gate1"], s["up1"], s["a1"] = run(ffn_up_fwd, s["n1"], gu(f"gu1_{l}"), f"ffn1_up{l}",
                                            ag=(f"d1_{l}", ("in_0", 0) if l == 0 else (f"in_{l}", 1)))
        h = run(ffn_down_fwd, s["a1"], full[f"d1_{l}"], h, None, f"ffn1_down{l}", ag=(("in_0", 1),) if l == 0 else ())
        s["h1"] = h
        s["n2"], s["p"] = mix_in_fwd(h, (g_mix, l), full[f"in_{l}"], f"mix_in{l}")
        s["o_sb"], s["tot"], s["o_sw"], s["lse"] = run(attn_fwd, s["p"], (sinks, l), f"attn_fwd{l}",
                                                       ag=(f"out_{l}", f"gu2_{l}", f"d2_{l}", (f"gu1_{nx}", 0)))
        s["mixed"], h, s["n3"] = run(mix_out_fwd, s["o_sb"], s["o_sw"], (g_osb, l), (g_osw, l),
                                     full[f"out_{l}"], h, (g_ffn2, l), f"mix_out{l}")
        s["h2"] = h
        s["gate2"], s["up2"], s["a2"] = run(ffn_up_fwd, s["n3"], gu(f"gu2_{l}"), f"ffn2_up{l}",
                                            ag=((f"gu1_{nx}", 1), (f"in_{nx}", 0)))
        if nx < L:
            h, n_next = run(ffn_down_fwd, s["a2"], full[f"d2_{l}"], h, (g_ffn1, nx), f"ffn2_down{l}")
        else:
            h = run(ffn_down_fwd, s["a2"], full[f"d2_{l}"], h, None, f"ffn2_down{l}")
        saved.append(s)

    loss_part, dh, dhb, dg_final = loss_head(h, vec(norm_final), target, "loss_head")

    small = {k: [None] * L for k in ("ffn1", "mix", "sinks", "osb", "osw", "ffn2", "dsc")}
    for l in reversed(range(L)):
        s = saved[l]

        def ffn_bwd(dh, dhb, tag, gate, up, a, n, h_in, g, r_down, r_up):
            gu_n, d_n = f"gu{tag}_{l}", f"d{tag}_{l}"
            dgu, dwd, dwdb, dwgu, dwgub = run(ffn_down_bwd, dhb, full[d_n], gate, up, a, n, f"ffn{tag}_down_bwd{l}", **r_down)
            grads[gu_n], grads[d_n] = slots((dwgu, dwgub)), slots((dwd, dwdb))
            return run(nn_rms_bwd, dgu, gu(gu_n), h_in, g, dh, f"ffn{tag}_up_bwd{l}", **r_up)

        later = l + 1 < L
        dh, dhb, small["ffn2"][l] = ffn_bwd(dh, dhb, 2, s["gate2"], s["up2"], s["a2"], s["n3"], s["h2"], (g_ffn2, l),
                                            dict(rs2=((f"gu1_{l + 1}", 0), f"d1_{l + 1}") if later else ()),
                                            dict(rs1=(f"gu2_{l}", f"d2_{l}"), rs2=((f"gu1_{l + 1}", 1),) if later else ()))
        do_sb, do_sw, small["osb"][l], small["osw"][l], dw_out, dw_out_b = mix_out_bwd(
            dhb, full[f"out_{l}"], s["mixed"], s["o_sb"], s["o_sw"], (g_osb, l), (g_osw, l), f"mix_out_bwd{l}")
        grads[f"out_{l}"] = slots((dw_out, dw_out_b))
        dq_sb, dk_sb, dv_sb, dq_sw, dk_sw, dv_sw, small["sinks"][l], small["dsc"][l] = run(
            attn_bwd, s["p"], do_sb, s["tot"], do_sw, s["lse"], (sinks, l), f"attn_bwd{l}",
            rs2=(f"gu2_{l}", f"d2_{l}"), rs1=(f"out_{l}",))
        dp = jnp.concatenate([dq_sb, dk_sb, dv_sb, dq_sw, dk_sw, dv_sw], axis=1)
        dh, dhb, small["mix"][l] = nn_rms_bwd(dp[None], full[f"in_{l}"][None], s["h1"], (g_mix, l), dh, f"mix_in_bwd{l}")
        grads[f"in_{l}"] = slots(tn_matmul(dp[None], s["n2"], 1.0, f"dwin{l}"))
        dh, dhb, small["ffn1"][l] = ffn_bwd(dh, dhb, 1, s["gate1"], s["up1"], s["a1"], s["n1"], s["h0"], (g_ffn1, l),
                                            dict(rs1=(f"in_{l}",), rs2=(f"out_{l}",)),
                                            dict(rs1=(f"gu1_{l}", f"d1_{l}"), rs2=(f"in_{l}",)))

    grad_x = dh.reshape(x.shape)

    upd = {}
    for nm, w, m, v, transposed, last in (
            ("gu2", w_ffn2_gu, m_w_ffn2_gu, v_w_ffn2_gu, True, (("gu1_0", 0),)), ("d2", w_ffn2_down, m_w_ffn2_down, v_w_ffn2_down, False, (("gu1_0", 1),)),
            ("in", w_in, m_w_in, v_w_in, True, ("d1_0",)), ("out", w_out, m_w_out, v_w_out, False, ()),
            ("gu1", w_ffn1_gu, m_w_ffn1_gu, v_w_ffn1_gu, True, ()), ("d1", w_ffn1_down, m_w_ffn1_down, v_w_ffn1_down, False, ())):
        turn = (lambda a: jnp.swapaxes(a, 1, 2)) if transposed else (lambda a: a)
        names = [f"{nm}_{l}" for l in range(L)]
        res = run(adamw_scattered, turn(w), turn(m), turn(v), [chip_sum[n][0] for n in names], [recv_b[n] for n in names],
                  f"adamw_{nm}", rs2=last)
        upd[nm] = tuple(turn(r) for r in res)

    d_rel = rel_bias_grad(small["dsc"], bprev, bcur, "rel_bias_grad")[:, :8]
    g_small = pack(small["ffn1"], small["mix"], small["ffn2"], dg_final, small["osb"], small["osw"], small["sinks"], d_rel,
                   loss_part[0, :1])
    m_small = pack(m_norm_ffn1, m_norm_mix, m_norm_ffn2, m_norm_final, m_norm_out_sb, m_norm_out_swa, m_sinks, m_rel_bias, zero)
    v_small = pack(v_norm_ffn1, v_norm_mix, v_norm_ffn2, v_norm_final, v_norm_out_sb, v_norm_out_swa, v_sinks, v_rel_bias, zero)
    gs_small = all_gather_rows(g_small, "ag_small")
    summed = adamw_small(w_small, gs_small, m_small, v_small, "adamw_small")
    small_out = [unpack(a) for a in summed]
    loss = summed[0][4 * L + 1, 8 * L + N_BUCKETS * 8]

    def group(k):
        sm = small_out[k]
        return (sm[0], upd["gu1"][k], upd["d1"][k], sm[1], upd["in"][k], sm[2], sm[3], sm[4], upd["out"][k], sm[5],
                upd["gu2"][k], upd["d2"][k], sm[6], sm[7])

    return (loss, grad_x, *group(0), *group(1), *group(2), *group(3))
```

```python
import math

import jax
import jax.numpy as jnp
from jax import lax
from jax.experimental import pallas as pl
from jax.experimental.pallas import tpu as pltpu

F32 = jnp.float32
BF16 = jnp.bfloat16
S = jax.ShapeDtypeStruct

N_DEV = 8
HEAD_DIM = 64
SB_HEADS = 8
PAIR = 2 * HEAD_DIM
SB_W = 512
SWA_W = 512
KV_W = 128
IN_W = 3 * SB_W + SWA_W + 2 * KV_W
QB = 128
N_BUCKETS = 32
MAX_DISTANCE = 128
EPS = 1e-6
NEG_INF = -1e30
SCALE = HEAD_DIM ** -0.5

ADAM_LR = 0.001
ADAM_B1 = 0.9
ADAM_B2 = 0.999
ADAM_EPS = 1e-08
ADAM_WD = 0.01
ADAM_STEP = 10

VMEM_LIMIT = 56 * 1024 * 1024
MESH = pl.DeviceIdType.MESH


def _params(sem=None, vmem=VMEM_LIMIT):
    return pltpu.CompilerParams(dimension_semantics=sem, vmem_limit_bytes=vmem)


def _nn(a, b):
    return jnp.dot(a, b, preferred_element_type=F32)


def _nt(a, b):
    return lax.dot_general(a, b, (((1,), (1,)), ((), ())), preferred_element_type=F32)


def _tn(a, b):
    return lax.dot_general(a, b, (((0,), (0,)), ((), ())), preferred_element_type=F32)


def _tri(xs, m):
    return [_nn(x.astype(BF16), m) for x in xs]


def _rms(x, g):
    r = lax.rsqrt(jnp.mean(x * x, axis=-1, keepdims=True) + EPS)
    return x * r * g


def _rms_bwd(dy, x, g):
    r = lax.rsqrt(jnp.mean(x * x, axis=-1, keepdims=True) + EPS)
    xhat = x * r
    u = dy * g
    dx = r * (u - xhat * jnp.mean(u * xhat, axis=-1, keepdims=True))
    return dx, jnp.sum(dy * xhat, axis=0, keepdims=True)


def _softplus_logsig(z):
    sp = jnp.maximum(z, 0.0) + jnp.log(1.0 + jnp.exp(-jnp.abs(z)))
    return sp, z - sp


def _gain(g):
    if isinstance(g, tuple):
        rows, n = g
        return rows, pl.BlockSpec((None, 1, rows.shape[2]), lambda *_: (n, 0, 0))
    return g, pl.BlockSpec((1, g.shape[1]), lambda *_: (0, 0))


def _tile(n, want):
    t = min(n, want)
    while n % t:
        t //= 2
    return t


def _place():
    x, y, c = lax.axis_index("x"), lax.axis_index("y"), lax.axis_index("c")
    chips = [(1 - x, y), (x, 1 - y), (1 - x, 1 - y)]
    return x, y, c, chips


def all_gather_rows(v, name):
    R, C = v.shape

    def body(v_ref, out_ref, send_sems, recv_sems, local_sem):
        x, y, c, chips = _place()
        me, sibling = (x, y, c), (x, y, 1 - c)

        def slot(px, py, pc):
            return out_ref.at[4 * px + 2 * py + pc]

        def copy(k, block, to, src=None):
            return pltpu.make_async_remote_copy(
                src_ref=slot(*block) if src is None else src, dst_ref=slot(*block),
                send_sem=send_sems.at[k], recv_sem=recv_sems.at[k], device_id=to, device_id_type=MESH)

        mine = pltpu.make_async_copy(v_ref, slot(*me), local_sem)
        mine.start()
        first = [copy(0, me, sibling, src=v_ref)]
        first += [copy(1 + j, me, (*chip, c), src=v_ref) for j, chip in enumerate(chips)]
        for cp in first:
            cp.start()
        passed = [copy(4 + j, (*chip, c), sibling) for j, chip in enumerate(chips)]
        for j, chip in enumerate(chips):
            copy(1 + j, (*chip, c), me).wait_recv()
            passed[j].start()
        copy(0, sibling, me).wait_recv()
        for j, chip in enumerate(chips):
            copy(4 + j, (*chip, 1 - c), me).wait_recv()
        for cp in first + passed:
            cp.wait_send()
        mine.wait()

    return pl.pallas_call(
        body, name=name, out_shape=S((N_DEV, R, C), v.dtype),
        in_specs=[pl.BlockSpec(memory_space=pl.ANY)], out_specs=pl.BlockSpec(memory_space=pl.ANY),
        scratch_shapes=[pltpu.SemaphoreType.DMA((7,)), pltpu.SemaphoreType.DMA((7,)), pltpu.SemaphoreType.DMA],
    )(v)


class _Exchange:
    def __init__(self, ins, outs, sizes, n_local, plan, aliases=None):
        self.ins, self.outs, self.plan, self.aliases = list(ins), list(outs), plan, aliases or {}
        self.sizes, self.n_local = list(sizes), n_local

    def scratch(self):
        n = sum(self.sizes)
        return [pltpu.SemaphoreType.DMA((n,)), pltpu.SemaphoreType.DMA((n,)), pltpu.SemaphoreType.DMA((max(self.n_local, 1),))]

    def _copies(self, in_refs, out_refs, sems):
        send_sems, recv_sems, local_sems = sems
        phases, local = self.plan(in_refs, out_refs)
        out, k = [], 0
        for phase in phases:
            out.append([pltpu.make_async_remote_copy(src_ref=s, dst_ref=d, send_sem=send_sems.at[k + n], recv_sem=recv_sems.at[k + n],
                                                     device_id=dev, device_id_type=MESH) for n, (s, d, dev) in enumerate(phase)])
            k += len(phase)
        return out, [pltpu.make_async_copy(s, d, local_sems.at[n]) for n, (s, d) in enumerate(local)]

    def start(self, in_refs, out_refs, sems):
        phases, loc = self._copies(in_refs, out_refs, sems)
        for cp in phases[0] + loc:
            cp.start()

    def advance(self, hook, in_refs, out_refs, sems):
        p = hook - (3 - len(self.sizes))
        if p >= 1:
            phases, _ = self._copies(in_refs, out_refs, sems)
            for cp in phases[p - 1]:
                cp.wait_recv()
            for cp in phases[p]:
                cp.start()

    def finish(self, in_refs, out_refs, sems):
        phases, loc = self._copies(in_refs, out_refs, sems)
        for cp in phases[-1]:
            cp.wait_recv()
        for phase in phases:
            for cp in phase:
                cp.wait_send()
        for cp in loc:
            cp.wait()


def gather(v, rows=None, into=None):
    R, C = v.shape
    r0, nr = rows or (0, R)
    na = min(nr, ((nr // 2 + 15) // 16) * 16)

    def plan(ins, outs):
        x, y, c, _ = _place()
        xn, yn, dg, sibling = (1 - x, y), (x, 1 - y), (1 - x, 1 - y), (x, y, 1 - c)
        slot = lambda chip, start=r0, count=nr: outs[0].at[4 * chip[0] + 2 * chip[1] + c, pl.ds(start, count), :]
        src, mine = ins[0].at[pl.ds(r0, nr), :], slot((x, y))
        same = lambda ref, to: (ref, ref, to)
        first = [(src, mine, sibling), (src, mine, (*xn, c)), (src, mine, (*yn, c))]
        relay = [same(slot(xn, r0, na), (*yn, c)), same(slot(yn, r0 + na, nr - na), (*xn, c))]
        onward = [same(slot(xn), sibling), same(slot(yn), sibling), same(slot(dg), sibling)]
        return [first, relay, onward], [(src, mine)]

    if into is None:
        return _Exchange([v], [S((N_DEV, R, C), v.dtype)], (3, 2, 3), 1, plan)
    return _Exchange([v, into], [S((N_DEV, R, C), v.dtype)], (3, 2, 3), 1, plan, aliases={1: 0})


def scatter_first(gb):
    _, R, C = gb.shape

    def plan(ins, outs):
        x, y, c, chips = _place()
        owners = [(x, y)] + chips
        return [[(ins[0].at[4 * px + 2 * py + (1 - c)], outs[0].at[j], (x, y, 1 - c)) for j, (px, py) in enumerate(owners)]], []

    return _Exchange([gb], [S((4, R, C), BF16)], (4,), 0, plan)


def scatter_second(sb, rows=None, into=None):
    r0, nr = rows or (0, sb.shape[1])

    def plan(ins, outs):
        x, y, c, chips = _place()
        part = lambda ref, j: ref.at[j, pl.ds(r0, nr), :]
        return [[(part(ins[0], j), part(outs[0], j), (*chips[j], c)) for j in range(3)]], []

    if into is None:
        return _Exchange([sb], [S(sb.shape, BF16)], (3,), 0, plan)
    return _Exchange([sb, into], [S(sb.shape, BF16)], (3,), 0, plan, aliases={1: 0})


PARTS = "parts"


def _call(body, *, name, grid, in_specs, out_specs, out_shape, args, scratch=(), sem=None, riders=(), marks=None):
    single = not isinstance(out_shape, (tuple, list))
    out_shape = (out_shape,) if single else tuple(out_shape)
    out_specs = (out_specs,) if single else tuple(out_specs)
    n_in, n_out, n_sc = len(in_specs), len(out_shape), len(scratch)
    if riders is PARTS:
        return dict(body=body, grid=grid, in_specs=list(in_specs), out_specs=out_specs, out_shape=out_shape, args=tuple(args),
                    scratch=list(scratch), marks=marks)
    if not riders:
        res = pl.pallas_call(body, name=name, grid=grid, in_specs=list(in_specs), out_specs=out_specs, out_shape=out_shape,
                             scratch_shapes=list(scratch), compiler_params=_params(sem))(*args)
        return res[0] if single else res
    r_ins = [a for r in riders for a in r.ins]
    r_outs = [o for r in riders for o in r.outs]
    r_scr = [s for r in riders for s in r.scratch()]
    aliases, i0, o0 = {}, n_in, n_out
    for r in riders:
        for a, b in r.aliases.items():
            aliases[i0 + a] = o0 + b
        i0, o0 = i0 + len(r.ins), o0 + len(r.outs)
    steps = math.prod(grid)

    def full(*refs):
        ins, rin = refs[:n_in], refs[n_in:n_in + len(r_ins)]
        pos = n_in + len(r_ins)
        outs, rout = refs[pos:pos + n_out], refs[pos + n_out:pos + n_out + len(r_outs)]
        pos += n_out + len(r_outs)
        sc, rsc = refs[pos:pos + n_sc], refs[pos + n_sc:]
        step = 0
        for d, n in enumerate(grid):
            step = step * n + pl.program_id(d)

        def each(method, *lead):
            i, o = 0, 0
            for k, r in enumerate(riders):
                getattr(r, method)(*lead, rin[i:i + len(r.ins)], rout[o:o + len(r.outs)], rsc[3 * k:3 * k + 3])
                i, o = i + len(r.ins), o + len(r.outs)

        @pl.when(step == 0)
        def _():
            each("start")
        body(*ins, *outs, *sc)

        late = max(steps - 1 - max(steps // 8, 1), 0)
        first, second = marks or (min((3 * steps) // 5, late), late)

        @pl.when(step == first)
        def _():
            each("advance", 1)

        @pl.when(step == second)
        def _():
            each("advance", 2)

        @pl.when(step == steps - 1)
        def _():
            each("finish")

    anywhere = pl.BlockSpec(memory_space=pl.ANY)
    res = pl.pallas_call(
        full, name=name, grid=grid, in_specs=list(in_specs) + [anywhere] * len(r_ins),
        out_specs=out_specs + (anywhere,) * len(r_outs), out_shape=out_shape + tuple(r_outs),
        scratch_shapes=list(scratch) + r_scr, input_output_aliases=aliases,
        compiler_params=_params(("arbitrary",) * len(grid)))(*args, *r_ins)
    host, rest, per = res[:n_out], list(res[n_out:]), []
    for r in riders:
        per.append(rest[:len(r.outs)])
        rest = rest[len(r.outs):]
    return (host[0] if single else tuple(host)), per


def side_by_side(first, second, name, riders=()):
    a_in, a_out, a_sc = len(first["in_specs"]), len(first["out_shape"]), len(first["scratch"])
    n_in, n_out = a_in + len(second["in_specs"]), a_out + len(second["out_shape"])

    def body(*refs):
        ins, outs, sc = refs[:n_in], refs[n_in:n_in + n_out], refs[n_in + n_out:]
        first["body"](*ins[:a_in], *outs[:a_out], *sc[:a_sc])
        second["body"](*ins[a_in:], *outs[a_out:], *sc[a_sc:])

    return _call(body, name=name, grid=first["grid"], in_specs=first["in_specs"] + second["in_specs"],
                 out_specs=first["out_specs"] + second["out_specs"], out_shape=first["out_shape"] + second["out_shape"],
                 args=first["args"] + second["args"], scratch=first["scratch"] + second["scratch"],
                 sem=("arbitrary",) * len(first["grid"]), riders=riders, marks=first["marks"])


def _rows_tile(n, cap):
    return max(t for t in range(16, min(n, cap) + 1, 16) if n % t == 0)


def scatter_add(gs, ras, name):
    C = gs[0].shape[2]
    trs = [_rows_tile(g.shape[1], 176) for g in gs]
    nts = [g.shape[1] // tr for g, tr in zip(gs, trs)]
    steps = max(nts)
    x, y, c, chips = _place()
    slots = jnp.stack([4 * px + 2 * py + c for px, py in [(x, y)] + chips]).astype(jnp.int32)

    def body(s_ref, *refs):
        ins, outs = refs[:5 * len(gs)], refs[5 * len(gs):]
        for k in range(len(gs)):
            g0, g1, g2, g3, ra_ref = ins[5 * k:5 * k + 5]
            own_ref, sb_ref = outs[2 * k:2 * k + 2]

            def work(g0=g0, g1=g1, g2=g2, g3=g3, ra_ref=ra_ref, own_ref=own_ref, sb_ref=sb_ref):
                own_ref[...] = g0[...] + ra_ref[0].astype(F32)
                for j, gj in enumerate((g1, g2, g3)):
                    sb_ref[j] = (gj[...] + ra_ref[j + 1].astype(F32)).astype(BF16)

            if nts[k] == steps:
                work()
            else:
                pl.when(pl.program_id(0) < nts[k])(work)

    in_specs, out_specs, out_shape, args = [], [], [], [slots]
    for k, (g, ra, tr) in enumerate(zip(gs, ras, trs)):
        tile = lambda i, k=k: jnp.minimum(i, nts[k] - 1)
        in_specs += [pl.BlockSpec((None, tr, C), lambda i, s, j=j, tile=tile: (s[j], tile(i), 0)) for j in range(4)]
        in_specs.append(pl.BlockSpec((4, tr, C), lambda i, s, tile=tile: (0, tile(i), 0)))
        out_specs += [pl.BlockSpec((tr, C), lambda i, s, tile=tile: (tile(i), 0)),
                      pl.BlockSpec((3, tr, C), lambda i, s, tile=tile: (0, tile(i), 0))]
        out_shape += [S((g.shape[1], C), F32), S((3, g.shape[1], C), BF16)]
        args += [g, g, g, g, ra]
    spec = pltpu.PrefetchScalarGridSpec(num_scalar_prefetch=1, grid=(steps,), in_specs=in_specs, out_specs=tuple(out_specs))
    res = pl.pallas_call(body, name=name, grid_spec=spec, out_shape=tuple(out_shape), compiler_params=_params(("arbitrary",)))(*args)
    return [(res[2 * k], res[2 * k + 1]) for k in range(len(gs))]


def rms_cast(h, g, name, riders=()):
    T, D = h.shape
    tm = _tile(T, 512)

    def body(h_ref, g_ref, n_ref):
        n_ref[...] = _rms(h_ref[...], g_ref[...]).astype(BF16)

    row = pl.BlockSpec((tm, D), lambda i: (i, 0))
    g, g_spec = _gain(g)
    return _call(body, name=name, grid=(T // tm,), out_shape=S((T, D), BF16), in_specs=[row, g_spec],
                 out_specs=row, sem=("parallel",), args=(h, g), riders=riders)


def ffn_up_fwd(n, wgu, name, riders=()):
    T, D = n.shape
    F = wgu.shape[1]
    tr, tn = _tile(T, 512), _tile(F, 256)

    def body(n_ref, wg_ref, wu_ref, dgate_ref, dup_ref, a_ref):
        wg, wu = wg_ref[...], wu_ref[...]
        for r in range(T // tr):
            rows = slice(r * tr, (r + 1) * tr)
            x = n_ref[rows, :]
            gate = _nt(x, wg)
            up = _nt(x, wu)
            s = jax.nn.sigmoid(gate)
            silu = gate * s
            dgate_ref[rows, :] = (up * (s * (1.0 + gate * (1.0 - s)))).astype(BF16)
            dup_ref[rows, :] = silu.astype(BF16)
            a_ref[rows, :] = (silu * up).astype(BF16)

    tile = pl.BlockSpec((T, tn), lambda j: (0, j))
    return _call(
        body, name=name, grid=(F // tn,), out_shape=(S((T, F), BF16),) * 3,
        in_specs=[pl.BlockSpec((T, D), lambda j: (0, 0)),
                  pl.BlockSpec((None, tn, D), lambda j: (0, j, 0)), pl.BlockSpec((None, tn, D), lambda j: (1, j, 0))],
        out_specs=(tile, tile, tile), sem=("parallel",), args=(n, wgu, wgu), riders=riders)


def ffn_down_fwd(a, wd, h, g_next, name, riders=()):
    T, F = a.shape
    D = wd.shape[1]
    tm = _tile(T, 256)

    def body(a_ref, w_ref, h_ref, *rest):
        out = h_ref[...] + 0.5 * _nn(a_ref[...], w_ref[...])
        if g_next is None:
            rest[0][...] = out
        else:
            g_ref, o_ref, n_ref = rest
            o_ref[...] = out
            n_ref[...] = _rms(out, g_ref[...]).astype(BF16)

    row = pl.BlockSpec((tm, D), lambda i: (i, 0))
    more = g_next is not None
    g_arg, g_spec = _gain(g_next) if more else (None, None)
    return _call(
        body, name=name, grid=(T // tm,), out_shape=(S((T, D), F32), S((T, D), BF16)) if more else S((T, D), F32),
        in_specs=[pl.BlockSpec((tm, F), lambda i: (i, 0)), pl.BlockSpec((F, D), lambda i: (0, 0)), row] + ([g_spec] if more else []),
        out_specs=(row, row) if more else row,
        sem=("parallel",), args=(a, wd, h) + ((g_arg,) if more else ()), riders=riders)


def mix_in_fwd(h, g, win, name):
    T, D = h.shape
    N = win.shape[0]
    tm = _tile(T, 256)

    def body(h_ref, g_ref, w_ref, n_ref, p_ref):
        n = _rms(h_ref[...], g_ref[...]).astype(BF16)
        n_ref[...] = n
        p_ref[...] = _nt(n, w_ref[...]).astype(BF16)

    g, g_spec = _gain(g)
    return pl.pallas_call(
        body, name=name, grid=(T // tm,), out_shape=(S((T, D), BF16), S((T, N), BF16)),
        in_specs=[pl.BlockSpec((tm, D), lambda i: (i, 0)), g_spec, pl.BlockSpec((N, D), lambda i: (0, 0))],
        out_specs=(pl.BlockSpec((tm, D), lambda i: (i, 0)), pl.BlockSpec((tm, N), lambda i: (i, 0))),
        compiler_params=_params(("parallel",)),
    )(h, g, win)


def _tri_consts():
    r = lax.broadcasted_iota(jnp.int32, (QB, QB), 0)
    c = lax.broadcasted_iota(jnp.int32, (QB, QB), 1)
    ones = jnp.ones((QB, QB), BF16)
    with_sums = lambda tri: jnp.concatenate([tri.astype(BF16), ones], axis=1)
    return with_sums(r > c), with_sums(r <= c), with_sums(r < c)


def _half_masks():
    lane = lax.broadcasted_iota(jnp.int32, (QB, PAIR), 1)
    row = lax.broadcasted_iota(jnp.int32, (QB, PAIR), 0)
    return lane < HEAD_DIM, lane, row


def sb_attn_fwd(p, after, name, riders=()):
    T = p.shape[0]
    nq = T // QB

    def body(q_ref, k_ref, v_ref, m_ref, o_ref, tot_ref, q_sc, acc_ref, z_sc):
        i = pl.program_id(0)
        lo, lane, row = _half_masks()
        causal = lane < row
        heads, pairs = range(SB_HEADS), range(SB_HEADS // 2)
        for hp in pairs:
            q_sc[hp] = (q_ref[:, hp * PAIR:(hp + 1) * PAIR].astype(F32) * SCALE).astype(BF16)
        m2 = m_ref[...]

        def by_head(ref, j, hp):
            t = ref[pl.ds(pl.multiple_of(j * QB, QB), QB), hp * PAIR:(hp + 1) * PAIR]
            return jnp.concatenate([jnp.where(lo, t, 0), jnp.where(lo, 0, t)], axis=0)

        def scores(j):
            return [_nt(q_sc[hp], by_head(k_ref, j, hp)) for hp in pairs]

        def block(j, diag):
            z2 = [z_sc[hp] for hp in pairs]
            ahead = scores(jnp.maximum(j - 1, 0))
            for hp in pairs:
                z_sc[hp] = ahead[hp]
            vs = [by_head(v_ref, j, hp) for hp in pairs]
            spls = [_softplus_logsig(z2[h // 2][:, (h % 2) * QB:(h % 2 + 1) * QB]) for h in heads]
            sp = [jnp.where(causal, spls[h][0], 0.0) if diag else spls[h][0] for h in heads]
            rr = _tri(sp, m2)
            if diag:
                w = [jnp.where(causal, jnp.exp(spls[h][1] - rr[h][:, :QB]), 0.0).astype(BF16) for h in heads]
            else:
                c = [tot_ref[:, h * QB:(h + 1) * QB] for h in heads]
                w = [jnp.exp(spls[h][1] - (c[h] + rr[h][:, :QB])).astype(BF16) for h in heads]
            pv = [_nn(jnp.concatenate([w[2 * hp], w[2 * hp + 1]], axis=1), vs[hp]) for hp in pairs]
            for hp in pairs:
                acc_ref[hp] = pv[hp] if diag else acc_ref[hp] + pv[hp]
            for h in heads:
                tot_ref[:, h * QB:(h + 1) * QB] = rr[h][:, QB:] if diag else c[h] + rr[h][:, QB:]

        first = scores(i)
        for hp in pairs:
            z_sc[hp] = first[hp]
        block(i, True)

        def step(t, carry):
            block(i - 1 - t, False)
            return carry
        lax.fori_loop(0, i, step, 0)
        for hp in pairs:
            o_ref[:, hp * PAIR:(hp + 1) * PAIR] = acc_ref[hp]

    npair = SB_HEADS // 2
    return _call(
        body, name=name, grid=(nq,), out_shape=(S((T, SB_W), F32), S((T, SB_HEADS * QB), F32)),
        in_specs=[pl.BlockSpec((QB, SB_W), lambda i: (i, 0)), pl.BlockSpec((T, SB_W), lambda i: (0, 1)),
                  pl.BlockSpec((T, SB_W), lambda i: (0, 2)), pl.BlockSpec((QB, 2 * QB), lambda i: (0, 0))],
        out_specs=(pl.BlockSpec((QB, SB_W), lambda i: (i, 0)), pl.BlockSpec((QB, SB_HEADS * QB), lambda i: (i, 0))),
        scratch=[pltpu.VMEM((npair, QB, PAIR), BF16), pltpu.VMEM((npair, QB, PAIR), F32), pltpu.VMEM((npair, QB, 2 * QB), F32)],
        sem=("arbitrary",), args=(p, p, p, after), riders=riders,
        marks=((11 * nq) // 16, (14 * nq) // 16))


def sb_attn_bwd(p, do, tot, upto, before, name, riders=()):
    T = p.shape[0]
    nq = T // QB

    def body(q_ref, k_ref, v_ref, do_ref, tot_ref, mp_ref, mg_ref, dq_ref, dk_ref, dv_ref,
             q_sc, d_sc, qd_sc, pg_sc, dq_acc, dk_acc, dv_acc, zd_sc):
        i = pl.program_id(0)
        lo, lane, row = _half_masks()
        causal = lane < row
        heads, pairs = range(SB_HEADS), range(SB_HEADS // 2)

        def by_head(t):
            return jnp.concatenate([jnp.where(lo, t, 0), jnp.where(lo, 0, t)], axis=0)

        for hp in pairs:
            q2 = (q_ref[:, hp * PAIR:(hp + 1) * PAIR].astype(F32) * SCALE).astype(BF16)
            d2 = do_ref[:, hp * PAIR:(hp + 1) * PAIR].astype(BF16)
            q_sc[hp] = q2
            d_sc[hp] = d2
            qd_sc[hp] = by_head(q2)
            qd_sc[SB_HEADS // 2 + hp] = by_head(d2)
        mp, mg = mp_ref[...], mg_ref[...]

        @pl.when(i == 0)
        def _():
            dk_acc[...] = jnp.zeros_like(dk_acc)
            dv_acc[...] = jnp.zeros_like(dv_acc)
        pg_sc[...] = jnp.zeros_like(pg_sc)
        dq_acc[...] = jnp.zeros_like(dq_acc)

        def rows(ref, j, hp):
            return ref[pl.ds(pl.multiple_of(j * QB, QB), QB), hp * PAIR:(hp + 1) * PAIR]

        def products(j):
            return ([_nt(q_sc[hp], by_head(rows(k_ref, j, hp))) for hp in pairs]
                    + [_nt(d_sc[hp], by_head(rows(v_ref, j, hp))) for hp in pairs])

        def block(j, diag):
            r0 = pl.multiple_of(j * QB, QB)
            half = lambda t, h: t[:, (h % 2) * QB:(h % 2 + 1) * QB]
            z = [half(zd_sc[h // 2], h) for h in heads]
            dw = [half(zd_sc[SB_HEADS // 2 + h // 2], h) for h in heads]
            if not diag:
                ahead = products(j + 1)
                for hp in range(SB_HEADS):
                    zd_sc[hp] = ahead[hp]
            ks = [by_head(rows(k_ref, j, hp)) for hp in pairs]
            spls = [_softplus_logsig(z[h]) for h in heads]
            sp = [jnp.where(causal, spls[h][0], 0.0) if diag else spls[h][0] for h in heads]
            rr = _tri(sp, mp)
            pc = [pg_sc[2 * h] for h in heads]
            w = [jnp.exp(spls[h][1] - (tot_ref[:, h * QB:(h + 1) * QB] - (pc[h] + rr[h][:, :QB]))) for h in heads]
            if diag:
                w = [jnp.where(causal, w[h], 0.0) for h in heads]
            gg = [dw[h] * w[h] for h in heads]
            rg = _tri(gg, mg)
            gc = [pg_sc[2 * h + 1] for h in heads]
            dz = [gg[h] - (gg[h] + gc[h] + rg[h][:, :QB]) * jnp.exp(spls[h][1]) for h in heads]
            if diag:
                dz = [jnp.where(causal, dz[h], 0.0) for h in heads]
            dzb = [dz[h].astype(BF16) for h in heads]
            wb = [w[h].astype(BF16) for h in heads]
            both = lambda t, hp, axis: jnp.concatenate([t[2 * hp], t[2 * hp + 1]], axis=axis)
            dq = [_nn(both(dzb, hp, 1), ks[hp]) for hp in pairs]
            dk = [_tn(both(dzb, hp, 0), qd_sc[hp]) for hp in pairs]
            dv = [_tn(both(wb, hp, 0), qd_sc[SB_HEADS // 2 + hp]) for hp in pairs]
            for h in heads:
                if not diag:
                    pg_sc[2 * h] = pc[h] + rr[h][:, QB:]
                    pg_sc[2 * h + 1] = gc[h] + rg[h][:, QB:]
            for hp in pairs:
                dq_acc[hp] += dq[hp]
                dk_acc[pl.ds(r0, QB), hp * PAIR:(hp + 1) * PAIR] += dk[hp]
                dv_acc[pl.ds(r0, QB), hp * PAIR:(hp + 1) * PAIR] += dv[hp]

        first = products(0)
        for hp in range(SB_HEADS):
            zd_sc[hp] = first[hp]

        def step(t, carry):
            block(t, False)
            return carry
        lax.fori_loop(0, i, step, 0)
        block(i, True)
        for hp in pairs:
            dq_ref[:, hp * PAIR:(hp + 1) * PAIR] = (dq_acc[hp] * SCALE).astype(BF16)

        @pl.when(i == nq - 1)
        def _():
            dk_ref[...] = dk_acc[...].astype(BF16)
            dv_ref[...] = dv_acc[...].astype(BF16)

    qtile = pl.BlockSpec((QB, SB_W), lambda i: (i, 0))
    whole = pl.BlockSpec((T, SB_W), lambda i: (0, 0))
    const = pl.BlockSpec((QB, 2 * QB), lambda i: (0, 0))
    return _call(
        body, name=name, grid=(nq,), out_shape=(S((T, SB_W), BF16),) * 3,
        in_specs=[qtile, pl.BlockSpec((T, SB_W), lambda i: (0, 1)), pl.BlockSpec((T, SB_W), lambda i: (0, 2)), qtile,
                  pl.BlockSpec((QB, SB_HEADS * QB), lambda i: (i, 0)), const, const],
        out_specs=(qtile, whole, whole),
        scratch=[pltpu.VMEM((SB_HEADS // 2, QB, PAIR), BF16), pltpu.VMEM((SB_HEADS // 2, QB, PAIR), BF16),
                 pltpu.VMEM((SB_HEADS, 2 * QB, PAIR), BF16),
                 pltpu.VMEM((2 * SB_HEADS, QB, QB), F32), pltpu.VMEM((SB_HEADS // 2, QB, PAIR), F32),
                 pltpu.VMEM((T, SB_W), F32), pltpu.VMEM((T, SB_W), F32), pltpu.VMEM((SB_HEADS, QB, 2 * QB), F32)],
        sem=("arbitrary",), args=(p, p, p, do, tot, upto, before), riders=riders)


def _t5_buckets():
    a = lax.broadcasted_iota(jnp.int32, (QB, QB), 0)
    c = lax.broadcasted_iota(jnp.int32, (QB, QB), 1)

    def bucket(dist):
        dist = jnp.maximum(dist, 0)
        max_exact = N_BUCKETS // 2
        d = jnp.maximum(dist, 1).astype(F32)
        large = max_exact + (jnp.log(d / max_exact) / math.log(MAX_DISTANCE / max_exact)
                             * (N_BUCKETS - max_exact)).astype(jnp.int32)
        large = jnp.minimum(large, N_BUCKETS - 1)
        return jnp.where(dist < max_exact, dist, large)

    return bucket(QB + a - c), bucket(a - c)


def _swa_common(i, kp_ref, kc_ref, vp_ref, vc_ref, bp_ref, bc_ref, rb_ref, bias_ref):
    lo, lane, row = _half_masks()

    @pl.when(i == 0)
    def _():
        for blk, b_ref in enumerate((bp_ref, bc_ref)):
            bk = b_ref[...]
            for h in range(8):
                acc = jnp.zeros((QB, QB), F32)
                for b in range(N_BUCKETS):
                    acc = jnp.where(bk == b, rb_ref[b, h], acc)
                bias_ref[h, blk] = acc

    band = [(lane > row) & (i > 0), lane <= row]

    def stacks(ref):
        t = ref[...].astype(F32)
        sw = pltpu.roll(t, HEAD_DIM, 1)
        return [jnp.concatenate([jnp.where(lo, t, 0.0), jnp.where(lo, 0.0, sw)], axis=0).astype(BF16),
                jnp.concatenate([jnp.where(lo, sw, 0.0), jnp.where(lo, 0.0, t)], axis=0).astype(BF16)]

    ks = [stacks(kp_ref), stacks(kc_ref)]
    vs = [stacks(vp_ref), stacks(vc_ref)]
    return lo, band, ks, vs


def _lane_half(t, h):
    return t[:, (h % 2) * QB:(h % 2 + 1) * QB]


def swa_fwd(p, sinks, rel_bias, bprev, bcur, name, riders=()):
    T = p.shape[0]
    nq = T // QB
    kcol, vcol = (3 * SB_W + SWA_W) // KV_W, (3 * SB_W + SWA_W) // KV_W + 1
    sinks, srow = sinks if isinstance(sinks, tuple) else (sinks, 0)

    def body(q_ref, kp_ref, kc_ref, vp_ref, vc_ref, bp_ref, bc_ref, sink_ref, rb_ref, o_ref, lse_ref, bias_ref):
        i = pl.program_id(0)
        lo, band, ks, vs = _swa_common(i, kp_ref, kc_ref, vp_ref, vc_ref, bp_ref, bc_ref, rb_ref, bias_ref)
        heads, pairs, blocks = range(8), range(4), range(2)
        rowmax = lambda t: jnp.max(t, axis=1, keepdims=True)
        rowsum = lambda t: jnp.sum(t, axis=1, keepdims=True)
        q2 = [q_ref[:, g * PAIR:(g + 1) * PAIR] for g in pairs]
        s2 = [[_nt(q2[g], ks[b][g // 2]) for b in blocks] for g in pairs]
        sc = [[jnp.where(band[b], _lane_half(s2[h // 2][b], h) * SCALE + bias_ref[h, b], NEG_INF) for b in blocks] for h in heads]
        sink = [sink_ref[srow, h] for h in heads]
        m = [jnp.maximum(jnp.maximum(rowmax(sc[h][0]), rowmax(sc[h][1])), sink[h]) for h in heads]
        e = [[jnp.exp(sc[h][b] - m[h]) for b in blocks] for h in heads]
        den = [rowsum(e[h][0]) + rowsum(e[h][1]) + jnp.exp(sink[h] - m[h]) for h in heads]
        pb = [[(e[h][b] / den[h]).astype(BF16) for b in blocks] for h in heads]
        for g in pairs:
            both = lambda b: jnp.concatenate([pb[2 * g][b], pb[2 * g + 1][b]], axis=1)
            o_ref[:, g * PAIR:(g + 1) * PAIR] = _nn(both(0), vs[0][g // 2]) + _nn(both(1), vs[1][g // 2])
        for h in heads:
            lse_ref[:, h * QB:(h + 1) * QB] = jnp.broadcast_to(m[h] + jnp.log(den[h]), (QB, QB))

    kv = lambda col, prev: pl.BlockSpec((QB, KV_W), (lambda i: (jnp.maximum(i - 1, 0), col)) if prev else (lambda i: (i, col)))
    full = pl.BlockSpec((QB, QB), lambda i: (0, 0))
    smem = pl.BlockSpec(memory_space=pltpu.SMEM)
    return _call(
        body, name=name, grid=(nq,), out_shape=(S((T, SWA_W), F32), S((T, 8 * QB), F32)),
        in_specs=[pl.BlockSpec((QB, SWA_W), lambda i: (i, 3)), kv(kcol, True), kv(kcol, False), kv(vcol, True), kv(vcol, False),
                  full, full, smem, smem],
        out_specs=(pl.BlockSpec((QB, SWA_W), lambda i: (i, 0)), pl.BlockSpec((QB, 8 * QB), lambda i: (i, 0))),
        scratch=[pltpu.VMEM((8, 2, QB, QB), F32)],
        sem=("arbitrary",), args=(p, p, p, p, p, bprev, bcur, sinks, rel_bias), riders=riders)


def swa_bwd(p, do, lse, sinks, rel_bias, bprev, bcur, name, riders=()):
    T = p.shape[0]
    nq = T // QB
    kcol, vcol = (3 * SB_W + SWA_W) // KV_W, (3 * SB_W + SWA_W) // KV_W + 1
    sinks, srow = sinks if isinstance(sinks, tuple) else (sinks, 0)

    def body(q_ref, kp_ref, kc_ref, vp_ref, vc_ref, do_ref, lse_ref, bp_ref, bc_ref, sink_ref, rb_ref,
             dq_ref, dk_ref, dv_ref, dsink_ref, dsc_ref, bias_ref, dk_acc, dv_acc):
        i = pl.program_id(0)
        lo, band, ks, vs = _swa_common(i, kp_ref, kc_ref, vp_ref, vc_ref, bp_ref, bc_ref, rb_ref, bias_ref)

        @pl.when(i == 0)
        def _():
            dk_acc[...] = jnp.zeros_like(dk_acc)
            dv_acc[...] = jnp.zeros_like(dv_acc)
            dsc_ref[...] = jnp.zeros_like(dsc_ref)
            dsink_ref[...] = jnp.zeros_like(dsink_ref)

        heads, pairs, blocks = range(8), range(4), range(2)
        rowsum = lambda t: jnp.sum(t, axis=1, keepdims=True)
        by_head = lambda t: jnp.concatenate([jnp.where(lo, t, 0), jnp.where(lo, 0, t)], axis=0)
        q2 = [q_ref[:, g * PAIR:(g + 1) * PAIR] for g in pairs]
        d2 = [do_ref[:, g * PAIR:(g + 1) * PAIR].astype(BF16) for g in pairs]
        qs = [by_head(q2[g]) for g in pairs]
        dos = [by_head(d2[g]) for g in pairs]
        s2 = [[_nt(q2[g], ks[b][g // 2]) for b in blocks] for g in pairs]
        dp2 = [[_nt(d2[g], vs[b][g // 2]) for b in blocks] for g in pairs]
        lse_h = [lse_ref[:, h * QB:(h + 1) * QB] for h in heads]
        sink = [sink_ref[srow, h] for h in heads]
        pr = [[jnp.exp(jnp.where(band[b], _lane_half(s2[h // 2][b], h) * SCALE + bias_ref[h, b], NEG_INF) - lse_h[h])
               for b in blocks] for h in heads]
        dp = [[_lane_half(dp2[h // 2][b], h) for b in blocks] for h in heads]
        delta = [rowsum(pr[h][0] * dp[h][0]) + rowsum(pr[h][1] * dp[h][1]) for h in heads]
        lane1 = lax.broadcasted_iota(jnp.int32, (1, QB), 1)
        dsink = jnp.zeros((1, QB), F32)
        for h in heads:
            dsink = dsink + jnp.where(lane1 == h, -jnp.sum(jnp.exp(sink[h] - lse_h[h][:, :1]) * delta[h]), 0.0)
        dsink_ref[...] += dsink
        dsc = [[pr[h][b] * (dp[h][b] - delta[h]) for b in blocks] for h in heads]
        for h in heads:
            for b in blocks:
                dsc_ref[h, b] += dsc[h][b]
        dzb = [[(dsc[h][b] * SCALE).astype(BF16) for b in blocks] for h in heads]
        prb = [[pr[h][b].astype(BF16) for b in blocks] for h in heads]
        pair_of = lambda t, g, b, axis: jnp.concatenate([t[2 * g][b], t[2 * g + 1][b]], axis=axis)
        for g in pairs:
            dq = _nn(pair_of(dzb, g, 0, 1), ks[0][g // 2]) + _nn(pair_of(dzb, g, 1, 1), ks[1][g // 2])
            dq_ref[:, g * PAIR:(g + 1) * PAIR] = dq.astype(BF16)

        def key_grad(t, other, b):
            per_kv = [_tn(pair_of(t, 2 * kh, b, 0), other[2 * kh]) + _tn(pair_of(t, 2 * kh + 1, b, 0), other[2 * kh + 1]) for kh in range(2)]
            both = [s + pltpu.roll(s, HEAD_DIM, 1) for s in per_kv]
            return jnp.where(lo, both[0], both[1])

        rp = pl.multiple_of(jnp.maximum(i - 1, 0) * QB, QB)
        rc = pl.multiple_of(i * QB, QB)
        dk_acc[pl.ds(rp, QB), :] += key_grad(dzb, qs, 0)
        dv_acc[pl.ds(rp, QB), :] += key_grad(prb, dos, 0)
        dk_acc[pl.ds(rc, QB), :] += key_grad(dzb, qs, 1)
        dv_acc[pl.ds(rc, QB), :] += key_grad(prb, dos, 1)

        @pl.when(i == nq - 1)
        def _():
            dk_ref[...] = dk_acc[...].astype(BF16)
            dv_ref[...] = dv_acc[...].astype(BF16)

    kv = lambda col, prev: pl.BlockSpec((QB, KV_W), (lambda i: (jnp.maximum(i - 1, 0), col)) if prev else (lambda i: (i, col)))
    full = pl.BlockSpec((QB, QB), lambda i: (0, 0))
    smem = pl.BlockSpec(memory_space=pltpu.SMEM)
    whole = lambda shape: pl.BlockSpec(shape, lambda i: (0,) * len(shape))
    return _call(
        body, name=name, grid=(nq,),
        out_shape=(S((T, SWA_W), BF16), S((T, KV_W), BF16), S((T, KV_W), BF16), S((1, QB), F32), S((8, 2, QB, QB), F32)),
        in_specs=[pl.BlockSpec((QB, SWA_W), lambda i: (i, 3)), kv(kcol, True), kv(kcol, False), kv(vcol, True), kv(vcol, False),
                  pl.BlockSpec((QB, SWA_W), lambda i: (i, 0)), pl.BlockSpec((QB, 8 * QB), lambda i: (i, 0)),
                  full, full, smem, smem],
        out_specs=(pl.BlockSpec((QB, SWA_W), lambda i: (i, 0)), whole((T, KV_W)), whole((T, KV_W)), whole((1, QB)),
                   whole((8, 2, QB, QB))),
        scratch=[pltpu.VMEM((8, 2, QB, QB), F32), pltpu.VMEM((T, KV_W), F32), pltpu.VMEM((T, KV_W), F32)],
        sem=("arbitrary",), args=(p, p, p, p, p, do, lse, bprev, bcur, sinks, rel_bias), riders=riders)


def mix_out_fwd(o_sb, o_sw, g_sb, g_sw, wout, h, g_next, name, riders=()):
    T, D = h.shape
    M = SB_W + SWA_W
    tm = _tile(T, 256)

    def body(a_ref, b_ref, ga_ref, gb_ref, w_ref, h_ref, gn_ref, mx_ref, o_ref, n_ref):
        mx_ref[:, :SB_W] = _rms(a_ref[...], ga_ref[...]).astype(BF16)
        mx_ref[:, SB_W:] = _rms(b_ref[...], gb_ref[...]).astype(BF16)
        out = h_ref[...] + _nn(mx_ref[...], w_ref[...])
        o_ref[...] = out
        n_ref[...] = _rms(out, gn_ref[...]).astype(BF16)

    row = lambda n: pl.BlockSpec((tm, n), lambda i: (i, 0))
    (g_sb, sb_spec), (g_sw, sw_spec), (g_next, next_spec) = _gain(g_sb), _gain(g_sw), _gain(g_next)
    return _call(
        body, name=name, grid=(T // tm,), out_shape=(S((T, M), BF16), S((T, D), F32), S((T, D), BF16)),
        in_specs=[row(SB_W), row(SWA_W), sb_spec, sw_spec, pl.BlockSpec((M, D), lambda i: (0, 0)), row(D), next_spec],
        out_specs=(row(M), row(D), row(D)),
        sem=("parallel",), args=(o_sb, o_sw, g_sb, g_sw, wout, h, g_next), riders=riders)


def loss_head(h, g, target, name):
    T, D = h.shape
    tm = _tile(T, 256)

    def body(h_ref, g_ref, t_ref, loss_ref, dh_ref, dhb_ref, dg_ref):
        @pl.when(pl.program_id(0) == 0)
        def _():
            loss_ref[...] = jnp.zeros_like(loss_ref)
            dg_ref[...] = jnp.zeros_like(dg_ref)
        x = h_ref[...]
        err = _rms(x, g_ref[...]) - t_ref[...]
        loss_ref[...] += jnp.full((1, QB), 0.5 * jnp.sum(jnp.mean(err * err, axis=-1)), F32)
        dx, dg = _rms_bwd(err / D, x, g_ref[...])
        dh_ref[...] = dx
        dhb_ref[...] = dx.astype(BF16)
        dg_ref[...] += dg

    row = pl.BlockSpec((tm, D), lambda i: (i, 0))
    vec = pl.BlockSpec((1, D), lambda i: (0, 0))
    return pl.pallas_call(
        body, name=name, grid=(T // tm,), out_shape=(S((1, QB), F32), S((T, D), F32), S((T, D), BF16), S((1, D), F32)),
        in_specs=[row, vec, row], out_specs=(pl.BlockSpec((1, QB), lambda i: (0, 0)), row, row, vec),
        compiler_params=_params(("arbitrary",)),
    )(h, g, target)


def ffn_down_bwd(dhb, wd, gate, up, a, n, name, riders=()):
    T, D = dhb.shape
    F = wd.shape[0]
    tr, tn = _tile(T, 512), _tile(F, 256)

    def body(d_ref, n_ref, w_ref, g_ref, u_ref, a_ref, o_ref, dwd_ref, dwdb_ref, dwgu_ref, dwgub_ref):
        w = w_ref[...]
        for r in range(T // tr):
            rows = slice(r * tr, (r + 1) * tr)
            da = 0.5 * _nt(d_ref[rows, :], w)
            o_ref[0, rows, :] = (da * g_ref[rows, :].astype(F32)).astype(BF16)
            o_ref[1, rows, :] = (da * u_ref[rows, :].astype(F32)).astype(BF16)
        dwd = 0.5 * _tn(a_ref[...], d_ref[...])
        dwd_ref[...] = dwd
        dwdb_ref[...] = dwd.astype(BF16)
        for s in range(2):
            dwgu = _tn(o_ref[s], n_ref[...])
            dwgu_ref[s] = dwgu
            dwgub_ref[s] = dwgu.astype(BF16)

    tile = pl.BlockSpec((T, tn), lambda j: (0, j))
    whole = pl.BlockSpec((T, D), lambda j: (0, 0))
    rows1, rows2 = pl.BlockSpec((tn, D), lambda j: (j, 0)), pl.BlockSpec((2, tn, D), lambda j: (0, j, 0))
    return _call(
        body, name=name, grid=(F // tn,),
        out_shape=(S((2, T, F), BF16), S((F, D), F32), S((F, D), BF16), S((2, F, D), F32), S((2, F, D), BF16)),
        in_specs=[whole, whole, rows1, tile, tile, tile],
        out_specs=(pl.BlockSpec((2, T, tn), lambda j: (0, 0, j)), rows1, rows1, rows2, rows2),
        sem=("parallel",), args=(dhb, n, wd, gate, up, a), riders=riders)


def tn_matmul(xs, y, alpha, name, riders=()):
    B, T, N = xs.shape
    D = y.shape[1]
    tn = _tile(N, 256)

    def body(x_ref, y_ref, o_ref, ob_ref):
        o = alpha * _tn(x_ref[...], y_ref[...])
        o_ref[...] = o
        ob_ref[...] = o.astype(BF16)

    tile = pl.BlockSpec((None, tn, D), lambda s, j: (s, j, 0))
    return _call(
        body, name=name, grid=(B, N // tn), out_shape=(S((B, N, D), F32), S((B, N, D), BF16)),
        in_specs=[pl.BlockSpec((None, T, tn), lambda s, j: (s, 0, j)), pl.BlockSpec((T, D), lambda s, j: (0, 0))],
        out_specs=(tile, tile), sem=("parallel", "parallel"), args=(xs, y), riders=riders)


def nn_rms_bwd(xs, ws, h_in, g, dh, name, riders=()):
    B, T, K = xs.shape
    D = ws.shape[2]
    tm = _tile(T, 256)

    def body(x_ref, w_ref, h_ref, g_ref, d_ref, o_ref, ob_ref, dg_ref):
        @pl.when(pl.program_id(0) == 0)
        def _():
            dg_ref[...] = jnp.zeros_like(dg_ref)
        dn = _nn(x_ref[0], w_ref[0])
        for s in range(1, B):
            dn = dn + _nn(x_ref[s], w_ref[s])
        dx, dg = _rms_bwd(dn, h_ref[...], g_ref[...])
        out = d_ref[...] + dx
        o_ref[...] = out
        ob_ref[...] = out.astype(BF16)
        dg_ref[...] += dg

    row = pl.BlockSpec((tm, D), lambda i: (i, 0))
    vec = pl.BlockSpec((1, D), lambda i: (0, 0))
    g, g_spec = _gain(g)
    return _call(
        body, name=name, grid=(T // tm,), out_shape=(S((T, D), F32), S((T, D), BF16), S((1, D), F32)),
        in_specs=[pl.BlockSpec((B, tm, K), lambda i: (0, i, 0)), pl.BlockSpec((B, K, D), lambda i: (0, 0, 0)), row, g_spec, row],
        out_specs=(row, row, vec),
        sem=("arbitrary",), args=(xs, ws, h_in, g, dh), riders=riders)


def mix_out_bwd(dhb, wout, mixed, o_sb, o_sw, g_sb, g_sw, name):
    T, D = dhb.shape
    M = SB_W + SWA_W
    tm = _tile(T, 256)
    steps = T // tm

    def body(d_ref, w_ref, mx_ref, a_ref, b_ref, ga_ref, gb_ref, da_ref, db_ref, dga_ref, dgb_ref, dw_ref, dwb_ref):
        i = pl.program_id(0)

        @pl.when(i == 0)
        def _():
            dga_ref[...] = jnp.zeros_like(dga_ref)
            dgb_ref[...] = jnp.zeros_like(dgb_ref)
            dw_ref[...] = jnp.zeros_like(dw_ref)
        dm = _nt(d_ref[...], w_ref[...])
        dxa, dga = _rms_bwd(dm[:, :SB_W], a_ref[...], ga_ref[...])
        dxb, dgb = _rms_bwd(dm[:, SB_W:], b_ref[...], gb_ref[...])
        da_ref[...] = dxa
        db_ref[...] = dxb
        dga_ref[...] += dga
        dgb_ref[...] += dgb
        dw_ref[...] += _tn(mx_ref[...], d_ref[...])

        @pl.when(i == steps - 1)
        def _():
            dwb_ref[...] = dw_ref[...].astype(BF16)

    row = lambda n: pl.BlockSpec((tm, n), lambda i: (i, 0))
    vec = lambda n: pl.BlockSpec((1, n), lambda i: (0, 0))
    whole = pl.BlockSpec((M, D), lambda i: (0, 0))
    (g_sb, sb_spec), (g_sw, sw_spec) = _gain(g_sb), _gain(g_sw)
    return pl.pallas_call(
        body, name=name, grid=(steps,),
        out_shape=(S((T, SB_W), F32), S((T, SWA_W), F32), S((1, SB_W), F32), S((1, SWA_W), F32), S((M, D), F32), S((M, D), BF16)),
        in_specs=[row(D), whole, row(M), row(SB_W), row(SWA_W), sb_spec, sw_spec],
        out_specs=(row(SB_W), row(SWA_W), vec(SB_W), vec(SWA_W), whole, whole),
        compiler_params=_params(("arbitrary",)),
    )(dhb, wout, mixed, o_sb, o_sw, g_sb, g_sw)


def rel_bias_grad(dscs, bprev, bcur, name):
    n = len(dscs)

    def body(*refs):
        bp_ref, bc_ref, o_ref = refs[n], refs[n + 1], refs[n + 2]
        bks = [bp_ref[...], bc_ref[...]]
        row = lax.broadcasted_iota(jnp.int32, (N_BUCKETS, QB), 0)
        lane = lax.broadcasted_iota(jnp.int32, (N_BUCKETS, QB), 1)
        out = jnp.zeros((N_BUCKETS, QB), F32)
        for h in range(8):
            tot = [sum(refs[l][h, b] for l in range(n)) for b in range(2)]
            for b in range(N_BUCKETS):
                val = jnp.sum(jnp.where(bks[0] == b, tot[0], 0.0)) + jnp.sum(jnp.where(bks[1] == b, tot[1], 0.0))
                out = jnp.where((row == b) & (lane == h), val, out)
        o_ref[...] = out

    return pl.pallas_call(body, name=name, out_shape=S((N_BUCKETS, QB), F32), compiler_params=_params())(*dscs, bprev, bcur)


def _adamw(w, g, m, v):
    m = ADAM_B1 * m + (1.0 - ADAM_B1) * g
    v = ADAM_B2 * v + (1.0 - ADAM_B2) * (g * g)
    m_hat = m / (1.0 - ADAM_B1 ** ADAM_STEP)
    v_hat = v / (1.0 - ADAM_B2 ** ADAM_STEP)
    delta = -ADAM_LR * (m_hat / (jnp.sqrt(v_hat) + ADAM_EPS) + ADAM_WD * w)
    return delta, m, v


def adamw_scattered(w, m, v, owns, others, name, riders=()):
    L, R, C = w.shape
    tr = _rows_tile(R, 176)

    def body(w_ref, m_ref, v_ref, *rest):
        own_refs, other_refs = rest[:L], rest[L:2 * L]
        g_ref, d_ref, mo_ref, vo_ref = rest[2 * L:]
        layer = pl.program_id(0)

        def grad(k):
            o = other_refs[k]
            return own_refs[k][...] + o[0].astype(F32) + o[1].astype(F32) + o[2].astype(F32)

        g = grad(0)
        for k in range(1, L):
            g = jnp.where(layer == k, grad(k), g)
        d, mn, vn = _adamw(w_ref[...], g, m_ref[...], v_ref[...])
        g_ref[...] = g
        d_ref[...] = d
        mo_ref[...] = mn
        vo_ref[...] = vn

    tile = pl.BlockSpec((None, tr, C), lambda l, i: (l, i, 0))
    return _call(
        body, name=name, grid=(L, R // tr), out_shape=(S((L, R, C), F32),) * 4,
        in_specs=[tile] * 3 + [pl.BlockSpec((tr, C), lambda l, i: (i, 0))] * L + [pl.BlockSpec((3, tr, C), lambda l, i: (0, i, 0))] * L,
        out_specs=(tile,) * 4, sem=("parallel", "parallel"), args=(w, m, v, *owns, *others), riders=riders)


def adamw_small(w, gs, m, v, name):
    R, C = w.shape

    def body(w_ref, g_ref, m_ref, v_ref, go_ref, d_ref, mo_ref, vo_ref):
        g = g_ref[0]
        for k in range(1, N_DEV):
            g = g + g_ref[k]
        d, mn, vn = _adamw(w_ref[...], g, m_ref[...], v_ref[...])
        go_ref[...] = g
        d_ref[...] = d
        mo_ref[...] = mn
        vo_ref[...] = vn

    return pl.pallas_call(body, name=name, out_shape=(S((R, C), F32),) * 4, compiler_params=_params())(w, gs, m, v)


def kernel(x, norm_ffn1, w_ffn1_gu, w_ffn1_down, norm_mix, w_in, sinks, norm_out_sb, norm_out_swa, w_out, norm_ffn2, w_ffn2_gu, w_ffn2_down, rel_bias, norm_final, loss_target, m_norm_ffn1, m_w_ffn1_gu, m_w_ffn1_down, m_norm_mix, m_w_in, m_sinks, m_norm_out_sb, m_norm_out_swa, m_w_out, m_norm_ffn2, m_w_ffn2_gu, m_w_ffn2_down, m_rel_bias, m_norm_final, v_norm_ffn1, v_w_ffn1_gu, v_w_ffn1_down, v_norm_mix, v_w_in, v_sinks, v_norm_out_sb, v_norm_out_swa, v_w_out, v_norm_ffn2, v_w_ffn2_gu, v_w_ffn2_down, v_rel_bias, v_norm_final):
    L = norm_ffn1.shape[0]
    T, D = x.shape[1], x.shape[2]
    F = w_ffn1_down.shape[1] * N_DEV
    h = x.reshape(T, D)
    target = loss_target.reshape(T, D)
    after, upto, before = _tri_consts()
    bprev, bcur = _t5_buckets()

    local = {}
    for l in range(L):
        local[f"gu1_{l}"] = w_ffn1_gu[l].T.astype(BF16)
        local[f"d1_{l}"] = w_ffn1_down[l].astype(BF16)
        local[f"in_{l}"] = w_in[l].T.astype(BF16)
        local[f"out_{l}"] = w_out[l].astype(BF16)
        local[f"gu2_{l}"] = w_ffn2_gu[l].T.astype(BF16)
        local[f"d2_{l}"] = w_ffn2_down[l].astype(BF16)
    full, partial = {}, {}
    grads, chip_sum, recv_b = {}, {}, {}

    def run(fn, *args, ag=(), rs1=(), rs2=()):
        halves = lambda names: [n if isinstance(n, tuple) else (n, None) for n in names]
        ag, rs2 = [(n, k) for n, k in halves(ag) if n in local], halves(rs2)
        rows = lambda k, total: None if k is None else (k * (total // 2), total // 2)

        def second(n, k):
            sb = chip_sum[n][1]
            return scatter_second(sb, rows(k, sb.shape[1]), recv_b.get(n))

        riders = ([gather(local[n], rows(k, local[n].shape[0]), partial.get(n)) for n, k in ag]
                  + [scatter_first(grads[n][1]) for n in rs1] + [second(n, k) for n, k in rs2])
        if not riders:
            return fn(*args)
        outs, per = fn(*args, riders=riders)
        per = [p[0] for p in per]
        for n, k in ag:
            buf = per.pop(0)
            if k == 0:
                partial[n] = buf
            else:
                full[n] = buf.reshape(N_DEV * buf.shape[1], D)
        if rs1:
            sums = scatter_add([grads[n][0] for n in rs1], [per.pop(0) for n in rs1], "rs_add_" + "_".join(rs1))
            chip_sum.update(zip(rs1, sums))
        for n, _ in rs2:
            recv_b[n] = per.pop(0)
        return outs

    def attn_fwd(p, sink, name, riders=()):
        return side_by_side(sb_attn_fwd(p, after, name, riders=PARTS), swa_fwd(p, sink, rel_bias, bprev, bcur, name, riders=PARTS),
                            name, riders)

    def attn_bwd(p, do_sb, tot, do_sw, lse, sink, name, riders=()):
        return side_by_side(sb_attn_bwd(p, do_sb, tot, upto, before, name, riders=PARTS),
                            swa_bwd(p, do_sw, lse, sink, rel_bias, bprev, bcur, name, riders=PARTS), name, riders)

    gu = lambda n: full[n].reshape(2, F, D)
    slots = lambda pair: tuple(t.reshape(N_DEV, -1, D) for t in pair)
    vec = lambda a: a.reshape(1, -1)

    PW = max(D, SB_W + SWA_W)
    n_rows = 4 * L + 2
    n_rows += (-n_rows) % 8

    def pack(ffn1, mix, ffn2, final, osb, osw, snk, rel, extra):
        pieces = []

        def row(*parts):
            flat = [a.reshape(-1) for a in parts]
            pieces.extend(flat)
            used = sum(a.size for a in flat)
            if used < PW:
                pieces.append(jnp.zeros((PW - used,), F32))

        for group in (ffn1, mix, ffn2):
            for l in range(L):
                row(group[l])
        row(final)
        for l in range(L):
            row(osb[l], osw[l])
        row(*[snk[l].reshape(-1)[:8] for l in range(L)], rel, extra)
        pieces.append(jnp.zeros(((n_rows - 4 * L - 2) * PW,), F32))
        return jnp.concatenate(pieces).reshape(n_rows, PW)

    def unpack(arr):
        ffn1, mix, ffn2 = arr[0:L, :D], arr[L:2 * L, :D], arr[2 * L:3 * L, :D]
        final = arr[3 * L, :D]
        ob = arr[3 * L + 1:4 * L + 1]
        tail = arr[4 * L + 1]
        return (ffn1, mix, tail[:8 * L].reshape(L, 8), ob[:, :SB_W], ob[:, SB_W:SB_W + SWA_W], ffn2,
                tail[8 * L:8 * L + N_BUCKETS * 8].reshape(N_BUCKETS, 8), final)

    zero = jnp.zeros((1,), F32)
    w_small = pack(norm_ffn1, norm_mix, norm_ffn2, norm_final, norm_out_sb, norm_out_swa, sinks, rel_bias, zero)
    g_ffn1, g_mix, g_ffn2, g_osb, g_osw = [a.reshape(L, 1, -1) for a in (norm_ffn1, norm_mix, norm_ffn2, norm_out_sb, norm_out_swa)]

    saved = []
    n_next = run(rms_cast, h, (g_ffn1, 0), "rms_first", ag=("gu1_0",))
    for l in range(L):
        nx = l + 1
        s = {"h0": h, "n1": n_next}
        s["gate1"], s["up1"], s["a1"] = run(ffn_up_fwd, s["n1"], gu(f"gu1_{l}"), f"ffn1_up{l}",
                                            ag=(f"d1_{l}", ("in_0", 0) if l == 0 else (f"in_{l}", 1)))
        h = run(ffn_down_fwd, s["a1"], full[f"d1_{l}"], h, None, f"ffn1_down{l}", ag=(("in_0", 1),) if l == 0 else ())
        s["h1"] = h
        s["n2"], s["p"] = mix_in_fwd(h, (g_mix, l), full[f"in_{l}"], f"mix_in{l}")
        s["o_sb"], s["tot"], s["o_sw"], s["lse"] = run(attn_fwd, s["p"], (sinks, l), f"attn_fwd{l}",
                                                       ag=(f"out_{l}", f"gu2_{l}", f"d2_{l}", (f"gu1_{nx}", 0)))
        s["mixed"], h, s["n3"] = run(mix_out_fwd, s["o_sb"], s["o_sw"], (g_osb, l), (g_osw, l),
                                     full[f"out_{l}"], h, (g_ffn2, l), f"mix_out{l}")
        s["h2"] = h
        s["gate2"], s["up2"], s["a2"] = run(ffn_up_fwd, s["n3"], gu(f"gu2_{l}"), f"ffn2_up{l}",
                                            ag=((f"gu1_{nx}", 1), (f"in_{nx}", 0)))
        if nx < L:
            h, n_next = run(ffn_down_fwd, s["a2"], full[f"d2_{l}"], h, (g_ffn1, nx), f"ffn2_down{l}")
        else:
            h = run(ffn_down_fwd, s["a2"], full[f"d2_{l}"], h, None, f"ffn2_down{l}")
        saved.append(s)

    loss_part, dh, dhb, dg_final = loss_head(h, vec(norm_final), target, "loss_head")

    small = {k: [None] * L for k in ("ffn1", "mix", "sinks", "osb", "osw", "ffn2", "dsc")}
    for l in reversed(range(L)):
        s = saved[l]

        def ffn_bwd(dh, dhb, tag, gate, up, a, n, h_in, g, r_down, r_up):
            gu_n, d_n = f"gu{tag}_{l}", f"d{tag}_{l}"
            dgu, dwd, dwdb, dwgu, dwgub = run(ffn_down_bwd, dhb, full[d_n], gate, up, a, n, f"ffn{tag}_down_bwd{l}", **r_down)
            grads[gu_n], grads[d_n] = slots((dwgu, dwgub)), slots((dwd, dwdb))
            return run(nn_rms_bwd, dgu, gu(gu_n), h_in, g, dh, f"ffn{tag}_up_bwd{l}", **r_up)

        later = l + 1 < L
        dh, dhb, small["ffn2"][l] = ffn_bwd(dh, dhb, 2, s["gate2"], s["up2"], s["a2"], s["n3"], s["h2"], (g_ffn2, l),
                                            dict(rs2=((f"gu1_{l + 1}", 0), f"d1_{l + 1}") if later else ()),
                                            dict(rs1=(f"gu2_{l}", f"d2_{l}")))
        do_sb, do_sw, small["osb"][l], small["osw"][l], dw_out, dw_out_b = mix_out_bwd(
            dhb, full[f"out_{l}"], s["mixed"], s["o_sb"], s["o_sw"], (g_osb, l), (g_osw, l), f"mix_out_bwd{l}")
        grads[f"out_{l}"] = slots((dw_out, dw_out_b))
        dq_sb, dk_sb, dv_sb, dq_sw, dk_sw, dv_sw, small["sinks"][l], small["dsc"][l] = run(
            attn_bwd, s["p"], do_sb, s["tot"], do_sw, s["lse"], (sinks, l), f"attn_bwd{l}",
            rs2=(f"gu2_{l}", f"d2_{l}") + (((f"gu1_{l + 1}", 1),) if later else ()), rs1=(f"out_{l}",))
        dp = jnp.concatenate([dq_sb, dk_sb, dv_sb, dq_sw, dk_sw, dv_sw], axis=1)
        dh, dhb, small["mix"][l] = nn_rms_bwd(dp[None], full[f"in_{l}"][None], s["h1"], (g_mix, l), dh, f"mix_in_bwd{l}")
        grads[f"in_{l}"] = slots(tn_matmul(dp[None], s["n2"], 1.0, f"dwin{l}"))
        dh, dhb, small["ffn1"][l] = ffn_bwd(dh, dhb, 1, s["gate1"], s["up1"], s["a1"], s["n1"], s["h0"], (g_ffn1, l),
                                            dict(rs1=(f"in_{l}",), rs2=(f"out_{l}",)),
                                            dict(rs1=(f"gu1_{l}", f"d1_{l}"), rs2=(f"in_{l}",)))

    grad_x = dh.reshape(x.shape)

    upd = {}
    for nm, w, m, v, transposed, last in (
            ("gu2", w_ffn2_gu, m_w_ffn2_gu, v_w_ffn2_gu, True, ("gu1_0", "d1_0")), ("d2", w_ffn2_down, m_w_ffn2_down, v_w_ffn2_down, False, ()),
            ("in", w_in, m_w_in, v_w_in, True, ()), ("out", w_out, m_w_out, v_w_out, False, ()),
            ("gu1", w_ffn1_gu, m_w_ffn1_gu, v_w_ffn1_gu, True, ()), ("d1", w_ffn1_down, m_w_ffn1_down, v_w_ffn1_down, False, ())):
        turn = (lambda a: jnp.swapaxes(a, 1, 2)) if transposed else (lambda a: a)
        names = [f"{nm}_{l}" for l in range(L)]
        res = run(adamw_scattered, turn(w), turn(m), turn(v), [chip_sum[n][0] for n in names], [recv_b[n] for n in names],
                  f"adamw_{nm}", rs2=last)
        upd[nm] = tuple(turn(r) for r in res)

    d_rel = rel_bias_grad(small["dsc"], bprev, bcur, "rel_bias_grad")[:, :8]
    g_small = pack(small["ffn1"], small["mix"], small["ffn2"], dg_final, small["osb"], small["osw"], small["sinks"], d_rel,
                   loss_part[0, :1])
    m_small = pack(m_norm_ffn1, m_norm_mix, m_norm_ffn2, m_norm_final, m_norm_out_sb, m_norm_out_swa, m_sinks, m_rel_bias, zero)
    v_small = pack(v_norm_ffn1, v_norm_mix, v_norm_ffn2, v_norm_final, v_norm_out_sb, v_norm_out_swa, v_sinks, v_rel_bias, zero)
    gs_small = all_gather_rows(g_small, "ag_small")
    summed = adamw_small(w_small, gs_small, m_small, v_small, "adamw_small")
    small_out = [unpack(a) for a in summed]
    loss = summed[0][4 * L + 1, 8 * L + N_BUCKETS * 8]

    def group(k):
        sm = small_out[k]
        return (sm[0], upd["gu1"][k], upd["d1"][k], sm[1], upd["in"][k], sm[2], sm[3], sm[4], upd["out"][k], sm[5],
                upd["gu2"][k], upd["d2"][k], sm[6], sm[7])

    return (loss, grad_x, *group(0), *group(1), *group(2), *group(3))
```

```python
import math

import jax
import jax.numpy as jnp
from jax import lax
from jax.experimental import pallas as pl
from jax.experimental.pallas import tpu as pltpu

F32 = jnp.float32
BF16 = jnp.bfloat16
S = jax.ShapeDtypeStruct

N_DEV = 8
HEAD_DIM = 64
SB_HEADS = 8
PAIR = 2 * HEAD_DIM
SB_W = 512
SWA_W = 512
KV_W = 128
IN_W = 3 * SB_W + SWA_W + 2 * KV_W
QB = 128
N_BUCKETS = 32
MAX_DISTANCE = 128
EPS = 1e-6
NEG_INF = -1e30
SCALE = HEAD_DIM ** -0.5

ADAM_LR = 0.001
ADAM_B1 = 0.9
ADAM_B2 = 0.999
ADAM_EPS = 1e-08
ADAM_WD = 0.01
ADAM_STEP = 10

VMEM_LIMIT = 56 * 1024 * 1024
MESH = pl.DeviceIdType.MESH


def _params(sem=None, vmem=VMEM_LIMIT):
    return pltpu.CompilerParams(dimension_semantics=sem, vmem_limit_bytes=vmem)


def _nn(a, b):
    return jnp.dot(a, b, preferred_element_type=F32)


def _nt(a, b):
    return lax.dot_general(a, b, (((1,), (1,)), ((), ())), preferred_element_type=F32)


def _tn(a, b):
    return lax.dot_general(a, b, (((0,), (0,)), ((), ())), preferred_element_type=F32)


def _tri(xs, m):
    return [_nn(x.astype(BF16), m) for x in xs]


def _rms(x, g):
    r = lax.rsqrt(jnp.mean(x * x, axis=-1, keepdims=True) + EPS)
    return x * r * g


def _rms_bwd(dy, x, g):
    r = lax.rsqrt(jnp.mean(x * x, axis=-1, keepdims=True) + EPS)
    xhat = x * r
    u = dy * g
    dx = r * (u - xhat * jnp.mean(u * xhat, axis=-1, keepdims=True))
    return dx, jnp.sum(dy * xhat, axis=0, keepdims=True)


def _softplus_logsig(z):
    sp = jnp.maximum(z, 0.0) + jnp.log(1.0 + jnp.exp(-jnp.abs(z)))
    return sp, z - sp


def _gain(g):
    if isinstance(g, tuple):
        rows, n = g
        return rows, pl.BlockSpec((None, 1, rows.shape[2]), lambda *_: (n, 0, 0))
    return g, pl.BlockSpec((1, g.shape[1]), lambda *_: (0, 0))


def _tile(n, want):
    t = min(n, want)
    while n % t:
        t //= 2
    return t


def _place():
    x, y, c = lax.axis_index("x"), lax.axis_index("y"), lax.axis_index("c")
    chips = [(1 - x, y), (x, 1 - y), (1 - x, 1 - y)]
    return x, y, c, chips


def all_gather_rows(v, name):
    R, C = v.shape

    def body(v_ref, out_ref, send_sems, recv_sems, local_sem):
        x, y, c, chips = _place()
        me, sibling = (x, y, c), (x, y, 1 - c)

        def slot(px, py, pc):
            return out_ref.at[4 * px + 2 * py + pc]

        def copy(k, block, to, src=None):
            return pltpu.make_async_remote_copy(
                src_ref=slot(*block) if src is None else src, dst_ref=slot(*block),
                send_sem=send_sems.at[k], recv_sem=recv_sems.at[k], device_id=to, device_id_type=MESH)

        mine = pltpu.make_async_copy(v_ref, slot(*me), local_sem)
        mine.start()
        first = [copy(0, me, sibling, src=v_ref)]
        first += [copy(1 + j, me, (*chip, c), src=v_ref) for j, chip in enumerate(chips)]
        for cp in first:
            cp.start()
        passed = [copy(4 + j, (*chip, c), sibling) for j, chip in enumerate(chips)]
        for j, chip in enumerate(chips):
            copy(1 + j, (*chip, c), me).wait_recv()
            passed[j].start()
        copy(0, sibling, me).wait_recv()
        for j, chip in enumerate(chips):
            copy(4 + j, (*chip, 1 - c), me).wait_recv()
        for cp in first + passed:
            cp.wait_send()
        mine.wait()

    return pl.pallas_call(
        body, name=name, out_shape=S((N_DEV, R, C), v.dtype),
        in_specs=[pl.BlockSpec(memory_space=pl.ANY)], out_specs=pl.BlockSpec(memory_space=pl.ANY),
        scratch_shapes=[pltpu.SemaphoreType.DMA((7,)), pltpu.SemaphoreType.DMA((7,)), pltpu.SemaphoreType.DMA],
    )(v)


class _Exchange:
    def __init__(self, ins, outs, sizes, n_local, plan, aliases=None):
        self.ins, self.outs, self.plan, self.aliases = list(ins), list(outs), plan, aliases or {}
        self.sizes, self.n_local = list(sizes), n_local

    def scratch(self):
        n = sum(self.sizes)
        return [pltpu.SemaphoreType.DMA((n,)), pltpu.SemaphoreType.DMA((n,)), pltpu.SemaphoreType.DMA((max(self.n_local, 1),))]

    def _copies(self, in_refs, out_refs, sems):
        send_sems, recv_sems, local_sems = sems
        phases, local = self.plan(in_refs, out_refs)
        out, k = [], 0
        for phase in phases:
            out.append([pltpu.make_async_remote_copy(src_ref=s, dst_ref=d, send_sem=send_sems.at[k + n], recv_sem=recv_sems.at[k + n],
                                                     device_id=dev, device_id_type=MESH) for n, (s, d, dev) in enumerate(phase)])
            k += len(phase)
        return out, [pltpu.make_async_copy(s, d, local_sems.at[n]) for n, (s, d) in enumerate(local)]

    def start(self, in_refs, out_refs, sems):
        phases, loc = self._copies(in_refs, out_refs, sems)
        for cp in phases[0] + loc:
            cp.start()

    def advance(self, hook, in_refs, out_refs, sems):
        p = hook - (3 - len(self.sizes))
        if p >= 1:
            phases, _ = self._copies(in_refs, out_refs, sems)
            for cp in phases[p - 1]:
                cp.wait_recv()
            for cp in phases[p]:
                cp.start()

    def finish(self, in_refs, out_refs, sems):
        phases, loc = self._copies(in_refs, out_refs, sems)
        for cp in phases[-1]:
            cp.wait_recv()
        for phase in phases:
            for cp in phase:
                cp.wait_send()
        for cp in loc:
            cp.wait()


def gather(v, rows=None, into=None):
    R, C = v.shape
    r0, nr = rows or (0, R)
    na = min(nr, ((nr // 2 + 15) // 16) * 16)

    def plan(ins, outs):
        x, y, c, _ = _place()
        xn, yn, dg, sibling = (1 - x, y), (x, 1 - y), (1 - x, 1 - y), (x, y, 1 - c)
        slot = lambda chip, start=r0, count=nr: outs[0].at[4 * chip[0] + 2 * chip[1] + c, pl.ds(start, count), :]
        src, mine = ins[0].at[pl.ds(r0, nr), :], slot((x, y))
        same = lambda ref, to: (ref, ref, to)
        first = [(src, mine, sibling), (src, mine, (*xn, c)), (src, mine, (*yn, c))]
        relay = [same(slot(xn, r0, na), (*yn, c)), same(slot(yn, r0 + na, nr - na), (*xn, c))]
        onward = [same(slot(xn), sibling), same(slot(yn), sibling), same(slot(dg), sibling)]
        return [first, relay, onward], [(src, mine)]

    if into is None:
        return _Exchange([v], [S((N_DEV, R, C), v.dtype)], (3, 2, 3), 1, plan)
    return _Exchange([v, into], [S((N_DEV, R, C), v.dtype)], (3, 2, 3), 1, plan, aliases={1: 0})


def scatter_first(gb):
    _, R, C = gb.shape

    def plan(ins, outs):
        x, y, c, chips = _place()
        owners = [(x, y)] + chips
        return [[(ins[0].at[4 * px + 2 * py + (1 - c)], outs[0].at[j], (x, y, 1 - c)) for j, (px, py) in enumerate(owners)]], []

    return _Exchange([gb], [S((4, R, C), BF16)], (4,), 0, plan)


def scatter_second(sb, rows=None, into=None):
    r0, nr = rows or (0, sb.shape[1])

    def plan(ins, outs):
        x, y, c, chips = _place()
        part = lambda ref, j: ref.at[j, pl.ds(r0, nr), :]
        return [[(part(ins[0], j), part(outs[0], j), (*chips[j], c)) for j in range(3)]], []

    if into is None:
        return _Exchange([sb], [S(sb.shape, BF16)], (3,), 0, plan)
    return _Exchange([sb, into], [S(sb.shape, BF16)], (3,), 0, plan, aliases={1: 0})


PARTS = "parts"


def _call(body, *, name, grid, in_specs, out_specs, out_shape, args, scratch=(), sem=None, riders=(), marks=None):
    single = not isinstance(out_shape, (tuple, list))
    out_shape = (out_shape,) if single else tuple(out_shape)
    out_specs = (out_specs,) if single else tuple(out_specs)
    n_in, n_out, n_sc = len(in_specs), len(out_shape), len(scratch)
    if riders is PARTS:
        return dict(body=body, grid=grid, in_specs=list(in_specs), out_specs=out_specs, out_shape=out_shape, args=tuple(args),
                    scratch=list(scratch), marks=marks)
    if not riders:
        res = pl.pallas_call(body, name=name, grid=grid, in_specs=list(in_specs), out_specs=out_specs, out_shape=out_shape,
                             scratch_shapes=list(scratch), compiler_params=_params(sem))(*args)
        return res[0] if single else res
    r_ins = [a for r in riders for a in r.ins]
    r_outs = [o for r in riders for o in r.outs]
    r_scr = [s for r in riders for s in r.scratch()]
    aliases, i0, o0 = {}, n_in, n_out
    for r in riders:
        for a, b in r.aliases.items():
            aliases[i0 + a] = o0 + b
        i0, o0 = i0 + len(r.ins), o0 + len(r.outs)
    steps = math.prod(grid)

    def full(*refs):
        ins, rin = refs[:n_in], refs[n_in:n_in + len(r_ins)]
        pos = n_in + len(r_ins)
        outs, rout = refs[pos:pos + n_out], refs[pos + n_out:pos + n_out + len(r_outs)]
        pos += n_out + len(r_outs)
        sc, rsc = refs[pos:pos + n_sc], refs[pos + n_sc:]
        step = 0
        for d, n in enumerate(grid):
            step = step * n + pl.program_id(d)

        def each(method, *lead):
            i, o = 0, 0
            for k, r in enumerate(riders):
                getattr(r, method)(*lead, rin[i:i + len(r.ins)], rout[o:o + len(r.outs)], rsc[3 * k:3 * k + 3])
                i, o = i + len(r.ins), o + len(r.outs)

        @pl.when(step == 0)
        def _():
            each("start")
        body(*ins, *outs, *sc)

        late = max(steps - 1 - max(steps // 8, 1), 0)
        first, second = marks or (min((3 * steps) // 5, late), late)

        @pl.when(step == first)
        def _():
            each("advance", 1)

        @pl.when(step == second)
        def _():
            each("advance", 2)

        @pl.when(step == steps - 1)
        def _():
            each("finish")

    anywhere = pl.BlockSpec(memory_space=pl.ANY)
    res = pl.pallas_call(
        full, name=name, grid=grid, in_specs=list(in_specs) + [anywhere] * len(r_ins),
        out_specs=out_specs + (anywhere,) * len(r_outs), out_shape=out_shape + tuple(r_outs),
        scratch_shapes=list(scratch) + r_scr, input_output_aliases=aliases,
        compiler_params=_params(("arbitrary",) * len(grid)))(*args, *r_ins)
    host, rest, per = res[:n_out], list(res[n_out:]), []
    for r in riders:
        per.append(rest[:len(r.outs)])
        rest = rest[len(r.outs):]
    return (host[0] if single else tuple(host)), per


def side_by_side(first, second, name, riders=()):
    a_in, a_out, a_sc = len(first["in_specs"]), len(first["out_shape"]), len(first["scratch"])
    n_in, n_out = a_in + len(second["in_specs"]), a_out + len(second["out_shape"])

    def body(*refs):
        ins, outs, sc = refs[:n_in], refs[n_in:n_in + n_out], refs[n_in + n_out:]
        first["body"](*ins[:a_in], *outs[:a_out], *sc[:a_sc])
        second["body"](*ins[a_in:], *outs[a_out:], *sc[a_sc:])

    return _call(body, name=name, grid=first["grid"], in_specs=first["in_specs"] + second["in_specs"],
                 out_specs=first["out_specs"] + second["out_specs"], out_shape=first["out_shape"] + second["out_shape"],
                 args=first["args"] + second["args"], scratch=first["scratch"] + second["scratch"],
                 sem=("arbitrary",) * len(first["grid"]), riders=riders, marks=first["marks"])


def in_one_call(parts, name, riders=()):
    extents = [p["grid"][-1] for p in parts]
    longest = max(extents)

    def clamp(spec, n):
        if n == longest or spec.index_map is None:
            return spec
        return pl.BlockSpec(spec.block_shape, lambda *ids, f=spec.index_map: f(*ids[:-1], jnp.minimum(ids[-1], n - 1)))

    counts = [(len(p["in_specs"]), len(p["out_shape"]), len(p["scratch"])) for p in parts]
    n_in, n_out = sum(c[0] for c in counts), sum(c[1] for c in counts)

    def body(*refs):
        ins, outs, sc = refs[:n_in], refs[n_in:n_in + n_out], refs[n_in + n_out:]
        i = o = s = 0
        for p, n, (ci, co, cs) in zip(parts, extents, counts):
            run_part = lambda p=p, a=ins[i:i + ci], b=outs[o:o + co], c=sc[s:s + cs]: p["body"](*a, *b, *c)
            if n == longest:
                run_part()
            else:
                pl.when(pl.program_id(len(p["grid"]) - 1) < n)(run_part)
            i, o, s = i + ci, o + co, s + cs

    cat = lambda key: [x for p in parts for x in p[key]]
    return _call(body, name=name, grid=parts[0]["grid"][:-1] + (longest,),
                 in_specs=[clamp(sp, n) for p, n in zip(parts, extents) for sp in p["in_specs"]],
                 out_specs=tuple(clamp(sp, n) for p, n in zip(parts, extents) for sp in p["out_specs"]),
                 out_shape=tuple(cat("out_shape")), args=tuple(cat("args")), scratch=cat("scratch"),
                 sem=("arbitrary",) * len(parts[0]["grid"]), riders=riders)


def _rows_tile(n, cap):
    return max(t for t in range(16, min(n, cap) + 1, 16) if n % t == 0)


def scatter_add(gs, ras, name):
    C = gs[0].shape[2]
    trs = [_rows_tile(g.shape[1], 176) for g in gs]
    nts = [g.shape[1] // tr for g, tr in zip(gs, trs)]
    steps = max(nts)
    x, y, c, chips = _place()
    slots = jnp.stack([4 * px + 2 * py + c for px, py in [(x, y)] + chips]).astype(jnp.int32)

    def body(s_ref, *refs):
        ins, outs = refs[:5 * len(gs)], refs[5 * len(gs):]
        for k in range(len(gs)):
            g0, g1, g2, g3, ra_ref = ins[5 * k:5 * k + 5]
            own_ref, sb_ref = outs[2 * k:2 * k + 2]

            def work(g0=g0, g1=g1, g2=g2, g3=g3, ra_ref=ra_ref, own_ref=own_ref, sb_ref=sb_ref):
                own_ref[...] = g0[...] + ra_ref[0].astype(F32)
                for j, gj in enumerate((g1, g2, g3)):
                    sb_ref[j] = (gj[...] + ra_ref[j + 1].astype(F32)).astype(BF16)

            if nts[k] == steps:
                work()
            else:
                pl.when(pl.program_id(0) < nts[k])(work)

    in_specs, out_specs, out_shape, args = [], [], [], [slots]
    for k, (g, ra, tr) in enumerate(zip(gs, ras, trs)):
        tile = lambda i, k=k: jnp.minimum(i, nts[k] - 1)
        in_specs += [pl.BlockSpec((None, tr, C), lambda i, s, j=j, tile=tile: (s[j], tile(i), 0)) for j in range(4)]
        in_specs.append(pl.BlockSpec((4, tr, C), lambda i, s, tile=tile: (0, tile(i), 0)))
        out_specs += [pl.BlockSpec((tr, C), lambda i, s, tile=tile: (tile(i), 0)),
                      pl.BlockSpec((3, tr, C), lambda i, s, tile=tile: (0, tile(i), 0))]
        out_shape += [S((g.shape[1], C), F32), S((3, g.shape[1], C), BF16)]
        args += [g, g, g, g, ra]
    spec = pltpu.PrefetchScalarGridSpec(num_scalar_prefetch=1, grid=(steps,), in_specs=in_specs, out_specs=tuple(out_specs))
    res = pl.pallas_call(body, name=name, grid_spec=spec, out_shape=tuple(out_shape), compiler_params=_params(("arbitrary",)))(*args)
    return [(res[2 * k], res[2 * k + 1]) for k in range(len(gs))]


def rms_cast(h, g, name, riders=()):
    T, D = h.shape
    tm = _tile(T, 512)

    def body(h_ref, g_ref, n_ref):
        n_ref[...] = _rms(h_ref[...], g_ref[...]).astype(BF16)

    row = pl.BlockSpec((tm, D), lambda i: (i, 0))
    g, g_spec = _gain(g)
    return _call(body, name=name, grid=(T // tm,), out_shape=S((T, D), BF16), in_specs=[row, g_spec],
                 out_specs=row, sem=("parallel",), args=(h, g), riders=riders)


def ffn_up_fwd(n, wgu, name, riders=()):
    T, D = n.shape
    F = wgu.shape[1]
    tr, tn = _tile(T, 512), _tile(F, 256)

    def body(n_ref, wg_ref, wu_ref, dgate_ref, dup_ref, a_ref):
        wg, wu = wg_ref[...], wu_ref[...]
        for r in range(T // tr):
            rows = slice(r * tr, (r + 1) * tr)
            x = n_ref[rows, :]
            gate = _nt(x, wg)
            up = _nt(x, wu)
            s = jax.nn.sigmoid(gate)
            silu = gate * s
            dgate_ref[rows, :] = (up * (s * (1.0 + gate * (1.0 - s)))).astype(BF16)
            dup_ref[rows, :] = silu.astype(BF16)
            a_ref[rows, :] = (silu * up).astype(BF16)

    tile = pl.BlockSpec((T, tn), lambda j: (0, j))
    return _call(
        body, name=name, grid=(F // tn,), out_shape=(S((T, F), BF16),) * 3,
        in_specs=[pl.BlockSpec((T, D), lambda j: (0, 0)),
                  pl.BlockSpec((None, tn, D), lambda j: (0, j, 0)), pl.BlockSpec((None, tn, D), lambda j: (1, j, 0))],
        out_specs=(tile, tile, tile), sem=("parallel",), args=(n, wgu, wgu), riders=riders)


def ffn_down_fwd(a, wd, h, g_next, name, riders=()):
    T, F = a.shape
    D = wd.shape[1]
    tm = _tile(T, 256)

    def body(a_ref, w_ref, h_ref, *rest):
        out = h_ref[...] + 0.5 * _nn(a_ref[...], w_ref[...])
        if g_next is None:
            rest[0][...] = out
        else:
            g_ref, o_ref, n_ref = rest
            o_ref[...] = out
            n_ref[...] = _rms(out, g_ref[...]).astype(BF16)

    row = pl.BlockSpec((tm, D), lambda i: (i, 0))
    more = g_next is not None
    g_arg, g_spec = _gain(g_next) if more else (None, None)
    return _call(
        body, name=name, grid=(T // tm,), out_shape=(S((T, D), F32), S((T, D), BF16)) if more else S((T, D), F32),
        in_specs=[pl.BlockSpec((tm, F), lambda i: (i, 0)), pl.BlockSpec((F, D), lambda i: (0, 0)), row] + ([g_spec] if more else []),
        out_specs=(row, row) if more else row,
        sem=("parallel",), args=(a, wd, h) + ((g_arg,) if more else ()), riders=riders)


def mix_in_fwd(h, g, win, name):
    T, D = h.shape
    N = win.shape[0]
    tm = _tile(T, 256)

    def body(h_ref, g_ref, w_ref, n_ref, p_ref):
        n = _rms(h_ref[...], g_ref[...]).astype(BF16)
        n_ref[...] = n
        p_ref[...] = _nt(n, w_ref[...]).astype(BF16)

    g, g_spec = _gain(g)
    return pl.pallas_call(
        body, name=name, grid=(T // tm,), out_shape=(S((T, D), BF16), S((T, N), BF16)),
        in_specs=[pl.BlockSpec((tm, D), lambda i: (i, 0)), g_spec, pl.BlockSpec((N, D), lambda i: (0, 0))],
        out_specs=(pl.BlockSpec((tm, D), lambda i: (i, 0)), pl.BlockSpec((tm, N), lambda i: (i, 0))),
        compiler_params=_params(("parallel",)),
    )(h, g, win)


def _tri_consts():
    r = lax.broadcasted_iota(jnp.int32, (QB, QB), 0)
    c = lax.broadcasted_iota(jnp.int32, (QB, QB), 1)
    ones = jnp.ones((QB, QB), BF16)
    with_sums = lambda tri: jnp.concatenate([tri.astype(BF16), ones], axis=1)
    return with_sums(r > c), with_sums(r <= c), with_sums(r < c)


def _half_masks():
    lane = lax.broadcasted_iota(jnp.int32, (QB, PAIR), 1)
    row = lax.broadcasted_iota(jnp.int32, (QB, PAIR), 0)
    return lane < HEAD_DIM, lane, row


def sb_attn_fwd(p, after, name, riders=()):
    T = p.shape[0]
    nq = T // QB

    def body(q_ref, k_ref, v_ref, m_ref, o_ref, tot_ref, q_sc, acc_ref, z_sc):
        i = pl.program_id(0)
        lo, lane, row = _half_masks()
        causal = lane < row
        heads, pairs = range(SB_HEADS), range(SB_HEADS // 2)
        for hp in pairs:
            q_sc[hp] = (q_ref[:, hp * PAIR:(hp + 1) * PAIR].astype(F32) * SCALE).astype(BF16)
        m2 = m_ref[...]

        def by_head(ref, j, hp):
            t = ref[pl.ds(pl.multiple_of(j * QB, QB), QB), hp * PAIR:(hp + 1) * PAIR]
            return jnp.concatenate([jnp.where(lo, t, 0), jnp.where(lo, 0, t)], axis=0)

        def scores(j):
            return [_nt(q_sc[hp], by_head(k_ref, j, hp)) for hp in pairs]

        def block(j, diag):
            z2 = [z_sc[hp] for hp in pairs]
            ahead = scores(jnp.maximum(j - 1, 0))
            for hp in pairs:
                z_sc[hp] = ahead[hp]
            vs = [by_head(v_ref, j, hp) for hp in pairs]
            spls = [_softplus_logsig(z2[h // 2][:, (h % 2) * QB:(h % 2 + 1) * QB]) for h in heads]
            sp = [jnp.where(causal, spls[h][0], 0.0) if diag else spls[h][0] for h in heads]
            rr = _tri(sp, m2)
            if diag:
                w = [jnp.where(causal, jnp.exp(spls[h][1] - rr[h][:, :QB]), 0.0).astype(BF16) for h in heads]
            else:
                c = [tot_ref[:, h * QB:(h + 1) * QB] for h in heads]
                w = [jnp.exp(spls[h][1] - (c[h] + rr[h][:, :QB])).astype(BF16) for h in heads]
            pv = [_nn(jnp.concatenate([w[2 * hp], w[2 * hp + 1]], axis=1), vs[hp]) for hp in pairs]
            for hp in pairs:
                acc_ref[hp] = pv[hp] if diag else acc_ref[hp] + pv[hp]
            for h in heads:
                tot_ref[:, h * QB:(h + 1) * QB] = rr[h][:, QB:] if diag else c[h] + rr[h][:, QB:]

        first = scores(i)
        for hp in pairs:
            z_sc[hp] = first[hp]
        block(i, True)

        def step(t, carry):
            block(i - 1 - t, False)
            return carry
        lax.fori_loop(0, i, step, 0)
        for hp in pairs:
            o_ref[:, hp * PAIR:(hp + 1) * PAIR] = acc_ref[hp]

    npair = SB_HEADS // 2
    return _call(
        body, name=name, grid=(nq,), out_shape=(S((T, SB_W), F32), S((T, SB_HEADS * QB), F32)),
        in_specs=[pl.BlockSpec((QB, SB_W), lambda i: (i, 0)), pl.BlockSpec((T, SB_W), lambda i: (0, 1)),
                  pl.BlockSpec((T, SB_W), lambda i: (0, 2)), pl.BlockSpec((QB, 2 * QB), lambda i: (0, 0))],
        out_specs=(pl.BlockSpec((QB, SB_W), lambda i: (i, 0)), pl.BlockSpec((QB, SB_HEADS * QB), lambda i: (i, 0))),
        scratch=[pltpu.VMEM((npair, QB, PAIR), BF16), pltpu.VMEM((npair, QB, PAIR), F32), pltpu.VMEM((npair, QB, 2 * QB), F32)],
        sem=("arbitrary",), args=(p, p, p, after), riders=riders,
        marks=((11 * nq) // 16, (14 * nq) // 16))


def sb_attn_bwd(p, do, tot, upto, before, name, riders=()):
    T = p.shape[0]
    nq = T // QB

    def body(q_ref, k_ref, v_ref, do_ref, tot_ref, mp_ref, mg_ref, dq_ref, dk_ref, dv_ref,
             q_sc, d_sc, qd_sc, pg_sc, dq_acc, dk_acc, dv_acc, zd_sc):
        i = pl.program_id(0)
        lo, lane, row = _half_masks()
        causal = lane < row
        heads, pairs = range(SB_HEADS), range(SB_HEADS // 2)

        def by_head(t):
            return jnp.concatenate([jnp.where(lo, t, 0), jnp.where(lo, 0, t)], axis=0)

        for hp in pairs:
            q2 = (q_ref[:, hp * PAIR:(hp + 1) * PAIR].astype(F32) * SCALE).astype(BF16)
            d2 = do_ref[:, hp * PAIR:(hp + 1) * PAIR].astype(BF16)
            q_sc[hp] = q2
            d_sc[hp] = d2
            qd_sc[hp] = by_head(q2)
            qd_sc[SB_HEADS // 2 + hp] = by_head(d2)
        mp, mg = mp_ref[...], mg_ref[...]

        @pl.when(i == 0)
        def _():
            dk_acc[...] = jnp.zeros_like(dk_acc)
            dv_acc[...] = jnp.zeros_like(dv_acc)
        pg_sc[...] = jnp.zeros_like(pg_sc)
        dq_acc[...] = jnp.zeros_like(dq_acc)

        def rows(ref, j, hp):
            return ref[pl.ds(pl.multiple_of(j * QB, QB), QB), hp * PAIR:(hp + 1) * PAIR]

        def products(j):
            return ([_nt(q_sc[hp], by_head(rows(k_ref, j, hp))) for hp in pairs]
                    + [_nt(d_sc[hp], by_head(rows(v_ref, j, hp))) for hp in pairs])

        def block(j, diag):
            r0 = pl.multiple_of(j * QB, QB)
            half = lambda t, h: t[:, (h % 2) * QB:(h % 2 + 1) * QB]
            z = [half(zd_sc[h // 2], h) for h in heads]
            dw = [half(zd_sc[SB_HEADS // 2 + h // 2], h) for h in heads]
            if not diag:
                ahead = products(j + 1)
                for hp in range(SB_HEADS):
                    zd_sc[hp] = ahead[hp]
            ks = [by_head(rows(k_ref, j, hp)) for hp in pairs]
            spls = [_softplus_logsig(z[h]) for h in heads]
            sp = [jnp.where(causal, spls[h][0], 0.0) if diag else spls[h][0] for h in heads]
            rr = _tri(sp, mp)
            pc = [pg_sc[2 * h] for h in heads]
            w = [jnp.exp(spls[h][1] - (tot_ref[:, h * QB:(h + 1) * QB] - (pc[h] + rr[h][:, :QB]))) for h in heads]
            if diag:
                w = [jnp.where(causal, w[h], 0.0) for h in heads]
            gg = [dw[h] * w[h] for h in heads]
            rg = _tri(gg, mg)
            gc = [pg_sc[2 * h + 1] for h in heads]
            dz = [gg[h] - (gg[h] + gc[h] + rg[h][:, :QB]) * jnp.exp(spls[h][1]) for h in heads]
            if diag:
                dz = [jnp.where(causal, dz[h], 0.0) for h in heads]
            dzb = [dz[h].astype(BF16) for h in heads]
            wb = [w[h].astype(BF16) for h in heads]
            both = lambda t, hp, axis: jnp.concatenate([t[2 * hp], t[2 * hp + 1]], axis=axis)
            dq = [_nn(both(dzb, hp, 1), ks[hp]) for hp in pairs]
            dk = [_tn(both(dzb, hp, 0), qd_sc[hp]) for hp in pairs]
            dv = [_tn(both(wb, hp, 0), qd_sc[SB_HEADS // 2 + hp]) for hp in pairs]
            for h in heads:
                if not diag:
                    pg_sc[2 * h] = pc[h] + rr[h][:, QB:]
                    pg_sc[2 * h + 1] = gc[h] + rg[h][:, QB:]
            for hp in pairs:
                dq_acc[hp] += dq[hp]
                dk_acc[pl.ds(r0, QB), hp * PAIR:(hp + 1) * PAIR] += dk[hp]
                dv_acc[pl.ds(r0, QB), hp * PAIR:(hp + 1) * PAIR] += dv[hp]

        first = products(0)
        for hp in range(SB_HEADS):
            zd_sc[hp] = first[hp]

        def step(t, carry):
            block(t, False)
            return carry
        lax.fori_loop(0, i, step, 0)
        block(i, True)
        for hp in pairs:
            dq_ref[:, hp * PAIR:(hp + 1) * PAIR] = (dq_acc[hp] * SCALE).astype(BF16)

        @pl.when(i == nq - 1)
        def _():
            dk_ref[...] = dk_acc[...].astype(BF16)
            dv_ref[...] = dv_acc[...].astype(BF16)

    qtile = pl.BlockSpec((QB, SB_W), lambda i: (i, 0))
    whole = pl.BlockSpec((T, SB_W), lambda i: (0, 0))
    const = pl.BlockSpec((QB, 2 * QB), lambda i: (0, 0))
    return _call(
        body, name=name, grid=(nq,), out_shape=(S((T, SB_W), BF16),) * 3,
        in_specs=[qtile, pl.BlockSpec((T, SB_W), lambda i: (0, 1)), pl.BlockSpec((T, SB_W), lambda i: (0, 2)), qtile,
                  pl.BlockSpec((QB, SB_HEADS * QB), lambda i: (i, 0)), const, const],
        out_specs=(qtile, whole, whole),
        scratch=[pltpu.VMEM((SB_HEADS // 2, QB, PAIR), BF16), pltpu.VMEM((SB_HEADS // 2, QB, PAIR), BF16),
                 pltpu.VMEM((SB_HEADS, 2 * QB, PAIR), BF16),
                 pltpu.VMEM((2 * SB_HEADS, QB, QB), F32), pltpu.VMEM((SB_HEADS // 2, QB, PAIR), F32),
                 pltpu.VMEM((T, SB_W), F32), pltpu.VMEM((T, SB_W), F32), pltpu.VMEM((SB_HEADS, QB, 2 * QB), F32)],
        sem=("arbitrary",), args=(p, p, p, do, tot, upto, before), riders=riders)


def _t5_buckets():
    a = lax.broadcasted_iota(jnp.int32, (QB, QB), 0)
    c = lax.broadcasted_iota(jnp.int32, (QB, QB), 1)

    def bucket(dist):
        dist = jnp.maximum(dist, 0)
        max_exact = N_BUCKETS // 2
        d = jnp.maximum(dist, 1).astype(F32)
        large = max_exact + (jnp.log(d / max_exact) / math.log(MAX_DISTANCE / max_exact)
                             * (N_BUCKETS - max_exact)).astype(jnp.int32)
        large = jnp.minimum(large, N_BUCKETS - 1)
        return jnp.where(dist < max_exact, dist, large)

    return bucket(QB + a - c), bucket(a - c)


def _swa_common(i, kp_ref, kc_ref, vp_ref, vc_ref, bp_ref, bc_ref, rb_ref, bias_ref):
    lo, lane, row = _half_masks()

    @pl.when(i == 0)
    def _():
        for blk, b_ref in enumerate((bp_ref, bc_ref)):
            bk = b_ref[...]
            for h in range(8):
                acc = jnp.zeros((QB, QB), F32)
                for b in range(N_BUCKETS):
                    acc = jnp.where(bk == b, rb_ref[b, h], acc)
                bias_ref[h, blk] = acc

    band = [(lane > row) & (i > 0), lane <= row]

    def stacks(ref):
        t = ref[...].astype(F32)
        sw = pltpu.roll(t, HEAD_DIM, 1)
        return [jnp.concatenate([jnp.where(lo, t, 0.0), jnp.where(lo, 0.0, sw)], axis=0).astype(BF16),
                jnp.concatenate([jnp.where(lo, sw, 0.0), jnp.where(lo, 0.0, t)], axis=0).astype(BF16)]

    ks = [stacks(kp_ref), stacks(kc_ref)]
    vs = [stacks(vp_ref), stacks(vc_ref)]
    return lo, band, ks, vs


def _lane_half(t, h):
    return t[:, (h % 2) * QB:(h % 2 + 1) * QB]


def swa_fwd(p, sinks, rel_bias, bprev, bcur, name, riders=()):
    T = p.shape[0]
    nq = T // QB
    kcol, vcol = (3 * SB_W + SWA_W) // KV_W, (3 * SB_W + SWA_W) // KV_W + 1
    sinks, srow = sinks if isinstance(sinks, tuple) else (sinks, 0)

    def body(q_ref, kp_ref, kc_ref, vp_ref, vc_ref, bp_ref, bc_ref, sink_ref, rb_ref, o_ref, lse_ref, bias_ref):
        i = pl.program_id(0)
        lo, band, ks, vs = _swa_common(i, kp_ref, kc_ref, vp_ref, vc_ref, bp_ref, bc_ref, rb_ref, bias_ref)
        heads, pairs, blocks = range(8), range(4), range(2)
        rowmax = lambda t: jnp.max(t, axis=1, keepdims=True)
        rowsum = lambda t: jnp.sum(t, axis=1, keepdims=True)
        q2 = [q_ref[:, g * PAIR:(g + 1) * PAIR] for g in pairs]
        s2 = [[_nt(q2[g], ks[b][g // 2]) for b in blocks] for g in pairs]
        sc = [[jnp.where(band[b], _lane_half(s2[h // 2][b], h) * SCALE + bias_ref[h, b], NEG_INF) for b in blocks] for h in heads]
        sink = [sink_ref[srow, h] for h in heads]
        m = [jnp.maximum(jnp.maximum(rowmax(sc[h][0]), rowmax(sc[h][1])), sink[h]) for h in heads]
        e = [[jnp.exp(sc[h][b] - m[h]) for b in blocks] for h in heads]
        den = [rowsum(e[h][0]) + rowsum(e[h][1]) + jnp.exp(sink[h] - m[h]) for h in heads]
        pb = [[(e[h][b] / den[h]).astype(BF16) for b in blocks] for h in heads]
        for g in pairs:
            both = lambda b: jnp.concatenate([pb[2 * g][b], pb[2 * g + 1][b]], axis=1)
            o_ref[:, g * PAIR:(g + 1) * PAIR] = _nn(both(0), vs[0][g // 2]) + _nn(both(1), vs[1][g // 2])
        for h in heads:
            lse_ref[:, h * QB:(h + 1) * QB] = jnp.broadcast_to(m[h] + jnp.log(den[h]), (QB, QB))

    kv = lambda col, prev: pl.BlockSpec((QB, KV_W), (lambda i: (jnp.maximum(i - 1, 0), col)) if prev else (lambda i: (i, col)))
    full = pl.BlockSpec((QB, QB), lambda i: (0, 0))
    smem = pl.BlockSpec(memory_space=pltpu.SMEM)
    return _call(
        body, name=name, grid=(nq,), out_shape=(S((T, SWA_W), F32), S((T, 8 * QB), F32)),
        in_specs=[pl.BlockSpec((QB, SWA_W), lambda i: (i, 3)), kv(kcol, True), kv(kcol, False), kv(vcol, True), kv(vcol, False),
                  full, full, smem, smem],
        out_specs=(pl.BlockSpec((QB, SWA_W), lambda i: (i, 0)), pl.BlockSpec((QB, 8 * QB), lambda i: (i, 0))),
        scratch=[pltpu.VMEM((8, 2, QB, QB), F32)],
        sem=("arbitrary",), args=(p, p, p, p, p, bprev, bcur, sinks, rel_bias), riders=riders)


def swa_bwd(p, do, lse, sinks, rel_bias, bprev, bcur, name, riders=()):
    T = p.shape[0]
    nq = T // QB
    kcol, vcol = (3 * SB_W + SWA_W) // KV_W, (3 * SB_W + SWA_W) // KV_W + 1
    sinks, srow = sinks if isinstance(sinks, tuple) else (sinks, 0)

    def body(q_ref, kp_ref, kc_ref, vp_ref, vc_ref, do_ref, lse_ref, bp_ref, bc_ref, sink_ref, rb_ref,
             dq_ref, dk_ref, dv_ref, dsink_ref, dsc_ref, bias_ref, dk_acc, dv_acc):
        i = pl.program_id(0)
        lo, band, ks, vs = _swa_common(i, kp_ref, kc_ref, vp_ref, vc_ref, bp_ref, bc_ref, rb_ref, bias_ref)

        @pl.when(i == 0)
        def _():
            dk_acc[...] = jnp.zeros_like(dk_acc)
            dv_acc[...] = jnp.zeros_like(dv_acc)
            dsc_ref[...] = jnp.zeros_like(dsc_ref)
            dsink_ref[...] = jnp.zeros_like(dsink_ref)

        heads, pairs, blocks = range(8), range(4), range(2)
        rowsum = lambda t: jnp.sum(t, axis=1, keepdims=True)
        by_head = lambda t: jnp.concatenate([jnp.where(lo, t, 0), jnp.where(lo, 0, t)], axis=0)
        q2 = [q_ref[:, g * PAIR:(g + 1) * PAIR] for g in pairs]
        d2 = [do_ref[:, g * PAIR:(g + 1) * PAIR].astype(BF16) for g in pairs]
        qs = [by_head(q2[g]) for g in pairs]
        dos = [by_head(d2[g]) for g in pairs]
        s2 = [[_nt(q2[g], ks[b][g // 2]) for b in blocks] for g in pairs]
        dp2 = [[_nt(d2[g], vs[b][g // 2]) for b in blocks] for g in pairs]
        lse_h = [lse_ref[:, h * QB:(h + 1) * QB] for h in heads]
        sink = [sink_ref[srow, h] for h in heads]
        pr = [[jnp.exp(jnp.where(band[b], _lane_half(s2[h // 2][b], h) * SCALE + bias_ref[h, b], NEG_INF) - lse_h[h])
               for b in blocks] for h in heads]
        dp = [[_lane_half(dp2[h // 2][b], h) for b in blocks] for h in heads]
        delta = [rowsum(pr[h][0] * dp[h][0]) + rowsum(pr[h][1] * dp[h][1]) for h in heads]
        lane1 = lax.broadcasted_iota(jnp.int32, (1, QB), 1)
        dsink = jnp.zeros((1, QB), F32)
        for h in heads:
            dsink = dsink + jnp.where(lane1 == h, -jnp.sum(jnp.exp(sink[h] - lse_h[h][:, :1]) * delta[h]), 0.0)
        dsink_ref[...] += dsink
        dsc = [[pr[h][b] * (dp[h][b] - delta[h]) for b in blocks] for h in heads]
        for h in heads:
            for b in blocks:
                dsc_ref[h, b] += dsc[h][b]
        dzb = [[(dsc[h][b] * SCALE).astype(BF16) for b in blocks] for h in heads]
        prb = [[pr[h][b].astype(BF16) for b in blocks] for h in heads]
        pair_of = lambda t, g, b, axis: jnp.concatenate([t[2 * g][b], t[2 * g + 1][b]], axis=axis)
        for g in pairs:
            dq = _nn(pair_of(dzb, g, 0, 1), ks[0][g // 2]) + _nn(pair_of(dzb, g, 1, 1), ks[1][g // 2])
            dq_ref[:, g * PAIR:(g + 1) * PAIR] = dq.astype(BF16)

        def key_grad(t, other, b):
            per_kv = [_tn(pair_of(t, 2 * kh, b, 0), other[2 * kh]) + _tn(pair_of(t, 2 * kh + 1, b, 0), other[2 * kh + 1]) for kh in range(2)]
            both = [s + pltpu.roll(s, HEAD_DIM, 1) for s in per_kv]
            return jnp.where(lo, both[0], both[1])

        rp = pl.multiple_of(jnp.maximum(i - 1, 0) * QB, QB)
        rc = pl.multiple_of(i * QB, QB)
        dk_acc[pl.ds(rp, QB), :] += key_grad(dzb, qs, 0)
        dv_acc[pl.ds(rp, QB), :] += key_grad(prb, dos, 0)
        dk_acc[pl.ds(rc, QB), :] += key_grad(dzb, qs, 1)
        dv_acc[pl.ds(rc, QB), :] += key_grad(prb, dos, 1)

        @pl.when(i == nq - 1)
        def _():
            dk_ref[...] = dk_acc[...].astype(BF16)
            dv_ref[...] = dv_acc[...].astype(BF16)

    kv = lambda col, prev: pl.BlockSpec((QB, KV_W), (lambda i: (jnp.maximum(i - 1, 0), col)) if prev else (lambda i: (i, col)))
    full = pl.BlockSpec((QB, QB), lambda i: (0, 0))
    smem = pl.BlockSpec(memory_space=pltpu.SMEM)
    whole = lambda shape: pl.BlockSpec(shape, lambda i: (0,) * len(shape))
    return _call(
        body, name=name, grid=(nq,),
        out_shape=(S((T, SWA_W), BF16), S((T, KV_W), BF16), S((T, KV_W), BF16), S((1, QB), F32), S((8, 2, QB, QB), F32)),
        in_specs=[pl.BlockSpec((QB, SWA_W), lambda i: (i, 3)), kv(kcol, True), kv(kcol, False), kv(vcol, True), kv(vcol, False),
                  pl.BlockSpec((QB, SWA_W), lambda i: (i, 0)), pl.BlockSpec((QB, 8 * QB), lambda i: (i, 0)),
                  full, full, smem, smem],
        out_specs=(pl.BlockSpec((QB, SWA_W), lambda i: (i, 0)), whole((T, KV_W)), whole((T, KV_W)), whole((1, QB)),
                   whole((8, 2, QB, QB))),
        scratch=[pltpu.VMEM((8, 2, QB, QB), F32), pltpu.VMEM((T, KV_W), F32), pltpu.VMEM((T, KV_W), F32)],
        sem=("arbitrary",), args=(p, p, p, p, p, do, lse, bprev, bcur, sinks, rel_bias), riders=riders)


def mix_out_fwd(o_sb, o_sw, g_sb, g_sw, wout, h, g_next, name, riders=()):
    T, D = h.shape
    M = SB_W + SWA_W
    tm = _tile(T, 256)

    def body(a_ref, b_ref, ga_ref, gb_ref, w_ref, h_ref, gn_ref, mx_ref, o_ref, n_ref):
        mx_ref[:, :SB_W] = _rms(a_ref[...], ga_ref[...]).astype(BF16)
        mx_ref[:, SB_W:] = _rms(b_ref[...], gb_ref[...]).astype(BF16)
        out = h_ref[...] + _nn(mx_ref[...], w_ref[...])
        o_ref[...] = out
        n_ref[...] = _rms(out, gn_ref[...]).astype(BF16)

    row = lambda n: pl.BlockSpec((tm, n), lambda i: (i, 0))
    (g_sb, sb_spec), (g_sw, sw_spec), (g_next, next_spec) = _gain(g_sb), _gain(g_sw), _gain(g_next)
    return _call(
        body, name=name, grid=(T // tm,), out_shape=(S((T, M), BF16), S((T, D), F32), S((T, D), BF16)),
        in_specs=[row(SB_W), row(SWA_W), sb_spec, sw_spec, pl.BlockSpec((M, D), lambda i: (0, 0)), row(D), next_spec],
        out_specs=(row(M), row(D), row(D)),
        sem=("parallel",), args=(o_sb, o_sw, g_sb, g_sw, wout, h, g_next), riders=riders)


def loss_head(h, g, target, name):
    T, D = h.shape
    tm = _tile(T, 256)

    def body(h_ref, g_ref, t_ref, loss_ref, dh_ref, dhb_ref, dg_ref):
        @pl.when(pl.program_id(0) == 0)
        def _():
            loss_ref[...] = jnp.zeros_like(loss_ref)
            dg_ref[...] = jnp.zeros_like(dg_ref)
        x = h_ref[...]
        err = _rms(x, g_ref[...]) - t_ref[...]
        loss_ref[...] += jnp.full((1, QB), 0.5 * jnp.sum(jnp.mean(err * err, axis=-1)), F32)
        dx, dg = _rms_bwd(err / D, x, g_ref[...])
        dh_ref[...] = dx
        dhb_ref[...] = dx.astype(BF16)
        dg_ref[...] += dg

    row = pl.BlockSpec((tm, D), lambda i: (i, 0))
    vec = pl.BlockSpec((1, D), lambda i: (0, 0))
    return pl.pallas_call(
        body, name=name, grid=(T // tm,), out_shape=(S((1, QB), F32), S((T, D), F32), S((T, D), BF16), S((1, D), F32)),
        in_specs=[row, vec, row], out_specs=(pl.BlockSpec((1, QB), lambda i: (0, 0)), row, row, vec),
        compiler_params=_params(("arbitrary",)),
    )(h, g, target)


def ffn_down_bwd(dhb, wd, gate, up, a, n, name, riders=()):
    T, D = dhb.shape
    F = wd.shape[0]
    tr, tn = _tile(T, 512), _tile(F, 256)

    def body(d_ref, n_ref, w_ref, g_ref, u_ref, a_ref, o_ref, dwd_ref, dwdb_ref, dwgu_ref, dwgub_ref):
        w = w_ref[...]
        for r in range(T // tr):
            rows = slice(r * tr, (r + 1) * tr)
            da = 0.5 * _nt(d_ref[rows, :], w)
            o_ref[0, rows, :] = (da * g_ref[rows, :].astype(F32)).astype(BF16)
            o_ref[1, rows, :] = (da * u_ref[rows, :].astype(F32)).astype(BF16)
        dwd = 0.5 * _tn(a_ref[...], d_ref[...])
        dwd_ref[...] = dwd
        dwdb_ref[...] = dwd.astype(BF16)
        for s in range(2):
            dwgu = _tn(o_ref[s], n_ref[...])
            dwgu_ref[s] = dwgu
            dwgub_ref[s] = dwgu.astype(BF16)

    tile = pl.BlockSpec((T, tn), lambda j: (0, j))
    whole = pl.BlockSpec((T, D), lambda j: (0, 0))
    rows1, rows2 = pl.BlockSpec((tn, D), lambda j: (j, 0)), pl.BlockSpec((2, tn, D), lambda j: (0, j, 0))
    return _call(
        body, name=name, grid=(F // tn,),
        out_shape=(S((2, T, F), BF16), S((F, D), F32), S((F, D), BF16), S((2, F, D), F32), S((2, F, D), BF16)),
        in_specs=[whole, whole, rows1, tile, tile, tile],
        out_specs=(pl.BlockSpec((2, T, tn), lambda j: (0, 0, j)), rows1, rows1, rows2, rows2),
        sem=("parallel",), args=(dhb, n, wd, gate, up, a), riders=riders)


def tn_matmul(xs, y, alpha, name, riders=()):
    B, T, N = xs.shape
    D = y.shape[1]
    tn = _tile(N, 256)

    def body(x_ref, y_ref, o_ref, ob_ref):
        o = alpha * _tn(x_ref[...], y_ref[...])
        o_ref[...] = o
        ob_ref[...] = o.astype(BF16)

    tile = pl.BlockSpec((None, tn, D), lambda s, j: (s, j, 0))
    return _call(
        body, name=name, grid=(B, N // tn), out_shape=(S((B, N, D), F32), S((B, N, D), BF16)),
        in_specs=[pl.BlockSpec((None, T, tn), lambda s, j: (s, 0, j)), pl.BlockSpec((T, D), lambda s, j: (0, 0))],
        out_specs=(tile, tile), sem=("parallel", "parallel"), args=(xs, y), riders=riders)


def nn_rms_bwd(xs, ws, h_in, g, dh, name, riders=()):
    B, T, K = xs.shape
    D = ws.shape[2]
    tm = _tile(T, 256)

    def body(x_ref, w_ref, h_ref, g_ref, d_ref, o_ref, ob_ref, dg_ref):
        @pl.when(pl.program_id(0) == 0)
        def _():
            dg_ref[...] = jnp.zeros_like(dg_ref)
        dn = _nn(x_ref[0], w_ref[0])
        for s in range(1, B):
            dn = dn + _nn(x_ref[s], w_ref[s])
        dx, dg = _rms_bwd(dn, h_ref[...], g_ref[...])
        out = d_ref[...] + dx
        o_ref[...] = out
        ob_ref[...] = out.astype(BF16)
        dg_ref[...] += dg

    row = pl.BlockSpec((tm, D), lambda i: (i, 0))
    vec = pl.BlockSpec((1, D), lambda i: (0, 0))
    g, g_spec = _gain(g)
    return _call(
        body, name=name, grid=(T // tm,), out_shape=(S((T, D), F32), S((T, D), BF16), S((1, D), F32)),
        in_specs=[pl.BlockSpec((B, tm, K), lambda i: (0, i, 0)), pl.BlockSpec((B, K, D), lambda i: (0, 0, 0)), row, g_spec, row],
        out_specs=(row, row, vec),
        sem=("arbitrary",), args=(xs, ws, h_in, g, dh), riders=riders)


def mix_out_bwd(dhb, wout, mixed, o_sb, o_sw, g_sb, g_sw, name):
    T, D = dhb.shape
    M = SB_W + SWA_W
    tm = _tile(T, 256)
    steps = T // tm

    def body(d_ref, w_ref, mx_ref, a_ref, b_ref, ga_ref, gb_ref, da_ref, db_ref, dga_ref, dgb_ref, dw_ref, dwb_ref):
        i = pl.program_id(0)

        @pl.when(i == 0)
        def _():
            dga_ref[...] = jnp.zeros_like(dga_ref)
            dgb_ref[...] = jnp.zeros_like(dgb_ref)
            dw_ref[...] = jnp.zeros_like(dw_ref)
        dm = _nt(d_ref[...], w_ref[...])
        dxa, dga = _rms_bwd(dm[:, :SB_W], a_ref[...], ga_ref[...])
        dxb, dgb = _rms_bwd(dm[:, SB_W:], b_ref[...], gb_ref[...])
        da_ref[...] = dxa
        db_ref[...] = dxb
        dga_ref[...] += dga
        dgb_ref[...] += dgb
        dw_ref[...] += _tn(mx_ref[...], d_ref[...])

        @pl.when(i == steps - 1)
        def _():
            dwb_ref[...] = dw_ref[...].astype(BF16)

    row = lambda n: pl.BlockSpec((tm, n), lambda i: (i, 0))
    vec = lambda n: pl.BlockSpec((1, n), lambda i: (0, 0))
    whole = pl.BlockSpec((M, D), lambda i: (0, 0))
    (g_sb, sb_spec), (g_sw, sw_spec) = _gain(g_sb), _gain(g_sw)
    return pl.pallas_call(
        body, name=name, grid=(steps,),
        out_shape=(S((T, SB_W), F32), S((T, SWA_W), F32), S((1, SB_W), F32), S((1, SWA_W), F32), S((M, D), F32), S((M, D), BF16)),
        in_specs=[row(D), whole, row(M), row(SB_W), row(SWA_W), sb_spec, sw_spec],
        out_specs=(row(SB_W), row(SWA_W), vec(SB_W), vec(SWA_W), whole, whole),
        compiler_params=_params(("arbitrary",)),
    )(dhb, wout, mixed, o_sb, o_sw, g_sb, g_sw)


def rel_bias_grad(dscs, bprev, bcur, name):
    n = len(dscs)

    def body(*refs):
        bp_ref, bc_ref, o_ref = refs[n], refs[n + 1], refs[n + 2]
        bks = [bp_ref[...], bc_ref[...]]
        row = lax.broadcasted_iota(jnp.int32, (N_BUCKETS, QB), 0)
        lane = lax.broadcasted_iota(jnp.int32, (N_BUCKETS, QB), 1)
        out = jnp.zeros((N_BUCKETS, QB), F32)
        for h in range(8):
            tot = [sum(refs[l][h, b] for l in range(n)) for b in range(2)]
            for b in range(N_BUCKETS):
                val = jnp.sum(jnp.where(bks[0] == b, tot[0], 0.0)) + jnp.sum(jnp.where(bks[1] == b, tot[1], 0.0))
                out = jnp.where((row == b) & (lane == h), val, out)
        o_ref[...] = out

    return pl.pallas_call(body, name=name, out_shape=S((N_BUCKETS, QB), F32), compiler_params=_params())(*dscs, bprev, bcur)


def _adamw(w, g, m, v):
    m = ADAM_B1 * m + (1.0 - ADAM_B1) * g
    v = ADAM_B2 * v + (1.0 - ADAM_B2) * (g * g)
    m_hat = m / (1.0 - ADAM_B1 ** ADAM_STEP)
    v_hat = v / (1.0 - ADAM_B2 ** ADAM_STEP)
    delta = -ADAM_LR * (m_hat / (jnp.sqrt(v_hat) + ADAM_EPS) + ADAM_WD * w)
    return delta, m, v


def adamw_scattered(w, m, v, owns, others, name, riders=(), rows=176):
    L, R, C = w.shape
    tr = _rows_tile(R, rows)

    def body(w_ref, m_ref, v_ref, *rest):
        own_refs, other_refs = rest[:L], rest[L:2 * L]
        g_ref, d_ref, mo_ref, vo_ref = rest[2 * L:]
        layer = pl.program_id(0)

        def grad(k):
            o = other_refs[k]
            return own_refs[k][...] + o[0].astype(F32) + o[1].astype(F32) + o[2].astype(F32)

        g = grad(0)
        for k in range(1, L):
            g = jnp.where(layer == k, grad(k), g)
        d, mn, vn = _adamw(w_ref[...], g, m_ref[...], v_ref[...])
        g_ref[...] = g
        d_ref[...] = d
        mo_ref[...] = mn
        vo_ref[...] = vn

    tile = pl.BlockSpec((None, tr, C), lambda l, i: (l, i, 0))
    return _call(
        body, name=name, grid=(L, R // tr), out_shape=(S((L, R, C), F32),) * 4,
        in_specs=[tile] * 3 + [pl.BlockSpec((tr, C), lambda l, i: (i, 0))] * L + [pl.BlockSpec((3, tr, C), lambda l, i: (0, i, 0))] * L,
        out_specs=(tile,) * 4, sem=("parallel", "parallel"), args=(w, m, v, *owns, *others), riders=riders)


def adamw_small(w, gs, m, v, name):
    R, C = w.shape

    def body(w_ref, g_ref, m_ref, v_ref, go_ref, d_ref, mo_ref, vo_ref):
        g = g_ref[0]
        for k in range(1, N_DEV):
            g = g + g_ref[k]
        d, mn, vn = _adamw(w_ref[...], g, m_ref[...], v_ref[...])
        go_ref[...] = g
        d_ref[...] = d
        mo_ref[...] = mn
        vo_ref[...] = vn

    return pl.pallas_call(body, name=name, out_shape=(S((R, C), F32),) * 4, compiler_params=_params())(w, gs, m, v)


def kernel(x, norm_ffn1, w_ffn1_gu, w_ffn1_down, norm_mix, w_in, sinks, norm_out_sb, norm_out_swa, w_out, norm_ffn2, w_ffn2_gu, w_ffn2_down, rel_bias, norm_final, loss_target, m_norm_ffn1, m_w_ffn1_gu, m_w_ffn1_down, m_norm_mix, m_w_in, m_sinks, m_norm_out_sb, m_norm_out_swa, m_w_out, m_norm_ffn2, m_w_ffn2_gu, m_w_ffn2_down, m_rel_bias, m_norm_final, v_norm_ffn1, v_w_ffn1_gu, v_w_ffn1_down, v_norm_mix, v_w_in, v_sinks, v_norm_out_sb, v_norm_out_swa, v_w_out, v_norm_ffn2, v_w_ffn2_gu, v_w_ffn2_down, v_rel_bias, v_norm_final):
    L = norm_ffn1.shape[0]
    T, D = x.shape[1], x.shape[2]
    F = w_ffn1_down.shape[1] * N_DEV
    h = x.reshape(T, D)
    target = loss_target.reshape(T, D)
    after, upto, before = _tri_consts()
    bprev, bcur = _t5_buckets()

    local = {}
    for l in range(L):
        local[f"gu1_{l}"] = w_ffn1_gu[l].T.astype(BF16)
        local[f"d1_{l}"] = w_ffn1_down[l].astype(BF16)
        local[f"in_{l}"] = w_in[l].T.astype(BF16)
        local[f"out_{l}"] = w_out[l].astype(BF16)
        local[f"gu2_{l}"] = w_ffn2_gu[l].T.astype(BF16)
        local[f"d2_{l}"] = w_ffn2_down[l].astype(BF16)
    full, partial = {}, {}
    grads, chip_sum, recv_b = {}, {}, {}

    def run(fn, *args, ag=(), rs1=(), rs2=()):
        halves = lambda names: [n if isinstance(n, tuple) else (n, None) for n in names]
        ag, rs2 = [(n, k) for n, k in halves(ag) if n in local], halves(rs2)
        rows = lambda k, total: None if k is None else (k * (total // 2), total // 2)

        def second(n, k):
            sb = chip_sum[n][1]
            return scatter_second(sb, rows(k, sb.shape[1]), recv_b.get(n))

        riders = ([gather(local[n], rows(k, local[n].shape[0]), partial.get(n)) for n, k in ag]
                  + [scatter_first(grads[n][1]) for n in rs1] + [second(n, k) for n, k in rs2])
        if not riders:
            return fn(*args)
        outs, per = fn(*args, riders=riders)
        per = [p[0] for p in per]
        for n, k in ag:
            buf = per.pop(0)
            if k == 0:
                partial[n] = buf
            else:
                full[n] = buf.reshape(N_DEV * buf.shape[1], D)
        if rs1:
            sums = scatter_add([grads[n][0] for n in rs1], [per.pop(0) for n in rs1], "rs_add_" + "_".join(rs1))
            chip_sum.update(zip(rs1, sums))
        for n, _ in rs2:
            recv_b[n] = per.pop(0)
        return outs

    def attn_fwd(p, sink, name, riders=()):
        return side_by_side(sb_attn_fwd(p, after, name, riders=PARTS), swa_fwd(p, sink, rel_bias, bprev, bcur, name, riders=PARTS),
                            name, riders)

    def attn_bwd(p, do_sb, tot, do_sw, lse, sink, name, riders=()):
        return side_by_side(sb_attn_bwd(p, do_sb, tot, upto, before, name, riders=PARTS),
                            swa_bwd(p, do_sw, lse, sink, rel_bias, bprev, bcur, name, riders=PARTS), name, riders)

    gu = lambda n: full[n].reshape(2, F, D)
    slots = lambda pair: tuple(t.reshape(N_DEV, -1, D) for t in pair)
    vec = lambda a: a.reshape(1, -1)

    PW = max(D, SB_W + SWA_W)
    n_rows = 4 * L + 2
    n_rows += (-n_rows) % 8

    def pack(ffn1, mix, ffn2, final, osb, osw, snk, rel, extra):
        pieces = []

        def row(*parts):
            flat = [a.reshape(-1) for a in parts]
            pieces.extend(flat)
            used = sum(a.size for a in flat)
            if used < PW:
                pieces.append(jnp.zeros((PW - used,), F32))

        for group in (ffn1, mix, ffn2):
            for l in range(L):
                row(group[l])
        row(final)
        for l in range(L):
            row(osb[l], osw[l])
        row(*[snk[l].reshape(-1)[:8] for l in range(L)], rel, extra)
        pieces.append(jnp.zeros(((n_rows - 4 * L - 2) * PW,), F32))
        return jnp.concatenate(pieces).reshape(n_rows, PW)

    def unpack(arr):
        ffn1, mix, ffn2 = arr[0:L, :D], arr[L:2 * L, :D], arr[2 * L:3 * L, :D]
        final = arr[3 * L, :D]
        ob = arr[3 * L + 1:4 * L + 1]
        tail = arr[4 * L + 1]
        return (ffn1, mix, tail[:8 * L].reshape(L, 8), ob[:, :SB_W], ob[:, SB_W:SB_W + SWA_W], ffn2,
                tail[8 * L:8 * L + N_BUCKETS * 8].reshape(N_BUCKETS, 8), final)

    zero = jnp.zeros((1,), F32)
    w_small = pack(norm_ffn1, norm_mix, norm_ffn2, norm_final, norm_out_sb, norm_out_swa, sinks, rel_bias, zero)
    g_ffn1, g_mix, g_ffn2, g_osb, g_osw = [a.reshape(L, 1, -1) for a in (norm_ffn1, norm_mix, norm_ffn2, norm_out_sb, norm_out_swa)]

    saved = []
    n_next = run(rms_cast, h, (g_ffn1, 0), "rms_first", ag=("gu1_0",))
    for l in range(L):
        nx = l + 1
        s = {"h0": h, "n1": n_next}
        s["gate1"], s["up1"], s["a1"] = run(ffn_up_fwd, s["n1"], gu(f"gu1_{l}"), f"ffn1_up{l}",
                                            ag=(f"d1_{l}", ("in_0", 0) if l == 0 else (f"in_{l}", 1)))
        h = run(ffn_down_fwd, s["a1"], full[f"d1_{l}"], h, None, f"ffn1_down{l}", ag=(("in_0", 1),) if l == 0 else ())
        s["h1"] = h
        s["n2"], s["p"] = mix_in_fwd(h, (g_mix, l), full[f"in_{l}"], f"mix_in{l}")
        s["o_sb"], s["tot"], s["o_sw"], s["lse"] = run(attn_fwd, s["p"], (sinks, l), f"attn_fwd{l}",
                                                       ag=(f"out_{l}", f"gu2_{l}", f"d2_{l}", (f"gu1_{nx}", 0)))
        s["mixed"], h, s["n3"] = run(mix_out_fwd, s["o_sb"], s["o_sw"], (g_osb, l), (g_osw, l),
                                     full[f"out_{l}"], h, (g_ffn2, l), f"mix_out{l}")
        s["h2"] = h
        s["gate2"], s["up2"], s["a2"] = run(ffn_up_fwd, s["n3"], gu(f"gu2_{l}"), f"ffn2_up{l}",
                                            ag=((f"gu1_{nx}", 1), (f"in_{nx}", 0)))
        if nx < L:
            h, n_next = run(ffn_down_fwd, s["a2"], full[f"d2_{l}"], h, (g_ffn1, nx), f"ffn2_down{l}")
        else:
            h = run(ffn_down_fwd, s["a2"], full[f"d2_{l}"], h, None, f"ffn2_down{l}")
        saved.append(s)

    loss_part, dh, dhb, dg_final = loss_head(h, vec(norm_final), target, "loss_head")

    small = {k: [None] * L for k in ("ffn1", "mix", "sinks", "osb", "osw", "ffn2", "dsc")}
    for l in reversed(range(L)):
        s = saved[l]

        def ffn_bwd(dh, dhb, tag, gate, up, a, n, h_in, g, r_down, r_up):
            gu_n, d_n = f"gu{tag}_{l}", f"d{tag}_{l}"
            dgu, dwd, dwdb, dwgu, dwgub = run(ffn_down_bwd, dhb, full[d_n], gate, up, a, n, f"ffn{tag}_down_bwd{l}", **r_down)
            grads[gu_n], grads[d_n] = slots((dwgu, dwgub)), slots((dwd, dwdb))
            return run(nn_rms_bwd, dgu, gu(gu_n), h_in, g, dh, f"ffn{tag}_up_bwd{l}", **r_up)

        later = l + 1 < L
        dh, dhb, small["ffn2"][l] = ffn_bwd(dh, dhb, 2, s["gate2"], s["up2"], s["a2"], s["n3"], s["h2"], (g_ffn2, l),
                                            dict(rs2=((f"gu1_{l + 1}", 0), f"d1_{l + 1}") if later else ()),
                                            dict(rs1=(f"gu2_{l}", f"d2_{l}")))
        do_sb, do_sw, small["osb"][l], small["osw"][l], dw_out, dw_out_b = mix_out_bwd(
            dhb, full[f"out_{l}"], s["mixed"], s["o_sb"], s["o_sw"], (g_osb, l), (g_osw, l), f"mix_out_bwd{l}")
        grads[f"out_{l}"] = slots((dw_out, dw_out_b))
        dq_sb, dk_sb, dv_sb, dq_sw, dk_sw, dv_sw, small["sinks"][l], small["dsc"][l] = run(
            attn_bwd, s["p"], do_sb, s["tot"], do_sw, s["lse"], (sinks, l), f"attn_bwd{l}",
            rs2=(f"gu2_{l}", f"d2_{l}") + (((f"gu1_{l + 1}", 1),) if later else ()), rs1=(f"out_{l}",))
        dp = jnp.concatenate([dq_sb, dk_sb, dv_sb, dq_sw, dk_sw, dv_sw], axis=1)
        dh, dhb, small["mix"][l] = nn_rms_bwd(dp[None], full[f"in_{l}"][None], s["h1"], (g_mix, l), dh, f"mix_in_bwd{l}")
        grads[f"in_{l}"] = slots(tn_matmul(dp[None], s["n2"], 1.0, f"dwin{l}"))
        dh, dhb, small["ffn1"][l] = ffn_bwd(dh, dhb, 1, s["gate1"], s["up1"], s["a1"], s["n1"], s["h0"], (g_ffn1, l),
                                            dict(rs1=(f"in_{l}",), rs2=(f"out_{l}",)),
                                            dict(rs1=(f"gu1_{l}", f"d1_{l}"), rs2=(f"in_{l}",)))

    grad_x = dh.reshape(x.shape)

    upd = {}
    turn_of = lambda transposed: (lambda a: jnp.swapaxes(a, 1, 2)) if transposed else (lambda a: a)

    def update(nm, w, m, v, transposed, riders=()):
        turn = turn_of(transposed)
        names = [f"{nm}_{l}" for l in range(L)]
        return adamw_scattered(turn(w), turn(m), turn(v), [chip_sum[n][0] for n in names], [recv_b[n] for n in names],
                               f"adamw_{nm}", riders=riders, rows=88 if riders is PARTS else 176)

    early = (("gu2", w_ffn2_gu, m_w_ffn2_gu, v_w_ffn2_gu, True), ("d2", w_ffn2_down, m_w_ffn2_down, v_w_ffn2_down, False),
             ("in", w_in, m_w_in, v_w_in, True), ("out", w_out, m_w_out, v_w_out, False))
    res = run(lambda name, riders=(): in_one_call([update(*e, riders=PARTS) for e in early], name, riders),
              "adamw_early", rs2=("gu1_0", "d1_0"))
    for k, e in enumerate(early):
        upd[e[0]] = tuple(turn_of(e[4])(r) for r in res[4 * k:4 * k + 4])
    for e in (("gu1", w_ffn1_gu, m_w_ffn1_gu, v_w_ffn1_gu, True), ("d1", w_ffn1_down, m_w_ffn1_down, v_w_ffn1_down, False)):
        upd[e[0]] = tuple(turn_of(e[4])(r) for r in update(*e))

    d_rel = rel_bias_grad(small["dsc"], bprev, bcur, "rel_bias_grad")[:, :8]
    g_small = pack(small["ffn1"], small["mix"], small["ffn2"], dg_final, small["osb"], small["osw"], small["sinks"], d_rel,
                   loss_part[0, :1])
    m_small = pack(m_norm_ffn1, m_norm_mix, m_norm_ffn2, m_norm_final, m_norm_out_sb, m_norm_out_swa, m_sinks, m_rel_bias, zero)
    v_small = pack(v_norm_ffn1, v_norm_mix, v_norm_ffn2, v_norm_final, v_norm_out_sb, v_norm_out_swa, v_sinks, v_rel_bias, zero)
    gs_small = all_gather_rows(g_small, "ag_small")
    summed = adamw_small(w_small, gs_small, m_small, v_small, "adamw_small")
    small_out = [unpack(a) for a in summed]
    loss = summed[0][4 * L + 1, 8 * L + N_BUCKETS * 8]

    def group(k):
        sm = small_out[k]
        return (sm[0], upd["gu1"][k], upd["d1"][k], sm[1], upd["in"][k], sm[2], sm[3], sm[4], upd["out"][k], sm[5],
                upd["gu2"][k], upd["d2"][k], sm[6], sm[7])

    return (loss, grad_x, *group(0), *group(1), *group(2), *group(3))
```

```python
import math

import jax
import jax.numpy as jnp
from jax import lax
from jax.experimental import pallas as pl
from jax.experimental.pallas import tpu as pltpu

F32 = jnp.float32
BF16 = jnp.bfloat16
S = jax.ShapeDtypeStruct

N_DEV = 8
HEAD_DIM = 64
SB_HEADS = 8
PAIR = 2 * HEAD_DIM
SB_W = 512
SWA_W = 512
KV_W = 128
IN_W = 3 * SB_W + SWA_W + 2 * KV_W
QB = 128
N_BUCKETS = 32
MAX_DISTANCE = 128
EPS = 1e-6
NEG_INF = -1e30
SCALE = HEAD_DIM ** -0.5

ADAM_LR = 0.001
ADAM_B1 = 0.9
ADAM_B2 = 0.999
ADAM_EPS = 1e-08
ADAM_WD = 0.01
ADAM_STEP = 10

VMEM_LIMIT = 56 * 1024 * 1024
MESH = pl.DeviceIdType.MESH


def _params(sem=None, vmem=VMEM_LIMIT):
    return pltpu.CompilerParams(dimension_semantics=sem, vmem_limit_bytes=vmem)


def _nn(a, b):
    return jnp.dot(a, b, preferred_element_type=F32)


def _nt(a, b):
    return lax.dot_general(a, b, (((1,), (1,)), ((), ())), preferred_element_type=F32)


def _tn(a, b):
    return lax.dot_general(a, b, (((0,), (0,)), ((), ())), preferred_element_type=F32)


def _tri(xs, m):
    return [_nn(x.astype(BF16), m) for x in xs]


def _rms(x, g):
    r = lax.rsqrt(jnp.mean(x * x, axis=-1, keepdims=True) + EPS)
    return x * r * g


def _rms_bwd(dy, x, g):
    r = lax.rsqrt(jnp.mean(x * x, axis=-1, keepdims=True) + EPS)
    xhat = x * r
    u = dy * g
    dx = r * (u - xhat * jnp.mean(u * xhat, axis=-1, keepdims=True))
    return dx, jnp.sum(dy * xhat, axis=0, keepdims=True)


def _softplus_logsig(z):
    sp = jnp.maximum(z, 0.0) + jnp.log(1.0 + jnp.exp(-jnp.abs(z)))
    return sp, z - sp


def _gain(g):
    if isinstance(g, tuple):
        rows, n = g
        return rows, pl.BlockSpec((None, 1, rows.shape[2]), lambda *_: (n, 0, 0))
    return g, pl.BlockSpec((1, g.shape[1]), lambda *_: (0, 0))


def _tile(n, want):
    t = min(n, want)
    while n % t:
        t //= 2
    return t


def _place():
    x, y, c = lax.axis_index("x"), lax.axis_index("y"), lax.axis_index("c")
    chips = [(1 - x, y), (x, 1 - y), (1 - x, 1 - y)]
    return x, y, c, chips


def all_gather_rows(v, name):
    R, C = v.shape

    def body(v_ref, out_ref, send_sems, recv_sems, local_sem):
        x, y, c, chips = _place()
        me, sibling = (x, y, c), (x, y, 1 - c)

        def slot(px, py, pc):
            return out_ref.at[4 * px + 2 * py + pc]

        def copy(k, block, to, src=None):
            return pltpu.make_async_remote_copy(
                src_ref=slot(*block) if src is None else src, dst_ref=slot(*block),
                send_sem=send_sems.at[k], recv_sem=recv_sems.at[k], device_id=to, device_id_type=MESH)

        mine = pltpu.make_async_copy(v_ref, slot(*me), local_sem)
        mine.start()
        first = [copy(0, me, sibling, src=v_ref)]
        first += [copy(1 + j, me, (*chip, c), src=v_ref) for j, chip in enumerate(chips)]
        for cp in first:
            cp.start()
        passed = [copy(4 + j, (*chip, c), sibling) for j, chip in enumerate(chips)]
        for j, chip in enumerate(chips):
            copy(1 + j, (*chip, c), me).wait_recv()
            passed[j].start()
        copy(0, sibling, me).wait_recv()
        for j, chip in enumerate(chips):
            copy(4 + j, (*chip, 1 - c), me).wait_recv()
        for cp in first + passed:
            cp.wait_send()
        mine.wait()

    return pl.pallas_call(
        body, name=name, out_shape=S((N_DEV, R, C), v.dtype),
        in_specs=[pl.BlockSpec(memory_space=pl.ANY)], out_specs=pl.BlockSpec(memory_space=pl.ANY),
        scratch_shapes=[pltpu.SemaphoreType.DMA((7,)), pltpu.SemaphoreType.DMA((7,)), pltpu.SemaphoreType.DMA],
    )(v)


class _Exchange:
    def __init__(self, ins, outs, sizes, n_local, plan, aliases=None):
        self.ins, self.outs, self.plan, self.aliases = list(ins), list(outs), plan, aliases or {}
        self.sizes, self.n_local = list(sizes), n_local

    def scratch(self):
        n = sum(self.sizes)
        return [pltpu.SemaphoreType.DMA((n,)), pltpu.SemaphoreType.DMA((n,)), pltpu.SemaphoreType.DMA((max(self.n_local, 1),))]

    def _copies(self, in_refs, out_refs, sems):
        send_sems, recv_sems, local_sems = sems
        phases, local = self.plan(in_refs, out_refs)
        out, k = [], 0
        for phase in phases:
            out.append([pltpu.make_async_remote_copy(src_ref=s, dst_ref=d, send_sem=send_sems.at[k + n], recv_sem=recv_sems.at[k + n],
                                                     device_id=dev, device_id_type=MESH) for n, (s, d, dev) in enumerate(phase)])
            k += len(phase)
        return out, [pltpu.make_async_copy(s, d, local_sems.at[n]) for n, (s, d) in enumerate(local)]

    def start(self, in_refs, out_refs, sems):
        phases, loc = self._copies(in_refs, out_refs, sems)
        for cp in phases[0] + loc:
            cp.start()

    def advance(self, hook, in_refs, out_refs, sems):
        p = hook - (3 - len(self.sizes))
        if p >= 1:
            phases, _ = self._copies(in_refs, out_refs, sems)
            for cp in phases[p - 1]:
                cp.wait_recv()
            for cp in phases[p]:
                cp.start()

    def finish(self, in_refs, out_refs, sems):
        phases, loc = self._copies(in_refs, out_refs, sems)
        for cp in phases[-1]:
            cp.wait_recv()
        for phase in phases:
            for cp in phase:
                cp.wait_send()
        for cp in loc:
            cp.wait()


def gather(v, rows=None, into=None):
    R, C = v.shape
    r0, nr = rows or (0, R)
    na = min(nr, ((nr // 2 + 15) // 16) * 16)

    def plan(ins, outs):
        x, y, c, _ = _place()
        xn, yn, dg, sibling = (1 - x, y), (x, 1 - y), (1 - x, 1 - y), (x, y, 1 - c)
        slot = lambda chip, start=r0, count=nr: outs[0].at[4 * chip[0] + 2 * chip[1] + c, pl.ds(start, count), :]
        src, mine = ins[0].at[pl.ds(r0, nr), :], slot((x, y))
        same = lambda ref, to: (ref, ref, to)
        first = [(src, mine, sibling), (src, mine, (*xn, c)), (src, mine, (*yn, c))]
        relay = [same(slot(xn, r0, na), (*yn, c)), same(slot(yn, r0 + na, nr - na), (*xn, c))]
        onward = [same(slot(xn), sibling), same(slot(yn), sibling), same(slot(dg), sibling)]
        return [first, relay, onward], [(src, mine)]

    if into is None:
        return _Exchange([v], [S((N_DEV, R, C), v.dtype)], (3, 2, 3), 1, plan)
    return _Exchange([v, into], [S((N_DEV, R, C), v.dtype)], (3, 2, 3), 1, plan, aliases={1: 0})


def scatter_first(gb):
    _, R, C = gb.shape

    def plan(ins, outs):
        x, y, c, chips = _place()
        owners = [(x, y)] + chips
        return [[(ins[0].at[4 * px + 2 * py + (1 - c)], outs[0].at[j], (x, y, 1 - c)) for j, (px, py) in enumerate(owners)]], []

    return _Exchange([gb], [S((4, R, C), BF16)], (4,), 0, plan)


def scatter_second(sb, rows=None, into=None):
    r0, nr = rows or (0, sb.shape[1])

    def plan(ins, outs):
        x, y, c, chips = _place()
        part = lambda ref, j: ref.at[j, pl.ds(r0, nr), :]
        return [[(part(ins[0], j), part(outs[0], j), (*chips[j], c)) for j in range(3)]], []

    if into is None:
        return _Exchange([sb], [S(sb.shape, BF16)], (3,), 0, plan)
    return _Exchange([sb, into], [S(sb.shape, BF16)], (3,), 0, plan, aliases={1: 0})


PARTS = "parts"


def _call(body, *, name, grid, in_specs, out_specs, out_shape, args, scratch=(), sem=None, riders=(), marks=None):
    single = not isinstance(out_shape, (tuple, list))
    out_shape = (out_shape,) if single else tuple(out_shape)
    out_specs = (out_specs,) if single else tuple(out_specs)
    n_in, n_out, n_sc = len(in_specs), len(out_shape), len(scratch)
    if riders is PARTS:
        return dict(body=body, grid=grid, in_specs=list(in_specs), out_specs=out_specs, out_shape=out_shape, args=tuple(args),
                    scratch=list(scratch), marks=marks)
    if not riders:
        res = pl.pallas_call(body, name=name, grid=grid, in_specs=list(in_specs), out_specs=out_specs, out_shape=out_shape,
                             scratch_shapes=list(scratch), compiler_params=_params(sem))(*args)
        return res[0] if single else res
    r_ins = [a for r in riders for a in r.ins]
    r_outs = [o for r in riders for o in r.outs]
    r_scr = [s for r in riders for s in r.scratch()]
    aliases, i0, o0 = {}, n_in, n_out
    for r in riders:
        for a, b in r.aliases.items():
            aliases[i0 + a] = o0 + b
        i0, o0 = i0 + len(r.ins), o0 + len(r.outs)
    steps = math.prod(grid)

    def full(*refs):
        ins, rin = refs[:n_in], refs[n_in:n_in + len(r_ins)]
        pos = n_in + len(r_ins)
        outs, rout = refs[pos:pos + n_out], refs[pos + n_out:pos + n_out + len(r_outs)]
        pos += n_out + len(r_outs)
        sc, rsc = refs[pos:pos + n_sc], refs[pos + n_sc:]
        step = 0
        for d, n in enumerate(grid):
            step = step * n + pl.program_id(d)

        def each(method, *lead):
            i, o = 0, 0
            for k, r in enumerate(riders):
                getattr(r, method)(*lead, rin[i:i + len(r.ins)], rout[o:o + len(r.outs)], rsc[3 * k:3 * k + 3])
                i, o = i + len(r.ins), o + len(r.outs)

        @pl.when(step == 0)
        def _():
            each("start")
        body(*ins, *outs, *sc)

        late = max(steps - 1 - max(steps // 8, 1), 0)
        first, second = marks or (min((3 * steps) // 5, late), late)

        @pl.when(step == first)
        def _():
            each("advance", 1)

        @pl.when(step == second)
        def _():
            each("advance", 2)

        @pl.when(step == steps - 1)
        def _():
            each("finish")

    anywhere = pl.BlockSpec(memory_space=pl.ANY)
    res = pl.pallas_call(
        full, name=name, grid=grid, in_specs=list(in_specs) + [anywhere] * len(r_ins),
        out_specs=out_specs + (anywhere,) * len(r_outs), out_shape=out_shape + tuple(r_outs),
        scratch_shapes=list(scratch) + r_scr, input_output_aliases=aliases,
        compiler_params=_params(("arbitrary",) * len(grid)))(*args, *r_ins)
    host, rest, per = res[:n_out], list(res[n_out:]), []
    for r in riders:
        per.append(rest[:len(r.outs)])
        rest = rest[len(r.outs):]
    return (host[0] if single else tuple(host)), per


def side_by_side(first, second, name, riders=()):
    a_in, a_out, a_sc = len(first["in_specs"]), len(first["out_shape"]), len(first["scratch"])
    n_in, n_out = a_in + len(second["in_specs"]), a_out + len(second["out_shape"])

    def body(*refs):
        ins, outs, sc = refs[:n_in], refs[n_in:n_in + n_out], refs[n_in + n_out:]
        first["body"](*ins[:a_in], *outs[:a_out], *sc[:a_sc])
        second["body"](*ins[a_in:], *outs[a_out:], *sc[a_sc:])

    return _call(body, name=name, grid=first["grid"], in_specs=first["in_specs"] + second["in_specs"],
                 out_specs=first["out_specs"] + second["out_specs"], out_shape=first["out_shape"] + second["out_shape"],
                 args=first["args"] + second["args"], scratch=first["scratch"] + second["scratch"],
                 sem=("arbitrary",) * len(first["grid"]), riders=riders, marks=first["marks"])


def in_one_call(parts, name, riders=()):
    extents = [p["grid"][-1] for p in parts]
    longest = max(extents)

    def clamp(spec, n):
        if n == longest or spec.index_map is None:
            return spec
        return pl.BlockSpec(spec.block_shape, lambda *ids, f=spec.index_map: f(*ids[:-1], jnp.minimum(ids[-1], n - 1)))

    counts = [(len(p["in_specs"]), len(p["out_shape"]), len(p["scratch"])) for p in parts]
    n_in, n_out = sum(c[0] for c in counts), sum(c[1] for c in counts)

    def body(*refs):
        ins, outs, sc = refs[:n_in], refs[n_in:n_in + n_out], refs[n_in + n_out:]
        i = o = s = 0
        for p, n, (ci, co, cs) in zip(parts, extents, counts):
            run_part = lambda p=p, a=ins[i:i + ci], b=outs[o:o + co], c=sc[s:s + cs]: p["body"](*a, *b, *c)
            if n == longest:
                run_part()
            else:
                pl.when(pl.program_id(len(p["grid"]) - 1) < n)(run_part)
            i, o, s = i + ci, o + co, s + cs

    cat = lambda key: [x for p in parts for x in p[key]]
    return _call(body, name=name, grid=parts[0]["grid"][:-1] + (longest,),
                 in_specs=[clamp(sp, n) for p, n in zip(parts, extents) for sp in p["in_specs"]],
                 out_specs=tuple(clamp(sp, n) for p, n in zip(parts, extents) for sp in p["out_specs"]),
                 out_shape=tuple(cat("out_shape")), args=tuple(cat("args")), scratch=cat("scratch"),
                 sem=("arbitrary",) * len(parts[0]["grid"]), riders=riders)


def _rows_tile(n, cap):
    return max(t for t in range(16, min(n, cap) + 1, 16) if n % t == 0)


def scatter_add(gs, ras, name):
    C = gs[0].shape[2]
    trs = [_rows_tile(g.shape[1], 176) for g in gs]
    nts = [g.shape[1] // tr for g, tr in zip(gs, trs)]
    steps = max(nts)
    x, y, c, chips = _place()
    slots = jnp.stack([4 * px + 2 * py + c for px, py in [(x, y)] + chips]).astype(jnp.int32)

    def body(s_ref, *refs):
        ins, outs = refs[:5 * len(gs)], refs[5 * len(gs):]
        for k in range(len(gs)):
            g0, g1, g2, g3, ra_ref = ins[5 * k:5 * k + 5]
            own_ref, sb_ref = outs[2 * k:2 * k + 2]

            def work(g0=g0, g1=g1, g2=g2, g3=g3, ra_ref=ra_ref, own_ref=own_ref, sb_ref=sb_ref):
                own_ref[...] = g0[...] + ra_ref[0].astype(F32)
                for j, gj in enumerate((g1, g2, g3)):
                    sb_ref[j] = (gj[...] + ra_ref[j + 1].astype(F32)).astype(BF16)

            if nts[k] == steps:
                work()
            else:
                pl.when(pl.program_id(0) < nts[k])(work)

    in_specs, out_specs, out_shape, args = [], [], [], [slots]
    for k, (g, ra, tr) in enumerate(zip(gs, ras, trs)):
        tile = lambda i, k=k: jnp.minimum(i, nts[k] - 1)
        in_specs += [pl.BlockSpec((None, tr, C), lambda i, s, j=j, tile=tile: (s[j], tile(i), 0)) for j in range(4)]
        in_specs.append(pl.BlockSpec((4, tr, C), lambda i, s, tile=tile: (0, tile(i), 0)))
        out_specs += [pl.BlockSpec((tr, C), lambda i, s, tile=tile: (tile(i), 0)),
                      pl.BlockSpec((3, tr, C), lambda i, s, tile=tile: (0, tile(i), 0))]
        out_shape += [S((g.shape[1], C), F32), S((3, g.shape[1], C), BF16)]
        args += [g, g, g, g, ra]
    spec = pltpu.PrefetchScalarGridSpec(num_scalar_prefetch=1, grid=(steps,), in_specs=in_specs, out_specs=tuple(out_specs))
    res = pl.pallas_call(body, name=name, grid_spec=spec, out_shape=tuple(out_shape), compiler_params=_params(("arbitrary",)))(*args)
    return [(res[2 * k], res[2 * k + 1]) for k in range(len(gs))]


def rms_cast(h, g, name, riders=()):
    T, D = h.shape
    tm = _tile(T, 512)

    def body(h_ref, g_ref, n_ref):
        n_ref[...] = _rms(h_ref[...], g_ref[...]).astype(BF16)

    row = pl.BlockSpec((tm, D), lambda i: (i, 0))
    g, g_spec = _gain(g)
    return _call(body, name=name, grid=(T // tm,), out_shape=S((T, D), BF16), in_specs=[row, g_spec],
                 out_specs=row, sem=("parallel",), args=(h, g), riders=riders)


def ffn_up_fwd(n, wgu, name, riders=()):
    T, D = n.shape
    F = wgu.shape[1]
    tr, tn = _tile(T, 512), _tile(F, 256)

    def body(n_ref, wg_ref, wu_ref, dgate_ref, dup_ref, a_ref):
        wg, wu = wg_ref[...], wu_ref[...]
        for r in range(T // tr):
            rows = slice(r * tr, (r + 1) * tr)
            x = n_ref[rows, :]
            gate = _nt(x, wg)
            up = _nt(x, wu)
            s = jax.nn.sigmoid(gate)
            silu = gate * s
            dgate_ref[rows, :] = (up * (s * (1.0 + gate * (1.0 - s)))).astype(BF16)
            dup_ref[rows, :] = silu.astype(BF16)
            a_ref[rows, :] = (silu * up).astype(BF16)

    tile = pl.BlockSpec((T, tn), lambda j: (0, j))
    return _call(
        body, name=name, grid=(F // tn,), out_shape=(S((T, F), BF16),) * 3,
        in_specs=[pl.BlockSpec((T, D), lambda j: (0, 0)),
                  pl.BlockSpec((None, tn, D), lambda j: (0, j, 0)), pl.BlockSpec((None, tn, D), lambda j: (1, j, 0))],
        out_specs=(tile, tile, tile), sem=("parallel",), args=(n, wgu, wgu), riders=riders)


def ffn_down_fwd(a, wd, h, g_next, name, riders=()):
    T, F = a.shape
    D = wd.shape[1]
    tm = _tile(T, 256)

    def body(a_ref, w_ref, h_ref, *rest):
        out = h_ref[...] + 0.5 * _nn(a_ref[...], w_ref[...])
        if g_next is None:
            rest[0][...] = out
        else:
            g_ref, o_ref, n_ref = rest
            o_ref[...] = out
            n_ref[...] = _rms(out, g_ref[...]).astype(BF16)

    row = pl.BlockSpec((tm, D), lambda i: (i, 0))
    more = g_next is not None
    g_arg, g_spec = _gain(g_next) if more else (None, None)
    return _call(
        body, name=name, grid=(T // tm,), out_shape=(S((T, D), F32), S((T, D), BF16)) if more else S((T, D), F32),
        in_specs=[pl.BlockSpec((tm, F), lambda i: (i, 0)), pl.BlockSpec((F, D), lambda i: (0, 0)), row] + ([g_spec] if more else []),
        out_specs=(row, row) if more else row,
        sem=("parallel",), args=(a, wd, h) + ((g_arg,) if more else ()), riders=riders)


def mix_in_fwd(h, g, win, name):
    T, D = h.shape
    N = win.shape[0]
    tm = _tile(T, 256)

    def body(h_ref, g_ref, w_ref, n_ref, p_ref):
        n = _rms(h_ref[...], g_ref[...]).astype(BF16)
        n_ref[...] = n
        p_ref[...] = _nt(n, w_ref[...]).astype(BF16)

    g, g_spec = _gain(g)
    return pl.pallas_call(
        body, name=name, grid=(T // tm,), out_shape=(S((T, D), BF16), S((T, N), BF16)),
        in_specs=[pl.BlockSpec((tm, D), lambda i: (i, 0)), g_spec, pl.BlockSpec((N, D), lambda i: (0, 0))],
        out_specs=(pl.BlockSpec((tm, D), lambda i: (i, 0)), pl.BlockSpec((tm, N), lambda i: (i, 0))),
        compiler_params=_params(("parallel",)),
    )(h, g, win)


def _tri_consts():
    r = lax.broadcasted_iota(jnp.int32, (QB, QB), 0)
    c = lax.broadcasted_iota(jnp.int32, (QB, QB), 1)
    ones = jnp.ones((QB, QB), BF16)
    with_sums = lambda tri: jnp.concatenate([tri.astype(BF16), ones], axis=1)
    return with_sums(r > c), with_sums(r <= c), with_sums(r < c)


def _half_masks():
    lane = lax.broadcasted_iota(jnp.int32, (QB, PAIR), 1)
    row = lax.broadcasted_iota(jnp.int32, (QB, PAIR), 0)
    return lane < HEAD_DIM, lane, row


def sb_attn_fwd(p, after, name, riders=()):
    T = p.shape[0]
    nq = T // QB

    def body(q_ref, k_ref, v_ref, m_ref, o_ref, tot_ref, q_sc, acc_ref, z_sc):
        i = pl.program_id(0)
        lo, lane, row = _half_masks()
        causal = lane < row
        heads, pairs = range(SB_HEADS), range(SB_HEADS // 2)
        for hp in pairs:
            q_sc[hp] = (q_ref[:, hp * PAIR:(hp + 1) * PAIR].astype(F32) * SCALE).astype(BF16)
        m2 = m_ref[...]

        def by_head(ref, j, hp):
            t = ref[pl.ds(pl.multiple_of(j * QB, QB), QB), hp * PAIR:(hp + 1) * PAIR]
            return jnp.concatenate([jnp.where(lo, t, 0), jnp.where(lo, 0, t)], axis=0)

        def scores(j):
            return [_nt(q_sc[hp], by_head(k_ref, j, hp)) for hp in pairs]

        def block(j, diag):
            z2 = [z_sc[hp] for hp in pairs]
            ahead = scores(jnp.maximum(j - 1, 0))
            for hp in pairs:
                z_sc[hp] = ahead[hp]
            vs = [by_head(v_ref, j, hp) for hp in pairs]
            spls = [_softplus_logsig(z2[h // 2][:, (h % 2) * QB:(h % 2 + 1) * QB]) for h in heads]
            sp = [jnp.where(causal, spls[h][0], 0.0) if diag else spls[h][0] for h in heads]
            rr = _tri(sp, m2)
            if diag:
                w = [jnp.where(causal, jnp.exp(spls[h][1] - rr[h][:, :QB]), 0.0).astype(BF16) for h in heads]
            else:
                c = [tot_ref[:, h * QB:(h + 1) * QB] for h in heads]
                w = [jnp.exp(spls[h][1] - (c[h] + rr[h][:, :QB])).astype(BF16) for h in heads]
            pv = [_nn(jnp.concatenate([w[2 * hp], w[2 * hp + 1]], axis=1), vs[hp]) for hp in pairs]
            for hp in pairs:
                acc_ref[hp] = pv[hp] if diag else acc_ref[hp] + pv[hp]
            for h in heads:
                tot_ref[:, h * QB:(h + 1) * QB] = rr[h][:, QB:] if diag else c[h] + rr[h][:, QB:]

        first = scores(i)
        for hp in pairs:
            z_sc[hp] = first[hp]
        block(i, True)

        def step(t, carry):
            block(i - 1 - t, False)
            return carry
        lax.fori_loop(0, i, step, 0)
        for hp in pairs:
            o_ref[:, hp * PAIR:(hp + 1) * PAIR] = acc_ref[hp]

    npair = SB_HEADS // 2
    return _call(
        body, name=name, grid=(nq,), out_shape=(S((T, SB_W), F32), S((T, SB_HEADS * QB), F32)),
        in_specs=[pl.BlockSpec((QB, SB_W), lambda i: (i, 0)), pl.BlockSpec((T, SB_W), lambda i: (0, 1)),
                  pl.BlockSpec((T, SB_W), lambda i: (0, 2)), pl.BlockSpec((QB, 2 * QB), lambda i: (0, 0))],
        out_specs=(pl.BlockSpec((QB, SB_W), lambda i: (i, 0)), pl.BlockSpec((QB, SB_HEADS * QB), lambda i: (i, 0))),
        scratch=[pltpu.VMEM((npair, QB, PAIR), BF16), pltpu.VMEM((npair, QB, PAIR), F32), pltpu.VMEM((npair, QB, 2 * QB), F32)],
        sem=("arbitrary",), args=(p, p, p, after), riders=riders,
        marks=((12 * nq) // 16, (15 * nq) // 16))


def sb_attn_bwd(p, do, tot, upto, before, name, riders=()):
    T = p.shape[0]
    nq = T // QB

    def body(q_ref, k_ref, v_ref, do_ref, tot_ref, mp_ref, mg_ref, dq_ref, dk_ref, dv_ref,
             q_sc, d_sc, qd_sc, pg_sc, dq_acc, dk_acc, dv_acc, zd_sc):
        i = pl.program_id(0)
        lo, lane, row = _half_masks()
        causal = lane < row
        heads, pairs = range(SB_HEADS), range(SB_HEADS // 2)

        def by_head(t):
            return jnp.concatenate([jnp.where(lo, t, 0), jnp.where(lo, 0, t)], axis=0)

        for hp in pairs:
            q2 = (q_ref[:, hp * PAIR:(hp + 1) * PAIR].astype(F32) * SCALE).astype(BF16)
            d2 = do_ref[:, hp * PAIR:(hp + 1) * PAIR].astype(BF16)
            q_sc[hp] = q2
            d_sc[hp] = d2
            qd_sc[hp] = by_head(q2)
            qd_sc[SB_HEADS // 2 + hp] = by_head(d2)
        mp, mg = mp_ref[...], mg_ref[...]

        @pl.when(i == 0)
        def _():
            dk_acc[...] = jnp.zeros_like(dk_acc)
            dv_acc[...] = jnp.zeros_like(dv_acc)
        pg_sc[...] = jnp.zeros_like(pg_sc)
        dq_acc[...] = jnp.zeros_like(dq_acc)

        def rows(ref, j, hp):
            return ref[pl.ds(pl.multiple_of(j * QB, QB), QB), hp * PAIR:(hp + 1) * PAIR]

        def products(j):
            return ([_nt(q_sc[hp], by_head(rows(k_ref, j, hp))) for hp in pairs]
                    + [_nt(d_sc[hp], by_head(rows(v_ref, j, hp))) for hp in pairs])

        def block(j, diag):
            r0 = pl.multiple_of(j * QB, QB)
            half = lambda t, h: t[:, (h % 2) * QB:(h % 2 + 1) * QB]
            z = [half(zd_sc[h // 2], h) for h in heads]
            dw = [half(zd_sc[SB_HEADS // 2 + h // 2], h) for h in heads]
            if not diag:
                ahead = products(j + 1)
                for hp in range(SB_HEADS):
                    zd_sc[hp] = ahead[hp]
            ks = [by_head(rows(k_ref, j, hp)) for hp in pairs]
            spls = [_softplus_logsig(z[h]) for h in heads]
            sp = [jnp.where(causal, spls[h][0], 0.0) if diag else spls[h][0] for h in heads]
            rr = _tri(sp, mp)
            pc = [pg_sc[2 * h] for h in heads]
            w = [jnp.exp(spls[h][1] - (tot_ref[:, h * QB:(h + 1) * QB] - (pc[h] + rr[h][:, :QB]))) for h in heads]
            if diag:
                w = [jnp.where(causal, w[h], 0.0) for h in heads]
            gg = [dw[h] * w[h] for h in heads]
            rg = _tri(gg, mg)
            gc = [pg_sc[2 * h + 1] for h in heads]
            dz = [gg[h] - (gg[h] + gc[h] + rg[h][:, :QB]) * jnp.exp(spls[h][1]) for h in heads]
            if diag:
                dz = [jnp.where(causal, dz[h], 0.0) for h in heads]
            dzb = [dz[h].astype(BF16) for h in heads]
            wb = [w[h].astype(BF16) for h in heads]
            both = lambda t, hp, axis: jnp.concatenate([t[2 * hp], t[2 * hp + 1]], axis=axis)
            dq = [_nn(both(dzb, hp, 1), ks[hp]) for hp in pairs]
            dk = [_tn(both(dzb, hp, 0), qd_sc[hp]) for hp in pairs]
            dv = [_tn(both(wb, hp, 0), qd_sc[SB_HEADS // 2 + hp]) for hp in pairs]
            for h in heads:
                if not diag:
                    pg_sc[2 * h] = pc[h] + rr[h][:, QB:]
                    pg_sc[2 * h + 1] = gc[h] + rg[h][:, QB:]
            for hp in pairs:
                dq_acc[hp] += dq[hp]
                dk_acc[pl.ds(r0, QB), hp * PAIR:(hp + 1) * PAIR] += dk[hp]
                dv_acc[pl.ds(r0, QB), hp * PAIR:(hp + 1) * PAIR] += dv[hp]

        first = products(0)
        for hp in range(SB_HEADS):
            zd_sc[hp] = first[hp]

        def step(t, carry):
            block(t, False)
            return carry
        lax.fori_loop(0, i, step, 0)
        block(i, True)
        for hp in pairs:
            dq_ref[:, hp * PAIR:(hp + 1) * PAIR] = (dq_acc[hp] * SCALE).astype(BF16)

        @pl.when(i == nq - 1)
        def _():
            dk_ref[...] = dk_acc[...].astype(BF16)
            dv_ref[...] = dv_acc[...].astype(BF16)

    qtile = pl.BlockSpec((QB, SB_W), lambda i: (i, 0))
    whole = pl.BlockSpec((T, SB_W), lambda i: (0, 0))
    const = pl.BlockSpec((QB, 2 * QB), lambda i: (0, 0))
    return _call(
        body, name=name, grid=(nq,), out_shape=(S((T, SB_W), BF16),) * 3,
        in_specs=[qtile, pl.BlockSpec((T, SB_W), lambda i: (0, 1)), pl.BlockSpec((T, SB_W), lambda i: (0, 2)), qtile,
                  pl.BlockSpec((QB, SB_HEADS * QB), lambda i: (i, 0)), const, const],
        out_specs=(qtile, whole, whole),
        scratch=[pltpu.VMEM((SB_HEADS // 2, QB, PAIR), BF16), pltpu.VMEM((SB_HEADS // 2, QB, PAIR), BF16),
                 pltpu.VMEM((SB_HEADS, 2 * QB, PAIR), BF16),
                 pltpu.VMEM((2 * SB_HEADS, QB, QB), F32), pltpu.VMEM((SB_HEADS // 2, QB, PAIR), F32),
                 pltpu.VMEM((T, SB_W), F32), pltpu.VMEM((T, SB_W), F32), pltpu.VMEM((SB_HEADS, QB, 2 * QB), F32)],
        sem=("arbitrary",), args=(p, p, p, do, tot, upto, before), riders=riders)


def _t5_buckets():
    a = lax.broadcasted_iota(jnp.int32, (QB, QB), 0)
    c = lax.broadcasted_iota(jnp.int32, (QB, QB), 1)

    def bucket(dist):
        dist = jnp.maximum(dist, 0)
        max_exact = N_BUCKETS // 2
        d = jnp.maximum(dist, 1).astype(F32)
        large = max_exact + (jnp.log(d / max_exact) / math.log(MAX_DISTANCE / max_exact)
                             * (N_BUCKETS - max_exact)).astype(jnp.int32)
        large = jnp.minimum(large, N_BUCKETS - 1)
        return jnp.where(dist < max_exact, dist, large)

    return bucket(QB + a - c), bucket(a - c)


def _swa_common(i, kp_ref, kc_ref, vp_ref, vc_ref, bp_ref, bc_ref, rb_ref, bias_ref):
    lo, lane, row = _half_masks()

    @pl.when(i == 0)
    def _():
        for blk, b_ref in enumerate((bp_ref, bc_ref)):
            bk = b_ref[...]
            for h in range(8):
                acc = jnp.zeros((QB, QB), F32)
                for b in range(N_BUCKETS):
                    acc = jnp.where(bk == b, rb_ref[b, h], acc)
                bias_ref[h, blk] = acc

    band = [(lane > row) & (i > 0), lane <= row]

    def stacks(ref):
        t = ref[...].astype(F32)
        sw = pltpu.roll(t, HEAD_DIM, 1)
        return [jnp.concatenate([jnp.where(lo, t, 0.0), jnp.where(lo, 0.0, sw)], axis=0).astype(BF16),
                jnp.concatenate([jnp.where(lo, sw, 0.0), jnp.where(lo, 0.0, t)], axis=0).astype(BF16)]

    ks = [stacks(kp_ref), stacks(kc_ref)]
    vs = [stacks(vp_ref), stacks(vc_ref)]
    return lo, band, ks, vs


def _lane_half(t, h):
    return t[:, (h % 2) * QB:(h % 2 + 1) * QB]


def swa_fwd(p, sinks, rel_bias, bprev, bcur, name, riders=()):
    T = p.shape[0]
    nq = T // QB
    kcol, vcol = (3 * SB_W + SWA_W) // KV_W, (3 * SB_W + SWA_W) // KV_W + 1
    sinks, srow = sinks if isinstance(sinks, tuple) else (sinks, 0)

    def body(q_ref, kp_ref, kc_ref, vp_ref, vc_ref, bp_ref, bc_ref, sink_ref, rb_ref, o_ref, lse_ref, bias_ref):
        i = pl.program_id(0)
        lo, band, ks, vs = _swa_common(i, kp_ref, kc_ref, vp_ref, vc_ref, bp_ref, bc_ref, rb_ref, bias_ref)
        heads, pairs, blocks = range(8), range(4), range(2)
        rowmax = lambda t: jnp.max(t, axis=1, keepdims=True)
        rowsum = lambda t: jnp.sum(t, axis=1, keepdims=True)
        q2 = [q_ref[:, g * PAIR:(g + 1) * PAIR] for g in pairs]
        s2 = [[_nt(q2[g], ks[b][g // 2]) for b in blocks] for g in pairs]
        sc = [[jnp.where(band[b], _lane_half(s2[h // 2][b], h) * SCALE + bias_ref[h, b], NEG_INF) for b in blocks] for h in heads]
        sink = [sink_ref[srow, h] for h in heads]
        m = [jnp.maximum(jnp.maximum(rowmax(sc[h][0]), rowmax(sc[h][1])), sink[h]) for h in heads]
        e = [[jnp.exp(sc[h][b] - m[h]) for b in blocks] for h in heads]
        den = [rowsum(e[h][0]) + rowsum(e[h][1]) + jnp.exp(sink[h] - m[h]) for h in heads]
        pb = [[(e[h][b] / den[h]).astype(BF16) for b in blocks] for h in heads]
        for g in pairs:
            both = lambda b: jnp.concatenate([pb[2 * g][b], pb[2 * g + 1][b]], axis=1)
            o_ref[:, g * PAIR:(g + 1) * PAIR] = _nn(both(0), vs[0][g // 2]) + _nn(both(1), vs[1][g // 2])
        for h in heads:
            lse_ref[:, h * QB:(h + 1) * QB] = jnp.broadcast_to(m[h] + jnp.log(den[h]), (QB, QB))

    kv = lambda col, prev: pl.BlockSpec((QB, KV_W), (lambda i: (jnp.maximum(i - 1, 0), col)) if prev else (lambda i: (i, col)))
    full = pl.BlockSpec((QB, QB), lambda i: (0, 0))
    smem = pl.BlockSpec(memory_space=pltpu.SMEM)
    return _call(
        body, name=name, grid=(nq,), out_shape=(S((T, SWA_W), F32), S((T, 8 * QB), F32)),
        in_specs=[pl.BlockSpec((QB, SWA_W), lambda i: (i, 3)), kv(kcol, True), kv(kcol, False), kv(vcol, True), kv(vcol, False),
                  full, full, smem, smem],
        out_specs=(pl.BlockSpec((QB, SWA_W), lambda i: (i, 0)), pl.BlockSpec((QB, 8 * QB), lambda i: (i, 0))),
        scratch=[pltpu.VMEM((8, 2, QB, QB), F32)],
        sem=("arbitrary",), args=(p, p, p, p, p, bprev, bcur, sinks, rel_bias), riders=riders)


def swa_bwd(p, do, lse, sinks, rel_bias, bprev, bcur, name, riders=()):
    T = p.shape[0]
    nq = T // QB
    kcol, vcol = (3 * SB_W + SWA_W) // KV_W, (3 * SB_W + SWA_W) // KV_W + 1
    sinks, srow = sinks if isinstance(sinks, tuple) else (sinks, 0)

    def body(q_ref, kp_ref, kc_ref, vp_ref, vc_ref, do_ref, lse_ref, bp_ref, bc_ref, sink_ref, rb_ref,
             dq_ref, dk_ref, dv_ref, dsink_ref, dsc_ref, bias_ref, dk_acc, dv_acc):
        i = pl.program_id(0)
        lo, band, ks, vs = _swa_common(i, kp_ref, kc_ref, vp_ref, vc_ref, bp_ref, bc_ref, rb_ref, bias_ref)

        @pl.when(i == 0)
        def _():
            dk_acc[...] = jnp.zeros_like(dk_acc)
            dv_acc[...] = jnp.zeros_like(dv_acc)
            dsc_ref[...] = jnp.zeros_like(dsc_ref)
            dsink_ref[...] = jnp.zeros_like(dsink_ref)

        heads, pairs, blocks = range(8), range(4), range(2)
        rowsum = lambda t: jnp.sum(t, axis=1, keepdims=True)
        by_head = lambda t: jnp.concatenate([jnp.where(lo, t, 0), jnp.where(lo, 0, t)], axis=0)
        q2 = [q_ref[:, g * PAIR:(g + 1) * PAIR] for g in pairs]
        d2 = [do_ref[:, g * PAIR:(g + 1) * PAIR].astype(BF16) for g in pairs]
        qs = [by_head(q2[g]) for g in pairs]
        dos = [by_head(d2[g]) for g in pairs]
        s2 = [[_nt(q2[g], ks[b][g // 2]) for b in blocks] for g in pairs]
        dp2 = [[_nt(d2[g], vs[b][g // 2]) for b in blocks] for g in pairs]
        lse_h = [lse_ref[:, h * QB:(h + 1) * QB] for h in heads]
        sink = [sink_ref[srow, h] for h in heads]
        pr = [[jnp.exp(jnp.where(band[b], _lane_half(s2[h // 2][b], h) * SCALE + bias_ref[h, b], NEG_INF) - lse_h[h])
               for b in blocks] for h in heads]
        dp = [[_lane_half(dp2[h // 2][b], h) for b in blocks] for h in heads]
        delta = [rowsum(pr[h][0] * dp[h][0]) + rowsum(pr[h][1] * dp[h][1]) for h in heads]
        lane1 = lax.broadcasted_iota(jnp.int32, (1, QB), 1)
        dsink = jnp.zeros((1, QB), F32)
        for h in heads:
            dsink = dsink + jnp.where(lane1 == h, -jnp.sum(jnp.exp(sink[h] - lse_h[h][:, :1]) * delta[h]), 0.0)
        dsink_ref[...] += dsink
        dsc = [[pr[h][b] * (dp[h][b] - delta[h]) for b in blocks] for h in heads]
        for h in heads:
            for b in blocks:
                dsc_ref[h, b] += dsc[h][b]
        dzb = [[(dsc[h][b] * SCALE).astype(BF16) for b in blocks] for h in heads]
        prb = [[pr[h][b].astype(BF16) for b in blocks] for h in heads]
        pair_of = lambda t, g, b, axis: jnp.concatenate([t[2 * g][b], t[2 * g + 1][b]], axis=axis)
        for g in pairs:
            dq = _nn(pair_of(dzb, g, 0, 1), ks[0][g // 2]) + _nn(pair_of(dzb, g, 1, 1), ks[1][g // 2])
            dq_ref[:, g * PAIR:(g + 1) * PAIR] = dq.astype(BF16)

        def key_grad(t, other, b):
            per_kv = [_tn(pair_of(t, 2 * kh, b, 0), other[2 * kh]) + _tn(pair_of(t, 2 * kh + 1, b, 0), other[2 * kh + 1]) for kh in range(2)]
            both = [s + pltpu.roll(s, HEAD_DIM, 1) for s in per_kv]
            return jnp.where(lo, both[0], both[1])

        rp = pl.multiple_of(jnp.maximum(i - 1, 0) * QB, QB)
        rc = pl.multiple_of(i * QB, QB)
        dk_acc[pl.ds(rp, QB), :] += key_grad(dzb, qs, 0)
        dv_acc[pl.ds(rp, QB), :] += key_grad(prb, dos, 0)
        dk_acc[pl.ds(rc, QB), :] += key_grad(dzb, qs, 1)
        dv_acc[pl.ds(rc, QB), :] += key_grad(prb, dos, 1)

        @pl.when(i == nq - 1)
        def _():
            dk_ref[...] = dk_acc[...].astype(BF16)
            dv_ref[...] = dv_acc[...].astype(BF16)

    kv = lambda col, prev: pl.BlockSpec((QB, KV_W), (lambda i: (jnp.maximum(i - 1, 0), col)) if prev else (lambda i: (i, col)))
    full = pl.BlockSpec((QB, QB), lambda i: (0, 0))
    smem = pl.BlockSpec(memory_space=pltpu.SMEM)
    whole = lambda shape: pl.BlockSpec(shape, lambda i: (0,) * len(shape))
    return _call(
        body, name=name, grid=(nq,),
        out_shape=(S((T, SWA_W), BF16), S((T, KV_W), BF16), S((T, KV_W), BF16), S((1, QB), F32), S((8, 2, QB, QB), F32)),
        in_specs=[pl.BlockSpec((QB, SWA_W), lambda i: (i, 3)), kv(kcol, True), kv(kcol, False), kv(vcol, True), kv(vcol, False),
                  pl.BlockSpec((QB, SWA_W), lambda i: (i, 0)), pl.BlockSpec((QB, 8 * QB), lambda i: (i, 0)),
                  full, full, smem, smem],
        out_specs=(pl.BlockSpec((QB, SWA_W), lambda i: (i, 0)), whole((T, KV_W)), whole((T, KV_W)), whole((1, QB)),
                   whole((8, 2, QB, QB))),
        scratch=[pltpu.VMEM((8, 2, QB, QB), F32), pltpu.VMEM((T, KV_W), F32), pltpu.VMEM((T, KV_W), F32)],
        sem=("arbitrary",), args=(p, p, p, p, p, do, lse, bprev, bcur, sinks, rel_bias), riders=riders)


def mix_out_fwd(o_sb, o_sw, g_sb, g_sw, wout, h, g_next, name, riders=()):
    T, D = h.shape
    M = SB_W + SWA_W
    tm = _tile(T, 256)

    def body(a_ref, b_ref, ga_ref, gb_ref, w_ref, h_ref, gn_ref, mx_ref, o_ref, n_ref):
        mx_ref[:, :SB_W] = _rms(a_ref[...], ga_ref[...]).astype(BF16)
        mx_ref[:, SB_W:] = _rms(b_ref[...], gb_ref[...]).astype(BF16)
        out = h_ref[...] + _nn(mx_ref[...], w_ref[...])
        o_ref[...] = out
        n_ref[...] = _rms(out, gn_ref[...]).astype(BF16)

    row = lambda n: pl.BlockSpec((tm, n), lambda i: (i, 0))
    (g_sb, sb_spec), (g_sw, sw_spec), (g_next, next_spec) = _gain(g_sb), _gain(g_sw), _gain(g_next)
    return _call(
        body, name=name, grid=(T // tm,), out_shape=(S((T, M), BF16), S((T, D), F32), S((T, D), BF16)),
        in_specs=[row(SB_W), row(SWA_W), sb_spec, sw_spec, pl.BlockSpec((M, D), lambda i: (0, 0)), row(D), next_spec],
        out_specs=(row(M), row(D), row(D)),
        sem=("parallel",), args=(o_sb, o_sw, g_sb, g_sw, wout, h, g_next), riders=riders)


def loss_head(h, g, target, name):
    T, D = h.shape
    tm = _tile(T, 256)

    def body(h_ref, g_ref, t_ref, loss_ref, dh_ref, dhb_ref, dg_ref):
        @pl.when(pl.program_id(0) == 0)
        def _():
            loss_ref[...] = jnp.zeros_like(loss_ref)
            dg_ref[...] = jnp.zeros_like(dg_ref)
        x = h_ref[...]
        err = _rms(x, g_ref[...]) - t_ref[...]
        loss_ref[...] += jnp.full((1, QB), 0.5 * jnp.sum(jnp.mean(err * err, axis=-1)), F32)
        dx, dg = _rms_bwd(err / D, x, g_ref[...])
        dh_ref[...] = dx
        dhb_ref[...] = dx.astype(BF16)
        dg_ref[...] += dg

    row = pl.BlockSpec((tm, D), lambda i: (i, 0))
    vec = pl.BlockSpec((1, D), lambda i: (0, 0))
    return pl.pallas_call(
        body, name=name, grid=(T // tm,), out_shape=(S((1, QB), F32), S((T, D), F32), S((T, D), BF16), S((1, D), F32)),
        in_specs=[row, vec, row], out_specs=(pl.BlockSpec((1, QB), lambda i: (0, 0)), row, row, vec),
        compiler_params=_params(("arbitrary",)),
    )(h, g, target)


def ffn_down_bwd(dhb, wd, gate, up, a, n, name, riders=()):
    T, D = dhb.shape
    F = wd.shape[0]
    tr, tn = _tile(T, 512), _tile(F, 256)

    def body(d_ref, n_ref, w_ref, g_ref, u_ref, a_ref, o_ref, dwd_ref, dwdb_ref, dwgu_ref, dwgub_ref):
        w = w_ref[...]
        for r in range(T // tr):
            rows = slice(r * tr, (r + 1) * tr)
            da = 0.5 * _nt(d_ref[rows, :], w)
            o_ref[0, rows, :] = (da * g_ref[rows, :].astype(F32)).astype(BF16)
            o_ref[1, rows, :] = (da * u_ref[rows, :].astype(F32)).astype(BF16)
        dwd = 0.5 * _tn(a_ref[...], d_ref[...])
        dwd_ref[...] = dwd
        dwdb_ref[...] = dwd.astype(BF16)
        for s in range(2):
            dwgu = _tn(o_ref[s], n_ref[...])
            dwgu_ref[s] = dwgu
            dwgub_ref[s] = dwgu.astype(BF16)

    tile = pl.BlockSpec((T, tn), lambda j: (0, j))
    whole = pl.BlockSpec((T, D), lambda j: (0, 0))
    rows1, rows2 = pl.BlockSpec((tn, D), lambda j: (j, 0)), pl.BlockSpec((2, tn, D), lambda j: (0, j, 0))
    return _call(
        body, name=name, grid=(F // tn,),
        out_shape=(S((2, T, F), BF16), S((F, D), F32), S((F, D), BF16), S((2, F, D), F32), S((2, F, D), BF16)),
        in_specs=[whole, whole, rows1, tile, tile, tile],
        out_specs=(pl.BlockSpec((2, T, tn), lambda j: (0, 0, j)), rows1, rows1, rows2, rows2),
        sem=("parallel",), args=(dhb, n, wd, gate, up, a), riders=riders)


def tn_matmul(xs, y, alpha, name, riders=()):
    B, T, N = xs.shape
    D = y.shape[1]
    tn = _tile(N, 256)

    def body(x_ref, y_ref, o_ref, ob_ref):
        o = alpha * _tn(x_ref[...], y_ref[...])
        o_ref[...] = o
        ob_ref[...] = o.astype(BF16)

    tile = pl.BlockSpec((None, tn, D), lambda s, j: (s, j, 0))
    return _call(
        body, name=name, grid=(B, N // tn), out_shape=(S((B, N, D), F32), S((B, N, D), BF16)),
        in_specs=[pl.BlockSpec((None, T, tn), lambda s, j: (s, 0, j)), pl.BlockSpec((T, D), lambda s, j: (0, 0))],
        out_specs=(tile, tile), sem=("parallel", "parallel"), args=(xs, y), riders=riders)


def nn_rms_bwd(xs, ws, h_in, g, dh, name, riders=()):
    B, T, K = xs.shape
    D = ws.shape[2]
    tm = _tile(T, 256)

    def body(x_ref, w_ref, h_ref, g_ref, d_ref, o_ref, ob_ref, dg_ref):
        @pl.when(pl.program_id(0) == 0)
        def _():
            dg_ref[...] = jnp.zeros_like(dg_ref)
        dn = _nn(x_ref[0], w_ref[0])
        for s in range(1, B):
            dn = dn + _nn(x_ref[s], w_ref[s])
        dx, dg = _rms_bwd(dn, h_ref[...], g_ref[...])
        out = d_ref[...] + dx
        o_ref[...] = out
        ob_ref[...] = out.astype(BF16)
        dg_ref[...] += dg

    row = pl.BlockSpec((tm, D), lambda i: (i, 0))
    vec = pl.BlockSpec((1, D), lambda i: (0, 0))
    g, g_spec = _gain(g)
    return _call(
        body, name=name, grid=(T // tm,), out_shape=(S((T, D), F32), S((T, D), BF16), S((1, D), F32)),
        in_specs=[pl.BlockSpec((B, tm, K), lambda i: (0, i, 0)), pl.BlockSpec((B, K, D), lambda i: (0, 0, 0)), row, g_spec, row],
        out_specs=(row, row, vec),
        sem=("arbitrary",), args=(xs, ws, h_in, g, dh), riders=riders)


def mix_out_bwd(dhb, wout, mixed, o_sb, o_sw, g_sb, g_sw, name):
    T, D = dhb.shape
    M = SB_W + SWA_W
    tm = _tile(T, 256)
    steps = T // tm

    def body(d_ref, w_ref, mx_ref, a_ref, b_ref, ga_ref, gb_ref, da_ref, db_ref, dga_ref, dgb_ref, dw_ref, dwb_ref):
        i = pl.program_id(0)

        @pl.when(i == 0)
        def _():
            dga_ref[...] = jnp.zeros_like(dga_ref)
            dgb_ref[...] = jnp.zeros_like(dgb_ref)
            dw_ref[...] = jnp.zeros_like(dw_ref)
        dm = _nt(d_ref[...], w_ref[...])
        dxa, dga = _rms_bwd(dm[:, :SB_W], a_ref[...], ga_ref[...])
        dxb, dgb = _rms_bwd(dm[:, SB_W:], b_ref[...], gb_ref[...])
        da_ref[...] = dxa
        db_ref[...] = dxb
        dga_ref[...] += dga
        dgb_ref[...] += dgb
        dw_ref[...] += _tn(mx_ref[...], d_ref[...])

        @pl.when(i == steps - 1)
        def _():
            dwb_ref[...] = dw_ref[...].astype(BF16)

    row = lambda n: pl.BlockSpec((tm, n), lambda i: (i, 0))
    vec = lambda n: pl.BlockSpec((1, n), lambda i: (0, 0))
    whole = pl.BlockSpec((M, D), lambda i: (0, 0))
    (g_sb, sb_spec), (g_sw, sw_spec) = _gain(g_sb), _gain(g_sw)
    return pl.pallas_call(
        body, name=name, grid=(steps,),
        out_shape=(S((T, SB_W), F32), S((T, SWA_W), F32), S((1, SB_W), F32), S((1, SWA_W), F32), S((M, D), F32), S((M, D), BF16)),
        in_specs=[row(D), whole, row(M), row(SB_W), row(SWA_W), sb_spec, sw_spec],
        out_specs=(row(SB_W), row(SWA_W), vec(SB_W), vec(SWA_W), whole, whole),
        compiler_params=_params(("arbitrary",)),
    )(dhb, wout, mixed, o_sb, o_sw, g_sb, g_sw)


def rel_bias_grad(dscs, bprev, bcur, name):
    n = len(dscs)

    def body(*refs):
        bp_ref, bc_ref, o_ref = refs[n], refs[n + 1], refs[n + 2]
        bks = [bp_ref[...], bc_ref[...]]
        row = lax.broadcasted_iota(jnp.int32, (N_BUCKETS, QB), 0)
        lane = lax.broadcasted_iota(jnp.int32, (N_BUCKETS, QB), 1)
        out = jnp.zeros((N_BUCKETS, QB), F32)
        for h in range(8):
            tot = [sum(refs[l][h, b] for l in range(n)) for b in range(2)]
            for b in range(N_BUCKETS):
                val = jnp.sum(jnp.where(bks[0] == b, tot[0], 0.0)) + jnp.sum(jnp.where(bks[1] == b, tot[1], 0.0))
                out = jnp.where((row == b) & (lane == h), val, out)
        o_ref[...] = out

    return pl.pallas_call(body, name=name, out_shape=S((N_BUCKETS, QB), F32), compiler_params=_params())(*dscs, bprev, bcur)


def _adamw(w, g, m, v):
    m = ADAM_B1 * m + (1.0 - ADAM_B1) * g
    v = ADAM_B2 * v + (1.0 - ADAM_B2) * (g * g)
    m_hat = m / (1.0 - ADAM_B1 ** ADAM_STEP)
    v_hat = v / (1.0 - ADAM_B2 ** ADAM_STEP)
    delta = -ADAM_LR * (m_hat / (jnp.sqrt(v_hat) + ADAM_EPS) + ADAM_WD * w)
    return delta, m, v


def adamw_scattered(w, m, v, owns, others, name, riders=(), rows=176):
    L, R, C = w.shape
    tr = _rows_tile(R, rows)

    def body(w_ref, m_ref, v_ref, *rest):
        own_refs, other_refs = rest[:L], rest[L:2 * L]
        g_ref, d_ref, mo_ref, vo_ref = rest[2 * L:]
        layer = pl.program_id(0)

        def grad(k):
            o = other_refs[k]
            return own_refs[k][...] + o[0].astype(F32) + o[1].astype(F32) + o[2].astype(F32)

        g = grad(0)
        for k in range(1, L):
            g = jnp.where(layer == k, grad(k), g)
        d, mn, vn = _adamw(w_ref[...], g, m_ref[...], v_ref[...])
        g_ref[...] = g
        d_ref[...] = d
        mo_ref[...] = mn
        vo_ref[...] = vn

    tile = pl.BlockSpec((None, tr, C), lambda l, i: (l, i, 0))
    return _call(
        body, name=name, grid=(L, R // tr), out_shape=(S((L, R, C), F32),) * 4,
        in_specs=[tile] * 3 + [pl.BlockSpec((tr, C), lambda l, i: (i, 0))] * L + [pl.BlockSpec((3, tr, C), lambda l, i: (0, i, 0))] * L,
        out_specs=(tile,) * 4, sem=("parallel", "parallel"), args=(w, m, v, *owns, *others), riders=riders)


def adamw_small(w, gs, m, v, name):
    R, C = w.shape

    def body(w_ref, g_ref, m_ref, v_ref, go_ref, d_ref, mo_ref, vo_ref):
        g = g_ref[0]
        for k in range(1, N_DEV):
            g = g + g_ref[k]
        d, mn, vn = _adamw(w_ref[...], g, m_ref[...], v_ref[...])
        go_ref[...] = g
        d_ref[...] = d
        mo_ref[...] = mn
        vo_ref[...] = vn

    return pl.pallas_call(body, name=name, out_shape=(S((R, C), F32),) * 4, compiler_params=_params())(w, gs, m, v)


def kernel(x, norm_ffn1, w_ffn1_gu, w_ffn1_down, norm_mix, w_in, sinks, norm_out_sb, norm_out_swa, w_out, norm_ffn2, w_ffn2_gu, w_ffn2_down, rel_bias, norm_final, loss_target, m_norm_ffn1, m_w_ffn1_gu, m_w_ffn1_down, m_norm_mix, m_w_in, m_sinks, m_norm_out_sb, m_norm_out_swa, m_w_out, m_norm_ffn2, m_w_ffn2_gu, m_w_ffn2_down, m_rel_bias, m_norm_final, v_norm_ffn1, v_w_ffn1_gu, v_w_ffn1_down, v_norm_mix, v_w_in, v_sinks, v_norm_out_sb, v_norm_out_swa, v_w_out, v_norm_ffn2, v_w_ffn2_gu, v_w_ffn2_down, v_rel_bias, v_norm_final):
    L = norm_ffn1.shape[0]
    T, D = x.shape[1], x.shape[2]
    F = w_ffn1_down.shape[1] * N_DEV
    h = x.reshape(T, D)
    target = loss_target.reshape(T, D)
    after, upto, before = _tri_consts()
    bprev, bcur = _t5_buckets()

    local = {}
    for l in range(L):
        local[f"gu1_{l}"] = w_ffn1_gu[l].T.astype(BF16)
        local[f"d1_{l}"] = w_ffn1_down[l].astype(BF16)
        local[f"in_{l}"] = w_in[l].T.astype(BF16)
        local[f"out_{l}"] = w_out[l].astype(BF16)
        local[f"gu2_{l}"] = w_ffn2_gu[l].T.astype(BF16)
        local[f"d2_{l}"] = w_ffn2_down[l].astype(BF16)
    full, partial = {}, {}
    grads, chip_sum, recv_b = {}, {}, {}

    def run(fn, *args, ag=(), rs1=(), rs2=()):
        halves = lambda names: [n if isinstance(n, tuple) else (n, None) for n in names]
        ag, rs2 = [(n, k) for n, k in halves(ag) if n in local], halves(rs2)
        rows = lambda k, total: None if k is None else (k * (total // 2), total // 2)

        def second(n, k):
            sb = chip_sum[n][1]
            return scatter_second(sb, rows(k, sb.shape[1]), recv_b.get(n))

        riders = ([gather(local[n], rows(k, local[n].shape[0]), partial.get(n)) for n, k in ag]
                  + [scatter_first(grads[n][1]) for n in rs1] + [second(n, k) for n, k in rs2])
        if not riders:
            return fn(*args)
        outs, per = fn(*args, riders=riders)
        per = [p[0] for p in per]
        for n, k in ag:
            buf = per.pop(0)
            if k == 0:
                partial[n] = buf
            else:
                full[n] = buf.reshape(N_DEV * buf.shape[1], D)
        if rs1:
            sums = scatter_add([grads[n][0] for n in rs1], [per.pop(0) for n in rs1], "rs_add_" + "_".join(rs1))
            chip_sum.update(zip(rs1, sums))
        for n, _ in rs2:
            recv_b[n] = per.pop(0)
        return outs

    def attn_fwd(p, sink, name, riders=()):
        return side_by_side(sb_attn_fwd(p, after, name, riders=PARTS), swa_fwd(p, sink, rel_bias, bprev, bcur, name, riders=PARTS),
                            name, riders)

    def attn_bwd(p, do_sb, tot, do_sw, lse, sink, name, riders=()):
        return side_by_side(sb_attn_bwd(p, do_sb, tot, upto, before, name, riders=PARTS),
                            swa_bwd(p, do_sw, lse, sink, rel_bias, bprev, bcur, name, riders=PARTS), name, riders)

    gu = lambda n: full[n].reshape(2, F, D)
    slots = lambda pair: tuple(t.reshape(N_DEV, -1, D) for t in pair)
    vec = lambda a: a.reshape(1, -1)

    PW = max(D, SB_W + SWA_W)
    n_rows = 4 * L + 2
    n_rows += (-n_rows) % 8

    def pack(ffn1, mix, ffn2, final, osb, osw, snk, rel, extra):
        pieces = []

        def row(*parts):
            flat = [a.reshape(-1) for a in parts]
            pieces.extend(flat)
            used = sum(a.size for a in flat)
            if used < PW:
                pieces.append(jnp.zeros((PW - used,), F32))

        for group in (ffn1, mix, ffn2):
            for l in range(L):
                row(group[l])
        row(final)
        for l in range(L):
            row(osb[l], osw[l])
        row(*[snk[l].reshape(-1)[:8] for l in range(L)], rel, extra)
        pieces.append(jnp.zeros(((n_rows - 4 * L - 2) * PW,), F32))
        return jnp.concatenate(pieces).reshape(n_rows, PW)

    def unpack(arr):
        ffn1, mix, ffn2 = arr[0:L, :D], arr[L:2 * L, :D], arr[2 * L:3 * L, :D]
        final = arr[3 * L, :D]
        ob = arr[3 * L + 1:4 * L + 1]
        tail = arr[4 * L + 1]
        return (ffn1, mix, tail[:8 * L].reshape(L, 8), ob[:, :SB_W], ob[:, SB_W:SB_W + SWA_W], ffn2,
                tail[8 * L:8 * L + N_BUCKETS * 8].reshape(N_BUCKETS, 8), final)

    zero = jnp.zeros((1,), F32)
    w_small = pack(norm_ffn1, norm_mix, norm_ffn2, norm_final, norm_out_sb, norm_out_swa, sinks, rel_bias, zero)
    g_ffn1, g_mix, g_ffn2, g_osb, g_osw = [a.reshape(L, 1, -1) for a in (norm_ffn1, norm_mix, norm_ffn2, norm_out_sb, norm_out_swa)]

    saved = []
    n_next = run(rms_cast, h, (g_ffn1, 0), "rms_first", ag=("gu1_0",))
    for l in range(L):
        nx = l + 1
        s = {"h0": h, "n1": n_next}
        s["gate1"], s["up1"], s["a1"] = run(ffn_up_fwd, s["n1"], gu(f"gu1_{l}"), f"ffn1_up{l}",
                                            ag=(f"d1_{l}", ("in_0", 0) if l == 0 else (f"in_{l}", 1)))
        h = run(ffn_down_fwd, s["a1"], full[f"d1_{l}"], h, None, f"ffn1_down{l}", ag=(("in_0", 1),) if l == 0 else ())
        s["h1"] = h
        s["n2"], s["p"] = mix_in_fwd(h, (g_mix, l), full[f"in_{l}"], f"mix_in{l}")
        s["o_sb"], s["tot"], s["o_sw"], s["lse"] = run(attn_fwd, s["p"], (sinks, l), f"attn_fwd{l}",
                                                       ag=(f"out_{l}", f"gu2_{l}", f"d2_{l}", (f"gu1_{nx}", 0)))
        s["mixed"], h, s["n3"] = run(mix_out_fwd, s["o_sb"], s["o_sw"], (g_osb, l), (g_osw, l),
                                     full[f"out_{l}"], h, (g_ffn2, l), f"mix_out{l}")
        s["h2"] = h
        s["gate2"], s["up2"], s["a2"] = run(ffn_up_fwd, s["n3"], gu(f"gu2_{l}"), f"ffn2_up{l}",
                                            ag=((f"gu1_{nx}", 1), (f"in_{nx}", 0)))
        if nx < L:
            h, n_next = run(ffn_down_fwd, s["a2"], full[f"d2_{l}"], h, (g_ffn1, nx), f"ffn2_down{l}")
        else:
            h = run(ffn_down_fwd, s["a2"], full[f"d2_{l}"], h, None, f"ffn2_down{l}")
        saved.append(s)

    loss_part, dh, dhb, dg_final = loss_head(h, vec(norm_final), target, "loss_head")

    small = {k: [None] * L for k in ("ffn1", "mix", "sinks", "osb", "osw", "ffn2", "dsc")}
    for l in reversed(range(L)):
        s = saved[l]

        def ffn_bwd(dh, dhb, tag, gate, up, a, n, h_in, g, r_down, r_up):
            gu_n, d_n = f"gu{tag}_{l}", f"d{tag}_{l}"
            dgu, dwd, dwdb, dwgu, dwgub = run(ffn_down_bwd, dhb, full[d_n], gate, up, a, n, f"ffn{tag}_down_bwd{l}", **r_down)
            grads[gu_n], grads[d_n] = slots((dwgu, dwgub)), slots((dwd, dwdb))
            return run(nn_rms_bwd, dgu, gu(gu_n), h_in, g, dh, f"ffn{tag}_up_bwd{l}", **r_up)

        later = l + 1 < L
        dh, dhb, small["ffn2"][l] = ffn_bwd(dh, dhb, 2, s["gate2"], s["up2"], s["a2"], s["n3"], s["h2"], (g_ffn2, l),
                                            dict(rs2=((f"gu1_{l + 1}", 0), f"d1_{l + 1}") if later else ()),
                                            dict(rs1=(f"gu2_{l}", f"d2_{l}")))
        do_sb, do_sw, small["osb"][l], small["osw"][l], dw_out, dw_out_b = mix_out_bwd(
            dhb, full[f"out_{l}"], s["mixed"], s["o_sb"], s["o_sw"], (g_osb, l), (g_osw, l), f"mix_out_bwd{l}")
        grads[f"out_{l}"] = slots((dw_out, dw_out_b))
        dq_sb, dk_sb, dv_sb, dq_sw, dk_sw, dv_sw, small["sinks"][l], small["dsc"][l] = run(
            attn_bwd, s["p"], do_sb, s["tot"], do_sw, s["lse"], (sinks, l), f"attn_bwd{l}",
            rs2=(f"gu2_{l}", f"d2_{l}") + (((f"gu1_{l + 1}", 1),) if later else ()), rs1=(f"out_{l}",))
        dp = jnp.concatenate([dq_sb, dk_sb, dv_sb, dq_sw, dk_sw, dv_sw], axis=1)
        dh, dhb, small["mix"][l] = nn_rms_bwd(dp[None], full[f"in_{l}"][None], s["h1"], (g_mix, l), dh, f"mix_in_bwd{l}")
        grads[f"in_{l}"] = slots(tn_matmul(dp[None], s["n2"], 1.0, f"dwin{l}"))
        dh, dhb, small["ffn1"][l] = ffn_bwd(dh, dhb, 1, s["gate1"], s["up1"], s["a1"], s["n1"], s["h0"], (g_ffn1, l),
                                            dict(rs1=(f"in_{l}",), rs2=(f"out_{l}",)),
                                            dict(rs1=(f"gu1_{l}", f"d1_{l}"), rs2=(f"in_{l}",)))

    grad_x = dh.reshape(x.shape)

    upd = {}
    turn_of = lambda transposed: (lambda a: jnp.swapaxes(a, 1, 2)) if transposed else (lambda a: a)

    def update(nm, w, m, v, transposed, riders=()):
        turn = turn_of(transposed)
        names = [f"{nm}_{l}" for l in range(L)]
        return adamw_scattered(turn(w), turn(m), turn(v), [chip_sum[n][0] for n in names], [recv_b[n] for n in names],
                               f"adamw_{nm}", riders=riders, rows=88 if riders is PARTS else 176)

    early = (("gu2", w_ffn2_gu, m_w_ffn2_gu, v_w_ffn2_gu, True), ("d2", w_ffn2_down, m_w_ffn2_down, v_w_ffn2_down, False),
             ("in", w_in, m_w_in, v_w_in, True), ("out", w_out, m_w_out, v_w_out, False))
    res = run(lambda name, riders=(): in_one_call([update(*e, riders=PARTS) for e in early], name, riders),
              "adamw_early", rs2=("gu1_0", "d1_0"))
    for k, e in enumerate(early):
        upd[e[0]] = tuple(turn_of(e[4])(r) for r in res[4 * k:4 * k + 4])
    for e in (("gu1", w_ffn1_gu, m_w_ffn1_gu, v_w_ffn1_gu, True), ("d1", w_ffn1_down, m_w_ffn1_down, v_w_ffn1_down, False)):
        upd[e[0]] = tuple(turn_of(e[4])(r) for r in update(*e))

    d_rel = rel_bias_grad(small["dsc"], bprev, bcur, "rel_bias_grad")[:, :8]
    g_small = pack(small["ffn1"], small["mix"], small["ffn2"], dg_final, small["osb"], small["osw"], small["sinks"], d_rel,
                   loss_part[0, :1])
    m_small = pack(m_norm_ffn1, m_norm_mix, m_norm_ffn2, m_norm_final, m_norm_out_sb, m_norm_out_swa, m_sinks, m_rel_bias, zero)
    v_small = pack(v_norm_ffn1, v_norm_mix, v_norm_ffn2, v_norm_final, v_norm_out_sb, v_norm_out_swa, v_sinks, v_rel_bias, zero)
    gs_small = all_gather_rows(g_small, "ag_small")
    summed = adamw_small(w_small, gs_small, m_small, v_small, "adamw_small")
    small_out = [unpack(a) for a in summed]
    loss = summed[0][4 * L + 1, 8 * L + N_BUCKETS * 8]

    def group(k):
        sm = small_out[k]
        return (sm[0], upd["gu1"][k], upd["d1"][k], sm[1], upd["in"][k], sm[2], sm[3], sm[4], upd["out"][k], sm[5],
                upd["gu2"][k], upd["d2"][k], sm[6], sm[7])

    return (loss, grad_x, *group(0), *group(1), *group(2), *group(3))
```

```python
import math

import jax
import jax.numpy as jnp
from jax import lax
from jax.experimental import pallas as pl
from jax.experimental.pallas import tpu as pltpu

F32 = jnp.float32
BF16 = jnp.bfloat16
S = jax.ShapeDtypeStruct

N_DEV = 8
HEAD_DIM = 64
SB_HEADS = 8
PAIR = 2 * HEAD_DIM
SB_W = 512
SWA_W = 512
KV_W = 128
IN_W = 3 * SB_W + SWA_W + 2 * KV_W
QB = 128
N_BUCKETS = 32
MAX_DISTANCE = 128
EPS = 1e-6
NEG_INF = -1e30
SCALE = HEAD_DIM ** -0.5

ADAM_LR = 0.001
ADAM_B1 = 0.9
ADAM_B2 = 0.999
ADAM_EPS = 1e-08
ADAM_WD = 0.01
ADAM_STEP = 10

VMEM_LIMIT = 56 * 1024 * 1024
MESH = pl.DeviceIdType.MESH


def _params(sem=None, vmem=VMEM_LIMIT):
    return pltpu.CompilerParams(dimension_semantics=sem, vmem_limit_bytes=vmem)


def _nn(a, b):
    return jnp.dot(a, b, preferred_element_type=F32)


def _nt(a, b):
    return lax.dot_general(a, b, (((1,), (1,)), ((), ())), preferred_element_type=F32)


def _tn(a, b):
    return lax.dot_general(a, b, (((0,), (0,)), ((), ())), preferred_element_type=F32)


def _tri(xs, m):
    return [_nn(x.astype(BF16), m) for x in xs]


def _rms(x, g):
    r = lax.rsqrt(jnp.mean(x * x, axis=-1, keepdims=True) + EPS)
    return x * r * g


def _rms_bwd(dy, x, g):
    r = lax.rsqrt(jnp.mean(x * x, axis=-1, keepdims=True) + EPS)
    xhat = x * r
    u = dy * g
    dx = r * (u - xhat * jnp.mean(u * xhat, axis=-1, keepdims=True))
    return dx, jnp.sum(dy * xhat, axis=0, keepdims=True)


def _softplus_logsig(z):
    sp = jnp.maximum(z, 0.0) + jnp.log(1.0 + jnp.exp(-jnp.abs(z)))
    return sp, z - sp


def _gain(g):
    if isinstance(g, tuple):
        rows, n = g
        return rows, pl.BlockSpec((None, 1, rows.shape[2]), lambda *_: (n, 0, 0))
    return g, pl.BlockSpec((1, g.shape[1]), lambda *_: (0, 0))


def _tile(n, want):
    t = min(n, want)
    while n % t:
        t //= 2
    return t


def _place():
    x, y, c = lax.axis_index("x"), lax.axis_index("y"), lax.axis_index("c")
    chips = [(1 - x, y), (x, 1 - y), (1 - x, 1 - y)]
    return x, y, c, chips


def all_gather_rows(v, name):
    R, C = v.shape

    def body(v_ref, out_ref, send_sems, recv_sems, local_sem):
        x, y, c, chips = _place()
        me, sibling = (x, y, c), (x, y, 1 - c)

        def slot(px, py, pc):
            return out_ref.at[4 * px + 2 * py + pc]

        def copy(k, block, to, src=None):
            return pltpu.make_async_remote_copy(
                src_ref=slot(*block) if src is None else src, dst_ref=slot(*block),
                send_sem=send_sems.at[k], recv_sem=recv_sems.at[k], device_id=to, device_id_type=MESH)

        mine = pltpu.make_async_copy(v_ref, slot(*me), local_sem)
        mine.start()
        first = [copy(0, me, sibling, src=v_ref)]
        first += [copy(1 + j, me, (*chip, c), src=v_ref) for j, chip in enumerate(chips)]
        for cp in first:
            cp.start()
        passed = [copy(4 + j, (*chip, c), sibling) for j, chip in enumerate(chips)]
        for j, chip in enumerate(chips):
            copy(1 + j, (*chip, c), me).wait_recv()
            passed[j].start()
        copy(0, sibling, me).wait_recv()
        for j, chip in enumerate(chips):
            copy(4 + j, (*chip, 1 - c), me).wait_recv()
        for cp in first + passed:
            cp.wait_send()
        mine.wait()

    return pl.pallas_call(
        body, name=name, out_shape=S((N_DEV, R, C), v.dtype),
        in_specs=[pl.BlockSpec(memory_space=pl.ANY)], out_specs=pl.BlockSpec(memory_space=pl.ANY),
        scratch_shapes=[pltpu.SemaphoreType.DMA((7,)), pltpu.SemaphoreType.DMA((7,)), pltpu.SemaphoreType.DMA],
    )(v)


class _Exchange:
    def __init__(self, ins, outs, sizes, n_local, plan, aliases=None):
        self.ins, self.outs, self.plan, self.aliases = list(ins), list(outs), plan, aliases or {}
        self.sizes, self.n_local = list(sizes), n_local

    def scratch(self):
        n = sum(self.sizes)
        return [pltpu.SemaphoreType.DMA((n,)), pltpu.SemaphoreType.DMA((n,)), pltpu.SemaphoreType.DMA((max(self.n_local, 1),))]

    def _copies(self, in_refs, out_refs, sems):
        send_sems, recv_sems, local_sems = sems
        phases, local = self.plan(in_refs, out_refs)
        out, k = [], 0
        for phase in phases:
            out.append([pltpu.make_async_remote_copy(src_ref=s, dst_ref=d, send_sem=send_sems.at[k + n], recv_sem=recv_sems.at[k + n],
                                                     device_id=dev, device_id_type=MESH) for n, (s, d, dev) in enumerate(phase)])
            k += len(phase)
        return out, [pltpu.make_async_copy(s, d, local_sems.at[n]) for n, (s, d) in enumerate(local)]

    def start(self, in_refs, out_refs, sems):
        phases, loc = self._copies(in_refs, out_refs, sems)
        for cp in phases[0] + loc:
            cp.start()

    def advance(self, hook, in_refs, out_refs, sems):
        p = hook - (3 - len(self.sizes))
        if p >= 1:
            phases, _ = self._copies(in_refs, out_refs, sems)
            for cp in phases[p - 1]:
                cp.wait_recv()
            for cp in phases[p]:
                cp.start()

    def finish(self, in_refs, out_refs, sems):
        phases, loc = self._copies(in_refs, out_refs, sems)
        for cp in phases[-1]:
            cp.wait_recv()
        for phase in phases:
            for cp in phase:
                cp.wait_send()
        for cp in loc:
            cp.wait()


def gather(v, rows=None, into=None):
    R, C = v.shape
    r0, nr = rows or (0, R)
    na = min(nr, ((nr // 2 + 15) // 16) * 16)

    def plan(ins, outs):
        x, y, c, _ = _place()
        xn, yn, dg, sibling = (1 - x, y), (x, 1 - y), (1 - x, 1 - y), (x, y, 1 - c)
        slot = lambda chip, start=r0, count=nr: outs[0].at[4 * chip[0] + 2 * chip[1] + c, pl.ds(start, count), :]
        src, mine = ins[0].at[pl.ds(r0, nr), :], slot((x, y))
        same = lambda ref, to: (ref, ref, to)
        first = [(src, mine, sibling), (src, mine, (*xn, c)), (src, mine, (*yn, c))]
        relay = [same(slot(xn, r0, na), (*yn, c)), same(slot(yn, r0 + na, nr - na), (*xn, c))]
        onward = [same(slot(xn), sibling), same(slot(yn), sibling), same(slot(dg), sibling)]
        return [first, relay, onward], [(src, mine)]

    if into is None:
        return _Exchange([v], [S((N_DEV, R, C), v.dtype)], (3, 2, 3), 1, plan)
    return _Exchange([v, into], [S((N_DEV, R, C), v.dtype)], (3, 2, 3), 1, plan, aliases={1: 0})


def scatter_first(gb):
    _, R, C = gb.shape

    def plan(ins, outs):
        x, y, c, chips = _place()
        owners = [(x, y)] + chips
        return [[(ins[0].at[4 * px + 2 * py + (1 - c)], outs[0].at[j], (x, y, 1 - c)) for j, (px, py) in enumerate(owners)]], []

    return _Exchange([gb], [S((4, R, C), BF16)], (4,), 0, plan)


def scatter_second(sb, rows=None, into=None):
    r0, nr = rows or (0, sb.shape[1])

    def plan(ins, outs):
        x, y, c, chips = _place()
        part = lambda ref, j: ref.at[j, pl.ds(r0, nr), :]
        return [[(part(ins[0], j), part(outs[0], j), (*chips[j], c)) for j in range(3)]], []

    if into is None:
        return _Exchange([sb], [S(sb.shape, BF16)], (3,), 0, plan)
    return _Exchange([sb, into], [S(sb.shape, BF16)], (3,), 0, plan, aliases={1: 0})


PARTS = "parts"


def _call(body, *, name, grid, in_specs, out_specs, out_shape, args, scratch=(), sem=None, riders=(), marks=None):
    single = not isinstance(out_shape, (tuple, list))
    out_shape = (out_shape,) if single else tuple(out_shape)
    out_specs = (out_specs,) if single else tuple(out_specs)
    n_in, n_out, n_sc = len(in_specs), len(out_shape), len(scratch)
    if riders is PARTS:
        return dict(body=body, grid=grid, in_specs=list(in_specs), out_specs=out_specs, out_shape=out_shape, args=tuple(args),
                    scratch=list(scratch), marks=marks)
    if not riders:
        res = pl.pallas_call(body, name=name, grid=grid, in_specs=list(in_specs), out_specs=out_specs, out_shape=out_shape,
                             scratch_shapes=list(scratch), compiler_params=_params(sem))(*args)
        return res[0] if single else res
    r_ins = [a for r in riders for a in r.ins]
    r_outs = [o for r in riders for o in r.outs]
    r_scr = [s for r in riders for s in r.scratch()]
    aliases, i0, o0 = {}, n_in, n_out
    for r in riders:
        for a, b in r.aliases.items():
            aliases[i0 + a] = o0 + b
        i0, o0 = i0 + len(r.ins), o0 + len(r.outs)
    steps = math.prod(grid)

    def full(*refs):
        ins, rin = refs[:n_in], refs[n_in:n_in + len(r_ins)]
        pos = n_in + len(r_ins)
        outs, rout = refs[pos:pos + n_out], refs[pos + n_out:pos + n_out + len(r_outs)]
        pos += n_out + len(r_outs)
        sc, rsc = refs[pos:pos + n_sc], refs[pos + n_sc:]
        step = 0
        for d, n in enumerate(grid):
            step = step * n + pl.program_id(d)

        def each(method, *lead):
            i, o = 0, 0
            for k, r in enumerate(riders):
                getattr(r, method)(*lead, rin[i:i + len(r.ins)], rout[o:o + len(r.outs)], rsc[3 * k:3 * k + 3])
                i, o = i + len(r.ins), o + len(r.outs)

        @pl.when(step == 0)
        def _():
            each("start")
        body(*ins, *outs, *sc)

        late = max(steps - 1 - max(steps // 8, 1), 0)
        first, second = marks or (min((3 * steps) // 5, late), late)

        @pl.when(step == first)
        def _():
            each("advance", 1)

        @pl.when(step == second)
        def _():
            each("advance", 2)

        @pl.when(step == steps - 1)
        def _():
            each("finish")

    anywhere = pl.BlockSpec(memory_space=pl.ANY)
    res = pl.pallas_call(
        full, name=name, grid=grid, in_specs=list(in_specs) + [anywhere] * len(r_ins),
        out_specs=out_specs + (anywhere,) * len(r_outs), out_shape=out_shape + tuple(r_outs),
        scratch_shapes=list(scratch) + r_scr, input_output_aliases=aliases,
        compiler_params=_params(("arbitrary",) * len(grid)))(*args, *r_ins)
    host, rest, per = res[:n_out], list(res[n_out:]), []
    for r in riders:
        per.append(rest[:len(r.outs)])
        rest = rest[len(r.outs):]
    return (host[0] if single else tuple(host)), per


def side_by_side(first, second, name, riders=()):
    a_in, a_out, a_sc = len(first["in_specs"]), len(first["out_shape"]), len(first["scratch"])
    n_in, n_out = a_in + len(second["in_specs"]), a_out + len(second["out_shape"])

    def body(*refs):
        ins, outs, sc = refs[:n_in], refs[n_in:n_in + n_out], refs[n_in + n_out:]
        first["body"](*ins[:a_in], *outs[:a_out], *sc[:a_sc])
        second["body"](*ins[a_in:], *outs[a_out:], *sc[a_sc:])

    return _call(body, name=name, grid=first["grid"], in_specs=first["in_specs"] + second["in_specs"],
                 out_specs=first["out_specs"] + second["out_specs"], out_shape=first["out_shape"] + second["out_shape"],
                 args=first["args"] + second["args"], scratch=first["scratch"] + second["scratch"],
                 sem=("arbitrary",) * len(first["grid"]), riders=riders, marks=first["marks"])


def in_one_call(parts, name, riders=()):
    extents = [p["grid"][-1] for p in parts]
    longest = max(extents)

    def clamp(spec, n):
        if n == longest or spec.index_map is None:
            return spec
        return pl.BlockSpec(spec.block_shape, lambda *ids, f=spec.index_map: f(*ids[:-1], jnp.minimum(ids[-1], n - 1)))

    counts = [(len(p["in_specs"]), len(p["out_shape"]), len(p["scratch"])) for p in parts]
    n_in, n_out = sum(c[0] for c in counts), sum(c[1] for c in counts)

    def body(*refs):
        ins, outs, sc = refs[:n_in], refs[n_in:n_in + n_out], refs[n_in + n_out:]
        i = o = s = 0
        for p, n, (ci, co, cs) in zip(parts, extents, counts):
            run_part = lambda p=p, a=ins[i:i + ci], b=outs[o:o + co], c=sc[s:s + cs]: p["body"](*a, *b, *c)
            if n == longest:
                run_part()
            else:
                pl.when(pl.program_id(len(p["grid"]) - 1) < n)(run_part)
            i, o, s = i + ci, o + co, s + cs

    cat = lambda key: [x for p in parts for x in p[key]]
    return _call(body, name=name, grid=parts[0]["grid"][:-1] + (longest,),
                 in_specs=[clamp(sp, n) for p, n in zip(parts, extents) for sp in p["in_specs"]],
                 out_specs=tuple(clamp(sp, n) for p, n in zip(parts, extents) for sp in p["out_specs"]),
                 out_shape=tuple(cat("out_shape")), args=tuple(cat("args")), scratch=cat("scratch"),
                 sem=("arbitrary",) * len(parts[0]["grid"]), riders=riders)


def _rows_tile(n, cap):
    return max(t for t in range(16, min(n, cap) + 1, 16) if n % t == 0)


def scatter_add(gs, ras, name):
    C = gs[0].shape[2]
    trs = [_rows_tile(g.shape[1], 176) for g in gs]
    nts = [g.shape[1] // tr for g, tr in zip(gs, trs)]
    steps = max(nts)
    x, y, c, chips = _place()
    slots = jnp.stack([4 * px + 2 * py + c for px, py in [(x, y)] + chips]).astype(jnp.int32)

    def body(s_ref, *refs):
        ins, outs = refs[:5 * len(gs)], refs[5 * len(gs):]
        for k in range(len(gs)):
            g0, g1, g2, g3, ra_ref = ins[5 * k:5 * k + 5]
            own_ref, sb_ref = outs[2 * k:2 * k + 2]

            def work(g0=g0, g1=g1, g2=g2, g3=g3, ra_ref=ra_ref, own_ref=own_ref, sb_ref=sb_ref):
                own_ref[...] = g0[...] + ra_ref[0].astype(F32)
                for j, gj in enumerate((g1, g2, g3)):
                    sb_ref[j] = (gj[...] + ra_ref[j + 1].astype(F32)).astype(BF16)

            if nts[k] == steps:
                work()
            else:
                pl.when(pl.program_id(0) < nts[k])(work)

    in_specs, out_specs, out_shape, args = [], [], [], [slots]
    for k, (g, ra, tr) in enumerate(zip(gs, ras, trs)):
        tile = lambda i, k=k: jnp.minimum(i, nts[k] - 1)
        in_specs += [pl.BlockSpec((None, tr, C), lambda i, s, j=j, tile=tile: (s[j], tile(i), 0)) for j in range(4)]
        in_specs.append(pl.BlockSpec((4, tr, C), lambda i, s, tile=tile: (0, tile(i), 0)))
        out_specs += [pl.BlockSpec((tr, C), lambda i, s, tile=tile: (tile(i), 0)),
                      pl.BlockSpec((3, tr, C), lambda i, s, tile=tile: (0, tile(i), 0))]
        out_shape += [S((g.shape[1], C), F32), S((3, g.shape[1], C), BF16)]
        args += [g, g, g, g, ra]
    spec = pltpu.PrefetchScalarGridSpec(num_scalar_prefetch=1, grid=(steps,), in_specs=in_specs, out_specs=tuple(out_specs))
    res = pl.pallas_call(body, name=name, grid_spec=spec, out_shape=tuple(out_shape), compiler_params=_params(("arbitrary",)))(*args)
    return [(res[2 * k], res[2 * k + 1]) for k in range(len(gs))]


def rms_cast(h, g, name, riders=()):
    T, D = h.shape
    tm = _tile(T, 512)

    def body(h_ref, g_ref, n_ref):
        n_ref[...] = _rms(h_ref[...], g_ref[...]).astype(BF16)

    row = pl.BlockSpec((tm, D), lambda i: (i, 0))
    g, g_spec = _gain(g)
    return _call(body, name=name, grid=(T // tm,), out_shape=S((T, D), BF16), in_specs=[row, g_spec],
                 out_specs=row, sem=("parallel",), args=(h, g), riders=riders)


def ffn_up_fwd(n, wgu, name, riders=()):
    T, D = n.shape
    F = wgu.shape[1]
    tr, tn = _tile(T, 512), _tile(F, 256)

    def body(n_ref, wg_ref, wu_ref, dgate_ref, dup_ref, a_ref):
        wg, wu = wg_ref[...], wu_ref[...]
        for r in range(T // tr):
            rows = slice(r * tr, (r + 1) * tr)
            x = n_ref[rows, :]
            gate = _nt(x, wg)
            up = _nt(x, wu)
            s = jax.nn.sigmoid(gate)
            silu = gate * s
            dgate_ref[rows, :] = (up * (s * (1.0 + gate * (1.0 - s)))).astype(BF16)
            dup_ref[rows, :] = silu.astype(BF16)
            a_ref[rows, :] = (silu * up).astype(BF16)

    tile = pl.BlockSpec((T, tn), lambda j: (0, j))
    return _call(
        body, name=name, grid=(F // tn,), out_shape=(S((T, F), BF16),) * 3,
        in_specs=[pl.BlockSpec((T, D), lambda j: (0, 0)),
                  pl.BlockSpec((None, tn, D), lambda j: (0, j, 0)), pl.BlockSpec((None, tn, D), lambda j: (1, j, 0))],
        out_specs=(tile, tile, tile), sem=("parallel",), args=(n, wgu, wgu), riders=riders)


def ffn_down_fwd(a, wd, h, g_next, name, riders=()):
    T, F = a.shape
    D = wd.shape[1]
    tm = _tile(T, 256)

    def body(a_ref, w_ref, h_ref, *rest):
        out = h_ref[...] + 0.5 * _nn(a_ref[...], w_ref[...])
        if g_next is None:
            rest[0][...] = out
        else:
            g_ref, o_ref, n_ref = rest
            o_ref[...] = out
            n_ref[...] = _rms(out, g_ref[...]).astype(BF16)

    row = pl.BlockSpec((tm, D), lambda i: (i, 0))
    more = g_next is not None
    g_arg, g_spec = _gain(g_next) if more else (None, None)
    return _call(
        body, name=name, grid=(T // tm,), out_shape=(S((T, D), F32), S((T, D), BF16)) if more else S((T, D), F32),
        in_specs=[pl.BlockSpec((tm, F), lambda i: (i, 0)), pl.BlockSpec((F, D), lambda i: (0, 0)), row] + ([g_spec] if more else []),
        out_specs=(row, row) if more else row,
        sem=("parallel",), args=(a, wd, h) + ((g_arg,) if more else ()), riders=riders)


def mix_in_fwd(h, g, win, name):
    T, D = h.shape
    N = win.shape[0]
    tm = _tile(T, 256)

    def body(h_ref, g_ref, w_ref, n_ref, p_ref):
        n = _rms(h_ref[...], g_ref[...]).astype(BF16)
        n_ref[...] = n
        p_ref[...] = _nt(n, w_ref[...]).astype(BF16)

    g, g_spec = _gain(g)
    return pl.pallas_call(
        body, name=name, grid=(T // tm,), out_shape=(S((T, D), BF16), S((T, N), BF16)),
        in_specs=[pl.BlockSpec((tm, D), lambda i: (i, 0)), g_spec, pl.BlockSpec((N, D), lambda i: (0, 0))],
        out_specs=(pl.BlockSpec((tm, D), lambda i: (i, 0)), pl.BlockSpec((tm, N), lambda i: (i, 0))),
        compiler_params=_params(("parallel",)),
    )(h, g, win)


def _tri_consts():
    r = lax.broadcasted_iota(jnp.int32, (QB, QB), 0)
    c = lax.broadcasted_iota(jnp.int32, (QB, QB), 1)
    ones = jnp.ones((QB, QB), BF16)
    with_sums = lambda tri: jnp.concatenate([tri.astype(BF16), ones], axis=1)
    return with_sums(r > c), with_sums(r <= c), with_sums(r < c)


def _half_masks():
    lane = lax.broadcasted_iota(jnp.int32, (QB, PAIR), 1)
    row = lax.broadcasted_iota(jnp.int32, (QB, PAIR), 0)
    return lane < HEAD_DIM, lane, row


def sb_attn_fwd(p, after, name, riders=()):
    T = p.shape[0]
    nq = T // QB

    def body(q_ref, k_ref, v_ref, m_ref, o_ref, tot_ref, q_sc, acc_ref, z_sc):
        i = pl.program_id(0)
        lo, lane, row = _half_masks()
        causal = lane < row
        heads, pairs = range(SB_HEADS), range(SB_HEADS // 2)
        for hp in pairs:
            q_sc[hp] = (q_ref[:, hp * PAIR:(hp + 1) * PAIR].astype(F32) * SCALE).astype(BF16)
        m2 = m_ref[...]

        def by_head(ref, j, hp):
            t = ref[pl.ds(pl.multiple_of(j * QB, QB), QB), hp * PAIR:(hp + 1) * PAIR]
            return jnp.concatenate([jnp.where(lo, t, 0), jnp.where(lo, 0, t)], axis=0)

        def scores(j):
            return [_nt(q_sc[hp], by_head(k_ref, j, hp)) for hp in pairs]

        def block(j, diag):
            z2 = [z_sc[hp] for hp in pairs]
            ahead = scores(jnp.maximum(j - 1, 0))
            for hp in pairs:
                z_sc[hp] = ahead[hp]
            vs = [by_head(v_ref, j, hp) for hp in pairs]
            spls = [_softplus_logsig(z2[h // 2][:, (h % 2) * QB:(h % 2 + 1) * QB]) for h in heads]
            sp = [jnp.where(causal, spls[h][0], 0.0) if diag else spls[h][0] for h in heads]
            rr = _tri(sp, m2)
            if diag:
                w = [jnp.where(causal, jnp.exp(spls[h][1] - rr[h][:, :QB]), 0.0).astype(BF16) for h in heads]
            else:
                c = [tot_ref[:, h * QB:(h + 1) * QB] for h in heads]
                w = [jnp.exp(spls[h][1] - (c[h] + rr[h][:, :QB])).astype(BF16) for h in heads]
            pv = [_nn(jnp.concatenate([w[2 * hp], w[2 * hp + 1]], axis=1), vs[hp]) for hp in pairs]
            for hp in pairs:
                acc_ref[hp] = pv[hp] if diag else acc_ref[hp] + pv[hp]
            for h in heads:
                tot_ref[:, h * QB:(h + 1) * QB] = rr[h][:, QB:] if diag else c[h] + rr[h][:, QB:]

        first = scores(i)
        for hp in pairs:
            z_sc[hp] = first[hp]
        block(i, True)

        def step(t, carry):
            block(i - 1 - t, False)
            return carry
        lax.fori_loop(0, i, step, 0)
        for hp in pairs:
            o_ref[:, hp * PAIR:(hp + 1) * PAIR] = acc_ref[hp]

    npair = SB_HEADS // 2
    return _call(
        body, name=name, grid=(nq,), out_shape=(S((T, SB_W), F32), S((T, SB_HEADS * QB), F32)),
        in_specs=[pl.BlockSpec((QB, SB_W), lambda i: (i, 0)), pl.BlockSpec((T, SB_W), lambda i: (0, 1)),
                  pl.BlockSpec((T, SB_W), lambda i: (0, 2)), pl.BlockSpec((QB, 2 * QB), lambda i: (0, 0))],
        out_specs=(pl.BlockSpec((QB, SB_W), lambda i: (i, 0)), pl.BlockSpec((QB, SB_HEADS * QB), lambda i: (i, 0))),
        scratch=[pltpu.VMEM((npair, QB, PAIR), BF16), pltpu.VMEM((npair, QB, PAIR), F32), pltpu.VMEM((npair, QB, 2 * QB), F32)],
        sem=("arbitrary",), args=(p, p, p, after), riders=riders,
        marks=((11 * nq) // 16, (14 * nq) // 16))


def sb_attn_bwd(p, do, tot, upto, before, name, riders=()):
    T = p.shape[0]
    nq = T // QB

    def body(q_ref, k_ref, v_ref, do_ref, tot_ref, mp_ref, mg_ref, dq_ref, dk_ref, dv_ref,
             q_sc, d_sc, qd_sc, pg_sc, dq_acc, dk_acc, dv_acc, zd_sc):
        i = pl.program_id(0)
        lo, lane, row = _half_masks()
        causal = lane < row
        heads, pairs = range(SB_HEADS), range(SB_HEADS // 2)

        def by_head(t):
            return jnp.concatenate([jnp.where(lo, t, 0), jnp.where(lo, 0, t)], axis=0)

        for hp in pairs:
            q2 = (q_ref[:, hp * PAIR:(hp + 1) * PAIR].astype(F32) * SCALE).astype(BF16)
            d2 = do_ref[:, hp * PAIR:(hp + 1) * PAIR].astype(BF16)
            q_sc[hp] = q2
            d_sc[hp] = d2
            qd_sc[hp] = by_head(q2)
            qd_sc[SB_HEADS // 2 + hp] = by_head(d2)
        mp, mg = mp_ref[...], mg_ref[...]

        @pl.when(i == 0)
        def _():
            dk_acc[...] = jnp.zeros_like(dk_acc)
            dv_acc[...] = jnp.zeros_like(dv_acc)
        pg_sc[...] = jnp.zeros_like(pg_sc)
        dq_acc[...] = jnp.zeros_like(dq_acc)

        def rows(ref, j, hp):
            return ref[pl.ds(pl.multiple_of(j * QB, QB), QB), hp * PAIR:(hp + 1) * PAIR]

        def products(j):
            return ([_nt(q_sc[hp], by_head(rows(k_ref, j, hp))) for hp in pairs]
                    + [_nt(d_sc[hp], by_head(rows(v_ref, j, hp))) for hp in pairs])

        def block(j, diag):
            r0 = pl.multiple_of(j * QB, QB)
            half = lambda t, h: t[:, (h % 2) * QB:(h % 2 + 1) * QB]
            z = [half(zd_sc[h // 2], h) for h in heads]
            dw = [half(zd_sc[SB_HEADS // 2 + h // 2], h) for h in heads]
            if not diag:
                ahead = products(j + 1)
                for hp in range(SB_HEADS):
                    zd_sc[hp] = ahead[hp]
            ks = [by_head(rows(k_ref, j, hp)) for hp in pairs]
            spls = [_softplus_logsig(z[h]) for h in heads]
            sp = [jnp.where(causal, spls[h][0], 0.0) if diag else spls[h][0] for h in heads]
            rr = _tri(sp, mp)
            pc = [pg_sc[2 * h] for h in heads]
            w = [jnp.exp(spls[h][1] - (tot_ref[:, h * QB:(h + 1) * QB] - (pc[h] + rr[h][:, :QB]))) for h in heads]
            if diag:
                w = [jnp.where(causal, w[h], 0.0) for h in heads]
            gg = [dw[h] * w[h] for h in heads]
            rg = _tri(gg, mg)
            gc = [pg_sc[2 * h + 1] for h in heads]
            dz = [gg[h] - (gg[h] + gc[h] + rg[h][:, :QB]) * jnp.exp(spls[h][1]) for h in heads]
            if diag:
                dz = [jnp.where(causal, dz[h], 0.0) for h in heads]
            dzb = [dz[h].astype(BF16) for h in heads]
            wb = [w[h].astype(BF16) for h in heads]
            both = lambda t, hp, axis: jnp.concatenate([t[2 * hp], t[2 * hp + 1]], axis=axis)
            dq = [_nn(both(dzb, hp, 1), ks[hp]) for hp in pairs]
            dk = [_tn(both(dzb, hp, 0), qd_sc[hp]) for hp in pairs]
            dv = [_tn(both(wb, hp, 0), qd_sc[SB_HEADS // 2 + hp]) for hp in pairs]
            for h in heads:
                if not diag:
                    pg_sc[2 * h] = pc[h] + rr[h][:, QB:]
                    pg_sc[2 * h + 1] = gc[h] + rg[h][:, QB:]
            for hp in pairs:
                dq_acc[hp] += dq[hp]
                dk_acc[pl.ds(r0, QB), hp * PAIR:(hp + 1) * PAIR] += dk[hp]
                dv_acc[pl.ds(r0, QB), hp * PAIR:(hp + 1) * PAIR] += dv[hp]

        first = products(0)
        for hp in range(SB_HEADS):
            zd_sc[hp] = first[hp]

        def step(t, carry):
            block(t, False)
            return carry
        lax.fori_loop(0, i, step, 0)
        block(i, True)
        for hp in pairs:
            dq_ref[:, hp * PAIR:(hp + 1) * PAIR] = (dq_acc[hp] * SCALE).astype(BF16)

        @pl.when(i == nq - 1)
        def _():
            dk_ref[...] = dk_acc[...].astype(BF16)
            dv_ref[...] = dv_acc[...].astype(BF16)

    qtile = pl.BlockSpec((QB, SB_W), lambda i: (i, 0))
    whole = pl.BlockSpec((T, SB_W), lambda i: (0, 0))
    const = pl.BlockSpec((QB, 2 * QB), lambda i: (0, 0))
    return _call(
        body, name=name, grid=(nq,), out_shape=(S((T, SB_W), BF16),) * 3,
        in_specs=[qtile, pl.BlockSpec((T, SB_W), lambda i: (0, 1)), pl.BlockSpec((T, SB_W), lambda i: (0, 2)), qtile,
                  pl.BlockSpec((QB, SB_HEADS * QB), lambda i: (i, 0)), const, const],
        out_specs=(qtile, whole, whole),
        scratch=[pltpu.VMEM((SB_HEADS // 2, QB, PAIR), BF16), pltpu.VMEM((SB_HEADS // 2, QB, PAIR), BF16),
                 pltpu.VMEM((SB_HEADS, 2 * QB, PAIR), BF16),
                 pltpu.VMEM((2 * SB_HEADS, QB, QB), F32), pltpu.VMEM((SB_HEADS // 2, QB, PAIR), F32),
                 pltpu.VMEM((T, SB_W), F32), pltpu.VMEM((T, SB_W), F32), pltpu.VMEM((SB_HEADS, QB, 2 * QB), F32)],
        sem=("arbitrary",), args=(p, p, p, do, tot, upto, before), riders=riders)


def _t5_buckets():
    a = lax.broadcasted_iota(jnp.int32, (QB, QB), 0)
    c = lax.broadcasted_iota(jnp.int32, (QB, QB), 1)

    def bucket(dist):
        dist = jnp.maximum(dist, 0)
        max_exact = N_BUCKETS // 2
        d = jnp.maximum(dist, 1).astype(F32)
        large = max_exact + (jnp.log(d / max_exact) / math.log(MAX_DISTANCE / max_exact)
                             * (N_BUCKETS - max_exact)).astype(jnp.int32)
        large = jnp.minimum(large, N_BUCKETS - 1)
        return jnp.where(dist < max_exact, dist, large)

    return bucket(QB + a - c), bucket(a - c)


def _swa_common(i, kp_ref, kc_ref, vp_ref, vc_ref, bp_ref, bc_ref, rb_ref, bias_ref):
    lo, lane, row = _half_masks()

    @pl.when(i == 0)
    def _():
        for blk, b_ref in enumerate((bp_ref, bc_ref)):
            bk = b_ref[...]
            for h in range(8):
                acc = jnp.zeros((QB, QB), F32)
                for b in range(N_BUCKETS):
                    acc = jnp.where(bk == b, rb_ref[b, h], acc)
                bias_ref[h, blk] = acc

    band = [(lane > row) & (i > 0), lane <= row]

    def stacks(ref):
        t = ref[...].astype(F32)
        sw = pltpu.roll(t, HEAD_DIM, 1)
        return [jnp.concatenate([jnp.where(lo, t, 0.0), jnp.where(lo, 0.0, sw)], axis=0).astype(BF16),
                jnp.concatenate([jnp.where(lo, sw, 0.0), jnp.where(lo, 0.0, t)], axis=0).astype(BF16)]

    ks = [stacks(kp_ref), stacks(kc_ref)]
    vs = [stacks(vp_ref), stacks(vc_ref)]
    return lo, band, ks, vs


def _lane_half(t, h):
    return t[:, (h % 2) * QB:(h % 2 + 1) * QB]


def swa_fwd(p, sinks, rel_bias, bprev, bcur, name, riders=()):
    T = p.shape[0]
    nq = T // QB
    kcol, vcol = (3 * SB_W + SWA_W) // KV_W, (3 * SB_W + SWA_W) // KV_W + 1
    sinks, srow = sinks if isinstance(sinks, tuple) else (sinks, 0)

    def body(q_ref, kp_ref, kc_ref, vp_ref, vc_ref, bp_ref, bc_ref, sink_ref, rb_ref, o_ref, lse_ref, bias_ref):
        i = pl.program_id(0)
        lo, band, ks, vs = _swa_common(i, kp_ref, kc_ref, vp_ref, vc_ref, bp_ref, bc_ref, rb_ref, bias_ref)
        heads, pairs, blocks = range(8), range(4), range(2)
        rowmax = lambda t: jnp.max(t, axis=1, keepdims=True)
        rowsum = lambda t: jnp.sum(t, axis=1, keepdims=True)
        q2 = [q_ref[:, g * PAIR:(g + 1) * PAIR] for g in pairs]
        s2 = [[_nt(q2[g], ks[b][g // 2]) for b in blocks] for g in pairs]
        sc = [[jnp.where(band[b], _lane_half(s2[h // 2][b], h) * SCALE + bias_ref[h, b], NEG_INF) for b in blocks] for h in heads]
        sink = [sink_ref[srow, h] for h in heads]
        m = [jnp.maximum(jnp.maximum(rowmax(sc[h][0]), rowmax(sc[h][1])), sink[h]) for h in heads]
        e = [[jnp.exp(sc[h][b] - m[h]) for b in blocks] for h in heads]
        den = [rowsum(e[h][0]) + rowsum(e[h][1]) + jnp.exp(sink[h] - m[h]) for h in heads]
        pb = [[(e[h][b] / den[h]).astype(BF16) for b in blocks] for h in heads]
        for g in pairs:
            both = lambda b: jnp.concatenate([pb[2 * g][b], pb[2 * g + 1][b]], axis=1)
            o_ref[:, g * PAIR:(g + 1) * PAIR] = _nn(both(0), vs[0][g // 2]) + _nn(both(1), vs[1][g // 2])
        for h in heads:
            lse_ref[:, h * QB:(h + 1) * QB] = jnp.broadcast_to(m[h] + jnp.log(den[h]), (QB, QB))

    kv = lambda col, prev: pl.BlockSpec((QB, KV_W), (lambda i: (jnp.maximum(i - 1, 0), col)) if prev else (lambda i: (i, col)))
    full = pl.BlockSpec((QB, QB), lambda i: (0, 0))
    smem = pl.BlockSpec(memory_space=pltpu.SMEM)
    return _call(
        body, name=name, grid=(nq,), out_shape=(S((T, SWA_W), F32), S((T, 8 * QB), F32)),
        in_specs=[pl.BlockSpec((QB, SWA_W), lambda i: (i, 3)), kv(kcol, True), kv(kcol, False), kv(vcol, True), kv(vcol, False),
                  full, full, smem, smem],
        out_specs=(pl.BlockSpec((QB, SWA_W), lambda i: (i, 0)), pl.BlockSpec((QB, 8 * QB), lambda i: (i, 0))),
        scratch=[pltpu.VMEM((8, 2, QB, QB), F32)],
        sem=("arbitrary",), args=(p, p, p, p, p, bprev, bcur, sinks, rel_bias), riders=riders)


def swa_bwd(p, do, lse, sinks, rel_bias, bprev, bcur, name, riders=()):
    T = p.shape[0]
    nq = T // QB
    kcol, vcol = (3 * SB_W + SWA_W) // KV_W, (3 * SB_W + SWA_W) // KV_W + 1
    sinks, srow = sinks if isinstance(sinks, tuple) else (sinks, 0)

    def body(q_ref, kp_ref, kc_ref, vp_ref, vc_ref, do_ref, lse_ref, bp_ref, bc_ref, sink_ref, rb_ref,
             dq_ref, dk_ref, dv_ref, dsink_ref, dsc_ref, bias_ref, dk_acc, dv_acc):
        i = pl.program_id(0)
        lo, band, ks, vs = _swa_common(i, kp_ref, kc_ref, vp_ref, vc_ref, bp_ref, bc_ref, rb_ref, bias_ref)

        @pl.when(i == 0)
        def _():
            dk_acc[...] = jnp.zeros_like(dk_acc)
            dv_acc[...] = jnp.zeros_like(dv_acc)
            dsc_ref[...] = jnp.zeros_like(dsc_ref)
            dsink_ref[...] = jnp.zeros_like(dsink_ref)

        heads, pairs, blocks = range(8), range(4), range(2)
        rowsum = lambda t: jnp.sum(t, axis=1, keepdims=True)
        by_head = lambda t: jnp.concatenate([jnp.where(lo, t, 0), jnp.where(lo, 0, t)], axis=0)
        q2 = [q_ref[:, g * PAIR:(g + 1) * PAIR] for g in pairs]
        d2 = [do_ref[:, g * PAIR:(g + 1) * PAIR].astype(BF16) for g in pairs]
        qs = [by_head(q2[g]) for g in pairs]
        dos = [by_head(d2[g]) for g in pairs]
        s2 = [[_nt(q2[g], ks[b][g // 2]) for b in blocks] for g in pairs]
        dp2 = [[_nt(d2[g], vs[b][g // 2]) for b in blocks] for g in pairs]
        lse_h = [lse_ref[:, h * QB:(h + 1) * QB] for h in heads]
        sink = [sink_ref[srow, h] for h in heads]
        pr = [[jnp.exp(jnp.where(band[b], _lane_half(s2[h // 2][b], h) * SCALE + bias_ref[h, b], NEG_INF) - lse_h[h])
               for b in blocks] for h in heads]
        dp = [[_lane_half(dp2[h // 2][b], h) for b in blocks] for h in heads]
        delta = [rowsum(pr[h][0] * dp[h][0]) + rowsum(pr[h][1] * dp[h][1]) for h in heads]
        lane1 = lax.broadcasted_iota(jnp.int32, (1, QB), 1)
        dsink = jnp.zeros((1, QB), F32)
        for h in heads:
            dsink = dsink + jnp.where(lane1 == h, -jnp.sum(jnp.exp(sink[h] - lse_h[h][:, :1]) * delta[h]), 0.0)
        dsink_ref[...] += dsink
        dsc = [[pr[h][b] * (dp[h][b] - delta[h]) for b in blocks] for h in heads]
        for h in heads:
            for b in blocks:
                dsc_ref[h, b] += dsc[h][b]
        dzb = [[(dsc[h][b] * SCALE).astype(BF16) for b in blocks] for h in heads]
        prb = [[pr[h][b].astype(BF16) for b in blocks] for h in heads]
        pair_of = lambda t, g, b, axis: jnp.concatenate([t[2 * g][b], t[2 * g + 1][b]], axis=axis)
        for g in pairs:
            dq = _nn(pair_of(dzb, g, 0, 1), ks[0][g // 2]) + _nn(pair_of(dzb, g, 1, 1), ks[1][g // 2])
            dq_ref[:, g * PAIR:(g + 1) * PAIR] = dq.astype(BF16)

        def key_grad(t, other, b):
            per_kv = [_tn(pair_of(t, 2 * kh, b, 0), other[2 * kh]) + _tn(pair_of(t, 2 * kh + 1, b, 0), other[2 * kh + 1]) for kh in range(2)]
            both = [s + pltpu.roll(s, HEAD_DIM, 1) for s in per_kv]
            return jnp.where(lo, both[0], both[1])

        rp = pl.multiple_of(jnp.maximum(i - 1, 0) * QB, QB)
        rc = pl.multiple_of(i * QB, QB)
        dk_acc[pl.ds(rp, QB), :] += key_grad(dzb, qs, 0)
        dv_acc[pl.ds(rp, QB), :] += key_grad(prb, dos, 0)
        dk_acc[pl.ds(rc, QB), :] += key_grad(dzb, qs, 1)
        dv_acc[pl.ds(rc, QB), :] += key_grad(prb, dos, 1)

        @pl.when(i == nq - 1)
        def _():
            dk_ref[...] = dk_acc[...].astype(BF16)
            dv_ref[...] = dv_acc[...].astype(BF16)

    kv = lambda col, prev: pl.BlockSpec((QB, KV_W), (lambda i: (jnp.maximum(i - 1, 0), col)) if prev else (lambda i: (i, col)))
    full = pl.BlockSpec((QB, QB), lambda i: (0, 0))
    smem = pl.BlockSpec(memory_space=pltpu.SMEM)
    whole = lambda shape: pl.BlockSpec(shape, lambda i: (0,) * len(shape))
    return _call(
        body, name=name, grid=(nq,),
        out_shape=(S((T, SWA_W), BF16), S((T, KV_W), BF16), S((T, KV_W), BF16), S((1, QB), F32), S((8, 2, QB, QB), F32)),
        in_specs=[pl.BlockSpec((QB, SWA_W), lambda i: (i, 3)), kv(kcol, True), kv(kcol, False), kv(vcol, True), kv(vcol, False),
                  pl.BlockSpec((QB, SWA_W), lambda i: (i, 0)), pl.BlockSpec((QB, 8 * QB), lambda i: (i, 0)),
                  full, full, smem, smem],
        out_specs=(pl.BlockSpec((QB, SWA_W), lambda i: (i, 0)), whole((T, KV_W)), whole((T, KV_W)), whole((1, QB)),
                   whole((8, 2, QB, QB))),
        scratch=[pltpu.VMEM((8, 2, QB, QB), F32), pltpu.VMEM((T, KV_W), F32), pltpu.VMEM((T, KV_W), F32)],
        sem=("arbitrary",), args=(p, p, p, p, p, do, lse, bprev, bcur, sinks, rel_bias), riders=riders)


def mix_out_fwd(o_sb, o_sw, g_sb, g_sw, wout, h, g_next, name, riders=()):
    T, D = h.shape
    M = SB_W + SWA_W
    tm = _tile(T, 256)

    def body(a_ref, b_ref, ga_ref, gb_ref, w_ref, h_ref, gn_ref, mx_ref, o_ref, n_ref):
        mx_ref[:, :SB_W] = _rms(a_ref[...], ga_ref[...]).astype(BF16)
        mx_ref[:, SB_W:] = _rms(b_ref[...], gb_ref[...]).astype(BF16)
        out = h_ref[...] + _nn(mx_ref[...], w_ref[...])
        o_ref[...] = out
        n_ref[...] = _rms(out, gn_ref[...]).astype(BF16)

    row = lambda n: pl.BlockSpec((tm, n), lambda i: (i, 0))
    (g_sb, sb_spec), (g_sw, sw_spec), (g_next, next_spec) = _gain(g_sb), _gain(g_sw), _gain(g_next)
    return _call(
        body, name=name, grid=(T // tm,), out_shape=(S((T, M), BF16), S((T, D), F32), S((T, D), BF16)),
        in_specs=[row(SB_W), row(SWA_W), sb_spec, sw_spec, pl.BlockSpec((M, D), lambda i: (0, 0)), row(D), next_spec],
        out_specs=(row(M), row(D), row(D)),
        sem=("parallel",), args=(o_sb, o_sw, g_sb, g_sw, wout, h, g_next), riders=riders)


def loss_head(h, g, target, name):
    T, D = h.shape
    tm = _tile(T, 256)

    def body(h_ref, g_ref, t_ref, loss_ref, dh_ref, dhb_ref, dg_ref):
        @pl.when(pl.program_id(0) == 0)
        def _():
            loss_ref[...] = jnp.zeros_like(loss_ref)
            dg_ref[...] = jnp.zeros_like(dg_ref)
        x = h_ref[...]
        err = _rms(x, g_ref[...]) - t_ref[...]
        loss_ref[...] += jnp.full((1, QB), 0.5 * jnp.sum(jnp.mean(err * err, axis=-1)), F32)
        dx, dg = _rms_bwd(err / D, x, g_ref[...])
        dh_ref[...] = dx
        dhb_ref[...] = dx.astype(BF16)
        dg_ref[...] += dg

    row = pl.BlockSpec((tm, D), lambda i: (i, 0))
    vec = pl.BlockSpec((1, D), lambda i: (0, 0))
    return pl.pallas_call(
        body, name=name, grid=(T // tm,), out_shape=(S((1, QB), F32), S((T, D), F32), S((T, D), BF16), S((1, D), F32)),
        in_specs=[row, vec, row], out_specs=(pl.BlockSpec((1, QB), lambda i: (0, 0)), row, row, vec),
        compiler_params=_params(("arbitrary",)),
    )(h, g, target)


def ffn_down_bwd(dhb, wd, gate, up, a, n, name, riders=()):
    T, D = dhb.shape
    F = wd.shape[0]
    tr, tn = _tile(T, 512), _tile(F, 256)

    def body(d_ref, n_ref, w_ref, g_ref, u_ref, a_ref, o_ref, dwd_ref, dwdb_ref, dwgu_ref, dwgub_ref):
        w = w_ref[...]
        for r in range(T // tr):
            rows = slice(r * tr, (r + 1) * tr)
            da = 0.5 * _nt(d_ref[rows, :], w)
            o_ref[0, rows, :] = (da * g_ref[rows, :].astype(F32)).astype(BF16)
            o_ref[1, rows, :] = (da * u_ref[rows, :].astype(F32)).astype(BF16)
        dwd = 0.5 * _tn(a_ref[...], d_ref[...])
        dwd_ref[...] = dwd
        dwdb_ref[...] = dwd.astype(BF16)
        for s in range(2):
            dwgu = _tn(o_ref[s], n_ref[...])
            dwgu_ref[s] = dwgu
            dwgub_ref[s] = dwgu.astype(BF16)

    tile = pl.BlockSpec((T, tn), lambda j: (0, j))
    whole = pl.BlockSpec((T, D), lambda j: (0, 0))
    rows1, rows2 = pl.BlockSpec((tn, D), lambda j: (j, 0)), pl.BlockSpec((2, tn, D), lambda j: (0, j, 0))
    return _call(
        body, name=name, grid=(F // tn,),
        out_shape=(S((2, T, F), BF16), S((F, D), F32), S((F, D), BF16), S((2, F, D), F32), S((2, F, D), BF16)),
        in_specs=[whole, whole, rows1, tile, tile, tile],
        out_specs=(pl.BlockSpec((2, T, tn), lambda j: (0, 0, j)), rows1, rows1, rows2, rows2),
        sem=("parallel",), args=(dhb, n, wd, gate, up, a), riders=riders)


def tn_matmul(xs, y, alpha, name, riders=()):
    B, T, N = xs.shape
    D = y.shape[1]
    tn = _tile(N, 256)

    def body(x_ref, y_ref, o_ref, ob_ref):
        o = alpha * _tn(x_ref[...], y_ref[...])
        o_ref[...] = o
        ob_ref[...] = o.astype(BF16)

    tile = pl.BlockSpec((None, tn, D), lambda s, j: (s, j, 0))
    return _call(
        body, name=name, grid=(B, N // tn), out_shape=(S((B, N, D), F32), S((B, N, D), BF16)),
        in_specs=[pl.BlockSpec((None, T, tn), lambda s, j: (s, 0, j)), pl.BlockSpec((T, D), lambda s, j: (0, 0))],
        out_specs=(tile, tile), sem=("parallel", "parallel"), args=(xs, y), riders=riders)


def nn_rms_bwd(xs, ws, h_in, g, dh, name, riders=()):
    B, T, K = xs.shape
    D = ws.shape[2]
    tm = _tile(T, 256)

    def body(x_ref, w_ref, h_ref, g_ref, d_ref, o_ref, ob_ref, dg_ref):
        @pl.when(pl.program_id(0) == 0)
        def _():
            dg_ref[...] = jnp.zeros_like(dg_ref)
        dn = _nn(x_ref[0], w_ref[0])
        for s in range(1, B):
            dn = dn + _nn(x_ref[s], w_ref[s])
        dx, dg = _rms_bwd(dn, h_ref[...], g_ref[...])
        out = d_ref[...] + dx
        o_ref[...] = out
        ob_ref[...] = out.astype(BF16)
        dg_ref[...] += dg

    row = pl.BlockSpec((tm, D), lambda i: (i, 0))
    vec = pl.BlockSpec((1, D), lambda i: (0, 0))
    g, g_spec = _gain(g)
    return _call(
        body, name=name, grid=(T // tm,), out_shape=(S((T, D), F32), S((T, D), BF16), S((1, D), F32)),
        in_specs=[pl.BlockSpec((B, tm, K), lambda i: (0, i, 0)), pl.BlockSpec((B, K, D), lambda i: (0, 0, 0)), row, g_spec, row],
        out_specs=(row, row, vec),
        sem=("arbitrary",), args=(xs, ws, h_in, g, dh), riders=riders)


def mix_out_bwd(dhb, wout, mixed, o_sb, o_sw, g_sb, g_sw, name):
    T, D = dhb.shape
    M = SB_W + SWA_W
    tm = _tile(T, 256)
    steps = T // tm

    def body(d_ref, w_ref, mx_ref, a_ref, b_ref, ga_ref, gb_ref, da_ref, db_ref, dga_ref, dgb_ref, dw_ref, dwb_ref):
        i = pl.program_id(0)

        @pl.when(i == 0)
        def _():
            dga_ref[...] = jnp.zeros_like(dga_ref)
            dgb_ref[...] = jnp.zeros_like(dgb_ref)
            dw_ref[...] = jnp.zeros_like(dw_ref)
        dm = _nt(d_ref[...], w_ref[...])
        dxa, dga = _rms_bwd(dm[:, :SB_W], a_ref[...], ga_ref[...])
        dxb, dgb = _rms_bwd(dm[:, SB_W:], b_ref[...], gb_ref[...])
        da_ref[...] = dxa
        db_ref[...] = dxb
        dga_ref[...] += dga
        dgb_ref[...] += dgb
        dw_ref[...] += _tn(mx_ref[...], d_ref[...])

        @pl.when(i == steps - 1)
        def _():
            dwb_ref[...] = dw_ref[...].astype(BF16)

    row = lambda n: pl.BlockSpec((tm, n), lambda i: (i, 0))
    vec = lambda n: pl.BlockSpec((1, n), lambda i: (0, 0))
    whole = pl.BlockSpec((M, D), lambda i: (0, 0))
    (g_sb, sb_spec), (g_sw, sw_spec) = _gain(g_sb), _gain(g_sw)
    return pl.pallas_call(
        body, name=name, grid=(steps,),
        out_shape=(S((T, SB_W), F32), S((T, SWA_W), F32), S((1, SB_W), F32), S((1, SWA_W), F32), S((M, D), F32), S((M, D), BF16)),
        in_specs=[row(D), whole, row(M), row(SB_W), row(SWA_W), sb_spec, sw_spec],
        out_specs=(row(SB_W), row(SWA_W), vec(SB_W), vec(SWA_W), whole, whole),
        compiler_params=_params(("arbitrary",)),
    )(dhb, wout, mixed, o_sb, o_sw, g_sb, g_sw)


def rel_bias_grad(dscs, bprev, bcur, name):
    n = len(dscs)

    def body(*refs):
        bp_ref, bc_ref, o_ref = refs[n], refs[n + 1], refs[n + 2]
        bks = [bp_ref[...], bc_ref[...]]
        row = lax.broadcasted_iota(jnp.int32, (N_BUCKETS, QB), 0)
        lane = lax.broadcasted_iota(jnp.int32, (N_BUCKETS, QB), 1)
        out = jnp.zeros((N_BUCKETS, QB), F32)
        for h in range(8):
            tot = [sum(refs[l][h, b] for l in range(n)) for b in range(2)]
            for b in range(N_BUCKETS):
                val = jnp.sum(jnp.where(bks[0] == b, tot[0], 0.0)) + jnp.sum(jnp.where(bks[1] == b, tot[1], 0.0))
                out = jnp.where((row == b) & (lane == h), val, out)
        o_ref[...] = out

    return pl.pallas_call(body, name=name, out_shape=S((N_BUCKETS, QB), F32), compiler_params=_params())(*dscs, bprev, bcur)


def _adamw(w, g, m, v):
    m = ADAM_B1 * m + (1.0 - ADAM_B1) * g
    v = ADAM_B2 * v + (1.0 - ADAM_B2) * (g * g)
    m_hat = m / (1.0 - ADAM_B1 ** ADAM_STEP)
    v_hat = v / (1.0 - ADAM_B2 ** ADAM_STEP)
    delta = -ADAM_LR * (m_hat / (jnp.sqrt(v_hat) + ADAM_EPS) + ADAM_WD * w)
    return delta, m, v


def adamw_scattered(w, m, v, owns, others, name, riders=(), rows=176):
    L, R, C = w.shape
    tr = _rows_tile(R, rows)

    def body(w_ref, m_ref, v_ref, *rest):
        own_refs, other_refs = rest[:L], rest[L:2 * L]
        g_ref, d_ref, mo_ref, vo_ref = rest[2 * L:]
        layer = pl.program_id(0)

        def grad(k):
            o = other_refs[k]
            return own_refs[k][...] + o[0].astype(F32) + o[1].astype(F32) + o[2].astype(F32)

        g = grad(0)
        for k in range(1, L):
            g = jnp.where(layer == k, grad(k), g)
        d, mn, vn = _adamw(w_ref[...], g, m_ref[...], v_ref[...])
        g_ref[...] = g
        d_ref[...] = d
        mo_ref[...] = mn
        vo_ref[...] = vn

    tile = pl.BlockSpec((None, tr, C), lambda l, i: (l, i, 0))
    return _call(
        body, name=name, grid=(L, R // tr), out_shape=(S((L, R, C), F32),) * 4,
        in_specs=[tile] * 3 + [pl.BlockSpec((tr, C), lambda l, i: (i, 0))] * L + [pl.BlockSpec((3, tr, C), lambda l, i: (0, i, 0))] * L,
        out_specs=(tile,) * 4, sem=("parallel", "parallel"), args=(w, m, v, *owns, *others), riders=riders)


def adamw_small(w, gs, m, v, name):
    R, C = w.shape

    def body(w_ref, g_ref, m_ref, v_ref, go_ref, d_ref, mo_ref, vo_ref):
        g = g_ref[0]
        for k in range(1, N_DEV):
            g = g + g_ref[k]
        d, mn, vn = _adamw(w_ref[...], g, m_ref[...], v_ref[...])
        go_ref[...] = g
        d_ref[...] = d
        mo_ref[...] = mn
        vo_ref[...] = vn

    return pl.pallas_call(body, name=name, out_shape=(S((R, C), F32),) * 4, compiler_params=_params())(w, gs, m, v)


def kernel(x, norm_ffn1, w_ffn1_gu, w_ffn1_down, norm_mix, w_in, sinks, norm_out_sb, norm_out_swa, w_out, norm_ffn2, w_ffn2_gu, w_ffn2_down, rel_bias, norm_final, loss_target, m_norm_ffn1, m_w_ffn1_gu, m_w_ffn1_down, m_norm_mix, m_w_in, m_sinks, m_norm_out_sb, m_norm_out_swa, m_w_out, m_norm_ffn2, m_w_ffn2_gu, m_w_ffn2_down, m_rel_bias, m_norm_final, v_norm_ffn1, v_w_ffn1_gu, v_w_ffn1_down, v_norm_mix, v_w_in, v_sinks, v_norm_out_sb, v_norm_out_swa, v_w_out, v_norm_ffn2, v_w_ffn2_gu, v_w_ffn2_down, v_rel_bias, v_norm_final):
    L = norm_ffn1.shape[0]
    T, D = x.shape[1], x.shape[2]
    F = w_ffn1_down.shape[1] * N_DEV
    h = x.reshape(T, D)
    target = loss_target.reshape(T, D)
    after, upto, before = _tri_consts()
    bprev, bcur = _t5_buckets()

    local = {}
    for l in range(L):
        local[f"gu1_{l}"] = w_ffn1_gu[l].T.astype(BF16)
        local[f"d1_{l}"] = w_ffn1_down[l].astype(BF16)
        local[f"in_{l}"] = w_in[l].T.astype(BF16)
        local[f"out_{l}"] = w_out[l].astype(BF16)
        local[f"gu2_{l}"] = w_ffn2_gu[l].T.astype(BF16)
        local[f"d2_{l}"] = w_ffn2_down[l].astype(BF16)
    full, partial = {}, {}
    grads, chip_sum, recv_b = {}, {}, {}

    def run(fn, *args, ag=(), rs1=(), rs2=()):
        halves = lambda names: [n if isinstance(n, tuple) else (n, None) for n in names]
        ag, rs2 = [(n, k) for n, k in halves(ag) if n in local], halves(rs2)
        rows = lambda k, total: None if k is None else (k * (total // 2), total // 2)

        def second(n, k):
            sb = chip_sum[n][1]
            return scatter_second(sb, rows(k, sb.shape[1]), recv_b.get(n))

        riders = ([gather(local[n], rows(k, local[n].shape[0]), partial.get(n)) for n, k in ag]
                  + [scatter_first(grads[n][1]) for n in rs1] + [second(n, k) for n, k in rs2])
        if not riders:
            return fn(*args)
        outs, per = fn(*args, riders=riders)
        per = [p[0] for p in per]
        for n, k in ag:
            buf = per.pop(0)
            if k == 0:
                partial[n] = buf
            else:
                full[n] = buf.reshape(N_DEV * buf.shape[1], D)
        if rs1:
            sums = scatter_add([grads[n][0] for n in rs1], [per.pop(0) for n in rs1], "rs_add_" + "_".join(rs1))
            chip_sum.update(zip(rs1, sums))
        for n, _ in rs2:
            recv_b[n] = per.pop(0)
        return outs

    def attn_fwd(p, sink, name, riders=()):
        return side_by_side(sb_attn_fwd(p, after, name, riders=PARTS), swa_fwd(p, sink, rel_bias, bprev, bcur, name, riders=PARTS),
                            name, riders)

    def attn_bwd(p, do_sb, tot, do_sw, lse, sink, name, riders=()):
        return side_by_side(sb_attn_bwd(p, do_sb, tot, upto, before, name, riders=PARTS),
                            swa_bwd(p, do_sw, lse, sink, rel_bias, bprev, bcur, name, riders=PARTS), name, riders)

    gu = lambda n: full[n].reshape(2, F, D)
    slots = lambda pair: tuple(t.reshape(N_DEV, -1, D) for t in pair)
    vec = lambda a: a.reshape(1, -1)

    PW = max(D, SB_W + SWA_W)
    n_rows = 4 * L + 2
    n_rows += (-n_rows) % 8

    def pack(ffn1, mix, ffn2, final, osb, osw, snk, rel, extra):
        pieces = []

        def row(*parts):
            flat = [a.reshape(-1) for a in parts]
            pieces.extend(flat)
            used = sum(a.size for a in flat)
            if used < PW:
                pieces.append(jnp.zeros((PW - used,), F32))

        for group in (ffn1, mix, ffn2):
            for l in range(L):
                row(group[l])
        row(final)
        for l in range(L):
            row(osb[l], osw[l])
        row(*[snk[l].reshape(-1)[:8] for l in range(L)], rel, extra)
        pieces.append(jnp.zeros(((n_rows - 4 * L - 2) * PW,), F32))
        return jnp.concatenate(pieces).reshape(n_rows, PW)

    def unpack(arr):
        ffn1, mix, ffn2 = arr[0:L, :D], arr[L:2 * L, :D], arr[2 * L:3 * L, :D]
        final = arr[3 * L, :D]
        ob = arr[3 * L + 1:4 * L + 1]
        tail = arr[4 * L + 1]
        return (ffn1, mix, tail[:8 * L].reshape(L, 8), ob[:, :SB_W], ob[:, SB_W:SB_W + SWA_W], ffn2,
                tail[8 * L:8 * L + N_BUCKETS * 8].reshape(N_BUCKETS, 8), final)

    zero = jnp.zeros((1,), F32)
    w_small = pack(norm_ffn1, norm_mix, norm_ffn2, norm_final, norm_out_sb, norm_out_swa, sinks, rel_bias, zero)
    g_ffn1, g_mix, g_ffn2, g_osb, g_osw = [a.reshape(L, 1, -1) for a in (norm_ffn1, norm_mix, norm_ffn2, norm_out_sb, norm_out_swa)]

    saved = []
    n_next = run(rms_cast, h, (g_ffn1, 0), "rms_first", ag=("gu1_0",))
    for l in range(L):
        nx = l + 1
        s = {"h0": h, "n1": n_next}
        s["gate1"], s["up1"], s["a1"] = run(ffn_up_fwd, s["n1"], gu(f"gu1_{l}"), f"ffn1_up{l}",
                                            ag=(f"d1_{l}", ("in_0", 0) if l == 0 else (f"in_{l}", 1)))
        h = run(ffn_down_fwd, s["a1"], full[f"d1_{l}"], h, None, f"ffn1_down{l}", ag=(("in_0", 1), "out_0") if l == 0 else ())
        s["h1"] = h
        s["n2"], s["p"] = mix_in_fwd(h, (g_mix, l), full[f"in_{l}"], f"mix_in{l}")
        s["o_sb"], s["tot"], s["o_sw"], s["lse"] = run(attn_fwd, s["p"], (sinks, l), f"attn_fwd{l}",
                                                       ag=((f"out_{l}",) if l else ()) + (f"gu2_{l}", f"d2_{l}", (f"gu1_{nx}", 0)))
        s["mixed"], h, s["n3"] = run(mix_out_fwd, s["o_sb"], s["o_sw"], (g_osb, l), (g_osw, l),
                                     full[f"out_{l}"], h, (g_ffn2, l), f"mix_out{l}")
        s["h2"] = h
        s["gate2"], s["up2"], s["a2"] = run(ffn_up_fwd, s["n3"], gu(f"gu2_{l}"), f"ffn2_up{l}",
                                            ag=((f"gu1_{nx}", 1), (f"in_{nx}", 0)))
        if nx < L:
            h, n_next = run(ffn_down_fwd, s["a2"], full[f"d2_{l}"], h, (g_ffn1, nx), f"ffn2_down{l}")
        else:
            h = run(ffn_down_fwd, s["a2"], full[f"d2_{l}"], h, None, f"ffn2_down{l}")
        saved.append(s)

    loss_part, dh, dhb, dg_final = loss_head(h, vec(norm_final), target, "loss_head")

    small = {k: [None] * L for k in ("ffn1", "mix", "sinks", "osb", "osw", "ffn2", "dsc")}
    for l in reversed(range(L)):
        s = saved[l]

        def ffn_bwd(dh, dhb, tag, gate, up, a, n, h_in, g, r_down, r_up):
            gu_n, d_n = f"gu{tag}_{l}", f"d{tag}_{l}"
            dgu, dwd, dwdb, dwgu, dwgub = run(ffn_down_bwd, dhb, full[d_n], gate, up, a, n, f"ffn{tag}_down_bwd{l}", **r_down)
            grads[gu_n], grads[d_n] = slots((dwgu, dwgub)), slots((dwd, dwdb))
            return run(nn_rms_bwd, dgu, gu(gu_n), h_in, g, dh, f"ffn{tag}_up_bwd{l}", **r_up)

        later = l + 1 < L
        dh, dhb, small["ffn2"][l] = ffn_bwd(dh, dhb, 2, s["gate2"], s["up2"], s["a2"], s["n3"], s["h2"], (g_ffn2, l),
                                            dict(rs2=((f"gu1_{l + 1}", 0), f"d1_{l + 1}") if later else ()),
                                            dict(rs1=(f"gu2_{l}", f"d2_{l}")))
        do_sb, do_sw, small["osb"][l], small["osw"][l], dw_out, dw_out_b = mix_out_bwd(
            dhb, full[f"out_{l}"], s["mixed"], s["o_sb"], s["o_sw"], (g_osb, l), (g_osw, l), f"mix_out_bwd{l}")
        grads[f"out_{l}"] = slots((dw_out, dw_out_b))
        dq_sb, dk_sb, dv_sb, dq_sw, dk_sw, dv_sw, small["sinks"][l], small["dsc"][l] = run(
            attn_bwd, s["p"], do_sb, s["tot"], do_sw, s["lse"], (sinks, l), f"attn_bwd{l}",
            rs2=(f"gu2_{l}", f"d2_{l}") + (((f"gu1_{l + 1}", 1),) if later else ()), rs1=(f"out_{l}",))
        dp = jnp.concatenate([dq_sb, dk_sb, dv_sb, dq_sw, dk_sw, dv_sw], axis=1)
        dh, dhb, small["mix"][l] = nn_rms_bwd(dp[None], full[f"in_{l}"][None], s["h1"], (g_mix, l), dh, f"mix_in_bwd{l}")
        grads[f"in_{l}"] = slots(tn_matmul(dp[None], s["n2"], 1.0, f"dwin{l}"))
        dh, dhb, small["ffn1"][l] = ffn_bwd(dh, dhb, 1, s["gate1"], s["up1"], s["a1"], s["n1"], s["h0"], (g_ffn1, l),
                                            dict(rs1=(f"in_{l}",), rs2=(f"out_{l}",)),
                                            dict(rs1=(f"gu1_{l}", f"d1_{l}"), rs2=(f"in_{l}",)))

    grad_x = dh.reshape(x.shape)

    upd = {}
    turn_of = lambda transposed: (lambda a: jnp.swapaxes(a, 1, 2)) if transposed else (lambda a: a)

    def update(nm, w, m, v, transposed, riders=()):
        turn = turn_of(transposed)
        names = [f"{nm}_{l}" for l in range(L)]
        return adamw_scattered(turn(w), turn(m), turn(v), [chip_sum[n][0] for n in names], [recv_b[n] for n in names],
                               f"adamw_{nm}", riders=riders, rows=88 if riders is PARTS else 176)

    early = (("gu2", w_ffn2_gu, m_w_ffn2_gu, v_w_ffn2_gu, True), ("d2", w_ffn2_down, m_w_ffn2_down, v_w_ffn2_down, False),
             ("in", w_in, m_w_in, v_w_in, True), ("out", w_out, m_w_out, v_w_out, False))
    res = run(lambda name, riders=(): in_one_call([update(*e, riders=PARTS) for e in early], name, riders),
              "adamw_early", rs2=("gu1_0", "d1_0"))
    for k, e in enumerate(early):
        upd[e[0]] = tuple(turn_of(e[4])(r) for r in res[4 * k:4 * k + 4])
    for e in (("gu1", w_ffn1_gu, m_w_ffn1_gu, v_w_ffn1_gu, True), ("d1", w_ffn1_down, m_w_ffn1_down, v_w_ffn1_down, False)):
        upd[e[0]] = tuple(turn_of(e[4])(r) for r in update(*e))

    d_rel = rel_bias_grad(small["dsc"], bprev, bcur, "rel_bias_grad")[:, :8]
    g_small = pack(small["ffn1"], small["mix"], small["ffn2"], dg_final, small["osb"], small["osw"], small["sinks"], d_rel,
                   loss_part[0, :1])
    m_small = pack(m_norm_ffn1, m_norm_mix, m_norm_ffn2, m_norm_final, m_norm_out_sb, m_norm_out_swa, m_sinks, m_rel_bias, zero)
    v_small = pack(v_norm_ffn1, v_norm_mix, v_norm_ffn2, v_norm_final, v_norm_out_sb, v_norm_out_swa, v_sinks, v_rel_bias, zero)
    gs_small = all_gather_rows(g_small, "ag_small")
    summed = adamw_small(w_small, gs_small, m_small, v_small, "adamw_small")
    small_out = [unpack(a) for a in summed]
    loss = summed[0][4 * L + 1, 8 * L + N_BUCKETS * 8]

    def group(k):
        sm = small_out[k]
        return (sm[0], upd["gu1"][k], upd["d1"][k], sm[1], upd["in"][k], sm[2], sm[3], sm[4], upd["out"][k], sm[5],
                upd["gu2"][k], upd["d2"][k], sm[6], sm[7])

    return (loss, grad_x, *group(0), *group(1), *group(2), *group(3))
```

```python
import math

import jax
import jax.numpy as jnp
from jax import lax
from jax.experimental import pallas as pl
from jax.experimental.pallas import tpu as pltpu

F32 = jnp.float32
BF16 = jnp.bfloat16
S = jax.ShapeDtypeStruct

N_DEV = 8
HEAD_DIM = 64
SB_HEADS = 8
PAIR = 2 * HEAD_DIM
SB_W = 512
SWA_W = 512
KV_W = 128
IN_W = 3 * SB_W + SWA_W + 2 * KV_W
QB = 128
N_BUCKETS = 32
MAX_DISTANCE = 128
EPS = 1e-6
NEG_INF = -1e30
SCALE = HEAD_DIM ** -0.5

ADAM_LR = 0.001
ADAM_B1 = 0.9
ADAM_B2 = 0.999
ADAM_EPS = 1e-08
ADAM_WD = 0.01
ADAM_STEP = 10

VMEM_LIMIT = 56 * 1024 * 1024
MESH = pl.DeviceIdType.MESH


def _params(sem=None, vmem=VMEM_LIMIT):
    return pltpu.CompilerParams(dimension_semantics=sem, vmem_limit_bytes=vmem)


def _nn(a, b):
    return jnp.dot(a, b, preferred_element_type=F32)


def _nt(a, b):
    return lax.dot_general(a, b, (((1,), (1,)), ((), ())), preferred_element_type=F32)


def _tn(a, b):
    return lax.dot_general(a, b, (((0,), (0,)), ((), ())), preferred_element_type=F32)


def _tri(xs, m):
    return [_nn(x.astype(BF16), m) for x in xs]


def _rms(x, g):
    r = lax.rsqrt(jnp.mean(x * x, axis=-1, keepdims=True) + EPS)
    return x * r * g


def _rms_bwd(dy, x, g):
    r = lax.rsqrt(jnp.mean(x * x, axis=-1, keepdims=True) + EPS)
    xhat = x * r
    u = dy * g
    dx = r * (u - xhat * jnp.mean(u * xhat, axis=-1, keepdims=True))
    return dx, jnp.sum(dy * xhat, axis=0, keepdims=True)


def _softplus_logsig(z):
    sp = jnp.maximum(z, 0.0) + jnp.log(1.0 + jnp.exp(-jnp.abs(z)))
    return sp, z - sp


def _gain(g):
    if isinstance(g, tuple):
        rows, n = g
        return rows, pl.BlockSpec((None, 1, rows.shape[2]), lambda *_: (n, 0, 0))
    return g, pl.BlockSpec((1, g.shape[1]), lambda *_: (0, 0))


def _tile(n, want):
    t = min(n, want)
    while n % t:
        t //= 2
    return t


def _place():
    x, y, c = lax.axis_index("x"), lax.axis_index("y"), lax.axis_index("c")
    chips = [(1 - x, y), (x, 1 - y), (1 - x, 1 - y)]
    return x, y, c, chips


def all_gather_rows(v, name):
    R, C = v.shape

    def body(v_ref, out_ref, send_sems, recv_sems, local_sem):
        x, y, c, chips = _place()
        me, sibling = (x, y, c), (x, y, 1 - c)

        def slot(px, py, pc):
            return out_ref.at[4 * px + 2 * py + pc]

        def copy(k, block, to, src=None):
            return pltpu.make_async_remote_copy(
                src_ref=slot(*block) if src is None else src, dst_ref=slot(*block),
                send_sem=send_sems.at[k], recv_sem=recv_sems.at[k], device_id=to, device_id_type=MESH)

        mine = pltpu.make_async_copy(v_ref, slot(*me), local_sem)
        mine.start()
        first = [copy(0, me, sibling, src=v_ref)]
        first += [copy(1 + j, me, (*chip, c), src=v_ref) for j, chip in enumerate(chips)]
        for cp in first:
            cp.start()
        passed = [copy(4 + j, (*chip, c), sibling) for j, chip in enumerate(chips)]
        for j, chip in enumerate(chips):
            copy(1 + j, (*chip, c), me).wait_recv()
            passed[j].start()
        copy(0, sibling, me).wait_recv()
        for j, chip in enumerate(chips):
            copy(4 + j, (*chip, 1 - c), me).wait_recv()
        for cp in first + passed:
            cp.wait_send()
        mine.wait()

    return pl.pallas_call(
        body, name=name, out_shape=S((N_DEV, R, C), v.dtype),
        in_specs=[pl.BlockSpec(memory_space=pl.ANY)], out_specs=pl.BlockSpec(memory_space=pl.ANY),
        scratch_shapes=[pltpu.SemaphoreType.DMA((7,)), pltpu.SemaphoreType.DMA((7,)), pltpu.SemaphoreType.DMA],
    )(v)


class _Exchange:
    def __init__(self, ins, outs, sizes, n_local, plan, aliases=None):
        self.ins, self.outs, self.plan, self.aliases = list(ins), list(outs), plan, aliases or {}
        self.sizes, self.n_local = list(sizes), n_local

    def scratch(self):
        n = sum(self.sizes)
        return [pltpu.SemaphoreType.DMA((n,)), pltpu.SemaphoreType.DMA((n,)), pltpu.SemaphoreType.DMA((max(self.n_local, 1),))]

    def _copies(self, in_refs, out_refs, sems):
        send_sems, recv_sems, local_sems = sems
        phases, local = self.plan(in_refs, out_refs)
        out, k = [], 0
        for phase in phases:
            out.append([pltpu.make_async_remote_copy(src_ref=s, dst_ref=d, send_sem=send_sems.at[k + n], recv_sem=recv_sems.at[k + n],
                                                     device_id=dev, device_id_type=MESH) for n, (s, d, dev) in enumerate(phase)])
            k += len(phase)
        return out, [pltpu.make_async_copy(s, d, local_sems.at[n]) for n, (s, d) in enumerate(local)]

    def start(self, in_refs, out_refs, sems):
        phases, loc = self._copies(in_refs, out_refs, sems)
        for cp in phases[0] + loc:
            cp.start()

    def advance(self, hook, in_refs, out_refs, sems):
        p = hook - (3 - len(self.sizes))
        if p >= 1:
            phases, _ = self._copies(in_refs, out_refs, sems)
            for cp in phases[p - 1]:
                cp.wait_recv()
            for cp in phases[p]:
                cp.start()

    def finish(self, in_refs, out_refs, sems):
        phases, loc = self._copies(in_refs, out_refs, sems)
        for cp in phases[-1]:
            cp.wait_recv()
        for phase in phases:
            for cp in phase:
                cp.wait_send()
        for cp in loc:
            cp.wait()


def gather(v, rows=None, into=None):
    R, C = v.shape
    r0, nr = rows or (0, R)
    na = min(nr, ((nr // 2 + 15) // 16) * 16)

    def plan(ins, outs):
        x, y, c, _ = _place()
        xn, yn, dg, sibling = (1 - x, y), (x, 1 - y), (1 - x, 1 - y), (x, y, 1 - c)
        slot = lambda chip, start=r0, count=nr: outs[0].at[4 * chip[0] + 2 * chip[1] + c, pl.ds(start, count), :]
        src, mine = ins[0].at[pl.ds(r0, nr), :], slot((x, y))
        same = lambda ref, to: (ref, ref, to)
        first = [(src, mine, sibling), (src, mine, (*xn, c)), (src, mine, (*yn, c))]
        relay = [same(slot(xn, r0, na), (*yn, c)), same(slot(yn, r0 + na, nr - na), (*xn, c))]
        onward = [same(slot(xn), sibling), same(slot(yn), sibling), same(slot(dg), sibling)]
        return [first, relay, onward], [(src, mine)]

    if into is None:
        return _Exchange([v], [S((N_DEV, R, C), v.dtype)], (3, 2, 3), 1, plan)
    return _Exchange([v, into], [S((N_DEV, R, C), v.dtype)], (3, 2, 3), 1, plan, aliases={1: 0})


def scatter_first(gb):
    _, R, C = gb.shape

    def plan(ins, outs):
        x, y, c, chips = _place()
        owners = [(x, y)] + chips
        return [[(ins[0].at[4 * px + 2 * py + (1 - c)], outs[0].at[j], (x, y, 1 - c)) for j, (px, py) in enumerate(owners)]], []

    return _Exchange([gb], [S((4, R, C), BF16)], (4,), 0, plan)


def scatter_second(sb, rows=None, into=None):
    r0, nr = rows or (0, sb.shape[1])

    def plan(ins, outs):
        x, y, c, chips = _place()
        part = lambda ref, j: ref.at[j, pl.ds(r0, nr), :]
        return [[(part(ins[0], j), part(outs[0], j), (*chips[j], c)) for j in range(3)]], []

    if into is None:
        return _Exchange([sb], [S(sb.shape, BF16)], (3,), 0, plan)
    return _Exchange([sb, into], [S(sb.shape, BF16)], (3,), 0, plan, aliases={1: 0})


PARTS = "parts"


def _call(body, *, name, grid, in_specs, out_specs, out_shape, args, scratch=(), sem=None, riders=(), marks=None):
    single = not isinstance(out_shape, (tuple, list))
    out_shape = (out_shape,) if single else tuple(out_shape)
    out_specs = (out_specs,) if single else tuple(out_specs)
    n_in, n_out, n_sc = len(in_specs), len(out_shape), len(scratch)
    if riders is PARTS:
        return dict(body=body, grid=grid, in_specs=list(in_specs), out_specs=out_specs, out_shape=out_shape, args=tuple(args),
                    scratch=list(scratch), marks=marks)
    if not riders:
        res = pl.pallas_call(body, name=name, grid=grid, in_specs=list(in_specs), out_specs=out_specs, out_shape=out_shape,
                             scratch_shapes=list(scratch), compiler_params=_params(sem))(*args)
        return res[0] if single else res
    r_ins = [a for r in riders for a in r.ins]
    r_outs = [o for r in riders for o in r.outs]
    r_scr = [s for r in riders for s in r.scratch()]
    aliases, i0, o0 = {}, n_in, n_out
    for r in riders:
        for a, b in r.aliases.items():
            aliases[i0 + a] = o0 + b
        i0, o0 = i0 + len(r.ins), o0 + len(r.outs)
    steps = math.prod(grid)

    def full(*refs):
        ins, rin = refs[:n_in], refs[n_in:n_in + len(r_ins)]
        pos = n_in + len(r_ins)
        outs, rout = refs[pos:pos + n_out], refs[pos + n_out:pos + n_out + len(r_outs)]
        pos += n_out + len(r_outs)
        sc, rsc = refs[pos:pos + n_sc], refs[pos + n_sc:]
        step = 0
        for d, n in enumerate(grid):
            step = step * n + pl.program_id(d)

        def each(method, *lead):
            i, o = 0, 0
            for k, r in enumerate(riders):
                getattr(r, method)(*lead, rin[i:i + len(r.ins)], rout[o:o + len(r.outs)], rsc[3 * k:3 * k + 3])
                i, o = i + len(r.ins), o + len(r.outs)

        @pl.when(step == 0)
        def _():
            each("start")
        body(*ins, *outs, *sc)

        late = max(steps - 1 - max(steps // 8, 1), 0)
        first, second = marks or (min((3 * steps) // 5, late), late)

        @pl.when(step == first)
        def _():
            each("advance", 1)

        @pl.when(step == second)
        def _():
            each("advance", 2)

        @pl.when(step == steps - 1)
        def _():
            each("finish")

    anywhere = pl.BlockSpec(memory_space=pl.ANY)
    res = pl.pallas_call(
        full, name=name, grid=grid, in_specs=list(in_specs) + [anywhere] * len(r_ins),
        out_specs=out_specs + (anywhere,) * len(r_outs), out_shape=out_shape + tuple(r_outs),
        scratch_shapes=list(scratch) + r_scr, input_output_aliases=aliases,
        compiler_params=_params(("arbitrary",) * len(grid)))(*args, *r_ins)
    host, rest, per = res[:n_out], list(res[n_out:]), []
    for r in riders:
        per.append(rest[:len(r.outs)])
        rest = rest[len(r.outs):]
    return (host[0] if single else tuple(host)), per


def side_by_side(first, second, name, riders=()):
    a_in, a_out, a_sc = len(first["in_specs"]), len(first["out_shape"]), len(first["scratch"])
    n_in, n_out = a_in + len(second["in_specs"]), a_out + len(second["out_shape"])

    def body(*refs):
        ins, outs, sc = refs[:n_in], refs[n_in:n_in + n_out], refs[n_in + n_out:]
        first["body"](*ins[:a_in], *outs[:a_out], *sc[:a_sc])
        second["body"](*ins[a_in:], *outs[a_out:], *sc[a_sc:])

    return _call(body, name=name, grid=first["grid"], in_specs=first["in_specs"] + second["in_specs"],
                 out_specs=first["out_specs"] + second["out_specs"], out_shape=first["out_shape"] + second["out_shape"],
                 args=first["args"] + second["args"], scratch=first["scratch"] + second["scratch"],
                 sem=("arbitrary",) * len(first["grid"]), riders=riders, marks=first["marks"])


def in_one_call(parts, name, riders=()):
    extents = [p["grid"][-1] for p in parts]
    longest = max(extents)

    def clamp(spec, n):
        if n == longest or spec.index_map is None:
            return spec
        return pl.BlockSpec(spec.block_shape, lambda *ids, f=spec.index_map: f(*ids[:-1], jnp.minimum(ids[-1], n - 1)))

    counts = [(len(p["in_specs"]), len(p["out_shape"]), len(p["scratch"])) for p in parts]
    n_in, n_out = sum(c[0] for c in counts), sum(c[1] for c in counts)

    def body(*refs):
        ins, outs, sc = refs[:n_in], refs[n_in:n_in + n_out], refs[n_in + n_out:]
        i = o = s = 0
        for p, n, (ci, co, cs) in zip(parts, extents, counts):
            run_part = lambda p=p, a=ins[i:i + ci], b=outs[o:o + co], c=sc[s:s + cs]: p["body"](*a, *b, *c)
            if n == longest:
                run_part()
            else:
                pl.when(pl.program_id(len(p["grid"]) - 1) < n)(run_part)
            i, o, s = i + ci, o + co, s + cs

    cat = lambda key: [x for p in parts for x in p[key]]
    return _call(body, name=name, grid=parts[0]["grid"][:-1] + (longest,),
                 in_specs=[clamp(sp, n) for p, n in zip(parts, extents) for sp in p["in_specs"]],
                 out_specs=tuple(clamp(sp, n) for p, n in zip(parts, extents) for sp in p["out_specs"]),
                 out_shape=tuple(cat("out_shape")), args=tuple(cat("args")), scratch=cat("scratch"),
                 sem=("arbitrary",) * len(parts[0]["grid"]), riders=riders)


def _rows_tile(n, cap):
    return max(t for t in range(16, min(n, cap) + 1, 16) if n % t == 0)


def scatter_add(gs, ras, name):
    C = gs[0].shape[2]
    trs = [_rows_tile(g.shape[1], 176) for g in gs]
    nts = [g.shape[1] // tr for g, tr in zip(gs, trs)]
    steps = max(nts)
    x, y, c, chips = _place()
    slots = jnp.stack([4 * px + 2 * py + c for px, py in [(x, y)] + chips]).astype(jnp.int32)

    def body(s_ref, *refs):
        ins, outs = refs[:5 * len(gs)], refs[5 * len(gs):]
        for k in range(len(gs)):
            g0, g1, g2, g3, ra_ref = ins[5 * k:5 * k + 5]
            own_ref, sb_ref = outs[2 * k:2 * k + 2]

            def work(g0=g0, g1=g1, g2=g2, g3=g3, ra_ref=ra_ref, own_ref=own_ref, sb_ref=sb_ref):
                own_ref[...] = g0[...] + ra_ref[0].astype(F32)
                for j, gj in enumerate((g1, g2, g3)):
                    sb_ref[j] = (gj[...] + ra_ref[j + 1].astype(F32)).astype(BF16)

            if nts[k] == steps:
                work()
            else:
                pl.when(pl.program_id(0) < nts[k])(work)

    in_specs, out_specs, out_shape, args = [], [], [], [slots]
    for k, (g, ra, tr) in enumerate(zip(gs, ras, trs)):
        tile = lambda i, k=k: jnp.minimum(i, nts[k] - 1)
        in_specs += [pl.BlockSpec((None, tr, C), lambda i, s, j=j, tile=tile: (s[j], tile(i), 0)) for j in range(4)]
        in_specs.append(pl.BlockSpec((4, tr, C), lambda i, s, tile=tile: (0, tile(i), 0)))
        out_specs += [pl.BlockSpec((tr, C), lambda i, s, tile=tile: (tile(i), 0)),
                      pl.BlockSpec((3, tr, C), lambda i, s, tile=tile: (0, tile(i), 0))]
        out_shape += [S((g.shape[1], C), F32), S((3, g.shape[1], C), BF16)]
        args += [g, g, g, g, ra]
    spec = pltpu.PrefetchScalarGridSpec(num_scalar_prefetch=1, grid=(steps,), in_specs=in_specs, out_specs=tuple(out_specs))
    res = pl.pallas_call(body, name=name, grid_spec=spec, out_shape=tuple(out_shape), compiler_params=_params(("arbitrary",)))(*args)
    return [(res[2 * k], res[2 * k + 1]) for k in range(len(gs))]


def rms_cast(h, g, name, riders=()):
    T, D = h.shape
    tm = _tile(T, 512)

    def body(h_ref, g_ref, n_ref):
        n_ref[...] = _rms(h_ref[...], g_ref[...]).astype(BF16)

    row = pl.BlockSpec((tm, D), lambda i: (i, 0))
    g, g_spec = _gain(g)
    return _call(body, name=name, grid=(T // tm,), out_shape=S((T, D), BF16), in_specs=[row, g_spec],
                 out_specs=row, sem=("parallel",), args=(h, g), riders=riders)


def ffn_up_fwd(n, wgu, name, riders=()):
    T, D = n.shape
    F = wgu.shape[1]
    tr, tn = _tile(T, 512), _tile(F, 256)

    def body(n_ref, wg_ref, wu_ref, dgate_ref, dup_ref, a_ref):
        wg, wu = wg_ref[...], wu_ref[...]
        for r in range(T // tr):
            rows = slice(r * tr, (r + 1) * tr)
            x = n_ref[rows, :]
            gate = _nt(x, wg)
            up = _nt(x, wu)
            s = jax.nn.sigmoid(gate)
            silu = gate * s
            dgate_ref[rows, :] = (up * (s * (1.0 + gate * (1.0 - s)))).astype(BF16)
            dup_ref[rows, :] = silu.astype(BF16)
            a_ref[rows, :] = (silu * up).astype(BF16)

    tile = pl.BlockSpec((T, tn), lambda j: (0, j))
    return _call(
        body, name=name, grid=(F // tn,), out_shape=(S((T, F), BF16),) * 3,
        in_specs=[pl.BlockSpec((T, D), lambda j: (0, 0)),
                  pl.BlockSpec((None, tn, D), lambda j: (0, j, 0)), pl.BlockSpec((None, tn, D), lambda j: (1, j, 0))],
        out_specs=(tile, tile, tile), sem=("parallel",), args=(n, wgu, wgu), riders=riders)


def ffn_down_fwd(a, wd, h, g_next, name, riders=()):
    T, F = a.shape
    D = wd.shape[1]
    tm = _tile(T, 256)

    def body(a_ref, w_ref, h_ref, *rest):
        out = h_ref[...] + 0.5 * _nn(a_ref[...], w_ref[...])
        if g_next is None:
            rest[0][...] = out
        else:
            g_ref, o_ref, n_ref = rest
            o_ref[...] = out
            n_ref[...] = _rms(out, g_ref[...]).astype(BF16)

    row = pl.BlockSpec((tm, D), lambda i: (i, 0))
    more = g_next is not None
    g_arg, g_spec = _gain(g_next) if more else (None, None)
    return _call(
        body, name=name, grid=(T // tm,), out_shape=(S((T, D), F32), S((T, D), BF16)) if more else S((T, D), F32),
        in_specs=[pl.BlockSpec((tm, F), lambda i: (i, 0)), pl.BlockSpec((F, D), lambda i: (0, 0)), row] + ([g_spec] if more else []),
        out_specs=(row, row) if more else row,
        sem=("parallel",), args=(a, wd, h) + ((g_arg,) if more else ()), riders=riders)


def mix_in_fwd(h, g, win, name):
    T, D = h.shape
    N = win.shape[0]
    tm = _tile(T, 256)

    def body(h_ref, g_ref, w_ref, n_ref, p_ref):
        n = _rms(h_ref[...], g_ref[...]).astype(BF16)
        n_ref[...] = n
        p_ref[...] = _nt(n, w_ref[...]).astype(BF16)

    g, g_spec = _gain(g)
    return pl.pallas_call(
        body, name=name, grid=(T // tm,), out_shape=(S((T, D), BF16), S((T, N), BF16)),
        in_specs=[pl.BlockSpec((tm, D), lambda i: (i, 0)), g_spec, pl.BlockSpec((N, D), lambda i: (0, 0))],
        out_specs=(pl.BlockSpec((tm, D), lambda i: (i, 0)), pl.BlockSpec((tm, N), lambda i: (i, 0))),
        compiler_params=_params(("parallel",)),
    )(h, g, win)


def _tri_consts():
    r = lax.broadcasted_iota(jnp.int32, (QB, QB), 0)
    c = lax.broadcasted_iota(jnp.int32, (QB, QB), 1)
    ones = jnp.ones((QB, QB), BF16)
    with_sums = lambda tri: jnp.concatenate([tri.astype(BF16), ones], axis=1)
    return with_sums(r > c), with_sums(r <= c), with_sums(r < c)


def _half_masks():
    lane = lax.broadcasted_iota(jnp.int32, (QB, PAIR), 1)
    row = lax.broadcasted_iota(jnp.int32, (QB, PAIR), 0)
    return lane < HEAD_DIM, lane, row


def sb_attn_fwd(p, after, name, riders=()):
    T = p.shape[0]
    nq = T // QB

    def body(q_ref, k_ref, v_ref, m_ref, o_ref, tot_ref, q_sc, acc_ref, z_sc):
        i = pl.program_id(0)
        lo, lane, row = _half_masks()
        causal = lane < row
        heads, pairs = range(SB_HEADS), range(SB_HEADS // 2)
        for hp in pairs:
            q_sc[hp] = (q_ref[:, hp * PAIR:(hp + 1) * PAIR].astype(F32) * SCALE).astype(BF16)
        m2 = m_ref[...]

        def by_head(ref, j, hp):
            t = ref[pl.ds(pl.multiple_of(j * QB, QB), QB), hp * PAIR:(hp + 1) * PAIR]
            return jnp.concatenate([jnp.where(lo, t, 0), jnp.where(lo, 0, t)], axis=0)

        def scores(j):
            return [_nt(q_sc[hp], by_head(k_ref, j, hp)) for hp in pairs]

        def block(j, diag):
            z2 = [z_sc[hp] for hp in pairs]
            ahead = scores(jnp.maximum(j - 1, 0))
            for hp in pairs:
                z_sc[hp] = ahead[hp]
            vs = [by_head(v_ref, j, hp) for hp in pairs]
            spls = [_softplus_logsig(z2[h // 2][:, (h % 2) * QB:(h % 2 + 1) * QB]) for h in heads]
            sp = [jnp.where(causal, spls[h][0], 0.0) if diag else spls[h][0] for h in heads]
            rr = _tri(sp, m2)
            if diag:
                w = [jnp.where(causal, jnp.exp(spls[h][1] - rr[h][:, :QB]), 0.0).astype(BF16) for h in heads]
            else:
                c = [tot_ref[:, h * QB:(h + 1) * QB] for h in heads]
                w = [jnp.exp(spls[h][1] - (c[h] + rr[h][:, :QB])).astype(BF16) for h in heads]
            pv = [_nn(jnp.concatenate([w[2 * hp], w[2 * hp + 1]], axis=1), vs[hp]) for hp in pairs]
            for hp in pairs:
                acc_ref[hp] = pv[hp] if diag else acc_ref[hp] + pv[hp]
            for h in heads:
                tot_ref[:, h * QB:(h + 1) * QB] = rr[h][:, QB:] if diag else c[h] + rr[h][:, QB:]

        first = scores(i)
        for hp in pairs:
            z_sc[hp] = first[hp]
        block(i, True)

        def step(t, carry):
            block(i - 1 - t, False)
            return carry
        lax.fori_loop(0, i, step, 0)
        for hp in pairs:
            o_ref[:, hp * PAIR:(hp + 1) * PAIR] = acc_ref[hp]

    npair = SB_HEADS // 2
    return _call(
        body, name=name, grid=(nq,), out_shape=(S((T, SB_W), F32), S((T, SB_HEADS * QB), F32)),
        in_specs=[pl.BlockSpec((QB, SB_W), lambda i: (i, 0)), pl.BlockSpec((T, SB_W), lambda i: (0, 1)),
                  pl.BlockSpec((T, SB_W), lambda i: (0, 2)), pl.BlockSpec((QB, 2 * QB), lambda i: (0, 0))],
        out_specs=(pl.BlockSpec((QB, SB_W), lambda i: (i, 0)), pl.BlockSpec((QB, SB_HEADS * QB), lambda i: (i, 0))),
        scratch=[pltpu.VMEM((npair, QB, PAIR), BF16), pltpu.VMEM((npair, QB, PAIR), F32), pltpu.VMEM((npair, QB, 2 * QB), F32)],
        sem=("arbitrary",), args=(p, p, p, after), riders=riders,
        marks=((11 * nq) // 16, (14 * nq) // 16))


def sb_attn_bwd(p, do, tot, upto, before, name, riders=()):
    T = p.shape[0]
    nq = T // QB

    def body(q_ref, k_ref, v_ref, do_ref, tot_ref, mp_ref, mg_ref, dq_ref, dk_ref, dv_ref,
             q_sc, d_sc, qd_sc, pg_sc, dq_acc, dk_acc, dv_acc, zd_sc):
        i = pl.program_id(0)
        lo, lane, row = _half_masks()
        causal = lane < row
        heads, pairs = range(SB_HEADS), range(SB_HEADS // 2)

        def by_head(t):
            return jnp.concatenate([jnp.where(lo, t, 0), jnp.where(lo, 0, t)], axis=0)

        for hp in pairs:
            q2 = (q_ref[:, hp * PAIR:(hp + 1) * PAIR].astype(F32) * SCALE).astype(BF16)
            d2 = do_ref[:, hp * PAIR:(hp + 1) * PAIR].astype(BF16)
            q_sc[hp] = q2
            d_sc[hp] = d2
            qd_sc[hp] = by_head(q2)
            qd_sc[SB_HEADS // 2 + hp] = by_head(d2)
        mp, mg = mp_ref[...], mg_ref[...]

        @pl.when(i == 0)
        def _():
            dk_acc[...] = jnp.zeros_like(dk_acc)
            dv_acc[...] = jnp.zeros_like(dv_acc)
        pg_sc[...] = jnp.zeros_like(pg_sc)
        dq_acc[...] = jnp.zeros_like(dq_acc)

        def rows(ref, j, hp):
            return ref[pl.ds(pl.multiple_of(j * QB, QB), QB), hp * PAIR:(hp + 1) * PAIR]

        def products(j):
            return ([_nt(q_sc[hp], by_head(rows(k_ref, j, hp))) for hp in pairs]
                    + [_nt(d_sc[hp], by_head(rows(v_ref, j, hp))) for hp in pairs])

        def block(j, diag):
            r0 = pl.multiple_of(j * QB, QB)
            half = lambda t, h: t[:, (h % 2) * QB:(h % 2 + 1) * QB]
            z = [half(zd_sc[h // 2], h) for h in heads]
            dw = [half(zd_sc[SB_HEADS // 2 + h // 2], h) for h in heads]
            if not diag:
                ahead = products(j + 1)
                for hp in range(SB_HEADS):
                    zd_sc[hp] = ahead[hp]
            ks = [by_head(rows(k_ref, j, hp)) for hp in pairs]
            spls = [_softplus_logsig(z[h]) for h in heads]
            sp = [jnp.where(causal, spls[h][0], 0.0) if diag else spls[h][0] for h in heads]
            rr = _tri(sp, mp)
            pc = [pg_sc[2 * h] for h in heads]
            w = [jnp.exp(spls[h][1] - (tot_ref[:, h * QB:(h + 1) * QB] - (pc[h] + rr[h][:, :QB]))) for h in heads]
            if diag:
                w = [jnp.where(causal, w[h], 0.0) for h in heads]
            gg = [dw[h] * w[h] for h in heads]
            rg = _tri(gg, mg)
            gc = [pg_sc[2 * h + 1] for h in heads]
            dz = [gg[h] - (gg[h] + gc[h] + rg[h][:, :QB]) * jnp.exp(spls[h][1]) for h in heads]
            if diag:
                dz = [jnp.where(causal, dz[h], 0.0) for h in heads]
            dzb = [dz[h].astype(BF16) for h in heads]
            wb = [w[h].astype(BF16) for h in heads]
            both = lambda t, hp, axis: jnp.concatenate([t[2 * hp], t[2 * hp + 1]], axis=axis)
            dq = [_nn(both(dzb, hp, 1), ks[hp]) for hp in pairs]
            dk = [_tn(both(dzb, hp, 0), qd_sc[hp]) for hp in pairs]
            dv = [_tn(both(wb, hp, 0), qd_sc[SB_HEADS // 2 + hp]) for hp in pairs]
            for h in heads:
                if not diag:
                    pg_sc[2 * h] = pc[h] + rr[h][:, QB:]
                    pg_sc[2 * h + 1] = gc[h] + rg[h][:, QB:]
            for hp in pairs:
                dq_acc[hp] += dq[hp]
                dk_acc[pl.ds(r0, QB), hp * PAIR:(hp + 1) * PAIR] += dk[hp]
                dv_acc[pl.ds(r0, QB), hp * PAIR:(hp + 1) * PAIR] += dv[hp]

        first = products(0)
        for hp in range(SB_HEADS):
            zd_sc[hp] = first[hp]

        def step(t, carry):
            block(t, False)
            return carry
        lax.fori_loop(0, i, step, 0)
        block(i, True)
        for hp in pairs:
            dq_ref[:, hp * PAIR:(hp + 1) * PAIR] = (dq_acc[hp] * SCALE).astype(BF16)

        @pl.when(i == nq - 1)
        def _():
            dk_ref[...] = dk_acc[...].astype(BF16)
            dv_ref[...] = dv_acc[...].astype(BF16)

    qtile = pl.BlockSpec((QB, SB_W), lambda i: (i, 0))
    whole = pl.BlockSpec((T, SB_W), lambda i: (0, 0))
    const = pl.BlockSpec((QB, 2 * QB), lambda i: (0, 0))
    return _call(
        body, name=name, grid=(nq,), out_shape=(S((T, SB_W), BF16),) * 3,
        in_specs=[qtile, pl.BlockSpec((T, SB_W), lambda i: (0, 1)), pl.BlockSpec((T, SB_W), lambda i: (0, 2)), qtile,
                  pl.BlockSpec((QB, SB_HEADS * QB), lambda i: (i, 0)), const, const],
        out_specs=(qtile, whole, whole),
        scratch=[pltpu.VMEM((SB_HEADS // 2, QB, PAIR), BF16), pltpu.VMEM((SB_HEADS // 2, QB, PAIR), BF16),
                 pltpu.VMEM((SB_HEADS, 2 * QB, PAIR), BF16),
                 pltpu.VMEM((2 * SB_HEADS, QB, QB), F32), pltpu.VMEM((SB_HEADS // 2, QB, PAIR), F32),
                 pltpu.VMEM((T, SB_W), F32), pltpu.VMEM((T, SB_W), F32), pltpu.VMEM((SB_HEADS, QB, 2 * QB), F32)],
        sem=("arbitrary",), args=(p, p, p, do, tot, upto, before), riders=riders)


def _t5_buckets():
    a = lax.broadcasted_iota(jnp.int32, (QB, QB), 0)
    c = lax.broadcasted_iota(jnp.int32, (QB, QB), 1)

    def bucket(dist):
        dist = jnp.maximum(dist, 0)
        max_exact = N_BUCKETS // 2
        d = jnp.maximum(dist, 1).astype(F32)
        large = max_exact + (jnp.log(d / max_exact) / math.log(MAX_DISTANCE / max_exact)
                             * (N_BUCKETS - max_exact)).astype(jnp.int32)
        large = jnp.minimum(large, N_BUCKETS - 1)
        return jnp.where(dist < max_exact, dist, large)

    return bucket(QB + a - c), bucket(a - c)


def _swa_common(i, kp_ref, kc_ref, vp_ref, vc_ref, bp_ref, bc_ref, rb_ref, bias_ref):
    lo, lane, row = _half_masks()

    @pl.when(i == 0)
    def _():
        for blk, b_ref in enumerate((bp_ref, bc_ref)):
            bk = b_ref[...]
            for h in range(8):
                acc = jnp.zeros((QB, QB), F32)
                for b in range(N_BUCKETS):
                    acc = jnp.where(bk == b, rb_ref[b, h], acc)
                bias_ref[h, blk] = acc

    band = [(lane > row) & (i > 0), lane <= row]

    def stacks(ref):
        t = ref[...].astype(F32)
        sw = pltpu.roll(t, HEAD_DIM, 1)
        return [jnp.concatenate([jnp.where(lo, t, 0.0), jnp.where(lo, 0.0, sw)], axis=0).astype(BF16),
                jnp.concatenate([jnp.where(lo, sw, 0.0), jnp.where(lo, 0.0, t)], axis=0).astype(BF16)]

    ks = [stacks(kp_ref), stacks(kc_ref)]
    vs = [stacks(vp_ref), stacks(vc_ref)]
    return lo, band, ks, vs


def _lane_half(t, h):
    return t[:, (h % 2) * QB:(h % 2 + 1) * QB]


def swa_fwd(p, sinks, rel_bias, bprev, bcur, name, riders=()):
    T = p.shape[0]
    nq = T // QB
    kcol, vcol = (3 * SB_W + SWA_W) // KV_W, (3 * SB_W + SWA_W) // KV_W + 1
    sinks, srow = sinks if isinstance(sinks, tuple) else (sinks, 0)

    def body(q_ref, kp_ref, kc_ref, vp_ref, vc_ref, bp_ref, bc_ref, sink_ref, rb_ref, o_ref, lse_ref, bias_ref):
        i = pl.program_id(0)
        lo, band, ks, vs = _swa_common(i, kp_ref, kc_ref, vp_ref, vc_ref, bp_ref, bc_ref, rb_ref, bias_ref)
        heads, pairs, blocks = range(8), range(4), range(2)
        rowmax = lambda t: jnp.max(t, axis=1, keepdims=True)
        rowsum = lambda t: jnp.sum(t, axis=1, keepdims=True)
        q2 = [q_ref[:, g * PAIR:(g + 1) * PAIR] for g in pairs]
        s2 = [[_nt(q2[g], ks[b][g // 2]) for b in blocks] for g in pairs]
        sc = [[jnp.where(band[b], _lane_half(s2[h // 2][b], h) * SCALE + bias_ref[h, b], NEG_INF) for b in blocks] for h in heads]
        sink = [sink_ref[srow, h] for h in heads]
        m = [jnp.maximum(jnp.maximum(rowmax(sc[h][0]), rowmax(sc[h][1])), sink[h]) for h in heads]
        e = [[jnp.exp(sc[h][b] - m[h]) for b in blocks] for h in heads]
        den = [rowsum(e[h][0]) + rowsum(e[h][1]) + jnp.exp(sink[h] - m[h]) for h in heads]
        pb = [[(e[h][b] / den[h]).astype(BF16) for b in blocks] for h in heads]
        for g in pairs:
            both = lambda b: jnp.concatenate([pb[2 * g][b], pb[2 * g + 1][b]], axis=1)
            o_ref[:, g * PAIR:(g + 1) * PAIR] = _nn(both(0), vs[0][g // 2]) + _nn(both(1), vs[1][g // 2])
        for h in heads:
            lse_ref[:, h * QB:(h + 1) * QB] = jnp.broadcast_to(m[h] + jnp.log(den[h]), (QB, QB))

    kv = lambda col, prev: pl.BlockSpec((QB, KV_W), (lambda i: (jnp.maximum(i - 1, 0), col)) if prev else (lambda i: (i, col)))
    full = pl.BlockSpec((QB, QB), lambda i: (0, 0))
    smem = pl.BlockSpec(memory_space=pltpu.SMEM)
    return _call(
        body, name=name, grid=(nq,), out_shape=(S((T, SWA_W), F32), S((T, 8 * QB), F32)),
        in_specs=[pl.BlockSpec((QB, SWA_W), lambda i: (i, 3)), kv(kcol, True), kv(kcol, False), kv(vcol, True), kv(vcol, False),
                  full, full, smem, smem],
        out_specs=(pl.BlockSpec((QB, SWA_W), lambda i: (i, 0)), pl.BlockSpec((QB, 8 * QB), lambda i: (i, 0))),
        scratch=[pltpu.VMEM((8, 2, QB, QB), F32)],
        sem=("arbitrary",), args=(p, p, p, p, p, bprev, bcur, sinks, rel_bias), riders=riders)


def swa_bwd(p, do, lse, sinks, rel_bias, bprev, bcur, name, riders=()):
    T = p.shape[0]
    nq = T // QB
    kcol, vcol = (3 * SB_W + SWA_W) // KV_W, (3 * SB_W + SWA_W) // KV_W + 1
    sinks, srow = sinks if isinstance(sinks, tuple) else (sinks, 0)

    def body(q_ref, kp_ref, kc_ref, vp_ref, vc_ref, do_ref, lse_ref, bp_ref, bc_ref, sink_ref, rb_ref,
             dq_ref, dk_ref, dv_ref, dsink_ref, dsc_ref, bias_ref, dk_acc, dv_acc):
        i = pl.program_id(0)
        lo, band, ks, vs = _swa_common(i, kp_ref, kc_ref, vp_ref, vc_ref, bp_ref, bc_ref, rb_ref, bias_ref)

        @pl.when(i == 0)
        def _():
            dk_acc[...] = jnp.zeros_like(dk_acc)
            dv_acc[...] = jnp.zeros_like(dv_acc)
            dsc_ref[...] = jnp.zeros_like(dsc_ref)
            dsink_ref[...] = jnp.zeros_like(dsink_ref)

        heads, pairs, blocks = range(8), range(4), range(2)
        rowsum = lambda t: jnp.sum(t, axis=1, keepdims=True)
        by_head = lambda t: jnp.concatenate([jnp.where(lo, t, 0), jnp.where(lo, 0, t)], axis=0)
        q2 = [q_ref[:, g * PAIR:(g + 1) * PAIR] for g in pairs]
        d2 = [do_ref[:, g * PAIR:(g + 1) * PAIR].astype(BF16) for g in pairs]
        qs = [by_head(q2[g]) for g in pairs]
        dos = [by_head(d2[g]) for g in pairs]
        s2 = [[_nt(q2[g], ks[b][g // 2]) for b in blocks] for g in pairs]
        dp2 = [[_nt(d2[g], vs[b][g // 2]) for b in blocks] for g in pairs]
        lse_h = [lse_ref[:, h * QB:(h + 1) * QB] for h in heads]
        sink = [sink_ref[srow, h] for h in heads]
        pr = [[jnp.exp(jnp.where(band[b], _lane_half(s2[h // 2][b], h) * SCALE + bias_ref[h, b], NEG_INF) - lse_h[h])
               for b in blocks] for h in heads]
        dp = [[_lane_half(dp2[h // 2][b], h) for b in blocks] for h in heads]
        delta = [rowsum(pr[h][0] * dp[h][0]) + rowsum(pr[h][1] * dp[h][1]) for h in heads]
        lane1 = lax.broadcasted_iota(jnp.int32, (1, QB), 1)
        dsink = jnp.zeros((1, QB), F32)
        for h in heads:
            dsink = dsink + jnp.where(lane1 == h, -jnp.sum(jnp.exp(sink[h] - lse_h[h][:, :1]) * delta[h]), 0.0)
        dsink_ref[...] += dsink
        dsc = [[pr[h][b] * (dp[h][b] - delta[h]) for b in blocks] for h in heads]
        for h in heads:
            for b in blocks:
                dsc_ref[h, b] += dsc[h][b]
        dzb = [[(dsc[h][b] * SCALE).astype(BF16) for b in blocks] for h in heads]
        prb = [[pr[h][b].astype(BF16) for b in blocks] for h in heads]
        pair_of = lambda t, g, b, axis: jnp.concatenate([t[2 * g][b], t[2 * g + 1][b]], axis=axis)
        for g in pairs:
            dq = _nn(pair_of(dzb, g, 0, 1), ks[0][g // 2]) + _nn(pair_of(dzb, g, 1, 1), ks[1][g // 2])
            dq_ref[:, g * PAIR:(g + 1) * PAIR] = dq.astype(BF16)

        def key_grad(t, other, b):
            per_kv = [_tn(pair_of(t, 2 * kh, b, 0), other[2 * kh]) + _tn(pair_of(t, 2 * kh + 1, b, 0), other[2 * kh + 1]) for kh in range(2)]
            both = [s + pltpu.roll(s, HEAD_DIM, 1) for s in per_kv]
            return jnp.where(lo, both[0], both[1])

        rp = pl.multiple_of(jnp.maximum(i - 1, 0) * QB, QB)
        rc = pl.multiple_of(i * QB, QB)
        dk_acc[pl.ds(rp, QB), :] += key_grad(dzb, qs, 0)
        dv_acc[pl.ds(rp, QB), :] += key_grad(prb, dos, 0)
        dk_acc[pl.ds(rc, QB), :] += key_grad(dzb, qs, 1)
        dv_acc[pl.ds(rc, QB), :] += key_grad(prb, dos, 1)

        @pl.when(i == nq - 1)
        def _():
            dk_ref[...] = dk_acc[...].astype(BF16)
            dv_ref[...] = dv_acc[...].astype(BF16)

    kv = lambda col, prev: pl.BlockSpec((QB, KV_W), (lambda i: (jnp.maximum(i - 1, 0), col)) if prev else (lambda i: (i, col)))
    full = pl.BlockSpec((QB, QB), lambda i: (0, 0))
    smem = pl.BlockSpec(memory_space=pltpu.SMEM)
    whole = lambda shape: pl.BlockSpec(shape, lambda i: (0,) * len(shape))
    return _call(
        body, name=name, grid=(nq,),
        out_shape=(S((T, SWA_W), BF16), S((T, KV_W), BF16), S((T, KV_W), BF16), S((1, QB), F32), S((8, 2, QB, QB), F32)),
        in_specs=[pl.BlockSpec((QB, SWA_W), lambda i: (i, 3)), kv(kcol, True), kv(kcol, False), kv(vcol, True), kv(vcol, False),
                  pl.BlockSpec((QB, SWA_W), lambda i: (i, 0)), pl.BlockSpec((QB, 8 * QB), lambda i: (i, 0)),
                  full, full, smem, smem],
        out_specs=(pl.BlockSpec((QB, SWA_W), lambda i: (i, 0)), whole((T, KV_W)), whole((T, KV_W)), whole((1, QB)),
                   whole((8, 2, QB, QB))),
        scratch=[pltpu.VMEM((8, 2, QB, QB), F32), pltpu.VMEM((T, KV_W), F32), pltpu.VMEM((T, KV_W), F32)],
        sem=("arbitrary",), args=(p, p, p, p, p, do, lse, bprev, bcur, sinks, rel_bias), riders=riders)


def mix_out_fwd(o_sb, o_sw, g_sb, g_sw, wout, h, g_next, name, riders=()):
    T, D = h.shape
    M = SB_W + SWA_W
    tm = _tile(T, 256)

    def body(a_ref, b_ref, ga_ref, gb_ref, w_ref, h_ref, gn_ref, mx_ref, o_ref, n_ref):
        mx_ref[:, :SB_W] = _rms(a_ref[...], ga_ref[...]).astype(BF16)
        mx_ref[:, SB_W:] = _rms(b_ref[...], gb_ref[...]).astype(BF16)
        out = h_ref[...] + _nn(mx_ref[...], w_ref[...])
        o_ref[...] = out
        n_ref[...] = _rms(out, gn_ref[...]).astype(BF16)

    row = lambda n: pl.BlockSpec((tm, n), lambda i: (i, 0))
    (g_sb, sb_spec), (g_sw, sw_spec), (g_next, next_spec) = _gain(g_sb), _gain(g_sw), _gain(g_next)
    return _call(
        body, name=name, grid=(T // tm,), out_shape=(S((T, M), BF16), S((T, D), F32), S((T, D), BF16)),
        in_specs=[row(SB_W), row(SWA_W), sb_spec, sw_spec, pl.BlockSpec((M, D), lambda i: (0, 0)), row(D), next_spec],
        out_specs=(row(M), row(D), row(D)),
        sem=("parallel",), args=(o_sb, o_sw, g_sb, g_sw, wout, h, g_next), riders=riders)


def loss_head(h, g, target, name):
    T, D = h.shape
    tm = _tile(T, 256)

    def body(h_ref, g_ref, t_ref, loss_ref, dh_ref, dhb_ref, dg_ref):
        @pl.when(pl.program_id(0) == 0)
        def _():
            loss_ref[...] = jnp.zeros_like(loss_ref)
            dg_ref[...] = jnp.zeros_like(dg_ref)
        x = h_ref[...]
        err = _rms(x, g_ref[...]) - t_ref[...]
        loss_ref[...] += jnp.full((1, QB), 0.5 * jnp.sum(jnp.mean(err * err, axis=-1)), F32)
        dx, dg = _rms_bwd(err / D, x, g_ref[...])
        dh_ref[...] = dx
        dhb_ref[...] = dx.astype(BF16)
        dg_ref[...] += dg

    row = pl.BlockSpec((tm, D), lambda i: (i, 0))
    vec = pl.BlockSpec((1, D), lambda i: (0, 0))
    return pl.pallas_call(
        body, name=name, grid=(T // tm,), out_shape=(S((1, QB), F32), S((T, D), F32), S((T, D), BF16), S((1, D), F32)),
        in_specs=[row, vec, row], out_specs=(pl.BlockSpec((1, QB), lambda i: (0, 0)), row, row, vec),
        compiler_params=_params(("arbitrary",)),
    )(h, g, target)


def ffn_down_bwd(dhb, wd, gate, up, a, n, name, riders=()):
    T, D = dhb.shape
    F = wd.shape[0]
    tr, tn = _tile(T, 512), _tile(F, 256)

    def body(d_ref, n_ref, w_ref, g_ref, u_ref, a_ref, o_ref, dwd_ref, dwdb_ref, dwgu_ref, dwgub_ref):
        w = w_ref[...]
        for r in range(T // tr):
            rows = slice(r * tr, (r + 1) * tr)
            da = 0.5 * _nt(d_ref[rows, :], w)
            o_ref[0, rows, :] = (da * g_ref[rows, :].astype(F32)).astype(BF16)
            o_ref[1, rows, :] = (da * u_ref[rows, :].astype(F32)).astype(BF16)
        dwd = 0.5 * _tn(a_ref[...], d_ref[...])
        dwd_ref[...] = dwd
        dwdb_ref[...] = dwd.astype(BF16)
        for s in range(2):
            dwgu = _tn(o_ref[s], n_ref[...])
            dwgu_ref[s] = dwgu
            dwgub_ref[s] = dwgu.astype(BF16)

    tile = pl.BlockSpec((T, tn), lambda j: (0, j))
    whole = pl.BlockSpec((T, D), lambda j: (0, 0))
    rows1, rows2 = pl.BlockSpec((tn, D), lambda j: (j, 0)), pl.BlockSpec((2, tn, D), lambda j: (0, j, 0))
    return _call(
        body, name=name, grid=(F // tn,),
        out_shape=(S((2, T, F), BF16), S((F, D), F32), S((F, D), BF16), S((2, F, D), F32), S((2, F, D), BF16)),
        in_specs=[whole, whole, rows1, tile, tile, tile],
        out_specs=(pl.BlockSpec((2, T, tn), lambda j: (0, 0, j)), rows1, rows1, rows2, rows2),
        sem=("parallel",), args=(dhb, n, wd, gate, up, a), riders=riders)


def tn_matmul(xs, y, alpha, name, riders=()):
    B, T, N = xs.shape
    D = y.shape[1]
    tn = _tile(N, 256)

    def body(x_ref, y_ref, o_ref, ob_ref):
        o = alpha * _tn(x_ref[...], y_ref[...])
        o_ref[...] = o
        ob_ref[...] = o.astype(BF16)

    tile = pl.BlockSpec((None, tn, D), lambda s, j: (s, j, 0))
    return _call(
        body, name=name, grid=(B, N // tn), out_shape=(S((B, N, D), F32), S((B, N, D), BF16)),
        in_specs=[pl.BlockSpec((None, T, tn), lambda s, j: (s, 0, j)), pl.BlockSpec((T, D), lambda s, j: (0, 0))],
        out_specs=(tile, tile), sem=("parallel", "parallel"), args=(xs, y), riders=riders)


def nn_rms_bwd(xs, ws, h_in, g, dh, name, riders=()):
    B, T, K = xs.shape
    D = ws.shape[2]
    tm = _tile(T, 256)

    def body(x_ref, w_ref, h_ref, g_ref, d_ref, o_ref, ob_ref, dg_ref):
        @pl.when(pl.program_id(0) == 0)
        def _():
            dg_ref[...] = jnp.zeros_like(dg_ref)
        dn = _nn(x_ref[0], w_ref[0])
        for s in range(1, B):
            dn = dn + _nn(x_ref[s], w_ref[s])
        dx, dg = _rms_bwd(dn, h_ref[...], g_ref[...])
        out = d_ref[...] + dx
        o_ref[...] = out
        ob_ref[...] = out.astype(BF16)
        dg_ref[...] += dg

    row = pl.BlockSpec((tm, D), lambda i: (i, 0))
    vec = pl.BlockSpec((1, D), lambda i: (0, 0))
    g, g_spec = _gain(g)
    return _call(
        body, name=name, grid=(T // tm,), out_shape=(S((T, D), F32), S((T, D), BF16), S((1, D), F32)),
        in_specs=[pl.BlockSpec((B, tm, K), lambda i: (0, i, 0)), pl.BlockSpec((B, K, D), lambda i: (0, 0, 0)), row, g_spec, row],
        out_specs=(row, row, vec),
        sem=("arbitrary",), args=(xs, ws, h_in, g, dh), riders=riders)


def mix_out_bwd(dhb, wout, mixed, o_sb, o_sw, g_sb, g_sw, name):
    T, D = dhb.shape
    M = SB_W + SWA_W
    tm = _tile(T, 256)
    steps = T // tm

    def body(d_ref, w_ref, mx_ref, a_ref, b_ref, ga_ref, gb_ref, da_ref, db_ref, dga_ref, dgb_ref, dw_ref, dwb_ref):
        i = pl.program_id(0)

        @pl.when(i == 0)
        def _():
            dga_ref[...] = jnp.zeros_like(dga_ref)
            dgb_ref[...] = jnp.zeros_like(dgb_ref)
            dw_ref[...] = jnp.zeros_like(dw_ref)
        dm = _nt(d_ref[...], w_ref[...])
        dxa, dga = _rms_bwd(dm[:, :SB_W], a_ref[...], ga_ref[...])
        dxb, dgb = _rms_bwd(dm[:, SB_W:], b_ref[...], gb_ref[...])
        da_ref[...] = dxa
        db_ref[...] = dxb
        dga_ref[...] += dga
        dgb_ref[...] += dgb
        dw_ref[...] += _tn(mx_ref[...], d_ref[...])

        @pl.when(i == steps - 1)
        def _():
            dwb_ref[...] = dw_ref[...].astype(BF16)

    row = lambda n: pl.BlockSpec((tm, n), lambda i: (i, 0))
    vec = lambda n: pl.BlockSpec((1, n), lambda i: (0, 0))
    whole = pl.BlockSpec((M, D), lambda i: (0, 0))
    (g_sb, sb_spec), (g_sw, sw_spec) = _gain(g_sb), _gain(g_sw)
    return pl.pallas_call(
        body, name=name, grid=(steps,),
        out_shape=(S((T, SB_W), F32), S((T, SWA_W), F32), S((1, SB_W), F32), S((1, SWA_W), F32), S((M, D), F32), S((M, D), BF16)),
        in_specs=[row(D), whole, row(M), row(SB_W), row(SWA_W), sb_spec, sw_spec],
        out_specs=(row(SB_W), row(SWA_W), vec(SB_W), vec(SWA_W), whole, whole),
        compiler_params=_params(("arbitrary",)),
    )(dhb, wout, mixed, o_sb, o_sw, g_sb, g_sw)


def rel_bias_grad(dscs, bprev, bcur, name):
    n = len(dscs)

    def body(*refs):
        bp_ref, bc_ref, o_ref = refs[n], refs[n + 1], refs[n + 2]
        bks = [bp_ref[...], bc_ref[...]]
        row = lax.broadcasted_iota(jnp.int32, (N_BUCKETS, QB), 0)
        lane = lax.broadcasted_iota(jnp.int32, (N_BUCKETS, QB), 1)
        out = jnp.zeros((N_BUCKETS, QB), F32)
        for h in range(8):
            tot = [sum(refs[l][h, b] for l in range(n)) for b in range(2)]
            for b in range(N_BUCKETS):
                val = jnp.sum(jnp.where(bks[0] == b, tot[0], 0.0)) + jnp.sum(jnp.where(bks[1] == b, tot[1], 0.0))
                out = jnp.where((row == b) & (lane == h), val, out)
        o_ref[...] = out

    return pl.pallas_call(body, name=name, out_shape=S((N_BUCKETS, QB), F32), compiler_params=_params())(*dscs, bprev, bcur)


def _adamw(w, g, m, v):
    m = ADAM_B1 * m + (1.0 - ADAM_B1) * g
    v = ADAM_B2 * v + (1.0 - ADAM_B2) * (g * g)
    m_hat = m / (1.0 - ADAM_B1 ** ADAM_STEP)
    v_hat = v / (1.0 - ADAM_B2 ** ADAM_STEP)
    delta = -ADAM_LR * (m_hat / (jnp.sqrt(v_hat) + ADAM_EPS) + ADAM_WD * w)
    return delta, m, v


def adamw_scattered(w, m, v, owns, others, name, riders=(), rows=176):
    L, R, C = w.shape
    tr = _rows_tile(R, rows)

    def body(w_ref, m_ref, v_ref, *rest):
        own_refs, other_refs = rest[:L], rest[L:2 * L]
        g_ref, d_ref, mo_ref, vo_ref = rest[2 * L:]
        layer = pl.program_id(0)

        def grad(k):
            o = other_refs[k]
            return own_refs[k][...] + o[0].astype(F32) + o[1].astype(F32) + o[2].astype(F32)

        g = grad(0)
        for k in range(1, L):
            g = jnp.where(layer == k, grad(k), g)
        d, mn, vn = _adamw(w_ref[...], g, m_ref[...], v_ref[...])
        g_ref[...] = g
        d_ref[...] = d
        mo_ref[...] = mn
        vo_ref[...] = vn

    tile = pl.BlockSpec((None, tr, C), lambda l, i: (l, i, 0))
    return _call(
        body, name=name, grid=(L, R // tr), out_shape=(S((L, R, C), F32),) * 4,
        in_specs=[tile] * 3 + [pl.BlockSpec((tr, C), lambda l, i: (i, 0))] * L + [pl.BlockSpec((3, tr, C), lambda l, i: (0, i, 0))] * L,
        out_specs=(tile,) * 4, sem=("parallel", "parallel"), args=(w, m, v, *owns, *others), riders=riders)


def adamw_small(w, gs, m, v, name):
    R, C = w.shape

    def body(w_ref, g_ref, m_ref, v_ref, go_ref, d_ref, mo_ref, vo_ref):
        g = g_ref[0]
        for k in range(1, N_DEV):
            g = g + g_ref[k]
        d, mn, vn = _adamw(w_ref[...], g, m_ref[...], v_ref[...])
        go_ref[...] = g
        d_ref[...] = d
        mo_ref[...] = mn
        vo_ref[...] = vn

    return pl.pallas_call(body, name=name, out_shape=(S((R, C), F32),) * 4, compiler_params=_params())(w, gs, m, v)


def kernel(x, norm_ffn1, w_ffn1_gu, w_ffn1_down, norm_mix, w_in, sinks, norm_out_sb, norm_out_swa, w_out, norm_ffn2, w_ffn2_gu, w_ffn2_down, rel_bias, norm_final, loss_target, m_norm_ffn1, m_w_ffn1_gu, m_w_ffn1_down, m_norm_mix, m_w_in, m_sinks, m_norm_out_sb, m_norm_out_swa, m_w_out, m_norm_ffn2, m_w_ffn2_gu, m_w_ffn2_down, m_rel_bias, m_norm_final, v_norm_ffn1, v_w_ffn1_gu, v_w_ffn1_down, v_norm_mix, v_w_in, v_sinks, v_norm_out_sb, v_norm_out_swa, v_w_out, v_norm_ffn2, v_w_ffn2_gu, v_w_ffn2_down, v_rel_bias, v_norm_final):
    L = norm_ffn1.shape[0]
    T, D = x.shape[1], x.shape[2]
    F = w_ffn1_down.shape[1] * N_DEV
    h = x.reshape(T, D)
    target = loss_target.reshape(T, D)
    after, upto, before = _tri_consts()
    bprev, bcur = _t5_buckets()

    local = {}
    for l in range(L):
        local[f"gu1_{l}"] = w_ffn1_gu[l].T.astype(BF16)
        local[f"d1_{l}"] = w_ffn1_down[l].astype(BF16)
        local[f"in_{l}"] = w_in[l].T.astype(BF16)
        local[f"out_{l}"] = w_out[l].astype(BF16)
        local[f"gu2_{l}"] = w_ffn2_gu[l].T.astype(BF16)
        local[f"d2_{l}"] = w_ffn2_down[l].astype(BF16)
    full, partial = {}, {}
    grads, chip_sum, recv_b = {}, {}, {}

    def run(fn, *args, ag=(), rs1=(), rs2=()):
        halves = lambda names: [n if isinstance(n, tuple) else (n, None) for n in names]
        ag, rs2 = [(n, k) for n, k in halves(ag) if n in local], halves(rs2)
        rows = lambda k, total: None if k is None else (k * (total // 2), total // 2)

        def second(n, k):
            sb = chip_sum[n][1]
            return scatter_second(sb, rows(k, sb.shape[1]), recv_b.get(n))

        riders = ([gather(local[n], rows(k, local[n].shape[0]), partial.get(n)) for n, k in ag]
                  + [scatter_first(grads[n][1]) for n in rs1] + [second(n, k) for n, k in rs2])
        if not riders:
            return fn(*args)
        outs, per = fn(*args, riders=riders)
        per = [p[0] for p in per]
        for n, k in ag:
            buf = per.pop(0)
            if k == 0:
                partial[n] = buf
            else:
                full[n] = buf.reshape(N_DEV * buf.shape[1], D)
        if rs1:
            sums = scatter_add([grads[n][0] for n in rs1], [per.pop(0) for n in rs1], "rs_add_" + "_".join(rs1))
            chip_sum.update(zip(rs1, sums))
        for n, _ in rs2:
            recv_b[n] = per.pop(0)
        return outs

    def attn_fwd(p, sink, name, riders=()):
        return side_by_side(sb_attn_fwd(p, after, name, riders=PARTS), swa_fwd(p, sink, rel_bias, bprev, bcur, name, riders=PARTS),
                            name, riders)

    def attn_bwd(p, do_sb, tot, do_sw, lse, sink, name, riders=()):
        return side_by_side(sb_attn_bwd(p, do_sb, tot, upto, before, name, riders=PARTS),
                            swa_bwd(p, do_sw, lse, sink, rel_bias, bprev, bcur, name, riders=PARTS), name, riders)

    gu = lambda n: full[n].reshape(2, F, D)
    slots = lambda pair: tuple(t.reshape(N_DEV, -1, D) for t in pair)
    vec = lambda a: a.reshape(1, -1)

    PW = max(D, SB_W + SWA_W)
    n_rows = 4 * L + 2
    n_rows += (-n_rows) % 8

    def pack(ffn1, mix, ffn2, final, osb, osw, snk, rel, extra):
        pieces = []

        def row(*parts):
            flat = [a.reshape(-1) for a in parts]
            pieces.extend(flat)
            used = sum(a.size for a in flat)
            if used < PW:
                pieces.append(jnp.zeros((PW - used,), F32))

        for group in (ffn1, mix, ffn2):
            for l in range(L):
                row(group[l])
        row(final)
        for l in range(L):
            row(osb[l], osw[l])
        row(*[snk[l].reshape(-1)[:8] for l in range(L)], rel, extra)
        pieces.append(jnp.zeros(((n_rows - 4 * L - 2) * PW,), F32))
        return jnp.concatenate(pieces).reshape(n_rows, PW)

    def unpack(arr):
        ffn1, mix, ffn2 = arr[0:L, :D], arr[L:2 * L, :D], arr[2 * L:3 * L, :D]
        final = arr[3 * L, :D]
        ob = arr[3 * L + 1:4 * L + 1]
        tail = arr[4 * L + 1]
        return (ffn1, mix, tail[:8 * L].reshape(L, 8), ob[:, :SB_W], ob[:, SB_W:SB_W + SWA_W], ffn2,
                tail[8 * L:8 * L + N_BUCKETS * 8].reshape(N_BUCKETS, 8), final)

    zero = jnp.zeros((1,), F32)
    w_small = pack(norm_ffn1, norm_mix, norm_ffn2, norm_final, norm_out_sb, norm_out_swa, sinks, rel_bias, zero)
    g_ffn1, g_mix, g_ffn2, g_osb, g_osw = [a.reshape(L, 1, -1) for a in (norm_ffn1, norm_mix, norm_ffn2, norm_out_sb, norm_out_swa)]

    saved = []
    n_next = run(rms_cast, h, (g_ffn1, 0), "rms_first", ag=("gu1_0",))
    for l in range(L):
        nx = l + 1
        s = {"h0": h, "n1": n_next}
        s["gate1"], s["up1"], s["a1"] = run(ffn_up_fwd, s["n1"], gu(f"gu1_{l}"), f"ffn1_up{l}",
                                            ag=(f"d1_{l}", ("in_0", 0) if l == 0 else (f"in_{l}", 1)))
        h = run(ffn_down_fwd, s["a1"], full[f"d1_{l}"], h, None, f"ffn1_down{l}", ag=(("in_0", 1), "out_0") if l == 0 else ())
        s["h1"] = h
        s["n2"], s["p"] = mix_in_fwd(h, (g_mix, l), full[f"in_{l}"], f"mix_in{l}")
        s["o_sb"], s["tot"], s["o_sw"], s["lse"] = run(attn_fwd, s["p"], (sinks, l), f"attn_fwd{l}",
                                                       ag=((f"out_{l}",) if l else ()) + (f"gu2_{l}", f"d2_{l}", (f"gu1_{nx}", 0)))
        s["mixed"], h, s["n3"] = run(mix_out_fwd, s["o_sb"], s["o_sw"], (g_osb, l), (g_osw, l),
                                     full[f"out_{l}"], h, (g_ffn2, l), f"mix_out{l}")
        s["h2"] = h
        s["gate2"], s["up2"], s["a2"] = run(ffn_up_fwd, s["n3"], gu(f"gu2_{l}"), f"ffn2_up{l}",
                                            ag=((f"gu1_{nx}", 1), (f"in_{nx}", 0)))
        if nx < L:
            h, n_next = run(ffn_down_fwd, s["a2"], full[f"d2_{l}"], h, (g_ffn1, nx), f"ffn2_down{l}")
        else:
            h = run(ffn_down_fwd, s["a2"], full[f"d2_{l}"], h, None, f"ffn2_down{l}")
        saved.append(s)

    loss_part, dh, dhb, dg_final = loss_head(h, vec(norm_final), target, "loss_head")

    small = {k: [None] * L for k in ("ffn1", "mix", "sinks", "osb", "osw", "ffn2", "dsc")}
    for l in reversed(range(L)):
        s = saved[l]

        def ffn_bwd(dh, dhb, tag, gate, up, a, n, h_in, g, r_down, r_up):
            gu_n, d_n = f"gu{tag}_{l}", f"d{tag}_{l}"
            dgu, dwd, dwdb, dwgu, dwgub = run(ffn_down_bwd, dhb, full[d_n], gate, up, a, n, f"ffn{tag}_down_bwd{l}", **r_down)
            grads[gu_n], grads[d_n] = slots((dwgu, dwgub)), slots((dwd, dwdb))
            return run(nn_rms_bwd, dgu, gu(gu_n), h_in, g, dh, f"ffn{tag}_up_bwd{l}", **r_up)

        later = l + 1 < L
        dh, dhb, small["ffn2"][l] = ffn_bwd(dh, dhb, 2, s["gate2"], s["up2"], s["a2"], s["n3"], s["h2"], (g_ffn2, l),
                                            dict(rs2=((f"gu1_{l + 1}", 0), f"d1_{l + 1}") if later else ()),
                                            dict(rs1=(f"gu2_{l}", f"d2_{l}")))
        do_sb, do_sw, small["osb"][l], small["osw"][l], dw_out, dw_out_b = mix_out_bwd(
            dhb, full[f"out_{l}"], s["mixed"], s["o_sb"], s["o_sw"], (g_osb, l), (g_osw, l), f"mix_out_bwd{l}")
        grads[f"out_{l}"] = slots((dw_out, dw_out_b))
        dq_sb, dk_sb, dv_sb, dq_sw, dk_sw, dv_sw, small["sinks"][l], small["dsc"][l] = run(
            attn_bwd, s["p"], do_sb, s["tot"], do_sw, s["lse"], (sinks, l), f"attn_bwd{l}",
            rs2=(f"gu2_{l}", f"d2_{l}") + (((f"gu1_{l + 1}", 1),) if later else ()), rs1=(f"out_{l}",))
        dp = jnp.concatenate([dq_sb, dk_sb, dv_sb, dq_sw, dk_sw, dv_sw], axis=1)
        dh, dhb, small["mix"][l] = nn_rms_bwd(dp[None], full[f"in_{l}"][None], s["h1"], (g_mix, l), dh, f"mix_in_bwd{l}")
        grads[f"in_{l}"] = slots(tn_matmul(dp[None], s["n2"], 1.0, f"dwin{l}"))
        dh, dhb, small["ffn1"][l] = ffn_bwd(dh, dhb, 1, s["gate1"], s["up1"], s["a1"], s["n1"], s["h0"], (g_ffn1, l),
                                            dict(rs1=(f"in_{l}",), rs2=(f"out_{l}",)),
                                            dict(rs1=(f"gu1_{l}", f"d1_{l}"), rs2=(f"in_{l}",)))

    grad_x = dh.reshape(x.shape)

    upd = {}
    turn_of = lambda transposed: (lambda a: jnp.swapaxes(a, 1, 2)) if transposed else (lambda a: a)

    def update(nm, w, m, v, transposed, riders=()):
        turn = turn_of(transposed)
        names = [f"{nm}_{l}" for l in range(L)]
        return adamw_scattered(turn(w), turn(m), turn(v), [chip_sum[n][0] for n in names], [recv_b[n] for n in names],
                               f"adamw_{nm}", riders=riders, rows=88 if riders is PARTS else 176)

    early = (("gu2", w_ffn2_gu, m_w_ffn2_gu, v_w_ffn2_gu, True), ("d2", w_ffn2_down, m_w_ffn2_down, v_w_ffn2_down, False),
             ("in", w_in, m_w_in, v_w_in, True), ("out", w_out, m_w_out, v_w_out, False))
    res = run(lambda name, riders=(): in_one_call([update(*e, riders=PARTS) for e in early], name, riders),
              "adamw_early", rs2=("d1_0", ("gu1_0", 0)))
    for k, e in enumerate(early):
        upd[e[0]] = tuple(turn_of(e[4])(r) for r in res[4 * k:4 * k + 4])
    res = run(lambda name, riders=(): update("d1", w_ffn1_down, m_w_ffn1_down, v_w_ffn1_down, False, riders=riders),
              "adamw_d1", rs2=(("gu1_0", 1),))
    upd["d1"] = tuple(res)
    upd["gu1"] = tuple(turn_of(True)(r) for r in update("gu1", w_ffn1_gu, m_w_ffn1_gu, v_w_ffn1_gu, True))

    d_rel = rel_bias_grad(small["dsc"], bprev, bcur, "rel_bias_grad")[:, :8]
    g_small = pack(small["ffn1"], small["mix"], small["ffn2"], dg_final, small["osb"], small["osw"], small["sinks"], d_rel,
                   loss_part[0, :1])
    m_small = pack(m_norm_ffn1, m_norm_mix, m_norm_ffn2, m_norm_final, m_norm_out_sb, m_norm_out_swa, m_sinks, m_rel_bias, zero)
    v_small = pack(v_norm_ffn1, v_norm_mix, v_norm_ffn2, v_norm_final, v_norm_out_sb, v_norm_out_swa, v_sinks, v_rel_bias, zero)
    gs_small = all_gather_rows(g_small, "ag_small")
    summed = adamw_small(w_small, gs_small, m_small, v_small, "adamw_small")
    small_out = [unpack(a) for a in summed]
    loss = summed[0][4 * L + 1, 8 * L + N_BUCKETS * 8]

    def group(k):
        sm = small_out[k]
        return (sm[0], upd["gu1"][k], upd["d1"][k], sm[1], upd["in"][k], sm[2], sm[3], sm[4], upd["out"][k], sm[5],
                upd["gu2"][k], upd["d2"][k], sm[6], sm[7])

    return (loss, grad_x, *group(0), *group(1), *group(2), *group(3))
```

```python
import math

import jax
import jax.numpy as jnp
from jax import lax
from jax.experimental import pallas as pl
from jax.experimental.pallas import tpu as pltpu

F32 = jnp.float32
BF16 = jnp.bfloat16
S = jax.ShapeDtypeStruct

N_DEV = 8
HEAD_DIM = 64
SB_HEADS = 8
PAIR = 2 * HEAD_DIM
SB_W = 512
SWA_W = 512
KV_W = 128
IN_W = 3 * SB_W + SWA_W + 2 * KV_W
QB = 128
N_BUCKETS = 32
MAX_DISTANCE = 128
EPS = 1e-6
NEG_INF = -1e30
SCALE = HEAD_DIM ** -0.5

ADAM_LR = 0.001
ADAM_B1 = 0.9
ADAM_B2 = 0.999
ADAM_EPS = 1e-08
ADAM_WD = 0.01
ADAM_STEP = 10

VMEM_LIMIT = 56 * 1024 * 1024
MESH = pl.DeviceIdType.MESH


def _params(sem=None, vmem=VMEM_LIMIT):
    return pltpu.CompilerParams(dimension_semantics=sem, vmem_limit_bytes=vmem)


def _nn(a, b):
    return jnp.dot(a, b, preferred_element_type=F32)


def _nt(a, b):
    return lax.dot_general(a, b, (((1,), (1,)), ((), ())), preferred_element_type=F32)


def _tn(a, b):
    return lax.dot_general(a, b, (((0,), (0,)), ((), ())), preferred_element_type=F32)


def _tri(xs, m):
    return [_nn(x.astype(BF16), m) for x in xs]


def _rms(x, g):
    r = lax.rsqrt(jnp.mean(x * x, axis=-1, keepdims=True) + EPS)
    return x * r * g


def _rms_bwd(dy, x, g):
    r = lax.rsqrt(jnp.mean(x * x, axis=-1, keepdims=True) + EPS)
    xhat = x * r
    u = dy * g
    dx = r * (u - xhat * jnp.mean(u * xhat, axis=-1, keepdims=True))
    return dx, jnp.sum(dy * xhat, axis=0, keepdims=True)


def _softplus_logsig(z):
    sp = jnp.maximum(z, 0.0) + jnp.log(1.0 + jnp.exp(-jnp.abs(z)))
    return sp, z - sp


def _gain(g):
    if isinstance(g, tuple):
        rows, n = g
        return rows, pl.BlockSpec((None, 1, rows.shape[2]), lambda *_: (n, 0, 0))
    return g, pl.BlockSpec((1, g.shape[1]), lambda *_: (0, 0))


def _tile(n, want):
    t = min(n, want)
    while n % t:
        t //= 2
    return t


def _place():
    x, y, c = lax.axis_index("x"), lax.axis_index("y"), lax.axis_index("c")
    chips = [(1 - x, y), (x, 1 - y), (1 - x, 1 - y)]
    return x, y, c, chips


def all_gather_rows(v, name):
    R, C = v.shape

    def body(v_ref, out_ref, send_sems, recv_sems, local_sem):
        x, y, c, chips = _place()
        me, sibling = (x, y, c), (x, y, 1 - c)

        def slot(px, py, pc):
            return out_ref.at[4 * px + 2 * py + pc]

        def copy(k, block, to, src=None):
            return pltpu.make_async_remote_copy(
                src_ref=slot(*block) if src is None else src, dst_ref=slot(*block),
                send_sem=send_sems.at[k], recv_sem=recv_sems.at[k], device_id=to, device_id_type=MESH)

        mine = pltpu.make_async_copy(v_ref, slot(*me), local_sem)
        mine.start()
        first = [copy(0, me, sibling, src=v_ref)]
        first += [copy(1 + j, me, (*chip, c), src=v_ref) for j, chip in enumerate(chips)]
        for cp in first:
            cp.start()
        passed = [copy(4 + j, (*chip, c), sibling) for j, chip in enumerate(chips)]
        for j, chip in enumerate(chips):
            copy(1 + j, (*chip, c), me).wait_recv()
            passed[j].start()
        copy(0, sibling, me).wait_recv()
        for j, chip in enumerate(chips):
            copy(4 + j, (*chip, 1 - c), me).wait_recv()
        for cp in first + passed:
            cp.wait_send()
        mine.wait()

    return pl.pallas_call(
        body, name=name, out_shape=S((N_DEV, R, C), v.dtype),
        in_specs=[pl.BlockSpec(memory_space=pl.ANY)], out_specs=pl.BlockSpec(memory_space=pl.ANY),
        scratch_shapes=[pltpu.SemaphoreType.DMA((7,)), pltpu.SemaphoreType.DMA((7,)), pltpu.SemaphoreType.DMA],
    )(v)


class _Exchange:
    def __init__(self, ins, outs, sizes, n_local, plan, aliases=None):
        self.ins, self.outs, self.plan, self.aliases = list(ins), list(outs), plan, aliases or {}
        self.sizes, self.n_local = list(sizes), n_local

    def scratch(self):
        n = sum(self.sizes)
        return [pltpu.SemaphoreType.DMA((n,)), pltpu.SemaphoreType.DMA((n,)), pltpu.SemaphoreType.DMA((max(self.n_local, 1),))]

    def _copies(self, in_refs, out_refs, sems):
        send_sems, recv_sems, local_sems = sems
        phases, local = self.plan(in_refs, out_refs)
        out, k = [], 0
        for phase in phases:
            out.append([pltpu.make_async_remote_copy(src_ref=s, dst_ref=d, send_sem=send_sems.at[k + n], recv_sem=recv_sems.at[k + n],
                                                     device_id=dev, device_id_type=MESH) for n, (s, d, dev) in enumerate(phase)])
            k += len(phase)
        return out, [pltpu.make_async_copy(s, d, local_sems.at[n]) for n, (s, d) in enumerate(local)]

    def start(self, in_refs, out_refs, sems):
        phases, loc = self._copies(in_refs, out_refs, sems)
        for cp in phases[0] + loc:
            cp.start()

    def advance(self, hook, in_refs, out_refs, sems):
        p = hook - (3 - len(self.sizes))
        if p >= 1:
            phases, _ = self._copies(in_refs, out_refs, sems)
            for cp in phases[p - 1]:
                cp.wait_recv()
            for cp in phases[p]:
                cp.start()

    def finish(self, in_refs, out_refs, sems):
        phases, loc = self._copies(in_refs, out_refs, sems)
        for cp in phases[-1]:
            cp.wait_recv()
        for phase in phases:
            for cp in phase:
                cp.wait_send()
        for cp in loc:
            cp.wait()


def gather(v, rows=None, into=None):
    R, C = v.shape
    r0, nr = rows or (0, R)
    na = min(nr, ((nr // 2 + 15) // 16) * 16)

    def plan(ins, outs):
        x, y, c, _ = _place()
        xn, yn, dg, sibling = (1 - x, y), (x, 1 - y), (1 - x, 1 - y), (x, y, 1 - c)
        slot = lambda chip, start=r0, count=nr: outs[0].at[4 * chip[0] + 2 * chip[1] + c, pl.ds(start, count), :]
        src, mine = ins[0].at[pl.ds(r0, nr), :], slot((x, y))
        same = lambda ref, to: (ref, ref, to)
        first = [(src, mine, sibling), (src, mine, (*xn, c)), (src, mine, (*yn, c))]
        relay = [same(slot(xn, r0, na), (*yn, c)), same(slot(yn, r0 + na, nr - na), (*xn, c))]
        onward = [same(slot(xn), sibling), same(slot(yn), sibling), same(slot(dg), sibling)]
        return [first, relay, onward], [(src, mine)]

    if into is None:
        return _Exchange([v], [S((N_DEV, R, C), v.dtype)], (3, 2, 3), 1, plan)
    return _Exchange([v, into], [S((N_DEV, R, C), v.dtype)], (3, 2, 3), 1, plan, aliases={1: 0})


def scatter_first(gb):
    _, R, C = gb.shape

    def plan(ins, outs):
        x, y, c, chips = _place()
        owners = [(x, y)] + chips
        return [[(ins[0].at[4 * px + 2 * py + (1 - c)], outs[0].at[j], (x, y, 1 - c)) for j, (px, py) in enumerate(owners)]], []

    return _Exchange([gb], [S((4, R, C), BF16)], (4,), 0, plan)


def scatter_second(sb, rows=None, into=None):
    r0, nr = rows or (0, sb.shape[1])

    def plan(ins, outs):
        x, y, c, chips = _place()
        part = lambda ref, j: ref.at[j, pl.ds(r0, nr), :]
        return [[(part(ins[0], j), part(outs[0], j), (*chips[j], c)) for j in range(3)]], []

    if into is None:
        return _Exchange([sb], [S(sb.shape, BF16)], (3,), 0, plan)
    return _Exchange([sb, into], [S(sb.shape, BF16)], (3,), 0, plan, aliases={1: 0})


PARTS = "parts"


def _call(body, *, name, grid, in_specs, out_specs, out_shape, args, scratch=(), sem=None, riders=(), marks=None):
    single = not isinstance(out_shape, (tuple, list))
    out_shape = (out_shape,) if single else tuple(out_shape)
    out_specs = (out_specs,) if single else tuple(out_specs)
    n_in, n_out, n_sc = len(in_specs), len(out_shape), len(scratch)
    if riders is PARTS:
        return dict(body=body, grid=grid, in_specs=list(in_specs), out_specs=out_specs, out_shape=out_shape, args=tuple(args),
                    scratch=list(scratch), marks=marks)
    if not riders:
        res = pl.pallas_call(body, name=name, grid=grid, in_specs=list(in_specs), out_specs=out_specs, out_shape=out_shape,
                             scratch_shapes=list(scratch), compiler_params=_params(sem))(*args)
        return res[0] if single else res
    r_ins = [a for r in riders for a in r.ins]
    r_outs = [o for r in riders for o in r.outs]
    r_scr = [s for r in riders for s in r.scratch()]
    aliases, i0, o0 = {}, n_in, n_out
    for r in riders:
        for a, b in r.aliases.items():
            aliases[i0 + a] = o0 + b
        i0, o0 = i0 + len(r.ins), o0 + len(r.outs)
    steps = math.prod(grid)

    def full(*refs):
        ins, rin = refs[:n_in], refs[n_in:n_in + len(r_ins)]
        pos = n_in + len(r_ins)
        outs, rout = refs[pos:pos + n_out], refs[pos + n_out:pos + n_out + len(r_outs)]
        pos += n_out + len(r_outs)
        sc, rsc = refs[pos:pos + n_sc], refs[pos + n_sc:]
        step = 0
        for d, n in enumerate(grid):
            step = step * n + pl.program_id(d)

        def each(method, *lead):
            i, o = 0, 0
            for k, r in enumerate(riders):
                getattr(r, method)(*lead, rin[i:i + len(r.ins)], rout[o:o + len(r.outs)], rsc[3 * k:3 * k + 3])
                i, o = i + len(r.ins), o + len(r.outs)

        @pl.when(step == 0)
        def _():
            each("start")
        body(*ins, *outs, *sc)

        late = max(steps - 1 - max(steps // 8, 1), 0)
        first, second = marks or (min((3 * steps) // 5, late), late)

        @pl.when(step == first)
        def _():
            each("advance", 1)

        @pl.when(step == second)
        def _():
            each("advance", 2)

        @pl.when(step == steps - 1)
        def _():
            each("finish")

    anywhere = pl.BlockSpec(memory_space=pl.ANY)
    res = pl.pallas_call(
        full, name=name, grid=grid, in_specs=list(in_specs) + [anywhere] * len(r_ins),
        out_specs=out_specs + (anywhere,) * len(r_outs), out_shape=out_shape + tuple(r_outs),
        scratch_shapes=list(scratch) + r_scr, input_output_aliases=aliases,
        compiler_params=_params(("arbitrary",) * len(grid)))(*args, *r_ins)
    host, rest, per = res[:n_out], list(res[n_out:]), []
    for r in riders:
        per.append(rest[:len(r.outs)])
        rest = rest[len(r.outs):]
    return (host[0] if single else tuple(host)), per


def side_by_side(first, second, name, riders=()):
    a_in, a_out, a_sc = len(first["in_specs"]), len(first["out_shape"]), len(first["scratch"])
    n_in, n_out = a_in + len(second["in_specs"]), a_out + len(second["out_shape"])

    def body(*refs):
        ins, outs, sc = refs[:n_in], refs[n_in:n_in + n_out], refs[n_in + n_out:]
        first["body"](*ins[:a_in], *outs[:a_out], *sc[:a_sc])
        second["body"](*ins[a_in:], *outs[a_out:], *sc[a_sc:])

    return _call(body, name=name, grid=first["grid"], in_specs=first["in_specs"] + second["in_specs"],
                 out_specs=first["out_specs"] + second["out_specs"], out_shape=first["out_shape"] + second["out_shape"],
                 args=first["args"] + second["args"], scratch=first["scratch"] + second["scratch"],
                 sem=("arbitrary",) * len(first["grid"]), riders=riders, marks=first["marks"])


def in_one_call(parts, name, riders=()):
    extents = [p["grid"][-1] for p in parts]
    longest = max(extents)

    def clamp(spec, n):
        if n == longest or spec.index_map is None:
            return spec
        return pl.BlockSpec(spec.block_shape, lambda *ids, f=spec.index_map: f(*ids[:-1], jnp.minimum(ids[-1], n - 1)))

    counts = [(len(p["in_specs"]), len(p["out_shape"]), len(p["scratch"])) for p in parts]
    n_in, n_out = sum(c[0] for c in counts), sum(c[1] for c in counts)

    def body(*refs):
        ins, outs, sc = refs[:n_in], refs[n_in:n_in + n_out], refs[n_in + n_out:]
        i = o = s = 0
        for p, n, (ci, co, cs) in zip(parts, extents, counts):
            run_part = lambda p=p, a=ins[i:i + ci], b=outs[o:o + co], c=sc[s:s + cs]: p["body"](*a, *b, *c)
            if n == longest:
                run_part()
            else:
                pl.when(pl.program_id(len(p["grid"]) - 1) < n)(run_part)
            i, o, s = i + ci, o + co, s + cs

    cat = lambda key: [x for p in parts for x in p[key]]
    return _call(body, name=name, grid=parts[0]["grid"][:-1] + (longest,),
                 in_specs=[clamp(sp, n) for p, n in zip(parts, extents) for sp in p["in_specs"]],
                 out_specs=tuple(clamp(sp, n) for p, n in zip(parts, extents) for sp in p["out_specs"]),
                 out_shape=tuple(cat("out_shape")), args=tuple(cat("args")), scratch=cat("scratch"),
                 sem=("arbitrary",) * len(parts[0]["grid"]), riders=riders)


def _rows_tile(n, cap):
    return max(t for t in range(16, min(n, cap) + 1, 16) if n % t == 0)


def scatter_add(gs, ras, name):
    C = gs[0].shape[2]
    trs = [_rows_tile(g.shape[1], 176) for g in gs]
    nts = [g.shape[1] // tr for g, tr in zip(gs, trs)]
    steps = max(nts)
    x, y, c, chips = _place()
    slots = jnp.stack([4 * px + 2 * py + c for px, py in [(x, y)] + chips]).astype(jnp.int32)

    def body(s_ref, *refs):
        ins, outs = refs[:5 * len(gs)], refs[5 * len(gs):]
        for k in range(len(gs)):
            g0, g1, g2, g3, ra_ref = ins[5 * k:5 * k + 5]
            own_ref, sb_ref = outs[2 * k:2 * k + 2]

            def work(g0=g0, g1=g1, g2=g2, g3=g3, ra_ref=ra_ref, own_ref=own_ref, sb_ref=sb_ref):
                own_ref[...] = g0[...] + ra_ref[0].astype(F32)
                for j, gj in enumerate((g1, g2, g3)):
                    sb_ref[j] = (gj[...] + ra_ref[j + 1].astype(F32)).astype(BF16)

            if nts[k] == steps:
                work()
            else:
                pl.when(pl.program_id(0) < nts[k])(work)

    in_specs, out_specs, out_shape, args = [], [], [], [slots]
    for k, (g, ra, tr) in enumerate(zip(gs, ras, trs)):
        tile = lambda i, k=k: jnp.minimum(i, nts[k] - 1)
        in_specs += [pl.BlockSpec((None, tr, C), lambda i, s, j=j, tile=tile: (s[j], tile(i), 0)) for j in range(4)]
        in_specs.append(pl.BlockSpec((4, tr, C), lambda i, s, tile=tile: (0, tile(i), 0)))
        out_specs += [pl.BlockSpec((tr, C), lambda i, s, tile=tile: (tile(i), 0)),
                      pl.BlockSpec((3, tr, C), lambda i, s, tile=tile: (0, tile(i), 0))]
        out_shape += [S((g.shape[1], C), F32), S((3, g.shape[1], C), BF16)]
        args += [g, g, g, g, ra]
    spec = pltpu.PrefetchScalarGridSpec(num_scalar_prefetch=1, grid=(steps,), in_specs=in_specs, out_specs=tuple(out_specs))
    res = pl.pallas_call(body, name=name, grid_spec=spec, out_shape=tuple(out_shape), compiler_params=_params(("arbitrary",)))(*args)
    return [(res[2 * k], res[2 * k + 1]) for k in range(len(gs))]


def rms_cast(h, g, name, riders=()):
    T, D = h.shape
    tm = _tile(T, 512)

    def body(h_ref, g_ref, n_ref):
        n_ref[...] = _rms(h_ref[...], g_ref[...]).astype(BF16)

    row = pl.BlockSpec((tm, D), lambda i: (i, 0))
    g, g_spec = _gain(g)
    return _call(body, name=name, grid=(T // tm,), out_shape=S((T, D), BF16), in_specs=[row, g_spec],
                 out_specs=row, sem=("parallel",), args=(h, g), riders=riders)


def ffn_up_fwd(n, wgu, name, riders=()):
    T, D = n.shape
    F = wgu.shape[1]
    tr, tn = _tile(T, 512), _tile(F, 256)

    def body(n_ref, wg_ref, wu_ref, dgate_ref, dup_ref, a_ref):
        wg, wu = wg_ref[...], wu_ref[...]
        for r in range(T // tr):
            rows = slice(r * tr, (r + 1) * tr)
            x = n_ref[rows, :]
            gate = _nt(x, wg)
            up = _nt(x, wu)
            s = jax.nn.sigmoid(gate)
            silu = gate * s
            dgate_ref[rows, :] = (up * (s * (1.0 + gate * (1.0 - s)))).astype(BF16)
            dup_ref[rows, :] = silu.astype(BF16)
            a_ref[rows, :] = (silu * up).astype(BF16)

    tile = pl.BlockSpec((T, tn), lambda j: (0, j))
    return _call(
        body, name=name, grid=(F // tn,), out_shape=(S((T, F), BF16),) * 3,
        in_specs=[pl.BlockSpec((T, D), lambda j: (0, 0)),
                  pl.BlockSpec((None, tn, D), lambda j: (0, j, 0)), pl.BlockSpec((None, tn, D), lambda j: (1, j, 0))],
        out_specs=(tile, tile, tile), sem=("parallel",), args=(n, wgu, wgu), riders=riders)


def ffn_down_fwd(a, wd, h, g_next, name, riders=()):
    T, F = a.shape
    D = wd.shape[1]
    tm = _tile(T, 256)

    def body(a_ref, w_ref, h_ref, *rest):
        out = h_ref[...] + 0.5 * _nn(a_ref[...], w_ref[...])
        if g_next is None:
            rest[0][...] = out
        else:
            g_ref, o_ref, n_ref = rest
            o_ref[...] = out
            n_ref[...] = _rms(out, g_ref[...]).astype(BF16)

    row = pl.BlockSpec((tm, D), lambda i: (i, 0))
    more = g_next is not None
    g_arg, g_spec = _gain(g_next) if more else (None, None)
    return _call(
        body, name=name, grid=(T // tm,), out_shape=(S((T, D), F32), S((T, D), BF16)) if more else S((T, D), F32),
        in_specs=[pl.BlockSpec((tm, F), lambda i: (i, 0)), pl.BlockSpec((F, D), lambda i: (0, 0)), row] + ([g_spec] if more else []),
        out_specs=(row, row) if more else row,
        sem=("parallel",), args=(a, wd, h) + ((g_arg,) if more else ()), riders=riders)


def mix_in_fwd(h, g, win, name):
    T, D = h.shape
    N = win.shape[0]
    tm = _tile(T, 256)

    def body(h_ref, g_ref, w_ref, n_ref, p_ref):
        n = _rms(h_ref[...], g_ref[...]).astype(BF16)
        n_ref[...] = n
        p_ref[...] = _nt(n, w_ref[...]).astype(BF16)

    g, g_spec = _gain(g)
    return pl.pallas_call(
        body, name=name, grid=(T // tm,), out_shape=(S((T, D), BF16), S((T, N), BF16)),
        in_specs=[pl.BlockSpec((tm, D), lambda i: (i, 0)), g_spec, pl.BlockSpec((N, D), lambda i: (0, 0))],
        out_specs=(pl.BlockSpec((tm, D), lambda i: (i, 0)), pl.BlockSpec((tm, N), lambda i: (i, 0))),
        compiler_params=_params(("parallel",)),
    )(h, g, win)


def _tri_consts():
    r = lax.broadcasted_iota(jnp.int32, (QB, QB), 0)
    c = lax.broadcasted_iota(jnp.int32, (QB, QB), 1)
    ones = jnp.ones((QB, QB), BF16)
    with_sums = lambda tri: jnp.concatenate([tri.astype(BF16), ones], axis=1)
    return with_sums(r > c), with_sums(r <= c), with_sums(r < c)


def _half_masks():
    lane = lax.broadcasted_iota(jnp.int32, (QB, PAIR), 1)
    row = lax.broadcasted_iota(jnp.int32, (QB, PAIR), 0)
    return lane < HEAD_DIM, lane, row


def sb_attn_fwd(p, after, name, riders=()):
    T = p.shape[0]
    nq = T // QB

    def body(q_ref, k_ref, v_ref, m_ref, o_ref, tot_ref, q_sc, acc_ref, z_sc):
        i = pl.program_id(0)
        lo, lane, row = _half_masks()
        causal = lane < row
        heads, pairs = range(SB_HEADS), range(SB_HEADS // 2)
        for hp in pairs:
            q_sc[hp] = (q_ref[:, hp * PAIR:(hp + 1) * PAIR].astype(F32) * SCALE).astype(BF16)
        m2 = m_ref[...]

        def by_head(ref, j, hp):
            t = ref[pl.ds(pl.multiple_of(j * QB, QB), QB), hp * PAIR:(hp + 1) * PAIR]
            return jnp.concatenate([jnp.where(lo, t, 0), jnp.where(lo, 0, t)], axis=0)

        def scores(j):
            return [_nt(q_sc[hp], by_head(k_ref, j, hp)) for hp in pairs]

        def block(j, diag):
            z2 = [z_sc[hp] for hp in pairs]
            ahead = scores(jnp.maximum(j - 1, 0))
            for hp in pairs:
                z_sc[hp] = ahead[hp]
            vs = [by_head(v_ref, j, hp) for hp in pairs]
            spls = [_softplus_logsig(z2[h // 2][:, (h % 2) * QB:(h % 2 + 1) * QB]) for h in heads]
            sp = [jnp.where(causal, spls[h][0], 0.0) if diag else spls[h][0] for h in heads]
            rr = _tri(sp, m2)
            if diag:
                w = [jnp.where(causal, jnp.exp(spls[h][1] - rr[h][:, :QB]), 0.0).astype(BF16) for h in heads]
            else:
                c = [tot_ref[:, h * QB:(h + 1) * QB] for h in heads]
                w = [jnp.exp(spls[h][1] - (c[h] + rr[h][:, :QB])).astype(BF16) for h in heads]
            pv = [_nn(jnp.concatenate([w[2 * hp], w[2 * hp + 1]], axis=1), vs[hp]) for hp in pairs]
            for hp in pairs:
                acc_ref[hp] = pv[hp] if diag else acc_ref[hp] + pv[hp]
            for h in heads:
                tot_ref[:, h * QB:(h + 1) * QB] = rr[h][:, QB:] if diag else c[h] + rr[h][:, QB:]

        first = scores(i)
        for hp in pairs:
            z_sc[hp] = first[hp]
        block(i, True)

        def step(t, carry):
            block(i - 1 - t, False)
            return carry
        lax.fori_loop(0, i, step, 0)
        for hp in pairs:
            o_ref[:, hp * PAIR:(hp + 1) * PAIR] = acc_ref[hp]

    npair = SB_HEADS // 2
    return _call(
        body, name=name, grid=(nq,), out_shape=(S((T, SB_W), F32), S((T, SB_HEADS * QB), F32)),
        in_specs=[pl.BlockSpec((QB, SB_W), lambda i: (i, 0)), pl.BlockSpec((T, SB_W), lambda i: (0, 1)),
                  pl.BlockSpec((T, SB_W), lambda i: (0, 2)), pl.BlockSpec((QB, 2 * QB), lambda i: (0, 0))],
        out_specs=(pl.BlockSpec((QB, SB_W), lambda i: (i, 0)), pl.BlockSpec((QB, SB_HEADS * QB), lambda i: (i, 0))),
        scratch=[pltpu.VMEM((npair, QB, PAIR), BF16), pltpu.VMEM((npair, QB, PAIR), F32), pltpu.VMEM((npair, QB, 2 * QB), F32)],
        sem=("arbitrary",), args=(p, p, p, after), riders=riders,
        marks=((11 * nq) // 16, (14 * nq) // 16))


def sb_attn_bwd(p, do, tot, upto, before, name, riders=()):
    T = p.shape[0]
    nq = T // QB

    def body(q_ref, k_ref, v_ref, do_ref, tot_ref, mp_ref, mg_ref, dq_ref, dk_ref, dv_ref,
             q_sc, d_sc, qd_sc, pg_sc, dq_acc, dk_acc, dv_acc, zd_sc):
        i = pl.program_id(0)
        lo, lane, row = _half_masks()
        causal = lane < row
        heads, pairs = range(SB_HEADS), range(SB_HEADS // 2)

        def by_head(t):
            return jnp.concatenate([jnp.where(lo, t, 0), jnp.where(lo, 0, t)], axis=0)

        for hp in pairs:
            q2 = (q_ref[:, hp * PAIR:(hp + 1) * PAIR].astype(F32) * SCALE).astype(BF16)
            d2 = do_ref[:, hp * PAIR:(hp + 1) * PAIR].astype(BF16)
            q_sc[hp] = q2
            d_sc[hp] = d2
            qd_sc[hp] = by_head(q2)
            qd_sc[SB_HEADS // 2 + hp] = by_head(d2)
        mp, mg = mp_ref[...], mg_ref[...]

        @pl.when(i == 0)
        def _():
            dk_acc[...] = jnp.zeros_like(dk_acc)
            dv_acc[...] = jnp.zeros_like(dv_acc)
        pg_sc[...] = jnp.zeros_like(pg_sc)
        dq_acc[...] = jnp.zeros_like(dq_acc)

        def rows(ref, j, hp):
            return ref[pl.ds(pl.multiple_of(j * QB, QB), QB), hp * PAIR:(hp + 1) * PAIR]

        def products(j):
            return ([_nt(q_sc[hp], by_head(rows(k_ref, j, hp))) for hp in pairs]
                    + [_nt(d_sc[hp], by_head(rows(v_ref, j, hp))) for hp in pairs])

        def block(j, diag):
            r0 = pl.multiple_of(j * QB, QB)
            half = lambda t, h: t[:, (h % 2) * QB:(h % 2 + 1) * QB]
            z = [half(zd_sc[h // 2], h) for h in heads]
            dw = [half(zd_sc[SB_HEADS // 2 + h // 2], h) for h in heads]
            if not diag:
                ahead = products(j + 1)
                for hp in range(SB_HEADS):
                    zd_sc[hp] = ahead[hp]
            ks = [by_head(rows(k_ref, j, hp)) for hp in pairs]
            spls = [_softplus_logsig(z[h]) for h in heads]
            sp = [jnp.where(causal, spls[h][0], 0.0) if diag else spls[h][0] for h in heads]
            rr = _tri(sp, mp)
            pc = [pg_sc[2 * h] for h in heads]
            w = [jnp.exp(spls[h][1] - (tot_ref[:, h * QB:(h + 1) * QB] - (pc[h] + rr[h][:, :QB]))) for h in heads]
            if diag:
                w = [jnp.where(causal, w[h], 0.0) for h in heads]
            gg = [dw[h] * w[h] for h in heads]
            rg = _tri(gg, mg)
            gc = [pg_sc[2 * h + 1] for h in heads]
            dz = [gg[h] - (gg[h] + gc[h] + rg[h][:, :QB]) * jnp.exp(spls[h][1]) for h in heads]
            if diag:
                dz = [jnp.where(causal, dz[h], 0.0) for h in heads]
            dzb = [dz[h].astype(BF16) for h in heads]
            wb = [w[h].astype(BF16) for h in heads]
            both = lambda t, hp, axis: jnp.concatenate([t[2 * hp], t[2 * hp + 1]], axis=axis)
            dq = [_nn(both(dzb, hp, 1), ks[hp]) for hp in pairs]
            dk = [_tn(both(dzb, hp, 0), qd_sc[hp]) for hp in pairs]
            dv = [_tn(both(wb, hp, 0), qd_sc[SB_HEADS // 2 + hp]) for hp in pairs]
            for h in heads:
                if not diag:
                    pg_sc[2 * h] = pc[h] + rr[h][:, QB:]
                    pg_sc[2 * h + 1] = gc[h] + rg[h][:, QB:]
            for hp in pairs:
                dq_acc[hp] += dq[hp]
                dk_acc[pl.ds(r0, QB), hp * PAIR:(hp + 1) * PAIR] += dk[hp]
                dv_acc[pl.ds(r0, QB), hp * PAIR:(hp + 1) * PAIR] += dv[hp]

        first = products(0)
        for hp in range(SB_HEADS):
            zd_sc[hp] = first[hp]

        def step(t, carry):
            block(t, False)
            return carry
        lax.fori_loop(0, i, step, 0)
        block(i, True)
        for hp in pairs:
            dq_ref[:, hp * PAIR:(hp + 1) * PAIR] = (dq_acc[hp] * SCALE).astype(BF16)

        @pl.when(i == nq - 1)
        def _():
            dk_ref[...] = dk_acc[...].astype(BF16)
            dv_ref[...] = dv_acc[...].astype(BF16)

    qtile = pl.BlockSpec((QB, SB_W), lambda i: (i, 0))
    whole = pl.BlockSpec((T, SB_W), lambda i: (0, 0))
    const = pl.BlockSpec((QB, 2 * QB), lambda i: (0, 0))
    return _call(
        body, name=name, grid=(nq,), out_shape=(S((T, SB_W), BF16),) * 3,
        in_specs=[qtile, pl.BlockSpec((T, SB_W), lambda i: (0, 1)), pl.BlockSpec((T, SB_W), lambda i: (0, 2)), qtile,
                  pl.BlockSpec((QB, SB_HEADS * QB), lambda i: (i, 0)), const, const],
        out_specs=(qtile, whole, whole),
        scratch=[pltpu.VMEM((SB_HEADS // 2, QB, PAIR), BF16), pltpu.VMEM((SB_HEADS // 2, QB, PAIR), BF16),
                 pltpu.VMEM((SB_HEADS, 2 * QB, PAIR), BF16),
                 pltpu.VMEM((2 * SB_HEADS, QB, QB), F32), pltpu.VMEM((SB_HEADS // 2, QB, PAIR), F32),
                 pltpu.VMEM((T, SB_W), F32), pltpu.VMEM((T, SB_W), F32), pltpu.VMEM((SB_HEADS, QB, 2 * QB), F32)],
        sem=("arbitrary",), args=(p, p, p, do, tot, upto, before), riders=riders)


def _t5_buckets():
    a = lax.broadcasted_iota(jnp.int32, (QB, QB), 0)
    c = lax.broadcasted_iota(jnp.int32, (QB, QB), 1)

    def bucket(dist):
        dist = jnp.maximum(dist, 0)
        max_exact = N_BUCKETS // 2
        d = jnp.maximum(dist, 1).astype(F32)
        large = max_exact + (jnp.log(d / max_exact) / math.log(MAX_DISTANCE / max_exact)
                             * (N_BUCKETS - max_exact)).astype(jnp.int32)
        large = jnp.minimum(large, N_BUCKETS - 1)
        return jnp.where(dist < max_exact, dist, large)

    return bucket(QB + a - c), bucket(a - c)


def _swa_common(i, kp_ref, kc_ref, vp_ref, vc_ref, bp_ref, bc_ref, rb_ref, bias_ref):
    lo, lane, row = _half_masks()

    @pl.when(i == 0)
    def _():
        for blk, b_ref in enumerate((bp_ref, bc_ref)):
            bk = b_ref[...]
            for h in range(8):
                acc = jnp.zeros((QB, QB), F32)
                for b in range(N_BUCKETS):
                    acc = jnp.where(bk == b, rb_ref[b, h], acc)
                bias_ref[h, blk] = acc

    band = [(lane > row) & (i > 0), lane <= row]

    def stacks(ref):
        t = ref[...].astype(F32)
        sw = pltpu.roll(t, HEAD_DIM, 1)
        return [jnp.concatenate([jnp.where(lo, t, 0.0), jnp.where(lo, 0.0, sw)], axis=0).astype(BF16),
                jnp.concatenate([jnp.where(lo, sw, 0.0), jnp.where(lo, 0.0, t)], axis=0).astype(BF16)]

    ks = [stacks(kp_ref), stacks(kc_ref)]
    vs = [stacks(vp_ref), stacks(vc_ref)]
    return lo, band, ks, vs


def _lane_half(t, h):
    return t[:, (h % 2) * QB:(h % 2 + 1) * QB]


def swa_fwd(p, sinks, rel_bias, bprev, bcur, name, riders=()):
    T = p.shape[0]
    nq = T // QB
    kcol, vcol = (3 * SB_W + SWA_W) // KV_W, (3 * SB_W + SWA_W) // KV_W + 1
    sinks, srow = sinks if isinstance(sinks, tuple) else (sinks, 0)

    def body(q_ref, kp_ref, kc_ref, vp_ref, vc_ref, bp_ref, bc_ref, sink_ref, rb_ref, o_ref, lse_ref, bias_ref):
        i = pl.program_id(0)
        lo, band, ks, vs = _swa_common(i, kp_ref, kc_ref, vp_ref, vc_ref, bp_ref, bc_ref, rb_ref, bias_ref)
        heads, pairs, blocks = range(8), range(4), range(2)
        rowmax = lambda t: jnp.max(t, axis=1, keepdims=True)
        rowsum = lambda t: jnp.sum(t, axis=1, keepdims=True)
        q2 = [q_ref[:, g * PAIR:(g + 1) * PAIR] for g in pairs]
        s2 = [[_nt(q2[g], ks[b][g // 2]) for b in blocks] for g in pairs]
        sc = [[jnp.where(band[b], _lane_half(s2[h // 2][b], h) * SCALE + bias_ref[h, b], NEG_INF) for b in blocks] for h in heads]
        sink = [sink_ref[srow, h] for h in heads]
        m = [jnp.maximum(jnp.maximum(rowmax(sc[h][0]), rowmax(sc[h][1])), sink[h]) for h in heads]
        e = [[jnp.exp(sc[h][b] - m[h]) for b in blocks] for h in heads]
        den = [rowsum(e[h][0]) + rowsum(e[h][1]) + jnp.exp(sink[h] - m[h]) for h in heads]
        pb = [[(e[h][b] / den[h]).astype(BF16) for b in blocks] for h in heads]
        for g in pairs:
            both = lambda b: jnp.concatenate([pb[2 * g][b], pb[2 * g + 1][b]], axis=1)
            o_ref[:, g * PAIR:(g + 1) * PAIR] = _nn(both(0), vs[0][g // 2]) + _nn(both(1), vs[1][g // 2])
        for h in heads:
            lse_ref[:, h * QB:(h + 1) * QB] = jnp.broadcast_to(m[h] + jnp.log(den[h]), (QB, QB))

    kv = lambda col, prev: pl.BlockSpec((QB, KV_W), (lambda i: (jnp.maximum(i - 1, 0), col)) if prev else (lambda i: (i, col)))
    full = pl.BlockSpec((QB, QB), lambda i: (0, 0))
    smem = pl.BlockSpec(memory_space=pltpu.SMEM)
    return _call(
        body, name=name, grid=(nq,), out_shape=(S((T, SWA_W), F32), S((T, 8 * QB), F32)),
        in_specs=[pl.BlockSpec((QB, SWA_W), lambda i: (i, 3)), kv(kcol, True), kv(kcol, False), kv(vcol, True), kv(vcol, False),
                  full, full, smem, smem],
        out_specs=(pl.BlockSpec((QB, SWA_W), lambda i: (i, 0)), pl.BlockSpec((QB, 8 * QB), lambda i: (i, 0))),
        scratch=[pltpu.VMEM((8, 2, QB, QB), F32)],
        sem=("arbitrary",), args=(p, p, p, p, p, bprev, bcur, sinks, rel_bias), riders=riders)


def swa_bwd(p, do, lse, sinks, rel_bias, bprev, bcur, name, riders=()):
    T = p.shape[0]
    nq = T // QB
    kcol, vcol = (3 * SB_W + SWA_W) // KV_W, (3 * SB_W + SWA_W) // KV_W + 1
    sinks, srow = sinks if isinstance(sinks, tuple) else (sinks, 0)

    def body(q_ref, kp_ref, kc_ref, vp_ref, vc_ref, do_ref, lse_ref, bp_ref, bc_ref, sink_ref, rb_ref,
             dq_ref, dk_ref, dv_ref, dsink_ref, dsc_ref, bias_ref, dk_acc, dv_acc):
        i = pl.program_id(0)
        lo, band, ks, vs = _swa_common(i, kp_ref, kc_ref, vp_ref, vc_ref, bp_ref, bc_ref, rb_ref, bias_ref)

        @pl.when(i == 0)
        def _():
            dk_acc[...] = jnp.zeros_like(dk_acc)
            dv_acc[...] = jnp.zeros_like(dv_acc)
            dsc_ref[...] = jnp.zeros_like(dsc_ref)
            dsink_ref[...] = jnp.zeros_like(dsink_ref)

        heads, pairs, blocks = range(8), range(4), range(2)
        rowsum = lambda t: jnp.sum(t, axis=1, keepdims=True)
        by_head = lambda t: jnp.concatenate([jnp.where(lo, t, 0), jnp.where(lo, 0, t)], axis=0)
        q2 = [q_ref[:, g * PAIR:(g + 1) * PAIR] for g in pairs]
        d2 = [do_ref[:, g * PAIR:(g + 1) * PAIR].astype(BF16) for g in pairs]
        qs = [by_head(q2[g]) for g in pairs]
        dos = [by_head(d2[g]) for g in pairs]
        s2 = [[_nt(q2[g], ks[b][g // 2]) for b in blocks] for g in pairs]
        dp2 = [[_nt(d2[g], vs[b][g // 2]) for b in blocks] for g in pairs]
        lse_h = [lse_ref[:, h * QB:(h + 1) * QB] for h in heads]
        sink = [sink_ref[srow, h] for h in heads]
        pr = [[jnp.exp(jnp.where(band[b], _lane_half(s2[h // 2][b], h) * SCALE + bias_ref[h, b], NEG_INF) - lse_h[h])
               for b in blocks] for h in heads]
        dp = [[_lane_half(dp2[h // 2][b], h) for b in blocks] for h in heads]
        delta = [rowsum(pr[h][0] * dp[h][0]) + rowsum(pr[h][1] * dp[h][1]) for h in heads]
        lane1 = lax.broadcasted_iota(jnp.int32, (1, QB), 1)
        dsink = jnp.zeros((1, QB), F32)
        for h in heads:
            dsink = dsink + jnp.where(lane1 == h, -jnp.sum(jnp.exp(sink[h] - lse_h[h][:, :1]) * delta[h]), 0.0)
        dsink_ref[...] += dsink
        dsc = [[pr[h][b] * (dp[h][b] - delta[h]) for b in blocks] for h in heads]
        for h in heads:
            for b in blocks:
                dsc_ref[h, b] += dsc[h][b]
        dzb = [[(dsc[h][b] * SCALE).astype(BF16) for b in blocks] for h in heads]
        prb = [[pr[h][b].astype(BF16) for b in blocks] for h in heads]
        pair_of = lambda t, g, b, axis: jnp.concatenate([t[2 * g][b], t[2 * g + 1][b]], axis=axis)
        for g in pairs:
            dq = _nn(pair_of(dzb, g, 0, 1), ks[0][g // 2]) + _nn(pair_of(dzb, g, 1, 1), ks[1][g // 2])
            dq_ref[:, g * PAIR:(g + 1) * PAIR] = dq.astype(BF16)

        def key_grad(t, other, b):
            per_kv = [_tn(pair_of(t, 2 * kh, b, 0), other[2 * kh]) + _tn(pair_of(t, 2 * kh + 1, b, 0), other[2 * kh + 1]) for kh in range(2)]
            both = [s + pltpu.roll(s, HEAD_DIM, 1) for s in per_kv]
            return jnp.where(lo, both[0], both[1])

        rp = pl.multiple_of(jnp.maximum(i - 1, 0) * QB, QB)
        rc = pl.multiple_of(i * QB, QB)
        dk_acc[pl.ds(rp, QB), :] += key_grad(dzb, qs, 0)
        dv_acc[pl.ds(rp, QB), :] += key_grad(prb, dos, 0)
        dk_acc[pl.ds(rc, QB), :] += key_grad(dzb, qs, 1)
        dv_acc[pl.ds(rc, QB), :] += key_grad(prb, dos, 1)

        @pl.when(i == nq - 1)
        def _():
            dk_ref[...] = dk_acc[...].astype(BF16)
            dv_ref[...] = dv_acc[...].astype(BF16)

    kv = lambda col, prev: pl.BlockSpec((QB, KV_W), (lambda i: (jnp.maximum(i - 1, 0), col)) if prev else (lambda i: (i, col)))
    full = pl.BlockSpec((QB, QB), lambda i: (0, 0))
    smem = pl.BlockSpec(memory_space=pltpu.SMEM)
    whole = lambda shape: pl.BlockSpec(shape, lambda i: (0,) * len(shape))
    return _call(
        body, name=name, grid=(nq,),
        out_shape=(S((T, SWA_W), BF16), S((T, KV_W), BF16), S((T, KV_W), BF16), S((1, QB), F32), S((8, 2, QB, QB), F32)),
        in_specs=[pl.BlockSpec((QB, SWA_W), lambda i: (i, 3)), kv(kcol, True), kv(kcol, False), kv(vcol, True), kv(vcol, False),
                  pl.BlockSpec((QB, SWA_W), lambda i: (i, 0)), pl.BlockSpec((QB, 8 * QB), lambda i: (i, 0)),
                  full, full, smem, smem],
        out_specs=(pl.BlockSpec((QB, SWA_W), lambda i: (i, 0)), whole((T, KV_W)), whole((T, KV_W)), whole((1, QB)),
                   whole((8, 2, QB, QB))),
        scratch=[pltpu.VMEM((8, 2, QB, QB), F32), pltpu.VMEM((T, KV_W), F32), pltpu.VMEM((T, KV_W), F32)],
        sem=("arbitrary",), args=(p, p, p, p, p, do, lse, bprev, bcur, sinks, rel_bias), riders=riders)


def mix_out_fwd(o_sb, o_sw, g_sb, g_sw, wout, h, g_next, name, riders=()):
    T, D = h.shape
    M = SB_W + SWA_W
    tm = _tile(T, 256)

    def body(a_ref, b_ref, ga_ref, gb_ref, w_ref, h_ref, gn_ref, mx_ref, o_ref, n_ref):
        mx_ref[:, :SB_W] = _rms(a_ref[...], ga_ref[...]).astype(BF16)
        mx_ref[:, SB_W:] = _rms(b_ref[...], gb_ref[...]).astype(BF16)
        out = h_ref[...] + _nn(mx_ref[...], w_ref[...])
        o_ref[...] = out
        n_ref[...] = _rms(out, gn_ref[...]).astype(BF16)

    row = lambda n: pl.BlockSpec((tm, n), lambda i: (i, 0))
    (g_sb, sb_spec), (g_sw, sw_spec), (g_next, next_spec) = _gain(g_sb), _gain(g_sw), _gain(g_next)
    return _call(
        body, name=name, grid=(T // tm,), out_shape=(S((T, M), BF16), S((T, D), F32), S((T, D), BF16)),
        in_specs=[row(SB_W), row(SWA_W), sb_spec, sw_spec, pl.BlockSpec((M, D), lambda i: (0, 0)), row(D), next_spec],
        out_specs=(row(M), row(D), row(D)),
        sem=("parallel",), args=(o_sb, o_sw, g_sb, g_sw, wout, h, g_next), riders=riders)


def loss_head(h, g, target, name):
    T, D = h.shape
    tm = _tile(T, 256)

    def body(h_ref, g_ref, t_ref, loss_ref, dh_ref, dhb_ref, dg_ref):
        @pl.when(pl.program_id(0) == 0)
        def _():
            loss_ref[...] = jnp.zeros_like(loss_ref)
            dg_ref[...] = jnp.zeros_like(dg_ref)
        x = h_ref[...]
        err = _rms(x, g_ref[...]) - t_ref[...]
        loss_ref[...] += jnp.full((1, QB), 0.5 * jnp.sum(jnp.mean(err * err, axis=-1)), F32)
        dx, dg = _rms_bwd(err / D, x, g_ref[...])
        dh_ref[...] = dx
        dhb_ref[...] = dx.astype(BF16)
        dg_ref[...] += dg

    row = pl.BlockSpec((tm, D), lambda i: (i, 0))
    vec = pl.BlockSpec((1, D), lambda i: (0, 0))
    return pl.pallas_call(
        body, name=name, grid=(T // tm,), out_shape=(S((1, QB), F32), S((T, D), F32), S((T, D), BF16), S((1, D), F32)),
        in_specs=[row, vec, row], out_specs=(pl.BlockSpec((1, QB), lambda i: (0, 0)), row, row, vec),
        compiler_params=_params(("arbitrary",)),
    )(h, g, target)


def ffn_down_bwd(dhb, wd, gate, up, a, n, name, riders=()):
    T, D = dhb.shape
    F = wd.shape[0]
    tr, tn = _tile(T, 512), _tile(F, 256)

    def body(d_ref, n_ref, w_ref, g_ref, u_ref, a_ref, o_ref, dwd_ref, dwdb_ref, dwgu_ref, dwgub_ref):
        w = w_ref[...]
        for r in range(T // tr):
            rows = slice(r * tr, (r + 1) * tr)
            da = 0.5 * _nt(d_ref[rows, :], w)
            o_ref[0, rows, :] = (da * g_ref[rows, :].astype(F32)).astype(BF16)
            o_ref[1, rows, :] = (da * u_ref[rows, :].astype(F32)).astype(BF16)
        dwd = 0.5 * _tn(a_ref[...], d_ref[...])
        dwd_ref[...] = dwd
        dwdb_ref[...] = dwd.astype(BF16)
        for s in range(2):
            dwgu = _tn(o_ref[s], n_ref[...])
            dwgu_ref[s] = dwgu
            dwgub_ref[s] = dwgu.astype(BF16)

    tile = pl.BlockSpec((T, tn), lambda j: (0, j))
    whole = pl.BlockSpec((T, D), lambda j: (0, 0))
    rows1, rows2 = pl.BlockSpec((tn, D), lambda j: (j, 0)), pl.BlockSpec((2, tn, D), lambda j: (0, j, 0))
    return _call(
        body, name=name, grid=(F // tn,),
        out_shape=(S((2, T, F), BF16), S((F, D), F32), S((F, D), BF16), S((2, F, D), F32), S((2, F, D), BF16)),
        in_specs=[whole, whole, rows1, tile, tile, tile],
        out_specs=(pl.BlockSpec((2, T, tn), lambda j: (0, 0, j)), rows1, rows1, rows2, rows2),
        sem=("parallel",), args=(dhb, n, wd, gate, up, a), riders=riders)


def tn_matmul(xs, y, alpha, name, riders=()):
    B, T, N = xs.shape
    D = y.shape[1]
    tn = _tile(N, 256)

    def body(x_ref, y_ref, o_ref, ob_ref):
        o = alpha * _tn(x_ref[...], y_ref[...])
        o_ref[...] = o
        ob_ref[...] = o.astype(BF16)

    tile = pl.BlockSpec((None, tn, D), lambda s, j: (s, j, 0))
    return _call(
        body, name=name, grid=(B, N // tn), out_shape=(S((B, N, D), F32), S((B, N, D), BF16)),
        in_specs=[pl.BlockSpec((None, T, tn), lambda s, j: (s, 0, j)), pl.BlockSpec((T, D), lambda s, j: (0, 0))],
        out_specs=(tile, tile), sem=("parallel", "parallel"), args=(xs, y), riders=riders)


def nn_rms_bwd(xs, ws, h_in, g, dh, name, riders=()):
    B, T, K = xs.shape
    D = ws.shape[2]
    tm = _tile(T, 256)

    def body(x_ref, w_ref, h_ref, g_ref, d_ref, o_ref, ob_ref, dg_ref):
        @pl.when(pl.program_id(0) == 0)
        def _():
            dg_ref[...] = jnp.zeros_like(dg_ref)
        dn = _nn(x_ref[0], w_ref[0])
        for s in range(1, B):
            dn = dn + _nn(x_ref[s], w_ref[s])
        dx, dg = _rms_bwd(dn, h_ref[...], g_ref[...])
        out = d_ref[...] + dx
        o_ref[...] = out
        ob_ref[...] = out.astype(BF16)
        dg_ref[...] += dg

    row = pl.BlockSpec((tm, D), lambda i: (i, 0))
    vec = pl.BlockSpec((1, D), lambda i: (0, 0))
    g, g_spec = _gain(g)
    return _call(
        body, name=name, grid=(T // tm,), out_shape=(S((T, D), F32), S((T, D), BF16), S((1, D), F32)),
        in_specs=[pl.BlockSpec((B, tm, K), lambda i: (0, i, 0)), pl.BlockSpec((B, K, D), lambda i: (0, 0, 0)), row, g_spec, row],
        out_specs=(row, row, vec),
        sem=("arbitrary",), args=(xs, ws, h_in, g, dh), riders=riders)


def mix_out_bwd(dhb, wout, mixed, o_sb, o_sw, g_sb, g_sw, name):
    T, D = dhb.shape
    M = SB_W + SWA_W
    tm = _tile(T, 256)
    steps = T // tm

    def body(d_ref, w_ref, mx_ref, a_ref, b_ref, ga_ref, gb_ref, da_ref, db_ref, dga_ref, dgb_ref, dw_ref, dwb_ref):
        i = pl.program_id(0)

        @pl.when(i == 0)
        def _():
            dga_ref[...] = jnp.zeros_like(dga_ref)
            dgb_ref[...] = jnp.zeros_like(dgb_ref)
            dw_ref[...] = jnp.zeros_like(dw_ref)
        dm = _nt(d_ref[...], w_ref[...])
        dxa, dga = _rms_bwd(dm[:, :SB_W], a_ref[...], ga_ref[...])
        dxb, dgb = _rms_bwd(dm[:, SB_W:], b_ref[...], gb_ref[...])
        da_ref[...] = dxa
        db_ref[...] = dxb
        dga_ref[...] += dga
        dgb_ref[...] += dgb
        dw_ref[...] += _tn(mx_ref[...], d_ref[...])

        @pl.when(i == steps - 1)
        def _():
            dwb_ref[...] = dw_ref[...].astype(BF16)

    row = lambda n: pl.BlockSpec((tm, n), lambda i: (i, 0))
    vec = lambda n: pl.BlockSpec((1, n), lambda i: (0, 0))
    whole = pl.BlockSpec((M, D), lambda i: (0, 0))
    (g_sb, sb_spec), (g_sw, sw_spec) = _gain(g_sb), _gain(g_sw)
    return pl.pallas_call(
        body, name=name, grid=(steps,),
        out_shape=(S((T, SB_W), F32), S((T, SWA_W), F32), S((1, SB_W), F32), S((1, SWA_W), F32), S((M, D), F32), S((M, D), BF16)),
        in_specs=[row(D), whole, row(M), row(SB_W), row(SWA_W), sb_spec, sw_spec],
        out_specs=(row(SB_W), row(SWA_W), vec(SB_W), vec(SWA_W), whole, whole),
        compiler_params=_params(("arbitrary",)),
    )(dhb, wout, mixed, o_sb, o_sw, g_sb, g_sw)


def rel_bias_grad(dscs, bprev, bcur, name):
    n = len(dscs)

    def body(*refs):
        bp_ref, bc_ref, o_ref = refs[n], refs[n + 1], refs[n + 2]
        bks = [bp_ref[...], bc_ref[...]]
        row = lax.broadcasted_iota(jnp.int32, (N_BUCKETS, QB), 0)
        lane = lax.broadcasted_iota(jnp.int32, (N_BUCKETS, QB), 1)
        out = jnp.zeros((N_BUCKETS, QB), F32)
        for h in range(8):
            tot = [sum(refs[l][h, b] for l in range(n)) for b in range(2)]
            for b in range(N_BUCKETS):
                val = jnp.sum(jnp.where(bks[0] == b, tot[0], 0.0)) + jnp.sum(jnp.where(bks[1] == b, tot[1], 0.0))
                out = jnp.where((row == b) & (lane == h), val, out)
        o_ref[...] = out

    return pl.pallas_call(body, name=name, out_shape=S((N_BUCKETS, QB), F32), compiler_params=_params())(*dscs, bprev, bcur)


def _adamw(w, g, m, v):
    m = ADAM_B1 * m + (1.0 - ADAM_B1) * g
    v = ADAM_B2 * v + (1.0 - ADAM_B2) * (g * g)
    m_hat = m / (1.0 - ADAM_B1 ** ADAM_STEP)
    v_hat = v / (1.0 - ADAM_B2 ** ADAM_STEP)
    delta = -ADAM_LR * (m_hat / (jnp.sqrt(v_hat) + ADAM_EPS) + ADAM_WD * w)
    return delta, m, v


def adamw_scattered(w, m, v, owns, others, name, riders=(), rows=176):
    L, R, C = w.shape
    tr = _rows_tile(R, rows)

    def body(w_ref, m_ref, v_ref, *rest):
        own_refs, other_refs = rest[:L], rest[L:2 * L]
        g_ref, d_ref, mo_ref, vo_ref = rest[2 * L:]
        layer = pl.program_id(0)

        def grad(k):
            o = other_refs[k]
            return own_refs[k][...] + o[0].astype(F32) + o[1].astype(F32) + o[2].astype(F32)

        g = grad(0)
        for k in range(1, L):
            g = jnp.where(layer == k, grad(k), g)
        d, mn, vn = _adamw(w_ref[...], g, m_ref[...], v_ref[...])
        g_ref[...] = g
        d_ref[...] = d
        mo_ref[...] = mn
        vo_ref[...] = vn

    tile = pl.BlockSpec((None, tr, C), lambda l, i: (l, i, 0))
    return _call(
        body, name=name, grid=(L, R // tr), out_shape=(S((L, R, C), F32),) * 4,
        in_specs=[tile] * 3 + [pl.BlockSpec((tr, C), lambda l, i: (i, 0))] * L + [pl.BlockSpec((3, tr, C), lambda l, i: (0, i, 0))] * L,
        out_specs=(tile,) * 4, sem=("parallel", "parallel"), args=(w, m, v, *owns, *others), riders=riders)


def adamw_small(w, gs, m, v, name):
    R, C = w.shape

    def body(w_ref, g_ref, m_ref, v_ref, go_ref, d_ref, mo_ref, vo_ref):
        g = g_ref[0]
        for k in range(1, N_DEV):
            g = g + g_ref[k]
        d, mn, vn = _adamw(w_ref[...], g, m_ref[...], v_ref[...])
        go_ref[...] = g
        d_ref[...] = d
        mo_ref[...] = mn
        vo_ref[...] = vn

    return pl.pallas_call(body, name=name, out_shape=(S((R, C), F32),) * 4, compiler_params=_params())(w, gs, m, v)


def kernel(x, norm_ffn1, w_ffn1_gu, w_ffn1_down, norm_mix, w_in, sinks, norm_out_sb, norm_out_swa, w_out, norm_ffn2, w_ffn2_gu, w_ffn2_down, rel_bias, norm_final, loss_target, m_norm_ffn1, m_w_ffn1_gu, m_w_ffn1_down, m_norm_mix, m_w_in, m_sinks, m_norm_out_sb, m_norm_out_swa, m_w_out, m_norm_ffn2, m_w_ffn2_gu, m_w_ffn2_down, m_rel_bias, m_norm_final, v_norm_ffn1, v_w_ffn1_gu, v_w_ffn1_down, v_norm_mix, v_w_in, v_sinks, v_norm_out_sb, v_norm_out_swa, v_w_out, v_norm_ffn2, v_w_ffn2_gu, v_w_ffn2_down, v_rel_bias, v_norm_final):
    L = norm_ffn1.shape[0]
    T, D = x.shape[1], x.shape[2]
    F = w_ffn1_down.shape[1] * N_DEV
    h = x.reshape(T, D)
    target = loss_target.reshape(T, D)
    after, upto, before = _tri_consts()
    bprev, bcur = _t5_buckets()

    local = {}
    for l in range(L):
        local[f"gu1_{l}"] = w_ffn1_gu[l].T.astype(BF16)
        local[f"d1_{l}"] = w_ffn1_down[l].astype(BF16)
        local[f"in_{l}"] = w_in[l].T.astype(BF16)
        local[f"out_{l}"] = w_out[l].astype(BF16)
        local[f"gu2_{l}"] = w_ffn2_gu[l].T.astype(BF16)
        local[f"d2_{l}"] = w_ffn2_down[l].astype(BF16)
    full, partial = {}, {}
    grads, chip_sum, recv_b = {}, {}, {}

    def run(fn, *args, ag=(), rs1=(), rs2=()):
        halves = lambda names: [n if isinstance(n, tuple) else (n, None) for n in names]
        ag, rs2 = [(n, k) for n, k in halves(ag) if n in local], halves(rs2)
        rows = lambda k, total: None if k is None else (k * (total // 2), total // 2)

        def second(n, k):
            sb = chip_sum[n][1]
            return scatter_second(sb, rows(k, sb.shape[1]), recv_b.get(n))

        riders = ([gather(local[n], rows(k, local[n].shape[0]), partial.get(n)) for n, k in ag]
                  + [scatter_first(grads[n][1]) for n in rs1] + [second(n, k) for n, k in rs2])
        if not riders:
            return fn(*args)
        outs, per = fn(*args, riders=riders)
        per = [p[0] for p in per]
        for n, k in ag:
            buf = per.pop(0)
            if k == 0:
                partial[n] = buf
            else:
                full[n] = buf.reshape(N_DEV * buf.shape[1], D)
        if rs1:
            sums = scatter_add([grads[n][0] for n in rs1], [per.pop(0) for n in rs1], "rs_add_" + "_".join(rs1))
            chip_sum.update(zip(rs1, sums))
        for n, _ in rs2:
            recv_b[n] = per.pop(0)
        return outs

    def attn_fwd(p, sink, name, riders=()):
        return side_by_side(sb_attn_fwd(p, after, name, riders=PARTS), swa_fwd(p, sink, rel_bias, bprev, bcur, name, riders=PARTS),
                            name, riders)

    def attn_bwd(p, do_sb, tot, do_sw, lse, sink, name, riders=()):
        return side_by_side(sb_attn_bwd(p, do_sb, tot, upto, before, name, riders=PARTS),
                            swa_bwd(p, do_sw, lse, sink, rel_bias, bprev, bcur, name, riders=PARTS), name, riders)

    gu = lambda n: full[n].reshape(2, F, D)
    slots = lambda pair: tuple(t.reshape(N_DEV, -1, D) for t in pair)
    vec = lambda a: a.reshape(1, -1)

    PW = max(D, SB_W + SWA_W)
    n_rows = 4 * L + 2
    n_rows += (-n_rows) % 8

    def pack(ffn1, mix, ffn2, final, osb, osw, snk, rel, extra):
        pieces = []

        def row(*parts):
            flat = [a.reshape(-1) for a in parts]
            pieces.extend(flat)
            used = sum(a.size for a in flat)
            if used < PW:
                pieces.append(jnp.zeros((PW - used,), F32))

        for group in (ffn1, mix, ffn2):
            for l in range(L):
                row(group[l])
        row(final)
        for l in range(L):
            row(osb[l], osw[l])
        row(*[snk[l].reshape(-1)[:8] for l in range(L)], rel, extra)
        pieces.append(jnp.zeros(((n_rows - 4 * L - 2) * PW,), F32))
        return jnp.concatenate(pieces).reshape(n_rows, PW)

    def unpack(arr):
        ffn1, mix, ffn2 = arr[0:L, :D], arr[L:2 * L, :D], arr[2 * L:3 * L, :D]
        final = arr[3 * L, :D]
        ob = arr[3 * L + 1:4 * L + 1]
        tail = arr[4 * L + 1]
        return (ffn1, mix, tail[:8 * L].reshape(L, 8), ob[:, :SB_W], ob[:, SB_W:SB_W + SWA_W], ffn2,
                tail[8 * L:8 * L + N_BUCKETS * 8].reshape(N_BUCKETS, 8), final)

    zero = jnp.zeros((1,), F32)
    w_small = pack(norm_ffn1, norm_mix, norm_ffn2, norm_final, norm_out_sb, norm_out_swa, sinks, rel_bias, zero)
    g_ffn1, g_mix, g_ffn2, g_osb, g_osw = [a.reshape(L, 1, -1) for a in (norm_ffn1, norm_mix, norm_ffn2, norm_out_sb, norm_out_swa)]

    saved = []
    n_next = run(rms_cast, h, (g_ffn1, 0), "rms_first", ag=("gu1_0",))
    for l in range(L):
        nx = l + 1
        s = {"h0": h, "n1": n_next}
        s["gate1"], s["up1"], s["a1"] = run(ffn_up_fwd, s["n1"], gu(f"gu1_{l}"), f"ffn1_up{l}",
                                            ag=(f"d1_{l}", ("in_0", 0) if l == 0 else (f"in_{l}", 1)))
        h = run(ffn_down_fwd, s["a1"], full[f"d1_{l}"], h, None, f"ffn1_down{l}", ag=(("in_0", 1), "out_0") if l == 0 else ())
        s["h1"] = h
        s["n2"], s["p"] = mix_in_fwd(h, (g_mix, l), full[f"in_{l}"], f"mix_in{l}")
        s["o_sb"], s["tot"], s["o_sw"], s["lse"] = run(attn_fwd, s["p"], (sinks, l), f"attn_fwd{l}",
                                                       ag=((f"out_{l}",) if l else ()) + (f"gu2_{l}", f"d2_{l}", (f"gu1_{nx}", 0)))
        s["mixed"], h, s["n3"] = run(mix_out_fwd, s["o_sb"], s["o_sw"], (g_osb, l), (g_osw, l),
                                     full[f"out_{l}"], h, (g_ffn2, l), f"mix_out{l}")
        s["h2"] = h
        s["gate2"], s["up2"], s["a2"] = run(ffn_up_fwd, s["n3"], gu(f"gu2_{l}"), f"ffn2_up{l}",
                                            ag=((f"gu1_{nx}", 1), (f"in_{nx}", 0)))
        if nx < L:
            h, n_next = run(ffn_down_fwd, s["a2"], full[f"d2_{l}"], h, (g_ffn1, nx), f"ffn2_down{l}")
        else:
            h = run(ffn_down_fwd, s["a2"], full[f"d2_{l}"], h, None, f"ffn2_down{l}")
        saved.append(s)

    loss_part, dh, dhb, dg_final = loss_head(h, vec(norm_final), target, "loss_head")

    small = {k: [None] * L for k in ("ffn1", "mix", "sinks", "osb", "osw", "ffn2", "dsc")}
    for l in reversed(range(L)):
        s = saved[l]

        def ffn_bwd(dh, dhb, tag, gate, up, a, n, h_in, g, r_down, r_up):
            gu_n, d_n = f"gu{tag}_{l}", f"d{tag}_{l}"
            dgu, dwd, dwdb, dwgu, dwgub = run(ffn_down_bwd, dhb, full[d_n], gate, up, a, n, f"ffn{tag}_down_bwd{l}", **r_down)
            grads[gu_n], grads[d_n] = slots((dwgu, dwgub)), slots((dwd, dwdb))
            return run(nn_rms_bwd, dgu, gu(gu_n), h_in, g, dh, f"ffn{tag}_up_bwd{l}", **r_up)

        later = l + 1 < L
        dh, dhb, small["ffn2"][l] = ffn_bwd(dh, dhb, 2, s["gate2"], s["up2"], s["a2"], s["n3"], s["h2"], (g_ffn2, l),
                                            dict(rs2=((f"gu1_{l + 1}", 0), f"d1_{l + 1}") if later else ()),
                                            dict(rs1=(f"gu2_{l}", f"d2_{l}")))
        do_sb, do_sw, small["osb"][l], small["osw"][l], dw_out, dw_out_b = mix_out_bwd(
            dhb, full[f"out_{l}"], s["mixed"], s["o_sb"], s["o_sw"], (g_osb, l), (g_osw, l), f"mix_out_bwd{l}")
        grads[f"out_{l}"] = slots((dw_out, dw_out_b))
        dq_sb, dk_sb, dv_sb, dq_sw, dk_sw, dv_sw, small["sinks"][l], small["dsc"][l] = run(
            attn_bwd, s["p"], do_sb, s["tot"], do_sw, s["lse"], (sinks, l), f"attn_bwd{l}",
            rs2=(f"gu2_{l}", f"d2_{l}") + (((f"gu1_{l + 1}", 1),) if later else ()))
        dp = jnp.concatenate([dq_sb, dk_sb, dv_sb, dq_sw, dk_sw, dv_sw], axis=1)
        dh, dhb, small["mix"][l] = nn_rms_bwd(dp[None], full[f"in_{l}"][None], s["h1"], (g_mix, l), dh, f"mix_in_bwd{l}")
        grads[f"in_{l}"] = slots(tn_matmul(dp[None], s["n2"], 1.0, f"dwin{l}"))
        dh, dhb, small["ffn1"][l] = ffn_bwd(dh, dhb, 1, s["gate1"], s["up1"], s["a1"], s["n1"], s["h0"], (g_ffn1, l),
                                            dict(rs1=(f"in_{l}", f"out_{l}")),
                                            dict(rs1=(f"gu1_{l}", f"d1_{l}"), rs2=(f"in_{l}", f"out_{l}")))

    grad_x = dh.reshape(x.shape)

    upd = {}
    turn_of = lambda transposed: (lambda a: jnp.swapaxes(a, 1, 2)) if transposed else (lambda a: a)

    def update(nm, w, m, v, transposed, riders=()):
        turn = turn_of(transposed)
        names = [f"{nm}_{l}" for l in range(L)]
        return adamw_scattered(turn(w), turn(m), turn(v), [chip_sum[n][0] for n in names], [recv_b[n] for n in names],
                               f"adamw_{nm}", riders=riders, rows=88 if riders is PARTS else 176)

    early = (("gu2", w_ffn2_gu, m_w_ffn2_gu, v_w_ffn2_gu, True), ("d2", w_ffn2_down, m_w_ffn2_down, v_w_ffn2_down, False),
             ("in", w_in, m_w_in, v_w_in, True), ("out", w_out, m_w_out, v_w_out, False))
    res = run(lambda name, riders=(): in_one_call([update(*e, riders=PARTS) for e in early], name, riders),
              "adamw_early", rs2=("gu1_0", "d1_0"))
    for k, e in enumerate(early):
        upd[e[0]] = tuple(turn_of(e[4])(r) for r in res[4 * k:4 * k + 4])
    for e in (("gu1", w_ffn1_gu, m_w_ffn1_gu, v_w_ffn1_gu, True), ("d1", w_ffn1_down, m_w_ffn1_down, v_w_ffn1_down, False)):
        upd[e[0]] = tuple(turn_of(e[4])(r) for r in update(*e))

    d_rel = rel_bias_grad(small["dsc"], bprev, bcur, "rel_bias_grad")[:, :8]
    g_small = pack(small["ffn1"], small["mix"], small["ffn2"], dg_final, small["osb"], small["osw"], small["sinks"], d_rel,
                   loss_part[0, :1])
    m_small = pack(m_norm_ffn1, m_norm_mix, m_norm_ffn2, m_norm_final, m_norm_out_sb, m_norm_out_swa, m_sinks, m_rel_bias, zero)
    v_small = pack(v_norm_ffn1, v_norm_mix, v_norm_ffn2, v_norm_final, v_norm_out_sb, v_norm_out_swa, v_sinks, v_rel_bias, zero)
    gs_small = all_gather_rows(g_small, "ag_small")
    summed = adamw_small(w_small, gs_small, m_small, v_small, "adamw_small")
    small_out = [unpack(a) for a in summed]
    loss = summed[0][4 * L + 1, 8 * L + N_BUCKETS * 8]

    def group(k):
        sm = small_out[k]
        return (sm[0], upd["gu1"][k], upd["d1"][k], sm[1], upd["in"][k], sm[2], sm[3], sm[4], upd["out"][k], sm[5],
                upd["gu2"][k], upd["d2"][k], sm[6], sm[7])

    return (loss, grad_x, *group(0), *group(1), *group(2), *group(3))
```

```python
import math

import jax
import jax.numpy as jnp
from jax import lax
from jax.experimental import pallas as pl
from jax.experimental.pallas import tpu as pltpu

F32 = jnp.float32
BF16 = jnp.bfloat16
S = jax.ShapeDtypeStruct

N_DEV = 8
HEAD_DIM = 64
SB_HEADS = 8
PAIR = 2 * HEAD_DIM
SB_W = 512
SWA_W = 512
KV_W = 128
IN_W = 3 * SB_W + SWA_W + 2 * KV_W
QB = 128
N_BUCKETS = 32
MAX_DISTANCE = 128
EPS = 1e-6
NEG_INF = -1e30
SCALE = HEAD_DIM ** -0.5

ADAM_LR = 0.001
ADAM_B1 = 0.9
ADAM_B2 = 0.999
ADAM_EPS = 1e-08
ADAM_WD = 0.01
ADAM_STEP = 10

VMEM_LIMIT = 56 * 1024 * 1024
MESH = pl.DeviceIdType.MESH


def _params(sem=None, vmem=VMEM_LIMIT):
    return pltpu.CompilerParams(dimension_semantics=sem, vmem_limit_bytes=vmem)


def _nn(a, b):
    return jnp.dot(a, b, preferred_element_type=F32)


def _nt(a, b):
    return lax.dot_general(a, b, (((1,), (1,)), ((), ())), preferred_element_type=F32)


def _tn(a, b):
    return lax.dot_general(a, b, (((0,), (0,)), ((), ())), preferred_element_type=F32)


def _tri(xs, m):
    return [_nn(x.astype(BF16), m) for x in xs]


def _rms(x, g):
    r = lax.rsqrt(jnp.mean(x * x, axis=-1, keepdims=True) + EPS)
    return x * r * g


def _rms_bwd(dy, x, g):
    r = lax.rsqrt(jnp.mean(x * x, axis=-1, keepdims=True) + EPS)
    xhat = x * r
    u = dy * g
    dx = r * (u - xhat * jnp.mean(u * xhat, axis=-1, keepdims=True))
    return dx, jnp.sum(dy * xhat, axis=0, keepdims=True)


def _softplus_logsig(z):
    sp = jnp.maximum(z, 0.0) + jnp.log(1.0 + jnp.exp(-jnp.abs(z)))
    return sp, z - sp


def _gain(g):
    if isinstance(g, tuple):
        rows, n = g
        return rows, pl.BlockSpec((None, 1, rows.shape[2]), lambda *_: (n, 0, 0))
    return g, pl.BlockSpec((1, g.shape[1]), lambda *_: (0, 0))


def _tile(n, want):
    t = min(n, want)
    while n % t:
        t //= 2
    return t


def _place():
    x, y, c = lax.axis_index("x"), lax.axis_index("y"), lax.axis_index("c")
    chips = [(1 - x, y), (x, 1 - y), (1 - x, 1 - y)]
    return x, y, c, chips


def all_gather_rows(v, name):
    R, C = v.shape

    def body(v_ref, out_ref, send_sems, recv_sems, local_sem):
        x, y, c, chips = _place()
        me, sibling = (x, y, c), (x, y, 1 - c)

        def slot(px, py, pc):
            return out_ref.at[4 * px + 2 * py + pc]

        def copy(k, block, to, src=None):
            return pltpu.make_async_remote_copy(
                src_ref=slot(*block) if src is None else src, dst_ref=slot(*block),
                send_sem=send_sems.at[k], recv_sem=recv_sems.at[k], device_id=to, device_id_type=MESH)

        mine = pltpu.make_async_copy(v_ref, slot(*me), local_sem)
        mine.start()
        first = [copy(0, me, sibling, src=v_ref)]
        first += [copy(1 + j, me, (*chip, c), src=v_ref) for j, chip in enumerate(chips)]
        for cp in first:
            cp.start()
        passed = [copy(4 + j, (*chip, c), sibling) for j, chip in enumerate(chips)]
        for j, chip in enumerate(chips):
            copy(1 + j, (*chip, c), me).wait_recv()
            passed[j].start()
        copy(0, sibling, me).wait_recv()
        for j, chip in enumerate(chips):
            copy(4 + j, (*chip, 1 - c), me).wait_recv()
        for cp in first + passed:
            cp.wait_send()
        mine.wait()

    return pl.pallas_call(
        body, name=name, out_shape=S((N_DEV, R, C), v.dtype),
        in_specs=[pl.BlockSpec(memory_space=pl.ANY)], out_specs=pl.BlockSpec(memory_space=pl.ANY),
        scratch_shapes=[pltpu.SemaphoreType.DMA((7,)), pltpu.SemaphoreType.DMA((7,)), pltpu.SemaphoreType.DMA],
    )(v)


class _Exchange:
    def __init__(self, ins, outs, sizes, n_local, plan, aliases=None):
        self.ins, self.outs, self.plan, self.aliases = list(ins), list(outs), plan, aliases or {}
        self.sizes, self.n_local = list(sizes), n_local

    def scratch(self):
        n = sum(self.sizes)
        return [pltpu.SemaphoreType.DMA((n,)), pltpu.SemaphoreType.DMA((n,)), pltpu.SemaphoreType.DMA((max(self.n_local, 1),))]

    def _copies(self, in_refs, out_refs, sems):
        send_sems, recv_sems, local_sems = sems
        phases, local = self.plan(in_refs, out_refs)
        out, k = [], 0
        for phase in phases:
            out.append([pltpu.make_async_remote_copy(src_ref=s, dst_ref=d, send_sem=send_sems.at[k + n], recv_sem=recv_sems.at[k + n],
                                                     device_id=dev, device_id_type=MESH) for n, (s, d, dev) in enumerate(phase)])
            k += len(phase)
        return out, [pltpu.make_async_copy(s, d, local_sems.at[n]) for n, (s, d) in enumerate(local)]

    def start(self, in_refs, out_refs, sems):
        phases, loc = self._copies(in_refs, out_refs, sems)
        for cp in phases[0] + loc:
            cp.start()

    def advance(self, hook, in_refs, out_refs, sems):
        p = hook - (3 - len(self.sizes))
        if p >= 1:
            phases, _ = self._copies(in_refs, out_refs, sems)
            for cp in phases[p - 1]:
                cp.wait_recv()
            for cp in phases[p]:
                cp.start()

    def finish(self, in_refs, out_refs, sems):
        phases, loc = self._copies(in_refs, out_refs, sems)
        for cp in phases[-1]:
            cp.wait_recv()
        for phase in phases:
            for cp in phase:
                cp.wait_send()
        for cp in loc:
            cp.wait()


def gather(v, rows=None, into=None):
    R, C = v.shape
    r0, nr = rows or (0, R)
    na = min(nr, ((nr // 2 + 15) // 16) * 16)

    def plan(ins, outs):
        x, y, c, _ = _place()
        xn, yn, dg, sibling = (1 - x, y), (x, 1 - y), (1 - x, 1 - y), (x, y, 1 - c)
        slot = lambda chip, start=r0, count=nr: outs[0].at[4 * chip[0] + 2 * chip[1] + c, pl.ds(start, count), :]
        src, mine = ins[0].at[pl.ds(r0, nr), :], slot((x, y))
        same = lambda ref, to: (ref, ref, to)
        first = [(src, mine, sibling), (src, mine, (*xn, c)), (src, mine, (*yn, c))]
        relay = [same(slot(xn, r0, na), (*yn, c)), same(slot(yn, r0 + na, nr - na), (*xn, c))]
        onward = [same(slot(xn), sibling), same(slot(yn), sibling), same(slot(dg), sibling)]
        return [first, relay, onward], [(src, mine)]

    if into is None:
        return _Exchange([v], [S((N_DEV, R, C), v.dtype)], (3, 2, 3), 1, plan)
    return _Exchange([v, into], [S((N_DEV, R, C), v.dtype)], (3, 2, 3), 1, plan, aliases={1: 0})


def scatter_first(gb):
    _, R, C = gb.shape

    def plan(ins, outs):
        x, y, c, chips = _place()
        owners = [(x, y)] + chips
        return [[(ins[0].at[4 * px + 2 * py + (1 - c)], outs[0].at[j], (x, y, 1 - c)) for j, (px, py) in enumerate(owners)]], []

    return _Exchange([gb], [S((4, R, C), BF16)], (4,), 0, plan)


def scatter_second(sb, rows=None, into=None):
    r0, nr = rows or (0, sb.shape[1])

    def plan(ins, outs):
        x, y, c, chips = _place()
        part = lambda ref, j: ref.at[j, pl.ds(r0, nr), :]
        return [[(part(ins[0], j), part(outs[0], j), (*chips[j], c)) for j in range(3)]], []

    if into is None:
        return _Exchange([sb], [S(sb.shape, BF16)], (3,), 0, plan)
    return _Exchange([sb, into], [S(sb.shape, BF16)], (3,), 0, plan, aliases={1: 0})


PARTS = "parts"


def _call(body, *, name, grid, in_specs, out_specs, out_shape, args, scratch=(), sem=None, riders=(), marks=None):
    single = not isinstance(out_shape, (tuple, list))
    out_shape = (out_shape,) if single else tuple(out_shape)
    out_specs = (out_specs,) if single else tuple(out_specs)
    n_in, n_out, n_sc = len(in_specs), len(out_shape), len(scratch)
    if riders is PARTS:
        return dict(body=body, grid=grid, in_specs=list(in_specs), out_specs=out_specs, out_shape=out_shape, args=tuple(args),
                    scratch=list(scratch), marks=marks)
    if not riders:
        res = pl.pallas_call(body, name=name, grid=grid, in_specs=list(in_specs), out_specs=out_specs, out_shape=out_shape,
                             scratch_shapes=list(scratch), compiler_params=_params(sem))(*args)
        return res[0] if single else res
    r_ins = [a for r in riders for a in r.ins]
    r_outs = [o for r in riders for o in r.outs]
    r_scr = [s for r in riders for s in r.scratch()]
    aliases, i0, o0 = {}, n_in, n_out
    for r in riders:
        for a, b in r.aliases.items():
            aliases[i0 + a] = o0 + b
        i0, o0 = i0 + len(r.ins), o0 + len(r.outs)
    steps = math.prod(grid)

    def full(*refs):
        ins, rin = refs[:n_in], refs[n_in:n_in + len(r_ins)]
        pos = n_in + len(r_ins)
        outs, rout = refs[pos:pos + n_out], refs[pos + n_out:pos + n_out + len(r_outs)]
        pos += n_out + len(r_outs)
        sc, rsc = refs[pos:pos + n_sc], refs[pos + n_sc:]
        step = 0
        for d, n in enumerate(grid):
            step = step * n + pl.program_id(d)

        def each(method, *lead):
            i, o = 0, 0
            for k, r in enumerate(riders):
                getattr(r, method)(*lead, rin[i:i + len(r.ins)], rout[o:o + len(r.outs)], rsc[3 * k:3 * k + 3])
                i, o = i + len(r.ins), o + len(r.outs)

        @pl.when(step == 0)
        def _():
            each("start")
        body(*ins, *outs, *sc)

        late = max(steps - 1 - max(steps // 8, 1), 0)
        first, second = marks or (min((3 * steps) // 5, late), late)

        @pl.when(step == first)
        def _():
            each("advance", 1)

        @pl.when(step == second)
        def _():
            each("advance", 2)

        @pl.when(step == steps - 1)
        def _():
            each("finish")

    anywhere = pl.BlockSpec(memory_space=pl.ANY)
    res = pl.pallas_call(
        full, name=name, grid=grid, in_specs=list(in_specs) + [anywhere] * len(r_ins),
        out_specs=out_specs + (anywhere,) * len(r_outs), out_shape=out_shape + tuple(r_outs),
        scratch_shapes=list(scratch) + r_scr, input_output_aliases=aliases,
        compiler_params=_params(("arbitrary",) * len(grid)))(*args, *r_ins)
    host, rest, per = res[:n_out], list(res[n_out:]), []
    for r in riders:
        per.append(rest[:len(r.outs)])
        rest = rest[len(r.outs):]
    return (host[0] if single else tuple(host)), per


def side_by_side(first, second, name, riders=()):
    a_in, a_out, a_sc = len(first["in_specs"]), len(first["out_shape"]), len(first["scratch"])
    n_in, n_out = a_in + len(second["in_specs"]), a_out + len(second["out_shape"])

    def body(*refs):
        ins, outs, sc = refs[:n_in], refs[n_in:n_in + n_out], refs[n_in + n_out:]
        first["body"](*ins[:a_in], *outs[:a_out], *sc[:a_sc])
        second["body"](*ins[a_in:], *outs[a_out:], *sc[a_sc:])

    return _call(body, name=name, grid=first["grid"], in_specs=first["in_specs"] + second["in_specs"],
                 out_specs=first["out_specs"] + second["out_specs"], out_shape=first["out_shape"] + second["out_shape"],
                 args=first["args"] + second["args"], scratch=first["scratch"] + second["scratch"],
                 sem=("arbitrary",) * len(first["grid"]), riders=riders, marks=first["marks"])


def in_one_call(parts, name, riders=()):
    extents = [p["grid"][-1] for p in parts]
    longest = max(extents)

    def clamp(spec, n):
        if n == longest or spec.index_map is None:
            return spec
        return pl.BlockSpec(spec.block_shape, lambda *ids, f=spec.index_map: f(*ids[:-1], jnp.minimum(ids[-1], n - 1)))

    counts = [(len(p["in_specs"]), len(p["out_shape"]), len(p["scratch"])) for p in parts]
    n_in, n_out = sum(c[0] for c in counts), sum(c[1] for c in counts)

    def body(*refs):
        ins, outs, sc = refs[:n_in], refs[n_in:n_in + n_out], refs[n_in + n_out:]
        i = o = s = 0
        for p, n, (ci, co, cs) in zip(parts, extents, counts):
            run_part = lambda p=p, a=ins[i:i + ci], b=outs[o:o + co], c=sc[s:s + cs]: p["body"](*a, *b, *c)
            if n == longest:
                run_part()
            else:
                pl.when(pl.program_id(len(p["grid"]) - 1) < n)(run_part)
            i, o, s = i + ci, o + co, s + cs

    cat = lambda key: [x for p in parts for x in p[key]]
    return _call(body, name=name, grid=parts[0]["grid"][:-1] + (longest,),
                 in_specs=[clamp(sp, n) for p, n in zip(parts, extents) for sp in p["in_specs"]],
                 out_specs=tuple(clamp(sp, n) for p, n in zip(parts, extents) for sp in p["out_specs"]),
                 out_shape=tuple(cat("out_shape")), args=tuple(cat("args")), scratch=cat("scratch"),
                 sem=("arbitrary",) * len(parts[0]["grid"]), riders=riders)


def _rows_tile(n, cap):
    return max(t for t in range(16, min(n, cap) + 1, 16) if n % t == 0)


def scatter_add(gs, ras, name):
    C = gs[0].shape[2]
    trs = [_rows_tile(g.shape[1], 176) for g in gs]
    nts = [g.shape[1] // tr for g, tr in zip(gs, trs)]
    steps = max(nts)
    x, y, c, chips = _place()
    slots = jnp.stack([4 * px + 2 * py + c for px, py in [(x, y)] + chips]).astype(jnp.int32)

    def body(s_ref, *refs):
        ins, outs = refs[:5 * len(gs)], refs[5 * len(gs):]
        for k in range(len(gs)):
            g0, g1, g2, g3, ra_ref = ins[5 * k:5 * k + 5]
            own_ref, sb_ref = outs[2 * k:2 * k + 2]

            def work(g0=g0, g1=g1, g2=g2, g3=g3, ra_ref=ra_ref, own_ref=own_ref, sb_ref=sb_ref):
                own_ref[...] = g0[...] + ra_ref[0].astype(F32)
                for j, gj in enumerate((g1, g2, g3)):
                    sb_ref[j] = (gj[...] + ra_ref[j + 1].astype(F32)).astype(BF16)

            if nts[k] == steps:
                work()
            else:
                pl.when(pl.program_id(0) < nts[k])(work)

    in_specs, out_specs, out_shape, args = [], [], [], [slots]
    for k, (g, ra, tr) in enumerate(zip(gs, ras, trs)):
        tile = lambda i, k=k: jnp.minimum(i, nts[k] - 1)
        in_specs += [pl.BlockSpec((None, tr, C), lambda i, s, j=j, tile=tile: (s[j], tile(i), 0)) for j in range(4)]
        in_specs.append(pl.BlockSpec((4, tr, C), lambda i, s, tile=tile: (0, tile(i), 0)))
        out_specs += [pl.BlockSpec((tr, C), lambda i, s, tile=tile: (tile(i), 0)),
                      pl.BlockSpec((3, tr, C), lambda i, s, tile=tile: (0, tile(i), 0))]
        out_shape += [S((g.shape[1], C), F32), S((3, g.shape[1], C), BF16)]
        args += [g, g, g, g, ra]
    spec = pltpu.PrefetchScalarGridSpec(num_scalar_prefetch=1, grid=(steps,), in_specs=in_specs, out_specs=tuple(out_specs))
    res = pl.pallas_call(body, name=name, grid_spec=spec, out_shape=tuple(out_shape), compiler_params=_params(("arbitrary",)))(*args)
    return [(res[2 * k], res[2 * k + 1]) for k in range(len(gs))]


def rms_cast(h, g, name, riders=()):
    T, D = h.shape
    tm = _tile(T, 512)

    def body(h_ref, g_ref, n_ref):
        n_ref[...] = _rms(h_ref[...], g_ref[...]).astype(BF16)

    row = pl.BlockSpec((tm, D), lambda i: (i, 0))
    g, g_spec = _gain(g)
    return _call(body, name=name, grid=(T // tm,), out_shape=S((T, D), BF16), in_specs=[row, g_spec],
                 out_specs=row, sem=("parallel",), args=(h, g), riders=riders)


def ffn_up_fwd(n, wgu, name, riders=()):
    T, D = n.shape
    F = wgu.shape[1]
    tr, tn = _tile(T, 512), _tile(F, 256)

    def body(n_ref, wg_ref, wu_ref, dgate_ref, dup_ref, a_ref):
        wg, wu = wg_ref[...], wu_ref[...]
        for r in range(T // tr):
            rows = slice(r * tr, (r + 1) * tr)
            x = n_ref[rows, :]
            gate = _nt(x, wg)
            up = _nt(x, wu)
            s = jax.nn.sigmoid(gate)
            silu = gate * s
            dgate_ref[rows, :] = (up * (s * (1.0 + gate * (1.0 - s)))).astype(BF16)
            dup_ref[rows, :] = silu.astype(BF16)
            a_ref[rows, :] = (silu * up).astype(BF16)

    tile = pl.BlockSpec((T, tn), lambda j: (0, j))
    return _call(
        body, name=name, grid=(F // tn,), out_shape=(S((T, F), BF16),) * 3,
        in_specs=[pl.BlockSpec((T, D), lambda j: (0, 0)),
                  pl.BlockSpec((None, tn, D), lambda j: (0, j, 0)), pl.BlockSpec((None, tn, D), lambda j: (1, j, 0))],
        out_specs=(tile, tile, tile), sem=("parallel",), args=(n, wgu, wgu), riders=riders)


def ffn_down_fwd(a, wd, h, g_next, name, riders=()):
    T, F = a.shape
    D = wd.shape[1]
    tm = _tile(T, 256)

    def body(a_ref, w_ref, h_ref, *rest):
        out = h_ref[...] + 0.5 * _nn(a_ref[...], w_ref[...])
        if g_next is None:
            rest[0][...] = out
        else:
            g_ref, o_ref, n_ref = rest
            o_ref[...] = out
            n_ref[...] = _rms(out, g_ref[...]).astype(BF16)

    row = pl.BlockSpec((tm, D), lambda i: (i, 0))
    more = g_next is not None
    g_arg, g_spec = _gain(g_next) if more else (None, None)
    return _call(
        body, name=name, grid=(T // tm,), out_shape=(S((T, D), F32), S((T, D), BF16)) if more else S((T, D), F32),
        in_specs=[pl.BlockSpec((tm, F), lambda i: (i, 0)), pl.BlockSpec((F, D), lambda i: (0, 0)), row] + ([g_spec] if more else []),
        out_specs=(row, row) if more else row,
        sem=("parallel",), args=(a, wd, h) + ((g_arg,) if more else ()), riders=riders)


def mix_in_fwd(h, g, win, name):
    T, D = h.shape
    N = win.shape[0]
    tm = _tile(T, 256)

    def body(h_ref, g_ref, w_ref, n_ref, p_ref):
        n = _rms(h_ref[...], g_ref[...]).astype(BF16)
        n_ref[...] = n
        p_ref[...] = _nt(n, w_ref[...]).astype(BF16)

    g, g_spec = _gain(g)
    return pl.pallas_call(
        body, name=name, grid=(T // tm,), out_shape=(S((T, D), BF16), S((T, N), BF16)),
        in_specs=[pl.BlockSpec((tm, D), lambda i: (i, 0)), g_spec, pl.BlockSpec((N, D), lambda i: (0, 0))],
        out_specs=(pl.BlockSpec((tm, D), lambda i: (i, 0)), pl.BlockSpec((tm, N), lambda i: (i, 0))),
        compiler_params=_params(("parallel",)),
    )(h, g, win)


def _tri_consts():
    r = lax.broadcasted_iota(jnp.int32, (QB, QB), 0)
    c = lax.broadcasted_iota(jnp.int32, (QB, QB), 1)
    ones = jnp.ones((QB, QB), BF16)
    with_sums = lambda tri: jnp.concatenate([tri.astype(BF16), ones], axis=1)
    return with_sums(r > c), with_sums(r <= c), with_sums(r < c)


def _half_masks():
    lane = lax.broadcasted_iota(jnp.int32, (QB, PAIR), 1)
    row = lax.broadcasted_iota(jnp.int32, (QB, PAIR), 0)
    return lane < HEAD_DIM, lane, row


def sb_attn_fwd(p, after, name, riders=()):
    T = p.shape[0]
    nq = T // QB

    def body(q_ref, k_ref, v_ref, m_ref, o_ref, tot_ref, q_sc, acc_ref, z_sc):
        i = pl.program_id(0)
        lo, lane, row = _half_masks()
        causal = lane < row
        heads, pairs = range(SB_HEADS), range(SB_HEADS // 2)
        for hp in pairs:
            q_sc[hp] = (q_ref[:, hp * PAIR:(hp + 1) * PAIR].astype(F32) * SCALE).astype(BF16)
        m2 = m_ref[...]

        def by_head(ref, j, hp):
            t = ref[pl.ds(pl.multiple_of(j * QB, QB), QB), hp * PAIR:(hp + 1) * PAIR]
            return jnp.concatenate([jnp.where(lo, t, 0), jnp.where(lo, 0, t)], axis=0)

        def scores(j):
            return [_nt(q_sc[hp], by_head(k_ref, j, hp)) for hp in pairs]

        def block(j, diag):
            z2 = [z_sc[hp] for hp in pairs]
            ahead = scores(jnp.maximum(j - 1, 0))
            for hp in pairs:
                z_sc[hp] = ahead[hp]
            vs = [by_head(v_ref, j, hp) for hp in pairs]
            spls = [_softplus_logsig(z2[h // 2][:, (h % 2) * QB:(h % 2 + 1) * QB]) for h in heads]
            sp = [jnp.where(causal, spls[h][0], 0.0) if diag else spls[h][0] for h in heads]
            rr = _tri(sp, m2)
            if diag:
                w = [jnp.where(causal, jnp.exp(spls[h][1] - rr[h][:, :QB]), 0.0).astype(BF16) for h in heads]
            else:
                c = [tot_ref[:, h * QB:(h + 1) * QB] for h in heads]
                w = [jnp.exp(spls[h][1] - (c[h] + rr[h][:, :QB])).astype(BF16) for h in heads]
            pv = [_nn(jnp.concatenate([w[2 * hp], w[2 * hp + 1]], axis=1), vs[hp]) for hp in pairs]
            for hp in pairs:
                acc_ref[hp] = pv[hp] if diag else acc_ref[hp] + pv[hp]
            for h in heads:
                tot_ref[:, h * QB:(h + 1) * QB] = rr[h][:, QB:] if diag else c[h] + rr[h][:, QB:]

        first = scores(i)
        for hp in pairs:
            z_sc[hp] = first[hp]
        block(i, True)

        def step(t, carry):
            block(i - 1 - t, False)
            return carry
        lax.fori_loop(0, i, step, 0)
        for hp in pairs:
            o_ref[:, hp * PAIR:(hp + 1) * PAIR] = acc_ref[hp]

    npair = SB_HEADS // 2
    return _call(
        body, name=name, grid=(nq,), out_shape=(S((T, SB_W), F32), S((T, SB_HEADS * QB), F32)),
        in_specs=[pl.BlockSpec((QB, SB_W), lambda i: (i, 0)), pl.BlockSpec((T, SB_W), lambda i: (0, 1)),
                  pl.BlockSpec((T, SB_W), lambda i: (0, 2)), pl.BlockSpec((QB, 2 * QB), lambda i: (0, 0))],
        out_specs=(pl.BlockSpec((QB, SB_W), lambda i: (i, 0)), pl.BlockSpec((QB, SB_HEADS * QB), lambda i: (i, 0))),
        scratch=[pltpu.VMEM((npair, QB, PAIR), BF16), pltpu.VMEM((npair, QB, PAIR), F32), pltpu.VMEM((npair, QB, 2 * QB), F32)],
        sem=("arbitrary",), args=(p, p, p, after), riders=riders,
        marks=((10 * nq) // 16, (14 * nq) // 16))


def sb_attn_bwd(p, do, tot, upto, before, name, riders=()):
    T = p.shape[0]
    nq = T // QB

    def body(q_ref, k_ref, v_ref, do_ref, tot_ref, mp_ref, mg_ref, dq_ref, dk_ref, dv_ref,
             q_sc, d_sc, qd_sc, pg_sc, dq_acc, dk_acc, dv_acc, zd_sc):
        i = pl.program_id(0)
        lo, lane, row = _half_masks()
        causal = lane < row
        heads, pairs = range(SB_HEADS), range(SB_HEADS // 2)

        def by_head(t):
            return jnp.concatenate([jnp.where(lo, t, 0), jnp.where(lo, 0, t)], axis=0)

        for hp in pairs:
            q2 = (q_ref[:, hp * PAIR:(hp + 1) * PAIR].astype(F32) * SCALE).astype(BF16)
            d2 = do_ref[:, hp * PAIR:(hp + 1) * PAIR].astype(BF16)
            q_sc[hp] = q2
            d_sc[hp] = d2
            qd_sc[hp] = by_head(q2)
            qd_sc[SB_HEADS // 2 + hp] = by_head(d2)
        mp, mg = mp_ref[...], mg_ref[...]

        @pl.when(i == 0)
        def _():
            dk_acc[...] = jnp.zeros_like(dk_acc)
            dv_acc[...] = jnp.zeros_like(dv_acc)
        pg_sc[...] = jnp.zeros_like(pg_sc)
        dq_acc[...] = jnp.zeros_like(dq_acc)

        def rows(ref, j, hp):
            return ref[pl.ds(pl.multiple_of(j * QB, QB), QB), hp * PAIR:(hp + 1) * PAIR]

        def products(j):
            return ([_nt(q_sc[hp], by_head(rows(k_ref, j, hp))) for hp in pairs]
                    + [_nt(d_sc[hp], by_head(rows(v_ref, j, hp))) for hp in pairs])

        def block(j, diag):
            r0 = pl.multiple_of(j * QB, QB)
            half = lambda t, h: t[:, (h % 2) * QB:(h % 2 + 1) * QB]
            z = [half(zd_sc[h // 2], h) for h in heads]
            dw = [half(zd_sc[SB_HEADS // 2 + h // 2], h) for h in heads]
            if not diag:
                ahead = products(j + 1)
                for hp in range(SB_HEADS):
                    zd_sc[hp] = ahead[hp]
            ks = [by_head(rows(k_ref, j, hp)) for hp in pairs]
            spls = [_softplus_logsig(z[h]) for h in heads]
            sp = [jnp.where(causal, spls[h][0], 0.0) if diag else spls[h][0] for h in heads]
            rr = _tri(sp, mp)
            pc = [pg_sc[2 * h] for h in heads]
            w = [jnp.exp(spls[h][1] - (tot_ref[:, h * QB:(h + 1) * QB] - (pc[h] + rr[h][:, :QB]))) for h in heads]
            if diag:
                w = [jnp.where(causal, w[h], 0.0) for h in heads]
            gg = [dw[h] * w[h] for h in heads]
            rg = _tri(gg, mg)
            gc = [pg_sc[2 * h + 1] for h in heads]
            dz = [gg[h] - (gg[h] + gc[h] + rg[h][:, :QB]) * jnp.exp(spls[h][1]) for h in heads]
            if diag:
                dz = [jnp.where(causal, dz[h], 0.0) for h in heads]
            dzb = [dz[h].astype(BF16) for h in heads]
            wb = [w[h].astype(BF16) for h in heads]
            both = lambda t, hp, axis: jnp.concatenate([t[2 * hp], t[2 * hp + 1]], axis=axis)
            dq = [_nn(both(dzb, hp, 1), ks[hp]) for hp in pairs]
            dk = [_tn(both(dzb, hp, 0), qd_sc[hp]) for hp in pairs]
            dv = [_tn(both(wb, hp, 0), qd_sc[SB_HEADS // 2 + hp]) for hp in pairs]
            for h in heads:
                if not diag:
                    pg_sc[2 * h] = pc[h] + rr[h][:, QB:]
                    pg_sc[2 * h + 1] = gc[h] + rg[h][:, QB:]
            for hp in pairs:
                dq_acc[hp] += dq[hp]
                dk_acc[pl.ds(r0, QB), hp * PAIR:(hp + 1) * PAIR] += dk[hp]
                dv_acc[pl.ds(r0, QB), hp * PAIR:(hp + 1) * PAIR] += dv[hp]

        first = products(0)
        for hp in range(SB_HEADS):
            zd_sc[hp] = first[hp]

        def step(t, carry):
            block(t, False)
            return carry
        lax.fori_loop(0, i, step, 0)
        block(i, True)
        for hp in pairs:
            dq_ref[:, hp * PAIR:(hp + 1) * PAIR] = (dq_acc[hp] * SCALE).astype(BF16)

        @pl.when(i == nq - 1)
        def _():
            dk_ref[...] = dk_acc[...].astype(BF16)
            dv_ref[...] = dv_acc[...].astype(BF16)

    qtile = pl.BlockSpec((QB, SB_W), lambda i: (i, 0))
    whole = pl.BlockSpec((T, SB_W), lambda i: (0, 0))
    const = pl.BlockSpec((QB, 2 * QB), lambda i: (0, 0))
    return _call(
        body, name=name, grid=(nq,), out_shape=(S((T, SB_W), BF16),) * 3,
        in_specs=[qtile, pl.BlockSpec((T, SB_W), lambda i: (0, 1)), pl.BlockSpec((T, SB_W), lambda i: (0, 2)), qtile,
                  pl.BlockSpec((QB, SB_HEADS * QB), lambda i: (i, 0)), const, const],
        out_specs=(qtile, whole, whole),
        scratch=[pltpu.VMEM((SB_HEADS // 2, QB, PAIR), BF16), pltpu.VMEM((SB_HEADS // 2, QB, PAIR), BF16),
                 pltpu.VMEM((SB_HEADS, 2 * QB, PAIR), BF16),
                 pltpu.VMEM((2 * SB_HEADS, QB, QB), F32), pltpu.VMEM((SB_HEADS // 2, QB, PAIR), F32),
                 pltpu.VMEM((T, SB_W), F32), pltpu.VMEM((T, SB_W), F32), pltpu.VMEM((SB_HEADS, QB, 2 * QB), F32)],
        sem=("arbitrary",), args=(p, p, p, do, tot, upto, before), riders=riders)


def _t5_buckets():
    a = lax.broadcasted_iota(jnp.int32, (QB, QB), 0)
    c = lax.broadcasted_iota(jnp.int32, (QB, QB), 1)

    def bucket(dist):
        dist = jnp.maximum(dist, 0)
        max_exact = N_BUCKETS // 2
        d = jnp.maximum(dist, 1).astype(F32)
        large = max_exact + (jnp.log(d / max_exact) / math.log(MAX_DISTANCE / max_exact)
                             * (N_BUCKETS - max_exact)).astype(jnp.int32)
        large = jnp.minimum(large, N_BUCKETS - 1)
        return jnp.where(dist < max_exact, dist, large)

    return bucket(QB + a - c), bucket(a - c)


def _swa_common(i, kp_ref, kc_ref, vp_ref, vc_ref, bp_ref, bc_ref, rb_ref, bias_ref):
    lo, lane, row = _half_masks()

    @pl.when(i == 0)
    def _():
        for blk, b_ref in enumerate((bp_ref, bc_ref)):
            bk = b_ref[...]
            for h in range(8):
                acc = jnp.zeros((QB, QB), F32)
                for b in range(N_BUCKETS):
                    acc = jnp.where(bk == b, rb_ref[b, h], acc)
                bias_ref[h, blk] = acc

    band = [(lane > row) & (i > 0), lane <= row]

    def stacks(ref):
        t = ref[...].astype(F32)
        sw = pltpu.roll(t, HEAD_DIM, 1)
        return [jnp.concatenate([jnp.where(lo, t, 0.0), jnp.where(lo, 0.0, sw)], axis=0).astype(BF16),
                jnp.concatenate([jnp.where(lo, sw, 0.0), jnp.where(lo, 0.0, t)], axis=0).astype(BF16)]

    ks = [stacks(kp_ref), stacks(kc_ref)]
    vs = [stacks(vp_ref), stacks(vc_ref)]
    return lo, band, ks, vs


def _lane_half(t, h):
    return t[:, (h % 2) * QB:(h % 2 + 1) * QB]


def swa_fwd(p, sinks, rel_bias, bprev, bcur, name, riders=()):
    T = p.shape[0]
    nq = T // QB
    kcol, vcol = (3 * SB_W + SWA_W) // KV_W, (3 * SB_W + SWA_W) // KV_W + 1
    sinks, srow = sinks if isinstance(sinks, tuple) else (sinks, 0)

    def body(q_ref, kp_ref, kc_ref, vp_ref, vc_ref, bp_ref, bc_ref, sink_ref, rb_ref, o_ref, lse_ref, bias_ref):
        i = pl.program_id(0)
        lo, band, ks, vs = _swa_common(i, kp_ref, kc_ref, vp_ref, vc_ref, bp_ref, bc_ref, rb_ref, bias_ref)
        heads, pairs, blocks = range(8), range(4), range(2)
        rowmax = lambda t: jnp.max(t, axis=1, keepdims=True)
        rowsum = lambda t: jnp.sum(t, axis=1, keepdims=True)
        q2 = [q_ref[:, g * PAIR:(g + 1) * PAIR] for g in pairs]
        s2 = [[_nt(q2[g], ks[b][g // 2]) for b in blocks] for g in pairs]
        sc = [[jnp.where(band[b], _lane_half(s2[h // 2][b], h) * SCALE + bias_ref[h, b], NEG_INF) for b in blocks] for h in heads]
        sink = [sink_ref[srow, h] for h in heads]
        m = [jnp.maximum(jnp.maximum(rowmax(sc[h][0]), rowmax(sc[h][1])), sink[h]) for h in heads]
        e = [[jnp.exp(sc[h][b] - m[h]) for b in blocks] for h in heads]
        den = [rowsum(e[h][0]) + rowsum(e[h][1]) + jnp.exp(sink[h] - m[h]) for h in heads]
        pb = [[(e[h][b] / den[h]).astype(BF16) for b in blocks] for h in heads]
        for g in pairs:
            both = lambda b: jnp.concatenate([pb[2 * g][b], pb[2 * g + 1][b]], axis=1)
            o_ref[:, g * PAIR:(g + 1) * PAIR] = _nn(both(0), vs[0][g // 2]) + _nn(both(1), vs[1][g // 2])
        for h in heads:
            lse_ref[:, h * QB:(h + 1) * QB] = jnp.broadcast_to(m[h] + jnp.log(den[h]), (QB, QB))

    kv = lambda col, prev: pl.BlockSpec((QB, KV_W), (lambda i: (jnp.maximum(i - 1, 0), col)) if prev else (lambda i: (i, col)))
    full = pl.BlockSpec((QB, QB), lambda i: (0, 0))
    smem = pl.BlockSpec(memory_space=pltpu.SMEM)
    return _call(
        body, name=name, grid=(nq,), out_shape=(S((T, SWA_W), F32), S((T, 8 * QB), F32)),
        in_specs=[pl.BlockSpec((QB, SWA_W), lambda i: (i, 3)), kv(kcol, True), kv(kcol, False), kv(vcol, True), kv(vcol, False),
                  full, full, smem, smem],
        out_specs=(pl.BlockSpec((QB, SWA_W), lambda i: (i, 0)), pl.BlockSpec((QB, 8 * QB), lambda i: (i, 0))),
        scratch=[pltpu.VMEM((8, 2, QB, QB), F32)],
        sem=("arbitrary",), args=(p, p, p, p, p, bprev, bcur, sinks, rel_bias), riders=riders)


def swa_bwd(p, do, lse, sinks, rel_bias, bprev, bcur, name, riders=()):
    T = p.shape[0]
    nq = T // QB
    kcol, vcol = (3 * SB_W + SWA_W) // KV_W, (3 * SB_W + SWA_W) // KV_W + 1
    sinks, srow = sinks if isinstance(sinks, tuple) else (sinks, 0)

    def body(q_ref, kp_ref, kc_ref, vp_ref, vc_ref, do_ref, lse_ref, bp_ref, bc_ref, sink_ref, rb_ref,
             dq_ref, dk_ref, dv_ref, dsink_ref, dsc_ref, bias_ref, dk_acc, dv_acc):
        i = pl.program_id(0)
        lo, band, ks, vs = _swa_common(i, kp_ref, kc_ref, vp_ref, vc_ref, bp_ref, bc_ref, rb_ref, bias_ref)

        @pl.when(i == 0)
        def _():
            dk_acc[...] = jnp.zeros_like(dk_acc)
            dv_acc[...] = jnp.zeros_like(dv_acc)
            dsc_ref[...] = jnp.zeros_like(dsc_ref)
            dsink_ref[...] = jnp.zeros_like(dsink_ref)

        heads, pairs, blocks = range(8), range(4), range(2)
        rowsum = lambda t: jnp.sum(t, axis=1, keepdims=True)
        by_head = lambda t: jnp.concatenate([jnp.where(lo, t, 0), jnp.where(lo, 0, t)], axis=0)
        q2 = [q_ref[:, g * PAIR:(g + 1) * PAIR] for g in pairs]
        d2 = [do_ref[:, g * PAIR:(g + 1) * PAIR].astype(BF16) for g in pairs]
        qs = [by_head(q2[g]) for g in pairs]
        dos = [by_head(d2[g]) for g in pairs]
        s2 = [[_nt(q2[g], ks[b][g // 2]) for b in blocks] for g in pairs]
        dp2 = [[_nt(d2[g], vs[b][g // 2]) for b in blocks] for g in pairs]
        lse_h = [lse_ref[:, h * QB:(h + 1) * QB] for h in heads]
        sink = [sink_ref[srow, h] for h in heads]
        pr = [[jnp.exp(jnp.where(band[b], _lane_half(s2[h // 2][b], h) * SCALE + bias_ref[h, b], NEG_INF) - lse_h[h])
               for b in blocks] for h in heads]
        dp = [[_lane_half(dp2[h // 2][b], h) for b in blocks] for h in heads]
        delta = [rowsum(pr[h][0] * dp[h][0]) + rowsum(pr[h][1] * dp[h][1]) for h in heads]
        lane1 = lax.broadcasted_iota(jnp.int32, (1, QB), 1)
        dsink = jnp.zeros((1, QB), F32)
        for h in heads:
            dsink = dsink + jnp.where(lane1 == h, -jnp.sum(jnp.exp(sink[h] - lse_h[h][:, :1]) * delta[h]), 0.0)
        dsink_ref[...] += dsink
        dsc = [[pr[h][b] * (dp[h][b] - delta[h]) for b in blocks] for h in heads]
        for h in heads:
            for b in blocks:
                dsc_ref[h, b] += dsc[h][b]
        dzb = [[(dsc[h][b] * SCALE).astype(BF16) for b in blocks] for h in heads]
        prb = [[pr[h][b].astype(BF16) for b in blocks] for h in heads]
        pair_of = lambda t, g, b, axis: jnp.concatenate([t[2 * g][b], t[2 * g + 1][b]], axis=axis)
        for g in pairs:
            dq = _nn(pair_of(dzb, g, 0, 1), ks[0][g // 2]) + _nn(pair_of(dzb, g, 1, 1), ks[1][g // 2])
            dq_ref[:, g * PAIR:(g + 1) * PAIR] = dq.astype(BF16)

        def key_grad(t, other, b):
            per_kv = [_tn(pair_of(t, 2 * kh, b, 0), other[2 * kh]) + _tn(pair_of(t, 2 * kh + 1, b, 0), other[2 * kh + 1]) for kh in range(2)]
            both = [s + pltpu.roll(s, HEAD_DIM, 1) for s in per_kv]
            return jnp.where(lo, both[0], both[1])

        rp = pl.multiple_of(jnp.maximum(i - 1, 0) * QB, QB)
        rc = pl.multiple_of(i * QB, QB)
        dk_acc[pl.ds(rp, QB), :] += key_grad(dzb, qs, 0)
        dv_acc[pl.ds(rp, QB), :] += key_grad(prb, dos, 0)
        dk_acc[pl.ds(rc, QB), :] += key_grad(dzb, qs, 1)
        dv_acc[pl.ds(rc, QB), :] += key_grad(prb, dos, 1)

        @pl.when(i == nq - 1)
        def _():
            dk_ref[...] = dk_acc[...].astype(BF16)
            dv_ref[...] = dv_acc[...].astype(BF16)

    kv = lambda col, prev: pl.BlockSpec((QB, KV_W), (lambda i: (jnp.maximum(i - 1, 0), col)) if prev else (lambda i: (i, col)))
    full = pl.BlockSpec((QB, QB), lambda i: (0, 0))
    smem = pl.BlockSpec(memory_space=pltpu.SMEM)
    whole = lambda shape: pl.BlockSpec(shape, lambda i: (0,) * len(shape))
    return _call(
        body, name=name, grid=(nq,),
        out_shape=(S((T, SWA_W), BF16), S((T, KV_W), BF16), S((T, KV_W), BF16), S((1, QB), F32), S((8, 2, QB, QB), F32)),
        in_specs=[pl.BlockSpec((QB, SWA_W), lambda i: (i, 3)), kv(kcol, True), kv(kcol, False), kv(vcol, True), kv(vcol, False),
                  pl.BlockSpec((QB, SWA_W), lambda i: (i, 0)), pl.BlockSpec((QB, 8 * QB), lambda i: (i, 0)),
                  full, full, smem, smem],
        out_specs=(pl.BlockSpec((QB, SWA_W), lambda i: (i, 0)), whole((T, KV_W)), whole((T, KV_W)), whole((1, QB)),
                   whole((8, 2, QB, QB))),
        scratch=[pltpu.VMEM((8, 2, QB, QB), F32), pltpu.VMEM((T, KV_W), F32), pltpu.VMEM((T, KV_W), F32)],
        sem=("arbitrary",), args=(p, p, p, p, p, do, lse, bprev, bcur, sinks, rel_bias), riders=riders)


def mix_out_fwd(o_sb, o_sw, g_sb, g_sw, wout, h, g_next, name, riders=()):
    T, D = h.shape
    M = SB_W + SWA_W
    tm = _tile(T, 256)

    def body(a_ref, b_ref, ga_ref, gb_ref, w_ref, h_ref, gn_ref, mx_ref, o_ref, n_ref):
        mx_ref[:, :SB_W] = _rms(a_ref[...], ga_ref[...]).astype(BF16)
        mx_ref[:, SB_W:] = _rms(b_ref[...], gb_ref[...]).astype(BF16)
        out = h_ref[...] + _nn(mx_ref[...], w_ref[...])
        o_ref[...] = out
        n_ref[...] = _rms(out, gn_ref[...]).astype(BF16)

    row = lambda n: pl.BlockSpec((tm, n), lambda i: (i, 0))
    (g_sb, sb_spec), (g_sw, sw_spec), (g_next, next_spec) = _gain(g_sb), _gain(g_sw), _gain(g_next)
    return _call(
        body, name=name, grid=(T // tm,), out_shape=(S((T, M), BF16), S((T, D), F32), S((T, D), BF16)),
        in_specs=[row(SB_W), row(SWA_W), sb_spec, sw_spec, pl.BlockSpec((M, D), lambda i: (0, 0)), row(D), next_spec],
        out_specs=(row(M), row(D), row(D)),
        sem=("parallel",), args=(o_sb, o_sw, g_sb, g_sw, wout, h, g_next), riders=riders)


def loss_head(h, g, target, name):
    T, D = h.shape
    tm = _tile(T, 256)

    def body(h_ref, g_ref, t_ref, loss_ref, dh_ref, dhb_ref, dg_ref):
        @pl.when(pl.program_id(0) == 0)
        def _():
            loss_ref[...] = jnp.zeros_like(loss_ref)
            dg_ref[...] = jnp.zeros_like(dg_ref)
        x = h_ref[...]
        err = _rms(x, g_ref[...]) - t_ref[...]
        loss_ref[...] += jnp.full((1, QB), 0.5 * jnp.sum(jnp.mean(err * err, axis=-1)), F32)
        dx, dg = _rms_bwd(err / D, x, g_ref[...])
        dh_ref[...] = dx
        dhb_ref[...] = dx.astype(BF16)
        dg_ref[...] += dg

    row = pl.BlockSpec((tm, D), lambda i: (i, 0))
    vec = pl.BlockSpec((1, D), lambda i: (0, 0))
    return pl.pallas_call(
        body, name=name, grid=(T // tm,), out_shape=(S((1, QB), F32), S((T, D), F32), S((T, D), BF16), S((1, D), F32)),
        in_specs=[row, vec, row], out_specs=(pl.BlockSpec((1, QB), lambda i: (0, 0)), row, row, vec),
        compiler_params=_params(("arbitrary",)),
    )(h, g, target)


def ffn_down_bwd(dhb, wd, gate, up, a, n, name, riders=()):
    T, D = dhb.shape
    F = wd.shape[0]
    tr, tn = _tile(T, 512), _tile(F, 256)

    def body(d_ref, n_ref, w_ref, g_ref, u_ref, a_ref, o_ref, dwd_ref, dwdb_ref, dwgu_ref, dwgub_ref):
        w = w_ref[...]
        for r in range(T // tr):
            rows = slice(r * tr, (r + 1) * tr)
            da = 0.5 * _nt(d_ref[rows, :], w)
            o_ref[0, rows, :] = (da * g_ref[rows, :].astype(F32)).astype(BF16)
            o_ref[1, rows, :] = (da * u_ref[rows, :].astype(F32)).astype(BF16)
        dwd = 0.5 * _tn(a_ref[...], d_ref[...])
        dwd_ref[...] = dwd
        dwdb_ref[...] = dwd.astype(BF16)
        for s in range(2):
            dwgu = _tn(o_ref[s], n_ref[...])
            dwgu_ref[s] = dwgu
            dwgub_ref[s] = dwgu.astype(BF16)

    tile = pl.BlockSpec((T, tn), lambda j: (0, j))
    whole = pl.BlockSpec((T, D), lambda j: (0, 0))
    rows1, rows2 = pl.BlockSpec((tn, D), lambda j: (j, 0)), pl.BlockSpec((2, tn, D), lambda j: (0, j, 0))
    return _call(
        body, name=name, grid=(F // tn,),
        out_shape=(S((2, T, F), BF16), S((F, D), F32), S((F, D), BF16), S((2, F, D), F32), S((2, F, D), BF16)),
        in_specs=[whole, whole, rows1, tile, tile, tile],
        out_specs=(pl.BlockSpec((2, T, tn), lambda j: (0, 0, j)), rows1, rows1, rows2, rows2),
        sem=("parallel",), args=(dhb, n, wd, gate, up, a), riders=riders)


def tn_matmul(xs, y, alpha, name, riders=()):
    B, T, N = xs.shape
    D = y.shape[1]
    tn = _tile(N, 256)

    def body(x_ref, y_ref, o_ref, ob_ref):
        o = alpha * _tn(x_ref[...], y_ref[...])
        o_ref[...] = o
        ob_ref[...] = o.astype(BF16)

    tile = pl.BlockSpec((None, tn, D), lambda s, j: (s, j, 0))
    return _call(
        body, name=name, grid=(B, N // tn), out_shape=(S((B, N, D), F32), S((B, N, D), BF16)),
        in_specs=[pl.BlockSpec((None, T, tn), lambda s, j: (s, 0, j)), pl.BlockSpec((T, D), lambda s, j: (0, 0))],
        out_specs=(tile, tile), sem=("parallel", "parallel"), args=(xs, y), riders=riders)


def nn_rms_bwd(xs, ws, h_in, g, dh, name, riders=()):
    B, T, K = xs.shape
    D = ws.shape[2]
    tm = _tile(T, 256)

    def body(x_ref, w_ref, h_ref, g_ref, d_ref, o_ref, ob_ref, dg_ref):
        @pl.when(pl.program_id(0) == 0)
        def _():
            dg_ref[...] = jnp.zeros_like(dg_ref)
        dn = _nn(x_ref[0], w_ref[0])
        for s in range(1, B):
            dn = dn + _nn(x_ref[s], w_ref[s])
        dx, dg = _rms_bwd(dn, h_ref[...], g_ref[...])
        out = d_ref[...] + dx
        o_ref[...] = out
        ob_ref[...] = out.astype(BF16)
        dg_ref[...] += dg

    row = pl.BlockSpec((tm, D), lambda i: (i, 0))
    vec = pl.BlockSpec((1, D), lambda i: (0, 0))
    g, g_spec = _gain(g)
    return _call(
        body, name=name, grid=(T // tm,), out_shape=(S((T, D), F32), S((T, D), BF16), S((1, D), F32)),
        in_specs=[pl.BlockSpec((B, tm, K), lambda i: (0, i, 0)), pl.BlockSpec((B, K, D), lambda i: (0, 0, 0)), row, g_spec, row],
        out_specs=(row, row, vec),
        sem=("arbitrary",), args=(xs, ws, h_in, g, dh), riders=riders)


def mix_out_bwd(dhb, wout, mixed, o_sb, o_sw, g_sb, g_sw, name):
    T, D = dhb.shape
    M = SB_W + SWA_W
    tm = _tile(T, 256)
    steps = T // tm

    def body(d_ref, w_ref, mx_ref, a_ref, b_ref, ga_ref, gb_ref, da_ref, db_ref, dga_ref, dgb_ref, dw_ref, dwb_ref):
        i = pl.program_id(0)

        @pl.when(i == 0)
        def _():
            dga_ref[...] = jnp.zeros_like(dga_ref)
            dgb_ref[...] = jnp.zeros_like(dgb_ref)
            dw_ref[...] = jnp.zeros_like(dw_ref)
        dm = _nt(d_ref[...], w_ref[...])
        dxa, dga = _rms_bwd(dm[:, :SB_W], a_ref[...], ga_ref[...])
        dxb, dgb = _rms_bwd(dm[:, SB_W:], b_ref[...], gb_ref[...])
        da_ref[...] = dxa
        db_ref[...] = dxb
        dga_ref[...] += dga
        dgb_ref[...] += dgb
        dw_ref[...] += _tn(mx_ref[...], d_ref[...])

        @pl.when(i == steps - 1)
        def _():
            dwb_ref[...] = dw_ref[...].astype(BF16)

    row = lambda n: pl.BlockSpec((tm, n), lambda i: (i, 0))
    vec = lambda n: pl.BlockSpec((1, n), lambda i: (0, 0))
    whole = pl.BlockSpec((M, D), lambda i: (0, 0))
    (g_sb, sb_spec), (g_sw, sw_spec) = _gain(g_sb), _gain(g_sw)
    return pl.pallas_call(
        body, name=name, grid=(steps,),
        out_shape=(S((T, SB_W), F32), S((T, SWA_W), F32), S((1, SB_W), F32), S((1, SWA_W), F32), S((M, D), F32), S((M, D), BF16)),
        in_specs=[row(D), whole, row(M), row(SB_W), row(SWA_W), sb_spec, sw_spec],
        out_specs=(row(SB_W), row(SWA_W), vec(SB_W), vec(SWA_W), whole, whole),
        compiler_params=_params(("arbitrary",)),
    )(dhb, wout, mixed, o_sb, o_sw, g_sb, g_sw)


def rel_bias_grad(dscs, bprev, bcur, name):
    n = len(dscs)

    def body(*refs):
        bp_ref, bc_ref, o_ref = refs[n], refs[n + 1], refs[n + 2]
        bks = [bp_ref[...], bc_ref[...]]
        row = lax.broadcasted_iota(jnp.int32, (N_BUCKETS, QB), 0)
        lane = lax.broadcasted_iota(jnp.int32, (N_BUCKETS, QB), 1)
        out = jnp.zeros((N_BUCKETS, QB), F32)
        for h in range(8):
            tot = [sum(refs[l][h, b] for l in range(n)) for b in range(2)]
            for b in range(N_BUCKETS):
                val = jnp.sum(jnp.where(bks[0] == b, tot[0], 0.0)) + jnp.sum(jnp.where(bks[1] == b, tot[1], 0.0))
                out = jnp.where((row == b) & (lane == h), val, out)
        o_ref[...] = out

    return pl.pallas_call(body, name=name, out_shape=S((N_BUCKETS, QB), F32), compiler_params=_params())(*dscs, bprev, bcur)


def _adamw(w, g, m, v):
    m = ADAM_B1 * m + (1.0 - ADAM_B1) * g
    v = ADAM_B2 * v + (1.0 - ADAM_B2) * (g * g)
    m_hat = m / (1.0 - ADAM_B1 ** ADAM_STEP)
    v_hat = v / (1.0 - ADAM_B2 ** ADAM_STEP)
    delta = -ADAM_LR * (m_hat / (jnp.sqrt(v_hat) + ADAM_EPS) + ADAM_WD * w)
    return delta, m, v


def adamw_scattered(w, m, v, owns, others, name, riders=(), rows=176):
    L, R, C = w.shape
    tr = _rows_tile(R, rows)

    def body(w_ref, m_ref, v_ref, *rest):
        own_refs, other_refs = rest[:L], rest[L:2 * L]
        g_ref, d_ref, mo_ref, vo_ref = rest[2 * L:]
        layer = pl.program_id(0)

        def grad(k):
            o = other_refs[k]
            return own_refs[k][...] + o[0].astype(F32) + o[1].astype(F32) + o[2].astype(F32)

        g = grad(0)
        for k in range(1, L):
            g = jnp.where(layer == k, grad(k), g)
        d, mn, vn = _adamw(w_ref[...], g, m_ref[...], v_ref[...])
        g_ref[...] = g
        d_ref[...] = d
        mo_ref[...] = mn
        vo_ref[...] = vn

    tile = pl.BlockSpec((None, tr, C), lambda l, i: (l, i, 0))
    return _call(
        body, name=name, grid=(L, R // tr), out_shape=(S((L, R, C), F32),) * 4,
        in_specs=[tile] * 3 + [pl.BlockSpec((tr, C), lambda l, i: (i, 0))] * L + [pl.BlockSpec((3, tr, C), lambda l, i: (0, i, 0))] * L,
        out_specs=(tile,) * 4, sem=("parallel", "parallel"), args=(w, m, v, *owns, *others), riders=riders)


def adamw_small(w, gs, m, v, name):
    R, C = w.shape

    def body(w_ref, g_ref, m_ref, v_ref, go_ref, d_ref, mo_ref, vo_ref):
        g = g_ref[0]
        for k in range(1, N_DEV):
            g = g + g_ref[k]
        d, mn, vn = _adamw(w_ref[...], g, m_ref[...], v_ref[...])
        go_ref[...] = g
        d_ref[...] = d
        mo_ref[...] = mn
        vo_ref[...] = vn

    return pl.pallas_call(body, name=name, out_shape=(S((R, C), F32),) * 4, compiler_params=_params())(w, gs, m, v)


def kernel(x, norm_ffn1, w_ffn1_gu, w_ffn1_down, norm_mix, w_in, sinks, norm_out_sb, norm_out_swa, w_out, norm_ffn2, w_ffn2_gu, w_ffn2_down, rel_bias, norm_final, loss_target, m_norm_ffn1, m_w_ffn1_gu, m_w_ffn1_down, m_norm_mix, m_w_in, m_sinks, m_norm_out_sb, m_norm_out_swa, m_w_out, m_norm_ffn2, m_w_ffn2_gu, m_w_ffn2_down, m_rel_bias, m_norm_final, v_norm_ffn1, v_w_ffn1_gu, v_w_ffn1_down, v_norm_mix, v_w_in, v_sinks, v_norm_out_sb, v_norm_out_swa, v_w_out, v_norm_ffn2, v_w_ffn2_gu, v_w_ffn2_down, v_rel_bias, v_norm_final):
    L = norm_ffn1.shape[0]
    T, D = x.shape[1], x.shape[2]
    F = w_ffn1_down.shape[1] * N_DEV
    h = x.reshape(T, D)
    target = loss_target.reshape(T, D)
    after, upto, before = _tri_consts()
    bprev, bcur = _t5_buckets()

    local = {}
    for l in range(L):
        local[f"gu1_{l}"] = w_ffn1_gu[l].T.astype(BF16)
        local[f"d1_{l}"] = w_ffn1_down[l].astype(BF16)
        local[f"in_{l}"] = w_in[l].T.astype(BF16)
        local[f"out_{l}"] = w_out[l].astype(BF16)
        local[f"gu2_{l}"] = w_ffn2_gu[l].T.astype(BF16)
        local[f"d2_{l}"] = w_ffn2_down[l].astype(BF16)
    full, partial = {}, {}
    grads, chip_sum, recv_b = {}, {}, {}

    def run(fn, *args, ag=(), rs1=(), rs2=()):
        halves = lambda names: [n if isinstance(n, tuple) else (n, None) for n in names]
        ag, rs2 = [(n, k) for n, k in halves(ag) if n in local], halves(rs2)
        rows = lambda k, total: None if k is None else (k * (total // 2), total // 2)

        def second(n, k):
            sb = chip_sum[n][1]
            return scatter_second(sb, rows(k, sb.shape[1]), recv_b.get(n))

        riders = ([gather(local[n], rows(k, local[n].shape[0]), partial.get(n)) for n, k in ag]
                  + [scatter_first(grads[n][1]) for n in rs1] + [second(n, k) for n, k in rs2])
        if not riders:
            return fn(*args)
        outs, per = fn(*args, riders=riders)
        per = [p[0] for p in per]
        for n, k in ag:
            buf = per.pop(0)
            if k == 0:
                partial[n] = buf
            else:
                full[n] = buf.reshape(N_DEV * buf.shape[1], D)
        if rs1:
            sums = scatter_add([grads[n][0] for n in rs1], [per.pop(0) for n in rs1], "rs_add_" + "_".join(rs1))
            chip_sum.update(zip(rs1, sums))
        for n, _ in rs2:
            recv_b[n] = per.pop(0)
        return outs

    def attn_fwd(p, sink, name, riders=()):
        return side_by_side(sb_attn_fwd(p, after, name, riders=PARTS), swa_fwd(p, sink, rel_bias, bprev, bcur, name, riders=PARTS),
                            name, riders)

    def attn_bwd(p, do_sb, tot, do_sw, lse, sink, name, riders=()):
        return side_by_side(sb_attn_bwd(p, do_sb, tot, upto, before, name, riders=PARTS),
                            swa_bwd(p, do_sw, lse, sink, rel_bias, bprev, bcur, name, riders=PARTS), name, riders)

    gu = lambda n: full[n].reshape(2, F, D)
    slots = lambda pair: tuple(t.reshape(N_DEV, -1, D) for t in pair)
    vec = lambda a: a.reshape(1, -1)

    PW = max(D, SB_W + SWA_W)
    n_rows = 4 * L + 2
    n_rows += (-n_rows) % 8

    def pack(ffn1, mix, ffn2, final, osb, osw, snk, rel, extra):
        pieces = []

        def row(*parts):
            flat = [a.reshape(-1) for a in parts]
            pieces.extend(flat)
            used = sum(a.size for a in flat)
            if used < PW:
                pieces.append(jnp.zeros((PW - used,), F32))

        for group in (ffn1, mix, ffn2):
            for l in range(L):
                row(group[l])
        row(final)
        for l in range(L):
            row(osb[l], osw[l])
        row(*[snk[l].reshape(-1)[:8] for l in range(L)], rel, extra)
        pieces.append(jnp.zeros(((n_rows - 4 * L - 2) * PW,), F32))
        return jnp.concatenate(pieces).reshape(n_rows, PW)

    def unpack(arr):
        ffn1, mix, ffn2 = arr[0:L, :D], arr[L:2 * L, :D], arr[2 * L:3 * L, :D]
        final = arr[3 * L, :D]
        ob = arr[3 * L + 1:4 * L + 1]
        tail = arr[4 * L + 1]
        return (ffn1, mix, tail[:8 * L].reshape(L, 8), ob[:, :SB_W], ob[:, SB_W:SB_W + SWA_W], ffn2,
                tail[8 * L:8 * L + N_BUCKETS * 8].reshape(N_BUCKETS, 8), final)

    zero = jnp.zeros((1,), F32)
    w_small = pack(norm_ffn1, norm_mix, norm_ffn2, norm_final, norm_out_sb, norm_out_swa, sinks, rel_bias, zero)
    g_ffn1, g_mix, g_ffn2, g_osb, g_osw = [a.reshape(L, 1, -1) for a in (norm_ffn1, norm_mix, norm_ffn2, norm_out_sb, norm_out_swa)]

    saved = []
    n_next = run(rms_cast, h, (g_ffn1, 0), "rms_first", ag=("gu1_0",))
    for l in range(L):
        nx = l + 1
        s = {"h0": h, "n1": n_next}
        s["gate1"], s["up1"], s["a1"] = run(ffn_up_fwd, s["n1"], gu(f"gu1_{l}"), f"ffn1_up{l}",
                                            ag=(f"d1_{l}", ("in_0", 0) if l == 0 else (f"in_{l}", 1)))
        h = run(ffn_down_fwd, s["a1"], full[f"d1_{l}"], h, None, f"ffn1_down{l}", ag=(("in_0", 1), "out_0") if l == 0 else ())
        s["h1"] = h
        s["n2"], s["p"] = mix_in_fwd(h, (g_mix, l), full[f"in_{l}"], f"mix_in{l}")
        s["o_sb"], s["tot"], s["o_sw"], s["lse"] = run(attn_fwd, s["p"], (sinks, l), f"attn_fwd{l}",
                                                       ag=((f"out_{l}",) if l else ()) + (f"gu2_{l}", f"d2_{l}", (f"gu1_{nx}", 0)))
        s["mixed"], h, s["n3"] = run(mix_out_fwd, s["o_sb"], s["o_sw"], (g_osb, l), (g_osw, l),
                                     full[f"out_{l}"], h, (g_ffn2, l), f"mix_out{l}")
        s["h2"] = h
        s["gate2"], s["up2"], s["a2"] = run(ffn_up_fwd, s["n3"], gu(f"gu2_{l}"), f"ffn2_up{l}",
                                            ag=((f"gu1_{nx}", 1), (f"in_{nx}", 0)))
        if nx < L:
            h, n_next = run(ffn_down_fwd, s["a2"], full[f"d2_{l}"], h, (g_ffn1, nx), f"ffn2_down{l}")
        else:
            h = run(ffn_down_fwd, s["a2"], full[f"d2_{l}"], h, None, f"ffn2_down{l}")
        saved.append(s)

    loss_part, dh, dhb, dg_final = loss_head(h, vec(norm_final), target, "loss_head")

    small = {k: [None] * L for k in ("ffn1", "mix", "sinks", "osb", "osw", "ffn2", "dsc")}
    for l in reversed(range(L)):
        s = saved[l]

        def ffn_bwd(dh, dhb, tag, gate, up, a, n, h_in, g, r_down, r_up):
            gu_n, d_n = f"gu{tag}_{l}", f"d{tag}_{l}"
            dgu, dwd, dwdb, dwgu, dwgub = run(ffn_down_bwd, dhb, full[d_n], gate, up, a, n, f"ffn{tag}_down_bwd{l}", **r_down)
            grads[gu_n], grads[d_n] = slots((dwgu, dwgub)), slots((dwd, dwdb))
            return run(nn_rms_bwd, dgu, gu(gu_n), h_in, g, dh, f"ffn{tag}_up_bwd{l}", **r_up)

        later = l + 1 < L
        dh, dhb, small["ffn2"][l] = ffn_bwd(dh, dhb, 2, s["gate2"], s["up2"], s["a2"], s["n3"], s["h2"], (g_ffn2, l),
                                            dict(rs2=((f"gu1_{l + 1}", 0), f"d1_{l + 1}") if later else ()),
                                            dict(rs1=(f"gu2_{l}", f"d2_{l}")))
        do_sb, do_sw, small["osb"][l], small["osw"][l], dw_out, dw_out_b = mix_out_bwd(
            dhb, full[f"out_{l}"], s["mixed"], s["o_sb"], s["o_sw"], (g_osb, l), (g_osw, l), f"mix_out_bwd{l}")
        grads[f"out_{l}"] = slots((dw_out, dw_out_b))
        dq_sb, dk_sb, dv_sb, dq_sw, dk_sw, dv_sw, small["sinks"][l], small["dsc"][l] = run(
            attn_bwd, s["p"], do_sb, s["tot"], do_sw, s["lse"], (sinks, l), f"attn_bwd{l}",
            rs2=(f"gu2_{l}", f"d2_{l}") + (((f"gu1_{l + 1}", 1),) if later else ()), rs1=(f"out_{l}",))
        dp = jnp.concatenate([dq_sb, dk_sb, dv_sb, dq_sw, dk_sw, dv_sw], axis=1)
        dh, dhb, small["mix"][l] = nn_rms_bwd(dp[None], full[f"in_{l}"][None], s["h1"], (g_mix, l), dh, f"mix_in_bwd{l}")
        grads[f"in_{l}"] = slots(tn_matmul(dp[None], s["n2"], 1.0, f"dwin{l}"))
        dh, dhb, small["ffn1"][l] = ffn_bwd(dh, dhb, 1, s["gate1"], s["up1"], s["a1"], s["n1"], s["h0"], (g_ffn1, l),
                                            dict(rs1=(f"in_{l}",), rs2=(f"out_{l}",)),
                                            dict(rs1=(f"gu1_{l}", f"d1_{l}"), rs2=(f"in_{l}",)))

    grad_x = dh.reshape(x.shape)

    upd = {}
    turn_of = lambda transposed: (lambda a: jnp.swapaxes(a, 1, 2)) if transposed else (lambda a: a)

    def update(nm, w, m, v, transposed, riders=()):
        turn = turn_of(transposed)
        names = [f"{nm}_{l}" for l in range(L)]
        return adamw_scattered(turn(w), turn(m), turn(v), [chip_sum[n][0] for n in names], [recv_b[n] for n in names],
                               f"adamw_{nm}", riders=riders, rows=88 if riders is PARTS else 176)

    early = (("gu2", w_ffn2_gu, m_w_ffn2_gu, v_w_ffn2_gu, True), ("d2", w_ffn2_down, m_w_ffn2_down, v_w_ffn2_down, False),
             ("in", w_in, m_w_in, v_w_in, True), ("out", w_out, m_w_out, v_w_out, False))
    res = run(lambda name, riders=(): in_one_call([update(*e, riders=PARTS) for e in early], name, riders),
              "adamw_early", rs2=("gu1_0", "d1_0"))
    for k, e in enumerate(early):
        upd[e[0]] = tuple(turn_of(e[4])(r) for r in res[4 * k:4 * k + 4])
    for e in (("gu1", w_ffn1_gu, m_w_ffn1_gu, v_w_ffn1_gu, True), ("d1", w_ffn1_down, m_w_ffn1_down, v_w_ffn1_down, False)):
        upd[e[0]] = tuple(turn_of(e[4])(r) for r in update(*e))

    d_rel = rel_bias_grad(small["dsc"], bprev, bcur, "rel_bias_grad")[:, :8]
    g_small = pack(small["ffn1"], small["mix"], small["ffn2"], dg_final, small["osb"], small["osw"], small["sinks"], d_rel,
                   loss_part[0, :1])
    m_small = pack(m_norm_ffn1, m_norm_mix, m_norm_ffn2, m_norm_final, m_norm_out_sb, m_norm_out_swa, m_sinks, m_rel_bias, zero)
    v_small = pack(v_norm_ffn1, v_norm_mix, v_norm_ffn2, v_norm_final, v_norm_out_sb, v_norm_out_swa, v_sinks, v_rel_bias, zero)
    gs_small = all_gather_rows(g_small, "ag_small")
    summed = adamw_small(w_small, gs_small, m_small, v_small, "adamw_small")
    small_out = [unpack(a) for a in summed]
    loss = summed[0][4 * L + 1, 8 * L + N_BUCKETS * 8]

    def group(k):
        sm = small_out[k]
        return (sm[0], upd["gu1"][k], upd["d1"][k], sm[1], upd["in"][k], sm[2], sm[3], sm[4], upd["out"][k], sm[5],
                upd["gu2"][k], upd["d2"][k], sm[6], sm[7])

    return (loss, grad_x, *group(0), *group(1), *group(2), *group(3))
```
